```python
import jax, jax.numpy as jnp
from jax import lax
import numpy as np

D_MODEL = 1024
BATCH = 8
SEQ = 2048
DEPTH = 1

MEM_LEN = 256
HD = 64
SB_HEADS = 8
FOX_HEADS = 8
MEM_HEADS = 4
MEM_HD = 128
D_SB = SB_HEADS * HD
D_FOX = FOX_HEADS * HD
D_MEM = MEM_HEADS * MEM_HD
N_BRANCH = 3
D_FF = 4 * D_MODEL
BLOCK_Q = 128
EPS = 1e-6
NEG_INF = -1e30
SPLITS = (D_SB, D_SB, D_SB, D_FOX, D_FOX, D_FOX, FOX_HEADS, D_MEM, N_BRANCH * D_MODEL)
D_IN = sum(SPLITS)

kernel_name = "hybrid_stickbreak_fox_memory_block"


def split_columns(t, sizes):
    pieces = []
    start = 0
    for n in sizes:
        pieces.append(t[..., start:start + n])
        start += n
    return pieces


def rmsnorm(x, g):
    xf = x.astype(jnp.float32)
    y = xf * lax.rsqrt(jnp.mean(xf * xf, axis=-1, keepdims=True) + EPS)
    return (y * g.astype(jnp.float32)).astype(x.dtype)


def to_heads(t, n, d):
    b, s, _ = t.shape
    return t.reshape(b, s, n, d).transpose(0, 2, 1, 3)


def from_heads(t):
    b, h, s, d = t.shape
    return t.transpose(0, 2, 1, 3).reshape(b, s, h * d)


def stick_breaking_attention(q, k, v):
    s_len = q.shape[2]
    scale = HD ** -0.5
    outs = []
    for i in range(s_len // BLOCK_Q):
        q0 = i * BLOCK_Q
        kend = q0 + BLOCK_Q
        z = jnp.einsum('bhtd,bhsd->bhts', q[:, :, q0:kend], k[:, :, :kend]).astype(jnp.float32) * scale
        t_idx = q0 + jnp.arange(BLOCK_Q)[:, None]
        s_idx = jnp.arange(kend)[None, :]
        strict = s_idx < t_idx
        log_rem = jnp.where(strict, jax.nn.log_sigmoid(-z), 0.0)
        after = lax.cumsum(log_rem, axis=3, reverse=True) - log_rem
        w = jnp.where(strict, jnp.exp(jax.nn.log_sigmoid(z) + after), 0.0)
        outs.append(jnp.einsum('bhts,bhsd->bhtd', w.astype(v.dtype), v[:, :, :kend]))
    return jnp.concatenate(outs, axis=2)


def forgetting_attention(q, k, v, log_f):
    s_len = q.shape[2]
    scale = HD ** -0.5
    F = lax.cumsum(log_f.astype(jnp.float32), axis=2)
    outs = []
    for i in range(s_len // BLOCK_Q):
        q0 = i * BLOCK_Q
        kend = q0 + BLOCK_Q
        z = jnp.einsum('bhtd,bhsd->bhts', q[:, :, q0:kend], k[:, :, :kend]).astype(jnp.float32) * scale
        z = z + F[:, :, q0:kend, None] - F[:, :, None, :kend]
        causal = jnp.arange(kend)[None, :] <= (q0 + jnp.arange(BLOCK_Q)[:, None])
        p = jax.nn.softmax(jnp.where(causal, z, NEG_INF), axis=-1)
        outs.append(jnp.einsum('bhts,bhsd->bhtd', p.astype(v.dtype), v[:, :, :kend]))
    return jnp.concatenate(outs, axis=2)


def memory_attention(q, k, v):
    z = jnp.einsum('bhtd,bhmd->bhtm', q, k).astype(jnp.float32) * (MEM_HD ** -0.5)
    p = jax.nn.softmax(z, axis=-1)
    return jnp.einsum('bhtm,bhmd->bhtd', p.astype(v.dtype), v)


def _fwd_setup_inputs(seed: int = 0) -> dict:
    key = jax.random.key(seed)
    ks = jax.random.split(key, 20)

    def w(k, shape, fan_in):
        return jax.random.normal(k, shape, jnp.float32) * fan_in ** -0.5

    def gain(k, shape):
        return 1.0 + 0.02 * jax.random.normal(k, shape, jnp.float32)

    L = DEPTH
    return {
        "x": jax.random.normal(ks[0], (BATCH, SEQ, D_MODEL), jnp.float32),
        "mem": jax.random.normal(ks[1], (BATCH, MEM_LEN, D_MODEL), jnp.float32),
        "g_mix_norm": gain(ks[2], (L, D_MODEL)),
        "g_mem_norm": gain(ks[3], (L, D_MODEL)),
        "w_in": w(ks[4], (L, D_MODEL, D_IN), D_MODEL),
        "b_forget": 3.0 + 0.5 * jax.random.normal(ks[5], (L, FOX_HEADS), jnp.float32),
        "g_fox_q": gain(ks[6], (L, HD)),
        "g_fox_k": gain(ks[7], (L, HD)),
        "g_mem_q": gain(ks[8], (L, MEM_HD)),
        "g_mem_k": gain(ks[9], (L, MEM_HD)),
        "w_mem_kv": w(ks[10], (L, D_MODEL, 2 * D_MEM), D_MODEL),
        "w_branch_sb": w(ks[11], (L, D_SB, D_MODEL), D_SB),
        "w_branch_fox": w(ks[12], (L, D_FOX, D_MODEL), D_FOX),
        "w_branch_mem": w(ks[13], (L, D_MEM, D_MODEL), D_MEM),
        "w_out": w(ks[14], (L, D_MODEL, D_MODEL), D_MODEL),
        "g_mlp_norm": gain(ks[15], (L, D_MODEL)),
        "w_ff_up": w(ks[16], (L, D_MODEL, D_FF), D_MODEL),
        "w_ff_down": w(ks[17], (L, D_FF, D_MODEL), D_FF),
    }


def _fwd_reference(x, mem, g_mix_norm, g_mem_norm, w_in, b_forget, g_fox_q, g_fox_k, g_mem_q, g_mem_k,
              w_mem_kv, w_branch_sb, w_branch_fox, w_branch_mem, w_out, g_mlp_norm, w_ff_up, w_ff_down):
    b, s, _ = x.shape
    for l in range(DEPTH):
        h = rmsnorm(x, g_mix_norm[l])
        proj = jnp.einsum('bsd,de->bse', h, w_in[l])
        sb_q, sb_k, sb_v, fx_q, fx_k, fx_v, f_logit, m_q, gate_logit = split_columns(proj, SPLITS)

        o_sb = stick_breaking_attention(to_heads(sb_q, SB_HEADS, HD), to_heads(sb_k, SB_HEADS, HD),
                                        to_heads(sb_v, SB_HEADS, HD))

        fq = rmsnorm(to_heads(fx_q, FOX_HEADS, HD), g_fox_q[l])
        fk = rmsnorm(to_heads(fx_k, FOX_HEADS, HD), g_fox_k[l])
        log_f = jax.nn.log_sigmoid((f_logit + b_forget[l]).astype(jnp.float32)).transpose(0, 2, 1)
        o_fox = forgetting_attention(fq, fk, to_heads(fx_v, FOX_HEADS, HD), log_f)

        mh = rmsnorm(mem, g_mem_norm[l])
        mkv = jnp.einsum('bmd,de->bme', mh, w_mem_kv[l])
        mk, mv = split_columns(mkv, (D_MEM, D_MEM))
        mq = rmsnorm(to_heads(m_q, MEM_HEADS, MEM_HD), g_mem_q[l])
        mk = rmsnorm(to_heads(mk, MEM_HEADS, MEM_HD), g_mem_k[l])
        o_mem = memory_attention(mq, mk, to_heads(mv, MEM_HEADS, MEM_HD))

        gates = jax.nn.sigmoid(gate_logit.reshape(b, s, N_BRANCH, D_MODEL))
        br_sb = jnp.einsum('bse,ed->bsd', from_heads(o_sb), w_branch_sb[l])
        br_fox = jnp.einsum('bse,ed->bsd', from_heads(o_fox), w_branch_fox[l])
        br_mem = jnp.einsum('bse,ed->bsd', from_heads(o_mem), w_branch_mem[l])
        merged = gates[:, :, 0] * br_sb + gates[:, :, 1] * br_fox + gates[:, :, 2] * br_mem
        x = x + jnp.einsum('bsd,de->bse', merged, w_out[l])

        h2 = rmsnorm(x, g_mlp_norm[l])
        u = jax.nn.relu(jnp.einsum('bsd,df->bsf', h2, w_ff_up[l]))
        x = x + jnp.einsum('bsf,fd->bsd', u * u, w_ff_down[l])
    return x


import jax as _jax
import jax.numpy as _jnp

TWIN_FORMAT = 'train_step'
FWD_PARAMS = ['x', 'mem', 'g_mix_norm', 'g_mem_norm', 'w_in', 'b_forget', 'g_fox_q', 'g_fox_k', 'g_mem_q', 'g_mem_k', 'w_mem_kv', 'w_branch_sb', 'w_branch_fox', 'w_branch_mem', 'w_out', 'g_mlp_norm', 'w_ff_up', 'w_ff_down']
TWIN_WEIGHTS = ['g_mix_norm', 'g_mem_norm', 'w_in', 'b_forget', 'g_fox_q', 'g_fox_k', 'g_mem_q', 'g_mem_k', 'w_mem_kv', 'w_branch_sb', 'w_branch_fox', 'w_branch_mem', 'w_out', 'g_mlp_norm', 'w_ff_up', 'w_ff_down']
TWIN_DIFF_INPUT = 'x'
TWIN_INPUTS = ['x', 'mem', 'g_mix_norm', 'g_mem_norm', 'w_in', 'b_forget', 'g_fox_q', 'g_fox_k', 'g_mem_q', 'g_mem_k', 'w_mem_kv', 'w_branch_sb', 'w_branch_fox', 'w_branch_mem', 'w_out', 'g_mlp_norm', 'w_ff_up', 'w_ff_down', 'loss_target', 'm_g_mix_norm', 'm_g_mem_norm', 'm_w_in', 'm_b_forget', 'm_g_fox_q', 'm_g_fox_k', 'm_g_mem_q', 'm_g_mem_k', 'm_w_mem_kv', 'm_w_branch_sb', 'm_w_branch_fox', 'm_w_branch_mem', 'm_w_out', 'm_g_mlp_norm', 'm_w_ff_up', 'm_w_ff_down', 'v_g_mix_norm', 'v_g_mem_norm', 'v_w_in', 'v_b_forget', 'v_g_fox_q', 'v_g_fox_k', 'v_g_mem_q', 'v_g_mem_k', 'v_w_mem_kv', 'v_w_branch_sb', 'v_w_branch_fox', 'v_w_branch_mem', 'v_w_out', 'v_g_mlp_norm', 'v_w_ff_up', 'v_w_ff_down']
TWIN_OUTPUTS = ['loss', 'grad_x', 'grad_g_mix_norm', 'grad_g_mem_norm', 'grad_w_in', 'grad_b_forget', 'grad_g_fox_q', 'grad_g_fox_k', 'grad_g_mem_q', 'grad_g_mem_k', 'grad_w_mem_kv', 'grad_w_branch_sb', 'grad_w_branch_fox', 'grad_w_branch_mem', 'grad_w_out', 'grad_g_mlp_norm', 'grad_w_ff_up', 'grad_w_ff_down', 'delta_g_mix_norm', 'delta_g_mem_norm', 'delta_w_in', 'delta_b_forget', 'delta_g_fox_q', 'delta_g_fox_k', 'delta_g_mem_q', 'delta_g_mem_k', 'delta_w_mem_kv', 'delta_w_branch_sb', 'delta_w_branch_fox', 'delta_w_branch_mem', 'delta_w_out', 'delta_g_mlp_norm', 'delta_w_ff_up', 'delta_w_ff_down', 'new_m_g_mix_norm', 'new_m_g_mem_norm', 'new_m_w_in', 'new_m_b_forget', 'new_m_g_fox_q', 'new_m_g_fox_k', 'new_m_g_mem_q', 'new_m_g_mem_k', 'new_m_w_mem_kv', 'new_m_w_branch_sb', 'new_m_w_branch_fox', 'new_m_w_branch_mem', 'new_m_w_out', 'new_m_g_mlp_norm', 'new_m_w_ff_up', 'new_m_w_ff_down', 'new_v_g_mix_norm', 'new_v_g_mem_norm', 'new_v_w_in', 'new_v_b_forget', 'new_v_g_fox_q', 'new_v_g_fox_k', 'new_v_g_mem_q', 'new_v_g_mem_k', 'new_v_w_mem_kv', 'new_v_w_branch_sb', 'new_v_w_branch_fox', 'new_v_w_branch_mem', 'new_v_w_out', 'new_v_g_mlp_norm', 'new_v_w_ff_up', 'new_v_w_ff_down']
TWIN_LEAF_KINDS = {'loss': 'loss', 'grad_x': 'grad_x', 'grad_g_mix_norm': 'grad_w', 'grad_g_mem_norm': 'grad_w', 'grad_w_in': 'grad_w', 'grad_b_forget': 'grad_w', 'grad_g_fox_q': 'grad_w', 'grad_g_fox_k': 'grad_w', 'grad_g_mem_q': 'grad_w', 'grad_g_mem_k': 'grad_w', 'grad_w_mem_kv': 'grad_w', 'grad_w_branch_sb': 'grad_w', 'grad_w_branch_fox': 'grad_w', 'grad_w_branch_mem': 'grad_w', 'grad_w_out': 'grad_w', 'grad_g_mlp_norm': 'grad_w', 'grad_w_ff_up': 'grad_w', 'grad_w_ff_down': 'grad_w', 'delta_g_mix_norm': 'delta_w', 'delta_g_mem_norm': 'delta_w', 'delta_w_in': 'delta_w', 'delta_b_forget': 'delta_w', 'delta_g_fox_q': 'delta_w', 'delta_g_fox_k': 'delta_w', 'delta_g_mem_q': 'delta_w', 'delta_g_mem_k': 'delta_w', 'delta_w_mem_kv': 'delta_w', 'delta_w_branch_sb': 'delta_w', 'delta_w_branch_fox': 'delta_w', 'delta_w_branch_mem': 'delta_w', 'delta_w_out': 'delta_w', 'delta_g_mlp_norm': 'delta_w', 'delta_w_ff_up': 'delta_w', 'delta_w_ff_down': 'delta_w', 'new_m_g_mix_norm': 'new_m', 'new_m_g_mem_norm': 'new_m', 'new_m_w_in': 'new_m', 'new_m_b_forget': 'new_m', 'new_m_g_fox_q': 'new_m', 'new_m_g_fox_k': 'new_m', 'new_m_g_mem_q': 'new_m', 'new_m_g_mem_k': 'new_m', 'new_m_w_mem_kv': 'new_m', 'new_m_w_branch_sb': 'new_m', 'new_m_w_branch_fox': 'new_m', 'new_m_w_branch_mem': 'new_m', 'new_m_w_out': 'new_m', 'new_m_g_mlp_norm': 'new_m', 'new_m_w_ff_up': 'new_m', 'new_m_w_ff_down': 'new_m', 'new_v_g_mix_norm': 'new_v', 'new_v_g_mem_norm': 'new_v', 'new_v_w_in': 'new_v', 'new_v_b_forget': 'new_v', 'new_v_g_fox_q': 'new_v', 'new_v_g_fox_k': 'new_v', 'new_v_g_mem_q': 'new_v', 'new_v_g_mem_k': 'new_v', 'new_v_w_mem_kv': 'new_v', 'new_v_w_branch_sb': 'new_v', 'new_v_w_branch_fox': 'new_v', 'new_v_w_branch_mem': 'new_v', 'new_v_w_out': 'new_v', 'new_v_g_mlp_norm': 'new_v', 'new_v_w_ff_up': 'new_v', 'new_v_w_ff_down': 'new_v'}


def _forward(args):
    return _fwd_reference(*[args[k] for k in FWD_PARAMS])


def _output_shape():
    out = _jax.eval_shape(lambda: _forward(_fwd_setup_inputs(0)))
    return out.shape, out.dtype

N_MICROBATCH = 1
ADAM_LR = 0.001
ADAM_B1 = 0.9
ADAM_B2 = 0.999
ADAM_EPS = 1e-08
ADAM_WD = 0.01
ADAM_STEP = 10
PER_EXAMPLE_BATCH_AXIS = {'x': 0, 'mem': 0, 'loss_target': 0}
SHARED_INPUTS = []
_WEIGHT_DTYPES = {'g_mix_norm': _jnp.float32, 'g_mem_norm': _jnp.float32, 'w_in': _jnp.float32, 'b_forget': _jnp.float32, 'g_fox_q': _jnp.float32, 'g_fox_k': _jnp.float32, 'g_mem_q': _jnp.float32, 'g_mem_k': _jnp.float32, 'w_mem_kv': _jnp.float32, 'w_branch_sb': _jnp.float32, 'w_branch_fox': _jnp.float32, 'w_branch_mem': _jnp.float32, 'w_out': _jnp.float32, 'g_mlp_norm': _jnp.float32, 'w_ff_up': _jnp.float32, 'w_ff_down': _jnp.float32}
MOMENT_SCALE = {'g_mix_norm': 3.068060e+00, 'g_mem_norm': 2.268673e-01, 'w_in': 1.155318e-01, 'b_forget': 1.475102e+01, 'g_fox_q': 2.803030e+00, 'g_fox_k': 2.809482e+00, 'g_mem_q': 3.350612e-01, 'g_mem_k': 3.370032e-01, 'w_mem_kv': 2.248179e-01, 'w_branch_sb': 2.365730e-01, 'w_branch_fox': 1.051904e-01, 'w_branch_mem': 2.081684e-01, 'w_out': 3.098681e-01, 'g_mlp_norm': 4.816492e+01, 'w_ff_up': 4.471381e-01, 'w_ff_down': 4.000463e+00}


def _to_microbatches(a, axis):
    t = _jnp.moveaxis(a, axis, 0)
    t = t.reshape((N_MICROBATCH, t.shape[0] // N_MICROBATCH) + t.shape[1:])
    return _jnp.moveaxis(t, 1, axis + 1)


def setup_inputs(seed: int = 0) -> dict:
    inp = _fwd_setup_inputs(seed)
    key = _jax.random.fold_in(_jax.random.key(seed), 7919)
    shape, _ = _output_shape()
    out = dict(inp)
    out["loss_target"] = _jax.random.normal(_jax.random.fold_in(key, 0), shape, _jnp.float32)
    for i, name in enumerate(TWIN_WEIGHTS):
        w = inp[name].astype(_jnp.float32)
        if MOMENT_SCALE is None:
            s = _jnp.sqrt(_jnp.mean(_jnp.square(w)) + 1e-30)
        else:
            s = MOMENT_SCALE[name]
        km, kv = _jax.random.split(_jax.random.fold_in(key, i + 1))
        out[name] = w
        out["m_" + name] = s * _jax.random.normal(km, w.shape, _jnp.float32)
        out["v_" + name] = (s * s) * _jax.random.uniform(kv, w.shape, _jnp.float32, 0.5, 1.5)
    if N_MICROBATCH > 1:
        for name, axis in PER_EXAMPLE_BATCH_AXIS.items():
            out[name] = _to_microbatches(out[name], axis)
    return {'x': out['x'], 'mem': out['mem'], 'g_mix_norm': out['g_mix_norm'], 'g_mem_norm': out['g_mem_norm'], 'w_in': out['w_in'], 'b_forget': out['b_forget'], 'g_fox_q': out['g_fox_q'], 'g_fox_k': out['g_fox_k'], 'g_mem_q': out['g_mem_q'], 'g_mem_k': out['g_mem_k'], 'w_mem_kv': out['w_mem_kv'], 'w_branch_sb': out['w_branch_sb'], 'w_branch_fox': out['w_branch_fox'], 'w_branch_mem': out['w_branch_mem'], 'w_out': out['w_out'], 'g_mlp_norm': out['g_mlp_norm'], 'w_ff_up': out['w_ff_up'], 'w_ff_down': out['w_ff_down'], 'loss_target': out['loss_target'], 'm_g_mix_norm': out['m_g_mix_norm'], 'm_g_mem_norm': out['m_g_mem_norm'], 'm_w_in': out['m_w_in'], 'm_b_forget': out['m_b_forget'], 'm_g_fox_q': out['m_g_fox_q'], 'm_g_fox_k': out['m_g_fox_k'], 'm_g_mem_q': out['m_g_mem_q'], 'm_g_mem_k': out['m_g_mem_k'], 'm_w_mem_kv': out['m_w_mem_kv'], 'm_w_branch_sb': out['m_w_branch_sb'], 'm_w_branch_fox': out['m_w_branch_fox'], 'm_w_branch_mem': out['m_w_branch_mem'], 'm_w_out': out['m_w_out'], 'm_g_mlp_norm': out['m_g_mlp_norm'], 'm_w_ff_up': out['m_w_ff_up'], 'm_w_ff_down': out['m_w_ff_down'], 'v_g_mix_norm': out['v_g_mix_norm'], 'v_g_mem_norm': out['v_g_mem_norm'], 'v_w_in': out['v_w_in'], 'v_b_forget': out['v_b_forget'], 'v_g_fox_q': out['v_g_fox_q'], 'v_g_fox_k': out['v_g_fox_k'], 'v_g_mem_q': out['v_g_mem_q'], 'v_g_mem_k': out['v_g_mem_k'], 'v_w_mem_kv': out['v_w_mem_kv'], 'v_w_branch_sb': out['v_w_branch_sb'], 'v_w_branch_fox': out['v_w_branch_fox'], 'v_w_branch_mem': out['v_w_branch_mem'], 'v_w_out': out['v_w_out'], 'v_g_mlp_norm': out['v_g_mlp_norm'], 'v_w_ff_up': out['v_w_ff_up'], 'v_w_ff_down': out['v_w_ff_down']}


def _loss(weights, diff, rest, loss_target):
    with _jax.named_scope("forward"):
        args = {**rest, TWIN_DIFF_INPUT: diff, **{k: w.astype(_WEIGHT_DTYPES[k]) for k, w in weights.items()}}
        y = _forward(args)
    with _jax.named_scope("loss_head"):
        err = _jnp.square(y.astype(_jnp.float32) - loss_target)
        return 0.5 * _jnp.sum(_jnp.mean(err, axis=-1)) if err.ndim else 0.5 * err


def _adamw(w, g, m, v):
    m = ADAM_B1 * m + (1.0 - ADAM_B1) * g
    v = ADAM_B2 * v + (1.0 - ADAM_B2) * _jnp.square(g)
    m_hat = m / (1.0 - ADAM_B1 ** ADAM_STEP)
    v_hat = v / (1.0 - ADAM_B2 ** ADAM_STEP)
    delta = -ADAM_LR * (m_hat / (_jnp.sqrt(v_hat) + ADAM_EPS) + ADAM_WD * w)
    return delta, m, v


def reference(x, mem, g_mix_norm, g_mem_norm, w_in, b_forget, g_fox_q, g_fox_k, g_mem_q, g_mem_k, w_mem_kv, w_branch_sb, w_branch_fox, w_branch_mem, w_out, g_mlp_norm, w_ff_up, w_ff_down, loss_target, m_g_mix_norm, m_g_mem_norm, m_w_in, m_b_forget, m_g_fox_q, m_g_fox_k, m_g_mem_q, m_g_mem_k, m_w_mem_kv, m_w_branch_sb, m_w_branch_fox, m_w_branch_mem, m_w_out, m_g_mlp_norm, m_w_ff_up, m_w_ff_down, v_g_mix_norm, v_g_mem_norm, v_w_in, v_b_forget, v_g_fox_q, v_g_fox_k, v_g_mem_q, v_g_mem_k, v_w_mem_kv, v_w_branch_sb, v_w_branch_fox, v_w_branch_mem, v_w_out, v_g_mlp_norm, v_w_ff_up, v_w_ff_down):
    given = dict(x=x, mem=mem, g_mix_norm=g_mix_norm, g_mem_norm=g_mem_norm, w_in=w_in, b_forget=b_forget, g_fox_q=g_fox_q, g_fox_k=g_fox_k, g_mem_q=g_mem_q, g_mem_k=g_mem_k, w_mem_kv=w_mem_kv, w_branch_sb=w_branch_sb, w_branch_fox=w_branch_fox, w_branch_mem=w_branch_mem, w_out=w_out, g_mlp_norm=g_mlp_norm, w_ff_up=w_ff_up, w_ff_down=w_ff_down, loss_target=loss_target, m_g_mix_norm=m_g_mix_norm, m_g_mem_norm=m_g_mem_norm, m_w_in=m_w_in, m_b_forget=m_b_forget, m_g_fox_q=m_g_fox_q, m_g_fox_k=m_g_fox_k, m_g_mem_q=m_g_mem_q, m_g_mem_k=m_g_mem_k, m_w_mem_kv=m_w_mem_kv, m_w_branch_sb=m_w_branch_sb, m_w_branch_fox=m_w_branch_fox, m_w_branch_mem=m_w_branch_mem, m_w_out=m_w_out, m_g_mlp_norm=m_g_mlp_norm, m_w_ff_up=m_w_ff_up, m_w_ff_down=m_w_ff_down, v_g_mix_norm=v_g_mix_norm, v_g_mem_norm=v_g_mem_norm, v_w_in=v_w_in, v_b_forget=v_b_forget, v_g_fox_q=v_g_fox_q, v_g_fox_k=v_g_fox_k, v_g_mem_q=v_g_mem_q, v_g_mem_k=v_g_mem_k, v_w_mem_kv=v_w_mem_kv, v_w_branch_sb=v_w_branch_sb, v_w_branch_fox=v_w_branch_fox, v_w_branch_mem=v_w_branch_mem, v_w_out=v_w_out, v_g_mlp_norm=v_g_mlp_norm, v_w_ff_up=v_w_ff_up, v_w_ff_down=v_w_ff_down)
    weights = {n: given[n] for n in TWIN_WEIGHTS}
    shared = {n: given[n] for n in SHARED_INPUTS}
    per_example = {n: given[n] for n in ['x', 'mem']}
    grad_fn = _jax.value_and_grad(_loss, argnums=(0, 1))

    def one_microbatch(ex, loss_target):
        ex = dict(ex)
        diff = ex.pop(TWIN_DIFF_INPUT)
        return grad_fn(weights, diff, {**shared, **ex}, loss_target)

    if N_MICROBATCH == 1:
        loss, (grad_w, grad_x) = one_microbatch(per_example, given["loss_target"])
    else:
        def body(carry, xs):
            loss_sum, grad_sum = carry
            l_k, (gw_k, gx_k) = one_microbatch(xs[0], xs[1])
            with _jax.named_scope("update"):
                return (loss_sum + l_k, _jax.tree.map(_jnp.add, grad_sum, gw_k)), gx_k

        init = (_jnp.zeros((), _jnp.float32), _jax.tree.map(_jnp.zeros_like, weights))
        (loss, grad_w), grad_x = _jax.lax.scan(body, init, (per_example, given["loss_target"]))
    with _jax.named_scope("update"):
        delta_w, new_m, new_v = {}, {}, {}
        for n in TWIN_WEIGHTS:
            delta_w[n], new_m[n], new_v[n] = _adamw(weights[n], grad_w[n], given["m_" + n], given["v_" + n])
    return (loss, grad_x, *[grad_w[n] for n in TWIN_WEIGHTS], *[delta_w[n] for n in TWIN_WEIGHTS],
            *[new_m[n] for n in TWIN_WEIGHTS], *[new_v[n] for n in TWIN_WEIGHTS])
```

```python
import functools

import jax
import jax.numpy as jnp
from jax import lax
from jax.experimental import pallas as pl
from jax.experimental.pallas import tpu as pltpu

F32 = jnp.float32
BF16 = jnp.bfloat16
MESH_ID = pl.DeviceIdType.MESH

HD = 64
SB_HEADS = 8
FOX_HEADS = 8
MEM_HEADS = 4
MEM_HD = 128
D_SB = SB_HEADS * HD
D_FOX = FOX_HEADS * HD
D_MEM = MEM_HEADS * MEM_HD
EPS = 1e-6
NEG_INF = -1e30

ADAM_LR = 0.001
ADAM_B1 = 0.9
ADAM_B2 = 0.999
ADAM_EPS = 1e-08
ADAM_WD = 0.01
ADAM_STEP = 10

N_CHIPS = 4
VMEM_LIMIT = 56 * 1024 * 1024

F_PAD = 256


def _tile(n, target, align=128):
    if n <= target:
        return n
    best = None
    t = align
    while t <= target:
        if n % t == 0:
            best = t
        t += align
    assert best is not None, (n, target, align)
    return best


def _params(sem):
    return pltpu.CompilerParams(dimension_semantics=sem, vmem_limit_bytes=VMEM_LIMIT)


def _mm(a, b, *, name, ta=False, tb=False, out_dtypes=(F32,), epilogue=None, extras=(),
        tm=512, tn=512, tk=1024):
    if ta:
        K, M = a.shape
    else:
        M, K = a.shape
    if tb:
        N, K2 = b.shape
    else:
        K2, N = b.shape
    assert K == K2, (a.shape, b.shape, ta, tb)
    tm, tn, tk = _tile(M, tm), _tile(N, tn), _tile(K, tk)
    nk = K // tk
    n_extra, n_out = len(extras), len(out_dtypes)
    if epilogue is None:
        epilogue = lambda acc: (acc,)
    dims = (((0 if ta else 1,), (1 if tb else 0,)), ((), ()))

    def body(a_ref, b_ref, *rest):
        extra_refs = rest[:n_extra]
        out_refs = rest[n_extra:n_extra + n_out]
        acc_ref = rest[n_extra + n_out]
        k = pl.program_id(2)

        @pl.when(k == 0)
        def _():
            acc_ref[...] = jnp.zeros_like(acc_ref)

        acc_ref[...] += lax.dot_general(a_ref[...].astype(BF16), b_ref[...].astype(BF16), dims,
                                        preferred_element_type=F32)

        @pl.when(k == nk - 1)
        def _():
            outs = epilogue(acc_ref[...], *[r[...] for r in extra_refs])
            for o_ref, o in zip(out_refs, outs):
                o_ref[...] = o.astype(o_ref.dtype)

    a_spec = pl.BlockSpec((tk, tm), lambda i, j, k: (k, i)) if ta else pl.BlockSpec((tm, tk), lambda i, j, k: (i, k))
    b_spec = pl.BlockSpec((tn, tk), lambda i, j, k: (j, k)) if tb else pl.BlockSpec((tk, tn), lambda i, j, k: (k, j))
    mn_spec = pl.BlockSpec((tm, tn), lambda i, j, k: (i, j))
    outs = pl.pallas_call(
        body, name=name,
        grid=(M // tm, N // tn, nk),
        in_specs=[a_spec, b_spec] + [mn_spec] * n_extra,
        out_specs=[mn_spec] * n_out,
        out_shape=[jax.ShapeDtypeStruct((M, N), dt) for dt in out_dtypes],
        scratch_shapes=[pltpu.VMEM((tm, tn), F32)],
        compiler_params=_params(("parallel", "parallel", "arbitrary")),
    )(a, b, *extras)
    return outs if n_out > 1 else outs[0]


def _row_tile(rows, cols, n_arrays):
    budget = 10 * 1024 * 1024
    cols_padded = -(-cols // 128) * 128
    target = max(16, budget // (cols_padded * 4 * n_arrays * 2))
    return _tile(rows, target, align=16)


def _ew(fn, ins, out_dtypes, *, name):
    R, C = ins[0].shape
    n_in, n_out = len(ins), len(out_dtypes)
    tr = _row_tile(R, C, n_in + n_out)

    def body(*refs):
        outs = fn(*[r[...] for r in refs[:n_in]])
        for o_ref, o in zip(refs[n_in:], outs):
            o_ref[...] = o.astype(o_ref.dtype)

    spec = pl.BlockSpec((tr, C), lambda i: (i, 0))
    outs = pl.pallas_call(
        body, name=name, grid=(R // tr,),
        in_specs=[spec] * n_in, out_specs=[spec] * n_out,
        out_shape=[jax.ShapeDtypeStruct((R, C), dt) for dt in out_dtypes],
        compiler_params=_params(("parallel",)),
    )(*ins)
    return outs if n_out > 1 else outs[0]


def _rmsnorm_fwd(x, g, out_dtype, *, name):
    R, d = x.shape
    tr = _row_tile(R, d, 3)

    def body(x_ref, g_ref, o_ref):
        xv = x_ref[...].astype(F32)
        r = lax.rsqrt(jnp.mean(xv * xv, axis=-1, keepdims=True) + EPS)
        o_ref[...] = (xv * r * g_ref[...]).astype(o_ref.dtype)

    return pl.pallas_call(
        body, name=name, grid=(R // tr,),
        in_specs=[pl.BlockSpec((tr, d), lambda i: (i, 0)), pl.BlockSpec((1, d), lambda i: (0, 0))],
        out_specs=pl.BlockSpec((tr, d), lambda i: (i, 0)),
        out_shape=jax.ShapeDtypeStruct((R, d), out_dtype),
        compiler_params=_params(("parallel",)),
    )(x, g)


def _rmsnorm_bwd(x, g, dy, add=None, *, name):
    R, d = x.shape
    has_add = add is not None
    tr = _row_tile(R, d, 5)

    def body(*refs):
        x_ref, g_ref, dy_ref = refs[:3]
        add_ref = refs[3] if has_add else None
        dx_ref, dg_ref = refs[-2:]
        xv = x_ref[...].astype(F32)
        dyv = dy_ref[...].astype(F32)
        r = lax.rsqrt(jnp.mean(xv * xv, axis=-1, keepdims=True) + EPS)
        xh = xv * r
        dyg = dyv * g_ref[...]
        c = jnp.mean(dyg * xh, axis=-1, keepdims=True)
        dx = r * (dyg - xh * c)
        if has_add:
            dx = dx + add_ref[...]
        dx_ref[...] = dx

        @pl.when(pl.program_id(0) == 0)
        def _():
            dg_ref[...] = jnp.zeros_like(dg_ref)

        dg_ref[...] += jnp.sum(dyv * xh, axis=0, keepdims=True)

    row = pl.BlockSpec((tr, d), lambda i: (i, 0))
    vec = pl.BlockSpec((1, d), lambda i: (0, 0))
    ins = [x, g, dy] + ([add] if has_add else [])
    return pl.pallas_call(
        body, name=name, grid=(R // tr,),
        in_specs=[row, vec, row] + ([row] if has_add else []),
        out_specs=[row, vec],
        out_shape=[jax.ShapeDtypeStruct((R, d), F32), jax.ShapeDtypeStruct((1, d), F32)],
        compiler_params=_params(("arbitrary",)),
    )(*ins)


_NT = (((1,), (1,)), ((), ()))
_TN = (((0,), (0,)), ((), ()))


def _dot(a, b, dims=(((1,), (0,)), ((), ()))):
    return lax.dot_general(a, b, dims, preferred_element_type=F32)


def _split_dot(x, tri):
    hi = x.astype(BF16)
    lo = (x - hi.astype(F32)).astype(BF16)
    return _dot(hi, tri) + _dot(lo, tri)


def _log_sigmoid_pair(z):
    sp = jnp.log(1.0 + jnp.exp(-jnp.abs(z)))
    return jnp.minimum(z, 0.0) - sp, jnp.minimum(-z, 0.0) - sp


def _sb_fwd(q, k, v, *, name, tq=256):
    H, S, hd = q.shape
    tq = _tile(S, tq)
    tk = tq
    scale = hd ** -0.5

    def body(q_ref, k_ref, v_ref, o_ref, tot_ref):
        qi = pl.program_id(1)
        qv = q_ref[0]
        row = qi * tq + lax.broadcasted_iota(jnp.int32, (tq, tk), 0)
        col0 = lax.broadcasted_iota(jnp.int32, (tq, tk), 1)
        later = (lax.broadcasted_iota(jnp.int32, (tk, tk), 0) > lax.broadcasted_iota(jnp.int32, (tk, tk), 1)).astype(BF16)

        def step(i, carry):
            acc, c_rem = carry
            kb = qi - i
            ks = pl.multiple_of(kb * tk, tk)
            kv = k_ref[0, pl.ds(ks, tk), :]
            vv = v_ref[0, pl.ds(ks, tk), :]
            z = _dot(qv, kv, _NT) * scale
            mask = (col0 + ks) < row
            ls, lsn = _log_sigmoid_pair(z)
            rem = jnp.where(mask, lsn, 0.0)
            after = _split_dot(rem, later) + c_rem
            w = jnp.where(mask, jnp.exp(ls + after), 0.0)
            acc = acc + _dot(w.astype(BF16), vv)
            c_rem = c_rem + jnp.sum(rem, axis=1, keepdims=True)
            return acc, c_rem

        acc, c_rem = lax.fori_loop(0, qi + 1, step, (jnp.zeros((tq, hd), F32), jnp.zeros((tq, 1), F32)))
        o_ref[0] = acc
        tot_ref[0] = c_rem

    qspec = pl.BlockSpec((1, tq, hd), lambda h, i: (h, i, 0))
    kspec = pl.BlockSpec((1, S, hd), lambda h, i: (h, 0, 0))
    return pl.pallas_call(
        body, name=name, grid=(H, S // tq),
        in_specs=[qspec, kspec, kspec],
        out_specs=[qspec, pl.BlockSpec((1, tq, 1), lambda h, i: (h, i, 0))],
        out_shape=[jax.ShapeDtypeStruct((H, S, hd), F32), jax.ShapeDtypeStruct((H, S, 1), F32)],
        compiler_params=_params(("parallel", "arbitrary")),
    )(q, k, v)


def _sb_bwd(q, k, v, do, tot, *, name, tq=256):
    H, S, hd = q.shape
    tq = _tile(S, tq)
    tk = tq
    scale = hd ** -0.5

    def body(q_ref, k_ref, v_ref, do_ref, tot_ref, dq_ref, dk_ref, dv_ref):
        qi = pl.program_id(1)

        @pl.when(qi == 0)
        def _():
            dk_ref[...] = jnp.zeros_like(dk_ref)
            dv_ref[...] = jnp.zeros_like(dv_ref)

        qv = q_ref[0]
        dov = do_ref[0].astype(BF16)
        tot_v = tot_ref[0]
        row = qi * tq + lax.broadcasted_iota(jnp.int32, (tq, tk), 0)
        col0 = lax.broadcasted_iota(jnp.int32, (tq, tk), 1)
        r_i = lax.broadcasted_iota(jnp.int32, (tk, tk), 0)
        c_i = lax.broadcasted_iota(jnp.int32, (tk, tk), 1)
        upto = (r_i <= c_i).astype(BF16)
        before = (r_i < c_i).astype(BF16)

        def step(kb, carry):
            dq, c_rem, c_g = carry
            ks = pl.multiple_of(kb * tk, tk)
            kv = k_ref[0, pl.ds(ks, tk), :]
            vv = v_ref[0, pl.ds(ks, tk), :]
            z = _dot(qv, kv, _NT) * scale
            mask = (col0 + ks) < row
            ls, lsn = _log_sigmoid_pair(z)
            rem = jnp.where(mask, lsn, 0.0)
            after = tot_v - (_split_dot(rem, upto) + c_rem)
            w = jnp.where(mask, jnp.exp(ls + after), 0.0)
            dw = _dot(dov, vv, _NT)
            g = dw * w
            g_before = _split_dot(g, before) + c_g
            beta = jnp.exp(ls)
            dz = jnp.where(mask, g * (1.0 - beta) - beta * g_before, 0.0) * scale
            dzb = dz.astype(BF16)
            dq = dq + _dot(dzb, kv)
            dk_ref[0, pl.ds(ks, tk), :] += _dot(dzb, qv, _TN)
            dv_ref[0, pl.ds(ks, tk), :] += _dot(w.astype(BF16), dov, _TN)
            return dq, c_rem + jnp.sum(rem, axis=1, keepdims=True), c_g + jnp.sum(g, axis=1, keepdims=True)

        zero = jnp.zeros((tq, 1), F32)
        dq, _, _ = lax.fori_loop(0, qi + 1, step, (jnp.zeros((tq, hd), F32), zero, zero))
        dq_ref[0] = dq

    qspec = pl.BlockSpec((1, tq, hd), lambda h, i: (h, i, 0))
    kspec = pl.BlockSpec((1, S, hd), lambda h, i: (h, 0, 0))
    full = jax.ShapeDtypeStruct((H, S, hd), F32)
    return pl.pallas_call(
        body, name=name, grid=(H, S // tq),
        in_specs=[qspec, kspec, kspec, qspec, pl.BlockSpec((1, tq, 1), lambda h, i: (h, i, 0))],
        out_specs=[qspec, kspec, kspec],
        out_shape=[full, full, full],
        compiler_params=_params(("parallel", "arbitrary")),
    )(q, k, v, do, tot)


def _softmax_fwd(q, k, v, bias=None, *, name, causal, tq=256, tk=256):
    H, S, hd = q.shape
    Sk = k.shape[1]
    tq, tk = _tile(S, tq), _tile(Sk, tk)
    if causal:
        assert tq == tk and S == Sk
    nk = Sk // tk
    scale = hd ** -0.5
    has_bias = bias is not None

    def body(*refs):
        q_ref, k_ref, v_ref = refs[:3]
        o_ref, lse_ref = refs[-2:]
        qi = pl.program_id(1)
        qv = q_ref[0]
        row = qi * tq + lax.broadcasted_iota(jnp.int32, (tq, tk), 0)
        col0 = lax.broadcasted_iota(jnp.int32, (tq, tk), 1)
        if has_bias:
            b_row = refs[3][0]

        def step(kb, carry):
            m, l, acc = carry
            ks = pl.multiple_of(kb * tk, tk)
            kv = k_ref[0, pl.ds(ks, tk), :]
            vv = v_ref[0, pl.ds(ks, tk), :]
            z = _dot(qv, kv, _NT) * scale
            if has_bias:
                z = z + b_row - refs[4][0, kb]
            if causal:
                z = jnp.where((col0 + ks) <= row, z, NEG_INF)
            m2 = jnp.maximum(m, jnp.max(z, axis=1, keepdims=True))
            p = jnp.exp(z - m2)
            alpha = jnp.exp(m - m2)
            l = alpha * l + jnp.sum(p, axis=1, keepdims=True)
            acc = alpha * acc + _dot(p.astype(BF16), vv)
            return m2, l, acc

        init = (jnp.full((tq, 1), NEG_INF, F32), jnp.zeros((tq, 1), F32), jnp.zeros((tq, hd), F32))
        m, l, acc = lax.fori_loop(0, (qi + 1) if causal else nk, step, init)
        o_ref[0] = acc / l
        lse_ref[0] = m + jnp.log(l)

    qspec = pl.BlockSpec((1, tq, hd), lambda h, i: (h, i, 0))
    kspec = pl.BlockSpec((1, Sk, hd), lambda h, i: (h, 0, 0))
    vspec = pl.BlockSpec((1, tq, 1), lambda h, i: (h, i, 0))
    in_specs = [qspec, kspec, kspec]
    ins = [q, k, v]
    if has_bias:
        in_specs += [vspec, pl.BlockSpec((1, nk, 1, tk), lambda h, i: (h, 0, 0, 0))]
        ins += list(bias)
    return pl.pallas_call(
        body, name=name, grid=(H, S // tq),
        in_specs=in_specs, out_specs=[qspec, vspec],
        out_shape=[jax.ShapeDtypeStruct((H, S, hd), F32), jax.ShapeDtypeStruct((H, S, 1), F32)],
        compiler_params=_params(("parallel", "arbitrary")),
    )(*ins)


def _softmax_bwd(q, k, v, o, lse, do, bias=None, *, name, causal, tq=256, tk=256):
    H, S, hd = q.shape
    Sk = k.shape[1]
    tq, tk = _tile(S, tq), _tile(Sk, tk)
    nk = Sk // tk
    scale = hd ** -0.5
    has_bias = bias is not None
    n_in = 8 if has_bias else 6

    def body(*refs):
        q_ref, k_ref, v_ref, o_ref, lse_ref, do_ref = refs[:6]
        dq_ref, dk_ref, dv_ref = refs[n_in:n_in + 3]
        qi = pl.program_id(1)

        @pl.when(qi == 0)
        def _():
            dk_ref[...] = jnp.zeros_like(dk_ref)
            dv_ref[...] = jnp.zeros_like(dv_ref)
            if has_bias:
                refs[n_in + 4][...] = jnp.zeros_like(refs[n_in + 4])

        qv = q_ref[0]
        do32 = do_ref[0]
        dov = do32.astype(BF16)
        delta = jnp.sum(do32 * o_ref[0], axis=1, keepdims=True)
        lse_v = lse_ref[0]
        row = qi * tq + lax.broadcasted_iota(jnp.int32, (tq, tk), 0)
        col0 = lax.broadcasted_iota(jnp.int32, (tq, tk), 1)
        if has_bias:
            b_row = refs[6][0]

        def step(kb, carry):
            dq, db_row = carry
            ks = pl.multiple_of(kb * tk, tk)
            kv = k_ref[0, pl.ds(ks, tk), :]
            vv = v_ref[0, pl.ds(ks, tk), :]
            z = _dot(qv, kv, _NT) * scale
            if has_bias:
                z = z + b_row - refs[7][0, kb]
            p = jnp.exp(z - lse_v)
            if causal:
                p = jnp.where((col0 + ks) <= row, p, 0.0)
            dp = _dot(dov, vv, _NT)
            dz = p * (dp - delta)
            dzb = dz.astype(BF16)
            dq = dq + _dot(dzb, kv)
            dk_ref[0, pl.ds(ks, tk), :] += _dot(dzb, qv, _TN) * scale
            dv_ref[0, pl.ds(ks, tk), :] += _dot(p.astype(BF16), dov, _TN)
            if has_bias:
                db_row = db_row + jnp.sum(dz, axis=1, keepdims=True)
                refs[n_in + 4][0, kb] += jnp.sum(dz, axis=0, keepdims=True)
            return dq, db_row

        dq, db_row = lax.fori_loop(0, (qi + 1) if causal else nk, step,
                                   (jnp.zeros((tq, hd), F32), jnp.zeros((tq, 1), F32)))
        dq_ref[0] = dq * scale
        if has_bias:
            refs[n_in + 3][0] = db_row

    qspec = pl.BlockSpec((1, tq, hd), lambda h, i: (h, i, 0))
    kspec = pl.BlockSpec((1, Sk, hd), lambda h, i: (h, 0, 0))
    vspec = pl.BlockSpec((1, tq, 1), lambda h, i: (h, i, 0))
    cspec = pl.BlockSpec((1, nk, 1, tk), lambda h, i: (h, 0, 0, 0))
    in_specs = [qspec, kspec, kspec, qspec, vspec, qspec]
    ins = [q, k, v, o, lse, do]
    out_specs = [qspec, kspec, kspec]
    out_shape = [jax.ShapeDtypeStruct((H, S, hd), F32), jax.ShapeDtypeStruct((H, Sk, hd), F32),
                 jax.ShapeDtypeStruct((H, Sk, hd), F32)]
    if has_bias:
        in_specs += [vspec, cspec]
        ins += list(bias)
        out_specs += [vspec, cspec]
        out_shape += [jax.ShapeDtypeStruct((H, S, 1), F32), jax.ShapeDtypeStruct((H, nk, 1, tk), F32)]
    return pl.pallas_call(
        body, name=name, grid=(H, S // tq),
        in_specs=in_specs, out_specs=out_specs, out_shape=out_shape,
        compiler_params=_params(("parallel", "arbitrary")),
    )(*ins)


def _split3_dot(x, tri):
    a = x.astype(BF16)
    r = x - a.astype(F32)
    b = r.astype(BF16)
    c = (r - b.astype(F32)).astype(BF16)
    return _dot(a, tri) + _dot(b, tri) + _dot(c, tri)


def _forget_fwd(logit_t, b_col, *, name, blk=512):
    H, S = logit_t.shape
    blk = _tile(S, blk)

    def body(l_ref, b_ref, f_ref):
        r_i = lax.broadcasted_iota(jnp.int32, (blk, blk), 0)
        c_i = lax.broadcasted_iota(jnp.int32, (blk, blk), 1)
        upto = (r_i <= c_i).astype(BF16)
        carry = jnp.zeros((H, 1), F32)
        for j in range(S // blk):
            u = l_ref[:, j * blk:(j + 1) * blk] + b_ref[...]
            lf, _ = _log_sigmoid_pair(u)
            f_ref[:, j * blk:(j + 1) * blk] = _split3_dot(lf, upto) + carry
            carry = carry + jnp.sum(lf, axis=1, keepdims=True)

    return pl.pallas_call(
        body, name=name,
        out_shape=jax.ShapeDtypeStruct((H, S), F32),
        compiler_params=pltpu.CompilerParams(vmem_limit_bytes=VMEM_LIMIT),
    )(logit_t, b_col)


def _forget_bwd(logit_t, b_col, d_f, *, name, blk=512):
    H, S = logit_t.shape
    blk = _tile(S, blk)

    def body(l_ref, b_ref, df_ref, dl_ref, db_ref):
        r_i = lax.broadcasted_iota(jnp.int32, (blk, blk), 0)
        c_i = lax.broadcasted_iota(jnp.int32, (blk, blk), 1)
        fromon = (r_i >= c_i).astype(BF16)
        carry = jnp.zeros((H, 1), F32)
        db = jnp.zeros((H, 1), F32)
        for j in reversed(range(S // blk)):
            sl = slice(j * blk, (j + 1) * blk)
            dfv = df_ref[:, sl]
            d_lf = _split3_dot(dfv, fromon) + carry
            carry = carry + jnp.sum(dfv, axis=1, keepdims=True)
            u = l_ref[:, sl] + b_ref[...]
            _, lsn = _log_sigmoid_pair(u)
            dl = d_lf * jnp.exp(lsn)
            dl_ref[:, sl] = dl
            db = db + jnp.sum(dl, axis=1, keepdims=True)
        db_ref[...] = db

    return pl.pallas_call(
        body, name=name,
        out_shape=[jax.ShapeDtypeStruct((H, S), F32), jax.ShapeDtypeStruct((H, 1), F32)],
        compiler_params=pltpu.CompilerParams(vmem_limit_bytes=VMEM_LIMIT),
    )(logit_t, b_col, d_f)


def _sigmoid(t):
    return 1.0 / (1.0 + jnp.exp(-t))


def _gate_fwd(o3, w3, proj, D, *, name, tm=256):
    S = proj.shape[0]
    tm = _tile(S, tm)

    def body(o0, o1, o2, w0, w1, w2, g0, g1, g2, out_ref):
        acc = None
        for o_ref, w_ref, g_ref in ((o0, w0, g0), (o1, w1, g1), (o2, w2, g2)):
            t = _sigmoid(g_ref[...]) * _dot(o_ref[...], w_ref[...])
            acc = t if acc is None else acc + t
        out_ref[...] = acc.astype(out_ref.dtype)

    ospec = lambda d: pl.BlockSpec((tm, d), lambda i: (i, 0))
    wspec = lambda w: pl.BlockSpec(w.shape, lambda i: (0, 0))
    gspec = lambda j: pl.BlockSpec((tm, D), lambda i: (i, j))
    return pl.pallas_call(
        body, name=name, grid=(S // tm,),
        in_specs=[ospec(o.shape[1]) for o in o3] + [wspec(w) for w in w3] + [gspec(j) for j in range(3)],
        out_specs=pl.BlockSpec((tm, D), lambda i: (i, 0)),
        out_shape=jax.ShapeDtypeStruct((S, D), BF16),
        compiler_params=_params(("parallel",)),
    )(*o3, *w3, proj, proj, proj)


def _gate_bwd(o3, w3, proj, dmerged, D, *, name, tm=256):
    S = proj.shape[0]
    tm = _tile(S, tm)

    def body(o0, o1, o2, w0, w1, w2, g0, g1, g2, dm_ref, dg_ref, db0, db1, db2, do0, do1, do2):
        dm = dm_ref[...]
        for j, (o_ref, w_ref, g_ref, db_ref, do_ref) in enumerate(
                ((o0, w0, g0, db0, do0), (o1, w1, g1, db1, do1), (o2, w2, g2, db2, do2))):
            s = _sigmoid(g_ref[...])
            br = _dot(o_ref[...], w_ref[...])
            dg_ref[:, j * D:(j + 1) * D] = (dm * br * s * (1.0 - s)).astype(dg_ref.dtype)
            dbr = (dm * s).astype(BF16)
            db_ref[...] = dbr
            do_ref[...] = _dot(dbr, w_ref[...], _NT)

    ospec = lambda d: pl.BlockSpec((tm, d), lambda i: (i, 0))
    wspec = lambda w: pl.BlockSpec(w.shape, lambda i: (0, 0))
    gspec = lambda j: pl.BlockSpec((tm, D), lambda i: (i, j))
    dspec = pl.BlockSpec((tm, D), lambda i: (i, 0))
    return pl.pallas_call(
        body, name=name, grid=(S // tm,),
        in_specs=[ospec(o.shape[1]) for o in o3] + [wspec(w) for w in w3] + [gspec(j) for j in range(3)] + [dspec],
        out_specs=[pl.BlockSpec((tm, 3 * D), lambda i: (i, 0))] + [dspec] * 3 + [ospec(o.shape[1]) for o in o3],
        out_shape=[jax.ShapeDtypeStruct((S, 3 * D), BF16)] + [jax.ShapeDtypeStruct((S, D), BF16)] * 3
        + [jax.ShapeDtypeStruct((S, o.shape[1]), F32) for o in o3],
        compiler_params=_params(("parallel",)),
    )(*o3, *w3, proj, proj, proj, dmerged)


def _loss_sum(dy, D, *, name):
    R, C = dy.shape
    tr = _row_tile(R, C, 2)

    def body(dy_ref, out_ref):
        @pl.when(pl.program_id(0) == 0)
        def _():
            out_ref[...] = jnp.zeros_like(out_ref)

        v = dy_ref[...]
        out_ref[...] += (0.5 * D) * jnp.sum(v * v)

    return pl.pallas_call(
        body, name=name, grid=(R // tr,),
        in_specs=[pl.BlockSpec((tr, C), lambda i: (i, 0))],
        out_specs=pl.BlockSpec((8, 128), lambda i: (0, 0)),
        out_shape=jax.ShapeDtypeStruct((8, 128), F32),
        compiler_params=_params(("arbitrary",)),
    )(dy)[0, 0]


def _adamw_math(w, g, m, v):
    m2 = ADAM_B1 * m + (1.0 - ADAM_B1) * g
    v2 = ADAM_B2 * v + (1.0 - ADAM_B2) * (g * g)
    m_hat = m2 / (1.0 - ADAM_B1 ** ADAM_STEP)
    v_hat = v2 / (1.0 - ADAM_B2 ** ADAM_STEP)
    delta = -ADAM_LR * (m_hat / (jnp.sqrt(v_hat) + ADAM_EPS) + ADAM_WD * w)
    return delta, m2, v2


def _adamw(w, g, m, v, *, name):
    return _ew(_adamw_math, [w, g, m, v], (F32, F32, F32), name=name)


def _adamw_small(w, parts, m, v, *, name):
    n = parts.shape[0]

    def body(w_ref, p_ref, m_ref, v_ref, g_ref, d_ref, m2_ref, v2_ref):
        g = p_ref[0]
        for i in range(1, n):
            g = g + p_ref[i]
        g_ref[...] = g
        d_ref[...], m2_ref[...], v2_ref[...] = _adamw_math(w_ref[...], g, m_ref[...], v_ref[...])

    shp = jax.ShapeDtypeStruct(w.shape, F32)
    return pl.pallas_call(body, name=name, out_shape=[shp] * 4)(w, parts, m, v)


_ANY = pl.BlockSpec(memory_space=pl.ANY)


def _mesh_place():
    x, y, c = lax.axis_index("x"), lax.axis_index("y"), lax.axis_index("c")
    chips = [(1 - x, y), (x, 1 - y), (1 - x, 1 - y)]
    return x, y, c, chips


def _remote(src, dst, sems, i, to):
    send_sems, recv_sems = sems
    return pltpu.make_async_remote_copy(src_ref=src, dst_ref=dst, send_sem=send_sems.at[i], recv_sem=recv_sems.at[i],
                                        device_id=to, device_id_type=MESH_ID)


def _gather_weights(shards):
    n = len(shards)

    def body(*refs):
        ins, outs = refs[:n], refs[n:2 * n]
        sems, local_sems = refs[2 * n:2 * n + 2], refs[2 * n + 2]
        x, y, c, chips = _mesh_place()
        me = 2 * x + y
        sibling = (x, y, 1 - c)
        local = [pltpu.make_async_copy(ins[w], outs[w].at[me], local_sems.at[w]) for w in range(n)]
        for cp in local:
            cp.start()
        sent = []
        for w in range(n):
            for j, chip in enumerate(chips):
                cp = _remote(ins[w].at[c], outs[w].at[me, c], sems, 6 * w + j, (chip[0], chip[1], c))
                cp.start()
                sent.append(cp)
        for w in range(n):
            for j, chip in enumerate(chips):
                got = outs[w].at[2 * chip[0] + chip[1], c]
                _remote(got, got, sems, 6 * w + j, sibling).wait_recv()
                cp = _remote(got, got, sems, 6 * w + 3 + j, sibling)
                cp.start()
                sent.append(cp)
        for w in range(n):
            for j, chip in enumerate(chips):
                got = outs[w].at[2 * chip[0] + chip[1], 1 - c]
                _remote(got, got, sems, 6 * w + 3 + j, sibling).wait_recv()
        for cp in sent:
            cp.wait_send()
        for cp in local:
            cp.wait()

    return pl.pallas_call(
        body, name="gather_weights",
        in_specs=[_ANY] * n, out_specs=[_ANY] * n,
        out_shape=[jax.ShapeDtypeStruct((N_CHIPS,) + s.shape, s.dtype) for s in shards],
        scratch_shapes=[pltpu.SemaphoreType.DMA((6 * n,)), pltpu.SemaphoreType.DMA((6 * n,)),
                        pltpu.SemaphoreType.DMA((n,))],
    )(*shards)


def _exchange_siblings(grads, small):
    n = len(grads)

    def body(*refs):
        ins, small_ref = refs[:n], refs[n]
        own, got, small_out = refs[n + 1:2 * n + 1], refs[2 * n + 1:3 * n + 1], refs[3 * n + 1]
        sems, local_sems = refs[3 * n + 2:3 * n + 4], refs[3 * n + 4]
        x, y, c, chips = _mesh_place()
        sibling = (x, y, 1 - c)
        me = 4 * x + 2 * y + c
        copies = []
        for w in range(n):
            for s in range(N_CHIPS):
                cp = pltpu.make_async_copy(ins[w].at[s, c], own[w].at[s], local_sems.at[N_CHIPS * w + s])
                cp.start()
                copies.append(cp)
        cp = pltpu.make_async_copy(small_ref, small_out.at[me], local_sems.at[N_CHIPS * n])
        cp.start()
        copies.append(cp)
        sent = []
        for w in range(n):
            for s in range(N_CHIPS):
                cp = _remote(ins[w].at[s, 1 - c], got[w].at[s], sems, N_CHIPS * w + s, sibling)
                cp.start()
                sent.append(cp)
        peers = [(x, y, 1 - c)] + [(ch[0], ch[1], cc) for ch in chips for cc in (c, 1 - c)]
        for j, peer in enumerate(peers):
            cp = _remote(small_ref, small_out.at[me], sems, N_CHIPS * n + j, peer)
            cp.start()
            sent.append(cp)
        for w in range(n):
            for s in range(N_CHIPS):
                _remote(got[w].at[s], got[w].at[s], sems, N_CHIPS * w + s, sibling).wait_recv()
        for j, peer in enumerate(peers):
            frm = small_out.at[4 * peer[0] + 2 * peer[1] + peer[2]]
            _remote(frm, frm, sems, N_CHIPS * n + j, peer).wait_recv()
        for cp in sent:
            cp.wait_send()
        for cp in copies:
            cp.wait()

    halves = [jax.ShapeDtypeStruct((N_CHIPS,) + g.shape[2:], g.dtype) for g in grads]
    n_sem = N_CHIPS * n + 7
    outs = pl.pallas_call(
        body, name="exchange_siblings",
        in_specs=[_ANY] * (n + 1), out_specs=[_ANY] * (2 * n + 1),
        out_shape=halves + halves + [jax.ShapeDtypeStruct((8,) + small.shape, small.dtype)],
        scratch_shapes=[pltpu.SemaphoreType.DMA((n_sem,)), pltpu.SemaphoreType.DMA((n_sem,)),
                        pltpu.SemaphoreType.DMA((N_CHIPS * n + 1,))],
    )(*grads, small)
    return outs[:n], outs[n:2 * n], outs[2 * n]


def _exchange_chips(parts):
    n = len(parts)

    def body(*refs):
        ins = refs[:n]
        own, got = refs[n:2 * n], refs[2 * n:3 * n]
        sems, local_sems = refs[3 * n:3 * n + 2], refs[3 * n + 2]
        x, y, c, chips = _mesh_place()
        me = 2 * x + y
        copies = [pltpu.make_async_copy(ins[w].at[me], own[w], local_sems.at[w]) for w in range(n)]
        for cp in copies:
            cp.start()
        sent = []
        for w in range(n):
            for j, chip in enumerate(chips):
                cp = _remote(ins[w].at[2 * chip[0] + chip[1]], got[w].at[j], sems, 3 * w + j, (chip[0], chip[1], c))
                cp.start()
                sent.append(cp)
        for w in range(n):
            for j in range(3):
                _remote(got[w].at[j], got[w].at[j], sems, 3 * w + j, (x, y, c)).wait_recv()
        for cp in sent:
            cp.wait_send()
        for cp in copies:
            cp.wait()

    outs = pl.pallas_call(
        body, name="exchange_chips",
        in_specs=[_ANY] * n, out_specs=[_ANY] * (2 * n),
        out_shape=[jax.ShapeDtypeStruct(p.shape[1:], p.dtype) for p in parts]
        + [jax.ShapeDtypeStruct((3,) + p.shape[1:], p.dtype) for p in parts],
        scratch_shapes=[pltpu.SemaphoreType.DMA((3 * n,)), pltpu.SemaphoreType.DMA((3 * n,)),
                        pltpu.SemaphoreType.DMA((n,))],
    )(*parts)
    return outs[:n], outs[n:]


def _share_halves(halves):
    n = len(halves)

    def body(*refs):
        ins, outs = refs[:n], refs[n:2 * n]
        sems, local_sems = refs[2 * n:2 * n + 2], refs[2 * n + 2]
        x, y, c, _ = _mesh_place()
        sibling = (x, y, 1 - c)
        copies = [pltpu.make_async_copy(ins[w], outs[w].at[c], local_sems.at[w]) for w in range(n)]
        for cp in copies:
            cp.start()
        sent = [_remote(ins[w], outs[w].at[c], sems, w, sibling) for w in range(n)]
        for cp in sent:
            cp.start()
        for w in range(n):
            _remote(outs[w].at[1 - c], outs[w].at[1 - c], sems, w, sibling).wait_recv()
        for cp in sent:
            cp.wait_send()
        for cp in copies:
            cp.wait()

    return pl.pallas_call(
        body, name="share_halves",
        in_specs=[_ANY] * n, out_specs=[_ANY] * n,
        out_shape=[jax.ShapeDtypeStruct((2,) + h.shape, h.dtype) for h in halves],
        scratch_shapes=[pltpu.SemaphoreType.DMA((n,)), pltpu.SemaphoreType.DMA((n,)),
                        pltpu.SemaphoreType.DMA((n,))],
    )(*halves)


def _to_heads(t, n, d):
    s = t.shape[0]
    return t.reshape(s, n, d).transpose(1, 0, 2)


def _from_heads(t):
    h, s, d = t.shape
    return t.transpose(1, 0, 2).reshape(s, h * d)


def _head_norm_fwd(t, g, *, name):
    h, s, d = t.shape
    return _rmsnorm_fwd(t.reshape(h * s, d), g, BF16, name=name).reshape(h, s, d)


def _head_norm_bwd(t, g, dy, *, name):
    h, s, d = t.shape
    dx, dg = _rmsnorm_bwd(t.reshape(h * s, d), g, dy.reshape(h * s, d), name=name)
    return dx.reshape(h, s, d), dg


def _local_step(x, mem, target, small, W):
    S, D = x.shape
    o_gate, o_qkv, o_mq, o_f = 0, 3 * D, 3 * D + 2 * 3 * D_SB, 3 * D + 2 * 3 * D_SB + D_MEM
    tq = 256

    h = _rmsnorm_fwd(x, small["g_mix_norm"], BF16, name="mix_norm")
    proj = _mm(h, W["w_in"], name="in_proj", tn=768)
    qkv = lambda j: proj[:, o_qkv + j * D_SB:o_qkv + (j + 1) * D_SB]
    sb_q, sb_k, sb_v = [_to_heads(qkv(j), SB_HEADS, HD).astype(BF16) for j in range(3)]
    fx_q, fx_k = [_to_heads(qkv(j), FOX_HEADS, HD) for j in (3, 4)]
    fx_v = _to_heads(qkv(5), FOX_HEADS, HD).astype(BF16)
    m_q = _to_heads(proj[:, o_mq:o_mq + D_MEM], MEM_HEADS, MEM_HD)
    f_logit_t = proj[:, o_f:o_f + FOX_HEADS].T
    b_col = small["b_forget"].reshape(FOX_HEADS, 1)

    o_sb, sb_tot = _sb_fwd(sb_q, sb_k, sb_v, name="sb_fwd", tq=tq)

    fq = _head_norm_fwd(fx_q, small["g_fox_q"], name="fox_q_norm")
    fk = _head_norm_fwd(fx_k, small["g_fox_k"], name="fox_k_norm")
    f_cum = _forget_fwd(f_logit_t, b_col, name="forget_fwd")
    tkf = _tile(S, tq)
    f_bias = (f_cum.reshape(FOX_HEADS, S, 1), f_cum.reshape(FOX_HEADS, S // tkf, 1, tkf))
    o_fox, fox_lse = _softmax_fwd(fq, fk, fx_v, f_bias, name="fox_fwd", causal=True, tq=tq, tk=tq)

    mh = _rmsnorm_fwd(mem, small["g_mem_norm"], BF16, name="mem_norm")
    mkv = _mm(mh, W["w_mem_kv"], name="mem_kv")
    mk_raw = _to_heads(mkv[:, :D_MEM], MEM_HEADS, MEM_HD)
    mv = _to_heads(mkv[:, D_MEM:], MEM_HEADS, MEM_HD).astype(BF16)
    mq = _head_norm_fwd(m_q, small["g_mem_q"], name="mem_q_norm")
    mk = _head_norm_fwd(mk_raw, small["g_mem_k"], name="mem_k_norm")
    o_mem, mem_lse = _softmax_fwd(mq, mk, mv, name="mem_fwd", causal=False, tq=tq, tk=256)

    o3 = [_from_heads(o).astype(BF16) for o in (o_sb, o_fox, o_mem)]
    w3 = [W["w_branch_sb"], W["w_branch_fox"], W["w_branch_mem"]]
    merged = _gate_fwd(o3, w3, proj, D, name="gate_fwd")
    x1 = _mm(merged, W["w_out"], name="out_proj", extras=(x,), epilogue=lambda acc, res: (res + acc,))
    h2 = _rmsnorm_fwd(x1, small["g_mlp_norm"], BF16, name="mlp_norm")

    def relu2(acc):
        u = jnp.maximum(acc, 0.0)
        return u, u * u

    u, a = _mm(h2, W["w_ff_up"], name="ff_up", out_dtypes=(BF16, BF16), epilogue=relu2)
    dy = _mm(a, W["w_ff_down"], name="ff_down", extras=(x1, target),
             epilogue=lambda acc, res, tgt: ((res + acc - tgt) * (1.0 / D),))
    loss = _loss_sum(dy, D, name="loss")

    G = {}
    du = _mm(dy, W["w_ff_down"], name="d_ff_act", tb=True, out_dtypes=(BF16,), extras=(u,),
             epilogue=lambda acc, uu: (acc * (2.0 * uu.astype(F32)),))
    G["w_ff_down"] = _mm(a, dy, name="d_w_ff_down", ta=True, out_dtypes=(BF16,))
    G["w_ff_up"] = _mm(h2, du, name="d_w_ff_up", ta=True, out_dtypes=(BF16,))
    dh2 = _mm(du, W["w_ff_up"], name="d_mlp_in", tb=True)
    dx1, dg_mlp = _rmsnorm_bwd(x1, small["g_mlp_norm"], dh2, add=dy, name="d_mlp_norm")
    dmerged = _mm(dx1, W["w_out"], name="d_merged", tb=True)
    G["w_out"] = _mm(merged, dx1, name="d_w_out", ta=True, out_dtypes=(BF16,))
    dgate, db0, db1, db2, do_sb, do_fox, do_mem = _gate_bwd(o3, w3, proj, dmerged, D, name="gate_bwd")
    for nm, o, db in zip(("w_branch_sb", "w_branch_fox", "w_branch_mem"), o3, (db0, db1, db2)):
        G[nm] = _mm(o, db, name="d_" + nm, ta=True, out_dtypes=(BF16,))

    dsb = _sb_bwd(sb_q, sb_k, sb_v, _to_heads(do_sb, SB_HEADS, HD), sb_tot, name="sb_bwd", tq=tq)
    dfq, dfk, dfv, df_row, df_col = _softmax_bwd(fq, fk, fx_v, o_fox, fox_lse, _to_heads(do_fox, FOX_HEADS, HD),
                                                 f_bias, name="fox_bwd", causal=True, tq=tq, tk=tq)
    dfx_q, dg_fox_q = _head_norm_bwd(fx_q, small["g_fox_q"], dfq, name="d_fox_q_norm")
    dfx_k, dg_fox_k = _head_norm_bwd(fx_k, small["g_fox_k"], dfk, name="d_fox_k_norm")
    d_fcum = df_row.reshape(FOX_HEADS, S) - df_col.reshape(FOX_HEADS, S)
    d_flogit_t, db_forget = _forget_bwd(f_logit_t, b_col, d_fcum, name="forget_bwd")

    dmq_n, dmk_n, dmv = _softmax_bwd(mq, mk, mv, o_mem, mem_lse, _to_heads(do_mem, MEM_HEADS, MEM_HD),
                                     name="mem_bwd", causal=False, tq=tq, tk=256)
    dm_q, dg_mem_q = _head_norm_bwd(m_q, small["g_mem_q"], dmq_n, name="d_mem_q_norm")
    dmk_raw, dg_mem_k = _head_norm_bwd(mk_raw, small["g_mem_k"], dmk_n, name="d_mem_k_norm")
    dmkv = jnp.concatenate([_from_heads(dmk_raw), _from_heads(dmv)], axis=1).astype(BF16)
    G["w_mem_kv"] = _mm(mh, dmkv, name="d_w_mem_kv", ta=True, out_dtypes=(BF16,))
    dmh = _mm(dmkv, W["w_mem_kv"], name="d_mem_h", tb=True)
    _, dg_mem = _rmsnorm_bwd(mem, small["g_mem_norm"], dmh, name="d_mem_norm")

    dproj = jnp.concatenate(
        [dgate] + [_from_heads(t).astype(BF16) for t in (*dsb, dfx_q, dfx_k, dfv, dm_q)]
        + [d_flogit_t.T.astype(BF16), jnp.zeros((S, F_PAD - FOX_HEADS), BF16)], axis=1)
    G["w_in"] = _mm(h, dproj, name="d_w_in", ta=True, out_dtypes=(BF16,), tn=768)
    dh = _mm(dproj, W["w_in"], name="d_mix_in", tb=True, tk=768)
    grad_x, dg_mix = _rmsnorm_bwd(x, small["g_mix_norm"], dh, add=dx1, name="d_mix_norm")

    small_grads = dict(g_mix_norm=dg_mix, g_mem_norm=dg_mem, b_forget=db_forget.reshape(1, FOX_HEADS),
                       g_fox_q=dg_fox_q, g_fox_k=dg_fox_k, g_mem_q=dg_mem_q, g_mem_k=dg_mem_k, g_mlp_norm=dg_mlp)
    return loss, grad_x, G, small_grads


BIG = ("w_in", "w_mem_kv", "w_branch_sb", "w_branch_fox", "w_branch_mem", "w_out", "w_ff_up", "w_ff_down")
COLUMN_SHARDED = ("w_in", "w_branch_sb", "w_branch_fox", "w_branch_mem", "w_ff_up")
SMALL = ("g_mix_norm", "g_mem_norm", "b_forget", "g_fox_q", "g_fox_k", "g_mem_q", "g_mem_k", "g_mlp_norm")
ORDER = ("g_mix_norm", "g_mem_norm", "w_in", "b_forget", "g_fox_q", "g_fox_k", "g_mem_q", "g_mem_k", "w_mem_kv",
         "w_branch_sb", "w_branch_fox", "w_branch_mem", "w_out", "g_mlp_norm", "w_ff_up", "w_ff_down")


def _unshard(name, gathered):
    n, _, rh, c = gathered.shape
    t = gathered.reshape(n, 2 * rh, c)
    if name in COLUMN_SHARDED:
        return t.transpose(1, 0, 2).reshape(2 * rh, n * c)
    return t.reshape(n * 2 * rh, c)


def _reshard(name, full):
    if name in COLUMN_SHARDED:
        r, c = full.shape
        t = full.reshape(r, N_CHIPS, c // N_CHIPS).transpose(1, 0, 2)
    else:
        r, c = full.shape[0] // N_CHIPS, full.shape[1]
        t = full.reshape(N_CHIPS, r, c)
    return t.reshape(N_CHIPS, 2, t.shape[1] // 2, t.shape[2])


def _pad_in_proj(w_in, D):
    n_qkv = 6 * D_SB
    o_mq = n_qkv + FOX_HEADS
    o_gate = o_mq + D_MEM
    return jnp.concatenate([w_in[:, o_gate:], w_in[:, :n_qkv], w_in[:, o_mq:o_gate], w_in[:, n_qkv:o_mq],
                            jnp.zeros((w_in.shape[0], F_PAD - FOX_HEADS), w_in.dtype)], axis=1)


def _unpad_in_proj(g, D):
    n_qkv = 6 * D_SB
    o_qkv, o_mq, o_f = 3 * D, 3 * D + n_qkv, 3 * D + n_qkv + D_MEM
    return jnp.concatenate([g[:, o_qkv:o_mq], g[:, o_f:o_f + FOX_HEADS], g[:, o_mq:o_f], g[:, :o_qkv]], axis=1)


def _pack_small(vals):
    width = max(vals[k].shape[1] for k in SMALL)
    return jnp.concatenate([jnp.pad(vals[k].astype(F32), ((0, 0), (0, width - vals[k].shape[1]))) for k in SMALL], axis=0)


def _unpack_small(packed, like):
    return {k: packed[i:i + 1, :like[k].shape[1]] for i, k in enumerate(SMALL)}


def kernel(x, mem, g_mix_norm, g_mem_norm, w_in, b_forget, g_fox_q, g_fox_k, g_mem_q, g_mem_k, w_mem_kv, w_branch_sb, w_branch_fox, w_branch_mem, w_out, g_mlp_norm, w_ff_up, w_ff_down, loss_target, m_g_mix_norm, m_g_mem_norm, m_w_in, m_b_forget, m_g_fox_q, m_g_fox_k, m_g_mem_q, m_g_mem_k, m_w_mem_kv, m_w_branch_sb, m_w_branch_fox, m_w_branch_mem, m_w_out, m_g_mlp_norm, m_w_ff_up, m_w_ff_down, v_g_mix_norm, v_g_mem_norm, v_w_in, v_b_forget, v_g_fox_q, v_g_fox_k, v_g_mem_q, v_g_mem_k, v_w_mem_kv, v_w_branch_sb, v_w_branch_fox, v_w_branch_mem, v_w_out, v_g_mlp_norm, v_w_ff_up, v_w_ff_down):
    given = dict(locals())
    D = x.shape[-1]
    weights = {k: given[k] for k in ORDER}
    moms = {k: given["m_" + k] for k in ORDER}
    vars_ = {k: given["v_" + k] for k in ORDER}

    shards = []
    for k in BIG:
        w = weights[k][0].astype(BF16)
        shards.append(w.reshape(2, w.shape[0] // 2, w.shape[1]))
    gathered = _gather_weights(shards)
    W = {k: _unshard(k, g) for k, g in zip(BIG, gathered)}
    W["w_in"] = _pad_in_proj(W["w_in"], D)

    small = {k: weights[k] for k in SMALL}
    loss_part, grad_x, G, small_grads = _local_step(x[0], mem[0], loss_target[0], small, W)
    G["w_in"] = _unpad_in_proj(G["w_in"], D)

    stacked = [_reshard(k, G[k]) for k in BIG]
    own, sib, small_parts = _exchange_siblings(stacked, _pack_small(small_grads))
    add2 = lambda p, q: (p.astype(F32) + q.astype(F32),)
    parts = []
    for k, p, q in zip(BIG, own, sib):
        n, r, c = p.shape
        parts.append(_ew(add2, [p.reshape(n * r, c), q.reshape(n * r, c)], (BF16,), name="sum_pair_" + k).reshape(n, r, c))
    own2, got2 = _exchange_chips(parts)
    add4 = lambda p, q0, q1, q2: (((p.astype(F32) + q0.astype(F32)) + q1.astype(F32)) + q2.astype(F32),)
    halves = [_ew(add4, [p, q[0], q[1], q[2]], (F32,), name="sum_chips_" + k) for k, p, q in zip(BIG, own2, got2)]
    reduced = _share_halves(halves)

    grads, deltas, new_m, new_v = {}, {}, {}, {}
    for k, g in zip(BIG, reduced):
        shp = weights[k].shape
        g2 = g.reshape(shp[1], shp[2])
        d, m2, v2 = _adamw(weights[k][0], g2, moms[k][0], vars_[k][0], name="adamw_" + k)
        grads[k], deltas[k], new_m[k], new_v[k] = g2.reshape(shp), d.reshape(shp), m2.reshape(shp), v2.reshape(shp)
    sg, sd, sm, sv = _adamw_small(_pack_small(small), small_parts, _pack_small({k: moms[k] for k in SMALL}),
                                  _pack_small({k: vars_[k] for k in SMALL}), name="adamw_small")
    for dst, packed in ((grads, sg), (deltas, sd), (new_m, sm), (new_v, sv)):
        dst.update(_unpack_small(packed, small))

    loss = lax.psum(loss_part, ("x", "y", "c"))
    return (loss, grad_x[None], *[grads[k] for k in ORDER], *[deltas[k] for k in ORDER],
            *[new_m[k] for k in ORDER], *[new_v[k] for k in ORDER])
```

```python
import functools

import jax
import jax.numpy as jnp
from jax import lax
from jax.experimental import pallas as pl
from jax.experimental.pallas import tpu as pltpu

F32 = jnp.float32
BF16 = jnp.bfloat16
MESH_ID = pl.DeviceIdType.MESH

HD = 64
SB_HEADS = 8
FOX_HEADS = 8
MEM_HEADS = 4
MEM_HD = 128
D_SB = SB_HEADS * HD
D_FOX = FOX_HEADS * HD
D_MEM = MEM_HEADS * MEM_HD
EPS = 1e-6
NEG_INF = -1e30

ADAM_LR = 0.001
ADAM_B1 = 0.9
ADAM_B2 = 0.999
ADAM_EPS = 1e-08
ADAM_WD = 0.01
ADAM_STEP = 10

N_CHIPS = 4
VMEM_LIMIT = 56 * 1024 * 1024

F_PAD = 256


def _tile(n, target, align=128):
    if n <= target:
        return n
    best = None
    t = align
    while t <= target:
        if n % t == 0:
            best = t
        t += align
    assert best is not None, (n, target, align)
    return best


def _params(sem):
    return pltpu.CompilerParams(dimension_semantics=sem, vmem_limit_bytes=VMEM_LIMIT)


def _mm(a, b, *, name, ta=False, tb=False, out_dtypes=(F32,), epilogue=None, extras=(),
        tm=512, tn=512, tk=1024):
    if ta:
        K, M = a.shape
    else:
        M, K = a.shape
    if tb:
        N, K2 = b.shape
    else:
        K2, N = b.shape
    assert K == K2, (a.shape, b.shape, ta, tb)
    tm, tn, tk = _tile(M, tm), _tile(N, tn), _tile(K, tk)
    nk = K // tk
    n_extra, n_out = len(extras), len(out_dtypes)
    if epilogue is None:
        epilogue = lambda acc: (acc,)
    dims = (((0 if ta else 1,), (1 if tb else 0,)), ((), ()))

    def body(a_ref, b_ref, *rest):
        extra_refs = rest[:n_extra]
        out_refs = rest[n_extra:n_extra + n_out]
        acc_ref = rest[n_extra + n_out]
        k = pl.program_id(2)

        @pl.when(k == 0)
        def _():
            acc_ref[...] = jnp.zeros_like(acc_ref)

        acc_ref[...] += lax.dot_general(a_ref[...].astype(BF16), b_ref[...].astype(BF16), dims,
                                        preferred_element_type=F32)

        @pl.when(k == nk - 1)
        def _():
            outs = epilogue(acc_ref[...], *[r[...] for r in extra_refs])
            for o_ref, o in zip(out_refs, outs):
                o_ref[...] = o.astype(o_ref.dtype)

    a_spec = pl.BlockSpec((tk, tm), lambda i, j, k: (k, i)) if ta else pl.BlockSpec((tm, tk), lambda i, j, k: (i, k))
    b_spec = pl.BlockSpec((tn, tk), lambda i, j, k: (j, k)) if tb else pl.BlockSpec((tk, tn), lambda i, j, k: (k, j))
    mn_spec = pl.BlockSpec((tm, tn), lambda i, j, k: (i, j))
    outs = pl.pallas_call(
        body, name=name,
        grid=(M // tm, N // tn, nk),
        in_specs=[a_spec, b_spec] + [mn_spec] * n_extra,
        out_specs=[mn_spec] * n_out,
        out_shape=[jax.ShapeDtypeStruct((M, N), dt) for dt in out_dtypes],
        scratch_shapes=[pltpu.VMEM((tm, tn), F32)],
        compiler_params=_params(("parallel", "parallel", "arbitrary")),
    )(a, b, *extras)
    return outs if n_out > 1 else outs[0]


def _row_tile(rows, cols, n_arrays):
    budget = 10 * 1024 * 1024
    cols_padded = -(-cols // 128) * 128
    target = max(16, budget // (cols_padded * 4 * n_arrays * 2))
    return _tile(rows, target, align=16)


def _ew(fn, ins, out_dtypes, *, name):
    R, C = ins[0].shape
    n_in, n_out = len(ins), len(out_dtypes)
    tr = _row_tile(R, C, n_in + n_out)

    def body(*refs):
        outs = fn(*[r[...] for r in refs[:n_in]])
        for o_ref, o in zip(refs[n_in:], outs):
            o_ref[...] = o.astype(o_ref.dtype)

    spec = pl.BlockSpec((tr, C), lambda i: (i, 0))
    outs = pl.pallas_call(
        body, name=name, grid=(R // tr,),
        in_specs=[spec] * n_in, out_specs=[spec] * n_out,
        out_shape=[jax.ShapeDtypeStruct((R, C), dt) for dt in out_dtypes],
        compiler_params=_params(("parallel",)),
    )(*ins)
    return outs if n_out > 1 else outs[0]


def _rmsnorm_fwd(x, g, out_dtype, *, name):
    R, d = x.shape
    tr = _row_tile(R, d, 3)

    def body(x_ref, g_ref, o_ref):
        xv = x_ref[...].astype(F32)
        r = lax.rsqrt(jnp.mean(xv * xv, axis=-1, keepdims=True) + EPS)
        o_ref[...] = (xv * r * g_ref[...]).astype(o_ref.dtype)

    return pl.pallas_call(
        body, name=name, grid=(R // tr,),
        in_specs=[pl.BlockSpec((tr, d), lambda i: (i, 0)), pl.BlockSpec((1, d), lambda i: (0, 0))],
        out_specs=pl.BlockSpec((tr, d), lambda i: (i, 0)),
        out_shape=jax.ShapeDtypeStruct((R, d), out_dtype),
        compiler_params=_params(("parallel",)),
    )(x, g)


def _rmsnorm_bwd(x, g, dy, add=None, *, name):
    R, d = x.shape
    has_add = add is not None
    tr = _row_tile(R, d, 5)

    def body(*refs):
        x_ref, g_ref, dy_ref = refs[:3]
        add_ref = refs[3] if has_add else None
        dx_ref, dg_ref = refs[-2:]
        xv = x_ref[...].astype(F32)
        dyv = dy_ref[...].astype(F32)
        r = lax.rsqrt(jnp.mean(xv * xv, axis=-1, keepdims=True) + EPS)
        xh = xv * r
        dyg = dyv * g_ref[...]
        c = jnp.mean(dyg * xh, axis=-1, keepdims=True)
        dx = r * (dyg - xh * c)
        if has_add:
            dx = dx + add_ref[...]
        dx_ref[...] = dx

        @pl.when(pl.program_id(0) == 0)
        def _():
            dg_ref[...] = jnp.zeros_like(dg_ref)

        dg_ref[...] += jnp.sum(dyv * xh, axis=0, keepdims=True)

    row = pl.BlockSpec((tr, d), lambda i: (i, 0))
    vec = pl.BlockSpec((1, d), lambda i: (0, 0))
    ins = [x, g, dy] + ([add] if has_add else [])
    return pl.pallas_call(
        body, name=name, grid=(R // tr,),
        in_specs=[row, vec, row] + ([row] if has_add else []),
        out_specs=[row, vec],
        out_shape=[jax.ShapeDtypeStruct((R, d), F32), jax.ShapeDtypeStruct((1, d), F32)],
        compiler_params=_params(("arbitrary",)),
    )(*ins)


_NT = (((1,), (1,)), ((), ()))
_TN = (((0,), (0,)), ((), ()))


def _dot(a, b, dims=(((1,), (0,)), ((), ()))):
    return lax.dot_general(a, b, dims, preferred_element_type=F32)


def _split_dot(x, tri):
    hi = x.astype(BF16)
    lo = (x - hi.astype(F32)).astype(BF16)
    return _dot(hi, tri) + _dot(lo, tri)


def _log_sigmoid_pair(z):
    sp = jnp.log(1.0 + jnp.exp(-jnp.abs(z)))
    return jnp.minimum(z, 0.0) - sp, jnp.minimum(-z, 0.0) - sp


def _sb_fwd(q, k, v, *, name, tq=256):
    H, S, hd = q.shape
    tq = _tile(S, tq)
    tk = tq
    scale = hd ** -0.5

    def body(q_ref, k_ref, v_ref, o_ref, tot_ref):
        qi = pl.program_id(1)
        qv = q_ref[0]
        row = qi * tq + lax.broadcasted_iota(jnp.int32, (tq, tk), 0)
        col0 = lax.broadcasted_iota(jnp.int32, (tq, tk), 1)
        later = (lax.broadcasted_iota(jnp.int32, (tk, tk), 0) > lax.broadcasted_iota(jnp.int32, (tk, tk), 1)).astype(BF16)

        def step(i, carry):
            acc, c_rem = carry
            kb = qi - i
            ks = pl.multiple_of(kb * tk, tk)
            kv = k_ref[0, pl.ds(ks, tk), :]
            vv = v_ref[0, pl.ds(ks, tk), :]
            z = _dot(qv, kv, _NT) * scale
            mask = (col0 + ks) < row
            ls, lsn = _log_sigmoid_pair(z)
            rem = jnp.where(mask, lsn, 0.0)
            after = _split_dot(rem, later) + c_rem
            w = jnp.where(mask, jnp.exp(ls + after), 0.0)
            acc = acc + _dot(w.astype(BF16), vv)
            c_rem = c_rem + jnp.sum(rem, axis=1, keepdims=True)
            return acc, c_rem

        acc, c_rem = lax.fori_loop(0, qi + 1, step, (jnp.zeros((tq, hd), F32), jnp.zeros((tq, 1), F32)))
        o_ref[0] = acc
        tot_ref[0] = c_rem

    qspec = pl.BlockSpec((1, tq, hd), lambda h, i: (h, i, 0))
    kspec = pl.BlockSpec((1, S, hd), lambda h, i: (h, 0, 0))
    return pl.pallas_call(
        body, name=name, grid=(H, S // tq),
        in_specs=[qspec, kspec, kspec],
        out_specs=[qspec, pl.BlockSpec((1, tq, 1), lambda h, i: (h, i, 0))],
        out_shape=[jax.ShapeDtypeStruct((H, S, hd), F32), jax.ShapeDtypeStruct((H, S, 1), F32)],
        compiler_params=_params(("parallel", "arbitrary")),
    )(q, k, v)


def _sb_bwd(q, k, v, do, tot, *, name, tq=256):
    H, S, hd = q.shape
    tq = _tile(S, tq)
    tk = tq
    scale = hd ** -0.5

    def body(q_ref, k_ref, v_ref, do_ref, tot_ref, dq_ref, dk_ref, dv_ref):
        qi = pl.program_id(1)

        @pl.when(qi == 0)
        def _():
            dk_ref[...] = jnp.zeros_like(dk_ref)
            dv_ref[...] = jnp.zeros_like(dv_ref)

        qv = q_ref[0]
        dov = do_ref[0].astype(BF16)
        tot_v = tot_ref[0]
        row = qi * tq + lax.broadcasted_iota(jnp.int32, (tq, tk), 0)
        col0 = lax.broadcasted_iota(jnp.int32, (tq, tk), 1)
        r_i = lax.broadcasted_iota(jnp.int32, (tk, tk), 0)
        c_i = lax.broadcasted_iota(jnp.int32, (tk, tk), 1)
        upto = (r_i <= c_i).astype(BF16)
        before = (r_i < c_i).astype(BF16)

        def step(kb, carry):
            dq, c_rem, c_g = carry
            ks = pl.multiple_of(kb * tk, tk)
            kv = k_ref[0, pl.ds(ks, tk), :]
            vv = v_ref[0, pl.ds(ks, tk), :]
            z = _dot(qv, kv, _NT) * scale
            mask = (col0 + ks) < row
            ls, lsn = _log_sigmoid_pair(z)
            rem = jnp.where(mask, lsn, 0.0)
            after = tot_v - (_split_dot(rem, upto) + c_rem)
            w = jnp.where(mask, jnp.exp(ls + after), 0.0)
            dw = _dot(dov, vv, _NT)
            g = dw * w
            g_before = _split_dot(g, before) + c_g
            beta = jnp.exp(ls)
            dz = jnp.where(mask, g * (1.0 - beta) - beta * g_before, 0.0) * scale
            dzb = dz.astype(BF16)
            dq = dq + _dot(dzb, kv)
            dk_ref[0, pl.ds(ks, tk), :] += _dot(dzb, qv, _TN)
            dv_ref[0, pl.ds(ks, tk), :] += _dot(w.astype(BF16), dov, _TN)
            return dq, c_rem + jnp.sum(rem, axis=1, keepdims=True), c_g + jnp.sum(g, axis=1, keepdims=True)

        zero = jnp.zeros((tq, 1), F32)
        dq, _, _ = lax.fori_loop(0, qi + 1, step, (jnp.zeros((tq, hd), F32), zero, zero))
        dq_ref[0] = dq

    qspec = pl.BlockSpec((1, tq, hd), lambda h, i: (h, i, 0))
    kspec = pl.BlockSpec((1, S, hd), lambda h, i: (h, 0, 0))
    full = jax.ShapeDtypeStruct((H, S, hd), F32)
    return pl.pallas_call(
        body, name=name, grid=(H, S // tq),
        in_specs=[qspec, kspec, kspec, qspec, pl.BlockSpec((1, tq, 1), lambda h, i: (h, i, 0))],
        out_specs=[qspec, kspec, kspec],
        out_shape=[full, full, full],
        compiler_params=_params(("parallel", "arbitrary")),
    )(q, k, v, do, tot)


def _softmax_fwd(q, k, v, bias=None, *, name, causal, tq=256, tk=256):
    H, S, hd = q.shape
    Sk = k.shape[1]
    tq, tk = _tile(S, tq), _tile(Sk, tk)
    if causal:
        assert tq == tk and S == Sk
    nk = Sk // tk
    scale = hd ** -0.5
    has_bias = bias is not None

    def body(*refs):
        q_ref, k_ref, v_ref = refs[:3]
        o_ref, lse_ref = refs[-2:]
        qi = pl.program_id(1)
        qv = q_ref[0]
        row = qi * tq + lax.broadcasted_iota(jnp.int32, (tq, tk), 0)
        col0 = lax.broadcasted_iota(jnp.int32, (tq, tk), 1)
        if has_bias:
            b_row = refs[3][0]

        def step(kb, carry):
            m, l, acc = carry
            ks = pl.multiple_of(kb * tk, tk)
            kv = k_ref[0, pl.ds(ks, tk), :]
            vv = v_ref[0, pl.ds(ks, tk), :]
            z = _dot(qv, kv, _NT) * scale
            if has_bias:
                z = z + b_row - refs[4][0, kb]
            if causal:
                z = jnp.where((col0 + ks) <= row, z, NEG_INF)
            m2 = jnp.maximum(m, jnp.max(z, axis=1, keepdims=True))
            p = jnp.exp(z - m2)
            alpha = jnp.exp(m - m2)
            l = alpha * l + jnp.sum(p, axis=1, keepdims=True)
            acc = alpha * acc + _dot(p.astype(BF16), vv)
            return m2, l, acc

        init = (jnp.full((tq, 1), NEG_INF, F32), jnp.zeros((tq, 1), F32), jnp.zeros((tq, hd), F32))
        m, l, acc = lax.fori_loop(0, (qi + 1) if causal else nk, step, init)
        o_ref[0] = acc / l
        lse_ref[0] = m + jnp.log(l)

    qspec = pl.BlockSpec((1, tq, hd), lambda h, i: (h, i, 0))
    kspec = pl.BlockSpec((1, Sk, hd), lambda h, i: (h, 0, 0))
    vspec = pl.BlockSpec((1, tq, 1), lambda h, i: (h, i, 0))
    in_specs = [qspec, kspec, kspec]
    ins = [q, k, v]
    if has_bias:
        in_specs += [vspec, pl.BlockSpec((1, nk, 1, tk), lambda h, i: (h, 0, 0, 0))]
        ins += list(bias)
    return pl.pallas_call(
        body, name=name, grid=(H, S // tq),
        in_specs=in_specs, out_specs=[qspec, vspec],
        out_shape=[jax.ShapeDtypeStruct((H, S, hd), F32), jax.ShapeDtypeStruct((H, S, 1), F32)],
        compiler_params=_params(("parallel", "arbitrary")),
    )(*ins)


def _softmax_bwd(q, k, v, o, lse, do, bias=None, *, name, causal, tq=256, tk=256):
    H, S, hd = q.shape
    Sk = k.shape[1]
    tq, tk = _tile(S, tq), _tile(Sk, tk)
    nk = Sk // tk
    scale = hd ** -0.5
    has_bias = bias is not None
    n_in = 8 if has_bias else 6

    def body(*refs):
        q_ref, k_ref, v_ref, o_ref, lse_ref, do_ref = refs[:6]
        dq_ref, dk_ref, dv_ref = refs[n_in:n_in + 3]
        qi = pl.program_id(1)

        @pl.when(qi == 0)
        def _():
            dk_ref[...] = jnp.zeros_like(dk_ref)
            dv_ref[...] = jnp.zeros_like(dv_ref)
            if has_bias:
                refs[n_in + 4][...] = jnp.zeros_like(refs[n_in + 4])

        qv = q_ref[0]
        do32 = do_ref[0]
        dov = do32.astype(BF16)
        delta = jnp.sum(do32 * o_ref[0], axis=1, keepdims=True)
        lse_v = lse_ref[0]
        row = qi * tq + lax.broadcasted_iota(jnp.int32, (tq, tk), 0)
        col0 = lax.broadcasted_iota(jnp.int32, (tq, tk), 1)
        if has_bias:
            b_row = refs[6][0]

        def step(kb, carry):
            dq, db_row = carry
            ks = pl.multiple_of(kb * tk, tk)
            kv = k_ref[0, pl.ds(ks, tk), :]
            vv = v_ref[0, pl.ds(ks, tk), :]
            z = _dot(qv, kv, _NT) * scale
            if has_bias:
                z = z + b_row - refs[7][0, kb]
            p = jnp.exp(z - lse_v)
            if causal:
                p = jnp.where((col0 + ks) <= row, p, 0.0)
            dp = _dot(dov, vv, _NT)
            dz = p * (dp - delta)
            dzb = dz.astype(BF16)
            dq = dq + _dot(dzb, kv)
            dk_ref[0, pl.ds(ks, tk), :] += _dot(dzb, qv, _TN) * scale
            dv_ref[0, pl.ds(ks, tk), :] += _dot(p.astype(BF16), dov, _TN)
            if has_bias:
                db_row = db_row + jnp.sum(dz, axis=1, keepdims=True)
                refs[n_in + 4][0, kb] += jnp.sum(dz, axis=0, keepdims=True)
            return dq, db_row

        dq, db_row = lax.fori_loop(0, (qi + 1) if causal else nk, step,
                                   (jnp.zeros((tq, hd), F32), jnp.zeros((tq, 1), F32)))
        dq_ref[0] = dq * scale
        if has_bias:
            refs[n_in + 3][0] = db_row

    qspec = pl.BlockSpec((1, tq, hd), lambda h, i: (h, i, 0))
    kspec = pl.BlockSpec((1, Sk, hd), lambda h, i: (h, 0, 0))
    vspec = pl.BlockSpec((1, tq, 1), lambda h, i: (h, i, 0))
    cspec = pl.BlockSpec((1, nk, 1, tk), lambda h, i: (h, 0, 0, 0))
    in_specs = [qspec, kspec, kspec, qspec, vspec, qspec]
    ins = [q, k, v, o, lse, do]
    out_specs = [qspec, kspec, kspec]
    out_shape = [jax.ShapeDtypeStruct((H, S, hd), F32), jax.ShapeDtypeStruct((H, Sk, hd), F32),
                 jax.ShapeDtypeStruct((H, Sk, hd), F32)]
    if has_bias:
        in_specs += [vspec, cspec]
        ins += list(bias)
        out_specs += [vspec, cspec]
        out_shape += [jax.ShapeDtypeStruct((H, S, 1), F32), jax.ShapeDtypeStruct((H, nk, 1, tk), F32)]
    return pl.pallas_call(
        body, name=name, grid=(H, S // tq),
        in_specs=in_specs, out_specs=out_specs, out_shape=out_shape,
        compiler_params=_params(("parallel", "arbitrary")),
    )(*ins)


def _split3_dot(x, tri):
    a = x.astype(BF16)
    r = x - a.astype(F32)
    b = r.astype(BF16)
    c = (r - b.astype(F32)).astype(BF16)
    return _dot(a, tri) + _dot(b, tri) + _dot(c, tri)


def _forget_fwd(logit_t, b_col, *, name, blk=512):
    H, S = logit_t.shape
    blk = _tile(S, blk)

    def body(l_ref, b_ref, f_ref):
        r_i = lax.broadcasted_iota(jnp.int32, (blk, blk), 0)
        c_i = lax.broadcasted_iota(jnp.int32, (blk, blk), 1)
        upto = (r_i <= c_i).astype(BF16)
        carry = jnp.zeros((H, 1), F32)
        for j in range(S // blk):
            u = l_ref[:, j * blk:(j + 1) * blk] + b_ref[...]
            lf, _ = _log_sigmoid_pair(u)
            f_ref[:, j * blk:(j + 1) * blk] = _split3_dot(lf, upto) + carry
            carry = carry + jnp.sum(lf, axis=1, keepdims=True)

    return pl.pallas_call(
        body, name=name,
        out_shape=jax.ShapeDtypeStruct((H, S), F32),
        compiler_params=pltpu.CompilerParams(vmem_limit_bytes=VMEM_LIMIT),
    )(logit_t, b_col)


def _forget_bwd(logit_t, b_col, d_f, *, name, blk=512):
    H, S = logit_t.shape
    blk = _tile(S, blk)

    def body(l_ref, b_ref, df_ref, dl_ref, db_ref):
        r_i = lax.broadcasted_iota(jnp.int32, (blk, blk), 0)
        c_i = lax.broadcasted_iota(jnp.int32, (blk, blk), 1)
        fromon = (r_i >= c_i).astype(BF16)
        carry = jnp.zeros((H, 1), F32)
        db = jnp.zeros((H, 1), F32)
        for j in reversed(range(S // blk)):
            sl = slice(j * blk, (j + 1) * blk)
            dfv = df_ref[:, sl]
            d_lf = _split3_dot(dfv, fromon) + carry
            carry = carry + jnp.sum(dfv, axis=1, keepdims=True)
            u = l_ref[:, sl] + b_ref[...]
            _, lsn = _log_sigmoid_pair(u)
            dl = d_lf * jnp.exp(lsn)
            dl_ref[:, sl] = dl
            db = db + jnp.sum(dl, axis=1, keepdims=True)
        db_ref[...] = db

    return pl.pallas_call(
        body, name=name,
        out_shape=[jax.ShapeDtypeStruct((H, S), F32), jax.ShapeDtypeStruct((H, 1), F32)],
        compiler_params=pltpu.CompilerParams(vmem_limit_bytes=VMEM_LIMIT),
    )(logit_t, b_col, d_f)


def _sigmoid(t):
    return 1.0 / (1.0 + jnp.exp(-t))


def _gate_fwd(o3, w3, proj, D, *, name, tm=256):
    S = proj.shape[0]
    tm = _tile(S, tm)

    def body(o0, o1, o2, w0, w1, w2, g0, g1, g2, out_ref):
        acc = None
        for o_ref, w_ref, g_ref in ((o0, w0, g0), (o1, w1, g1), (o2, w2, g2)):
            t = _sigmoid(g_ref[...]) * _dot(o_ref[...], w_ref[...])
            acc = t if acc is None else acc + t
        out_ref[...] = acc.astype(out_ref.dtype)

    ospec = lambda d: pl.BlockSpec((tm, d), lambda i: (i, 0))
    wspec = lambda w: pl.BlockSpec(w.shape, lambda i: (0, 0))
    gspec = lambda j: pl.BlockSpec((tm, D), lambda i: (i, j))
    return pl.pallas_call(
        body, name=name, grid=(S // tm,),
        in_specs=[ospec(o.shape[1]) for o in o3] + [wspec(w) for w in w3] + [gspec(j) for j in range(3)],
        out_specs=pl.BlockSpec((tm, D), lambda i: (i, 0)),
        out_shape=jax.ShapeDtypeStruct((S, D), BF16),
        compiler_params=_params(("parallel",)),
    )(*o3, *w3, proj, proj, proj)


def _gate_bwd(o3, w3, proj, dmerged, D, *, name, tm=256):
    S = proj.shape[0]
    tm = _tile(S, tm)

    def body(o0, o1, o2, w0, w1, w2, g0, g1, g2, dm_ref, dg_ref, db0, db1, db2, do0, do1, do2):
        dm = dm_ref[...]
        for j, (o_ref, w_ref, g_ref, db_ref, do_ref) in enumerate(
                ((o0, w0, g0, db0, do0), (o1, w1, g1, db1, do1), (o2, w2, g2, db2, do2))):
            s = _sigmoid(g_ref[...])
            br = _dot(o_ref[...], w_ref[...])
            dg_ref[:, j * D:(j + 1) * D] = (dm * br * s * (1.0 - s)).astype(dg_ref.dtype)
            dbr = (dm * s).astype(BF16)
            db_ref[...] = dbr
            do_ref[...] = _dot(dbr, w_ref[...], _NT)

    ospec = lambda d: pl.BlockSpec((tm, d), lambda i: (i, 0))
    wspec = lambda w: pl.BlockSpec(w.shape, lambda i: (0, 0))
    gspec = lambda j: pl.BlockSpec((tm, D), lambda i: (i, j))
    dspec = pl.BlockSpec((tm, D), lambda i: (i, 0))
    return pl.pallas_call(
        body, name=name, grid=(S // tm,),
        in_specs=[ospec(o.shape[1]) for o in o3] + [wspec(w) for w in w3] + [gspec(j) for j in range(3)] + [dspec],
        out_specs=[pl.BlockSpec((tm, 3 * D), lambda i: (i, 0))] + [dspec] * 3 + [ospec(o.shape[1]) for o in o3],
        out_shape=[jax.ShapeDtypeStruct((S, 3 * D), BF16)] + [jax.ShapeDtypeStruct((S, D), BF16)] * 3
        + [jax.ShapeDtypeStruct((S, o.shape[1]), F32) for o in o3],
        compiler_params=_params(("parallel",)),
    )(*o3, *w3, proj, proj, proj, dmerged)


def _loss_sum(dy, D, *, name):
    R, C = dy.shape
    tr = _row_tile(R, C, 2)

    def body(dy_ref, out_ref):
        @pl.when(pl.program_id(0) == 0)
        def _():
            out_ref[...] = jnp.zeros_like(out_ref)

        v = dy_ref[...]
        out_ref[...] += (0.5 * D) * jnp.sum(v * v)

    return pl.pallas_call(
        body, name=name, grid=(R // tr,),
        in_specs=[pl.BlockSpec((tr, C), lambda i: (i, 0))],
        out_specs=pl.BlockSpec((8, 128), lambda i: (0, 0)),
        out_shape=jax.ShapeDtypeStruct((8, 128), F32),
        compiler_params=_params(("arbitrary",)),
    )(dy)[0, 0]


def _adamw_math(w, g, m, v):
    m2 = ADAM_B1 * m + (1.0 - ADAM_B1) * g
    v2 = ADAM_B2 * v + (1.0 - ADAM_B2) * (g * g)
    m_hat = m2 / (1.0 - ADAM_B1 ** ADAM_STEP)
    v_hat = v2 / (1.0 - ADAM_B2 ** ADAM_STEP)
    delta = -ADAM_LR * (m_hat / (jnp.sqrt(v_hat) + ADAM_EPS) + ADAM_WD * w)
    return delta, m2, v2


def _adamw(w, g, m, v, *, name):
    return _ew(_adamw_math, [w, g, m, v], (F32, F32, F32), name=name)


def _adamw_small(w, parts, m, v, *, name):
    n = parts.shape[0]

    def body(w_ref, p_ref, m_ref, v_ref, g_ref, d_ref, m2_ref, v2_ref):
        g = p_ref[0]
        for i in range(1, n):
            g = g + p_ref[i]
        g_ref[...] = g
        d_ref[...], m2_ref[...], v2_ref[...] = _adamw_math(w_ref[...], g, m_ref[...], v_ref[...])

    shp = jax.ShapeDtypeStruct(w.shape, F32)
    return pl.pallas_call(body, name=name, out_shape=[shp] * 4)(w, parts, m, v)


_ANY = pl.BlockSpec(memory_space=pl.ANY)


def _mesh_place():
    x, y, c = lax.axis_index("x"), lax.axis_index("y"), lax.axis_index("c")
    chips = [(1 - x, y), (x, 1 - y), (1 - x, 1 - y)]
    return x, y, c, chips


def _remote(src, dst, sems, i, to):
    send_sems, recv_sems = sems
    return pltpu.make_async_remote_copy(src_ref=src, dst_ref=dst, send_sem=send_sems.at[i], recv_sem=recv_sems.at[i],
                                        device_id=to, device_id_type=MESH_ID)


def _gather_weights(shards):
    n = len(shards)

    def body(*refs):
        ins, outs = refs[:n], refs[n:2 * n]
        sems = refs[2 * n:2 * n + 2]
        x, y, c, chips = _mesh_place()
        me = 2 * x + y
        sibling = (x, y, 1 - c)
        sent = []
        for w in range(n):
            for j, chip in enumerate(chips):
                cp = _remote(ins[w].at[c], outs[w].at[me, c], sems, 6 * w + j, (chip[0], chip[1], c))
                cp.start()
                sent.append(cp)
        for w in range(n):
            for j, chip in enumerate(chips):
                got = outs[w].at[2 * chip[0] + chip[1], c]
                _remote(got, got, sems, 6 * w + j, sibling).wait_recv()
                cp = _remote(got, got, sems, 6 * w + 3 + j, sibling)
                cp.start()
                sent.append(cp)
        for w in range(n):
            for j, chip in enumerate(chips):
                got = outs[w].at[2 * chip[0] + chip[1], 1 - c]
                _remote(got, got, sems, 6 * w + 3 + j, sibling).wait_recv()
        for cp in sent:
            cp.wait_send()

    outs = pl.pallas_call(
        body, name="gather_weights",
        in_specs=[_ANY] * n, out_specs=[_ANY] * n,
        out_shape=[jax.ShapeDtypeStruct((N_CHIPS,) + s.shape, s.dtype) for s in shards],
        scratch_shapes=[pltpu.SemaphoreType.DMA((6 * n,)), pltpu.SemaphoreType.DMA((6 * n,))],
    )(*shards)
    me = 2 * lax.axis_index("x") + lax.axis_index("y")
    return [lax.dynamic_update_index_in_dim(o, s, me, 0) for o, s in zip(outs, shards)]


def _exchange_siblings(grads, small):
    n = len(grads)

    def body(*refs):
        ins, small_ref = refs[:n], refs[n]
        got, small_out = refs[n + 1:2 * n + 1], refs[2 * n + 1]
        sems = refs[2 * n + 2:2 * n + 4]
        x, y, c, chips = _mesh_place()
        sibling = (x, y, 1 - c)
        me = 4 * x + 2 * y + c
        sent = []
        for w in range(n):
            for s in range(N_CHIPS):
                cp = _remote(ins[w].at[s, 1 - c], got[w].at[s], sems, N_CHIPS * w + s, sibling)
                cp.start()
                sent.append(cp)
        peers = [(x, y, 1 - c)] + [(ch[0], ch[1], cc) for ch in chips for cc in (c, 1 - c)]
        for j, peer in enumerate(peers):
            cp = _remote(small_ref, small_out.at[me], sems, N_CHIPS * n + j, peer)
            cp.start()
            sent.append(cp)
        for w in range(n):
            for s in range(N_CHIPS):
                _remote(got[w].at[s], got[w].at[s], sems, N_CHIPS * w + s, sibling).wait_recv()
        for j, peer in enumerate(peers):
            frm = small_out.at[4 * peer[0] + 2 * peer[1] + peer[2]]
            _remote(frm, frm, sems, N_CHIPS * n + j, peer).wait_recv()
        for cp in sent:
            cp.wait_send()

    halves = [jax.ShapeDtypeStruct((N_CHIPS,) + g.shape[2:], g.dtype) for g in grads]
    n_sem = N_CHIPS * n + 7
    outs = pl.pallas_call(
        body, name="exchange_siblings",
        in_specs=[_ANY] * (n + 1), out_specs=[_ANY] * (n + 1),
        out_shape=halves + [jax.ShapeDtypeStruct((8,) + small.shape, small.dtype)],
        scratch_shapes=[pltpu.SemaphoreType.DMA((n_sem,)), pltpu.SemaphoreType.DMA((n_sem,))],
    )(*grads, small)
    c = lax.axis_index("c")
    me = 4 * lax.axis_index("x") + 2 * lax.axis_index("y") + c
    own = [lax.dynamic_index_in_dim(g, c, 1, keepdims=False) for g in grads]
    return own, outs[:n], lax.dynamic_update_index_in_dim(outs[n], small, me, 0)


def _exchange_chips(parts):
    n = len(parts)

    def body(*refs):
        ins, got = refs[:n], refs[n:2 * n]
        sems = refs[2 * n:2 * n + 2]
        x, y, c, chips = _mesh_place()
        sent = []
        for w in range(n):
            for j, chip in enumerate(chips):
                cp = _remote(ins[w].at[2 * chip[0] + chip[1]], got[w].at[j], sems, 3 * w + j, (chip[0], chip[1], c))
                cp.start()
                sent.append(cp)
        for w in range(n):
            for j in range(3):
                _remote(got[w].at[j], got[w].at[j], sems, 3 * w + j, (x, y, c)).wait_recv()
        for cp in sent:
            cp.wait_send()

    got = pl.pallas_call(
        body, name="exchange_chips",
        in_specs=[_ANY] * n, out_specs=[_ANY] * n,
        out_shape=[jax.ShapeDtypeStruct((3,) + p.shape[1:], p.dtype) for p in parts],
        scratch_shapes=[pltpu.SemaphoreType.DMA((3 * n,)), pltpu.SemaphoreType.DMA((3 * n,))],
    )(*parts)
    me = 2 * lax.axis_index("x") + lax.axis_index("y")
    return [lax.dynamic_index_in_dim(p, me, 0, keepdims=False) for p in parts], got


def _share_halves(halves):
    n = len(halves)

    def body(*refs):
        ins, outs = refs[:n], refs[n:2 * n]
        sems = refs[2 * n:2 * n + 2]
        x, y, c, _ = _mesh_place()
        sibling = (x, y, 1 - c)
        sent = [_remote(ins[w], outs[w].at[c], sems, w, sibling) for w in range(n)]
        for cp in sent:
            cp.start()
        for w in range(n):
            _remote(outs[w].at[1 - c], outs[w].at[1 - c], sems, w, sibling).wait_recv()
        for cp in sent:
            cp.wait_send()

    outs = pl.pallas_call(
        body, name="share_halves",
        in_specs=[_ANY] * n, out_specs=[_ANY] * n,
        out_shape=[jax.ShapeDtypeStruct((2,) + h.shape, h.dtype) for h in halves],
        scratch_shapes=[pltpu.SemaphoreType.DMA((n,)), pltpu.SemaphoreType.DMA((n,))],
    )(*halves)
    c = lax.axis_index("c")
    return [lax.dynamic_update_index_in_dim(o, h, c, 0) for o, h in zip(outs, halves)]


def _to_heads(t, n, d):
    s = t.shape[0]
    return t.reshape(s, n, d).transpose(1, 0, 2)


def _from_heads(t):
    h, s, d = t.shape
    return t.transpose(1, 0, 2).reshape(s, h * d)


def _head_norm_fwd(t, g, *, name):
    h, s, d = t.shape
    return _rmsnorm_fwd(t.reshape(h * s, d), g, BF16, name=name).reshape(h, s, d)


def _head_norm_bwd(t, g, dy, *, name):
    h, s, d = t.shape
    dx, dg = _rmsnorm_bwd(t.reshape(h * s, d), g, dy.reshape(h * s, d), name=name)
    return dx.reshape(h, s, d), dg


def _local_step(x, mem, target, small, W):
    S, D = x.shape
    o_gate, o_qkv, o_mq, o_f = 0, 3 * D, 3 * D + 2 * 3 * D_SB, 3 * D + 2 * 3 * D_SB + D_MEM
    tq = 256

    h = _rmsnorm_fwd(x, small["g_mix_norm"], BF16, name="mix_norm")
    proj = _mm(h, W["w_in"], name="in_proj", tn=768)
    qkv = lambda j: proj[:, o_qkv + j * D_SB:o_qkv + (j + 1) * D_SB]
    sb_q, sb_k, sb_v = [_to_heads(qkv(j), SB_HEADS, HD).astype(BF16) for j in range(3)]
    fx_q, fx_k = [_to_heads(qkv(j), FOX_HEADS, HD) for j in (3, 4)]
    fx_v = _to_heads(qkv(5), FOX_HEADS, HD).astype(BF16)
    m_q = _to_heads(proj[:, o_mq:o_mq + D_MEM], MEM_HEADS, MEM_HD)
    f_logit_t = proj[:, o_f:o_f + FOX_HEADS].T
    b_col = small["b_forget"].reshape(FOX_HEADS, 1)

    o_sb, sb_tot = _sb_fwd(sb_q, sb_k, sb_v, name="sb_fwd", tq=tq)

    fq = _head_norm_fwd(fx_q, small["g_fox_q"], name="fox_q_norm")
    fk = _head_norm_fwd(fx_k, small["g_fox_k"], name="fox_k_norm")
    f_cum = _forget_fwd(f_logit_t, b_col, name="forget_fwd")
    tkf = _tile(S, tq)
    f_bias = (f_cum.reshape(FOX_HEADS, S, 1), f_cum.reshape(FOX_HEADS, S // tkf, 1, tkf))
    o_fox, fox_lse = _softmax_fwd(fq, fk, fx_v, f_bias, name="fox_fwd", causal=True, tq=tq, tk=tq)

    mh = _rmsnorm_fwd(mem, small["g_mem_norm"], BF16, name="mem_norm")
    mkv = _mm(mh, W["w_mem_kv"], name="mem_kv")
    mk_raw = _to_heads(mkv[:, :D_MEM], MEM_HEADS, MEM_HD)
    mv = _to_heads(mkv[:, D_MEM:], MEM_HEADS, MEM_HD).astype(BF16)
    mq = _head_norm_fwd(m_q, small["g_mem_q"], name="mem_q_norm")
    mk = _head_norm_fwd(mk_raw, small["g_mem_k"], name="mem_k_norm")
    o_mem, mem_lse = _softmax_fwd(mq, mk, mv, name="mem_fwd", causal=False, tq=tq, tk=256)

    o3 = [_from_heads(o).astype(BF16) for o in (o_sb, o_fox, o_mem)]
    w3 = [W["w_branch_sb"], W["w_branch_fox"], W["w_branch_mem"]]
    merged = _gate_fwd(o3, w3, proj, D, name="gate_fwd")
    x1 = _mm(merged, W["w_out"], name="out_proj", extras=(x,), epilogue=lambda acc, res: (res + acc,))
    h2 = _rmsnorm_fwd(x1, small["g_mlp_norm"], BF16, name="mlp_norm")

    def relu2(acc):
        u = jnp.maximum(acc, 0.0)
        return u, u * u

    u, a = _mm(h2, W["w_ff_up"], name="ff_up", out_dtypes=(BF16, BF16), epilogue=relu2)
    dy = _mm(a, W["w_ff_down"], name="ff_down", extras=(x1, target),
             epilogue=lambda acc, res, tgt: ((res + acc - tgt) * (1.0 / D),))
    loss = _loss_sum(dy, D, name="loss")

    G = {}
    du = _mm(dy, W["w_ff_down"], name="d_ff_act", tb=True, out_dtypes=(BF16,), extras=(u,),
             epilogue=lambda acc, uu: (acc * (2.0 * uu.astype(F32)),))
    G["w_ff_down"] = _mm(a, dy, name="d_w_ff_down", ta=True, out_dtypes=(BF16,))
    G["w_ff_up"] = _mm(h2, du, name="d_w_ff_up", ta=True, out_dtypes=(BF16,))
    dh2 = _mm(du, W["w_ff_up"], name="d_mlp_in", tb=True)
    dx1, dg_mlp = _rmsnorm_bwd(x1, small["g_mlp_norm"], dh2, add=dy, name="d_mlp_norm")
    dmerged = _mm(dx1, W["w_out"], name="d_merged", tb=True)
    G["w_out"] = _mm(merged, dx1, name="d_w_out", ta=True, out_dtypes=(BF16,))
    dgate, db0, db1, db2, do_sb, do_fox, do_mem = _gate_bwd(o3, w3, proj, dmerged, D, name="gate_bwd")
    for nm, o, db in zip(("w_branch_sb", "w_branch_fox", "w_branch_mem"), o3, (db0, db1, db2)):
        G[nm] = _mm(o, db, name="d_" + nm, ta=True, out_dtypes=(BF16,))

    dsb = _sb_bwd(sb_q, sb_k, sb_v, _to_heads(do_sb, SB_HEADS, HD), sb_tot, name="sb_bwd", tq=tq)
    dfq, dfk, dfv, df_row, df_col = _softmax_bwd(fq, fk, fx_v, o_fox, fox_lse, _to_heads(do_fox, FOX_HEADS, HD),
                                                 f_bias, name="fox_bwd", causal=True, tq=tq, tk=tq)
    dfx_q, dg_fox_q = _head_norm_bwd(fx_q, small["g_fox_q"], dfq, name="d_fox_q_norm")
    dfx_k, dg_fox_k = _head_norm_bwd(fx_k, small["g_fox_k"], dfk, name="d_fox_k_norm")
    d_fcum = df_row.reshape(FOX_HEADS, S) - df_col.reshape(FOX_HEADS, S)
    d_flogit_t, db_forget = _forget_bwd(f_logit_t, b_col, d_fcum, name="forget_bwd")

    dmq_n, dmk_n, dmv = _softmax_bwd(mq, mk, mv, o_mem, mem_lse, _to_heads(do_mem, MEM_HEADS, MEM_HD),
                                     name="mem_bwd", causal=False, tq=tq, tk=256)
    dm_q, dg_mem_q = _head_norm_bwd(m_q, small["g_mem_q"], dmq_n, name="d_mem_q_norm")
    dmk_raw, dg_mem_k = _head_norm_bwd(mk_raw, small["g_mem_k"], dmk_n, name="d_mem_k_norm")
    dmkv = jnp.concatenate([_from_heads(dmk_raw), _from_heads(dmv)], axis=1).astype(BF16)
    G["w_mem_kv"] = _mm(mh, dmkv, name="d_w_mem_kv", ta=True, out_dtypes=(BF16,))
    dmh = _mm(dmkv, W["w_mem_kv"], name="d_mem_h", tb=True)
    _, dg_mem = _rmsnorm_bwd(mem, small["g_mem_norm"], dmh, name="d_mem_norm")

    dproj = jnp.concatenate(
        [dgate] + [_from_heads(t).astype(BF16) for t in (*dsb, dfx_q, dfx_k, dfv, dm_q)]
        + [d_flogit_t.T.astype(BF16), jnp.zeros((S, F_PAD - FOX_HEADS), BF16)], axis=1)
    G["w_in"] = _mm(h, dproj, name="d_w_in", ta=True, out_dtypes=(BF16,), tn=768)
    dh = _mm(dproj, W["w_in"], name="d_mix_in", tb=True, tk=768)
    grad_x, dg_mix = _rmsnorm_bwd(x, small["g_mix_norm"], dh, add=dx1, name="d_mix_norm")

    small_grads = dict(g_mix_norm=dg_mix, g_mem_norm=dg_mem, b_forget=db_forget.reshape(1, FOX_HEADS),
                       g_fox_q=dg_fox_q, g_fox_k=dg_fox_k, g_mem_q=dg_mem_q, g_mem_k=dg_mem_k, g_mlp_norm=dg_mlp)
    return loss, grad_x, G, small_grads


BIG = ("w_in", "w_mem_kv", "w_branch_sb", "w_branch_fox", "w_branch_mem", "w_out", "w_ff_up", "w_ff_down")
COLUMN_SHARDED = ("w_in", "w_branch_sb", "w_branch_fox", "w_branch_mem", "w_ff_up")
SMALL = ("g_mix_norm", "g_mem_norm", "b_forget", "g_fox_q", "g_fox_k", "g_mem_q", "g_mem_k", "g_mlp_norm")
ORDER = ("g_mix_norm", "g_mem_norm", "w_in", "b_forget", "g_fox_q", "g_fox_k", "g_mem_q", "g_mem_k", "w_mem_kv",
         "w_branch_sb", "w_branch_fox", "w_branch_mem", "w_out", "g_mlp_norm", "w_ff_up", "w_ff_down")


def _unshard(name, gathered):
    n, _, rh, c = gathered.shape
    t = gathered.reshape(n, 2 * rh, c)
    if name in COLUMN_SHARDED:
        return t.transpose(1, 0, 2).reshape(2 * rh, n * c)
    return t.reshape(n * 2 * rh, c)


def _reshard(name, full):
    if name in COLUMN_SHARDED:
        r, c = full.shape
        t = full.reshape(r, N_CHIPS, c // N_CHIPS).transpose(1, 0, 2)
    else:
        r, c = full.shape[0] // N_CHIPS, full.shape[1]
        t = full.reshape(N_CHIPS, r, c)
    return t.reshape(N_CHIPS, 2, t.shape[1] // 2, t.shape[2])


def _pad_in_proj(w_in, D):
    n_qkv = 6 * D_SB
    o_mq = n_qkv + FOX_HEADS
    o_gate = o_mq + D_MEM
    return jnp.concatenate([w_in[:, o_gate:], w_in[:, :n_qkv], w_in[:, o_mq:o_gate], w_in[:, n_qkv:o_mq],
                            jnp.zeros((w_in.shape[0], F_PAD - FOX_HEADS), w_in.dtype)], axis=1)


def _unpad_in_proj(g, D):
    n_qkv = 6 * D_SB
    o_qkv, o_mq, o_f = 3 * D, 3 * D + n_qkv, 3 * D + n_qkv + D_MEM
    return jnp.concatenate([g[:, o_qkv:o_mq], g[:, o_f:o_f + FOX_HEADS], g[:, o_mq:o_f], g[:, :o_qkv]], axis=1)


def _pack_small(vals):
    width = max(vals[k].shape[1] for k in SMALL)
    return jnp.concatenate([jnp.pad(vals[k].astype(F32), ((0, 0), (0, width - vals[k].shape[1]))) for k in SMALL], axis=0)


def _unpack_small(packed, like):
    return {k: packed[i:i + 1, :like[k].shape[1]] for i, k in enumerate(SMALL)}


def kernel(x, mem, g_mix_norm, g_mem_norm, w_in, b_forget, g_fox_q, g_fox_k, g_mem_q, g_mem_k, w_mem_kv, w_branch_sb, w_branch_fox, w_branch_mem, w_out, g_mlp_norm, w_ff_up, w_ff_down, loss_target, m_g_mix_norm, m_g_mem_norm, m_w_in, m_b_forget, m_g_fox_q, m_g_fox_k, m_g_mem_q, m_g_mem_k, m_w_mem_kv, m_w_branch_sb, m_w_branch_fox, m_w_branch_mem, m_w_out, m_g_mlp_norm, m_w_ff_up, m_w_ff_down, v_g_mix_norm, v_g_mem_norm, v_w_in, v_b_forget, v_g_fox_q, v_g_fox_k, v_g_mem_q, v_g_mem_k, v_w_mem_kv, v_w_branch_sb, v_w_branch_fox, v_w_branch_mem, v_w_out, v_g_mlp_norm, v_w_ff_up, v_w_ff_down):
    given = dict(locals())
    D = x.shape[-1]
    weights = {k: given[k] for k in ORDER}
    moms = {k: given["m_" + k] for k in ORDER}
    vars_ = {k: given["v_" + k] for k in ORDER}

    shards = []
    for k in BIG:
        w = weights[k][0].astype(BF16)
        shards.append(w.reshape(2, w.shape[0] // 2, w.shape[1]))
    gathered = _gather_weights(shards)
    W = {k: _unshard(k, g) for k, g in zip(BIG, gathered)}
    W["w_in"] = _pad_in_proj(W["w_in"], D)

    small = {k: weights[k] for k in SMALL}
    loss_part, grad_x, G, small_grads = _local_step(x[0], mem[0], loss_target[0], small, W)
    G["w_in"] = _unpad_in_proj(G["w_in"], D)

    stacked = [_reshard(k, G[k]) for k in BIG]
    own, sib, small_parts = _exchange_siblings(stacked, _pack_small(small_grads))
    add2 = lambda p, q: (p.astype(F32) + q.astype(F32),)
    parts = []
    for k, p, q in zip(BIG, own, sib):
        n, r, c = p.shape
        parts.append(_ew(add2, [p.reshape(n * r, c), q.reshape(n * r, c)], (BF16,), name="sum_pair_" + k).reshape(n, r, c))
    own2, got2 = _exchange_chips(parts)
    add4 = lambda p, q0, q1, q2: (((p.astype(F32) + q0.astype(F32)) + q1.astype(F32)) + q2.astype(F32),)
    halves = [_ew(add4, [p, q[0], q[1], q[2]], (F32,), name="sum_chips_" + k) for k, p, q in zip(BIG, own2, got2)]
    reduced = _share_halves(halves)

    grads, deltas, new_m, new_v = {}, {}, {}, {}
    for k, g in zip(BIG, reduced):
        shp = weights[k].shape
        g2 = g.reshape(shp[1], shp[2])
        d, m2, v2 = _adamw(weights[k][0], g2, moms[k][0], vars_[k][0], name="adamw_" + k)
        grads[k], deltas[k], new_m[k], new_v[k] = g2.reshape(shp), d.reshape(shp), m2.reshape(shp), v2.reshape(shp)
    sg, sd, sm, sv = _adamw_small(_pack_small(small), small_parts, _pack_small({k: moms[k] for k in SMALL}),
                                  _pack_small({k: vars_[k] for k in SMALL}), name="adamw_small")
    for dst, packed in ((grads, sg), (deltas, sd), (new_m, sm), (new_v, sv)):
        dst.update(_unpack_small(packed, small))

    loss = lax.psum(loss_part, ("x", "y", "c"))
    return (loss, grad_x[None], *[grads[k] for k in ORDER], *[deltas[k] for k in ORDER],
            *[new_m[k] for k in ORDER], *[new_v[k] for k in ORDER])
```

```python
import functools

import jax
import jax.numpy as jnp
from jax import lax
from jax.experimental import pallas as pl
from jax.experimental.pallas import tpu as pltpu

F32 = jnp.float32
BF16 = jnp.bfloat16
MESH_ID = pl.DeviceIdType.MESH

HD = 64
SB_HEADS = 8
FOX_HEADS = 8
MEM_HEADS = 4
MEM_HD = 128
D_SB = SB_HEADS * HD
D_FOX = FOX_HEADS * HD
D_MEM = MEM_HEADS * MEM_HD
EPS = 1e-6
NEG_INF = -1e30

ADAM_LR = 0.001
ADAM_B1 = 0.9
ADAM_B2 = 0.999
ADAM_EPS = 1e-08
ADAM_WD = 0.01
ADAM_STEP = 10

N_CHIPS = 4
VMEM_LIMIT = 56 * 1024 * 1024

F_PAD = 256


def _tile(n, target, align=128):
    if n <= target:
        return n
    best = None
    t = align
    while t <= target:
        if n % t == 0:
            best = t
        t += align
    assert best is not None, (n, target, align)
    return best


def _params(sem):
    return pltpu.CompilerParams(dimension_semantics=sem, vmem_limit_bytes=VMEM_LIMIT)


def _mm(a, b, *, name, ta=False, tb=False, out_dtypes=(F32,), epilogue=None, extras=(),
        tm=512, tn=512, tk=1024):
    if ta:
        K, M = a.shape
    else:
        M, K = a.shape
    if tb:
        N, K2 = b.shape
    else:
        K2, N = b.shape
    assert K == K2, (a.shape, b.shape, ta, tb)
    tm, tn, tk = _tile(M, tm), _tile(N, tn), _tile(K, tk)
    nk = K // tk
    n_extra, n_out = len(extras), len(out_dtypes)
    if epilogue is None:
        epilogue = lambda acc: (acc,)
    dims = (((0 if ta else 1,), (1 if tb else 0,)), ((), ()))

    def body(a_ref, b_ref, *rest):
        extra_refs = rest[:n_extra]
        out_refs = rest[n_extra:n_extra + n_out]
        acc_ref = rest[n_extra + n_out]
        k = pl.program_id(2)

        @pl.when(k == 0)
        def _():
            acc_ref[...] = jnp.zeros_like(acc_ref)

        acc_ref[...] += lax.dot_general(a_ref[...].astype(BF16), b_ref[...].astype(BF16), dims,
                                        preferred_element_type=F32)

        @pl.when(k == nk - 1)
        def _():
            outs = epilogue(acc_ref[...], *[r[...] for r in extra_refs])
            for o_ref, o in zip(out_refs, outs):
                o_ref[...] = o.astype(o_ref.dtype)

    a_spec = pl.BlockSpec((tk, tm), lambda i, j, k: (k, i)) if ta else pl.BlockSpec((tm, tk), lambda i, j, k: (i, k))
    b_spec = pl.BlockSpec((tn, tk), lambda i, j, k: (j, k)) if tb else pl.BlockSpec((tk, tn), lambda i, j, k: (k, j))
    mn_spec = pl.BlockSpec((tm, tn), lambda i, j, k: (i, j))
    outs = pl.pallas_call(
        body, name=name,
        grid=(M // tm, N // tn, nk),
        in_specs=[a_spec, b_spec] + [mn_spec] * n_extra,
        out_specs=[mn_spec] * n_out,
        out_shape=[jax.ShapeDtypeStruct((M, N), dt) for dt in out_dtypes],
        scratch_shapes=[pltpu.VMEM((tm, tn), F32)],
        compiler_params=_params(("parallel", "parallel", "arbitrary")),
    )(a, b, *extras)
    return outs if n_out > 1 else outs[0]


def _row_tile(rows, cols, n_arrays):
    budget = 10 * 1024 * 1024
    cols_padded = -(-cols // 128) * 128
    target = max(16, budget // (cols_padded * 4 * n_arrays * 2))
    return _tile(rows, target, align=16)


def _ew(fn, ins, out_dtypes, *, name):
    R, C = ins[0].shape
    n_in, n_out = len(ins), len(out_dtypes)
    tr = _row_tile(R, C, n_in + n_out)

    def body(*refs):
        outs = fn(*[r[...] for r in refs[:n_in]])
        for o_ref, o in zip(refs[n_in:], outs):
            o_ref[...] = o.astype(o_ref.dtype)

    spec = pl.BlockSpec((tr, C), lambda i: (i, 0))
    outs = pl.pallas_call(
        body, name=name, grid=(R // tr,),
        in_specs=[spec] * n_in, out_specs=[spec] * n_out,
        out_shape=[jax.ShapeDtypeStruct((R, C), dt) for dt in out_dtypes],
        compiler_params=_params(("parallel",)),
    )(*ins)
    return outs if n_out > 1 else outs[0]


def _rmsnorm_fwd(x, g, out_dtype, *, name):
    R, d = x.shape
    tr = _row_tile(R, d, 3)

    def body(x_ref, g_ref, o_ref):
        xv = x_ref[...].astype(F32)
        r = lax.rsqrt(jnp.mean(xv * xv, axis=-1, keepdims=True) + EPS)
        o_ref[...] = (xv * r * g_ref[...]).astype(o_ref.dtype)

    return pl.pallas_call(
        body, name=name, grid=(R // tr,),
        in_specs=[pl.BlockSpec((tr, d), lambda i: (i, 0)), pl.BlockSpec((1, d), lambda i: (0, 0))],
        out_specs=pl.BlockSpec((tr, d), lambda i: (i, 0)),
        out_shape=jax.ShapeDtypeStruct((R, d), out_dtype),
        compiler_params=_params(("parallel",)),
    )(x, g)


def _rmsnorm_bwd(x, g, dy, add=None, *, name):
    R, d = x.shape
    has_add = add is not None
    tr = _row_tile(R, d, 5)

    def body(*refs):
        x_ref, g_ref, dy_ref = refs[:3]
        add_ref = refs[3] if has_add else None
        dx_ref, dg_ref = refs[-2:]
        xv = x_ref[...].astype(F32)
        dyv = dy_ref[...].astype(F32)
        r = lax.rsqrt(jnp.mean(xv * xv, axis=-1, keepdims=True) + EPS)
        xh = xv * r
        dyg = dyv * g_ref[...]
        c = jnp.mean(dyg * xh, axis=-1, keepdims=True)
        dx = r * (dyg - xh * c)
        if has_add:
            dx = dx + add_ref[...]
        dx_ref[...] = dx

        @pl.when(pl.program_id(0) == 0)
        def _():
            dg_ref[...] = jnp.zeros_like(dg_ref)

        dg_ref[...] += jnp.sum(dyv * xh, axis=0, keepdims=True)

    row = pl.BlockSpec((tr, d), lambda i: (i, 0))
    vec = pl.BlockSpec((1, d), lambda i: (0, 0))
    ins = [x, g, dy] + ([add] if has_add else [])
    return pl.pallas_call(
        body, name=name, grid=(R // tr,),
        in_specs=[row, vec, row] + ([row] if has_add else []),
        out_specs=[row, vec],
        out_shape=[jax.ShapeDtypeStruct((R, d), F32), jax.ShapeDtypeStruct((1, d), F32)],
        compiler_params=_params(("arbitrary",)),
    )(*ins)


_NT = (((1,), (1,)), ((), ()))
_TN = (((0,), (0,)), ((), ()))


def _dot(a, b, dims=(((1,), (0,)), ((), ()))):
    return lax.dot_general(a, b, dims, preferred_element_type=F32)


def _split_dot(x, tri):
    hi = x.astype(BF16)
    lo = (x - hi.astype(F32)).astype(BF16)
    return _dot(hi, tri) + _dot(lo, tri)


def _log_sigmoid_pair(z):
    sp = jnp.log(1.0 + jnp.exp(-jnp.abs(z)))
    return jnp.minimum(z, 0.0) - sp, jnp.minimum(-z, 0.0) - sp


def _sb_fwd(q, k, v, *, name, tq=256):
    H, S, hd = q.shape
    tq = _tile(S, tq)
    tk = tq
    scale = hd ** -0.5

    def body(q_ref, k_ref, v_ref, o_ref, tot_ref):
        qi = pl.program_id(1)
        qv = q_ref[0]
        row = qi * tq + lax.broadcasted_iota(jnp.int32, (tq, tk), 0)
        col0 = lax.broadcasted_iota(jnp.int32, (tq, tk), 1)
        later = (lax.broadcasted_iota(jnp.int32, (tk, tk), 0) > lax.broadcasted_iota(jnp.int32, (tk, tk), 1)).astype(BF16)

        def step(i, carry):
            acc, c_rem = carry
            kb = qi - i
            ks = pl.multiple_of(kb * tk, tk)
            kv = k_ref[0, pl.ds(ks, tk), :]
            vv = v_ref[0, pl.ds(ks, tk), :]
            z = _dot(qv, kv, _NT) * scale
            mask = (col0 + ks) < row
            ls, lsn = _log_sigmoid_pair(z)
            rem = jnp.where(mask, lsn, 0.0)
            after = _split_dot(rem, later) + c_rem
            w = jnp.where(mask, jnp.exp(ls + after), 0.0)
            acc = acc + _dot(w.astype(BF16), vv)
            c_rem = c_rem + jnp.sum(rem, axis=1, keepdims=True)
            return acc, c_rem

        acc, c_rem = lax.fori_loop(0, qi + 1, step, (jnp.zeros((tq, hd), F32), jnp.zeros((tq, 1), F32)))
        o_ref[0] = acc
        tot_ref[0] = c_rem

    qspec = pl.BlockSpec((1, tq, hd), lambda h, i: (h, i, 0))
    kspec = pl.BlockSpec((1, S, hd), lambda h, i: (h, 0, 0))
    return pl.pallas_call(
        body, name=name, grid=(H, S // tq),
        in_specs=[qspec, kspec, kspec],
        out_specs=[qspec, pl.BlockSpec((1, tq, 1), lambda h, i: (h, i, 0))],
        out_shape=[jax.ShapeDtypeStruct((H, S, hd), F32), jax.ShapeDtypeStruct((H, S, 1), F32)],
        compiler_params=_params(("parallel", "arbitrary")),
    )(q, k, v)


def _sb_bwd(q, k, v, do, tot, *, name, tq=256):
    H, S, hd = q.shape
    tq = _tile(S, tq)
    tk = tq
    scale = hd ** -0.5

    def body(q_ref, k_ref, v_ref, do_ref, tot_ref, dq_ref, dk_ref, dv_ref):
        qi = pl.program_id(1)

        @pl.when(qi == 0)
        def _():
            dk_ref[...] = jnp.zeros_like(dk_ref)
            dv_ref[...] = jnp.zeros_like(dv_ref)

        qv = q_ref[0]
        dov = do_ref[0].astype(BF16)
        tot_v = tot_ref[0]
        row = qi * tq + lax.broadcasted_iota(jnp.int32, (tq, tk), 0)
        col0 = lax.broadcasted_iota(jnp.int32, (tq, tk), 1)
        r_i = lax.broadcasted_iota(jnp.int32, (tk, tk), 0)
        c_i = lax.broadcasted_iota(jnp.int32, (tk, tk), 1)
        upto = (r_i <= c_i).astype(BF16)
        before = (r_i < c_i).astype(BF16)

        def step(kb, carry):
            dq, c_rem, c_g = carry
            ks = pl.multiple_of(kb * tk, tk)
            kv = k_ref[0, pl.ds(ks, tk), :]
            vv = v_ref[0, pl.ds(ks, tk), :]
            z = _dot(qv, kv, _NT) * scale
            mask = (col0 + ks) < row
            ls, lsn = _log_sigmoid_pair(z)
            rem = jnp.where(mask, lsn, 0.0)
            after = tot_v - (_split_dot(rem, upto) + c_rem)
            w = jnp.where(mask, jnp.exp(ls + after), 0.0)
            dw = _dot(dov, vv, _NT)
            g = dw * w
            g_before = _split_dot(g, before) + c_g
            beta = jnp.exp(ls)
            dz = jnp.where(mask, g * (1.0 - beta) - beta * g_before, 0.0) * scale
            dzb = dz.astype(BF16)
            dq = dq + _dot(dzb, kv)
            dk_ref[0, pl.ds(ks, tk), :] += _dot(dzb, qv, _TN)
            dv_ref[0, pl.ds(ks, tk), :] += _dot(w.astype(BF16), dov, _TN)
            return dq, c_rem + jnp.sum(rem, axis=1, keepdims=True), c_g + jnp.sum(g, axis=1, keepdims=True)

        zero = jnp.zeros((tq, 1), F32)
        dq, _, _ = lax.fori_loop(0, qi + 1, step, (jnp.zeros((tq, hd), F32), zero, zero))
        dq_ref[0] = dq

    qspec = pl.BlockSpec((1, tq, hd), lambda h, i: (h, i, 0))
    kspec = pl.BlockSpec((1, S, hd), lambda h, i: (h, 0, 0))
    full = jax.ShapeDtypeStruct((H, S, hd), F32)
    return pl.pallas_call(
        body, name=name, grid=(H, S // tq),
        in_specs=[qspec, kspec, kspec, qspec, pl.BlockSpec((1, tq, 1), lambda h, i: (h, i, 0))],
        out_specs=[qspec, kspec, kspec],
        out_shape=[full, full, full],
        compiler_params=_params(("parallel", "arbitrary")),
    )(q, k, v, do, tot)


def _softmax_fwd(q, k, v, bias=None, *, name, causal, tq=256, tk=256):
    H, S, hd = q.shape
    Sk = k.shape[1]
    tq, tk = _tile(S, tq), _tile(Sk, tk)
    if causal:
        assert tq == tk and S == Sk
    nk = Sk // tk
    scale = hd ** -0.5
    has_bias = bias is not None

    def body(*refs):
        q_ref, k_ref, v_ref = refs[:3]
        o_ref, lse_ref = refs[-2:]
        qi = pl.program_id(1)
        qv = q_ref[0]
        row = qi * tq + lax.broadcasted_iota(jnp.int32, (tq, tk), 0)
        col0 = lax.broadcasted_iota(jnp.int32, (tq, tk), 1)
        if has_bias:
            b_row = refs[3][0]

        def step(kb, carry):
            m, l, acc = carry
            ks = pl.multiple_of(kb * tk, tk)
            kv = k_ref[0, pl.ds(ks, tk), :]
            vv = v_ref[0, pl.ds(ks, tk), :]
            z = _dot(qv, kv, _NT) * scale
            if has_bias:
                z = z + b_row - refs[4][0, kb]
            if causal:
                z = jnp.where((col0 + ks) <= row, z, NEG_INF)
            m2 = jnp.maximum(m, jnp.max(z, axis=1, keepdims=True))
            p = jnp.exp(z - m2)
            alpha = jnp.exp(m - m2)
            l = alpha * l + jnp.sum(p, axis=1, keepdims=True)
            acc = alpha * acc + _dot(p.astype(BF16), vv)
            return m2, l, acc

        init = (jnp.full((tq, 1), NEG_INF, F32), jnp.zeros((tq, 1), F32), jnp.zeros((tq, hd), F32))
        m, l, acc = lax.fori_loop(0, (qi + 1) if causal else nk, step, init)
        o_ref[0] = acc / l
        lse_ref[0] = m + jnp.log(l)

    qspec = pl.BlockSpec((1, tq, hd), lambda h, i: (h, i, 0))
    kspec = pl.BlockSpec((1, Sk, hd), lambda h, i: (h, 0, 0))
    vspec = pl.BlockSpec((1, tq, 1), lambda h, i: (h, i, 0))
    in_specs = [qspec, kspec, kspec]
    ins = [q, k, v]
    if has_bias:
        in_specs += [vspec, pl.BlockSpec((1, nk, 1, tk), lambda h, i: (h, 0, 0, 0))]
        ins += list(bias)
    return pl.pallas_call(
        body, name=name, grid=(H, S // tq),
        in_specs=in_specs, out_specs=[qspec, vspec],
        out_shape=[jax.ShapeDtypeStruct((H, S, hd), F32), jax.ShapeDtypeStruct((H, S, 1), F32)],
        compiler_params=_params(("parallel", "arbitrary")),
    )(*ins)


def _softmax_bwd(q, k, v, o, lse, do, bias=None, *, name, causal, tq=256, tk=256):
    H, S, hd = q.shape
    Sk = k.shape[1]
    tq, tk = _tile(S, tq), _tile(Sk, tk)
    nk = Sk // tk
    scale = hd ** -0.5
    has_bias = bias is not None
    n_in = 8 if has_bias else 6

    def body(*refs):
        q_ref, k_ref, v_ref, o_ref, lse_ref, do_ref = refs[:6]
        dq_ref, dk_ref, dv_ref = refs[n_in:n_in + 3]
        qi = pl.program_id(1)

        @pl.when(qi == 0)
        def _():
            dk_ref[...] = jnp.zeros_like(dk_ref)
            dv_ref[...] = jnp.zeros_like(dv_ref)
            if has_bias:
                refs[n_in + 4][...] = jnp.zeros_like(refs[n_in + 4])

        qv = q_ref[0]
        do32 = do_ref[0]
        dov = do32.astype(BF16)
        delta = jnp.sum(do32 * o_ref[0], axis=1, keepdims=True)
        lse_v = lse_ref[0]
        row = qi * tq + lax.broadcasted_iota(jnp.int32, (tq, tk), 0)
        col0 = lax.broadcasted_iota(jnp.int32, (tq, tk), 1)
        if has_bias:
            b_row = refs[6][0]

        def step(kb, carry):
            dq, db_row = carry
            ks = pl.multiple_of(kb * tk, tk)
            kv = k_ref[0, pl.ds(ks, tk), :]
            vv = v_ref[0, pl.ds(ks, tk), :]
            z = _dot(qv, kv, _NT) * scale
            if has_bias:
                z = z + b_row - refs[7][0, kb]
            p = jnp.exp(z - lse_v)
            if causal:
                p = jnp.where((col0 + ks) <= row, p, 0.0)
            dp = _dot(dov, vv, _NT)
            dz = p * (dp - delta)
            dzb = dz.astype(BF16)
            dq = dq + _dot(dzb, kv)
            dk_ref[0, pl.ds(ks, tk), :] += _dot(dzb, qv, _TN) * scale
            dv_ref[0, pl.ds(ks, tk), :] += _dot(p.astype(BF16), dov, _TN)
            if has_bias:
                db_row = db_row + jnp.sum(dz, axis=1, keepdims=True)
                refs[n_in + 4][0, kb] += jnp.sum(dz, axis=0, keepdims=True)
            return dq, db_row

        dq, db_row = lax.fori_loop(0, (qi + 1) if causal else nk, step,
                                   (jnp.zeros((tq, hd), F32), jnp.zeros((tq, 1), F32)))
        dq_ref[0] = dq * scale
        if has_bias:
            refs[n_in + 3][0] = db_row

    qspec = pl.BlockSpec((1, tq, hd), lambda h, i: (h, i, 0))
    kspec = pl.BlockSpec((1, Sk, hd), lambda h, i: (h, 0, 0))
    vspec = pl.BlockSpec((1, tq, 1), lambda h, i: (h, i, 0))
    cspec = pl.BlockSpec((1, nk, 1, tk), lambda h, i: (h, 0, 0, 0))
    in_specs = [qspec, kspec, kspec, qspec, vspec, qspec]
    ins = [q, k, v, o, lse, do]
    out_specs = [qspec, kspec, kspec]
    out_shape = [jax.ShapeDtypeStruct((H, S, hd), F32), jax.ShapeDtypeStruct((H, Sk, hd), F32),
                 jax.ShapeDtypeStruct((H, Sk, hd), F32)]
    if has_bias:
        in_specs += [vspec, cspec]
        ins += list(bias)
        out_specs += [vspec, cspec]
        out_shape += [jax.ShapeDtypeStruct((H, S, 1), F32), jax.ShapeDtypeStruct((H, nk, 1, tk), F32)]
    return pl.pallas_call(
        body, name=name, grid=(H, S // tq),
        in_specs=in_specs, out_specs=out_specs, out_shape=out_shape,
        compiler_params=_params(("parallel", "arbitrary")),
    )(*ins)


LANES = 128
_LOW = -3e38


def _lane_masks(hd, rows):
    if hd == LANES:
        return [None]
    lane = lax.broadcasted_iota(jnp.int32, (rows, LANES), 1)
    return [(lane >= hh * hd) & (lane < (hh + 1) * hd) for hh in range(LANES // hd)]


def _keep(t, m):
    return t if m is None else jnp.where(m, t, 0.0)


def _merge(parts, masks):
    out = parts[-1]
    for p, m in zip(parts[-2::-1], masks[-2::-1]):
        out = jnp.where(m, p, out)
    return out


def _row_value(t, m):
    return jnp.max(t if m is None else jnp.where(m, t, _LOW), axis=1, keepdims=True)


def _cols(tq, off):
    return pl.BlockSpec((tq, LANES), lambda g, i: (i, off + g))


def _cols_all(rows, off):
    return pl.BlockSpec((rows, LANES), lambda g, i: (0, off + g))


def _softplus_parts(z):
    pos = jnp.maximum(z, 0.0) + jnp.log(1.0 + jnp.exp(-jnp.abs(z)))
    return pos, z - pos


def _sbl_fwd(q, k, v, *, width, hd, name, tq=256):
    (qa, qo), (ka, ko), (va, vo) = q, k, v
    S = qa.shape[0]
    tq = _tile(S, tq)
    tk = tq
    scale = hd ** -0.5

    def body(q_ref, k_ref, v_ref, o_ref, tot_ref):
        qi = pl.program_id(1)
        masks = _lane_masks(hd, tq)
        qs = q_ref[...].astype(F32) * scale
        qm = [_keep(qs, m).astype(BF16) for m in masks]
        strict = lax.broadcasted_iota(jnp.int32, (tq, tk), 1) < lax.broadcasted_iota(jnp.int32, (tq, tk), 0)
        later = (lax.broadcasted_iota(jnp.int32, (tk, tk), 0) > lax.broadcasted_iota(jnp.int32, (tk, tk), 1)).astype(BF16)

        def tile(kb, carry, diag):
            ks = pl.multiple_of(kb * tk, tk)
            kv = k_ref[pl.ds(ks, tk), :].astype(BF16)
            vv = v_ref[pl.ds(ks, tk), :].astype(BF16)
            out = []
            for hh in range(len(masks)):
                acc, c_pos = carry[2 * hh], carry[2 * hh + 1]
                pos, ls = _softplus_parts(_dot(qm[hh], kv, _NT))
                if diag:
                    pos = jnp.where(strict, pos, 0.0)
                w = jnp.exp(ls - (_split_dot(pos, later) + c_pos))
                if diag:
                    w = jnp.where(strict, w, 0.0)
                out += [acc + _dot(w.astype(BF16), vv), c_pos + jnp.sum(pos, axis=1, keepdims=True)]
            return tuple(out)

        init = (jnp.zeros((tq, LANES), F32), jnp.zeros((tq, 1), F32)) * len(masks)
        carry = tile(qi, init, True)
        carry = lax.fori_loop(0, qi, lambda i, c: tile(qi - 1 - i, c, False), carry)
        o_ref[...] = _merge(carry[0::2], masks).astype(o_ref.dtype)
        tot_ref[...] = _merge([jnp.broadcast_to(-c, (tq, LANES)) for c in carry[1::2]], masks)

    return pl.pallas_call(
        body, name=name, grid=(width // LANES, S // tq),
        in_specs=[_cols(tq, qo), _cols_all(S, ko), _cols_all(S, vo)],
        out_specs=[_cols(tq, 0), _cols(tq, 0)],
        out_shape=[jax.ShapeDtypeStruct((S, width), BF16), jax.ShapeDtypeStruct((S, width), F32)],
        compiler_params=_params(("parallel", "arbitrary")),
    )(qa, ka, va)


def _sbl_bwd(q, k, v, do, tot, *, width, hd, name, tq=256):
    (qa, qo), (ka, ko), (va, vo) = q, k, v
    S = qa.shape[0]
    tq = _tile(S, tq)
    tk = tq
    scale = hd ** -0.5

    def body(q_ref, k_ref, v_ref, do_ref, tot_ref, dq_ref, dk_ref, dv_ref):
        qi = pl.program_id(1)

        @pl.when(qi == 0)
        def _():
            dk_ref[...] = jnp.zeros_like(dk_ref)
            dv_ref[...] = jnp.zeros_like(dv_ref)

        masks = _lane_masks(hd, tq)
        qs = q_ref[...].astype(F32) * scale
        qm = [_keep(qs, m).astype(BF16) for m in masks]
        dov = [_keep(do_ref[...], m).astype(BF16) for m in masks]
        tots = [_row_value(tot_ref[...], m) for m in masks]
        r_q = lax.broadcasted_iota(jnp.int32, (tq, tk), 0)
        strict = lax.broadcasted_iota(jnp.int32, (tq, tk), 1) < r_q
        r_i = lax.broadcasted_iota(jnp.int32, (tk, tk), 0)
        c_i = lax.broadcasted_iota(jnp.int32, (tk, tk), 1)
        upto = (r_i <= c_i).astype(BF16)
        before = (r_i < c_i).astype(BF16)

        def tile(kb, carry, diag):
            ks = pl.multiple_of(kb * tk, tk)
            kv = k_ref[pl.ds(ks, tk), :].astype(BF16)
            vv = v_ref[pl.ds(ks, tk), :].astype(BF16)
            out = []
            dk_t, dv_t = None, None
            for hh in range(len(masks)):
                dq, c_pos, c_g = carry[3 * hh:3 * hh + 3]
                pos, ls = _softplus_parts(_dot(qm[hh], kv, _NT))
                if diag:
                    pos = jnp.where(strict, pos, 0.0)
                w = jnp.exp(ls + tots[hh] + (_split_dot(pos, upto) + c_pos))
                if diag:
                    w = jnp.where(strict, w, 0.0)
                g = _dot(dov[hh], vv, _NT) * w
                g_before = _split_dot(g, before) + c_g
                dz = g - jnp.exp(ls) * (g + g_before)
                if diag:
                    dz = jnp.where(strict, dz, 0.0)
                dzb = dz.astype(BF16)
                dk_h = _dot(dzb, qm[hh], _TN)
                dv_h = _dot(w.astype(BF16), dov[hh], _TN)
                dk_t = dk_h if dk_t is None else dk_t + dk_h
                dv_t = dv_h if dv_t is None else dv_t + dv_h
                out += [dq + _dot(dzb, kv), c_pos + jnp.sum(pos, axis=1, keepdims=True),
                        c_g + jnp.sum(g, axis=1, keepdims=True)]
            dk_ref[pl.ds(ks, tk), :] += dk_t
            dv_ref[pl.ds(ks, tk), :] += dv_t
            return tuple(out)

        zero = jnp.zeros((tq, 1), F32)
        init = (jnp.zeros((tq, LANES), F32), zero, zero) * len(masks)
        carry = lax.fori_loop(0, qi, lambda kb, c: tile(kb, c, False), init)
        carry = tile(qi, carry, True)
        dq_ref[...] = _merge(carry[0::3], masks) * scale

    full = jax.ShapeDtypeStruct((S, width), F32)
    return pl.pallas_call(
        body, name=name, grid=(width // LANES, S // tq),
        in_specs=[_cols(tq, qo), _cols_all(S, ko), _cols_all(S, vo), _cols(tq, do[1]), _cols(tq, tot[1])],
        out_specs=[_cols(tq, 0), _cols_all(S, 0), _cols_all(S, 0)],
        out_shape=[full, full, full],
        compiler_params=_params(("parallel", "arbitrary")),
    )(qa, ka, va, do[0], tot[0])


def _sml_fwd(q, k, v, bias=None, *, width, hd, causal, name, tq=256, tk=256):
    (qa, qo), (ka, ko), (va, vo) = q, k, v
    S, Sk = qa.shape[0], ka.shape[0]
    tq, tk = _tile(S, tq), _tile(Sk, tk)
    if causal:
        assert tq == tk and S == Sk
    nk = Sk // tk
    hpg = LANES // hd
    scale = hd ** -0.5
    has_bias = bias is not None

    def body(*refs):
        q_ref, k_ref, v_ref = refs[:3]
        o_ref, lse_ref = refs[-2:]
        qi = pl.program_id(1)
        masks = _lane_masks(hd, tq)
        qs = q_ref[...].astype(F32) * scale
        qm = [_keep(qs, m).astype(BF16) for m in masks]
        allowed = lax.broadcasted_iota(jnp.int32, (tq, tk), 1) <= lax.broadcasted_iota(jnp.int32, (tq, tk), 0)

        def tile(kb, carry, diag):
            ks = pl.multiple_of(kb * tk, tk)
            kv = k_ref[pl.ds(ks, tk), :].astype(BF16)
            vv = v_ref[pl.ds(ks, tk), :].astype(BF16)
            out = []
            for hh in range(hpg):
                m, l, acc = carry[3 * hh:3 * hh + 3]
                z = _dot(qm[hh], kv, _NT)
                if has_bias:
                    z = z + refs[3][hh] - refs[4][hh, kb]
                if diag:
                    z = jnp.where(allowed, z, NEG_INF)
                m2 = jnp.maximum(m, jnp.max(z, axis=1, keepdims=True))
                p = jnp.exp(z - m2)
                alpha = jnp.exp(m - m2)
                out += [m2, alpha * l + jnp.sum(p, axis=1, keepdims=True), alpha * acc + _dot(p.astype(BF16), vv)]
            return tuple(out)

        init = (jnp.full((tq, 1), NEG_INF, F32), jnp.zeros((tq, 1), F32), jnp.zeros((tq, LANES), F32)) * hpg
        if causal:
            carry = lax.fori_loop(0, qi, lambda kb, c: tile(kb, c, False), init)
            carry = tile(qi, carry, True)
        else:
            carry = lax.fori_loop(0, nk, lambda kb, c: tile(kb, c, False), init)
        o_ref[...] = _merge([acc / l for l, acc in zip(carry[1::3], carry[2::3])], masks).astype(o_ref.dtype)
        lse_ref[...] = _merge([jnp.broadcast_to(m + jnp.log(l), (tq, LANES)) for m, l in zip(carry[0::3], carry[1::3])], masks)

    in_specs = [_cols(tq, qo), _cols_all(Sk, ko), _cols_all(Sk, vo)]
    ins = [qa, ka, va]
    if has_bias:
        in_specs += [pl.BlockSpec((hpg, tq, 1), lambda g, i: (g, i, 0)),
                     pl.BlockSpec((hpg, nk, 1, tk), lambda g, i: (g, 0, 0, 0))]
        ins += list(bias)
    return pl.pallas_call(
        body, name=name, grid=(width // LANES, S // tq),
        in_specs=in_specs, out_specs=[_cols(tq, 0), _cols(tq, 0)],
        out_shape=[jax.ShapeDtypeStruct((S, width), BF16), jax.ShapeDtypeStruct((S, width), F32)],
        compiler_params=_params(("parallel", "arbitrary")),
    )(*ins)


def _sml_bwd(q, k, v, o, lse, do, bias=None, *, width, hd, causal, name, tq=256, tk=256):
    (qa, qo), (ka, ko), (va, vo) = q, k, v
    S, Sk = qa.shape[0], ka.shape[0]
    tq, tk = _tile(S, tq), _tile(Sk, tk)
    nk = Sk // tk
    hpg = LANES // hd
    scale = hd ** -0.5
    has_bias = bias is not None
    n_in = 8 if has_bias else 6

    def body(*refs):
        q_ref, k_ref, v_ref, o_ref, lse_ref, do_ref = refs[:6]
        dq_ref, dk_ref, dv_ref = refs[n_in:n_in + 3]
        qi = pl.program_id(1)

        @pl.when(qi == 0)
        def _():
            dk_ref[...] = jnp.zeros_like(dk_ref)
            dv_ref[...] = jnp.zeros_like(dv_ref)
            if has_bias:
                refs[n_in + 4][...] = jnp.zeros_like(refs[n_in + 4])

        masks = _lane_masks(hd, tq)
        qs = q_ref[...].astype(F32) * scale
        qm = [_keep(qs, m).astype(BF16) for m in masks]
        do32 = do_ref[...]
        dov = [_keep(do32, m).astype(BF16) for m in masks]
        prod = do32 * o_ref[...].astype(F32)
        delta = [jnp.sum(_keep(prod, m), axis=1, keepdims=True) for m in masks]
        lses = [_row_value(lse_ref[...], m) for m in masks]
        allowed = lax.broadcasted_iota(jnp.int32, (tq, tk), 1) <= lax.broadcasted_iota(jnp.int32, (tq, tk), 0)

        def tile(kb, carry, diag):
            ks = pl.multiple_of(kb * tk, tk)
            kv = k_ref[pl.ds(ks, tk), :].astype(BF16)
            vv = v_ref[pl.ds(ks, tk), :].astype(BF16)
            out = []
            dk_t, dv_t = None, None
            for hh in range(hpg):
                dq, db_row = carry[2 * hh:2 * hh + 2]
                z = _dot(qm[hh], kv, _NT)
                if has_bias:
                    z = z + refs[6][hh] - refs[7][hh, kb]
                p = jnp.exp(z - lses[hh])
                if diag:
                    p = jnp.where(allowed, p, 0.0)
                dz = p * (_dot(dov[hh], vv, _NT) - delta[hh])
                dzb = dz.astype(BF16)
                dk_h = _dot(dzb, qm[hh], _TN)
                dv_h = _dot(p.astype(BF16), dov[hh], _TN)
                dk_t = dk_h if dk_t is None else dk_t + dk_h
                dv_t = dv_h if dv_t is None else dv_t + dv_h
                if has_bias:
                    db_row = db_row + jnp.sum(dz, axis=1, keepdims=True)
                    refs[n_in + 4][hh, kb] += jnp.sum(dz, axis=0, keepdims=True)
                out += [dq + _dot(dzb, kv), db_row]
            dk_ref[pl.ds(ks, tk), :] += dk_t
            dv_ref[pl.ds(ks, tk), :] += dv_t
            return tuple(out)

        init = (jnp.zeros((tq, LANES), F32), jnp.zeros((tq, 1), F32)) * hpg
        if causal:
            carry = lax.fori_loop(0, qi, lambda kb, c: tile(kb, c, False), init)
            carry = tile(qi, carry, True)
        else:
            carry = lax.fori_loop(0, nk, lambda kb, c: tile(kb, c, False), init)
        dq_ref[...] = _merge(carry[0::2], masks) * scale
        if has_bias:
            for hh in range(hpg):
                refs[n_in + 3][hh] = carry[2 * hh + 1]

    in_specs = [_cols(tq, qo), _cols_all(Sk, ko), _cols_all(Sk, vo), _cols(tq, o[1]), _cols(tq, lse[1]), _cols(tq, do[1])]
    ins = [qa, ka, va, o[0], lse[0], do[0]]
    out_specs = [_cols(tq, 0), _cols_all(Sk, 0), _cols_all(Sk, 0)]
    out_shape = [jax.ShapeDtypeStruct((S, width), F32), jax.ShapeDtypeStruct((Sk, width), F32),
                 jax.ShapeDtypeStruct((Sk, width), F32)]
    if has_bias:
        rspec = pl.BlockSpec((hpg, tq, 1), lambda g, i: (g, i, 0))
        cspec = pl.BlockSpec((hpg, nk, 1, tk), lambda g, i: (g, 0, 0, 0))
        in_specs += [rspec, cspec]
        ins += list(bias)
        out_specs += [rspec, cspec]
        n_heads = width // hd
        out_shape += [jax.ShapeDtypeStruct((n_heads, S, 1), F32), jax.ShapeDtypeStruct((n_heads, nk, 1, tk), F32)]
    return pl.pallas_call(
        body, name=name, grid=(width // LANES, S // tq),
        in_specs=in_specs, out_specs=out_specs, out_shape=out_shape,
        compiler_params=_params(("parallel", "arbitrary")),
    )(*ins)


def _head_sums(t, masks):
    sums = [jnp.sum(_keep(t, m), axis=1, keepdims=True) for m in masks]
    return _merge([jnp.broadcast_to(s, t.shape) for s in sums], masks) if len(masks) > 1 else sums[0]


def _hnorm_fwd(x, g_lanes, *, width, hd, name, tr=512):
    xa, xo = x
    R = xa.shape[0]
    tr = _tile(R, tr, align=16)
    n_blk = width // LANES

    def body(x_ref, g_ref, o_ref):
        masks = _lane_masks(hd, tr)
        for j in range(n_blk):
            sl = slice(j * LANES, (j + 1) * LANES)
            xv = x_ref[:, sl].astype(F32)
            r = lax.rsqrt(_head_sums(xv * xv, masks) * (1.0 / hd) + EPS)
            o_ref[:, sl] = (xv * r * g_ref[...]).astype(o_ref.dtype)

    assert (xo * LANES) % width == 0
    return pl.pallas_call(
        body, name=name, grid=(R // tr,),
        in_specs=[pl.BlockSpec((tr, width), lambda i: (i, xo * LANES // width)), pl.BlockSpec((1, LANES), lambda i: (0, 0))],
        out_specs=pl.BlockSpec((tr, width), lambda i: (i, 0)),
        out_shape=jax.ShapeDtypeStruct((R, width), BF16),
        compiler_params=_params(("parallel",)),
    )(xa, g_lanes)


def _hnorm_bwd(x, g_lanes, dy, *, width, hd, name, tr=512):
    xa, xo = x
    R = xa.shape[0]
    tr = _tile(R, tr, align=16)
    n_blk = width // LANES

    def body(x_ref, g_ref, dy_ref, dx_ref, dg_ref):
        masks = _lane_masks(hd, tr)
        dg = jnp.zeros((1, LANES), F32)
        for j in range(n_blk):
            sl = slice(j * LANES, (j + 1) * LANES)
            xv = x_ref[:, sl].astype(F32)
            dyv = dy_ref[:, sl].astype(F32)
            r = lax.rsqrt(_head_sums(xv * xv, masks) * (1.0 / hd) + EPS)
            xh = xv * r
            dyg = dyv * g_ref[...]
            c = _head_sums(dyg * xh, masks) * (1.0 / hd)
            dx_ref[:, sl] = (r * (dyg - xh * c)).astype(dx_ref.dtype)
            dg = dg + jnp.sum(dyv * xh, axis=0, keepdims=True)
        if hd * 2 == LANES:
            dg8 = jnp.broadcast_to(dg, (8, LANES))
            dg = (dg8 + pltpu.roll(dg8, shift=hd, axis=1))[0:1]
        else:
            assert hd == LANES

        @pl.when(pl.program_id(0) == 0)
        def _():
            dg_ref[...] = jnp.zeros_like(dg_ref)

        dg_ref[...] += dg

    assert (xo * LANES) % width == 0
    return pl.pallas_call(
        body, name=name, grid=(R // tr,),
        in_specs=[pl.BlockSpec((tr, width), lambda i: (i, xo * LANES // width)), pl.BlockSpec((1, LANES), lambda i: (0, 0)),
                  pl.BlockSpec((tr, width), lambda i: (i, 0))],
        out_specs=[pl.BlockSpec((tr, width), lambda i: (i, 0)), pl.BlockSpec((1, LANES), lambda i: (0, 0))],
        out_shape=[jax.ShapeDtypeStruct((R, width), BF16), jax.ShapeDtypeStruct((1, LANES), F32)],
        compiler_params=_params(("arbitrary",)),
    )(xa, g_lanes, dy)


def _split3_dot(x, tri):
    a = x.astype(BF16)
    r = x - a.astype(F32)
    b = r.astype(BF16)
    c = (r - b.astype(F32)).astype(BF16)
    return _dot(a, tri) + _dot(b, tri) + _dot(c, tri)


def _forget_fwd(logit_t, b_col, *, name, blk=512):
    H, S = logit_t.shape
    blk = _tile(S, blk)

    def body(l_ref, b_ref, f_ref):
        r_i = lax.broadcasted_iota(jnp.int32, (blk, blk), 0)
        c_i = lax.broadcasted_iota(jnp.int32, (blk, blk), 1)
        upto = (r_i <= c_i).astype(BF16)
        carry = jnp.zeros((H, 1), F32)
        for j in range(S // blk):
            u = l_ref[:, j * blk:(j + 1) * blk] + b_ref[...]
            lf, _ = _log_sigmoid_pair(u)
            f_ref[:, j * blk:(j + 1) * blk] = _split3_dot(lf, upto) + carry
            carry = carry + jnp.sum(lf, axis=1, keepdims=True)

    return pl.pallas_call(
        body, name=name,
        out_shape=jax.ShapeDtypeStruct((H, S), F32),
        compiler_params=pltpu.CompilerParams(vmem_limit_bytes=VMEM_LIMIT),
    )(logit_t, b_col)


def _forget_bwd(logit_t, b_col, d_f, *, name, blk=512):
    H, S = logit_t.shape
    blk = _tile(S, blk)

    def body(l_ref, b_ref, df_ref, dl_ref, db_ref):
        r_i = lax.broadcasted_iota(jnp.int32, (blk, blk), 0)
        c_i = lax.broadcasted_iota(jnp.int32, (blk, blk), 1)
        fromon = (r_i >= c_i).astype(BF16)
        carry = jnp.zeros((H, 1), F32)
        db = jnp.zeros((H, 1), F32)
        for j in reversed(range(S // blk)):
            sl = slice(j * blk, (j + 1) * blk)
            dfv = df_ref[:, sl]
            d_lf = _split3_dot(dfv, fromon) + carry
            carry = carry + jnp.sum(dfv, axis=1, keepdims=True)
            u = l_ref[:, sl] + b_ref[...]
            _, lsn = _log_sigmoid_pair(u)
            dl = d_lf * jnp.exp(lsn)
            dl_ref[:, sl] = dl
            db = db + jnp.sum(dl, axis=1, keepdims=True)
        db_ref[...] = db

    return pl.pallas_call(
        body, name=name,
        out_shape=[jax.ShapeDtypeStruct((H, S), F32), jax.ShapeDtypeStruct((H, 1), F32)],
        compiler_params=pltpu.CompilerParams(vmem_limit_bytes=VMEM_LIMIT),
    )(logit_t, b_col, d_f)


def _sigmoid(t):
    return 1.0 / (1.0 + jnp.exp(-t))


def _gate_fwd(o3, w3, proj, D, *, name, tm=256):
    S = proj.shape[0]
    tm = _tile(S, tm)

    def body(o0, o1, o2, w0, w1, w2, g0, g1, g2, out_ref):
        acc = None
        for o_ref, w_ref, g_ref in ((o0, w0, g0), (o1, w1, g1), (o2, w2, g2)):
            t = _sigmoid(g_ref[...]) * _dot(o_ref[...], w_ref[...])
            acc = t if acc is None else acc + t
        out_ref[...] = acc.astype(out_ref.dtype)

    ospec = lambda d: pl.BlockSpec((tm, d), lambda i: (i, 0))
    wspec = lambda w: pl.BlockSpec(w.shape, lambda i: (0, 0))
    gspec = lambda j: pl.BlockSpec((tm, D), lambda i: (i, j))
    return pl.pallas_call(
        body, name=name, grid=(S // tm,),
        in_specs=[ospec(o.shape[1]) for o in o3] + [wspec(w) for w in w3] + [gspec(j) for j in range(3)],
        out_specs=pl.BlockSpec((tm, D), lambda i: (i, 0)),
        out_shape=jax.ShapeDtypeStruct((S, D), BF16),
        compiler_params=_params(("parallel",)),
    )(*o3, *w3, proj, proj, proj)


def _gate_bwd(o3, w3, proj, dmerged, D, *, name, tm=256):
    S = proj.shape[0]
    tm = _tile(S, tm)

    def body(o0, o1, o2, w0, w1, w2, g0, g1, g2, dm_ref, dg_ref, db0, db1, db2, do0, do1, do2):
        dm = dm_ref[...]
        for j, (o_ref, w_ref, g_ref, db_ref, do_ref) in enumerate(
                ((o0, w0, g0, db0, do0), (o1, w1, g1, db1, do1), (o2, w2, g2, db2, do2))):
            s = _sigmoid(g_ref[...])
            br = _dot(o_ref[...], w_ref[...])
            dg_ref[:, j * D:(j + 1) * D] = (dm * br * s * (1.0 - s)).astype(dg_ref.dtype)
            dbr = (dm * s).astype(BF16)
            db_ref[...] = dbr
            do_ref[...] = _dot(dbr, w_ref[...], _NT)

    ospec = lambda d: pl.BlockSpec((tm, d), lambda i: (i, 0))
    wspec = lambda w: pl.BlockSpec(w.shape, lambda i: (0, 0))
    gspec = lambda j: pl.BlockSpec((tm, D), lambda i: (i, j))
    dspec = pl.BlockSpec((tm, D), lambda i: (i, 0))
    return pl.pallas_call(
        body, name=name, grid=(S // tm,),
        in_specs=[ospec(o.shape[1]) for o in o3] + [wspec(w) for w in w3] + [gspec(j) for j in range(3)] + [dspec],
        out_specs=[pl.BlockSpec((tm, 3 * D), lambda i: (i, 0))] + [dspec] * 3 + [ospec(o.shape[1]) for o in o3],
        out_shape=[jax.ShapeDtypeStruct((S, 3 * D), BF16)] + [jax.ShapeDtypeStruct((S, D), BF16)] * 3
        + [jax.ShapeDtypeStruct((S, o.shape[1]), F32) for o in o3],
        compiler_params=_params(("parallel",)),
    )(*o3, *w3, proj, proj, proj, dmerged)


def _loss_sum(dy, D, *, name):
    R, C = dy.shape
    tr = _row_tile(R, C, 2)

    def body(dy_ref, out_ref):
        @pl.when(pl.program_id(0) == 0)
        def _():
            out_ref[...] = jnp.zeros_like(out_ref)

        v = dy_ref[...]
        out_ref[...] += (0.5 * D) * jnp.sum(v * v)

    return pl.pallas_call(
        body, name=name, grid=(R // tr,),
        in_specs=[pl.BlockSpec((tr, C), lambda i: (i, 0))],
        out_specs=pl.BlockSpec((8, 128), lambda i: (0, 0)),
        out_shape=jax.ShapeDtypeStruct((8, 128), F32),
        compiler_params=_params(("arbitrary",)),
    )(dy)[0, 0]


def _adamw_math(w, g, m, v):
    m2 = ADAM_B1 * m + (1.0 - ADAM_B1) * g
    v2 = ADAM_B2 * v + (1.0 - ADAM_B2) * (g * g)
    m_hat = m2 / (1.0 - ADAM_B1 ** ADAM_STEP)
    v_hat = v2 / (1.0 - ADAM_B2 ** ADAM_STEP)
    delta = -ADAM_LR * (m_hat / (jnp.sqrt(v_hat) + ADAM_EPS) + ADAM_WD * w)
    return delta, m2, v2


def _adamw(w, g, m, v, *, name):
    return _ew(_adamw_math, [w, g, m, v], (F32, F32, F32), name=name)


def _adamw_small(w, parts, m, v, *, name):
    n = parts.shape[0]

    def body(w_ref, p_ref, m_ref, v_ref, g_ref, d_ref, m2_ref, v2_ref):
        g = p_ref[0]
        for i in range(1, n):
            g = g + p_ref[i]
        g_ref[...] = g
        d_ref[...], m2_ref[...], v2_ref[...] = _adamw_math(w_ref[...], g, m_ref[...], v_ref[...])

    shp = jax.ShapeDtypeStruct(w.shape, F32)
    return pl.pallas_call(body, name=name, out_shape=[shp] * 4)(w, parts, m, v)


_ANY = pl.BlockSpec(memory_space=pl.ANY)


def _mesh_place():
    x, y, c = lax.axis_index("x"), lax.axis_index("y"), lax.axis_index("c")
    chips = [(1 - x, y), (x, 1 - y), (1 - x, 1 - y)]
    return x, y, c, chips


def _remote(src, dst, sems, i, to):
    send_sems, recv_sems = sems
    return pltpu.make_async_remote_copy(src_ref=src, dst_ref=dst, send_sem=send_sems.at[i], recv_sem=recv_sems.at[i],
                                        device_id=to, device_id_type=MESH_ID)


def _gather_weights(shards):
    n = len(shards)

    def body(*refs):
        ins, outs = refs[:n], refs[n:2 * n]
        sems = refs[2 * n:2 * n + 2]
        x, y, c, chips = _mesh_place()
        me = 2 * x + y
        sibling = (x, y, 1 - c)
        sent = []
        for w in range(n):
            for j, chip in enumerate(chips):
                cp = _remote(ins[w].at[c], outs[w].at[me, c], sems, 6 * w + j, (chip[0], chip[1], c))
                cp.start()
                sent.append(cp)
        for w in range(n):
            for j, chip in enumerate(chips):
                got = outs[w].at[2 * chip[0] + chip[1], c]
                _remote(got, got, sems, 6 * w + j, sibling).wait_recv()
                cp = _remote(got, got, sems, 6 * w + 3 + j, sibling)
                cp.start()
                sent.append(cp)
        for w in range(n):
            for j, chip in enumerate(chips):
                got = outs[w].at[2 * chip[0] + chip[1], 1 - c]
                _remote(got, got, sems, 6 * w + 3 + j, sibling).wait_recv()
        for cp in sent:
            cp.wait_send()

    outs = pl.pallas_call(
        body, name="gather_weights",
        in_specs=[_ANY] * n, out_specs=[_ANY] * n,
        out_shape=[jax.ShapeDtypeStruct((N_CHIPS,) + s.shape, s.dtype) for s in shards],
        scratch_shapes=[pltpu.SemaphoreType.DMA((6 * n,)), pltpu.SemaphoreType.DMA((6 * n,))],
    )(*shards)
    me = 2 * lax.axis_index("x") + lax.axis_index("y")
    return [lax.dynamic_update_index_in_dim(o, s, me, 0) for o, s in zip(outs, shards)]


def _exchange_siblings(grads, small):
    n = len(grads)

    def body(*refs):
        ins, small_ref = refs[:n], refs[n]
        got, small_out = refs[n + 1:2 * n + 1], refs[2 * n + 1]
        sems = refs[2 * n + 2:2 * n + 4]
        x, y, c, chips = _mesh_place()
        sibling = (x, y, 1 - c)
        me = 4 * x + 2 * y + c
        sent = []
        for w in range(n):
            for s in range(N_CHIPS):
                cp = _remote(ins[w].at[s, 1 - c], got[w].at[s], sems, N_CHIPS * w + s, sibling)
                cp.start()
                sent.append(cp)
        peers = [(x, y, 1 - c)] + [(ch[0], ch[1], cc) for ch in chips for cc in (c, 1 - c)]
        for j, peer in enumerate(peers):
            cp = _remote(small_ref, small_out.at[me], sems, N_CHIPS * n + j, peer)
            cp.start()
            sent.append(cp)
        for w in range(n):
            for s in range(N_CHIPS):
                _remote(got[w].at[s], got[w].at[s], sems, N_CHIPS * w + s, sibling).wait_recv()
        for j, peer in enumerate(peers):
            frm = small_out.at[4 * peer[0] + 2 * peer[1] + peer[2]]
            _remote(frm, frm, sems, N_CHIPS * n + j, peer).wait_recv()
        for cp in sent:
            cp.wait_send()

    halves = [jax.ShapeDtypeStruct((N_CHIPS,) + g.shape[2:], g.dtype) for g in grads]
    n_sem = N_CHIPS * n + 7
    outs = pl.pallas_call(
        body, name="exchange_siblings",
        in_specs=[_ANY] * (n + 1), out_specs=[_ANY] * (n + 1),
        out_shape=halves + [jax.ShapeDtypeStruct((8,) + small.shape, small.dtype)],
        scratch_shapes=[pltpu.SemaphoreType.DMA((n_sem,)), pltpu.SemaphoreType.DMA((n_sem,))],
    )(*grads, small)
    c = lax.axis_index("c")
    me = 4 * lax.axis_index("x") + 2 * lax.axis_index("y") + c
    own = [lax.dynamic_index_in_dim(g, c, 1, keepdims=False) for g in grads]
    return own, outs[:n], lax.dynamic_update_index_in_dim(outs[n], small, me, 0)


def _exchange_chips(parts):
    n = len(parts)

    def body(*refs):
        ins, got = refs[:n], refs[n:2 * n]
        sems = refs[2 * n:2 * n + 2]
        x, y, c, chips = _mesh_place()
        sent = []
        for w in range(n):
            for j, chip in enumerate(chips):
                cp = _remote(ins[w].at[2 * chip[0] + chip[1]], got[w].at[j], sems, 3 * w + j, (chip[0], chip[1], c))
                cp.start()
                sent.append(cp)
        for w in range(n):
            for j in range(3):
                _remote(got[w].at[j], got[w].at[j], sems, 3 * w + j, (x, y, c)).wait_recv()
        for cp in sent:
            cp.wait_send()

    got = pl.pallas_call(
        body, name="exchange_chips",
        in_specs=[_ANY] * n, out_specs=[_ANY] * n,
        out_shape=[jax.ShapeDtypeStruct((3,) + p.shape[1:], p.dtype) for p in parts],
        scratch_shapes=[pltpu.SemaphoreType.DMA((3 * n,)), pltpu.SemaphoreType.DMA((3 * n,))],
    )(*parts)
    me = 2 * lax.axis_index("x") + lax.axis_index("y")
    return [lax.dynamic_index_in_dim(p, me, 0, keepdims=False) for p in parts], got


def _share_halves(halves):
    n = len(halves)

    def body(*refs):
        ins, outs = refs[:n], refs[n:2 * n]
        sems = refs[2 * n:2 * n + 2]
        x, y, c, _ = _mesh_place()
        sibling = (x, y, 1 - c)
        sent = [_remote(ins[w], outs[w].at[c], sems, w, sibling) for w in range(n)]
        for cp in sent:
            cp.start()
        for w in range(n):
            _remote(outs[w].at[1 - c], outs[w].at[1 - c], sems, w, sibling).wait_recv()
        for cp in sent:
            cp.wait_send()

    outs = pl.pallas_call(
        body, name="share_halves",
        in_specs=[_ANY] * n, out_specs=[_ANY] * n,
        out_shape=[jax.ShapeDtypeStruct((2,) + h.shape, h.dtype) for h in halves],
        scratch_shapes=[pltpu.SemaphoreType.DMA((n,)), pltpu.SemaphoreType.DMA((n,))],
    )(*halves)
    c = lax.axis_index("c")
    return [lax.dynamic_update_index_in_dim(o, h, c, 0) for o, h in zip(outs, halves)]


def _to_heads(t, n, d):
    s = t.shape[0]
    return t.reshape(s, n, d).transpose(1, 0, 2)


def _from_heads(t):
    h, s, d = t.shape
    return t.transpose(1, 0, 2).reshape(s, h * d)


def _head_norm_fwd(t, g, *, name):
    h, s, d = t.shape
    return _rmsnorm_fwd(t.reshape(h * s, d), g, BF16, name=name).reshape(h, s, d)


def _head_norm_bwd(t, g, dy, *, name):
    h, s, d = t.shape
    dx, dg = _rmsnorm_bwd(t.reshape(h * s, d), g, dy.reshape(h * s, d), name=name)
    return dx.reshape(h, s, d), dg


def _local_step(x, mem, target, small, W):
    S, D = x.shape
    o_gate, o_qkv, o_mq, o_f = 0, 3 * D, 3 * D + 2 * 3 * D_SB, 3 * D + 2 * 3 * D_SB + D_MEM
    tq = 256

    h = _rmsnorm_fwd(x, small["g_mix_norm"], BF16, name="mix_norm")
    proj = _mm(h, W["w_in"], name="in_proj", tn=768)
    blk = lambda j: (proj, (o_qkv + j * D_SB) // LANES)
    sb_q, sb_k, sb_v, fx_q, fx_k, fx_v = [blk(j) for j in range(6)]
    m_q = (proj, o_mq // LANES)
    f_logit_t = proj[:, o_f:o_f + FOX_HEADS].T
    b_col = small["b_forget"].reshape(FOX_HEADS, 1)
    lanes = lambda g: jnp.tile(g, (1, LANES // g.shape[1]))
    g_fq, g_fk, g_mq, g_mk = [lanes(small[k]) for k in ("g_fox_q", "g_fox_k", "g_mem_q", "g_mem_k")]

    o_sb, sb_tot = _sbl_fwd(sb_q, sb_k, sb_v, width=D_SB, hd=HD, name="sb_fwd", tq=tq)

    fq = _hnorm_fwd(fx_q, g_fq, width=D_FOX, hd=HD, name="fox_q_norm")
    fk = _hnorm_fwd(fx_k, g_fk, width=D_FOX, hd=HD, name="fox_k_norm")
    f_cum = _forget_fwd(f_logit_t, b_col, name="forget_fwd")
    tkf = _tile(S, tq)
    f_bias = (f_cum.reshape(FOX_HEADS, S, 1), f_cum.reshape(FOX_HEADS, S // tkf, 1, tkf))
    o_fox, fox_lse = _sml_fwd((fq, 0), (fk, 0), fx_v, f_bias, width=D_FOX, hd=HD, causal=True, name="fox_fwd",
                              tq=tq, tk=tq)

    mh = _rmsnorm_fwd(mem, small["g_mem_norm"], BF16, name="mem_norm")
    mkv = _mm(mh, W["w_mem_kv"], name="mem_kv")
    mv = (mkv, D_MEM // LANES)
    mq = _hnorm_fwd(m_q, g_mq, width=D_MEM, hd=MEM_HD, name="mem_q_norm")
    mk = _hnorm_fwd((mkv, 0), g_mk, width=D_MEM, hd=MEM_HD, name="mem_k_norm")
    o_mem, mem_lse = _sml_fwd((mq, 0), (mk, 0), mv, width=D_MEM, hd=MEM_HD, causal=False, name="mem_fwd", tq=tq, tk=256)

    o3 = [o_sb, o_fox, o_mem]
    w3 = [W["w_branch_sb"], W["w_branch_fox"], W["w_branch_mem"]]
    merged = _gate_fwd(o3, w3, proj, D, name="gate_fwd")
    x1 = _mm(merged, W["w_out"], name="out_proj", extras=(x,), epilogue=lambda acc, res: (res + acc,))
    h2 = _rmsnorm_fwd(x1, small["g_mlp_norm"], BF16, name="mlp_norm")

    def relu2(acc):
        u = jnp.maximum(acc, 0.0)
        return u, u * u

    u, a = _mm(h2, W["w_ff_up"], name="ff_up", out_dtypes=(BF16, BF16), epilogue=relu2)
    dy = _mm(a, W["w_ff_down"], name="ff_down", extras=(x1, target),
             epilogue=lambda acc, res, tgt: ((res + acc - tgt) * (1.0 / D),))
    loss = _loss_sum(dy, D, name="loss")

    G = {}
    du = _mm(dy, W["w_ff_down"], name="d_ff_act", tb=True, out_dtypes=(BF16,), extras=(u,),
             epilogue=lambda acc, uu: (acc * (2.0 * uu.astype(F32)),))
    G["w_ff_down"] = _mm(a, dy, name="d_w_ff_down", ta=True, out_dtypes=(BF16,))
    G["w_ff_up"] = _mm(h2, du, name="d_w_ff_up", ta=True, out_dtypes=(BF16,))
    dh2 = _mm(du, W["w_ff_up"], name="d_mlp_in", tb=True)
    dx1, dg_mlp = _rmsnorm_bwd(x1, small["g_mlp_norm"], dh2, add=dy, name="d_mlp_norm")
    dmerged = _mm(dx1, W["w_out"], name="d_merged", tb=True)
    G["w_out"] = _mm(merged, dx1, name="d_w_out", ta=True, out_dtypes=(BF16,))
    dgate, db0, db1, db2, do_sb, do_fox, do_mem = _gate_bwd(o3, w3, proj, dmerged, D, name="gate_bwd")
    for nm, o, db in zip(("w_branch_sb", "w_branch_fox", "w_branch_mem"), o3, (db0, db1, db2)):
        G[nm] = _mm(o, db, name="d_" + nm, ta=True, out_dtypes=(BF16,))

    dsb = _sbl_bwd(sb_q, sb_k, sb_v, (do_sb, 0), (sb_tot, 0), width=D_SB, hd=HD, name="sb_bwd", tq=tq)
    dfq, dfk, dfv, df_row, df_col = _sml_bwd((fq, 0), (fk, 0), fx_v, (o_fox, 0), (fox_lse, 0), (do_fox, 0), f_bias,
                                             width=D_FOX, hd=HD, causal=True, name="fox_bwd", tq=tq, tk=tq)
    dfx_q, dg_fox_q = _hnorm_bwd(fx_q, g_fq, dfq, width=D_FOX, hd=HD, name="d_fox_q_norm")
    dfx_k, dg_fox_k = _hnorm_bwd(fx_k, g_fk, dfk, width=D_FOX, hd=HD, name="d_fox_k_norm")
    d_fcum = df_row.reshape(FOX_HEADS, S) - df_col.reshape(FOX_HEADS, S)
    d_flogit_t, db_forget = _forget_bwd(f_logit_t, b_col, d_fcum, name="forget_bwd")

    dmq_n, dmk_n, dmv = _sml_bwd((mq, 0), (mk, 0), mv, (o_mem, 0), (mem_lse, 0), (do_mem, 0), width=D_MEM, hd=MEM_HD,
                                 causal=False, name="mem_bwd", tq=tq, tk=256)
    dm_q, dg_mem_q = _hnorm_bwd(m_q, g_mq, dmq_n, width=D_MEM, hd=MEM_HD, name="d_mem_q_norm")
    dmk_raw, dg_mem_k = _hnorm_bwd((mkv, 0), g_mk, dmk_n, width=D_MEM, hd=MEM_HD, name="d_mem_k_norm")
    dmkv = jnp.concatenate([dmk_raw, dmv.astype(BF16)], axis=1)
    G["w_mem_kv"] = _mm(mh, dmkv, name="d_w_mem_kv", ta=True, out_dtypes=(BF16,))
    dmh = _mm(dmkv, W["w_mem_kv"], name="d_mem_h", tb=True)
    _, dg_mem = _rmsnorm_bwd(mem, small["g_mem_norm"], dmh, name="d_mem_norm")
    dg_fox_q, dg_fox_k = dg_fox_q[:, :HD], dg_fox_k[:, :HD]

    dproj = jnp.concatenate(
        [dgate] + [t.astype(BF16) for t in (*dsb, dfx_q, dfx_k, dfv, dm_q)]
        + [d_flogit_t.T.astype(BF16), jnp.zeros((S, F_PAD - FOX_HEADS), BF16)], axis=1)
    G["w_in"] = _mm(h, dproj, name="d_w_in", ta=True, out_dtypes=(BF16,), tn=768)
    dh = _mm(dproj, W["w_in"], name="d_mix_in", tb=True, tk=768)
    grad_x, dg_mix = _rmsnorm_bwd(x, small["g_mix_norm"], dh, add=dx1, name="d_mix_norm")

    small_grads = dict(g_mix_norm=dg_mix, g_mem_norm=dg_mem, b_forget=db_forget.reshape(1, FOX_HEADS),
                       g_fox_q=dg_fox_q, g_fox_k=dg_fox_k, g_mem_q=dg_mem_q, g_mem_k=dg_mem_k, g_mlp_norm=dg_mlp)
    return loss, grad_x, G, small_grads


BIG = ("w_in", "w_mem_kv", "w_branch_sb", "w_branch_fox", "w_branch_mem", "w_out", "w_ff_up", "w_ff_down")
COLUMN_SHARDED = ("w_in", "w_branch_sb", "w_branch_fox", "w_branch_mem", "w_ff_up")
SMALL = ("g_mix_norm", "g_mem_norm", "b_forget", "g_fox_q", "g_fox_k", "g_mem_q", "g_mem_k", "g_mlp_norm")
ORDER = ("g_mix_norm", "g_mem_norm", "w_in", "b_forget", "g_fox_q", "g_fox_k", "g_mem_q", "g_mem_k", "w_mem_kv",
         "w_branch_sb", "w_branch_fox", "w_branch_mem", "w_out", "g_mlp_norm", "w_ff_up", "w_ff_down")


def _unshard(name, gathered):
    n, _, rh, c = gathered.shape
    t = gathered.reshape(n, 2 * rh, c)
    if name in COLUMN_SHARDED:
        return t.transpose(1, 0, 2).reshape(2 * rh, n * c)
    return t.reshape(n * 2 * rh, c)


def _reshard(name, full):
    if name in COLUMN_SHARDED:
        r, c = full.shape
        t = full.reshape(r, N_CHIPS, c // N_CHIPS).transpose(1, 0, 2)
    else:
        r, c = full.shape[0] // N_CHIPS, full.shape[1]
        t = full.reshape(N_CHIPS, r, c)
    return t.reshape(N_CHIPS, 2, t.shape[1] // 2, t.shape[2])


def _pad_in_proj(w_in, D):
    n_qkv = 6 * D_SB
    o_mq = n_qkv + FOX_HEADS
    o_gate = o_mq + D_MEM
    return jnp.concatenate([w_in[:, o_gate:], w_in[:, :n_qkv], w_in[:, o_mq:o_gate], w_in[:, n_qkv:o_mq],
                            jnp.zeros((w_in.shape[0], F_PAD - FOX_HEADS), w_in.dtype)], axis=1)


def _unpad_in_proj(g, D):
    n_qkv = 6 * D_SB
    o_qkv, o_mq, o_f = 3 * D, 3 * D + n_qkv, 3 * D + n_qkv + D_MEM
    return jnp.concatenate([g[:, o_qkv:o_mq], g[:, o_f:o_f + FOX_HEADS], g[:, o_mq:o_f], g[:, :o_qkv]], axis=1)


def _pack_small(vals):
    width = max(vals[k].shape[1] for k in SMALL)
    return jnp.concatenate([jnp.pad(vals[k].astype(F32), ((0, 0), (0, width - vals[k].shape[1]))) for k in SMALL], axis=0)


def _unpack_small(packed, like):
    return {k: packed[i:i + 1, :like[k].shape[1]] for i, k in enumerate(SMALL)}


def kernel(x, mem, g_mix_norm, g_mem_norm, w_in, b_forget, g_fox_q, g_fox_k, g_mem_q, g_mem_k, w_mem_kv, w_branch_sb, w_branch_fox, w_branch_mem, w_out, g_mlp_norm, w_ff_up, w_ff_down, loss_target, m_g_mix_norm, m_g_mem_norm, m_w_in, m_b_forget, m_g_fox_q, m_g_fox_k, m_g_mem_q, m_g_mem_k, m_w_mem_kv, m_w_branch_sb, m_w_branch_fox, m_w_branch_mem, m_w_out, m_g_mlp_norm, m_w_ff_up, m_w_ff_down, v_g_mix_norm, v_g_mem_norm, v_w_in, v_b_forget, v_g_fox_q, v_g_fox_k, v_g_mem_q, v_g_mem_k, v_w_mem_kv, v_w_branch_sb, v_w_branch_fox, v_w_branch_mem, v_w_out, v_g_mlp_norm, v_w_ff_up, v_w_ff_down):
    given = dict(locals())
    D = x.shape[-1]
    weights = {k: given[k] for k in ORDER}
    moms = {k: given["m_" + k] for k in ORDER}
    vars_ = {k: given["v_" + k] for k in ORDER}

    shards = []
    for k in BIG:
        w = weights[k][0].astype(BF16)
        shards.append(w.reshape(2, w.shape[0] // 2, w.shape[1]))
    gathered = _gather_weights(shards)
    W = {k: _unshard(k, g) for k, g in zip(BIG, gathered)}
    W["w_in"] = _pad_in_proj(W["w_in"], D)

    small = {k: weights[k] for k in SMALL}
    loss_part, grad_x, G, small_grads = _local_step(x[0], mem[0], loss_target[0], small, W)
    G["w_in"] = _unpad_in_proj(G["w_in"], D)

    stacked = [_reshard(k, G[k]) for k in BIG]
    own, sib, small_parts = _exchange_siblings(stacked, _pack_small(small_grads))
    add2 = lambda p, q: (p.astype(F32) + q.astype(F32),)
    parts = []
    for k, p, q in zip(BIG, own, sib):
        n, r, c = p.shape
        parts.append(_ew(add2, [p.reshape(n * r, c), q.reshape(n * r, c)], (BF16,), name="sum_pair_" + k).reshape(n, r, c))
    own2, got2 = _exchange_chips(parts)
    add4 = lambda p, q0, q1, q2: (((p.astype(F32) + q0.astype(F32)) + q1.astype(F32)) + q2.astype(F32),)
    halves = [_ew(add4, [p, q[0], q[1], q[2]], (F32,), name="sum_chips_" + k) for k, p, q in zip(BIG, own2, got2)]
    reduced = _share_halves(halves)

    grads, deltas, new_m, new_v = {}, {}, {}, {}
    for k, g in zip(BIG, reduced):
        shp = weights[k].shape
        g2 = g.reshape(shp[1], shp[2])
        d, m2, v2 = _adamw(weights[k][0], g2, moms[k][0], vars_[k][0], name="adamw_" + k)
        grads[k], deltas[k], new_m[k], new_v[k] = g2.reshape(shp), d.reshape(shp), m2.reshape(shp), v2.reshape(shp)
    sg, sd, sm, sv = _adamw_small(_pack_small(small), small_parts, _pack_small({k: moms[k] for k in SMALL}),
                                  _pack_small({k: vars_[k] for k in SMALL}), name="adamw_small")
    for dst, packed in ((grads, sg), (deltas, sd), (new_m, sm), (new_v, sv)):
        dst.update(_unpack_small(packed, small))

    loss = lax.psum(loss_part, ("x", "y", "c"))
    return (loss, grad_x[None], *[grads[k] for k in ORDER], *[deltas[k] for k in ORDER],
            *[new_m[k] for k in ORDER], *[new_v[k] for k in ORDER])
```

```python
import functools

import jax
import jax.numpy as jnp
from jax import lax
from jax.experimental import pallas as pl
from jax.experimental.pallas import tpu as pltpu

F32 = jnp.float32
BF16 = jnp.bfloat16
MESH_ID = pl.DeviceIdType.MESH

HD = 64
SB_HEADS = 8
FOX_HEADS = 8
MEM_HEADS = 4
MEM_HD = 128
D_SB = SB_HEADS * HD
D_FOX = FOX_HEADS * HD
D_MEM = MEM_HEADS * MEM_HD
EPS = 1e-6
NEG_INF = -1e30

ADAM_LR = 0.001
ADAM_B1 = 0.9
ADAM_B2 = 0.999
ADAM_EPS = 1e-08
ADAM_WD = 0.01
ADAM_STEP = 10

N_CHIPS = 4
VMEM_LIMIT = 56 * 1024 * 1024

F_PAD = 256


def _tile(n, target, align=128):
    if n <= target:
        return n
    best = None
    t = align
    while t <= target:
        if n % t == 0:
            best = t
        t += align
    assert best is not None, (n, target, align)
    return best


def _params(sem):
    return pltpu.CompilerParams(dimension_semantics=sem, vmem_limit_bytes=VMEM_LIMIT)


def _mm(a, b, *, name, ta=False, tb=False, out_dtypes=(F32,), epilogue=None, extras=(),
        tm=512, tn=512, tk=1024):
    if ta:
        K, M = a.shape
    else:
        M, K = a.shape
    if tb:
        N, K2 = b.shape
    else:
        K2, N = b.shape
    assert K == K2, (a.shape, b.shape, ta, tb)
    tm, tn, tk = _tile(M, tm), _tile(N, tn), _tile(K, tk)
    nk = K // tk
    n_extra, n_out = len(extras), len(out_dtypes)
    if epilogue is None:
        epilogue = lambda acc: (acc,)
    dims = (((0 if ta else 1,), (1 if tb else 0,)), ((), ()))

    def body(a_ref, b_ref, *rest):
        extra_refs = rest[:n_extra]
        out_refs = rest[n_extra:n_extra + n_out]
        acc_ref = rest[n_extra + n_out]
        k = pl.program_id(2)

        @pl.when(k == 0)
        def _():
            acc_ref[...] = jnp.zeros_like(acc_ref)

        acc_ref[...] += lax.dot_general(a_ref[...].astype(BF16), b_ref[...].astype(BF16), dims,
                                        preferred_element_type=F32)

        @pl.when(k == nk - 1)
        def _():
            outs = epilogue(acc_ref[...], *[r[...] for r in extra_refs])
            for o_ref, o in zip(out_refs, outs):
                o_ref[...] = o.astype(o_ref.dtype)

    a_spec = pl.BlockSpec((tk, tm), lambda i, j, k: (k, i)) if ta else pl.BlockSpec((tm, tk), lambda i, j, k: (i, k))
    b_spec = pl.BlockSpec((tn, tk), lambda i, j, k: (j, k)) if tb else pl.BlockSpec((tk, tn), lambda i, j, k: (k, j))
    mn_spec = pl.BlockSpec((tm, tn), lambda i, j, k: (i, j))
    outs = pl.pallas_call(
        body, name=name,
        grid=(M // tm, N // tn, nk),
        in_specs=[a_spec, b_spec] + [mn_spec] * n_extra,
        out_specs=[mn_spec] * n_out,
        out_shape=[jax.ShapeDtypeStruct((M, N), dt) for dt in out_dtypes],
        scratch_shapes=[pltpu.VMEM((tm, tn), F32)],
        compiler_params=_params(("parallel", "parallel", "arbitrary")),
    )(a, b, *extras)
    return outs if n_out > 1 else outs[0]


def _row_tile(rows, cols, n_arrays):
    budget = 10 * 1024 * 1024
    cols_padded = -(-cols // 128) * 128
    target = max(16, budget // (cols_padded * 4 * n_arrays * 2))
    return _tile(rows, target, align=16)


def _ew(fn, ins, out_dtypes, *, name):
    R, C = ins[0].shape
    n_in, n_out = len(ins), len(out_dtypes)
    tr = _row_tile(R, C, n_in + n_out)

    def body(*refs):
        outs = fn(*[r[...] for r in refs[:n_in]])
        for o_ref, o in zip(refs[n_in:], outs):
            o_ref[...] = o.astype(o_ref.dtype)

    spec = pl.BlockSpec((tr, C), lambda i: (i, 0))
    outs = pl.pallas_call(
        body, name=name, grid=(R // tr,),
        in_specs=[spec] * n_in, out_specs=[spec] * n_out,
        out_shape=[jax.ShapeDtypeStruct((R, C), dt) for dt in out_dtypes],
        compiler_params=_params(("parallel",)),
    )(*ins)
    return outs if n_out > 1 else outs[0]


def _rmsnorm_fwd(x, g, out_dtype, *, name):
    R, d = x.shape
    tr = _row_tile(R, d, 3)

    def body(x_ref, g_ref, o_ref):
        xv = x_ref[...].astype(F32)
        r = lax.rsqrt(jnp.mean(xv * xv, axis=-1, keepdims=True) + EPS)
        o_ref[...] = (xv * r * g_ref[...]).astype(o_ref.dtype)

    return pl.pallas_call(
        body, name=name, grid=(R // tr,),
        in_specs=[pl.BlockSpec((tr, d), lambda i: (i, 0)), pl.BlockSpec((1, d), lambda i: (0, 0))],
        out_specs=pl.BlockSpec((tr, d), lambda i: (i, 0)),
        out_shape=jax.ShapeDtypeStruct((R, d), out_dtype),
        compiler_params=_params(("parallel",)),
    )(x, g)


def _rmsnorm_bwd(x, g, dy, add=None, *, name):
    R, d = x.shape
    has_add = add is not None
    tr = _row_tile(R, d, 5)

    def body(*refs):
        x_ref, g_ref, dy_ref = refs[:3]
        add_ref = refs[3] if has_add else None
        dx_ref, dg_ref = refs[-2:]
        xv = x_ref[...].astype(F32)
        dyv = dy_ref[...].astype(F32)
        r = lax.rsqrt(jnp.mean(xv * xv, axis=-1, keepdims=True) + EPS)
        xh = xv * r
        dyg = dyv * g_ref[...]
        c = jnp.mean(dyg * xh, axis=-1, keepdims=True)
        dx = r * (dyg - xh * c)
        if has_add:
            dx = dx + add_ref[...]
        dx_ref[...] = dx

        @pl.when(pl.program_id(0) == 0)
        def _():
            dg_ref[...] = jnp.zeros_like(dg_ref)

        dg_ref[...] += jnp.sum(dyv * xh, axis=0, keepdims=True)

    row = pl.BlockSpec((tr, d), lambda i: (i, 0))
    vec = pl.BlockSpec((1, d), lambda i: (0, 0))
    ins = [x, g, dy] + ([add] if has_add else [])
    return pl.pallas_call(
        body, name=name, grid=(R // tr,),
        in_specs=[row, vec, row] + ([row] if has_add else []),
        out_specs=[row, vec],
        out_shape=[jax.ShapeDtypeStruct((R, d), F32), jax.ShapeDtypeStruct((1, d), F32)],
        compiler_params=_params(("arbitrary",)),
    )(*ins)


_NT = (((1,), (1,)), ((), ()))
_TN = (((0,), (0,)), ((), ()))


def _dot(a, b, dims=(((1,), (0,)), ((), ()))):
    return lax.dot_general(a, b, dims, preferred_element_type=F32)


def _split_dot(x, tri):
    hi = x.astype(BF16)
    lo = (x - hi.astype(F32)).astype(BF16)
    return _dot(hi, tri) + _dot(lo, tri)


def _log_sigmoid_pair(z):
    sp = jnp.log(1.0 + jnp.exp(-jnp.abs(z)))
    return jnp.minimum(z, 0.0) - sp, jnp.minimum(-z, 0.0) - sp


def _sb_fwd(q, k, v, *, name, tq=256):
    H, S, hd = q.shape
    tq = _tile(S, tq)
    tk = tq
    scale = hd ** -0.5

    def body(q_ref, k_ref, v_ref, o_ref, tot_ref):
        qi = pl.program_id(1)
        qv = q_ref[0]
        row = qi * tq + lax.broadcasted_iota(jnp.int32, (tq, tk), 0)
        col0 = lax.broadcasted_iota(jnp.int32, (tq, tk), 1)
        later = (lax.broadcasted_iota(jnp.int32, (tk, tk), 0) > lax.broadcasted_iota(jnp.int32, (tk, tk), 1)).astype(BF16)

        def step(i, carry):
            acc, c_rem = carry
            kb = qi - i
            ks = pl.multiple_of(kb * tk, tk)
            kv = k_ref[0, pl.ds(ks, tk), :]
            vv = v_ref[0, pl.ds(ks, tk), :]
            z = _dot(qv, kv, _NT) * scale
            mask = (col0 + ks) < row
            ls, lsn = _log_sigmoid_pair(z)
            rem = jnp.where(mask, lsn, 0.0)
            after = _split_dot(rem, later) + c_rem
            w = jnp.where(mask, jnp.exp(ls + after), 0.0)
            acc = acc + _dot(w.astype(BF16), vv)
            c_rem = c_rem + jnp.sum(rem, axis=1, keepdims=True)
            return acc, c_rem

        acc, c_rem = lax.fori_loop(0, qi + 1, step, (jnp.zeros((tq, hd), F32), jnp.zeros((tq, 1), F32)))
        o_ref[0] = acc
        tot_ref[0] = c_rem

    qspec = pl.BlockSpec((1, tq, hd), lambda h, i: (h, i, 0))
    kspec = pl.BlockSpec((1, S, hd), lambda h, i: (h, 0, 0))
    return pl.pallas_call(
        body, name=name, grid=(H, S // tq),
        in_specs=[qspec, kspec, kspec],
        out_specs=[qspec, pl.BlockSpec((1, tq, 1), lambda h, i: (h, i, 0))],
        out_shape=[jax.ShapeDtypeStruct((H, S, hd), F32), jax.ShapeDtypeStruct((H, S, 1), F32)],
        compiler_params=_params(("parallel", "arbitrary")),
    )(q, k, v)


def _sb_bwd(q, k, v, do, tot, *, name, tq=256):
    H, S, hd = q.shape
    tq = _tile(S, tq)
    tk = tq
    scale = hd ** -0.5

    def body(q_ref, k_ref, v_ref, do_ref, tot_ref, dq_ref, dk_ref, dv_ref):
        qi = pl.program_id(1)

        @pl.when(qi == 0)
        def _():
            dk_ref[...] = jnp.zeros_like(dk_ref)
            dv_ref[...] = jnp.zeros_like(dv_ref)

        qv = q_ref[0]
        dov = do_ref[0].astype(BF16)
        tot_v = tot_ref[0]
        row = qi * tq + lax.broadcasted_iota(jnp.int32, (tq, tk), 0)
        col0 = lax.broadcasted_iota(jnp.int32, (tq, tk), 1)
        r_i = lax.broadcasted_iota(jnp.int32, (tk, tk), 0)
        c_i = lax.broadcasted_iota(jnp.int32, (tk, tk), 1)
        upto = (r_i <= c_i).astype(BF16)
        before = (r_i < c_i).astype(BF16)

        def step(kb, carry):
            dq, c_rem, c_g = carry
            ks = pl.multiple_of(kb * tk, tk)
            kv = k_ref[0, pl.ds(ks, tk), :]
            vv = v_ref[0, pl.ds(ks, tk), :]
            z = _dot(qv, kv, _NT) * scale
            mask = (col0 + ks) < row
            ls, lsn = _log_sigmoid_pair(z)
            rem = jnp.where(mask, lsn, 0.0)
            after = tot_v - (_split_dot(rem, upto) + c_rem)
            w = jnp.where(mask, jnp.exp(ls + after), 0.0)
            dw = _dot(dov, vv, _NT)
            g = dw * w
            g_before = _split_dot(g, before) + c_g
            beta = jnp.exp(ls)
            dz = jnp.where(mask, g * (1.0 - beta) - beta * g_before, 0.0) * scale
            dzb = dz.astype(BF16)
            dq = dq + _dot(dzb, kv)
            dk_ref[0, pl.ds(ks, tk), :] += _dot(dzb, qv, _TN)
            dv_ref[0, pl.ds(ks, tk), :] += _dot(w.astype(BF16), dov, _TN)
            return dq, c_rem + jnp.sum(rem, axis=1, keepdims=True), c_g + jnp.sum(g, axis=1, keepdims=True)

        zero = jnp.zeros((tq, 1), F32)
        dq, _, _ = lax.fori_loop(0, qi + 1, step, (jnp.zeros((tq, hd), F32), zero, zero))
        dq_ref[0] = dq

    qspec = pl.BlockSpec((1, tq, hd), lambda h, i: (h, i, 0))
    kspec = pl.BlockSpec((1, S, hd), lambda h, i: (h, 0, 0))
    full = jax.ShapeDtypeStruct((H, S, hd), F32)
    return pl.pallas_call(
        body, name=name, grid=(H, S // tq),
        in_specs=[qspec, kspec, kspec, qspec, pl.BlockSpec((1, tq, 1), lambda h, i: (h, i, 0))],
        out_specs=[qspec, kspec, kspec],
        out_shape=[full, full, full],
        compiler_params=_params(("parallel", "arbitrary")),
    )(q, k, v, do, tot)


def _softmax_fwd(q, k, v, bias=None, *, name, causal, tq=256, tk=256):
    H, S, hd = q.shape
    Sk = k.shape[1]
    tq, tk = _tile(S, tq), _tile(Sk, tk)
    if causal:
        assert tq == tk and S == Sk
    nk = Sk // tk
    scale = hd ** -0.5
    has_bias = bias is not None

    def body(*refs):
        q_ref, k_ref, v_ref = refs[:3]
        o_ref, lse_ref = refs[-2:]
        qi = pl.program_id(1)
        qv = q_ref[0]
        row = qi * tq + lax.broadcasted_iota(jnp.int32, (tq, tk), 0)
        col0 = lax.broadcasted_iota(jnp.int32, (tq, tk), 1)
        if has_bias:
            b_row = refs[3][0]

        def step(kb, carry):
            m, l, acc = carry
            ks = pl.multiple_of(kb * tk, tk)
            kv = k_ref[0, pl.ds(ks, tk), :]
            vv = v_ref[0, pl.ds(ks, tk), :]
            z = _dot(qv, kv, _NT) * scale
            if has_bias:
                z = z + b_row - refs[4][0, kb]
            if causal:
                z = jnp.where((col0 + ks) <= row, z, NEG_INF)
            m2 = jnp.maximum(m, jnp.max(z, axis=1, keepdims=True))
            p = jnp.exp(z - m2)
            alpha = jnp.exp(m - m2)
            l = alpha * l + jnp.sum(p, axis=1, keepdims=True)
            acc = alpha * acc + _dot(p.astype(BF16), vv)
            return m2, l, acc

        init = (jnp.full((tq, 1), NEG_INF, F32), jnp.zeros((tq, 1), F32), jnp.zeros((tq, hd), F32))
        m, l, acc = lax.fori_loop(0, (qi + 1) if causal else nk, step, init)
        o_ref[0] = acc / l
        lse_ref[0] = m + jnp.log(l)

    qspec = pl.BlockSpec((1, tq, hd), lambda h, i: (h, i, 0))
    kspec = pl.BlockSpec((1, Sk, hd), lambda h, i: (h, 0, 0))
    vspec = pl.BlockSpec((1, tq, 1), lambda h, i: (h, i, 0))
    in_specs = [qspec, kspec, kspec]
    ins = [q, k, v]
    if has_bias:
        in_specs += [vspec, pl.BlockSpec((1, nk, 1, tk), lambda h, i: (h, 0, 0, 0))]
        ins += list(bias)
    return pl.pallas_call(
        body, name=name, grid=(H, S // tq),
        in_specs=in_specs, out_specs=[qspec, vspec],
        out_shape=[jax.ShapeDtypeStruct((H, S, hd), F32), jax.ShapeDtypeStruct((H, S, 1), F32)],
        compiler_params=_params(("parallel", "arbitrary")),
    )(*ins)


def _softmax_bwd(q, k, v, o, lse, do, bias=None, *, name, causal, tq=256, tk=256):
    H, S, hd = q.shape
    Sk = k.shape[1]
    tq, tk = _tile(S, tq), _tile(Sk, tk)
    nk = Sk // tk
    scale = hd ** -0.5
    has_bias = bias is not None
    n_in = 8 if has_bias else 6

    def body(*refs):
        q_ref, k_ref, v_ref, o_ref, lse_ref, do_ref = refs[:6]
        dq_ref, dk_ref, dv_ref = refs[n_in:n_in + 3]
        qi = pl.program_id(1)

        @pl.when(qi == 0)
        def _():
            dk_ref[...] = jnp.zeros_like(dk_ref)
            dv_ref[...] = jnp.zeros_like(dv_ref)
            if has_bias:
                refs[n_in + 4][...] = jnp.zeros_like(refs[n_in + 4])

        qv = q_ref[0]
        do32 = do_ref[0]
        dov = do32.astype(BF16)
        delta = jnp.sum(do32 * o_ref[0], axis=1, keepdims=True)
        lse_v = lse_ref[0]
        row = qi * tq + lax.broadcasted_iota(jnp.int32, (tq, tk), 0)
        col0 = lax.broadcasted_iota(jnp.int32, (tq, tk), 1)
        if has_bias:
            b_row = refs[6][0]

        def step(kb, carry):
            dq, db_row = carry
            ks = pl.multiple_of(kb * tk, tk)
            kv = k_ref[0, pl.ds(ks, tk), :]
            vv = v_ref[0, pl.ds(ks, tk), :]
            z = _dot(qv, kv, _NT) * scale
            if has_bias:
                z = z + b_row - refs[7][0, kb]
            p = jnp.exp(z - lse_v)
            if causal:
                p = jnp.where((col0 + ks) <= row, p, 0.0)
            dp = _dot(dov, vv, _NT)
            dz = p * (dp - delta)
            dzb = dz.astype(BF16)
            dq = dq + _dot(dzb, kv)
            dk_ref[0, pl.ds(ks, tk), :] += _dot(dzb, qv, _TN) * scale
            dv_ref[0, pl.ds(ks, tk), :] += _dot(p.astype(BF16), dov, _TN)
            if has_bias:
                db_row = db_row + jnp.sum(dz, axis=1, keepdims=True)
                refs[n_in + 4][0, kb] += jnp.sum(dz, axis=0, keepdims=True)
            return dq, db_row

        dq, db_row = lax.fori_loop(0, (qi + 1) if causal else nk, step,
                                   (jnp.zeros((tq, hd), F32), jnp.zeros((tq, 1), F32)))
        dq_ref[0] = dq * scale
        if has_bias:
            refs[n_in + 3][0] = db_row

    qspec = pl.BlockSpec((1, tq, hd), lambda h, i: (h, i, 0))
    kspec = pl.BlockSpec((1, Sk, hd), lambda h, i: (h, 0, 0))
    vspec = pl.BlockSpec((1, tq, 1), lambda h, i: (h, i, 0))
    cspec = pl.BlockSpec((1, nk, 1, tk), lambda h, i: (h, 0, 0, 0))
    in_specs = [qspec, kspec, kspec, qspec, vspec, qspec]
    ins = [q, k, v, o, lse, do]
    out_specs = [qspec, kspec, kspec]
    out_shape = [jax.ShapeDtypeStruct((H, S, hd), F32), jax.ShapeDtypeStruct((H, Sk, hd), F32),
                 jax.ShapeDtypeStruct((H, Sk, hd), F32)]
    if has_bias:
        in_specs += [vspec, cspec]
        ins += list(bias)
        out_specs += [vspec, cspec]
        out_shape += [jax.ShapeDtypeStruct((H, S, 1), F32), jax.ShapeDtypeStruct((H, nk, 1, tk), F32)]
    return pl.pallas_call(
        body, name=name, grid=(H, S // tq),
        in_specs=in_specs, out_specs=out_specs, out_shape=out_shape,
        compiler_params=_params(("parallel", "arbitrary")),
    )(*ins)


LANES = 128
_LOW = -3e38


def _lane_masks(hd, rows):
    if hd == LANES:
        return [None]
    lane = lax.broadcasted_iota(jnp.int32, (rows, LANES), 1)
    return [(lane >= hh * hd) & (lane < (hh + 1) * hd) for hh in range(LANES // hd)]


def _keep(t, m):
    return t if m is None else jnp.where(m, t, 0.0)


def _merge(parts, masks):
    out = parts[-1]
    for p, m in zip(parts[-2::-1], masks[-2::-1]):
        out = jnp.where(m, p, out)
    return out


def _row_value(t, m):
    return jnp.max(t if m is None else jnp.where(m, t, _LOW), axis=1, keepdims=True)


def _cols(tq, off):
    return pl.BlockSpec((tq, LANES), lambda g, i: (i, off + g))


def _cols_all(rows, off):
    return pl.BlockSpec((rows, LANES), lambda g, i: (0, off + g))


def _softplus_parts(z):
    pos = jnp.maximum(z, 0.0) + jnp.log(1.0 + jnp.exp(-jnp.abs(z)))
    return pos, z - pos


class _ChipExchange:
    def __init__(self, kind, ins):
        assert kind in ("gather", "scatter")
        self.kind, self.ins = kind, list(ins)
        lead = (lambda s: (N_CHIPS,) + s) if kind == "gather" else (lambda s: (3,) + s[1:])
        self.out_shape = [jax.ShapeDtypeStruct(lead(a.shape), a.dtype) for a in ins]
        n = 3 * len(ins)
        self.scratch = [pltpu.SemaphoreType.DMA((n,)), pltpu.SemaphoreType.DMA((n,))]

    def _copies(self, in_refs, out_refs, sems, landing):
        x, y, c, chips = _mesh_place()
        me = 2 * x + y
        out = []
        for w in range(len(self.ins)):
            for j, chip in enumerate(chips):
                peer = 2 * chip[0] + chip[1]
                if self.kind == "gather":
                    src, dst, land = in_refs[w].at[c], out_refs[w].at[me, c], out_refs[w].at[peer, c]
                else:
                    src, dst, land = in_refs[w].at[peer], out_refs[w].at[j], out_refs[w].at[j]
                if landing:
                    src, dst = land, land
                out.append(_remote(src, dst, sems, 3 * w + j, (chip[0], chip[1], c)))
        return out

    def start(self, in_refs, out_refs, sems):
        for cp in self._copies(in_refs, out_refs, sems, False):
            cp.start()

    def finish(self, in_refs, out_refs, sems):
        for cp in self._copies(in_refs, out_refs, sems, True):
            cp.wait_recv()
        for cp in self._copies(in_refs, out_refs, sems, False):
            cp.wait_send()


def _carry(comm, n_in, n_out, first, last, refs):
    if comm is None:
        return refs[:n_in], refs[n_in:n_in + n_out], (lambda: None)
    a, b = len(comm.ins), len(comm.out_shape)
    ins, c_in = refs[:n_in], refs[n_in:n_in + a]
    outs, c_out = refs[n_in + a:n_in + a + n_out], refs[n_in + a + n_out:n_in + a + n_out + b]
    sems = refs[n_in + a + n_out + b:n_in + a + n_out + b + 2]
    pl.when(first)(lambda: comm.start(c_in, c_out, sems))
    return ins, outs, (lambda: pl.when(last)(lambda: comm.finish(c_in, c_out, sems)))


def _sbl_fwd(q, k, v, *, width, hd, name, tq=256, comm=None):
    (qa, qo), (ka, ko), (va, vo) = q, k, v
    S = qa.shape[0]
    tq = _tile(S, tq)
    tk = tq
    scale = hd ** -0.5
    n_g, n_q = width // LANES, S // tq

    def body(*refs):
        qi = pl.program_id(1)
        gi = pl.program_id(0)
        got = _carry(comm, 3, 2, (gi == 0) & (qi == 0), (gi == n_g - 1) & (qi == n_q - 1), refs)
        (q_ref, k_ref, v_ref), (o_ref, tot_ref) = got[0], got[1]
        masks = _lane_masks(hd, tq)
        qs = q_ref[...].astype(F32) * scale
        qm = [_keep(qs, m).astype(BF16) for m in masks]
        strict = lax.broadcasted_iota(jnp.int32, (tq, tk), 1) < lax.broadcasted_iota(jnp.int32, (tq, tk), 0)
        later = (lax.broadcasted_iota(jnp.int32, (tk, tk), 0) > lax.broadcasted_iota(jnp.int32, (tk, tk), 1)).astype(BF16)

        def tile(kb, carry, diag):
            ks = pl.multiple_of(kb * tk, tk)
            kv = k_ref[pl.ds(ks, tk), :].astype(BF16)
            vv = v_ref[pl.ds(ks, tk), :].astype(BF16)
            out = []
            for hh in range(len(masks)):
                acc, c_pos = carry[2 * hh], carry[2 * hh + 1]
                pos, ls = _softplus_parts(_dot(qm[hh], kv, _NT))
                if diag:
                    pos = jnp.where(strict, pos, 0.0)
                w = jnp.exp(ls - (_split_dot(pos, later) + c_pos))
                if diag:
                    w = jnp.where(strict, w, 0.0)
                out += [acc + _dot(w.astype(BF16), vv), c_pos + jnp.sum(pos, axis=1, keepdims=True)]
            return tuple(out)

        init = (jnp.zeros((tq, LANES), F32), jnp.zeros((tq, 1), F32)) * len(masks)
        carry = tile(qi, init, True)
        carry = lax.fori_loop(0, qi, lambda i, c: tile(qi - 1 - i, c, False), carry)
        o_ref[...] = _merge(carry[0::2], masks).astype(o_ref.dtype)
        tot_ref[...] = _merge([jnp.broadcast_to(-c, (tq, LANES)) for c in carry[1::2]], masks)
        got[2]()

    c_ins, c_in_specs, c_out_specs, c_out_shape, c_scratch = _comm_args(comm)
    return pl.pallas_call(
        body, name=name, grid=(n_g, n_q),
        in_specs=[_cols(tq, qo), _cols_all(S, ko), _cols_all(S, vo)] + c_in_specs,
        out_specs=[_cols(tq, 0), _cols(tq, 0)] + c_out_specs,
        out_shape=[jax.ShapeDtypeStruct((S, width), BF16), jax.ShapeDtypeStruct((S, width), F32)] + c_out_shape,
        scratch_shapes=c_scratch,
        compiler_params=_params(("arbitrary", "arbitrary")),
    )(qa, ka, va, *c_ins)


def _comm_args(comm):
    if comm is None:
        return [], [], [], [], []
    return comm.ins, [_ANY] * len(comm.ins), [_ANY] * len(comm.out_shape), comm.out_shape, comm.scratch


def _sbl_bwd(q, k, v, do, tot, *, width, hd, name, tq=256, comm=None):
    (qa, qo), (ka, ko), (va, vo) = q, k, v
    S = qa.shape[0]
    tq = _tile(S, tq)
    tk = tq
    scale = hd ** -0.5
    n_g, n_q = width // LANES, S // tq

    def body(*refs):
        qi = pl.program_id(1)
        gi = pl.program_id(0)
        got = _carry(comm, 5, 3, (gi == 0) & (qi == 0), (gi == n_g - 1) & (qi == n_q - 1), refs)
        (q_ref, k_ref, v_ref, do_ref, tot_ref), (dq_ref, dk_ref, dv_ref) = got[0], got[1]

        @pl.when(qi == 0)
        def _():
            dk_ref[...] = jnp.zeros_like(dk_ref)
            dv_ref[...] = jnp.zeros_like(dv_ref)

        masks = _lane_masks(hd, tq)
        qs = q_ref[...].astype(F32) * scale
        qm = [_keep(qs, m).astype(BF16) for m in masks]
        dov = [_keep(do_ref[...], m).astype(BF16) for m in masks]
        tots = [_row_value(tot_ref[...], m) for m in masks]
        r_q = lax.broadcasted_iota(jnp.int32, (tq, tk), 0)
        strict = lax.broadcasted_iota(jnp.int32, (tq, tk), 1) < r_q
        r_i = lax.broadcasted_iota(jnp.int32, (tk, tk), 0)
        c_i = lax.broadcasted_iota(jnp.int32, (tk, tk), 1)
        upto = (r_i <= c_i).astype(BF16)
        before = (r_i < c_i).astype(BF16)

        def tile(kb, carry, diag):
            ks = pl.multiple_of(kb * tk, tk)
            kv = k_ref[pl.ds(ks, tk), :].astype(BF16)
            vv = v_ref[pl.ds(ks, tk), :].astype(BF16)
            out = []
            dk_t, dv_t = None, None
            for hh in range(len(masks)):
                dq, c_pos, c_g = carry[3 * hh:3 * hh + 3]
                pos, ls = _softplus_parts(_dot(qm[hh], kv, _NT))
                if diag:
                    pos = jnp.where(strict, pos, 0.0)
                w = jnp.exp(ls + tots[hh] + (_split_dot(pos, upto) + c_pos))
                if diag:
                    w = jnp.where(strict, w, 0.0)
                g = _dot(dov[hh], vv, _NT) * w
                g_before = _split_dot(g, before) + c_g
                dz = g - jnp.exp(ls) * (g + g_before)
                if diag:
                    dz = jnp.where(strict, dz, 0.0)
                dzb = dz.astype(BF16)
                dk_h = _dot(dzb, qm[hh], _TN)
                dv_h = _dot(w.astype(BF16), dov[hh], _TN)
                dk_t = dk_h if dk_t is None else dk_t + dk_h
                dv_t = dv_h if dv_t is None else dv_t + dv_h
                out += [dq + _dot(dzb, kv), c_pos + jnp.sum(pos, axis=1, keepdims=True),
                        c_g + jnp.sum(g, axis=1, keepdims=True)]
            dk_ref[pl.ds(ks, tk), :] += dk_t
            dv_ref[pl.ds(ks, tk), :] += dv_t
            return tuple(out)

        zero = jnp.zeros((tq, 1), F32)
        init = (jnp.zeros((tq, LANES), F32), zero, zero) * len(masks)
        carry = lax.fori_loop(0, qi, lambda kb, c: tile(kb, c, False), init)
        carry = tile(qi, carry, True)
        dq_ref[...] = _merge(carry[0::3], masks) * scale
        got[2]()

    full = jax.ShapeDtypeStruct((S, width), F32)
    c_ins, c_in_specs, c_out_specs, c_out_shape, c_scratch = _comm_args(comm)
    return pl.pallas_call(
        body, name=name, grid=(n_g, n_q),
        in_specs=[_cols(tq, qo), _cols_all(S, ko), _cols_all(S, vo), _cols(tq, do[1]), _cols(tq, tot[1])] + c_in_specs,
        out_specs=[_cols(tq, 0), _cols_all(S, 0), _cols_all(S, 0)] + c_out_specs,
        out_shape=[full, full, full] + c_out_shape,
        scratch_shapes=c_scratch,
        compiler_params=_params(("arbitrary", "arbitrary")),
    )(qa, ka, va, do[0], tot[0], *c_ins)


def _sml_fwd(q, k, v, bias=None, *, width, hd, causal, name, tq=256, tk=256):
    (qa, qo), (ka, ko), (va, vo) = q, k, v
    S, Sk = qa.shape[0], ka.shape[0]
    tq, tk = _tile(S, tq), _tile(Sk, tk)
    if causal:
        assert tq == tk and S == Sk
    nk = Sk // tk
    hpg = LANES // hd
    scale = hd ** -0.5
    has_bias = bias is not None

    def body(*refs):
        q_ref, k_ref, v_ref = refs[:3]
        o_ref, lse_ref = refs[-2:]
        qi = pl.program_id(1)
        masks = _lane_masks(hd, tq)
        qs = q_ref[...].astype(F32) * scale
        qm = [_keep(qs, m).astype(BF16) for m in masks]
        allowed = lax.broadcasted_iota(jnp.int32, (tq, tk), 1) <= lax.broadcasted_iota(jnp.int32, (tq, tk), 0)

        def tile(kb, carry, diag):
            ks = pl.multiple_of(kb * tk, tk)
            kv = k_ref[pl.ds(ks, tk), :].astype(BF16)
            vv = v_ref[pl.ds(ks, tk), :].astype(BF16)
            out = []
            for hh in range(hpg):
                m, l, acc = carry[3 * hh:3 * hh + 3]
                z = _dot(qm[hh], kv, _NT)
                if has_bias:
                    z = z + refs[3][hh] - refs[4][hh, kb]
                if diag:
                    z = jnp.where(allowed, z, NEG_INF)
                m2 = jnp.maximum(m, jnp.max(z, axis=1, keepdims=True))
                p = jnp.exp(z - m2)
                alpha = jnp.exp(m - m2)
                out += [m2, alpha * l + jnp.sum(p, axis=1, keepdims=True), alpha * acc + _dot(p.astype(BF16), vv)]
            return tuple(out)

        init = (jnp.full((tq, 1), NEG_INF, F32), jnp.zeros((tq, 1), F32), jnp.zeros((tq, LANES), F32)) * hpg
        if causal:
            carry = lax.fori_loop(0, qi, lambda kb, c: tile(kb, c, False), init)
            carry = tile(qi, carry, True)
        else:
            carry = lax.fori_loop(0, nk, lambda kb, c: tile(kb, c, False), init)
        o_ref[...] = _merge([acc / l for l, acc in zip(carry[1::3], carry[2::3])], masks).astype(o_ref.dtype)
        lse_ref[...] = _merge([jnp.broadcast_to(m + jnp.log(l), (tq, LANES)) for m, l in zip(carry[0::3], carry[1::3])], masks)

    in_specs = [_cols(tq, qo), _cols_all(Sk, ko), _cols_all(Sk, vo)]
    ins = [qa, ka, va]
    if has_bias:
        in_specs += [pl.BlockSpec((hpg, tq, 1), lambda g, i: (g, i, 0)),
                     pl.BlockSpec((hpg, nk, 1, tk), lambda g, i: (g, 0, 0, 0))]
        ins += list(bias)
    return pl.pallas_call(
        body, name=name, grid=(width // LANES, S // tq),
        in_specs=in_specs, out_specs=[_cols(tq, 0), _cols(tq, 0)],
        out_shape=[jax.ShapeDtypeStruct((S, width), BF16), jax.ShapeDtypeStruct((S, width), F32)],
        compiler_params=_params(("parallel", "arbitrary")),
    )(*ins)


def _sml_bwd(q, k, v, o, lse, do, bias=None, *, width, hd, causal, name, tq=256, tk=256):
    (qa, qo), (ka, ko), (va, vo) = q, k, v
    S, Sk = qa.shape[0], ka.shape[0]
    tq, tk = _tile(S, tq), _tile(Sk, tk)
    nk = Sk // tk
    hpg = LANES // hd
    scale = hd ** -0.5
    has_bias = bias is not None
    n_in = 8 if has_bias else 6

    def body(*refs):
        q_ref, k_ref, v_ref, o_ref, lse_ref, do_ref = refs[:6]
        dq_ref, dk_ref, dv_ref = refs[n_in:n_in + 3]
        qi = pl.program_id(1)

        @pl.when(qi == 0)
        def _():
            dk_ref[...] = jnp.zeros_like(dk_ref)
            dv_ref[...] = jnp.zeros_like(dv_ref)
            if has_bias:
                refs[n_in + 4][...] = jnp.zeros_like(refs[n_in + 4])

        masks = _lane_masks(hd, tq)
        qs = q_ref[...].astype(F32) * scale
        qm = [_keep(qs, m).astype(BF16) for m in masks]
        do32 = do_ref[...]
        dov = [_keep(do32, m).astype(BF16) for m in masks]
        prod = do32 * o_ref[...].astype(F32)
        delta = [jnp.sum(_keep(prod, m), axis=1, keepdims=True) for m in masks]
        lses = [_row_value(lse_ref[...], m) for m in masks]
        allowed = lax.broadcasted_iota(jnp.int32, (tq, tk), 1) <= lax.broadcasted_iota(jnp.int32, (tq, tk), 0)

        def tile(kb, carry, diag):
            ks = pl.multiple_of(kb * tk, tk)
            kv = k_ref[pl.ds(ks, tk), :].astype(BF16)
            vv = v_ref[pl.ds(ks, tk), :].astype(BF16)
            out = []
            dk_t, dv_t = None, None
            for hh in range(hpg):
                dq, db_row = carry[2 * hh:2 * hh + 2]
                z = _dot(qm[hh], kv, _NT)
                if has_bias:
                    z = z + refs[6][hh] - refs[7][hh, kb]
                p = jnp.exp(z - lses[hh])
                if diag:
                    p = jnp.where(allowed, p, 0.0)
                dz = p * (_dot(dov[hh], vv, _NT) - delta[hh])
                dzb = dz.astype(BF16)
                dk_h = _dot(dzb, qm[hh], _TN)
                dv_h = _dot(p.astype(BF16), dov[hh], _TN)
                dk_t = dk_h if dk_t is None else dk_t + dk_h
                dv_t = dv_h if dv_t is None else dv_t + dv_h
                if has_bias:
                    db_row = db_row + jnp.sum(dz, axis=1, keepdims=True)
                    refs[n_in + 4][hh, kb] += jnp.sum(dz, axis=0, keepdims=True)
                out += [dq + _dot(dzb, kv), db_row]
            dk_ref[pl.ds(ks, tk), :] += dk_t
            dv_ref[pl.ds(ks, tk), :] += dv_t
            return tuple(out)

        init = (jnp.zeros((tq, LANES), F32), jnp.zeros((tq, 1), F32)) * hpg
        if causal:
            carry = lax.fori_loop(0, qi, lambda kb, c: tile(kb, c, False), init)
            carry = tile(qi, carry, True)
        else:
            carry = lax.fori_loop(0, nk, lambda kb, c: tile(kb, c, False), init)
        dq_ref[...] = _merge(carry[0::2], masks) * scale
        if has_bias:
            for hh in range(hpg):
                refs[n_in + 3][hh] = carry[2 * hh + 1]

    in_specs = [_cols(tq, qo), _cols_all(Sk, ko), _cols_all(Sk, vo), _cols(tq, o[1]), _cols(tq, lse[1]), _cols(tq, do[1])]
    ins = [qa, ka, va, o[0], lse[0], do[0]]
    out_specs = [_cols(tq, 0), _cols_all(Sk, 0), _cols_all(Sk, 0)]
    out_shape = [jax.ShapeDtypeStruct((S, width), F32), jax.ShapeDtypeStruct((Sk, width), F32),
                 jax.ShapeDtypeStruct((Sk, width), F32)]
    if has_bias:
        rspec = pl.BlockSpec((hpg, tq, 1), lambda g, i: (g, i, 0))
        cspec = pl.BlockSpec((hpg, nk, 1, tk), lambda g, i: (g, 0, 0, 0))
        in_specs += [rspec, cspec]
        ins += list(bias)
        out_specs += [rspec, cspec]
        n_heads = width // hd
        out_shape += [jax.ShapeDtypeStruct((n_heads, S, 1), F32), jax.ShapeDtypeStruct((n_heads, nk, 1, tk), F32)]
    return pl.pallas_call(
        body, name=name, grid=(width // LANES, S // tq),
        in_specs=in_specs, out_specs=out_specs, out_shape=out_shape,
        compiler_params=_params(("parallel", "arbitrary")),
    )(*ins)


def _head_sums(t, masks):
    sums = [jnp.sum(_keep(t, m), axis=1, keepdims=True) for m in masks]
    return _merge([jnp.broadcast_to(s, t.shape) for s in sums], masks) if len(masks) > 1 else sums[0]


def _hnorm_fwd(x, g_lanes, *, width, hd, name, tr=512):
    xa, xo = x
    R = xa.shape[0]
    tr = _tile(R, tr, align=16)
    n_blk = width // LANES

    def body(x_ref, g_ref, o_ref):
        masks = _lane_masks(hd, tr)
        for j in range(n_blk):
            sl = slice(j * LANES, (j + 1) * LANES)
            xv = x_ref[:, sl].astype(F32)
            r = lax.rsqrt(_head_sums(xv * xv, masks) * (1.0 / hd) + EPS)
            o_ref[:, sl] = (xv * r * g_ref[...]).astype(o_ref.dtype)

    assert (xo * LANES) % width == 0
    return pl.pallas_call(
        body, name=name, grid=(R // tr,),
        in_specs=[pl.BlockSpec((tr, width), lambda i: (i, xo * LANES // width)), pl.BlockSpec((1, LANES), lambda i: (0, 0))],
        out_specs=pl.BlockSpec((tr, width), lambda i: (i, 0)),
        out_shape=jax.ShapeDtypeStruct((R, width), BF16),
        compiler_params=_params(("parallel",)),
    )(xa, g_lanes)


def _hnorm_bwd(x, g_lanes, dy, *, width, hd, name, tr=512):
    xa, xo = x
    R = xa.shape[0]
    tr = _tile(R, tr, align=16)
    n_blk = width // LANES

    def body(x_ref, g_ref, dy_ref, dx_ref, dg_ref):
        masks = _lane_masks(hd, tr)
        dg = jnp.zeros((1, LANES), F32)
        for j in range(n_blk):
            sl = slice(j * LANES, (j + 1) * LANES)
            xv = x_ref[:, sl].astype(F32)
            dyv = dy_ref[:, sl].astype(F32)
            r = lax.rsqrt(_head_sums(xv * xv, masks) * (1.0 / hd) + EPS)
            xh = xv * r
            dyg = dyv * g_ref[...]
            c = _head_sums(dyg * xh, masks) * (1.0 / hd)
            dx_ref[:, sl] = (r * (dyg - xh * c)).astype(dx_ref.dtype)
            dg = dg + jnp.sum(dyv * xh, axis=0, keepdims=True)
        if hd * 2 == LANES:
            dg8 = jnp.broadcast_to(dg, (8, LANES))
            dg = (dg8 + pltpu.roll(dg8, shift=hd, axis=1))[0:1]
        else:
            assert hd == LANES

        @pl.when(pl.program_id(0) == 0)
        def _():
            dg_ref[...] = jnp.zeros_like(dg_ref)

        dg_ref[...] += dg

    assert (xo * LANES) % width == 0
    return pl.pallas_call(
        body, name=name, grid=(R // tr,),
        in_specs=[pl.BlockSpec((tr, width), lambda i: (i, xo * LANES // width)), pl.BlockSpec((1, LANES), lambda i: (0, 0)),
                  pl.BlockSpec((tr, width), lambda i: (i, 0))],
        out_specs=[pl.BlockSpec((tr, width), lambda i: (i, 0)), pl.BlockSpec((1, LANES), lambda i: (0, 0))],
        out_shape=[jax.ShapeDtypeStruct((R, width), BF16), jax.ShapeDtypeStruct((1, LANES), F32)],
        compiler_params=_params(("arbitrary",)),
    )(xa, g_lanes, dy)


def _split3_dot(x, tri):
    a = x.astype(BF16)
    r = x - a.astype(F32)
    b = r.astype(BF16)
    c = (r - b.astype(F32)).astype(BF16)
    return _dot(a, tri) + _dot(b, tri) + _dot(c, tri)


def _forget_fwd(logit_t, b_col, *, name, blk=512):
    H, S = logit_t.shape
    blk = _tile(S, blk)

    def body(l_ref, b_ref, f_ref):
        r_i = lax.broadcasted_iota(jnp.int32, (blk, blk), 0)
        c_i = lax.broadcasted_iota(jnp.int32, (blk, blk), 1)
        upto = (r_i <= c_i).astype(BF16)
        carry = jnp.zeros((H, 1), F32)
        for j in range(S // blk):
            u = l_ref[:, j * blk:(j + 1) * blk] + b_ref[...]
            lf, _ = _log_sigmoid_pair(u)
            f_ref[:, j * blk:(j + 1) * blk] = _split3_dot(lf, upto) + carry
            carry = carry + jnp.sum(lf, axis=1, keepdims=True)

    return pl.pallas_call(
        body, name=name,
        out_shape=jax.ShapeDtypeStruct((H, S), F32),
        compiler_params=pltpu.CompilerParams(vmem_limit_bytes=VMEM_LIMIT),
    )(logit_t, b_col)


def _forget_bwd(logit_t, b_col, d_f, *, name, blk=512):
    H, S = logit_t.shape
    blk = _tile(S, blk)

    def body(l_ref, b_ref, df_ref, dl_ref, db_ref):
        r_i = lax.broadcasted_iota(jnp.int32, (blk, blk), 0)
        c_i = lax.broadcasted_iota(jnp.int32, (blk, blk), 1)
        fromon = (r_i >= c_i).astype(BF16)
        carry = jnp.zeros((H, 1), F32)
        db = jnp.zeros((H, 1), F32)
        for j in reversed(range(S // blk)):
            sl = slice(j * blk, (j + 1) * blk)
            dfv = df_ref[:, sl]
            d_lf = _split3_dot(dfv, fromon) + carry
            carry = carry + jnp.sum(dfv, axis=1, keepdims=True)
            u = l_ref[:, sl] + b_ref[...]
            _, lsn = _log_sigmoid_pair(u)
            dl = d_lf * jnp.exp(lsn)
            dl_ref[:, sl] = dl
            db = db + jnp.sum(dl, axis=1, keepdims=True)
        db_ref[...] = db

    return pl.pallas_call(
        body, name=name,
        out_shape=[jax.ShapeDtypeStruct((H, S), F32), jax.ShapeDtypeStruct((H, 1), F32)],
        compiler_params=pltpu.CompilerParams(vmem_limit_bytes=VMEM_LIMIT),
    )(logit_t, b_col, d_f)


def _sigmoid(t):
    return 1.0 / (1.0 + jnp.exp(-t))


def _gate_fwd(o3, w3, proj, D, *, name, tm=256):
    S = proj.shape[0]
    tm = _tile(S, tm)

    def body(o0, o1, o2, w0, w1, w2, g0, g1, g2, out_ref):
        acc = None
        for o_ref, w_ref, g_ref in ((o0, w0, g0), (o1, w1, g1), (o2, w2, g2)):
            t = _sigmoid(g_ref[...]) * _dot(o_ref[...], w_ref[...])
            acc = t if acc is None else acc + t
        out_ref[...] = acc.astype(out_ref.dtype)

    ospec = lambda d: pl.BlockSpec((tm, d), lambda i: (i, 0))
    wspec = lambda w: pl.BlockSpec(w.shape, lambda i: (0, 0))
    gspec = lambda j: pl.BlockSpec((tm, D), lambda i: (i, j))
    return pl.pallas_call(
        body, name=name, grid=(S // tm,),
        in_specs=[ospec(o.shape[1]) for o in o3] + [wspec(w) for w in w3] + [gspec(j) for j in range(3)],
        out_specs=pl.BlockSpec((tm, D), lambda i: (i, 0)),
        out_shape=jax.ShapeDtypeStruct((S, D), BF16),
        compiler_params=_params(("parallel",)),
    )(*o3, *w3, proj, proj, proj)


def _gate_bwd(o3, w3, proj, dmerged, D, *, name, tm=256):
    S = proj.shape[0]
    tm = _tile(S, tm)

    def body(o0, o1, o2, w0, w1, w2, g0, g1, g2, dm_ref, dg_ref, db0, db1, db2, do0, do1, do2):
        dm = dm_ref[...]
        for j, (o_ref, w_ref, g_ref, db_ref, do_ref) in enumerate(
                ((o0, w0, g0, db0, do0), (o1, w1, g1, db1, do1), (o2, w2, g2, db2, do2))):
            s = _sigmoid(g_ref[...])
            br = _dot(o_ref[...], w_ref[...])
            dg_ref[:, j * D:(j + 1) * D] = (dm * br * s * (1.0 - s)).astype(dg_ref.dtype)
            dbr = (dm * s).astype(BF16)
            db_ref[...] = dbr
            do_ref[...] = _dot(dbr, w_ref[...], _NT)

    ospec = lambda d: pl.BlockSpec((tm, d), lambda i: (i, 0))
    wspec = lambda w: pl.BlockSpec(w.shape, lambda i: (0, 0))
    gspec = lambda j: pl.BlockSpec((tm, D), lambda i: (i, j))
    dspec = pl.BlockSpec((tm, D), lambda i: (i, 0))
    return pl.pallas_call(
        body, name=name, grid=(S // tm,),
        in_specs=[ospec(o.shape[1]) for o in o3] + [wspec(w) for w in w3] + [gspec(j) for j in range(3)] + [dspec],
        out_specs=[pl.BlockSpec((tm, 3 * D), lambda i: (i, 0))] + [dspec] * 3 + [ospec(o.shape[1]) for o in o3],
        out_shape=[jax.ShapeDtypeStruct((S, 3 * D), BF16)] + [jax.ShapeDtypeStruct((S, D), BF16)] * 3
        + [jax.ShapeDtypeStruct((S, o.shape[1]), F32) for o in o3],
        compiler_params=_params(("parallel",)),
    )(*o3, *w3, proj, proj, proj, dmerged)


def _loss_sum(dy, D, *, name):
    R, C = dy.shape
    tr = _row_tile(R, C, 2)

    def body(dy_ref, out_ref):
        @pl.when(pl.program_id(0) == 0)
        def _():
            out_ref[...] = jnp.zeros_like(out_ref)

        v = dy_ref[...]
        out_ref[...] += (0.5 * D) * jnp.sum(v * v)

    return pl.pallas_call(
        body, name=name, grid=(R // tr,),
        in_specs=[pl.BlockSpec((tr, C), lambda i: (i, 0))],
        out_specs=pl.BlockSpec((8, 128), lambda i: (0, 0)),
        out_shape=jax.ShapeDtypeStruct((8, 128), F32),
        compiler_params=_params(("arbitrary",)),
    )(dy)[0, 0]


def _adamw_math(w, g, m, v):
    m2 = ADAM_B1 * m + (1.0 - ADAM_B1) * g
    v2 = ADAM_B2 * v + (1.0 - ADAM_B2) * (g * g)
    m_hat = m2 / (1.0 - ADAM_B1 ** ADAM_STEP)
    v_hat = v2 / (1.0 - ADAM_B2 ** ADAM_STEP)
    delta = -ADAM_LR * (m_hat / (jnp.sqrt(v_hat) + ADAM_EPS) + ADAM_WD * w)
    return delta, m2, v2


def _adamw(w, g, m, v, *, name):
    return _ew(_adamw_math, [w, g, m, v], (F32, F32, F32), name=name)


def _adamw_small(w, parts, m, v, *, name):
    n = parts.shape[0]

    def body(w_ref, p_ref, m_ref, v_ref, g_ref, d_ref, m2_ref, v2_ref):
        g = p_ref[0]
        for i in range(1, n):
            g = g + p_ref[i]
        g_ref[...] = g
        d_ref[...], m2_ref[...], v2_ref[...] = _adamw_math(w_ref[...], g, m_ref[...], v_ref[...])

    shp = jax.ShapeDtypeStruct(w.shape, F32)
    return pl.pallas_call(body, name=name, out_shape=[shp] * 4)(w, parts, m, v)


_ANY = pl.BlockSpec(memory_space=pl.ANY)


def _mesh_place():
    x, y, c = lax.axis_index("x"), lax.axis_index("y"), lax.axis_index("c")
    chips = [(1 - x, y), (x, 1 - y), (1 - x, 1 - y)]
    return x, y, c, chips


def _remote(src, dst, sems, i, to):
    send_sems, recv_sems = sems
    return pltpu.make_async_remote_copy(src_ref=src, dst_ref=dst, send_sem=send_sems.at[i], recv_sem=recv_sems.at[i],
                                        device_id=to, device_id_type=MESH_ID)


def _gather_weights(shards, *, name):
    n = len(shards)

    def body(*refs):
        ins, outs = refs[:n], refs[n:2 * n]
        sems = refs[2 * n:2 * n + 2]
        x, y, c, chips = _mesh_place()
        me = 2 * x + y
        sibling = (x, y, 1 - c)
        sent = []
        for w in range(n):
            for j, chip in enumerate(chips):
                cp = _remote(ins[w].at[c], outs[w].at[me, c], sems, 6 * w + j, (chip[0], chip[1], c))
                cp.start()
                sent.append(cp)
        for w in range(n):
            for j, chip in enumerate(chips):
                got = outs[w].at[2 * chip[0] + chip[1], c]
                _remote(got, got, sems, 6 * w + j, sibling).wait_recv()
                cp = _remote(got, got, sems, 6 * w + 3 + j, sibling)
                cp.start()
                sent.append(cp)
        for w in range(n):
            for j, chip in enumerate(chips):
                got = outs[w].at[2 * chip[0] + chip[1], 1 - c]
                _remote(got, got, sems, 6 * w + 3 + j, sibling).wait_recv()
        for cp in sent:
            cp.wait_send()

    outs = pl.pallas_call(
        body, name=name,
        in_specs=[_ANY] * n, out_specs=[_ANY] * n,
        out_shape=[jax.ShapeDtypeStruct((N_CHIPS,) + s.shape, s.dtype) for s in shards],
        scratch_shapes=[pltpu.SemaphoreType.DMA((6 * n,)), pltpu.SemaphoreType.DMA((6 * n,))],
    )(*shards)
    me = 2 * lax.axis_index("x") + lax.axis_index("y")
    return [lax.dynamic_update_index_in_dim(o, s, me, 0) for o, s in zip(outs, shards)]


def _forward_halves(gathered, *, name):
    n = len(gathered)

    def body(*refs):
        ins, outs = refs[:n], refs[n:2 * n]
        sems = refs[2 * n:2 * n + 2]
        x, y, c, chips = _mesh_place()
        sibling = (x, y, 1 - c)
        sent = []
        for w in range(n):
            for j, chip in enumerate(chips):
                peer = 2 * chip[0] + chip[1]
                cp = _remote(ins[w].at[peer, c], outs[w].at[peer, c], sems, 3 * w + j, sibling)
                cp.start()
                sent.append(cp)
        for w in range(n):
            for j, chip in enumerate(chips):
                land = outs[w].at[2 * chip[0] + chip[1], 1 - c]
                _remote(land, land, sems, 3 * w + j, sibling).wait_recv()
        for cp in sent:
            cp.wait_send()

    return pl.pallas_call(
        body, name=name,
        in_specs=[_ANY] * n, out_specs=[_ANY] * n,
        out_shape=[jax.ShapeDtypeStruct(g.shape, g.dtype) for g in gathered],
        input_output_aliases={w: w for w in range(n)},
        scratch_shapes=[pltpu.SemaphoreType.DMA((3 * n,)), pltpu.SemaphoreType.DMA((3 * n,))],
    )(*gathered)


def _exchange_siblings(grads, small=None, *, name):
    n = len(grads)
    has_small = small is not None

    def body(*refs):
        ins = refs[:n]
        n_in = n + has_small
        got = refs[n_in:n_in + n]
        sems = refs[2 * n_in:2 * n_in + 2]
        x, y, c, chips = _mesh_place()
        sibling = (x, y, 1 - c)
        me = 4 * x + 2 * y + c
        sent = []
        for w in range(n):
            for s in range(N_CHIPS):
                cp = _remote(ins[w].at[s, 1 - c], got[w].at[s], sems, N_CHIPS * w + s, sibling)
                cp.start()
                sent.append(cp)
        if has_small:
            small_ref, small_out = refs[n], refs[n_in + n]
            peers = [(x, y, 1 - c)] + [(ch[0], ch[1], cc) for ch in chips for cc in (c, 1 - c)]
            for j, peer in enumerate(peers):
                cp = _remote(small_ref, small_out.at[me], sems, N_CHIPS * n + j, peer)
                cp.start()
                sent.append(cp)
        for w in range(n):
            for s in range(N_CHIPS):
                _remote(got[w].at[s], got[w].at[s], sems, N_CHIPS * w + s, sibling).wait_recv()
        if has_small:
            for j, peer in enumerate(peers):
                frm = small_out.at[4 * peer[0] + 2 * peer[1] + peer[2]]
                _remote(frm, frm, sems, N_CHIPS * n + j, peer).wait_recv()
        for cp in sent:
            cp.wait_send()

    halves = [jax.ShapeDtypeStruct((N_CHIPS,) + g.shape[2:], g.dtype) for g in grads]
    n_sem = N_CHIPS * n + 7 * has_small
    outs = pl.pallas_call(
        body, name=name,
        in_specs=[_ANY] * (n + has_small), out_specs=[_ANY] * (n + has_small),
        out_shape=halves + ([jax.ShapeDtypeStruct((8,) + small.shape, small.dtype)] if has_small else []),
        scratch_shapes=[pltpu.SemaphoreType.DMA((n_sem,)), pltpu.SemaphoreType.DMA((n_sem,))],
    )(*grads, *([small] if has_small else []))
    c = lax.axis_index("c")
    own = [lax.dynamic_index_in_dim(g, c, 1, keepdims=False) for g in grads]
    if not has_small:
        return own, outs[:n], None
    me = 4 * lax.axis_index("x") + 2 * lax.axis_index("y") + c
    return own, outs[:n], lax.dynamic_update_index_in_dim(outs[n], small, me, 0)


def _exchange_chips(parts, *, name):
    n = len(parts)

    def body(*refs):
        ins, got = refs[:n], refs[n:2 * n]
        sems = refs[2 * n:2 * n + 2]
        x, y, c, chips = _mesh_place()
        sent = []
        for w in range(n):
            for j, chip in enumerate(chips):
                cp = _remote(ins[w].at[2 * chip[0] + chip[1]], got[w].at[j], sems, 3 * w + j, (chip[0], chip[1], c))
                cp.start()
                sent.append(cp)
        for w in range(n):
            for j in range(3):
                _remote(got[w].at[j], got[w].at[j], sems, 3 * w + j, (x, y, c)).wait_recv()
        for cp in sent:
            cp.wait_send()

    got = pl.pallas_call(
        body, name=name,
        in_specs=[_ANY] * n, out_specs=[_ANY] * n,
        out_shape=[jax.ShapeDtypeStruct((3,) + p.shape[1:], p.dtype) for p in parts],
        scratch_shapes=[pltpu.SemaphoreType.DMA((3 * n,)), pltpu.SemaphoreType.DMA((3 * n,))],
    )(*parts)
    me = 2 * lax.axis_index("x") + lax.axis_index("y")
    return [lax.dynamic_index_in_dim(p, me, 0, keepdims=False) for p in parts], got


def _share_halves(halves):
    n = len(halves)

    def body(*refs):
        ins, outs = refs[:n], refs[n:2 * n]
        sems = refs[2 * n:2 * n + 2]
        x, y, c, _ = _mesh_place()
        sibling = (x, y, 1 - c)
        sent = [_remote(ins[w], outs[w].at[c], sems, w, sibling) for w in range(n)]
        for cp in sent:
            cp.start()
        for w in range(n):
            _remote(outs[w].at[1 - c], outs[w].at[1 - c], sems, w, sibling).wait_recv()
        for cp in sent:
            cp.wait_send()

    outs = pl.pallas_call(
        body, name="share_halves",
        in_specs=[_ANY] * n, out_specs=[_ANY] * n,
        out_shape=[jax.ShapeDtypeStruct((2,) + h.shape, h.dtype) for h in halves],
        scratch_shapes=[pltpu.SemaphoreType.DMA((n,)), pltpu.SemaphoreType.DMA((n,))],
    )(*halves)
    c = lax.axis_index("c")
    return [lax.dynamic_update_index_in_dim(o, h, c, 0) for o, h in zip(outs, halves)]


def _to_heads(t, n, d):
    s = t.shape[0]
    return t.reshape(s, n, d).transpose(1, 0, 2)


def _from_heads(t):
    h, s, d = t.shape
    return t.transpose(1, 0, 2).reshape(s, h * d)


def _head_norm_fwd(t, g, *, name):
    h, s, d = t.shape
    return _rmsnorm_fwd(t.reshape(h * s, d), g, BF16, name=name).reshape(h, s, d)


def _head_norm_bwd(t, g, dy, *, name):
    h, s, d = t.shape
    dx, dg = _rmsnorm_bwd(t.reshape(h * s, d), g, dy.reshape(h * s, d), name=name)
    return dx.reshape(h, s, d), dg


EARLY = ("w_ff_down", "w_ff_up", "w_out", "w_branch_sb", "w_branch_fox", "w_branch_mem")


def _local_step(x, mem, target, small, W, gather_rest=None, reduce_early=None):
    S, D = x.shape
    o_gate, o_qkv, o_mq, o_f = 0, 3 * D, 3 * D + 2 * 3 * D_SB, 3 * D + 2 * 3 * D_SB + D_MEM
    tq = 256

    h = _rmsnorm_fwd(x, small["g_mix_norm"], BF16, name="mix_norm")
    proj = _mm(h, W["w_in"], name="in_proj", tn=768)
    blk = lambda j: (proj, (o_qkv + j * D_SB) // LANES)
    sb_q, sb_k, sb_v, fx_q, fx_k, fx_v = [blk(j) for j in range(6)]
    m_q = (proj, o_mq // LANES)
    f_logit_t = proj[:, o_f:o_f + FOX_HEADS].T
    b_col = small["b_forget"].reshape(FOX_HEADS, 1)
    lanes = lambda g: jnp.tile(g, (1, LANES // g.shape[1]))
    g_fq, g_fk, g_mq, g_mk = [lanes(small[k]) for k in ("g_fox_q", "g_fox_k", "g_mem_q", "g_mem_k")]

    if gather_rest is None:
        o_sb, sb_tot = _sbl_fwd(sb_q, sb_k, sb_v, width=D_SB, hd=HD, name="sb_fwd", tq=tq)
    else:
        comm, finish = gather_rest
        o_sb, sb_tot, *landed = _sbl_fwd(sb_q, sb_k, sb_v, width=D_SB, hd=HD, name="sb_fwd", tq=tq, comm=comm)
        W = {**W, **finish(landed)}

    fq = _hnorm_fwd(fx_q, g_fq, width=D_FOX, hd=HD, name="fox_q_norm")
    fk = _hnorm_fwd(fx_k, g_fk, width=D_FOX, hd=HD, name="fox_k_norm")
    f_cum = _forget_fwd(f_logit_t, b_col, name="forget_fwd")
    tkf = _tile(S, tq)
    f_bias = (f_cum.reshape(FOX_HEADS, S, 1), f_cum.reshape(FOX_HEADS, S // tkf, 1, tkf))
    o_fox, fox_lse = _sml_fwd((fq, 0), (fk, 0), fx_v, f_bias, width=D_FOX, hd=HD, causal=True, name="fox_fwd",
                              tq=tq, tk=tq)

    mh = _rmsnorm_fwd(mem, small["g_mem_norm"], BF16, name="mem_norm")
    mkv = _mm(mh, W["w_mem_kv"], name="mem_kv")
    mv = (mkv, D_MEM // LANES)
    mq = _hnorm_fwd(m_q, g_mq, width=D_MEM, hd=MEM_HD, name="mem_q_norm")
    mk = _hnorm_fwd((mkv, 0), g_mk, width=D_MEM, hd=MEM_HD, name="mem_k_norm")
    o_mem, mem_lse = _sml_fwd((mq, 0), (mk, 0), mv, width=D_MEM, hd=MEM_HD, causal=False, name="mem_fwd", tq=tq, tk=256)

    o3 = [o_sb, o_fox, o_mem]
    w3 = [W["w_branch_sb"], W["w_branch_fox"], W["w_branch_mem"]]
    merged = _gate_fwd(o3, w3, proj, D, name="gate_fwd")
    x1 = _mm(merged, W["w_out"], name="out_proj", extras=(x,), epilogue=lambda acc, res: (res + acc,))
    h2 = _rmsnorm_fwd(x1, small["g_mlp_norm"], BF16, name="mlp_norm")

    def relu2(acc):
        u = jnp.maximum(acc, 0.0)
        return u, u * u

    u, a = _mm(h2, W["w_ff_up"], name="ff_up", out_dtypes=(BF16, BF16), epilogue=relu2)
    dy = _mm(a, W["w_ff_down"], name="ff_down", extras=(x1, target),
             epilogue=lambda acc, res, tgt: ((res + acc - tgt) * (1.0 / D),))
    loss = _loss_sum(dy, D, name="loss")

    G = {}
    du = _mm(dy, W["w_ff_down"], name="d_ff_act", tb=True, out_dtypes=(BF16,), extras=(u,),
             epilogue=lambda acc, uu: (acc * (2.0 * uu.astype(F32)),))
    G["w_ff_down"] = _mm(a, dy, name="d_w_ff_down", ta=True, out_dtypes=(BF16,))
    G["w_ff_up"] = _mm(h2, du, name="d_w_ff_up", ta=True, out_dtypes=(BF16,))
    dh2 = _mm(du, W["w_ff_up"], name="d_mlp_in", tb=True)
    dx1, dg_mlp = _rmsnorm_bwd(x1, small["g_mlp_norm"], dh2, add=dy, name="d_mlp_norm")
    dmerged = _mm(dx1, W["w_out"], name="d_merged", tb=True)
    G["w_out"] = _mm(merged, dx1, name="d_w_out", ta=True, out_dtypes=(BF16,))
    dgate, db0, db1, db2, do_sb, do_fox, do_mem = _gate_bwd(o3, w3, proj, dmerged, D, name="gate_bwd")
    for nm, o, db in zip(("w_branch_sb", "w_branch_fox", "w_branch_mem"), o3, (db0, db1, db2)):
        G[nm] = _mm(o, db, name="d_" + nm, ta=True, out_dtypes=(BF16,))

    early = None
    if reduce_early is None:
        dsb = _sbl_bwd(sb_q, sb_k, sb_v, (do_sb, 0), (sb_tot, 0), width=D_SB, hd=HD, name="sb_bwd", tq=tq)
    else:
        comm, finish = reduce_early({k: G.pop(k) for k in EARLY})
        *dsb, = _sbl_bwd(sb_q, sb_k, sb_v, (do_sb, 0), (sb_tot, 0), width=D_SB, hd=HD, name="sb_bwd", tq=tq, comm=comm)
        dsb, landed = dsb[:3], dsb[3:]
        early = finish(landed)
    dfq, dfk, dfv, df_row, df_col = _sml_bwd((fq, 0), (fk, 0), fx_v, (o_fox, 0), (fox_lse, 0), (do_fox, 0), f_bias,
                                             width=D_FOX, hd=HD, causal=True, name="fox_bwd", tq=tq, tk=tq)
    dfx_q, dg_fox_q = _hnorm_bwd(fx_q, g_fq, dfq, width=D_FOX, hd=HD, name="d_fox_q_norm")
    dfx_k, dg_fox_k = _hnorm_bwd(fx_k, g_fk, dfk, width=D_FOX, hd=HD, name="d_fox_k_norm")
    d_fcum = df_row.reshape(FOX_HEADS, S) - df_col.reshape(FOX_HEADS, S)
    d_flogit_t, db_forget = _forget_bwd(f_logit_t, b_col, d_fcum, name="forget_bwd")

    dmq_n, dmk_n, dmv = _sml_bwd((mq, 0), (mk, 0), mv, (o_mem, 0), (mem_lse, 0), (do_mem, 0), width=D_MEM, hd=MEM_HD,
                                 causal=False, name="mem_bwd", tq=tq, tk=256)
    dm_q, dg_mem_q = _hnorm_bwd(m_q, g_mq, dmq_n, width=D_MEM, hd=MEM_HD, name="d_mem_q_norm")
    dmk_raw, dg_mem_k = _hnorm_bwd((mkv, 0), g_mk, dmk_n, width=D_MEM, hd=MEM_HD, name="d_mem_k_norm")
    dmkv = jnp.concatenate([dmk_raw, dmv.astype(BF16)], axis=1)
    G["w_mem_kv"] = _mm(mh, dmkv, name="d_w_mem_kv", ta=True, out_dtypes=(BF16,))
    dmh = _mm(dmkv, W["w_mem_kv"], name="d_mem_h", tb=True)
    _, dg_mem = _rmsnorm_bwd(mem, small["g_mem_norm"], dmh, name="d_mem_norm")
    dg_fox_q, dg_fox_k = dg_fox_q[:, :HD], dg_fox_k[:, :HD]

    dproj = jnp.concatenate(
        [dgate] + [t.astype(BF16) for t in (*dsb, dfx_q, dfx_k, dfv, dm_q)]
        + [d_flogit_t.T.astype(BF16), jnp.zeros((S, F_PAD - FOX_HEADS), BF16)], axis=1)
    G["w_in"] = _mm(h, dproj, name="d_w_in", ta=True, out_dtypes=(BF16,), tn=768)
    dh = _mm(dproj, W["w_in"], name="d_mix_in", tb=True, tk=768)
    grad_x, dg_mix = _rmsnorm_bwd(x, small["g_mix_norm"], dh, add=dx1, name="d_mix_norm")

    small_grads = dict(g_mix_norm=dg_mix, g_mem_norm=dg_mem, b_forget=db_forget.reshape(1, FOX_HEADS),
                       g_fox_q=dg_fox_q, g_fox_k=dg_fox_k, g_mem_q=dg_mem_q, g_mem_k=dg_mem_k, g_mlp_norm=dg_mlp)
    return loss, grad_x, G, small_grads, early


BIG = ("w_in", "w_mem_kv", "w_branch_sb", "w_branch_fox", "w_branch_mem", "w_out", "w_ff_up", "w_ff_down")
COLUMN_SHARDED = ("w_in", "w_branch_sb", "w_branch_fox", "w_branch_mem", "w_ff_up")
SMALL = ("g_mix_norm", "g_mem_norm", "b_forget", "g_fox_q", "g_fox_k", "g_mem_q", "g_mem_k", "g_mlp_norm")
ORDER = ("g_mix_norm", "g_mem_norm", "w_in", "b_forget", "g_fox_q", "g_fox_k", "g_mem_q", "g_mem_k", "w_mem_kv",
         "w_branch_sb", "w_branch_fox", "w_branch_mem", "w_out", "g_mlp_norm", "w_ff_up", "w_ff_down")


def _unshard(name, gathered):
    n, _, rh, c = gathered.shape
    t = gathered.reshape(n, 2 * rh, c)
    if name in COLUMN_SHARDED:
        return t.transpose(1, 0, 2).reshape(2 * rh, n * c)
    return t.reshape(n * 2 * rh, c)


def _reshard(name, full):
    if name in COLUMN_SHARDED:
        r, c = full.shape
        t = full.reshape(r, N_CHIPS, c // N_CHIPS).transpose(1, 0, 2)
    else:
        r, c = full.shape[0] // N_CHIPS, full.shape[1]
        t = full.reshape(N_CHIPS, r, c)
    return t.reshape(N_CHIPS, 2, t.shape[1] // 2, t.shape[2])


def _pad_in_proj(w_in, D):
    n_qkv = 6 * D_SB
    o_mq = n_qkv + FOX_HEADS
    o_gate = o_mq + D_MEM
    return jnp.concatenate([w_in[:, o_gate:], w_in[:, :n_qkv], w_in[:, o_mq:o_gate], w_in[:, n_qkv:o_mq],
                            jnp.zeros((w_in.shape[0], F_PAD - FOX_HEADS), w_in.dtype)], axis=1)


def _unpad_in_proj(g, D):
    n_qkv = 6 * D_SB
    o_qkv, o_mq, o_f = 3 * D, 3 * D + n_qkv, 3 * D + n_qkv + D_MEM
    return jnp.concatenate([g[:, o_qkv:o_mq], g[:, o_f:o_f + FOX_HEADS], g[:, o_mq:o_f], g[:, :o_qkv]], axis=1)


def _pack_small(vals):
    width = max(vals[k].shape[1] for k in SMALL)
    return jnp.concatenate([jnp.pad(vals[k].astype(F32), ((0, 0), (0, width - vals[k].shape[1]))) for k in SMALL], axis=0)


def _unpack_small(packed, like):
    return {k: packed[i:i + 1, :like[k].shape[1]] for i, k in enumerate(SMALL)}


def kernel(x, mem, g_mix_norm, g_mem_norm, w_in, b_forget, g_fox_q, g_fox_k, g_mem_q, g_mem_k, w_mem_kv, w_branch_sb, w_branch_fox, w_branch_mem, w_out, g_mlp_norm, w_ff_up, w_ff_down, loss_target, m_g_mix_norm, m_g_mem_norm, m_w_in, m_b_forget, m_g_fox_q, m_g_fox_k, m_g_mem_q, m_g_mem_k, m_w_mem_kv, m_w_branch_sb, m_w_branch_fox, m_w_branch_mem, m_w_out, m_g_mlp_norm, m_w_ff_up, m_w_ff_down, v_g_mix_norm, v_g_mem_norm, v_w_in, v_b_forget, v_g_fox_q, v_g_fox_k, v_g_mem_q, v_g_mem_k, v_w_mem_kv, v_w_branch_sb, v_w_branch_fox, v_w_branch_mem, v_w_out, v_g_mlp_norm, v_w_ff_up, v_w_ff_down):
    given = dict(locals())
    D = x.shape[-1]
    weights = {k: given[k] for k in ORDER}
    moms = {k: given["m_" + k] for k in ORDER}
    vars_ = {k: given["v_" + k] for k in ORDER}

    me_chip = 2 * lax.axis_index("x") + lax.axis_index("y")

    shards = {}
    for k in BIG:
        w = weights[k][0].astype(BF16)
        shards[k] = w.reshape(2, w.shape[0] // 2, w.shape[1])
    W = {"w_in": _pad_in_proj(_unshard("w_in", _gather_weights([shards["w_in"]], name="gather_w_in")[0]), D)}
    rest = [k for k in BIG if k != "w_in"]

    def finish_weights(landed):
        full = _forward_halves(landed, name="forward_halves")
        full = [lax.dynamic_update_index_in_dim(o, shards[k], me_chip, 0) for k, o in zip(rest, full)]
        return {k: _unshard(k, g) for k, g in zip(rest, full)}

    def sum_pairs(names, own, sib):
        add2 = lambda p, q: (p.astype(F32) + q.astype(F32),)
        out = []
        for k, p, q in zip(names, own, sib):
            n, r, c = p.shape
            out.append(_ew(add2, [p.reshape(n * r, c), q.reshape(n * r, c)], (BF16,), name="sum_pair_" + k).reshape(n, r, c))
        return out

    def sum_chips(names, parts, got):
        add4 = lambda p, q0, q1, q2: (((p.astype(F32) + q0.astype(F32)) + q1.astype(F32)) + q2.astype(F32),)
        own = [lax.dynamic_index_in_dim(p, me_chip, 0, keepdims=False) for p in parts]
        return {k: _ew(add4, [p, q[0], q[1], q[2]], (F32,), name="sum_chips_" + k) for k, p, q in zip(names, own, got)}

    def reduce_early(grads):
        names = list(grads)
        own, sib, _ = _exchange_siblings([_reshard(k, grads[k]) for k in names], name="exchange_siblings_early")
        parts = sum_pairs(names, own, sib)
        return _ChipExchange("scatter", parts), functools.partial(sum_chips, names, parts)

    small = {k: weights[k] for k in SMALL}
    loss_part, grad_x, G, small_grads, halves = _local_step(
        x[0], mem[0], loss_target[0], small, W,
        gather_rest=(_ChipExchange("gather", [shards[k] for k in rest]), finish_weights), reduce_early=reduce_early)
    G["w_in"] = _unpad_in_proj(G["w_in"], D)

    late = [k for k in BIG if k not in halves]
    own, sib, small_parts = _exchange_siblings([_reshard(k, G[k]) for k in late], _pack_small(small_grads),
                                               name="exchange_siblings_late")
    parts = sum_pairs(late, own, sib)
    _, got = _exchange_chips(parts, name="exchange_chips_late")
    halves.update(sum_chips(late, parts, got))
    reduced = _share_halves([halves[k] for k in BIG])

    grads, deltas, new_m, new_v = {}, {}, {}, {}
    for k, g in zip(BIG, reduced):
        shp = weights[k].shape
        g2 = g.reshape(shp[1], shp[2])
        d, m2, v2 = _adamw(weights[k][0], g2, moms[k][0], vars_[k][0], name="adamw_" + k)
        grads[k], deltas[k], new_m[k], new_v[k] = g2.reshape(shp), d.reshape(shp), m2.reshape(shp), v2.reshape(shp)
    sg, sd, sm, sv = _adamw_small(_pack_small(small), small_parts, _pack_small({k: moms[k] for k in SMALL}),
                                  _pack_small({k: vars_[k] for k in SMALL}), name="adamw_small")
    for dst, packed in ((grads, sg), (deltas, sd), (new_m, sm), (new_v, sv)):
        dst.update(_unpack_small(packed, small))

    loss = lax.psum(loss_part, ("x", "y", "c"))
    return (loss, grad_x[None], *[grads[k] for k in ORDER], *[deltas[k] for k in ORDER],
            *[new_m[k] for k in ORDER], *[new_v[k] for k in ORDER])
```

```python
import functools

import jax
import jax.numpy as jnp
from jax import lax
from jax.experimental import pallas as pl
from jax.experimental.pallas import tpu as pltpu

F32 = jnp.float32
BF16 = jnp.bfloat16
MESH_ID = pl.DeviceIdType.MESH

HD = 64
SB_HEADS = 8
FOX_HEADS = 8
MEM_HEADS = 4
MEM_HD = 128
D_SB = SB_HEADS * HD
D_FOX = FOX_HEADS * HD
D_MEM = MEM_HEADS * MEM_HD
EPS = 1e-6
NEG_INF = -1e30

ADAM_LR = 0.001
ADAM_B1 = 0.9
ADAM_B2 = 0.999
ADAM_EPS = 1e-08
ADAM_WD = 0.01
ADAM_STEP = 10

N_CHIPS = 4
VMEM_LIMIT = 56 * 1024 * 1024

F_PAD = 256


def _tile(n, target, align=128):
    if n <= target:
        return n
    best = None
    t = align
    while t <= target:
        if n % t == 0:
            best = t
        t += align
    assert best is not None, (n, target, align)
    return best


def _params(sem):
    return pltpu.CompilerParams(dimension_semantics=sem, vmem_limit_bytes=VMEM_LIMIT)


def _mm(a, b, *, name, ta=False, tb=False, out_dtypes=(F32,), epilogue=None, extras=(),
        tm=512, tn=512, tk=1024, comm=None):
    if ta:
        K, M = a.shape
    else:
        M, K = a.shape
    if tb:
        N, K2 = b.shape
    else:
        K2, N = b.shape
    assert K == K2, (a.shape, b.shape, ta, tb)
    tm, tn, tk = _tile(M, tm), _tile(N, tn), _tile(K, tk)
    nk = K // tk
    n_extra, n_out = len(extras), len(out_dtypes)
    if epilogue is None:
        epilogue = lambda acc: (acc,)
    dims = (((0 if ta else 1,), (1 if tb else 0,)), ((), ()))

    gm, gn = M // tm, N // tn

    def body(*refs):
        i, j, k = pl.program_id(0), pl.program_id(1), pl.program_id(2)
        got = _carry(comm, 2 + n_extra, n_out, (i == 0) & (j == 0) & (k == 0),
                     (i == gm - 1) & (j == gn - 1) & (k == nk - 1), refs)
        (a_ref, b_ref, *extra_refs), out_refs = got[0], got[1]
        acc_ref = refs[-1]

        @pl.when(k == 0)
        def _():
            acc_ref[...] = jnp.zeros_like(acc_ref)

        acc_ref[...] += lax.dot_general(a_ref[...].astype(BF16), b_ref[...].astype(BF16), dims,
                                        preferred_element_type=F32)

        @pl.when(k == nk - 1)
        def _():
            outs = epilogue(acc_ref[...], *[r[...] for r in extra_refs])
            for o_ref, o in zip(out_refs, outs):
                o_ref[...] = o.astype(o_ref.dtype)

        got[2]()

    a_spec = pl.BlockSpec((tk, tm), lambda i, j, k: (k, i)) if ta else pl.BlockSpec((tm, tk), lambda i, j, k: (i, k))
    b_spec = pl.BlockSpec((tn, tk), lambda i, j, k: (j, k)) if tb else pl.BlockSpec((tk, tn), lambda i, j, k: (k, j))
    mn_spec = pl.BlockSpec((tm, tn), lambda i, j, k: (i, j))
    c_ins, c_in_specs, c_out_specs, c_out_shape, c_scratch = _comm_args(comm)
    sem = ("parallel", "parallel", "arbitrary") if comm is None else ("arbitrary",) * 3
    outs = pl.pallas_call(
        body, name=name,
        grid=(gm, gn, nk),
        in_specs=[a_spec, b_spec] + [mn_spec] * n_extra + c_in_specs,
        out_specs=[mn_spec] * n_out + c_out_specs,
        out_shape=[jax.ShapeDtypeStruct((M, N), dt) for dt in out_dtypes] + c_out_shape,
        scratch_shapes=c_scratch + [pltpu.VMEM((tm, tn), F32)],
        compiler_params=_params(sem),
    )(a, b, *extras, *c_ins)
    return outs if len(outs) > 1 else outs[0]


def _row_tile(rows, cols, n_arrays):
    budget = 10 * 1024 * 1024
    cols_padded = -(-cols // 128) * 128
    target = max(16, budget // (cols_padded * 4 * n_arrays * 2))
    return _tile(rows, target, align=16)


def _ew(fn, ins, out_dtypes, *, name):
    R, C = ins[0].shape
    n_in, n_out = len(ins), len(out_dtypes)
    tr = _row_tile(R, C, n_in + n_out)

    def body(*refs):
        outs = fn(*[r[...] for r in refs[:n_in]])
        for o_ref, o in zip(refs[n_in:], outs):
            o_ref[...] = o.astype(o_ref.dtype)

    spec = pl.BlockSpec((tr, C), lambda i: (i, 0))
    outs = pl.pallas_call(
        body, name=name, grid=(R // tr,),
        in_specs=[spec] * n_in, out_specs=[spec] * n_out,
        out_shape=[jax.ShapeDtypeStruct((R, C), dt) for dt in out_dtypes],
        compiler_params=_params(("parallel",)),
    )(*ins)
    return outs if n_out > 1 else outs[0]


def _rmsnorm_fwd(x, g, out_dtype, *, name):
    R, d = x.shape
    tr = _row_tile(R, d, 3)

    def body(x_ref, g_ref, o_ref):
        xv = x_ref[...].astype(F32)
        r = lax.rsqrt(jnp.mean(xv * xv, axis=-1, keepdims=True) + EPS)
        o_ref[...] = (xv * r * g_ref[...]).astype(o_ref.dtype)

    return pl.pallas_call(
        body, name=name, grid=(R // tr,),
        in_specs=[pl.BlockSpec((tr, d), lambda i: (i, 0)), pl.BlockSpec((1, d), lambda i: (0, 0))],
        out_specs=pl.BlockSpec((tr, d), lambda i: (i, 0)),
        out_shape=jax.ShapeDtypeStruct((R, d), out_dtype),
        compiler_params=_params(("parallel",)),
    )(x, g)


def _rmsnorm_bwd(x, g, dy, add=None, *, name):
    R, d = x.shape
    has_add = add is not None
    tr = _row_tile(R, d, 5)

    def body(*refs):
        x_ref, g_ref, dy_ref = refs[:3]
        add_ref = refs[3] if has_add else None
        dx_ref, dg_ref = refs[-2:]
        xv = x_ref[...].astype(F32)
        dyv = dy_ref[...].astype(F32)
        r = lax.rsqrt(jnp.mean(xv * xv, axis=-1, keepdims=True) + EPS)
        xh = xv * r
        dyg = dyv * g_ref[...]
        c = jnp.mean(dyg * xh, axis=-1, keepdims=True)
        dx = r * (dyg - xh * c)
        if has_add:
            dx = dx + add_ref[...]
        dx_ref[...] = dx

        @pl.when(pl.program_id(0) == 0)
        def _():
            dg_ref[...] = jnp.zeros_like(dg_ref)

        dg_ref[...] += jnp.sum(dyv * xh, axis=0, keepdims=True)

    row = pl.BlockSpec((tr, d), lambda i: (i, 0))
    vec = pl.BlockSpec((1, d), lambda i: (0, 0))
    ins = [x, g, dy] + ([add] if has_add else [])
    return pl.pallas_call(
        body, name=name, grid=(R // tr,),
        in_specs=[row, vec, row] + ([row] if has_add else []),
        out_specs=[row, vec],
        out_shape=[jax.ShapeDtypeStruct((R, d), F32), jax.ShapeDtypeStruct((1, d), F32)],
        compiler_params=_params(("arbitrary",)),
    )(*ins)


_NT = (((1,), (1,)), ((), ()))
_TN = (((0,), (0,)), ((), ()))


def _dot(a, b, dims=(((1,), (0,)), ((), ()))):
    return lax.dot_general(a, b, dims, preferred_element_type=F32)


def _split_dot(x, tri):
    hi = x.astype(BF16)
    lo = (x - hi.astype(F32)).astype(BF16)
    return _dot(hi, tri) + _dot(lo, tri)


def _log_sigmoid_pair(z):
    sp = jnp.log(1.0 + jnp.exp(-jnp.abs(z)))
    return jnp.minimum(z, 0.0) - sp, jnp.minimum(-z, 0.0) - sp


def _sb_fwd(q, k, v, *, name, tq=256):
    H, S, hd = q.shape
    tq = _tile(S, tq)
    tk = tq
    scale = hd ** -0.5

    def body(q_ref, k_ref, v_ref, o_ref, tot_ref):
        qi = pl.program_id(1)
        qv = q_ref[0]
        row = qi * tq + lax.broadcasted_iota(jnp.int32, (tq, tk), 0)
        col0 = lax.broadcasted_iota(jnp.int32, (tq, tk), 1)
        later = (lax.broadcasted_iota(jnp.int32, (tk, tk), 0) > lax.broadcasted_iota(jnp.int32, (tk, tk), 1)).astype(BF16)

        def step(i, carry):
            acc, c_rem = carry
            kb = qi - i
            ks = pl.multiple_of(kb * tk, tk)
            kv = k_ref[0, pl.ds(ks, tk), :]
            vv = v_ref[0, pl.ds(ks, tk), :]
            z = _dot(qv, kv, _NT) * scale
            mask = (col0 + ks) < row
            ls, lsn = _log_sigmoid_pair(z)
            rem = jnp.where(mask, lsn, 0.0)
            after = _split_dot(rem, later) + c_rem
            w = jnp.where(mask, jnp.exp(ls + after), 0.0)
            acc = acc + _dot(w.astype(BF16), vv)
            c_rem = c_rem + jnp.sum(rem, axis=1, keepdims=True)
            return acc, c_rem

        acc, c_rem = lax.fori_loop(0, qi + 1, step, (jnp.zeros((tq, hd), F32), jnp.zeros((tq, 1), F32)))
        o_ref[0] = acc
        tot_ref[0] = c_rem

    qspec = pl.BlockSpec((1, tq, hd), lambda h, i: (h, i, 0))
    kspec = pl.BlockSpec((1, S, hd), lambda h, i: (h, 0, 0))
    return pl.pallas_call(
        body, name=name, grid=(H, S // tq),
        in_specs=[qspec, kspec, kspec],
        out_specs=[qspec, pl.BlockSpec((1, tq, 1), lambda h, i: (h, i, 0))],
        out_shape=[jax.ShapeDtypeStruct((H, S, hd), F32), jax.ShapeDtypeStruct((H, S, 1), F32)],
        compiler_params=_params(("parallel", "arbitrary")),
    )(q, k, v)


def _sb_bwd(q, k, v, do, tot, *, name, tq=256):
    H, S, hd = q.shape
    tq = _tile(S, tq)
    tk = tq
    scale = hd ** -0.5

    def body(q_ref, k_ref, v_ref, do_ref, tot_ref, dq_ref, dk_ref, dv_ref):
        qi = pl.program_id(1)

        @pl.when(qi == 0)
        def _():
            dk_ref[...] = jnp.zeros_like(dk_ref)
            dv_ref[...] = jnp.zeros_like(dv_ref)

        qv = q_ref[0]
        dov = do_ref[0].astype(BF16)
        tot_v = tot_ref[0]
        row = qi * tq + lax.broadcasted_iota(jnp.int32, (tq, tk), 0)
        col0 = lax.broadcasted_iota(jnp.int32, (tq, tk), 1)
        r_i = lax.broadcasted_iota(jnp.int32, (tk, tk), 0)
        c_i = lax.broadcasted_iota(jnp.int32, (tk, tk), 1)
        upto = (r_i <= c_i).astype(BF16)
        before = (r_i < c_i).astype(BF16)

        def step(kb, carry):
            dq, c_rem, c_g = carry
            ks = pl.multiple_of(kb * tk, tk)
            kv = k_ref[0, pl.ds(ks, tk), :]
            vv = v_ref[0, pl.ds(ks, tk), :]
            z = _dot(qv, kv, _NT) * scale
            mask = (col0 + ks) < row
            ls, lsn = _log_sigmoid_pair(z)
            rem = jnp.where(mask, lsn, 0.0)
            after = tot_v - (_split_dot(rem, upto) + c_rem)
            w = jnp.where(mask, jnp.exp(ls + after), 0.0)
            dw = _dot(dov, vv, _NT)
            g = dw * w
            g_before = _split_dot(g, before) + c_g
            beta = jnp.exp(ls)
            dz = jnp.where(mask, g * (1.0 - beta) - beta * g_before, 0.0) * scale
            dzb = dz.astype(BF16)
            dq = dq + _dot(dzb, kv)
            dk_ref[0, pl.ds(ks, tk), :] += _dot(dzb, qv, _TN)
            dv_ref[0, pl.ds(ks, tk), :] += _dot(w.astype(BF16), dov, _TN)
            return dq, c_rem + jnp.sum(rem, axis=1, keepdims=True), c_g + jnp.sum(g, axis=1, keepdims=True)

        zero = jnp.zeros((tq, 1), F32)
        dq, _, _ = lax.fori_loop(0, qi + 1, step, (jnp.zeros((tq, hd), F32), zero, zero))
        dq_ref[0] = dq

    qspec = pl.BlockSpec((1, tq, hd), lambda h, i: (h, i, 0))
    kspec = pl.BlockSpec((1, S, hd), lambda h, i: (h, 0, 0))
    full = jax.ShapeDtypeStruct((H, S, hd), F32)
    return pl.pallas_call(
        body, name=name, grid=(H, S // tq),
        in_specs=[qspec, kspec, kspec, qspec, pl.BlockSpec((1, tq, 1), lambda h, i: (h, i, 0))],
        out_specs=[qspec, kspec, kspec],
        out_shape=[full, full, full],
        compiler_params=_params(("parallel", "arbitrary")),
    )(q, k, v, do, tot)


def _softmax_fwd(q, k, v, bias=None, *, name, causal, tq=256, tk=256):
    H, S, hd = q.shape
    Sk = k.shape[1]
    tq, tk = _tile(S, tq), _tile(Sk, tk)
    if causal:
        assert tq == tk and S == Sk
    nk = Sk // tk
    scale = hd ** -0.5
    has_bias = bias is not None

    def body(*refs):
        q_ref, k_ref, v_ref = refs[:3]
        o_ref, lse_ref = refs[-2:]
        qi = pl.program_id(1)
        qv = q_ref[0]
        row = qi * tq + lax.broadcasted_iota(jnp.int32, (tq, tk), 0)
        col0 = lax.broadcasted_iota(jnp.int32, (tq, tk), 1)
        if has_bias:
            b_row = refs[3][0]

        def step(kb, carry):
            m, l, acc = carry
            ks = pl.multiple_of(kb * tk, tk)
            kv = k_ref[0, pl.ds(ks, tk), :]
            vv = v_ref[0, pl.ds(ks, tk), :]
            z = _dot(qv, kv, _NT) * scale
            if has_bias:
                z = z + b_row - refs[4][0, kb]
            if causal:
                z = jnp.where((col0 + ks) <= row, z, NEG_INF)
            m2 = jnp.maximum(m, jnp.max(z, axis=1, keepdims=True))
            p = jnp.exp(z - m2)
            alpha = jnp.exp(m - m2)
            l = alpha * l + jnp.sum(p, axis=1, keepdims=True)
            acc = alpha * acc + _dot(p.astype(BF16), vv)
            return m2, l, acc

        init = (jnp.full((tq, 1), NEG_INF, F32), jnp.zeros((tq, 1), F32), jnp.zeros((tq, hd), F32))
        m, l, acc = lax.fori_loop(0, (qi + 1) if causal else nk, step, init)
        o_ref[0] = acc / l
        lse_ref[0] = m + jnp.log(l)

    qspec = pl.BlockSpec((1, tq, hd), lambda h, i: (h, i, 0))
    kspec = pl.BlockSpec((1, Sk, hd), lambda h, i: (h, 0, 0))
    vspec = pl.BlockSpec((1, tq, 1), lambda h, i: (h, i, 0))
    in_specs = [qspec, kspec, kspec]
    ins = [q, k, v]
    if has_bias:
        in_specs += [vspec, pl.BlockSpec((1, nk, 1, tk), lambda h, i: (h, 0, 0, 0))]
        ins += list(bias)
    return pl.pallas_call(
        body, name=name, grid=(H, S // tq),
        in_specs=in_specs, out_specs=[qspec, vspec],
        out_shape=[jax.ShapeDtypeStruct((H, S, hd), F32), jax.ShapeDtypeStruct((H, S, 1), F32)],
        compiler_params=_params(("parallel", "arbitrary")),
    )(*ins)


def _softmax_bwd(q, k, v, o, lse, do, bias=None, *, name, causal, tq=256, tk=256):
    H, S, hd = q.shape
    Sk = k.shape[1]
    tq, tk = _tile(S, tq), _tile(Sk, tk)
    nk = Sk // tk
    scale = hd ** -0.5
    has_bias = bias is not None
    n_in = 8 if has_bias else 6

    def body(*refs):
        q_ref, k_ref, v_ref, o_ref, lse_ref, do_ref = refs[:6]
        dq_ref, dk_ref, dv_ref = refs[n_in:n_in + 3]
        qi = pl.program_id(1)

        @pl.when(qi == 0)
        def _():
            dk_ref[...] = jnp.zeros_like(dk_ref)
            dv_ref[...] = jnp.zeros_like(dv_ref)
            if has_bias:
                refs[n_in + 4][...] = jnp.zeros_like(refs[n_in + 4])

        qv = q_ref[0]
        do32 = do_ref[0]
        dov = do32.astype(BF16)
        delta = jnp.sum(do32 * o_ref[0], axis=1, keepdims=True)
        lse_v = lse_ref[0]
        row = qi * tq + lax.broadcasted_iota(jnp.int32, (tq, tk), 0)
        col0 = lax.broadcasted_iota(jnp.int32, (tq, tk), 1)
        if has_bias:
            b_row = refs[6][0]

        def step(kb, carry):
            dq, db_row = carry
            ks = pl.multiple_of(kb * tk, tk)
            kv = k_ref[0, pl.ds(ks, tk), :]
            vv = v_ref[0, pl.ds(ks, tk), :]
            z = _dot(qv, kv, _NT) * scale
            if has_bias:
                z = z + b_row - refs[7][0, kb]
            p = jnp.exp(z - lse_v)
            if causal:
                p = jnp.where((col0 + ks) <= row, p, 0.0)
            dp = _dot(dov, vv, _NT)
            dz = p * (dp - delta)
            dzb = dz.astype(BF16)
            dq = dq + _dot(dzb, kv)
            dk_ref[0, pl.ds(ks, tk), :] += _dot(dzb, qv, _TN) * scale
            dv_ref[0, pl.ds(ks, tk), :] += _dot(p.astype(BF16), dov, _TN)
            if has_bias:
                db_row = db_row + jnp.sum(dz, axis=1, keepdims=True)
                refs[n_in + 4][0, kb] += jnp.sum(dz, axis=0, keepdims=True)
            return dq, db_row

        dq, db_row = lax.fori_loop(0, (qi + 1) if causal else nk, step,
                                   (jnp.zeros((tq, hd), F32), jnp.zeros((tq, 1), F32)))
        dq_ref[0] = dq * scale
        if has_bias:
            refs[n_in + 3][0] = db_row

    qspec = pl.BlockSpec((1, tq, hd), lambda h, i: (h, i, 0))
    kspec = pl.BlockSpec((1, Sk, hd), lambda h, i: (h, 0, 0))
    vspec = pl.BlockSpec((1, tq, 1), lambda h, i: (h, i, 0))
    cspec = pl.BlockSpec((1, nk, 1, tk), lambda h, i: (h, 0, 0, 0))
    in_specs = [qspec, kspec, kspec, qspec, vspec, qspec]
    ins = [q, k, v, o, lse, do]
    out_specs = [qspec, kspec, kspec]
    out_shape = [jax.ShapeDtypeStruct((H, S, hd), F32), jax.ShapeDtypeStruct((H, Sk, hd), F32),
                 jax.ShapeDtypeStruct((H, Sk, hd), F32)]
    if has_bias:
        in_specs += [vspec, cspec]
        ins += list(bias)
        out_specs += [vspec, cspec]
        out_shape += [jax.ShapeDtypeStruct((H, S, 1), F32), jax.ShapeDtypeStruct((H, nk, 1, tk), F32)]
    return pl.pallas_call(
        body, name=name, grid=(H, S // tq),
        in_specs=in_specs, out_specs=out_specs, out_shape=out_shape,
        compiler_params=_params(("parallel", "arbitrary")),
    )(*ins)


LANES = 128
_LOW = -3e38


def _lane_masks(hd, rows):
    if hd == LANES:
        return [None]
    lane = lax.broadcasted_iota(jnp.int32, (rows, LANES), 1)
    return [(lane >= hh * hd) & (lane < (hh + 1) * hd) for hh in range(LANES // hd)]


def _keep(t, m):
    return t if m is None else jnp.where(m, t, 0.0)


def _merge(parts, masks):
    out = parts[-1]
    for p, m in zip(parts[-2::-1], masks[-2::-1]):
        out = jnp.where(m, p, out)
    return out


def _row_value(t, m):
    return jnp.max(t if m is None else jnp.where(m, t, _LOW), axis=1, keepdims=True)


def _cols(tq, off):
    return pl.BlockSpec((tq, LANES), lambda g, i: (i, off + g))


def _cols_all(rows, off):
    return pl.BlockSpec((rows, LANES), lambda g, i: (0, off + g))


def _softplus_parts(z):
    pos = jnp.maximum(z, 0.0) + jnp.log(1.0 + jnp.exp(-jnp.abs(z)))
    return pos, z - pos


class _ChipExchange:
    def __init__(self, kind, ins):
        assert kind in ("gather", "scatter")
        self.kind, self.ins = kind, list(ins)
        lead = (lambda s: (N_CHIPS,) + s) if kind == "gather" else (lambda s: (3,) + s[1:])
        self.out_shape = [jax.ShapeDtypeStruct(lead(a.shape), a.dtype) for a in ins]
        n = 3 * len(ins)
        self.scratch = [pltpu.SemaphoreType.DMA((n,)), pltpu.SemaphoreType.DMA((n,))]

    def _copies(self, in_refs, out_refs, sems, landing):
        x, y, c, chips = _mesh_place()
        me = 2 * x + y
        out = []
        for w in range(len(self.ins)):
            for j, chip in enumerate(chips):
                peer = 2 * chip[0] + chip[1]
                if self.kind == "gather":
                    src, dst, land = in_refs[w].at[c], out_refs[w].at[me, c], out_refs[w].at[peer, c]
                else:
                    src, dst, land = in_refs[w].at[peer], out_refs[w].at[j], out_refs[w].at[j]
                if landing:
                    src, dst = land, land
                out.append(_remote(src, dst, sems, 3 * w + j, (chip[0], chip[1], c)))
        return out

    def start(self, in_refs, out_refs, sems):
        for cp in self._copies(in_refs, out_refs, sems, False):
            cp.start()

    def finish(self, in_refs, out_refs, sems):
        for cp in self._copies(in_refs, out_refs, sems, True):
            cp.wait_recv()
        for cp in self._copies(in_refs, out_refs, sems, False):
            cp.wait_send()


def _carry(comm, n_in, n_out, first, last, refs):
    if comm is None:
        return refs[:n_in], refs[n_in:n_in + n_out], (lambda: None)
    a, b = len(comm.ins), len(comm.out_shape)
    ins, c_in = refs[:n_in], refs[n_in:n_in + a]
    outs, c_out = refs[n_in + a:n_in + a + n_out], refs[n_in + a + n_out:n_in + a + n_out + b]
    sems = refs[n_in + a + n_out + b:n_in + a + n_out + b + 2]
    pl.when(first)(lambda: comm.start(c_in, c_out, sems))
    return ins, outs, (lambda: pl.when(last)(lambda: comm.finish(c_in, c_out, sems)))


def _sbl_fwd(q, k, v, *, width, hd, name, tq=256, comm=None):
    (qa, qo), (ka, ko), (va, vo) = q, k, v
    S = qa.shape[0]
    tq = _tile(S, tq)
    tk = tq
    scale = hd ** -0.5
    n_g, n_q = width // LANES, S // tq

    def body(*refs):
        qi = pl.program_id(1)
        gi = pl.program_id(0)
        got = _carry(comm, 3, 2, (gi == 0) & (qi == 0), (gi == n_g - 1) & (qi == n_q - 1), refs)
        (q_ref, k_ref, v_ref), (o_ref, tot_ref) = got[0], got[1]
        masks = _lane_masks(hd, tq)
        qs = q_ref[...].astype(F32) * scale
        qm = [_keep(qs, m).astype(BF16) for m in masks]
        strict = lax.broadcasted_iota(jnp.int32, (tq, tk), 1) < lax.broadcasted_iota(jnp.int32, (tq, tk), 0)
        later = (lax.broadcasted_iota(jnp.int32, (tk, tk), 0) > lax.broadcasted_iota(jnp.int32, (tk, tk), 1)).astype(BF16)

        def tile(kb, carry, diag):
            ks = pl.multiple_of(kb * tk, tk)
            kv = k_ref[pl.ds(ks, tk), :].astype(BF16)
            vv = v_ref[pl.ds(ks, tk), :].astype(BF16)
            out = []
            for hh in range(len(masks)):
                acc, c_pos = carry[2 * hh], carry[2 * hh + 1]
                pos, ls = _softplus_parts(_dot(qm[hh], kv, _NT))
                if diag:
                    pos = jnp.where(strict, pos, 0.0)
                w = jnp.exp(ls - (_split_dot(pos, later) + c_pos))
                if diag:
                    w = jnp.where(strict, w, 0.0)
                out += [acc + _dot(w.astype(BF16), vv), c_pos + jnp.sum(pos, axis=1, keepdims=True)]
            return tuple(out)

        init = (jnp.zeros((tq, LANES), F32), jnp.zeros((tq, 1), F32)) * len(masks)
        carry = tile(qi, init, True)
        carry = lax.fori_loop(0, qi, lambda i, c: tile(qi - 1 - i, c, False), carry)
        o_ref[...] = _merge(carry[0::2], masks).astype(o_ref.dtype)
        tot_ref[...] = _merge([jnp.broadcast_to(-c, (tq, LANES)) for c in carry[1::2]], masks)
        got[2]()

    c_ins, c_in_specs, c_out_specs, c_out_shape, c_scratch = _comm_args(comm)
    return pl.pallas_call(
        body, name=name, grid=(n_g, n_q),
        in_specs=[_cols(tq, qo), _cols_all(S, ko), _cols_all(S, vo)] + c_in_specs,
        out_specs=[_cols(tq, 0), _cols(tq, 0)] + c_out_specs,
        out_shape=[jax.ShapeDtypeStruct((S, width), BF16), jax.ShapeDtypeStruct((S, width), F32)] + c_out_shape,
        scratch_shapes=c_scratch,
        compiler_params=_params(("arbitrary", "arbitrary")),
    )(qa, ka, va, *c_ins)


def _comm_args(comm):
    if comm is None:
        return [], [], [], [], []
    return comm.ins, [_ANY] * len(comm.ins), [_ANY] * len(comm.out_shape), comm.out_shape, comm.scratch


def _sbl_bwd(q, k, v, do, tot, *, width, hd, name, tq=256, comm=None):
    (qa, qo), (ka, ko), (va, vo) = q, k, v
    S = qa.shape[0]
    tq = _tile(S, tq)
    tk = tq
    scale = hd ** -0.5
    n_g, n_q = width // LANES, S // tq

    def body(*refs):
        qi = pl.program_id(1)
        gi = pl.program_id(0)
        got = _carry(comm, 5, 3, (gi == 0) & (qi == 0), (gi == n_g - 1) & (qi == n_q - 1), refs)
        (q_ref, k_ref, v_ref, do_ref, tot_ref), (dq_ref, dk_ref, dv_ref) = got[0], got[1]

        @pl.when(qi == 0)
        def _():
            dk_ref[...] = jnp.zeros_like(dk_ref)
            dv_ref[...] = jnp.zeros_like(dv_ref)

        masks = _lane_masks(hd, tq)
        qs = q_ref[...].astype(F32) * scale
        qm = [_keep(qs, m).astype(BF16) for m in masks]
        dov = [_keep(do_ref[...], m).astype(BF16) for m in masks]
        tots = [_row_value(tot_ref[...], m) for m in masks]
        r_q = lax.broadcasted_iota(jnp.int32, (tq, tk), 0)
        strict = lax.broadcasted_iota(jnp.int32, (tq, tk), 1) < r_q
        r_i = lax.broadcasted_iota(jnp.int32, (tk, tk), 0)
        c_i = lax.broadcasted_iota(jnp.int32, (tk, tk), 1)
        upto = (r_i <= c_i).astype(BF16)
        before = (r_i < c_i).astype(BF16)

        def tile(kb, carry, diag):
            ks = pl.multiple_of(kb * tk, tk)
            kv = k_ref[pl.ds(ks, tk), :].astype(BF16)
            vv = v_ref[pl.ds(ks, tk), :].astype(BF16)
            out = []
            dk_t, dv_t = None, None
            for hh in range(len(masks)):
                dq, c_pos, c_g = carry[3 * hh:3 * hh + 3]
                pos, ls = _softplus_parts(_dot(qm[hh], kv, _NT))
                if diag:
                    pos = jnp.where(strict, pos, 0.0)
                w = jnp.exp(ls + tots[hh] + (_split_dot(pos, upto) + c_pos))
                if diag:
                    w = jnp.where(strict, w, 0.0)
                g = _dot(dov[hh], vv, _NT) * w
                g_before = _split_dot(g, before) + c_g
                dz = g - jnp.exp(ls) * (g + g_before)
                if diag:
                    dz = jnp.where(strict, dz, 0.0)
                dzb = dz.astype(BF16)
                dk_h = _dot(dzb, qm[hh], _TN)
                dv_h = _dot(w.astype(BF16), dov[hh], _TN)
                dk_t = dk_h if dk_t is None else dk_t + dk_h
                dv_t = dv_h if dv_t is None else dv_t + dv_h
                out += [dq + _dot(dzb, kv), c_pos + jnp.sum(pos, axis=1, keepdims=True),
                        c_g + jnp.sum(g, axis=1, keepdims=True)]
            dk_ref[pl.ds(ks, tk), :] += dk_t
            dv_ref[pl.ds(ks, tk), :] += dv_t
            return tuple(out)

        zero = jnp.zeros((tq, 1), F32)
        init = (jnp.zeros((tq, LANES), F32), zero, zero) * len(masks)
        carry = lax.fori_loop(0, qi, lambda kb, c: tile(kb, c, False), init)
        carry = tile(qi, carry, True)
        dq_ref[...] = _merge(carry[0::3], masks) * scale
        got[2]()

    full = jax.ShapeDtypeStruct((S, width), F32)
    c_ins, c_in_specs, c_out_specs, c_out_shape, c_scratch = _comm_args(comm)
    return pl.pallas_call(
        body, name=name, grid=(n_g, n_q),
        in_specs=[_cols(tq, qo), _cols_all(S, ko), _cols_all(S, vo), _cols(tq, do[1]), _cols(tq, tot[1])] + c_in_specs,
        out_specs=[_cols(tq, 0), _cols_all(S, 0), _cols_all(S, 0)] + c_out_specs,
        out_shape=[full, full, full] + c_out_shape,
        scratch_shapes=c_scratch,
        compiler_params=_params(("arbitrary", "arbitrary")),
    )(qa, ka, va, do[0], tot[0], *c_ins)


def _sml_fwd(q, k, v, bias=None, *, width, hd, causal, name, tq=256, tk=256):
    (qa, qo), (ka, ko), (va, vo) = q, k, v
    S, Sk = qa.shape[0], ka.shape[0]
    tq, tk = _tile(S, tq), _tile(Sk, tk)
    if causal:
        assert tq == tk and S == Sk
    nk = Sk // tk
    hpg = LANES // hd
    scale = hd ** -0.5
    has_bias = bias is not None

    def body(*refs):
        q_ref, k_ref, v_ref = refs[:3]
        o_ref, lse_ref = refs[-2:]
        qi = pl.program_id(1)
        masks = _lane_masks(hd, tq)
        qs = q_ref[...].astype(F32) * scale
        qm = [_keep(qs, m).astype(BF16) for m in masks]
        allowed = lax.broadcasted_iota(jnp.int32, (tq, tk), 1) <= lax.broadcasted_iota(jnp.int32, (tq, tk), 0)

        def tile(kb, carry, diag):
            ks = pl.multiple_of(kb * tk, tk)
            kv = k_ref[pl.ds(ks, tk), :].astype(BF16)
            vv = v_ref[pl.ds(ks, tk), :].astype(BF16)
            out = []
            for hh in range(hpg):
                m, l, acc = carry[3 * hh:3 * hh + 3]
                z = _dot(qm[hh], kv, _NT)
                if has_bias:
                    z = z + refs[3][hh] - refs[4][hh, kb]
                if diag:
                    z = jnp.where(allowed, z, NEG_INF)
                m2 = jnp.maximum(m, jnp.max(z, axis=1, keepdims=True))
                p = jnp.exp(z - m2)
                alpha = jnp.exp(m - m2)
                out += [m2, alpha * l + jnp.sum(p, axis=1, keepdims=True), alpha * acc + _dot(p.astype(BF16), vv)]
            return tuple(out)

        init = (jnp.full((tq, 1), NEG_INF, F32), jnp.zeros((tq, 1), F32), jnp.zeros((tq, LANES), F32)) * hpg
        if causal:
            carry = lax.fori_loop(0, qi, lambda kb, c: tile(kb, c, False), init)
            carry = tile(qi, carry, True)
        else:
            carry = lax.fori_loop(0, nk, lambda kb, c: tile(kb, c, False), init)
        o_ref[...] = _merge([acc / l for l, acc in zip(carry[1::3], carry[2::3])], masks).astype(o_ref.dtype)
        lse_ref[...] = _merge([jnp.broadcast_to(m + jnp.log(l), (tq, LANES)) for m, l in zip(carry[0::3], carry[1::3])], masks)

    in_specs = [_cols(tq, qo), _cols_all(Sk, ko), _cols_all(Sk, vo)]
    ins = [qa, ka, va]
    if has_bias:
        in_specs += [pl.BlockSpec((hpg, tq, 1), lambda g, i: (g, i, 0)),
                     pl.BlockSpec((hpg, nk, 1, tk), lambda g, i: (g, 0, 0, 0))]
        ins += list(bias)
    return pl.pallas_call(
        body, name=name, grid=(width // LANES, S // tq),
        in_specs=in_specs, out_specs=[_cols(tq, 0), _cols(tq, 0)],
        out_shape=[jax.ShapeDtypeStruct((S, width), BF16), jax.ShapeDtypeStruct((S, width), F32)],
        compiler_params=_params(("parallel", "arbitrary")),
    )(*ins)


def _sml_bwd(q, k, v, o, lse, do, bias=None, *, width, hd, causal, name, tq=256, tk=256):
    (qa, qo), (ka, ko), (va, vo) = q, k, v
    S, Sk = qa.shape[0], ka.shape[0]
    tq, tk = _tile(S, tq), _tile(Sk, tk)
    nk = Sk // tk
    hpg = LANES // hd
    scale = hd ** -0.5
    has_bias = bias is not None
    n_in = 8 if has_bias else 6

    def body(*refs):
        q_ref, k_ref, v_ref, o_ref, lse_ref, do_ref = refs[:6]
        dq_ref, dk_ref, dv_ref = refs[n_in:n_in + 3]
        qi = pl.program_id(1)

        @pl.when(qi == 0)
        def _():
            dk_ref[...] = jnp.zeros_like(dk_ref)
            dv_ref[...] = jnp.zeros_like(dv_ref)
            if has_bias:
                refs[n_in + 4][...] = jnp.zeros_like(refs[n_in + 4])

        masks = _lane_masks(hd, tq)
        qs = q_ref[...].astype(F32) * scale
        qm = [_keep(qs, m).astype(BF16) for m in masks]
        do32 = do_ref[...]
        dov = [_keep(do32, m).astype(BF16) for m in masks]
        prod = do32 * o_ref[...].astype(F32)
        delta = [jnp.sum(_keep(prod, m), axis=1, keepdims=True) for m in masks]
        lses = [_row_value(lse_ref[...], m) for m in masks]
        allowed = lax.broadcasted_iota(jnp.int32, (tq, tk), 1) <= lax.broadcasted_iota(jnp.int32, (tq, tk), 0)

        def tile(kb, carry, diag):
            ks = pl.multiple_of(kb * tk, tk)
            kv = k_ref[pl.ds(ks, tk), :].astype(BF16)
            vv = v_ref[pl.ds(ks, tk), :].astype(BF16)
            out = []
            dk_t, dv_t = None, None
            for hh in range(hpg):
                dq, db_row = carry[2 * hh:2 * hh + 2]
                z = _dot(qm[hh], kv, _NT)
                if has_bias:
                    z = z + refs[6][hh] - refs[7][hh, kb]
                p = jnp.exp(z - lses[hh])
                if diag:
                    p = jnp.where(allowed, p, 0.0)
                dz = p * (_dot(dov[hh], vv, _NT) - delta[hh])
                dzb = dz.astype(BF16)
                dk_h = _dot(dzb, qm[hh], _TN)
                dv_h = _dot(p.astype(BF16), dov[hh], _TN)
                dk_t = dk_h if dk_t is None else dk_t + dk_h
                dv_t = dv_h if dv_t is None else dv_t + dv_h
                if has_bias:
                    db_row = db_row + jnp.sum(dz, axis=1, keepdims=True)
                    refs[n_in + 4][hh, kb] += jnp.sum(dz, axis=0, keepdims=True)
                out += [dq + _dot(dzb, kv), db_row]
            dk_ref[pl.ds(ks, tk), :] += dk_t
            dv_ref[pl.ds(ks, tk), :] += dv_t
            return tuple(out)

        init = (jnp.zeros((tq, LANES), F32), jnp.zeros((tq, 1), F32)) * hpg
        if causal:
            carry = lax.fori_loop(0, qi, lambda kb, c: tile(kb, c, False), init)
            carry = tile(qi, carry, True)
        else:
            carry = lax.fori_loop(0, nk, lambda kb, c: tile(kb, c, False), init)
        dq_ref[...] = _merge(carry[0::2], masks) * scale
        if has_bias:
            for hh in range(hpg):
                refs[n_in + 3][hh] = carry[2 * hh + 1]

    in_specs = [_cols(tq, qo), _cols_all(Sk, ko), _cols_all(Sk, vo), _cols(tq, o[1]), _cols(tq, lse[1]), _cols(tq, do[1])]
    ins = [qa, ka, va, o[0], lse[0], do[0]]
    out_specs = [_cols(tq, 0), _cols_all(Sk, 0), _cols_all(Sk, 0)]
    out_shape = [jax.ShapeDtypeStruct((S, width), F32), jax.ShapeDtypeStruct((Sk, width), F32),
                 jax.ShapeDtypeStruct((Sk, width), F32)]
    if has_bias:
        rspec = pl.BlockSpec((hpg, tq, 1), lambda g, i: (g, i, 0))
        cspec = pl.BlockSpec((hpg, nk, 1, tk), lambda g, i: (g, 0, 0, 0))
        in_specs += [rspec, cspec]
        ins += list(bias)
        out_specs += [rspec, cspec]
        n_heads = width // hd
        out_shape += [jax.ShapeDtypeStruct((n_heads, S, 1), F32), jax.ShapeDtypeStruct((n_heads, nk, 1, tk), F32)]
    return pl.pallas_call(
        body, name=name, grid=(width // LANES, S // tq),
        in_specs=in_specs, out_specs=out_specs, out_shape=out_shape,
        compiler_params=_params(("parallel", "arbitrary")),
    )(*ins)


def _head_sums(t, masks):
    sums = [jnp.sum(_keep(t, m), axis=1, keepdims=True) for m in masks]
    return _merge([jnp.broadcast_to(s, t.shape) for s in sums], masks) if len(masks) > 1 else sums[0]


def _hnorm_fwd(x, g_lanes, *, width, hd, name, tr=512):
    xa, xo = x
    R = xa.shape[0]
    tr = _tile(R, tr, align=16)
    n_blk = width // LANES

    def body(x_ref, g_ref, o_ref):
        masks = _lane_masks(hd, tr)
        for j in range(n_blk):
            sl = slice(j * LANES, (j + 1) * LANES)
            xv = x_ref[:, sl].astype(F32)
            r = lax.rsqrt(_head_sums(xv * xv, masks) * (1.0 / hd) + EPS)
            o_ref[:, sl] = (xv * r * g_ref[...]).astype(o_ref.dtype)

    assert (xo * LANES) % width == 0
    return pl.pallas_call(
        body, name=name, grid=(R // tr,),
        in_specs=[pl.BlockSpec((tr, width), lambda i: (i, xo * LANES // width)), pl.BlockSpec((1, LANES), lambda i: (0, 0))],
        out_specs=pl.BlockSpec((tr, width), lambda i: (i, 0)),
        out_shape=jax.ShapeDtypeStruct((R, width), BF16),
        compiler_params=_params(("parallel",)),
    )(xa, g_lanes)


def _hnorm_bwd(x, g_lanes, dy, *, width, hd, name, tr=512):
    xa, xo = x
    R = xa.shape[0]
    tr = _tile(R, tr, align=16)
    n_blk = width // LANES

    def body(x_ref, g_ref, dy_ref, dx_ref, dg_ref):
        masks = _lane_masks(hd, tr)
        dg = jnp.zeros((1, LANES), F32)
        for j in range(n_blk):
            sl = slice(j * LANES, (j + 1) * LANES)
            xv = x_ref[:, sl].astype(F32)
            dyv = dy_ref[:, sl].astype(F32)
            r = lax.rsqrt(_head_sums(xv * xv, masks) * (1.0 / hd) + EPS)
            xh = xv * r
            dyg = dyv * g_ref[...]
            c = _head_sums(dyg * xh, masks) * (1.0 / hd)
            dx_ref[:, sl] = (r * (dyg - xh * c)).astype(dx_ref.dtype)
            dg = dg + jnp.sum(dyv * xh, axis=0, keepdims=True)
        if hd * 2 == LANES:
            dg8 = jnp.broadcast_to(dg, (8, LANES))
            dg = (dg8 + pltpu.roll(dg8, shift=hd, axis=1))[0:1]
        else:
            assert hd == LANES

        @pl.when(pl.program_id(0) == 0)
        def _():
            dg_ref[...] = jnp.zeros_like(dg_ref)

        dg_ref[...] += dg

    assert (xo * LANES) % width == 0
    return pl.pallas_call(
        body, name=name, grid=(R // tr,),
        in_specs=[pl.BlockSpec((tr, width), lambda i: (i, xo * LANES // width)), pl.BlockSpec((1, LANES), lambda i: (0, 0)),
                  pl.BlockSpec((tr, width), lambda i: (i, 0))],
        out_specs=[pl.BlockSpec((tr, width), lambda i: (i, 0)), pl.BlockSpec((1, LANES), lambda i: (0, 0))],
        out_shape=[jax.ShapeDtypeStruct((R, width), BF16), jax.ShapeDtypeStruct((1, LANES), F32)],
        compiler_params=_params(("arbitrary",)),
    )(xa, g_lanes, dy)


def _split3_dot(x, tri):
    a = x.astype(BF16)
    r = x - a.astype(F32)
    b = r.astype(BF16)
    c = (r - b.astype(F32)).astype(BF16)
    return _dot(a, tri) + _dot(b, tri) + _dot(c, tri)


def _forget_fwd(logit_t, b_col, *, name, blk=512):
    H, S = logit_t.shape
    blk = _tile(S, blk)

    def body(l_ref, b_ref, f_ref):
        r_i = lax.broadcasted_iota(jnp.int32, (blk, blk), 0)
        c_i = lax.broadcasted_iota(jnp.int32, (blk, blk), 1)
        upto = (r_i <= c_i).astype(BF16)
        carry = jnp.zeros((H, 1), F32)
        for j in range(S // blk):
            u = l_ref[:, j * blk:(j + 1) * blk] + b_ref[...]
            lf, _ = _log_sigmoid_pair(u)
            f_ref[:, j * blk:(j + 1) * blk] = _split3_dot(lf, upto) + carry
            carry = carry + jnp.sum(lf, axis=1, keepdims=True)

    return pl.pallas_call(
        body, name=name,
        out_shape=jax.ShapeDtypeStruct((H, S), F32),
        compiler_params=pltpu.CompilerParams(vmem_limit_bytes=VMEM_LIMIT),
    )(logit_t, b_col)


def _forget_bwd(logit_t, b_col, d_f, *, name, blk=512):
    H, S = logit_t.shape
    blk = _tile(S, blk)

    def body(l_ref, b_ref, df_ref, dl_ref, db_ref):
        r_i = lax.broadcasted_iota(jnp.int32, (blk, blk), 0)
        c_i = lax.broadcasted_iota(jnp.int32, (blk, blk), 1)
        fromon = (r_i >= c_i).astype(BF16)
        carry = jnp.zeros((H, 1), F32)
        db = jnp.zeros((H, 1), F32)
        for j in reversed(range(S // blk)):
            sl = slice(j * blk, (j + 1) * blk)
            dfv = df_ref[:, sl]
            d_lf = _split3_dot(dfv, fromon) + carry
            carry = carry + jnp.sum(dfv, axis=1, keepdims=True)
            u = l_ref[:, sl] + b_ref[...]
            _, lsn = _log_sigmoid_pair(u)
            dl = d_lf * jnp.exp(lsn)
            dl_ref[:, sl] = dl
            db = db + jnp.sum(dl, axis=1, keepdims=True)
        db_ref[...] = db

    return pl.pallas_call(
        body, name=name,
        out_shape=[jax.ShapeDtypeStruct((H, S), F32), jax.ShapeDtypeStruct((H, 1), F32)],
        compiler_params=pltpu.CompilerParams(vmem_limit_bytes=VMEM_LIMIT),
    )(logit_t, b_col, d_f)


def _sigmoid(t):
    return 1.0 / (1.0 + jnp.exp(-t))


def _gate_fwd(o3, w3, proj, D, *, name, tm=256):
    S = proj.shape[0]
    tm = _tile(S, tm)

    def body(o0, o1, o2, w0, w1, w2, g0, g1, g2, out_ref):
        acc = None
        for o_ref, w_ref, g_ref in ((o0, w0, g0), (o1, w1, g1), (o2, w2, g2)):
            t = _sigmoid(g_ref[...]) * _dot(o_ref[...], w_ref[...])
            acc = t if acc is None else acc + t
        out_ref[...] = acc.astype(out_ref.dtype)

    ospec = lambda d: pl.BlockSpec((tm, d), lambda i: (i, 0))
    wspec = lambda w: pl.BlockSpec(w.shape, lambda i: (0, 0))
    gspec = lambda j: pl.BlockSpec((tm, D), lambda i: (i, j))
    return pl.pallas_call(
        body, name=name, grid=(S // tm,),
        in_specs=[ospec(o.shape[1]) for o in o3] + [wspec(w) for w in w3] + [gspec(j) for j in range(3)],
        out_specs=pl.BlockSpec((tm, D), lambda i: (i, 0)),
        out_shape=jax.ShapeDtypeStruct((S, D), BF16),
        compiler_params=_params(("parallel",)),
    )(*o3, *w3, proj, proj, proj)


def _gate_bwd(o3, w3, proj, dmerged, D, *, name, tm=256):
    S = proj.shape[0]
    tm = _tile(S, tm)

    def body(o0, o1, o2, w0, w1, w2, g0, g1, g2, dm_ref, dg_ref, db0, db1, db2, do0, do1, do2):
        dm = dm_ref[...]
        for j, (o_ref, w_ref, g_ref, db_ref, do_ref) in enumerate(
                ((o0, w0, g0, db0, do0), (o1, w1, g1, db1, do1), (o2, w2, g2, db2, do2))):
            s = _sigmoid(g_ref[...])
            br = _dot(o_ref[...], w_ref[...])
            dg_ref[:, j * D:(j + 1) * D] = (dm * br * s * (1.0 - s)).astype(dg_ref.dtype)
            dbr = (dm * s).astype(BF16)
            db_ref[...] = dbr
            do_ref[...] = _dot(dbr, w_ref[...], _NT)

    ospec = lambda d: pl.BlockSpec((tm, d), lambda i: (i, 0))
    wspec = lambda w: pl.BlockSpec(w.shape, lambda i: (0, 0))
    gspec = lambda j: pl.BlockSpec((tm, D), lambda i: (i, j))
    dspec = pl.BlockSpec((tm, D), lambda i: (i, 0))
    return pl.pallas_call(
        body, name=name, grid=(S // tm,),
        in_specs=[ospec(o.shape[1]) for o in o3] + [wspec(w) for w in w3] + [gspec(j) for j in range(3)] + [dspec],
        out_specs=[pl.BlockSpec((tm, 3 * D), lambda i: (i, 0))] + [dspec] * 3 + [ospec(o.shape[1]) for o in o3],
        out_shape=[jax.ShapeDtypeStruct((S, 3 * D), BF16)] + [jax.ShapeDtypeStruct((S, D), BF16)] * 3
        + [jax.ShapeDtypeStruct((S, o.shape[1]), F32) for o in o3],
        compiler_params=_params(("parallel",)),
    )(*o3, *w3, proj, proj, proj, dmerged)


def _loss_sum(dy, D, *, name):
    R, C = dy.shape
    tr = _row_tile(R, C, 2)

    def body(dy_ref, out_ref):
        @pl.when(pl.program_id(0) == 0)
        def _():
            out_ref[...] = jnp.zeros_like(out_ref)

        v = dy_ref[...]
        out_ref[...] += (0.5 * D) * jnp.sum(v * v)

    return pl.pallas_call(
        body, name=name, grid=(R // tr,),
        in_specs=[pl.BlockSpec((tr, C), lambda i: (i, 0))],
        out_specs=pl.BlockSpec((8, 128), lambda i: (0, 0)),
        out_shape=jax.ShapeDtypeStruct((8, 128), F32),
        compiler_params=_params(("arbitrary",)),
    )(dy)[0, 0]


def _adamw_math(w, g, m, v):
    m2 = ADAM_B1 * m + (1.0 - ADAM_B1) * g
    v2 = ADAM_B2 * v + (1.0 - ADAM_B2) * (g * g)
    m_hat = m2 / (1.0 - ADAM_B1 ** ADAM_STEP)
    v_hat = v2 / (1.0 - ADAM_B2 ** ADAM_STEP)
    delta = -ADAM_LR * (m_hat / (jnp.sqrt(v_hat) + ADAM_EPS) + ADAM_WD * w)
    return delta, m2, v2


def _adamw(w, g, m, v, *, name):
    return _ew(_adamw_math, [w, g, m, v], (F32, F32, F32), name=name)


def _adamw_small(w, parts, m, v, *, name):
    n = parts.shape[0]

    def body(w_ref, p_ref, m_ref, v_ref, g_ref, d_ref, m2_ref, v2_ref):
        g = p_ref[0]
        for i in range(1, n):
            g = g + p_ref[i]
        g_ref[...] = g
        d_ref[...], m2_ref[...], v2_ref[...] = _adamw_math(w_ref[...], g, m_ref[...], v_ref[...])

    shp = jax.ShapeDtypeStruct(w.shape, F32)
    return pl.pallas_call(body, name=name, out_shape=[shp] * 4)(w, parts, m, v)


_ANY = pl.BlockSpec(memory_space=pl.ANY)


def _mesh_place():
    x, y, c = lax.axis_index("x"), lax.axis_index("y"), lax.axis_index("c")
    chips = [(1 - x, y), (x, 1 - y), (1 - x, 1 - y)]
    return x, y, c, chips


def _remote(src, dst, sems, i, to):
    send_sems, recv_sems = sems
    return pltpu.make_async_remote_copy(src_ref=src, dst_ref=dst, send_sem=send_sems.at[i], recv_sem=recv_sems.at[i],
                                        device_id=to, device_id_type=MESH_ID)


def _gather_weights(shards, *, name):
    n = len(shards)

    def body(*refs):
        ins, outs = refs[:n], refs[n:2 * n]
        sems = refs[2 * n:2 * n + 2]
        x, y, c, chips = _mesh_place()
        me = 2 * x + y
        sibling = (x, y, 1 - c)
        sent = []
        for w in range(n):
            for j, chip in enumerate(chips):
                cp = _remote(ins[w].at[c], outs[w].at[me, c], sems, 6 * w + j, (chip[0], chip[1], c))
                cp.start()
                sent.append(cp)
        for w in range(n):
            for j, chip in enumerate(chips):
                got = outs[w].at[2 * chip[0] + chip[1], c]
                _remote(got, got, sems, 6 * w + j, sibling).wait_recv()
                cp = _remote(got, got, sems, 6 * w + 3 + j, sibling)
                cp.start()
                sent.append(cp)
        for w in range(n):
            for j, chip in enumerate(chips):
                got = outs[w].at[2 * chip[0] + chip[1], 1 - c]
                _remote(got, got, sems, 6 * w + 3 + j, sibling).wait_recv()
        for cp in sent:
            cp.wait_send()

    outs = pl.pallas_call(
        body, name=name,
        in_specs=[_ANY] * n, out_specs=[_ANY] * n,
        out_shape=[jax.ShapeDtypeStruct((N_CHIPS,) + s.shape, s.dtype) for s in shards],
        scratch_shapes=[pltpu.SemaphoreType.DMA((6 * n,)), pltpu.SemaphoreType.DMA((6 * n,))],
    )(*shards)
    me = 2 * lax.axis_index("x") + lax.axis_index("y")
    return [lax.dynamic_update_index_in_dim(o, s, me, 0) for o, s in zip(outs, shards)]


def _forward_halves(gathered, *, name):
    n = len(gathered)

    def body(*refs):
        ins, outs = refs[:n], refs[n:2 * n]
        sems = refs[2 * n:2 * n + 2]
        x, y, c, chips = _mesh_place()
        sibling = (x, y, 1 - c)
        sent = []
        for w in range(n):
            for j, chip in enumerate(chips):
                peer = 2 * chip[0] + chip[1]
                cp = _remote(ins[w].at[peer, c], outs[w].at[peer, c], sems, 3 * w + j, sibling)
                cp.start()
                sent.append(cp)
        for w in range(n):
            for j, chip in enumerate(chips):
                land = outs[w].at[2 * chip[0] + chip[1], 1 - c]
                _remote(land, land, sems, 3 * w + j, sibling).wait_recv()
        for cp in sent:
            cp.wait_send()

    return pl.pallas_call(
        body, name=name,
        in_specs=[_ANY] * n, out_specs=[_ANY] * n,
        out_shape=[jax.ShapeDtypeStruct(g.shape, g.dtype) for g in gathered],
        input_output_aliases={w: w for w in range(n)},
        scratch_shapes=[pltpu.SemaphoreType.DMA((3 * n,)), pltpu.SemaphoreType.DMA((3 * n,))],
    )(*gathered)


def _exchange_siblings(grads, *, name):
    n = len(grads)

    def body(*refs):
        ins, got = refs[:n], refs[n:2 * n]
        sems = refs[2 * n:2 * n + 2]
        x, y, c, _ = _mesh_place()
        sibling = (x, y, 1 - c)
        sent = []
        for w in range(n):
            for s in range(N_CHIPS):
                cp = _remote(ins[w].at[s, 1 - c], got[w].at[s], sems, N_CHIPS * w + s, sibling)
                cp.start()
                sent.append(cp)
        for w in range(n):
            for s in range(N_CHIPS):
                _remote(got[w].at[s], got[w].at[s], sems, N_CHIPS * w + s, sibling).wait_recv()
        for cp in sent:
            cp.wait_send()

    n_sem = N_CHIPS * n
    got = pl.pallas_call(
        body, name=name,
        in_specs=[_ANY] * n, out_specs=[_ANY] * n,
        out_shape=[jax.ShapeDtypeStruct((N_CHIPS,) + g.shape[2:], g.dtype) for g in grads],
        scratch_shapes=[pltpu.SemaphoreType.DMA((n_sem,)), pltpu.SemaphoreType.DMA((n_sem,))],
    )(*grads)
    c = lax.axis_index("c")
    return [lax.dynamic_index_in_dim(g, c, 1, keepdims=False) for g in grads], got


def _exchange_chips(parts, *, name):
    n = len(parts)

    def body(*refs):
        ins, got = refs[:n], refs[n:2 * n]
        sems = refs[2 * n:2 * n + 2]
        x, y, c, chips = _mesh_place()
        sent = []
        for w in range(n):
            for j, chip in enumerate(chips):
                cp = _remote(ins[w].at[2 * chip[0] + chip[1]], got[w].at[j], sems, 3 * w + j, (chip[0], chip[1], c))
                cp.start()
                sent.append(cp)
        for w in range(n):
            for j in range(3):
                _remote(got[w].at[j], got[w].at[j], sems, 3 * w + j, (x, y, c)).wait_recv()
        for cp in sent:
            cp.wait_send()

    got = pl.pallas_call(
        body, name=name,
        in_specs=[_ANY] * n, out_specs=[_ANY] * n,
        out_shape=[jax.ShapeDtypeStruct((3,) + p.shape[1:], p.dtype) for p in parts],
        scratch_shapes=[pltpu.SemaphoreType.DMA((3 * n,)), pltpu.SemaphoreType.DMA((3 * n,))],
    )(*parts)
    me = 2 * lax.axis_index("x") + lax.axis_index("y")
    return [lax.dynamic_index_in_dim(p, me, 0, keepdims=False) for p in parts], got


def _share_halves(halves, small):
    n = len(halves)

    def body(*refs):
        ins, small_ref = refs[:n], refs[n]
        outs, small_out = refs[n + 1:2 * n + 1], refs[2 * n + 1]
        sems = refs[2 * n + 2:2 * n + 4]
        x, y, c, chips = _mesh_place()
        sibling = (x, y, 1 - c)
        me = 4 * x + 2 * y + c
        sent = [_remote(ins[w], outs[w].at[c], sems, w, sibling) for w in range(n)]
        peers = [sibling] + [(ch[0], ch[1], cc) for ch in chips for cc in (c, 1 - c)]
        sent += [_remote(small_ref, small_out.at[me], sems, n + j, peer) for j, peer in enumerate(peers)]
        for cp in sent:
            cp.start()
        for w in range(n):
            _remote(outs[w].at[1 - c], outs[w].at[1 - c], sems, w, sibling).wait_recv()
        for j, peer in enumerate(peers):
            frm = small_out.at[4 * peer[0] + 2 * peer[1] + peer[2]]
            _remote(frm, frm, sems, n + j, peer).wait_recv()
        for cp in sent:
            cp.wait_send()

    n_sem = n + 7
    outs = pl.pallas_call(
        body, name="share_halves",
        in_specs=[_ANY] * (n + 1), out_specs=[_ANY] * (n + 1),
        out_shape=[jax.ShapeDtypeStruct((2,) + h.shape, h.dtype) for h in halves]
        + [jax.ShapeDtypeStruct((8,) + small.shape, small.dtype)],
        scratch_shapes=[pltpu.SemaphoreType.DMA((n_sem,)), pltpu.SemaphoreType.DMA((n_sem,))],
    )(*halves, small)
    c = lax.axis_index("c")
    me = 4 * lax.axis_index("x") + 2 * lax.axis_index("y") + c
    return ([lax.dynamic_update_index_in_dim(o, h, c, 0) for o, h in zip(outs[:n], halves)],
            lax.dynamic_update_index_in_dim(outs[n], small, me, 0))


EARLY = ("w_ff_down", "w_ff_up", "w_out", "w_branch_sb", "w_branch_fox", "w_branch_mem")


def _local_step(x, mem, target, small, W, gather_rest=None, reduce_early=None, reduce_late=None):
    S, D = x.shape
    o_gate, o_qkv, o_mq, o_f = 0, 3 * D, 3 * D + 2 * 3 * D_SB, 3 * D + 2 * 3 * D_SB + D_MEM
    tq = 256

    h = _rmsnorm_fwd(x, small["g_mix_norm"], BF16, name="mix_norm")
    proj = _mm(h, W["w_in"], name="in_proj", tn=768)
    blk = lambda j: (proj, (o_qkv + j * D_SB) // LANES)
    sb_q, sb_k, sb_v, fx_q, fx_k, fx_v = [blk(j) for j in range(6)]
    m_q = (proj, o_mq // LANES)
    f_logit_t = proj[:, o_f:o_f + FOX_HEADS].T
    b_col = small["b_forget"].reshape(FOX_HEADS, 1)
    lanes = lambda g: jnp.tile(g, (1, LANES // g.shape[1]))
    g_fq, g_fk, g_mq, g_mk = [lanes(small[k]) for k in ("g_fox_q", "g_fox_k", "g_mem_q", "g_mem_k")]

    if gather_rest is None:
        o_sb, sb_tot = _sbl_fwd(sb_q, sb_k, sb_v, width=D_SB, hd=HD, name="sb_fwd", tq=tq)
    else:
        comm, finish = gather_rest
        o_sb, sb_tot, *landed = _sbl_fwd(sb_q, sb_k, sb_v, width=D_SB, hd=HD, name="sb_fwd", tq=tq, comm=comm)
        W = {**W, **finish(landed)}

    fq = _hnorm_fwd(fx_q, g_fq, width=D_FOX, hd=HD, name="fox_q_norm")
    fk = _hnorm_fwd(fx_k, g_fk, width=D_FOX, hd=HD, name="fox_k_norm")
    f_cum = _forget_fwd(f_logit_t, b_col, name="forget_fwd")
    tkf = _tile(S, tq)
    f_bias = (f_cum.reshape(FOX_HEADS, S, 1), f_cum.reshape(FOX_HEADS, S // tkf, 1, tkf))
    o_fox, fox_lse = _sml_fwd((fq, 0), (fk, 0), fx_v, f_bias, width=D_FOX, hd=HD, causal=True, name="fox_fwd",
                              tq=tq, tk=tq)

    mh = _rmsnorm_fwd(mem, small["g_mem_norm"], BF16, name="mem_norm")
    mkv = _mm(mh, W["w_mem_kv"], name="mem_kv")
    mv = (mkv, D_MEM // LANES)
    mq = _hnorm_fwd(m_q, g_mq, width=D_MEM, hd=MEM_HD, name="mem_q_norm")
    mk = _hnorm_fwd((mkv, 0), g_mk, width=D_MEM, hd=MEM_HD, name="mem_k_norm")
    o_mem, mem_lse = _sml_fwd((mq, 0), (mk, 0), mv, width=D_MEM, hd=MEM_HD, causal=False, name="mem_fwd", tq=tq, tk=256)

    o3 = [o_sb, o_fox, o_mem]
    w3 = [W["w_branch_sb"], W["w_branch_fox"], W["w_branch_mem"]]
    merged = _gate_fwd(o3, w3, proj, D, name="gate_fwd")
    x1 = _mm(merged, W["w_out"], name="out_proj", extras=(x,), epilogue=lambda acc, res: (res + acc,))
    h2 = _rmsnorm_fwd(x1, small["g_mlp_norm"], BF16, name="mlp_norm")

    def relu2(acc):
        u = jnp.maximum(acc, 0.0)
        return u, u * u

    u, a = _mm(h2, W["w_ff_up"], name="ff_up", out_dtypes=(BF16, BF16), epilogue=relu2)
    dy = _mm(a, W["w_ff_down"], name="ff_down", extras=(x1, target),
             epilogue=lambda acc, res, tgt: ((res + acc - tgt) * (1.0 / D),))
    loss = _loss_sum(dy, D, name="loss")

    G = {}
    du = _mm(dy, W["w_ff_down"], name="d_ff_act", tb=True, out_dtypes=(BF16,), extras=(u,),
             epilogue=lambda acc, uu: (acc * (2.0 * uu.astype(F32)),))
    G["w_ff_down"] = _mm(a, dy, name="d_w_ff_down", ta=True, out_dtypes=(BF16,))
    G["w_ff_up"] = _mm(h2, du, name="d_w_ff_up", ta=True, out_dtypes=(BF16,))
    dh2 = _mm(du, W["w_ff_up"], name="d_mlp_in", tb=True)
    dx1, dg_mlp = _rmsnorm_bwd(x1, small["g_mlp_norm"], dh2, add=dy, name="d_mlp_norm")
    dmerged = _mm(dx1, W["w_out"], name="d_merged", tb=True)
    G["w_out"] = _mm(merged, dx1, name="d_w_out", ta=True, out_dtypes=(BF16,))
    dgate, db0, db1, db2, do_sb, do_fox, do_mem = _gate_bwd(o3, w3, proj, dmerged, D, name="gate_bwd")
    for nm, o, db in zip(("w_branch_sb", "w_branch_fox", "w_branch_mem"), o3, (db0, db1, db2)):
        G[nm] = _mm(o, db, name="d_" + nm, ta=True, out_dtypes=(BF16,))

    early = {}
    if reduce_early is None:
        dsb = _sbl_bwd(sb_q, sb_k, sb_v, (do_sb, 0), (sb_tot, 0), width=D_SB, hd=HD, name="sb_bwd", tq=tq)
    else:
        comm, finish = reduce_early({k: G.pop(k) for k in EARLY})
        *dsb, = _sbl_bwd(sb_q, sb_k, sb_v, (do_sb, 0), (sb_tot, 0), width=D_SB, hd=HD, name="sb_bwd", tq=tq, comm=comm)
        dsb, landed = dsb[:3], dsb[3:]
        early = finish(landed)
    dfq, dfk, dfv, df_row, df_col = _sml_bwd((fq, 0), (fk, 0), fx_v, (o_fox, 0), (fox_lse, 0), (do_fox, 0), f_bias,
                                             width=D_FOX, hd=HD, causal=True, name="fox_bwd", tq=tq, tk=tq)
    dfx_q, dg_fox_q = _hnorm_bwd(fx_q, g_fq, dfq, width=D_FOX, hd=HD, name="d_fox_q_norm")
    dfx_k, dg_fox_k = _hnorm_bwd(fx_k, g_fk, dfk, width=D_FOX, hd=HD, name="d_fox_k_norm")
    d_fcum = df_row.reshape(FOX_HEADS, S) - df_col.reshape(FOX_HEADS, S)
    d_flogit_t, db_forget = _forget_bwd(f_logit_t, b_col, d_fcum, name="forget_bwd")

    dmq_n, dmk_n, dmv = _sml_bwd((mq, 0), (mk, 0), mv, (o_mem, 0), (mem_lse, 0), (do_mem, 0), width=D_MEM, hd=MEM_HD,
                                 causal=False, name="mem_bwd", tq=tq, tk=256)
    dm_q, dg_mem_q = _hnorm_bwd(m_q, g_mq, dmq_n, width=D_MEM, hd=MEM_HD, name="d_mem_q_norm")
    dmk_raw, dg_mem_k = _hnorm_bwd((mkv, 0), g_mk, dmk_n, width=D_MEM, hd=MEM_HD, name="d_mem_k_norm")
    dmkv = jnp.concatenate([dmk_raw, dmv.astype(BF16)], axis=1)
    G["w_mem_kv"] = _mm(mh, dmkv, name="d_w_mem_kv", ta=True, out_dtypes=(BF16,))
    dmh = _mm(dmkv, W["w_mem_kv"], name="d_mem_h", tb=True)
    _, dg_mem = _rmsnorm_bwd(mem, small["g_mem_norm"], dmh, name="d_mem_norm")
    dg_fox_q, dg_fox_k = dg_fox_q[:, :HD], dg_fox_k[:, :HD]

    dproj = jnp.concatenate(
        [dgate] + [t.astype(BF16) for t in (*dsb, dfx_q, dfx_k, dfv, dm_q)]
        + [d_flogit_t.T.astype(BF16), jnp.zeros((S, F_PAD - FOX_HEADS), BF16)], axis=1)
    g_w_in = _mm(h, dproj, name="d_w_in", ta=True, out_dtypes=(BF16,), tn=768)
    if reduce_late is None:
        G["w_in"] = g_w_in
        dh = _mm(dproj, W["w_in"], name="d_mix_in", tb=True, tk=768)
    else:
        comm, finish = reduce_late({"w_in": g_w_in, "w_mem_kv": G.pop("w_mem_kv")})
        dh, *landed = _mm(dproj, W["w_in"], name="d_mix_in", tb=True, tk=768, comm=comm)
        early.update(finish(landed))
    grad_x, dg_mix = _rmsnorm_bwd(x, small["g_mix_norm"], dh, add=dx1, name="d_mix_norm")

    small_grads = dict(g_mix_norm=dg_mix, g_mem_norm=dg_mem, b_forget=db_forget.reshape(1, FOX_HEADS),
                       g_fox_q=dg_fox_q, g_fox_k=dg_fox_k, g_mem_q=dg_mem_q, g_mem_k=dg_mem_k, g_mlp_norm=dg_mlp)
    return loss, grad_x, G, small_grads, early


BIG = ("w_in", "w_mem_kv", "w_branch_sb", "w_branch_fox", "w_branch_mem", "w_out", "w_ff_up", "w_ff_down")
COLUMN_SHARDED = ("w_in", "w_branch_sb", "w_branch_fox", "w_branch_mem", "w_ff_up")
SMALL = ("g_mix_norm", "g_mem_norm", "b_forget", "g_fox_q", "g_fox_k", "g_mem_q", "g_mem_k", "g_mlp_norm")
ORDER = ("g_mix_norm", "g_mem_norm", "w_in", "b_forget", "g_fox_q", "g_fox_k", "g_mem_q", "g_mem_k", "w_mem_kv",
         "w_branch_sb", "w_branch_fox", "w_branch_mem", "w_out", "g_mlp_norm", "w_ff_up", "w_ff_down")


def _unshard(name, gathered):
    n, _, rh, c = gathered.shape
    t = gathered.reshape(n, 2 * rh, c)
    if name in COLUMN_SHARDED:
        return t.transpose(1, 0, 2).reshape(2 * rh, n * c)
    return t.reshape(n * 2 * rh, c)


def _reshard(name, full):
    if name in COLUMN_SHARDED:
        r, c = full.shape
        t = full.reshape(r, N_CHIPS, c // N_CHIPS).transpose(1, 0, 2)
    else:
        r, c = full.shape[0] // N_CHIPS, full.shape[1]
        t = full.reshape(N_CHIPS, r, c)
    return t.reshape(N_CHIPS, 2, t.shape[1] // 2, t.shape[2])


def _pad_in_proj(w_in, D):
    n_qkv = 6 * D_SB
    o_mq = n_qkv + FOX_HEADS
    o_gate = o_mq + D_MEM
    return jnp.concatenate([w_in[:, o_gate:], w_in[:, :n_qkv], w_in[:, o_mq:o_gate], w_in[:, n_qkv:o_mq],
                            jnp.zeros((w_in.shape[0], F_PAD - FOX_HEADS), w_in.dtype)], axis=1)


def _unpad_in_proj(g, D):
    n_qkv = 6 * D_SB
    o_qkv, o_mq, o_f = 3 * D, 3 * D + n_qkv, 3 * D + n_qkv + D_MEM
    return jnp.concatenate([g[:, o_qkv:o_mq], g[:, o_f:o_f + FOX_HEADS], g[:, o_mq:o_f], g[:, :o_qkv]], axis=1)


def _pack_small(vals):
    width = max(vals[k].shape[1] for k in SMALL)
    return jnp.concatenate([jnp.pad(vals[k].astype(F32), ((0, 0), (0, width - vals[k].shape[1]))) for k in SMALL], axis=0)


def _unpack_small(packed, like):
    return {k: packed[i:i + 1, :like[k].shape[1]] for i, k in enumerate(SMALL)}


def kernel(x, mem, g_mix_norm, g_mem_norm, w_in, b_forget, g_fox_q, g_fox_k, g_mem_q, g_mem_k, w_mem_kv, w_branch_sb, w_branch_fox, w_branch_mem, w_out, g_mlp_norm, w_ff_up, w_ff_down, loss_target, m_g_mix_norm, m_g_mem_norm, m_w_in, m_b_forget, m_g_fox_q, m_g_fox_k, m_g_mem_q, m_g_mem_k, m_w_mem_kv, m_w_branch_sb, m_w_branch_fox, m_w_branch_mem, m_w_out, m_g_mlp_norm, m_w_ff_up, m_w_ff_down, v_g_mix_norm, v_g_mem_norm, v_w_in, v_b_forget, v_g_fox_q, v_g_fox_k, v_g_mem_q, v_g_mem_k, v_w_mem_kv, v_w_branch_sb, v_w_branch_fox, v_w_branch_mem, v_w_out, v_g_mlp_norm, v_w_ff_up, v_w_ff_down):
    given = dict(locals())
    D = x.shape[-1]
    weights = {k: given[k] for k in ORDER}
    moms = {k: given["m_" + k] for k in ORDER}
    vars_ = {k: given["v_" + k] for k in ORDER}

    me_chip = 2 * lax.axis_index("x") + lax.axis_index("y")

    shards = {}
    for k in BIG:
        w = weights[k][0].astype(BF16)
        shards[k] = w.reshape(2, w.shape[0] // 2, w.shape[1])
    W = {"w_in": _pad_in_proj(_unshard("w_in", _gather_weights([shards["w_in"]], name="gather_w_in")[0]), D)}
    rest = [k for k in BIG if k != "w_in"]

    def finish_weights(landed):
        full = _forward_halves(landed, name="forward_halves")
        full = [lax.dynamic_update_index_in_dim(o, shards[k], me_chip, 0) for k, o in zip(rest, full)]
        return {k: _unshard(k, g) for k, g in zip(rest, full)}

    def sum_pairs(names, own, sib):
        add2 = lambda p, q: (p.astype(F32) + q.astype(F32),)
        out = []
        for k, p, q in zip(names, own, sib):
            n, r, c = p.shape
            out.append(_ew(add2, [p.reshape(n * r, c), q.reshape(n * r, c)], (BF16,), name="sum_pair_" + k).reshape(n, r, c))
        return out

    def sum_chips(names, parts, got):
        add4 = lambda p, q0, q1, q2: (((p.astype(F32) + q0.astype(F32)) + q1.astype(F32)) + q2.astype(F32),)
        own = [lax.dynamic_index_in_dim(p, me_chip, 0, keepdims=False) for p in parts]
        return {k: _ew(add4, [p, q[0], q[1], q[2]], (F32,), name="sum_chips_" + k) for k, p, q in zip(names, own, got)}

    def reducer(tag):
        def reduce(grads):
            names = list(grads)
            if "w_in" in grads:
                grads = {**grads, "w_in": _unpad_in_proj(grads["w_in"], D)}
            own, sib = _exchange_siblings([_reshard(k, grads[k]) for k in names], name="exchange_siblings_" + tag)
            parts = sum_pairs(names, own, sib)
            return _ChipExchange("scatter", parts), functools.partial(sum_chips, names, parts)
        return reduce

    small = {k: weights[k] for k in SMALL}
    loss_part, grad_x, G, small_grads, halves = _local_step(
        x[0], mem[0], loss_target[0], small, W,
        gather_rest=(_ChipExchange("gather", [shards[k] for k in rest]), finish_weights),
        reduce_early=reducer("early"), reduce_late=reducer("late"))
    assert not G, list(G)
    reduced, small_parts = _share_halves([halves[k] for k in BIG], _pack_small(small_grads))

    grads, deltas, new_m, new_v = {}, {}, {}, {}
    for k, g in zip(BIG, reduced):
        shp = weights[k].shape
        g2 = g.reshape(shp[1], shp[2])
        d, m2, v2 = _adamw(weights[k][0], g2, moms[k][0], vars_[k][0], name="adamw_" + k)
        grads[k], deltas[k], new_m[k], new_v[k] = g2.reshape(shp), d.reshape(shp), m2.reshape(shp), v2.reshape(shp)
    sg, sd, sm, sv = _adamw_small(_pack_small(small), small_parts, _pack_small({k: moms[k] for k in SMALL}),
                                  _pack_small({k: vars_[k] for k in SMALL}), name="adamw_small")
    for dst, packed in ((grads, sg), (deltas, sd), (new_m, sm), (new_v, sv)):
        dst.update(_unpack_small(packed, small))

    loss = lax.psum(loss_part, ("x", "y", "c"))
    return (loss, grad_x[None], *[grads[k] for k in ORDER], *[deltas[k] for k in ORDER],
            *[new_m[k] for k in ORDER], *[new_v[k] for k in ORDER])
```

```python
import functools

import jax
import jax.numpy as jnp
from jax import lax
from jax.experimental import pallas as pl
from jax.experimental.pallas import tpu as pltpu

F32 = jnp.float32
BF16 = jnp.bfloat16
MESH_ID = pl.DeviceIdType.MESH

HD = 64
SB_HEADS = 8
FOX_HEADS = 8
MEM_HEADS = 4
MEM_HD = 128
D_SB = SB_HEADS * HD
D_FOX = FOX_HEADS * HD
D_MEM = MEM_HEADS * MEM_HD
EPS = 1e-6
NEG_INF = -1e30

ADAM_LR = 0.001
ADAM_B1 = 0.9
ADAM_B2 = 0.999
ADAM_EPS = 1e-08
ADAM_WD = 0.01
ADAM_STEP = 10

N_CHIPS = 4
VMEM_LIMIT = 56 * 1024 * 1024

F_PAD = 256


def _tile(n, target, align=128):
    if n <= target:
        return n
    best = None
    t = align
    while t <= target:
        if n % t == 0:
            best = t
        t += align
    assert best is not None, (n, target, align)
    return best


def _params(sem):
    return pltpu.CompilerParams(dimension_semantics=sem, vmem_limit_bytes=VMEM_LIMIT)


def _mm(a, b, *, name, ta=False, tb=False, out_dtypes=(F32,), epilogue=None, extras=(),
        tm=512, tn=512, tk=1024, comm=None):
    if ta:
        K, M = a.shape
    else:
        M, K = a.shape
    if tb:
        N, K2 = b.shape
    else:
        K2, N = b.shape
    assert K == K2, (a.shape, b.shape, ta, tb)
    tm, tn, tk = _tile(M, tm), _tile(N, tn), _tile(K, tk)
    nk = K // tk
    n_extra, n_out = len(extras), len(out_dtypes)
    if epilogue is None:
        epilogue = lambda acc: (acc,)
    dims = (((0 if ta else 1,), (1 if tb else 0,)), ((), ()))

    gm, gn = M // tm, N // tn

    def body(*refs):
        i, j, k = pl.program_id(0), pl.program_id(1), pl.program_id(2)
        got = _carry(comm, 2 + n_extra, n_out, (i == 0) & (j == 0) & (k == 0),
                     (i == gm - 1) & (j == gn - 1) & (k == nk - 1), refs)
        (a_ref, b_ref, *extra_refs), out_refs = got[0], got[1]
        acc_ref = refs[-1]

        @pl.when(k == 0)
        def _():
            acc_ref[...] = jnp.zeros_like(acc_ref)

        acc_ref[...] += lax.dot_general(a_ref[...].astype(BF16), b_ref[...].astype(BF16), dims,
                                        preferred_element_type=F32)

        @pl.when(k == nk - 1)
        def _():
            outs = epilogue(acc_ref[...], *[r[...] for r in extra_refs])
            for o_ref, o in zip(out_refs, outs):
                o_ref[...] = o.astype(o_ref.dtype)

        got[2]()

    a_spec = pl.BlockSpec((tk, tm), lambda i, j, k: (k, i)) if ta else pl.BlockSpec((tm, tk), lambda i, j, k: (i, k))
    b_spec = pl.BlockSpec((tn, tk), lambda i, j, k: (j, k)) if tb else pl.BlockSpec((tk, tn), lambda i, j, k: (k, j))
    mn_spec = pl.BlockSpec((tm, tn), lambda i, j, k: (i, j))
    c_ins, c_in_specs, c_out_specs, c_out_shape, c_scratch = _comm_args(comm)
    sem = ("parallel", "parallel", "arbitrary") if comm is None else ("arbitrary",) * 3
    outs = pl.pallas_call(
        body, name=name,
        grid=(gm, gn, nk),
        in_specs=[a_spec, b_spec] + [mn_spec] * n_extra + c_in_specs,
        out_specs=[mn_spec] * n_out + c_out_specs,
        out_shape=[jax.ShapeDtypeStruct((M, N), dt) for dt in out_dtypes] + c_out_shape,
        scratch_shapes=c_scratch + [pltpu.VMEM((tm, tn), F32)],
        compiler_params=_params(sem),
    )(a, b, *extras, *c_ins)
    return outs if len(outs) > 1 else outs[0]


def _row_tile(rows, cols, n_arrays):
    budget = 10 * 1024 * 1024
    cols_padded = -(-cols // 128) * 128
    target = max(16, budget // (cols_padded * 4 * n_arrays * 2))
    return _tile(rows, target, align=16)


def _ew(fn, ins, out_dtypes, *, name):
    R, C = ins[0].shape
    n_in, n_out = len(ins), len(out_dtypes)
    tr = _row_tile(R, C, n_in + n_out)

    def body(*refs):
        outs = fn(*[r[...] for r in refs[:n_in]])
        for o_ref, o in zip(refs[n_in:], outs):
            o_ref[...] = o.astype(o_ref.dtype)

    spec = pl.BlockSpec((tr, C), lambda i: (i, 0))
    outs = pl.pallas_call(
        body, name=name, grid=(R // tr,),
        in_specs=[spec] * n_in, out_specs=[spec] * n_out,
        out_shape=[jax.ShapeDtypeStruct((R, C), dt) for dt in out_dtypes],
        compiler_params=_params(("parallel",)),
    )(*ins)
    return outs if n_out > 1 else outs[0]


def _rmsnorm_fwd(x, g, out_dtype, *, name):
    R, d = x.shape
    tr = _row_tile(R, d, 3)

    def body(x_ref, g_ref, o_ref):
        xv = x_ref[...].astype(F32)
        r = lax.rsqrt(jnp.mean(xv * xv, axis=-1, keepdims=True) + EPS)
        o_ref[...] = (xv * r * g_ref[...]).astype(o_ref.dtype)

    return pl.pallas_call(
        body, name=name, grid=(R // tr,),
        in_specs=[pl.BlockSpec((tr, d), lambda i: (i, 0)), pl.BlockSpec((1, d), lambda i: (0, 0))],
        out_specs=pl.BlockSpec((tr, d), lambda i: (i, 0)),
        out_shape=jax.ShapeDtypeStruct((R, d), out_dtype),
        compiler_params=_params(("parallel",)),
    )(x, g)


def _rmsnorm_bwd(x, g, dy, add=None, *, name):
    R, d = x.shape
    has_add = add is not None
    tr = _row_tile(R, d, 5)

    def body(*refs):
        x_ref, g_ref, dy_ref = refs[:3]
        add_ref = refs[3] if has_add else None
        dx_ref, dg_ref = refs[-2:]
        xv = x_ref[...].astype(F32)
        dyv = dy_ref[...].astype(F32)
        r = lax.rsqrt(jnp.mean(xv * xv, axis=-1, keepdims=True) + EPS)
        xh = xv * r
        dyg = dyv * g_ref[...]
        c = jnp.mean(dyg * xh, axis=-1, keepdims=True)
        dx = r * (dyg - xh * c)
        if has_add:
            dx = dx + add_ref[...]
        dx_ref[...] = dx

        @pl.when(pl.program_id(0) == 0)
        def _():
            dg_ref[...] = jnp.zeros_like(dg_ref)

        dg_ref[...] += jnp.sum(dyv * xh, axis=0, keepdims=True)

    row = pl.BlockSpec((tr, d), lambda i: (i, 0))
    vec = pl.BlockSpec((1, d), lambda i: (0, 0))
    ins = [x, g, dy] + ([add] if has_add else [])
    return pl.pallas_call(
        body, name=name, grid=(R // tr,),
        in_specs=[row, vec, row] + ([row] if has_add else []),
        out_specs=[row, vec],
        out_shape=[jax.ShapeDtypeStruct((R, d), F32), jax.ShapeDtypeStruct((1, d), F32)],
        compiler_params=_params(("arbitrary",)),
    )(*ins)


_NT = (((1,), (1,)), ((), ()))
_TN = (((0,), (0,)), ((), ()))


def _dot(a, b, dims=(((1,), (0,)), ((), ()))):
    return lax.dot_general(a, b, dims, preferred_element_type=F32)


def _split_dot(x, tri):
    hi = x.astype(BF16)
    lo = (x - hi.astype(F32)).astype(BF16)
    return _dot(hi, tri) + _dot(lo, tri)


def _log_sigmoid_pair(z):
    sp = jnp.log(1.0 + jnp.exp(-jnp.abs(z)))
    return jnp.minimum(z, 0.0) - sp, jnp.minimum(-z, 0.0) - sp


def _sb_fwd(q, k, v, *, name, tq=256):
    H, S, hd = q.shape
    tq = _tile(S, tq)
    tk = tq
    scale = hd ** -0.5

    def body(q_ref, k_ref, v_ref, o_ref, tot_ref):
        qi = pl.program_id(1)
        qv = q_ref[0]
        row = qi * tq + lax.broadcasted_iota(jnp.int32, (tq, tk), 0)
        col0 = lax.broadcasted_iota(jnp.int32, (tq, tk), 1)
        later = (lax.broadcasted_iota(jnp.int32, (tk, tk), 0) > lax.broadcasted_iota(jnp.int32, (tk, tk), 1)).astype(BF16)

        def step(i, carry):
            acc, c_rem = carry
            kb = qi - i
            ks = pl.multiple_of(kb * tk, tk)
            kv = k_ref[0, pl.ds(ks, tk), :]
            vv = v_ref[0, pl.ds(ks, tk), :]
            z = _dot(qv, kv, _NT) * scale
            mask = (col0 + ks) < row
            ls, lsn = _log_sigmoid_pair(z)
            rem = jnp.where(mask, lsn, 0.0)
            after = _split_dot(rem, later) + c_rem
            w = jnp.where(mask, jnp.exp(ls + after), 0.0)
            acc = acc + _dot(w.astype(BF16), vv)
            c_rem = c_rem + jnp.sum(rem, axis=1, keepdims=True)
            return acc, c_rem

        acc, c_rem = lax.fori_loop(0, qi + 1, step, (jnp.zeros((tq, hd), F32), jnp.zeros((tq, 1), F32)))
        o_ref[0] = acc
        tot_ref[0] = c_rem

    qspec = pl.BlockSpec((1, tq, hd), lambda h, i: (h, i, 0))
    kspec = pl.BlockSpec((1, S, hd), lambda h, i: (h, 0, 0))
    return pl.pallas_call(
        body, name=name, grid=(H, S // tq),
        in_specs=[qspec, kspec, kspec],
        out_specs=[qspec, pl.BlockSpec((1, tq, 1), lambda h, i: (h, i, 0))],
        out_shape=[jax.ShapeDtypeStruct((H, S, hd), F32), jax.ShapeDtypeStruct((H, S, 1), F32)],
        compiler_params=_params(("parallel", "arbitrary")),
    )(q, k, v)


def _sb_bwd(q, k, v, do, tot, *, name, tq=256):
    H, S, hd = q.shape
    tq = _tile(S, tq)
    tk = tq
    scale = hd ** -0.5

    def body(q_ref, k_ref, v_ref, do_ref, tot_ref, dq_ref, dk_ref, dv_ref):
        qi = pl.program_id(1)

        @pl.when(qi == 0)
        def _():
            dk_ref[...] = jnp.zeros_like(dk_ref)
            dv_ref[...] = jnp.zeros_like(dv_ref)

        qv = q_ref[0]
        dov = do_ref[0].astype(BF16)
        tot_v = tot_ref[0]
        row = qi * tq + lax.broadcasted_iota(jnp.int32, (tq, tk), 0)
        col0 = lax.broadcasted_iota(jnp.int32, (tq, tk), 1)
        r_i = lax.broadcasted_iota(jnp.int32, (tk, tk), 0)
        c_i = lax.broadcasted_iota(jnp.int32, (tk, tk), 1)
        upto = (r_i <= c_i).astype(BF16)
        before = (r_i < c_i).astype(BF16)

        def step(kb, carry):
            dq, c_rem, c_g = carry
            ks = pl.multiple_of(kb * tk, tk)
            kv = k_ref[0, pl.ds(ks, tk), :]
            vv = v_ref[0, pl.ds(ks, tk), :]
            z = _dot(qv, kv, _NT) * scale
            mask = (col0 + ks) < row
            ls, lsn = _log_sigmoid_pair(z)
            rem = jnp.where(mask, lsn, 0.0)
            after = tot_v - (_split_dot(rem, upto) + c_rem)
            w = jnp.where(mask, jnp.exp(ls + after), 0.0)
            dw = _dot(dov, vv, _NT)
            g = dw * w
            g_before = _split_dot(g, before) + c_g
            beta = jnp.exp(ls)
            dz = jnp.where(mask, g * (1.0 - beta) - beta * g_before, 0.0) * scale
            dzb = dz.astype(BF16)
            dq = dq + _dot(dzb, kv)
            dk_ref[0, pl.ds(ks, tk), :] += _dot(dzb, qv, _TN)
            dv_ref[0, pl.ds(ks, tk), :] += _dot(w.astype(BF16), dov, _TN)
            return dq, c_rem + jnp.sum(rem, axis=1, keepdims=True), c_g + jnp.sum(g, axis=1, keepdims=True)

        zero = jnp.zeros((tq, 1), F32)
        dq, _, _ = lax.fori_loop(0, qi + 1, step, (jnp.zeros((tq, hd), F32), zero, zero))
        dq_ref[0] = dq

    qspec = pl.BlockSpec((1, tq, hd), lambda h, i: (h, i, 0))
    kspec = pl.BlockSpec((1, S, hd), lambda h, i: (h, 0, 0))
    full = jax.ShapeDtypeStruct((H, S, hd), F32)
    return pl.pallas_call(
        body, name=name, grid=(H, S // tq),
        in_specs=[qspec, kspec, kspec, qspec, pl.BlockSpec((1, tq, 1), lambda h, i: (h, i, 0))],
        out_specs=[qspec, kspec, kspec],
        out_shape=[full, full, full],
        compiler_params=_params(("parallel", "arbitrary")),
    )(q, k, v, do, tot)


def _softmax_fwd(q, k, v, bias=None, *, name, causal, tq=256, tk=256):
    H, S, hd = q.shape
    Sk = k.shape[1]
    tq, tk = _tile(S, tq), _tile(Sk, tk)
    if causal:
        assert tq == tk and S == Sk
    nk = Sk // tk
    scale = hd ** -0.5
    has_bias = bias is not None

    def body(*refs):
        q_ref, k_ref, v_ref = refs[:3]
        o_ref, lse_ref = refs[-2:]
        qi = pl.program_id(1)
        qv = q_ref[0]
        row = qi * tq + lax.broadcasted_iota(jnp.int32, (tq, tk), 0)
        col0 = lax.broadcasted_iota(jnp.int32, (tq, tk), 1)
        if has_bias:
            b_row = refs[3][0]

        def step(kb, carry):
            m, l, acc = carry
            ks = pl.multiple_of(kb * tk, tk)
            kv = k_ref[0, pl.ds(ks, tk), :]
            vv = v_ref[0, pl.ds(ks, tk), :]
            z = _dot(qv, kv, _NT) * scale
            if has_bias:
                z = z + b_row - refs[4][0, kb]
            if causal:
                z = jnp.where((col0 + ks) <= row, z, NEG_INF)
            m2 = jnp.maximum(m, jnp.max(z, axis=1, keepdims=True))
            p = jnp.exp(z - m2)
            alpha = jnp.exp(m - m2)
            l = alpha * l + jnp.sum(p, axis=1, keepdims=True)
            acc = alpha * acc + _dot(p.astype(BF16), vv)
            return m2, l, acc

        init = (jnp.full((tq, 1), NEG_INF, F32), jnp.zeros((tq, 1), F32), jnp.zeros((tq, hd), F32))
        m, l, acc = lax.fori_loop(0, (qi + 1) if causal else nk, step, init)
        o_ref[0] = acc / l
        lse_ref[0] = m + jnp.log(l)

    qspec = pl.BlockSpec((1, tq, hd), lambda h, i: (h, i, 0))
    kspec = pl.BlockSpec((1, Sk, hd), lambda h, i: (h, 0, 0))
    vspec = pl.BlockSpec((1, tq, 1), lambda h, i: (h, i, 0))
    in_specs = [qspec, kspec, kspec]
    ins = [q, k, v]
    if has_bias:
        in_specs += [vspec, pl.BlockSpec((1, nk, 1, tk), lambda h, i: (h, 0, 0, 0))]
        ins += list(bias)
    return pl.pallas_call(
        body, name=name, grid=(H, S // tq),
        in_specs=in_specs, out_specs=[qspec, vspec],
        out_shape=[jax.ShapeDtypeStruct((H, S, hd), F32), jax.ShapeDtypeStruct((H, S, 1), F32)],
        compiler_params=_params(("parallel", "arbitrary")),
    )(*ins)


def _softmax_bwd(q, k, v, o, lse, do, bias=None, *, name, causal, tq=256, tk=256):
    H, S, hd = q.shape
    Sk = k.shape[1]
    tq, tk = _tile(S, tq), _tile(Sk, tk)
    nk = Sk // tk
    scale = hd ** -0.5
    has_bias = bias is not None
    n_in = 8 if has_bias else 6

    def body(*refs):
        q_ref, k_ref, v_ref, o_ref, lse_ref, do_ref = refs[:6]
        dq_ref, dk_ref, dv_ref = refs[n_in:n_in + 3]
        qi = pl.program_id(1)

        @pl.when(qi == 0)
        def _():
            dk_ref[...] = jnp.zeros_like(dk_ref)
            dv_ref[...] = jnp.zeros_like(dv_ref)
            if has_bias:
                refs[n_in + 4][...] = jnp.zeros_like(refs[n_in + 4])

        qv = q_ref[0]
        do32 = do_ref[0]
        dov = do32.astype(BF16)
        delta = jnp.sum(do32 * o_ref[0], axis=1, keepdims=True)
        lse_v = lse_ref[0]
        row = qi * tq + lax.broadcasted_iota(jnp.int32, (tq, tk), 0)
        col0 = lax.broadcasted_iota(jnp.int32, (tq, tk), 1)
        if has_bias:
            b_row = refs[6][0]

        def step(kb, carry):
            dq, db_row = carry
            ks = pl.multiple_of(kb * tk, tk)
            kv = k_ref[0, pl.ds(ks, tk), :]
            vv = v_ref[0, pl.ds(ks, tk), :]
            z = _dot(qv, kv, _NT) * scale
            if has_bias:
                z = z + b_row - refs[7][0, kb]
            p = jnp.exp(z - lse_v)
            if causal:
                p = jnp.where((col0 + ks) <= row, p, 0.0)
            dp = _dot(dov, vv, _NT)
            dz = p * (dp - delta)
            dzb = dz.astype(BF16)
            dq = dq + _dot(dzb, kv)
            dk_ref[0, pl.ds(ks, tk), :] += _dot(dzb, qv, _TN) * scale
            dv_ref[0, pl.ds(ks, tk), :] += _dot(p.astype(BF16), dov, _TN)
            if has_bias:
                db_row = db_row + jnp.sum(dz, axis=1, keepdims=True)
                refs[n_in + 4][0, kb] += jnp.sum(dz, axis=0, keepdims=True)
            return dq, db_row

        dq, db_row = lax.fori_loop(0, (qi + 1) if causal else nk, step,
                                   (jnp.zeros((tq, hd), F32), jnp.zeros((tq, 1), F32)))
        dq_ref[0] = dq * scale
        if has_bias:
            refs[n_in + 3][0] = db_row

    qspec = pl.BlockSpec((1, tq, hd), lambda h, i: (h, i, 0))
    kspec = pl.BlockSpec((1, Sk, hd), lambda h, i: (h, 0, 0))
    vspec = pl.BlockSpec((1, tq, 1), lambda h, i: (h, i, 0))
    cspec = pl.BlockSpec((1, nk, 1, tk), lambda h, i: (h, 0, 0, 0))
    in_specs = [qspec, kspec, kspec, qspec, vspec, qspec]
    ins = [q, k, v, o, lse, do]
    out_specs = [qspec, kspec, kspec]
    out_shape = [jax.ShapeDtypeStruct((H, S, hd), F32), jax.ShapeDtypeStruct((H, Sk, hd), F32),
                 jax.ShapeDtypeStruct((H, Sk, hd), F32)]
    if has_bias:
        in_specs += [vspec, cspec]
        ins += list(bias)
        out_specs += [vspec, cspec]
        out_shape += [jax.ShapeDtypeStruct((H, S, 1), F32), jax.ShapeDtypeStruct((H, nk, 1, tk), F32)]
    return pl.pallas_call(
        body, name=name, grid=(H, S // tq),
        in_specs=in_specs, out_specs=out_specs, out_shape=out_shape,
        compiler_params=_params(("parallel", "arbitrary")),
    )(*ins)


LANES = 128
_LOW = -3e38


def _lane_masks(hd, rows):
    if hd == LANES:
        return [None]
    lane = lax.broadcasted_iota(jnp.int32, (rows, LANES), 1)
    return [(lane >= hh * hd) & (lane < (hh + 1) * hd) for hh in range(LANES // hd)]


def _keep(t, m):
    return t if m is None else jnp.where(m, t, 0.0)


def _merge(parts, masks):
    out = parts[-1]
    for p, m in zip(parts[-2::-1], masks[-2::-1]):
        out = jnp.where(m, p, out)
    return out


def _row_value(t, m):
    return jnp.max(t if m is None else jnp.where(m, t, _LOW), axis=1, keepdims=True)


def _cols(tq, off):
    return pl.BlockSpec((tq, LANES), lambda g, i: (i, off + g))


def _cols_all(rows, off):
    return pl.BlockSpec((rows, LANES), lambda g, i: (0, off + g))


SCAN_BLOCK = 256


def _tri(kind, cols):
    n = min(SCAN_BLOCK, cols)
    r = lax.broadcasted_iota(jnp.int32, (n, n), 0)
    c = lax.broadcasted_iota(jnp.int32, (n, n), 1)
    return ((r > c) if kind == "after" else (r < c)).astype(BF16)


def _scan_cols(x, tri, reverse):
    cols = x.shape[1]
    cb = min(SCAN_BLOCK, cols)
    assert cols % cb == 0 and tri.shape == (cb, cb)
    nb = cols // cb
    blocks = [x[:, b * cb:(b + 1) * cb] for b in range(nb)]
    outs, carry = [None] * nb, None
    for b in (reversed(range(nb)) if reverse else range(nb)):
        y = _dot(blocks[b].astype(BF16), tri)
        outs[b] = y if carry is None else y + carry
        s = jnp.sum(blocks[b], axis=1, keepdims=True)
        carry = s if carry is None else carry + s
    return (outs[0] if nb == 1 else jnp.concatenate(outs, axis=1)), carry


def _softplus_parts(z):
    pos = jnp.maximum(z, 0.0) + jnp.log(1.0 + jnp.exp(-jnp.abs(z)))
    return pos, z - pos


class _ChipExchange:
    def __init__(self, kind, ins):
        assert kind in ("gather", "scatter")
        self.kind, self.ins = kind, list(ins)
        lead = (lambda s: (N_CHIPS,) + s) if kind == "gather" else (lambda s: (3,) + s[1:])
        self.out_shape = [jax.ShapeDtypeStruct(lead(a.shape), a.dtype) for a in ins]
        n = 3 * len(ins)
        self.scratch = [pltpu.SemaphoreType.DMA((n,)), pltpu.SemaphoreType.DMA((n,))]

    def _copies(self, in_refs, out_refs, sems, landing):
        x, y, c, chips = _mesh_place()
        me = 2 * x + y
        out = []
        for w in range(len(self.ins)):
            for j, chip in enumerate(chips):
                peer = 2 * chip[0] + chip[1]
                if self.kind == "gather":
                    src, dst, land = in_refs[w].at[c], out_refs[w].at[me, c], out_refs[w].at[peer, c]
                else:
                    src, dst, land = in_refs[w].at[peer], out_refs[w].at[j], out_refs[w].at[j]
                if landing:
                    src, dst = land, land
                out.append(_remote(src, dst, sems, 3 * w + j, (chip[0], chip[1], c)))
        return out

    def start(self, in_refs, out_refs, sems):
        for cp in self._copies(in_refs, out_refs, sems, False):
            cp.start()

    def finish(self, in_refs, out_refs, sems):
        for cp in self._copies(in_refs, out_refs, sems, True):
            cp.wait_recv()
        for cp in self._copies(in_refs, out_refs, sems, False):
            cp.wait_send()


def _carry(comm, n_in, n_out, first, last, refs):
    if comm is None:
        return refs[:n_in], refs[n_in:n_in + n_out], (lambda: None)
    a, b = len(comm.ins), len(comm.out_shape)
    ins, c_in = refs[:n_in], refs[n_in:n_in + a]
    outs, c_out = refs[n_in + a:n_in + a + n_out], refs[n_in + a + n_out:n_in + a + n_out + b]
    sems = refs[n_in + a + n_out + b:n_in + a + n_out + b + 2]
    pl.when(first)(lambda: comm.start(c_in, c_out, sems))
    return ins, outs, (lambda: pl.when(last)(lambda: comm.finish(c_in, c_out, sems)))


def _sbl_fwd(q, k, v, *, width, hd, name, tq=256, comm=None):
    (qa, qo), (ka, ko), (va, vo) = q, k, v
    S = qa.shape[0]
    tq = _tile(S, tq)
    tk = tq
    scale = hd ** -0.5
    n_g, n_q = width // LANES, S // tq

    def body(*refs):
        qi = pl.program_id(1)
        gi = pl.program_id(0)
        got = _carry(comm, 3, 2, (gi == 0) & (qi == 0), (gi == n_g - 1) & (qi == n_q - 1), refs)
        (q_ref, k_ref, v_ref), (o_ref, tot_ref) = got[0], got[1]
        masks = _lane_masks(hd, tq)
        qs = q_ref[...].astype(F32) * scale
        qm = [_keep(qs, m).astype(BF16) for m in masks]
        strict = lax.broadcasted_iota(jnp.int32, (tq, tk), 1) < lax.broadcasted_iota(jnp.int32, (tq, tk), 0)
        later = _tri("after", tk)

        def tile(kb, carry, diag):
            ks = pl.multiple_of(kb * tk, tk)
            kv = k_ref[pl.ds(ks, tk), :].astype(BF16)
            vv = v_ref[pl.ds(ks, tk), :].astype(BF16)
            out = []
            for hh in range(len(masks)):
                acc, c_pos = carry[2 * hh], carry[2 * hh + 1]
                pos, ls = _softplus_parts(_dot(qm[hh], kv, _NT))
                if diag:
                    pos = jnp.where(strict, pos, 0.0)
                pos_after, pos_all = _scan_cols(pos, later, True)
                w = jnp.exp(ls - (pos_after + c_pos))
                if diag:
                    w = jnp.where(strict, w, 0.0)
                out += [acc + _dot(w.astype(BF16), vv), c_pos + pos_all]
            return tuple(out)

        init = (jnp.zeros((tq, LANES), F32), jnp.zeros((tq, 1), F32)) * len(masks)
        carry = tile(qi, init, True)
        carry = lax.fori_loop(0, qi, lambda i, c: tile(qi - 1 - i, c, False), carry)
        o_ref[...] = _merge(carry[0::2], masks).astype(o_ref.dtype)
        tot_ref[...] = _merge([jnp.broadcast_to(-c, (tq, LANES)) for c in carry[1::2]], masks)
        got[2]()

    c_ins, c_in_specs, c_out_specs, c_out_shape, c_scratch = _comm_args(comm)
    return pl.pallas_call(
        body, name=name, grid=(n_g, n_q),
        in_specs=[_cols(tq, qo), _cols_all(S, ko), _cols_all(S, vo)] + c_in_specs,
        out_specs=[_cols(tq, 0), _cols(tq, 0)] + c_out_specs,
        out_shape=[jax.ShapeDtypeStruct((S, width), BF16), jax.ShapeDtypeStruct((S, width), F32)] + c_out_shape,
        scratch_shapes=c_scratch,
        compiler_params=_params(("arbitrary", "arbitrary")),
    )(qa, ka, va, *c_ins)


def _comm_args(comm):
    if comm is None:
        return [], [], [], [], []
    return comm.ins, [_ANY] * len(comm.ins), [_ANY] * len(comm.out_shape), comm.out_shape, comm.scratch


def _sbl_bwd(q, k, v, do, tot, *, width, hd, name, tq=256, comm=None):
    (qa, qo), (ka, ko), (va, vo) = q, k, v
    S = qa.shape[0]
    tq = _tile(S, tq)
    tk = tq
    scale = hd ** -0.5
    n_g, n_q = width // LANES, S // tq

    def body(*refs):
        qi = pl.program_id(1)
        gi = pl.program_id(0)
        got = _carry(comm, 5, 3, (gi == 0) & (qi == 0), (gi == n_g - 1) & (qi == n_q - 1), refs)
        (q_ref, k_ref, v_ref, do_ref, tot_ref), (dq_ref, dk_ref, dv_ref) = got[0], got[1]

        @pl.when(qi == 0)
        def _():
            dk_ref[...] = jnp.zeros_like(dk_ref)
            dv_ref[...] = jnp.zeros_like(dv_ref)

        masks = _lane_masks(hd, tq)
        qs = q_ref[...].astype(F32) * scale
        qm = [_keep(qs, m).astype(BF16) for m in masks]
        dov = [_keep(do_ref[...], m).astype(BF16) for m in masks]
        rest = [-_row_value(tot_ref[...], m) for m in masks]
        strict = lax.broadcasted_iota(jnp.int32, (tq, tk), 1) < lax.broadcasted_iota(jnp.int32, (tq, tk), 0)
        later, before = _tri("after", tk), _tri("before", tk)

        def tile(kb, carry, diag):
            ks = pl.multiple_of(kb * tk, tk)
            kv = k_ref[pl.ds(ks, tk), :].astype(BF16)
            vv = v_ref[pl.ds(ks, tk), :].astype(BF16)
            out = []
            dk_t, dv_t = None, None
            for hh in range(len(masks)):
                dq, c_pos, c_g = carry[3 * hh:3 * hh + 3]
                pos, ls = _softplus_parts(_dot(qm[hh], kv, _NT))
                if diag:
                    pos = jnp.where(strict, pos, 0.0)
                pos_after, pos_all = _scan_cols(pos, later, True)
                c_pos = c_pos + pos_all
                w = jnp.exp(ls - (pos_after + (rest[hh] - c_pos)))
                if diag:
                    w = jnp.where(strict, w, 0.0)
                g = _dot(dov[hh], vv, _NT) * w
                g_before, g_all = _scan_cols(g, before, False)
                g_before = g_before + c_g
                dz = g - jnp.exp(ls) * (g + g_before)
                if diag:
                    dz = jnp.where(strict, dz, 0.0)
                dzb = dz.astype(BF16)
                dk_h = _dot(dzb, qm[hh], _TN)
                dv_h = _dot(w.astype(BF16), dov[hh], _TN)
                dk_t = dk_h if dk_t is None else dk_t + dk_h
                dv_t = dv_h if dv_t is None else dv_t + dv_h
                out += [dq + _dot(dzb, kv), c_pos, c_g + g_all]
            dk_ref[pl.ds(ks, tk), :] += dk_t
            dv_ref[pl.ds(ks, tk), :] += dv_t
            return tuple(out)

        zero = jnp.zeros((tq, 1), F32)
        init = (jnp.zeros((tq, LANES), F32), zero, zero) * len(masks)
        carry = lax.fori_loop(0, qi, lambda kb, c: tile(kb, c, False), init)
        carry = tile(qi, carry, True)
        dq_ref[...] = _merge(carry[0::3], masks) * scale
        got[2]()

    full = jax.ShapeDtypeStruct((S, width), F32)
    c_ins, c_in_specs, c_out_specs, c_out_shape, c_scratch = _comm_args(comm)
    return pl.pallas_call(
        body, name=name, grid=(n_g, n_q),
        in_specs=[_cols(tq, qo), _cols_all(S, ko), _cols_all(S, vo), _cols(tq, do[1]), _cols(tq, tot[1])] + c_in_specs,
        out_specs=[_cols(tq, 0), _cols_all(S, 0), _cols_all(S, 0)] + c_out_specs,
        out_shape=[full, full, full] + c_out_shape,
        scratch_shapes=c_scratch,
        compiler_params=_params(("arbitrary", "arbitrary")),
    )(qa, ka, va, do[0], tot[0], *c_ins)


def _sml_fwd(q, k, v, bias=None, *, width, hd, causal, name, tq=256, tk=256):
    (qa, qo), (ka, ko), (va, vo) = q, k, v
    S, Sk = qa.shape[0], ka.shape[0]
    tq, tk = _tile(S, tq), _tile(Sk, tk)
    if causal:
        assert tq == tk and S == Sk
    nk = Sk // tk
    hpg = LANES // hd
    scale = hd ** -0.5
    has_bias = bias is not None

    def body(*refs):
        q_ref, k_ref, v_ref = refs[:3]
        o_ref, lse_ref = refs[-2:]
        qi = pl.program_id(1)
        masks = _lane_masks(hd, tq)
        qs = q_ref[...].astype(F32) * scale
        qm = [_keep(qs, m).astype(BF16) for m in masks]
        allowed = lax.broadcasted_iota(jnp.int32, (tq, tk), 1) <= lax.broadcasted_iota(jnp.int32, (tq, tk), 0)

        def tile(kb, carry, diag):
            ks = pl.multiple_of(kb * tk, tk)
            kv = k_ref[pl.ds(ks, tk), :].astype(BF16)
            vv = v_ref[pl.ds(ks, tk), :].astype(BF16)
            out = []
            for hh in range(hpg):
                m, l, acc = carry[3 * hh:3 * hh + 3]
                z = _dot(qm[hh], kv, _NT)
                if has_bias:
                    z = z + refs[3][hh] - refs[4][hh, kb]
                if diag:
                    z = jnp.where(allowed, z, NEG_INF)
                m2 = jnp.maximum(m, jnp.max(z, axis=1, keepdims=True))
                p = jnp.exp(z - m2)
                alpha = jnp.exp(m - m2)
                out += [m2, alpha * l + jnp.sum(p, axis=1, keepdims=True), alpha * acc + _dot(p.astype(BF16), vv)]
            return tuple(out)

        init = (jnp.full((tq, 1), NEG_INF, F32), jnp.zeros((tq, 1), F32), jnp.zeros((tq, LANES), F32)) * hpg
        if causal:
            carry = lax.fori_loop(0, qi, lambda kb, c: tile(kb, c, False), init)
            carry = tile(qi, carry, True)
        else:
            carry = lax.fori_loop(0, nk, lambda kb, c: tile(kb, c, False), init)
        o_ref[...] = _merge([acc / l for l, acc in zip(carry[1::3], carry[2::3])], masks).astype(o_ref.dtype)
        lse_ref[...] = _merge([jnp.broadcast_to(m + jnp.log(l), (tq, LANES)) for m, l in zip(carry[0::3], carry[1::3])], masks)

    in_specs = [_cols(tq, qo), _cols_all(Sk, ko), _cols_all(Sk, vo)]
    ins = [qa, ka, va]
    if has_bias:
        in_specs += [pl.BlockSpec((hpg, tq, 1), lambda g, i: (g, i, 0)),
                     pl.BlockSpec((hpg, nk, 1, tk), lambda g, i: (g, 0, 0, 0))]
        ins += list(bias)
    return pl.pallas_call(
        body, name=name, grid=(width // LANES, S // tq),
        in_specs=in_specs, out_specs=[_cols(tq, 0), _cols(tq, 0)],
        out_shape=[jax.ShapeDtypeStruct((S, width), BF16), jax.ShapeDtypeStruct((S, width), F32)],
        compiler_params=_params(("parallel", "arbitrary")),
    )(*ins)


def _sml_bwd(q, k, v, o, lse, do, bias=None, *, width, hd, causal, name, tq=256, tk=256):
    (qa, qo), (ka, ko), (va, vo) = q, k, v
    S, Sk = qa.shape[0], ka.shape[0]
    tq, tk = _tile(S, tq), _tile(Sk, tk)
    nk = Sk // tk
    hpg = LANES // hd
    scale = hd ** -0.5
    has_bias = bias is not None
    n_in = 8 if has_bias else 6

    def body(*refs):
        q_ref, k_ref, v_ref, o_ref, lse_ref, do_ref = refs[:6]
        dq_ref, dk_ref, dv_ref = refs[n_in:n_in + 3]
        qi = pl.program_id(1)

        @pl.when(qi == 0)
        def _():
            dk_ref[...] = jnp.zeros_like(dk_ref)
            dv_ref[...] = jnp.zeros_like(dv_ref)
            if has_bias:
                refs[n_in + 4][...] = jnp.zeros_like(refs[n_in + 4])

        masks = _lane_masks(hd, tq)
        qs = q_ref[...].astype(F32) * scale
        qm = [_keep(qs, m).astype(BF16) for m in masks]
        do32 = do_ref[...]
        dov = [_keep(do32, m).astype(BF16) for m in masks]
        prod = do32 * o_ref[...].astype(F32)
        delta = [jnp.sum(_keep(prod, m), axis=1, keepdims=True) for m in masks]
        lses = [_row_value(lse_ref[...], m) for m in masks]
        allowed = lax.broadcasted_iota(jnp.int32, (tq, tk), 1) <= lax.broadcasted_iota(jnp.int32, (tq, tk), 0)

        def tile(kb, carry, diag):
            ks = pl.multiple_of(kb * tk, tk)
            kv = k_ref[pl.ds(ks, tk), :].astype(BF16)
            vv = v_ref[pl.ds(ks, tk), :].astype(BF16)
            out = []
            dk_t, dv_t = None, None
            for hh in range(hpg):
                dq, db_row = carry[2 * hh:2 * hh + 2]
                z = _dot(qm[hh], kv, _NT)
                if has_bias:
                    z = z + refs[6][hh] - refs[7][hh, kb]
                p = jnp.exp(z - lses[hh])
                if diag:
                    p = jnp.where(allowed, p, 0.0)
                dz = p * (_dot(dov[hh], vv, _NT) - delta[hh])
                dzb = dz.astype(BF16)
                dk_h = _dot(dzb, qm[hh], _TN)
                dv_h = _dot(p.astype(BF16), dov[hh], _TN)
                dk_t = dk_h if dk_t is None else dk_t + dk_h
                dv_t = dv_h if dv_t is None else dv_t + dv_h
                if has_bias:
                    db_row = db_row + jnp.sum(dz, axis=1, keepdims=True)
                    refs[n_in + 4][hh, kb] += jnp.sum(dz, axis=0, keepdims=True)
                out += [dq + _dot(dzb, kv), db_row]
            dk_ref[pl.ds(ks, tk), :] += dk_t
            dv_ref[pl.ds(ks, tk), :] += dv_t
            return tuple(out)

        init = (jnp.zeros((tq, LANES), F32), jnp.zeros((tq, 1), F32)) * hpg
        if causal:
            carry = lax.fori_loop(0, qi, lambda kb, c: tile(kb, c, False), init)
            carry = tile(qi, carry, True)
        else:
            carry = lax.fori_loop(0, nk, lambda kb, c: tile(kb, c, False), init)
        dq_ref[...] = _merge(carry[0::2], masks) * scale
        if has_bias:
            for hh in range(hpg):
                refs[n_in + 3][hh] = carry[2 * hh + 1]

    in_specs = [_cols(tq, qo), _cols_all(Sk, ko), _cols_all(Sk, vo), _cols(tq, o[1]), _cols(tq, lse[1]), _cols(tq, do[1])]
    ins = [qa, ka, va, o[0], lse[0], do[0]]
    out_specs = [_cols(tq, 0), _cols_all(Sk, 0), _cols_all(Sk, 0)]
    out_shape = [jax.ShapeDtypeStruct((S, width), F32), jax.ShapeDtypeStruct((Sk, width), F32),
                 jax.ShapeDtypeStruct((Sk, width), F32)]
    if has_bias:
        rspec = pl.BlockSpec((hpg, tq, 1), lambda g, i: (g, i, 0))
        cspec = pl.BlockSpec((hpg, nk, 1, tk), lambda g, i: (g, 0, 0, 0))
        in_specs += [rspec, cspec]
        ins += list(bias)
        out_specs += [rspec, cspec]
        n_heads = width // hd
        out_shape += [jax.ShapeDtypeStruct((n_heads, S, 1), F32), jax.ShapeDtypeStruct((n_heads, nk, 1, tk), F32)]
    return pl.pallas_call(
        body, name=name, grid=(width // LANES, S // tq),
        in_specs=in_specs, out_specs=out_specs, out_shape=out_shape,
        compiler_params=_params(("parallel", "arbitrary")),
    )(*ins)


def _head_sums(t, masks):
    sums = [jnp.sum(_keep(t, m), axis=1, keepdims=True) for m in masks]
    return _merge([jnp.broadcast_to(s, t.shape) for s in sums], masks) if len(masks) > 1 else sums[0]


def _hnorm_fwd(x, g_lanes, *, width, hd, name, tr=512):
    xa, xo = x
    R = xa.shape[0]
    tr = _tile(R, tr, align=16)
    n_blk = width // LANES

    def body(x_ref, g_ref, o_ref):
        masks = _lane_masks(hd, tr)
        for j in range(n_blk):
            sl = slice(j * LANES, (j + 1) * LANES)
            xv = x_ref[:, sl].astype(F32)
            r = lax.rsqrt(_head_sums(xv * xv, masks) * (1.0 / hd) + EPS)
            o_ref[:, sl] = (xv * r * g_ref[...]).astype(o_ref.dtype)

    assert (xo * LANES) % width == 0
    return pl.pallas_call(
        body, name=name, grid=(R // tr,),
        in_specs=[pl.BlockSpec((tr, width), lambda i: (i, xo * LANES // width)), pl.BlockSpec((1, LANES), lambda i: (0, 0))],
        out_specs=pl.BlockSpec((tr, width), lambda i: (i, 0)),
        out_shape=jax.ShapeDtypeStruct((R, width), BF16),
        compiler_params=_params(("parallel",)),
    )(xa, g_lanes)


def _hnorm_bwd(x, g_lanes, dy, *, width, hd, name, tr=512):
    xa, xo = x
    R = xa.shape[0]
    tr = _tile(R, tr, align=16)
    n_blk = width // LANES

    def body(x_ref, g_ref, dy_ref, dx_ref, dg_ref):
        masks = _lane_masks(hd, tr)
        dg = jnp.zeros((1, LANES), F32)
        for j in range(n_blk):
            sl = slice(j * LANES, (j + 1) * LANES)
            xv = x_ref[:, sl].astype(F32)
            dyv = dy_ref[:, sl].astype(F32)
            r = lax.rsqrt(_head_sums(xv * xv, masks) * (1.0 / hd) + EPS)
            xh = xv * r
            dyg = dyv * g_ref[...]
            c = _head_sums(dyg * xh, masks) * (1.0 / hd)
            dx_ref[:, sl] = (r * (dyg - xh * c)).astype(dx_ref.dtype)
            dg = dg + jnp.sum(dyv * xh, axis=0, keepdims=True)
        if hd * 2 == LANES:
            dg8 = jnp.broadcast_to(dg, (8, LANES))
            dg = (dg8 + pltpu.roll(dg8, shift=hd, axis=1))[0:1]
        else:
            assert hd == LANES

        @pl.when(pl.program_id(0) == 0)
        def _():
            dg_ref[...] = jnp.zeros_like(dg_ref)

        dg_ref[...] += dg

    assert (xo * LANES) % width == 0
    return pl.pallas_call(
        body, name=name, grid=(R // tr,),
        in_specs=[pl.BlockSpec((tr, width), lambda i: (i, xo * LANES // width)), pl.BlockSpec((1, LANES), lambda i: (0, 0)),
                  pl.BlockSpec((tr, width), lambda i: (i, 0))],
        out_specs=[pl.BlockSpec((tr, width), lambda i: (i, 0)), pl.BlockSpec((1, LANES), lambda i: (0, 0))],
        out_shape=[jax.ShapeDtypeStruct((R, width), BF16), jax.ShapeDtypeStruct((1, LANES), F32)],
        compiler_params=_params(("arbitrary",)),
    )(xa, g_lanes, dy)


def _split3_dot(x, tri):
    a = x.astype(BF16)
    r = x - a.astype(F32)
    b = r.astype(BF16)
    c = (r - b.astype(F32)).astype(BF16)
    return _dot(a, tri) + _dot(b, tri) + _dot(c, tri)


def _forget_fwd(logit_t, b_col, *, name, blk=512):
    H, S = logit_t.shape
    blk = _tile(S, blk)

    def body(l_ref, b_ref, f_ref):
        r_i = lax.broadcasted_iota(jnp.int32, (blk, blk), 0)
        c_i = lax.broadcasted_iota(jnp.int32, (blk, blk), 1)
        upto = (r_i <= c_i).astype(BF16)
        carry = jnp.zeros((H, 1), F32)
        for j in range(S // blk):
            u = l_ref[:, j * blk:(j + 1) * blk] + b_ref[...]
            lf, _ = _log_sigmoid_pair(u)
            f_ref[:, j * blk:(j + 1) * blk] = _split3_dot(lf, upto) + carry
            carry = carry + jnp.sum(lf, axis=1, keepdims=True)

    return pl.pallas_call(
        body, name=name,
        out_shape=jax.ShapeDtypeStruct((H, S), F32),
        compiler_params=pltpu.CompilerParams(vmem_limit_bytes=VMEM_LIMIT),
    )(logit_t, b_col)


def _forget_bwd(logit_t, b_col, d_f, *, name, blk=512):
    H, S = logit_t.shape
    blk = _tile(S, blk)

    def body(l_ref, b_ref, df_ref, dl_ref, db_ref):
        r_i = lax.broadcasted_iota(jnp.int32, (blk, blk), 0)
        c_i = lax.broadcasted_iota(jnp.int32, (blk, blk), 1)
        fromon = (r_i >= c_i).astype(BF16)
        carry = jnp.zeros((H, 1), F32)
        db = jnp.zeros((H, 1), F32)
        for j in reversed(range(S // blk)):
            sl = slice(j * blk, (j + 1) * blk)
            dfv = df_ref[:, sl]
            d_lf = _split3_dot(dfv, fromon) + carry
            carry = carry + jnp.sum(dfv, axis=1, keepdims=True)
            u = l_ref[:, sl] + b_ref[...]
            _, lsn = _log_sigmoid_pair(u)
            dl = d_lf * jnp.exp(lsn)
            dl_ref[:, sl] = dl
            db = db + jnp.sum(dl, axis=1, keepdims=True)
        db_ref[...] = db

    return pl.pallas_call(
        body, name=name,
        out_shape=[jax.ShapeDtypeStruct((H, S), F32), jax.ShapeDtypeStruct((H, 1), F32)],
        compiler_params=pltpu.CompilerParams(vmem_limit_bytes=VMEM_LIMIT),
    )(logit_t, b_col, d_f)


def _sigmoid(t):
    return 1.0 / (1.0 + jnp.exp(-t))


def _gate_fwd(o3, w3, proj, D, *, name, tm=256):
    S = proj.shape[0]
    tm = _tile(S, tm)

    def body(o0, o1, o2, w0, w1, w2, g0, g1, g2, out_ref):
        acc = None
        for o_ref, w_ref, g_ref in ((o0, w0, g0), (o1, w1, g1), (o2, w2, g2)):
            t = _sigmoid(g_ref[...]) * _dot(o_ref[...], w_ref[...])
            acc = t if acc is None else acc + t
        out_ref[...] = acc.astype(out_ref.dtype)

    ospec = lambda d: pl.BlockSpec((tm, d), lambda i: (i, 0))
    wspec = lambda w: pl.BlockSpec(w.shape, lambda i: (0, 0))
    gspec = lambda j: pl.BlockSpec((tm, D), lambda i: (i, j))
    return pl.pallas_call(
        body, name=name, grid=(S // tm,),
        in_specs=[ospec(o.shape[1]) for o in o3] + [wspec(w) for w in w3] + [gspec(j) for j in range(3)],
        out_specs=pl.BlockSpec((tm, D), lambda i: (i, 0)),
        out_shape=jax.ShapeDtypeStruct((S, D), BF16),
        compiler_params=_params(("parallel",)),
    )(*o3, *w3, proj, proj, proj)


def _gate_bwd(o3, w3, proj, dmerged, D, *, name, tm=256):
    S = proj.shape[0]
    tm = _tile(S, tm)

    def body(o0, o1, o2, w0, w1, w2, g0, g1, g2, dm_ref, dg_ref, db0, db1, db2, do0, do1, do2):
        dm = dm_ref[...]
        for j, (o_ref, w_ref, g_ref, db_ref, do_ref) in enumerate(
                ((o0, w0, g0, db0, do0), (o1, w1, g1, db1, do1), (o2, w2, g2, db2, do2))):
            s = _sigmoid(g_ref[...])
            br = _dot(o_ref[...], w_ref[...])
            dg_ref[:, j * D:(j + 1) * D] = (dm * br * s * (1.0 - s)).astype(dg_ref.dtype)
            dbr = (dm * s).astype(BF16)
            db_ref[...] = dbr
            do_ref[...] = _dot(dbr, w_ref[...], _NT)

    ospec = lambda d: pl.BlockSpec((tm, d), lambda i: (i, 0))
    wspec = lambda w: pl.BlockSpec(w.shape, lambda i: (0, 0))
    gspec = lambda j: pl.BlockSpec((tm, D), lambda i: (i, j))
    dspec = pl.BlockSpec((tm, D), lambda i: (i, 0))
    return pl.pallas_call(
        body, name=name, grid=(S // tm,),
        in_specs=[ospec(o.shape[1]) for o in o3] + [wspec(w) for w in w3] + [gspec(j) for j in range(3)] + [dspec],
        out_specs=[pl.BlockSpec((tm, 3 * D), lambda i: (i, 0))] + [dspec] * 3 + [ospec(o.shape[1]) for o in o3],
        out_shape=[jax.ShapeDtypeStruct((S, 3 * D), BF16)] + [jax.ShapeDtypeStruct((S, D), BF16)] * 3
        + [jax.ShapeDtypeStruct((S, o.shape[1]), F32) for o in o3],
        compiler_params=_params(("parallel",)),
    )(*o3, *w3, proj, proj, proj, dmerged)


def _loss_sum(dy, D, *, name):
    R, C = dy.shape
    tr = _row_tile(R, C, 2)

    def body(dy_ref, out_ref):
        @pl.when(pl.program_id(0) == 0)
        def _():
            out_ref[...] = jnp.zeros_like(out_ref)

        v = dy_ref[...]
        out_ref[...] += (0.5 * D) * jnp.sum(v * v)

    return pl.pallas_call(
        body, name=name, grid=(R // tr,),
        in_specs=[pl.BlockSpec((tr, C), lambda i: (i, 0))],
        out_specs=pl.BlockSpec((8, 128), lambda i: (0, 0)),
        out_shape=jax.ShapeDtypeStruct((8, 128), F32),
        compiler_params=_params(("arbitrary",)),
    )(dy)[0, 0]


def _adamw_math(w, g, m, v):
    m2 = ADAM_B1 * m + (1.0 - ADAM_B1) * g
    v2 = ADAM_B2 * v + (1.0 - ADAM_B2) * (g * g)
    m_hat = m2 / (1.0 - ADAM_B1 ** ADAM_STEP)
    v_hat = v2 / (1.0 - ADAM_B2 ** ADAM_STEP)
    delta = -ADAM_LR * (m_hat / (jnp.sqrt(v_hat) + ADAM_EPS) + ADAM_WD * w)
    return delta, m2, v2


def _adamw(w, g, m, v, *, name):
    return _ew(_adamw_math, [w, g, m, v], (F32, F32, F32), name=name)


def _adamw_small(w, parts, m, v, *, name):
    n = parts.shape[0]

    def body(w_ref, p_ref, m_ref, v_ref, g_ref, d_ref, m2_ref, v2_ref):
        g = p_ref[0]
        for i in range(1, n):
            g = g + p_ref[i]
        g_ref[...] = g
        d_ref[...], m2_ref[...], v2_ref[...] = _adamw_math(w_ref[...], g, m_ref[...], v_ref[...])

    shp = jax.ShapeDtypeStruct(w.shape, F32)
    return pl.pallas_call(body, name=name, out_shape=[shp] * 4)(w, parts, m, v)


_ANY = pl.BlockSpec(memory_space=pl.ANY)


def _mesh_place():
    x, y, c = lax.axis_index("x"), lax.axis_index("y"), lax.axis_index("c")
    chips = [(1 - x, y), (x, 1 - y), (1 - x, 1 - y)]
    return x, y, c, chips


def _remote(src, dst, sems, i, to):
    send_sems, recv_sems = sems
    return pltpu.make_async_remote_copy(src_ref=src, dst_ref=dst, send_sem=send_sems.at[i], recv_sem=recv_sems.at[i],
                                        device_id=to, device_id_type=MESH_ID)


def _gather_weights(shards, *, name):
    n = len(shards)

    def body(*refs):
        ins, outs = refs[:n], refs[n:2 * n]
        sems = refs[2 * n:2 * n + 2]
        x, y, c, chips = _mesh_place()
        me = 2 * x + y
        sibling = (x, y, 1 - c)
        sent = []
        for w in range(n):
            for j, chip in enumerate(chips):
                cp = _remote(ins[w].at[c], outs[w].at[me, c], sems, 6 * w + j, (chip[0], chip[1], c))
                cp.start()
                sent.append(cp)
        for w in range(n):
            for j, chip in enumerate(chips):
                got = outs[w].at[2 * chip[0] + chip[1], c]
                _remote(got, got, sems, 6 * w + j, sibling).wait_recv()
                cp = _remote(got, got, sems, 6 * w + 3 + j, sibling)
                cp.start()
                sent.append(cp)
        for w in range(n):
            for j, chip in enumerate(chips):
                got = outs[w].at[2 * chip[0] + chip[1], 1 - c]
                _remote(got, got, sems, 6 * w + 3 + j, sibling).wait_recv()
        for cp in sent:
            cp.wait_send()

    outs = pl.pallas_call(
        body, name=name,
        in_specs=[_ANY] * n, out_specs=[_ANY] * n,
        out_shape=[jax.ShapeDtypeStruct((N_CHIPS,) + s.shape, s.dtype) for s in shards],
        scratch_shapes=[pltpu.SemaphoreType.DMA((6 * n,)), pltpu.SemaphoreType.DMA((6 * n,))],
    )(*shards)
    me = 2 * lax.axis_index("x") + lax.axis_index("y")
    return [lax.dynamic_update_index_in_dim(o, s, me, 0) for o, s in zip(outs, shards)]


def _forward_halves(gathered, *, name):
    n = len(gathered)

    def body(*refs):
        ins, outs = refs[:n], refs[n:2 * n]
        sems = refs[2 * n:2 * n + 2]
        x, y, c, chips = _mesh_place()
        sibling = (x, y, 1 - c)
        sent = []
        for w in range(n):
            for j, chip in enumerate(chips):
                peer = 2 * chip[0] + chip[1]
                cp = _remote(ins[w].at[peer, c], outs[w].at[peer, c], sems, 3 * w + j, sibling)
                cp.start()
                sent.append(cp)
        for w in range(n):
            for j, chip in enumerate(chips):
                land = outs[w].at[2 * chip[0] + chip[1], 1 - c]
                _remote(land, land, sems, 3 * w + j, sibling).wait_recv()
        for cp in sent:
            cp.wait_send()

    return pl.pallas_call(
        body, name=name,
        in_specs=[_ANY] * n, out_specs=[_ANY] * n,
        out_shape=[jax.ShapeDtypeStruct(g.shape, g.dtype) for g in gathered],
        input_output_aliases={w: w for w in range(n)},
        scratch_shapes=[pltpu.SemaphoreType.DMA((3 * n,)), pltpu.SemaphoreType.DMA((3 * n,))],
    )(*gathered)


def _exchange_siblings(grads, *, name):
    n = len(grads)

    def body(*refs):
        ins, got = refs[:n], refs[n:2 * n]
        sems = refs[2 * n:2 * n + 2]
        x, y, c, _ = _mesh_place()
        sibling = (x, y, 1 - c)
        sent = []
        for w in range(n):
            for s in range(N_CHIPS):
                cp = _remote(ins[w].at[s, 1 - c], got[w].at[s], sems, N_CHIPS * w + s, sibling)
                cp.start()
                sent.append(cp)
        for w in range(n):
            for s in range(N_CHIPS):
                _remote(got[w].at[s], got[w].at[s], sems, N_CHIPS * w + s, sibling).wait_recv()
        for cp in sent:
            cp.wait_send()

    n_sem = N_CHIPS * n
    got = pl.pallas_call(
        body, name=name,
        in_specs=[_ANY] * n, out_specs=[_ANY] * n,
        out_shape=[jax.ShapeDtypeStruct((N_CHIPS,) + g.shape[2:], g.dtype) for g in grads],
        scratch_shapes=[pltpu.SemaphoreType.DMA((n_sem,)), pltpu.SemaphoreType.DMA((n_sem,))],
    )(*grads)
    c = lax.axis_index("c")
    return [lax.dynamic_index_in_dim(g, c, 1, keepdims=False) for g in grads], got


def _exchange_chips(parts, *, name):
    n = len(parts)

    def body(*refs):
        ins, got = refs[:n], refs[n:2 * n]
        sems = refs[2 * n:2 * n + 2]
        x, y, c, chips = _mesh_place()
        sent = []
        for w in range(n):
            for j, chip in enumerate(chips):
                cp = _remote(ins[w].at[2 * chip[0] + chip[1]], got[w].at[j], sems, 3 * w + j, (chip[0], chip[1], c))
                cp.start()
                sent.append(cp)
        for w in range(n):
            for j in range(3):
                _remote(got[w].at[j], got[w].at[j], sems, 3 * w + j, (x, y, c)).wait_recv()
        for cp in sent:
            cp.wait_send()

    got = pl.pallas_call(
        body, name=name,
        in_specs=[_ANY] * n, out_specs=[_ANY] * n,
        out_shape=[jax.ShapeDtypeStruct((3,) + p.shape[1:], p.dtype) for p in parts],
        scratch_shapes=[pltpu.SemaphoreType.DMA((3 * n,)), pltpu.SemaphoreType.DMA((3 * n,))],
    )(*parts)
    me = 2 * lax.axis_index("x") + lax.axis_index("y")
    return [lax.dynamic_index_in_dim(p, me, 0, keepdims=False) for p in parts], got


def _share_halves(halves, small):
    n = len(halves)

    def body(*refs):
        ins, small_ref = refs[:n], refs[n]
        outs, small_out = refs[n + 1:2 * n + 1], refs[2 * n + 1]
        sems = refs[2 * n + 2:2 * n + 4]
        x, y, c, chips = _mesh_place()
        sibling = (x, y, 1 - c)
        me = 4 * x + 2 * y + c
        sent = [_remote(ins[w], outs[w].at[c], sems, w, sibling) for w in range(n)]
        peers = [sibling] + [(ch[0], ch[1], cc) for ch in chips for cc in (c, 1 - c)]
        sent += [_remote(small_ref, small_out.at[me], sems, n + j, peer) for j, peer in enumerate(peers)]
        for cp in sent:
            cp.start()
        for w in range(n):
            _remote(outs[w].at[1 - c], outs[w].at[1 - c], sems, w, sibling).wait_recv()
        for j, peer in enumerate(peers):
            frm = small_out.at[4 * peer[0] + 2 * peer[1] + peer[2]]
            _remote(frm, frm, sems, n + j, peer).wait_recv()
        for cp in sent:
            cp.wait_send()

    n_sem = n + 7
    outs = pl.pallas_call(
        body, name="share_halves",
        in_specs=[_ANY] * (n + 1), out_specs=[_ANY] * (n + 1),
        out_shape=[jax.ShapeDtypeStruct((2,) + h.shape, h.dtype) for h in halves]
        + [jax.ShapeDtypeStruct((8,) + small.shape, small.dtype)],
        scratch_shapes=[pltpu.SemaphoreType.DMA((n_sem,)), pltpu.SemaphoreType.DMA((n_sem,))],
    )(*halves, small)
    c = lax.axis_index("c")
    me = 4 * lax.axis_index("x") + 2 * lax.axis_index("y") + c
    return ([lax.dynamic_update_index_in_dim(o, h, c, 0) for o, h in zip(outs[:n], halves)],
            lax.dynamic_update_index_in_dim(outs[n], small, me, 0))


EARLY = ("w_ff_down", "w_ff_up", "w_out", "w_branch_sb", "w_branch_fox", "w_branch_mem")


def _local_step(x, mem, target, small, W, gather_rest=None, reduce_early=None, reduce_late=None):
    S, D = x.shape
    o_gate, o_qkv, o_mq, o_f = 0, 3 * D, 3 * D + 2 * 3 * D_SB, 3 * D + 2 * 3 * D_SB + D_MEM
    tq = 512

    h = _rmsnorm_fwd(x, small["g_mix_norm"], BF16, name="mix_norm")
    proj = _mm(h, W["w_in"], name="in_proj", tn=768)
    blk = lambda j: (proj, (o_qkv + j * D_SB) // LANES)
    sb_q, sb_k, sb_v, fx_q, fx_k, fx_v = [blk(j) for j in range(6)]
    m_q = (proj, o_mq // LANES)
    f_logit_t = proj[:, o_f:o_f + FOX_HEADS].T
    b_col = small["b_forget"].reshape(FOX_HEADS, 1)
    lanes = lambda g: jnp.tile(g, (1, LANES // g.shape[1]))
    g_fq, g_fk, g_mq, g_mk = [lanes(small[k]) for k in ("g_fox_q", "g_fox_k", "g_mem_q", "g_mem_k")]

    if gather_rest is None:
        o_sb, sb_tot = _sbl_fwd(sb_q, sb_k, sb_v, width=D_SB, hd=HD, name="sb_fwd", tq=tq)
    else:
        comm, finish = gather_rest
        o_sb, sb_tot, *landed = _sbl_fwd(sb_q, sb_k, sb_v, width=D_SB, hd=HD, name="sb_fwd", tq=tq, comm=comm)
        W = {**W, **finish(landed)}

    fq = _hnorm_fwd(fx_q, g_fq, width=D_FOX, hd=HD, name="fox_q_norm")
    fk = _hnorm_fwd(fx_k, g_fk, width=D_FOX, hd=HD, name="fox_k_norm")
    f_cum = _forget_fwd(f_logit_t, b_col, name="forget_fwd")
    tkf = _tile(S, tq)
    f_bias = (f_cum.reshape(FOX_HEADS, S, 1), f_cum.reshape(FOX_HEADS, S // tkf, 1, tkf))
    o_fox, fox_lse = _sml_fwd((fq, 0), (fk, 0), fx_v, f_bias, width=D_FOX, hd=HD, causal=True, name="fox_fwd",
                              tq=tq, tk=tq)

    mh = _rmsnorm_fwd(mem, small["g_mem_norm"], BF16, name="mem_norm")
    mkv = _mm(mh, W["w_mem_kv"], name="mem_kv")
    mv = (mkv, D_MEM // LANES)
    mq = _hnorm_fwd(m_q, g_mq, width=D_MEM, hd=MEM_HD, name="mem_q_norm")
    mk = _hnorm_fwd((mkv, 0), g_mk, width=D_MEM, hd=MEM_HD, name="mem_k_norm")
    o_mem, mem_lse = _sml_fwd((mq, 0), (mk, 0), mv, width=D_MEM, hd=MEM_HD, causal=False, name="mem_fwd", tq=tq, tk=256)

    o3 = [o_sb, o_fox, o_mem]
    w3 = [W["w_branch_sb"], W["w_branch_fox"], W["w_branch_mem"]]
    merged = _gate_fwd(o3, w3, proj, D, name="gate_fwd")
    x1 = _mm(merged, W["w_out"], name="out_proj", extras=(x,), epilogue=lambda acc, res: (res + acc,))
    h2 = _rmsnorm_fwd(x1, small["g_mlp_norm"], BF16, name="mlp_norm")

    def relu2(acc):
        u = jnp.maximum(acc, 0.0)
        return u, u * u

    u, a = _mm(h2, W["w_ff_up"], name="ff_up", out_dtypes=(BF16, BF16), epilogue=relu2)
    dy = _mm(a, W["w_ff_down"], name="ff_down", extras=(x1, target),
             epilogue=lambda acc, res, tgt: ((res + acc - tgt) * (1.0 / D),))
    loss = _loss_sum(dy, D, name="loss")

    G = {}
    du = _mm(dy, W["w_ff_down"], name="d_ff_act", tb=True, out_dtypes=(BF16,), extras=(u,),
             epilogue=lambda acc, uu: (acc * (2.0 * uu.astype(F32)),))
    G["w_ff_down"] = _mm(a, dy, name="d_w_ff_down", ta=True, out_dtypes=(BF16,))
    G["w_ff_up"] = _mm(h2, du, name="d_w_ff_up", ta=True, out_dtypes=(BF16,))
    dh2 = _mm(du, W["w_ff_up"], name="d_mlp_in", tb=True)
    dx1, dg_mlp = _rmsnorm_bwd(x1, small["g_mlp_norm"], dh2, add=dy, name="d_mlp_norm")
    dmerged = _mm(dx1, W["w_out"], name="d_merged", tb=True)
    G["w_out"] = _mm(merged, dx1, name="d_w_out", ta=True, out_dtypes=(BF16,))
    dgate, db0, db1, db2, do_sb, do_fox, do_mem = _gate_bwd(o3, w3, proj, dmerged, D, name="gate_bwd")
    for nm, o, db in zip(("w_branch_sb", "w_branch_fox", "w_branch_mem"), o3, (db0, db1, db2)):
        G[nm] = _mm(o, db, name="d_" + nm, ta=True, out_dtypes=(BF16,))

    early = {}
    if reduce_early is None:
        dsb = _sbl_bwd(sb_q, sb_k, sb_v, (do_sb, 0), (sb_tot, 0), width=D_SB, hd=HD, name="sb_bwd", tq=tq)
    else:
        comm, finish = reduce_early({k: G.pop(k) for k in EARLY})
        *dsb, = _sbl_bwd(sb_q, sb_k, sb_v, (do_sb, 0), (sb_tot, 0), width=D_SB, hd=HD, name="sb_bwd", tq=tq, comm=comm)
        dsb, landed = dsb[:3], dsb[3:]
        early = finish(landed)
    dfq, dfk, dfv, df_row, df_col = _sml_bwd((fq, 0), (fk, 0), fx_v, (o_fox, 0), (fox_lse, 0), (do_fox, 0), f_bias,
                                             width=D_FOX, hd=HD, causal=True, name="fox_bwd", tq=tq, tk=tq)
    dfx_q, dg_fox_q = _hnorm_bwd(fx_q, g_fq, dfq, width=D_FOX, hd=HD, name="d_fox_q_norm")
    dfx_k, dg_fox_k = _hnorm_bwd(fx_k, g_fk, dfk, width=D_FOX, hd=HD, name="d_fox_k_norm")
    d_fcum = df_row.reshape(FOX_HEADS, S) - df_col.reshape(FOX_HEADS, S)
    d_flogit_t, db_forget = _forget_bwd(f_logit_t, b_col, d_fcum, name="forget_bwd")

    dmq_n, dmk_n, dmv = _sml_bwd((mq, 0), (mk, 0), mv, (o_mem, 0), (mem_lse, 0), (do_mem, 0), width=D_MEM, hd=MEM_HD,
                                 causal=False, name="mem_bwd", tq=tq, tk=256)
    dm_q, dg_mem_q = _hnorm_bwd(m_q, g_mq, dmq_n, width=D_MEM, hd=MEM_HD, name="d_mem_q_norm")
    dmk_raw, dg_mem_k = _hnorm_bwd((mkv, 0), g_mk, dmk_n, width=D_MEM, hd=MEM_HD, name="d_mem_k_norm")
    dmkv = jnp.concatenate([dmk_raw, dmv.astype(BF16)], axis=1)
    G["w_mem_kv"] = _mm(mh, dmkv, name="d_w_mem_kv", ta=True, out_dtypes=(BF16,))
    dmh = _mm(dmkv, W["w_mem_kv"], name="d_mem_h", tb=True)
    _, dg_mem = _rmsnorm_bwd(mem, small["g_mem_norm"], dmh, name="d_mem_norm")
    dg_fox_q, dg_fox_k = dg_fox_q[:, :HD], dg_fox_k[:, :HD]

    dproj = jnp.concatenate(
        [dgate] + [t.astype(BF16) for t in (*dsb, dfx_q, dfx_k, dfv, dm_q)]
        + [d_flogit_t.T.astype(BF16), jnp.zeros((S, F_PAD - FOX_HEADS), BF16)], axis=1)
    g_w_in = _mm(h, dproj, name="d_w_in", ta=True, out_dtypes=(BF16,), tn=768)
    if reduce_late is None:
        G["w_in"] = g_w_in
        dh = _mm(dproj, W["w_in"], name="d_mix_in", tb=True, tk=768)
    else:
        comm, finish = reduce_late({"w_in": g_w_in, "w_mem_kv": G.pop("w_mem_kv")})
        dh, *landed = _mm(dproj, W["w_in"], name="d_mix_in", tb=True, tk=768, comm=comm)
        early.update(finish(landed))
    grad_x, dg_mix = _rmsnorm_bwd(x, small["g_mix_norm"], dh, add=dx1, name="d_mix_norm")

    small_grads = dict(g_mix_norm=dg_mix, g_mem_norm=dg_mem, b_forget=db_forget.reshape(1, FOX_HEADS),
                       g_fox_q=dg_fox_q, g_fox_k=dg_fox_k, g_mem_q=dg_mem_q, g_mem_k=dg_mem_k, g_mlp_norm=dg_mlp)
    return loss, grad_x, G, small_grads, early


BIG = ("w_in", "w_mem_kv", "w_branch_sb", "w_branch_fox", "w_branch_mem", "w_out", "w_ff_up", "w_ff_down")
COLUMN_SHARDED = ("w_in", "w_branch_sb", "w_branch_fox", "w_branch_mem", "w_ff_up")
SMALL = ("g_mix_norm", "g_mem_norm", "b_forget", "g_fox_q", "g_fox_k", "g_mem_q", "g_mem_k", "g_mlp_norm")
ORDER = ("g_mix_norm", "g_mem_norm", "w_in", "b_forget", "g_fox_q", "g_fox_k", "g_mem_q", "g_mem_k", "w_mem_kv",
         "w_branch_sb", "w_branch_fox", "w_branch_mem", "w_out", "g_mlp_norm", "w_ff_up", "w_ff_down")


def _unshard(name, gathered):
    n, _, rh, c = gathered.shape
    t = gathered.reshape(n, 2 * rh, c)
    if name in COLUMN_SHARDED:
        return t.transpose(1, 0, 2).reshape(2 * rh, n * c)
    return t.reshape(n * 2 * rh, c)


def _reshard(name, full):
    if name in COLUMN_SHARDED:
        r, c = full.shape
        t = full.reshape(r, N_CHIPS, c // N_CHIPS).transpose(1, 0, 2)
    else:
        r, c = full.shape[0] // N_CHIPS, full.shape[1]
        t = full.reshape(N_CHIPS, r, c)
    return t.reshape(N_CHIPS, 2, t.shape[1] // 2, t.shape[2])


def _pad_in_proj(w_in, D):
    n_qkv = 6 * D_SB
    o_mq = n_qkv + FOX_HEADS
    o_gate = o_mq + D_MEM
    return jnp.concatenate([w_in[:, o_gate:], w_in[:, :n_qkv], w_in[:, o_mq:o_gate], w_in[:, n_qkv:o_mq],
                            jnp.zeros((w_in.shape[0], F_PAD - FOX_HEADS), w_in.dtype)], axis=1)


def _unpad_in_proj(g, D):
    n_qkv = 6 * D_SB
    o_qkv, o_mq, o_f = 3 * D, 3 * D + n_qkv, 3 * D + n_qkv + D_MEM
    return jnp.concatenate([g[:, o_qkv:o_mq], g[:, o_f:o_f + FOX_HEADS], g[:, o_mq:o_f], g[:, :o_qkv]], axis=1)


def _pack_small(vals):
    width = max(vals[k].shape[1] for k in SMALL)
    return jnp.concatenate([jnp.pad(vals[k].astype(F32), ((0, 0), (0, width - vals[k].shape[1]))) for k in SMALL], axis=0)


def _unpack_small(packed, like):
    return {k: packed[i:i + 1, :like[k].shape[1]] for i, k in enumerate(SMALL)}


def kernel(x, mem, g_mix_norm, g_mem_norm, w_in, b_forget, g_fox_q, g_fox_k, g_mem_q, g_mem_k, w_mem_kv, w_branch_sb, w_branch_fox, w_branch_mem, w_out, g_mlp_norm, w_ff_up, w_ff_down, loss_target, m_g_mix_norm, m_g_mem_norm, m_w_in, m_b_forget, m_g_fox_q, m_g_fox_k, m_g_mem_q, m_g_mem_k, m_w_mem_kv, m_w_branch_sb, m_w_branch_fox, m_w_branch_mem, m_w_out, m_g_mlp_norm, m_w_ff_up, m_w_ff_down, v_g_mix_norm, v_g_mem_norm, v_w_in, v_b_forget, v_g_fox_q, v_g_fox_k, v_g_mem_q, v_g_mem_k, v_w_mem_kv, v_w_branch_sb, v_w_branch_fox, v_w_branch_mem, v_w_out, v_g_mlp_norm, v_w_ff_up, v_w_ff_down):
    given = dict(locals())
    D = x.shape[-1]
    weights = {k: given[k] for k in ORDER}
    moms = {k: given["m_" + k] for k in ORDER}
    vars_ = {k: given["v_" + k] for k in ORDER}

    me_chip = 2 * lax.axis_index("x") + lax.axis_index("y")

    shards = {}
    for k in BIG:
        w = weights[k][0].astype(BF16)
        shards[k] = w.reshape(2, w.shape[0] // 2, w.shape[1])
    W = {"w_in": _pad_in_proj(_unshard("w_in", _gather_weights([shards["w_in"]], name="gather_w_in")[0]), D)}
    rest = [k for k in BIG if k != "w_in"]

    def finish_weights(landed):
        full = _forward_halves(landed, name="forward_halves")
        full = [lax.dynamic_update_index_in_dim(o, shards[k], me_chip, 0) for k, o in zip(rest, full)]
        return {k: _unshard(k, g) for k, g in zip(rest, full)}

    def sum_pairs(names, own, sib):
        add2 = lambda p, q: (p.astype(F32) + q.astype(F32),)
        out = []
        for k, p, q in zip(names, own, sib):
            n, r, c = p.shape
            out.append(_ew(add2, [p.reshape(n * r, c), q.reshape(n * r, c)], (BF16,), name="sum_pair_" + k).reshape(n, r, c))
        return out

    def sum_chips(names, parts, got):
        add4 = lambda p, q0, q1, q2: (((p.astype(F32) + q0.astype(F32)) + q1.astype(F32)) + q2.astype(F32),)
        own = [lax.dynamic_index_in_dim(p, me_chip, 0, keepdims=False) for p in parts]
        return {k: _ew(add4, [p, q[0], q[1], q[2]], (F32,), name="sum_chips_" + k) for k, p, q in zip(names, own, got)}

    def reducer(tag):
        def reduce(grads):
            names = list(grads)
            if "w_in" in grads:
                grads = {**grads, "w_in": _unpad_in_proj(grads["w_in"], D)}
            own, sib = _exchange_siblings([_reshard(k, grads[k]) for k in names], name="exchange_siblings_" + tag)
            parts = sum_pairs(names, own, sib)
            return _ChipExchange("scatter", parts), functools.partial(sum_chips, names, parts)
        return reduce

    small = {k: weights[k] for k in SMALL}
    loss_part, grad_x, G, small_grads, halves = _local_step(
        x[0], mem[0], loss_target[0], small, W,
        gather_rest=(_ChipExchange("gather", [shards[k] for k in rest]), finish_weights),
        reduce_early=reducer("early"), reduce_late=reducer("late"))
    assert not G, list(G)
    reduced, small_parts = _share_halves([halves[k] for k in BIG], _pack_small(small_grads))

    grads, deltas, new_m, new_v = {}, {}, {}, {}
    for k, g in zip(BIG, reduced):
        shp = weights[k].shape
        g2 = g.reshape(shp[1], shp[2])
        d, m2, v2 = _adamw(weights[k][0], g2, moms[k][0], vars_[k][0], name="adamw_" + k)
        grads[k], deltas[k], new_m[k], new_v[k] = g2.reshape(shp), d.reshape(shp), m2.reshape(shp), v2.reshape(shp)
    sg, sd, sm, sv = _adamw_small(_pack_small(small), small_parts, _pack_small({k: moms[k] for k in SMALL}),
                                  _pack_small({k: vars_[k] for k in SMALL}), name="adamw_small")
    for dst, packed in ((grads, sg), (deltas, sd), (new_m, sm), (new_v, sv)):
        dst.update(_unpack_small(packed, small))

    loss = lax.psum(loss_part, ("x", "y", "c"))
    return (loss, grad_x[None], *[grads[k] for k in ORDER], *[deltas[k] for k in ORDER],
            *[new_m[k] for k in ORDER], *[new_v[k] for k in ORDER])
```

```python
import functools

import jax
import jax.numpy as jnp
from jax import lax
from jax.experimental import pallas as pl
from jax.experimental.pallas import tpu as pltpu

F32 = jnp.float32
BF16 = jnp.bfloat16
MESH_ID = pl.DeviceIdType.MESH

HD = 64
SB_HEADS = 8
FOX_HEADS = 8
MEM_HEADS = 4
MEM_HD = 128
D_SB = SB_HEADS * HD
D_FOX = FOX_HEADS * HD
D_MEM = MEM_HEADS * MEM_HD
EPS = 1e-6
NEG_INF = -1e30

ADAM_LR = 0.001
ADAM_B1 = 0.9
ADAM_B2 = 0.999
ADAM_EPS = 1e-08
ADAM_WD = 0.01
ADAM_STEP = 10

N_CHIPS = 4
VMEM_LIMIT = 56 * 1024 * 1024

F_PAD = 256


def _tile(n, target, align=128):
    if n <= target:
        return n
    best = None
    t = align
    while t <= target:
        if n % t == 0:
            best = t
        t += align
    assert best is not None, (n, target, align)
    return best


def _params(sem):
    return pltpu.CompilerParams(dimension_semantics=sem, vmem_limit_bytes=VMEM_LIMIT)


def _mm(a, b, *, name, ta=False, tb=False, out_dtypes=(F32,), epilogue=None, extras=(),
        tm=1024, tn=1024, tk=2048, comm=None):
    if ta:
        K, M = a.shape
    else:
        M, K = a.shape
    if tb:
        N, K2 = b.shape
    else:
        K2, N = b.shape
    assert K == K2, (a.shape, b.shape, ta, tb)
    tm, tn, tk = _tile(M, tm), _tile(N, tn), _tile(K, tk)
    nk = K // tk
    n_extra, n_out = len(extras), len(out_dtypes)
    if epilogue is None:
        epilogue = lambda acc: (acc,)
    dims = (((0 if ta else 1,), (1 if tb else 0,)), ((), ()))

    gm, gn = M // tm, N // tn

    def body(*refs):
        i, j, k = pl.program_id(0), pl.program_id(1), pl.program_id(2)
        got = _carry(comm, 2 + n_extra, n_out, (i == 0) & (j == 0) & (k == 0),
                     (i == gm - 1) & (j == gn - 1) & (k == nk - 1), refs)
        (a_ref, b_ref, *extra_refs), out_refs = got[0], got[1]
        part = lax.dot_general(a_ref[...].astype(BF16), b_ref[...].astype(BF16), dims, preferred_element_type=F32)

        def finish(acc):
            outs = epilogue(acc, *[r[...] for r in extra_refs])
            for o_ref, o in zip(out_refs, outs):
                o_ref[...] = o.astype(o_ref.dtype)

        if nk == 1:
            finish(part)
        else:
            acc_ref = refs[-1]

            @pl.when(k == 0)
            def _():
                acc_ref[...] = part

            @pl.when((k > 0) & (k < nk - 1))
            def _():
                acc_ref[...] += part

            @pl.when(k == nk - 1)
            def _():
                finish(acc_ref[...] + part)

        got[2]()

    a_spec = pl.BlockSpec((tk, tm), lambda i, j, k: (k, i)) if ta else pl.BlockSpec((tm, tk), lambda i, j, k: (i, k))
    b_spec = pl.BlockSpec((tn, tk), lambda i, j, k: (j, k)) if tb else pl.BlockSpec((tk, tn), lambda i, j, k: (k, j))
    mn_spec = pl.BlockSpec((tm, tn), lambda i, j, k: (i, j))
    c_ins, c_in_specs, c_out_specs, c_out_shape, c_scratch = _comm_args(comm)
    sem = ("parallel", "parallel", "arbitrary") if comm is None else ("arbitrary",) * 3
    outs = pl.pallas_call(
        body, name=name,
        grid=(gm, gn, nk),
        in_specs=[a_spec, b_spec] + [mn_spec] * n_extra + c_in_specs,
        out_specs=[mn_spec] * n_out + c_out_specs,
        out_shape=[jax.ShapeDtypeStruct((M, N), dt) for dt in out_dtypes] + c_out_shape,
        scratch_shapes=c_scratch + ([pltpu.VMEM((tm, tn), F32)] if nk > 1 else []),
        compiler_params=_params(sem),
    )(a, b, *extras, *c_ins)
    return outs if len(outs) > 1 else outs[0]


def _row_tile(rows, cols, n_arrays):
    budget = 10 * 1024 * 1024
    cols_padded = -(-cols // 128) * 128
    target = max(16, budget // (cols_padded * 4 * n_arrays * 2))
    return _tile(rows, target, align=16)


def _ew(fn, ins, out_dtypes, *, name):
    R, C = ins[0].shape
    n_in, n_out = len(ins), len(out_dtypes)
    tr = _row_tile(R, C, n_in + n_out)

    def body(*refs):
        outs = fn(*[r[...] for r in refs[:n_in]])
        for o_ref, o in zip(refs[n_in:], outs):
            o_ref[...] = o.astype(o_ref.dtype)

    spec = pl.BlockSpec((tr, C), lambda i: (i, 0))
    outs = pl.pallas_call(
        body, name=name, grid=(R // tr,),
        in_specs=[spec] * n_in, out_specs=[spec] * n_out,
        out_shape=[jax.ShapeDtypeStruct((R, C), dt) for dt in out_dtypes],
        compiler_params=_params(("parallel",)),
    )(*ins)
    return outs if n_out > 1 else outs[0]


def _rmsnorm_fwd(x, g, out_dtype, *, name):
    R, d = x.shape
    tr = _row_tile(R, d, 3)

    def body(x_ref, g_ref, o_ref):
        xv = x_ref[...].astype(F32)
        r = lax.rsqrt(jnp.mean(xv * xv, axis=-1, keepdims=True) + EPS)
        o_ref[...] = (xv * r * g_ref[...]).astype(o_ref.dtype)

    return pl.pallas_call(
        body, name=name, grid=(R // tr,),
        in_specs=[pl.BlockSpec((tr, d), lambda i: (i, 0)), pl.BlockSpec((1, d), lambda i: (0, 0))],
        out_specs=pl.BlockSpec((tr, d), lambda i: (i, 0)),
        out_shape=jax.ShapeDtypeStruct((R, d), out_dtype),
        compiler_params=_params(("parallel",)),
    )(x, g)


def _rmsnorm_bwd(x, g, dy, add=None, *, name):
    R, d = x.shape
    has_add = add is not None
    tr = _row_tile(R, d, 5)

    def body(*refs):
        x_ref, g_ref, dy_ref = refs[:3]
        add_ref = refs[3] if has_add else None
        dx_ref, dg_ref = refs[-2:]
        xv = x_ref[...].astype(F32)
        dyv = dy_ref[...].astype(F32)
        r = lax.rsqrt(jnp.mean(xv * xv, axis=-1, keepdims=True) + EPS)
        xh = xv * r
        dyg = dyv * g_ref[...]
        c = jnp.mean(dyg * xh, axis=-1, keepdims=True)
        dx = r * (dyg - xh * c)
        if has_add:
            dx = dx + add_ref[...]
        dx_ref[...] = dx

        @pl.when(pl.program_id(0) == 0)
        def _():
            dg_ref[...] = jnp.zeros_like(dg_ref)

        dg_ref[...] += jnp.sum(dyv * xh, axis=0, keepdims=True)

    row = pl.BlockSpec((tr, d), lambda i: (i, 0))
    vec = pl.BlockSpec((1, d), lambda i: (0, 0))
    ins = [x, g, dy] + ([add] if has_add else [])
    return pl.pallas_call(
        body, name=name, grid=(R // tr,),
        in_specs=[row, vec, row] + ([row] if has_add else []),
        out_specs=[row, vec],
        out_shape=[jax.ShapeDtypeStruct((R, d), F32), jax.ShapeDtypeStruct((1, d), F32)],
        compiler_params=_params(("arbitrary",)),
    )(*ins)


_NT = (((1,), (1,)), ((), ()))
_TN = (((0,), (0,)), ((), ()))


def _dot(a, b, dims=(((1,), (0,)), ((), ()))):
    return lax.dot_general(a, b, dims, preferred_element_type=F32)


def _split_dot(x, tri):
    hi = x.astype(BF16)
    lo = (x - hi.astype(F32)).astype(BF16)
    return _dot(hi, tri) + _dot(lo, tri)


def _log_sigmoid_pair(z):
    sp = jnp.log(1.0 + jnp.exp(-jnp.abs(z)))
    return jnp.minimum(z, 0.0) - sp, jnp.minimum(-z, 0.0) - sp


def _sb_fwd(q, k, v, *, name, tq=256):
    H, S, hd = q.shape
    tq = _tile(S, tq)
    tk = tq
    scale = hd ** -0.5

    def body(q_ref, k_ref, v_ref, o_ref, tot_ref):
        qi = pl.program_id(1)
        qv = q_ref[0]
        row = qi * tq + lax.broadcasted_iota(jnp.int32, (tq, tk), 0)
        col0 = lax.broadcasted_iota(jnp.int32, (tq, tk), 1)
        later = (lax.broadcasted_iota(jnp.int32, (tk, tk), 0) > lax.broadcasted_iota(jnp.int32, (tk, tk), 1)).astype(BF16)

        def step(i, carry):
            acc, c_rem = carry
            kb = qi - i
            ks = pl.multiple_of(kb * tk, tk)
            kv = k_ref[0, pl.ds(ks, tk), :]
            vv = v_ref[0, pl.ds(ks, tk), :]
            z = _dot(qv, kv, _NT) * scale
            mask = (col0 + ks) < row
            ls, lsn = _log_sigmoid_pair(z)
            rem = jnp.where(mask, lsn, 0.0)
            after = _split_dot(rem, later) + c_rem
            w = jnp.where(mask, jnp.exp(ls + after), 0.0)
            acc = acc + _dot(w.astype(BF16), vv)
            c_rem = c_rem + jnp.sum(rem, axis=1, keepdims=True)
            return acc, c_rem

        acc, c_rem = lax.fori_loop(0, qi + 1, step, (jnp.zeros((tq, hd), F32), jnp.zeros((tq, 1), F32)))
        o_ref[0] = acc
        tot_ref[0] = c_rem

    qspec = pl.BlockSpec((1, tq, hd), lambda h, i: (h, i, 0))
    kspec = pl.BlockSpec((1, S, hd), lambda h, i: (h, 0, 0))
    return pl.pallas_call(
        body, name=name, grid=(H, S // tq),
        in_specs=[qspec, kspec, kspec],
        out_specs=[qspec, pl.BlockSpec((1, tq, 1), lambda h, i: (h, i, 0))],
        out_shape=[jax.ShapeDtypeStruct((H, S, hd), F32), jax.ShapeDtypeStruct((H, S, 1), F32)],
        compiler_params=_params(("parallel", "arbitrary")),
    )(q, k, v)


def _sb_bwd(q, k, v, do, tot, *, name, tq=256):
    H, S, hd = q.shape
    tq = _tile(S, tq)
    tk = tq
    scale = hd ** -0.5

    def body(q_ref, k_ref, v_ref, do_ref, tot_ref, dq_ref, dk_ref, dv_ref):
        qi = pl.program_id(1)

        @pl.when(qi == 0)
        def _():
            dk_ref[...] = jnp.zeros_like(dk_ref)
            dv_ref[...] = jnp.zeros_like(dv_ref)

        qv = q_ref[0]
        dov = do_ref[0].astype(BF16)
        tot_v = tot_ref[0]
        row = qi * tq + lax.broadcasted_iota(jnp.int32, (tq, tk), 0)
        col0 = lax.broadcasted_iota(jnp.int32, (tq, tk), 1)
        r_i = lax.broadcasted_iota(jnp.int32, (tk, tk), 0)
        c_i = lax.broadcasted_iota(jnp.int32, (tk, tk), 1)
        upto = (r_i <= c_i).astype(BF16)
        before = (r_i < c_i).astype(BF16)

        def step(kb, carry):
            dq, c_rem, c_g = carry
            ks = pl.multiple_of(kb * tk, tk)
            kv = k_ref[0, pl.ds(ks, tk), :]
            vv = v_ref[0, pl.ds(ks, tk), :]
            z = _dot(qv, kv, _NT) * scale
            mask = (col0 + ks) < row
            ls, lsn = _log_sigmoid_pair(z)
            rem = jnp.where(mask, lsn, 0.0)
            after = tot_v - (_split_dot(rem, upto) + c_rem)
            w = jnp.where(mask, jnp.exp(ls + after), 0.0)
            dw = _dot(dov, vv, _NT)
            g = dw * w
            g_before = _split_dot(g, before) + c_g
            beta = jnp.exp(ls)
            dz = jnp.where(mask, g * (1.0 - beta) - beta * g_before, 0.0) * scale
            dzb = dz.astype(BF16)
            dq = dq + _dot(dzb, kv)
            dk_ref[0, pl.ds(ks, tk), :] += _dot(dzb, qv, _TN)
            dv_ref[0, pl.ds(ks, tk), :] += _dot(w.astype(BF16), dov, _TN)
            return dq, c_rem + jnp.sum(rem, axis=1, keepdims=True), c_g + jnp.sum(g, axis=1, keepdims=True)

        zero = jnp.zeros((tq, 1), F32)
        dq, _, _ = lax.fori_loop(0, qi + 1, step, (jnp.zeros((tq, hd), F32), zero, zero))
        dq_ref[0] = dq

    qspec = pl.BlockSpec((1, tq, hd), lambda h, i: (h, i, 0))
    kspec = pl.BlockSpec((1, S, hd), lambda h, i: (h, 0, 0))
    full = jax.ShapeDtypeStruct((H, S, hd), F32)
    return pl.pallas_call(
        body, name=name, grid=(H, S // tq),
        in_specs=[qspec, kspec, kspec, qspec, pl.BlockSpec((1, tq, 1), lambda h, i: (h, i, 0))],
        out_specs=[qspec, kspec, kspec],
        out_shape=[full, full, full],
        compiler_params=_params(("parallel", "arbitrary")),
    )(q, k, v, do, tot)


def _softmax_fwd(q, k, v, bias=None, *, name, causal, tq=256, tk=256):
    H, S, hd = q.shape
    Sk = k.shape[1]
    tq, tk = _tile(S, tq), _tile(Sk, tk)
    if causal:
        assert tq == tk and S == Sk
    nk = Sk // tk
    scale = hd ** -0.5
    has_bias = bias is not None

    def body(*refs):
        q_ref, k_ref, v_ref = refs[:3]
        o_ref, lse_ref = refs[-2:]
        qi = pl.program_id(1)
        qv = q_ref[0]
        row = qi * tq + lax.broadcasted_iota(jnp.int32, (tq, tk), 0)
        col0 = lax.broadcasted_iota(jnp.int32, (tq, tk), 1)
        if has_bias:
            b_row = refs[3][0]

        def step(kb, carry):
            m, l, acc = carry
            ks = pl.multiple_of(kb * tk, tk)
            kv = k_ref[0, pl.ds(ks, tk), :]
            vv = v_ref[0, pl.ds(ks, tk), :]
            z = _dot(qv, kv, _NT) * scale
            if has_bias:
                z = z + b_row - refs[4][0, kb]
            if causal:
                z = jnp.where((col0 + ks) <= row, z, NEG_INF)
            m2 = jnp.maximum(m, jnp.max(z, axis=1, keepdims=True))
            p = jnp.exp(z - m2)
            alpha = jnp.exp(m - m2)
            l = alpha * l + jnp.sum(p, axis=1, keepdims=True)
            acc = alpha * acc + _dot(p.astype(BF16), vv)
            return m2, l, acc

        init = (jnp.full((tq, 1), NEG_INF, F32), jnp.zeros((tq, 1), F32), jnp.zeros((tq, hd), F32))
        m, l, acc = lax.fori_loop(0, (qi + 1) if causal else nk, step, init)
        o_ref[0] = acc / l
        lse_ref[0] = m + jnp.log(l)

    qspec = pl.BlockSpec((1, tq, hd), lambda h, i: (h, i, 0))
    kspec = pl.BlockSpec((1, Sk, hd), lambda h, i: (h, 0, 0))
    vspec = pl.BlockSpec((1, tq, 1), lambda h, i: (h, i, 0))
    in_specs = [qspec, kspec, kspec]
    ins = [q, k, v]
    if has_bias:
        in_specs += [vspec, pl.BlockSpec((1, nk, 1, tk), lambda h, i: (h, 0, 0, 0))]
        ins += list(bias)
    return pl.pallas_call(
        body, name=name, grid=(H, S // tq),
        in_specs=in_specs, out_specs=[qspec, vspec],
        out_shape=[jax.ShapeDtypeStruct((H, S, hd), F32), jax.ShapeDtypeStruct((H, S, 1), F32)],
        compiler_params=_params(("parallel", "arbitrary")),
    )(*ins)


def _softmax_bwd(q, k, v, o, lse, do, bias=None, *, name, causal, tq=256, tk=256):
    H, S, hd = q.shape
    Sk = k.shape[1]
    tq, tk = _tile(S, tq), _tile(Sk, tk)
    nk = Sk // tk
    scale = hd ** -0.5
    has_bias = bias is not None
    n_in = 8 if has_bias else 6

    def body(*refs):
        q_ref, k_ref, v_ref, o_ref, lse_ref, do_ref = refs[:6]
        dq_ref, dk_ref, dv_ref = refs[n_in:n_in + 3]
        qi = pl.program_id(1)

        @pl.when(qi == 0)
        def _():
            dk_ref[...] = jnp.zeros_like(dk_ref)
            dv_ref[...] = jnp.zeros_like(dv_ref)
            if has_bias:
                refs[n_in + 4][...] = jnp.zeros_like(refs[n_in + 4])

        qv = q_ref[0]
        do32 = do_ref[0]
        dov = do32.astype(BF16)
        delta = jnp.sum(do32 * o_ref[0], axis=1, keepdims=True)
        lse_v = lse_ref[0]
        row = qi * tq + lax.broadcasted_iota(jnp.int32, (tq, tk), 0)
        col0 = lax.broadcasted_iota(jnp.int32, (tq, tk), 1)
        if has_bias:
            b_row = refs[6][0]

        def step(kb, carry):
            dq, db_row = carry
            ks = pl.multiple_of(kb * tk, tk)
            kv = k_ref[0, pl.ds(ks, tk), :]
            vv = v_ref[0, pl.ds(ks, tk), :]
            z = _dot(qv, kv, _NT) * scale
            if has_bias:
                z = z + b_row - refs[7][0, kb]
            p = jnp.exp(z - lse_v)
            if causal:
                p = jnp.where((col0 + ks) <= row, p, 0.0)
            dp = _dot(dov, vv, _NT)
            dz = p * (dp - delta)
            dzb = dz.astype(BF16)
            dq = dq + _dot(dzb, kv)
            dk_ref[0, pl.ds(ks, tk), :] += _dot(dzb, qv, _TN) * scale
            dv_ref[0, pl.ds(ks, tk), :] += _dot(p.astype(BF16), dov, _TN)
            if has_bias:
                db_row = db_row + jnp.sum(dz, axis=1, keepdims=True)
                refs[n_in + 4][0, kb] += jnp.sum(dz, axis=0, keepdims=True)
            return dq, db_row

        dq, db_row = lax.fori_loop(0, (qi + 1) if causal else nk, step,
                                   (jnp.zeros((tq, hd), F32), jnp.zeros((tq, 1), F32)))
        dq_ref[0] = dq * scale
        if has_bias:
            refs[n_in + 3][0] = db_row

    qspec = pl.BlockSpec((1, tq, hd), lambda h, i: (h, i, 0))
    kspec = pl.BlockSpec((1, Sk, hd), lambda h, i: (h, 0, 0))
    vspec = pl.BlockSpec((1, tq, 1), lambda h, i: (h, i, 0))
    cspec = pl.BlockSpec((1, nk, 1, tk), lambda h, i: (h, 0, 0, 0))
    in_specs = [qspec, kspec, kspec, qspec, vspec, qspec]
    ins = [q, k, v, o, lse, do]
    out_specs = [qspec, kspec, kspec]
    out_shape = [jax.ShapeDtypeStruct((H, S, hd), F32), jax.ShapeDtypeStruct((H, Sk, hd), F32),
                 jax.ShapeDtypeStruct((H, Sk, hd), F32)]
    if has_bias:
        in_specs += [vspec, cspec]
        ins += list(bias)
        out_specs += [vspec, cspec]
        out_shape += [jax.ShapeDtypeStruct((H, S, 1), F32), jax.ShapeDtypeStruct((H, nk, 1, tk), F32)]
    return pl.pallas_call(
        body, name=name, grid=(H, S // tq),
        in_specs=in_specs, out_specs=out_specs, out_shape=out_shape,
        compiler_params=_params(("parallel", "arbitrary")),
    )(*ins)


LANES = 128
_LOW = -3e38


def _lane_masks(hd, rows):
    if hd == LANES:
        return [None]
    lane = lax.broadcasted_iota(jnp.int32, (rows, LANES), 1)
    return [(lane >= hh * hd) & (lane < (hh + 1) * hd) for hh in range(LANES // hd)]


def _keep(t, m):
    return t if m is None else jnp.where(m, t, 0.0)


def _merge(parts, masks):
    out = parts[-1]
    for p, m in zip(parts[-2::-1], masks[-2::-1]):
        out = jnp.where(m, p, out)
    return out


def _row_value(t, m):
    return jnp.max(t if m is None else jnp.where(m, t, _LOW), axis=1, keepdims=True)


def _cols(tq, off):
    return pl.BlockSpec((tq, LANES), lambda g, i: (i, off + g))


def _cols_all(rows, off):
    return pl.BlockSpec((rows, LANES), lambda g, i: (0, off + g))


SCAN_BLOCK = 256


def _tri(kind, cols):
    n = min(SCAN_BLOCK, cols)
    r = lax.broadcasted_iota(jnp.int32, (n, n), 0)
    c = lax.broadcasted_iota(jnp.int32, (n, n), 1)
    return ((r > c) if kind == "after" else (r < c)).astype(BF16)


def _scan_cols(x, tri, reverse):
    cols = x.shape[1]
    cb = min(SCAN_BLOCK, cols)
    assert cols % cb == 0 and tri.shape == (cb, cb)
    nb = cols // cb
    blocks = [x[:, b * cb:(b + 1) * cb] for b in range(nb)]
    outs, carry = [None] * nb, None
    for b in (reversed(range(nb)) if reverse else range(nb)):
        y = _dot(blocks[b].astype(BF16), tri)
        outs[b] = y if carry is None else y + carry
        s = jnp.sum(blocks[b], axis=1, keepdims=True)
        carry = s if carry is None else carry + s
    return (outs[0] if nb == 1 else jnp.concatenate(outs, axis=1)), carry


def _softplus_parts(z):
    pos = jnp.maximum(z, 0.0) + jnp.log(1.0 + jnp.exp(-jnp.abs(z)))
    return pos, z - pos


class _ChipExchange:
    def __init__(self, kind, ins):
        assert kind in ("gather", "scatter")
        self.kind, self.ins = kind, list(ins)
        lead = (lambda s: (N_CHIPS,) + s) if kind == "gather" else (lambda s: (3,) + s[1:])
        self.out_shape = [jax.ShapeDtypeStruct(lead(a.shape), a.dtype) for a in ins]
        n = 3 * len(ins)
        self.scratch = [pltpu.SemaphoreType.DMA((n,)), pltpu.SemaphoreType.DMA((n,))]

    def _copies(self, in_refs, out_refs, sems, landing):
        x, y, c, chips = _mesh_place()
        me = 2 * x + y
        out = []
        for w in range(len(self.ins)):
            for j, chip in enumerate(chips):
                peer = 2 * chip[0] + chip[1]
                if self.kind == "gather":
                    src, dst, land = in_refs[w].at[c], out_refs[w].at[me, c], out_refs[w].at[peer, c]
                else:
                    src, dst, land = in_refs[w].at[peer], out_refs[w].at[j], out_refs[w].at[j]
                if landing:
                    src, dst = land, land
                out.append(_remote(src, dst, sems, 3 * w + j, (chip[0], chip[1], c)))
        return out

    def start(self, in_refs, out_refs, sems):
        for cp in self._copies(in_refs, out_refs, sems, False):
            cp.start()

    def finish(self, in_refs, out_refs, sems):
        for cp in self._copies(in_refs, out_refs, sems, True):
            cp.wait_recv()
        for cp in self._copies(in_refs, out_refs, sems, False):
            cp.wait_send()


def _carry(comm, n_in, n_out, first, last, refs):
    if comm is None:
        return refs[:n_in], refs[n_in:n_in + n_out], (lambda: None)
    a, b = len(comm.ins), len(comm.out_shape)
    ins, c_in = refs[:n_in], refs[n_in:n_in + a]
    outs, c_out = refs[n_in + a:n_in + a + n_out], refs[n_in + a + n_out:n_in + a + n_out + b]
    sems = refs[n_in + a + n_out + b:n_in + a + n_out + b + 2]
    pl.when(first)(lambda: comm.start(c_in, c_out, sems))
    return ins, outs, (lambda: pl.when(last)(lambda: comm.finish(c_in, c_out, sems)))


def _sbl_fwd(q, k, v, *, width, hd, name, tq=256, comm=None):
    (qa, qo), (ka, ko), (va, vo) = q, k, v
    S = qa.shape[0]
    tq = _tile(S, tq)
    tk = tq
    scale = hd ** -0.5
    n_g, n_q = width // LANES, S // tq

    def body(*refs):
        qi = pl.program_id(1)
        gi = pl.program_id(0)
        got = _carry(comm, 3, 2, (gi == 0) & (qi == 0), (gi == n_g - 1) & (qi == n_q - 1), refs)
        (q_ref, k_ref, v_ref), (o_ref, tot_ref) = got[0], got[1]
        masks = _lane_masks(hd, tq)
        qs = q_ref[...].astype(F32) * scale
        qm = [_keep(qs, m).astype(BF16) for m in masks]
        strict = lax.broadcasted_iota(jnp.int32, (tq, tk), 1) < lax.broadcasted_iota(jnp.int32, (tq, tk), 0)
        later = _tri("after", tk)

        def tile(kb, carry, diag):
            ks = pl.multiple_of(kb * tk, tk)
            kv = k_ref[pl.ds(ks, tk), :].astype(BF16)
            vv = v_ref[pl.ds(ks, tk), :].astype(BF16)
            out = []
            for hh in range(len(masks)):
                acc, c_pos = carry[2 * hh], carry[2 * hh + 1]
                pos, ls = _softplus_parts(_dot(qm[hh], kv, _NT))
                if diag:
                    pos = jnp.where(strict, pos, 0.0)
                pos_after, pos_all = _scan_cols(pos, later, True)
                w = jnp.exp(ls - (pos_after + c_pos))
                if diag:
                    w = jnp.where(strict, w, 0.0)
                out += [acc + _dot(w.astype(BF16), vv), c_pos + pos_all]
            return tuple(out)

        init = (jnp.zeros((tq, LANES), F32), jnp.zeros((tq, 1), F32)) * len(masks)
        carry = tile(qi, init, True)
        carry = lax.fori_loop(0, qi, lambda i, c: tile(qi - 1 - i, c, False), carry)
        o_ref[...] = _merge(carry[0::2], masks).astype(o_ref.dtype)
        tot_ref[...] = _merge([jnp.broadcast_to(-c, (tq, LANES)) for c in carry[1::2]], masks)
        got[2]()

    c_ins, c_in_specs, c_out_specs, c_out_shape, c_scratch = _comm_args(comm)
    return pl.pallas_call(
        body, name=name, grid=(n_g, n_q),
        in_specs=[_cols(tq, qo), _cols_all(S, ko), _cols_all(S, vo)] + c_in_specs,
        out_specs=[_cols(tq, 0), _cols(tq, 0)] + c_out_specs,
        out_shape=[jax.ShapeDtypeStruct((S, width), BF16), jax.ShapeDtypeStruct((S, width), F32)] + c_out_shape,
        scratch_shapes=c_scratch,
        compiler_params=_params(("arbitrary", "arbitrary")),
    )(qa, ka, va, *c_ins)


def _comm_args(comm):
    if comm is None:
        return [], [], [], [], []
    return comm.ins, [_ANY] * len(comm.ins), [_ANY] * len(comm.out_shape), comm.out_shape, comm.scratch


def _sbl_bwd(q, k, v, do, tot, *, width, hd, name, tq=256, comm=None):
    (qa, qo), (ka, ko), (va, vo) = q, k, v
    S = qa.shape[0]
    tq = _tile(S, tq)
    tk = tq
    scale = hd ** -0.5
    n_g, n_q = width // LANES, S // tq

    def body(*refs):
        qi = pl.program_id(1)
        gi = pl.program_id(0)
        got = _carry(comm, 5, 3, (gi == 0) & (qi == 0), (gi == n_g - 1) & (qi == n_q - 1), refs)
        (q_ref, k_ref, v_ref, do_ref, tot_ref), (dq_ref, dk_ref, dv_ref) = got[0], got[1]

        @pl.when(qi == 0)
        def _():
            dk_ref[...] = jnp.zeros_like(dk_ref)
            dv_ref[...] = jnp.zeros_like(dv_ref)

        masks = _lane_masks(hd, tq)
        qs = q_ref[...].astype(F32) * scale
        qm = [_keep(qs, m).astype(BF16) for m in masks]
        dov = [_keep(do_ref[...], m).astype(BF16) for m in masks]
        rest = [-_row_value(tot_ref[...], m) for m in masks]
        strict = lax.broadcasted_iota(jnp.int32, (tq, tk), 1) < lax.broadcasted_iota(jnp.int32, (tq, tk), 0)
        later, before = _tri("after", tk), _tri("before", tk)

        def tile(kb, carry, diag):
            ks = pl.multiple_of(kb * tk, tk)
            kv = k_ref[pl.ds(ks, tk), :].astype(BF16)
            vv = v_ref[pl.ds(ks, tk), :].astype(BF16)
            out = []
            dk_t, dv_t = None, None
            for hh in range(len(masks)):
                dq, c_pos, c_g = carry[3 * hh:3 * hh + 3]
                pos, ls = _softplus_parts(_dot(qm[hh], kv, _NT))
                if diag:
                    pos = jnp.where(strict, pos, 0.0)
                pos_after, pos_all = _scan_cols(pos, later, True)
                c_pos = c_pos + pos_all
                w = jnp.exp(ls - (pos_after + (rest[hh] - c_pos)))
                if diag:
                    w = jnp.where(strict, w, 0.0)
                g = _dot(dov[hh], vv, _NT) * w
                g_before, g_all = _scan_cols(g, before, False)
                g_before = g_before + c_g
                dz = g - jnp.exp(ls) * (g + g_before)
                if diag:
                    dz = jnp.where(strict, dz, 0.0)
                dzb = dz.astype(BF16)
                dk_h = _dot(dzb, qm[hh], _TN)
                dv_h = _dot(w.astype(BF16), dov[hh], _TN)
                dk_t = dk_h if dk_t is None else dk_t + dk_h
                dv_t = dv_h if dv_t is None else dv_t + dv_h
                out += [dq + _dot(dzb, kv), c_pos, c_g + g_all]
            dk_ref[pl.ds(ks, tk), :] += dk_t
            dv_ref[pl.ds(ks, tk), :] += dv_t
            return tuple(out)

        zero = jnp.zeros((tq, 1), F32)
        init = (jnp.zeros((tq, LANES), F32), zero, zero) * len(masks)
        carry = lax.fori_loop(0, qi, lambda kb, c: tile(kb, c, False), init)
        carry = tile(qi, carry, True)
        dq_ref[...] = _merge(carry[0::3], masks) * scale
        got[2]()

    full = jax.ShapeDtypeStruct((S, width), F32)
    c_ins, c_in_specs, c_out_specs, c_out_shape, c_scratch = _comm_args(comm)
    return pl.pallas_call(
        body, name=name, grid=(n_g, n_q),
        in_specs=[_cols(tq, qo), _cols_all(S, ko), _cols_all(S, vo), _cols(tq, do[1]), _cols(tq, tot[1])] + c_in_specs,
        out_specs=[_cols(tq, 0), _cols_all(S, 0), _cols_all(S, 0)] + c_out_specs,
        out_shape=[full, full, full] + c_out_shape,
        scratch_shapes=c_scratch,
        compiler_params=_params(("arbitrary", "arbitrary")),
    )(qa, ka, va, do[0], tot[0], *c_ins)


def _sml_fwd(q, k, v, bias=None, *, width, hd, causal, name, tq=256, tk=256):
    (qa, qo), (ka, ko), (va, vo) = q, k, v
    S, Sk = qa.shape[0], ka.shape[0]
    tq, tk = _tile(S, tq), _tile(Sk, tk)
    if causal:
        assert tq == tk and S == Sk
    nk = Sk // tk
    hpg = LANES // hd
    scale = hd ** -0.5
    has_bias = bias is not None

    def body(*refs):
        q_ref, k_ref, v_ref = refs[:3]
        o_ref, lse_ref = refs[-2:]
        qi = pl.program_id(1)
        masks = _lane_masks(hd, tq)
        qs = q_ref[...].astype(F32) * scale
        qm = [_keep(qs, m).astype(BF16) for m in masks]
        allowed = lax.broadcasted_iota(jnp.int32, (tq, tk), 1) <= lax.broadcasted_iota(jnp.int32, (tq, tk), 0)

        def tile(kb, carry, diag):
            ks = pl.multiple_of(kb * tk, tk)
            kv = k_ref[pl.ds(ks, tk), :].astype(BF16)
            vv = v_ref[pl.ds(ks, tk), :].astype(BF16)
            out = []
            for hh in range(hpg):
                m, l, acc = carry[3 * hh:3 * hh + 3]
                z = _dot(qm[hh], kv, _NT)
                if has_bias:
                    z = z + refs[3][hh] - refs[4][hh, kb]
                if diag:
                    z = jnp.where(allowed, z, NEG_INF)
                m2 = jnp.maximum(m, jnp.max(z, axis=1, keepdims=True))
                p = jnp.exp(z - m2)
                alpha = jnp.exp(m - m2)
                out += [m2, alpha * l + jnp.sum(p, axis=1, keepdims=True), alpha * acc + _dot(p.astype(BF16), vv)]
            return tuple(out)

        init = (jnp.full((tq, 1), NEG_INF, F32), jnp.zeros((tq, 1), F32), jnp.zeros((tq, LANES), F32)) * hpg
        if causal:
            carry = lax.fori_loop(0, qi, lambda kb, c: tile(kb, c, False), init)
            carry = tile(qi, carry, True)
        else:
            carry = lax.fori_loop(0, nk, lambda kb, c: tile(kb, c, False), init)
        o_ref[...] = _merge([acc / l for l, acc in zip(carry[1::3], carry[2::3])], masks).astype(o_ref.dtype)
        lse_ref[...] = _merge([jnp.broadcast_to(m + jnp.log(l), (tq, LANES)) for m, l in zip(carry[0::3], carry[1::3])], masks)

    in_specs = [_cols(tq, qo), _cols_all(Sk, ko), _cols_all(Sk, vo)]
    ins = [qa, ka, va]
    if has_bias:
        in_specs += [pl.BlockSpec((hpg, tq, 1), lambda g, i: (g, i, 0)),
                     pl.BlockSpec((hpg, nk, 1, tk), lambda g, i: (g, 0, 0, 0))]
        ins += list(bias)
    return pl.pallas_call(
        body, name=name, grid=(width // LANES, S // tq),
        in_specs=in_specs, out_specs=[_cols(tq, 0), _cols(tq, 0)],
        out_shape=[jax.ShapeDtypeStruct((S, width), BF16), jax.ShapeDtypeStruct((S, width), F32)],
        compiler_params=_params(("parallel", "arbitrary")),
    )(*ins)


def _sml_bwd(q, k, v, o, lse, do, bias=None, *, width, hd, causal, name, tq=256, tk=256):
    (qa, qo), (ka, ko), (va, vo) = q, k, v
    S, Sk = qa.shape[0], ka.shape[0]
    tq, tk = _tile(S, tq), _tile(Sk, tk)
    nk = Sk // tk
    hpg = LANES // hd
    scale = hd ** -0.5
    has_bias = bias is not None
    n_in = 8 if has_bias else 6

    def body(*refs):
        q_ref, k_ref, v_ref, o_ref, lse_ref, do_ref = refs[:6]
        dq_ref, dk_ref, dv_ref = refs[n_in:n_in + 3]
        qi = pl.program_id(1)

        @pl.when(qi == 0)
        def _():
            dk_ref[...] = jnp.zeros_like(dk_ref)
            dv_ref[...] = jnp.zeros_like(dv_ref)
            if has_bias:
                refs[n_in + 4][...] = jnp.zeros_like(refs[n_in + 4])

        masks = _lane_masks(hd, tq)
        qs = q_ref[...].astype(F32) * scale
        qm = [_keep(qs, m).astype(BF16) for m in masks]
        do32 = do_ref[...]
        dov = [_keep(do32, m).astype(BF16) for m in masks]
        prod = do32 * o_ref[...].astype(F32)
        delta = [jnp.sum(_keep(prod, m), axis=1, keepdims=True) for m in masks]
        lses = [_row_value(lse_ref[...], m) for m in masks]
        allowed = lax.broadcasted_iota(jnp.int32, (tq, tk), 1) <= lax.broadcasted_iota(jnp.int32, (tq, tk), 0)

        def tile(kb, carry, diag):
            ks = pl.multiple_of(kb * tk, tk)
            kv = k_ref[pl.ds(ks, tk), :].astype(BF16)
            vv = v_ref[pl.ds(ks, tk), :].astype(BF16)
            out = []
            dk_t, dv_t = None, None
            for hh in range(hpg):
                dq, db_row = carry[2 * hh:2 * hh + 2]
                z = _dot(qm[hh], kv, _NT)
                if has_bias:
                    z = z + refs[6][hh] - refs[7][hh, kb]
                p = jnp.exp(z - lses[hh])
                if diag:
                    p = jnp.where(allowed, p, 0.0)
                dz = p * (_dot(dov[hh], vv, _NT) - delta[hh])
                dzb = dz.astype(BF16)
                dk_h = _dot(dzb, qm[hh], _TN)
                dv_h = _dot(p.astype(BF16), dov[hh], _TN)
                dk_t = dk_h if dk_t is None else dk_t + dk_h
                dv_t = dv_h if dv_t is None else dv_t + dv_h
                if has_bias:
                    db_row = db_row + jnp.sum(dz, axis=1, keepdims=True)
                    refs[n_in + 4][hh, kb] += jnp.sum(dz, axis=0, keepdims=True)
                out += [dq + _dot(dzb, kv), db_row]
            dk_ref[pl.ds(ks, tk), :] += dk_t
            dv_ref[pl.ds(ks, tk), :] += dv_t
            return tuple(out)

        init = (jnp.zeros((tq, LANES), F32), jnp.zeros((tq, 1), F32)) * hpg
        if causal:
            carry = lax.fori_loop(0, qi, lambda kb, c: tile(kb, c, False), init)
            carry = tile(qi, carry, True)
        else:
            carry = lax.fori_loop(0, nk, lambda kb, c: tile(kb, c, False), init)
        dq_ref[...] = _merge(carry[0::2], masks) * scale
        if has_bias:
            for hh in range(hpg):
                refs[n_in + 3][hh] = carry[2 * hh + 1]

    in_specs = [_cols(tq, qo), _cols_all(Sk, ko), _cols_all(Sk, vo), _cols(tq, o[1]), _cols(tq, lse[1]), _cols(tq, do[1])]
    ins = [qa, ka, va, o[0], lse[0], do[0]]
    out_specs = [_cols(tq, 0), _cols_all(Sk, 0), _cols_all(Sk, 0)]
    out_shape = [jax.ShapeDtypeStruct((S, width), F32), jax.ShapeDtypeStruct((Sk, width), F32),
                 jax.ShapeDtypeStruct((Sk, width), F32)]
    if has_bias:
        rspec = pl.BlockSpec((hpg, tq, 1), lambda g, i: (g, i, 0))
        cspec = pl.BlockSpec((hpg, nk, 1, tk), lambda g, i: (g, 0, 0, 0))
        in_specs += [rspec, cspec]
        ins += list(bias)
        out_specs += [rspec, cspec]
        n_heads = width // hd
        out_shape += [jax.ShapeDtypeStruct((n_heads, S, 1), F32), jax.ShapeDtypeStruct((n_heads, nk, 1, tk), F32)]
    return pl.pallas_call(
        body, name=name, grid=(width // LANES, S // tq),
        in_specs=in_specs, out_specs=out_specs, out_shape=out_shape,
        compiler_params=_params(("parallel", "arbitrary")),
    )(*ins)


def _head_sums(t, masks):
    sums = [jnp.sum(_keep(t, m), axis=1, keepdims=True) for m in masks]
    return _merge([jnp.broadcast_to(s, t.shape) for s in sums], masks) if len(masks) > 1 else sums[0]


def _hnorm_fwd(x, g_lanes, *, width, hd, name, tr=512):
    xa, xo = x
    R = xa.shape[0]
    tr = _tile(R, tr, align=16)
    n_blk = width // LANES

    def body(x_ref, g_ref, o_ref):
        masks = _lane_masks(hd, tr)
        for j in range(n_blk):
            sl = slice(j * LANES, (j + 1) * LANES)
            xv = x_ref[:, sl].astype(F32)
            r = lax.rsqrt(_head_sums(xv * xv, masks) * (1.0 / hd) + EPS)
            o_ref[:, sl] = (xv * r * g_ref[...]).astype(o_ref.dtype)

    assert (xo * LANES) % width == 0
    return pl.pallas_call(
        body, name=name, grid=(R // tr,),
        in_specs=[pl.BlockSpec((tr, width), lambda i: (i, xo * LANES // width)), pl.BlockSpec((1, LANES), lambda i: (0, 0))],
        out_specs=pl.BlockSpec((tr, width), lambda i: (i, 0)),
        out_shape=jax.ShapeDtypeStruct((R, width), BF16),
        compiler_params=_params(("parallel",)),
    )(xa, g_lanes)


def _hnorm_bwd(x, g_lanes, dy, *, width, hd, name, tr=512):
    xa, xo = x
    R = xa.shape[0]
    tr = _tile(R, tr, align=16)
    n_blk = width // LANES

    def body(x_ref, g_ref, dy_ref, dx_ref, dg_ref):
        masks = _lane_masks(hd, tr)
        dg = jnp.zeros((1, LANES), F32)
        for j in range(n_blk):
            sl = slice(j * LANES, (j + 1) * LANES)
            xv = x_ref[:, sl].astype(F32)
            dyv = dy_ref[:, sl].astype(F32)
            r = lax.rsqrt(_head_sums(xv * xv, masks) * (1.0 / hd) + EPS)
            xh = xv * r
            dyg = dyv * g_ref[...]
            c = _head_sums(dyg * xh, masks) * (1.0 / hd)
            dx_ref[:, sl] = (r * (dyg - xh * c)).astype(dx_ref.dtype)
            dg = dg + jnp.sum(dyv * xh, axis=0, keepdims=True)
        if hd * 2 == LANES:
            dg8 = jnp.broadcast_to(dg, (8, LANES))
            dg = (dg8 + pltpu.roll(dg8, shift=hd, axis=1))[0:1]
        else:
            assert hd == LANES

        @pl.when(pl.program_id(0) == 0)
        def _():
            dg_ref[...] = jnp.zeros_like(dg_ref)

        dg_ref[...] += dg

    assert (xo * LANES) % width == 0
    return pl.pallas_call(
        body, name=name, grid=(R // tr,),
        in_specs=[pl.BlockSpec((tr, width), lambda i: (i, xo * LANES // width)), pl.BlockSpec((1, LANES), lambda i: (0, 0)),
                  pl.BlockSpec((tr, width), lambda i: (i, 0))],
        out_specs=[pl.BlockSpec((tr, width), lambda i: (i, 0)), pl.BlockSpec((1, LANES), lambda i: (0, 0))],
        out_shape=[jax.ShapeDtypeStruct((R, width), BF16), jax.ShapeDtypeStruct((1, LANES), F32)],
        compiler_params=_params(("arbitrary",)),
    )(xa, g_lanes, dy)


def _split3_dot(x, tri):
    a = x.astype(BF16)
    r = x - a.astype(F32)
    b = r.astype(BF16)
    c = (r - b.astype(F32)).astype(BF16)
    return _dot(a, tri) + _dot(b, tri) + _dot(c, tri)


def _forget_fwd(logit_t, b_col, *, name, blk=512):
    H, S = logit_t.shape
    blk = _tile(S, blk)

    def body(l_ref, b_ref, f_ref):
        r_i = lax.broadcasted_iota(jnp.int32, (blk, blk), 0)
        c_i = lax.broadcasted_iota(jnp.int32, (blk, blk), 1)
        upto = (r_i <= c_i).astype(BF16)
        carry = jnp.zeros((H, 1), F32)
        for j in range(S // blk):
            u = l_ref[:, j * blk:(j + 1) * blk] + b_ref[...]
            lf, _ = _log_sigmoid_pair(u)
            f_ref[:, j * blk:(j + 1) * blk] = _split3_dot(lf, upto) + carry
            carry = carry + jnp.sum(lf, axis=1, keepdims=True)

    return pl.pallas_call(
        body, name=name,
        out_shape=jax.ShapeDtypeStruct((H, S), F32),
        compiler_params=pltpu.CompilerParams(vmem_limit_bytes=VMEM_LIMIT),
    )(logit_t, b_col)


def _forget_bwd(logit_t, b_col, d_f, *, name, blk=512):
    H, S = logit_t.shape
    blk = _tile(S, blk)

    def body(l_ref, b_ref, df_ref, dl_ref, db_ref):
        r_i = lax.broadcasted_iota(jnp.int32, (blk, blk), 0)
        c_i = lax.broadcasted_iota(jnp.int32, (blk, blk), 1)
        fromon = (r_i >= c_i).astype(BF16)
        carry = jnp.zeros((H, 1), F32)
        db = jnp.zeros((H, 1), F32)
        for j in reversed(range(S // blk)):
            sl = slice(j * blk, (j + 1) * blk)
            dfv = df_ref[:, sl]
            d_lf = _split3_dot(dfv, fromon) + carry
            carry = carry + jnp.sum(dfv, axis=1, keepdims=True)
            u = l_ref[:, sl] + b_ref[...]
            _, lsn = _log_sigmoid_pair(u)
            dl = d_lf * jnp.exp(lsn)
            dl_ref[:, sl] = dl
            db = db + jnp.sum(dl, axis=1, keepdims=True)
        db_ref[...] = db

    return pl.pallas_call(
        body, name=name,
        out_shape=[jax.ShapeDtypeStruct((H, S), F32), jax.ShapeDtypeStruct((H, 1), F32)],
        compiler_params=pltpu.CompilerParams(vmem_limit_bytes=VMEM_LIMIT),
    )(logit_t, b_col, d_f)


def _sigmoid(t):
    return 1.0 / (1.0 + jnp.exp(-t))


def _gate_fwd(o3, w3, proj, D, *, name, tm=256):
    S = proj.shape[0]
    tm = _tile(S, tm)

    def body(o0, o1, o2, w0, w1, w2, g0, g1, g2, out_ref):
        acc = None
        for o_ref, w_ref, g_ref in ((o0, w0, g0), (o1, w1, g1), (o2, w2, g2)):
            t = _sigmoid(g_ref[...]) * _dot(o_ref[...], w_ref[...])
            acc = t if acc is None else acc + t
        out_ref[...] = acc.astype(out_ref.dtype)

    ospec = lambda d: pl.BlockSpec((tm, d), lambda i: (i, 0))
    wspec = lambda w: pl.BlockSpec(w.shape, lambda i: (0, 0))
    gspec = lambda j: pl.BlockSpec((tm, D), lambda i: (i, j))
    return pl.pallas_call(
        body, name=name, grid=(S // tm,),
        in_specs=[ospec(o.shape[1]) for o in o3] + [wspec(w) for w in w3] + [gspec(j) for j in range(3)],
        out_specs=pl.BlockSpec((tm, D), lambda i: (i, 0)),
        out_shape=jax.ShapeDtypeStruct((S, D), BF16),
        compiler_params=_params(("parallel",)),
    )(*o3, *w3, proj, proj, proj)


def _gate_bwd(o3, w3, proj, dmerged, D, *, name, tm=256):
    S = proj.shape[0]
    tm = _tile(S, tm)

    def body(o0, o1, o2, w0, w1, w2, g0, g1, g2, dm_ref, dg_ref, db0, db1, db2, do0, do1, do2):
        dm = dm_ref[...]
        for j, (o_ref, w_ref, g_ref, db_ref, do_ref) in enumerate(
                ((o0, w0, g0, db0, do0), (o1, w1, g1, db1, do1), (o2, w2, g2, db2, do2))):
            s = _sigmoid(g_ref[...])
            br = _dot(o_ref[...], w_ref[...])
            dg_ref[:, j * D:(j + 1) * D] = (dm * br * s * (1.0 - s)).astype(dg_ref.dtype)
            dbr = (dm * s).astype(BF16)
            db_ref[...] = dbr
            do_ref[...] = _dot(dbr, w_ref[...], _NT)

    ospec = lambda d: pl.BlockSpec((tm, d), lambda i: (i, 0))
    wspec = lambda w: pl.BlockSpec(w.shape, lambda i: (0, 0))
    gspec = lambda j: pl.BlockSpec((tm, D), lambda i: (i, j))
    dspec = pl.BlockSpec((tm, D), lambda i: (i, 0))
    return pl.pallas_call(
        body, name=name, grid=(S // tm,),
        in_specs=[ospec(o.shape[1]) for o in o3] + [wspec(w) for w in w3] + [gspec(j) for j in range(3)] + [dspec],
        out_specs=[pl.BlockSpec((tm, 3 * D), lambda i: (i, 0))] + [dspec] * 3 + [ospec(o.shape[1]) for o in o3],
        out_shape=[jax.ShapeDtypeStruct((S, 3 * D), BF16)] + [jax.ShapeDtypeStruct((S, D), BF16)] * 3
        + [jax.ShapeDtypeStruct((S, o.shape[1]), F32) for o in o3],
        compiler_params=_params(("parallel",)),
    )(*o3, *w3, proj, proj, proj, dmerged)


def _loss_sum(dy, D, *, name):
    R, C = dy.shape
    tr = _row_tile(R, C, 2)

    def body(dy_ref, out_ref):
        @pl.when(pl.program_id(0) == 0)
        def _():
            out_ref[...] = jnp.zeros_like(out_ref)

        v = dy_ref[...]
        out_ref[...] += (0.5 * D) * jnp.sum(v * v)

    return pl.pallas_call(
        body, name=name, grid=(R // tr,),
        in_specs=[pl.BlockSpec((tr, C), lambda i: (i, 0))],
        out_specs=pl.BlockSpec((8, 128), lambda i: (0, 0)),
        out_shape=jax.ShapeDtypeStruct((8, 128), F32),
        compiler_params=_params(("arbitrary",)),
    )(dy)[0, 0]


def _adamw_math(w, g, m, v):
    m2 = ADAM_B1 * m + (1.0 - ADAM_B1) * g
    v2 = ADAM_B2 * v + (1.0 - ADAM_B2) * (g * g)
    m_hat = m2 / (1.0 - ADAM_B1 ** ADAM_STEP)
    v_hat = v2 / (1.0 - ADAM_B2 ** ADAM_STEP)
    delta = -ADAM_LR * (m_hat / (jnp.sqrt(v_hat) + ADAM_EPS) + ADAM_WD * w)
    return delta, m2, v2


def _adamw(w, g, m, v, *, name):
    return _ew(_adamw_math, [w, g, m, v], (F32, F32, F32), name=name)


def _adamw_small(w, parts, m, v, *, name):
    n = parts.shape[0]

    def body(w_ref, p_ref, m_ref, v_ref, g_ref, d_ref, m2_ref, v2_ref):
        g = p_ref[0]
        for i in range(1, n):
            g = g + p_ref[i]
        g_ref[...] = g
        d_ref[...], m2_ref[...], v2_ref[...] = _adamw_math(w_ref[...], g, m_ref[...], v_ref[...])

    shp = jax.ShapeDtypeStruct(w.shape, F32)
    return pl.pallas_call(body, name=name, out_shape=[shp] * 4)(w, parts, m, v)


_ANY = pl.BlockSpec(memory_space=pl.ANY)


def _mesh_place():
    x, y, c = lax.axis_index("x"), lax.axis_index("y"), lax.axis_index("c")
    chips = [(1 - x, y), (x, 1 - y), (1 - x, 1 - y)]
    return x, y, c, chips


def _remote(src, dst, sems, i, to):
    send_sems, recv_sems = sems
    return pltpu.make_async_remote_copy(src_ref=src, dst_ref=dst, send_sem=send_sems.at[i], recv_sem=recv_sems.at[i],
                                        device_id=to, device_id_type=MESH_ID)


def _gather_weights(shards, *, name):
    n = len(shards)

    def body(*refs):
        ins, outs = refs[:n], refs[n:2 * n]
        sems = refs[2 * n:2 * n + 2]
        x, y, c, chips = _mesh_place()
        me = 2 * x + y
        sibling = (x, y, 1 - c)
        sent = []
        for w in range(n):
            for j, chip in enumerate(chips):
                cp = _remote(ins[w].at[c], outs[w].at[me, c], sems, 6 * w + j, (chip[0], chip[1], c))
                cp.start()
                sent.append(cp)
        for w in range(n):
            for j, chip in enumerate(chips):
                got = outs[w].at[2 * chip[0] + chip[1], c]
                _remote(got, got, sems, 6 * w + j, sibling).wait_recv()
                cp = _remote(got, got, sems, 6 * w + 3 + j, sibling)
                cp.start()
                sent.append(cp)
        for w in range(n):
            for j, chip in enumerate(chips):
                got = outs[w].at[2 * chip[0] + chip[1], 1 - c]
                _remote(got, got, sems, 6 * w + 3 + j, sibling).wait_recv()
        for cp in sent:
            cp.wait_send()

    outs = pl.pallas_call(
        body, name=name,
        in_specs=[_ANY] * n, out_specs=[_ANY] * n,
        out_shape=[jax.ShapeDtypeStruct((N_CHIPS,) + s.shape, s.dtype) for s in shards],
        scratch_shapes=[pltpu.SemaphoreType.DMA((6 * n,)), pltpu.SemaphoreType.DMA((6 * n,))],
    )(*shards)
    me = 2 * lax.axis_index("x") + lax.axis_index("y")
    return [lax.dynamic_update_index_in_dim(o, s, me, 0) for o, s in zip(outs, shards)]


def _forward_halves(gathered, *, name):
    n = len(gathered)

    def body(*refs):
        ins, outs = refs[:n], refs[n:2 * n]
        sems = refs[2 * n:2 * n + 2]
        x, y, c, chips = _mesh_place()
        sibling = (x, y, 1 - c)
        sent = []
        for w in range(n):
            for j, chip in enumerate(chips):
                peer = 2 * chip[0] + chip[1]
                cp = _remote(ins[w].at[peer, c], outs[w].at[peer, c], sems, 3 * w + j, sibling)
                cp.start()
                sent.append(cp)
        for w in range(n):
            for j, chip in enumerate(chips):
                land = outs[w].at[2 * chip[0] + chip[1], 1 - c]
                _remote(land, land, sems, 3 * w + j, sibling).wait_recv()
        for cp in sent:
            cp.wait_send()

    return pl.pallas_call(
        body, name=name,
        in_specs=[_ANY] * n, out_specs=[_ANY] * n,
        out_shape=[jax.ShapeDtypeStruct(g.shape, g.dtype) for g in gathered],
        input_output_aliases={w: w for w in range(n)},
        scratch_shapes=[pltpu.SemaphoreType.DMA((3 * n,)), pltpu.SemaphoreType.DMA((3 * n,))],
    )(*gathered)


def _exchange_siblings(grads, *, name):
    n = len(grads)

    def body(*refs):
        ins, got = refs[:n], refs[n:2 * n]
        sems = refs[2 * n:2 * n + 2]
        x, y, c, _ = _mesh_place()
        sibling = (x, y, 1 - c)
        sent = []
        for w in range(n):
            for s in range(N_CHIPS):
                cp = _remote(ins[w].at[s, 1 - c], got[w].at[s], sems, N_CHIPS * w + s, sibling)
                cp.start()
                sent.append(cp)
        for w in range(n):
            for s in range(N_CHIPS):
                _remote(got[w].at[s], got[w].at[s], sems, N_CHIPS * w + s, sibling).wait_recv()
        for cp in sent:
            cp.wait_send()

    n_sem = N_CHIPS * n
    got = pl.pallas_call(
        body, name=name,
        in_specs=[_ANY] * n, out_specs=[_ANY] * n,
        out_shape=[jax.ShapeDtypeStruct((N_CHIPS,) + g.shape[2:], g.dtype) for g in grads],
        scratch_shapes=[pltpu.SemaphoreType.DMA((n_sem,)), pltpu.SemaphoreType.DMA((n_sem,))],
    )(*grads)
    c = lax.axis_index("c")
    return [lax.dynamic_index_in_dim(g, c, 1, keepdims=False) for g in grads], got


def _exchange_chips(parts, *, name):
    n = len(parts)

    def body(*refs):
        ins, got = refs[:n], refs[n:2 * n]
        sems = refs[2 * n:2 * n + 2]
        x, y, c, chips = _mesh_place()
        sent = []
        for w in range(n):
            for j, chip in enumerate(chips):
                cp = _remote(ins[w].at[2 * chip[0] + chip[1]], got[w].at[j], sems, 3 * w + j, (chip[0], chip[1], c))
                cp.start()
                sent.append(cp)
        for w in range(n):
            for j in range(3):
                _remote(got[w].at[j], got[w].at[j], sems, 3 * w + j, (x, y, c)).wait_recv()
        for cp in sent:
            cp.wait_send()

    got = pl.pallas_call(
        body, name=name,
        in_specs=[_ANY] * n, out_specs=[_ANY] * n,
        out_shape=[jax.ShapeDtypeStruct((3,) + p.shape[1:], p.dtype) for p in parts],
        scratch_shapes=[pltpu.SemaphoreType.DMA((3 * n,)), pltpu.SemaphoreType.DMA((3 * n,))],
    )(*parts)
    me = 2 * lax.axis_index("x") + lax.axis_index("y")
    return [lax.dynamic_index_in_dim(p, me, 0, keepdims=False) for p in parts], got


def _share_halves(halves, small):
    n = len(halves)

    def body(*refs):
        ins, small_ref = refs[:n], refs[n]
        outs, small_out = refs[n + 1:2 * n + 1], refs[2 * n + 1]
        sems = refs[2 * n + 2:2 * n + 4]
        x, y, c, chips = _mesh_place()
        sibling = (x, y, 1 - c)
        me = 4 * x + 2 * y + c
        sent = [_remote(ins[w], outs[w].at[c], sems, w, sibling) for w in range(n)]
        peers = [sibling] + [(ch[0], ch[1], cc) for ch in chips for cc in (c, 1 - c)]
        sent += [_remote(small_ref, small_out.at[me], sems, n + j, peer) for j, peer in enumerate(peers)]
        for cp in sent:
            cp.start()
        for w in range(n):
            _remote(outs[w].at[1 - c], outs[w].at[1 - c], sems, w, sibling).wait_recv()
        for j, peer in enumerate(peers):
            frm = small_out.at[4 * peer[0] + 2 * peer[1] + peer[2]]
            _remote(frm, frm, sems, n + j, peer).wait_recv()
        for cp in sent:
            cp.wait_send()

    n_sem = n + 7
    outs = pl.pallas_call(
        body, name="share_halves",
        in_specs=[_ANY] * (n + 1), out_specs=[_ANY] * (n + 1),
        out_shape=[jax.ShapeDtypeStruct((2,) + h.shape, h.dtype) for h in halves]
        + [jax.ShapeDtypeStruct((8,) + small.shape, small.dtype)],
        scratch_shapes=[pltpu.SemaphoreType.DMA((n_sem,)), pltpu.SemaphoreType.DMA((n_sem,))],
    )(*halves, small)
    c = lax.axis_index("c")
    me = 4 * lax.axis_index("x") + 2 * lax.axis_index("y") + c
    return ([lax.dynamic_update_index_in_dim(o, h, c, 0) for o, h in zip(outs[:n], halves)],
            lax.dynamic_update_index_in_dim(outs[n], small, me, 0))


EARLY = ("w_ff_down", "w_ff_up", "w_out", "w_branch_sb", "w_branch_fox", "w_branch_mem")


def _local_step(x, mem, target, small, W, gather_rest=None, reduce_early=None, reduce_late=None):
    S, D = x.shape
    o_gate, o_qkv, o_mq, o_f = 0, 3 * D, 3 * D + 2 * 3 * D_SB, 3 * D + 2 * 3 * D_SB + D_MEM
    tq = 512

    h = _rmsnorm_fwd(x, small["g_mix_norm"], BF16, name="mix_norm")
    proj = _mm(h, W["w_in"], name="in_proj", tn=768)
    blk = lambda j: (proj, (o_qkv + j * D_SB) // LANES)
    sb_q, sb_k, sb_v, fx_q, fx_k, fx_v = [blk(j) for j in range(6)]
    m_q = (proj, o_mq // LANES)
    f_logit_t = proj[:, o_f:o_f + FOX_HEADS].T
    b_col = small["b_forget"].reshape(FOX_HEADS, 1)
    lanes = lambda g: jnp.tile(g, (1, LANES // g.shape[1]))
    g_fq, g_fk, g_mq, g_mk = [lanes(small[k]) for k in ("g_fox_q", "g_fox_k", "g_mem_q", "g_mem_k")]

    if gather_rest is None:
        o_sb, sb_tot = _sbl_fwd(sb_q, sb_k, sb_v, width=D_SB, hd=HD, name="sb_fwd", tq=tq)
    else:
        comm, finish = gather_rest
        o_sb, sb_tot, *landed = _sbl_fwd(sb_q, sb_k, sb_v, width=D_SB, hd=HD, name="sb_fwd", tq=tq, comm=comm)
        W = {**W, **finish(landed)}

    fq = _hnorm_fwd(fx_q, g_fq, width=D_FOX, hd=HD, name="fox_q_norm")
    fk = _hnorm_fwd(fx_k, g_fk, width=D_FOX, hd=HD, name="fox_k_norm")
    f_cum = _forget_fwd(f_logit_t, b_col, name="forget_fwd")
    tkf = _tile(S, tq)
    f_bias = (f_cum.reshape(FOX_HEADS, S, 1), f_cum.reshape(FOX_HEADS, S // tkf, 1, tkf))
    o_fox, fox_lse = _sml_fwd((fq, 0), (fk, 0), fx_v, f_bias, width=D_FOX, hd=HD, causal=True, name="fox_fwd",
                              tq=tq, tk=tq)

    mh = _rmsnorm_fwd(mem, small["g_mem_norm"], BF16, name="mem_norm")
    mkv = _mm(mh, W["w_mem_kv"], name="mem_kv")
    mv = (mkv, D_MEM // LANES)
    mq = _hnorm_fwd(m_q, g_mq, width=D_MEM, hd=MEM_HD, name="mem_q_norm")
    mk = _hnorm_fwd((mkv, 0), g_mk, width=D_MEM, hd=MEM_HD, name="mem_k_norm")
    o_mem, mem_lse = _sml_fwd((mq, 0), (mk, 0), mv, width=D_MEM, hd=MEM_HD, causal=False, name="mem_fwd", tq=tq, tk=256)

    o3 = [o_sb, o_fox, o_mem]
    w3 = [W["w_branch_sb"], W["w_branch_fox"], W["w_branch_mem"]]
    merged = _gate_fwd(o3, w3, proj, D, name="gate_fwd")
    x1 = _mm(merged, W["w_out"], name="out_proj", extras=(x,), epilogue=lambda acc, res: (res + acc,))
    h2 = _rmsnorm_fwd(x1, small["g_mlp_norm"], BF16, name="mlp_norm")

    def relu2(acc):
        u = jnp.maximum(acc, 0.0)
        return u, u * u

    u, a = _mm(h2, W["w_ff_up"], name="ff_up", out_dtypes=(BF16, BF16), epilogue=relu2)
    def head(acc, res, tgt):
        d = (res + acc - tgt) * (1.0 / D)
        return d, d

    dy, dy16 = _mm(a, W["w_ff_down"], name="ff_down", extras=(x1, target), out_dtypes=(F32, BF16), epilogue=head, tm=512)
    loss = _loss_sum(dy, D, name="loss")

    G = {}
    du = _mm(dy16, W["w_ff_down"], name="d_ff_act", tb=True, out_dtypes=(BF16,), extras=(u,),
             epilogue=lambda acc, uu: (acc * (2.0 * uu.astype(F32)),))
    G["w_ff_down"] = _mm(a, dy16, name="d_w_ff_down", ta=True, out_dtypes=(BF16,))
    G["w_ff_up"] = _mm(h2, du, name="d_w_ff_up", ta=True, out_dtypes=(BF16,))
    dh2 = _mm(du, W["w_ff_up"], name="d_mlp_in", tb=True)
    dx1, dg_mlp = _rmsnorm_bwd(x1, small["g_mlp_norm"], dh2, add=dy, name="d_mlp_norm")
    dmerged = _mm(dx1, W["w_out"], name="d_merged", tb=True)
    G["w_out"] = _mm(merged, dx1, name="d_w_out", ta=True, out_dtypes=(BF16,))
    dgate, db0, db1, db2, do_sb, do_fox, do_mem = _gate_bwd(o3, w3, proj, dmerged, D, name="gate_bwd")
    for nm, o, db in zip(("w_branch_sb", "w_branch_fox", "w_branch_mem"), o3, (db0, db1, db2)):
        G[nm] = _mm(o, db, name="d_" + nm, ta=True, out_dtypes=(BF16,))

    early = {}
    if reduce_early is None:
        dsb = _sbl_bwd(sb_q, sb_k, sb_v, (do_sb, 0), (sb_tot, 0), width=D_SB, hd=HD, name="sb_bwd", tq=tq)
    else:
        comm, finish = reduce_early({k: G.pop(k) for k in EARLY})
        *dsb, = _sbl_bwd(sb_q, sb_k, sb_v, (do_sb, 0), (sb_tot, 0), width=D_SB, hd=HD, name="sb_bwd", tq=tq, comm=comm)
        dsb, landed = dsb[:3], dsb[3:]
        early = finish(landed)
    dfq, dfk, dfv, df_row, df_col = _sml_bwd((fq, 0), (fk, 0), fx_v, (o_fox, 0), (fox_lse, 0), (do_fox, 0), f_bias,
                                             width=D_FOX, hd=HD, causal=True, name="fox_bwd", tq=tq, tk=tq)
    dfx_q, dg_fox_q = _hnorm_bwd(fx_q, g_fq, dfq, width=D_FOX, hd=HD, name="d_fox_q_norm")
    dfx_k, dg_fox_k = _hnorm_bwd(fx_k, g_fk, dfk, width=D_FOX, hd=HD, name="d_fox_k_norm")
    d_fcum = df_row.reshape(FOX_HEADS, S) - df_col.reshape(FOX_HEADS, S)
    d_flogit_t, db_forget = _forget_bwd(f_logit_t, b_col, d_fcum, name="forget_bwd")

    dmq_n, dmk_n, dmv = _sml_bwd((mq, 0), (mk, 0), mv, (o_mem, 0), (mem_lse, 0), (do_mem, 0), width=D_MEM, hd=MEM_HD,
                                 causal=False, name="mem_bwd", tq=tq, tk=256)
    dm_q, dg_mem_q = _hnorm_bwd(m_q, g_mq, dmq_n, width=D_MEM, hd=MEM_HD, name="d_mem_q_norm")
    dmk_raw, dg_mem_k = _hnorm_bwd((mkv, 0), g_mk, dmk_n, width=D_MEM, hd=MEM_HD, name="d_mem_k_norm")
    dmkv = jnp.concatenate([dmk_raw, dmv.astype(BF16)], axis=1)
    G["w_mem_kv"] = _mm(mh, dmkv, name="d_w_mem_kv", ta=True, out_dtypes=(BF16,))
    dmh = _mm(dmkv, W["w_mem_kv"], name="d_mem_h", tb=True)
    _, dg_mem = _rmsnorm_bwd(mem, small["g_mem_norm"], dmh, name="d_mem_norm")
    dg_fox_q, dg_fox_k = dg_fox_q[:, :HD], dg_fox_k[:, :HD]

    dproj = jnp.concatenate(
        [dgate] + [t.astype(BF16) for t in (*dsb, dfx_q, dfx_k, dfv, dm_q)]
        + [d_flogit_t.T.astype(BF16), jnp.zeros((S, F_PAD - FOX_HEADS), BF16)], axis=1)
    g_w_in = _mm(h, dproj, name="d_w_in", ta=True, out_dtypes=(BF16,), tn=768)
    if reduce_late is None:
        G["w_in"] = g_w_in
        dh = _mm(dproj, W["w_in"], name="d_mix_in", tb=True, tk=2304)
    else:
        comm, finish = reduce_late({"w_in": g_w_in, "w_mem_kv": G.pop("w_mem_kv")})
        dh, *landed = _mm(dproj, W["w_in"], name="d_mix_in", tb=True, tk=2304, comm=comm)
        early.update(finish(landed))
    grad_x, dg_mix = _rmsnorm_bwd(x, small["g_mix_norm"], dh, add=dx1, name="d_mix_norm")

    small_grads = dict(g_mix_norm=dg_mix, g_mem_norm=dg_mem, b_forget=db_forget.reshape(1, FOX_HEADS),
                       g_fox_q=dg_fox_q, g_fox_k=dg_fox_k, g_mem_q=dg_mem_q, g_mem_k=dg_mem_k, g_mlp_norm=dg_mlp)
    return loss, grad_x, G, small_grads, early


BIG = ("w_in", "w_mem_kv", "w_branch_sb", "w_branch_fox", "w_branch_mem", "w_out", "w_ff_up", "w_ff_down")
COLUMN_SHARDED = ("w_in", "w_branch_sb", "w_branch_fox", "w_branch_mem", "w_ff_up")
SMALL = ("g_mix_norm", "g_mem_norm", "b_forget", "g_fox_q", "g_fox_k", "g_mem_q", "g_mem_k", "g_mlp_norm")
ORDER = ("g_mix_norm", "g_mem_norm", "w_in", "b_forget", "g_fox_q", "g_fox_k", "g_mem_q", "g_mem_k", "w_mem_kv",
         "w_branch_sb", "w_branch_fox", "w_branch_mem", "w_out", "g_mlp_norm", "w_ff_up", "w_ff_down")


def _unshard(name, gathered):
    n, _, rh, c = gathered.shape
    t = gathered.reshape(n, 2 * rh, c)
    if name in COLUMN_SHARDED:
        return t.transpose(1, 0, 2).reshape(2 * rh, n * c)
    return t.reshape(n * 2 * rh, c)


def _reshard(name, full):
    if name in COLUMN_SHARDED:
        r, c = full.shape
        t = full.reshape(r, N_CHIPS, c // N_CHIPS).transpose(1, 0, 2)
    else:
        r, c = full.shape[0] // N_CHIPS, full.shape[1]
        t = full.reshape(N_CHIPS, r, c)
    return t.reshape(N_CHIPS, 2, t.shape[1] // 2, t.shape[2])


def _pad_in_proj(w_in, D):
    n_qkv = 6 * D_SB
    o_mq = n_qkv + FOX_HEADS
    o_gate = o_mq + D_MEM
    return jnp.concatenate([w_in[:, o_gate:], w_in[:, :n_qkv], w_in[:, o_mq:o_gate], w_in[:, n_qkv:o_mq],
                            jnp.zeros((w_in.shape[0], F_PAD - FOX_HEADS), w_in.dtype)], axis=1)


def _unpad_in_proj(g, D):
    n_qkv = 6 * D_SB
    o_qkv, o_mq, o_f = 3 * D, 3 * D + n_qkv, 3 * D + n_qkv + D_MEM
    return jnp.concatenate([g[:, o_qkv:o_mq], g[:, o_f:o_f + FOX_HEADS], g[:, o_mq:o_f], g[:, :o_qkv]], axis=1)


def _pack_small(vals):
    width = max(vals[k].shape[1] for k in SMALL)
    return jnp.concatenate([jnp.pad(vals[k].astype(F32), ((0, 0), (0, width - vals[k].shape[1]))) for k in SMALL], axis=0)


def _unpack_small(packed, like):
    return {k: packed[i:i + 1, :like[k].shape[1]] for i, k in enumerate(SMALL)}


def kernel(x, mem, g_mix_norm, g_mem_norm, w_in, b_forget, g_fox_q, g_fox_k, g_mem_q, g_mem_k, w_mem_kv, w_branch_sb, w_branch_fox, w_branch_mem, w_out, g_mlp_norm, w_ff_up, w_ff_down, loss_target, m_g_mix_norm, m_g_mem_norm, m_w_in, m_b_forget, m_g_fox_q, m_g_fox_k, m_g_mem_q, m_g_mem_k, m_w_mem_kv, m_w_branch_sb, m_w_branch_fox, m_w_branch_mem, m_w_out, m_g_mlp_norm, m_w_ff_up, m_w_ff_down, v_g_mix_norm, v_g_mem_norm, v_w_in, v_b_forget, v_g_fox_q, v_g_fox_k, v_g_mem_q, v_g_mem_k, v_w_mem_kv, v_w_branch_sb, v_w_branch_fox, v_w_branch_mem, v_w_out, v_g_mlp_norm, v_w_ff_up, v_w_ff_down):
    given = dict(locals())
    D = x.shape[-1]
    weights = {k: given[k] for k in ORDER}
    moms = {k: given["m_" + k] for k in ORDER}
    vars_ = {k: given["v_" + k] for k in ORDER}

    me_chip = 2 * lax.axis_index("x") + lax.axis_index("y")

    shards = {}
    for k in BIG:
        w = weights[k][0].astype(BF16)
        shards[k] = w.reshape(2, w.shape[0] // 2, w.shape[1])
    W = {"w_in": _pad_in_proj(_unshard("w_in", _gather_weights([shards["w_in"]], name="gather_w_in")[0]), D)}
    rest = [k for k in BIG if k != "w_in"]

    def finish_weights(landed):
        full = _forward_halves(landed, name="forward_halves")
        full = [lax.dynamic_update_index_in_dim(o, shards[k], me_chip, 0) for k, o in zip(rest, full)]
        return {k: _unshard(k, g) for k, g in zip(rest, full)}

    def sum_pairs(names, own, sib):
        add2 = lambda p, q: (p.astype(F32) + q.astype(F32),)
        out = []
        for k, p, q in zip(names, own, sib):
            n, r, c = p.shape
            out.append(_ew(add2, [p.reshape(n * r, c), q.reshape(n * r, c)], (BF16,), name="sum_pair_" + k).reshape(n, r, c))
        return out

    def sum_chips(names, parts, got):
        add4 = lambda p, q0, q1, q2: (((p.astype(F32) + q0.astype(F32)) + q1.astype(F32)) + q2.astype(F32),)
        own = [lax.dynamic_index_in_dim(p, me_chip, 0, keepdims=False) for p in parts]
        return {k: _ew(add4, [p, q[0], q[1], q[2]], (F32,), name="sum_chips_" + k) for k, p, q in zip(names, own, got)}

    def reducer(tag):
        def reduce(grads):
            names = list(grads)
            if "w_in" in grads:
                grads = {**grads, "w_in": _unpad_in_proj(grads["w_in"], D)}
            own, sib = _exchange_siblings([_reshard(k, grads[k]) for k in names], name="exchange_siblings_" + tag)
            parts = sum_pairs(names, own, sib)
            return _ChipExchange("scatter", parts), functools.partial(sum_chips, names, parts)
        return reduce

    small = {k: weights[k] for k in SMALL}
    loss_part, grad_x, G, small_grads, halves = _local_step(
        x[0], mem[0], loss_target[0], small, W,
        gather_rest=(_ChipExchange("gather", [shards[k] for k in rest]), finish_weights),
        reduce_early=reducer("early"), reduce_late=reducer("late"))
    assert not G, list(G)
    reduced, small_parts = _share_halves([halves[k] for k in BIG], _pack_small(small_grads))

    grads, deltas, new_m, new_v = {}, {}, {}, {}
    for k, g in zip(BIG, reduced):
        shp = weights[k].shape
        g2 = g.reshape(shp[1], shp[2])
        d, m2, v2 = _adamw(weights[k][0], g2, moms[k][0], vars_[k][0], name="adamw_" + k)
        grads[k], deltas[k], new_m[k], new_v[k] = g2.reshape(shp), d.reshape(shp), m2.reshape(shp), v2.reshape(shp)
    sg, sd, sm, sv = _adamw_small(_pack_small(small), small_parts, _pack_small({k: moms[k] for k in SMALL}),
                                  _pack_small({k: vars_[k] for k in SMALL}), name="adamw_small")
    for dst, packed in ((grads, sg), (deltas, sd), (new_m, sm), (new_v, sv)):
        dst.update(_unpack_small(packed, small))

    loss = lax.psum(loss_part, ("x", "y", "c"))
    return (loss, grad_x[None], *[grads[k] for k in ORDER], *[deltas[k] for k in ORDER],
            *[new_m[k] for k in ORDER], *[new_v[k] for k in ORDER])
```

```python
import functools

import jax
import jax.numpy as jnp
from jax import lax
from jax.experimental import pallas as pl
from jax.experimental.pallas import tpu as pltpu

F32 = jnp.float32
BF16 = jnp.bfloat16
MESH_ID = pl.DeviceIdType.MESH

HD = 64
SB_HEADS = 8
FOX_HEADS = 8
MEM_HEADS = 4
MEM_HD = 128
D_SB = SB_HEADS * HD
D_FOX = FOX_HEADS * HD
D_MEM = MEM_HEADS * MEM_HD
EPS = 1e-6
NEG_INF = -1e30

ADAM_LR = 0.001
ADAM_B1 = 0.9
ADAM_B2 = 0.999
ADAM_EPS = 1e-08
ADAM_WD = 0.01
ADAM_STEP = 10

N_CHIPS = 4
VMEM_LIMIT = 56 * 1024 * 1024

F_PAD = 256


def _tile(n, target, align=128):
    if n <= target:
        return n
    best = None
    t = align
    while t <= target:
        if n % t == 0:
            best = t
        t += align
    assert best is not None, (n, target, align)
    return best


def _params(sem):
    return pltpu.CompilerParams(dimension_semantics=sem, vmem_limit_bytes=VMEM_LIMIT)


def _mm(a, b, *, name, ta=False, tb=False, out_dtypes=(F32,), epilogue=None, extras=(),
        tm=1024, tn=1024, tk=2048, comm=None):
    if ta:
        K, M = a.shape
    else:
        M, K = a.shape
    if tb:
        N, K2 = b.shape
    else:
        K2, N = b.shape
    assert K == K2, (a.shape, b.shape, ta, tb)
    tm, tn, tk = _tile(M, tm), _tile(N, tn), _tile(K, tk)
    nk = K // tk
    n_extra, n_out = len(extras), len(out_dtypes)
    if epilogue is None:
        epilogue = lambda acc: (acc,)
    dims = (((0 if ta else 1,), (1 if tb else 0,)), ((), ()))

    gm, gn = M // tm, N // tn

    def body(*refs):
        i, j, k = pl.program_id(0), pl.program_id(1), pl.program_id(2)
        got = _carry(comm, 2 + n_extra, n_out, (i == 0) & (j == 0) & (k == 0),
                     (i == gm - 1) & (j == gn - 1) & (k == nk - 1), refs)
        (a_ref, b_ref, *extra_refs), out_refs = got[0], got[1]
        part = lax.dot_general(a_ref[...].astype(BF16), b_ref[...].astype(BF16), dims, preferred_element_type=F32)

        def finish(acc):
            outs = epilogue(acc, *[r[...] for r in extra_refs])
            for o_ref, o in zip(out_refs, outs):
                o_ref[...] = o.astype(o_ref.dtype)

        if nk == 1:
            finish(part)
        else:
            acc_ref = refs[-1]

            @pl.when(k == 0)
            def _():
                acc_ref[...] = part

            @pl.when((k > 0) & (k < nk - 1))
            def _():
                acc_ref[...] += part

            @pl.when(k == nk - 1)
            def _():
                finish(acc_ref[...] + part)

        got[2]()

    a_spec = pl.BlockSpec((tk, tm), lambda i, j, k: (k, i)) if ta else pl.BlockSpec((tm, tk), lambda i, j, k: (i, k))
    b_spec = pl.BlockSpec((tn, tk), lambda i, j, k: (j, k)) if tb else pl.BlockSpec((tk, tn), lambda i, j, k: (k, j))
    mn_spec = pl.BlockSpec((tm, tn), lambda i, j, k: (i, j))
    c_ins, c_in_specs, c_out_specs, c_out_shape, c_scratch = _comm_args(comm)
    sem = ("parallel", "parallel", "arbitrary") if comm is None else ("arbitrary",) * 3
    outs = pl.pallas_call(
        body, name=name,
        grid=(gm, gn, nk),
        in_specs=[a_spec, b_spec] + [mn_spec] * n_extra + c_in_specs,
        out_specs=[mn_spec] * n_out + c_out_specs,
        out_shape=[jax.ShapeDtypeStruct((M, N), dt) for dt in out_dtypes] + c_out_shape,
        scratch_shapes=c_scratch + ([pltpu.VMEM((tm, tn), F32)] if nk > 1 else []),
        compiler_params=_params(sem),
    )(a, b, *extras, *c_ins)
    return outs if len(outs) > 1 else outs[0]


def _row_tile(rows, cols, n_arrays):
    budget = 10 * 1024 * 1024
    cols_padded = -(-cols // 128) * 128
    target = max(16, budget // (cols_padded * 4 * n_arrays * 2))
    return _tile(rows, target, align=16)


def _ew(fn, ins, out_dtypes, *, name):
    R, C = ins[0].shape
    n_in, n_out = len(ins), len(out_dtypes)
    tr = _row_tile(R, C, n_in + n_out)

    def body(*refs):
        outs = fn(*[r[...] for r in refs[:n_in]])
        for o_ref, o in zip(refs[n_in:], outs):
            o_ref[...] = o.astype(o_ref.dtype)

    spec = pl.BlockSpec((tr, C), lambda i: (i, 0))
    outs = pl.pallas_call(
        body, name=name, grid=(R // tr,),
        in_specs=[spec] * n_in, out_specs=[spec] * n_out,
        out_shape=[jax.ShapeDtypeStruct((R, C), dt) for dt in out_dtypes],
        compiler_params=_params(("parallel",)),
    )(*ins)
    return outs if n_out > 1 else outs[0]


def _rmsnorm_fwd(x, g, out_dtype, *, name):
    R, d = x.shape
    tr = _row_tile(R, d, 3)

    def body(x_ref, g_ref, o_ref):
        xv = x_ref[...].astype(F32)
        r = lax.rsqrt(jnp.mean(xv * xv, axis=-1, keepdims=True) + EPS)
        o_ref[...] = (xv * r * g_ref[...]).astype(o_ref.dtype)

    return pl.pallas_call(
        body, name=name, grid=(R // tr,),
        in_specs=[pl.BlockSpec((tr, d), lambda i: (i, 0)), pl.BlockSpec((1, d), lambda i: (0, 0))],
        out_specs=pl.BlockSpec((tr, d), lambda i: (i, 0)),
        out_shape=jax.ShapeDtypeStruct((R, d), out_dtype),
        compiler_params=_params(("parallel",)),
    )(x, g)


def _rmsnorm_bwd(x, g, dy, add=None, *, name):
    R, d = x.shape
    has_add = add is not None
    tr = _row_tile(R, d, 5)

    def body(*refs):
        x_ref, g_ref, dy_ref = refs[:3]
        add_ref = refs[3] if has_add else None
        dx_ref, dg_ref = refs[-2:]
        xv = x_ref[...].astype(F32)
        dyv = dy_ref[...].astype(F32)
        r = lax.rsqrt(jnp.mean(xv * xv, axis=-1, keepdims=True) + EPS)
        xh = xv * r
        dyg = dyv * g_ref[...]
        c = jnp.mean(dyg * xh, axis=-1, keepdims=True)
        dx = r * (dyg - xh * c)
        if has_add:
            dx = dx + add_ref[...]
        dx_ref[...] = dx

        @pl.when(pl.program_id(0) == 0)
        def _():
            dg_ref[...] = jnp.zeros_like(dg_ref)

        dg_ref[...] += jnp.sum(dyv * xh, axis=0, keepdims=True)

    row = pl.BlockSpec((tr, d), lambda i: (i, 0))
    vec = pl.BlockSpec((1, d), lambda i: (0, 0))
    ins = [x, g, dy] + ([add] if has_add else [])
    return pl.pallas_call(
        body, name=name, grid=(R // tr,),
        in_specs=[row, vec, row] + ([row] if has_add else []),
        out_specs=[row, vec],
        out_shape=[jax.ShapeDtypeStruct((R, d), F32), jax.ShapeDtypeStruct((1, d), F32)],
        compiler_params=_params(("arbitrary",)),
    )(*ins)


_NT = (((1,), (1,)), ((), ()))
_TN = (((0,), (0,)), ((), ()))


def _dot(a, b, dims=(((1,), (0,)), ((), ()))):
    return lax.dot_general(a, b, dims, preferred_element_type=F32)


def _split_dot(x, tri):
    hi = x.astype(BF16)
    lo = (x - hi.astype(F32)).astype(BF16)
    return _dot(hi, tri) + _dot(lo, tri)


def _log_sigmoid_pair(z):
    sp = jnp.log(1.0 + jnp.exp(-jnp.abs(z)))
    return jnp.minimum(z, 0.0) - sp, jnp.minimum(-z, 0.0) - sp


def _sb_fwd(q, k, v, *, name, tq=256):
    H, S, hd = q.shape
    tq = _tile(S, tq)
    tk = tq
    scale = hd ** -0.5

    def body(q_ref, k_ref, v_ref, o_ref, tot_ref):
        qi = pl.program_id(1)
        qv = q_ref[0]
        row = qi * tq + lax.broadcasted_iota(jnp.int32, (tq, tk), 0)
        col0 = lax.broadcasted_iota(jnp.int32, (tq, tk), 1)
        later = (lax.broadcasted_iota(jnp.int32, (tk, tk), 0) > lax.broadcasted_iota(jnp.int32, (tk, tk), 1)).astype(BF16)

        def step(i, carry):
            acc, c_rem = carry
            kb = qi - i
            ks = pl.multiple_of(kb * tk, tk)
            kv = k_ref[0, pl.ds(ks, tk), :]
            vv = v_ref[0, pl.ds(ks, tk), :]
            z = _dot(qv, kv, _NT) * scale
            mask = (col0 + ks) < row
            ls, lsn = _log_sigmoid_pair(z)
            rem = jnp.where(mask, lsn, 0.0)
            after = _split_dot(rem, later) + c_rem
            w = jnp.where(mask, jnp.exp(ls + after), 0.0)
            acc = acc + _dot(w.astype(BF16), vv)
            c_rem = c_rem + jnp.sum(rem, axis=1, keepdims=True)
            return acc, c_rem

        acc, c_rem = lax.fori_loop(0, qi + 1, step, (jnp.zeros((tq, hd), F32), jnp.zeros((tq, 1), F32)))
        o_ref[0] = acc
        tot_ref[0] = c_rem

    qspec = pl.BlockSpec((1, tq, hd), lambda h, i: (h, i, 0))
    kspec = pl.BlockSpec((1, S, hd), lambda h, i: (h, 0, 0))
    return pl.pallas_call(
        body, name=name, grid=(H, S // tq),
        in_specs=[qspec, kspec, kspec],
        out_specs=[qspec, pl.BlockSpec((1, tq, 1), lambda h, i: (h, i, 0))],
        out_shape=[jax.ShapeDtypeStruct((H, S, hd), F32), jax.ShapeDtypeStruct((H, S, 1), F32)],
        compiler_params=_params(("parallel", "arbitrary")),
    )(q, k, v)


def _sb_bwd(q, k, v, do, tot, *, name, tq=256):
    H, S, hd = q.shape
    tq = _tile(S, tq)
    tk = tq
    scale = hd ** -0.5

    def body(q_ref, k_ref, v_ref, do_ref, tot_ref, dq_ref, dk_ref, dv_ref):
        qi = pl.program_id(1)

        @pl.when(qi == 0)
        def _():
            dk_ref[...] = jnp.zeros_like(dk_ref)
            dv_ref[...] = jnp.zeros_like(dv_ref)

        qv = q_ref[0]
        dov = do_ref[0].astype(BF16)
        tot_v = tot_ref[0]
        row = qi * tq + lax.broadcasted_iota(jnp.int32, (tq, tk), 0)
        col0 = lax.broadcasted_iota(jnp.int32, (tq, tk), 1)
        r_i = lax.broadcasted_iota(jnp.int32, (tk, tk), 0)
        c_i = lax.broadcasted_iota(jnp.int32, (tk, tk), 1)
        upto = (r_i <= c_i).astype(BF16)
        before = (r_i < c_i).astype(BF16)

        def step(kb, carry):
            dq, c_rem, c_g = carry
            ks = pl.multiple_of(kb * tk, tk)
            kv = k_ref[0, pl.ds(ks, tk), :]
            vv = v_ref[0, pl.ds(ks, tk), :]
            z = _dot(qv, kv, _NT) * scale
            mask = (col0 + ks) < row
            ls, lsn = _log_sigmoid_pair(z)
            rem = jnp.where(mask, lsn, 0.0)
            after = tot_v - (_split_dot(rem, upto) + c_rem)
            w = jnp.where(mask, jnp.exp(ls + after), 0.0)
            dw = _dot(dov, vv, _NT)
            g = dw * w
            g_before = _split_dot(g, before) + c_g
            beta = jnp.exp(ls)
            dz = jnp.where(mask, g * (1.0 - beta) - beta * g_before, 0.0) * scale
            dzb = dz.astype(BF16)
            dq = dq + _dot(dzb, kv)
            dk_ref[0, pl.ds(ks, tk), :] += _dot(dzb, qv, _TN)
            dv_ref[0, pl.ds(ks, tk), :] += _dot(w.astype(BF16), dov, _TN)
            return dq, c_rem + jnp.sum(rem, axis=1, keepdims=True), c_g + jnp.sum(g, axis=1, keepdims=True)

        zero = jnp.zeros((tq, 1), F32)
        dq, _, _ = lax.fori_loop(0, qi + 1, step, (jnp.zeros((tq, hd), F32), zero, zero))
        dq_ref[0] = dq

    qspec = pl.BlockSpec((1, tq, hd), lambda h, i: (h, i, 0))
    kspec = pl.BlockSpec((1, S, hd), lambda h, i: (h, 0, 0))
    full = jax.ShapeDtypeStruct((H, S, hd), F32)
    return pl.pallas_call(
        body, name=name, grid=(H, S // tq),
        in_specs=[qspec, kspec, kspec, qspec, pl.BlockSpec((1, tq, 1), lambda h, i: (h, i, 0))],
        out_specs=[qspec, kspec, kspec],
        out_shape=[full, full, full],
        compiler_params=_params(("parallel", "arbitrary")),
    )(q, k, v, do, tot)


def _softmax_fwd(q, k, v, bias=None, *, name, causal, tq=256, tk=256):
    H, S, hd = q.shape
    Sk = k.shape[1]
    tq, tk = _tile(S, tq), _tile(Sk, tk)
    if causal:
        assert tq == tk and S == Sk
    nk = Sk // tk
    scale = hd ** -0.5
    has_bias = bias is not None

    def body(*refs):
        q_ref, k_ref, v_ref = refs[:3]
        o_ref, lse_ref = refs[-2:]
        qi = pl.program_id(1)
        qv = q_ref[0]
        row = qi * tq + lax.broadcasted_iota(jnp.int32, (tq, tk), 0)
        col0 = lax.broadcasted_iota(jnp.int32, (tq, tk), 1)
        if has_bias:
            b_row = refs[3][0]

        def step(kb, carry):
            m, l, acc = carry
            ks = pl.multiple_of(kb * tk, tk)
            kv = k_ref[0, pl.ds(ks, tk), :]
            vv = v_ref[0, pl.ds(ks, tk), :]
            z = _dot(qv, kv, _NT) * scale
            if has_bias:
                z = z + b_row - refs[4][0, kb]
            if causal:
                z = jnp.where((col0 + ks) <= row, z, NEG_INF)
            m2 = jnp.maximum(m, jnp.max(z, axis=1, keepdims=True))
            p = jnp.exp(z - m2)
            alpha = jnp.exp(m - m2)
            l = alpha * l + jnp.sum(p, axis=1, keepdims=True)
            acc = alpha * acc + _dot(p.astype(BF16), vv)
            return m2, l, acc

        init = (jnp.full((tq, 1), NEG_INF, F32), jnp.zeros((tq, 1), F32), jnp.zeros((tq, hd), F32))
        m, l, acc = lax.fori_loop(0, (qi + 1) if causal else nk, step, init)
        o_ref[0] = acc / l
        lse_ref[0] = m + jnp.log(l)

    qspec = pl.BlockSpec((1, tq, hd), lambda h, i: (h, i, 0))
    kspec = pl.BlockSpec((1, Sk, hd), lambda h, i: (h, 0, 0))
    vspec = pl.BlockSpec((1, tq, 1), lambda h, i: (h, i, 0))
    in_specs = [qspec, kspec, kspec]
    ins = [q, k, v]
    if has_bias:
        in_specs += [vspec, pl.BlockSpec((1, nk, 1, tk), lambda h, i: (h, 0, 0, 0))]
        ins += list(bias)
    return pl.pallas_call(
        body, name=name, grid=(H, S // tq),
        in_specs=in_specs, out_specs=[qspec, vspec],
        out_shape=[jax.ShapeDtypeStruct((H, S, hd), F32), jax.ShapeDtypeStruct((H, S, 1), F32)],
        compiler_params=_params(("parallel", "arbitrary")),
    )(*ins)


def _softmax_bwd(q, k, v, o, lse, do, bias=None, *, name, causal, tq=256, tk=256):
    H, S, hd = q.shape
    Sk = k.shape[1]
    tq, tk = _tile(S, tq), _tile(Sk, tk)
    nk = Sk // tk
    scale = hd ** -0.5
    has_bias = bias is not None
    n_in = 8 if has_bias else 6

    def body(*refs):
        q_ref, k_ref, v_ref, o_ref, lse_ref, do_ref = refs[:6]
        dq_ref, dk_ref, dv_ref = refs[n_in:n_in + 3]
        qi = pl.program_id(1)

        @pl.when(qi == 0)
        def _():
            dk_ref[...] = jnp.zeros_like(dk_ref)
            dv_ref[...] = jnp.zeros_like(dv_ref)
            if has_bias:
                refs[n_in + 4][...] = jnp.zeros_like(refs[n_in + 4])

        qv = q_ref[0]
        do32 = do_ref[0]
        dov = do32.astype(BF16)
        delta = jnp.sum(do32 * o_ref[0], axis=1, keepdims=True)
        lse_v = lse_ref[0]
        row = qi * tq + lax.broadcasted_iota(jnp.int32, (tq, tk), 0)
        col0 = lax.broadcasted_iota(jnp.int32, (tq, tk), 1)
        if has_bias:
            b_row = refs[6][0]

        def step(kb, carry):
            dq, db_row = carry
            ks = pl.multiple_of(kb * tk, tk)
            kv = k_ref[0, pl.ds(ks, tk), :]
            vv = v_ref[0, pl.ds(ks, tk), :]
            z = _dot(qv, kv, _NT) * scale
            if has_bias:
                z = z + b_row - refs[7][0, kb]
            p = jnp.exp(z - lse_v)
            if causal:
                p = jnp.where((col0 + ks) <= row, p, 0.0)
            dp = _dot(dov, vv, _NT)
            dz = p * (dp - delta)
            dzb = dz.astype(BF16)
            dq = dq + _dot(dzb, kv)
            dk_ref[0, pl.ds(ks, tk), :] += _dot(dzb, qv, _TN) * scale
            dv_ref[0, pl.ds(ks, tk), :] += _dot(p.astype(BF16), dov, _TN)
            if has_bias:
                db_row = db_row + jnp.sum(dz, axis=1, keepdims=True)
                refs[n_in + 4][0, kb] += jnp.sum(dz, axis=0, keepdims=True)
            return dq, db_row

        dq, db_row = lax.fori_loop(0, (qi + 1) if causal else nk, step,
                                   (jnp.zeros((tq, hd), F32), jnp.zeros((tq, 1), F32)))
        dq_ref[0] = dq * scale
        if has_bias:
            refs[n_in + 3][0] = db_row

    qspec = pl.BlockSpec((1, tq, hd), lambda h, i: (h, i, 0))
    kspec = pl.BlockSpec((1, Sk, hd), lambda h, i: (h, 0, 0))
    vspec = pl.BlockSpec((1, tq, 1), lambda h, i: (h, i, 0))
    cspec = pl.BlockSpec((1, nk, 1, tk), lambda h, i: (h, 0, 0, 0))
    in_specs = [qspec, kspec, kspec, qspec, vspec, qspec]
    ins = [q, k, v, o, lse, do]
    out_specs = [qspec, kspec, kspec]
    out_shape = [jax.ShapeDtypeStruct((H, S, hd), F32), jax.ShapeDtypeStruct((H, Sk, hd), F32),
                 jax.ShapeDtypeStruct((H, Sk, hd), F32)]
    if has_bias:
        in_specs += [vspec, cspec]
        ins += list(bias)
        out_specs += [vspec, cspec]
        out_shape += [jax.ShapeDtypeStruct((H, S, 1), F32), jax.ShapeDtypeStruct((H, nk, 1, tk), F32)]
    return pl.pallas_call(
        body, name=name, grid=(H, S // tq),
        in_specs=in_specs, out_specs=out_specs, out_shape=out_shape,
        compiler_params=_params(("parallel", "arbitrary")),
    )(*ins)


LANES = 128
_LOW = -3e38


def _lane_masks(hd, rows):
    if hd == LANES:
        return [None]
    lane = lax.broadcasted_iota(jnp.int32, (rows, LANES), 1)
    return [(lane >= hh * hd) & (lane < (hh + 1) * hd) for hh in range(LANES // hd)]


def _keep(t, m):
    return t if m is None else jnp.where(m, t, 0.0)


def _merge(parts, masks):
    out = parts[-1]
    for p, m in zip(parts[-2::-1], masks[-2::-1]):
        out = jnp.where(m, p, out)
    return out


def _row_value(t, m):
    return jnp.max(t if m is None else jnp.where(m, t, _LOW), axis=1, keepdims=True)


def _cols(tq, off):
    return pl.BlockSpec((tq, LANES), lambda g, i: (i, off + g))


def _cols_all(rows, off):
    return pl.BlockSpec((rows, LANES), lambda g, i: (0, off + g))


SCAN_BLOCK = 256


def _tri(kind, cols):
    n = min(SCAN_BLOCK, cols)
    r = lax.broadcasted_iota(jnp.int32, (n, n), 0)
    c = lax.broadcasted_iota(jnp.int32, (n, n), 1)
    return ((r > c) if kind == "after" else (r < c)).astype(BF16)


def _scan_cols(x, tri, reverse):
    cols = x.shape[1]
    cb = min(SCAN_BLOCK, cols)
    assert cols % cb == 0 and tri.shape == (cb, cb)
    nb = cols // cb
    blocks = [x[:, b * cb:(b + 1) * cb] for b in range(nb)]
    outs, carry = [None] * nb, None
    for b in (reversed(range(nb)) if reverse else range(nb)):
        y = _dot(blocks[b].astype(BF16), tri)
        outs[b] = y if carry is None else y + carry
        s = jnp.sum(blocks[b], axis=1, keepdims=True)
        carry = s if carry is None else carry + s
    return (outs[0] if nb == 1 else jnp.concatenate(outs, axis=1)), carry


def _softplus_parts(z):
    pos = jnp.maximum(z, 0.0) + jnp.log(1.0 + jnp.exp(-jnp.abs(z)))
    return pos, z - pos


class _ChipExchange:
    def __init__(self, kind, ins):
        assert kind in ("gather", "scatter")
        self.kind, self.ins = kind, list(ins)
        lead = (lambda s: (N_CHIPS,) + s) if kind == "gather" else (lambda s: (3,) + s[1:])
        self.out_shape = [jax.ShapeDtypeStruct(lead(a.shape), a.dtype) for a in ins]
        n = 3 * len(ins)
        self.scratch = [pltpu.SemaphoreType.DMA((n,)), pltpu.SemaphoreType.DMA((n,))]

    def _copies(self, in_refs, out_refs, sems, landing):
        x, y, c, chips = _mesh_place()
        me = 2 * x + y
        out = []
        for w in range(len(self.ins)):
            for j, chip in enumerate(chips):
                peer = 2 * chip[0] + chip[1]
                if self.kind == "gather":
                    src, dst, land = in_refs[w].at[c], out_refs[w].at[me, c], out_refs[w].at[peer, c]
                else:
                    src, dst, land = in_refs[w].at[peer], out_refs[w].at[j], out_refs[w].at[j]
                if landing:
                    src, dst = land, land
                out.append(_remote(src, dst, sems, 3 * w + j, (chip[0], chip[1], c)))
        return out

    def start(self, in_refs, out_refs, sems):
        for cp in self._copies(in_refs, out_refs, sems, False):
            cp.start()

    def finish(self, in_refs, out_refs, sems):
        for cp in self._copies(in_refs, out_refs, sems, True):
            cp.wait_recv()
        for cp in self._copies(in_refs, out_refs, sems, False):
            cp.wait_send()


def _carry(comm, n_in, n_out, first, last, refs):
    if comm is None:
        return refs[:n_in], refs[n_in:n_in + n_out], (lambda: None)
    a, b = len(comm.ins), len(comm.out_shape)
    ins, c_in = refs[:n_in], refs[n_in:n_in + a]
    outs, c_out = refs[n_in + a:n_in + a + n_out], refs[n_in + a + n_out:n_in + a + n_out + b]
    sems = refs[n_in + a + n_out + b:n_in + a + n_out + b + 2]
    pl.when(first)(lambda: comm.start(c_in, c_out, sems))
    return ins, outs, (lambda: pl.when(last)(lambda: comm.finish(c_in, c_out, sems)))


def _sbl_fwd(q, k, v, *, width, hd, name, tq=256, comm=None):
    (qa, qo), (ka, ko), (va, vo) = q, k, v
    S = qa.shape[0]
    tq = _tile(S, tq)
    tk = tq
    scale = hd ** -0.5
    n_g, n_q = width // LANES, S // tq

    def body(*refs):
        qi = pl.program_id(1)
        gi = pl.program_id(0)
        got = _carry(comm, 3, 2, (gi == 0) & (qi == 0), (gi == n_g - 1) & (qi == n_q - 1), refs)
        (q_ref, k_ref, v_ref), (o_ref, tot_ref) = got[0], got[1]
        masks = _lane_masks(hd, tq)
        qs = q_ref[...].astype(F32) * scale
        qm = [_keep(qs, m).astype(BF16) for m in masks]
        strict = lax.broadcasted_iota(jnp.int32, (tq, tk), 1) < lax.broadcasted_iota(jnp.int32, (tq, tk), 0)
        later = _tri("after", tk)

        def tile(kb, carry, diag):
            ks = pl.multiple_of(kb * tk, tk)
            kv = k_ref[pl.ds(ks, tk), :].astype(BF16)
            vv = v_ref[pl.ds(ks, tk), :].astype(BF16)
            out = []
            for hh in range(len(masks)):
                acc, c_pos = carry[2 * hh], carry[2 * hh + 1]
                pos, ls = _softplus_parts(_dot(qm[hh], kv, _NT))
                if diag:
                    pos = jnp.where(strict, pos, 0.0)
                pos_after, pos_all = _scan_cols(pos, later, True)
                w = jnp.exp(ls - (pos_after + c_pos))
                if diag:
                    w = jnp.where(strict, w, 0.0)
                out += [acc + _dot(w.astype(BF16), vv), c_pos + pos_all]
            return tuple(out)

        init = (jnp.zeros((tq, LANES), F32), jnp.zeros((tq, 1), F32)) * len(masks)
        carry = tile(qi, init, True)
        carry = lax.fori_loop(0, qi, lambda i, c: tile(qi - 1 - i, c, False), carry)
        o_ref[...] = _merge(carry[0::2], masks).astype(o_ref.dtype)
        tot_ref[...] = _merge([jnp.broadcast_to(-c, (tq, LANES)) for c in carry[1::2]], masks)
        got[2]()

    c_ins, c_in_specs, c_out_specs, c_out_shape, c_scratch = _comm_args(comm)
    return pl.pallas_call(
        body, name=name, grid=(n_g, n_q),
        in_specs=[_cols(tq, qo), _cols_all(S, ko), _cols_all(S, vo)] + c_in_specs,
        out_specs=[_cols(tq, 0), _cols(tq, 0)] + c_out_specs,
        out_shape=[jax.ShapeDtypeStruct((S, width), BF16), jax.ShapeDtypeStruct((S, width), F32)] + c_out_shape,
        scratch_shapes=c_scratch,
        compiler_params=_params(("arbitrary", "arbitrary")),
    )(qa, ka, va, *c_ins)


def _comm_args(comm):
    if comm is None:
        return [], [], [], [], []
    return comm.ins, [_ANY] * len(comm.ins), [_ANY] * len(comm.out_shape), comm.out_shape, comm.scratch


def _sbl_bwd(q, k, v, do, tot, *, width, hd, name, tq=256, comm=None):
    (qa, qo), (ka, ko), (va, vo) = q, k, v
    S = qa.shape[0]
    tq = _tile(S, tq)
    tk = tq
    scale = hd ** -0.5
    n_g, n_q = width // LANES, S // tq

    def body(*refs):
        qi = pl.program_id(1)
        gi = pl.program_id(0)
        got = _carry(comm, 5, 3, (gi == 0) & (qi == 0), (gi == n_g - 1) & (qi == n_q - 1), refs)
        (q_ref, k_ref, v_ref, do_ref, tot_ref), (dq_ref, dk_ref, dv_ref) = got[0], got[1]

        @pl.when(qi == 0)
        def _():
            dk_ref[...] = jnp.zeros_like(dk_ref)
            dv_ref[...] = jnp.zeros_like(dv_ref)

        masks = _lane_masks(hd, tq)
        qs = q_ref[...].astype(F32) * scale
        qm = [_keep(qs, m).astype(BF16) for m in masks]
        dov = [_keep(do_ref[...], m).astype(BF16) for m in masks]
        rest = [-_row_value(tot_ref[...], m) for m in masks]
        strict = lax.broadcasted_iota(jnp.int32, (tq, tk), 1) < lax.broadcasted_iota(jnp.int32, (tq, tk), 0)
        later, before = _tri("after", tk), _tri("before", tk)

        def tile(kb, carry, diag):
            ks = pl.multiple_of(kb * tk, tk)
            kv = k_ref[pl.ds(ks, tk), :].astype(BF16)
            vv = v_ref[pl.ds(ks, tk), :].astype(BF16)
            out = []
            dk_t, dv_t = None, None
            for hh in range(len(masks)):
                dq, c_pos, c_g = carry[3 * hh:3 * hh + 3]
                pos, ls = _softplus_parts(_dot(qm[hh], kv, _NT))
                if diag:
                    pos = jnp.where(strict, pos, 0.0)
                pos_after, pos_all = _scan_cols(pos, later, True)
                c_pos = c_pos + pos_all
                w = jnp.exp(ls - (pos_after + (rest[hh] - c_pos)))
                if diag:
                    w = jnp.where(strict, w, 0.0)
                g = _dot(dov[hh], vv, _NT) * w
                g_before, g_all = _scan_cols(g, before, False)
                g_before = g_before + c_g
                dz = g - jnp.exp(ls) * (g + g_before)
                if diag:
                    dz = jnp.where(strict, dz, 0.0)
                dzb = dz.astype(BF16)
                dk_h = _dot(dzb, qm[hh], _TN)
                dv_h = _dot(w.astype(BF16), dov[hh], _TN)
                dk_t = dk_h if dk_t is None else dk_t + dk_h
                dv_t = dv_h if dv_t is None else dv_t + dv_h
                out += [dq + _dot(dzb, kv), c_pos, c_g + g_all]
            dk_ref[pl.ds(ks, tk), :] += dk_t
            dv_ref[pl.ds(ks, tk), :] += dv_t
            return tuple(out)

        zero = jnp.zeros((tq, 1), F32)
        init = (jnp.zeros((tq, LANES), F32), zero, zero) * len(masks)
        carry = lax.fori_loop(0, qi, lambda kb, c: tile(kb, c, False), init)
        carry = tile(qi, carry, True)
        dq_ref[...] = _merge(carry[0::3], masks) * scale
        got[2]()

    full = jax.ShapeDtypeStruct((S, width), F32)
    c_ins, c_in_specs, c_out_specs, c_out_shape, c_scratch = _comm_args(comm)
    return pl.pallas_call(
        body, name=name, grid=(n_g, n_q),
        in_specs=[_cols(tq, qo), _cols_all(S, ko), _cols_all(S, vo), _cols(tq, do[1]), _cols(tq, tot[1])] + c_in_specs,
        out_specs=[_cols(tq, 0), _cols_all(S, 0), _cols_all(S, 0)] + c_out_specs,
        out_shape=[full, full, full] + c_out_shape,
        scratch_shapes=c_scratch,
        compiler_params=_params(("arbitrary", "arbitrary")),
    )(qa, ka, va, do[0], tot[0], *c_ins)


def _sml_fwd(q, k, v, bias=None, *, width, hd, causal, name, tq=256, tk=256):
    (qa, qo), (ka, ko), (va, vo) = q, k, v
    S, Sk = qa.shape[0], ka.shape[0]
    tq, tk = _tile(S, tq), _tile(Sk, tk)
    if causal:
        assert tq == tk and S == Sk
    nk = Sk // tk
    hpg = LANES // hd
    scale = hd ** -0.5
    has_bias = bias is not None

    def body(*refs):
        q_ref, k_ref, v_ref = refs[:3]
        o_ref, lse_ref = refs[-2:]
        qi = pl.program_id(1)
        masks = _lane_masks(hd, tq)
        qs = q_ref[...].astype(F32) * scale
        qm = [_keep(qs, m).astype(BF16) for m in masks]
        allowed = lax.broadcasted_iota(jnp.int32, (tq, tk), 1) <= lax.broadcasted_iota(jnp.int32, (tq, tk), 0)

        def tile(kb, carry, diag):
            ks = pl.multiple_of(kb * tk, tk)
            kv = k_ref[pl.ds(ks, tk), :].astype(BF16)
            vv = v_ref[pl.ds(ks, tk), :].astype(BF16)
            out = []
            for hh in range(hpg):
                m, l, acc = carry[3 * hh:3 * hh + 3]
                z = _dot(qm[hh], kv, _NT)
                if has_bias:
                    z = z + refs[3][hh] - refs[4][hh, kb]
                if diag:
                    z = jnp.where(allowed, z, NEG_INF)
                m2 = jnp.maximum(m, jnp.max(z, axis=1, keepdims=True))
                p = jnp.exp(z - m2)
                alpha = jnp.exp(m - m2)
                out += [m2, alpha * l + jnp.sum(p, axis=1, keepdims=True), alpha * acc + _dot(p.astype(BF16), vv)]
            return tuple(out)

        init = (jnp.full((tq, 1), NEG_INF, F32), jnp.zeros((tq, 1), F32), jnp.zeros((tq, LANES), F32)) * hpg
        if causal:
            carry = lax.fori_loop(0, qi, lambda kb, c: tile(kb, c, False), init)
            carry = tile(qi, carry, True)
        else:
            carry = lax.fori_loop(0, nk, lambda kb, c: tile(kb, c, False), init)
        o_ref[...] = _merge([acc / l for l, acc in zip(carry[1::3], carry[2::3])], masks).astype(o_ref.dtype)
        lse_ref[...] = _merge([jnp.broadcast_to(m + jnp.log(l), (tq, LANES)) for m, l in zip(carry[0::3], carry[1::3])], masks)

    in_specs = [_cols(tq, qo), _cols_all(Sk, ko), _cols_all(Sk, vo)]
    ins = [qa, ka, va]
    if has_bias:
        in_specs += [pl.BlockSpec((hpg, tq, 1), lambda g, i: (g, i, 0)),
                     pl.BlockSpec((hpg, nk, 1, tk), lambda g, i: (g, 0, 0, 0))]
        ins += list(bias)
    return pl.pallas_call(
        body, name=name, grid=(width // LANES, S // tq),
        in_specs=in_specs, out_specs=[_cols(tq, 0), _cols(tq, 0)],
        out_shape=[jax.ShapeDtypeStruct((S, width), BF16), jax.ShapeDtypeStruct((S, width), F32)],
        compiler_params=_params(("parallel", "arbitrary")),
    )(*ins)


def _sml_bwd(q, k, v, o, lse, do, bias=None, *, width, hd, causal, name, tq=256, tk=256):
    (qa, qo), (ka, ko), (va, vo) = q, k, v
    S, Sk = qa.shape[0], ka.shape[0]
    tq, tk = _tile(S, tq), _tile(Sk, tk)
    nk = Sk // tk
    hpg = LANES // hd
    scale = hd ** -0.5
    has_bias = bias is not None
    n_in = 8 if has_bias else 6

    def body(*refs):
        q_ref, k_ref, v_ref, o_ref, lse_ref, do_ref = refs[:6]
        dq_ref, dk_ref, dv_ref = refs[n_in:n_in + 3]
        qi = pl.program_id(1)

        @pl.when(qi == 0)
        def _():
            dk_ref[...] = jnp.zeros_like(dk_ref)
            dv_ref[...] = jnp.zeros_like(dv_ref)
            if has_bias:
                refs[n_in + 4][...] = jnp.zeros_like(refs[n_in + 4])

        masks = _lane_masks(hd, tq)
        qs = q_ref[...].astype(F32) * scale
        qm = [_keep(qs, m).astype(BF16) for m in masks]
        do32 = do_ref[...]
        dov = [_keep(do32, m).astype(BF16) for m in masks]
        prod = do32 * o_ref[...].astype(F32)
        delta = [jnp.sum(_keep(prod, m), axis=1, keepdims=True) for m in masks]
        lses = [_row_value(lse_ref[...], m) for m in masks]
        allowed = lax.broadcasted_iota(jnp.int32, (tq, tk), 1) <= lax.broadcasted_iota(jnp.int32, (tq, tk), 0)

        def tile(kb, carry, diag):
            ks = pl.multiple_of(kb * tk, tk)
            kv = k_ref[pl.ds(ks, tk), :].astype(BF16)
            vv = v_ref[pl.ds(ks, tk), :].astype(BF16)
            out = []
            dk_t, dv_t = None, None
            for hh in range(hpg):
                dq, db_row = carry[2 * hh:2 * hh + 2]
                z = _dot(qm[hh], kv, _NT)
                if has_bias:
                    z = z + refs[6][hh] - refs[7][hh, kb]
                p = jnp.exp(z - lses[hh])
                if diag:
                    p = jnp.where(allowed, p, 0.0)
                dz = p * (_dot(dov[hh], vv, _NT) - delta[hh])
                dzb = dz.astype(BF16)
                dk_h = _dot(dzb, qm[hh], _TN)
                dv_h = _dot(p.astype(BF16), dov[hh], _TN)
                dk_t = dk_h if dk_t is None else dk_t + dk_h
                dv_t = dv_h if dv_t is None else dv_t + dv_h
                if has_bias:
                    db_row = db_row + jnp.sum(dz, axis=1, keepdims=True)
                    refs[n_in + 4][hh, kb] += jnp.sum(dz, axis=0, keepdims=True)
                out += [dq + _dot(dzb, kv), db_row]
            dk_ref[pl.ds(ks, tk), :] += dk_t
            dv_ref[pl.ds(ks, tk), :] += dv_t
            return tuple(out)

        init = (jnp.zeros((tq, LANES), F32), jnp.zeros((tq, 1), F32)) * hpg
        if causal:
            carry = lax.fori_loop(0, qi, lambda kb, c: tile(kb, c, False), init)
            carry = tile(qi, carry, True)
        else:
            carry = lax.fori_loop(0, nk, lambda kb, c: tile(kb, c, False), init)
        dq_ref[...] = _merge(carry[0::2], masks) * scale
        if has_bias:
            for hh in range(hpg):
                refs[n_in + 3][hh] = carry[2 * hh + 1]

    in_specs = [_cols(tq, qo), _cols_all(Sk, ko), _cols_all(Sk, vo), _cols(tq, o[1]), _cols(tq, lse[1]), _cols(tq, do[1])]
    ins = [qa, ka, va, o[0], lse[0], do[0]]
    out_specs = [_cols(tq, 0), _cols_all(Sk, 0), _cols_all(Sk, 0)]
    out_shape = [jax.ShapeDtypeStruct((S, width), F32), jax.ShapeDtypeStruct((Sk, width), F32),
                 jax.ShapeDtypeStruct((Sk, width), F32)]
    if has_bias:
        rspec = pl.BlockSpec((hpg, tq, 1), lambda g, i: (g, i, 0))
        cspec = pl.BlockSpec((hpg, nk, 1, tk), lambda g, i: (g, 0, 0, 0))
        in_specs += [rspec, cspec]
        ins += list(bias)
        out_specs += [rspec, cspec]
        n_heads = width // hd
        out_shape += [jax.ShapeDtypeStruct((n_heads, S, 1), F32), jax.ShapeDtypeStruct((n_heads, nk, 1, tk), F32)]
    return pl.pallas_call(
        body, name=name, grid=(width // LANES, S // tq),
        in_specs=in_specs, out_specs=out_specs, out_shape=out_shape,
        compiler_params=_params(("parallel", "arbitrary")),
    )(*ins)


def _head_sums(t, masks):
    sums = [jnp.sum(_keep(t, m), axis=1, keepdims=True) for m in masks]
    return _merge([jnp.broadcast_to(s, t.shape) for s in sums], masks) if len(masks) > 1 else sums[0]


def _hnorm_fwd(x, g_lanes, *, width, hd, name, tr=512):
    xa, xo = x
    R = xa.shape[0]
    tr = _tile(R, tr, align=16)
    n_blk = width // LANES

    def body(x_ref, g_ref, o_ref):
        masks = _lane_masks(hd, tr)
        for j in range(n_blk):
            sl = slice(j * LANES, (j + 1) * LANES)
            xv = x_ref[:, sl].astype(F32)
            r = lax.rsqrt(_head_sums(xv * xv, masks) * (1.0 / hd) + EPS)
            o_ref[:, sl] = (xv * r * g_ref[...]).astype(o_ref.dtype)

    assert (xo * LANES) % width == 0
    return pl.pallas_call(
        body, name=name, grid=(R // tr,),
        in_specs=[pl.BlockSpec((tr, width), lambda i: (i, xo * LANES // width)), pl.BlockSpec((1, LANES), lambda i: (0, 0))],
        out_specs=pl.BlockSpec((tr, width), lambda i: (i, 0)),
        out_shape=jax.ShapeDtypeStruct((R, width), BF16),
        compiler_params=_params(("parallel",)),
    )(xa, g_lanes)


def _hnorm_bwd(x, g_lanes, dy, *, width, hd, name, tr=512):
    xa, xo = x
    R = xa.shape[0]
    tr = _tile(R, tr, align=16)
    n_blk = width // LANES

    def body(x_ref, g_ref, dy_ref, dx_ref, dg_ref):
        masks = _lane_masks(hd, tr)
        dg = jnp.zeros((1, LANES), F32)
        for j in range(n_blk):
            sl = slice(j * LANES, (j + 1) * LANES)
            xv = x_ref[:, sl].astype(F32)
            dyv = dy_ref[:, sl].astype(F32)
            r = lax.rsqrt(_head_sums(xv * xv, masks) * (1.0 / hd) + EPS)
            xh = xv * r
            dyg = dyv * g_ref[...]
            c = _head_sums(dyg * xh, masks) * (1.0 / hd)
            dx_ref[:, sl] = (r * (dyg - xh * c)).astype(dx_ref.dtype)
            dg = dg + jnp.sum(dyv * xh, axis=0, keepdims=True)
        if hd * 2 == LANES:
            dg8 = jnp.broadcast_to(dg, (8, LANES))
            dg = (dg8 + pltpu.roll(dg8, shift=hd, axis=1))[0:1]
        else:
            assert hd == LANES

        @pl.when(pl.program_id(0) == 0)
        def _():
            dg_ref[...] = jnp.zeros_like(dg_ref)

        dg_ref[...] += dg

    assert (xo * LANES) % width == 0
    return pl.pallas_call(
        body, name=name, grid=(R // tr,),
        in_specs=[pl.BlockSpec((tr, width), lambda i: (i, xo * LANES // width)), pl.BlockSpec((1, LANES), lambda i: (0, 0)),
                  pl.BlockSpec((tr, width), lambda i: (i, 0))],
        out_specs=[pl.BlockSpec((tr, width), lambda i: (i, 0)), pl.BlockSpec((1, LANES), lambda i: (0, 0))],
        out_shape=[jax.ShapeDtypeStruct((R, width), BF16), jax.ShapeDtypeStruct((1, LANES), F32)],
        compiler_params=_params(("arbitrary",)),
    )(xa, g_lanes, dy)


def _split3_dot(x, tri):
    a = x.astype(BF16)
    r = x - a.astype(F32)
    b = r.astype(BF16)
    c = (r - b.astype(F32)).astype(BF16)
    return _dot(a, tri) + _dot(b, tri) + _dot(c, tri)


def _forget_fwd(logit_t, b_col, *, name, blk=512):
    H, S = logit_t.shape
    blk = _tile(S, blk)

    def body(l_ref, b_ref, f_ref):
        r_i = lax.broadcasted_iota(jnp.int32, (blk, blk), 0)
        c_i = lax.broadcasted_iota(jnp.int32, (blk, blk), 1)
        upto = (r_i <= c_i).astype(BF16)
        carry = jnp.zeros((H, 1), F32)
        for j in range(S // blk):
            u = l_ref[:, j * blk:(j + 1) * blk] + b_ref[...]
            lf, _ = _log_sigmoid_pair(u)
            f_ref[:, j * blk:(j + 1) * blk] = _split3_dot(lf, upto) + carry
            carry = carry + jnp.sum(lf, axis=1, keepdims=True)

    return pl.pallas_call(
        body, name=name,
        out_shape=jax.ShapeDtypeStruct((H, S), F32),
        compiler_params=pltpu.CompilerParams(vmem_limit_bytes=VMEM_LIMIT),
    )(logit_t, b_col)


def _forget_bwd(logit_t, b_col, d_f, *, name, blk=512):
    H, S = logit_t.shape
    blk = _tile(S, blk)

    def body(l_ref, b_ref, df_ref, dl_ref, db_ref):
        r_i = lax.broadcasted_iota(jnp.int32, (blk, blk), 0)
        c_i = lax.broadcasted_iota(jnp.int32, (blk, blk), 1)
        fromon = (r_i >= c_i).astype(BF16)
        carry = jnp.zeros((H, 1), F32)
        db = jnp.zeros((H, 1), F32)
        for j in reversed(range(S // blk)):
            sl = slice(j * blk, (j + 1) * blk)
            dfv = df_ref[:, sl]
            d_lf = _split3_dot(dfv, fromon) + carry
            carry = carry + jnp.sum(dfv, axis=1, keepdims=True)
            u = l_ref[:, sl] + b_ref[...]
            _, lsn = _log_sigmoid_pair(u)
            dl = d_lf * jnp.exp(lsn)
            dl_ref[:, sl] = dl
            db = db + jnp.sum(dl, axis=1, keepdims=True)
        db_ref[...] = db

    return pl.pallas_call(
        body, name=name,
        out_shape=[jax.ShapeDtypeStruct((H, S), F32), jax.ShapeDtypeStruct((H, 1), F32)],
        compiler_params=pltpu.CompilerParams(vmem_limit_bytes=VMEM_LIMIT),
    )(logit_t, b_col, d_f)


def _sigmoid(t):
    return 1.0 / (1.0 + jnp.exp(-t))


def _gate_fwd(o3, w3, proj, D, *, name, tm=256):
    S = proj.shape[0]
    tm = _tile(S, tm)

    def body(o0, o1, o2, w0, w1, w2, g0, g1, g2, out_ref):
        acc = None
        for o_ref, w_ref, g_ref in ((o0, w0, g0), (o1, w1, g1), (o2, w2, g2)):
            t = _sigmoid(g_ref[...]) * _dot(o_ref[...], w_ref[...])
            acc = t if acc is None else acc + t
        out_ref[...] = acc.astype(out_ref.dtype)

    ospec = lambda d: pl.BlockSpec((tm, d), lambda i: (i, 0))
    wspec = lambda w: pl.BlockSpec(w.shape, lambda i: (0, 0))
    gspec = lambda j: pl.BlockSpec((tm, D), lambda i: (i, j))
    return pl.pallas_call(
        body, name=name, grid=(S // tm,),
        in_specs=[ospec(o.shape[1]) for o in o3] + [wspec(w) for w in w3] + [gspec(j) for j in range(3)],
        out_specs=pl.BlockSpec((tm, D), lambda i: (i, 0)),
        out_shape=jax.ShapeDtypeStruct((S, D), BF16),
        compiler_params=_params(("parallel",)),
    )(*o3, *w3, proj, proj, proj)


def _gate_bwd(o3, w3, proj, dmerged, D, *, name, tm=256):
    S = proj.shape[0]
    tm = _tile(S, tm)

    def body(o0, o1, o2, w0, w1, w2, g0, g1, g2, dm_ref, dg_ref, db0, db1, db2, do0, do1, do2):
        dm = dm_ref[...]
        for j, (o_ref, w_ref, g_ref, db_ref, do_ref) in enumerate(
                ((o0, w0, g0, db0, do0), (o1, w1, g1, db1, do1), (o2, w2, g2, db2, do2))):
            s = _sigmoid(g_ref[...])
            br = _dot(o_ref[...], w_ref[...])
            dg_ref[:, j * D:(j + 1) * D] = (dm * br * s * (1.0 - s)).astype(dg_ref.dtype)
            dbr = (dm * s).astype(BF16)
            db_ref[...] = dbr
            do_ref[...] = _dot(dbr, w_ref[...], _NT)

    ospec = lambda d: pl.BlockSpec((tm, d), lambda i: (i, 0))
    wspec = lambda w: pl.BlockSpec(w.shape, lambda i: (0, 0))
    gspec = lambda j: pl.BlockSpec((tm, D), lambda i: (i, j))
    dspec = pl.BlockSpec((tm, D), lambda i: (i, 0))
    return pl.pallas_call(
        body, name=name, grid=(S // tm,),
        in_specs=[ospec(o.shape[1]) for o in o3] + [wspec(w) for w in w3] + [gspec(j) for j in range(3)] + [dspec],
        out_specs=[pl.BlockSpec((tm, 3 * D), lambda i: (i, 0))] + [dspec] * 3 + [ospec(o.shape[1]) for o in o3],
        out_shape=[jax.ShapeDtypeStruct((S, 3 * D), BF16)] + [jax.ShapeDtypeStruct((S, D), BF16)] * 3
        + [jax.ShapeDtypeStruct((S, o.shape[1]), F32) for o in o3],
        compiler_params=_params(("parallel",)),
    )(*o3, *w3, proj, proj, proj, dmerged)


def _loss_sum(dy, D, *, name):
    R, C = dy.shape
    tr = _row_tile(R, C, 2)

    def body(dy_ref, out_ref):
        @pl.when(pl.program_id(0) == 0)
        def _():
            out_ref[...] = jnp.zeros_like(out_ref)

        v = dy_ref[...]
        out_ref[...] += (0.5 * D) * jnp.sum(v * v)

    return pl.pallas_call(
        body, name=name, grid=(R // tr,),
        in_specs=[pl.BlockSpec((tr, C), lambda i: (i, 0))],
        out_specs=pl.BlockSpec((8, 128), lambda i: (0, 0)),
        out_shape=jax.ShapeDtypeStruct((8, 128), F32),
        compiler_params=_params(("arbitrary",)),
    )(dy)[0, 0]


def _adamw_math(w, g, m, v):
    m2 = ADAM_B1 * m + (1.0 - ADAM_B1) * g
    v2 = ADAM_B2 * v + (1.0 - ADAM_B2) * (g * g)
    m_hat = m2 / (1.0 - ADAM_B1 ** ADAM_STEP)
    v_hat = v2 / (1.0 - ADAM_B2 ** ADAM_STEP)
    delta = -ADAM_LR * (m_hat / (jnp.sqrt(v_hat) + ADAM_EPS) + ADAM_WD * w)
    return delta, m2, v2


def _adamw(w, g, m, v, *, name):
    return _ew(_adamw_math, [w, g, m, v], (F32, F32, F32), name=name)


def _adamw_small(w, parts, m, v, *, name):
    n = parts.shape[0]

    def body(w_ref, p_ref, m_ref, v_ref, g_ref, d_ref, m2_ref, v2_ref):
        g = p_ref[0]
        for i in range(1, n):
            g = g + p_ref[i]
        g_ref[...] = g
        d_ref[...], m2_ref[...], v2_ref[...] = _adamw_math(w_ref[...], g, m_ref[...], v_ref[...])

    shp = jax.ShapeDtypeStruct(w.shape, F32)
    return pl.pallas_call(body, name=name, out_shape=[shp] * 4)(w, parts, m, v)


_ANY = pl.BlockSpec(memory_space=pl.ANY)


def _mesh_place():
    x, y, c = lax.axis_index("x"), lax.axis_index("y"), lax.axis_index("c")
    chips = [(1 - x, y), (x, 1 - y), (1 - x, 1 - y)]
    return x, y, c, chips


def _remote(src, dst, sems, i, to):
    send_sems, recv_sems = sems
    return pltpu.make_async_remote_copy(src_ref=src, dst_ref=dst, send_sem=send_sems.at[i], recv_sem=recv_sems.at[i],
                                        device_id=to, device_id_type=MESH_ID)


def _gather_weights(shards, *, name):
    n = len(shards)

    def body(*refs):
        ins, outs = refs[:n], refs[n:2 * n]
        sems = refs[2 * n:2 * n + 2]
        x, y, c, chips = _mesh_place()
        me = 2 * x + y
        sibling = (x, y, 1 - c)
        sent = []
        for w in range(n):
            for j, chip in enumerate(chips):
                cp = _remote(ins[w].at[c], outs[w].at[me, c], sems, 6 * w + j, (chip[0], chip[1], c))
                cp.start()
                sent.append(cp)
        for w in range(n):
            for j, chip in enumerate(chips):
                got = outs[w].at[2 * chip[0] + chip[1], c]
                _remote(got, got, sems, 6 * w + j, sibling).wait_recv()
                cp = _remote(got, got, sems, 6 * w + 3 + j, sibling)
                cp.start()
                sent.append(cp)
        for w in range(n):
            for j, chip in enumerate(chips):
                got = outs[w].at[2 * chip[0] + chip[1], 1 - c]
                _remote(got, got, sems, 6 * w + 3 + j, sibling).wait_recv()
        for cp in sent:
            cp.wait_send()

    outs = pl.pallas_call(
        body, name=name,
        in_specs=[_ANY] * n, out_specs=[_ANY] * n,
        out_shape=[jax.ShapeDtypeStruct((N_CHIPS,) + s.shape, s.dtype) for s in shards],
        scratch_shapes=[pltpu.SemaphoreType.DMA((6 * n,)), pltpu.SemaphoreType.DMA((6 * n,))],
    )(*shards)
    me = 2 * lax.axis_index("x") + lax.axis_index("y")
    return [lax.dynamic_update_index_in_dim(o, s, me, 0) for o, s in zip(outs, shards)]


def _forward_halves(gathered, *, name):
    n = len(gathered)

    def body(*refs):
        ins, outs = refs[:n], refs[n:2 * n]
        sems = refs[2 * n:2 * n + 2]
        x, y, c, chips = _mesh_place()
        sibling = (x, y, 1 - c)
        sent = []
        for w in range(n):
            for j, chip in enumerate(chips):
                peer = 2 * chip[0] + chip[1]
                cp = _remote(ins[w].at[peer, c], outs[w].at[peer, c], sems, 3 * w + j, sibling)
                cp.start()
                sent.append(cp)
        for w in range(n):
            for j, chip in enumerate(chips):
                land = outs[w].at[2 * chip[0] + chip[1], 1 - c]
                _remote(land, land, sems, 3 * w + j, sibling).wait_recv()
        for cp in sent:
            cp.wait_send()

    return pl.pallas_call(
        body, name=name,
        in_specs=[_ANY] * n, out_specs=[_ANY] * n,
        out_shape=[jax.ShapeDtypeStruct(g.shape, g.dtype) for g in gathered],
        input_output_aliases={w: w for w in range(n)},
        scratch_shapes=[pltpu.SemaphoreType.DMA((3 * n,)), pltpu.SemaphoreType.DMA((3 * n,))],
    )(*gathered)


def _exchange_siblings(grads, *, name):
    n = len(grads)

    def body(*refs):
        ins, got = refs[:n], refs[n:2 * n]
        sems = refs[2 * n:2 * n + 2]
        x, y, c, _ = _mesh_place()
        sibling = (x, y, 1 - c)
        sent = []
        for w in range(n):
            for s in range(N_CHIPS):
                cp = _remote(ins[w].at[s, 1 - c], got[w].at[s], sems, N_CHIPS * w + s, sibling)
                cp.start()
                sent.append(cp)
        for w in range(n):
            for s in range(N_CHIPS):
                _remote(got[w].at[s], got[w].at[s], sems, N_CHIPS * w + s, sibling).wait_recv()
        for cp in sent:
            cp.wait_send()

    n_sem = N_CHIPS * n
    got = pl.pallas_call(
        body, name=name,
        in_specs=[_ANY] * n, out_specs=[_ANY] * n,
        out_shape=[jax.ShapeDtypeStruct((N_CHIPS,) + g.shape[2:], g.dtype) for g in grads],
        scratch_shapes=[pltpu.SemaphoreType.DMA((n_sem,)), pltpu.SemaphoreType.DMA((n_sem,))],
    )(*grads)
    c = lax.axis_index("c")
    return [lax.dynamic_index_in_dim(g, c, 1, keepdims=False) for g in grads], got


def _exchange_chips(parts, *, name):
    n = len(parts)

    def body(*refs):
        ins, got = refs[:n], refs[n:2 * n]
        sems = refs[2 * n:2 * n + 2]
        x, y, c, chips = _mesh_place()
        sent = []
        for w in range(n):
            for j, chip in enumerate(chips):
                cp = _remote(ins[w].at[2 * chip[0] + chip[1]], got[w].at[j], sems, 3 * w + j, (chip[0], chip[1], c))
                cp.start()
                sent.append(cp)
        for w in range(n):
            for j in range(3):
                _remote(got[w].at[j], got[w].at[j], sems, 3 * w + j, (x, y, c)).wait_recv()
        for cp in sent:
            cp.wait_send()

    got = pl.pallas_call(
        body, name=name,
        in_specs=[_ANY] * n, out_specs=[_ANY] * n,
        out_shape=[jax.ShapeDtypeStruct((3,) + p.shape[1:], p.dtype) for p in parts],
        scratch_shapes=[pltpu.SemaphoreType.DMA((3 * n,)), pltpu.SemaphoreType.DMA((3 * n,))],
    )(*parts)
    me = 2 * lax.axis_index("x") + lax.axis_index("y")
    return [lax.dynamic_index_in_dim(p, me, 0, keepdims=False) for p in parts], got


def _share_halves(halves, small):
    n = len(halves)

    def body(*refs):
        ins, small_ref = refs[:n], refs[n]
        outs, small_out = refs[n + 1:2 * n + 1], refs[2 * n + 1]
        sems = refs[2 * n + 2:2 * n + 4]
        x, y, c, chips = _mesh_place()
        sibling = (x, y, 1 - c)
        me = 4 * x + 2 * y + c
        sent = [_remote(ins[w], outs[w].at[c], sems, w, sibling) for w in range(n)]
        peers = [sibling] + [(ch[0], ch[1], cc) for ch in chips for cc in (c, 1 - c)]
        sent += [_remote(small_ref, small_out.at[me], sems, n + j, peer) for j, peer in enumerate(peers)]
        for cp in sent:
            cp.start()
        for w in range(n):
            _remote(outs[w].at[1 - c], outs[w].at[1 - c], sems, w, sibling).wait_recv()
        for j, peer in enumerate(peers):
            frm = small_out.at[4 * peer[0] + 2 * peer[1] + peer[2]]
            _remote(frm, frm, sems, n + j, peer).wait_recv()
        for cp in sent:
            cp.wait_send()

    n_sem = n + 7
    outs = pl.pallas_call(
        body, name="share_halves",
        in_specs=[_ANY] * (n + 1), out_specs=[_ANY] * (n + 1),
        out_shape=[jax.ShapeDtypeStruct((2,) + h.shape, h.dtype) for h in halves]
        + [jax.ShapeDtypeStruct((8,) + small.shape, small.dtype)],
        scratch_shapes=[pltpu.SemaphoreType.DMA((n_sem,)), pltpu.SemaphoreType.DMA((n_sem,))],
    )(*halves, small)
    c = lax.axis_index("c")
    me = 4 * lax.axis_index("x") + 2 * lax.axis_index("y") + c
    return ([lax.dynamic_update_index_in_dim(o, h, c, 0) for o, h in zip(outs[:n], halves)],
            lax.dynamic_update_index_in_dim(outs[n], small, me, 0))


EARLY = ("w_ff_down", "w_ff_up", "w_out", "w_branch_sb", "w_branch_fox", "w_branch_mem")


def _local_step(x, mem, target, small, W, gather_rest=None, reduce_early=None, reduce_late=None):
    S, D = x.shape
    o_gate, o_qkv, o_mq, o_f = 0, 3 * D, 3 * D + 2 * 3 * D_SB, 3 * D + 2 * 3 * D_SB + D_MEM
    tq = 512

    h = _rmsnorm_fwd(x, small["g_mix_norm"], BF16, name="mix_norm")
    proj = _mm(h, W["w_in"], name="in_proj", tb=True, tn=768)
    blk = lambda j: (proj, (o_qkv + j * D_SB) // LANES)
    sb_q, sb_k, sb_v, fx_q, fx_k, fx_v = [blk(j) for j in range(6)]
    m_q = (proj, o_mq // LANES)
    f_logit_t = proj[:, o_f:o_f + FOX_HEADS].T
    b_col = small["b_forget"].reshape(FOX_HEADS, 1)
    lanes = lambda g: jnp.tile(g, (1, LANES // g.shape[1]))
    g_fq, g_fk, g_mq, g_mk = [lanes(small[k]) for k in ("g_fox_q", "g_fox_k", "g_mem_q", "g_mem_k")]

    if gather_rest is None:
        o_sb, sb_tot = _sbl_fwd(sb_q, sb_k, sb_v, width=D_SB, hd=HD, name="sb_fwd", tq=tq)
    else:
        comm, finish = gather_rest
        o_sb, sb_tot, *landed = _sbl_fwd(sb_q, sb_k, sb_v, width=D_SB, hd=HD, name="sb_fwd", tq=tq, comm=comm)
        W = {**W, **finish(landed)}

    fq = _hnorm_fwd(fx_q, g_fq, width=D_FOX, hd=HD, name="fox_q_norm")
    fk = _hnorm_fwd(fx_k, g_fk, width=D_FOX, hd=HD, name="fox_k_norm")
    f_cum = _forget_fwd(f_logit_t, b_col, name="forget_fwd")
    tkf = _tile(S, tq)
    f_bias = (f_cum.reshape(FOX_HEADS, S, 1), f_cum.reshape(FOX_HEADS, S // tkf, 1, tkf))
    o_fox, fox_lse = _sml_fwd((fq, 0), (fk, 0), fx_v, f_bias, width=D_FOX, hd=HD, causal=True, name="fox_fwd",
                              tq=tq, tk=tq)

    mh = _rmsnorm_fwd(mem, small["g_mem_norm"], BF16, name="mem_norm")
    mkv = _mm(mh, W["w_mem_kv"], name="mem_kv")
    mv = (mkv, D_MEM // LANES)
    mq = _hnorm_fwd(m_q, g_mq, width=D_MEM, hd=MEM_HD, name="mem_q_norm")
    mk = _hnorm_fwd((mkv, 0), g_mk, width=D_MEM, hd=MEM_HD, name="mem_k_norm")
    o_mem, mem_lse = _sml_fwd((mq, 0), (mk, 0), mv, width=D_MEM, hd=MEM_HD, causal=False, name="mem_fwd", tq=tq, tk=256)

    o3 = [o_sb, o_fox, o_mem]
    w3 = [W["w_branch_sb"], W["w_branch_fox"], W["w_branch_mem"]]
    merged = _gate_fwd(o3, w3, proj, D, name="gate_fwd")
    x1 = _mm(merged, W["w_out"], name="out_proj", extras=(x,), epilogue=lambda acc, res: (res + acc,))
    h2 = _rmsnorm_fwd(x1, small["g_mlp_norm"], BF16, name="mlp_norm")

    def relu2(acc):
        u = jnp.maximum(acc, 0.0)
        return u, u * u

    u, a = _mm(h2, W["w_ff_up"], name="ff_up", out_dtypes=(BF16, BF16), epilogue=relu2)
    def head(acc, res, tgt):
        d = (res + acc - tgt) * (1.0 / D)
        return d, d

    dy, dy16 = _mm(a, W["w_ff_down"], name="ff_down", extras=(x1, target), out_dtypes=(F32, BF16), epilogue=head, tm=512)
    loss = _loss_sum(dy, D, name="loss")

    G = {}
    du = _mm(dy16, W["w_ff_down"], name="d_ff_act", tb=True, out_dtypes=(BF16,), extras=(u,),
             epilogue=lambda acc, uu: (acc * (2.0 * uu.astype(F32)),))
    G["w_ff_down"] = _mm(a, dy16, name="d_w_ff_down", ta=True, out_dtypes=(BF16,))
    G["w_ff_up"] = _mm(h2, du, name="d_w_ff_up", ta=True, out_dtypes=(BF16,))
    dh2 = _mm(du, W["w_ff_up"], name="d_mlp_in", tb=True)
    dx1, dg_mlp = _rmsnorm_bwd(x1, small["g_mlp_norm"], dh2, add=dy, name="d_mlp_norm")
    dmerged = _mm(dx1, W["w_out"], name="d_merged", tb=True)
    G["w_out"] = _mm(merged, dx1, name="d_w_out", ta=True, out_dtypes=(BF16,))
    dgate, db0, db1, db2, do_sb, do_fox, do_mem = _gate_bwd(o3, w3, proj, dmerged, D, name="gate_bwd")
    for nm, o, db in zip(("w_branch_sb", "w_branch_fox", "w_branch_mem"), o3, (db0, db1, db2)):
        G[nm] = _mm(o, db, name="d_" + nm, ta=True, out_dtypes=(BF16,))

    early = {}
    if reduce_early is None:
        dsb = _sbl_bwd(sb_q, sb_k, sb_v, (do_sb, 0), (sb_tot, 0), width=D_SB, hd=HD, name="sb_bwd", tq=tq)
    else:
        comm, finish = reduce_early({k: G.pop(k) for k in EARLY})
        *dsb, = _sbl_bwd(sb_q, sb_k, sb_v, (do_sb, 0), (sb_tot, 0), width=D_SB, hd=HD, name="sb_bwd", tq=tq, comm=comm)
        dsb, landed = dsb[:3], dsb[3:]
        early = finish(landed)
    dfq, dfk, dfv, df_row, df_col = _sml_bwd((fq, 0), (fk, 0), fx_v, (o_fox, 0), (fox_lse, 0), (do_fox, 0), f_bias,
                                             width=D_FOX, hd=HD, causal=True, name="fox_bwd", tq=tq, tk=tq)
    dfx_q, dg_fox_q = _hnorm_bwd(fx_q, g_fq, dfq, width=D_FOX, hd=HD, name="d_fox_q_norm")
    dfx_k, dg_fox_k = _hnorm_bwd(fx_k, g_fk, dfk, width=D_FOX, hd=HD, name="d_fox_k_norm")
    d_fcum = df_row.reshape(FOX_HEADS, S) - df_col.reshape(FOX_HEADS, S)
    d_flogit_t, db_forget = _forget_bwd(f_logit_t, b_col, d_fcum, name="forget_bwd")

    dmq_n, dmk_n, dmv = _sml_bwd((mq, 0), (mk, 0), mv, (o_mem, 0), (mem_lse, 0), (do_mem, 0), width=D_MEM, hd=MEM_HD,
                                 causal=False, name="mem_bwd", tq=tq, tk=256)
    dm_q, dg_mem_q = _hnorm_bwd(m_q, g_mq, dmq_n, width=D_MEM, hd=MEM_HD, name="d_mem_q_norm")
    dmk_raw, dg_mem_k = _hnorm_bwd((mkv, 0), g_mk, dmk_n, width=D_MEM, hd=MEM_HD, name="d_mem_k_norm")
    dmkv = jnp.concatenate([dmk_raw, dmv.astype(BF16)], axis=1)
    G["w_mem_kv"] = _mm(mh, dmkv, name="d_w_mem_kv", ta=True, out_dtypes=(BF16,))
    dmh = _mm(dmkv, W["w_mem_kv"], name="d_mem_h", tb=True)
    _, dg_mem = _rmsnorm_bwd(mem, small["g_mem_norm"], dmh, name="d_mem_norm")
    dg_fox_q, dg_fox_k = dg_fox_q[:, :HD], dg_fox_k[:, :HD]

    dproj = jnp.concatenate(
        [dgate] + [t.astype(BF16) for t in (*dsb, dfx_q, dfx_k, dfv, dm_q)]
        + [d_flogit_t.T.astype(BF16), jnp.zeros((S, F_PAD - FOX_HEADS), BF16)], axis=1)
    g_w_in = _mm(dproj, h, name="d_w_in", ta=True, out_dtypes=(BF16,), tm=768)
    if reduce_late is None:
        G["w_in"] = g_w_in
        dh = _mm(dproj, W["w_in"], name="d_mix_in", tk=2304)
    else:
        comm, finish = reduce_late({"w_in": g_w_in, "w_mem_kv": G.pop("w_mem_kv")})
        dh, *landed = _mm(dproj, W["w_in"], name="d_mix_in", tk=2304, comm=comm)
        early.update(finish(landed))
    grad_x, dg_mix = _rmsnorm_bwd(x, small["g_mix_norm"], dh, add=dx1, name="d_mix_norm")

    small_grads = dict(g_mix_norm=dg_mix, g_mem_norm=dg_mem, b_forget=db_forget.reshape(1, FOX_HEADS),
                       g_fox_q=dg_fox_q, g_fox_k=dg_fox_k, g_mem_q=dg_mem_q, g_mem_k=dg_mem_k, g_mlp_norm=dg_mlp)
    return loss, grad_x, G, small_grads, early


BIG = ("w_in", "w_mem_kv", "w_branch_sb", "w_branch_fox", "w_branch_mem", "w_out", "w_ff_up", "w_ff_down")
COLUMN_SHARDED = ("w_in", "w_branch_sb", "w_branch_fox", "w_branch_mem", "w_ff_up")
SMALL = ("g_mix_norm", "g_mem_norm", "b_forget", "g_fox_q", "g_fox_k", "g_mem_q", "g_mem_k", "g_mlp_norm")
ORDER = ("g_mix_norm", "g_mem_norm", "w_in", "b_forget", "g_fox_q", "g_fox_k", "g_mem_q", "g_mem_k", "w_mem_kv",
         "w_branch_sb", "w_branch_fox", "w_branch_mem", "w_out", "g_mlp_norm", "w_ff_up", "w_ff_down")


def _unshard(name, gathered):
    n, _, rh, c = gathered.shape
    t = gathered.reshape(n, 2 * rh, c)
    if name in COLUMN_SHARDED:
        return t.transpose(1, 0, 2).reshape(2 * rh, n * c)
    return t.reshape(n * 2 * rh, c)


def _reshard(name, full):
    if name in COLUMN_SHARDED:
        r, c = full.shape
        t = full.reshape(r, N_CHIPS, c // N_CHIPS).transpose(1, 0, 2)
    else:
        r, c = full.shape[0] // N_CHIPS, full.shape[1]
        t = full.reshape(N_CHIPS, r, c)
    return t.reshape(N_CHIPS, 2, t.shape[1] // 2, t.shape[2])


IN_SHARD_PAD = 32


def _in_segments(D):
    n_qkv = 6 * D_SB
    o_mq, o_gate = n_qkv + FOX_HEADS, n_qkv + FOX_HEADS + D_MEM
    return [(0, n_qkv, 3 * D), (n_qkv, o_mq, 3 * D + n_qkv + D_MEM), (o_mq, o_gate, 3 * D + n_qkv), (o_gate, o_gate + 3 * D, 0)]


def _to_padded_rows(g, D, shard, pad):
    pieces = []
    for a, b, _ in sorted(_in_segments(D), key=lambda s: s[2]):
        r = a
        while r < b:
            e = min(b, (r // shard + 1) * shard)
            off = (r // shard) * pad
            pieces.append(g[r + off:e + off])
            r = e
    pieces.append(jnp.zeros((F_PAD - FOX_HEADS, g.shape[1]), g.dtype))
    return jnp.concatenate(pieces, axis=0)


def _from_padded_rows(gp, D, shard, pad, n):
    pieces = []
    for s in range(n):
        lo, hi = s * shard, (s + 1) * shard
        for a, b, p in _in_segments(D):
            x0, x1 = max(a, lo), min(b, hi)
            if x0 < x1:
                pieces.append(gp[p + x0 - a:p + x1 - a])
        if pad:
            pieces.append(jnp.zeros((pad, gp.shape[1]), gp.dtype))
    return jnp.concatenate(pieces, axis=0)


def _pack_small(vals):
    width = max(vals[k].shape[1] for k in SMALL)
    return jnp.concatenate([jnp.pad(vals[k].astype(F32), ((0, 0), (0, width - vals[k].shape[1]))) for k in SMALL], axis=0)


def _unpack_small(packed, like):
    return {k: packed[i:i + 1, :like[k].shape[1]] for i, k in enumerate(SMALL)}


def kernel(x, mem, g_mix_norm, g_mem_norm, w_in, b_forget, g_fox_q, g_fox_k, g_mem_q, g_mem_k, w_mem_kv, w_branch_sb, w_branch_fox, w_branch_mem, w_out, g_mlp_norm, w_ff_up, w_ff_down, loss_target, m_g_mix_norm, m_g_mem_norm, m_w_in, m_b_forget, m_g_fox_q, m_g_fox_k, m_g_mem_q, m_g_mem_k, m_w_mem_kv, m_w_branch_sb, m_w_branch_fox, m_w_branch_mem, m_w_out, m_g_mlp_norm, m_w_ff_up, m_w_ff_down, v_g_mix_norm, v_g_mem_norm, v_w_in, v_b_forget, v_g_fox_q, v_g_fox_k, v_g_mem_q, v_g_mem_k, v_w_mem_kv, v_w_branch_sb, v_w_branch_fox, v_w_branch_mem, v_w_out, v_g_mlp_norm, v_w_ff_up, v_w_ff_down):
    given = dict(locals())
    D = x.shape[-1]
    weights = {k: given[k] for k in ORDER}
    moms = {k: given["m_" + k] for k in ORDER}
    vars_ = {k: given["v_" + k] for k in ORDER}

    me_chip = 2 * lax.axis_index("x") + lax.axis_index("y")

    n_in = w_in.shape[2]
    in_pad = -n_in % IN_SHARD_PAD
    transposed = lambda t: jnp.transpose(t[0])
    shards = {}
    for k in BIG:
        w = weights[k][0].astype(BF16)
        if k == "w_in":
            w = jnp.pad(jnp.transpose(w), ((0, in_pad), (0, 0)))
        shards[k] = w.reshape(2, w.shape[0] // 2, w.shape[1])
    gathered_in = _gather_weights([shards["w_in"]], name="gather_w_in")[0]
    W = {"w_in": _to_padded_rows(gathered_in.reshape(N_CHIPS * (n_in + in_pad), D), D, n_in, in_pad)}
    rest = [k for k in BIG if k != "w_in"]

    def finish_weights(landed):
        full = _forward_halves(landed, name="forward_halves")
        full = [lax.dynamic_update_index_in_dim(o, shards[k], me_chip, 0) for k, o in zip(rest, full)]
        return {k: _unshard(k, g) for k, g in zip(rest, full)}

    def sum_pairs(names, own, sib):
        add2 = lambda p, q: (p.astype(F32) + q.astype(F32),)
        out = []
        for k, p, q in zip(names, own, sib):
            n, r, c = p.shape
            out.append(_ew(add2, [p.reshape(n * r, c), q.reshape(n * r, c)], (BF16,), name="sum_pair_" + k).reshape(n, r, c))
        return out

    def sum_chips(names, parts, got):
        add4 = lambda p, q0, q1, q2: (((p.astype(F32) + q0.astype(F32)) + q1.astype(F32)) + q2.astype(F32),)
        own = [lax.dynamic_index_in_dim(p, me_chip, 0, keepdims=False) for p in parts]
        return {k: _ew(add4, [p, q[0], q[1], q[2]], (F32,), name="sum_chips_" + k) for k, p, q in zip(names, own, got)}

    def reducer(tag):
        def reduce(grads):
            names = list(grads)
            stacked = {k: _reshard(k, grads[k]) for k in names if k != "w_in"}
            if "w_in" in grads:
                t = _from_padded_rows(grads["w_in"], D, n_in, in_pad, N_CHIPS)
                stacked["w_in"] = t.reshape(N_CHIPS, 2, (n_in + in_pad) // 2, D)
            own, sib = _exchange_siblings([stacked[k] for k in names], name="exchange_siblings_" + tag)
            parts = sum_pairs(names, own, sib)
            return _ChipExchange("scatter", parts), functools.partial(sum_chips, names, parts)
        return reduce

    small = {k: weights[k] for k in SMALL}
    loss_part, grad_x, G, small_grads, halves = _local_step(
        x[0], mem[0], loss_target[0], small, W,
        gather_rest=(_ChipExchange("gather", [shards[k] for k in rest]), finish_weights),
        reduce_early=reducer("early"), reduce_late=reducer("late"))
    assert not G, list(G)
    reduced, small_parts = _share_halves([halves[k] for k in BIG], _pack_small(small_grads))

    grads, deltas, new_m, new_v = {}, {}, {}, {}
    for k, g in zip(BIG, reduced):
        shp = weights[k].shape
        if k == "w_in":
            g2 = g.reshape(n_in + in_pad, D)
            padded = lambda t: jnp.pad(transposed(t), ((0, in_pad), (0, 0)))
            outs = _adamw(padded(weights[k]), g2, padded(moms[k]), padded(vars_[k]), name="adamw_" + k)
            g2, d, m2, v2 = [jnp.transpose(t[:n_in]) for t in (g2, *outs)]
        else:
            g2 = g.reshape(shp[1], shp[2])
            d, m2, v2 = _adamw(weights[k][0], g2, moms[k][0], vars_[k][0], name="adamw_" + k)
        grads[k], deltas[k], new_m[k], new_v[k] = g2.reshape(shp), d.reshape(shp), m2.reshape(shp), v2.reshape(shp)
    sg, sd, sm, sv = _adamw_small(_pack_small(small), small_parts, _pack_small({k: moms[k] for k in SMALL}),
                                  _pack_small({k: vars_[k] for k in SMALL}), name="adamw_small")
    for dst, packed in ((grads, sg), (deltas, sd), (new_m, sm), (new_v, sv)):
        dst.update(_unpack_small(packed, small))

    loss = lax.psum(loss_part, ("x", "y", "c"))
    return (loss, grad_x[None], *[grads[k] for k in ORDER], *[deltas[k] for k in ORDER],
            *[new_m[k] for k in ORDER], *[new_v[k] for k in ORDER])
```

```python
import functools

import jax
import jax.numpy as jnp
from jax import lax
from jax.experimental import pallas as pl
from jax.experimental.pallas import tpu as pltpu

F32 = jnp.float32
BF16 = jnp.bfloat16
MESH_ID = pl.DeviceIdType.MESH

HD = 64
SB_HEADS = 8
FOX_HEADS = 8
MEM_HEADS = 4
MEM_HD = 128
D_SB = SB_HEADS * HD
D_FOX = FOX_HEADS * HD
D_MEM = MEM_HEADS * MEM_HD
EPS = 1e-6
NEG_INF = -1e30

ADAM_LR = 0.001
ADAM_B1 = 0.9
ADAM_B2 = 0.999
ADAM_EPS = 1e-08
ADAM_WD = 0.01
ADAM_STEP = 10

N_CHIPS = 4
VMEM_LIMIT = 56 * 1024 * 1024

F_PAD = 256


def _tile(n, target, align=128):
    if n <= target:
        return n
    best = None
    t = align
    while t <= target:
        if n % t == 0:
            best = t
        t += align
    assert best is not None, (n, target, align)
    return best


def _params(sem):
    return pltpu.CompilerParams(dimension_semantics=sem, vmem_limit_bytes=VMEM_LIMIT)


def _mm(a, b, *, name, ta=False, tb=False, out_dtypes=(F32,), epilogue=None, extras=(),
        tm=1024, tn=1024, tk=2048, comm=None):
    if ta:
        K, M = a.shape
    else:
        M, K = a.shape
    if tb:
        N, K2 = b.shape
    else:
        K2, N = b.shape
    assert K == K2, (a.shape, b.shape, ta, tb)
    tm, tn, tk = _tile(M, tm), _tile(N, tn), _tile(K, tk)
    nk = K // tk
    n_extra, n_out = len(extras), len(out_dtypes)
    if epilogue is None:
        epilogue = lambda acc: (acc,)
    dims = (((0 if ta else 1,), (1 if tb else 0,)), ((), ()))

    gm, gn = M // tm, N // tn

    def body(*refs):
        i, j, k = pl.program_id(0), pl.program_id(1), pl.program_id(2)
        got = _carry(comm, 2 + n_extra, n_out, (i == 0) & (j == 0) & (k == 0),
                     (i == gm - 1) & (j == gn - 1) & (k == nk - 1), refs)
        (a_ref, b_ref, *extra_refs), out_refs = got[0], got[1]
        part = lax.dot_general(a_ref[...].astype(BF16), b_ref[...].astype(BF16), dims, preferred_element_type=F32)

        def finish(acc):
            outs = epilogue(acc, *[r[...] for r in extra_refs])
            for o_ref, o in zip(out_refs, outs):
                o_ref[...] = o.astype(o_ref.dtype)

        if nk == 1:
            finish(part)
        else:
            acc_ref = refs[-1]

            @pl.when(k == 0)
            def _():
                acc_ref[...] = part

            @pl.when((k > 0) & (k < nk - 1))
            def _():
                acc_ref[...] += part

            @pl.when(k == nk - 1)
            def _():
                finish(acc_ref[...] + part)

        got[2]()

    a_spec = pl.BlockSpec((tk, tm), lambda i, j, k: (k, i)) if ta else pl.BlockSpec((tm, tk), lambda i, j, k: (i, k))
    b_spec = pl.BlockSpec((tn, tk), lambda i, j, k: (j, k)) if tb else pl.BlockSpec((tk, tn), lambda i, j, k: (k, j))
    mn_spec = pl.BlockSpec((tm, tn), lambda i, j, k: (i, j))
    c_ins, c_in_specs, c_out_specs, c_out_shape, c_scratch = _comm_args(comm)
    sem = ("parallel", "parallel", "arbitrary") if comm is None else ("arbitrary",) * 3
    outs = pl.pallas_call(
        body, name=name,
        grid=(gm, gn, nk),
        in_specs=[a_spec, b_spec] + [mn_spec] * n_extra + c_in_specs,
        out_specs=[mn_spec] * n_out + c_out_specs,
        out_shape=[jax.ShapeDtypeStruct((M, N), dt) for dt in out_dtypes] + c_out_shape,
        scratch_shapes=c_scratch + ([pltpu.VMEM((tm, tn), F32)] if nk > 1 else []),
        compiler_params=_params(sem),
    )(a, b, *extras, *c_ins)
    return outs if len(outs) > 1 else outs[0]


def _row_tile(rows, cols, n_arrays):
    budget = 10 * 1024 * 1024
    cols_padded = -(-cols // 128) * 128
    target = max(16, budget // (cols_padded * 4 * n_arrays * 2))
    return _tile(rows, target, align=16)


def _ew(fn, ins, out_dtypes, *, name):
    R, C = ins[0].shape
    n_in, n_out = len(ins), len(out_dtypes)
    tr = _row_tile(R, C, n_in + n_out)

    def body(*refs):
        outs = fn(*[r[...] for r in refs[:n_in]])
        for o_ref, o in zip(refs[n_in:], outs):
            o_ref[...] = o.astype(o_ref.dtype)

    spec = pl.BlockSpec((tr, C), lambda i: (i, 0))
    outs = pl.pallas_call(
        body, name=name, grid=(R // tr,),
        in_specs=[spec] * n_in, out_specs=[spec] * n_out,
        out_shape=[jax.ShapeDtypeStruct((R, C), dt) for dt in out_dtypes],
        compiler_params=_params(("parallel",)),
    )(*ins)
    return outs if n_out > 1 else outs[0]


def _rmsnorm_fwd(x, g, out_dtype, *, name):
    R, d = x.shape
    tr = _row_tile(R, d, 3)

    def body(x_ref, g_ref, o_ref):
        xv = x_ref[...].astype(F32)
        r = lax.rsqrt(jnp.mean(xv * xv, axis=-1, keepdims=True) + EPS)
        o_ref[...] = (xv * r * g_ref[...]).astype(o_ref.dtype)

    return pl.pallas_call(
        body, name=name, grid=(R // tr,),
        in_specs=[pl.BlockSpec((tr, d), lambda i: (i, 0)), pl.BlockSpec((1, d), lambda i: (0, 0))],
        out_specs=pl.BlockSpec((tr, d), lambda i: (i, 0)),
        out_shape=jax.ShapeDtypeStruct((R, d), out_dtype),
        compiler_params=_params(("parallel",)),
    )(x, g)


def _rmsnorm_bwd(x, g, dy, add=None, *, name):
    R, d = x.shape
    has_add = add is not None
    tr = _row_tile(R, d, 5)

    def body(*refs):
        x_ref, g_ref, dy_ref = refs[:3]
        add_ref = refs[3] if has_add else None
        dx_ref, dg_ref = refs[-2:]
        xv = x_ref[...].astype(F32)
        dyv = dy_ref[...].astype(F32)
        r = lax.rsqrt(jnp.mean(xv * xv, axis=-1, keepdims=True) + EPS)
        xh = xv * r
        dyg = dyv * g_ref[...]
        c = jnp.mean(dyg * xh, axis=-1, keepdims=True)
        dx = r * (dyg - xh * c)
        if has_add:
            dx = dx + add_ref[...]
        dx_ref[...] = dx

        @pl.when(pl.program_id(0) == 0)
        def _():
            dg_ref[...] = jnp.zeros_like(dg_ref)

        dg_ref[...] += jnp.sum(dyv * xh, axis=0, keepdims=True)

    row = pl.BlockSpec((tr, d), lambda i: (i, 0))
    vec = pl.BlockSpec((1, d), lambda i: (0, 0))
    ins = [x, g, dy] + ([add] if has_add else [])
    return pl.pallas_call(
        body, name=name, grid=(R // tr,),
        in_specs=[row, vec, row] + ([row] if has_add else []),
        out_specs=[row, vec],
        out_shape=[jax.ShapeDtypeStruct((R, d), F32), jax.ShapeDtypeStruct((1, d), F32)],
        compiler_params=_params(("arbitrary",)),
    )(*ins)


_NT = (((1,), (1,)), ((), ()))
_TN = (((0,), (0,)), ((), ()))


def _dot(a, b, dims=(((1,), (0,)), ((), ()))):
    return lax.dot_general(a, b, dims, preferred_element_type=F32)


def _split_dot(x, tri):
    hi = x.astype(BF16)
    lo = (x - hi.astype(F32)).astype(BF16)
    return _dot(hi, tri) + _dot(lo, tri)


def _log_sigmoid_pair(z):
    sp = jnp.log(1.0 + jnp.exp(-jnp.abs(z)))
    return jnp.minimum(z, 0.0) - sp, jnp.minimum(-z, 0.0) - sp


def _sb_fwd(q, k, v, *, name, tq=256):
    H, S, hd = q.shape
    tq = _tile(S, tq)
    tk = tq
    scale = hd ** -0.5

    def body(q_ref, k_ref, v_ref, o_ref, tot_ref):
        qi = pl.program_id(1)
        qv = q_ref[0]
        row = qi * tq + lax.broadcasted_iota(jnp.int32, (tq, tk), 0)
        col0 = lax.broadcasted_iota(jnp.int32, (tq, tk), 1)
        later = (lax.broadcasted_iota(jnp.int32, (tk, tk), 0) > lax.broadcasted_iota(jnp.int32, (tk, tk), 1)).astype(BF16)

        def step(i, carry):
            acc, c_rem = carry
            kb = qi - i
            ks = pl.multiple_of(kb * tk, tk)
            kv = k_ref[0, pl.ds(ks, tk), :]
            vv = v_ref[0, pl.ds(ks, tk), :]
            z = _dot(qv, kv, _NT) * scale
            mask = (col0 + ks) < row
            ls, lsn = _log_sigmoid_pair(z)
            rem = jnp.where(mask, lsn, 0.0)
            after = _split_dot(rem, later) + c_rem
            w = jnp.where(mask, jnp.exp(ls + after), 0.0)
            acc = acc + _dot(w.astype(BF16), vv)
            c_rem = c_rem + jnp.sum(rem, axis=1, keepdims=True)
            return acc, c_rem

        acc, c_rem = lax.fori_loop(0, qi + 1, step, (jnp.zeros((tq, hd), F32), jnp.zeros((tq, 1), F32)))
        o_ref[0] = acc
        tot_ref[0] = c_rem

    qspec = pl.BlockSpec((1, tq, hd), lambda h, i: (h, i, 0))
    kspec = pl.BlockSpec((1, S, hd), lambda h, i: (h, 0, 0))
    return pl.pallas_call(
        body, name=name, grid=(H, S // tq),
        in_specs=[qspec, kspec, kspec],
        out_specs=[qspec, pl.BlockSpec((1, tq, 1), lambda h, i: (h, i, 0))],
        out_shape=[jax.ShapeDtypeStruct((H, S, hd), F32), jax.ShapeDtypeStruct((H, S, 1), F32)],
        compiler_params=_params(("parallel", "arbitrary")),
    )(q, k, v)


def _sb_bwd(q, k, v, do, tot, *, name, tq=256):
    H, S, hd = q.shape
    tq = _tile(S, tq)
    tk = tq
    scale = hd ** -0.5

    def body(q_ref, k_ref, v_ref, do_ref, tot_ref, dq_ref, dk_ref, dv_ref):
        qi = pl.program_id(1)

        @pl.when(qi == 0)
        def _():
            dk_ref[...] = jnp.zeros_like(dk_ref)
            dv_ref[...] = jnp.zeros_like(dv_ref)

        qv = q_ref[0]
        dov = do_ref[0].astype(BF16)
        tot_v = tot_ref[0]
        row = qi * tq + lax.broadcasted_iota(jnp.int32, (tq, tk), 0)
        col0 = lax.broadcasted_iota(jnp.int32, (tq, tk), 1)
        r_i = lax.broadcasted_iota(jnp.int32, (tk, tk), 0)
        c_i = lax.broadcasted_iota(jnp.int32, (tk, tk), 1)
        upto = (r_i <= c_i).astype(BF16)
        before = (r_i < c_i).astype(BF16)

        def step(kb, carry):
            dq, c_rem, c_g = carry
            ks = pl.multiple_of(kb * tk, tk)
            kv = k_ref[0, pl.ds(ks, tk), :]
            vv = v_ref[0, pl.ds(ks, tk), :]
            z = _dot(qv, kv, _NT) * scale
            mask = (col0 + ks) < row
            ls, lsn = _log_sigmoid_pair(z)
            rem = jnp.where(mask, lsn, 0.0)
            after = tot_v - (_split_dot(rem, upto) + c_rem)
            w = jnp.where(mask, jnp.exp(ls + after), 0.0)
            dw = _dot(dov, vv, _NT)
            g = dw * w
            g_before = _split_dot(g, before) + c_g
            beta = jnp.exp(ls)
            dz = jnp.where(mask, g * (1.0 - beta) - beta * g_before, 0.0) * scale
            dzb = dz.astype(BF16)
            dq = dq + _dot(dzb, kv)
            dk_ref[0, pl.ds(ks, tk), :] += _dot(dzb, qv, _TN)
            dv_ref[0, pl.ds(ks, tk), :] += _dot(w.astype(BF16), dov, _TN)
            return dq, c_rem + jnp.sum(rem, axis=1, keepdims=True), c_g + jnp.sum(g, axis=1, keepdims=True)

        zero = jnp.zeros((tq, 1), F32)
        dq, _, _ = lax.fori_loop(0, qi + 1, step, (jnp.zeros((tq, hd), F32), zero, zero))
        dq_ref[0] = dq

    qspec = pl.BlockSpec((1, tq, hd), lambda h, i: (h, i, 0))
    kspec = pl.BlockSpec((1, S, hd), lambda h, i: (h, 0, 0))
    full = jax.ShapeDtypeStruct((H, S, hd), F32)
    return pl.pallas_call(
        body, name=name, grid=(H, S // tq),
        in_specs=[qspec, kspec, kspec, qspec, pl.BlockSpec((1, tq, 1), lambda h, i: (h, i, 0))],
        out_specs=[qspec, kspec, kspec],
        out_shape=[full, full, full],
        compiler_params=_params(("parallel", "arbitrary")),
    )(q, k, v, do, tot)


def _softmax_fwd(q, k, v, bias=None, *, name, causal, tq=256, tk=256):
    H, S, hd = q.shape
    Sk = k.shape[1]
    tq, tk = _tile(S, tq), _tile(Sk, tk)
    if causal:
        assert tq == tk and S == Sk
    nk = Sk // tk
    scale = hd ** -0.5
    has_bias = bias is not None

    def body(*refs):
        q_ref, k_ref, v_ref = refs[:3]
        o_ref, lse_ref = refs[-2:]
        qi = pl.program_id(1)
        qv = q_ref[0]
        row = qi * tq + lax.broadcasted_iota(jnp.int32, (tq, tk), 0)
        col0 = lax.broadcasted_iota(jnp.int32, (tq, tk), 1)
        if has_bias:
            b_row = refs[3][0]

        def step(kb, carry):
            m, l, acc = carry
            ks = pl.multiple_of(kb * tk, tk)
            kv = k_ref[0, pl.ds(ks, tk), :]
            vv = v_ref[0, pl.ds(ks, tk), :]
            z = _dot(qv, kv, _NT) * scale
            if has_bias:
                z = z + b_row - refs[4][0, kb]
            if causal:
                z = jnp.where((col0 + ks) <= row, z, NEG_INF)
            m2 = jnp.maximum(m, jnp.max(z, axis=1, keepdims=True))
            p = jnp.exp(z - m2)
            alpha = jnp.exp(m - m2)
            l = alpha * l + jnp.sum(p, axis=1, keepdims=True)
            acc = alpha * acc + _dot(p.astype(BF16), vv)
            return m2, l, acc

        init = (jnp.full((tq, 1), NEG_INF, F32), jnp.zeros((tq, 1), F32), jnp.zeros((tq, hd), F32))
        m, l, acc = lax.fori_loop(0, (qi + 1) if causal else nk, step, init)
        o_ref[0] = acc / l
        lse_ref[0] = m + jnp.log(l)

    qspec = pl.BlockSpec((1, tq, hd), lambda h, i: (h, i, 0))
    kspec = pl.BlockSpec((1, Sk, hd), lambda h, i: (h, 0, 0))
    vspec = pl.BlockSpec((1, tq, 1), lambda h, i: (h, i, 0))
    in_specs = [qspec, kspec, kspec]
    ins = [q, k, v]
    if has_bias:
        in_specs += [vspec, pl.BlockSpec((1, nk, 1, tk), lambda h, i: (h, 0, 0, 0))]
        ins += list(bias)
    return pl.pallas_call(
        body, name=name, grid=(H, S // tq),
        in_specs=in_specs, out_specs=[qspec, vspec],
        out_shape=[jax.ShapeDtypeStruct((H, S, hd), F32), jax.ShapeDtypeStruct((H, S, 1), F32)],
        compiler_params=_params(("parallel", "arbitrary")),
    )(*ins)


def _softmax_bwd(q, k, v, o, lse, do, bias=None, *, name, causal, tq=256, tk=256):
    H, S, hd = q.shape
    Sk = k.shape[1]
    tq, tk = _tile(S, tq), _tile(Sk, tk)
    nk = Sk // tk
    scale = hd ** -0.5
    has_bias = bias is not None
    n_in = 8 if has_bias else 6

    def body(*refs):
        q_ref, k_ref, v_ref, o_ref, lse_ref, do_ref = refs[:6]
        dq_ref, dk_ref, dv_ref = refs[n_in:n_in + 3]
        qi = pl.program_id(1)

        @pl.when(qi == 0)
        def _():
            dk_ref[...] = jnp.zeros_like(dk_ref)
            dv_ref[...] = jnp.zeros_like(dv_ref)
            if has_bias:
                refs[n_in + 4][...] = jnp.zeros_like(refs[n_in + 4])

        qv = q_ref[0]
        do32 = do_ref[0]
        dov = do32.astype(BF16)
        delta = jnp.sum(do32 * o_ref[0], axis=1, keepdims=True)
        lse_v = lse_ref[0]
        row = qi * tq + lax.broadcasted_iota(jnp.int32, (tq, tk), 0)
        col0 = lax.broadcasted_iota(jnp.int32, (tq, tk), 1)
        if has_bias:
            b_row = refs[6][0]

        def step(kb, carry):
            dq, db_row = carry
            ks = pl.multiple_of(kb * tk, tk)
            kv = k_ref[0, pl.ds(ks, tk), :]
            vv = v_ref[0, pl.ds(ks, tk), :]
            z = _dot(qv, kv, _NT) * scale
            if has_bias:
                z = z + b_row - refs[7][0, kb]
            p = jnp.exp(z - lse_v)
            if causal:
                p = jnp.where((col0 + ks) <= row, p, 0.0)
            dp = _dot(dov, vv, _NT)
            dz = p * (dp - delta)
            dzb = dz.astype(BF16)
            dq = dq + _dot(dzb, kv)
            dk_ref[0, pl.ds(ks, tk), :] += _dot(dzb, qv, _TN) * scale
            dv_ref[0, pl.ds(ks, tk), :] += _dot(p.astype(BF16), dov, _TN)
            if has_bias:
                db_row = db_row + jnp.sum(dz, axis=1, keepdims=True)
                refs[n_in + 4][0, kb] += jnp.sum(dz, axis=0, keepdims=True)
            return dq, db_row

        dq, db_row = lax.fori_loop(0, (qi + 1) if causal else nk, step,
                                   (jnp.zeros((tq, hd), F32), jnp.zeros((tq, 1), F32)))
        dq_ref[0] = dq * scale
        if has_bias:
            refs[n_in + 3][0] = db_row

    qspec = pl.BlockSpec((1, tq, hd), lambda h, i: (h, i, 0))
    kspec = pl.BlockSpec((1, Sk, hd), lambda h, i: (h, 0, 0))
    vspec = pl.BlockSpec((1, tq, 1), lambda h, i: (h, i, 0))
    cspec = pl.BlockSpec((1, nk, 1, tk), lambda h, i: (h, 0, 0, 0))
    in_specs = [qspec, kspec, kspec, qspec, vspec, qspec]
    ins = [q, k, v, o, lse, do]
    out_specs = [qspec, kspec, kspec]
    out_shape = [jax.ShapeDtypeStruct((H, S, hd), F32), jax.ShapeDtypeStruct((H, Sk, hd), F32),
                 jax.ShapeDtypeStruct((H, Sk, hd), F32)]
    if has_bias:
        in_specs += [vspec, cspec]
        ins += list(bias)
        out_specs += [vspec, cspec]
        out_shape += [jax.ShapeDtypeStruct((H, S, 1), F32), jax.ShapeDtypeStruct((H, nk, 1, tk), F32)]
    return pl.pallas_call(
        body, name=name, grid=(H, S // tq),
        in_specs=in_specs, out_specs=out_specs, out_shape=out_shape,
        compiler_params=_params(("parallel", "arbitrary")),
    )(*ins)


LANES = 128
_LOW = -3e38


def _lane_masks(hd, rows):
    if hd == LANES:
        return [None]
    lane = lax.broadcasted_iota(jnp.int32, (rows, LANES), 1)
    return [(lane >= hh * hd) & (lane < (hh + 1) * hd) for hh in range(LANES // hd)]


def _keep(t, m):
    return t if m is None else jnp.where(m, t, 0.0)


def _merge(parts, masks):
    out = parts[-1]
    for p, m in zip(parts[-2::-1], masks[-2::-1]):
        out = jnp.where(m, p, out)
    return out


def _row_value(t, m):
    return jnp.max(t if m is None else jnp.where(m, t, _LOW), axis=1, keepdims=True)


def _cols(tq, off):
    return pl.BlockSpec((tq, LANES), lambda g, i: (i, off + g))


def _cols_all(rows, off):
    return pl.BlockSpec((rows, LANES), lambda g, i: (0, off + g))


SCAN_BLOCK = 256


def _tri(kind, cols):
    n = min(SCAN_BLOCK, cols)
    r = lax.broadcasted_iota(jnp.int32, (n, n), 0)
    c = lax.broadcasted_iota(jnp.int32, (n, n), 1)
    return ((r > c) if kind == "after" else (r < c)).astype(BF16)


def _scan_cols(x, tri, reverse):
    cols = x.shape[1]
    cb = min(SCAN_BLOCK, cols)
    assert cols % cb == 0 and tri.shape == (cb, cb)
    nb = cols // cb
    blocks = [x[:, b * cb:(b + 1) * cb] for b in range(nb)]
    outs, carry = [None] * nb, None
    for b in (reversed(range(nb)) if reverse else range(nb)):
        y = _dot(blocks[b].astype(BF16), tri)
        outs[b] = y if carry is None else y + carry
        s = jnp.sum(blocks[b], axis=1, keepdims=True)
        carry = s if carry is None else carry + s
    return (outs[0] if nb == 1 else jnp.concatenate(outs, axis=1)), carry


def _softplus_parts(z):
    pos = jnp.maximum(z, 0.0) + jnp.log(1.0 + jnp.exp(-jnp.abs(z)))
    return pos, z - pos


class _ChipExchange:
    def __init__(self, kind, ins):
        assert kind in ("gather", "scatter")
        self.kind, self.ins = kind, list(ins)
        lead = (lambda s: (N_CHIPS,) + s) if kind == "gather" else (lambda s: (3,) + s[1:])
        self.out_shape = [jax.ShapeDtypeStruct(lead(a.shape), a.dtype) for a in ins]
        n = 3 * len(ins)
        self.scratch = [pltpu.SemaphoreType.DMA((n,)), pltpu.SemaphoreType.DMA((n,))]

    def _copies(self, in_refs, out_refs, sems, landing):
        x, y, c, chips = _mesh_place()
        me = 2 * x + y
        out = []
        for w in range(len(self.ins)):
            for j, chip in enumerate(chips):
                peer = 2 * chip[0] + chip[1]
                if self.kind == "gather":
                    src, dst, land = in_refs[w].at[c], out_refs[w].at[me, c], out_refs[w].at[peer, c]
                else:
                    src, dst, land = in_refs[w].at[peer], out_refs[w].at[j], out_refs[w].at[j]
                if landing:
                    src, dst = land, land
                out.append(_remote(src, dst, sems, 3 * w + j, (chip[0], chip[1], c)))
        return out

    def start(self, in_refs, out_refs, sems):
        for cp in self._copies(in_refs, out_refs, sems, False):
            cp.start()

    def finish(self, in_refs, out_refs, sems):
        for cp in self._copies(in_refs, out_refs, sems, True):
            cp.wait_recv()
        for cp in self._copies(in_refs, out_refs, sems, False):
            cp.wait_send()


def _carry(comm, n_in, n_out, first, last, refs):
    if comm is None:
        return refs[:n_in], refs[n_in:n_in + n_out], (lambda: None)
    a, b = len(comm.ins), len(comm.out_shape)
    ins, c_in = refs[:n_in], refs[n_in:n_in + a]
    outs, c_out = refs[n_in + a:n_in + a + n_out], refs[n_in + a + n_out:n_in + a + n_out + b]
    sems = refs[n_in + a + n_out + b:n_in + a + n_out + b + 2]
    pl.when(first)(lambda: comm.start(c_in, c_out, sems))
    return ins, outs, (lambda: pl.when(last)(lambda: comm.finish(c_in, c_out, sems)))


def _sbl_fwd(q, k, v, *, width, hd, name, tq=256, comm=None):
    (qa, qo), (ka, ko), (va, vo) = q, k, v
    S = qa.shape[0]
    tq = _tile(S, tq)
    tk = tq
    scale = hd ** -0.5
    n_g, n_q = width // LANES, S // tq

    def body(*refs):
        qi = pl.program_id(1)
        gi = pl.program_id(0)
        got = _carry(comm, 3, 2, (gi == 0) & (qi == 0), (gi == n_g - 1) & (qi == n_q - 1), refs)
        (q_ref, k_ref, v_ref), (o_ref, tot_ref) = got[0], got[1]
        masks = _lane_masks(hd, tq)
        qs = q_ref[...].astype(F32) * scale
        qm = [_keep(qs, m).astype(BF16) for m in masks]
        strict = lax.broadcasted_iota(jnp.int32, (tq, tk), 1) < lax.broadcasted_iota(jnp.int32, (tq, tk), 0)
        later = _tri("after", tk)

        def tile(kb, carry, diag):
            ks = pl.multiple_of(kb * tk, tk)
            kv = k_ref[pl.ds(ks, tk), :].astype(BF16)
            vv = v_ref[pl.ds(ks, tk), :].astype(BF16)
            out = []
            for hh in range(len(masks)):
                acc, c_pos = carry[2 * hh], carry[2 * hh + 1]
                pos, ls = _softplus_parts(_dot(qm[hh], kv, _NT))
                if diag:
                    pos = jnp.where(strict, pos, 0.0)
                pos_after, pos_all = _scan_cols(pos, later, True)
                w = jnp.exp(ls - (pos_after + c_pos))
                if diag:
                    w = jnp.where(strict, w, 0.0)
                out += [acc + _dot(w.astype(BF16), vv), c_pos + pos_all]
            return tuple(out)

        init = (jnp.zeros((tq, LANES), F32), jnp.zeros((tq, 1), F32)) * len(masks)
        carry = tile(qi, init, True)
        carry = lax.fori_loop(0, qi, lambda i, c: tile(qi - 1 - i, c, False), carry)
        o_ref[...] = _merge(carry[0::2], masks).astype(o_ref.dtype)
        tot_ref[...] = _merge([jnp.broadcast_to(-c, (tq, LANES)) for c in carry[1::2]], masks)
        got[2]()

    c_ins, c_in_specs, c_out_specs, c_out_shape, c_scratch = _comm_args(comm)
    return pl.pallas_call(
        body, name=name, grid=(n_g, n_q),
        in_specs=[_cols(tq, qo), _cols_all(S, ko), _cols_all(S, vo)] + c_in_specs,
        out_specs=[_cols(tq, 0), _cols(tq, 0)] + c_out_specs,
        out_shape=[jax.ShapeDtypeStruct((S, width), BF16), jax.ShapeDtypeStruct((S, width), F32)] + c_out_shape,
        scratch_shapes=c_scratch,
        compiler_params=_params(("arbitrary", "arbitrary")),
    )(qa, ka, va, *c_ins)


def _comm_args(comm):
    if comm is None:
        return [], [], [], [], []
    return comm.ins, [_ANY] * len(comm.ins), [_ANY] * len(comm.out_shape), comm.out_shape, comm.scratch


def _sbl_bwd(q, k, v, do, tot, *, width, hd, name, tq=256, comm=None):
    (qa, qo), (ka, ko), (va, vo) = q, k, v
    S = qa.shape[0]
    tq = _tile(S, tq)
    tk = tq
    scale = hd ** -0.5
    n_g, n_q = width // LANES, S // tq

    def body(*refs):
        qi = pl.program_id(1)
        gi = pl.program_id(0)
        got = _carry(comm, 5, 3, (gi == 0) & (qi == 0), (gi == n_g - 1) & (qi == n_q - 1), refs)
        (q_ref, k_ref, v_ref, do_ref, tot_ref), (dq_ref, dk_ref, dv_ref) = got[0], got[1]

        @pl.when(qi == 0)
        def _():
            dk_ref[...] = jnp.zeros_like(dk_ref)
            dv_ref[...] = jnp.zeros_like(dv_ref)

        masks = _lane_masks(hd, tq)
        qs = q_ref[...].astype(F32) * scale
        qm = [_keep(qs, m).astype(BF16) for m in masks]
        dov = [_keep(do_ref[...], m).astype(BF16) for m in masks]
        rest = [-_row_value(tot_ref[...], m) for m in masks]
        strict = lax.broadcasted_iota(jnp.int32, (tq, tk), 1) < lax.broadcasted_iota(jnp.int32, (tq, tk), 0)
        later, before = _tri("after", tk), _tri("before", tk)

        def tile(kb, carry, diag):
            ks = pl.multiple_of(kb * tk, tk)
            kv = k_ref[pl.ds(ks, tk), :].astype(BF16)
            vv = v_ref[pl.ds(ks, tk), :].astype(BF16)
            out = []
            dk_t, dv_t = None, None
            for hh in range(len(masks)):
                dq, c_pos, c_g = carry[3 * hh:3 * hh + 3]
                pos, ls = _softplus_parts(_dot(qm[hh], kv, _NT))
                if diag:
                    pos = jnp.where(strict, pos, 0.0)
                pos_after, pos_all = _scan_cols(pos, later, True)
                c_pos = c_pos + pos_all
                w = jnp.exp(ls - (pos_after + (rest[hh] - c_pos)))
                if diag:
                    w = jnp.where(strict, w, 0.0)
                g = _dot(dov[hh], vv, _NT) * w
                g_before, g_all = _scan_cols(g, before, False)
                g_before = g_before + c_g
                dz = g - jnp.exp(ls) * (g + g_before)
                if diag:
                    dz = jnp.where(strict, dz, 0.0)
                dzb = dz.astype(BF16)
                dk_h = _dot(dzb, qm[hh], _TN)
                dv_h = _dot(w.astype(BF16), dov[hh], _TN)
                dk_t = dk_h if dk_t is None else dk_t + dk_h
                dv_t = dv_h if dv_t is None else dv_t + dv_h
                out += [dq + _dot(dzb, kv), c_pos, c_g + g_all]
            dk_ref[pl.ds(ks, tk), :] += dk_t
            dv_ref[pl.ds(ks, tk), :] += dv_t
            return tuple(out)

        zero = jnp.zeros((tq, 1), F32)
        init = (jnp.zeros((tq, LANES), F32), zero, zero) * len(masks)
        carry = lax.fori_loop(0, qi, lambda kb, c: tile(kb, c, False), init)
        carry = tile(qi, carry, True)
        dq_ref[...] = _merge(carry[0::3], masks) * scale
        got[2]()

    full = jax.ShapeDtypeStruct((S, width), F32)
    c_ins, c_in_specs, c_out_specs, c_out_shape, c_scratch = _comm_args(comm)
    return pl.pallas_call(
        body, name=name, grid=(n_g, n_q),
        in_specs=[_cols(tq, qo), _cols_all(S, ko), _cols_all(S, vo), _cols(tq, do[1]), _cols(tq, tot[1])] + c_in_specs,
        out_specs=[_cols(tq, 0), _cols_all(S, 0), _cols_all(S, 0)] + c_out_specs,
        out_shape=[full, full, full] + c_out_shape,
        scratch_shapes=c_scratch,
        compiler_params=_params(("arbitrary", "arbitrary")),
    )(qa, ka, va, do[0], tot[0], *c_ins)


def _sml_fwd(q, k, v, bias=None, *, width, hd, causal, name, tq=256, tk=256):
    (qa, qo), (ka, ko), (va, vo) = q, k, v
    S, Sk = qa.shape[0], ka.shape[0]
    tq, tk = _tile(S, tq), _tile(Sk, tk)
    if causal:
        assert tq == tk and S == Sk
    nk = Sk // tk
    hpg = LANES // hd
    scale = hd ** -0.5
    has_bias = bias is not None

    def body(*refs):
        q_ref, k_ref, v_ref = refs[:3]
        o_ref, lse_ref = refs[-2:]
        qi = pl.program_id(1)
        masks = _lane_masks(hd, tq)
        qs = q_ref[...].astype(F32) * scale
        qm = [_keep(qs, m).astype(BF16) for m in masks]
        allowed = lax.broadcasted_iota(jnp.int32, (tq, tk), 1) <= lax.broadcasted_iota(jnp.int32, (tq, tk), 0)

        def tile(kb, carry, diag):
            ks = pl.multiple_of(kb * tk, tk)
            kv = k_ref[pl.ds(ks, tk), :].astype(BF16)
            vv = v_ref[pl.ds(ks, tk), :].astype(BF16)
            out = []
            for hh in range(hpg):
                m, l, acc = carry[3 * hh:3 * hh + 3]
                z = _dot(qm[hh], kv, _NT)
                if has_bias:
                    z = z + refs[3][hh] - refs[4][hh, kb]
                if diag:
                    z = jnp.where(allowed, z, NEG_INF)
                m2 = jnp.maximum(m, jnp.max(z, axis=1, keepdims=True))
                p = jnp.exp(z - m2)
                alpha = jnp.exp(m - m2)
                out += [m2, alpha * l + jnp.sum(p, axis=1, keepdims=True), alpha * acc + _dot(p.astype(BF16), vv)]
            return tuple(out)

        init = (jnp.full((tq, 1), NEG_INF, F32), jnp.zeros((tq, 1), F32), jnp.zeros((tq, LANES), F32)) * hpg
        if causal:
            carry = lax.fori_loop(0, qi, lambda kb, c: tile(kb, c, False), init)
            carry = tile(qi, carry, True)
        else:
            carry = lax.fori_loop(0, nk, lambda kb, c: tile(kb, c, False), init)
        o_ref[...] = _merge([acc / l for l, acc in zip(carry[1::3], carry[2::3])], masks).astype(o_ref.dtype)
        lse_ref[...] = _merge([jnp.broadcast_to(m + jnp.log(l), (tq, LANES)) for m, l in zip(carry[0::3], carry[1::3])], masks)

    in_specs = [_cols(tq, qo), _cols_all(Sk, ko), _cols_all(Sk, vo)]
    ins = [qa, ka, va]
    if has_bias:
        in_specs += [pl.BlockSpec((hpg, tq, 1), lambda g, i: (g, i, 0)),
                     pl.BlockSpec((hpg, nk, 1, tk), lambda g, i: (g, 0, 0, 0))]
        ins += list(bias)
    return pl.pallas_call(
        body, name=name, grid=(width // LANES, S // tq),
        in_specs=in_specs, out_specs=[_cols(tq, 0), _cols(tq, 0)],
        out_shape=[jax.ShapeDtypeStruct((S, width), BF16), jax.ShapeDtypeStruct((S, width), F32)],
        compiler_params=_params(("parallel", "arbitrary")),
    )(*ins)


def _sml_bwd(q, k, v, o, lse, do, bias=None, *, width, hd, causal, name, tq=256, tk=256):
    (qa, qo), (ka, ko), (va, vo) = q, k, v
    S, Sk = qa.shape[0], ka.shape[0]
    tq, tk = _tile(S, tq), _tile(Sk, tk)
    nk = Sk // tk
    hpg = LANES // hd
    scale = hd ** -0.5
    has_bias = bias is not None
    n_in = 8 if has_bias else 6

    def body(*refs):
        q_ref, k_ref, v_ref, o_ref, lse_ref, do_ref = refs[:6]
        dq_ref, dk_ref, dv_ref = refs[n_in:n_in + 3]
        qi = pl.program_id(1)

        @pl.when(qi == 0)
        def _():
            dk_ref[...] = jnp.zeros_like(dk_ref)
            dv_ref[...] = jnp.zeros_like(dv_ref)
            if has_bias:
                refs[n_in + 4][...] = jnp.zeros_like(refs[n_in + 4])

        masks = _lane_masks(hd, tq)
        qs = q_ref[...].astype(F32) * scale
        qm = [_keep(qs, m).astype(BF16) for m in masks]
        do32 = do_ref[...]
        dov = [_keep(do32, m).astype(BF16) for m in masks]
        prod = do32 * o_ref[...].astype(F32)
        delta = [jnp.sum(_keep(prod, m), axis=1, keepdims=True) for m in masks]
        lses = [_row_value(lse_ref[...], m) for m in masks]
        allowed = lax.broadcasted_iota(jnp.int32, (tq, tk), 1) <= lax.broadcasted_iota(jnp.int32, (tq, tk), 0)

        def tile(kb, carry, diag):
            ks = pl.multiple_of(kb * tk, tk)
            kv = k_ref[pl.ds(ks, tk), :].astype(BF16)
            vv = v_ref[pl.ds(ks, tk), :].astype(BF16)
            out = []
            dk_t, dv_t = None, None
            for hh in range(hpg):
                dq, db_row = carry[2 * hh:2 * hh + 2]
                z = _dot(qm[hh], kv, _NT)
                if has_bias:
                    z = z + refs[6][hh] - refs[7][hh, kb]
                p = jnp.exp(z - lses[hh])
                if diag:
                    p = jnp.where(allowed, p, 0.0)
                dz = p * (_dot(dov[hh], vv, _NT) - delta[hh])
                dzb = dz.astype(BF16)
                dk_h = _dot(dzb, qm[hh], _TN)
                dv_h = _dot(p.astype(BF16), dov[hh], _TN)
                dk_t = dk_h if dk_t is None else dk_t + dk_h
                dv_t = dv_h if dv_t is None else dv_t + dv_h
                if has_bias:
                    db_row = db_row + jnp.sum(dz, axis=1, keepdims=True)
                    refs[n_in + 4][hh, kb] += jnp.sum(dz, axis=0, keepdims=True)
                out += [dq + _dot(dzb, kv), db_row]
            dk_ref[pl.ds(ks, tk), :] += dk_t
            dv_ref[pl.ds(ks, tk), :] += dv_t
            return tuple(out)

        init = (jnp.zeros((tq, LANES), F32), jnp.zeros((tq, 1), F32)) * hpg
        if causal:
            carry = lax.fori_loop(0, qi, lambda kb, c: tile(kb, c, False), init)
            carry = tile(qi, carry, True)
        else:
            carry = lax.fori_loop(0, nk, lambda kb, c: tile(kb, c, False), init)
        dq_ref[...] = _merge(carry[0::2], masks) * scale
        if has_bias:
            for hh in range(hpg):
                refs[n_in + 3][hh] = carry[2 * hh + 1]

    in_specs = [_cols(tq, qo), _cols_all(Sk, ko), _cols_all(Sk, vo), _cols(tq, o[1]), _cols(tq, lse[1]), _cols(tq, do[1])]
    ins = [qa, ka, va, o[0], lse[0], do[0]]
    out_specs = [_cols(tq, 0), _cols_all(Sk, 0), _cols_all(Sk, 0)]
    out_shape = [jax.ShapeDtypeStruct((S, width), F32), jax.ShapeDtypeStruct((Sk, width), F32),
                 jax.ShapeDtypeStruct((Sk, width), F32)]
    if has_bias:
        rspec = pl.BlockSpec((hpg, tq, 1), lambda g, i: (g, i, 0))
        cspec = pl.BlockSpec((hpg, nk, 1, tk), lambda g, i: (g, 0, 0, 0))
        in_specs += [rspec, cspec]
        ins += list(bias)
        out_specs += [rspec, cspec]
        n_heads = width // hd
        out_shape += [jax.ShapeDtypeStruct((n_heads, S, 1), F32), jax.ShapeDtypeStruct((n_heads, nk, 1, tk), F32)]
    return pl.pallas_call(
        body, name=name, grid=(width // LANES, S // tq),
        in_specs=in_specs, out_specs=out_specs, out_shape=out_shape,
        compiler_params=_params(("parallel", "arbitrary")),
    )(*ins)


def _head_sums(t, masks):
    sums = [jnp.sum(_keep(t, m), axis=1, keepdims=True) for m in masks]
    return _merge([jnp.broadcast_to(s, t.shape) for s in sums], masks) if len(masks) > 1 else sums[0]


def _hnorm_fwd(x, g_lanes, *, width, hd, name, tr=512):
    xa, xo = x
    R = xa.shape[0]
    tr = _tile(R, tr, align=16)
    n_blk = width // LANES

    def body(x_ref, g_ref, o_ref):
        masks = _lane_masks(hd, tr)
        for j in range(n_blk):
            sl = slice(j * LANES, (j + 1) * LANES)
            xv = x_ref[:, sl].astype(F32)
            r = lax.rsqrt(_head_sums(xv * xv, masks) * (1.0 / hd) + EPS)
            o_ref[:, sl] = (xv * r * g_ref[...]).astype(o_ref.dtype)

    assert (xo * LANES) % width == 0
    return pl.pallas_call(
        body, name=name, grid=(R // tr,),
        in_specs=[pl.BlockSpec((tr, width), lambda i: (i, xo * LANES // width)), pl.BlockSpec((1, LANES), lambda i: (0, 0))],
        out_specs=pl.BlockSpec((tr, width), lambda i: (i, 0)),
        out_shape=jax.ShapeDtypeStruct((R, width), BF16),
        compiler_params=_params(("parallel",)),
    )(xa, g_lanes)


def _hnorm_bwd(x, g_lanes, dy, *, width, hd, name, tr=512):
    xa, xo = x
    R = xa.shape[0]
    tr = _tile(R, tr, align=16)
    n_blk = width // LANES

    def body(x_ref, g_ref, dy_ref, dx_ref, dg_ref):
        masks = _lane_masks(hd, tr)
        dg = jnp.zeros((1, LANES), F32)
        for j in range(n_blk):
            sl = slice(j * LANES, (j + 1) * LANES)
            xv = x_ref[:, sl].astype(F32)
            dyv = dy_ref[:, sl].astype(F32)
            r = lax.rsqrt(_head_sums(xv * xv, masks) * (1.0 / hd) + EPS)
            xh = xv * r
            dyg = dyv * g_ref[...]
            c = _head_sums(dyg * xh, masks) * (1.0 / hd)
            dx_ref[:, sl] = (r * (dyg - xh * c)).astype(dx_ref.dtype)
            dg = dg + jnp.sum(dyv * xh, axis=0, keepdims=True)
        if hd * 2 == LANES:
            dg8 = jnp.broadcast_to(dg, (8, LANES))
            dg = (dg8 + pltpu.roll(dg8, shift=hd, axis=1))[0:1]
        else:
            assert hd == LANES

        @pl.when(pl.program_id(0) == 0)
        def _():
            dg_ref[...] = jnp.zeros_like(dg_ref)

        dg_ref[...] += dg

    assert (xo * LANES) % width == 0
    return pl.pallas_call(
        body, name=name, grid=(R // tr,),
        in_specs=[pl.BlockSpec((tr, width), lambda i: (i, xo * LANES // width)), pl.BlockSpec((1, LANES), lambda i: (0, 0)),
                  pl.BlockSpec((tr, width), lambda i: (i, 0))],
        out_specs=[pl.BlockSpec((tr, width), lambda i: (i, 0)), pl.BlockSpec((1, LANES), lambda i: (0, 0))],
        out_shape=[jax.ShapeDtypeStruct((R, width), BF16), jax.ShapeDtypeStruct((1, LANES), F32)],
        compiler_params=_params(("arbitrary",)),
    )(xa, g_lanes, dy)


def _split3_dot(x, tri):
    a = x.astype(BF16)
    r = x - a.astype(F32)
    b = r.astype(BF16)
    c = (r - b.astype(F32)).astype(BF16)
    return _dot(a, tri) + _dot(b, tri) + _dot(c, tri)


def _forget_fwd(logit_t, b_col, *, name, blk=512):
    H, S = logit_t.shape
    blk = _tile(S, blk)

    def body(l_ref, b_ref, f_ref):
        r_i = lax.broadcasted_iota(jnp.int32, (blk, blk), 0)
        c_i = lax.broadcasted_iota(jnp.int32, (blk, blk), 1)
        upto = (r_i <= c_i).astype(BF16)
        carry = jnp.zeros((H, 1), F32)
        for j in range(S // blk):
            u = l_ref[:, j * blk:(j + 1) * blk] + b_ref[...]
            lf, _ = _log_sigmoid_pair(u)
            f_ref[:, j * blk:(j + 1) * blk] = _split3_dot(lf, upto) + carry
            carry = carry + jnp.sum(lf, axis=1, keepdims=True)

    return pl.pallas_call(
        body, name=name,
        out_shape=jax.ShapeDtypeStruct((H, S), F32),
        compiler_params=pltpu.CompilerParams(vmem_limit_bytes=VMEM_LIMIT),
    )(logit_t, b_col)


def _forget_bwd(logit_t, b_col, d_f, *, name, blk=512):
    H, S = logit_t.shape
    blk = _tile(S, blk)

    def body(l_ref, b_ref, df_ref, dl_ref, db_ref):
        r_i = lax.broadcasted_iota(jnp.int32, (blk, blk), 0)
        c_i = lax.broadcasted_iota(jnp.int32, (blk, blk), 1)
        fromon = (r_i >= c_i).astype(BF16)
        carry = jnp.zeros((H, 1), F32)
        db = jnp.zeros((H, 1), F32)
        for j in reversed(range(S // blk)):
            sl = slice(j * blk, (j + 1) * blk)
            dfv = df_ref[:, sl]
            d_lf = _split3_dot(dfv, fromon) + carry
            carry = carry + jnp.sum(dfv, axis=1, keepdims=True)
            u = l_ref[:, sl] + b_ref[...]
            _, lsn = _log_sigmoid_pair(u)
            dl = d_lf * jnp.exp(lsn)
            dl_ref[:, sl] = dl
            db = db + jnp.sum(dl, axis=1, keepdims=True)
        db_ref[...] = db

    return pl.pallas_call(
        body, name=name,
        out_shape=[jax.ShapeDtypeStruct((H, S), F32), jax.ShapeDtypeStruct((H, 1), F32)],
        compiler_params=pltpu.CompilerParams(vmem_limit_bytes=VMEM_LIMIT),
    )(logit_t, b_col, d_f)


def _sigmoid(t):
    return 1.0 / (1.0 + jnp.exp(-t))


def _gate_fwd(o3, w3, proj, D, *, name, tm=256):
    S = proj.shape[0]
    tm = _tile(S, tm)

    def body(o0, o1, o2, w0, w1, w2, g0, g1, g2, out_ref):
        acc = None
        for o_ref, w_ref, g_ref in ((o0, w0, g0), (o1, w1, g1), (o2, w2, g2)):
            t = _sigmoid(g_ref[...]) * _dot(o_ref[...], w_ref[...])
            acc = t if acc is None else acc + t
        out_ref[...] = acc.astype(out_ref.dtype)

    ospec = lambda d: pl.BlockSpec((tm, d), lambda i: (i, 0))
    wspec = lambda w: pl.BlockSpec(w.shape, lambda i: (0, 0))
    gspec = lambda j: pl.BlockSpec((tm, D), lambda i: (i, j))
    return pl.pallas_call(
        body, name=name, grid=(S // tm,),
        in_specs=[ospec(o.shape[1]) for o in o3] + [wspec(w) for w in w3] + [gspec(j) for j in range(3)],
        out_specs=pl.BlockSpec((tm, D), lambda i: (i, 0)),
        out_shape=jax.ShapeDtypeStruct((S, D), BF16),
        compiler_params=_params(("parallel",)),
    )(*o3, *w3, proj, proj, proj)


def _gate_bwd(o3, w3, proj, dmerged, D, *, name, tm=256):
    S = proj.shape[0]
    tm = _tile(S, tm)

    def body(o0, o1, o2, w0, w1, w2, g0, g1, g2, dm_ref, dg_ref, db0, db1, db2, do0, do1, do2):
        dm = dm_ref[...]
        for j, (o_ref, w_ref, g_ref, db_ref, do_ref) in enumerate(
                ((o0, w0, g0, db0, do0), (o1, w1, g1, db1, do1), (o2, w2, g2, db2, do2))):
            s = _sigmoid(g_ref[...])
            br = _dot(o_ref[...], w_ref[...])
            dg_ref[:, j * D:(j + 1) * D] = (dm * br * s * (1.0 - s)).astype(dg_ref.dtype)
            dbr = (dm * s).astype(BF16)
            db_ref[...] = dbr
            do_ref[...] = _dot(dbr, w_ref[...], _NT)

    ospec = lambda d: pl.BlockSpec((tm, d), lambda i: (i, 0))
    wspec = lambda w: pl.BlockSpec(w.shape, lambda i: (0, 0))
    gspec = lambda j: pl.BlockSpec((tm, D), lambda i: (i, j))
    dspec = pl.BlockSpec((tm, D), lambda i: (i, 0))
    return pl.pallas_call(
        body, name=name, grid=(S // tm,),
        in_specs=[ospec(o.shape[1]) for o in o3] + [wspec(w) for w in w3] + [gspec(j) for j in range(3)] + [dspec],
        out_specs=[pl.BlockSpec((tm, 3 * D), lambda i: (i, 0))] + [dspec] * 3 + [ospec(o.shape[1]) for o in o3],
        out_shape=[jax.ShapeDtypeStruct((S, 3 * D), BF16)] + [jax.ShapeDtypeStruct((S, D), BF16)] * 3
        + [jax.ShapeDtypeStruct((S, o.shape[1]), F32) for o in o3],
        compiler_params=_params(("parallel",)),
    )(*o3, *w3, proj, proj, proj, dmerged)


def _loss_sum(dy, D, *, name):
    R, C = dy.shape
    tr = _row_tile(R, C, 2)

    def body(dy_ref, out_ref):
        @pl.when(pl.program_id(0) == 0)
        def _():
            out_ref[...] = jnp.zeros_like(out_ref)

        v = dy_ref[...]
        out_ref[...] += (0.5 * D) * jnp.sum(v * v)

    return pl.pallas_call(
        body, name=name, grid=(R // tr,),
        in_specs=[pl.BlockSpec((tr, C), lambda i: (i, 0))],
        out_specs=pl.BlockSpec((8, 128), lambda i: (0, 0)),
        out_shape=jax.ShapeDtypeStruct((8, 128), F32),
        compiler_params=_params(("arbitrary",)),
    )(dy)[0, 0]


def _adamw_math(w, g, m, v):
    m2 = ADAM_B1 * m + (1.0 - ADAM_B1) * g
    v2 = ADAM_B2 * v + (1.0 - ADAM_B2) * (g * g)
    m_hat = m2 / (1.0 - ADAM_B1 ** ADAM_STEP)
    v_hat = v2 / (1.0 - ADAM_B2 ** ADAM_STEP)
    delta = -ADAM_LR * (m_hat / (jnp.sqrt(v_hat) + ADAM_EPS) + ADAM_WD * w)
    return delta, m2, v2


def _adamw(w, g, m, v, *, name):
    return _ew(_adamw_math, [w, g, m, v], (F32, F32, F32), name=name)


def _adamw_small(w, parts, m, v, *, name):
    n = parts.shape[0]

    def body(w_ref, p_ref, m_ref, v_ref, g_ref, d_ref, m2_ref, v2_ref):
        g = p_ref[0]
        for i in range(1, n):
            g = g + p_ref[i]
        g_ref[...] = g
        d_ref[...], m2_ref[...], v2_ref[...] = _adamw_math(w_ref[...], g, m_ref[...], v_ref[...])

    shp = jax.ShapeDtypeStruct(w.shape, F32)
    return pl.pallas_call(body, name=name, out_shape=[shp] * 4)(w, parts, m, v)


_ANY = pl.BlockSpec(memory_space=pl.ANY)


def _mesh_place():
    x, y, c = lax.axis_index("x"), lax.axis_index("y"), lax.axis_index("c")
    chips = [(1 - x, y), (x, 1 - y), (1 - x, 1 - y)]
    return x, y, c, chips


def _remote(src, dst, sems, i, to):
    send_sems, recv_sems = sems
    return pltpu.make_async_remote_copy(src_ref=src, dst_ref=dst, send_sem=send_sems.at[i], recv_sem=recv_sems.at[i],
                                        device_id=to, device_id_type=MESH_ID)


def _gather_weights(shards, *, name):
    n = len(shards)

    def body(*refs):
        ins, outs = refs[:n], refs[n:2 * n]
        sems = refs[2 * n:2 * n + 2]
        x, y, c, chips = _mesh_place()
        me = 2 * x + y
        sibling = (x, y, 1 - c)
        sent = []
        for w in range(n):
            for j, chip in enumerate(chips):
                cp = _remote(ins[w].at[c], outs[w].at[me, c], sems, 6 * w + j, (chip[0], chip[1], c))
                cp.start()
                sent.append(cp)
        for w in range(n):
            for j, chip in enumerate(chips):
                got = outs[w].at[2 * chip[0] + chip[1], c]
                _remote(got, got, sems, 6 * w + j, sibling).wait_recv()
                cp = _remote(got, got, sems, 6 * w + 3 + j, sibling)
                cp.start()
                sent.append(cp)
        for w in range(n):
            for j, chip in enumerate(chips):
                got = outs[w].at[2 * chip[0] + chip[1], 1 - c]
                _remote(got, got, sems, 6 * w + 3 + j, sibling).wait_recv()
        for cp in sent:
            cp.wait_send()

    outs = pl.pallas_call(
        body, name=name,
        in_specs=[_ANY] * n, out_specs=[_ANY] * n,
        out_shape=[jax.ShapeDtypeStruct((N_CHIPS,) + s.shape, s.dtype) for s in shards],
        scratch_shapes=[pltpu.SemaphoreType.DMA((6 * n,)), pltpu.SemaphoreType.DMA((6 * n,))],
    )(*shards)
    me = 2 * lax.axis_index("x") + lax.axis_index("y")
    return [lax.dynamic_update_index_in_dim(o, s, me, 0) for o, s in zip(outs, shards)]


def _forward_halves(gathered, *, name):
    n = len(gathered)

    def body(*refs):
        ins, outs = refs[:n], refs[n:2 * n]
        sems = refs[2 * n:2 * n + 2]
        x, y, c, chips = _mesh_place()
        sibling = (x, y, 1 - c)
        sent = []
        for w in range(n):
            for j, chip in enumerate(chips):
                peer = 2 * chip[0] + chip[1]
                cp = _remote(ins[w].at[peer, c], outs[w].at[peer, c], sems, 3 * w + j, sibling)
                cp.start()
                sent.append(cp)
        for w in range(n):
            for j, chip in enumerate(chips):
                land = outs[w].at[2 * chip[0] + chip[1], 1 - c]
                _remote(land, land, sems, 3 * w + j, sibling).wait_recv()
        for cp in sent:
            cp.wait_send()

    return pl.pallas_call(
        body, name=name,
        in_specs=[_ANY] * n, out_specs=[_ANY] * n,
        out_shape=[jax.ShapeDtypeStruct(g.shape, g.dtype) for g in gathered],
        input_output_aliases={w: w for w in range(n)},
        scratch_shapes=[pltpu.SemaphoreType.DMA((3 * n,)), pltpu.SemaphoreType.DMA((3 * n,))],
    )(*gathered)


def _exchange_siblings(grads, *, name):
    n = len(grads)

    def body(*refs):
        ins, got = refs[:n], refs[n:2 * n]
        sems = refs[2 * n:2 * n + 2]
        x, y, c, _ = _mesh_place()
        sibling = (x, y, 1 - c)
        sent = []
        for w in range(n):
            for s in range(N_CHIPS):
                cp = _remote(ins[w].at[s, 1 - c], got[w].at[s], sems, N_CHIPS * w + s, sibling)
                cp.start()
                sent.append(cp)
        for w in range(n):
            for s in range(N_CHIPS):
                _remote(got[w].at[s], got[w].at[s], sems, N_CHIPS * w + s, sibling).wait_recv()
        for cp in sent:
            cp.wait_send()

    n_sem = N_CHIPS * n
    got = pl.pallas_call(
        body, name=name,
        in_specs=[_ANY] * n, out_specs=[_ANY] * n,
        out_shape=[jax.ShapeDtypeStruct((N_CHIPS,) + g.shape[2:], g.dtype) for g in grads],
        scratch_shapes=[pltpu.SemaphoreType.DMA((n_sem,)), pltpu.SemaphoreType.DMA((n_sem,))],
    )(*grads)
    c = lax.axis_index("c")
    return [lax.dynamic_index_in_dim(g, c, 1, keepdims=False) for g in grads], got


def _exchange_chips(parts, *, name):
    n = len(parts)

    def body(*refs):
        ins, got = refs[:n], refs[n:2 * n]
        sems = refs[2 * n:2 * n + 2]
        x, y, c, chips = _mesh_place()
        sent = []
        for w in range(n):
            for j, chip in enumerate(chips):
                cp = _remote(ins[w].at[2 * chip[0] + chip[1]], got[w].at[j], sems, 3 * w + j, (chip[0], chip[1], c))
                cp.start()
                sent.append(cp)
        for w in range(n):
            for j in range(3):
                _remote(got[w].at[j], got[w].at[j], sems, 3 * w + j, (x, y, c)).wait_recv()
        for cp in sent:
            cp.wait_send()

    got = pl.pallas_call(
        body, name=name,
        in_specs=[_ANY] * n, out_specs=[_ANY] * n,
        out_shape=[jax.ShapeDtypeStruct((3,) + p.shape[1:], p.dtype) for p in parts],
        scratch_shapes=[pltpu.SemaphoreType.DMA((3 * n,)), pltpu.SemaphoreType.DMA((3 * n,))],
    )(*parts)
    me = 2 * lax.axis_index("x") + lax.axis_index("y")
    return [lax.dynamic_index_in_dim(p, me, 0, keepdims=False) for p in parts], got


def _share_halves(halves, small):
    n = len(halves)

    def body(*refs):
        ins, small_ref = refs[:n], refs[n]
        outs, small_out = refs[n + 1:2 * n + 1], refs[2 * n + 1]
        sems = refs[2 * n + 2:2 * n + 4]
        x, y, c, chips = _mesh_place()
        sibling = (x, y, 1 - c)
        me = 4 * x + 2 * y + c
        sent = [_remote(ins[w], outs[w].at[c], sems, w, sibling) for w in range(n)]
        peers = [sibling] + [(ch[0], ch[1], cc) for ch in chips for cc in (c, 1 - c)]
        sent += [_remote(small_ref, small_out.at[me], sems, n + j, peer) for j, peer in enumerate(peers)]
        for cp in sent:
            cp.start()
        for w in range(n):
            _remote(outs[w].at[1 - c], outs[w].at[1 - c], sems, w, sibling).wait_recv()
        for j, peer in enumerate(peers):
            frm = small_out.at[4 * peer[0] + 2 * peer[1] + peer[2]]
            _remote(frm, frm, sems, n + j, peer).wait_recv()
        for cp in sent:
            cp.wait_send()

    n_sem = n + 7
    outs = pl.pallas_call(
        body, name="share_halves",
        in_specs=[_ANY] * (n + 1), out_specs=[_ANY] * (n + 1),
        out_shape=[jax.ShapeDtypeStruct((2,) + h.shape, h.dtype) for h in halves]
        + [jax.ShapeDtypeStruct((8,) + small.shape, small.dtype)],
        scratch_shapes=[pltpu.SemaphoreType.DMA((n_sem,)), pltpu.SemaphoreType.DMA((n_sem,))],
    )(*halves, small)
    c = lax.axis_index("c")
    me = 4 * lax.axis_index("x") + 2 * lax.axis_index("y") + c
    return ([lax.dynamic_update_index_in_dim(o, h, c, 0) for o, h in zip(outs[:n], halves)],
            lax.dynamic_update_index_in_dim(outs[n], small, me, 0))


EARLY = ("w_ff_down", "w_ff_up", "w_out", "w_branch_sb", "w_branch_fox", "w_branch_mem")


def _local_step(x, mem, target, small, W, gather_rest=None, reduce_early=None, reduce_late=None):
    S, D = x.shape
    o_gate, o_qkv, o_mq, o_f = 0, 3 * D, 3 * D + 2 * 3 * D_SB, 3 * D + 2 * 3 * D_SB + D_MEM
    tq = 512

    h = _rmsnorm_fwd(x, small["g_mix_norm"], BF16, name="mix_norm")
    proj = _mm(h, W["w_in"], name="in_proj", tb=True, tn=768)
    blk = lambda j: (proj, (o_qkv + j * D_SB) // LANES)
    sb_q, sb_k, sb_v, fx_q, fx_k, fx_v = [blk(j) for j in range(6)]
    m_q = (proj, o_mq // LANES)
    f_logit_t = proj[:, o_f:o_f + FOX_HEADS].T
    b_col = small["b_forget"].reshape(FOX_HEADS, 1)
    lanes = lambda g: jnp.tile(g, (1, LANES // g.shape[1]))
    g_fq, g_fk, g_mq, g_mk = [lanes(small[k]) for k in ("g_fox_q", "g_fox_k", "g_mem_q", "g_mem_k")]

    if gather_rest is None:
        o_sb, sb_tot = _sbl_fwd(sb_q, sb_k, sb_v, width=D_SB, hd=HD, name="sb_fwd", tq=tq)
    else:
        comm, finish = gather_rest
        o_sb, sb_tot, *landed = _sbl_fwd(sb_q, sb_k, sb_v, width=D_SB, hd=HD, name="sb_fwd", tq=tq, comm=comm)
        W = {**W, **finish(landed)}

    fq = _hnorm_fwd(fx_q, g_fq, width=D_FOX, hd=HD, name="fox_q_norm")
    fk = _hnorm_fwd(fx_k, g_fk, width=D_FOX, hd=HD, name="fox_k_norm")
    f_cum = _forget_fwd(f_logit_t, b_col, name="forget_fwd")
    tkf = _tile(S, tq)
    f_bias = (f_cum.reshape(FOX_HEADS, S, 1), f_cum.reshape(FOX_HEADS, S // tkf, 1, tkf))
    o_fox, fox_lse = _sml_fwd((fq, 0), (fk, 0), fx_v, f_bias, width=D_FOX, hd=HD, causal=True, name="fox_fwd",
                              tq=tq, tk=tq)

    mh = _rmsnorm_fwd(mem, small["g_mem_norm"], BF16, name="mem_norm")
    mkv = _mm(mh, W["w_mem_kv"], name="mem_kv")
    mv = (mkv, D_MEM // LANES)
    mq = _hnorm_fwd(m_q, g_mq, width=D_MEM, hd=MEM_HD, name="mem_q_norm")
    mk = _hnorm_fwd((mkv, 0), g_mk, width=D_MEM, hd=MEM_HD, name="mem_k_norm")
    o_mem, mem_lse = _sml_fwd((mq, 0), (mk, 0), mv, width=D_MEM, hd=MEM_HD, causal=False, name="mem_fwd", tq=tq, tk=256)

    o3 = [o_sb, o_fox, o_mem]
    w3 = [W["w_branch_sb"], W["w_branch_fox"], W["w_branch_mem"]]
    merged = _gate_fwd(o3, w3, proj, D, name="gate_fwd")
    x1 = _mm(merged, W["w_out"], name="out_proj", extras=(x,), epilogue=lambda acc, res: (res + acc,))
    h2 = _rmsnorm_fwd(x1, small["g_mlp_norm"], BF16, name="mlp_norm")

    def relu2(acc):
        u = jnp.maximum(acc, 0.0)
        return u, u * u

    u, a = _mm(h2, W["w_ff_up"], name="ff_up", out_dtypes=(BF16, BF16), epilogue=relu2)
    def head(acc, res, tgt):
        d = (res + acc - tgt) * (1.0 / D)
        return d, d

    dy, dy16 = _mm(a, W["w_ff_down"], name="ff_down", extras=(x1, target), out_dtypes=(F32, BF16), epilogue=head, tm=512)
    loss = _loss_sum(dy, D, name="loss")

    G = {}
    du = _mm(dy16, W["w_ff_down"], name="d_ff_act", tb=True, out_dtypes=(BF16,), extras=(u,),
             epilogue=lambda acc, uu: (acc * (2.0 * uu.astype(F32)),))
    G["w_ff_down"] = _mm(a, dy16, name="d_w_ff_down", ta=True, out_dtypes=(BF16,))
    G["w_ff_up"] = _mm(h2, du, name="d_w_ff_up", ta=True, out_dtypes=(BF16,))
    dh2 = _mm(du, W["w_ff_up"], name="d_mlp_in", tb=True)
    dx1, dg_mlp = _rmsnorm_bwd(x1, small["g_mlp_norm"], dh2, add=dy, name="d_mlp_norm")
    dmerged = _mm(dx1, W["w_out"], name="d_merged", tb=True)
    G["w_out"] = _mm(merged, dx1, name="d_w_out", ta=True, out_dtypes=(BF16,))
    dgate, db0, db1, db2, do_sb, do_fox, do_mem = _gate_bwd(o3, w3, proj, dmerged, D, name="gate_bwd")
    for nm, o, db in zip(("w_branch_sb", "w_branch_fox", "w_branch_mem"), o3, (db0, db1, db2)):
        G[nm] = _mm(o, db, name="d_" + nm, ta=True, out_dtypes=(BF16,))

    early = {}
    if reduce_early is None:
        dsb = _sbl_bwd(sb_q, sb_k, sb_v, (do_sb, 0), (sb_tot, 0), width=D_SB, hd=HD, name="sb_bwd", tq=tq)
    else:
        comm, finish = reduce_early({k: G.pop(k) for k in EARLY})
        *dsb, = _sbl_bwd(sb_q, sb_k, sb_v, (do_sb, 0), (sb_tot, 0), width=D_SB, hd=HD, name="sb_bwd", tq=tq, comm=comm)
        dsb, landed = dsb[:3], dsb[3:]
        early = finish(landed)
    dfq, dfk, dfv, df_row, df_col = _sml_bwd((fq, 0), (fk, 0), fx_v, (o_fox, 0), (fox_lse, 0), (do_fox, 0), f_bias,
                                             width=D_FOX, hd=HD, causal=True, name="fox_bwd", tq=tq, tk=tq)
    dfx_q, dg_fox_q = _hnorm_bwd(fx_q, g_fq, dfq, width=D_FOX, hd=HD, name="d_fox_q_norm")
    dfx_k, dg_fox_k = _hnorm_bwd(fx_k, g_fk, dfk, width=D_FOX, hd=HD, name="d_fox_k_norm")
    d_fcum = df_row.reshape(FOX_HEADS, S) - df_col.reshape(FOX_HEADS, S)
    d_flogit_t, db_forget = _forget_bwd(f_logit_t, b_col, d_fcum, name="forget_bwd")

    dmq_n, dmk_n, dmv = _sml_bwd((mq, 0), (mk, 0), mv, (o_mem, 0), (mem_lse, 0), (do_mem, 0), width=D_MEM, hd=MEM_HD,
                                 causal=False, name="mem_bwd", tq=tq, tk=256)
    dm_q, dg_mem_q = _hnorm_bwd(m_q, g_mq, dmq_n, width=D_MEM, hd=MEM_HD, name="d_mem_q_norm")
    dmk_raw, dg_mem_k = _hnorm_bwd((mkv, 0), g_mk, dmk_n, width=D_MEM, hd=MEM_HD, name="d_mem_k_norm")
    dmkv = jnp.concatenate([dmk_raw, dmv.astype(BF16)], axis=1)
    G["w_mem_kv"] = _mm(mh, dmkv, name="d_w_mem_kv", ta=True, out_dtypes=(BF16,))
    dmh = _mm(dmkv, W["w_mem_kv"], name="d_mem_h", tb=True)
    _, dg_mem = _rmsnorm_bwd(mem, small["g_mem_norm"], dmh, name="d_mem_norm")
    dg_fox_q, dg_fox_k = dg_fox_q[:, :HD], dg_fox_k[:, :HD]

    dproj = jnp.concatenate(
        [dgate] + [t.astype(BF16) for t in (*dsb, dfx_q, dfx_k, dfv, dm_q)]
        + [d_flogit_t.T.astype(BF16), jnp.zeros((S, F_PAD - FOX_HEADS), BF16)], axis=1)
    g_w_in = _mm(dproj, h, name="d_w_in", ta=True, out_dtypes=(BF16,), tm=768)
    if reduce_late is None:
        G["w_in"] = g_w_in
        dh = _mm(dproj, W["w_in"], name="d_mix_in", tk=2304)
    else:
        comm, finish = reduce_late({"w_in": g_w_in, "w_mem_kv": G.pop("w_mem_kv")})
        dh, *landed = _mm(dproj, W["w_in"], name="d_mix_in", tk=2304, comm=comm)
        early.update(finish(landed))
    grad_x, dg_mix = _rmsnorm_bwd(x, small["g_mix_norm"], dh, add=dx1, name="d_mix_norm")

    small_grads = dict(g_mix_norm=dg_mix, g_mem_norm=dg_mem, b_forget=db_forget.reshape(1, FOX_HEADS),
                       g_fox_q=dg_fox_q, g_fox_k=dg_fox_k, g_mem_q=dg_mem_q, g_mem_k=dg_mem_k, g_mlp_norm=dg_mlp)
    return loss, grad_x, G, small_grads, early


BIG = ("w_in", "w_mem_kv", "w_branch_sb", "w_branch_fox", "w_branch_mem", "w_out", "w_ff_up", "w_ff_down")
COLUMN_SHARDED = ("w_in", "w_branch_sb", "w_branch_fox", "w_branch_mem", "w_ff_up")
SMALL = ("g_mix_norm", "g_mem_norm", "b_forget", "g_fox_q", "g_fox_k", "g_mem_q", "g_mem_k", "g_mlp_norm")
ORDER = ("g_mix_norm", "g_mem_norm", "w_in", "b_forget", "g_fox_q", "g_fox_k", "g_mem_q", "g_mem_k", "w_mem_kv",
         "w_branch_sb", "w_branch_fox", "w_branch_mem", "w_out", "g_mlp_norm", "w_ff_up", "w_ff_down")


def _unshard(name, gathered):
    n, _, rh, c = gathered.shape
    t = gathered.reshape(n, 2 * rh, c)
    if name in COLUMN_SHARDED:
        return t.transpose(1, 0, 2).reshape(2 * rh, n * c)
    return t.reshape(n * 2 * rh, c)


def _reshard(name, full):
    if name in COLUMN_SHARDED:
        r, c = full.shape
        t = full.reshape(r, N_CHIPS, c // N_CHIPS).transpose(1, 0, 2)
    else:
        r, c = full.shape[0] // N_CHIPS, full.shape[1]
        t = full.reshape(N_CHIPS, r, c)
    return t.reshape(N_CHIPS, 2, t.shape[1] // 2, t.shape[2])


ROW_TILE = 16
IN_BUF_ALIGN = 256


def _in_segments(D):
    n_qkv = 6 * D_SB
    o_mq, o_gate = n_qkv + FOX_HEADS, n_qkv + FOX_HEADS + D_MEM
    return [(0, n_qkv, 3 * D), (n_qkv, o_mq, 3 * D + n_qkv + D_MEM), (o_mq, o_gate, 3 * D + n_qkv), (o_gate, o_gate + 3 * D, 0)]


class _InLayout:
    def __init__(self, D, shard, n):
        self.D, self.shard, self.n = D, shard, n
        down = lambda v: v // ROW_TILE * ROW_TILE
        up = lambda v: -(-v // ROW_TILE) * ROW_TILE
        self.pieces = []
        ends = []
        for s in range(n):
            cursor, mine = 0, []
            for a, b, p in _in_segments(D):
                x0, x1 = max(a, s * shard), min(b, (s + 1) * shard)
                if x0 < x1:
                    p0 = p + x0 - a
                    rows = up(p0 + x1 - x0) - down(p0)
                    mine.append((x0 - s * shard, x1 - x0, p0, cursor, rows))
                    cursor += rows
            self.pieces.append(mine)
            ends.append(cursor)
        self.rows = -(-max(ends) // IN_BUF_ALIGN) * IN_BUF_ALIGN
        self.padded_rows = 3 * D + 6 * D_SB + D_MEM + F_PAD

    def _per_shard(self, fn, chip, operand):
        return lax.switch(chip, [functools.partial(fn, s) for s in range(self.n)], operand)

    def pack(self, chip, rows):
        def one(s, t):
            out, at = [], 0
            for x0, n_rows, p0, start, region in self.pieces[s]:
                lead = p0 % ROW_TILE
                out += [jnp.zeros((start + lead - at, t.shape[1]), t.dtype), t[x0:x0 + n_rows]]
                at = start + lead + n_rows
            return jnp.concatenate(out + [jnp.zeros((self.rows - at, t.shape[1]), t.dtype)], axis=0)
        return self._per_shard(one, chip, rows)

    def unpack(self, chip, buf, pad_to):
        def one(s, t):
            out = [t[start + p0 % ROW_TILE:start + p0 % ROW_TILE + n_rows] for _, n_rows, p0, start, _ in self.pieces[s]]
            return jnp.concatenate(out + [jnp.zeros((pad_to - self.shard, t.shape[1]), t.dtype)], axis=0)
        return self._per_shard(one, chip, buf)

    def to_padded(self, bufs):
        runs = sorted((p0, s, start, region) for s in range(self.n) for _, _, p0, start, region in self.pieces[s])
        chunks, end = [], 0
        for p0, s, start, region in runs:
            d0 = p0 // ROW_TILE * ROW_TILE
            src = bufs[s, start:start + region]
            if d0 < end:
                assert end - d0 == ROW_TILE
                last = chunks.pop()
                chunks += [last[:-ROW_TILE], last[-ROW_TILE:] + src[:ROW_TILE], src[ROW_TILE:]]
            else:
                if d0 > end:
                    chunks.append(jnp.zeros((d0 - end, bufs.shape[2]), bufs.dtype))
                chunks.append(src)
            end = d0 + region
        chunks.append(jnp.zeros((self.padded_rows - end, bufs.shape[2]), bufs.dtype))
        return jnp.concatenate(chunks, axis=0)

    def from_padded(self, gp):
        bufs = []
        for s in range(self.n):
            out, at = [], 0
            for _, n_rows, p0, start, region in self.pieces[s]:
                d0 = p0 // ROW_TILE * ROW_TILE
                row = d0 + lax.broadcasted_iota(jnp.int32, (region, 1), 0)
                out.append(jnp.where((row >= p0) & (row < p0 + n_rows), gp[d0:d0 + region], jnp.zeros((), gp.dtype)))
                at = start + region
            bufs.append(jnp.concatenate(out + [jnp.zeros((self.rows - at, gp.shape[1]), gp.dtype)], axis=0))
        return jnp.stack(bufs)


def _pack_small(vals):
    width = max(vals[k].shape[1] for k in SMALL)
    return jnp.concatenate([jnp.pad(vals[k].astype(F32), ((0, 0), (0, width - vals[k].shape[1]))) for k in SMALL], axis=0)


def _unpack_small(packed, like):
    return {k: packed[i:i + 1, :like[k].shape[1]] for i, k in enumerate(SMALL)}


def kernel(x, mem, g_mix_norm, g_mem_norm, w_in, b_forget, g_fox_q, g_fox_k, g_mem_q, g_mem_k, w_mem_kv, w_branch_sb, w_branch_fox, w_branch_mem, w_out, g_mlp_norm, w_ff_up, w_ff_down, loss_target, m_g_mix_norm, m_g_mem_norm, m_w_in, m_b_forget, m_g_fox_q, m_g_fox_k, m_g_mem_q, m_g_mem_k, m_w_mem_kv, m_w_branch_sb, m_w_branch_fox, m_w_branch_mem, m_w_out, m_g_mlp_norm, m_w_ff_up, m_w_ff_down, v_g_mix_norm, v_g_mem_norm, v_w_in, v_b_forget, v_g_fox_q, v_g_fox_k, v_g_mem_q, v_g_mem_k, v_w_mem_kv, v_w_branch_sb, v_w_branch_fox, v_w_branch_mem, v_w_out, v_g_mlp_norm, v_w_ff_up, v_w_ff_down):
    given = dict(locals())
    D = x.shape[-1]
    weights = {k: given[k] for k in ORDER}
    moms = {k: given["m_" + k] for k in ORDER}
    vars_ = {k: given["v_" + k] for k in ORDER}

    me_chip = 2 * lax.axis_index("x") + lax.axis_index("y")

    n_in = w_in.shape[2]
    lay = _InLayout(D, n_in, N_CHIPS)
    transposed = lambda t: jnp.transpose(t[0])
    shards = {}
    for k in BIG:
        w = weights[k][0].astype(BF16)
        if k == "w_in":
            w = lay.pack(me_chip, jnp.transpose(w))
        shards[k] = w.reshape(2, w.shape[0] // 2, w.shape[1])
    gathered_in = _gather_weights([shards["w_in"]], name="gather_w_in")[0]
    W = {"w_in": lay.to_padded(gathered_in.reshape(N_CHIPS, lay.rows, D))}
    rest = [k for k in BIG if k != "w_in"]

    def finish_weights(landed):
        full = _forward_halves(landed, name="forward_halves")
        full = [lax.dynamic_update_index_in_dim(o, shards[k], me_chip, 0) for k, o in zip(rest, full)]
        return {k: _unshard(k, g) for k, g in zip(rest, full)}

    def sum_pairs(names, own, sib):
        add2 = lambda p, q: (p.astype(F32) + q.astype(F32),)
        out = []
        for k, p, q in zip(names, own, sib):
            n, r, c = p.shape
            out.append(_ew(add2, [p.reshape(n * r, c), q.reshape(n * r, c)], (BF16,), name="sum_pair_" + k).reshape(n, r, c))
        return out

    def sum_chips(names, parts, got):
        add4 = lambda p, q0, q1, q2: (((p.astype(F32) + q0.astype(F32)) + q1.astype(F32)) + q2.astype(F32),)
        own = [lax.dynamic_index_in_dim(p, me_chip, 0, keepdims=False) for p in parts]
        return {k: _ew(add4, [p, q[0], q[1], q[2]], (F32,), name="sum_chips_" + k) for k, p, q in zip(names, own, got)}

    def reducer(tag):
        def reduce(grads):
            names = list(grads)
            stacked = {k: _reshard(k, grads[k]) for k in names if k != "w_in"}
            if "w_in" in grads:
                stacked["w_in"] = lay.from_padded(grads["w_in"]).reshape(N_CHIPS, 2, lay.rows // 2, D)
            own, sib = _exchange_siblings([stacked[k] for k in names], name="exchange_siblings_" + tag)
            parts = sum_pairs(names, own, sib)
            return _ChipExchange("scatter", parts), functools.partial(sum_chips, names, parts)
        return reduce

    small = {k: weights[k] for k in SMALL}
    loss_part, grad_x, G, small_grads, halves = _local_step(
        x[0], mem[0], loss_target[0], small, W,
        gather_rest=(_ChipExchange("gather", [shards[k] for k in rest]), finish_weights),
        reduce_early=reducer("early"), reduce_late=reducer("late"))
    assert not G, list(G)
    reduced, small_parts = _share_halves([halves[k] for k in BIG], _pack_small(small_grads))

    grads, deltas, new_m, new_v = {}, {}, {}, {}
    for k, g in zip(BIG, reduced):
        shp = weights[k].shape
        if k == "w_in":
            g2 = lay.unpack(me_chip, g.reshape(lay.rows, D), lay.rows)
            padded = lambda t: jnp.pad(transposed(t), ((0, lay.rows - n_in), (0, 0)))
            outs = _adamw(padded(weights[k]), g2, padded(moms[k]), padded(vars_[k]), name="adamw_" + k)
            g2, d, m2, v2 = [jnp.transpose(t[:n_in]) for t in (g2, *outs)]
        else:
            g2 = g.reshape(shp[1], shp[2])
            d, m2, v2 = _adamw(weights[k][0], g2, moms[k][0], vars_[k][0], name="adamw_" + k)
        grads[k], deltas[k], new_m[k], new_v[k] = g2.reshape(shp), d.reshape(shp), m2.reshape(shp), v2.reshape(shp)
    sg, sd, sm, sv = _adamw_small(_pack_small(small), small_parts, _pack_small({k: moms[k] for k in SMALL}),
                                  _pack_small({k: vars_[k] for k in SMALL}), name="adamw_small")
    for dst, packed in ((grads, sg), (deltas, sd), (new_m, sm), (new_v, sv)):
        dst.update(_unpack_small(packed, small))

    loss = lax.psum(loss_part, ("x", "y", "c"))
    return (loss, grad_x[None], *[grads[k] for k in ORDER], *[deltas[k] for k in ORDER],
            *[new_m[k] for k in ORDER], *[new_v[k] for k in ORDER])
```

```python
import functools

import jax
import jax.numpy as jnp
from jax import lax
from jax.experimental import pallas as pl
from jax.experimental.pallas import tpu as pltpu

F32 = jnp.float32
BF16 = jnp.bfloat16
MESH_ID = pl.DeviceIdType.MESH

HD = 64
SB_HEADS = 8
FOX_HEADS = 8
MEM_HEADS = 4
MEM_HD = 128
D_SB = SB_HEADS * HD
D_FOX = FOX_HEADS * HD
D_MEM = MEM_HEADS * MEM_HD
EPS = 1e-6
NEG_INF = -1e30

ADAM_LR = 0.001
ADAM_B1 = 0.9
ADAM_B2 = 0.999
ADAM_EPS = 1e-08
ADAM_WD = 0.01
ADAM_STEP = 10

N_CHIPS = 4
VMEM_LIMIT = 56 * 1024 * 1024

F_PAD = 256


def _tile(n, target, align=128):
    if n <= target:
        return n
    best = None
    t = align
    while t <= target:
        if n % t == 0:
            best = t
        t += align
    assert best is not None, (n, target, align)
    return best


def _params(sem):
    return pltpu.CompilerParams(dimension_semantics=sem, vmem_limit_bytes=VMEM_LIMIT)


def _mm(a, b, *, name, ta=False, tb=False, out_dtypes=(F32,), epilogue=None, extras=(),
        tm=1024, tn=1024, tk=2048, comm=None):
    if ta:
        K, M = a.shape
    else:
        M, K = a.shape
    if tb:
        N, K2 = b.shape
    else:
        K2, N = b.shape
    assert K == K2, (a.shape, b.shape, ta, tb)
    tm, tn, tk = _tile(M, tm), _tile(N, tn), _tile(K, tk)
    nk = K // tk
    n_extra, n_out = len(extras), len(out_dtypes)
    if epilogue is None:
        epilogue = lambda acc: (acc,)
    dims = (((0 if ta else 1,), (1 if tb else 0,)), ((), ()))

    gm, gn = M // tm, N // tn

    def body(*refs):
        i, j, k = pl.program_id(0), pl.program_id(1), pl.program_id(2)
        got = _carry(comm, 2 + n_extra, n_out, (i == 0) & (j == 0) & (k == 0),
                     (i == gm - 1) & (j == gn - 1) & (k == nk - 1), refs)
        (a_ref, b_ref, *extra_refs), out_refs = got[0], got[1]
        part = lax.dot_general(a_ref[...].astype(BF16), b_ref[...].astype(BF16), dims, preferred_element_type=F32)

        def finish(acc):
            outs = epilogue(acc, *[r[...] for r in extra_refs])
            for o_ref, o in zip(out_refs, outs):
                o_ref[...] = o.astype(o_ref.dtype)

        if nk == 1:
            finish(part)
        else:
            acc_ref = refs[-1]

            @pl.when(k == 0)
            def _():
                acc_ref[...] = part

            @pl.when((k > 0) & (k < nk - 1))
            def _():
                acc_ref[...] += part

            @pl.when(k == nk - 1)
            def _():
                finish(acc_ref[...] + part)

        got[2]()

    a_spec = pl.BlockSpec((tk, tm), lambda i, j, k: (k, i)) if ta else pl.BlockSpec((tm, tk), lambda i, j, k: (i, k))
    b_spec = pl.BlockSpec((tn, tk), lambda i, j, k: (j, k)) if tb else pl.BlockSpec((tk, tn), lambda i, j, k: (k, j))
    mn_spec = pl.BlockSpec((tm, tn), lambda i, j, k: (i, j))
    c_ins, c_in_specs, c_out_specs, c_out_shape, c_scratch = _comm_args(comm)
    sem = ("parallel", "parallel", "arbitrary") if comm is None else ("arbitrary",) * 3
    outs = pl.pallas_call(
        body, name=name,
        grid=(gm, gn, nk),
        in_specs=[a_spec, b_spec] + [mn_spec] * n_extra + c_in_specs,
        out_specs=[mn_spec] * n_out + c_out_specs,
        out_shape=[jax.ShapeDtypeStruct((M, N), dt) for dt in out_dtypes] + c_out_shape,
        scratch_shapes=c_scratch + ([pltpu.VMEM((tm, tn), F32)] if nk > 1 else []),
        compiler_params=_params(sem),
    )(a, b, *extras, *c_ins)
    return outs if len(outs) > 1 else outs[0]


def _row_tile(rows, cols, n_arrays):
    budget = 10 * 1024 * 1024
    cols_padded = -(-cols // 128) * 128
    target = max(16, budget // (cols_padded * 4 * n_arrays * 2))
    return _tile(rows, target, align=16)


def _ew(fn, ins, out_dtypes, *, name):
    R, C = ins[0].shape
    n_in, n_out = len(ins), len(out_dtypes)
    tr = _row_tile(R, C, n_in + n_out)

    def body(*refs):
        outs = fn(*[r[...] for r in refs[:n_in]])
        for o_ref, o in zip(refs[n_in:], outs):
            o_ref[...] = o.astype(o_ref.dtype)

    spec = pl.BlockSpec((tr, C), lambda i: (i, 0))
    outs = pl.pallas_call(
        body, name=name, grid=(R // tr,),
        in_specs=[spec] * n_in, out_specs=[spec] * n_out,
        out_shape=[jax.ShapeDtypeStruct((R, C), dt) for dt in out_dtypes],
        compiler_params=_params(("parallel",)),
    )(*ins)
    return outs if n_out > 1 else outs[0]


def _rmsnorm_fwd(x, g, out_dtype, *, name):
    R, d = x.shape
    tr = _row_tile(R, d, 3)

    def body(x_ref, g_ref, o_ref):
        xv = x_ref[...].astype(F32)
        r = lax.rsqrt(jnp.mean(xv * xv, axis=-1, keepdims=True) + EPS)
        o_ref[...] = (xv * r * g_ref[...]).astype(o_ref.dtype)

    return pl.pallas_call(
        body, name=name, grid=(R // tr,),
        in_specs=[pl.BlockSpec((tr, d), lambda i: (i, 0)), pl.BlockSpec((1, d), lambda i: (0, 0))],
        out_specs=pl.BlockSpec((tr, d), lambda i: (i, 0)),
        out_shape=jax.ShapeDtypeStruct((R, d), out_dtype),
        compiler_params=_params(("parallel",)),
    )(x, g)


def _rmsnorm_bwd(x, g, dy, add=None, *, name):
    R, d = x.shape
    has_add = add is not None
    tr = _row_tile(R, d, 5)

    def body(*refs):
        x_ref, g_ref, dy_ref = refs[:3]
        add_ref = refs[3] if has_add else None
        dx_ref, dg_ref = refs[-2:]
        xv = x_ref[...].astype(F32)
        dyv = dy_ref[...].astype(F32)
        r = lax.rsqrt(jnp.mean(xv * xv, axis=-1, keepdims=True) + EPS)
        xh = xv * r
        dyg = dyv * g_ref[...]
        c = jnp.mean(dyg * xh, axis=-1, keepdims=True)
        dx = r * (dyg - xh * c)
        if has_add:
            dx = dx + add_ref[...]
        dx_ref[...] = dx

        @pl.when(pl.program_id(0) == 0)
        def _():
            dg_ref[...] = jnp.zeros_like(dg_ref)

        dg_ref[...] += jnp.sum(dyv * xh, axis=0, keepdims=True)

    row = pl.BlockSpec((tr, d), lambda i: (i, 0))
    vec = pl.BlockSpec((1, d), lambda i: (0, 0))
    ins = [x, g, dy] + ([add] if has_add else [])
    return pl.pallas_call(
        body, name=name, grid=(R // tr,),
        in_specs=[row, vec, row] + ([row] if has_add else []),
        out_specs=[row, vec],
        out_shape=[jax.ShapeDtypeStruct((R, d), F32), jax.ShapeDtypeStruct((1, d), F32)],
        compiler_params=_params(("arbitrary",)),
    )(*ins)


_NT = (((1,), (1,)), ((), ()))
_TN = (((0,), (0,)), ((), ()))


def _dot(a, b, dims=(((1,), (0,)), ((), ()))):
    return lax.dot_general(a, b, dims, preferred_element_type=F32)


def _split_dot(x, tri):
    hi = x.astype(BF16)
    lo = (x - hi.astype(F32)).astype(BF16)
    return _dot(hi, tri) + _dot(lo, tri)


def _log_sigmoid_pair(z):
    sp = jnp.log(1.0 + jnp.exp(-jnp.abs(z)))
    return jnp.minimum(z, 0.0) - sp, jnp.minimum(-z, 0.0) - sp


def _sb_fwd(q, k, v, *, name, tq=256):
    H, S, hd = q.shape
    tq = _tile(S, tq)
    tk = tq
    scale = hd ** -0.5

    def body(q_ref, k_ref, v_ref, o_ref, tot_ref):
        qi = pl.program_id(1)
        qv = q_ref[0]
        row = qi * tq + lax.broadcasted_iota(jnp.int32, (tq, tk), 0)
        col0 = lax.broadcasted_iota(jnp.int32, (tq, tk), 1)
        later = (lax.broadcasted_iota(jnp.int32, (tk, tk), 0) > lax.broadcasted_iota(jnp.int32, (tk, tk), 1)).astype(BF16)

        def step(i, carry):
            acc, c_rem = carry
            kb = qi - i
            ks = pl.multiple_of(kb * tk, tk)
            kv = k_ref[0, pl.ds(ks, tk), :]
            vv = v_ref[0, pl.ds(ks, tk), :]
            z = _dot(qv, kv, _NT) * scale
            mask = (col0 + ks) < row
            ls, lsn = _log_sigmoid_pair(z)
            rem = jnp.where(mask, lsn, 0.0)
            after = _split_dot(rem, later) + c_rem
            w = jnp.where(mask, jnp.exp(ls + after), 0.0)
            acc = acc + _dot(w.astype(BF16), vv)
            c_rem = c_rem + jnp.sum(rem, axis=1, keepdims=True)
            return acc, c_rem

        acc, c_rem = lax.fori_loop(0, qi + 1, step, (jnp.zeros((tq, hd), F32), jnp.zeros((tq, 1), F32)))
        o_ref[0] = acc
        tot_ref[0] = c_rem

    qspec = pl.BlockSpec((1, tq, hd), lambda h, i: (h, i, 0))
    kspec = pl.BlockSpec((1, S, hd), lambda h, i: (h, 0, 0))
    return pl.pallas_call(
        body, name=name, grid=(H, S // tq),
        in_specs=[qspec, kspec, kspec],
        out_specs=[qspec, pl.BlockSpec((1, tq, 1), lambda h, i: (h, i, 0))],
        out_shape=[jax.ShapeDtypeStruct((H, S, hd), F32), jax.ShapeDtypeStruct((H, S, 1), F32)],
        compiler_params=_params(("parallel", "arbitrary")),
    )(q, k, v)


def _sb_bwd(q, k, v, do, tot, *, name, tq=256):
    H, S, hd = q.shape
    tq = _tile(S, tq)
    tk = tq
    scale = hd ** -0.5

    def body(q_ref, k_ref, v_ref, do_ref, tot_ref, dq_ref, dk_ref, dv_ref):
        qi = pl.program_id(1)

        @pl.when(qi == 0)
        def _():
            dk_ref[...] = jnp.zeros_like(dk_ref)
            dv_ref[...] = jnp.zeros_like(dv_ref)

        qv = q_ref[0]
        dov = do_ref[0].astype(BF16)
        tot_v = tot_ref[0]
        row = qi * tq + lax.broadcasted_iota(jnp.int32, (tq, tk), 0)
        col0 = lax.broadcasted_iota(jnp.int32, (tq, tk), 1)
        r_i = lax.broadcasted_iota(jnp.int32, (tk, tk), 0)
        c_i = lax.broadcasted_iota(jnp.int32, (tk, tk), 1)
        upto = (r_i <= c_i).astype(BF16)
        before = (r_i < c_i).astype(BF16)

        def step(kb, carry):
            dq, c_rem, c_g = carry
            ks = pl.multiple_of(kb * tk, tk)
            kv = k_ref[0, pl.ds(ks, tk), :]
            vv = v_ref[0, pl.ds(ks, tk), :]
            z = _dot(qv, kv, _NT) * scale
            mask = (col0 + ks) < row
            ls, lsn = _log_sigmoid_pair(z)
            rem = jnp.where(mask, lsn, 0.0)
            after = tot_v - (_split_dot(rem, upto) + c_rem)
            w = jnp.where(mask, jnp.exp(ls + after), 0.0)
            dw = _dot(dov, vv, _NT)
            g = dw * w
            g_before = _split_dot(g, before) + c_g
            beta = jnp.exp(ls)
            dz = jnp.where(mask, g * (1.0 - beta) - beta * g_before, 0.0) * scale
            dzb = dz.astype(BF16)
            dq = dq + _dot(dzb, kv)
            dk_ref[0, pl.ds(ks, tk), :] += _dot(dzb, qv, _TN)
            dv_ref[0, pl.ds(ks, tk), :] += _dot(w.astype(BF16), dov, _TN)
            return dq, c_rem + jnp.sum(rem, axis=1, keepdims=True), c_g + jnp.sum(g, axis=1, keepdims=True)

        zero = jnp.zeros((tq, 1), F32)
        dq, _, _ = lax.fori_loop(0, qi + 1, step, (jnp.zeros((tq, hd), F32), zero, zero))
        dq_ref[0] = dq

    qspec = pl.BlockSpec((1, tq, hd), lambda h, i: (h, i, 0))
    kspec = pl.BlockSpec((1, S, hd), lambda h, i: (h, 0, 0))
    full = jax.ShapeDtypeStruct((H, S, hd), F32)
    return pl.pallas_call(
        body, name=name, grid=(H, S // tq),
        in_specs=[qspec, kspec, kspec, qspec, pl.BlockSpec((1, tq, 1), lambda h, i: (h, i, 0))],
        out_specs=[qspec, kspec, kspec],
        out_shape=[full, full, full],
        compiler_params=_params(("parallel", "arbitrary")),
    )(q, k, v, do, tot)


def _softmax_fwd(q, k, v, bias=None, *, name, causal, tq=256, tk=256):
    H, S, hd = q.shape
    Sk = k.shape[1]
    tq, tk = _tile(S, tq), _tile(Sk, tk)
    if causal:
        assert tq == tk and S == Sk
    nk = Sk // tk
    scale = hd ** -0.5
    has_bias = bias is not None

    def body(*refs):
        q_ref, k_ref, v_ref = refs[:3]
        o_ref, lse_ref = refs[-2:]
        qi = pl.program_id(1)
        qv = q_ref[0]
        row = qi * tq + lax.broadcasted_iota(jnp.int32, (tq, tk), 0)
        col0 = lax.broadcasted_iota(jnp.int32, (tq, tk), 1)
        if has_bias:
            b_row = refs[3][0]

        def step(kb, carry):
            m, l, acc = carry
            ks = pl.multiple_of(kb * tk, tk)
            kv = k_ref[0, pl.ds(ks, tk), :]
            vv = v_ref[0, pl.ds(ks, tk), :]
            z = _dot(qv, kv, _NT) * scale
            if has_bias:
                z = z + b_row - refs[4][0, kb]
            if causal:
                z = jnp.where((col0 + ks) <= row, z, NEG_INF)
            m2 = jnp.maximum(m, jnp.max(z, axis=1, keepdims=True))
            p = jnp.exp(z - m2)
            alpha = jnp.exp(m - m2)
            l = alpha * l + jnp.sum(p, axis=1, keepdims=True)
            acc = alpha * acc + _dot(p.astype(BF16), vv)
            return m2, l, acc

        init = (jnp.full((tq, 1), NEG_INF, F32), jnp.zeros((tq, 1), F32), jnp.zeros((tq, hd), F32))
        m, l, acc = lax.fori_loop(0, (qi + 1) if causal else nk, step, init)
        o_ref[0] = acc / l
        lse_ref[0] = m + jnp.log(l)

    qspec = pl.BlockSpec((1, tq, hd), lambda h, i: (h, i, 0))
    kspec = pl.BlockSpec((1, Sk, hd), lambda h, i: (h, 0, 0))
    vspec = pl.BlockSpec((1, tq, 1), lambda h, i: (h, i, 0))
    in_specs = [qspec, kspec, kspec]
    ins = [q, k, v]
    if has_bias:
        in_specs += [vspec, pl.BlockSpec((1, nk, 1, tk), lambda h, i: (h, 0, 0, 0))]
        ins += list(bias)
    return pl.pallas_call(
        body, name=name, grid=(H, S // tq),
        in_specs=in_specs, out_specs=[qspec, vspec],
        out_shape=[jax.ShapeDtypeStruct((H, S, hd), F32), jax.ShapeDtypeStruct((H, S, 1), F32)],
        compiler_params=_params(("parallel", "arbitrary")),
    )(*ins)


def _softmax_bwd(q, k, v, o, lse, do, bias=None, *, name, causal, tq=256, tk=256):
    H, S, hd = q.shape
    Sk = k.shape[1]
    tq, tk = _tile(S, tq), _tile(Sk, tk)
    nk = Sk // tk
    scale = hd ** -0.5
    has_bias = bias is not None
    n_in = 8 if has_bias else 6

    def body(*refs):
        q_ref, k_ref, v_ref, o_ref, lse_ref, do_ref = refs[:6]
        dq_ref, dk_ref, dv_ref = refs[n_in:n_in + 3]
        qi = pl.program_id(1)

        @pl.when(qi == 0)
        def _():
            dk_ref[...] = jnp.zeros_like(dk_ref)
            dv_ref[...] = jnp.zeros_like(dv_ref)
            if has_bias:
                refs[n_in + 4][...] = jnp.zeros_like(refs[n_in + 4])

        qv = q_ref[0]
        do32 = do_ref[0]
        dov = do32.astype(BF16)
        delta = jnp.sum(do32 * o_ref[0], axis=1, keepdims=True)
        lse_v = lse_ref[0]
        row = qi * tq + lax.broadcasted_iota(jnp.int32, (tq, tk), 0)
        col0 = lax.broadcasted_iota(jnp.int32, (tq, tk), 1)
        if has_bias:
            b_row = refs[6][0]

        def step(kb, carry):
            dq, db_row = carry
            ks = pl.multiple_of(kb * tk, tk)
            kv = k_ref[0, pl.ds(ks, tk), :]
            vv = v_ref[0, pl.ds(ks, tk), :]
            z = _dot(qv, kv, _NT) * scale
            if has_bias:
                z = z + b_row - refs[7][0, kb]
            p = jnp.exp(z - lse_v)
            if causal:
                p = jnp.where((col0 + ks) <= row, p, 0.0)
            dp = _dot(dov, vv, _NT)
            dz = p * (dp - delta)
            dzb = dz.astype(BF16)
            dq = dq + _dot(dzb, kv)
            dk_ref[0, pl.ds(ks, tk), :] += _dot(dzb, qv, _TN) * scale
            dv_ref[0, pl.ds(ks, tk), :] += _dot(p.astype(BF16), dov, _TN)
            if has_bias:
                db_row = db_row + jnp.sum(dz, axis=1, keepdims=True)
                refs[n_in + 4][0, kb] += jnp.sum(dz, axis=0, keepdims=True)
            return dq, db_row

        dq, db_row = lax.fori_loop(0, (qi + 1) if causal else nk, step,
                                   (jnp.zeros((tq, hd), F32), jnp.zeros((tq, 1), F32)))
        dq_ref[0] = dq * scale
        if has_bias:
            refs[n_in + 3][0] = db_row

    qspec = pl.BlockSpec((1, tq, hd), lambda h, i: (h, i, 0))
    kspec = pl.BlockSpec((1, Sk, hd), lambda h, i: (h, 0, 0))
    vspec = pl.BlockSpec((1, tq, 1), lambda h, i: (h, i, 0))
    cspec = pl.BlockSpec((1, nk, 1, tk), lambda h, i: (h, 0, 0, 0))
    in_specs = [qspec, kspec, kspec, qspec, vspec, qspec]
    ins = [q, k, v, o, lse, do]
    out_specs = [qspec, kspec, kspec]
    out_shape = [jax.ShapeDtypeStruct((H, S, hd), F32), jax.ShapeDtypeStruct((H, Sk, hd), F32),
                 jax.ShapeDtypeStruct((H, Sk, hd), F32)]
    if has_bias:
        in_specs += [vspec, cspec]
        ins += list(bias)
        out_specs += [vspec, cspec]
        out_shape += [jax.ShapeDtypeStruct((H, S, 1), F32), jax.ShapeDtypeStruct((H, nk, 1, tk), F32)]
    return pl.pallas_call(
        body, name=name, grid=(H, S // tq),
        in_specs=in_specs, out_specs=out_specs, out_shape=out_shape,
        compiler_params=_params(("parallel", "arbitrary")),
    )(*ins)


LANES = 128
_LOW = -3e38


def _lane_masks(hd, rows):
    if hd == LANES:
        return [None]
    lane = lax.broadcasted_iota(jnp.int32, (rows, LANES), 1)
    return [(lane >= hh * hd) & (lane < (hh + 1) * hd) for hh in range(LANES // hd)]


def _keep(t, m):
    return t if m is None else jnp.where(m, t, 0.0)


def _merge(parts, masks):
    out = parts[-1]
    for p, m in zip(parts[-2::-1], masks[-2::-1]):
        out = jnp.where(m, p, out)
    return out


def _row_value(t, m):
    return jnp.max(t if m is None else jnp.where(m, t, _LOW), axis=1, keepdims=True)


def _cols(tq, off):
    return pl.BlockSpec((tq, LANES), lambda g, i: (i, off + g))


def _cols_all(rows, off):
    return pl.BlockSpec((rows, LANES), lambda g, i: (0, off + g))


SCAN_BLOCK = 256


def _tri(kind, cols):
    n = min(SCAN_BLOCK, cols)
    r = lax.broadcasted_iota(jnp.int32, (n, n), 0)
    c = lax.broadcasted_iota(jnp.int32, (n, n), 1)
    return ((r > c) if kind == "after" else (r < c)).astype(BF16)


def _scan_cols(x, tri, reverse):
    cols = x.shape[1]
    cb = min(SCAN_BLOCK, cols)
    assert cols % cb == 0 and tri.shape == (cb, cb)
    nb = cols // cb
    blocks = [x[:, b * cb:(b + 1) * cb] for b in range(nb)]
    outs, carry = [None] * nb, None
    for b in (reversed(range(nb)) if reverse else range(nb)):
        y = _dot(blocks[b].astype(BF16), tri)
        outs[b] = y if carry is None else y + carry
        s = jnp.sum(blocks[b], axis=1, keepdims=True)
        carry = s if carry is None else carry + s
    return (outs[0] if nb == 1 else jnp.concatenate(outs, axis=1)), carry


def _softplus_parts(z):
    pos = jnp.maximum(z, 0.0) + jnp.log(1.0 + jnp.exp(-jnp.abs(z)))
    return pos, z - pos


class _ChipExchange:
    def __init__(self, kind, ins):
        assert kind in ("gather", "scatter")
        self.kind, self.ins = kind, list(ins)
        lead = (lambda s: (N_CHIPS,) + s) if kind == "gather" else (lambda s: (3,) + s[1:])
        self.out_shape = [jax.ShapeDtypeStruct(lead(a.shape), a.dtype) for a in ins]
        n = 3 * len(ins)
        self.scratch = [pltpu.SemaphoreType.DMA((n,)), pltpu.SemaphoreType.DMA((n,))]

    def _copies(self, in_refs, out_refs, sems, landing):
        x, y, c, chips = _mesh_place()
        me = 2 * x + y
        out = []
        for w in range(len(self.ins)):
            for j, chip in enumerate(chips):
                peer = 2 * chip[0] + chip[1]
                if self.kind == "gather":
                    src, dst, land = in_refs[w].at[c], out_refs[w].at[me, c], out_refs[w].at[peer, c]
                else:
                    src, dst, land = in_refs[w].at[peer], out_refs[w].at[j], out_refs[w].at[j]
                if landing:
                    src, dst = land, land
                out.append(_remote(src, dst, sems, 3 * w + j, (chip[0], chip[1], c)))
        return out

    def start(self, in_refs, out_refs, sems):
        for cp in self._copies(in_refs, out_refs, sems, False):
            cp.start()

    def finish(self, in_refs, out_refs, sems):
        for cp in self._copies(in_refs, out_refs, sems, True):
            cp.wait_recv()
        for cp in self._copies(in_refs, out_refs, sems, False):
            cp.wait_send()


def _carry(comm, n_in, n_out, first, last, refs):
    if comm is None:
        return refs[:n_in], refs[n_in:n_in + n_out], (lambda: None)
    a, b = len(comm.ins), len(comm.out_shape)
    ins, c_in = refs[:n_in], refs[n_in:n_in + a]
    outs, c_out = refs[n_in + a:n_in + a + n_out], refs[n_in + a + n_out:n_in + a + n_out + b]
    sems = refs[n_in + a + n_out + b:n_in + a + n_out + b + 2]
    pl.when(first)(lambda: comm.start(c_in, c_out, sems))
    return ins, outs, (lambda: pl.when(last)(lambda: comm.finish(c_in, c_out, sems)))


def _sbl_fwd(q, k, v, *, width, hd, name, tq=256, comm=None):
    (qa, qo), (ka, ko), (va, vo) = q, k, v
    S = qa.shape[0]
    tq = _tile(S, tq)
    tk = tq
    scale = hd ** -0.5
    n_g, n_q = width // LANES, S // tq

    def body(*refs):
        qi = pl.program_id(1)
        gi = pl.program_id(0)
        got = _carry(comm, 3, 2, (gi == 0) & (qi == 0), (gi == n_g - 1) & (qi == n_q - 1), refs)
        (q_ref, k_ref, v_ref), (o_ref, tot_ref) = got[0], got[1]
        masks = _lane_masks(hd, tq)
        qs = q_ref[...].astype(F32) * scale
        qm = [_keep(qs, m).astype(BF16) for m in masks]
        strict = lax.broadcasted_iota(jnp.int32, (tq, tk), 1) < lax.broadcasted_iota(jnp.int32, (tq, tk), 0)
        later = _tri("after", tk)

        def tile(kb, carry, diag):
            ks = pl.multiple_of(kb * tk, tk)
            kv = k_ref[pl.ds(ks, tk), :].astype(BF16)
            vv = v_ref[pl.ds(ks, tk), :].astype(BF16)
            out = []
            for hh in range(len(masks)):
                acc, c_pos = carry[2 * hh], carry[2 * hh + 1]
                pos, ls = _softplus_parts(_dot(qm[hh], kv, _NT))
                if diag:
                    pos = jnp.where(strict, pos, 0.0)
                pos_after, pos_all = _scan_cols(pos, later, True)
                w = jnp.exp(ls - (pos_after + c_pos))
                if diag:
                    w = jnp.where(strict, w, 0.0)
                out += [acc + _dot(w.astype(BF16), vv), c_pos + pos_all]
            return tuple(out)

        init = (jnp.zeros((tq, LANES), F32), jnp.zeros((tq, 1), F32)) * len(masks)
        carry = tile(qi, init, True)
        carry = lax.fori_loop(0, qi, lambda i, c: tile(qi - 1 - i, c, False), carry)
        o_ref[...] = _merge(carry[0::2], masks).astype(o_ref.dtype)
        tot_ref[...] = _merge([jnp.broadcast_to(-c, (tq, LANES)) for c in carry[1::2]], masks)
        got[2]()

    c_ins, c_in_specs, c_out_specs, c_out_shape, c_scratch = _comm_args(comm)
    return pl.pallas_call(
        body, name=name, grid=(n_g, n_q),
        in_specs=[_cols(tq, qo), _cols_all(S, ko), _cols_all(S, vo)] + c_in_specs,
        out_specs=[_cols(tq, 0), _cols(tq, 0)] + c_out_specs,
        out_shape=[jax.ShapeDtypeStruct((S, width), BF16), jax.ShapeDtypeStruct((S, width), F32)] + c_out_shape,
        scratch_shapes=c_scratch,
        compiler_params=_params(("arbitrary", "arbitrary")),
    )(qa, ka, va, *c_ins)


def _comm_args(comm):
    if comm is None:
        return [], [], [], [], []
    return comm.ins, [_ANY] * len(comm.ins), [_ANY] * len(comm.out_shape), comm.out_shape, comm.scratch


def _sbl_bwd(q, k, v, do, tot, *, width, hd, name, tq=256, comm=None):
    (qa, qo), (ka, ko), (va, vo) = q, k, v
    S = qa.shape[0]
    tq = _tile(S, tq)
    tk = tq
    scale = hd ** -0.5
    n_g, n_q = width // LANES, S // tq

    def body(*refs):
        qi = pl.program_id(1)
        gi = pl.program_id(0)
        got = _carry(comm, 5, 3, (gi == 0) & (qi == 0), (gi == n_g - 1) & (qi == n_q - 1), refs)
        (q_ref, k_ref, v_ref, do_ref, tot_ref), (dq_ref, dk_ref, dv_ref) = got[0], got[1]

        @pl.when(qi == 0)
        def _():
            dk_ref[...] = jnp.zeros_like(dk_ref)
            dv_ref[...] = jnp.zeros_like(dv_ref)

        masks = _lane_masks(hd, tq)
        qs = q_ref[...].astype(F32) * scale
        qm = [_keep(qs, m).astype(BF16) for m in masks]
        dov = [_keep(do_ref[...], m).astype(BF16) for m in masks]
        rest = [-_row_value(tot_ref[...], m) for m in masks]
        strict = lax.broadcasted_iota(jnp.int32, (tq, tk), 1) < lax.broadcasted_iota(jnp.int32, (tq, tk), 0)
        later, before = _tri("after", tk), _tri("before", tk)

        def tile(kb, carry, diag):
            ks = pl.multiple_of(kb * tk, tk)
            kv = k_ref[pl.ds(ks, tk), :].astype(BF16)
            vv = v_ref[pl.ds(ks, tk), :].astype(BF16)
            out = []
            dk_t, dv_t = None, None
            for hh in range(len(masks)):
                dq, c_pos, c_g = carry[3 * hh:3 * hh + 3]
                pos, ls = _softplus_parts(_dot(qm[hh], kv, _NT))
                if diag:
                    pos = jnp.where(strict, pos, 0.0)
                pos_after, pos_all = _scan_cols(pos, later, True)
                c_pos = c_pos + pos_all
                w = jnp.exp(ls - (pos_after + (rest[hh] - c_pos)))
                if diag:
                    w = jnp.where(strict, w, 0.0)
                g = _dot(dov[hh], vv, _NT) * w
                g_before, g_all = _scan_cols(g, before, False)
                g_before = g_before + c_g
                dz = g - jnp.exp(ls) * (g + g_before)
                if diag:
                    dz = jnp.where(strict, dz, 0.0)
                dzb = dz.astype(BF16)
                dk_h = _dot(dzb, qm[hh], _TN)
                dv_h = _dot(w.astype(BF16), dov[hh], _TN)
                dk_t = dk_h if dk_t is None else dk_t + dk_h
                dv_t = dv_h if dv_t is None else dv_t + dv_h
                out += [dq + _dot(dzb, kv), c_pos, c_g + g_all]
            dk_ref[pl.ds(ks, tk), :] += dk_t
            dv_ref[pl.ds(ks, tk), :] += dv_t
            return tuple(out)

        zero = jnp.zeros((tq, 1), F32)
        init = (jnp.zeros((tq, LANES), F32), zero, zero) * len(masks)
        carry = lax.fori_loop(0, qi, lambda kb, c: tile(kb, c, False), init)
        carry = tile(qi, carry, True)
        dq_ref[...] = _merge(carry[0::3], masks) * scale
        got[2]()

    full = jax.ShapeDtypeStruct((S, width), F32)
    c_ins, c_in_specs, c_out_specs, c_out_shape, c_scratch = _comm_args(comm)
    return pl.pallas_call(
        body, name=name, grid=(n_g, n_q),
        in_specs=[_cols(tq, qo), _cols_all(S, ko), _cols_all(S, vo), _cols(tq, do[1]), _cols(tq, tot[1])] + c_in_specs,
        out_specs=[_cols(tq, 0), _cols_all(S, 0), _cols_all(S, 0)] + c_out_specs,
        out_shape=[full, full, full] + c_out_shape,
        scratch_shapes=c_scratch,
        compiler_params=_params(("arbitrary", "arbitrary")),
    )(qa, ka, va, do[0], tot[0], *c_ins)


def _sml_fwd(q, k, v, bias=None, *, width, hd, causal, name, tq=256, tk=256, comm=None):
    (qa, qo), (ka, ko), (va, vo) = q, k, v
    S, Sk = qa.shape[0], ka.shape[0]
    tq, tk = _tile(S, tq), _tile(Sk, tk)
    if causal:
        assert tq == tk and S == Sk
    nk = Sk // tk
    hpg = LANES // hd
    scale = hd ** -0.5
    has_bias = bias is not None
    n_g, n_q = width // LANES, S // tq

    def body(*all_refs):
        qi, gi = pl.program_id(1), pl.program_id(0)
        got = _carry(comm, 5 if has_bias else 3, 2, (gi == 0) & (qi == 0), (gi == n_g - 1) & (qi == n_q - 1), all_refs)
        refs = tuple(got[0]) + tuple(got[1])
        q_ref, k_ref, v_ref = refs[:3]
        o_ref, lse_ref = refs[-2:]
        masks = _lane_masks(hd, tq)
        qs = q_ref[...].astype(F32) * scale
        qm = [_keep(qs, m).astype(BF16) for m in masks]
        allowed = lax.broadcasted_iota(jnp.int32, (tq, tk), 1) <= lax.broadcasted_iota(jnp.int32, (tq, tk), 0)

        def tile(kb, carry, diag):
            ks = pl.multiple_of(kb * tk, tk)
            kv = k_ref[pl.ds(ks, tk), :].astype(BF16)
            vv = v_ref[pl.ds(ks, tk), :].astype(BF16)
            out = []
            for hh in range(hpg):
                m, l, acc = carry[3 * hh:3 * hh + 3]
                z = _dot(qm[hh], kv, _NT)
                if has_bias:
                    z = z + refs[3][hh] - refs[4][hh, kb]
                if diag:
                    z = jnp.where(allowed, z, NEG_INF)
                m2 = jnp.maximum(m, jnp.max(z, axis=1, keepdims=True))
                p = jnp.exp(z - m2)
                alpha = jnp.exp(m - m2)
                out += [m2, alpha * l + jnp.sum(p, axis=1, keepdims=True), alpha * acc + _dot(p.astype(BF16), vv)]
            return tuple(out)

        init = (jnp.full((tq, 1), NEG_INF, F32), jnp.zeros((tq, 1), F32), jnp.zeros((tq, LANES), F32)) * hpg
        if causal:
            carry = lax.fori_loop(0, qi, lambda kb, c: tile(kb, c, False), init)
            carry = tile(qi, carry, True)
        else:
            carry = lax.fori_loop(0, nk, lambda kb, c: tile(kb, c, False), init)
        o_ref[...] = _merge([acc / l for l, acc in zip(carry[1::3], carry[2::3])], masks).astype(o_ref.dtype)
        lse_ref[...] = _merge([jnp.broadcast_to(m + jnp.log(l), (tq, LANES)) for m, l in zip(carry[0::3], carry[1::3])], masks)
        got[2]()

    in_specs = [_cols(tq, qo), _cols_all(Sk, ko), _cols_all(Sk, vo)]
    ins = [qa, ka, va]
    if has_bias:
        in_specs += [pl.BlockSpec((hpg, tq, 1), lambda g, i: (g, i, 0)),
                     pl.BlockSpec((hpg, nk, 1, tk), lambda g, i: (g, 0, 0, 0))]
        ins += list(bias)
    c_ins, c_in_specs, c_out_specs, c_out_shape, c_scratch = _comm_args(comm)
    return pl.pallas_call(
        body, name=name, grid=(n_g, n_q),
        in_specs=in_specs + c_in_specs, out_specs=[_cols(tq, 0), _cols(tq, 0)] + c_out_specs,
        out_shape=[jax.ShapeDtypeStruct((S, width), BF16), jax.ShapeDtypeStruct((S, width), F32)] + c_out_shape,
        scratch_shapes=c_scratch,
        compiler_params=_params(("arbitrary", "arbitrary") if comm is not None else ("parallel", "arbitrary")),
    )(*ins, *c_ins)


def _sml_bwd(q, k, v, o, lse, do, bias=None, *, width, hd, causal, name, tq=256, tk=256, comm=None):
    (qa, qo), (ka, ko), (va, vo) = q, k, v
    S, Sk = qa.shape[0], ka.shape[0]
    tq, tk = _tile(S, tq), _tile(Sk, tk)
    nk = Sk // tk
    hpg = LANES // hd
    scale = hd ** -0.5
    has_bias = bias is not None
    n_in = 8 if has_bias else 6
    n_g, n_q = width // LANES, S // tq

    def body(*all_refs):
        qi, gi = pl.program_id(1), pl.program_id(0)
        got = _carry(comm, n_in, 5 if has_bias else 3, (gi == 0) & (qi == 0), (gi == n_g - 1) & (qi == n_q - 1), all_refs)
        refs = tuple(got[0]) + tuple(got[1])
        q_ref, k_ref, v_ref, o_ref, lse_ref, do_ref = refs[:6]
        dq_ref, dk_ref, dv_ref = refs[n_in:n_in + 3]

        @pl.when(qi == 0)
        def _():
            dk_ref[...] = jnp.zeros_like(dk_ref)
            dv_ref[...] = jnp.zeros_like(dv_ref)
            if has_bias:
                refs[n_in + 4][...] = jnp.zeros_like(refs[n_in + 4])

        masks = _lane_masks(hd, tq)
        qs = q_ref[...].astype(F32) * scale
        qm = [_keep(qs, m).astype(BF16) for m in masks]
        do32 = do_ref[...]
        dov = [_keep(do32, m).astype(BF16) for m in masks]
        prod = do32 * o_ref[...].astype(F32)
        delta = [jnp.sum(_keep(prod, m), axis=1, keepdims=True) for m in masks]
        lses = [_row_value(lse_ref[...], m) for m in masks]
        allowed = lax.broadcasted_iota(jnp.int32, (tq, tk), 1) <= lax.broadcasted_iota(jnp.int32, (tq, tk), 0)

        def tile(kb, carry, diag):
            ks = pl.multiple_of(kb * tk, tk)
            kv = k_ref[pl.ds(ks, tk), :].astype(BF16)
            vv = v_ref[pl.ds(ks, tk), :].astype(BF16)
            out = []
            dk_t, dv_t = None, None
            for hh in range(hpg):
                dq, db_row = carry[2 * hh:2 * hh + 2]
                z = _dot(qm[hh], kv, _NT)
                if has_bias:
                    z = z + refs[6][hh] - refs[7][hh, kb]
                p = jnp.exp(z - lses[hh])
                if diag:
                    p = jnp.where(allowed, p, 0.0)
                dz = p * (_dot(dov[hh], vv, _NT) - delta[hh])
                dzb = dz.astype(BF16)
                dk_h = _dot(dzb, qm[hh], _TN)
                dv_h = _dot(p.astype(BF16), dov[hh], _TN)
                dk_t = dk_h if dk_t is None else dk_t + dk_h
                dv_t = dv_h if dv_t is None else dv_t + dv_h
                if has_bias:
                    db_row = db_row + jnp.sum(dz, axis=1, keepdims=True)
                    refs[n_in + 4][hh, kb] += jnp.sum(dz, axis=0, keepdims=True)
                out += [dq + _dot(dzb, kv), db_row]
            dk_ref[pl.ds(ks, tk), :] += dk_t
            dv_ref[pl.ds(ks, tk), :] += dv_t
            return tuple(out)

        init = (jnp.zeros((tq, LANES), F32), jnp.zeros((tq, 1), F32)) * hpg
        if causal:
            carry = lax.fori_loop(0, qi, lambda kb, c: tile(kb, c, False), init)
            carry = tile(qi, carry, True)
        else:
            carry = lax.fori_loop(0, nk, lambda kb, c: tile(kb, c, False), init)
        dq_ref[...] = _merge(carry[0::2], masks) * scale
        if has_bias:
            for hh in range(hpg):
                refs[n_in + 3][hh] = carry[2 * hh + 1]
        got[2]()

    in_specs = [_cols(tq, qo), _cols_all(Sk, ko), _cols_all(Sk, vo), _cols(tq, o[1]), _cols(tq, lse[1]), _cols(tq, do[1])]
    ins = [qa, ka, va, o[0], lse[0], do[0]]
    out_specs = [_cols(tq, 0), _cols_all(Sk, 0), _cols_all(Sk, 0)]
    out_shape = [jax.ShapeDtypeStruct((S, width), F32), jax.ShapeDtypeStruct((Sk, width), F32),
                 jax.ShapeDtypeStruct((Sk, width), F32)]
    if has_bias:
        rspec = pl.BlockSpec((hpg, tq, 1), lambda g, i: (g, i, 0))
        cspec = pl.BlockSpec((hpg, nk, 1, tk), lambda g, i: (g, 0, 0, 0))
        in_specs += [rspec, cspec]
        ins += list(bias)
        out_specs += [rspec, cspec]
        n_heads = width // hd
        out_shape += [jax.ShapeDtypeStruct((n_heads, S, 1), F32), jax.ShapeDtypeStruct((n_heads, nk, 1, tk), F32)]
    c_ins, c_in_specs, c_out_specs, c_out_shape, c_scratch = _comm_args(comm)
    return pl.pallas_call(
        body, name=name, grid=(n_g, n_q),
        in_specs=in_specs + c_in_specs, out_specs=out_specs + c_out_specs, out_shape=out_shape + c_out_shape,
        scratch_shapes=c_scratch,
        compiler_params=_params(("arbitrary", "arbitrary") if comm is not None else ("parallel", "arbitrary")),
    )(*ins, *c_ins)


def _head_sums(t, masks):
    sums = [jnp.sum(_keep(t, m), axis=1, keepdims=True) for m in masks]
    return _merge([jnp.broadcast_to(s, t.shape) for s in sums], masks) if len(masks) > 1 else sums[0]


def _hnorm_fwd(x, g_lanes, *, width, hd, name, tr=512):
    xa, xo = x
    R = xa.shape[0]
    tr = _tile(R, tr, align=16)
    n_blk = width // LANES

    def body(x_ref, g_ref, o_ref):
        masks = _lane_masks(hd, tr)
        for j in range(n_blk):
            sl = slice(j * LANES, (j + 1) * LANES)
            xv = x_ref[:, sl].astype(F32)
            r = lax.rsqrt(_head_sums(xv * xv, masks) * (1.0 / hd) + EPS)
            o_ref[:, sl] = (xv * r * g_ref[...]).astype(o_ref.dtype)

    assert (xo * LANES) % width == 0
    return pl.pallas_call(
        body, name=name, grid=(R // tr,),
        in_specs=[pl.BlockSpec((tr, width), lambda i: (i, xo * LANES // width)), pl.BlockSpec((1, LANES), lambda i: (0, 0))],
        out_specs=pl.BlockSpec((tr, width), lambda i: (i, 0)),
        out_shape=jax.ShapeDtypeStruct((R, width), BF16),
        compiler_params=_params(("parallel",)),
    )(xa, g_lanes)


def _hnorm_bwd(x, g_lanes, dy, *, width, hd, name, tr=512):
    xa, xo = x
    R = xa.shape[0]
    tr = _tile(R, tr, align=16)
    n_blk = width // LANES

    def body(x_ref, g_ref, dy_ref, dx_ref, dg_ref):
        masks = _lane_masks(hd, tr)
        dg = jnp.zeros((1, LANES), F32)
        for j in range(n_blk):
            sl = slice(j * LANES, (j + 1) * LANES)
            xv = x_ref[:, sl].astype(F32)
            dyv = dy_ref[:, sl].astype(F32)
            r = lax.rsqrt(_head_sums(xv * xv, masks) * (1.0 / hd) + EPS)
            xh = xv * r
            dyg = dyv * g_ref[...]
            c = _head_sums(dyg * xh, masks) * (1.0 / hd)
            dx_ref[:, sl] = (r * (dyg - xh * c)).astype(dx_ref.dtype)
            dg = dg + jnp.sum(dyv * xh, axis=0, keepdims=True)
        if hd * 2 == LANES:
            dg8 = jnp.broadcast_to(dg, (8, LANES))
            dg = (dg8 + pltpu.roll(dg8, shift=hd, axis=1))[0:1]
        else:
            assert hd == LANES

        @pl.when(pl.program_id(0) == 0)
        def _():
            dg_ref[...] = jnp.zeros_like(dg_ref)

        dg_ref[...] += dg

    assert (xo * LANES) % width == 0
    return pl.pallas_call(
        body, name=name, grid=(R // tr,),
        in_specs=[pl.BlockSpec((tr, width), lambda i: (i, xo * LANES // width)), pl.BlockSpec((1, LANES), lambda i: (0, 0)),
                  pl.BlockSpec((tr, width), lambda i: (i, 0))],
        out_specs=[pl.BlockSpec((tr, width), lambda i: (i, 0)), pl.BlockSpec((1, LANES), lambda i: (0, 0))],
        out_shape=[jax.ShapeDtypeStruct((R, width), BF16), jax.ShapeDtypeStruct((1, LANES), F32)],
        compiler_params=_params(("arbitrary",)),
    )(xa, g_lanes, dy)


def _split3_dot(x, tri):
    a = x.astype(BF16)
    r = x - a.astype(F32)
    b = r.astype(BF16)
    c = (r - b.astype(F32)).astype(BF16)
    return _dot(a, tri) + _dot(b, tri) + _dot(c, tri)


def _forget_fwd(logit_t, b_col, *, name, blk=512):
    H, S = logit_t.shape
    blk = _tile(S, blk)

    def body(l_ref, b_ref, f_ref):
        r_i = lax.broadcasted_iota(jnp.int32, (blk, blk), 0)
        c_i = lax.broadcasted_iota(jnp.int32, (blk, blk), 1)
        upto = (r_i <= c_i).astype(BF16)
        carry = jnp.zeros((H, 1), F32)
        for j in range(S // blk):
            u = l_ref[:, j * blk:(j + 1) * blk] + b_ref[...]
            lf, _ = _log_sigmoid_pair(u)
            f_ref[:, j * blk:(j + 1) * blk] = _split3_dot(lf, upto) + carry
            carry = carry + jnp.sum(lf, axis=1, keepdims=True)

    return pl.pallas_call(
        body, name=name,
        out_shape=jax.ShapeDtypeStruct((H, S), F32),
        compiler_params=pltpu.CompilerParams(vmem_limit_bytes=VMEM_LIMIT),
    )(logit_t, b_col)


def _forget_bwd(logit_t, b_col, d_f, *, name, blk=512):
    H, S = logit_t.shape
    blk = _tile(S, blk)

    def body(l_ref, b_ref, df_ref, dl_ref, db_ref):
        r_i = lax.broadcasted_iota(jnp.int32, (blk, blk), 0)
        c_i = lax.broadcasted_iota(jnp.int32, (blk, blk), 1)
        fromon = (r_i >= c_i).astype(BF16)
        carry = jnp.zeros((H, 1), F32)
        db = jnp.zeros((H, 1), F32)
        for j in reversed(range(S // blk)):
            sl = slice(j * blk, (j + 1) * blk)
            dfv = df_ref[:, sl]
            d_lf = _split3_dot(dfv, fromon) + carry
            carry = carry + jnp.sum(dfv, axis=1, keepdims=True)
            u = l_ref[:, sl] + b_ref[...]
            _, lsn = _log_sigmoid_pair(u)
            dl = d_lf * jnp.exp(lsn)
            dl_ref[:, sl] = dl
            db = db + jnp.sum(dl, axis=1, keepdims=True)
        db_ref[...] = db

    return pl.pallas_call(
        body, name=name,
        out_shape=[jax.ShapeDtypeStruct((H, S), F32), jax.ShapeDtypeStruct((H, 1), F32)],
        compiler_params=pltpu.CompilerParams(vmem_limit_bytes=VMEM_LIMIT),
    )(logit_t, b_col, d_f)


def _sigmoid(t):
    return 1.0 / (1.0 + jnp.exp(-t))


def _gate_fwd(o3, w3, proj, D, *, name, tm=256):
    S = proj.shape[0]
    tm = _tile(S, tm)

    def body(o0, o1, o2, w0, w1, w2, g0, g1, g2, out_ref):
        acc = None
        for o_ref, w_ref, g_ref in ((o0, w0, g0), (o1, w1, g1), (o2, w2, g2)):
            t = _sigmoid(g_ref[...]) * _dot(o_ref[...], w_ref[...])
            acc = t if acc is None else acc + t
        out_ref[...] = acc.astype(out_ref.dtype)

    ospec = lambda d: pl.BlockSpec((tm, d), lambda i: (i, 0))
    wspec = lambda w: pl.BlockSpec(w.shape, lambda i: (0, 0))
    gspec = lambda j: pl.BlockSpec((tm, D), lambda i: (i, j))
    return pl.pallas_call(
        body, name=name, grid=(S // tm,),
        in_specs=[ospec(o.shape[1]) for o in o3] + [wspec(w) for w in w3] + [gspec(j) for j in range(3)],
        out_specs=pl.BlockSpec((tm, D), lambda i: (i, 0)),
        out_shape=jax.ShapeDtypeStruct((S, D), BF16),
        compiler_params=_params(("parallel",)),
    )(*o3, *w3, proj, proj, proj)


def _gate_bwd(o3, w3, proj, dmerged, D, *, name, tm=256):
    S = proj.shape[0]
    tm = _tile(S, tm)

    def body(o0, o1, o2, w0, w1, w2, g0, g1, g2, dm_ref, dg_ref, db0, db1, db2, do0, do1, do2):
        dm = dm_ref[...]
        for j, (o_ref, w_ref, g_ref, db_ref, do_ref) in enumerate(
                ((o0, w0, g0, db0, do0), (o1, w1, g1, db1, do1), (o2, w2, g2, db2, do2))):
            s = _sigmoid(g_ref[...])
            br = _dot(o_ref[...], w_ref[...])
            dg_ref[:, j * D:(j + 1) * D] = (dm * br * s * (1.0 - s)).astype(dg_ref.dtype)
            dbr = (dm * s).astype(BF16)
            db_ref[...] = dbr
            do_ref[...] = _dot(dbr, w_ref[...], _NT)

    ospec = lambda d: pl.BlockSpec((tm, d), lambda i: (i, 0))
    wspec = lambda w: pl.BlockSpec(w.shape, lambda i: (0, 0))
    gspec = lambda j: pl.BlockSpec((tm, D), lambda i: (i, j))
    dspec = pl.BlockSpec((tm, D), lambda i: (i, 0))
    return pl.pallas_call(
        body, name=name, grid=(S // tm,),
        in_specs=[ospec(o.shape[1]) for o in o3] + [wspec(w) for w in w3] + [gspec(j) for j in range(3)] + [dspec],
        out_specs=[pl.BlockSpec((tm, 3 * D), lambda i: (i, 0))] + [dspec] * 3 + [ospec(o.shape[1]) for o in o3],
        out_shape=[jax.ShapeDtypeStruct((S, 3 * D), BF16)] + [jax.ShapeDtypeStruct((S, D), BF16)] * 3
        + [jax.ShapeDtypeStruct((S, o.shape[1]), F32) for o in o3],
        compiler_params=_params(("parallel",)),
    )(*o3, *w3, proj, proj, proj, dmerged)


def _loss_sum(dy, D, *, name):
    R, C = dy.shape
    tr = _row_tile(R, C, 2)

    def body(dy_ref, out_ref):
        @pl.when(pl.program_id(0) == 0)
        def _():
            out_ref[...] = jnp.zeros_like(out_ref)

        v = dy_ref[...]
        out_ref[...] += (0.5 * D) * jnp.sum(v * v)

    return pl.pallas_call(
        body, name=name, grid=(R // tr,),
        in_specs=[pl.BlockSpec((tr, C), lambda i: (i, 0))],
        out_specs=pl.BlockSpec((8, 128), lambda i: (0, 0)),
        out_shape=jax.ShapeDtypeStruct((8, 128), F32),
        compiler_params=_params(("arbitrary",)),
    )(dy)[0, 0]


def _adamw_math(w, g, m, v):
    m2 = ADAM_B1 * m + (1.0 - ADAM_B1) * g
    v2 = ADAM_B2 * v + (1.0 - ADAM_B2) * (g * g)
    m_hat = m2 / (1.0 - ADAM_B1 ** ADAM_STEP)
    v_hat = v2 / (1.0 - ADAM_B2 ** ADAM_STEP)
    delta = -ADAM_LR * (m_hat / (jnp.sqrt(v_hat) + ADAM_EPS) + ADAM_WD * w)
    return delta, m2, v2


def _adamw(w, g, m, v, *, name):
    return _ew(_adamw_math, [w, g, m, v], (F32, F32, F32), name=name)


def _adamw_small(w, parts, m, v, *, name):
    n = parts.shape[0]

    def body(w_ref, p_ref, m_ref, v_ref, g_ref, d_ref, m2_ref, v2_ref):
        g = p_ref[0]
        for i in range(1, n):
            g = g + p_ref[i]
        g_ref[...] = g
        d_ref[...], m2_ref[...], v2_ref[...] = _adamw_math(w_ref[...], g, m_ref[...], v_ref[...])

    shp = jax.ShapeDtypeStruct(w.shape, F32)
    return pl.pallas_call(body, name=name, out_shape=[shp] * 4)(w, parts, m, v)


_ANY = pl.BlockSpec(memory_space=pl.ANY)


def _mesh_place():
    x, y, c = lax.axis_index("x"), lax.axis_index("y"), lax.axis_index("c")
    chips = [(1 - x, y), (x, 1 - y), (1 - x, 1 - y)]
    return x, y, c, chips


def _remote(src, dst, sems, i, to):
    send_sems, recv_sems = sems
    return pltpu.make_async_remote_copy(src_ref=src, dst_ref=dst, send_sem=send_sems.at[i], recv_sem=recv_sems.at[i],
                                        device_id=to, device_id_type=MESH_ID)


def _gather_weights(shards, *, name):
    n = len(shards)

    def body(*refs):
        ins, outs = refs[:n], refs[n:2 * n]
        sems = refs[2 * n:2 * n + 2]
        x, y, c, chips = _mesh_place()
        me = 2 * x + y
        sibling = (x, y, 1 - c)
        sent = []
        for w in range(n):
            for j, chip in enumerate(chips):
                cp = _remote(ins[w].at[c], outs[w].at[me, c], sems, 6 * w + j, (chip[0], chip[1], c))
                cp.start()
                sent.append(cp)
        for w in range(n):
            for j, chip in enumerate(chips):
                got = outs[w].at[2 * chip[0] + chip[1], c]
                _remote(got, got, sems, 6 * w + j, sibling).wait_recv()
                cp = _remote(got, got, sems, 6 * w + 3 + j, sibling)
                cp.start()
                sent.append(cp)
        for w in range(n):
            for j, chip in enumerate(chips):
                got = outs[w].at[2 * chip[0] + chip[1], 1 - c]
                _remote(got, got, sems, 6 * w + 3 + j, sibling).wait_recv()
        for cp in sent:
            cp.wait_send()

    outs = pl.pallas_call(
        body, name=name,
        in_specs=[_ANY] * n, out_specs=[_ANY] * n,
        out_shape=[jax.ShapeDtypeStruct((N_CHIPS,) + s.shape, s.dtype) for s in shards],
        scratch_shapes=[pltpu.SemaphoreType.DMA((6 * n,)), pltpu.SemaphoreType.DMA((6 * n,))],
    )(*shards)
    me = 2 * lax.axis_index("x") + lax.axis_index("y")
    return [lax.dynamic_update_index_in_dim(o, s, me, 0) for o, s in zip(outs, shards)]


def _forward_halves(gathered, *, name):
    n = len(gathered)

    def body(*refs):
        ins, outs = refs[:n], refs[n:2 * n]
        sems = refs[2 * n:2 * n + 2]
        x, y, c, chips = _mesh_place()
        sibling = (x, y, 1 - c)
        sent = []
        for w in range(n):
            for j, chip in enumerate(chips):
                peer = 2 * chip[0] + chip[1]
                cp = _remote(ins[w].at[peer, c], outs[w].at[peer, c], sems, 3 * w + j, sibling)
                cp.start()
                sent.append(cp)
        for w in range(n):
            for j, chip in enumerate(chips):
                land = outs[w].at[2 * chip[0] + chip[1], 1 - c]
                _remote(land, land, sems, 3 * w + j, sibling).wait_recv()
        for cp in sent:
            cp.wait_send()

    return pl.pallas_call(
        body, name=name,
        in_specs=[_ANY] * n, out_specs=[_ANY] * n,
        out_shape=[jax.ShapeDtypeStruct(g.shape, g.dtype) for g in gathered],
        input_output_aliases={w: w for w in range(n)},
        scratch_shapes=[pltpu.SemaphoreType.DMA((3 * n,)), pltpu.SemaphoreType.DMA((3 * n,))],
    )(*gathered)


def _exchange_siblings(grads, *, name):
    n = len(grads)

    def body(*refs):
        ins, got = refs[:n], refs[n:2 * n]
        sems = refs[2 * n:2 * n + 2]
        x, y, c, _ = _mesh_place()
        sibling = (x, y, 1 - c)
        sent = []
        for w in range(n):
            for s in range(N_CHIPS):
                cp = _remote(ins[w].at[s, 1 - c], got[w].at[s], sems, N_CHIPS * w + s, sibling)
                cp.start()
                sent.append(cp)
        for w in range(n):
            for s in range(N_CHIPS):
                _remote(got[w].at[s], got[w].at[s], sems, N_CHIPS * w + s, sibling).wait_recv()
        for cp in sent:
            cp.wait_send()

    n_sem = N_CHIPS * n
    got = pl.pallas_call(
        body, name=name,
        in_specs=[_ANY] * n, out_specs=[_ANY] * n,
        out_shape=[jax.ShapeDtypeStruct((N_CHIPS,) + g.shape[2:], g.dtype) for g in grads],
        scratch_shapes=[pltpu.SemaphoreType.DMA((n_sem,)), pltpu.SemaphoreType.DMA((n_sem,))],
    )(*grads)
    c = lax.axis_index("c")
    return [lax.dynamic_index_in_dim(g, c, 1, keepdims=False) for g in grads], got


def _exchange_chips(parts, *, name):
    n = len(parts)

    def body(*refs):
        ins, got = refs[:n], refs[n:2 * n]
        sems = refs[2 * n:2 * n + 2]
        x, y, c, chips = _mesh_place()
        sent = []
        for w in range(n):
            for j, chip in enumerate(chips):
                cp = _remote(ins[w].at[2 * chip[0] + chip[1]], got[w].at[j], sems, 3 * w + j, (chip[0], chip[1], c))
                cp.start()
                sent.append(cp)
        for w in range(n):
            for j in range(3):
                _remote(got[w].at[j], got[w].at[j], sems, 3 * w + j, (x, y, c)).wait_recv()
        for cp in sent:
            cp.wait_send()

    got = pl.pallas_call(
        body, name=name,
        in_specs=[_ANY] * n, out_specs=[_ANY] * n,
        out_shape=[jax.ShapeDtypeStruct((3,) + p.shape[1:], p.dtype) for p in parts],
        scratch_shapes=[pltpu.SemaphoreType.DMA((3 * n,)), pltpu.SemaphoreType.DMA((3 * n,))],
    )(*parts)
    me = 2 * lax.axis_index("x") + lax.axis_index("y")
    return [lax.dynamic_index_in_dim(p, me, 0, keepdims=False) for p in parts], got


def _share_halves(halves, small):
    n = len(halves)

    def body(*refs):
        ins, small_ref = refs[:n], refs[n]
        outs, small_out = refs[n + 1:2 * n + 1], refs[2 * n + 1]
        sems = refs[2 * n + 2:2 * n + 4]
        x, y, c, chips = _mesh_place()
        sibling = (x, y, 1 - c)
        me = 4 * x + 2 * y + c
        sent = [_remote(ins[w], outs[w].at[c], sems, w, sibling) for w in range(n)]
        peers = [sibling] + [(ch[0], ch[1], cc) for ch in chips for cc in (c, 1 - c)]
        sent += [_remote(small_ref, small_out.at[me], sems, n + j, peer) for j, peer in enumerate(peers)]
        for cp in sent:
            cp.start()
        for w in range(n):
            _remote(outs[w].at[1 - c], outs[w].at[1 - c], sems, w, sibling).wait_recv()
        for j, peer in enumerate(peers):
            frm = small_out.at[4 * peer[0] + 2 * peer[1] + peer[2]]
            _remote(frm, frm, sems, n + j, peer).wait_recv()
        for cp in sent:
            cp.wait_send()

    n_sem = n + 7
    outs = pl.pallas_call(
        body, name="share_halves",
        in_specs=[_ANY] * (n + 1), out_specs=[_ANY] * (n + 1),
        out_shape=[jax.ShapeDtypeStruct((2,) + h.shape, h.dtype) for h in halves]
        + [jax.ShapeDtypeStruct((8,) + small.shape, small.dtype)],
        scratch_shapes=[pltpu.SemaphoreType.DMA((n_sem,)), pltpu.SemaphoreType.DMA((n_sem,))],
    )(*halves, small)
    c = lax.axis_index("c")
    me = 4 * lax.axis_index("x") + 2 * lax.axis_index("y") + c
    return ([lax.dynamic_update_index_in_dim(o, h, c, 0) for o, h in zip(outs[:n], halves)],
            lax.dynamic_update_index_in_dim(outs[n], small, me, 0))


EARLY = ("w_ff_down", "w_ff_up", "w_out", "w_branch_sb", "w_branch_fox", "w_branch_mem")


def _split(outs, n):
    outs = list(outs) if isinstance(outs, (list, tuple)) else [outs]
    return outs[:n], outs[n:]


def _local_step(x, mem, target, small, W, gather_rest=None, reduce_early=None, reduce_late=None):
    S, D = x.shape
    o_gate, o_qkv, o_mq, o_f = 0, 3 * D, 3 * D + 2 * 3 * D_SB, 3 * D + 2 * 3 * D_SB + D_MEM
    tq = 512

    g_comms, finish_weights = gather_rest if gather_rest is not None else ([None] * 3, None)
    h = _rmsnorm_fwd(x, small["g_mix_norm"], BF16, name="mix_norm")
    (proj,), landed = _split(_mm(h, W["w_in"], name="in_proj", tb=True, tn=768, comm=g_comms[0]), 1)
    blk = lambda j: (proj, (o_qkv + j * D_SB) // LANES)
    sb_q, sb_k, sb_v, fx_q, fx_k, fx_v = [blk(j) for j in range(6)]
    m_q = (proj, o_mq // LANES)
    f_logit_t = proj[:, o_f:o_f + FOX_HEADS].T
    b_col = small["b_forget"].reshape(FOX_HEADS, 1)
    lanes = lambda g: jnp.tile(g, (1, LANES // g.shape[1]))
    g_fq, g_fk, g_mq, g_mk = [lanes(small[k]) for k in ("g_fox_q", "g_fox_k", "g_mem_q", "g_mem_k")]

    (o_sb, sb_tot), more = _split(_sbl_fwd(sb_q, sb_k, sb_v, width=D_SB, hd=HD, name="sb_fwd", tq=tq, comm=g_comms[1]), 2)
    landed += more

    fq = _hnorm_fwd(fx_q, g_fq, width=D_FOX, hd=HD, name="fox_q_norm")
    fk = _hnorm_fwd(fx_k, g_fk, width=D_FOX, hd=HD, name="fox_k_norm")
    f_cum = _forget_fwd(f_logit_t, b_col, name="forget_fwd")
    tkf = _tile(S, tq)
    f_bias = (f_cum.reshape(FOX_HEADS, S, 1), f_cum.reshape(FOX_HEADS, S // tkf, 1, tkf))
    (o_fox, fox_lse), more = _split(_sml_fwd((fq, 0), (fk, 0), fx_v, f_bias, width=D_FOX, hd=HD, causal=True,
                                             name="fox_fwd", tq=tq, tk=tq, comm=g_comms[2]), 2)
    landed += more
    if finish_weights is not None:
        W = {**W, **finish_weights(landed)}

    mh = _rmsnorm_fwd(mem, small["g_mem_norm"], BF16, name="mem_norm")
    mkv = _mm(mh, W["w_mem_kv"], name="mem_kv")
    mv = (mkv, D_MEM // LANES)
    mq = _hnorm_fwd(m_q, g_mq, width=D_MEM, hd=MEM_HD, name="mem_q_norm")
    mk = _hnorm_fwd((mkv, 0), g_mk, width=D_MEM, hd=MEM_HD, name="mem_k_norm")
    o_mem, mem_lse = _sml_fwd((mq, 0), (mk, 0), mv, width=D_MEM, hd=MEM_HD, causal=False, name="mem_fwd", tq=tq, tk=256)

    o3 = [o_sb, o_fox, o_mem]
    w3 = [W["w_branch_sb"], W["w_branch_fox"], W["w_branch_mem"]]
    merged = _gate_fwd(o3, w3, proj, D, name="gate_fwd")
    x1 = _mm(merged, W["w_out"], name="out_proj", extras=(x,), epilogue=lambda acc, res: (res + acc,))
    h2 = _rmsnorm_fwd(x1, small["g_mlp_norm"], BF16, name="mlp_norm")

    def relu2(acc):
        u = jnp.maximum(acc, 0.0)
        return u, u * u

    u, a = _mm(h2, W["w_ff_up"], name="ff_up", out_dtypes=(BF16, BF16), epilogue=relu2)
    def head(acc, res, tgt):
        d = (res + acc - tgt) * (1.0 / D)
        return d, d

    dy, dy16 = _mm(a, W["w_ff_down"], name="ff_down", extras=(x1, target), out_dtypes=(F32, BF16), epilogue=head, tm=512)
    loss = _loss_sum(dy, D, name="loss")

    G = {}
    du = _mm(dy16, W["w_ff_down"], name="d_ff_act", tb=True, out_dtypes=(BF16,), extras=(u,),
             epilogue=lambda acc, uu: (acc * (2.0 * uu.astype(F32)),))
    G["w_ff_down"] = _mm(a, dy16, name="d_w_ff_down", ta=True, out_dtypes=(BF16,))
    G["w_ff_up"] = _mm(h2, du, name="d_w_ff_up", ta=True, out_dtypes=(BF16,))
    dh2 = _mm(du, W["w_ff_up"], name="d_mlp_in", tb=True)
    dx1, dg_mlp = _rmsnorm_bwd(x1, small["g_mlp_norm"], dh2, add=dy, name="d_mlp_norm")
    dmerged = _mm(dx1, W["w_out"], name="d_merged", tb=True)
    G["w_out"] = _mm(merged, dx1, name="d_w_out", ta=True, out_dtypes=(BF16,))
    dgate, db0, db1, db2, do_sb, do_fox, do_mem = _gate_bwd(o3, w3, proj, dmerged, D, name="gate_bwd")
    for nm, o, db in zip(("w_branch_sb", "w_branch_fox", "w_branch_mem"), o3, (db0, db1, db2)):
        G[nm] = _mm(o, db, name="d_" + nm, ta=True, out_dtypes=(BF16,))

    r_comms, r_finish = reduce_early({k: G.pop(k) for k in EARLY}) if reduce_early is not None else ([None] * 2, None)
    dsb, landed_sb = _split(_sbl_bwd(sb_q, sb_k, sb_v, (do_sb, 0), (sb_tot, 0), width=D_SB, hd=HD, name="sb_bwd", tq=tq,
                                     comm=r_comms[0]), 3)
    (dfq, dfk, dfv, df_row, df_col), landed_fox = _split(
        _sml_bwd((fq, 0), (fk, 0), fx_v, (o_fox, 0), (fox_lse, 0), (do_fox, 0), f_bias, width=D_FOX, hd=HD, causal=True,
                 name="fox_bwd", tq=tq, tk=tq, comm=r_comms[1]), 5)
    early = r_finish(landed_sb, landed_fox) if r_finish is not None else {}
    dfx_q, dg_fox_q = _hnorm_bwd(fx_q, g_fq, dfq, width=D_FOX, hd=HD, name="d_fox_q_norm")
    dfx_k, dg_fox_k = _hnorm_bwd(fx_k, g_fk, dfk, width=D_FOX, hd=HD, name="d_fox_k_norm")
    d_fcum = df_row.reshape(FOX_HEADS, S) - df_col.reshape(FOX_HEADS, S)
    d_flogit_t, db_forget = _forget_bwd(f_logit_t, b_col, d_fcum, name="forget_bwd")

    dmq_n, dmk_n, dmv = _sml_bwd((mq, 0), (mk, 0), mv, (o_mem, 0), (mem_lse, 0), (do_mem, 0), width=D_MEM, hd=MEM_HD,
                                 causal=False, name="mem_bwd", tq=tq, tk=256)
    dm_q, dg_mem_q = _hnorm_bwd(m_q, g_mq, dmq_n, width=D_MEM, hd=MEM_HD, name="d_mem_q_norm")
    dmk_raw, dg_mem_k = _hnorm_bwd((mkv, 0), g_mk, dmk_n, width=D_MEM, hd=MEM_HD, name="d_mem_k_norm")
    dmkv = jnp.concatenate([dmk_raw, dmv.astype(BF16)], axis=1)
    G["w_mem_kv"] = _mm(mh, dmkv, name="d_w_mem_kv", ta=True, out_dtypes=(BF16,))
    dmh = _mm(dmkv, W["w_mem_kv"], name="d_mem_h", tb=True)
    _, dg_mem = _rmsnorm_bwd(mem, small["g_mem_norm"], dmh, name="d_mem_norm")
    dg_fox_q, dg_fox_k = dg_fox_q[:, :HD], dg_fox_k[:, :HD]

    dproj = jnp.concatenate(
        [dgate] + [t.astype(BF16) for t in (*dsb, dfx_q, dfx_k, dfv, dm_q)]
        + [d_flogit_t.T.astype(BF16), jnp.zeros((S, F_PAD - FOX_HEADS), BF16)], axis=1)
    g_w_in = _mm(dproj, h, name="d_w_in", ta=True, out_dtypes=(BF16,), tm=768)
    if reduce_late is None:
        G["w_in"] = g_w_in
        dh = _mm(dproj, W["w_in"], name="d_mix_in", tk=2304)
    else:
        comm, finish = reduce_late({"w_in": g_w_in, "w_mem_kv": G.pop("w_mem_kv")})
        dh, *landed = _mm(dproj, W["w_in"], name="d_mix_in", tk=2304, comm=comm)
        early.update(finish(landed))
    grad_x, dg_mix = _rmsnorm_bwd(x, small["g_mix_norm"], dh, add=dx1, name="d_mix_norm")

    small_grads = dict(g_mix_norm=dg_mix, g_mem_norm=dg_mem, b_forget=db_forget.reshape(1, FOX_HEADS),
                       g_fox_q=dg_fox_q, g_fox_k=dg_fox_k, g_mem_q=dg_mem_q, g_mem_k=dg_mem_k, g_mlp_norm=dg_mlp)
    return loss, grad_x, G, small_grads, early


BIG = ("w_in", "w_mem_kv", "w_branch_sb", "w_branch_fox", "w_branch_mem", "w_out", "w_ff_up", "w_ff_down")
COLUMN_SHARDED = ("w_in", "w_branch_sb", "w_branch_fox", "w_branch_mem", "w_ff_up")
SMALL = ("g_mix_norm", "g_mem_norm", "b_forget", "g_fox_q", "g_fox_k", "g_mem_q", "g_mem_k", "g_mlp_norm")
ORDER = ("g_mix_norm", "g_mem_norm", "w_in", "b_forget", "g_fox_q", "g_fox_k", "g_mem_q", "g_mem_k", "w_mem_kv",
         "w_branch_sb", "w_branch_fox", "w_branch_mem", "w_out", "g_mlp_norm", "w_ff_up", "w_ff_down")


def _unshard(name, gathered):
    n, _, rh, c = gathered.shape
    t = gathered.reshape(n, 2 * rh, c)
    if name in COLUMN_SHARDED:
        return t.transpose(1, 0, 2).reshape(2 * rh, n * c)
    return t.reshape(n * 2 * rh, c)


def _reshard(name, full):
    if name in COLUMN_SHARDED:
        r, c = full.shape
        t = full.reshape(r, N_CHIPS, c // N_CHIPS).transpose(1, 0, 2)
    else:
        r, c = full.shape[0] // N_CHIPS, full.shape[1]
        t = full.reshape(N_CHIPS, r, c)
    return t.reshape(N_CHIPS, 2, t.shape[1] // 2, t.shape[2])


ROW_TILE = 16
IN_BUF_ALIGN = 256


def _in_segments(D):
    n_qkv = 6 * D_SB
    o_mq, o_gate = n_qkv + FOX_HEADS, n_qkv + FOX_HEADS + D_MEM
    return [(0, n_qkv, 3 * D), (n_qkv, o_mq, 3 * D + n_qkv + D_MEM), (o_mq, o_gate, 3 * D + n_qkv), (o_gate, o_gate + 3 * D, 0)]


class _InLayout:
    def __init__(self, D, shard, n):
        self.D, self.shard, self.n = D, shard, n
        down = lambda v: v // ROW_TILE * ROW_TILE
        up = lambda v: -(-v // ROW_TILE) * ROW_TILE
        self.pieces = []
        ends = []
        for s in range(n):
            cursor, mine = 0, []
            for a, b, p in _in_segments(D):
                x0, x1 = max(a, s * shard), min(b, (s + 1) * shard)
                if x0 < x1:
                    p0 = p + x0 - a
                    rows = up(p0 + x1 - x0) - down(p0)
                    mine.append((x0 - s * shard, x1 - x0, p0, cursor, rows))
                    cursor += rows
            self.pieces.append(mine)
            ends.append(cursor)
        self.rows = -(-max(ends) // IN_BUF_ALIGN) * IN_BUF_ALIGN
        self.padded_rows = 3 * D + 6 * D_SB + D_MEM + F_PAD

    def _per_shard(self, fn, chip, operand):
        return lax.switch(chip, [functools.partial(fn, s) for s in range(self.n)], operand)

    def pack(self, chip, rows):
        def one(s, t):
            out, at = [], 0
            for x0, n_rows, p0, start, region in self.pieces[s]:
                lead = p0 % ROW_TILE
                out += [jnp.zeros((start + lead - at, t.shape[1]), t.dtype), t[x0:x0 + n_rows]]
                at = start + lead + n_rows
            return jnp.concatenate(out + [jnp.zeros((self.rows - at, t.shape[1]), t.dtype)], axis=0)
        return self._per_shard(one, chip, rows)

    def unpack(self, chip, buf, pad_to):
        def one(s, t):
            out = [t[start + p0 % ROW_TILE:start + p0 % ROW_TILE + n_rows] for _, n_rows, p0, start, _ in self.pieces[s]]
            return jnp.concatenate(out + [jnp.zeros((pad_to - self.shard, t.shape[1]), t.dtype)], axis=0)
        return self._per_shard(one, chip, buf)

    def to_padded(self, bufs):
        runs = sorted((p0, s, start, region) for s in range(self.n) for _, _, p0, start, region in self.pieces[s])
        chunks, end = [], 0
        for p0, s, start, region in runs:
            d0 = p0 // ROW_TILE * ROW_TILE
            src = bufs[s, start:start + region]
            if d0 < end:
                assert end - d0 == ROW_TILE
                last = chunks.pop()
                chunks += [last[:-ROW_TILE], last[-ROW_TILE:] + src[:ROW_TILE], src[ROW_TILE:]]
            else:
                if d0 > end:
                    chunks.append(jnp.zeros((d0 - end, bufs.shape[2]), bufs.dtype))
                chunks.append(src)
            end = d0 + region
        chunks.append(jnp.zeros((self.padded_rows - end, bufs.shape[2]), bufs.dtype))
        return jnp.concatenate(chunks, axis=0)

    def from_padded(self, gp):
        bufs = []
        for s in range(self.n):
            out, at = [], 0
            for _, n_rows, p0, start, region in self.pieces[s]:
                d0 = p0 // ROW_TILE * ROW_TILE
                row = d0 + lax.broadcasted_iota(jnp.int32, (region, 1), 0)
                out.append(jnp.where((row >= p0) & (row < p0 + n_rows), gp[d0:d0 + region], jnp.zeros((), gp.dtype)))
                at = start + region
            bufs.append(jnp.concatenate(out + [jnp.zeros((self.rows - at, gp.shape[1]), gp.dtype)], axis=0))
        return jnp.stack(bufs)


def _pack_small(vals):
    width = max(vals[k].shape[1] for k in SMALL)
    return jnp.concatenate([jnp.pad(vals[k].astype(F32), ((0, 0), (0, width - vals[k].shape[1]))) for k in SMALL], axis=0)


def _unpack_small(packed, like):
    return {k: packed[i:i + 1, :like[k].shape[1]] for i, k in enumerate(SMALL)}


def kernel(x, mem, g_mix_norm, g_mem_norm, w_in, b_forget, g_fox_q, g_fox_k, g_mem_q, g_mem_k, w_mem_kv, w_branch_sb, w_branch_fox, w_branch_mem, w_out, g_mlp_norm, w_ff_up, w_ff_down, loss_target, m_g_mix_norm, m_g_mem_norm, m_w_in, m_b_forget, m_g_fox_q, m_g_fox_k, m_g_mem_q, m_g_mem_k, m_w_mem_kv, m_w_branch_sb, m_w_branch_fox, m_w_branch_mem, m_w_out, m_g_mlp_norm, m_w_ff_up, m_w_ff_down, v_g_mix_norm, v_g_mem_norm, v_w_in, v_b_forget, v_g_fox_q, v_g_fox_k, v_g_mem_q, v_g_mem_k, v_w_mem_kv, v_w_branch_sb, v_w_branch_fox, v_w_branch_mem, v_w_out, v_g_mlp_norm, v_w_ff_up, v_w_ff_down):
    given = dict(locals())
    D = x.shape[-1]
    weights = {k: given[k] for k in ORDER}
    moms = {k: given["m_" + k] for k in ORDER}
    vars_ = {k: given["v_" + k] for k in ORDER}

    me_chip = 2 * lax.axis_index("x") + lax.axis_index("y")

    n_in = w_in.shape[2]
    lay = _InLayout(D, n_in, N_CHIPS)
    transposed = lambda t: jnp.transpose(t[0])
    shards = {}
    for k in BIG:
        w = weights[k][0].astype(BF16)
        if k == "w_in":
            w = lay.pack(me_chip, jnp.transpose(w))
        shards[k] = w.reshape(2, w.shape[0] // 2, w.shape[1])
    gathered_in = _gather_weights([shards["w_in"]], name="gather_w_in")[0]
    W = {"w_in": lay.to_padded(gathered_in.reshape(N_CHIPS, lay.rows, D))}
    carried = (("w_ff_up",), ("w_ff_down", "w_mem_kv"), ("w_branch_sb", "w_branch_fox", "w_branch_mem", "w_out"))
    rest = [k for grp in carried for k in grp]
    assert sorted(rest + ["w_in"]) == sorted(BIG)

    def finish_weights(landed):
        full = _forward_halves(landed, name="forward_halves")
        full = [lax.dynamic_update_index_in_dim(o, shards[k], me_chip, 0) for k, o in zip(rest, full)]
        return {k: _unshard(k, g) for k, g in zip(rest, full)}

    def sum_pairs(names, own, sib):
        add2 = lambda p, q: (p.astype(F32) + q.astype(F32),)
        out = []
        for k, p, q in zip(names, own, sib):
            n, r, c = p.shape
            out.append(_ew(add2, [p.reshape(n * r, c), q.reshape(n * r, c)], (BF16,), name="sum_pair_" + k).reshape(n, r, c))
        return out

    def sum_chips(names, parts, got):
        add4 = lambda p, q0, q1, q2: (((p.astype(F32) + q0.astype(F32)) + q1.astype(F32)) + q2.astype(F32),)
        own = [lax.dynamic_index_in_dim(p, me_chip, 0, keepdims=False) for p in parts]
        return {k: _ew(add4, [p, q[0], q[1], q[2]], (F32,), name="sum_chips_" + k) for k, p, q in zip(names, own, got)}

    def pair_sums(grads, tag):
        names = list(grads)
        stacked = {k: _reshard(k, grads[k]) for k in names if k != "w_in"}
        if "w_in" in grads:
            stacked["w_in"] = lay.from_padded(grads["w_in"]).reshape(N_CHIPS, 2, lay.rows // 2, D)
        own, sib = _exchange_siblings([stacked[k] for k in names], name="exchange_siblings_" + tag)
        return dict(zip(names, sum_pairs(names, own, sib)))

    def reduce_early(grads):
        parts = pair_sums(grads, "early")
        groups = [[k for k in parts if k in ("w_ff_down", "w_ff_up")], [k for k in parts if k not in ("w_ff_down", "w_ff_up")]]
        comms = [_ChipExchange("scatter", [parts[k] for k in grp]) for grp in groups]

        def finish(*landed):
            out = {}
            for grp, got in zip(groups, landed):
                out.update(sum_chips(grp, [parts[k] for k in grp], got))
            return out
        return comms, finish

    def reduce_late(grads):
        parts = pair_sums(grads, "late")
        names = list(parts)
        return _ChipExchange("scatter", [parts[k] for k in names]), functools.partial(sum_chips, names, [parts[k] for k in names])

    small = {k: weights[k] for k in SMALL}
    loss_part, grad_x, G, small_grads, halves = _local_step(
        x[0], mem[0], loss_target[0], small, W,
        gather_rest=([_ChipExchange("gather", [shards[k] for k in grp]) for grp in carried], finish_weights),
        reduce_early=reduce_early, reduce_late=reduce_late)
    assert not G, list(G)
    reduced, small_parts = _share_halves([halves[k] for k in BIG], _pack_small(small_grads))

    grads, deltas, new_m, new_v = {}, {}, {}, {}
    for k, g in zip(BIG, reduced):
        shp = weights[k].shape
        if k == "w_in":
            g2 = lay.unpack(me_chip, g.reshape(lay.rows, D), lay.rows)
            padded = lambda t: jnp.pad(transposed(t), ((0, lay.rows - n_in), (0, 0)))
            outs = _adamw(padded(weights[k]), g2, padded(moms[k]), padded(vars_[k]), name="adamw_" + k)
            g2, d, m2, v2 = [jnp.transpose(t[:n_in]) for t in (g2, *outs)]
        else:
            g2 = g.reshape(shp[1], shp[2])
            d, m2, v2 = _adamw(weights[k][0], g2, moms[k][0], vars_[k][0], name="adamw_" + k)
        grads[k], deltas[k], new_m[k], new_v[k] = g2.reshape(shp), d.reshape(shp), m2.reshape(shp), v2.reshape(shp)
    sg, sd, sm, sv = _adamw_small(_pack_small(small), small_parts, _pack_small({k: moms[k] for k in SMALL}),
                                  _pack_small({k: vars_[k] for k in SMALL}), name="adamw_small")
    for dst, packed in ((grads, sg), (deltas, sd), (new_m, sm), (new_v, sv)):
        dst.update(_unpack_small(packed, small))

    loss = lax.psum(loss_part, ("x", "y", "c"))
    return (loss, grad_x[None], *[grads[k] for k in ORDER], *[deltas[k] for k in ORDER],
            *[new_m[k] for k in ORDER], *[new_v[k] for k in ORDER])
```

```python
import functools

import jax
import jax.numpy as jnp
from jax import lax
from jax.experimental import pallas as pl
from jax.experimental.pallas import tpu as pltpu

F32 = jnp.float32
BF16 = jnp.bfloat16
MESH_ID = pl.DeviceIdType.MESH

HD = 64
SB_HEADS = 8
FOX_HEADS = 8
MEM_HEADS = 4
MEM_HD = 128
D_SB = SB_HEADS * HD
D_FOX = FOX_HEADS * HD
D_MEM = MEM_HEADS * MEM_HD
EPS = 1e-6
NEG_INF = -1e30

ADAM_LR = 0.001
ADAM_B1 = 0.9
ADAM_B2 = 0.999
ADAM_EPS = 1e-08
ADAM_WD = 0.01
ADAM_STEP = 10

N_CHIPS = 4
VMEM_LIMIT = 56 * 1024 * 1024

F_PAD = 256


def _tile(n, target, align=128):
    if n <= target:
        return n
    best = None
    t = align
    while t <= target:
        if n % t == 0:
            best = t
        t += align
    assert best is not None, (n, target, align)
    return best


def _params(sem):
    return pltpu.CompilerParams(dimension_semantics=sem, vmem_limit_bytes=VMEM_LIMIT)


def _mm(a, b, *, name, ta=False, tb=False, out_dtypes=(F32,), epilogue=None, extras=(),
        tm=1024, tn=1024, tk=2048, comm=None):
    if ta:
        K, M = a.shape
    else:
        M, K = a.shape
    if tb:
        N, K2 = b.shape
    else:
        K2, N = b.shape
    assert K == K2, (a.shape, b.shape, ta, tb)
    tm, tn, tk = _tile(M, tm), _tile(N, tn), _tile(K, tk)
    nk = K // tk
    n_extra, n_out = len(extras), len(out_dtypes)
    if epilogue is None:
        epilogue = lambda acc: (acc,)
    dims = (((0 if ta else 1,), (1 if tb else 0,)), ((), ()))

    gm, gn = M // tm, N // tn

    def body(*refs):
        i, j, k = pl.program_id(0), pl.program_id(1), pl.program_id(2)
        got = _carry(comm, 2 + n_extra, n_out, (i == 0) & (j == 0) & (k == 0),
                     (i == gm - 1) & (j == gn - 1) & (k == nk - 1), refs)
        (a_ref, b_ref, *extra_refs), out_refs = got[0], got[1]
        part = lax.dot_general(a_ref[...].astype(BF16), b_ref[...].astype(BF16), dims, preferred_element_type=F32)

        def finish(acc):
            outs = epilogue(acc, *[r[...] for r in extra_refs])
            for o_ref, o in zip(out_refs, outs):
                o_ref[...] = o.astype(o_ref.dtype)

        if nk == 1:
            finish(part)
        else:
            acc_ref = refs[-1]

            @pl.when(k == 0)
            def _():
                acc_ref[...] = part

            @pl.when((k > 0) & (k < nk - 1))
            def _():
                acc_ref[...] += part

            @pl.when(k == nk - 1)
            def _():
                finish(acc_ref[...] + part)

        got[2]()

    a_spec = pl.BlockSpec((tk, tm), lambda i, j, k: (k, i)) if ta else pl.BlockSpec((tm, tk), lambda i, j, k: (i, k))
    b_spec = pl.BlockSpec((tn, tk), lambda i, j, k: (j, k)) if tb else pl.BlockSpec((tk, tn), lambda i, j, k: (k, j))
    mn_spec = pl.BlockSpec((tm, tn), lambda i, j, k: (i, j))
    c_ins, c_in_specs, c_out_specs, c_out_shape, c_scratch = _comm_args(comm)
    sem = ("parallel", "parallel", "arbitrary") if comm is None else ("arbitrary",) * 3
    outs = pl.pallas_call(
        body, name=name,
        grid=(gm, gn, nk),
        in_specs=[a_spec, b_spec] + [mn_spec] * n_extra + c_in_specs,
        out_specs=[mn_spec] * n_out + c_out_specs,
        out_shape=[jax.ShapeDtypeStruct((M, N), dt) for dt in out_dtypes] + c_out_shape,
        scratch_shapes=c_scratch + ([pltpu.VMEM((tm, tn), F32)] if nk > 1 else []),
        compiler_params=_params(sem),
    )(a, b, *extras, *c_ins)
    return outs if len(outs) > 1 else outs[0]


def _row_tile(rows, cols, n_arrays):
    budget = 10 * 1024 * 1024
    cols_padded = -(-cols // 128) * 128
    target = max(16, budget // (cols_padded * 4 * n_arrays * 2))
    return _tile(rows, target, align=16)


def _ew(fn, ins, out_dtypes, *, name):
    R, C = ins[0].shape
    n_in, n_out = len(ins), len(out_dtypes)
    tr = _row_tile(R, C, n_in + n_out)

    def body(*refs):
        outs = fn(*[r[...] for r in refs[:n_in]])
        for o_ref, o in zip(refs[n_in:], outs):
            o_ref[...] = o.astype(o_ref.dtype)

    spec = pl.BlockSpec((tr, C), lambda i: (i, 0))
    outs = pl.pallas_call(
        body, name=name, grid=(R // tr,),
        in_specs=[spec] * n_in, out_specs=[spec] * n_out,
        out_shape=[jax.ShapeDtypeStruct((R, C), dt) for dt in out_dtypes],
        compiler_params=_params(("parallel",)),
    )(*ins)
    return outs if n_out > 1 else outs[0]


def _rmsnorm_fwd(x, g, out_dtype, *, name):
    R, d = x.shape
    tr = _row_tile(R, d, 3)

    def body(x_ref, g_ref, o_ref):
        xv = x_ref[...].astype(F32)
        r = lax.rsqrt(jnp.mean(xv * xv, axis=-1, keepdims=True) + EPS)
        o_ref[...] = (xv * r * g_ref[...]).astype(o_ref.dtype)

    return pl.pallas_call(
        body, name=name, grid=(R // tr,),
        in_specs=[pl.BlockSpec((tr, d), lambda i: (i, 0)), pl.BlockSpec((1, d), lambda i: (0, 0))],
        out_specs=pl.BlockSpec((tr, d), lambda i: (i, 0)),
        out_shape=jax.ShapeDtypeStruct((R, d), out_dtype),
        compiler_params=_params(("parallel",)),
    )(x, g)


def _rmsnorm_bwd(x, g, dy, add=None, *, name):
    R, d = x.shape
    has_add = add is not None
    tr = _row_tile(R, d, 5)

    def body(*refs):
        x_ref, g_ref, dy_ref = refs[:3]
        add_ref = refs[3] if has_add else None
        dx_ref, dg_ref = refs[-2:]
        xv = x_ref[...].astype(F32)
        dyv = dy_ref[...].astype(F32)
        r = lax.rsqrt(jnp.mean(xv * xv, axis=-1, keepdims=True) + EPS)
        xh = xv * r
        dyg = dyv * g_ref[...]
        c = jnp.mean(dyg * xh, axis=-1, keepdims=True)
        dx = r * (dyg - xh * c)
        if has_add:
            dx = dx + add_ref[...]
        dx_ref[...] = dx

        @pl.when(pl.program_id(0) == 0)
        def _():
            dg_ref[...] = jnp.zeros_like(dg_ref)

        dg_ref[...] += jnp.sum(dyv * xh, axis=0, keepdims=True)

    row = pl.BlockSpec((tr, d), lambda i: (i, 0))
    vec = pl.BlockSpec((1, d), lambda i: (0, 0))
    ins = [x, g, dy] + ([add] if has_add else [])
    return pl.pallas_call(
        body, name=name, grid=(R // tr,),
        in_specs=[row, vec, row] + ([row] if has_add else []),
        out_specs=[row, vec],
        out_shape=[jax.ShapeDtypeStruct((R, d), F32), jax.ShapeDtypeStruct((1, d), F32)],
        compiler_params=_params(("arbitrary",)),
    )(*ins)


_NT = (((1,), (1,)), ((), ()))
_TN = (((0,), (0,)), ((), ()))


def _dot(a, b, dims=(((1,), (0,)), ((), ()))):
    return lax.dot_general(a, b, dims, preferred_element_type=F32)


def _split_dot(x, tri):
    hi = x.astype(BF16)
    lo = (x - hi.astype(F32)).astype(BF16)
    return _dot(hi, tri) + _dot(lo, tri)


def _log_sigmoid_pair(z):
    sp = jnp.log(1.0 + jnp.exp(-jnp.abs(z)))
    return jnp.minimum(z, 0.0) - sp, jnp.minimum(-z, 0.0) - sp


def _sb_fwd(q, k, v, *, name, tq=256):
    H, S, hd = q.shape
    tq = _tile(S, tq)
    tk = tq
    scale = hd ** -0.5

    def body(q_ref, k_ref, v_ref, o_ref, tot_ref):
        qi = pl.program_id(1)
        qv = q_ref[0]
        row = qi * tq + lax.broadcasted_iota(jnp.int32, (tq, tk), 0)
        col0 = lax.broadcasted_iota(jnp.int32, (tq, tk), 1)
        later = (lax.broadcasted_iota(jnp.int32, (tk, tk), 0) > lax.broadcasted_iota(jnp.int32, (tk, tk), 1)).astype(BF16)

        def step(i, carry):
            acc, c_rem = carry
            kb = qi - i
            ks = pl.multiple_of(kb * tk, tk)
            kv = k_ref[0, pl.ds(ks, tk), :]
            vv = v_ref[0, pl.ds(ks, tk), :]
            z = _dot(qv, kv, _NT) * scale
            mask = (col0 + ks) < row
            ls, lsn = _log_sigmoid_pair(z)
            rem = jnp.where(mask, lsn, 0.0)
            after = _split_dot(rem, later) + c_rem
            w = jnp.where(mask, jnp.exp(ls + after), 0.0)
            acc = acc + _dot(w.astype(BF16), vv)
            c_rem = c_rem + jnp.sum(rem, axis=1, keepdims=True)
            return acc, c_rem

        acc, c_rem = lax.fori_loop(0, qi + 1, step, (jnp.zeros((tq, hd), F32), jnp.zeros((tq, 1), F32)))
        o_ref[0] = acc
        tot_ref[0] = c_rem

    qspec = pl.BlockSpec((1, tq, hd), lambda h, i: (h, i, 0))
    kspec = pl.BlockSpec((1, S, hd), lambda h, i: (h, 0, 0))
    return pl.pallas_call(
        body, name=name, grid=(H, S // tq),
        in_specs=[qspec, kspec, kspec],
        out_specs=[qspec, pl.BlockSpec((1, tq, 1), lambda h, i: (h, i, 0))],
        out_shape=[jax.ShapeDtypeStruct((H, S, hd), F32), jax.ShapeDtypeStruct((H, S, 1), F32)],
        compiler_params=_params(("parallel", "arbitrary")),
    )(q, k, v)


def _sb_bwd(q, k, v, do, tot, *, name, tq=256):
    H, S, hd = q.shape
    tq = _tile(S, tq)
    tk = tq
    scale = hd ** -0.5

    def body(q_ref, k_ref, v_ref, do_ref, tot_ref, dq_ref, dk_ref, dv_ref):
        qi = pl.program_id(1)

        @pl.when(qi == 0)
        def _():
            dk_ref[...] = jnp.zeros_like(dk_ref)
            dv_ref[...] = jnp.zeros_like(dv_ref)

        qv = q_ref[0]
        dov = do_ref[0].astype(BF16)
        tot_v = tot_ref[0]
        row = qi * tq + lax.broadcasted_iota(jnp.int32, (tq, tk), 0)
        col0 = lax.broadcasted_iota(jnp.int32, (tq, tk), 1)
        r_i = lax.broadcasted_iota(jnp.int32, (tk, tk), 0)
        c_i = lax.broadcasted_iota(jnp.int32, (tk, tk), 1)
        upto = (r_i <= c_i).astype(BF16)
        before = (r_i < c_i).astype(BF16)

        def step(kb, carry):
            dq, c_rem, c_g = carry
            ks = pl.multiple_of(kb * tk, tk)
            kv = k_ref[0, pl.ds(ks, tk), :]
            vv = v_ref[0, pl.ds(ks, tk), :]
            z = _dot(qv, kv, _NT) * scale
            mask = (col0 + ks) < row
            ls, lsn = _log_sigmoid_pair(z)
            rem = jnp.where(mask, lsn, 0.0)
            after = tot_v - (_split_dot(rem, upto) + c_rem)
            w = jnp.where(mask, jnp.exp(ls + after), 0.0)
            dw = _dot(dov, vv, _NT)
            g = dw * w
            g_before = _split_dot(g, before) + c_g
            beta = jnp.exp(ls)
            dz = jnp.where(mask, g * (1.0 - beta) - beta * g_before, 0.0) * scale
            dzb = dz.astype(BF16)
            dq = dq + _dot(dzb, kv)
            dk_ref[0, pl.ds(ks, tk), :] += _dot(dzb, qv, _TN)
            dv_ref[0, pl.ds(ks, tk), :] += _dot(w.astype(BF16), dov, _TN)
            return dq, c_rem + jnp.sum(rem, axis=1, keepdims=True), c_g + jnp.sum(g, axis=1, keepdims=True)

        zero = jnp.zeros((tq, 1), F32)
        dq, _, _ = lax.fori_loop(0, qi + 1, step, (jnp.zeros((tq, hd), F32), zero, zero))
        dq_ref[0] = dq

    qspec = pl.BlockSpec((1, tq, hd), lambda h, i: (h, i, 0))
    kspec = pl.BlockSpec((1, S, hd), lambda h, i: (h, 0, 0))
    full = jax.ShapeDtypeStruct((H, S, hd), F32)
    return pl.pallas_call(
        body, name=name, grid=(H, S // tq),
        in_specs=[qspec, kspec, kspec, qspec, pl.BlockSpec((1, tq, 1), lambda h, i: (h, i, 0))],
        out_specs=[qspec, kspec, kspec],
        out_shape=[full, full, full],
        compiler_params=_params(("parallel", "arbitrary")),
    )(q, k, v, do, tot)


def _softmax_fwd(q, k, v, bias=None, *, name, causal, tq=256, tk=256):
    H, S, hd = q.shape
    Sk = k.shape[1]
    tq, tk = _tile(S, tq), _tile(Sk, tk)
    if causal:
        assert tq == tk and S == Sk
    nk = Sk // tk
    scale = hd ** -0.5
    has_bias = bias is not None

    def body(*refs):
        q_ref, k_ref, v_ref = refs[:3]
        o_ref, lse_ref = refs[-2:]
        qi = pl.program_id(1)
        qv = q_ref[0]
        row = qi * tq + lax.broadcasted_iota(jnp.int32, (tq, tk), 0)
        col0 = lax.broadcasted_iota(jnp.int32, (tq, tk), 1)
        if has_bias:
            b_row = refs[3][0]

        def step(kb, carry):
            m, l, acc = carry
            ks = pl.multiple_of(kb * tk, tk)
            kv = k_ref[0, pl.ds(ks, tk), :]
            vv = v_ref[0, pl.ds(ks, tk), :]
            z = _dot(qv, kv, _NT) * scale
            if has_bias:
                z = z + b_row - refs[4][0, kb]
            if causal:
                z = jnp.where((col0 + ks) <= row, z, NEG_INF)
            m2 = jnp.maximum(m, jnp.max(z, axis=1, keepdims=True))
            p = jnp.exp(z - m2)
            alpha = jnp.exp(m - m2)
            l = alpha * l + jnp.sum(p, axis=1, keepdims=True)
            acc = alpha * acc + _dot(p.astype(BF16), vv)
            return m2, l, acc

        init = (jnp.full((tq, 1), NEG_INF, F32), jnp.zeros((tq, 1), F32), jnp.zeros((tq, hd), F32))
        m, l, acc = lax.fori_loop(0, (qi + 1) if causal else nk, step, init)
        o_ref[0] = acc / l
        lse_ref[0] = m + jnp.log(l)

    qspec = pl.BlockSpec((1, tq, hd), lambda h, i: (h, i, 0))
    kspec = pl.BlockSpec((1, Sk, hd), lambda h, i: (h, 0, 0))
    vspec = pl.BlockSpec((1, tq, 1), lambda h, i: (h, i, 0))
    in_specs = [qspec, kspec, kspec]
    ins = [q, k, v]
    if has_bias:
        in_specs += [vspec, pl.BlockSpec((1, nk, 1, tk), lambda h, i: (h, 0, 0, 0))]
        ins += list(bias)
    return pl.pallas_call(
        body, name=name, grid=(H, S // tq),
        in_specs=in_specs, out_specs=[qspec, vspec],
        out_shape=[jax.ShapeDtypeStruct((H, S, hd), F32), jax.ShapeDtypeStruct((H, S, 1), F32)],
        compiler_params=_params(("parallel", "arbitrary")),
    )(*ins)


def _softmax_bwd(q, k, v, o, lse, do, bias=None, *, name, causal, tq=256, tk=256):
    H, S, hd = q.shape
    Sk = k.shape[1]
    tq, tk = _tile(S, tq), _tile(Sk, tk)
    nk = Sk // tk
    scale = hd ** -0.5
    has_bias = bias is not None
    n_in = 8 if has_bias else 6

    def body(*refs):
        q_ref, k_ref, v_ref, o_ref, lse_ref, do_ref = refs[:6]
        dq_ref, dk_ref, dv_ref = refs[n_in:n_in + 3]
        qi = pl.program_id(1)

        @pl.when(qi == 0)
        def _():
            dk_ref[...] = jnp.zeros_like(dk_ref)
            dv_ref[...] = jnp.zeros_like(dv_ref)
            if has_bias:
                refs[n_in + 4][...] = jnp.zeros_like(refs[n_in + 4])

        qv = q_ref[0]
        do32 = do_ref[0]
        dov = do32.astype(BF16)
        delta = jnp.sum(do32 * o_ref[0], axis=1, keepdims=True)
        lse_v = lse_ref[0]
        row = qi * tq + lax.broadcasted_iota(jnp.int32, (tq, tk), 0)
        col0 = lax.broadcasted_iota(jnp.int32, (tq, tk), 1)
        if has_bias:
            b_row = refs[6][0]

        def step(kb, carry):
            dq, db_row = carry
            ks = pl.multiple_of(kb * tk, tk)
            kv = k_ref[0, pl.ds(ks, tk), :]
            vv = v_ref[0, pl.ds(ks, tk), :]
            z = _dot(qv, kv, _NT) * scale
            if has_bias:
                z = z + b_row - refs[7][0, kb]
            p = jnp.exp(z - lse_v)
            if causal:
                p = jnp.where((col0 + ks) <= row, p, 0.0)
            dp = _dot(dov, vv, _NT)
            dz = p * (dp - delta)
            dzb = dz.astype(BF16)
            dq = dq + _dot(dzb, kv)
            dk_ref[0, pl.ds(ks, tk), :] += _dot(dzb, qv, _TN) * scale
            dv_ref[0, pl.ds(ks, tk), :] += _dot(p.astype(BF16), dov, _TN)
            if has_bias:
                db_row = db_row + jnp.sum(dz, axis=1, keepdims=True)
                refs[n_in + 4][0, kb] += jnp.sum(dz, axis=0, keepdims=True)
            return dq, db_row

        dq, db_row = lax.fori_loop(0, (qi + 1) if causal else nk, step,
                                   (jnp.zeros((tq, hd), F32), jnp.zeros((tq, 1), F32)))
        dq_ref[0] = dq * scale
        if has_bias:
            refs[n_in + 3][0] = db_row

    qspec = pl.BlockSpec((1, tq, hd), lambda h, i: (h, i, 0))
    kspec = pl.BlockSpec((1, Sk, hd), lambda h, i: (h, 0, 0))
    vspec = pl.BlockSpec((1, tq, 1), lambda h, i: (h, i, 0))
    cspec = pl.BlockSpec((1, nk, 1, tk), lambda h, i: (h, 0, 0, 0))
    in_specs = [qspec, kspec, kspec, qspec, vspec, qspec]
    ins = [q, k, v, o, lse, do]
    out_specs = [qspec, kspec, kspec]
    out_shape = [jax.ShapeDtypeStruct((H, S, hd), F32), jax.ShapeDtypeStruct((H, Sk, hd), F32),
                 jax.ShapeDtypeStruct((H, Sk, hd), F32)]
    if has_bias:
        in_specs += [vspec, cspec]
        ins += list(bias)
        out_specs += [vspec, cspec]
        out_shape += [jax.ShapeDtypeStruct((H, S, 1), F32), jax.ShapeDtypeStruct((H, nk, 1, tk), F32)]
    return pl.pallas_call(
        body, name=name, grid=(H, S // tq),
        in_specs=in_specs, out_specs=out_specs, out_shape=out_shape,
        compiler_params=_params(("parallel", "arbitrary")),
    )(*ins)


LANES = 128
_LOW = -3e38


def _lane_masks(hd, rows):
    if hd == LANES:
        return [None]
    lane = lax.broadcasted_iota(jnp.int32, (rows, LANES), 1)
    return [(lane >= hh * hd) & (lane < (hh + 1) * hd) for hh in range(LANES // hd)]


def _keep(t, m):
    return t if m is None else jnp.where(m, t, 0.0)


def _merge(parts, masks):
    out = parts[-1]
    for p, m in zip(parts[-2::-1], masks[-2::-1]):
        out = jnp.where(m, p, out)
    return out


def _row_value(t, m):
    return jnp.max(t if m is None else jnp.where(m, t, _LOW), axis=1, keepdims=True)


def _cols(tq, off):
    return pl.BlockSpec((tq, LANES), lambda g, i: (i, off + g))


def _cols_all(rows, off):
    return pl.BlockSpec((rows, LANES), lambda g, i: (0, off + g))


SCAN_BLOCK = 256


def _tri(kind, cols):
    n = min(SCAN_BLOCK, cols)
    r = lax.broadcasted_iota(jnp.int32, (n, n), 0)
    c = lax.broadcasted_iota(jnp.int32, (n, n), 1)
    return ((r > c) if kind == "after" else (r < c)).astype(BF16)


def _scan_cols(x, tri, reverse):
    cols = x.shape[1]
    cb = min(SCAN_BLOCK, cols)
    assert cols % cb == 0 and tri.shape == (cb, cb)
    nb = cols // cb
    blocks = [x[:, b * cb:(b + 1) * cb] for b in range(nb)]
    outs, carry = [None] * nb, None
    for b in (reversed(range(nb)) if reverse else range(nb)):
        y = _dot(blocks[b].astype(BF16), tri)
        outs[b] = y if carry is None else y + carry
        s = jnp.sum(blocks[b], axis=1, keepdims=True)
        carry = s if carry is None else carry + s
    return (outs[0] if nb == 1 else jnp.concatenate(outs, axis=1)), carry


def _softplus_parts(z):
    pos = jnp.maximum(z, 0.0) + jnp.log(1.0 + jnp.exp(-jnp.abs(z)))
    return pos, z - pos


class _ChipExchange:
    def __init__(self, kind, ins):
        assert kind in ("gather", "scatter")
        self.kind, self.ins = kind, list(ins)
        lead = (lambda s: (N_CHIPS,) + s) if kind == "gather" else (lambda s: (3,) + s[1:])
        self.out_shape = [jax.ShapeDtypeStruct(lead(a.shape), a.dtype) for a in ins]
        n = 3 * len(ins)
        self.scratch = [pltpu.SemaphoreType.DMA((n,)), pltpu.SemaphoreType.DMA((n,))]

    def _copies(self, in_refs, out_refs, sems, landing):
        x, y, c, chips = _mesh_place()
        me = 2 * x + y
        out = []
        for w in range(len(self.ins)):
            for j, chip in enumerate(chips):
                peer = 2 * chip[0] + chip[1]
                if self.kind == "gather":
                    src, dst, land = in_refs[w].at[c], out_refs[w].at[me, c], out_refs[w].at[peer, c]
                else:
                    src, dst, land = in_refs[w].at[peer], out_refs[w].at[j], out_refs[w].at[j]
                if landing:
                    src, dst = land, land
                out.append(_remote(src, dst, sems, 3 * w + j, (chip[0], chip[1], c)))
        return out

    def start(self, in_refs, out_refs, sems):
        for cp in self._copies(in_refs, out_refs, sems, False):
            cp.start()

    def finish(self, in_refs, out_refs, sems):
        for cp in self._copies(in_refs, out_refs, sems, True):
            cp.wait_recv()
        for cp in self._copies(in_refs, out_refs, sems, False):
            cp.wait_send()


def _carry(comm, n_in, n_out, first, last, refs):
    if comm is None:
        return refs[:n_in], refs[n_in:n_in + n_out], (lambda: None)
    a, b = len(comm.ins), len(comm.out_shape)
    ins, c_in = refs[:n_in], refs[n_in:n_in + a]
    outs, c_out = refs[n_in + a:n_in + a + n_out], refs[n_in + a + n_out:n_in + a + n_out + b]
    sems = refs[n_in + a + n_out + b:n_in + a + n_out + b + 2]
    pl.when(first)(lambda: comm.start(c_in, c_out, sems))
    return ins, outs, (lambda: pl.when(last)(lambda: comm.finish(c_in, c_out, sems)))


def _sbl_fwd(q, k, v, *, width, hd, name, tq=256, comm=None):
    (qa, qo), (ka, ko), (va, vo) = q, k, v
    S = qa.shape[0]
    tq = _tile(S, tq)
    tk = tq
    scale = hd ** -0.5
    n_g, n_q = width // LANES, S // tq

    def body(*refs):
        qi = pl.program_id(1)
        gi = pl.program_id(0)
        got = _carry(comm, 3, 2, (gi == 0) & (qi == 0), (gi == n_g - 1) & (qi == n_q - 1), refs)
        (q_ref, k_ref, v_ref), (o_ref, tot_ref) = got[0], got[1]
        masks = _lane_masks(hd, tq)
        qs = q_ref[...].astype(F32) * scale
        qm = [_keep(qs, m).astype(BF16) for m in masks]
        strict = lax.broadcasted_iota(jnp.int32, (tq, tk), 1) < lax.broadcasted_iota(jnp.int32, (tq, tk), 0)
        later = _tri("after", tk)

        def tile(kb, carry, diag):
            ks = pl.multiple_of(kb * tk, tk)
            kv = k_ref[pl.ds(ks, tk), :].astype(BF16)
            vv = v_ref[pl.ds(ks, tk), :].astype(BF16)
            out = []
            for hh in range(len(masks)):
                acc, c_pos = carry[2 * hh], carry[2 * hh + 1]
                pos, ls = _softplus_parts(_dot(qm[hh], kv, _NT))
                if diag:
                    pos = jnp.where(strict, pos, 0.0)
                pos_after, pos_all = _scan_cols(pos, later, True)
                w = jnp.exp(ls - (pos_after + c_pos))
                if diag:
                    w = jnp.where(strict, w, 0.0)
                out += [acc + _dot(w.astype(BF16), vv), c_pos + pos_all]
            return tuple(out)

        init = (jnp.zeros((tq, LANES), F32), jnp.zeros((tq, 1), F32)) * len(masks)
        carry = tile(qi, init, True)
        carry = lax.fori_loop(0, qi, lambda i, c: tile(qi - 1 - i, c, False), carry)
        o_ref[...] = _merge(carry[0::2], masks).astype(o_ref.dtype)
        tot_ref[...] = _merge([jnp.broadcast_to(-c, (tq, LANES)) for c in carry[1::2]], masks)
        got[2]()

    c_ins, c_in_specs, c_out_specs, c_out_shape, c_scratch = _comm_args(comm)
    return pl.pallas_call(
        body, name=name, grid=(n_g, n_q),
        in_specs=[_cols(tq, qo), _cols_all(S, ko), _cols_all(S, vo)] + c_in_specs,
        out_specs=[_cols(tq, 0), _cols(tq, 0)] + c_out_specs,
        out_shape=[jax.ShapeDtypeStruct((S, width), BF16), jax.ShapeDtypeStruct((S, width), F32)] + c_out_shape,
        scratch_shapes=c_scratch,
        compiler_params=_params(("arbitrary", "arbitrary")),
    )(qa, ka, va, *c_ins)


def _comm_args(comm):
    if comm is None:
        return [], [], [], [], []
    return comm.ins, [_ANY] * len(comm.ins), [_ANY] * len(comm.out_shape), comm.out_shape, comm.scratch


def _sbl_bwd(q, k, v, do, tot, *, width, hd, name, tq=256, comm=None):
    (qa, qo), (ka, ko), (va, vo) = q, k, v
    S = qa.shape[0]
    tq = _tile(S, tq)
    tk = tq
    scale = hd ** -0.5
    n_g, n_q = width // LANES, S // tq

    def body(*refs):
        qi = pl.program_id(1)
        gi = pl.program_id(0)
        got = _carry(comm, 5, 3, (gi == 0) & (qi == 0), (gi == n_g - 1) & (qi == n_q - 1), refs)
        (q_ref, k_ref, v_ref, do_ref, tot_ref), (dq_ref, dk_ref, dv_ref) = got[0], got[1]

        @pl.when(qi == 0)
        def _():
            dk_ref[...] = jnp.zeros_like(dk_ref)
            dv_ref[...] = jnp.zeros_like(dv_ref)

        masks = _lane_masks(hd, tq)
        qs = q_ref[...].astype(F32) * scale
        qm = [_keep(qs, m).astype(BF16) for m in masks]
        dov = [_keep(do_ref[...], m).astype(BF16) for m in masks]
        rest = [-_row_value(tot_ref[...], m) for m in masks]
        strict = lax.broadcasted_iota(jnp.int32, (tq, tk), 1) < lax.broadcasted_iota(jnp.int32, (tq, tk), 0)
        later, before = _tri("after", tk), _tri("before", tk)

        def tile(kb, carry, diag):
            ks = pl.multiple_of(kb * tk, tk)
            kv = k_ref[pl.ds(ks, tk), :].astype(BF16)
            vv = v_ref[pl.ds(ks, tk), :].astype(BF16)
            out = []
            dk_t, dv_t = None, None
            for hh in range(len(masks)):
                dq, c_pos, c_g = carry[3 * hh:3 * hh + 3]
                pos, ls = _softplus_parts(_dot(qm[hh], kv, _NT))
                if diag:
                    pos = jnp.where(strict, pos, 0.0)
                pos_after, pos_all = _scan_cols(pos, later, True)
                c_pos = c_pos + pos_all
                w = jnp.exp(ls - (pos_after + (rest[hh] - c_pos)))
                if diag:
                    w = jnp.where(strict, w, 0.0)
                g = _dot(dov[hh], vv, _NT) * w
                g_before, g_all = _scan_cols(g, before, False)
                g_before = g_before + c_g
                dz = g - jnp.exp(ls) * (g + g_before)
                if diag:
                    dz = jnp.where(strict, dz, 0.0)
                dzb = dz.astype(BF16)
                dk_h = _dot(dzb, qm[hh], _TN)
                dv_h = _dot(w.astype(BF16), dov[hh], _TN)
                dk_t = dk_h if dk_t is None else dk_t + dk_h
                dv_t = dv_h if dv_t is None else dv_t + dv_h
                out += [dq + _dot(dzb, kv), c_pos, c_g + g_all]
            dk_ref[pl.ds(ks, tk), :] += dk_t
            dv_ref[pl.ds(ks, tk), :] += dv_t
            return tuple(out)

        zero = jnp.zeros((tq, 1), F32)
        init = (jnp.zeros((tq, LANES), F32), zero, zero) * len(masks)
        carry = lax.fori_loop(0, qi, lambda kb, c: tile(kb, c, False), init)
        carry = tile(qi, carry, True)
        dq_ref[...] = _merge(carry[0::3], masks) * scale
        got[2]()

    full = jax.ShapeDtypeStruct((S, width), F32)
    c_ins, c_in_specs, c_out_specs, c_out_shape, c_scratch = _comm_args(comm)
    return pl.pallas_call(
        body, name=name, grid=(n_g, n_q),
        in_specs=[_cols(tq, qo), _cols_all(S, ko), _cols_all(S, vo), _cols(tq, do[1]), _cols(tq, tot[1])] + c_in_specs,
        out_specs=[_cols(tq, 0), _cols_all(S, 0), _cols_all(S, 0)] + c_out_specs,
        out_shape=[full, full, full] + c_out_shape,
        scratch_shapes=c_scratch,
        compiler_params=_params(("arbitrary", "arbitrary")),
    )(qa, ka, va, do[0], tot[0], *c_ins)


def _sml_fwd(q, k, v, bias=None, *, width, hd, causal, name, tq=256, tk=256, comm=None):
    (qa, qo), (ka, ko), (va, vo) = q, k, v
    S, Sk = qa.shape[0], ka.shape[0]
    tq, tk = _tile(S, tq), _tile(Sk, tk)
    if causal:
        assert tq == tk and S == Sk
    nk = Sk // tk
    hpg = LANES // hd
    scale = hd ** -0.5
    has_bias = bias is not None
    n_g, n_q = width // LANES, S // tq

    def body(*all_refs):
        qi, gi = pl.program_id(1), pl.program_id(0)
        got = _carry(comm, 5 if has_bias else 3, 2, (gi == 0) & (qi == 0), (gi == n_g - 1) & (qi == n_q - 1), all_refs)
        refs = tuple(got[0]) + tuple(got[1])
        q_ref, k_ref, v_ref = refs[:3]
        o_ref, lse_ref = refs[-2:]
        masks = _lane_masks(hd, tq)
        qs = q_ref[...].astype(F32) * scale
        qm = [_keep(qs, m).astype(BF16) for m in masks]
        allowed = lax.broadcasted_iota(jnp.int32, (tq, tk), 1) <= lax.broadcasted_iota(jnp.int32, (tq, tk), 0)

        def tile(kb, carry, diag):
            ks = pl.multiple_of(kb * tk, tk)
            kv = k_ref[pl.ds(ks, tk), :].astype(BF16)
            vv = v_ref[pl.ds(ks, tk), :].astype(BF16)
            out = []
            for hh in range(hpg):
                m, l, acc = carry[3 * hh:3 * hh + 3]
                z = _dot(qm[hh], kv, _NT)
                if has_bias:
                    z = z + refs[3][hh] - refs[4][hh, kb]
                if diag:
                    z = jnp.where(allowed, z, NEG_INF)
                m2 = jnp.maximum(m, jnp.max(z, axis=1, keepdims=True))
                p = jnp.exp(z - m2)
                alpha = jnp.exp(m - m2)
                out += [m2, alpha * l + jnp.sum(p, axis=1, keepdims=True), alpha * acc + _dot(p.astype(BF16), vv)]
            return tuple(out)

        init = (jnp.full((tq, 1), NEG_INF, F32), jnp.zeros((tq, 1), F32), jnp.zeros((tq, LANES), F32)) * hpg
        if causal:
            carry = lax.fori_loop(0, qi, lambda kb, c: tile(kb, c, False), init)
            carry = tile(qi, carry, True)
        else:
            carry = lax.fori_loop(0, nk, lambda kb, c: tile(kb, c, False), init)
        o_ref[...] = _merge([acc / l for l, acc in zip(carry[1::3], carry[2::3])], masks).astype(o_ref.dtype)
        lse_ref[...] = _merge([jnp.broadcast_to(m + jnp.log(l), (tq, LANES)) for m, l in zip(carry[0::3], carry[1::3])], masks)
        got[2]()

    in_specs = [_cols(tq, qo), _cols_all(Sk, ko), _cols_all(Sk, vo)]
    ins = [qa, ka, va]
    if has_bias:
        in_specs += [pl.BlockSpec((hpg, tq, 1), lambda g, i: (g, i, 0)),
                     pl.BlockSpec((hpg, nk, 1, tk), lambda g, i: (g, 0, 0, 0))]
        ins += list(bias)
    c_ins, c_in_specs, c_out_specs, c_out_shape, c_scratch = _comm_args(comm)
    return pl.pallas_call(
        body, name=name, grid=(n_g, n_q),
        in_specs=in_specs + c_in_specs, out_specs=[_cols(tq, 0), _cols(tq, 0)] + c_out_specs,
        out_shape=[jax.ShapeDtypeStruct((S, width), BF16), jax.ShapeDtypeStruct((S, width), F32)] + c_out_shape,
        scratch_shapes=c_scratch,
        compiler_params=_params(("arbitrary", "arbitrary") if comm is not None else ("parallel", "arbitrary")),
    )(*ins, *c_ins)


def _sml_bwd(q, k, v, o, lse, do, bias=None, *, width, hd, causal, name, tq=256, tk=256, comm=None):
    (qa, qo), (ka, ko), (va, vo) = q, k, v
    S, Sk = qa.shape[0], ka.shape[0]
    tq, tk = _tile(S, tq), _tile(Sk, tk)
    nk = Sk // tk
    hpg = LANES // hd
    scale = hd ** -0.5
    has_bias = bias is not None
    n_in = 8 if has_bias else 6
    n_g, n_q = width // LANES, S // tq

    def body(*all_refs):
        qi, gi = pl.program_id(1), pl.program_id(0)
        got = _carry(comm, n_in, 5 if has_bias else 3, (gi == 0) & (qi == 0), (gi == n_g - 1) & (qi == n_q - 1), all_refs)
        refs = tuple(got[0]) + tuple(got[1])
        q_ref, k_ref, v_ref, o_ref, lse_ref, do_ref = refs[:6]
        dq_ref, dk_ref, dv_ref = refs[n_in:n_in + 3]

        @pl.when(qi == 0)
        def _():
            dk_ref[...] = jnp.zeros_like(dk_ref)
            dv_ref[...] = jnp.zeros_like(dv_ref)
            if has_bias:
                refs[n_in + 4][...] = jnp.zeros_like(refs[n_in + 4])

        masks = _lane_masks(hd, tq)
        qs = q_ref[...].astype(F32) * scale
        qm = [_keep(qs, m).astype(BF16) for m in masks]
        do32 = do_ref[...]
        dov = [_keep(do32, m).astype(BF16) for m in masks]
        prod = do32 * o_ref[...].astype(F32)
        delta = [jnp.sum(_keep(prod, m), axis=1, keepdims=True) for m in masks]
        lses = [_row_value(lse_ref[...], m) for m in masks]
        allowed = lax.broadcasted_iota(jnp.int32, (tq, tk), 1) <= lax.broadcasted_iota(jnp.int32, (tq, tk), 0)

        def tile(kb, carry, diag):
            ks = pl.multiple_of(kb * tk, tk)
            kv = k_ref[pl.ds(ks, tk), :].astype(BF16)
            vv = v_ref[pl.ds(ks, tk), :].astype(BF16)
            out = []
            dk_t, dv_t = None, None
            for hh in range(hpg):
                dq, db_row = carry[2 * hh:2 * hh + 2]
                z = _dot(qm[hh], kv, _NT)
                if has_bias:
                    z = z + refs[6][hh] - refs[7][hh, kb]
                p = jnp.exp(z - lses[hh])
                if diag:
                    p = jnp.where(allowed, p, 0.0)
                dz = p * (_dot(dov[hh], vv, _NT) - delta[hh])
                dzb = dz.astype(BF16)
                dk_h = _dot(dzb, qm[hh], _TN)
                dv_h = _dot(p.astype(BF16), dov[hh], _TN)
                dk_t = dk_h if dk_t is None else dk_t + dk_h
                dv_t = dv_h if dv_t is None else dv_t + dv_h
                if has_bias:
                    db_row = db_row + jnp.sum(dz, axis=1, keepdims=True)
                    refs[n_in + 4][hh, kb] += jnp.sum(dz, axis=0, keepdims=True)
                out += [dq + _dot(dzb, kv), db_row]
            dk_ref[pl.ds(ks, tk), :] += dk_t
            dv_ref[pl.ds(ks, tk), :] += dv_t
            return tuple(out)

        init = (jnp.zeros((tq, LANES), F32), jnp.zeros((tq, 1), F32)) * hpg
        if causal:
            carry = lax.fori_loop(0, qi, lambda kb, c: tile(kb, c, False), init)
            carry = tile(qi, carry, True)
        else:
            carry = lax.fori_loop(0, nk, lambda kb, c: tile(kb, c, False), init)
        dq_ref[...] = _merge(carry[0::2], masks) * scale
        if has_bias:
            for hh in range(hpg):
                refs[n_in + 3][hh] = carry[2 * hh + 1]
        got[2]()

    in_specs = [_cols(tq, qo), _cols_all(Sk, ko), _cols_all(Sk, vo), _cols(tq, o[1]), _cols(tq, lse[1]), _cols(tq, do[1])]
    ins = [qa, ka, va, o[0], lse[0], do[0]]
    out_specs = [_cols(tq, 0), _cols_all(Sk, 0), _cols_all(Sk, 0)]
    out_shape = [jax.ShapeDtypeStruct((S, width), F32), jax.ShapeDtypeStruct((Sk, width), F32),
                 jax.ShapeDtypeStruct((Sk, width), F32)]
    if has_bias:
        rspec = pl.BlockSpec((hpg, tq, 1), lambda g, i: (g, i, 0))
        cspec = pl.BlockSpec((hpg, nk, 1, tk), lambda g, i: (g, 0, 0, 0))
        in_specs += [rspec, cspec]
        ins += list(bias)
        out_specs += [rspec, cspec]
        n_heads = width // hd
        out_shape += [jax.ShapeDtypeStruct((n_heads, S, 1), F32), jax.ShapeDtypeStruct((n_heads, nk, 1, tk), F32)]
    c_ins, c_in_specs, c_out_specs, c_out_shape, c_scratch = _comm_args(comm)
    return pl.pallas_call(
        body, name=name, grid=(n_g, n_q),
        in_specs=in_specs + c_in_specs, out_specs=out_specs + c_out_specs, out_shape=out_shape + c_out_shape,
        scratch_shapes=c_scratch,
        compiler_params=_params(("arbitrary", "arbitrary") if comm is not None else ("parallel", "arbitrary")),
    )(*ins, *c_ins)


def _head_sums(t, masks):
    sums = [jnp.sum(_keep(t, m), axis=1, keepdims=True) for m in masks]
    return _merge([jnp.broadcast_to(s, t.shape) for s in sums], masks) if len(masks) > 1 else sums[0]


def _hnorm_fwd(x, g_lanes, *, width, hd, name, tr=512):
    xa, xo = x
    R = xa.shape[0]
    tr = _tile(R, tr, align=16)
    n_blk = width // LANES

    def body(x_ref, g_ref, o_ref):
        masks = _lane_masks(hd, tr)
        for j in range(n_blk):
            sl = slice(j * LANES, (j + 1) * LANES)
            xv = x_ref[:, sl].astype(F32)
            r = lax.rsqrt(_head_sums(xv * xv, masks) * (1.0 / hd) + EPS)
            o_ref[:, sl] = (xv * r * g_ref[...]).astype(o_ref.dtype)

    assert (xo * LANES) % width == 0
    return pl.pallas_call(
        body, name=name, grid=(R // tr,),
        in_specs=[pl.BlockSpec((tr, width), lambda i: (i, xo * LANES // width)), pl.BlockSpec((1, LANES), lambda i: (0, 0))],
        out_specs=pl.BlockSpec((tr, width), lambda i: (i, 0)),
        out_shape=jax.ShapeDtypeStruct((R, width), BF16),
        compiler_params=_params(("parallel",)),
    )(xa, g_lanes)


def _hnorm_bwd(x, g_lanes, dy, *, width, hd, name, tr=512):
    xa, xo = x
    R = xa.shape[0]
    tr = _tile(R, tr, align=16)
    n_blk = width // LANES

    def body(x_ref, g_ref, dy_ref, dx_ref, dg_ref):
        masks = _lane_masks(hd, tr)
        dg = jnp.zeros((1, LANES), F32)
        for j in range(n_blk):
            sl = slice(j * LANES, (j + 1) * LANES)
            xv = x_ref[:, sl].astype(F32)
            dyv = dy_ref[:, sl].astype(F32)
            r = lax.rsqrt(_head_sums(xv * xv, masks) * (1.0 / hd) + EPS)
            xh = xv * r
            dyg = dyv * g_ref[...]
            c = _head_sums(dyg * xh, masks) * (1.0 / hd)
            dx_ref[:, sl] = (r * (dyg - xh * c)).astype(dx_ref.dtype)
            dg = dg + jnp.sum(dyv * xh, axis=0, keepdims=True)
        if hd * 2 == LANES:
            dg8 = jnp.broadcast_to(dg, (8, LANES))
            dg = (dg8 + pltpu.roll(dg8, shift=hd, axis=1))[0:1]
        else:
            assert hd == LANES

        @pl.when(pl.program_id(0) == 0)
        def _():
            dg_ref[...] = jnp.zeros_like(dg_ref)

        dg_ref[...] += dg

    assert (xo * LANES) % width == 0
    return pl.pallas_call(
        body, name=name, grid=(R // tr,),
        in_specs=[pl.BlockSpec((tr, width), lambda i: (i, xo * LANES // width)), pl.BlockSpec((1, LANES), lambda i: (0, 0)),
                  pl.BlockSpec((tr, width), lambda i: (i, 0))],
        out_specs=[pl.BlockSpec((tr, width), lambda i: (i, 0)), pl.BlockSpec((1, LANES), lambda i: (0, 0))],
        out_shape=[jax.ShapeDtypeStruct((R, width), BF16), jax.ShapeDtypeStruct((1, LANES), F32)],
        compiler_params=_params(("arbitrary",)),
    )(xa, g_lanes, dy)


def _split3_dot(x, tri):
    a = x.astype(BF16)
    r = x - a.astype(F32)
    b = r.astype(BF16)
    c = (r - b.astype(F32)).astype(BF16)
    return _dot(a, tri) + _dot(b, tri) + _dot(c, tri)


def _forget_fwd(logit_t, b_col, *, name, blk=512):
    H, S = logit_t.shape
    blk = _tile(S, blk)

    def body(l_ref, b_ref, f_ref):
        r_i = lax.broadcasted_iota(jnp.int32, (blk, blk), 0)
        c_i = lax.broadcasted_iota(jnp.int32, (blk, blk), 1)
        upto = (r_i <= c_i).astype(BF16)
        carry = jnp.zeros((H, 1), F32)
        for j in range(S // blk):
            u = l_ref[:, j * blk:(j + 1) * blk] + b_ref[...]
            lf, _ = _log_sigmoid_pair(u)
            f_ref[:, j * blk:(j + 1) * blk] = _split3_dot(lf, upto) + carry
            carry = carry + jnp.sum(lf, axis=1, keepdims=True)

    return pl.pallas_call(
        body, name=name,
        out_shape=jax.ShapeDtypeStruct((H, S), F32),
        compiler_params=pltpu.CompilerParams(vmem_limit_bytes=VMEM_LIMIT),
    )(logit_t, b_col)


def _forget_bwd(logit_t, b_col, d_f, *, name, blk=512):
    H, S = logit_t.shape
    blk = _tile(S, blk)

    def body(l_ref, b_ref, df_ref, dl_ref, db_ref):
        r_i = lax.broadcasted_iota(jnp.int32, (blk, blk), 0)
        c_i = lax.broadcasted_iota(jnp.int32, (blk, blk), 1)
        fromon = (r_i >= c_i).astype(BF16)
        carry = jnp.zeros((H, 1), F32)
        db = jnp.zeros((H, 1), F32)
        for j in reversed(range(S // blk)):
            sl = slice(j * blk, (j + 1) * blk)
            dfv = df_ref[:, sl]
            d_lf = _split3_dot(dfv, fromon) + carry
            carry = carry + jnp.sum(dfv, axis=1, keepdims=True)
            u = l_ref[:, sl] + b_ref[...]
            _, lsn = _log_sigmoid_pair(u)
            dl = d_lf * jnp.exp(lsn)
            dl_ref[:, sl] = dl
            db = db + jnp.sum(dl, axis=1, keepdims=True)
        db_ref[...] = db

    return pl.pallas_call(
        body, name=name,
        out_shape=[jax.ShapeDtypeStruct((H, S), F32), jax.ShapeDtypeStruct((H, 1), F32)],
        compiler_params=pltpu.CompilerParams(vmem_limit_bytes=VMEM_LIMIT),
    )(logit_t, b_col, d_f)


def _sigmoid(t):
    return 1.0 / (1.0 + jnp.exp(-t))


def _gate_fwd(o3, w3, proj, D, *, name, tm=256):
    S = proj.shape[0]
    tm = _tile(S, tm)

    def body(o0, o1, o2, w0, w1, w2, g0, g1, g2, out_ref):
        acc = None
        for o_ref, w_ref, g_ref in ((o0, w0, g0), (o1, w1, g1), (o2, w2, g2)):
            t = _sigmoid(g_ref[...]) * _dot(o_ref[...], w_ref[...])
            acc = t if acc is None else acc + t
        out_ref[...] = acc.astype(out_ref.dtype)

    ospec = lambda d: pl.BlockSpec((tm, d), lambda i: (i, 0))
    wspec = lambda w: pl.BlockSpec(w.shape, lambda i: (0, 0))
    gspec = lambda j: pl.BlockSpec((tm, D), lambda i: (i, j))
    return pl.pallas_call(
        body, name=name, grid=(S // tm,),
        in_specs=[ospec(o.shape[1]) for o in o3] + [wspec(w) for w in w3] + [gspec(j) for j in range(3)],
        out_specs=pl.BlockSpec((tm, D), lambda i: (i, 0)),
        out_shape=jax.ShapeDtypeStruct((S, D), BF16),
        compiler_params=_params(("parallel",)),
    )(*o3, *w3, proj, proj, proj)


def _gate_bwd(o3, w3, proj, dmerged, D, *, name, tm=256):
    S = proj.shape[0]
    tm = _tile(S, tm)

    def body(o0, o1, o2, w0, w1, w2, g0, g1, g2, dm_ref, dg_ref, db0, db1, db2, do0, do1, do2):
        dm = dm_ref[...]
        for j, (o_ref, w_ref, g_ref, db_ref, do_ref) in enumerate(
                ((o0, w0, g0, db0, do0), (o1, w1, g1, db1, do1), (o2, w2, g2, db2, do2))):
            s = _sigmoid(g_ref[...])
            br = _dot(o_ref[...], w_ref[...])
            dg_ref[:, j * D:(j + 1) * D] = (dm * br * s * (1.0 - s)).astype(dg_ref.dtype)
            dbr = (dm * s).astype(BF16)
            db_ref[...] = dbr
            do_ref[...] = _dot(dbr, w_ref[...], _NT)

    ospec = lambda d: pl.BlockSpec((tm, d), lambda i: (i, 0))
    wspec = lambda w: pl.BlockSpec(w.shape, lambda i: (0, 0))
    gspec = lambda j: pl.BlockSpec((tm, D), lambda i: (i, j))
    dspec = pl.BlockSpec((tm, D), lambda i: (i, 0))
    return pl.pallas_call(
        body, name=name, grid=(S // tm,),
        in_specs=[ospec(o.shape[1]) for o in o3] + [wspec(w) for w in w3] + [gspec(j) for j in range(3)] + [dspec],
        out_specs=[pl.BlockSpec((tm, 3 * D), lambda i: (i, 0))] + [dspec] * 3 + [ospec(o.shape[1]) for o in o3],
        out_shape=[jax.ShapeDtypeStruct((S, 3 * D), BF16)] + [jax.ShapeDtypeStruct((S, D), BF16)] * 3
        + [jax.ShapeDtypeStruct((S, o.shape[1]), F32) for o in o3],
        compiler_params=_params(("parallel",)),
    )(*o3, *w3, proj, proj, proj, dmerged)


def _loss_sum(dy, D, *, name):
    R, C = dy.shape
    tr = _row_tile(R, C, 2)

    def body(dy_ref, out_ref):
        @pl.when(pl.program_id(0) == 0)
        def _():
            out_ref[...] = jnp.zeros_like(out_ref)

        v = dy_ref[...]
        out_ref[...] += (0.5 * D) * jnp.sum(v * v)

    return pl.pallas_call(
        body, name=name, grid=(R // tr,),
        in_specs=[pl.BlockSpec((tr, C), lambda i: (i, 0))],
        out_specs=pl.BlockSpec((8, 128), lambda i: (0, 0)),
        out_shape=jax.ShapeDtypeStruct((8, 128), F32),
        compiler_params=_params(("arbitrary",)),
    )(dy)[0, 0]


def _adamw_math(w, g, m, v):
    m2 = ADAM_B1 * m + (1.0 - ADAM_B1) * g
    v2 = ADAM_B2 * v + (1.0 - ADAM_B2) * (g * g)
    m_hat = m2 / (1.0 - ADAM_B1 ** ADAM_STEP)
    v_hat = v2 / (1.0 - ADAM_B2 ** ADAM_STEP)
    delta = -ADAM_LR * (m_hat / (jnp.sqrt(v_hat) + ADAM_EPS) + ADAM_WD * w)
    return delta, m2, v2


def _adamw(w, g, m, v, *, name):
    return _ew(_adamw_math, [w, g, m, v], (F32, F32, F32), name=name)


def _adamw_small(w, parts, m, v, *, name):
    n = parts.shape[0]

    def body(w_ref, p_ref, m_ref, v_ref, g_ref, d_ref, m2_ref, v2_ref):
        g = p_ref[0]
        for i in range(1, n):
            g = g + p_ref[i]
        g_ref[...] = g
        d_ref[...], m2_ref[...], v2_ref[...] = _adamw_math(w_ref[...], g, m_ref[...], v_ref[...])

    shp = jax.ShapeDtypeStruct(w.shape, F32)
    return pl.pallas_call(body, name=name, out_shape=[shp] * 4)(w, parts, m, v)


_ANY = pl.BlockSpec(memory_space=pl.ANY)


def _mesh_place():
    x, y, c = lax.axis_index("x"), lax.axis_index("y"), lax.axis_index("c")
    chips = [(1 - x, y), (x, 1 - y), (1 - x, 1 - y)]
    return x, y, c, chips


def _remote(src, dst, sems, i, to):
    send_sems, recv_sems = sems
    return pltpu.make_async_remote_copy(src_ref=src, dst_ref=dst, send_sem=send_sems.at[i], recv_sem=recv_sems.at[i],
                                        device_id=to, device_id_type=MESH_ID)


def _gather_weights(shards, *, name):
    n = len(shards)

    def body(*refs):
        ins, outs = refs[:n], refs[n:2 * n]
        sems = refs[2 * n:2 * n + 2]
        x, y, c, chips = _mesh_place()
        me = 2 * x + y
        sibling = (x, y, 1 - c)
        sent = []
        for w in range(n):
            for j, chip in enumerate(chips):
                cp = _remote(ins[w].at[c], outs[w].at[me, c], sems, 6 * w + j, (chip[0], chip[1], c))
                cp.start()
                sent.append(cp)
        for w in range(n):
            for j, chip in enumerate(chips):
                got = outs[w].at[2 * chip[0] + chip[1], c]
                _remote(got, got, sems, 6 * w + j, sibling).wait_recv()
                cp = _remote(got, got, sems, 6 * w + 3 + j, sibling)
                cp.start()
                sent.append(cp)
        for w in range(n):
            for j, chip in enumerate(chips):
                got = outs[w].at[2 * chip[0] + chip[1], 1 - c]
                _remote(got, got, sems, 6 * w + 3 + j, sibling).wait_recv()
        for cp in sent:
            cp.wait_send()

    outs = pl.pallas_call(
        body, name=name,
        in_specs=[_ANY] * n, out_specs=[_ANY] * n,
        out_shape=[jax.ShapeDtypeStruct((N_CHIPS,) + s.shape, s.dtype) for s in shards],
        scratch_shapes=[pltpu.SemaphoreType.DMA((6 * n,)), pltpu.SemaphoreType.DMA((6 * n,))],
    )(*shards)
    me = 2 * lax.axis_index("x") + lax.axis_index("y")
    return [lax.dynamic_update_index_in_dim(o, s, me, 0) for o, s in zip(outs, shards)]


def _forward_halves(gathered, *, name):
    n = len(gathered)

    def body(*refs):
        ins, outs = refs[:n], refs[n:2 * n]
        sems = refs[2 * n:2 * n + 2]
        x, y, c, chips = _mesh_place()
        sibling = (x, y, 1 - c)
        sent = []
        for w in range(n):
            for j, chip in enumerate(chips):
                peer = 2 * chip[0] + chip[1]
                cp = _remote(ins[w].at[peer, c], outs[w].at[peer, c], sems, 3 * w + j, sibling)
                cp.start()
                sent.append(cp)
        for w in range(n):
            for j, chip in enumerate(chips):
                land = outs[w].at[2 * chip[0] + chip[1], 1 - c]
                _remote(land, land, sems, 3 * w + j, sibling).wait_recv()
        for cp in sent:
            cp.wait_send()

    return pl.pallas_call(
        body, name=name,
        in_specs=[_ANY] * n, out_specs=[_ANY] * n,
        out_shape=[jax.ShapeDtypeStruct(g.shape, g.dtype) for g in gathered],
        input_output_aliases={w: w for w in range(n)},
        scratch_shapes=[pltpu.SemaphoreType.DMA((3 * n,)), pltpu.SemaphoreType.DMA((3 * n,))],
    )(*gathered)


def _exchange_siblings(grads, *, name):
    n = len(grads)

    def body(*refs):
        ins, got = refs[:n], refs[n:2 * n]
        sems = refs[2 * n:2 * n + 2]
        x, y, c, _ = _mesh_place()
        sibling = (x, y, 1 - c)
        sent = []
        for w in range(n):
            for s in range(N_CHIPS):
                cp = _remote(ins[w].at[s, 1 - c], got[w].at[s], sems, N_CHIPS * w + s, sibling)
                cp.start()
                sent.append(cp)
        for w in range(n):
            for s in range(N_CHIPS):
                _remote(got[w].at[s], got[w].at[s], sems, N_CHIPS * w + s, sibling).wait_recv()
        for cp in sent:
            cp.wait_send()

    n_sem = N_CHIPS * n
    got = pl.pallas_call(
        body, name=name,
        in_specs=[_ANY] * n, out_specs=[_ANY] * n,
        out_shape=[jax.ShapeDtypeStruct((N_CHIPS,) + g.shape[2:], g.dtype) for g in grads],
        scratch_shapes=[pltpu.SemaphoreType.DMA((n_sem,)), pltpu.SemaphoreType.DMA((n_sem,))],
    )(*grads)
    c = lax.axis_index("c")
    return [lax.dynamic_index_in_dim(g, c, 1, keepdims=False) for g in grads], got


def _exchange_chips(parts, *, name):
    n = len(parts)

    def body(*refs):
        ins, got = refs[:n], refs[n:2 * n]
        sems = refs[2 * n:2 * n + 2]
        x, y, c, chips = _mesh_place()
        sent = []
        for w in range(n):
            for j, chip in enumerate(chips):
                cp = _remote(ins[w].at[2 * chip[0] + chip[1]], got[w].at[j], sems, 3 * w + j, (chip[0], chip[1], c))
                cp.start()
                sent.append(cp)
        for w in range(n):
            for j in range(3):
                _remote(got[w].at[j], got[w].at[j], sems, 3 * w + j, (x, y, c)).wait_recv()
        for cp in sent:
            cp.wait_send()

    got = pl.pallas_call(
        body, name=name,
        in_specs=[_ANY] * n, out_specs=[_ANY] * n,
        out_shape=[jax.ShapeDtypeStruct((3,) + p.shape[1:], p.dtype) for p in parts],
        scratch_shapes=[pltpu.SemaphoreType.DMA((3 * n,)), pltpu.SemaphoreType.DMA((3 * n,))],
    )(*parts)
    me = 2 * lax.axis_index("x") + lax.axis_index("y")
    return [lax.dynamic_index_in_dim(p, me, 0, keepdims=False) for p in parts], got


def _share_halves(halves, small):
    n = len(halves)

    def body(*refs):
        ins, small_ref = refs[:n], refs[n]
        outs, small_out = refs[n + 1:2 * n + 1], refs[2 * n + 1]
        sems = refs[2 * n + 2:2 * n + 4]
        x, y, c, chips = _mesh_place()
        sibling = (x, y, 1 - c)
        me = 4 * x + 2 * y + c
        sent = [_remote(ins[w], outs[w].at[c], sems, w, sibling) for w in range(n)]
        peers = [sibling] + [(ch[0], ch[1], cc) for ch in chips for cc in (c, 1 - c)]
        sent += [_remote(small_ref, small_out.at[me], sems, n + j, peer) for j, peer in enumerate(peers)]
        for cp in sent:
            cp.start()
        for w in range(n):
            _remote(outs[w].at[1 - c], outs[w].at[1 - c], sems, w, sibling).wait_recv()
        for j, peer in enumerate(peers):
            frm = small_out.at[4 * peer[0] + 2 * peer[1] + peer[2]]
            _remote(frm, frm, sems, n + j, peer).wait_recv()
        for cp in sent:
            cp.wait_send()

    n_sem = n + 7
    outs = pl.pallas_call(
        body, name="share_halves",
        in_specs=[_ANY] * (n + 1), out_specs=[_ANY] * (n + 1),
        out_shape=[jax.ShapeDtypeStruct((2,) + h.shape, h.dtype) for h in halves]
        + [jax.ShapeDtypeStruct((8,) + small.shape, small.dtype)],
        scratch_shapes=[pltpu.SemaphoreType.DMA((n_sem,)), pltpu.SemaphoreType.DMA((n_sem,))],
    )(*halves, small)
    c = lax.axis_index("c")
    me = 4 * lax.axis_index("x") + 2 * lax.axis_index("y") + c
    return ([lax.dynamic_update_index_in_dim(o, h, c, 0) for o, h in zip(outs[:n], halves)],
            lax.dynamic_update_index_in_dim(outs[n], small, me, 0))


EARLY = ("w_ff_down", "w_ff_up", "w_out", "w_branch_sb", "w_branch_fox", "w_branch_mem")


def _split(outs, n):
    outs = list(outs) if isinstance(outs, (list, tuple)) else [outs]
    return outs[:n], outs[n:]


def _local_step(x, mem, target, small, W, gather_rest=None, reduce_early=None, reduce_late=None):
    S, D = x.shape
    o_gate, o_qkv, o_mq, o_f = 0, 3 * D, 3 * D + 2 * 3 * D_SB, 3 * D + 2 * 3 * D_SB + D_MEM
    tq = 512

    g_comms, finish_weights = gather_rest if gather_rest is not None else ([None] * 3, None)
    h = _rmsnorm_fwd(x, small["g_mix_norm"], BF16, name="mix_norm")
    (proj,), landed = _split(_mm(h, W["w_in"], name="in_proj", tb=True, tn=768, comm=g_comms[0]), 1)
    blk = lambda j: (proj, (o_qkv + j * D_SB) // LANES)
    sb_q, sb_k, sb_v, fx_q, fx_k, fx_v = [blk(j) for j in range(6)]
    m_q = (proj, o_mq // LANES)
    f_logit_t = _mm(W["w_in"][o_f:o_f + ROW_TILE], h, name="forget_logits", tb=True)[:FOX_HEADS]
    b_col = small["b_forget"].reshape(FOX_HEADS, 1)
    lanes = lambda g: jnp.tile(g, (1, LANES // g.shape[1]))
    g_fq, g_fk, g_mq, g_mk = [lanes(small[k]) for k in ("g_fox_q", "g_fox_k", "g_mem_q", "g_mem_k")]

    (o_sb, sb_tot), more = _split(_sbl_fwd(sb_q, sb_k, sb_v, width=D_SB, hd=HD, name="sb_fwd", tq=tq, comm=g_comms[1]), 2)
    landed += more

    fq = _hnorm_fwd(fx_q, g_fq, width=D_FOX, hd=HD, name="fox_q_norm")
    fk = _hnorm_fwd(fx_k, g_fk, width=D_FOX, hd=HD, name="fox_k_norm")
    f_cum = _forget_fwd(f_logit_t, b_col, name="forget_fwd")
    tkf = _tile(S, tq)
    f_bias = (f_cum.reshape(FOX_HEADS, S, 1), f_cum.reshape(FOX_HEADS, S // tkf, 1, tkf))
    (o_fox, fox_lse), more = _split(_sml_fwd((fq, 0), (fk, 0), fx_v, f_bias, width=D_FOX, hd=HD, causal=True,
                                             name="fox_fwd", tq=tq, tk=tq, comm=g_comms[2]), 2)
    landed += more
    if finish_weights is not None:
        W = {**W, **finish_weights(landed)}

    mh = _rmsnorm_fwd(mem, small["g_mem_norm"], BF16, name="mem_norm")
    mkv = _mm(mh, W["w_mem_kv"], name="mem_kv")
    mv = (mkv, D_MEM // LANES)
    mq = _hnorm_fwd(m_q, g_mq, width=D_MEM, hd=MEM_HD, name="mem_q_norm")
    mk = _hnorm_fwd((mkv, 0), g_mk, width=D_MEM, hd=MEM_HD, name="mem_k_norm")
    o_mem, mem_lse = _sml_fwd((mq, 0), (mk, 0), mv, width=D_MEM, hd=MEM_HD, causal=False, name="mem_fwd", tq=tq, tk=256)

    o3 = [o_sb, o_fox, o_mem]
    w3 = [W["w_branch_sb"], W["w_branch_fox"], W["w_branch_mem"]]
    merged = _gate_fwd(o3, w3, proj, D, name="gate_fwd")
    x1 = _mm(merged, W["w_out"], name="out_proj", extras=(x,), epilogue=lambda acc, res: (res + acc,))
    h2 = _rmsnorm_fwd(x1, small["g_mlp_norm"], BF16, name="mlp_norm")

    def relu2(acc):
        u = jnp.maximum(acc, 0.0)
        return u, u * u

    u, a = _mm(h2, W["w_ff_up"], name="ff_up", out_dtypes=(BF16, BF16), epilogue=relu2)
    def head(acc, res, tgt):
        d = (res + acc - tgt) * (1.0 / D)
        return d, d

    dy, dy16 = _mm(a, W["w_ff_down"], name="ff_down", extras=(x1, target), out_dtypes=(F32, BF16), epilogue=head, tm=512)
    loss = _loss_sum(dy, D, name="loss")

    G = {}
    du = _mm(dy16, W["w_ff_down"], name="d_ff_act", tb=True, out_dtypes=(BF16,), extras=(u,),
             epilogue=lambda acc, uu: (acc * (2.0 * uu.astype(F32)),))
    G["w_ff_down"] = _mm(a, dy16, name="d_w_ff_down", ta=True, out_dtypes=(BF16,))
    G["w_ff_up"] = _mm(h2, du, name="d_w_ff_up", ta=True, out_dtypes=(BF16,))
    dh2 = _mm(du, W["w_ff_up"], name="d_mlp_in", tb=True)
    dx1, dg_mlp = _rmsnorm_bwd(x1, small["g_mlp_norm"], dh2, add=dy, name="d_mlp_norm")
    dmerged = _mm(dx1, W["w_out"], name="d_merged", tb=True)
    G["w_out"] = _mm(merged, dx1, name="d_w_out", ta=True, out_dtypes=(BF16,))
    dgate, db0, db1, db2, do_sb, do_fox, do_mem = _gate_bwd(o3, w3, proj, dmerged, D, name="gate_bwd")
    for nm, o, db in zip(("w_branch_sb", "w_branch_fox", "w_branch_mem"), o3, (db0, db1, db2)):
        G[nm] = _mm(o, db, name="d_" + nm, ta=True, out_dtypes=(BF16,))

    r_comms, r_finish = reduce_early({k: G.pop(k) for k in EARLY}) if reduce_early is not None else ([None] * 2, None)
    dsb, landed_sb = _split(_sbl_bwd(sb_q, sb_k, sb_v, (do_sb, 0), (sb_tot, 0), width=D_SB, hd=HD, name="sb_bwd", tq=tq,
                                     comm=r_comms[0]), 3)
    (dfq, dfk, dfv, df_row, df_col), landed_fox = _split(
        _sml_bwd((fq, 0), (fk, 0), fx_v, (o_fox, 0), (fox_lse, 0), (do_fox, 0), f_bias, width=D_FOX, hd=HD, causal=True,
                 name="fox_bwd", tq=tq, tk=tq, comm=r_comms[1]), 5)
    early = r_finish(landed_sb, landed_fox) if r_finish is not None else {}
    dfx_q, dg_fox_q = _hnorm_bwd(fx_q, g_fq, dfq, width=D_FOX, hd=HD, name="d_fox_q_norm")
    dfx_k, dg_fox_k = _hnorm_bwd(fx_k, g_fk, dfk, width=D_FOX, hd=HD, name="d_fox_k_norm")
    d_fcum = df_row.reshape(FOX_HEADS, S) - df_col.reshape(FOX_HEADS, S)
    d_flogit_t, db_forget = _forget_bwd(f_logit_t, b_col, d_fcum, name="forget_bwd")

    dmq_n, dmk_n, dmv = _sml_bwd((mq, 0), (mk, 0), mv, (o_mem, 0), (mem_lse, 0), (do_mem, 0), width=D_MEM, hd=MEM_HD,
                                 causal=False, name="mem_bwd", tq=tq, tk=256)
    dm_q, dg_mem_q = _hnorm_bwd(m_q, g_mq, dmq_n, width=D_MEM, hd=MEM_HD, name="d_mem_q_norm")
    dmk_raw, dg_mem_k = _hnorm_bwd((mkv, 0), g_mk, dmk_n, width=D_MEM, hd=MEM_HD, name="d_mem_k_norm")
    dmkv = jnp.concatenate([dmk_raw, dmv.astype(BF16)], axis=1)
    G["w_mem_kv"] = _mm(mh, dmkv, name="d_w_mem_kv", ta=True, out_dtypes=(BF16,))
    dmh = _mm(dmkv, W["w_mem_kv"], name="d_mem_h", tb=True)
    _, dg_mem = _rmsnorm_bwd(mem, small["g_mem_norm"], dmh, name="d_mem_norm")
    dg_fox_q, dg_fox_k = dg_fox_q[:, :HD], dg_fox_k[:, :HD]

    dproj = jnp.concatenate(
        [dgate] + [t.astype(BF16) for t in (*dsb, dfx_q, dfx_k, dfv, dm_q)]
        + [d_flogit_t.T.astype(BF16), jnp.zeros((S, F_PAD - FOX_HEADS), BF16)], axis=1)
    g_w_in = _mm(dproj, h, name="d_w_in", ta=True, out_dtypes=(BF16,), tm=768)
    if reduce_late is None:
        G["w_in"] = g_w_in
        dh = _mm(dproj, W["w_in"], name="d_mix_in", tk=2304)
    else:
        comm, finish = reduce_late({"w_in": g_w_in, "w_mem_kv": G.pop("w_mem_kv")})
        dh, *landed = _mm(dproj, W["w_in"], name="d_mix_in", tk=2304, comm=comm)
        early.update(finish(landed))
    grad_x, dg_mix = _rmsnorm_bwd(x, small["g_mix_norm"], dh, add=dx1, name="d_mix_norm")

    small_grads = dict(g_mix_norm=dg_mix, g_mem_norm=dg_mem, b_forget=db_forget.reshape(1, FOX_HEADS),
                       g_fox_q=dg_fox_q, g_fox_k=dg_fox_k, g_mem_q=dg_mem_q, g_mem_k=dg_mem_k, g_mlp_norm=dg_mlp)
    return loss, grad_x, G, small_grads, early


BIG = ("w_in", "w_mem_kv", "w_branch_sb", "w_branch_fox", "w_branch_mem", "w_out", "w_ff_up", "w_ff_down")
COLUMN_SHARDED = ("w_in", "w_branch_sb", "w_branch_fox", "w_branch_mem", "w_ff_up")
SMALL = ("g_mix_norm", "g_mem_norm", "b_forget", "g_fox_q", "g_fox_k", "g_mem_q", "g_mem_k", "g_mlp_norm")
ORDER = ("g_mix_norm", "g_mem_norm", "w_in", "b_forget", "g_fox_q", "g_fox_k", "g_mem_q", "g_mem_k", "w_mem_kv",
         "w_branch_sb", "w_branch_fox", "w_branch_mem", "w_out", "g_mlp_norm", "w_ff_up", "w_ff_down")


def _unshard(name, gathered):
    n, _, rh, c = gathered.shape
    t = gathered.reshape(n, 2 * rh, c)
    if name in COLUMN_SHARDED:
        return t.transpose(1, 0, 2).reshape(2 * rh, n * c)
    return t.reshape(n * 2 * rh, c)


def _reshard(name, full):
    if name in COLUMN_SHARDED:
        r, c = full.shape
        t = full.reshape(r, N_CHIPS, c // N_CHIPS).transpose(1, 0, 2)
    else:
        r, c = full.shape[0] // N_CHIPS, full.shape[1]
        t = full.reshape(N_CHIPS, r, c)
    return t.reshape(N_CHIPS, 2, t.shape[1] // 2, t.shape[2])


ROW_TILE = 16
IN_BUF_ALIGN = 256


def _in_segments(D):
    n_qkv = 6 * D_SB
    o_mq, o_gate = n_qkv + FOX_HEADS, n_qkv + FOX_HEADS + D_MEM
    return [(0, n_qkv, 3 * D), (n_qkv, o_mq, 3 * D + n_qkv + D_MEM), (o_mq, o_gate, 3 * D + n_qkv), (o_gate, o_gate + 3 * D, 0)]


class _InLayout:
    def __init__(self, D, shard, n):
        self.D, self.shard, self.n = D, shard, n
        down = lambda v: v // ROW_TILE * ROW_TILE
        up = lambda v: -(-v // ROW_TILE) * ROW_TILE
        self.pieces = []
        ends = []
        for s in range(n):
            cursor, mine = 0, []
            for a, b, p in _in_segments(D):
                x0, x1 = max(a, s * shard), min(b, (s + 1) * shard)
                if x0 < x1:
                    p0 = p + x0 - a
                    rows = up(p0 + x1 - x0) - down(p0)
                    mine.append((x0 - s * shard, x1 - x0, p0, cursor, rows))
                    cursor += rows
            self.pieces.append(mine)
            ends.append(cursor)
        self.rows = -(-max(ends) // IN_BUF_ALIGN) * IN_BUF_ALIGN
        self.padded_rows = 3 * D + 6 * D_SB + D_MEM + F_PAD

    def _per_shard(self, fn, chip, operand):
        return lax.switch(chip, [functools.partial(fn, s) for s in range(self.n)], operand)

    def pack(self, chip, rows):
        def one(s, t):
            out, at = [], 0
            for x0, n_rows, p0, start, region in self.pieces[s]:
                lead = p0 % ROW_TILE
                out += [jnp.zeros((start + lead - at, t.shape[1]), t.dtype), t[x0:x0 + n_rows]]
                at = start + lead + n_rows
            return jnp.concatenate(out + [jnp.zeros((self.rows - at, t.shape[1]), t.dtype)], axis=0)
        return self._per_shard(one, chip, rows)

    def unpack(self, chip, buf, pad_to):
        def one(s, t):
            out = [t[start + p0 % ROW_TILE:start + p0 % ROW_TILE + n_rows] for _, n_rows, p0, start, _ in self.pieces[s]]
            return jnp.concatenate(out + [jnp.zeros((pad_to - self.shard, t.shape[1]), t.dtype)], axis=0)
        return self._per_shard(one, chip, buf)

    def to_padded(self, bufs):
        runs = sorted((p0, s, start, region) for s in range(self.n) for _, _, p0, start, region in self.pieces[s])
        chunks, end = [], 0
        for p0, s, start, region in runs:
            d0 = p0 // ROW_TILE * ROW_TILE
            src = bufs[s, start:start + region]
            if d0 < end:
                assert end - d0 == ROW_TILE
                last = chunks.pop()
                chunks += [last[:-ROW_TILE], last[-ROW_TILE:] + src[:ROW_TILE], src[ROW_TILE:]]
            else:
                if d0 > end:
                    chunks.append(jnp.zeros((d0 - end, bufs.shape[2]), bufs.dtype))
                chunks.append(src)
            end = d0 + region
        chunks.append(jnp.zeros((self.padded_rows - end, bufs.shape[2]), bufs.dtype))
        return jnp.concatenate(chunks, axis=0)

    def from_padded(self, gp):
        bufs = []
        for s in range(self.n):
            out, at = [], 0
            for _, n_rows, p0, start, region in self.pieces[s]:
                d0 = p0 // ROW_TILE * ROW_TILE
                row = d0 + lax.broadcasted_iota(jnp.int32, (region, 1), 0)
                out.append(jnp.where((row >= p0) & (row < p0 + n_rows), gp[d0:d0 + region], jnp.zeros((), gp.dtype)))
                at = start + region
            bufs.append(jnp.concatenate(out + [jnp.zeros((self.rows - at, gp.shape[1]), gp.dtype)], axis=0))
        return jnp.stack(bufs)


def _pack_small(vals):
    width = max(vals[k].shape[1] for k in SMALL)
    return jnp.concatenate([jnp.pad(vals[k].astype(F32), ((0, 0), (0, width - vals[k].shape[1]))) for k in SMALL], axis=0)


def _unpack_small(packed, like):
    return {k: packed[i:i + 1, :like[k].shape[1]] for i, k in enumerate(SMALL)}


def kernel(x, mem, g_mix_norm, g_mem_norm, w_in, b_forget, g_fox_q, g_fox_k, g_mem_q, g_mem_k, w_mem_kv, w_branch_sb, w_branch_fox, w_branch_mem, w_out, g_mlp_norm, w_ff_up, w_ff_down, loss_target, m_g_mix_norm, m_g_mem_norm, m_w_in, m_b_forget, m_g_fox_q, m_g_fox_k, m_g_mem_q, m_g_mem_k, m_w_mem_kv, m_w_branch_sb, m_w_branch_fox, m_w_branch_mem, m_w_out, m_g_mlp_norm, m_w_ff_up, m_w_ff_down, v_g_mix_norm, v_g_mem_norm, v_w_in, v_b_forget, v_g_fox_q, v_g_fox_k, v_g_mem_q, v_g_mem_k, v_w_mem_kv, v_w_branch_sb, v_w_branch_fox, v_w_branch_mem, v_w_out, v_g_mlp_norm, v_w_ff_up, v_w_ff_down):
    given = dict(locals())
    D = x.shape[-1]
    weights = {k: given[k] for k in ORDER}
    moms = {k: given["m_" + k] for k in ORDER}
    vars_ = {k: given["v_" + k] for k in ORDER}

    me_chip = 2 * lax.axis_index("x") + lax.axis_index("y")

    n_in = w_in.shape[2]
    lay = _InLayout(D, n_in, N_CHIPS)
    transposed = lambda t: jnp.transpose(t[0])
    shards = {}
    for k in BIG:
        w = weights[k][0].astype(BF16)
        if k == "w_in":
            w = lay.pack(me_chip, jnp.transpose(w))
        shards[k] = w.reshape(2, w.shape[0] // 2, w.shape[1])
    gathered_in = _gather_weights([shards["w_in"]], name="gather_w_in")[0]
    W = {"w_in": lay.to_padded(gathered_in.reshape(N_CHIPS, lay.rows, D))}
    carried = (("w_ff_up",), ("w_ff_down", "w_mem_kv"), ("w_branch_sb", "w_branch_fox", "w_branch_mem", "w_out"))
    rest = [k for grp in carried for k in grp]
    assert sorted(rest + ["w_in"]) == sorted(BIG)

    def finish_weights(landed):
        full = _forward_halves(landed, name="forward_halves")
        full = [lax.dynamic_update_index_in_dim(o, shards[k], me_chip, 0) for k, o in zip(rest, full)]
        return {k: _unshard(k, g) for k, g in zip(rest, full)}

    def sum_pairs(names, own, sib):
        add2 = lambda p, q: (p.astype(F32) + q.astype(F32),)
        out = []
        for k, p, q in zip(names, own, sib):
            n, r, c = p.shape
            out.append(_ew(add2, [p.reshape(n * r, c), q.reshape(n * r, c)], (BF16,), name="sum_pair_" + k).reshape(n, r, c))
        return out

    def sum_chips(names, parts, got):
        add4 = lambda p, q0, q1, q2: (((p.astype(F32) + q0.astype(F32)) + q1.astype(F32)) + q2.astype(F32),)
        own = [lax.dynamic_index_in_dim(p, me_chip, 0, keepdims=False) for p in parts]
        return {k: _ew(add4, [p, q[0], q[1], q[2]], (F32,), name="sum_chips_" + k) for k, p, q in zip(names, own, got)}

    def pair_sums(grads, tag):
        names = list(grads)
        stacked = {k: _reshard(k, grads[k]) for k in names if k != "w_in"}
        if "w_in" in grads:
            stacked["w_in"] = lay.from_padded(grads["w_in"]).reshape(N_CHIPS, 2, lay.rows // 2, D)
        own, sib = _exchange_siblings([stacked[k] for k in names], name="exchange_siblings_" + tag)
        return dict(zip(names, sum_pairs(names, own, sib)))

    def reduce_early(grads):
        parts = pair_sums(grads, "early")
        groups = [[k for k in parts if k in ("w_ff_down", "w_ff_up")], [k for k in parts if k not in ("w_ff_down", "w_ff_up")]]
        comms = [_ChipExchange("scatter", [parts[k] for k in grp]) for grp in groups]

        def finish(*landed):
            out = {}
            for grp, got in zip(groups, landed):
                out.update(sum_chips(grp, [parts[k] for k in grp], got))
            return out
        return comms, finish

    def reduce_late(grads):
        parts = pair_sums(grads, "late")
        names = list(parts)
        return _ChipExchange("scatter", [parts[k] for k in names]), functools.partial(sum_chips, names, [parts[k] for k in names])

    small = {k: weights[k] for k in SMALL}
    loss_part, grad_x, G, small_grads, halves = _local_step(
        x[0], mem[0], loss_target[0], small, W,
        gather_rest=([_ChipExchange("gather", [shards[k] for k in grp]) for grp in carried], finish_weights),
        reduce_early=reduce_early, reduce_late=reduce_late)
    assert not G, list(G)
    reduced, small_parts = _share_halves([halves[k] for k in BIG], _pack_small(small_grads))

    grads, deltas, new_m, new_v = {}, {}, {}, {}
    for k, g in zip(BIG, reduced):
        shp = weights[k].shape
        if k == "w_in":
            g2 = lay.unpack(me_chip, g.reshape(lay.rows, D), lay.rows)
            padded = lambda t: jnp.pad(transposed(t), ((0, lay.rows - n_in), (0, 0)))
            outs = _adamw(padded(weights[k]), g2, padded(moms[k]), padded(vars_[k]), name="adamw_" + k)
            g2, d, m2, v2 = [jnp.transpose(t[:n_in]) for t in (g2, *outs)]
        else:
            g2 = g.reshape(shp[1], shp[2])
            d, m2, v2 = _adamw(weights[k][0], g2, moms[k][0], vars_[k][0], name="adamw_" + k)
        grads[k], deltas[k], new_m[k], new_v[k] = g2.reshape(shp), d.reshape(shp), m2.reshape(shp), v2.reshape(shp)
    sg, sd, sm, sv = _adamw_small(_pack_small(small), small_parts, _pack_small({k: moms[k] for k in SMALL}),
                                  _pack_small({k: vars_[k] for k in SMALL}), name="adamw_small")
    for dst, packed in ((grads, sg), (deltas, sd), (new_m, sm), (new_v, sv)):
        dst.update(_unpack_small(packed, small))

    loss = lax.psum(loss_part, ("x", "y", "c"))
    return (loss, grad_x[None], *[grads[k] for k in ORDER], *[deltas[k] for k in ORDER],
            *[new_m[k] for k in ORDER], *[new_v[k] for k in ORDER])
```

```python
import functools

import jax
import jax.numpy as jnp
from jax import lax
from jax.experimental import pallas as pl
from jax.experimental.pallas import tpu as pltpu

F32 = jnp.float32
BF16 = jnp.bfloat16
MESH_ID = pl.DeviceIdType.MESH

HD = 64
SB_HEADS = 8
FOX_HEADS = 8
MEM_HEADS = 4
MEM_HD = 128
D_SB = SB_HEADS * HD
D_FOX = FOX_HEADS * HD
D_MEM = MEM_HEADS * MEM_HD
EPS = 1e-6
NEG_INF = -1e30

ADAM_LR = 0.001
ADAM_B1 = 0.9
ADAM_B2 = 0.999
ADAM_EPS = 1e-08
ADAM_WD = 0.01
ADAM_STEP = 10

N_CHIPS = 4
VMEM_LIMIT = 56 * 1024 * 1024

F_PAD = 256


def _tile(n, target, align=128):
    if n <= target:
        return n
    best = None
    t = align
    while t <= target:
        if n % t == 0:
            best = t
        t += align
    assert best is not None, (n, target, align)
    return best


def _params(sem):
    return pltpu.CompilerParams(dimension_semantics=sem, vmem_limit_bytes=VMEM_LIMIT)


def _mm(a, b, *, name, ta=False, tb=False, out_dtypes=(F32,), epilogue=None, extras=(),
        tm=1024, tn=1024, tk=2048, comm=None):
    if ta:
        K, M = a.shape
    else:
        M, K = a.shape
    if tb:
        N, K2 = b.shape
    else:
        K2, N = b.shape
    assert K == K2, (a.shape, b.shape, ta, tb)
    tm, tn, tk = _tile(M, tm), _tile(N, tn), _tile(K, tk)
    nk = K // tk
    n_extra, n_out = len(extras), len(out_dtypes)
    if epilogue is None:
        epilogue = lambda acc: (acc,)
    dims = (((0 if ta else 1,), (1 if tb else 0,)), ((), ()))

    gm, gn = M // tm, N // tn

    def body(*refs):
        i, j, k = pl.program_id(0), pl.program_id(1), pl.program_id(2)
        got = _carry(comm, 2 + n_extra, n_out, (i == 0) & (j == 0) & (k == 0),
                     (i == gm - 1) & (j == gn - 1) & (k == nk - 1), refs)
        (a_ref, b_ref, *extra_refs), out_refs = got[0], got[1]
        part = lax.dot_general(a_ref[...].astype(BF16), b_ref[...].astype(BF16), dims, preferred_element_type=F32)

        def finish(acc):
            outs = epilogue(acc, *[r[...] for r in extra_refs])
            for o_ref, o in zip(out_refs, outs):
                o_ref[...] = o.astype(o_ref.dtype)

        if nk == 1:
            finish(part)
        else:
            acc_ref = refs[-1]

            @pl.when(k == 0)
            def _():
                acc_ref[...] = part

            @pl.when((k > 0) & (k < nk - 1))
            def _():
                acc_ref[...] += part

            @pl.when(k == nk - 1)
            def _():
                finish(acc_ref[...] + part)

        got[2]()

    a_spec = pl.BlockSpec((tk, tm), lambda i, j, k: (k, i)) if ta else pl.BlockSpec((tm, tk), lambda i, j, k: (i, k))
    b_spec = pl.BlockSpec((tn, tk), lambda i, j, k: (j, k)) if tb else pl.BlockSpec((tk, tn), lambda i, j, k: (k, j))
    mn_spec = pl.BlockSpec((tm, tn), lambda i, j, k: (i, j))
    c_ins, c_in_specs, c_out_specs, c_out_shape, c_scratch = _comm_args(comm)
    sem = ("parallel", "parallel", "arbitrary") if comm is None else ("arbitrary",) * 3
    outs = pl.pallas_call(
        body, name=name,
        grid=(gm, gn, nk),
        in_specs=[a_spec, b_spec] + [mn_spec] * n_extra + c_in_specs,
        out_specs=[mn_spec] * n_out + c_out_specs,
        out_shape=[jax.ShapeDtypeStruct((M, N), dt) for dt in out_dtypes] + c_out_shape,
        scratch_shapes=c_scratch + ([pltpu.VMEM((tm, tn), F32)] if nk > 1 else []),
        compiler_params=_params(sem),
    )(a, b, *extras, *c_ins)
    return outs if len(outs) > 1 else outs[0]


def _row_tile(rows, cols, n_arrays):
    budget = 10 * 1024 * 1024
    cols_padded = -(-cols // 128) * 128
    target = max(16, budget // (cols_padded * 4 * n_arrays * 2))
    return _tile(rows, target, align=16)


def _ew(fn, ins, out_dtypes, *, name):
    R, C = ins[0].shape
    n_in, n_out = len(ins), len(out_dtypes)
    tr = _row_tile(R, C, n_in + n_out)

    def body(*refs):
        outs = fn(*[r[...] for r in refs[:n_in]])
        for o_ref, o in zip(refs[n_in:], outs):
            o_ref[...] = o.astype(o_ref.dtype)

    spec = pl.BlockSpec((tr, C), lambda i: (i, 0))
    outs = pl.pallas_call(
        body, name=name, grid=(R // tr,),
        in_specs=[spec] * n_in, out_specs=[spec] * n_out,
        out_shape=[jax.ShapeDtypeStruct((R, C), dt) for dt in out_dtypes],
        compiler_params=_params(("parallel",)),
    )(*ins)
    return outs if n_out > 1 else outs[0]


def _rmsnorm_fwd(x, g, out_dtype, *, name):
    R, d = x.shape
    tr = _row_tile(R, d, 3)

    def body(x_ref, g_ref, o_ref):
        xv = x_ref[...].astype(F32)
        r = lax.rsqrt(jnp.mean(xv * xv, axis=-1, keepdims=True) + EPS)
        o_ref[...] = (xv * r * g_ref[...]).astype(o_ref.dtype)

    return pl.pallas_call(
        body, name=name, grid=(R // tr,),
        in_specs=[pl.BlockSpec((tr, d), lambda i: (i, 0)), pl.BlockSpec((1, d), lambda i: (0, 0))],
        out_specs=pl.BlockSpec((tr, d), lambda i: (i, 0)),
        out_shape=jax.ShapeDtypeStruct((R, d), out_dtype),
        compiler_params=_params(("parallel",)),
    )(x, g)


def _rmsnorm_bwd(x, g, dy, add=None, *, name):
    R, d = x.shape
    has_add = add is not None
    tr = _row_tile(R, d, 5)

    def body(*refs):
        x_ref, g_ref, dy_ref = refs[:3]
        add_ref = refs[3] if has_add else None
        dx_ref, dg_ref = refs[-2:]
        xv = x_ref[...].astype(F32)
        dyv = dy_ref[...].astype(F32)
        r = lax.rsqrt(jnp.mean(xv * xv, axis=-1, keepdims=True) + EPS)
        xh = xv * r
        dyg = dyv * g_ref[...]
        c = jnp.mean(dyg * xh, axis=-1, keepdims=True)
        dx = r * (dyg - xh * c)
        if has_add:
            dx = dx + add_ref[...]
        dx_ref[...] = dx

        @pl.when(pl.program_id(0) == 0)
        def _():
            dg_ref[...] = jnp.zeros_like(dg_ref)

        dg_ref[...] += jnp.sum(dyv * xh, axis=0, keepdims=True)

    row = pl.BlockSpec((tr, d), lambda i: (i, 0))
    vec = pl.BlockSpec((1, d), lambda i: (0, 0))
    ins = [x, g, dy] + ([add] if has_add else [])
    return pl.pallas_call(
        body, name=name, grid=(R // tr,),
        in_specs=[row, vec, row] + ([row] if has_add else []),
        out_specs=[row, vec],
        out_shape=[jax.ShapeDtypeStruct((R, d), F32), jax.ShapeDtypeStruct((1, d), F32)],
        compiler_params=_params(("arbitrary",)),
    )(*ins)


_NT = (((1,), (1,)), ((), ()))
_TN = (((0,), (0,)), ((), ()))


def _dot(a, b, dims=(((1,), (0,)), ((), ()))):
    return lax.dot_general(a, b, dims, preferred_element_type=F32)


def _log_sigmoid_pair(z):
    sp = jnp.log(1.0 + jnp.exp(-jnp.abs(z)))
    return jnp.minimum(z, 0.0) - sp, jnp.minimum(-z, 0.0) - sp


LANES = 128
_LOW = -3e38


def _lane_masks(hd, rows):
    if hd == LANES:
        return [None]
    lane = lax.broadcasted_iota(jnp.int32, (rows, LANES), 1)
    return [(lane >= hh * hd) & (lane < (hh + 1) * hd) for hh in range(LANES // hd)]


def _keep(t, m):
    return t if m is None else jnp.where(m, t, 0.0)


def _merge(parts, masks):
    out = parts[-1]
    for p, m in zip(parts[-2::-1], masks[-2::-1]):
        out = jnp.where(m, p, out)
    return out


def _row_value(t, m):
    return jnp.max(t if m is None else jnp.where(m, t, _LOW), axis=1, keepdims=True)


def _cols(tq, off):
    return pl.BlockSpec((tq, LANES), lambda g, i: (i, off + g))


def _cols_all(rows, off):
    return pl.BlockSpec((rows, LANES), lambda g, i: (0, off + g))


SCAN_BLOCK = 256


def _tri(kind, cols):
    n = min(SCAN_BLOCK, cols)
    r = lax.broadcasted_iota(jnp.int32, (n, n), 0)
    c = lax.broadcasted_iota(jnp.int32, (n, n), 1)
    return ((r > c) if kind == "after" else (r < c)).astype(BF16)


def _scan_cols(x, tri, reverse):
    cols = x.shape[1]
    cb = min(SCAN_BLOCK, cols)
    assert cols % cb == 0 and tri.shape == (cb, cb)
    nb = cols // cb
    blocks = [x[:, b * cb:(b + 1) * cb] for b in range(nb)]
    outs, carry = [None] * nb, None
    for b in (reversed(range(nb)) if reverse else range(nb)):
        y = _dot(blocks[b].astype(BF16), tri)
        outs[b] = y if carry is None else y + carry
        s = jnp.sum(blocks[b], axis=1, keepdims=True)
        carry = s if carry is None else carry + s
    return (outs[0] if nb == 1 else jnp.concatenate(outs, axis=1)), carry


def _softplus_parts(z):
    pos = jnp.maximum(z, 0.0) + jnp.log(1.0 + jnp.exp(-jnp.abs(z)))
    return pos, z - pos


class _ChipExchange:
    def __init__(self, kind, ins):
        assert kind in ("gather", "scatter")
        self.kind, self.ins = kind, list(ins)
        lead = (lambda s: (N_CHIPS,) + s) if kind == "gather" else (lambda s: (3,) + s[1:])
        self.out_shape = [jax.ShapeDtypeStruct(lead(a.shape), a.dtype) for a in ins]
        n = 3 * len(ins)
        self.scratch = [pltpu.SemaphoreType.DMA((n,)), pltpu.SemaphoreType.DMA((n,))]

    def _copies(self, in_refs, out_refs, sems, landing):
        x, y, c, chips = _mesh_place()
        me = 2 * x + y
        out = []
        for w in range(len(self.ins)):
            for j, chip in enumerate(chips):
                peer = 2 * chip[0] + chip[1]
                if self.kind == "gather":
                    src, dst, land = in_refs[w].at[c], out_refs[w].at[me, c], out_refs[w].at[peer, c]
                else:
                    src, dst, land = in_refs[w].at[peer], out_refs[w].at[j], out_refs[w].at[j]
                if landing:
                    src, dst = land, land
                out.append(_remote(src, dst, sems, 3 * w + j, (chip[0], chip[1], c)))
        return out

    def start(self, in_refs, out_refs, sems):
        for cp in self._copies(in_refs, out_refs, sems, False):
            cp.start()

    def finish(self, in_refs, out_refs, sems):
        for cp in self._copies(in_refs, out_refs, sems, True):
            cp.wait_recv()
        for cp in self._copies(in_refs, out_refs, sems, False):
            cp.wait_send()


def _carry(comm, n_in, n_out, first, last, refs):
    if comm is None:
        return refs[:n_in], refs[n_in:n_in + n_out], (lambda: None)
    a, b = len(comm.ins), len(comm.out_shape)
    ins, c_in = refs[:n_in], refs[n_in:n_in + a]
    outs, c_out = refs[n_in + a:n_in + a + n_out], refs[n_in + a + n_out:n_in + a + n_out + b]
    sems = refs[n_in + a + n_out + b:n_in + a + n_out + b + 2]
    pl.when(first)(lambda: comm.start(c_in, c_out, sems))
    return ins, outs, (lambda: pl.when(last)(lambda: comm.finish(c_in, c_out, sems)))


def _sbl_fwd(q, k, v, *, width, hd, name, tq=256, comm=None):
    (qa, qo), (ka, ko), (va, vo) = q, k, v
    S = qa.shape[0]
    tq = _tile(S, tq)
    tk = tq
    scale = hd ** -0.5
    n_g, n_q = width // LANES, S // tq

    def body(*refs):
        qi = pl.program_id(1)
        gi = pl.program_id(0)
        got = _carry(comm, 3, 2, (gi == 0) & (qi == 0), (gi == n_g - 1) & (qi == n_q - 1), refs)
        (q_ref, k_ref, v_ref), (o_ref, tot_ref) = got[0], got[1]
        masks = _lane_masks(hd, tq)
        qs = q_ref[...].astype(F32) * scale
        qm = [_keep(qs, m).astype(BF16) for m in masks]
        strict = lax.broadcasted_iota(jnp.int32, (tq, tk), 1) < lax.broadcasted_iota(jnp.int32, (tq, tk), 0)
        later = _tri("after", tk)

        def tile(kb, carry, diag):
            ks = pl.multiple_of(kb * tk, tk)
            kv = k_ref[pl.ds(ks, tk), :].astype(BF16)
            vv = v_ref[pl.ds(ks, tk), :].astype(BF16)
            out = []
            for hh in range(len(masks)):
                acc, c_pos = carry[2 * hh], carry[2 * hh + 1]
                pos, ls = _softplus_parts(_dot(qm[hh], kv, _NT))
                if diag:
                    pos = jnp.where(strict, pos, 0.0)
                pos_after, pos_all = _scan_cols(pos, later, True)
                w = jnp.exp(ls - (pos_after + c_pos))
                if diag:
                    w = jnp.where(strict, w, 0.0)
                out += [acc + _dot(w.astype(BF16), vv), c_pos + pos_all]
            return tuple(out)

        init = (jnp.zeros((tq, LANES), F32), jnp.zeros((tq, 1), F32)) * len(masks)
        carry = tile(qi, init, True)
        carry = lax.fori_loop(0, qi, lambda i, c: tile(qi - 1 - i, c, False), carry)
        o_ref[...] = _merge(carry[0::2], masks).astype(o_ref.dtype)
        tot_ref[...] = _merge([jnp.broadcast_to(-c, (tq, LANES)) for c in carry[1::2]], masks)
        got[2]()

    c_ins, c_in_specs, c_out_specs, c_out_shape, c_scratch = _comm_args(comm)
    return pl.pallas_call(
        body, name=name, grid=(n_g, n_q),
        in_specs=[_cols(tq, qo), _cols_all(S, ko), _cols_all(S, vo)] + c_in_specs,
        out_specs=[_cols(tq, 0), _cols(tq, 0)] + c_out_specs,
        out_shape=[jax.ShapeDtypeStruct((S, width), BF16), jax.ShapeDtypeStruct((S, width), F32)] + c_out_shape,
        scratch_shapes=c_scratch,
        compiler_params=_params(("arbitrary", "arbitrary")),
    )(qa, ka, va, *c_ins)


def _comm_args(comm):
    if comm is None:
        return [], [], [], [], []
    return comm.ins, [_ANY] * len(comm.ins), [_ANY] * len(comm.out_shape), comm.out_shape, comm.scratch


def _sbl_bwd(q, k, v, do, tot, *, width, hd, name, tq=256, comm=None):
    (qa, qo), (ka, ko), (va, vo) = q, k, v
    S = qa.shape[0]
    tq = _tile(S, tq)
    tk = tq
    scale = hd ** -0.5
    n_g, n_q = width // LANES, S // tq

    def body(*refs):
        qi = pl.program_id(1)
        gi = pl.program_id(0)
        got = _carry(comm, 5, 3, (gi == 0) & (qi == 0), (gi == n_g - 1) & (qi == n_q - 1), refs)
        (q_ref, k_ref, v_ref, do_ref, tot_ref), (dq_ref, dk_ref, dv_ref) = got[0], got[1]

        @pl.when(qi == 0)
        def _():
            dk_ref[...] = jnp.zeros_like(dk_ref)
            dv_ref[...] = jnp.zeros_like(dv_ref)

        masks = _lane_masks(hd, tq)
        qs = q_ref[...].astype(F32) * scale
        qm = [_keep(qs, m).astype(BF16) for m in masks]
        dov = [_keep(do_ref[...], m).astype(BF16) for m in masks]
        rest = [-_row_value(tot_ref[...], m) for m in masks]
        strict = lax.broadcasted_iota(jnp.int32, (tq, tk), 1) < lax.broadcasted_iota(jnp.int32, (tq, tk), 0)
        later, before = _tri("after", tk), _tri("before", tk)

        def tile(kb, carry, diag):
            ks = pl.multiple_of(kb * tk, tk)
            kv = k_ref[pl.ds(ks, tk), :].astype(BF16)
            vv = v_ref[pl.ds(ks, tk), :].astype(BF16)
            out = []
            dk_t, dv_t = None, None
            for hh in range(len(masks)):
                dq, c_pos, c_g = carry[3 * hh:3 * hh + 3]
                pos, ls = _softplus_parts(_dot(qm[hh], kv, _NT))
                if diag:
                    pos = jnp.where(strict, pos, 0.0)
                pos_after, pos_all = _scan_cols(pos, later, True)
                c_pos = c_pos + pos_all
                w = jnp.exp(ls - (pos_after + (rest[hh] - c_pos)))
                if diag:
                    w = jnp.where(strict, w, 0.0)
                g = _dot(dov[hh], vv, _NT) * w
                g_before, g_all = _scan_cols(g, before, False)
                g_before = g_before + c_g
                dz = g - jnp.exp(ls) * (g + g_before)
                if diag:
                    dz = jnp.where(strict, dz, 0.0)
                dzb = dz.astype(BF16)
                dk_h = _dot(dzb, qm[hh], _TN)
                dv_h = _dot(w.astype(BF16), dov[hh], _TN)
                dk_t = dk_h if dk_t is None else dk_t + dk_h
                dv_t = dv_h if dv_t is None else dv_t + dv_h
                out += [dq + _dot(dzb, kv), c_pos, c_g + g_all]
            dk_ref[pl.ds(ks, tk), :] += dk_t
            dv_ref[pl.ds(ks, tk), :] += dv_t
            return tuple(out)

        zero = jnp.zeros((tq, 1), F32)
        init = (jnp.zeros((tq, LANES), F32), zero, zero) * len(masks)
        carry = lax.fori_loop(0, qi, lambda kb, c: tile(kb, c, False), init)
        carry = tile(qi, carry, True)
        dq_ref[...] = _merge(carry[0::3], masks) * scale
        got[2]()

    full = jax.ShapeDtypeStruct((S, width), F32)
    c_ins, c_in_specs, c_out_specs, c_out_shape, c_scratch = _comm_args(comm)
    return pl.pallas_call(
        body, name=name, grid=(n_g, n_q),
        in_specs=[_cols(tq, qo), _cols_all(S, ko), _cols_all(S, vo), _cols(tq, do[1]), _cols(tq, tot[1])] + c_in_specs,
        out_specs=[_cols(tq, 0), _cols_all(S, 0), _cols_all(S, 0)] + c_out_specs,
        out_shape=[full, full, full] + c_out_shape,
        scratch_shapes=c_scratch,
        compiler_params=_params(("arbitrary", "arbitrary")),
    )(qa, ka, va, do[0], tot[0], *c_ins)


def _sml_fwd(q, k, v, bias=None, *, width, hd, causal, name, tq=256, tk=256, comm=None):
    (qa, qo), (ka, ko), (va, vo) = q, k, v
    S, Sk = qa.shape[0], ka.shape[0]
    tq, tk = _tile(S, tq), _tile(Sk, tk)
    if causal:
        assert tq == tk and S == Sk
    nk = Sk // tk
    hpg = LANES // hd
    scale = hd ** -0.5
    has_bias = bias is not None
    n_g, n_q = width // LANES, S // tq

    def body(*all_refs):
        qi, gi = pl.program_id(1), pl.program_id(0)
        got = _carry(comm, 5 if has_bias else 3, 2, (gi == 0) & (qi == 0), (gi == n_g - 1) & (qi == n_q - 1), all_refs)
        refs = tuple(got[0]) + tuple(got[1])
        q_ref, k_ref, v_ref = refs[:3]
        o_ref, lse_ref = refs[-2:]
        masks = _lane_masks(hd, tq)
        qs = q_ref[...].astype(F32) * scale
        qm = [_keep(qs, m).astype(BF16) for m in masks]
        allowed = lax.broadcasted_iota(jnp.int32, (tq, tk), 1) <= lax.broadcasted_iota(jnp.int32, (tq, tk), 0)

        def tile(kb, carry, diag):
            ks = pl.multiple_of(kb * tk, tk)
            kv = k_ref[pl.ds(ks, tk), :].astype(BF16)
            vv = v_ref[pl.ds(ks, tk), :].astype(BF16)
            out = []
            for hh in range(hpg):
                m, l, acc = carry[3 * hh:3 * hh + 3]
                z = _dot(qm[hh], kv, _NT)
                if has_bias:
                    z = z + refs[3][hh] - refs[4][hh, kb]
                if diag:
                    z = jnp.where(allowed, z, NEG_INF)
                m2 = jnp.maximum(m, jnp.max(z, axis=1, keepdims=True))
                p = jnp.exp(z - m2)
                alpha = jnp.exp(m - m2)
                out += [m2, alpha * l + jnp.sum(p, axis=1, keepdims=True), alpha * acc + _dot(p.astype(BF16), vv)]
            return tuple(out)

        init = (jnp.full((tq, 1), NEG_INF, F32), jnp.zeros((tq, 1), F32), jnp.zeros((tq, LANES), F32)) * hpg
        if causal:
            carry = lax.fori_loop(0, qi, lambda kb, c: tile(kb, c, False), init)
            carry = tile(qi, carry, True)
        else:
            carry = lax.fori_loop(0, nk, lambda kb, c: tile(kb, c, False), init)
        o_ref[...] = _merge([acc / l for l, acc in zip(carry[1::3], carry[2::3])], masks).astype(o_ref.dtype)
        lse_ref[...] = _merge([jnp.broadcast_to(m + jnp.log(l), (tq, LANES)) for m, l in zip(carry[0::3], carry[1::3])], masks)
        got[2]()

    in_specs = [_cols(tq, qo), _cols_all(Sk, ko), _cols_all(Sk, vo)]
    ins = [qa, ka, va]
    if has_bias:
        in_specs += [pl.BlockSpec((hpg, tq, 1), lambda g, i: (g, i, 0)),
                     pl.BlockSpec((hpg, nk, 1, tk), lambda g, i: (g, 0, 0, 0))]
        ins += list(bias)
    c_ins, c_in_specs, c_out_specs, c_out_shape, c_scratch = _comm_args(comm)
    return pl.pallas_call(
        body, name=name, grid=(n_g, n_q),
        in_specs=in_specs + c_in_specs, out_specs=[_cols(tq, 0), _cols(tq, 0)] + c_out_specs,
        out_shape=[jax.ShapeDtypeStruct((S, width), BF16), jax.ShapeDtypeStruct((S, width), F32)] + c_out_shape,
        scratch_shapes=c_scratch,
        compiler_params=_params(("arbitrary", "arbitrary") if comm is not None else ("parallel", "arbitrary")),
    )(*ins, *c_ins)


def _sml_bwd(q, k, v, o, lse, do, bias=None, *, width, hd, causal, name, tq=256, tk=256, comm=None):
    (qa, qo), (ka, ko), (va, vo) = q, k, v
    S, Sk = qa.shape[0], ka.shape[0]
    tq, tk = _tile(S, tq), _tile(Sk, tk)
    nk = Sk // tk
    hpg = LANES // hd
    scale = hd ** -0.5
    has_bias = bias is not None
    n_in = 8 if has_bias else 6
    n_g, n_q = width // LANES, S // tq

    def body(*all_refs):
        qi, gi = pl.program_id(1), pl.program_id(0)
        got = _carry(comm, n_in, 5 if has_bias else 3, (gi == 0) & (qi == 0), (gi == n_g - 1) & (qi == n_q - 1), all_refs)
        refs = tuple(got[0]) + tuple(got[1])
        q_ref, k_ref, v_ref, o_ref, lse_ref, do_ref = refs[:6]
        dq_ref, dk_ref, dv_ref = refs[n_in:n_in + 3]

        @pl.when(qi == 0)
        def _():
            dk_ref[...] = jnp.zeros_like(dk_ref)
            dv_ref[...] = jnp.zeros_like(dv_ref)
            if has_bias:
                refs[n_in + 4][...] = jnp.zeros_like(refs[n_in + 4])

        masks = _lane_masks(hd, tq)
        qs = q_ref[...].astype(F32) * scale
        qm = [_keep(qs, m).astype(BF16) for m in masks]
        do32 = do_ref[...]
        dov = [_keep(do32, m).astype(BF16) for m in masks]
        prod = do32 * o_ref[...].astype(F32)
        delta = [jnp.sum(_keep(prod, m), axis=1, keepdims=True) for m in masks]
        lses = [_row_value(lse_ref[...], m) for m in masks]
        allowed = lax.broadcasted_iota(jnp.int32, (tq, tk), 1) <= lax.broadcasted_iota(jnp.int32, (tq, tk), 0)

        def tile(kb, carry, diag):
            ks = pl.multiple_of(kb * tk, tk)
            kv = k_ref[pl.ds(ks, tk), :].astype(BF16)
            vv = v_ref[pl.ds(ks, tk), :].astype(BF16)
            out = []
            dk_t, dv_t = None, None
            for hh in range(hpg):
                dq, db_row = carry[2 * hh:2 * hh + 2]
                z = _dot(qm[hh], kv, _NT)
                if has_bias:
                    z = z + refs[6][hh] - refs[7][hh, kb]
                p = jnp.exp(z - lses[hh])
                if diag:
                    p = jnp.where(allowed, p, 0.0)
                dz = p * (_dot(dov[hh], vv, _NT) - delta[hh])
                dzb = dz.astype(BF16)
                dk_h = _dot(dzb, qm[hh], _TN)
                dv_h = _dot(p.astype(BF16), dov[hh], _TN)
                dk_t = dk_h if dk_t is None else dk_t + dk_h
                dv_t = dv_h if dv_t is None else dv_t + dv_h
                if has_bias:
                    db_row = db_row + jnp.sum(dz, axis=1, keepdims=True)
                    refs[n_in + 4][hh, kb] += jnp.sum(dz, axis=0, keepdims=True)
                out += [dq + _dot(dzb, kv), db_row]
            dk_ref[pl.ds(ks, tk), :] += dk_t
            dv_ref[pl.ds(ks, tk), :] += dv_t
            return tuple(out)

        init = (jnp.zeros((tq, LANES), F32), jnp.zeros((tq, 1), F32)) * hpg
        if causal:
            carry = lax.fori_loop(0, qi, lambda kb, c: tile(kb, c, False), init)
            carry = tile(qi, carry, True)
        else:
            carry = lax.fori_loop(0, nk, lambda kb, c: tile(kb, c, False), init)
        dq_ref[...] = _merge(carry[0::2], masks) * scale
        if has_bias:
            for hh in range(hpg):
                refs[n_in + 3][hh] = carry[2 * hh + 1]
        got[2]()

    in_specs = [_cols(tq, qo), _cols_all(Sk, ko), _cols_all(Sk, vo), _cols(tq, o[1]), _cols(tq, lse[1]), _cols(tq, do[1])]
    ins = [qa, ka, va, o[0], lse[0], do[0]]
    out_specs = [_cols(tq, 0), _cols_all(Sk, 0), _cols_all(Sk, 0)]
    out_shape = [jax.ShapeDtypeStruct((S, width), F32), jax.ShapeDtypeStruct((Sk, width), F32),
                 jax.ShapeDtypeStruct((Sk, width), F32)]
    if has_bias:
        rspec = pl.BlockSpec((hpg, tq, 1), lambda g, i: (g, i, 0))
        cspec = pl.BlockSpec((hpg, nk, 1, tk), lambda g, i: (g, 0, 0, 0))
        in_specs += [rspec, cspec]
        ins += list(bias)
        out_specs += [rspec, cspec]
        n_heads = width // hd
        out_shape += [jax.ShapeDtypeStruct((n_heads, S, 1), F32), jax.ShapeDtypeStruct((n_heads, nk, 1, tk), F32)]
    c_ins, c_in_specs, c_out_specs, c_out_shape, c_scratch = _comm_args(comm)
    return pl.pallas_call(
        body, name=name, grid=(n_g, n_q),
        in_specs=in_specs + c_in_specs, out_specs=out_specs + c_out_specs, out_shape=out_shape + c_out_shape,
        scratch_shapes=c_scratch,
        compiler_params=_params(("arbitrary", "arbitrary") if comm is not None else ("parallel", "arbitrary")),
    )(*ins, *c_ins)


def _head_sums(t, masks):
    sums = [jnp.sum(_keep(t, m), axis=1, keepdims=True) for m in masks]
    return _merge([jnp.broadcast_to(s, t.shape) for s in sums], masks) if len(masks) > 1 else sums[0]


def _hnorm_fwd(x, g_lanes, *, width, hd, name, tr=512):
    xa, xo = x
    R = xa.shape[0]
    tr = _tile(R, tr, align=16)
    n_blk = width // LANES

    def body(x_ref, g_ref, o_ref):
        masks = _lane_masks(hd, tr)
        for j in range(n_blk):
            sl = slice(j * LANES, (j + 1) * LANES)
            xv = x_ref[:, sl].astype(F32)
            r = lax.rsqrt(_head_sums(xv * xv, masks) * (1.0 / hd) + EPS)
            o_ref[:, sl] = (xv * r * g_ref[...]).astype(o_ref.dtype)

    assert (xo * LANES) % width == 0
    return pl.pallas_call(
        body, name=name, grid=(R // tr,),
        in_specs=[pl.BlockSpec((tr, width), lambda i: (i, xo * LANES // width)), pl.BlockSpec((1, LANES), lambda i: (0, 0))],
        out_specs=pl.BlockSpec((tr, width), lambda i: (i, 0)),
        out_shape=jax.ShapeDtypeStruct((R, width), BF16),
        compiler_params=_params(("parallel",)),
    )(xa, g_lanes)


def _hnorm_bwd(x, g_lanes, dy, *, width, hd, name, tr=512):
    xa, xo = x
    R = xa.shape[0]
    tr = _tile(R, tr, align=16)
    n_blk = width // LANES

    def body(x_ref, g_ref, dy_ref, dx_ref, dg_ref):
        masks = _lane_masks(hd, tr)
        dg = jnp.zeros((1, LANES), F32)
        for j in range(n_blk):
            sl = slice(j * LANES, (j + 1) * LANES)
            xv = x_ref[:, sl].astype(F32)
            dyv = dy_ref[:, sl].astype(F32)
            r = lax.rsqrt(_head_sums(xv * xv, masks) * (1.0 / hd) + EPS)
            xh = xv * r
            dyg = dyv * g_ref[...]
            c = _head_sums(dyg * xh, masks) * (1.0 / hd)
            dx_ref[:, sl] = (r * (dyg - xh * c)).astype(dx_ref.dtype)
            dg = dg + jnp.sum(dyv * xh, axis=0, keepdims=True)
        if hd * 2 == LANES:
            dg8 = jnp.broadcast_to(dg, (8, LANES))
            dg = (dg8 + pltpu.roll(dg8, shift=hd, axis=1))[0:1]
        else:
            assert hd == LANES

        @pl.when(pl.program_id(0) == 0)
        def _():
            dg_ref[...] = jnp.zeros_like(dg_ref)

        dg_ref[...] += dg

    assert (xo * LANES) % width == 0
    return pl.pallas_call(
        body, name=name, grid=(R // tr,),
        in_specs=[pl.BlockSpec((tr, width), lambda i: (i, xo * LANES // width)), pl.BlockSpec((1, LANES), lambda i: (0, 0)),
                  pl.BlockSpec((tr, width), lambda i: (i, 0))],
        out_specs=[pl.BlockSpec((tr, width), lambda i: (i, 0)), pl.BlockSpec((1, LANES), lambda i: (0, 0))],
        out_shape=[jax.ShapeDtypeStruct((R, width), BF16), jax.ShapeDtypeStruct((1, LANES), F32)],
        compiler_params=_params(("arbitrary",)),
    )(xa, g_lanes, dy)


def _split3_dot(x, tri):
    a = x.astype(BF16)
    r = x - a.astype(F32)
    b = r.astype(BF16)
    c = (r - b.astype(F32)).astype(BF16)
    return _dot(a, tri) + _dot(b, tri) + _dot(c, tri)


def _forget_fwd(logit_t, b_col, *, name, blk=512):
    H, S = logit_t.shape
    blk = _tile(S, blk)

    def body(l_ref, b_ref, f_ref):
        r_i = lax.broadcasted_iota(jnp.int32, (blk, blk), 0)
        c_i = lax.broadcasted_iota(jnp.int32, (blk, blk), 1)
        upto = (r_i <= c_i).astype(BF16)
        carry = jnp.zeros((H, 1), F32)
        for j in range(S // blk):
            u = l_ref[:, j * blk:(j + 1) * blk] + b_ref[...]
            lf, _ = _log_sigmoid_pair(u)
            f_ref[:, j * blk:(j + 1) * blk] = _split3_dot(lf, upto) + carry
            carry = carry + jnp.sum(lf, axis=1, keepdims=True)

    return pl.pallas_call(
        body, name=name,
        out_shape=jax.ShapeDtypeStruct((H, S), F32),
        compiler_params=pltpu.CompilerParams(vmem_limit_bytes=VMEM_LIMIT),
    )(logit_t, b_col)


def _forget_bwd(logit_t, b_col, d_f, *, name, blk=512):
    H, S = logit_t.shape
    blk = _tile(S, blk)

    def body(l_ref, b_ref, df_ref, dl_ref, db_ref):
        r_i = lax.broadcasted_iota(jnp.int32, (blk, blk), 0)
        c_i = lax.broadcasted_iota(jnp.int32, (blk, blk), 1)
        fromon = (r_i >= c_i).astype(BF16)
        carry = jnp.zeros((H, 1), F32)
        db = jnp.zeros((H, 1), F32)
        for j in reversed(range(S // blk)):
            sl = slice(j * blk, (j + 1) * blk)
            dfv = df_ref[:, sl]
            d_lf = _split3_dot(dfv, fromon) + carry
            carry = carry + jnp.sum(dfv, axis=1, keepdims=True)
            u = l_ref[:, sl] + b_ref[...]
            _, lsn = _log_sigmoid_pair(u)
            dl = d_lf * jnp.exp(lsn)
            dl_ref[:, sl] = dl
            db = db + jnp.sum(dl, axis=1, keepdims=True)
        db_ref[...] = db

    return pl.pallas_call(
        body, name=name,
        out_shape=[jax.ShapeDtypeStruct((H, S), F32), jax.ShapeDtypeStruct((H, 1), F32)],
        compiler_params=pltpu.CompilerParams(vmem_limit_bytes=VMEM_LIMIT),
    )(logit_t, b_col, d_f)


def _sigmoid(t):
    return 1.0 / (1.0 + jnp.exp(-t))


def _gate_fwd(o3, w3, proj, D, *, name, tm=256):
    S = proj.shape[0]
    tm = _tile(S, tm)

    def body(o0, o1, o2, w0, w1, w2, g0, g1, g2, out_ref):
        acc = None
        for o_ref, w_ref, g_ref in ((o0, w0, g0), (o1, w1, g1), (o2, w2, g2)):
            t = _sigmoid(g_ref[...]) * _dot(o_ref[...], w_ref[...])
            acc = t if acc is None else acc + t
        out_ref[...] = acc.astype(out_ref.dtype)

    ospec = lambda d: pl.BlockSpec((tm, d), lambda i: (i, 0))
    wspec = lambda w: pl.BlockSpec(w.shape, lambda i: (0, 0))
    gspec = lambda j: pl.BlockSpec((tm, D), lambda i: (i, j))
    return pl.pallas_call(
        body, name=name, grid=(S // tm,),
        in_specs=[ospec(o.shape[1]) for o in o3] + [wspec(w) for w in w3] + [gspec(j) for j in range(3)],
        out_specs=pl.BlockSpec((tm, D), lambda i: (i, 0)),
        out_shape=jax.ShapeDtypeStruct((S, D), BF16),
        compiler_params=_params(("parallel",)),
    )(*o3, *w3, proj, proj, proj)


def _gate_bwd(o3, w3, proj, dmerged, D, *, name, tm=256):
    S = proj.shape[0]
    tm = _tile(S, tm)

    def body(o0, o1, o2, w0, w1, w2, g0, g1, g2, dm_ref, dg_ref, db0, db1, db2, do0, do1, do2):
        dm = dm_ref[...]
        for j, (o_ref, w_ref, g_ref, db_ref, do_ref) in enumerate(
                ((o0, w0, g0, db0, do0), (o1, w1, g1, db1, do1), (o2, w2, g2, db2, do2))):
            s = _sigmoid(g_ref[...])
            br = _dot(o_ref[...], w_ref[...])
            dg_ref[:, j * D:(j + 1) * D] = (dm * br * s * (1.0 - s)).astype(dg_ref.dtype)
            dbr = (dm * s).astype(BF16)
            db_ref[...] = dbr
            do_ref[...] = _dot(dbr, w_ref[...], _NT)

    ospec = lambda d: pl.BlockSpec((tm, d), lambda i: (i, 0))
    wspec = lambda w: pl.BlockSpec(w.shape, lambda i: (0, 0))
    gspec = lambda j: pl.BlockSpec((tm, D), lambda i: (i, j))
    dspec = pl.BlockSpec((tm, D), lambda i: (i, 0))
    return pl.pallas_call(
        body, name=name, grid=(S // tm,),
        in_specs=[ospec(o.shape[1]) for o in o3] + [wspec(w) for w in w3] + [gspec(j) for j in range(3)] + [dspec],
        out_specs=[pl.BlockSpec((tm, 3 * D), lambda i: (i, 0))] + [dspec] * 3 + [ospec(o.shape[1]) for o in o3],
        out_shape=[jax.ShapeDtypeStruct((S, proj.shape[1]), BF16)] + [jax.ShapeDtypeStruct((S, D), BF16)] * 3
        + [jax.ShapeDtypeStruct((S, o.shape[1]), F32) for o in o3],
        compiler_params=_params(("parallel",)),
    )(*o3, *w3, proj, proj, proj, dmerged)


def _loss_sum(dy, D, *, name):
    R, C = dy.shape
    tr = _row_tile(R, C, 2)

    def body(dy_ref, out_ref):
        @pl.when(pl.program_id(0) == 0)
        def _():
            out_ref[...] = jnp.zeros_like(out_ref)

        v = dy_ref[...]
        out_ref[...] += (0.5 * D) * jnp.sum(v * v)

    return pl.pallas_call(
        body, name=name, grid=(R // tr,),
        in_specs=[pl.BlockSpec((tr, C), lambda i: (i, 0))],
        out_specs=pl.BlockSpec((8, 128), lambda i: (0, 0)),
        out_shape=jax.ShapeDtypeStruct((8, 128), F32),
        compiler_params=_params(("arbitrary",)),
    )(dy)[0, 0]


def _adamw_math(w, g, m, v):
    m2 = ADAM_B1 * m + (1.0 - ADAM_B1) * g
    v2 = ADAM_B2 * v + (1.0 - ADAM_B2) * (g * g)
    m_hat = m2 / (1.0 - ADAM_B1 ** ADAM_STEP)
    v_hat = v2 / (1.0 - ADAM_B2 ** ADAM_STEP)
    delta = -ADAM_LR * (m_hat / (jnp.sqrt(v_hat) + ADAM_EPS) + ADAM_WD * w)
    return delta, m2, v2


def _adamw(w, g, m, v, *, name):
    return _ew(_adamw_math, [w, g, m, v], (F32, F32, F32), name=name)


def _adamw_small(w, parts, m, v, *, name):
    n = parts.shape[0]

    def body(w_ref, p_ref, m_ref, v_ref, g_ref, d_ref, m2_ref, v2_ref):
        g = p_ref[0]
        for i in range(1, n):
            g = g + p_ref[i]
        g_ref[...] = g
        d_ref[...], m2_ref[...], v2_ref[...] = _adamw_math(w_ref[...], g, m_ref[...], v_ref[...])

    shp = jax.ShapeDtypeStruct(w.shape, F32)
    return pl.pallas_call(body, name=name, out_shape=[shp] * 4)(w, parts, m, v)


_ANY = pl.BlockSpec(memory_space=pl.ANY)


def _mesh_place():
    x, y, c = lax.axis_index("x"), lax.axis_index("y"), lax.axis_index("c")
    chips = [(1 - x, y), (x, 1 - y), (1 - x, 1 - y)]
    return x, y, c, chips


def _remote(src, dst, sems, i, to):
    send_sems, recv_sems = sems
    return pltpu.make_async_remote_copy(src_ref=src, dst_ref=dst, send_sem=send_sems.at[i], recv_sem=recv_sems.at[i],
                                        device_id=to, device_id_type=MESH_ID)


def _gather_weights(shards, *, name):
    n = len(shards)

    def body(*refs):
        ins, outs = refs[:n], refs[n:2 * n]
        sems = refs[2 * n:2 * n + 2]
        x, y, c, chips = _mesh_place()
        me = 2 * x + y
        sibling = (x, y, 1 - c)
        sent = []
        for w in range(n):
            for j, chip in enumerate(chips):
                cp = _remote(ins[w].at[c], outs[w].at[me, c], sems, 6 * w + j, (chip[0], chip[1], c))
                cp.start()
                sent.append(cp)
        for w in range(n):
            for j, chip in enumerate(chips):
                got = outs[w].at[2 * chip[0] + chip[1], c]
                _remote(got, got, sems, 6 * w + j, sibling).wait_recv()
                cp = _remote(got, got, sems, 6 * w + 3 + j, sibling)
                cp.start()
                sent.append(cp)
        for w in range(n):
            for j, chip in enumerate(chips):
                got = outs[w].at[2 * chip[0] + chip[1], 1 - c]
                _remote(got, got, sems, 6 * w + 3 + j, sibling).wait_recv()
        for cp in sent:
            cp.wait_send()

    outs = pl.pallas_call(
        body, name=name,
        in_specs=[_ANY] * n, out_specs=[_ANY] * n,
        out_shape=[jax.ShapeDtypeStruct((N_CHIPS,) + s.shape, s.dtype) for s in shards],
        scratch_shapes=[pltpu.SemaphoreType.DMA((6 * n,)), pltpu.SemaphoreType.DMA((6 * n,))],
    )(*shards)
    me = 2 * lax.axis_index("x") + lax.axis_index("y")
    return [lax.dynamic_update_index_in_dim(o, s, me, 0) for o, s in zip(outs, shards)]


def _forward_halves(gathered, *, name):
    n = len(gathered)

    def body(*refs):
        ins, outs = refs[:n], refs[n:2 * n]
        sems = refs[2 * n:2 * n + 2]
        x, y, c, chips = _mesh_place()
        sibling = (x, y, 1 - c)
        sent = []
        for w in range(n):
            for j, chip in enumerate(chips):
                peer = 2 * chip[0] + chip[1]
                cp = _remote(ins[w].at[peer, c], outs[w].at[peer, c], sems, 3 * w + j, sibling)
                cp.start()
                sent.append(cp)
        for w in range(n):
            for j, chip in enumerate(chips):
                land = outs[w].at[2 * chip[0] + chip[1], 1 - c]
                _remote(land, land, sems, 3 * w + j, sibling).wait_recv()
        for cp in sent:
            cp.wait_send()

    return pl.pallas_call(
        body, name=name,
        in_specs=[_ANY] * n, out_specs=[_ANY] * n,
        out_shape=[jax.ShapeDtypeStruct(g.shape, g.dtype) for g in gathered],
        input_output_aliases={w: w for w in range(n)},
        scratch_shapes=[pltpu.SemaphoreType.DMA((3 * n,)), pltpu.SemaphoreType.DMA((3 * n,))],
    )(*gathered)


def _exchange_siblings(grads, *, name):
    n = len(grads)

    def body(*refs):
        ins, got = refs[:n], refs[n:2 * n]
        sems = refs[2 * n:2 * n + 2]
        x, y, c, _ = _mesh_place()
        sibling = (x, y, 1 - c)
        sent = []
        for w in range(n):
            for s in range(N_CHIPS):
                cp = _remote(ins[w].at[s, 1 - c], got[w].at[s], sems, N_CHIPS * w + s, sibling)
                cp.start()
                sent.append(cp)
        for w in range(n):
            for s in range(N_CHIPS):
                _remote(got[w].at[s], got[w].at[s], sems, N_CHIPS * w + s, sibling).wait_recv()
        for cp in sent:
            cp.wait_send()

    n_sem = N_CHIPS * n
    got = pl.pallas_call(
        body, name=name,
        in_specs=[_ANY] * n, out_specs=[_ANY] * n,
        out_shape=[jax.ShapeDtypeStruct((N_CHIPS,) + g.shape[2:], g.dtype) for g in grads],
        scratch_shapes=[pltpu.SemaphoreType.DMA((n_sem,)), pltpu.SemaphoreType.DMA((n_sem,))],
    )(*grads)
    c = lax.axis_index("c")
    return [lax.dynamic_index_in_dim(g, c, 1, keepdims=False) for g in grads], got


def _share_halves(halves, small):
    n = len(halves)

    def body(*refs):
        ins, small_ref = refs[:n], refs[n]
        outs, small_out = refs[n + 1:2 * n + 1], refs[2 * n + 1]
        sems = refs[2 * n + 2:2 * n + 4]
        x, y, c, chips = _mesh_place()
        sibling = (x, y, 1 - c)
        me = 4 * x + 2 * y + c
        sent = [_remote(ins[w], outs[w].at[c], sems, w, sibling) for w in range(n)]
        peers = [sibling] + [(ch[0], ch[1], cc) for ch in chips for cc in (c, 1 - c)]
        sent += [_remote(small_ref, small_out.at[me], sems, n + j, peer) for j, peer in enumerate(peers)]
        for cp in sent:
            cp.start()
        for w in range(n):
            _remote(outs[w].at[1 - c], outs[w].at[1 - c], sems, w, sibling).wait_recv()
        for j, peer in enumerate(peers):
            frm = small_out.at[4 * peer[0] + 2 * peer[1] + peer[2]]
            _remote(frm, frm, sems, n + j, peer).wait_recv()
        for cp in sent:
            cp.wait_send()

    n_sem = n + 7
    outs = pl.pallas_call(
        body, name="share_halves",
        in_specs=[_ANY] * (n + 1), out_specs=[_ANY] * (n + 1),
        out_shape=[jax.ShapeDtypeStruct((2,) + h.shape, h.dtype) for h in halves]
        + [jax.ShapeDtypeStruct((8,) + small.shape, small.dtype)],
        scratch_shapes=[pltpu.SemaphoreType.DMA((n_sem,)), pltpu.SemaphoreType.DMA((n_sem,))],
    )(*halves, small)
    c = lax.axis_index("c")
    me = 4 * lax.axis_index("x") + 2 * lax.axis_index("y") + c
    return ([lax.dynamic_update_index_in_dim(o, h, c, 0) for o, h in zip(outs[:n], halves)],
            lax.dynamic_update_index_in_dim(outs[n], small, me, 0))


EARLY = ("w_ff_down", "w_ff_up", "w_out", "w_branch_sb", "w_branch_fox", "w_branch_mem")


def _split(outs, n):
    outs = list(outs) if isinstance(outs, (list, tuple)) else [outs]
    return outs[:n], outs[n:]


def _local_step(x, mem, target, small, W, gather_rest=None, reduce_early=None, reduce_late=None):
    S, D = x.shape
    o_qkv, o_mq, o_f = 3 * D, 3 * D + 2 * 3 * D_SB, 3 * D + 2 * 3 * D_SB + D_MEM
    tq = 512

    g_comms, finish_weights = gather_rest if gather_rest is not None else ([None] * 3, None)
    h = _rmsnorm_fwd(x, small["g_mix_norm"], BF16, name="mix_norm")
    (proj,), landed = _split(_mm(h, W["w_in"], name="in_proj", tb=True, tn=768, comm=g_comms[0]), 1)
    blk = lambda j: (proj, (o_qkv + j * D_SB) // LANES)
    sb_q, sb_k, sb_v, fx_q, fx_k, fx_v = [blk(j) for j in range(6)]
    m_q = (proj, o_mq // LANES)
    f_logit_t = _mm(W["w_in"][o_f:o_f + ROW_TILE], h, name="forget_logits", tb=True)[:FOX_HEADS]
    b_col = small["b_forget"].reshape(FOX_HEADS, 1)
    lanes = lambda g: jnp.tile(g, (1, LANES // g.shape[1]))
    g_fq, g_fk, g_mq, g_mk = [lanes(small[k]) for k in ("g_fox_q", "g_fox_k", "g_mem_q", "g_mem_k")]

    (o_sb, sb_tot), more = _split(_sbl_fwd(sb_q, sb_k, sb_v, width=D_SB, hd=HD, name="sb_fwd", tq=tq, comm=g_comms[1]), 2)
    landed += more

    fq = _hnorm_fwd(fx_q, g_fq, width=D_FOX, hd=HD, name="fox_q_norm")
    fk = _hnorm_fwd(fx_k, g_fk, width=D_FOX, hd=HD, name="fox_k_norm")
    f_cum = _forget_fwd(f_logit_t, b_col, name="forget_fwd")
    tkf = _tile(S, tq)
    f_bias = (f_cum.reshape(FOX_HEADS, S, 1), f_cum.reshape(FOX_HEADS, S // tkf, 1, tkf))
    (o_fox, fox_lse), more = _split(_sml_fwd((fq, 0), (fk, 0), fx_v, f_bias, width=D_FOX, hd=HD, causal=True,
                                             name="fox_fwd", tq=tq, tk=tq, comm=g_comms[2]), 2)
    landed += more
    if finish_weights is not None:
        W = {**W, **finish_weights(landed)}

    mh = _rmsnorm_fwd(mem, small["g_mem_norm"], BF16, name="mem_norm")
    mkv = _mm(mh, W["w_mem_kv"], name="mem_kv")
    mv = (mkv, D_MEM // LANES)
    mq = _hnorm_fwd(m_q, g_mq, width=D_MEM, hd=MEM_HD, name="mem_q_norm")
    mk = _hnorm_fwd((mkv, 0), g_mk, width=D_MEM, hd=MEM_HD, name="mem_k_norm")
    o_mem, mem_lse = _sml_fwd((mq, 0), (mk, 0), mv, width=D_MEM, hd=MEM_HD, causal=False, name="mem_fwd", tq=tq, tk=256)

    o3 = [o_sb, o_fox, o_mem]
    w3 = [W["w_branch_sb"], W["w_branch_fox"], W["w_branch_mem"]]
    merged = _gate_fwd(o3, w3, proj, D, name="gate_fwd")
    x1 = _mm(merged, W["w_out"], name="out_proj", extras=(x,), epilogue=lambda acc, res: (res + acc,))
    h2 = _rmsnorm_fwd(x1, small["g_mlp_norm"], BF16, name="mlp_norm")

    def relu2(acc):
        u = jnp.maximum(acc, 0.0)
        return u, u * u

    u, a = _mm(h2, W["w_ff_up"], name="ff_up", out_dtypes=(BF16, BF16), epilogue=relu2)
    def head(acc, res, tgt):
        d = (res + acc - tgt) * (1.0 / D)
        return d, d

    dy, dy16 = _mm(a, W["w_ff_down"], name="ff_down", extras=(x1, target), out_dtypes=(F32, BF16), epilogue=head, tm=512)
    loss = _loss_sum(dy, D, name="loss")

    G = {}
    du = _mm(dy16, W["w_ff_down"], name="d_ff_act", tb=True, out_dtypes=(BF16,), extras=(u,),
             epilogue=lambda acc, uu: (acc * (2.0 * uu.astype(F32)),))
    G["w_ff_down"] = _mm(a, dy16, name="d_w_ff_down", ta=True, out_dtypes=(BF16,))
    G["w_ff_up"] = _mm(h2, du, name="d_w_ff_up", ta=True, out_dtypes=(BF16,))
    dh2 = _mm(du, W["w_ff_up"], name="d_mlp_in", tb=True)
    dx1, dg_mlp = _rmsnorm_bwd(x1, small["g_mlp_norm"], dh2, add=dy, name="d_mlp_norm")
    dmerged = _mm(dx1, W["w_out"], name="d_merged", tb=True)
    G["w_out"] = _mm(merged, dx1, name="d_w_out", ta=True, out_dtypes=(BF16,))
    dgate, db0, db1, db2, do_sb, do_fox, do_mem = _gate_bwd(o3, w3, proj, dmerged, D, name="gate_bwd")
    for nm, o, db in zip(("w_branch_sb", "w_branch_fox", "w_branch_mem"), o3, (db0, db1, db2)):
        G[nm] = _mm(o, db, name="d_" + nm, ta=True, out_dtypes=(BF16,))

    r_comms, r_finish = reduce_early({k: G.pop(k) for k in EARLY}) if reduce_early is not None else ([None] * 2, None)
    dsb, landed_sb = _split(_sbl_bwd(sb_q, sb_k, sb_v, (do_sb, 0), (sb_tot, 0), width=D_SB, hd=HD, name="sb_bwd", tq=tq,
                                     comm=r_comms[0]), 3)
    (dfq, dfk, dfv, df_row, df_col), landed_fox = _split(
        _sml_bwd((fq, 0), (fk, 0), fx_v, (o_fox, 0), (fox_lse, 0), (do_fox, 0), f_bias, width=D_FOX, hd=HD, causal=True,
                 name="fox_bwd", tq=tq, tk=tq, comm=r_comms[1]), 5)
    early = r_finish(landed_sb, landed_fox) if r_finish is not None else {}
    dfx_q, dg_fox_q = _hnorm_bwd(fx_q, g_fq, dfq, width=D_FOX, hd=HD, name="d_fox_q_norm")
    dfx_k, dg_fox_k = _hnorm_bwd(fx_k, g_fk, dfk, width=D_FOX, hd=HD, name="d_fox_k_norm")
    d_fcum = df_row.reshape(FOX_HEADS, S) - df_col.reshape(FOX_HEADS, S)
    d_flogit_t, db_forget = _forget_bwd(f_logit_t, b_col, d_fcum, name="forget_bwd")

    dmq_n, dmk_n, dmv = _sml_bwd((mq, 0), (mk, 0), mv, (o_mem, 0), (mem_lse, 0), (do_mem, 0), width=D_MEM, hd=MEM_HD,
                                 causal=False, name="mem_bwd", tq=tq, tk=256)
    dm_q, dg_mem_q = _hnorm_bwd(m_q, g_mq, dmq_n, width=D_MEM, hd=MEM_HD, name="d_mem_q_norm")
    dmk_raw, dg_mem_k = _hnorm_bwd((mkv, 0), g_mk, dmk_n, width=D_MEM, hd=MEM_HD, name="d_mem_k_norm")
    dmkv = jnp.concatenate([dmk_raw, dmv.astype(BF16)], axis=1)
    G["w_mem_kv"] = _mm(mh, dmkv, name="d_w_mem_kv", ta=True, out_dtypes=(BF16,))
    dmh = _mm(dmkv, W["w_mem_kv"], name="d_mem_h", tb=True)
    _, dg_mem = _rmsnorm_bwd(mem, small["g_mem_norm"], dmh, name="d_mem_norm")
    dg_fox_q, dg_fox_k = dg_fox_q[:, :HD], dg_fox_k[:, :HD]

    rest_cols = jnp.concatenate([t.astype(BF16) for t in (*dsb, dfx_q, dfx_k, dfv, dm_q)]
                                + [d_flogit_t.T.astype(BF16), jnp.zeros((S, F_PAD - FOX_HEADS), BF16)], axis=1)
    dproj = lax.dynamic_update_slice(dgate, rest_cols, (0, 3 * D))
    g_w_in = _mm(dproj, h, name="d_w_in", ta=True, out_dtypes=(BF16,), tm=768)
    if reduce_late is None:
        G["w_in"] = g_w_in
        dh = _mm(dproj, W["w_in"], name="d_mix_in", tk=2304)
    else:
        comm, finish = reduce_late({"w_in": g_w_in, "w_mem_kv": G.pop("w_mem_kv")})
        dh, *landed = _mm(dproj, W["w_in"], name="d_mix_in", tk=2304, comm=comm)
        early.update(finish(landed))
    grad_x, dg_mix = _rmsnorm_bwd(x, small["g_mix_norm"], dh, add=dx1, name="d_mix_norm")

    small_grads = dict(g_mix_norm=dg_mix, g_mem_norm=dg_mem, b_forget=db_forget.reshape(1, FOX_HEADS),
                       g_fox_q=dg_fox_q, g_fox_k=dg_fox_k, g_mem_q=dg_mem_q, g_mem_k=dg_mem_k, g_mlp_norm=dg_mlp)
    return loss, grad_x, G, small_grads, early


BIG = ("w_in", "w_mem_kv", "w_branch_sb", "w_branch_fox", "w_branch_mem", "w_out", "w_ff_up", "w_ff_down")
COLUMN_SHARDED = ("w_in", "w_branch_sb", "w_branch_fox", "w_branch_mem", "w_ff_up")
SMALL = ("g_mix_norm", "g_mem_norm", "b_forget", "g_fox_q", "g_fox_k", "g_mem_q", "g_mem_k", "g_mlp_norm")
ORDER = ("g_mix_norm", "g_mem_norm", "w_in", "b_forget", "g_fox_q", "g_fox_k", "g_mem_q", "g_mem_k", "w_mem_kv",
         "w_branch_sb", "w_branch_fox", "w_branch_mem", "w_out", "g_mlp_norm", "w_ff_up", "w_ff_down")


def _unshard(name, gathered):
    n, _, rh, c = gathered.shape
    t = gathered.reshape(n, 2 * rh, c)
    if name in COLUMN_SHARDED:
        return t.transpose(1, 0, 2).reshape(2 * rh, n * c)
    return t.reshape(n * 2 * rh, c)


def _reshard(name, full):
    if name in COLUMN_SHARDED:
        r, c = full.shape
        t = full.reshape(r, N_CHIPS, c // N_CHIPS).transpose(1, 0, 2)
    else:
        r, c = full.shape[0] // N_CHIPS, full.shape[1]
        t = full.reshape(N_CHIPS, r, c)
    return t.reshape(N_CHIPS, 2, t.shape[1] // 2, t.shape[2])


ROW_TILE = 16
IN_BUF_ALIGN = 256


def _in_segments(D):
    n_qkv = 6 * D_SB
    o_mq, o_gate = n_qkv + FOX_HEADS, n_qkv + FOX_HEADS + D_MEM
    return [(0, n_qkv, 3 * D), (n_qkv, o_mq, 3 * D + n_qkv + D_MEM), (o_mq, o_gate, 3 * D + n_qkv), (o_gate, o_gate + 3 * D, 0)]


class _InLayout:
    def __init__(self, D, shard, n):
        self.D, self.shard, self.n = D, shard, n
        down = lambda v: v // ROW_TILE * ROW_TILE
        up = lambda v: -(-v // ROW_TILE) * ROW_TILE
        self.pieces = []
        ends = []
        for s in range(n):
            cursor, mine = 0, []
            for a, b, p in _in_segments(D):
                x0, x1 = max(a, s * shard), min(b, (s + 1) * shard)
                if x0 < x1:
                    p0 = p + x0 - a
                    rows = up(p0 + x1 - x0) - down(p0)
                    mine.append((x0 - s * shard, x1 - x0, p0, cursor, rows))
                    cursor += rows
            self.pieces.append(mine)
            ends.append(cursor)
        self.rows = -(-max(ends) // IN_BUF_ALIGN) * IN_BUF_ALIGN
        self.padded_rows = 3 * D + 6 * D_SB + D_MEM + F_PAD

    def _per_shard(self, fn, chip, operand):
        return lax.switch(chip, [functools.partial(fn, s) for s in range(self.n)], operand)

    def pack(self, chip, rows):
        def one(s, t):
            out, at = [], 0
            for x0, n_rows, p0, start, region in self.pieces[s]:
                lead = p0 % ROW_TILE
                out += [jnp.zeros((start + lead - at, t.shape[1]), t.dtype), t[x0:x0 + n_rows]]
                at = start + lead + n_rows
            return jnp.concatenate(out + [jnp.zeros((self.rows - at, t.shape[1]), t.dtype)], axis=0)
        return self._per_shard(one, chip, rows)

    def unpack(self, chip, buf, pad_to):
        def one(s, t):
            out = [t[start + p0 % ROW_TILE:start + p0 % ROW_TILE + n_rows] for _, n_rows, p0, start, _ in self.pieces[s]]
            return jnp.concatenate(out + [jnp.zeros((pad_to - self.shard, t.shape[1]), t.dtype)], axis=0)
        return self._per_shard(one, chip, buf)

    def to_padded(self, bufs):
        runs = sorted((p0, s, start, region) for s in range(self.n) for _, _, p0, start, region in self.pieces[s])
        chunks, end = [], 0
        for p0, s, start, region in runs:
            d0 = p0 // ROW_TILE * ROW_TILE
            src = bufs[s, start:start + region]
            if d0 < end:
                assert end - d0 == ROW_TILE
                last = chunks.pop()
                chunks += [last[:-ROW_TILE], last[-ROW_TILE:] + src[:ROW_TILE], src[ROW_TILE:]]
            else:
                if d0 > end:
                    chunks.append(jnp.zeros((d0 - end, bufs.shape[2]), bufs.dtype))
                chunks.append(src)
            end = d0 + region
        chunks.append(jnp.zeros((self.padded_rows - end, bufs.shape[2]), bufs.dtype))
        return jnp.concatenate(chunks, axis=0)

    def from_padded(self, gp):
        bufs = []
        for s in range(self.n):
            out, at = [], 0
            for _, n_rows, p0, start, region in self.pieces[s]:
                d0 = p0 // ROW_TILE * ROW_TILE
                row = d0 + lax.broadcasted_iota(jnp.int32, (region, 1), 0)
                out.append(jnp.where((row >= p0) & (row < p0 + n_rows), gp[d0:d0 + region], jnp.zeros((), gp.dtype)))
                at = start + region
            bufs.append(jnp.concatenate(out + [jnp.zeros((self.rows - at, gp.shape[1]), gp.dtype)], axis=0))
        return jnp.stack(bufs)


def _pack_small(vals):
    width = max(vals[k].shape[1] for k in SMALL)
    return jnp.concatenate([jnp.pad(vals[k].astype(F32), ((0, 0), (0, width - vals[k].shape[1]))) for k in SMALL], axis=0)


def _unpack_small(packed, like):
    return {k: packed[i:i + 1, :like[k].shape[1]] for i, k in enumerate(SMALL)}


def kernel(x, mem, g_mix_norm, g_mem_norm, w_in, b_forget, g_fox_q, g_fox_k, g_mem_q, g_mem_k, w_mem_kv, w_branch_sb, w_branch_fox, w_branch_mem, w_out, g_mlp_norm, w_ff_up, w_ff_down, loss_target, m_g_mix_norm, m_g_mem_norm, m_w_in, m_b_forget, m_g_fox_q, m_g_fox_k, m_g_mem_q, m_g_mem_k, m_w_mem_kv, m_w_branch_sb, m_w_branch_fox, m_w_branch_mem, m_w_out, m_g_mlp_norm, m_w_ff_up, m_w_ff_down, v_g_mix_norm, v_g_mem_norm, v_w_in, v_b_forget, v_g_fox_q, v_g_fox_k, v_g_mem_q, v_g_mem_k, v_w_mem_kv, v_w_branch_sb, v_w_branch_fox, v_w_branch_mem, v_w_out, v_g_mlp_norm, v_w_ff_up, v_w_ff_down):
    given = dict(locals())
    D = x.shape[-1]
    weights = {k: given[k] for k in ORDER}
    moms = {k: given["m_" + k] for k in ORDER}
    vars_ = {k: given["v_" + k] for k in ORDER}

    me_chip = 2 * lax.axis_index("x") + lax.axis_index("y")

    n_in = w_in.shape[2]
    lay = _InLayout(D, n_in, N_CHIPS)
    transposed = lambda t: jnp.transpose(t[0])
    shards = {}
    for k in BIG:
        w = weights[k][0].astype(BF16)
        if k == "w_in":
            w = lay.pack(me_chip, jnp.transpose(w))
        shards[k] = w.reshape(2, w.shape[0] // 2, w.shape[1])
    gathered_in = _gather_weights([shards["w_in"]], name="gather_w_in")[0]
    W = {"w_in": lay.to_padded(gathered_in.reshape(N_CHIPS, lay.rows, D))}
    carried = (("w_ff_up",), ("w_ff_down", "w_mem_kv"), ("w_branch_sb", "w_branch_fox", "w_branch_mem", "w_out"))
    rest = [k for grp in carried for k in grp]
    assert sorted(rest + ["w_in"]) == sorted(BIG)

    def finish_weights(landed):
        full = _forward_halves(landed, name="forward_halves")
        full = [lax.dynamic_update_index_in_dim(o, shards[k], me_chip, 0) for k, o in zip(rest, full)]
        return {k: _unshard(k, g) for k, g in zip(rest, full)}

    def sum_pairs(names, own, sib):
        add2 = lambda p, q: (p.astype(F32) + q.astype(F32),)
        out = []
        for k, p, q in zip(names, own, sib):
            n, r, c = p.shape
            out.append(_ew(add2, [p.reshape(n * r, c), q.reshape(n * r, c)], (BF16,), name="sum_pair_" + k).reshape(n, r, c))
        return out

    def sum_chips(names, parts, got):
        add4 = lambda p, q0, q1, q2: (((p.astype(F32) + q0.astype(F32)) + q1.astype(F32)) + q2.astype(F32),)
        own = [lax.dynamic_index_in_dim(p, me_chip, 0, keepdims=False) for p in parts]
        return {k: _ew(add4, [p, q[0], q[1], q[2]], (F32,), name="sum_chips_" + k) for k, p, q in zip(names, own, got)}

    def pair_sums(grads, tag):
        names = list(grads)
        stacked = {k: _reshard(k, grads[k]) for k in names if k != "w_in"}
        if "w_in" in grads:
            stacked["w_in"] = lay.from_padded(grads["w_in"]).reshape(N_CHIPS, 2, lay.rows // 2, D)
        own, sib = _exchange_siblings([stacked[k] for k in names], name="exchange_siblings_" + tag)
        return dict(zip(names, sum_pairs(names, own, sib)))

    def reduce_early(grads):
        parts = pair_sums(grads, "early")
        groups = [[k for k in parts if k in ("w_ff_down", "w_ff_up")], [k for k in parts if k not in ("w_ff_down", "w_ff_up")]]
        comms = [_ChipExchange("scatter", [parts[k] for k in grp]) for grp in groups]

        def finish(*landed):
            out = {}
            for grp, got in zip(groups, landed):
                out.update(sum_chips(grp, [parts[k] for k in grp], got))
            return out
        return comms, finish

    def reduce_late(grads):
        parts = pair_sums(grads, "late")
        names = list(parts)
        return _ChipExchange("scatter", [parts[k] for k in names]), functools.partial(sum_chips, names, [parts[k] for k in names])

    small = {k: weights[k] for k in SMALL}
    loss_part, grad_x, G, small_grads, halves = _local_step(
        x[0], mem[0], loss_target[0], small, W,
        gather_rest=([_ChipExchange("gather", [shards[k] for k in grp]) for grp in carried], finish_weights),
        reduce_early=reduce_early, reduce_late=reduce_late)
    assert not G, list(G)
    reduced, small_parts = _share_halves([halves[k] for k in BIG], _pack_small(small_grads))

    grads, deltas, new_m, new_v = {}, {}, {}, {}
    for k, g in zip(BIG, reduced):
        shp = weights[k].shape
        if k == "w_in":
            g2 = lay.unpack(me_chip, g.reshape(lay.rows, D), lay.rows)
            padded = lambda t: jnp.pad(transposed(t), ((0, lay.rows - n_in), (0, 0)))
            outs = _adamw(padded(weights[k]), g2, padded(moms[k]), padded(vars_[k]), name="adamw_" + k)
            g2, d, m2, v2 = [jnp.transpose(t[:n_in]) for t in (g2, *outs)]
        else:
            g2 = g.reshape(shp[1], shp[2])
            d, m2, v2 = _adamw(weights[k][0], g2, moms[k][0], vars_[k][0], name="adamw_" + k)
        grads[k], deltas[k], new_m[k], new_v[k] = g2.reshape(shp), d.reshape(shp), m2.reshape(shp), v2.reshape(shp)
    sg, sd, sm, sv = _adamw_small(_pack_small(small), small_parts, _pack_small({k: moms[k] for k in SMALL}),
                                  _pack_small({k: vars_[k] for k in SMALL}), name="adamw_small")
    for dst, packed in ((grads, sg), (deltas, sd), (new_m, sm), (new_v, sv)):
        dst.update(_unpack_small(packed, small))

    loss = lax.psum(loss_part, ("x", "y", "c"))
    return (loss, grad_x[None], *[grads[k] for k in ORDER], *[deltas[k] for k in ORDER],
            *[new_m[k] for k in ORDER], *[new_v[k] for k in ORDER])
```

```python
import functools

import jax
import jax.numpy as jnp
from jax import lax
from jax.experimental import pallas as pl
from jax.experimental.pallas import tpu as pltpu

F32 = jnp.float32
BF16 = jnp.bfloat16
MESH_ID = pl.DeviceIdType.MESH

HD = 64
SB_HEADS = 8
FOX_HEADS = 8
MEM_HEADS = 4
MEM_HD = 128
D_SB = SB_HEADS * HD
D_FOX = FOX_HEADS * HD
D_MEM = MEM_HEADS * MEM_HD
EPS = 1e-6
NEG_INF = -1e30

ADAM_LR = 0.001
ADAM_B1 = 0.9
ADAM_B2 = 0.999
ADAM_EPS = 1e-08
ADAM_WD = 0.01
ADAM_STEP = 10

N_CHIPS = 4
VMEM_LIMIT = 56 * 1024 * 1024

F_PAD = 256


def _tile(n, target, align=128):
    if n <= target:
        return n
    best = None
    t = align
    while t <= target:
        if n % t == 0:
            best = t
        t += align
    assert best is not None, (n, target, align)
    return best


def _params(sem):
    return pltpu.CompilerParams(dimension_semantics=sem, vmem_limit_bytes=VMEM_LIMIT)


def _mm(a, b, *, name, ta=False, tb=False, out_dtypes=(F32,), epilogue=None, extras=(),
        tm=1024, tn=1024, tk=2048, comm=None):
    if ta:
        K, M = a.shape
    else:
        M, K = a.shape
    if tb:
        N, K2 = b.shape
    else:
        K2, N = b.shape
    assert K == K2, (a.shape, b.shape, ta, tb)
    tm, tn, tk = _tile(M, tm), _tile(N, tn), _tile(K, tk)
    nk = K // tk
    n_extra, n_out = len(extras), len(out_dtypes)
    if epilogue is None:
        epilogue = lambda acc: (acc,)
    dims = (((0 if ta else 1,), (1 if tb else 0,)), ((), ()))

    gm, gn = M // tm, N // tn

    def body(*refs):
        i, j, k = pl.program_id(0), pl.program_id(1), pl.program_id(2)
        got = _carry(comm, 2 + n_extra, n_out, (i == 0) & (j == 0) & (k == 0),
                     (i == gm - 1) & (j == gn - 1) & (k == nk - 1), refs)
        (a_ref, b_ref, *extra_refs), out_refs = got[0], got[1]
        part = lax.dot_general(a_ref[...].astype(BF16), b_ref[...].astype(BF16), dims, preferred_element_type=F32)

        def finish(acc):
            outs = epilogue(acc, *[r[...] for r in extra_refs])
            for o_ref, o in zip(out_refs, outs):
                o_ref[...] = o.astype(o_ref.dtype)

        if nk == 1:
            finish(part)
        else:
            acc_ref = refs[-1]

            @pl.when(k == 0)
            def _():
                acc_ref[...] = part

            @pl.when((k > 0) & (k < nk - 1))
            def _():
                acc_ref[...] += part

            @pl.when(k == nk - 1)
            def _():
                finish(acc_ref[...] + part)

        got[2]()

    a_spec = pl.BlockSpec((tk, tm), lambda i, j, k: (k, i)) if ta else pl.BlockSpec((tm, tk), lambda i, j, k: (i, k))
    b_spec = pl.BlockSpec((tn, tk), lambda i, j, k: (j, k)) if tb else pl.BlockSpec((tk, tn), lambda i, j, k: (k, j))
    mn_spec = pl.BlockSpec((tm, tn), lambda i, j, k: (i, j))
    c_ins, c_in_specs, c_out_specs, c_out_shape, c_scratch = _comm_args(comm)
    sem = ("parallel", "parallel", "arbitrary") if comm is None else ("arbitrary",) * 3
    outs = pl.pallas_call(
        body, name=name,
        grid=(gm, gn, nk),
        in_specs=[a_spec, b_spec] + [mn_spec] * n_extra + c_in_specs,
        out_specs=[mn_spec] * n_out + c_out_specs,
        out_shape=[jax.ShapeDtypeStruct((M, N), dt) for dt in out_dtypes] + c_out_shape,
        scratch_shapes=c_scratch + ([pltpu.VMEM((tm, tn), F32)] if nk > 1 else []),
        compiler_params=_params(sem),
    )(a, b, *extras, *c_ins)
    return outs if len(outs) > 1 else outs[0]


def _row_tile(rows, cols, n_arrays):
    budget = 10 * 1024 * 1024
    cols_padded = -(-cols // 128) * 128
    target = max(16, budget // (cols_padded * 4 * n_arrays * 2))
    return _tile(rows, target, align=16)


def _ew(fn, ins, out_dtypes, *, name):
    R, C = ins[0].shape
    n_in, n_out = len(ins), len(out_dtypes)
    tr = _row_tile(R, C, n_in + n_out)

    def body(*refs):
        outs = fn(*[r[...] for r in refs[:n_in]])
        for o_ref, o in zip(refs[n_in:], outs):
            o_ref[...] = o.astype(o_ref.dtype)

    spec = pl.BlockSpec((tr, C), lambda i: (i, 0))
    outs = pl.pallas_call(
        body, name=name, grid=(R // tr,),
        in_specs=[spec] * n_in, out_specs=[spec] * n_out,
        out_shape=[jax.ShapeDtypeStruct((R, C), dt) for dt in out_dtypes],
        compiler_params=_params(("parallel",)),
    )(*ins)
    return outs if n_out > 1 else outs[0]


def _rmsnorm_fwd(x, g, out_dtype, *, name):
    R, d = x.shape
    tr = _row_tile(R, d, 3)

    def body(x_ref, g_ref, o_ref):
        xv = x_ref[...].astype(F32)
        r = lax.rsqrt(jnp.mean(xv * xv, axis=-1, keepdims=True) + EPS)
        o_ref[...] = (xv * r * g_ref[...]).astype(o_ref.dtype)

    return pl.pallas_call(
        body, name=name, grid=(R // tr,),
        in_specs=[pl.BlockSpec((tr, d), lambda i: (i, 0)), pl.BlockSpec((1, d), lambda i: (0, 0))],
        out_specs=pl.BlockSpec((tr, d), lambda i: (i, 0)),
        out_shape=jax.ShapeDtypeStruct((R, d), out_dtype),
        compiler_params=_params(("parallel",)),
    )(x, g)


def _rmsnorm_bwd(x, g, dy, add=None, *, name):
    R, d = x.shape
    has_add = add is not None
    tr = _row_tile(R, d, 5)

    def body(*refs):
        x_ref, g_ref, dy_ref = refs[:3]
        add_ref = refs[3] if has_add else None
        dx_ref, dg_ref = refs[-2:]
        xv = x_ref[...].astype(F32)
        dyv = dy_ref[...].astype(F32)
        r = lax.rsqrt(jnp.mean(xv * xv, axis=-1, keepdims=True) + EPS)
        xh = xv * r
        dyg = dyv * g_ref[...]
        c = jnp.mean(dyg * xh, axis=-1, keepdims=True)
        dx = r * (dyg - xh * c)
        if has_add:
            dx = dx + add_ref[...]
        dx_ref[...] = dx

        @pl.when(pl.program_id(0) == 0)
        def _():
            dg_ref[...] = jnp.zeros_like(dg_ref)

        dg_ref[...] += jnp.sum(dyv * xh, axis=0, keepdims=True)

    row = pl.BlockSpec((tr, d), lambda i: (i, 0))
    vec = pl.BlockSpec((1, d), lambda i: (0, 0))
    ins = [x, g, dy] + ([add] if has_add else [])
    return pl.pallas_call(
        body, name=name, grid=(R // tr,),
        in_specs=[row, vec, row] + ([row] if has_add else []),
        out_specs=[row, vec],
        out_shape=[jax.ShapeDtypeStruct((R, d), F32), jax.ShapeDtypeStruct((1, d), F32)],
        compiler_params=_params(("arbitrary",)),
    )(*ins)


_NT = (((1,), (1,)), ((), ()))
_TN = (((0,), (0,)), ((), ()))


def _dot(a, b, dims=(((1,), (0,)), ((), ()))):
    return lax.dot_general(a, b, dims, preferred_element_type=F32)


def _log_sigmoid_pair(z):
    sp = jnp.log(1.0 + jnp.exp(-jnp.abs(z)))
    return jnp.minimum(z, 0.0) - sp, jnp.minimum(-z, 0.0) - sp


LANES = 128
_LOW = -3e38


def _lane_masks(hd, rows):
    if hd == LANES:
        return [None]
    lane = lax.broadcasted_iota(jnp.int32, (rows, LANES), 1)
    return [(lane >= hh * hd) & (lane < (hh + 1) * hd) for hh in range(LANES // hd)]


def _keep(t, m):
    return t if m is None else jnp.where(m, t, 0.0)


def _merge(parts, masks):
    out = parts[-1]
    for p, m in zip(parts[-2::-1], masks[-2::-1]):
        out = jnp.where(m, p, out)
    return out


def _row_value(t, m):
    return jnp.max(t if m is None else jnp.where(m, t, _LOW), axis=1, keepdims=True)


def _cols(tq, off):
    return pl.BlockSpec((tq, LANES), lambda g, i: (i, off + g))


def _cols_all(rows, off):
    return pl.BlockSpec((rows, LANES), lambda g, i: (0, off + g))


SCAN_BLOCK = 256


def _tri(kind, cols):
    n = min(SCAN_BLOCK, cols)
    r = lax.broadcasted_iota(jnp.int32, (n, n), 0)
    c = lax.broadcasted_iota(jnp.int32, (n, n), 1)
    return ((r > c) if kind == "after" else (r < c)).astype(BF16)


def _scan_cols(x, tri, reverse):
    cols = x.shape[1]
    cb = min(SCAN_BLOCK, cols)
    assert cols % cb == 0 and tri.shape == (cb, cb)
    nb = cols // cb
    blocks = [x[:, b * cb:(b + 1) * cb] for b in range(nb)]
    outs, carry = [None] * nb, None
    for b in (reversed(range(nb)) if reverse else range(nb)):
        y = _dot(blocks[b].astype(BF16), tri)
        outs[b] = y if carry is None else y + carry
        s = jnp.sum(blocks[b], axis=1, keepdims=True)
        carry = s if carry is None else carry + s
    return (outs[0] if nb == 1 else jnp.concatenate(outs, axis=1)), carry


def _softplus_parts(z):
    pos = jnp.maximum(z, 0.0) + jnp.log(1.0 + jnp.exp(-jnp.abs(z)))
    return pos, z - pos


class _ChipExchange:
    def __init__(self, kind, ins):
        assert kind in ("gather", "scatter")
        self.kind, self.ins = kind, list(ins)
        lead = (lambda s: (N_CHIPS,) + s) if kind == "gather" else (lambda s: (3,) + s[1:])
        self.out_shape = [jax.ShapeDtypeStruct(lead(a.shape), a.dtype) for a in ins]
        n = 3 * len(ins)
        self.scratch = [pltpu.SemaphoreType.DMA((n,)), pltpu.SemaphoreType.DMA((n,))]

    def _copies(self, in_refs, out_refs, sems, landing):
        x, y, c, chips = _mesh_place()
        me = 2 * x + y
        out = []
        for w in range(len(self.ins)):
            for j, chip in enumerate(chips):
                peer = 2 * chip[0] + chip[1]
                if self.kind == "gather":
                    src, dst, land = in_refs[w].at[c], out_refs[w].at[me, c], out_refs[w].at[peer, c]
                else:
                    src, dst, land = in_refs[w].at[peer], out_refs[w].at[j], out_refs[w].at[j]
                if landing:
                    src, dst = land, land
                out.append(_remote(src, dst, sems, 3 * w + j, (chip[0], chip[1], c)))
        return out

    def start(self, in_refs, out_refs, sems):
        for cp in self._copies(in_refs, out_refs, sems, False):
            cp.start()

    def finish(self, in_refs, out_refs, sems):
        for cp in self._copies(in_refs, out_refs, sems, True):
            cp.wait_recv()
        for cp in self._copies(in_refs, out_refs, sems, False):
            cp.wait_send()


def _carry(comm, n_in, n_out, first, last, refs):
    if comm is None:
        return refs[:n_in], refs[n_in:n_in + n_out], (lambda: None)
    a, b = len(comm.ins), len(comm.out_shape)
    ins, c_in = refs[:n_in], refs[n_in:n_in + a]
    outs, c_out = refs[n_in + a:n_in + a + n_out], refs[n_in + a + n_out:n_in + a + n_out + b]
    sems = refs[n_in + a + n_out + b:n_in + a + n_out + b + 2]
    pl.when(first)(lambda: comm.start(c_in, c_out, sems))
    return ins, outs, (lambda: pl.when(last)(lambda: comm.finish(c_in, c_out, sems)))


def _sbl_fwd(q, k, v, *, width, hd, name, tq=256, comm=None):
    (qa, qo), (ka, ko), (va, vo) = q, k, v
    S = qa.shape[0]
    tq = _tile(S, tq)
    tk = tq
    scale = hd ** -0.5
    n_g, n_q = width // LANES, S // tq

    def body(*refs):
        qi = pl.program_id(1)
        gi = pl.program_id(0)
        got = _carry(comm, 3, 2, (gi == 0) & (qi == 0), (gi == n_g - 1) & (qi == n_q - 1), refs)
        (q_ref, k_ref, v_ref), (o_ref, tot_ref) = got[0], got[1]
        masks = _lane_masks(hd, tq)
        qs = q_ref[...].astype(F32) * scale
        qm = [_keep(qs, m).astype(BF16) for m in masks]
        strict = lax.broadcasted_iota(jnp.int32, (tq, tk), 1) < lax.broadcasted_iota(jnp.int32, (tq, tk), 0)
        later = _tri("after", tk)

        def tile(kb, carry, diag):
            ks = pl.multiple_of(kb * tk, tk)
            kv = k_ref[pl.ds(ks, tk), :].astype(BF16)
            vv = v_ref[pl.ds(ks, tk), :].astype(BF16)
            out = []
            for hh in range(len(masks)):
                acc, c_pos = carry[2 * hh], carry[2 * hh + 1]
                pos, ls = _softplus_parts(_dot(qm[hh], kv, _NT))
                if diag:
                    pos = jnp.where(strict, pos, 0.0)
                pos_after, pos_all = _scan_cols(pos, later, True)
                w = jnp.exp(ls - (pos_after + c_pos))
                if diag:
                    w = jnp.where(strict, w, 0.0)
                out += [acc + _dot(w.astype(BF16), vv), c_pos + pos_all]
            return tuple(out)

        init = (jnp.zeros((tq, LANES), F32), jnp.zeros((tq, 1), F32)) * len(masks)
        carry = tile(qi, init, True)
        carry = lax.fori_loop(0, qi, lambda i, c: tile(qi - 1 - i, c, False), carry)
        o_ref[...] = _merge(carry[0::2], masks).astype(o_ref.dtype)
        tot_ref[...] = _merge([jnp.broadcast_to(-c, (tq, LANES)) for c in carry[1::2]], masks)
        got[2]()

    c_ins, c_in_specs, c_out_specs, c_out_shape, c_scratch = _comm_args(comm)
    return pl.pallas_call(
        body, name=name, grid=(n_g, n_q),
        in_specs=[_cols(tq, qo), _cols_all(S, ko), _cols_all(S, vo)] + c_in_specs,
        out_specs=[_cols(tq, 0), _cols(tq, 0)] + c_out_specs,
        out_shape=[jax.ShapeDtypeStruct((S, width), BF16), jax.ShapeDtypeStruct((S, width), F32)] + c_out_shape,
        scratch_shapes=c_scratch,
        compiler_params=_params(("arbitrary", "arbitrary")),
    )(qa, ka, va, *c_ins)


def _comm_args(comm):
    if comm is None:
        return [], [], [], [], []
    return comm.ins, [_ANY] * len(comm.ins), [_ANY] * len(comm.out_shape), comm.out_shape, comm.scratch


def _sbl_bwd(q, k, v, do, tot, *, width, hd, name, tq=256, comm=None):
    (qa, qo), (ka, ko), (va, vo) = q, k, v
    S = qa.shape[0]
    tq = _tile(S, tq)
    tk = tq
    scale = hd ** -0.5
    n_g, n_q = width // LANES, S // tq

    def body(*refs):
        qi = pl.program_id(1)
        gi = pl.program_id(0)
        got = _carry(comm, 5, 3, (gi == 0) & (qi == 0), (gi == n_g - 1) & (qi == n_q - 1), refs)
        (q_ref, k_ref, v_ref, do_ref, tot_ref), (dq_ref, dk_ref, dv_ref) = got[0], got[1]

        @pl.when(qi == 0)
        def _():
            dk_ref[...] = jnp.zeros_like(dk_ref)
            dv_ref[...] = jnp.zeros_like(dv_ref)

        masks = _lane_masks(hd, tq)
        qs = q_ref[...].astype(F32) * scale
        qm = [_keep(qs, m).astype(BF16) for m in masks]
        dov = [_keep(do_ref[...], m).astype(BF16) for m in masks]
        rest = [-_row_value(tot_ref[...], m) for m in masks]
        strict = lax.broadcasted_iota(jnp.int32, (tq, tk), 1) < lax.broadcasted_iota(jnp.int32, (tq, tk), 0)
        later, before = _tri("after", tk), _tri("before", tk)

        def tile(kb, carry, diag):
            ks = pl.multiple_of(kb * tk, tk)
            kv = k_ref[pl.ds(ks, tk), :].astype(BF16)
            vv = v_ref[pl.ds(ks, tk), :].astype(BF16)
            out = []
            dk_t, dv_t = None, None
            for hh in range(len(masks)):
                dq, c_pos, c_g = carry[3 * hh:3 * hh + 3]
                pos, ls = _softplus_parts(_dot(qm[hh], kv, _NT))
                if diag:
                    pos = jnp.where(strict, pos, 0.0)
                pos_after, pos_all = _scan_cols(pos, later, True)
                c_pos = c_pos + pos_all
                w = jnp.exp(ls - (pos_after + (rest[hh] - c_pos)))
                if diag:
                    w = jnp.where(strict, w, 0.0)
                g = _dot(dov[hh], vv, _NT) * w
                g_before, g_all = _scan_cols(g, before, False)
                g_before = g_before + c_g
                dz = g - jnp.exp(ls) * (g + g_before)
                if diag:
                    dz = jnp.where(strict, dz, 0.0)
                dzb = dz.astype(BF16)
                dk_h = _dot(dzb, qm[hh], _TN)
                dv_h = _dot(w.astype(BF16), dov[hh], _TN)
                dk_t = dk_h if dk_t is None else dk_t + dk_h
                dv_t = dv_h if dv_t is None else dv_t + dv_h
                out += [dq + _dot(dzb, kv), c_pos, c_g + g_all]
            dk_ref[pl.ds(ks, tk), :] += dk_t
            dv_ref[pl.ds(ks, tk), :] += dv_t
            return tuple(out)

        zero = jnp.zeros((tq, 1), F32)
        init = (jnp.zeros((tq, LANES), F32), zero, zero) * len(masks)
        carry = lax.fori_loop(0, qi, lambda kb, c: tile(kb, c, False), init)
        carry = tile(qi, carry, True)
        dq_ref[...] = _merge(carry[0::3], masks) * scale
        got[2]()

    full = jax.ShapeDtypeStruct((S, width), F32)
    c_ins, c_in_specs, c_out_specs, c_out_shape, c_scratch = _comm_args(comm)
    return pl.pallas_call(
        body, name=name, grid=(n_g, n_q),
        in_specs=[_cols(tq, qo), _cols_all(S, ko), _cols_all(S, vo), _cols(tq, do[1]), _cols(tq, tot[1])] + c_in_specs,
        out_specs=[_cols(tq, 0), _cols_all(S, 0), _cols_all(S, 0)] + c_out_specs,
        out_shape=[full, full, full] + c_out_shape,
        scratch_shapes=c_scratch,
        compiler_params=_params(("arbitrary", "arbitrary")),
    )(qa, ka, va, do[0], tot[0], *c_ins)


def _sml_fwd(q, k, v, bias=None, *, width, hd, causal, name, tq=256, tk=256, comm=None):
    (qa, qo), (ka, ko), (va, vo) = q, k, v
    S, Sk = qa.shape[0], ka.shape[0]
    tq, tk = _tile(S, tq), _tile(Sk, tk)
    if causal:
        assert tq == tk and S == Sk
    nk = Sk // tk
    hpg = LANES // hd
    scale = hd ** -0.5
    has_bias = bias is not None
    n_g, n_q = width // LANES, S // tq

    def body(*all_refs):
        qi, gi = pl.program_id(1), pl.program_id(0)
        got = _carry(comm, 5 if has_bias else 3, 2, (gi == 0) & (qi == 0), (gi == n_g - 1) & (qi == n_q - 1), all_refs)
        refs = tuple(got[0]) + tuple(got[1])
        q_ref, k_ref, v_ref = refs[:3]
        o_ref, lse_ref = refs[-2:]
        masks = _lane_masks(hd, tq)
        qs = q_ref[...].astype(F32) * scale
        qm = [_keep(qs, m).astype(BF16) for m in masks]
        allowed = lax.broadcasted_iota(jnp.int32, (tq, tk), 1) <= lax.broadcasted_iota(jnp.int32, (tq, tk), 0)

        def tile(kb, carry, diag):
            ks = pl.multiple_of(kb * tk, tk)
            kv = k_ref[pl.ds(ks, tk), :].astype(BF16)
            vv = v_ref[pl.ds(ks, tk), :].astype(BF16)
            out = []
            for hh in range(hpg):
                m, l, acc = carry[3 * hh:3 * hh + 3]
                z = _dot(qm[hh], kv, _NT)
                if has_bias:
                    z = z + refs[3][hh] - refs[4][hh, kb]
                if diag:
                    z = jnp.where(allowed, z, NEG_INF)
                m2 = jnp.maximum(m, jnp.max(z, axis=1, keepdims=True))
                p = jnp.exp(z - m2)
                alpha = jnp.exp(m - m2)
                out += [m2, alpha * l + jnp.sum(p, axis=1, keepdims=True), alpha * acc + _dot(p.astype(BF16), vv)]
            return tuple(out)

        init = (jnp.full((tq, 1), NEG_INF, F32), jnp.zeros((tq, 1), F32), jnp.zeros((tq, LANES), F32)) * hpg
        if causal:
            carry = lax.fori_loop(0, qi, lambda kb, c: tile(kb, c, False), init)
            carry = tile(qi, carry, True)
        else:
            carry = lax.fori_loop(0, nk, lambda kb, c: tile(kb, c, False), init)
        o_ref[...] = _merge([acc / l for l, acc in zip(carry[1::3], carry[2::3])], masks).astype(o_ref.dtype)
        lse_ref[...] = _merge([jnp.broadcast_to(m + jnp.log(l), (tq, LANES)) for m, l in zip(carry[0::3], carry[1::3])], masks)
        got[2]()

    in_specs = [_cols(tq, qo), _cols_all(Sk, ko), _cols_all(Sk, vo)]
    ins = [qa, ka, va]
    if has_bias:
        in_specs += [pl.BlockSpec((hpg, tq, 1), lambda g, i: (g, i, 0)),
                     pl.BlockSpec((hpg, nk, 1, tk), lambda g, i: (g, 0, 0, 0))]
        ins += list(bias)
    c_ins, c_in_specs, c_out_specs, c_out_shape, c_scratch = _comm_args(comm)
    return pl.pallas_call(
        body, name=name, grid=(n_g, n_q),
        in_specs=in_specs + c_in_specs, out_specs=[_cols(tq, 0), _cols(tq, 0)] + c_out_specs,
        out_shape=[jax.ShapeDtypeStruct((S, width), BF16), jax.ShapeDtypeStruct((S, width), F32)] + c_out_shape,
        scratch_shapes=c_scratch,
        compiler_params=_params(("arbitrary", "arbitrary") if comm is not None else ("parallel", "arbitrary")),
    )(*ins, *c_ins)


def _sml_bwd(q, k, v, o, lse, do, bias=None, *, width, hd, causal, name, tq=256, tk=256, comm=None):
    (qa, qo), (ka, ko), (va, vo) = q, k, v
    S, Sk = qa.shape[0], ka.shape[0]
    tq, tk = _tile(S, tq), _tile(Sk, tk)
    nk = Sk // tk
    hpg = LANES // hd
    scale = hd ** -0.5
    has_bias = bias is not None
    n_in = 8 if has_bias else 6
    n_g, n_q = width // LANES, S // tq

    def body(*all_refs):
        qi, gi = pl.program_id(1), pl.program_id(0)
        got = _carry(comm, n_in, 5 if has_bias else 3, (gi == 0) & (qi == 0), (gi == n_g - 1) & (qi == n_q - 1), all_refs)
        refs = tuple(got[0]) + tuple(got[1])
        q_ref, k_ref, v_ref, o_ref, lse_ref, do_ref = refs[:6]
        dq_ref, dk_ref, dv_ref = refs[n_in:n_in + 3]

        @pl.when(qi == 0)
        def _():
            dk_ref[...] = jnp.zeros_like(dk_ref)
            dv_ref[...] = jnp.zeros_like(dv_ref)
            if has_bias:
                refs[n_in + 4][...] = jnp.zeros_like(refs[n_in + 4])

        masks = _lane_masks(hd, tq)
        qs = q_ref[...].astype(F32) * scale
        qm = [_keep(qs, m).astype(BF16) for m in masks]
        do32 = do_ref[...]
        dov = [_keep(do32, m).astype(BF16) for m in masks]
        prod = do32 * o_ref[...].astype(F32)
        delta = [jnp.sum(_keep(prod, m), axis=1, keepdims=True) for m in masks]
        lses = [_row_value(lse_ref[...], m) for m in masks]
        allowed = lax.broadcasted_iota(jnp.int32, (tq, tk), 1) <= lax.broadcasted_iota(jnp.int32, (tq, tk), 0)

        def tile(kb, carry, diag):
            ks = pl.multiple_of(kb * tk, tk)
            kv = k_ref[pl.ds(ks, tk), :].astype(BF16)
            vv = v_ref[pl.ds(ks, tk), :].astype(BF16)
            out = []
            dk_t, dv_t = None, None
            for hh in range(hpg):
                dq, db_row = carry[2 * hh:2 * hh + 2]
                z = _dot(qm[hh], kv, _NT)
                if has_bias:
                    z = z + refs[6][hh] - refs[7][hh, kb]
                p = jnp.exp(z - lses[hh])
                if diag:
                    p = jnp.where(allowed, p, 0.0)
                dz = p * (_dot(dov[hh], vv, _NT) - delta[hh])
                dzb = dz.astype(BF16)
                dk_h = _dot(dzb, qm[hh], _TN)
                dv_h = _dot(p.astype(BF16), dov[hh], _TN)
                dk_t = dk_h if dk_t is None else dk_t + dk_h
                dv_t = dv_h if dv_t is None else dv_t + dv_h
                if has_bias:
                    db_row = db_row + jnp.sum(dz, axis=1, keepdims=True)
                    refs[n_in + 4][hh, kb] += jnp.sum(dz, axis=0, keepdims=True)
                out += [dq + _dot(dzb, kv), db_row]
            dk_ref[pl.ds(ks, tk), :] += dk_t
            dv_ref[pl.ds(ks, tk), :] += dv_t
            return tuple(out)

        init = (jnp.zeros((tq, LANES), F32), jnp.zeros((tq, 1), F32)) * hpg
        if causal:
            carry = lax.fori_loop(0, qi, lambda kb, c: tile(kb, c, False), init)
            carry = tile(qi, carry, True)
        else:
            carry = lax.fori_loop(0, nk, lambda kb, c: tile(kb, c, False), init)
        dq_ref[...] = _merge(carry[0::2], masks) * scale
        if has_bias:
            for hh in range(hpg):
                refs[n_in + 3][hh] = carry[2 * hh + 1]
        got[2]()

    in_specs = [_cols(tq, qo), _cols_all(Sk, ko), _cols_all(Sk, vo), _cols(tq, o[1]), _cols(tq, lse[1]), _cols(tq, do[1])]
    ins = [qa, ka, va, o[0], lse[0], do[0]]
    out_specs = [_cols(tq, 0), _cols_all(Sk, 0), _cols_all(Sk, 0)]
    out_shape = [jax.ShapeDtypeStruct((S, width), F32), jax.ShapeDtypeStruct((Sk, width), F32),
                 jax.ShapeDtypeStruct((Sk, width), F32)]
    if has_bias:
        rspec = pl.BlockSpec((hpg, tq, 1), lambda g, i: (g, i, 0))
        cspec = pl.BlockSpec((hpg, nk, 1, tk), lambda g, i: (g, 0, 0, 0))
        in_specs += [rspec, cspec]
        ins += list(bias)
        out_specs += [rspec, cspec]
        n_heads = width // hd
        out_shape += [jax.ShapeDtypeStruct((n_heads, S, 1), F32), jax.ShapeDtypeStruct((n_heads, nk, 1, tk), F32)]
    c_ins, c_in_specs, c_out_specs, c_out_shape, c_scratch = _comm_args(comm)
    return pl.pallas_call(
        body, name=name, grid=(n_g, n_q),
        in_specs=in_specs + c_in_specs, out_specs=out_specs + c_out_specs, out_shape=out_shape + c_out_shape,
        scratch_shapes=c_scratch,
        compiler_params=_params(("arbitrary", "arbitrary") if comm is not None else ("parallel", "arbitrary")),
    )(*ins, *c_ins)


def _head_sums(t, masks):
    sums = [jnp.sum(_keep(t, m), axis=1, keepdims=True) for m in masks]
    return _merge([jnp.broadcast_to(s, t.shape) for s in sums], masks) if len(masks) > 1 else sums[0]


def _hnorm_fwd(x, g_lanes, *, width, hd, name, tr=512):
    xa, xo = x
    R = xa.shape[0]
    tr = _tile(R, tr, align=16)
    n_blk = width // LANES

    def body(x_ref, g_ref, o_ref):
        masks = _lane_masks(hd, tr)
        for j in range(n_blk):
            sl = slice(j * LANES, (j + 1) * LANES)
            xv = x_ref[:, sl].astype(F32)
            r = lax.rsqrt(_head_sums(xv * xv, masks) * (1.0 / hd) + EPS)
            o_ref[:, sl] = (xv * r * g_ref[...]).astype(o_ref.dtype)

    assert (xo * LANES) % width == 0
    return pl.pallas_call(
        body, name=name, grid=(R // tr,),
        in_specs=[pl.BlockSpec((tr, width), lambda i: (i, xo * LANES // width)), pl.BlockSpec((1, LANES), lambda i: (0, 0))],
        out_specs=pl.BlockSpec((tr, width), lambda i: (i, 0)),
        out_shape=jax.ShapeDtypeStruct((R, width), BF16),
        compiler_params=_params(("parallel",)),
    )(xa, g_lanes)


def _hnorm_bwd(x, g_lanes, dy, *, width, hd, name, tr=512):
    xa, xo = x
    R = xa.shape[0]
    tr = _tile(R, tr, align=16)
    n_blk = width // LANES

    def body(x_ref, g_ref, dy_ref, dx_ref, dg_ref):
        masks = _lane_masks(hd, tr)
        dg = jnp.zeros((1, LANES), F32)
        for j in range(n_blk):
            sl = slice(j * LANES, (j + 1) * LANES)
            xv = x_ref[:, sl].astype(F32)
            dyv = dy_ref[:, sl].astype(F32)
            r = lax.rsqrt(_head_sums(xv * xv, masks) * (1.0 / hd) + EPS)
            xh = xv * r
            dyg = dyv * g_ref[...]
            c = _head_sums(dyg * xh, masks) * (1.0 / hd)
            dx_ref[:, sl] = (r * (dyg - xh * c)).astype(dx_ref.dtype)
            dg = dg + jnp.sum(dyv * xh, axis=0, keepdims=True)
        if hd * 2 == LANES:
            dg8 = jnp.broadcast_to(dg, (8, LANES))
            dg = (dg8 + pltpu.roll(dg8, shift=hd, axis=1))[0:1]
        else:
            assert hd == LANES

        @pl.when(pl.program_id(0) == 0)
        def _():
            dg_ref[...] = jnp.zeros_like(dg_ref)

        dg_ref[...] += dg

    assert (xo * LANES) % width == 0
    return pl.pallas_call(
        body, name=name, grid=(R // tr,),
        in_specs=[pl.BlockSpec((tr, width), lambda i: (i, xo * LANES // width)), pl.BlockSpec((1, LANES), lambda i: (0, 0)),
                  pl.BlockSpec((tr, width), lambda i: (i, 0))],
        out_specs=[pl.BlockSpec((tr, width), lambda i: (i, 0)), pl.BlockSpec((1, LANES), lambda i: (0, 0))],
        out_shape=[jax.ShapeDtypeStruct((R, width), BF16), jax.ShapeDtypeStruct((1, LANES), F32)],
        compiler_params=_params(("arbitrary",)),
    )(xa, g_lanes, dy)


def _split3_dot(x, tri):
    a = x.astype(BF16)
    r = x - a.astype(F32)
    b = r.astype(BF16)
    c = (r - b.astype(F32)).astype(BF16)
    return _dot(a, tri) + _dot(b, tri) + _dot(c, tri)


def _forget_fwd(logit_t, b_col, *, name, blk=512):
    H, S = logit_t.shape
    blk = _tile(S, blk)

    def body(l_ref, b_ref, f_ref):
        r_i = lax.broadcasted_iota(jnp.int32, (blk, blk), 0)
        c_i = lax.broadcasted_iota(jnp.int32, (blk, blk), 1)
        upto = (r_i <= c_i).astype(BF16)
        carry = jnp.zeros((H, 1), F32)
        for j in range(S // blk):
            u = l_ref[:, j * blk:(j + 1) * blk] + b_ref[...]
            lf, _ = _log_sigmoid_pair(u)
            f_ref[:, j * blk:(j + 1) * blk] = _split3_dot(lf, upto) + carry
            carry = carry + jnp.sum(lf, axis=1, keepdims=True)

    return pl.pallas_call(
        body, name=name,
        out_shape=jax.ShapeDtypeStruct((H, S), F32),
        compiler_params=pltpu.CompilerParams(vmem_limit_bytes=VMEM_LIMIT),
    )(logit_t, b_col)


def _forget_bwd(logit_t, b_col, d_f, *, name, blk=512):
    H, S = logit_t.shape
    blk = _tile(S, blk)

    def body(l_ref, b_ref, df_ref, dl_ref, db_ref):
        r_i = lax.broadcasted_iota(jnp.int32, (blk, blk), 0)
        c_i = lax.broadcasted_iota(jnp.int32, (blk, blk), 1)
        fromon = (r_i >= c_i).astype(BF16)
        carry = jnp.zeros((H, 1), F32)
        db = jnp.zeros((H, 1), F32)
        for j in reversed(range(S // blk)):
            sl = slice(j * blk, (j + 1) * blk)
            dfv = df_ref[:, sl]
            d_lf = _split3_dot(dfv, fromon) + carry
            carry = carry + jnp.sum(dfv, axis=1, keepdims=True)
            u = l_ref[:, sl] + b_ref[...]
            _, lsn = _log_sigmoid_pair(u)
            dl = d_lf * jnp.exp(lsn)
            dl_ref[:, sl] = dl
            db = db + jnp.sum(dl, axis=1, keepdims=True)
        db_ref[...] = db

    return pl.pallas_call(
        body, name=name,
        out_shape=[jax.ShapeDtypeStruct((H, S), F32), jax.ShapeDtypeStruct((H, 1), F32)],
        compiler_params=pltpu.CompilerParams(vmem_limit_bytes=VMEM_LIMIT),
    )(logit_t, b_col, d_f)


def _sigmoid(t):
    return 1.0 / (1.0 + jnp.exp(-t))


def _gate_fwd(o3, w3, proj, D, *, name, tm=256):
    S = proj.shape[0]
    tm = _tile(S, tm)

    def body(o0, o1, o2, w0, w1, w2, g0, g1, g2, out_ref):
        acc = None
        for o_ref, w_ref, g_ref in ((o0, w0, g0), (o1, w1, g1), (o2, w2, g2)):
            t = _sigmoid(g_ref[...]) * _dot(o_ref[...], w_ref[...])
            acc = t if acc is None else acc + t
        out_ref[...] = acc.astype(out_ref.dtype)

    ospec = lambda d: pl.BlockSpec((tm, d), lambda i: (i, 0))
    wspec = lambda w: pl.BlockSpec(w.shape, lambda i: (0, 0))
    gspec = lambda j: pl.BlockSpec((tm, D), lambda i: (i, j))
    return pl.pallas_call(
        body, name=name, grid=(S // tm,),
        in_specs=[ospec(o.shape[1]) for o in o3] + [wspec(w) for w in w3] + [gspec(j) for j in range(3)],
        out_specs=pl.BlockSpec((tm, D), lambda i: (i, 0)),
        out_shape=jax.ShapeDtypeStruct((S, D), BF16),
        compiler_params=_params(("parallel",)),
    )(*o3, *w3, proj, proj, proj)


def _gate_bwd(o3, w3, proj, dmerged, D, *, name, tm=256):
    S = proj.shape[0]
    tm = _tile(S, tm)

    def body(o0, o1, o2, w0, w1, w2, g0, g1, g2, dm_ref, dg_ref, db0, db1, db2, do0, do1, do2):
        dm = dm_ref[...]
        for j, (o_ref, w_ref, g_ref, db_ref, do_ref) in enumerate(
                ((o0, w0, g0, db0, do0), (o1, w1, g1, db1, do1), (o2, w2, g2, db2, do2))):
            s = _sigmoid(g_ref[...])
            br = _dot(o_ref[...], w_ref[...])
            dg_ref[:, j * D:(j + 1) * D] = (dm * br * s * (1.0 - s)).astype(dg_ref.dtype)
            dbr = (dm * s).astype(BF16)
            db_ref[...] = dbr
            do_ref[...] = _dot(dbr, w_ref[...], _NT)

    ospec = lambda d: pl.BlockSpec((tm, d), lambda i: (i, 0))
    wspec = lambda w: pl.BlockSpec(w.shape, lambda i: (0, 0))
    gspec = lambda j: pl.BlockSpec((tm, D), lambda i: (i, j))
    dspec = pl.BlockSpec((tm, D), lambda i: (i, 0))
    return pl.pallas_call(
        body, name=name, grid=(S // tm,),
        in_specs=[ospec(o.shape[1]) for o in o3] + [wspec(w) for w in w3] + [gspec(j) for j in range(3)] + [dspec],
        out_specs=[pl.BlockSpec((tm, 3 * D), lambda i: (i, 0))] + [dspec] * 3 + [ospec(o.shape[1]) for o in o3],
        out_shape=[jax.ShapeDtypeStruct((S, proj.shape[1]), BF16)] + [jax.ShapeDtypeStruct((S, D), BF16)] * 3
        + [jax.ShapeDtypeStruct((S, o.shape[1]), F32) for o in o3],
        compiler_params=_params(("parallel",)),
    )(*o3, *w3, proj, proj, proj, dmerged)


def _loss_sum(dy, D, *, name):
    R, C = dy.shape
    tr = _row_tile(R, C, 2)

    def body(dy_ref, out_ref):
        @pl.when(pl.program_id(0) == 0)
        def _():
            out_ref[...] = jnp.zeros_like(out_ref)

        v = dy_ref[...]
        out_ref[...] += (0.5 * D) * jnp.sum(v * v)

    return pl.pallas_call(
        body, name=name, grid=(R // tr,),
        in_specs=[pl.BlockSpec((tr, C), lambda i: (i, 0))],
        out_specs=pl.BlockSpec((8, 128), lambda i: (0, 0)),
        out_shape=jax.ShapeDtypeStruct((8, 128), F32),
        compiler_params=_params(("arbitrary",)),
    )(dy)[0, 0]


def _pair_sum(stacked, got, core, *, name):
    n, _, r, c = stacked.shape
    tr = _row_tile(r, c, 3)

    def body(core_ref, a_ref, b_ref, o_ref):
        o_ref[...] = (a_ref[0].astype(F32) + b_ref[...].astype(F32)).astype(o_ref.dtype)

    spec = pl.BlockSpec((1, tr, c), lambda s, i, core_ref: (s, i, 0))
    return pl.pallas_call(
        body, name=name,
        grid_spec=pltpu.PrefetchScalarGridSpec(
            num_scalar_prefetch=1, grid=(n, r // tr),
            in_specs=[pl.BlockSpec((1, 1, tr, c), lambda s, i, core_ref: (s, core_ref[0], i, 0)), spec],
            out_specs=spec),
        out_shape=jax.ShapeDtypeStruct((n, r, c), BF16),
        compiler_params=_params(("parallel", "parallel")),
    )(core.astype(jnp.int32).reshape(1), stacked, got)


def _chip_sum(parts, got, chip, *, name):
    _, r, c = parts.shape
    tr = _row_tile(r, c, 6)

    def body(chip_ref, p_ref, q0_ref, q1_ref, q2_ref, o_ref):
        o_ref[...] = ((p_ref[0].astype(F32) + q0_ref[0].astype(F32)) + q1_ref[0].astype(F32)) + q2_ref[0].astype(F32)

    from_chip = lambda j: pl.BlockSpec((1, tr, c), lambda i, chip_ref: (j, i, 0))
    return pl.pallas_call(
        body, name=name,
        grid_spec=pltpu.PrefetchScalarGridSpec(
            num_scalar_prefetch=1, grid=(r // tr,),
            in_specs=[pl.BlockSpec((1, tr, c), lambda i, chip_ref: (chip_ref[0], i, 0))] + [from_chip(j) for j in range(3)],
            out_specs=pl.BlockSpec((tr, c), lambda i, chip_ref: (i, 0))),
        out_shape=jax.ShapeDtypeStruct((r, c), F32),
        compiler_params=_params(("parallel",)),
    )(chip.astype(jnp.int32).reshape(1), parts, got, got, got)


def _adamw_math(w, g, m, v):
    m2 = ADAM_B1 * m + (1.0 - ADAM_B1) * g
    v2 = ADAM_B2 * v + (1.0 - ADAM_B2) * (g * g)
    m_hat = m2 / (1.0 - ADAM_B1 ** ADAM_STEP)
    v_hat = v2 / (1.0 - ADAM_B2 ** ADAM_STEP)
    delta = -ADAM_LR * (m_hat / (jnp.sqrt(v_hat) + ADAM_EPS) + ADAM_WD * w)
    return delta, m2, v2


def _adamw(w, g, m, v, *, name):
    return _ew(_adamw_math, [w, g, m, v], (F32, F32, F32), name=name)


def _adamw_small(w, parts, m, v, *, name):
    n = parts.shape[0]

    def body(w_ref, p_ref, m_ref, v_ref, g_ref, d_ref, m2_ref, v2_ref):
        g = p_ref[0]
        for i in range(1, n):
            g = g + p_ref[i]
        g_ref[...] = g
        d_ref[...], m2_ref[...], v2_ref[...] = _adamw_math(w_ref[...], g, m_ref[...], v_ref[...])

    shp = jax.ShapeDtypeStruct(w.shape, F32)
    return pl.pallas_call(body, name=name, out_shape=[shp] * 4)(w, parts, m, v)


_ANY = pl.BlockSpec(memory_space=pl.ANY)


def _mesh_place():
    x, y, c = lax.axis_index("x"), lax.axis_index("y"), lax.axis_index("c")
    chips = [(1 - x, y), (x, 1 - y), (1 - x, 1 - y)]
    return x, y, c, chips


def _remote(src, dst, sems, i, to):
    send_sems, recv_sems = sems
    return pltpu.make_async_remote_copy(src_ref=src, dst_ref=dst, send_sem=send_sems.at[i], recv_sem=recv_sems.at[i],
                                        device_id=to, device_id_type=MESH_ID)


def _gather_weights(shards, *, name):
    n = len(shards)

    def body(*refs):
        ins, outs = refs[:n], refs[n:2 * n]
        sems = refs[2 * n:2 * n + 2]
        x, y, c, chips = _mesh_place()
        me = 2 * x + y
        sibling = (x, y, 1 - c)
        sent = []
        for w in range(n):
            for j, chip in enumerate(chips):
                cp = _remote(ins[w].at[c], outs[w].at[me, c], sems, 6 * w + j, (chip[0], chip[1], c))
                cp.start()
                sent.append(cp)
        for w in range(n):
            for j, chip in enumerate(chips):
                got = outs[w].at[2 * chip[0] + chip[1], c]
                _remote(got, got, sems, 6 * w + j, sibling).wait_recv()
                cp = _remote(got, got, sems, 6 * w + 3 + j, sibling)
                cp.start()
                sent.append(cp)
        for w in range(n):
            for j, chip in enumerate(chips):
                got = outs[w].at[2 * chip[0] + chip[1], 1 - c]
                _remote(got, got, sems, 6 * w + 3 + j, sibling).wait_recv()
        for cp in sent:
            cp.wait_send()

    outs = pl.pallas_call(
        body, name=name,
        in_specs=[_ANY] * n, out_specs=[_ANY] * n,
        out_shape=[jax.ShapeDtypeStruct((N_CHIPS,) + s.shape, s.dtype) for s in shards],
        scratch_shapes=[pltpu.SemaphoreType.DMA((6 * n,)), pltpu.SemaphoreType.DMA((6 * n,))],
    )(*shards)
    me = 2 * lax.axis_index("x") + lax.axis_index("y")
    return [lax.dynamic_update_index_in_dim(o, s, me, 0) for o, s in zip(outs, shards)]


def _forward_halves(gathered, *, name):
    n = len(gathered)

    def body(*refs):
        ins, outs = refs[:n], refs[n:2 * n]
        sems = refs[2 * n:2 * n + 2]
        x, y, c, chips = _mesh_place()
        sibling = (x, y, 1 - c)
        sent = []
        for w in range(n):
            for j, chip in enumerate(chips):
                peer = 2 * chip[0] + chip[1]
                cp = _remote(ins[w].at[peer, c], outs[w].at[peer, c], sems, 3 * w + j, sibling)
                cp.start()
                sent.append(cp)
        for w in range(n):
            for j, chip in enumerate(chips):
                land = outs[w].at[2 * chip[0] + chip[1], 1 - c]
                _remote(land, land, sems, 3 * w + j, sibling).wait_recv()
        for cp in sent:
            cp.wait_send()

    return pl.pallas_call(
        body, name=name,
        in_specs=[_ANY] * n, out_specs=[_ANY] * n,
        out_shape=[jax.ShapeDtypeStruct(g.shape, g.dtype) for g in gathered],
        input_output_aliases={w: w for w in range(n)},
        scratch_shapes=[pltpu.SemaphoreType.DMA((3 * n,)), pltpu.SemaphoreType.DMA((3 * n,))],
    )(*gathered)


def _exchange_siblings(grads, *, name):
    n = len(grads)

    def body(*refs):
        ins, got = refs[:n], refs[n:2 * n]
        sems = refs[2 * n:2 * n + 2]
        x, y, c, _ = _mesh_place()
        sibling = (x, y, 1 - c)
        sent = []
        for w in range(n):
            for s in range(N_CHIPS):
                cp = _remote(ins[w].at[s, 1 - c], got[w].at[s], sems, N_CHIPS * w + s, sibling)
                cp.start()
                sent.append(cp)
        for w in range(n):
            for s in range(N_CHIPS):
                _remote(got[w].at[s], got[w].at[s], sems, N_CHIPS * w + s, sibling).wait_recv()
        for cp in sent:
            cp.wait_send()

    n_sem = N_CHIPS * n
    return pl.pallas_call(
        body, name=name,
        in_specs=[_ANY] * n, out_specs=[_ANY] * n,
        out_shape=[jax.ShapeDtypeStruct((N_CHIPS,) + g.shape[2:], g.dtype) for g in grads],
        scratch_shapes=[pltpu.SemaphoreType.DMA((n_sem,)), pltpu.SemaphoreType.DMA((n_sem,))],
    )(*grads)


def _share_halves(halves, small):
    n = len(halves)

    def body(*refs):
        ins, small_ref = refs[:n], refs[n]
        outs, small_out = refs[n + 1:2 * n + 1], refs[2 * n + 1]
        sems = refs[2 * n + 2:2 * n + 4]
        x, y, c, chips = _mesh_place()
        sibling = (x, y, 1 - c)
        me = 4 * x + 2 * y + c
        sent = [_remote(ins[w], outs[w].at[c], sems, w, sibling) for w in range(n)]
        peers = [sibling] + [(ch[0], ch[1], cc) for ch in chips for cc in (c, 1 - c)]
        sent += [_remote(small_ref, small_out.at[me], sems, n + j, peer) for j, peer in enumerate(peers)]
        for cp in sent:
            cp.start()
        for w in range(n):
            _remote(outs[w].at[1 - c], outs[w].at[1 - c], sems, w, sibling).wait_recv()
        for j, peer in enumerate(peers):
            frm = small_out.at[4 * peer[0] + 2 * peer[1] + peer[2]]
            _remote(frm, frm, sems, n + j, peer).wait_recv()
        for cp in sent:
            cp.wait_send()

    n_sem = n + 7
    outs = pl.pallas_call(
        body, name="share_halves",
        in_specs=[_ANY] * (n + 1), out_specs=[_ANY] * (n + 1),
        out_shape=[jax.ShapeDtypeStruct((2,) + h.shape, h.dtype) for h in halves]
        + [jax.ShapeDtypeStruct((8,) + small.shape, small.dtype)],
        scratch_shapes=[pltpu.SemaphoreType.DMA((n_sem,)), pltpu.SemaphoreType.DMA((n_sem,))],
    )(*halves, small)
    c = lax.axis_index("c")
    me = 4 * lax.axis_index("x") + 2 * lax.axis_index("y") + c
    return ([lax.dynamic_update_index_in_dim(o, h, c, 0) for o, h in zip(outs[:n], halves)],
            lax.dynamic_update_index_in_dim(outs[n], small, me, 0))


EARLY = ("w_ff_down", "w_ff_up", "w_out", "w_branch_sb", "w_branch_fox", "w_branch_mem")


def _split(outs, n):
    outs = list(outs) if isinstance(outs, (list, tuple)) else [outs]
    return outs[:n], outs[n:]


def _local_step(x, mem, target, small, W, gather_rest=None, reduce_early=None, reduce_late=None):
    S, D = x.shape
    o_qkv, o_mq, o_f = 3 * D, 3 * D + 2 * 3 * D_SB, 3 * D + 2 * 3 * D_SB + D_MEM
    tq = 512

    g_comms, finish_weights = gather_rest if gather_rest is not None else ([None] * 3, None)
    h = _rmsnorm_fwd(x, small["g_mix_norm"], BF16, name="mix_norm")
    (proj,), landed = _split(_mm(h, W["w_in"], name="in_proj", tb=True, tn=768, comm=g_comms[0]), 1)
    blk = lambda j: (proj, (o_qkv + j * D_SB) // LANES)
    sb_q, sb_k, sb_v, fx_q, fx_k, fx_v = [blk(j) for j in range(6)]
    m_q = (proj, o_mq // LANES)
    f_logit_t = _mm(W["w_in"][o_f:o_f + ROW_TILE], h, name="forget_logits", tb=True)[:FOX_HEADS]
    b_col = small["b_forget"].reshape(FOX_HEADS, 1)
    lanes = lambda g: jnp.tile(g, (1, LANES // g.shape[1]))
    g_fq, g_fk, g_mq, g_mk = [lanes(small[k]) for k in ("g_fox_q", "g_fox_k", "g_mem_q", "g_mem_k")]

    (o_sb, sb_tot), more = _split(_sbl_fwd(sb_q, sb_k, sb_v, width=D_SB, hd=HD, name="sb_fwd", tq=tq, comm=g_comms[1]), 2)
    landed += more

    fq = _hnorm_fwd(fx_q, g_fq, width=D_FOX, hd=HD, name="fox_q_norm")
    fk = _hnorm_fwd(fx_k, g_fk, width=D_FOX, hd=HD, name="fox_k_norm")
    f_cum = _forget_fwd(f_logit_t, b_col, name="forget_fwd")
    tkf = _tile(S, tq)
    f_bias = (f_cum.reshape(FOX_HEADS, S, 1), f_cum.reshape(FOX_HEADS, S // tkf, 1, tkf))
    (o_fox, fox_lse), more = _split(_sml_fwd((fq, 0), (fk, 0), fx_v, f_bias, width=D_FOX, hd=HD, causal=True,
                                             name="fox_fwd", tq=tq, tk=tq, comm=g_comms[2]), 2)
    landed += more
    if finish_weights is not None:
        W = {**W, **finish_weights(landed)}

    mh = _rmsnorm_fwd(mem, small["g_mem_norm"], BF16, name="mem_norm")
    mkv = _mm(mh, W["w_mem_kv"], name="mem_kv")
    mv = (mkv, D_MEM // LANES)
    mq = _hnorm_fwd(m_q, g_mq, width=D_MEM, hd=MEM_HD, name="mem_q_norm")
    mk = _hnorm_fwd((mkv, 0), g_mk, width=D_MEM, hd=MEM_HD, name="mem_k_norm")
    o_mem, mem_lse = _sml_fwd((mq, 0), (mk, 0), mv, width=D_MEM, hd=MEM_HD, causal=False, name="mem_fwd", tq=tq, tk=256)

    o3 = [o_sb, o_fox, o_mem]
    w3 = [W["w_branch_sb"], W["w_branch_fox"], W["w_branch_mem"]]
    merged = _gate_fwd(o3, w3, proj, D, name="gate_fwd")
    x1 = _mm(merged, W["w_out"], name="out_proj", extras=(x,), epilogue=lambda acc, res: (res + acc,))
    h2 = _rmsnorm_fwd(x1, small["g_mlp_norm"], BF16, name="mlp_norm")

    def relu2(acc):
        u = jnp.maximum(acc, 0.0)
        return u, u * u

    u, a = _mm(h2, W["w_ff_up"], name="ff_up", out_dtypes=(BF16, BF16), epilogue=relu2)
    def head(acc, res, tgt):
        d = (res + acc - tgt) * (1.0 / D)
        return d, d

    dy, dy16 = _mm(a, W["w_ff_down"], name="ff_down", extras=(x1, target), out_dtypes=(F32, BF16), epilogue=head, tm=512)
    loss = _loss_sum(dy, D, name="loss")

    G = {}
    du = _mm(dy16, W["w_ff_down"], name="d_ff_act", tb=True, out_dtypes=(BF16,), extras=(u,),
             epilogue=lambda acc, uu: (acc * (2.0 * uu.astype(F32)),))
    G["w_ff_down"] = _mm(a, dy16, name="d_w_ff_down", ta=True, out_dtypes=(BF16,))
    G["w_ff_up"] = _mm(h2, du, name="d_w_ff_up", ta=True, out_dtypes=(BF16,))
    dh2 = _mm(du, W["w_ff_up"], name="d_mlp_in", tb=True)
    dx1, dg_mlp = _rmsnorm_bwd(x1, small["g_mlp_norm"], dh2, add=dy, name="d_mlp_norm")
    dmerged = _mm(dx1, W["w_out"], name="d_merged", tb=True)
    G["w_out"] = _mm(merged, dx1, name="d_w_out", ta=True, out_dtypes=(BF16,))
    dgate, db0, db1, db2, do_sb, do_fox, do_mem = _gate_bwd(o3, w3, proj, dmerged, D, name="gate_bwd")
    for nm, o, db in zip(("w_branch_sb", "w_branch_fox", "w_branch_mem"), o3, (db0, db1, db2)):
        G[nm] = _mm(o, db, name="d_" + nm, ta=True, out_dtypes=(BF16,))

    r_comms, r_finish = reduce_early({k: G.pop(k) for k in EARLY}) if reduce_early is not None else ([None] * 2, None)
    dsb, landed_sb = _split(_sbl_bwd(sb_q, sb_k, sb_v, (do_sb, 0), (sb_tot, 0), width=D_SB, hd=HD, name="sb_bwd", tq=tq,
                                     comm=r_comms[0]), 3)
    (dfq, dfk, dfv, df_row, df_col), landed_fox = _split(
        _sml_bwd((fq, 0), (fk, 0), fx_v, (o_fox, 0), (fox_lse, 0), (do_fox, 0), f_bias, width=D_FOX, hd=HD, causal=True,
                 name="fox_bwd", tq=tq, tk=tq, comm=r_comms[1]), 5)
    early = r_finish(landed_sb, landed_fox) if r_finish is not None else {}
    dfx_q, dg_fox_q = _hnorm_bwd(fx_q, g_fq, dfq, width=D_FOX, hd=HD, name="d_fox_q_norm")
    dfx_k, dg_fox_k = _hnorm_bwd(fx_k, g_fk, dfk, width=D_FOX, hd=HD, name="d_fox_k_norm")
    d_fcum = df_row.reshape(FOX_HEADS, S) - df_col.reshape(FOX_HEADS, S)
    d_flogit_t, db_forget = _forget_bwd(f_logit_t, b_col, d_fcum, name="forget_bwd")

    dmq_n, dmk_n, dmv = _sml_bwd((mq, 0), (mk, 0), mv, (o_mem, 0), (mem_lse, 0), (do_mem, 0), width=D_MEM, hd=MEM_HD,
                                 causal=False, name="mem_bwd", tq=tq, tk=256)
    dm_q, dg_mem_q = _hnorm_bwd(m_q, g_mq, dmq_n, width=D_MEM, hd=MEM_HD, name="d_mem_q_norm")
    dmk_raw, dg_mem_k = _hnorm_bwd((mkv, 0), g_mk, dmk_n, width=D_MEM, hd=MEM_HD, name="d_mem_k_norm")
    dmkv = jnp.concatenate([dmk_raw, dmv.astype(BF16)], axis=1)
    G["w_mem_kv"] = _mm(mh, dmkv, name="d_w_mem_kv", ta=True, out_dtypes=(BF16,))
    dmh = _mm(dmkv, W["w_mem_kv"], name="d_mem_h", tb=True)
    _, dg_mem = _rmsnorm_bwd(mem, small["g_mem_norm"], dmh, name="d_mem_norm")
    dg_fox_q, dg_fox_k = dg_fox_q[:, :HD], dg_fox_k[:, :HD]

    rest_cols = jnp.concatenate([t.astype(BF16) for t in (*dsb, dfx_q, dfx_k, dfv, dm_q)]
                                + [d_flogit_t.T.astype(BF16), jnp.zeros((S, F_PAD - FOX_HEADS), BF16)], axis=1)
    dproj = lax.dynamic_update_slice(dgate, rest_cols, (0, 3 * D))
    g_w_in = _mm(dproj, h, name="d_w_in", ta=True, out_dtypes=(BF16,), tm=768)
    if reduce_late is None:
        G["w_in"] = g_w_in
        dh = _mm(dproj, W["w_in"], name="d_mix_in", tk=2304)
    else:
        comm, finish = reduce_late({"w_in": g_w_in, "w_mem_kv": G.pop("w_mem_kv")})
        dh, *landed = _mm(dproj, W["w_in"], name="d_mix_in", tk=2304, comm=comm)
        early.update(finish(landed))
    grad_x, dg_mix = _rmsnorm_bwd(x, small["g_mix_norm"], dh, add=dx1, name="d_mix_norm")

    small_grads = dict(g_mix_norm=dg_mix, g_mem_norm=dg_mem, b_forget=db_forget.reshape(1, FOX_HEADS),
                       g_fox_q=dg_fox_q, g_fox_k=dg_fox_k, g_mem_q=dg_mem_q, g_mem_k=dg_mem_k, g_mlp_norm=dg_mlp)
    return loss, grad_x, G, small_grads, early


BIG = ("w_in", "w_mem_kv", "w_branch_sb", "w_branch_fox", "w_branch_mem", "w_out", "w_ff_up", "w_ff_down")
COLUMN_SHARDED = ("w_in", "w_branch_sb", "w_branch_fox", "w_branch_mem", "w_ff_up")
SMALL = ("g_mix_norm", "g_mem_norm", "b_forget", "g_fox_q", "g_fox_k", "g_mem_q", "g_mem_k", "g_mlp_norm")
ORDER = ("g_mix_norm", "g_mem_norm", "w_in", "b_forget", "g_fox_q", "g_fox_k", "g_mem_q", "g_mem_k", "w_mem_kv",
         "w_branch_sb", "w_branch_fox", "w_branch_mem", "w_out", "g_mlp_norm", "w_ff_up", "w_ff_down")


def _unshard(name, gathered):
    n, _, rh, c = gathered.shape
    t = gathered.reshape(n, 2 * rh, c)
    if name in COLUMN_SHARDED:
        return t.transpose(1, 0, 2).reshape(2 * rh, n * c)
    return t.reshape(n * 2 * rh, c)


def _reshard(name, full):
    if name in COLUMN_SHARDED:
        r, c = full.shape
        t = full.reshape(r, N_CHIPS, c // N_CHIPS).transpose(1, 0, 2)
    else:
        r, c = full.shape[0] // N_CHIPS, full.shape[1]
        t = full.reshape(N_CHIPS, r, c)
    return t.reshape(N_CHIPS, 2, t.shape[1] // 2, t.shape[2])


ROW_TILE = 16
IN_BUF_ALIGN = 256


def _in_segments(D):
    n_qkv = 6 * D_SB
    o_mq, o_gate = n_qkv + FOX_HEADS, n_qkv + FOX_HEADS + D_MEM
    return [(0, n_qkv, 3 * D), (n_qkv, o_mq, 3 * D + n_qkv + D_MEM), (o_mq, o_gate, 3 * D + n_qkv), (o_gate, o_gate + 3 * D, 0)]


class _InLayout:
    def __init__(self, D, shard, n):
        self.D, self.shard, self.n = D, shard, n
        down = lambda v: v // ROW_TILE * ROW_TILE
        up = lambda v: -(-v // ROW_TILE) * ROW_TILE
        self.pieces = []
        ends = []
        for s in range(n):
            cursor, mine = 0, []
            for a, b, p in _in_segments(D):
                x0, x1 = max(a, s * shard), min(b, (s + 1) * shard)
                if x0 < x1:
                    p0 = p + x0 - a
                    rows = up(p0 + x1 - x0) - down(p0)
                    mine.append((x0 - s * shard, x1 - x0, p0, cursor, rows))
                    cursor += rows
            self.pieces.append(mine)
            ends.append(cursor)
        self.rows = -(-max(ends) // IN_BUF_ALIGN) * IN_BUF_ALIGN
        self.padded_rows = 3 * D + 6 * D_SB + D_MEM + F_PAD

    def _per_shard(self, fn, chip, operand):
        return lax.switch(chip, [functools.partial(fn, s) for s in range(self.n)], operand)

    def pack(self, chip, rows):
        def one(s, t):
            out, at = [], 0
            for x0, n_rows, p0, start, region in self.pieces[s]:
                lead = p0 % ROW_TILE
                out += [jnp.zeros((start + lead - at, t.shape[1]), t.dtype), t[x0:x0 + n_rows]]
                at = start + lead + n_rows
            return jnp.concatenate(out + [jnp.zeros((self.rows - at, t.shape[1]), t.dtype)], axis=0)
        return self._per_shard(one, chip, rows)

    def unpack(self, chip, buf, pad_to):
        def one(s, t):
            out = [t[start + p0 % ROW_TILE:start + p0 % ROW_TILE + n_rows] for _, n_rows, p0, start, _ in self.pieces[s]]
            return jnp.concatenate(out + [jnp.zeros((pad_to - self.shard, t.shape[1]), t.dtype)], axis=0)
        return self._per_shard(one, chip, buf)

    def to_padded(self, bufs):
        runs = sorted((p0, s, start, region) for s in range(self.n) for _, _, p0, start, region in self.pieces[s])
        chunks, end = [], 0
        for p0, s, start, region in runs:
            d0 = p0 // ROW_TILE * ROW_TILE
            src = bufs[s, start:start + region]
            if d0 < end:
                assert end - d0 == ROW_TILE
                last = chunks.pop()
                chunks += [last[:-ROW_TILE], last[-ROW_TILE:] + src[:ROW_TILE], src[ROW_TILE:]]
            else:
                if d0 > end:
                    chunks.append(jnp.zeros((d0 - end, bufs.shape[2]), bufs.dtype))
                chunks.append(src)
            end = d0 + region
        chunks.append(jnp.zeros((self.padded_rows - end, bufs.shape[2]), bufs.dtype))
        return jnp.concatenate(chunks, axis=0)

    def from_padded(self, gp):
        bufs = []
        for s in range(self.n):
            out, at = [], 0
            for _, n_rows, p0, start, region in self.pieces[s]:
                d0 = p0 // ROW_TILE * ROW_TILE
                row = d0 + lax.broadcasted_iota(jnp.int32, (region, 1), 0)
                out.append(jnp.where((row >= p0) & (row < p0 + n_rows), gp[d0:d0 + region], jnp.zeros((), gp.dtype)))
                at = start + region
            bufs.append(jnp.concatenate(out + [jnp.zeros((self.rows - at, gp.shape[1]), gp.dtype)], axis=0))
        return jnp.stack(bufs)


def _pack_small(vals):
    width = max(vals[k].shape[1] for k in SMALL)
    return jnp.concatenate([jnp.pad(vals[k].astype(F32), ((0, 0), (0, width - vals[k].shape[1]))) for k in SMALL], axis=0)


def _unpack_small(packed, like):
    return {k: packed[i:i + 1, :like[k].shape[1]] for i, k in enumerate(SMALL)}


def kernel(x, mem, g_mix_norm, g_mem_norm, w_in, b_forget, g_fox_q, g_fox_k, g_mem_q, g_mem_k, w_mem_kv, w_branch_sb, w_branch_fox, w_branch_mem, w_out, g_mlp_norm, w_ff_up, w_ff_down, loss_target, m_g_mix_norm, m_g_mem_norm, m_w_in, m_b_forget, m_g_fox_q, m_g_fox_k, m_g_mem_q, m_g_mem_k, m_w_mem_kv, m_w_branch_sb, m_w_branch_fox, m_w_branch_mem, m_w_out, m_g_mlp_norm, m_w_ff_up, m_w_ff_down, v_g_mix_norm, v_g_mem_norm, v_w_in, v_b_forget, v_g_fox_q, v_g_fox_k, v_g_mem_q, v_g_mem_k, v_w_mem_kv, v_w_branch_sb, v_w_branch_fox, v_w_branch_mem, v_w_out, v_g_mlp_norm, v_w_ff_up, v_w_ff_down):
    given = dict(locals())
    D = x.shape[-1]
    weights = {k: given[k] for k in ORDER}
    moms = {k: given["m_" + k] for k in ORDER}
    vars_ = {k: given["v_" + k] for k in ORDER}

    me_chip = 2 * lax.axis_index("x") + lax.axis_index("y")

    n_in = w_in.shape[2]
    lay = _InLayout(D, n_in, N_CHIPS)
    transposed = lambda t: jnp.transpose(t[0])
    shards = {}
    for k in BIG:
        w = weights[k][0].astype(BF16)
        if k == "w_in":
            w = lay.pack(me_chip, jnp.transpose(w))
        shards[k] = w.reshape(2, w.shape[0] // 2, w.shape[1])
    gathered_in = _gather_weights([shards["w_in"]], name="gather_w_in")[0]
    W = {"w_in": lay.to_padded(gathered_in.reshape(N_CHIPS, lay.rows, D))}
    carried = (("w_ff_up",), ("w_ff_down", "w_mem_kv"), ("w_branch_sb", "w_branch_fox", "w_branch_mem", "w_out"))
    rest = [k for grp in carried for k in grp]
    assert sorted(rest + ["w_in"]) == sorted(BIG)

    def finish_weights(landed):
        full = _forward_halves(landed, name="forward_halves")
        full = [lax.dynamic_update_index_in_dim(o, shards[k], me_chip, 0) for k, o in zip(rest, full)]
        return {k: _unshard(k, g) for k, g in zip(rest, full)}

    me_core = lax.axis_index("c")

    def sum_chips(names, parts, got):
        return {k: _chip_sum(p, q, me_chip, name="sum_chips_" + k) for k, p, q in zip(names, parts, got)}

    def pair_sums(grads, tag):
        names = list(grads)
        stacked = {k: _reshard(k, grads[k]) for k in names if k != "w_in"}
        if "w_in" in grads:
            stacked["w_in"] = lay.from_padded(grads["w_in"]).reshape(N_CHIPS, 2, lay.rows // 2, D)
        got = _exchange_siblings([stacked[k] for k in names], name="exchange_siblings_" + tag)
        return {k: _pair_sum(stacked[k], q, me_core, name="sum_pair_" + k) for k, q in zip(names, got)}

    def reduce_early(grads):
        parts = pair_sums(grads, "early")
        groups = [[k for k in parts if k in ("w_ff_down", "w_ff_up")], [k for k in parts if k not in ("w_ff_down", "w_ff_up")]]
        comms = [_ChipExchange("scatter", [parts[k] for k in grp]) for grp in groups]

        def finish(*landed):
            out = {}
            for grp, got in zip(groups, landed):
                out.update(sum_chips(grp, [parts[k] for k in grp], got))
            return out
        return comms, finish

    def reduce_late(grads):
        parts = pair_sums(grads, "late")
        names = list(parts)
        return _ChipExchange("scatter", [parts[k] for k in names]), functools.partial(sum_chips, names, [parts[k] for k in names])

    small = {k: weights[k] for k in SMALL}
    loss_part, grad_x, G, small_grads, halves = _local_step(
        x[0], mem[0], loss_target[0], small, W,
        gather_rest=([_ChipExchange("gather", [shards[k] for k in grp]) for grp in carried], finish_weights),
        reduce_early=reduce_early, reduce_late=reduce_late)
    assert not G, list(G)
    reduced, small_parts = _share_halves([halves[k] for k in BIG], _pack_small(small_grads))

    grads, deltas, new_m, new_v = {}, {}, {}, {}
    for k, g in zip(BIG, reduced):
        shp = weights[k].shape
        if k == "w_in":
            g2 = lay.unpack(me_chip, g.reshape(lay.rows, D), lay.rows)
            padded = lambda t: jnp.pad(transposed(t), ((0, lay.rows - n_in), (0, 0)))
            outs = _adamw(padded(weights[k]), g2, padded(moms[k]), padded(vars_[k]), name="adamw_" + k)
            g2, d, m2, v2 = [jnp.transpose(t[:n_in]) for t in (g2, *outs)]
        else:
            g2 = g.reshape(shp[1], shp[2])
            d, m2, v2 = _adamw(weights[k][0], g2, moms[k][0], vars_[k][0], name="adamw_" + k)
        grads[k], deltas[k], new_m[k], new_v[k] = g2.reshape(shp), d.reshape(shp), m2.reshape(shp), v2.reshape(shp)
    sg, sd, sm, sv = _adamw_small(_pack_small(small), small_parts, _pack_small({k: moms[k] for k in SMALL}),
                                  _pack_small({k: vars_[k] for k in SMALL}), name="adamw_small")
    for dst, packed in ((grads, sg), (deltas, sd), (new_m, sm), (new_v, sv)):
        dst.update(_unpack_small(packed, small))

    loss = lax.psum(loss_part, ("x", "y", "c"))
    return (loss, grad_x[None], *[grads[k] for k in ORDER], *[deltas[k] for k in ORDER],
            *[new_m[k] for k in ORDER], *[new_v[k] for k in ORDER])
```

```python
import functools

import jax
import jax.numpy as jnp
from jax import lax
from jax.experimental import pallas as pl
from jax.experimental.pallas import tpu as pltpu

F32 = jnp.float32
BF16 = jnp.bfloat16
MESH_ID = pl.DeviceIdType.MESH

HD = 64
SB_HEADS = 8
FOX_HEADS = 8
MEM_HEADS = 4
MEM_HD = 128
D_SB = SB_HEADS * HD
D_FOX = FOX_HEADS * HD
D_MEM = MEM_HEADS * MEM_HD
EPS = 1e-6
NEG_INF = -1e30

ADAM_LR = 0.001
ADAM_B1 = 0.9
ADAM_B2 = 0.999
ADAM_EPS = 1e-08
ADAM_WD = 0.01
ADAM_STEP = 10

N_CHIPS = 4
VMEM_LIMIT = 56 * 1024 * 1024

F_PAD = 256


def _tile(n, target, align=128):
    if n <= target:
        return n
    best = None
    t = align
    while t <= target:
        if n % t == 0:
            best = t
        t += align
    assert best is not None, (n, target, align)
    return best


def _params(sem):
    return pltpu.CompilerParams(dimension_semantics=sem, vmem_limit_bytes=VMEM_LIMIT)


def _mm(a, b, *, name, ta=False, tb=False, out_dtypes=(F32,), epilogue=None, extras=(),
        tm=1024, tn=1024, tk=2048, comm=None):
    if ta:
        K, M = a.shape
    else:
        M, K = a.shape
    if tb:
        N, K2 = b.shape
    else:
        K2, N = b.shape
    assert K == K2, (a.shape, b.shape, ta, tb)
    tm, tn, tk = _tile(M, tm), _tile(N, tn), _tile(K, tk)
    nk = K // tk
    n_extra, n_out = len(extras), len(out_dtypes)
    if epilogue is None:
        epilogue = lambda acc: (acc,)
    dims = (((0 if ta else 1,), (1 if tb else 0,)), ((), ()))

    gm, gn = M // tm, N // tn

    def body(*refs):
        i, j, k = pl.program_id(0), pl.program_id(1), pl.program_id(2)
        got = _carry(comm, 2 + n_extra, n_out, (i == 0) & (j == 0) & (k == 0),
                     (i == gm - 1) & (j == gn - 1) & (k == nk - 1), refs)
        (a_ref, b_ref, *extra_refs), out_refs = got[0], got[1]
        part = lax.dot_general(a_ref[...].astype(BF16), b_ref[...].astype(BF16), dims, preferred_element_type=F32)

        def finish(acc):
            outs = epilogue(acc, *[r[...] for r in extra_refs])
            for o_ref, o in zip(out_refs, outs):
                o_ref[...] = o.astype(o_ref.dtype)

        if nk == 1:
            finish(part)
        else:
            acc_ref = refs[-1]

            @pl.when(k == 0)
            def _():
                acc_ref[...] = part

            @pl.when((k > 0) & (k < nk - 1))
            def _():
                acc_ref[...] += part

            @pl.when(k == nk - 1)
            def _():
                finish(acc_ref[...] + part)

        got[2]()

    a_spec = pl.BlockSpec((tk, tm), lambda i, j, k: (k, i)) if ta else pl.BlockSpec((tm, tk), lambda i, j, k: (i, k))
    b_spec = pl.BlockSpec((tn, tk), lambda i, j, k: (j, k)) if tb else pl.BlockSpec((tk, tn), lambda i, j, k: (k, j))
    mn_spec = pl.BlockSpec((tm, tn), lambda i, j, k: (i, j))
    c_ins, c_in_specs, c_out_specs, c_out_shape, c_scratch = _comm_args(comm)
    sem = ("parallel", "parallel", "arbitrary") if comm is None else ("arbitrary",) * 3
    outs = pl.pallas_call(
        body, name=name,
        grid=(gm, gn, nk),
        in_specs=[a_spec, b_spec] + [mn_spec] * n_extra + c_in_specs,
        out_specs=[mn_spec] * n_out + c_out_specs,
        out_shape=[jax.ShapeDtypeStruct((M, N), dt) for dt in out_dtypes] + c_out_shape,
        scratch_shapes=c_scratch + ([pltpu.VMEM((tm, tn), F32)] if nk > 1 else []),
        compiler_params=_params(sem),
    )(a, b, *extras, *c_ins)
    return outs if len(outs) > 1 else outs[0]


def _row_tile(rows, cols, n_arrays):
    budget = 10 * 1024 * 1024
    cols_padded = -(-cols // 128) * 128
    target = max(16, budget // (cols_padded * 4 * n_arrays * 2))
    return _tile(rows, target, align=16)


def _ew(fn, ins, out_dtypes, *, name):
    R, C = ins[0].shape
    n_in, n_out = len(ins), len(out_dtypes)
    tr = _row_tile(R, C, n_in + n_out)

    def body(*refs):
        outs = fn(*[r[...] for r in refs[:n_in]])
        for o_ref, o in zip(refs[n_in:], outs):
            o_ref[...] = o.astype(o_ref.dtype)

    spec = pl.BlockSpec((tr, C), lambda i: (i, 0))
    outs = pl.pallas_call(
        body, name=name, grid=(R // tr,),
        in_specs=[spec] * n_in, out_specs=[spec] * n_out,
        out_shape=[jax.ShapeDtypeStruct((R, C), dt) for dt in out_dtypes],
        compiler_params=_params(("parallel",)),
    )(*ins)
    return outs if n_out > 1 else outs[0]


def _rmsnorm_fwd(x, g, out_dtype, *, name):
    R, d = x.shape
    tr = _row_tile(R, d, 3)

    def body(x_ref, g_ref, o_ref):
        xv = x_ref[...].astype(F32)
        r = lax.rsqrt(jnp.mean(xv * xv, axis=-1, keepdims=True) + EPS)
        o_ref[...] = (xv * r * g_ref[...]).astype(o_ref.dtype)

    return pl.pallas_call(
        body, name=name, grid=(R // tr,),
        in_specs=[pl.BlockSpec((tr, d), lambda i: (i, 0)), pl.BlockSpec((1, d), lambda i: (0, 0))],
        out_specs=pl.BlockSpec((tr, d), lambda i: (i, 0)),
        out_shape=jax.ShapeDtypeStruct((R, d), out_dtype),
        compiler_params=_params(("parallel",)),
    )(x, g)


def _rmsnorm_bwd(x, g, dy, add=None, *, name):
    R, d = x.shape
    has_add = add is not None
    tr = _row_tile(R, d, 5)

    def body(*refs):
        x_ref, g_ref, dy_ref = refs[:3]
        add_ref = refs[3] if has_add else None
        dx_ref, dg_ref = refs[-2:]
        xv = x_ref[...].astype(F32)
        dyv = dy_ref[...].astype(F32)
        r = lax.rsqrt(jnp.mean(xv * xv, axis=-1, keepdims=True) + EPS)
        xh = xv * r
        dyg = dyv * g_ref[...]
        c = jnp.mean(dyg * xh, axis=-1, keepdims=True)
        dx = r * (dyg - xh * c)
        if has_add:
            dx = dx + add_ref[...]
        dx_ref[...] = dx

        @pl.when(pl.program_id(0) == 0)
        def _():
            dg_ref[...] = jnp.zeros_like(dg_ref)

        dg_ref[...] += jnp.sum(dyv * xh, axis=0, keepdims=True)

    row = pl.BlockSpec((tr, d), lambda i: (i, 0))
    vec = pl.BlockSpec((1, d), lambda i: (0, 0))
    ins = [x, g, dy] + ([add] if has_add else [])
    return pl.pallas_call(
        body, name=name, grid=(R // tr,),
        in_specs=[row, vec, row] + ([row] if has_add else []),
        out_specs=[row, vec],
        out_shape=[jax.ShapeDtypeStruct((R, d), F32), jax.ShapeDtypeStruct((1, d), F32)],
        compiler_params=_params(("arbitrary",)),
    )(*ins)


_NT = (((1,), (1,)), ((), ()))
_TN = (((0,), (0,)), ((), ()))


def _dot(a, b, dims=(((1,), (0,)), ((), ()))):
    return lax.dot_general(a, b, dims, preferred_element_type=F32)


def _log_sigmoid_pair(z):
    sp = jnp.log(1.0 + jnp.exp(-jnp.abs(z)))
    return jnp.minimum(z, 0.0) - sp, jnp.minimum(-z, 0.0) - sp


LANES = 128
_LOW = -3e38


def _lane_masks(hd, rows):
    if hd == LANES:
        return [None]
    lane = lax.broadcasted_iota(jnp.int32, (rows, LANES), 1)
    return [(lane >= hh * hd) & (lane < (hh + 1) * hd) for hh in range(LANES // hd)]


def _keep(t, m):
    return t if m is None else jnp.where(m, t, 0.0)


def _merge(parts, masks):
    out = parts[-1]
    for p, m in zip(parts[-2::-1], masks[-2::-1]):
        out = jnp.where(m, p, out)
    return out


def _row_value(t, m):
    return jnp.max(t if m is None else jnp.where(m, t, _LOW), axis=1, keepdims=True)


def _cols(tq, off):
    return pl.BlockSpec((tq, LANES), lambda g, i: (i, off + g))


def _cols_all(rows, off):
    return pl.BlockSpec((rows, LANES), lambda g, i: (0, off + g))


SCAN_BLOCK = 256


def _tri(kind, cols):
    n = min(SCAN_BLOCK, cols)
    r = lax.broadcasted_iota(jnp.int32, (n, n), 0)
    c = lax.broadcasted_iota(jnp.int32, (n, n), 1)
    return ((r > c) if kind == "after" else (r < c)).astype(BF16)


def _scan_cols(x, tri, reverse):
    cols = x.shape[1]
    cb = min(SCAN_BLOCK, cols)
    assert cols % cb == 0 and tri.shape == (cb, cb)
    nb = cols // cb
    blocks = [x[:, b * cb:(b + 1) * cb] for b in range(nb)]
    outs, carry = [None] * nb, None
    for b in (reversed(range(nb)) if reverse else range(nb)):
        y = _dot(blocks[b].astype(BF16), tri)
        outs[b] = y if carry is None else y + carry
        s = jnp.sum(blocks[b], axis=1, keepdims=True)
        carry = s if carry is None else carry + s
    return (outs[0] if nb == 1 else jnp.concatenate(outs, axis=1)), carry


def _softplus_parts(z):
    pos = jnp.maximum(z, 0.0) + jnp.log(1.0 + jnp.exp(-jnp.abs(z)))
    return pos, z - pos


class _ChipExchange:
    def __init__(self, kind, ins):
        assert kind in ("gather", "scatter")
        self.kind, self.ins = kind, list(ins)
        lead = (lambda s: (N_CHIPS,) + s) if kind == "gather" else (lambda s: (3,) + s[1:])
        self.out_shape = [jax.ShapeDtypeStruct(lead(a.shape), a.dtype) for a in ins]
        n = 3 * len(ins)
        self.scratch = [pltpu.SemaphoreType.DMA((n,)), pltpu.SemaphoreType.DMA((n,))]

    def _copies(self, in_refs, out_refs, sems, landing):
        x, y, c, chips = _mesh_place()
        me = 2 * x + y
        out = []
        for w in range(len(self.ins)):
            for j, chip in enumerate(chips):
                peer = 2 * chip[0] + chip[1]
                if self.kind == "gather":
                    src, dst, land = in_refs[w].at[c], out_refs[w].at[me, c], out_refs[w].at[peer, c]
                else:
                    src, dst, land = in_refs[w].at[peer], out_refs[w].at[j], out_refs[w].at[j]
                if landing:
                    src, dst = land, land
                out.append(_remote(src, dst, sems, 3 * w + j, (chip[0], chip[1], c)))
        return out

    def start(self, in_refs, out_refs, sems):
        for cp in self._copies(in_refs, out_refs, sems, False):
            cp.start()

    def finish(self, in_refs, out_refs, sems):
        for cp in self._copies(in_refs, out_refs, sems, True):
            cp.wait_recv()
        for cp in self._copies(in_refs, out_refs, sems, False):
            cp.wait_send()


def _carry(comm, n_in, n_out, first, last, refs):
    if comm is None:
        return refs[:n_in], refs[n_in:n_in + n_out], (lambda: None)
    a, b = len(comm.ins), len(comm.out_shape)
    ins, c_in = refs[:n_in], refs[n_in:n_in + a]
    outs, c_out = refs[n_in + a:n_in + a + n_out], refs[n_in + a + n_out:n_in + a + n_out + b]
    sems = refs[n_in + a + n_out + b:n_in + a + n_out + b + 2]
    pl.when(first)(lambda: comm.start(c_in, c_out, sems))
    return ins, outs, (lambda: pl.when(last)(lambda: comm.finish(c_in, c_out, sems)))


def _sbl_fwd(q, k, v, *, width, hd, name, tq=256, comm=None):
    (qa, qo), (ka, ko), (va, vo) = q, k, v
    S = qa.shape[0]
    tq = _tile(S, tq)
    tk = tq
    scale = hd ** -0.5
    n_g, n_q = width // LANES, S // tq

    def body(*refs):
        qi = pl.program_id(1)
        gi = pl.program_id(0)
        got = _carry(comm, 3, 2, (gi == 0) & (qi == 0), (gi == n_g - 1) & (qi == n_q - 1), refs)
        (q_ref, k_ref, v_ref), (o_ref, tot_ref) = got[0], got[1]
        masks = _lane_masks(hd, tq)
        qs = q_ref[...].astype(F32) * scale
        qm = [_keep(qs, m).astype(BF16) for m in masks]
        strict = lax.broadcasted_iota(jnp.int32, (tq, tk), 1) < lax.broadcasted_iota(jnp.int32, (tq, tk), 0)
        later = _tri("after", tk)

        def tile(kb, carry, diag):
            ks = pl.multiple_of(kb * tk, tk)
            kv = k_ref[pl.ds(ks, tk), :].astype(BF16)
            vv = v_ref[pl.ds(ks, tk), :].astype(BF16)
            out = []
            for hh in range(len(masks)):
                acc, c_pos = carry[2 * hh], carry[2 * hh + 1]
                pos, ls = _softplus_parts(_dot(qm[hh], kv, _NT))
                if diag:
                    pos = jnp.where(strict, pos, 0.0)
                pos_after, pos_all = _scan_cols(pos, later, True)
                w = jnp.exp(ls - (pos_after + c_pos))
                if diag:
                    w = jnp.where(strict, w, 0.0)
                out += [acc + _dot(w.astype(BF16), vv), c_pos + pos_all]
            return tuple(out)

        init = (jnp.zeros((tq, LANES), F32), jnp.zeros((tq, 1), F32)) * len(masks)
        carry = tile(qi, init, True)
        carry = lax.fori_loop(0, qi, lambda i, c: tile(qi - 1 - i, c, False), carry)
        o_ref[...] = _merge(carry[0::2], masks).astype(o_ref.dtype)
        tot_ref[...] = _merge([jnp.broadcast_to(-c, (tq, LANES)) for c in carry[1::2]], masks)
        got[2]()

    c_ins, c_in_specs, c_out_specs, c_out_shape, c_scratch = _comm_args(comm)
    return pl.pallas_call(
        body, name=name, grid=(n_g, n_q),
        in_specs=[_cols(tq, qo), _cols_all(S, ko), _cols_all(S, vo)] + c_in_specs,
        out_specs=[_cols(tq, 0), _cols(tq, 0)] + c_out_specs,
        out_shape=[jax.ShapeDtypeStruct((S, width), BF16), jax.ShapeDtypeStruct((S, width), F32)] + c_out_shape,
        scratch_shapes=c_scratch,
        compiler_params=_params(("arbitrary", "arbitrary")),
    )(qa, ka, va, *c_ins)


def _comm_args(comm):
    if comm is None:
        return [], [], [], [], []
    return comm.ins, [_ANY] * len(comm.ins), [_ANY] * len(comm.out_shape), comm.out_shape, comm.scratch


def _sbl_bwd(q, k, v, do, tot, *, width, hd, name, tq=256, comm=None):
    (qa, qo), (ka, ko), (va, vo) = q, k, v
    S = qa.shape[0]
    tq = _tile(S, tq)
    tk = tq
    scale = hd ** -0.5
    n_g, n_q = width // LANES, S // tq

    def body(*refs):
        qi = pl.program_id(1)
        gi = pl.program_id(0)
        got = _carry(comm, 5, 3, (gi == 0) & (qi == 0), (gi == n_g - 1) & (qi == n_q - 1), refs)
        (q_ref, k_ref, v_ref, do_ref, tot_ref), (dq_ref, dk_ref, dv_ref) = got[0], got[1]

        @pl.when(qi == 0)
        def _():
            dk_ref[...] = jnp.zeros_like(dk_ref)
            dv_ref[...] = jnp.zeros_like(dv_ref)

        masks = _lane_masks(hd, tq)
        qs = q_ref[...].astype(F32) * scale
        qm = [_keep(qs, m).astype(BF16) for m in masks]
        dov = [_keep(do_ref[...], m).astype(BF16) for m in masks]
        rest = [-_row_value(tot_ref[...], m) for m in masks]
        strict = lax.broadcasted_iota(jnp.int32, (tq, tk), 1) < lax.broadcasted_iota(jnp.int32, (tq, tk), 0)
        later, before = _tri("after", tk), _tri("before", tk)

        def tile(kb, carry, diag):
            ks = pl.multiple_of(kb * tk, tk)
            kv = k_ref[pl.ds(ks, tk), :].astype(BF16)
            vv = v_ref[pl.ds(ks, tk), :].astype(BF16)
            out = []
            dk_t, dv_t = None, None
            for hh in range(len(masks)):
                dq, c_pos, c_g = carry[3 * hh:3 * hh + 3]
                pos, ls = _softplus_parts(_dot(qm[hh], kv, _NT))
                if diag:
                    pos = jnp.where(strict, pos, 0.0)
                pos_after, pos_all = _scan_cols(pos, later, True)
                c_pos = c_pos + pos_all
                w = jnp.exp(ls - (pos_after + (rest[hh] - c_pos)))
                if diag:
                    w = jnp.where(strict, w, 0.0)
                g = _dot(dov[hh], vv, _NT) * w
                g_before, g_all = _scan_cols(g, before, False)
                g_before = g_before + c_g
                dz = g - jnp.exp(ls) * (g + g_before)
                if diag:
                    dz = jnp.where(strict, dz, 0.0)
                dzb = dz.astype(BF16)
                dk_h = _dot(dzb, qm[hh], _TN)
                dv_h = _dot(w.astype(BF16), dov[hh], _TN)
                dk_t = dk_h if dk_t is None else dk_t + dk_h
                dv_t = dv_h if dv_t is None else dv_t + dv_h
                out += [dq + _dot(dzb, kv), c_pos, c_g + g_all]
            dk_ref[pl.ds(ks, tk), :] += dk_t
            dv_ref[pl.ds(ks, tk), :] += dv_t
            return tuple(out)

        zero = jnp.zeros((tq, 1), F32)
        init = (jnp.zeros((tq, LANES), F32), zero, zero) * len(masks)
        carry = lax.fori_loop(0, qi, lambda kb, c: tile(kb, c, False), init)
        carry = tile(qi, carry, True)
        dq_ref[...] = _merge(carry[0::3], masks) * scale
        got[2]()

    full = jax.ShapeDtypeStruct((S, width), F32)
    c_ins, c_in_specs, c_out_specs, c_out_shape, c_scratch = _comm_args(comm)
    return pl.pallas_call(
        body, name=name, grid=(n_g, n_q),
        in_specs=[_cols(tq, qo), _cols_all(S, ko), _cols_all(S, vo), _cols(tq, do[1]), _cols(tq, tot[1])] + c_in_specs,
        out_specs=[_cols(tq, 0), _cols_all(S, 0), _cols_all(S, 0)] + c_out_specs,
        out_shape=[full, full, full] + c_out_shape,
        scratch_shapes=c_scratch,
        compiler_params=_params(("arbitrary", "arbitrary")),
    )(qa, ka, va, do[0], tot[0], *c_ins)


def _sml_fwd(q, k, v, bias=None, *, width, hd, causal, name, tq=256, tk=256, comm=None):
    (qa, qo), (ka, ko), (va, vo) = q, k, v
    S, Sk = qa.shape[0], ka.shape[0]
    tq, tk = _tile(S, tq), _tile(Sk, tk)
    if causal:
        assert tq == tk and S == Sk
    nk = Sk // tk
    hpg = LANES // hd
    scale = hd ** -0.5
    has_bias = bias is not None
    n_g, n_q = width // LANES, S // tq

    def body(*all_refs):
        qi, gi = pl.program_id(1), pl.program_id(0)
        got = _carry(comm, 5 if has_bias else 3, 2, (gi == 0) & (qi == 0), (gi == n_g - 1) & (qi == n_q - 1), all_refs)
        refs = tuple(got[0]) + tuple(got[1])
        q_ref, k_ref, v_ref = refs[:3]
        o_ref, lse_ref = refs[-2:]
        masks = _lane_masks(hd, tq)
        qs = q_ref[...].astype(F32) * scale
        qm = [_keep(qs, m).astype(BF16) for m in masks]
        allowed = lax.broadcasted_iota(jnp.int32, (tq, tk), 1) <= lax.broadcasted_iota(jnp.int32, (tq, tk), 0)

        def tile(kb, carry, diag):
            ks = pl.multiple_of(kb * tk, tk)
            kv = k_ref[pl.ds(ks, tk), :].astype(BF16)
            vv = v_ref[pl.ds(ks, tk), :].astype(BF16)
            out = []
            for hh in range(hpg):
                m, l, acc = carry[3 * hh:3 * hh + 3]
                z = _dot(qm[hh], kv, _NT)
                if has_bias:
                    z = z + refs[3][hh] - refs[4][hh, kb]
                if diag:
                    z = jnp.where(allowed, z, NEG_INF)
                m2 = jnp.maximum(m, jnp.max(z, axis=1, keepdims=True))
                p = jnp.exp(z - m2)
                alpha = jnp.exp(m - m2)
                out += [m2, alpha * l + jnp.sum(p, axis=1, keepdims=True), alpha * acc + _dot(p.astype(BF16), vv)]
            return tuple(out)

        init = (jnp.full((tq, 1), NEG_INF, F32), jnp.zeros((tq, 1), F32), jnp.zeros((tq, LANES), F32)) * hpg
        if causal:
            carry = lax.fori_loop(0, qi, lambda kb, c: tile(kb, c, False), init)
            carry = tile(qi, carry, True)
        else:
            carry = lax.fori_loop(0, nk, lambda kb, c: tile(kb, c, False), init)
        o_ref[...] = _merge([acc / l for l, acc in zip(carry[1::3], carry[2::3])], masks).astype(o_ref.dtype)
        lse_ref[...] = _merge([jnp.broadcast_to(m + jnp.log(l), (tq, LANES)) for m, l in zip(carry[0::3], carry[1::3])], masks)
        got[2]()

    in_specs = [_cols(tq, qo), _cols_all(Sk, ko), _cols_all(Sk, vo)]
    ins = [qa, ka, va]
    if has_bias:
        in_specs += [pl.BlockSpec((hpg, tq, 1), lambda g, i: (g, i, 0)),
                     pl.BlockSpec((hpg, nk, 1, tk), lambda g, i: (g, 0, 0, 0))]
        ins += list(bias)
    c_ins, c_in_specs, c_out_specs, c_out_shape, c_scratch = _comm_args(comm)
    return pl.pallas_call(
        body, name=name, grid=(n_g, n_q),
        in_specs=in_specs + c_in_specs, out_specs=[_cols(tq, 0), _cols(tq, 0)] + c_out_specs,
        out_shape=[jax.ShapeDtypeStruct((S, width), BF16), jax.ShapeDtypeStruct((S, width), F32)] + c_out_shape,
        scratch_shapes=c_scratch,
        compiler_params=_params(("arbitrary", "arbitrary") if comm is not None else ("parallel", "arbitrary")),
    )(*ins, *c_ins)


def _sml_bwd(q, k, v, o, lse, do, bias=None, *, width, hd, causal, name, tq=256, tk=256, comm=None):
    (qa, qo), (ka, ko), (va, vo) = q, k, v
    S, Sk = qa.shape[0], ka.shape[0]
    tq, tk = _tile(S, tq), _tile(Sk, tk)
    nk = Sk // tk
    hpg = LANES // hd
    scale = hd ** -0.5
    has_bias = bias is not None
    n_in = 8 if has_bias else 6
    n_g, n_q = width // LANES, S // tq

    def body(*all_refs):
        qi, gi = pl.program_id(1), pl.program_id(0)
        got = _carry(comm, n_in, 5 if has_bias else 3, (gi == 0) & (qi == 0), (gi == n_g - 1) & (qi == n_q - 1), all_refs)
        refs = tuple(got[0]) + tuple(got[1])
        q_ref, k_ref, v_ref, o_ref, lse_ref, do_ref = refs[:6]
        dq_ref, dk_ref, dv_ref = refs[n_in:n_in + 3]

        @pl.when(qi == 0)
        def _():
            dk_ref[...] = jnp.zeros_like(dk_ref)
            dv_ref[...] = jnp.zeros_like(dv_ref)
            if has_bias:
                refs[n_in + 4][...] = jnp.zeros_like(refs[n_in + 4])

        masks = _lane_masks(hd, tq)
        qs = q_ref[...].astype(F32) * scale
        qm = [_keep(qs, m).astype(BF16) for m in masks]
        do32 = do_ref[...]
        dov = [_keep(do32, m).astype(BF16) for m in masks]
        prod = do32 * o_ref[...].astype(F32)
        delta = [jnp.sum(_keep(prod, m), axis=1, keepdims=True) for m in masks]
        lses = [_row_value(lse_ref[...], m) for m in masks]
        allowed = lax.broadcasted_iota(jnp.int32, (tq, tk), 1) <= lax.broadcasted_iota(jnp.int32, (tq, tk), 0)

        def tile(kb, carry, diag):
            ks = pl.multiple_of(kb * tk, tk)
            kv = k_ref[pl.ds(ks, tk), :].astype(BF16)
            vv = v_ref[pl.ds(ks, tk), :].astype(BF16)
            out = []
            dk_t, dv_t = None, None
            for hh in range(hpg):
                dq, db_row = carry[2 * hh:2 * hh + 2]
                z = _dot(qm[hh], kv, _NT)
                if has_bias:
                    z = z + refs[6][hh] - refs[7][hh, kb]
                p = jnp.exp(z - lses[hh])
                if diag:
                    p = jnp.where(allowed, p, 0.0)
                dz = p * (_dot(dov[hh], vv, _NT) - delta[hh])
                dzb = dz.astype(BF16)
                dk_h = _dot(dzb, qm[hh], _TN)
                dv_h = _dot(p.astype(BF16), dov[hh], _TN)
                dk_t = dk_h if dk_t is None else dk_t + dk_h
                dv_t = dv_h if dv_t is None else dv_t + dv_h
                if has_bias:
                    db_row = db_row + jnp.sum(dz, axis=1, keepdims=True)
                    refs[n_in + 4][hh, kb] += jnp.sum(dz, axis=0, keepdims=True)
                out += [dq + _dot(dzb, kv), db_row]
            dk_ref[pl.ds(ks, tk), :] += dk_t
            dv_ref[pl.ds(ks, tk), :] += dv_t
            return tuple(out)

        init = (jnp.zeros((tq, LANES), F32), jnp.zeros((tq, 1), F32)) * hpg
        if causal:
            carry = lax.fori_loop(0, qi, lambda kb, c: tile(kb, c, False), init)
            carry = tile(qi, carry, True)
        else:
            carry = lax.fori_loop(0, nk, lambda kb, c: tile(kb, c, False), init)
        dq_ref[...] = _merge(carry[0::2], masks) * scale
        if has_bias:
            for hh in range(hpg):
                refs[n_in + 3][hh] = carry[2 * hh + 1]
        got[2]()

    in_specs = [_cols(tq, qo), _cols_all(Sk, ko), _cols_all(Sk, vo), _cols(tq, o[1]), _cols(tq, lse[1]), _cols(tq, do[1])]
    ins = [qa, ka, va, o[0], lse[0], do[0]]
    out_specs = [_cols(tq, 0), _cols_all(Sk, 0), _cols_all(Sk, 0)]
    out_shape = [jax.ShapeDtypeStruct((S, width), F32), jax.ShapeDtypeStruct((Sk, width), F32),
                 jax.ShapeDtypeStruct((Sk, width), F32)]
    if has_bias:
        rspec = pl.BlockSpec((hpg, tq, 1), lambda g, i: (g, i, 0))
        cspec = pl.BlockSpec((hpg, nk, 1, tk), lambda g, i: (g, 0, 0, 0))
        in_specs += [rspec, cspec]
        ins += list(bias)
        out_specs += [rspec, cspec]
        n_heads = width // hd
        out_shape += [jax.ShapeDtypeStruct((n_heads, S, 1), F32), jax.ShapeDtypeStruct((n_heads, nk, 1, tk), F32)]
    c_ins, c_in_specs, c_out_specs, c_out_shape, c_scratch = _comm_args(comm)
    return pl.pallas_call(
        body, name=name, grid=(n_g, n_q),
        in_specs=in_specs + c_in_specs, out_specs=out_specs + c_out_specs, out_shape=out_shape + c_out_shape,
        scratch_shapes=c_scratch,
        compiler_params=_params(("arbitrary", "arbitrary") if comm is not None else ("parallel", "arbitrary")),
    )(*ins, *c_ins)


def _head_sums(t, masks):
    sums = [jnp.sum(_keep(t, m), axis=1, keepdims=True) for m in masks]
    return _merge([jnp.broadcast_to(s, t.shape) for s in sums], masks) if len(masks) > 1 else sums[0]


def _hnorm_fwd(x, g_lanes, *, width, hd, name, tr=512):
    xa, xo = x
    R = xa.shape[0]
    tr = _tile(R, tr, align=16)
    n_blk = width // LANES

    def body(x_ref, g_ref, o_ref):
        masks = _lane_masks(hd, tr)
        for j in range(n_blk):
            sl = slice(j * LANES, (j + 1) * LANES)
            xv = x_ref[:, sl].astype(F32)
            r = lax.rsqrt(_head_sums(xv * xv, masks) * (1.0 / hd) + EPS)
            o_ref[:, sl] = (xv * r * g_ref[...]).astype(o_ref.dtype)

    assert (xo * LANES) % width == 0
    return pl.pallas_call(
        body, name=name, grid=(R // tr,),
        in_specs=[pl.BlockSpec((tr, width), lambda i: (i, xo * LANES // width)), pl.BlockSpec((1, LANES), lambda i: (0, 0))],
        out_specs=pl.BlockSpec((tr, width), lambda i: (i, 0)),
        out_shape=jax.ShapeDtypeStruct((R, width), BF16),
        compiler_params=_params(("parallel",)),
    )(xa, g_lanes)


def _hnorm_bwd(x, g_lanes, dy, *, width, hd, name, tr=512):
    xa, xo = x
    R = xa.shape[0]
    tr = _tile(R, tr, align=16)
    n_blk = width // LANES

    def body(x_ref, g_ref, dy_ref, dx_ref, dg_ref):
        masks = _lane_masks(hd, tr)
        dg = jnp.zeros((1, LANES), F32)
        for j in range(n_blk):
            sl = slice(j * LANES, (j + 1) * LANES)
            xv = x_ref[:, sl].astype(F32)
            dyv = dy_ref[:, sl].astype(F32)
            r = lax.rsqrt(_head_sums(xv * xv, masks) * (1.0 / hd) + EPS)
            xh = xv * r
            dyg = dyv * g_ref[...]
            c = _head_sums(dyg * xh, masks) * (1.0 / hd)
            dx_ref[:, sl] = (r * (dyg - xh * c)).astype(dx_ref.dtype)
            dg = dg + jnp.sum(dyv * xh, axis=0, keepdims=True)
        if hd * 2 == LANES:
            dg8 = jnp.broadcast_to(dg, (8, LANES))
            dg = (dg8 + pltpu.roll(dg8, shift=hd, axis=1))[0:1]
        else:
            assert hd == LANES

        @pl.when(pl.program_id(0) == 0)
        def _():
            dg_ref[...] = jnp.zeros_like(dg_ref)

        dg_ref[...] += dg

    assert (xo * LANES) % width == 0
    return pl.pallas_call(
        body, name=name, grid=(R // tr,),
        in_specs=[pl.BlockSpec((tr, width), lambda i: (i, xo * LANES // width)), pl.BlockSpec((1, LANES), lambda i: (0, 0)),
                  pl.BlockSpec((tr, width), lambda i: (i, 0))],
        out_specs=[pl.BlockSpec((tr, width), lambda i: (i, 0)), pl.BlockSpec((1, LANES), lambda i: (0, 0))],
        out_shape=[jax.ShapeDtypeStruct((R, width), BF16), jax.ShapeDtypeStruct((1, LANES), F32)],
        compiler_params=_params(("arbitrary",)),
    )(xa, g_lanes, dy)


def _split3_dot(x, tri):
    a = x.astype(BF16)
    r = x - a.astype(F32)
    b = r.astype(BF16)
    c = (r - b.astype(F32)).astype(BF16)
    return _dot(a, tri) + _dot(b, tri) + _dot(c, tri)


def _forget_fwd(logit_t, b_col, *, name, blk=512):
    H, S = logit_t.shape
    blk = _tile(S, blk)

    def body(l_ref, b_ref, f_ref):
        r_i = lax.broadcasted_iota(jnp.int32, (blk, blk), 0)
        c_i = lax.broadcasted_iota(jnp.int32, (blk, blk), 1)
        upto = (r_i <= c_i).astype(BF16)
        carry = jnp.zeros((H, 1), F32)
        for j in range(S // blk):
            u = l_ref[:, j * blk:(j + 1) * blk] + b_ref[...]
            lf, _ = _log_sigmoid_pair(u)
            f_ref[:, j * blk:(j + 1) * blk] = _split3_dot(lf, upto) + carry
            carry = carry + jnp.sum(lf, axis=1, keepdims=True)

    return pl.pallas_call(
        body, name=name,
        out_shape=jax.ShapeDtypeStruct((H, S), F32),
        compiler_params=pltpu.CompilerParams(vmem_limit_bytes=VMEM_LIMIT),
    )(logit_t, b_col)


def _forget_bwd(logit_t, b_col, d_f, *, name, blk=512):
    H, S = logit_t.shape
    blk = _tile(S, blk)

    def body(l_ref, b_ref, df_ref, dl_ref, db_ref):
        r_i = lax.broadcasted_iota(jnp.int32, (blk, blk), 0)
        c_i = lax.broadcasted_iota(jnp.int32, (blk, blk), 1)
        fromon = (r_i >= c_i).astype(BF16)
        carry = jnp.zeros((H, 1), F32)
        db = jnp.zeros((H, 1), F32)
        for j in reversed(range(S // blk)):
            sl = slice(j * blk, (j + 1) * blk)
            dfv = df_ref[:, sl]
            d_lf = _split3_dot(dfv, fromon) + carry
            carry = carry + jnp.sum(dfv, axis=1, keepdims=True)
            u = l_ref[:, sl] + b_ref[...]
            _, lsn = _log_sigmoid_pair(u)
            dl = d_lf * jnp.exp(lsn)
            dl_ref[:, sl] = dl
            db = db + jnp.sum(dl, axis=1, keepdims=True)
        db_ref[...] = db

    return pl.pallas_call(
        body, name=name,
        out_shape=[jax.ShapeDtypeStruct((H, S), F32), jax.ShapeDtypeStruct((H, 1), F32)],
        compiler_params=pltpu.CompilerParams(vmem_limit_bytes=VMEM_LIMIT),
    )(logit_t, b_col, d_f)


def _sigmoid(t):
    return 1.0 / (1.0 + jnp.exp(-t))


def _gate_fwd(o3, w3, proj, D, *, name, tm=256):
    S = proj.shape[0]
    tm = _tile(S, tm)

    def body(o0, o1, o2, w0, w1, w2, g0, g1, g2, out_ref):
        acc = None
        for o_ref, w_ref, g_ref in ((o0, w0, g0), (o1, w1, g1), (o2, w2, g2)):
            t = _sigmoid(g_ref[...]) * _dot(o_ref[...], w_ref[...])
            acc = t if acc is None else acc + t
        out_ref[...] = acc.astype(out_ref.dtype)

    ospec = lambda d: pl.BlockSpec((tm, d), lambda i: (i, 0))
    wspec = lambda w: pl.BlockSpec(w.shape, lambda i: (0, 0))
    gspec = lambda j: pl.BlockSpec((tm, D), lambda i: (i, j))
    return pl.pallas_call(
        body, name=name, grid=(S // tm,),
        in_specs=[ospec(o.shape[1]) for o in o3] + [wspec(w) for w in w3] + [gspec(j) for j in range(3)],
        out_specs=pl.BlockSpec((tm, D), lambda i: (i, 0)),
        out_shape=jax.ShapeDtypeStruct((S, D), BF16),
        compiler_params=_params(("parallel",)),
    )(*o3, *w3, proj, proj, proj)


def _gate_bwd(o3, w3, proj, dmerged, D, *, name, tm=256):
    S = proj.shape[0]
    tm = _tile(S, tm)

    def body(o0, o1, o2, w0, w1, w2, g0, g1, g2, dm_ref, dg_ref, db0, db1, db2, do0, do1, do2):
        dm = dm_ref[...]
        for j, (o_ref, w_ref, g_ref, db_ref, do_ref) in enumerate(
                ((o0, w0, g0, db0, do0), (o1, w1, g1, db1, do1), (o2, w2, g2, db2, do2))):
            s = _sigmoid(g_ref[...])
            br = _dot(o_ref[...], w_ref[...])
            dg_ref[:, j * D:(j + 1) * D] = (dm * br * s * (1.0 - s)).astype(dg_ref.dtype)
            dbr = (dm * s).astype(BF16)
            db_ref[...] = dbr
            do_ref[...] = _dot(dbr, w_ref[...], _NT)

    ospec = lambda d: pl.BlockSpec((tm, d), lambda i: (i, 0))
    wspec = lambda w: pl.BlockSpec(w.shape, lambda i: (0, 0))
    gspec = lambda j: pl.BlockSpec((tm, D), lambda i: (i, j))
    dspec = pl.BlockSpec((tm, D), lambda i: (i, 0))
    return pl.pallas_call(
        body, name=name, grid=(S // tm,),
        in_specs=[ospec(o.shape[1]) for o in o3] + [wspec(w) for w in w3] + [gspec(j) for j in range(3)] + [dspec],
        out_specs=[pl.BlockSpec((tm, 3 * D), lambda i: (i, 0))] + [dspec] * 3 + [ospec(o.shape[1]) for o in o3],
        out_shape=[jax.ShapeDtypeStruct((S, proj.shape[1]), BF16)] + [jax.ShapeDtypeStruct((S, D), BF16)] * 3
        + [jax.ShapeDtypeStruct((S, o.shape[1]), F32) for o in o3],
        compiler_params=_params(("parallel",)),
    )(*o3, *w3, proj, proj, proj, dmerged)


def _loss_sum(dy, D, *, name):
    R, C = dy.shape
    tr = _row_tile(R, C, 2)

    def body(dy_ref, out_ref):
        @pl.when(pl.program_id(0) == 0)
        def _():
            out_ref[...] = jnp.zeros_like(out_ref)

        v = dy_ref[...]
        out_ref[...] += (0.5 * D) * jnp.sum(v * v)

    return pl.pallas_call(
        body, name=name, grid=(R // tr,),
        in_specs=[pl.BlockSpec((tr, C), lambda i: (i, 0))],
        out_specs=pl.BlockSpec((8, 128), lambda i: (0, 0)),
        out_shape=jax.ShapeDtypeStruct((8, 128), F32),
        compiler_params=_params(("arbitrary",)),
    )(dy)[0, 0]


def _pair_sum(stacked, got, core, *, name):
    n, _, r, c = stacked.shape
    tr = _row_tile(r, c, 3)

    def body(core_ref, a_ref, b_ref, o_ref):
        o_ref[...] = (a_ref[0].astype(F32) + b_ref[...].astype(F32)).astype(o_ref.dtype)

    spec = pl.BlockSpec((1, tr, c), lambda s, i, core_ref: (s, i, 0))
    return pl.pallas_call(
        body, name=name,
        grid_spec=pltpu.PrefetchScalarGridSpec(
            num_scalar_prefetch=1, grid=(n, r // tr),
            in_specs=[pl.BlockSpec((1, 1, tr, c), lambda s, i, core_ref: (s, core_ref[0], i, 0)), spec],
            out_specs=spec),
        out_shape=jax.ShapeDtypeStruct((n, r, c), BF16),
        compiler_params=_params(("parallel", "parallel")),
    )(core.astype(jnp.int32).reshape(1), stacked, got)


def _chip_sum(parts, got, chip, *, name):
    _, r, c = parts.shape
    tr = _row_tile(r, c, 6)

    def body(chip_ref, p_ref, q0_ref, q1_ref, q2_ref, o_ref):
        o_ref[...] = ((p_ref[0].astype(F32) + q0_ref[0].astype(F32)) + q1_ref[0].astype(F32)) + q2_ref[0].astype(F32)

    from_chip = lambda j: pl.BlockSpec((1, tr, c), lambda i, chip_ref: (j, i, 0))
    return pl.pallas_call(
        body, name=name,
        grid_spec=pltpu.PrefetchScalarGridSpec(
            num_scalar_prefetch=1, grid=(r // tr,),
            in_specs=[pl.BlockSpec((1, tr, c), lambda i, chip_ref: (chip_ref[0], i, 0))] + [from_chip(j) for j in range(3)],
            out_specs=pl.BlockSpec((tr, c), lambda i, chip_ref: (i, 0))),
        out_shape=jax.ShapeDtypeStruct((r, c), F32),
        compiler_params=_params(("parallel",)),
    )(chip.astype(jnp.int32).reshape(1), parts, got, got, got)


def _adamw_math(w, g, m, v):
    m2 = ADAM_B1 * m + (1.0 - ADAM_B1) * g
    v2 = ADAM_B2 * v + (1.0 - ADAM_B2) * (g * g)
    m_hat = m2 / (1.0 - ADAM_B1 ** ADAM_STEP)
    v_hat = v2 / (1.0 - ADAM_B2 ** ADAM_STEP)
    delta = -ADAM_LR * (m_hat / (jnp.sqrt(v_hat) + ADAM_EPS) + ADAM_WD * w)
    return delta, m2, v2


def _adamw(w, g, m, v, *, name):
    return _ew(_adamw_math, [w, g, m, v], (F32, F32, F32), name=name)


def _adamw_small(w, parts, m, v, *, name):
    n = parts.shape[0]

    def body(w_ref, p_ref, m_ref, v_ref, g_ref, d_ref, m2_ref, v2_ref):
        g = p_ref[0]
        for i in range(1, n):
            g = g + p_ref[i]
        g_ref[...] = g
        d_ref[...], m2_ref[...], v2_ref[...] = _adamw_math(w_ref[...], g, m_ref[...], v_ref[...])

    shp = jax.ShapeDtypeStruct(w.shape, F32)
    return pl.pallas_call(body, name=name, out_shape=[shp] * 4)(w, parts, m, v)


_ANY = pl.BlockSpec(memory_space=pl.ANY)


def _mesh_place():
    x, y, c = lax.axis_index("x"), lax.axis_index("y"), lax.axis_index("c")
    chips = [(1 - x, y), (x, 1 - y), (1 - x, 1 - y)]
    return x, y, c, chips


def _remote(src, dst, sems, i, to):
    send_sems, recv_sems = sems
    return pltpu.make_async_remote_copy(src_ref=src, dst_ref=dst, send_sem=send_sems.at[i], recv_sem=recv_sems.at[i],
                                        device_id=to, device_id_type=MESH_ID)


def _gather_weights(shards, *, name):
    n = len(shards)

    def body(*refs):
        ins, outs = refs[:n], refs[n:2 * n]
        sems = refs[2 * n:2 * n + 2]
        x, y, c, chips = _mesh_place()
        me = 2 * x + y
        sibling = (x, y, 1 - c)
        sent = []
        for w in range(n):
            for j, chip in enumerate(chips):
                cp = _remote(ins[w].at[c], outs[w].at[me, c], sems, 6 * w + j, (chip[0], chip[1], c))
                cp.start()
                sent.append(cp)
        for w in range(n):
            for j, chip in enumerate(chips):
                got = outs[w].at[2 * chip[0] + chip[1], c]
                _remote(got, got, sems, 6 * w + j, sibling).wait_recv()
                cp = _remote(got, got, sems, 6 * w + 3 + j, sibling)
                cp.start()
                sent.append(cp)
        for w in range(n):
            for j, chip in enumerate(chips):
                got = outs[w].at[2 * chip[0] + chip[1], 1 - c]
                _remote(got, got, sems, 6 * w + 3 + j, sibling).wait_recv()
        for cp in sent:
            cp.wait_send()

    outs = pl.pallas_call(
        body, name=name,
        in_specs=[_ANY] * n, out_specs=[_ANY] * n,
        out_shape=[jax.ShapeDtypeStruct((N_CHIPS,) + s.shape, s.dtype) for s in shards],
        scratch_shapes=[pltpu.SemaphoreType.DMA((6 * n,)), pltpu.SemaphoreType.DMA((6 * n,))],
    )(*shards)
    me = 2 * lax.axis_index("x") + lax.axis_index("y")
    return [lax.dynamic_update_index_in_dim(o, s, me, 0) for o, s in zip(outs, shards)]


def _forward_halves(gathered, *, name):
    n = len(gathered)

    def body(*refs):
        ins, outs = refs[:n], refs[n:2 * n]
        sems = refs[2 * n:2 * n + 2]
        x, y, c, chips = _mesh_place()
        sibling = (x, y, 1 - c)
        sent = []
        for w in range(n):
            for j, chip in enumerate(chips):
                peer = 2 * chip[0] + chip[1]
                cp = _remote(ins[w].at[peer, c], outs[w].at[peer, c], sems, 3 * w + j, sibling)
                cp.start()
                sent.append(cp)
        for w in range(n):
            for j, chip in enumerate(chips):
                land = outs[w].at[2 * chip[0] + chip[1], 1 - c]
                _remote(land, land, sems, 3 * w + j, sibling).wait_recv()
        for cp in sent:
            cp.wait_send()

    return pl.pallas_call(
        body, name=name,
        in_specs=[_ANY] * n, out_specs=[_ANY] * n,
        out_shape=[jax.ShapeDtypeStruct(g.shape, g.dtype) for g in gathered],
        input_output_aliases={w: w for w in range(n)},
        scratch_shapes=[pltpu.SemaphoreType.DMA((3 * n,)), pltpu.SemaphoreType.DMA((3 * n,))],
    )(*gathered)


def _exchange_siblings(grads, *, name):
    n = len(grads)

    def body(*refs):
        ins, got = refs[:n], refs[n:2 * n]
        sems = refs[2 * n:2 * n + 2]
        x, y, c, _ = _mesh_place()
        sibling = (x, y, 1 - c)
        sent = []
        for w in range(n):
            for s in range(N_CHIPS):
                cp = _remote(ins[w].at[s, 1 - c], got[w].at[s], sems, N_CHIPS * w + s, sibling)
                cp.start()
                sent.append(cp)
        for w in range(n):
            for s in range(N_CHIPS):
                _remote(got[w].at[s], got[w].at[s], sems, N_CHIPS * w + s, sibling).wait_recv()
        for cp in sent:
            cp.wait_send()

    n_sem = N_CHIPS * n
    return pl.pallas_call(
        body, name=name,
        in_specs=[_ANY] * n, out_specs=[_ANY] * n,
        out_shape=[jax.ShapeDtypeStruct((N_CHIPS,) + g.shape[2:], g.dtype) for g in grads],
        scratch_shapes=[pltpu.SemaphoreType.DMA((n_sem,)), pltpu.SemaphoreType.DMA((n_sem,))],
    )(*grads)


def _share_halves(halves, small):
    n = len(halves)

    def body(*refs):
        ins, small_ref = refs[:n], refs[n]
        outs, small_out = refs[n + 1:2 * n + 1], refs[2 * n + 1]
        sems = refs[2 * n + 2:2 * n + 4]
        x, y, c, chips = _mesh_place()
        sibling = (x, y, 1 - c)
        me = 4 * x + 2 * y + c
        sent = [_remote(ins[w], outs[w].at[c], sems, w, sibling) for w in range(n)]
        peers = [sibling] + [(ch[0], ch[1], cc) for ch in chips for cc in (c, 1 - c)]
        sent += [_remote(small_ref, small_out.at[me], sems, n + j, peer) for j, peer in enumerate(peers)]
        for cp in sent:
            cp.start()
        for w in range(n):
            _remote(outs[w].at[1 - c], outs[w].at[1 - c], sems, w, sibling).wait_recv()
        for j, peer in enumerate(peers):
            frm = small_out.at[4 * peer[0] + 2 * peer[1] + peer[2]]
            _remote(frm, frm, sems, n + j, peer).wait_recv()
        for cp in sent:
            cp.wait_send()

    n_sem = n + 7
    outs = pl.pallas_call(
        body, name="share_halves",
        in_specs=[_ANY] * (n + 1), out_specs=[_ANY] * (n + 1),
        out_shape=[jax.ShapeDtypeStruct((2,) + h.shape, h.dtype) for h in halves]
        + [jax.ShapeDtypeStruct((8,) + small.shape, small.dtype)],
        scratch_shapes=[pltpu.SemaphoreType.DMA((n_sem,)), pltpu.SemaphoreType.DMA((n_sem,))],
    )(*halves, small)
    c = lax.axis_index("c")
    me = 4 * lax.axis_index("x") + 2 * lax.axis_index("y") + c
    return ([lax.dynamic_update_index_in_dim(o, h, c, 0) for o, h in zip(outs[:n], halves)],
            lax.dynamic_update_index_in_dim(outs[n], small, me, 0))


EARLY = ("w_ff_down", "w_ff_up", "w_out", "w_branch_sb", "w_branch_fox", "w_branch_mem", "w_mem_kv")


def _split(outs, n):
    outs = list(outs) if isinstance(outs, (list, tuple)) else [outs]
    return outs[:n], outs[n:]


def _local_step(x, mem, target, small, W, gather_rest=None, reduce_early=None, reduce_late=None):
    S, D = x.shape
    o_qkv, o_mq, o_f = 3 * D, 3 * D + 2 * 3 * D_SB, 3 * D + 2 * 3 * D_SB + D_MEM
    tq = 512

    g_comms, finish_weights = gather_rest if gather_rest is not None else ([None] * 3, None)
    h = _rmsnorm_fwd(x, small["g_mix_norm"], BF16, name="mix_norm")
    (proj,), landed = _split(_mm(h, W["w_in"], name="in_proj", tb=True, tn=768, comm=g_comms[0]), 1)
    blk = lambda j: (proj, (o_qkv + j * D_SB) // LANES)
    sb_q, sb_k, sb_v, fx_q, fx_k, fx_v = [blk(j) for j in range(6)]
    m_q = (proj, o_mq // LANES)
    f_logit_t = _mm(W["w_in"][o_f:o_f + ROW_TILE], h, name="forget_logits", tb=True)[:FOX_HEADS]
    b_col = small["b_forget"].reshape(FOX_HEADS, 1)
    lanes = lambda g: jnp.tile(g, (1, LANES // g.shape[1]))
    g_fq, g_fk, g_mq, g_mk = [lanes(small[k]) for k in ("g_fox_q", "g_fox_k", "g_mem_q", "g_mem_k")]

    (o_sb, sb_tot), more = _split(_sbl_fwd(sb_q, sb_k, sb_v, width=D_SB, hd=HD, name="sb_fwd", tq=tq, comm=g_comms[1]), 2)
    landed += more

    fq = _hnorm_fwd(fx_q, g_fq, width=D_FOX, hd=HD, name="fox_q_norm")
    fk = _hnorm_fwd(fx_k, g_fk, width=D_FOX, hd=HD, name="fox_k_norm")
    f_cum = _forget_fwd(f_logit_t, b_col, name="forget_fwd")
    tkf = _tile(S, tq)
    f_bias = (f_cum.reshape(FOX_HEADS, S, 1), f_cum.reshape(FOX_HEADS, S // tkf, 1, tkf))
    (o_fox, fox_lse), more = _split(_sml_fwd((fq, 0), (fk, 0), fx_v, f_bias, width=D_FOX, hd=HD, causal=True,
                                             name="fox_fwd", tq=tq, tk=tq, comm=g_comms[2]), 2)
    landed += more
    if finish_weights is not None:
        W = {**W, **finish_weights(landed)}

    mh = _rmsnorm_fwd(mem, small["g_mem_norm"], BF16, name="mem_norm")
    mkv = _mm(mh, W["w_mem_kv"], name="mem_kv")
    mv = (mkv, D_MEM // LANES)
    mq = _hnorm_fwd(m_q, g_mq, width=D_MEM, hd=MEM_HD, name="mem_q_norm")
    mk = _hnorm_fwd((mkv, 0), g_mk, width=D_MEM, hd=MEM_HD, name="mem_k_norm")
    o_mem, mem_lse = _sml_fwd((mq, 0), (mk, 0), mv, width=D_MEM, hd=MEM_HD, causal=False, name="mem_fwd", tq=tq, tk=256)

    o3 = [o_sb, o_fox, o_mem]
    w3 = [W["w_branch_sb"], W["w_branch_fox"], W["w_branch_mem"]]
    merged = _gate_fwd(o3, w3, proj, D, name="gate_fwd")
    x1 = _mm(merged, W["w_out"], name="out_proj", extras=(x,), epilogue=lambda acc, res: (res + acc,))
    h2 = _rmsnorm_fwd(x1, small["g_mlp_norm"], BF16, name="mlp_norm")

    def relu2(acc):
        u = jnp.maximum(acc, 0.0)
        return u, u * u

    u, a = _mm(h2, W["w_ff_up"], name="ff_up", out_dtypes=(BF16, BF16), epilogue=relu2)
    def head(acc, res, tgt):
        d = (res + acc - tgt) * (1.0 / D)
        return d, d

    dy, dy16 = _mm(a, W["w_ff_down"], name="ff_down", extras=(x1, target), out_dtypes=(F32, BF16), epilogue=head, tm=512)
    loss = _loss_sum(dy, D, name="loss")

    G = {}
    du = _mm(dy16, W["w_ff_down"], name="d_ff_act", tb=True, out_dtypes=(BF16,), extras=(u,),
             epilogue=lambda acc, uu: (acc * (2.0 * uu.astype(F32)),))
    G["w_ff_down"] = _mm(a, dy16, name="d_w_ff_down", ta=True, out_dtypes=(BF16,))
    G["w_ff_up"] = _mm(h2, du, name="d_w_ff_up", ta=True, out_dtypes=(BF16,))
    dh2 = _mm(du, W["w_ff_up"], name="d_mlp_in", tb=True)
    dx1, dg_mlp = _rmsnorm_bwd(x1, small["g_mlp_norm"], dh2, add=dy, name="d_mlp_norm")
    dmerged = _mm(dx1, W["w_out"], name="d_merged", tb=True)
    G["w_out"] = _mm(merged, dx1, name="d_w_out", ta=True, out_dtypes=(BF16,))
    dgate, db0, db1, db2, do_sb, do_fox, do_mem = _gate_bwd(o3, w3, proj, dmerged, D, name="gate_bwd")
    for nm, o, db in zip(("w_branch_sb", "w_branch_fox", "w_branch_mem"), o3, (db0, db1, db2)):
        G[nm] = _mm(o, db, name="d_" + nm, ta=True, out_dtypes=(BF16,))

    dmq_n, dmk_n, dmv = _sml_bwd((mq, 0), (mk, 0), mv, (o_mem, 0), (mem_lse, 0), (do_mem, 0), width=D_MEM, hd=MEM_HD,
                                 causal=False, name="mem_bwd", tq=tq, tk=256)
    dm_q, dg_mem_q = _hnorm_bwd(m_q, g_mq, dmq_n, width=D_MEM, hd=MEM_HD, name="d_mem_q_norm")
    dmk_raw, dg_mem_k = _hnorm_bwd((mkv, 0), g_mk, dmk_n, width=D_MEM, hd=MEM_HD, name="d_mem_k_norm")
    dmkv = jnp.concatenate([dmk_raw, dmv.astype(BF16)], axis=1)
    G["w_mem_kv"] = _mm(mh, dmkv, name="d_w_mem_kv", ta=True, out_dtypes=(BF16,))
    dmh = _mm(dmkv, W["w_mem_kv"], name="d_mem_h", tb=True)
    _, dg_mem = _rmsnorm_bwd(mem, small["g_mem_norm"], dmh, name="d_mem_norm")

    r_comms, r_finish = reduce_early({k: G.pop(k) for k in EARLY}) if reduce_early is not None else ([None] * 2, None)
    dsb, landed_sb = _split(_sbl_bwd(sb_q, sb_k, sb_v, (do_sb, 0), (sb_tot, 0), width=D_SB, hd=HD, name="sb_bwd", tq=tq,
                                     comm=r_comms[0]), 3)
    (dfq, dfk, dfv, df_row, df_col), landed_fox = _split(
        _sml_bwd((fq, 0), (fk, 0), fx_v, (o_fox, 0), (fox_lse, 0), (do_fox, 0), f_bias, width=D_FOX, hd=HD, causal=True,
                 name="fox_bwd", tq=tq, tk=tq, comm=r_comms[1]), 5)
    early = r_finish(landed_sb, landed_fox) if r_finish is not None else {}
    dfx_q, dg_fox_q = _hnorm_bwd(fx_q, g_fq, dfq, width=D_FOX, hd=HD, name="d_fox_q_norm")
    dfx_k, dg_fox_k = _hnorm_bwd(fx_k, g_fk, dfk, width=D_FOX, hd=HD, name="d_fox_k_norm")
    d_fcum = df_row.reshape(FOX_HEADS, S) - df_col.reshape(FOX_HEADS, S)
    d_flogit_t, db_forget = _forget_bwd(f_logit_t, b_col, d_fcum, name="forget_bwd")
    dg_fox_q, dg_fox_k = dg_fox_q[:, :HD], dg_fox_k[:, :HD]

    rest_cols = jnp.concatenate([t.astype(BF16) for t in (*dsb, dfx_q, dfx_k, dfv, dm_q)]
                                + [d_flogit_t.T.astype(BF16), jnp.zeros((S, F_PAD - FOX_HEADS), BF16)], axis=1)
    dproj = lax.dynamic_update_slice(dgate, rest_cols, (0, 3 * D))
    g_w_in = _mm(dproj, h, name="d_w_in", ta=True, out_dtypes=(BF16,), tm=768)
    if reduce_late is None:
        G["w_in"] = g_w_in
        dh = _mm(dproj, W["w_in"], name="d_mix_in", tk=2304)
    else:
        comm, finish = reduce_late({"w_in": g_w_in})
        dh, *landed = _mm(dproj, W["w_in"], name="d_mix_in", tk=2304, comm=comm)
        early.update(finish(landed))
    grad_x, dg_mix = _rmsnorm_bwd(x, small["g_mix_norm"], dh, add=dx1, name="d_mix_norm")

    small_grads = dict(g_mix_norm=dg_mix, g_mem_norm=dg_mem, b_forget=db_forget.reshape(1, FOX_HEADS),
                       g_fox_q=dg_fox_q, g_fox_k=dg_fox_k, g_mem_q=dg_mem_q, g_mem_k=dg_mem_k, g_mlp_norm=dg_mlp)
    return loss, grad_x, G, small_grads, early


BIG = ("w_in", "w_mem_kv", "w_branch_sb", "w_branch_fox", "w_branch_mem", "w_out", "w_ff_up", "w_ff_down")
COLUMN_SHARDED = ("w_in", "w_branch_sb", "w_branch_fox", "w_branch_mem", "w_ff_up")
SMALL = ("g_mix_norm", "g_mem_norm", "b_forget", "g_fox_q", "g_fox_k", "g_mem_q", "g_mem_k", "g_mlp_norm")
ORDER = ("g_mix_norm", "g_mem_norm", "w_in", "b_forget", "g_fox_q", "g_fox_k", "g_mem_q", "g_mem_k", "w_mem_kv",
         "w_branch_sb", "w_branch_fox", "w_branch_mem", "w_out", "g_mlp_norm", "w_ff_up", "w_ff_down")


def _unshard(name, gathered):
    n, _, rh, c = gathered.shape
    t = gathered.reshape(n, 2 * rh, c)
    if name in COLUMN_SHARDED:
        return t.transpose(1, 0, 2).reshape(2 * rh, n * c)
    return t.reshape(n * 2 * rh, c)


def _reshard(name, full):
    if name in COLUMN_SHARDED:
        r, c = full.shape
        t = full.reshape(r, N_CHIPS, c // N_CHIPS).transpose(1, 0, 2)
    else:
        r, c = full.shape[0] // N_CHIPS, full.shape[1]
        t = full.reshape(N_CHIPS, r, c)
    return t.reshape(N_CHIPS, 2, t.shape[1] // 2, t.shape[2])


ROW_TILE = 16
IN_BUF_ALIGN = 256


def _in_segments(D):
    n_qkv = 6 * D_SB
    o_mq, o_gate = n_qkv + FOX_HEADS, n_qkv + FOX_HEADS + D_MEM
    return [(0, n_qkv, 3 * D), (n_qkv, o_mq, 3 * D + n_qkv + D_MEM), (o_mq, o_gate, 3 * D + n_qkv), (o_gate, o_gate + 3 * D, 0)]


class _InLayout:
    def __init__(self, D, shard, n):
        self.D, self.shard, self.n = D, shard, n
        down = lambda v: v // ROW_TILE * ROW_TILE
        up = lambda v: -(-v // ROW_TILE) * ROW_TILE
        self.pieces = []
        ends = []
        for s in range(n):
            cursor, mine = 0, []
            for a, b, p in _in_segments(D):
                x0, x1 = max(a, s * shard), min(b, (s + 1) * shard)
                if x0 < x1:
                    p0 = p + x0 - a
                    rows = up(p0 + x1 - x0) - down(p0)
                    mine.append((x0 - s * shard, x1 - x0, p0, cursor, rows))
                    cursor += rows
            self.pieces.append(mine)
            ends.append(cursor)
        self.rows = -(-max(ends) // IN_BUF_ALIGN) * IN_BUF_ALIGN
        self.padded_rows = 3 * D + 6 * D_SB + D_MEM + F_PAD

    def _per_shard(self, fn, chip, operand):
        return lax.switch(chip, [functools.partial(fn, s) for s in range(self.n)], operand)

    def pack(self, chip, rows):
        def one(s, t):
            out, at = [], 0
            for x0, n_rows, p0, start, region in self.pieces[s]:
                lead = p0 % ROW_TILE
                out += [jnp.zeros((start + lead - at, t.shape[1]), t.dtype), t[x0:x0 + n_rows]]
                at = start + lead + n_rows
            return jnp.concatenate(out + [jnp.zeros((self.rows - at, t.shape[1]), t.dtype)], axis=0)
        return self._per_shard(one, chip, rows)

    def unpack(self, chip, buf, pad_to):
        def one(s, t):
            out = [t[start + p0 % ROW_TILE:start + p0 % ROW_TILE + n_rows] for _, n_rows, p0, start, _ in self.pieces[s]]
            return jnp.concatenate(out + [jnp.zeros((pad_to - self.shard, t.shape[1]), t.dtype)], axis=0)
        return self._per_shard(one, chip, buf)

    def to_padded(self, bufs):
        runs = sorted((p0, s, start, region) for s in range(self.n) for _, _, p0, start, region in self.pieces[s])
        chunks, end = [], 0
        for p0, s, start, region in runs:
            d0 = p0 // ROW_TILE * ROW_TILE
            src = bufs[s, start:start + region]
            if d0 < end:
                assert end - d0 == ROW_TILE
                last = chunks.pop()
                chunks += [last[:-ROW_TILE], last[-ROW_TILE:] + src[:ROW_TILE], src[ROW_TILE:]]
            else:
                if d0 > end:
                    chunks.append(jnp.zeros((d0 - end, bufs.shape[2]), bufs.dtype))
                chunks.append(src)
            end = d0 + region
        chunks.append(jnp.zeros((self.padded_rows - end, bufs.shape[2]), bufs.dtype))
        return jnp.concatenate(chunks, axis=0)

    def from_padded(self, gp):
        bufs = []
        for s in range(self.n):
            out, at = [], 0
            for _, n_rows, p0, start, region in self.pieces[s]:
                d0 = p0 // ROW_TILE * ROW_TILE
                row = d0 + lax.broadcasted_iota(jnp.int32, (region, 1), 0)
                out.append(jnp.where((row >= p0) & (row < p0 + n_rows), gp[d0:d0 + region], jnp.zeros((), gp.dtype)))
                at = start + region
            bufs.append(jnp.concatenate(out + [jnp.zeros((self.rows - at, gp.shape[1]), gp.dtype)], axis=0))
        return jnp.stack(bufs)


SMALL_ROWS = 16


def _pack_small(vals, scalar=None):
    width = max(vals[k].shape[1] for k in SMALL)
    rows = [jnp.pad(vals[k].astype(F32), ((0, 0), (0, width - vals[k].shape[1]))) for k in SMALL]
    extra = jnp.zeros((SMALL_ROWS - len(SMALL), width), F32)
    if scalar is not None:
        extra = extra.at[0, 0].set(scalar)
    return jnp.concatenate(rows + [extra], axis=0)


def _unpack_small(packed, like):
    return {k: packed[i:i + 1, :like[k].shape[1]] for i, k in enumerate(SMALL)}


def kernel(x, mem, g_mix_norm, g_mem_norm, w_in, b_forget, g_fox_q, g_fox_k, g_mem_q, g_mem_k, w_mem_kv, w_branch_sb, w_branch_fox, w_branch_mem, w_out, g_mlp_norm, w_ff_up, w_ff_down, loss_target, m_g_mix_norm, m_g_mem_norm, m_w_in, m_b_forget, m_g_fox_q, m_g_fox_k, m_g_mem_q, m_g_mem_k, m_w_mem_kv, m_w_branch_sb, m_w_branch_fox, m_w_branch_mem, m_w_out, m_g_mlp_norm, m_w_ff_up, m_w_ff_down, v_g_mix_norm, v_g_mem_norm, v_w_in, v_b_forget, v_g_fox_q, v_g_fox_k, v_g_mem_q, v_g_mem_k, v_w_mem_kv, v_w_branch_sb, v_w_branch_fox, v_w_branch_mem, v_w_out, v_g_mlp_norm, v_w_ff_up, v_w_ff_down):
    given = dict(locals())
    D = x.shape[-1]
    weights = {k: given[k] for k in ORDER}
    moms = {k: given["m_" + k] for k in ORDER}
    vars_ = {k: given["v_" + k] for k in ORDER}

    me_chip = 2 * lax.axis_index("x") + lax.axis_index("y")

    n_in = w_in.shape[2]
    lay = _InLayout(D, n_in, N_CHIPS)
    transposed = lambda t: jnp.transpose(t[0])
    shards = {}
    for k in BIG:
        w = weights[k][0].astype(BF16)
        if k == "w_in":
            w = lay.pack(me_chip, jnp.transpose(w))
        shards[k] = w.reshape(2, w.shape[0] // 2, w.shape[1])
    gathered_in = _gather_weights([shards["w_in"]], name="gather_w_in")[0]
    W = {"w_in": lay.to_padded(gathered_in.reshape(N_CHIPS, lay.rows, D))}
    carried = (("w_branch_sb", "w_branch_fox", "w_branch_mem", "w_out"), ("w_ff_up", "w_mem_kv"), ("w_ff_down",))
    rest = [k for grp in carried for k in grp]
    assert sorted(rest + ["w_in"]) == sorted(BIG)

    def finish_weights(landed):
        full = _forward_halves(landed, name="forward_halves")
        full = [lax.dynamic_update_index_in_dim(o, shards[k], me_chip, 0) for k, o in zip(rest, full)]
        return {k: _unshard(k, g) for k, g in zip(rest, full)}

    me_core = lax.axis_index("c")

    def sum_chips(names, parts, got):
        return {k: _chip_sum(p, q, me_chip, name="sum_chips_" + k) for k, p, q in zip(names, parts, got)}

    def pair_sums(grads, tag):
        names = list(grads)
        stacked = {k: _reshard(k, grads[k]) for k in names if k != "w_in"}
        if "w_in" in grads:
            stacked["w_in"] = lay.from_padded(grads["w_in"]).reshape(N_CHIPS, 2, lay.rows // 2, D)
        got = _exchange_siblings([stacked[k] for k in names], name="exchange_siblings_" + tag)
        return {k: _pair_sum(stacked[k], q, me_core, name="sum_pair_" + k) for k, q in zip(names, got)}

    def reduce_early(grads):
        parts = pair_sums(grads, "early")
        groups = [[k for k in parts if k in ("w_ff_down", "w_ff_up")], [k for k in parts if k not in ("w_ff_down", "w_ff_up")]]
        comms = [_ChipExchange("scatter", [parts[k] for k in grp]) for grp in groups]

        def finish(*landed):
            out = {}
            for grp, got in zip(groups, landed):
                out.update(sum_chips(grp, [parts[k] for k in grp], got))
            return out
        return comms, finish

    def reduce_late(grads):
        parts = pair_sums(grads, "late")
        names = list(parts)
        return _ChipExchange("scatter", [parts[k] for k in names]), functools.partial(sum_chips, names, [parts[k] for k in names])

    small = {k: weights[k] for k in SMALL}
    loss_part, grad_x, G, small_grads, halves = _local_step(
        x[0], mem[0], loss_target[0], small, W,
        gather_rest=([_ChipExchange("gather", [shards[k] for k in grp]) for grp in carried], finish_weights),
        reduce_early=reduce_early, reduce_late=reduce_late)
    assert not G, list(G)
    reduced, small_parts = _share_halves([halves[k] for k in BIG], _pack_small(small_grads, loss_part))

    grads, deltas, new_m, new_v = {}, {}, {}, {}
    for k, g in zip(BIG, reduced):
        shp = weights[k].shape
        if k == "w_in":
            g2 = lay.unpack(me_chip, g.reshape(lay.rows, D), lay.rows)
            padded = lambda t: jnp.pad(transposed(t), ((0, lay.rows - n_in), (0, 0)))
            outs = _adamw(padded(weights[k]), g2, padded(moms[k]), padded(vars_[k]), name="adamw_" + k)
            g2, d, m2, v2 = [jnp.transpose(t[:n_in]) for t in (g2, *outs)]
        else:
            g2 = g.reshape(shp[1], shp[2])
            d, m2, v2 = _adamw(weights[k][0], g2, moms[k][0], vars_[k][0], name="adamw_" + k)
        grads[k], deltas[k], new_m[k], new_v[k] = g2.reshape(shp), d.reshape(shp), m2.reshape(shp), v2.reshape(shp)
    sg, sd, sm, sv = _adamw_small(_pack_small(small), small_parts, _pack_small({k: moms[k] for k in SMALL}),
                                  _pack_small({k: vars_[k] for k in SMALL}), name="adamw_small")
    for dst, packed in ((grads, sg), (deltas, sd), (new_m, sm), (new_v, sv)):
        dst.update(_unpack_small(packed, small))

    loss = sg[len(SMALL), 0]
    return (loss, grad_x[None], *[grads[k] for k in ORDER], *[deltas[k] for k in ORDER],
            *[new_m[k] for k in ORDER], *[new_v[k] for k in ORDER])
```

```python
import functools

import jax
import jax.numpy as jnp
from jax import lax
from jax.experimental import pallas as pl
from jax.experimental.pallas import tpu as pltpu

F32 = jnp.float32
BF16 = jnp.bfloat16
MESH_ID = pl.DeviceIdType.MESH

HD = 64
SB_HEADS = 8
FOX_HEADS = 8
MEM_HEADS = 4
MEM_HD = 128
D_SB = SB_HEADS * HD
D_FOX = FOX_HEADS * HD
D_MEM = MEM_HEADS * MEM_HD
EPS = 1e-6
NEG_INF = -1e30

ADAM_LR = 0.001
ADAM_B1 = 0.9
ADAM_B2 = 0.999
ADAM_EPS = 1e-08
ADAM_WD = 0.01
ADAM_STEP = 10

N_CHIPS = 4
VMEM_LIMIT = 56 * 1024 * 1024

F_PAD = 256


def _tile(n, target, align=128):
    if n <= target:
        return n
    best = None
    t = align
    while t <= target:
        if n % t == 0:
            best = t
        t += align
    assert best is not None, (n, target, align)
    return best


def _params(sem):
    return pltpu.CompilerParams(dimension_semantics=sem, vmem_limit_bytes=VMEM_LIMIT)


def _mm(a, b, *, name, ta=False, tb=False, out_dtypes=(F32,), epilogue=None, extras=(),
        tm=1024, tn=1024, tk=2048, comm=None, col_sums=0):
    if ta:
        K, M = a.shape
    else:
        M, K = a.shape
    if tb:
        N, K2 = b.shape
    else:
        K2, N = b.shape
    assert K == K2, (a.shape, b.shape, ta, tb)
    tm, tn, tk = _tile(M, tm), _tile(N, tn), _tile(K, tk)
    nk = K // tk
    n_extra, n_out = len(extras), len(out_dtypes) + col_sums
    if epilogue is None:
        epilogue = lambda acc: (acc,)
    dims = (((0 if ta else 1,), (1 if tb else 0,)), ((), ()))

    gm, gn = M // tm, N // tn

    def body(*refs):
        i, j, k = pl.program_id(0), pl.program_id(1), pl.program_id(2)
        got = _carry(comm, 2 + n_extra, n_out, (i == 0) & (j == 0) & (k == 0),
                     (i == gm - 1) & (j == gn - 1) & (k == nk - 1), refs)
        (a_ref, b_ref, *extra_refs), out_refs = got[0], got[1]
        part = lax.dot_general(a_ref[...].astype(BF16), b_ref[...].astype(BF16), dims, preferred_element_type=F32)

        def finish(acc):
            outs = epilogue(acc, *[r[...] for r in extra_refs])
            for o_ref, o in zip(out_refs[:len(out_dtypes)], outs):
                o_ref[...] = o.astype(o_ref.dtype)
            for o_ref, o in zip(out_refs[len(out_dtypes):], outs[len(out_dtypes):]):
                first = lax.broadcasted_iota(jnp.int32, o_ref.shape, 0) == 0
                o_ref[...] = jnp.where(first, jnp.broadcast_to(o, o_ref.shape), 0.0)

        if nk == 1:
            finish(part)
        else:
            acc_ref = refs[-1]

            @pl.when(k == 0)
            def _():
                acc_ref[...] = part

            @pl.when((k > 0) & (k < nk - 1))
            def _():
                acc_ref[...] += part

            @pl.when(k == nk - 1)
            def _():
                finish(acc_ref[...] + part)

        got[2]()

    a_spec = pl.BlockSpec((tk, tm), lambda i, j, k: (k, i)) if ta else pl.BlockSpec((tm, tk), lambda i, j, k: (i, k))
    b_spec = pl.BlockSpec((tn, tk), lambda i, j, k: (j, k)) if tb else pl.BlockSpec((tk, tn), lambda i, j, k: (k, j))
    mn_spec = pl.BlockSpec((tm, tn), lambda i, j, k: (i, j))
    row_spec = pl.BlockSpec((1, tn), lambda i, j, k: (0, j))
    sum_spec = pl.BlockSpec((8, tn), lambda i, j, k: (i, j))
    c_ins, c_in_specs, c_out_specs, c_out_shape, c_scratch = _comm_args(comm)
    sem = ("parallel", "parallel", "arbitrary") if comm is None else ("arbitrary",) * 3
    outs = pl.pallas_call(
        body, name=name,
        grid=(gm, gn, nk),
        in_specs=[a_spec, b_spec] + [row_spec if e.shape[0] == 1 else mn_spec for e in extras] + c_in_specs,
        out_specs=[mn_spec] * len(out_dtypes) + [sum_spec] * col_sums + c_out_specs,
        out_shape=[jax.ShapeDtypeStruct((M, N), dt) for dt in out_dtypes]
        + [jax.ShapeDtypeStruct((8 * gm, N), F32)] * col_sums + c_out_shape,
        scratch_shapes=c_scratch + ([pltpu.VMEM((tm, tn), F32)] if nk > 1 else []),
        compiler_params=_params(sem),
    )(a, b, *extras, *c_ins)
    return outs if len(outs) > 1 else outs[0]


def _row_tile(rows, cols, n_arrays):
    budget = 10 * 1024 * 1024
    cols_padded = -(-cols // 128) * 128
    target = max(16, budget // (cols_padded * 4 * n_arrays * 2))
    return _tile(rows, target, align=16)


def _ew(fn, ins, out_dtypes, *, name):
    R, C = ins[0].shape
    n_in, n_out = len(ins), len(out_dtypes)
    tr = _row_tile(R, C, n_in + n_out)

    def body(*refs):
        outs = fn(*[r[...] for r in refs[:n_in]])
        for o_ref, o in zip(refs[n_in:], outs):
            o_ref[...] = o.astype(o_ref.dtype)

    spec = pl.BlockSpec((tr, C), lambda i: (i, 0))
    outs = pl.pallas_call(
        body, name=name, grid=(R // tr,),
        in_specs=[spec] * n_in, out_specs=[spec] * n_out,
        out_shape=[jax.ShapeDtypeStruct((R, C), dt) for dt in out_dtypes],
        compiler_params=_params(("parallel",)),
    )(*ins)
    return outs if n_out > 1 else outs[0]


def _rmsnorm_fwd(x, g, out_dtype, *, name):
    R, d = x.shape
    tr = _row_tile(R, d, 3)

    def body(x_ref, g_ref, o_ref):
        xv = x_ref[...].astype(F32)
        r = lax.rsqrt(jnp.mean(xv * xv, axis=-1, keepdims=True) + EPS)
        o_ref[...] = (xv * r * g_ref[...]).astype(o_ref.dtype)

    return pl.pallas_call(
        body, name=name, grid=(R // tr,),
        in_specs=[pl.BlockSpec((tr, d), lambda i: (i, 0)), pl.BlockSpec((1, d), lambda i: (0, 0))],
        out_specs=pl.BlockSpec((tr, d), lambda i: (i, 0)),
        out_shape=jax.ShapeDtypeStruct((R, d), out_dtype),
        compiler_params=_params(("parallel",)),
    )(x, g)


def _rmsnorm_bwd(x, g, dy, add=None, *, name):
    R, d = x.shape
    has_add = add is not None
    tr = _row_tile(R, d, 5)

    def body(*refs):
        x_ref, g_ref, dy_ref = refs[:3]
        add_ref = refs[3] if has_add else None
        dx_ref, dg_ref = refs[-2:]
        xv = x_ref[...].astype(F32)
        dyv = dy_ref[...].astype(F32)
        r = lax.rsqrt(jnp.mean(xv * xv, axis=-1, keepdims=True) + EPS)
        xh = xv * r
        dyg = dyv * g_ref[...]
        c = jnp.mean(dyg * xh, axis=-1, keepdims=True)
        dx = r * (dyg - xh * c)
        if has_add:
            dx = dx + add_ref[...]
        dx_ref[...] = dx

        @pl.when(pl.program_id(0) == 0)
        def _():
            dg_ref[...] = jnp.zeros_like(dg_ref)

        dg_ref[...] += jnp.sum(dyv * xh, axis=0, keepdims=True)

    row = pl.BlockSpec((tr, d), lambda i: (i, 0))
    vec = pl.BlockSpec((1, d), lambda i: (0, 0))
    ins = [x, g, dy] + ([add] if has_add else [])
    return pl.pallas_call(
        body, name=name, grid=(R // tr,),
        in_specs=[row, vec, row] + ([row] if has_add else []),
        out_specs=[row, vec],
        out_shape=[jax.ShapeDtypeStruct((R, d), F32), jax.ShapeDtypeStruct((1, d), F32)],
        compiler_params=_params(("arbitrary",)),
    )(*ins)


_NT = (((1,), (1,)), ((), ()))
_TN = (((0,), (0,)), ((), ()))


def _dot(a, b, dims=(((1,), (0,)), ((), ()))):
    return lax.dot_general(a, b, dims, preferred_element_type=F32)


def _log_sigmoid_pair(z):
    sp = jnp.log(1.0 + jnp.exp(-jnp.abs(z)))
    return jnp.minimum(z, 0.0) - sp, jnp.minimum(-z, 0.0) - sp


LANES = 128
_LOW = -3e38


def _lane_masks(hd, rows):
    if hd == LANES:
        return [None]
    lane = lax.broadcasted_iota(jnp.int32, (rows, LANES), 1)
    return [(lane >= hh * hd) & (lane < (hh + 1) * hd) for hh in range(LANES // hd)]


def _keep(t, m):
    return t if m is None else jnp.where(m, t, 0.0)


def _merge(parts, masks):
    out = parts[-1]
    for p, m in zip(parts[-2::-1], masks[-2::-1]):
        out = jnp.where(m, p, out)
    return out


def _row_value(t, m):
    return jnp.max(t if m is None else jnp.where(m, t, _LOW), axis=1, keepdims=True)


def _cols(tq, off):
    return pl.BlockSpec((tq, LANES), lambda g, i: (i, off + g))


def _cols_all(rows, off):
    return pl.BlockSpec((rows, LANES), lambda g, i: (0, off + g))


SCAN_BLOCK = 256


def _tri(kind, cols):
    n = min(SCAN_BLOCK, cols)
    r = lax.broadcasted_iota(jnp.int32, (n, n), 0)
    c = lax.broadcasted_iota(jnp.int32, (n, n), 1)
    return ((r > c) if kind == "after" else (r < c)).astype(BF16)


def _scan_cols(x, tri, reverse):
    cols = x.shape[1]
    cb = min(SCAN_BLOCK, cols)
    assert cols % cb == 0 and tri.shape == (cb, cb)
    nb = cols // cb
    blocks = [x[:, b * cb:(b + 1) * cb] for b in range(nb)]
    outs, carry = [None] * nb, None
    for b in (reversed(range(nb)) if reverse else range(nb)):
        y = _dot(blocks[b].astype(BF16), tri)
        outs[b] = y if carry is None else y + carry
        s = jnp.sum(blocks[b], axis=1, keepdims=True)
        carry = s if carry is None else carry + s
    return (outs[0] if nb == 1 else jnp.concatenate(outs, axis=1)), carry


def _softplus_parts(z):
    pos = jnp.maximum(z, 0.0) + jnp.log(1.0 + jnp.exp(-jnp.abs(z)))
    return pos, z - pos


class _ChipExchange:
    def __init__(self, kind, ins):
        assert kind in ("gather", "scatter")
        self.kind, self.ins = kind, list(ins)
        lead = (lambda s: (N_CHIPS,) + s) if kind == "gather" else (lambda s: (3,) + s[1:])
        self.out_shape = [jax.ShapeDtypeStruct(lead(a.shape), a.dtype) for a in ins]
        n = 3 * len(ins)
        self.scratch = [pltpu.SemaphoreType.DMA((n,)), pltpu.SemaphoreType.DMA((n,))]

    def _copies(self, in_refs, out_refs, sems, landing):
        x, y, c, chips = _mesh_place()
        me = 2 * x + y
        out = []
        for w in range(len(self.ins)):
            for j, chip in enumerate(chips):
                peer = 2 * chip[0] + chip[1]
                if self.kind == "gather":
                    src, dst, land = in_refs[w].at[c], out_refs[w].at[me, c], out_refs[w].at[peer, c]
                else:
                    src, dst, land = in_refs[w].at[peer], out_refs[w].at[j], out_refs[w].at[j]
                if landing:
                    src, dst = land, land
                out.append(_remote(src, dst, sems, 3 * w + j, (chip[0], chip[1], c)))
        return out

    def start(self, in_refs, out_refs, sems):
        for cp in self._copies(in_refs, out_refs, sems, False):
            cp.start()

    def finish(self, in_refs, out_refs, sems):
        for cp in self._copies(in_refs, out_refs, sems, True):
            cp.wait_recv()
        for cp in self._copies(in_refs, out_refs, sems, False):
            cp.wait_send()


def _carry(comm, n_in, n_out, first, last, refs):
    if comm is None:
        return refs[:n_in], refs[n_in:n_in + n_out], (lambda: None)
    a, b = len(comm.ins), len(comm.out_shape)
    ins, c_in = refs[:n_in], refs[n_in:n_in + a]
    outs, c_out = refs[n_in + a:n_in + a + n_out], refs[n_in + a + n_out:n_in + a + n_out + b]
    sems = refs[n_in + a + n_out + b:n_in + a + n_out + b + 2]
    pl.when(first)(lambda: comm.start(c_in, c_out, sems))
    return ins, outs, (lambda: pl.when(last)(lambda: comm.finish(c_in, c_out, sems)))


def _sbl_fwd(q, k, v, *, width, hd, name, tq=256, comm=None):
    (qa, qo), (ka, ko), (va, vo) = q, k, v
    S = qa.shape[0]
    tq = _tile(S, tq)
    tk = tq
    scale = hd ** -0.5
    n_g, n_q = width // LANES, S // tq

    def body(*refs):
        qi = pl.program_id(1)
        gi = pl.program_id(0)
        got = _carry(comm, 3, 2, (gi == 0) & (qi == 0), (gi == n_g - 1) & (qi == n_q - 1), refs)
        (q_ref, k_ref, v_ref), (o_ref, tot_ref) = got[0], got[1]
        masks = _lane_masks(hd, tq)
        qs = q_ref[...].astype(F32) * scale
        qm = [_keep(qs, m).astype(BF16) for m in masks]
        strict = lax.broadcasted_iota(jnp.int32, (tq, tk), 1) < lax.broadcasted_iota(jnp.int32, (tq, tk), 0)
        later = _tri("after", tk)

        def tile(kb, carry, diag):
            ks = pl.multiple_of(kb * tk, tk)
            kv = k_ref[pl.ds(ks, tk), :].astype(BF16)
            vv = v_ref[pl.ds(ks, tk), :].astype(BF16)
            out = []
            for hh in range(len(masks)):
                acc, c_pos = carry[2 * hh], carry[2 * hh + 1]
                pos, ls = _softplus_parts(_dot(qm[hh], kv, _NT))
                if diag:
                    pos = jnp.where(strict, pos, 0.0)
                pos_after, pos_all = _scan_cols(pos, later, True)
                w = jnp.exp(ls - (pos_after + c_pos))
                if diag:
                    w = jnp.where(strict, w, 0.0)
                out += [acc + _dot(w.astype(BF16), vv), c_pos + pos_all]
            return tuple(out)

        init = (jnp.zeros((tq, LANES), F32), jnp.zeros((tq, 1), F32)) * len(masks)
        carry = tile(qi, init, True)
        carry = lax.fori_loop(0, qi, lambda i, c: tile(qi - 1 - i, c, False), carry)
        o_ref[...] = _merge(carry[0::2], masks).astype(o_ref.dtype)
        tot_ref[...] = _merge([jnp.broadcast_to(-c, (tq, LANES)) for c in carry[1::2]], masks)
        got[2]()

    c_ins, c_in_specs, c_out_specs, c_out_shape, c_scratch = _comm_args(comm)
    return pl.pallas_call(
        body, name=name, grid=(n_g, n_q),
        in_specs=[_cols(tq, qo), _cols_all(S, ko), _cols_all(S, vo)] + c_in_specs,
        out_specs=[_cols(tq, 0), _cols(tq, 0)] + c_out_specs,
        out_shape=[jax.ShapeDtypeStruct((S, width), BF16), jax.ShapeDtypeStruct((S, width), F32)] + c_out_shape,
        scratch_shapes=c_scratch,
        compiler_params=_params(("arbitrary", "arbitrary")),
    )(qa, ka, va, *c_ins)


def _comm_args(comm):
    if comm is None:
        return [], [], [], [], []
    return comm.ins, [_ANY] * len(comm.ins), [_ANY] * len(comm.out_shape), comm.out_shape, comm.scratch


def _sbl_bwd(q, k, v, do, tot, *, width, hd, name, tq=256, comm=None):
    (qa, qo), (ka, ko), (va, vo) = q, k, v
    S = qa.shape[0]
    tq = _tile(S, tq)
    tk = tq
    scale = hd ** -0.5
    n_g, n_q = width // LANES, S // tq

    def body(*refs):
        qi = pl.program_id(1)
        gi = pl.program_id(0)
        got = _carry(comm, 5, 3, (gi == 0) & (qi == 0), (gi == n_g - 1) & (qi == n_q - 1), refs)
        (q_ref, k_ref, v_ref, do_ref, tot_ref), (dq_ref, dk_ref, dv_ref) = got[0], got[1]

        @pl.when(qi == 0)
        def _():
            dk_ref[...] = jnp.zeros_like(dk_ref)
            dv_ref[...] = jnp.zeros_like(dv_ref)

        masks = _lane_masks(hd, tq)
        qs = q_ref[...].astype(F32) * scale
        qm = [_keep(qs, m).astype(BF16) for m in masks]
        dov = [_keep(do_ref[...], m).astype(BF16) for m in masks]
        rest = [-_row_value(tot_ref[...], m) for m in masks]
        strict = lax.broadcasted_iota(jnp.int32, (tq, tk), 1) < lax.broadcasted_iota(jnp.int32, (tq, tk), 0)
        later, before = _tri("after", tk), _tri("before", tk)

        def tile(kb, carry, diag):
            ks = pl.multiple_of(kb * tk, tk)
            kv = k_ref[pl.ds(ks, tk), :].astype(BF16)
            vv = v_ref[pl.ds(ks, tk), :].astype(BF16)
            out = []
            dk_t, dv_t = None, None
            for hh in range(len(masks)):
                dq, c_pos, c_g = carry[3 * hh:3 * hh + 3]
                pos, ls = _softplus_parts(_dot(qm[hh], kv, _NT))
                if diag:
                    pos = jnp.where(strict, pos, 0.0)
                pos_after, pos_all = _scan_cols(pos, later, True)
                c_pos = c_pos + pos_all
                w = jnp.exp(ls - (pos_after + (rest[hh] - c_pos)))
                if diag:
                    w = jnp.where(strict, w, 0.0)
                g = _dot(dov[hh], vv, _NT) * w
                g_before, g_all = _scan_cols(g, before, False)
                g_before = g_before + c_g
                dz = g - jnp.exp(ls) * (g + g_before)
                if diag:
                    dz = jnp.where(strict, dz, 0.0)
                dzb = dz.astype(BF16)
                dk_h = _dot(dzb, qm[hh], _TN)
                dv_h = _dot(w.astype(BF16), dov[hh], _TN)
                dk_t = dk_h if dk_t is None else dk_t + dk_h
                dv_t = dv_h if dv_t is None else dv_t + dv_h
                out += [dq + _dot(dzb, kv), c_pos, c_g + g_all]
            dk_ref[pl.ds(ks, tk), :] += dk_t
            dv_ref[pl.ds(ks, tk), :] += dv_t
            return tuple(out)

        zero = jnp.zeros((tq, 1), F32)
        init = (jnp.zeros((tq, LANES), F32), zero, zero) * len(masks)
        carry = lax.fori_loop(0, qi, lambda kb, c: tile(kb, c, False), init)
        carry = tile(qi, carry, True)
        dq_ref[...] = _merge(carry[0::3], masks) * scale
        got[2]()

    full = jax.ShapeDtypeStruct((S, width), F32)
    c_ins, c_in_specs, c_out_specs, c_out_shape, c_scratch = _comm_args(comm)
    return pl.pallas_call(
        body, name=name, grid=(n_g, n_q),
        in_specs=[_cols(tq, qo), _cols_all(S, ko), _cols_all(S, vo), _cols(tq, do[1]), _cols(tq, tot[1])] + c_in_specs,
        out_specs=[_cols(tq, 0), _cols_all(S, 0), _cols_all(S, 0)] + c_out_specs,
        out_shape=[full, full, full] + c_out_shape,
        scratch_shapes=c_scratch,
        compiler_params=_params(("arbitrary", "arbitrary")),
    )(qa, ka, va, do[0], tot[0], *c_ins)


def _sml_fwd(q, k, v, bias=None, *, width, hd, causal, name, tq=256, tk=256, comm=None):
    (qa, qo), (ka, ko), (va, vo) = q, k, v
    S, Sk = qa.shape[0], ka.shape[0]
    tq, tk = _tile(S, tq), _tile(Sk, tk)
    if causal:
        assert tq == tk and S == Sk
    nk = Sk // tk
    hpg = LANES // hd
    scale = hd ** -0.5
    has_bias = bias is not None
    n_g, n_q = width // LANES, S // tq

    def body(*all_refs):
        qi, gi = pl.program_id(1), pl.program_id(0)
        got = _carry(comm, 5 if has_bias else 3, 2, (gi == 0) & (qi == 0), (gi == n_g - 1) & (qi == n_q - 1), all_refs)
        refs = tuple(got[0]) + tuple(got[1])
        q_ref, k_ref, v_ref = refs[:3]
        o_ref, lse_ref = refs[-2:]
        masks = _lane_masks(hd, tq)
        qs = q_ref[...].astype(F32) * scale
        qm = [_keep(qs, m).astype(BF16) for m in masks]
        allowed = lax.broadcasted_iota(jnp.int32, (tq, tk), 1) <= lax.broadcasted_iota(jnp.int32, (tq, tk), 0)

        def tile(kb, carry, diag):
            ks = pl.multiple_of(kb * tk, tk)
            kv = k_ref[pl.ds(ks, tk), :].astype(BF16)
            vv = v_ref[pl.ds(ks, tk), :].astype(BF16)
            out = []
            for hh in range(hpg):
                m, l, acc = carry[3 * hh:3 * hh + 3]
                z = _dot(qm[hh], kv, _NT)
                if has_bias:
                    z = z + refs[3][hh] - refs[4][hh, kb]
                if diag:
                    z = jnp.where(allowed, z, NEG_INF)
                m2 = jnp.maximum(m, jnp.max(z, axis=1, keepdims=True))
                p = jnp.exp(z - m2)
                alpha = jnp.exp(m - m2)
                out += [m2, alpha * l + jnp.sum(p, axis=1, keepdims=True), alpha * acc + _dot(p.astype(BF16), vv)]
            return tuple(out)

        init = (jnp.full((tq, 1), NEG_INF, F32), jnp.zeros((tq, 1), F32), jnp.zeros((tq, LANES), F32)) * hpg
        if causal:
            carry = lax.fori_loop(0, qi, lambda kb, c: tile(kb, c, False), init)
            carry = tile(qi, carry, True)
        else:
            carry = lax.fori_loop(0, nk, lambda kb, c: tile(kb, c, False), init)
        o_ref[...] = _merge([acc / l for l, acc in zip(carry[1::3], carry[2::3])], masks).astype(o_ref.dtype)
        lse_ref[...] = _merge([jnp.broadcast_to(m + jnp.log(l), (tq, LANES)) for m, l in zip(carry[0::3], carry[1::3])], masks)
        got[2]()

    in_specs = [_cols(tq, qo), _cols_all(Sk, ko), _cols_all(Sk, vo)]
    ins = [qa, ka, va]
    if has_bias:
        in_specs += [pl.BlockSpec((hpg, tq, 1), lambda g, i: (g, i, 0)),
                     pl.BlockSpec((hpg, nk, 1, tk), lambda g, i: (g, 0, 0, 0))]
        ins += list(bias)
    c_ins, c_in_specs, c_out_specs, c_out_shape, c_scratch = _comm_args(comm)
    return pl.pallas_call(
        body, name=name, grid=(n_g, n_q),
        in_specs=in_specs + c_in_specs, out_specs=[_cols(tq, 0), _cols(tq, 0)] + c_out_specs,
        out_shape=[jax.ShapeDtypeStruct((S, width), BF16), jax.ShapeDtypeStruct((S, width), F32)] + c_out_shape,
        scratch_shapes=c_scratch,
        compiler_params=_params(("arbitrary", "arbitrary") if comm is not None else ("parallel", "arbitrary")),
    )(*ins, *c_ins)


def _sml_bwd(q, k, v, o, lse, do, bias=None, *, width, hd, causal, name, tq=256, tk=256, comm=None):
    (qa, qo), (ka, ko), (va, vo) = q, k, v
    S, Sk = qa.shape[0], ka.shape[0]
    tq, tk = _tile(S, tq), _tile(Sk, tk)
    nk = Sk // tk
    hpg = LANES // hd
    scale = hd ** -0.5
    has_bias = bias is not None
    n_in = 8 if has_bias else 6
    n_g, n_q = width // LANES, S // tq

    def body(*all_refs):
        qi, gi = pl.program_id(1), pl.program_id(0)
        got = _carry(comm, n_in, 5 if has_bias else 3, (gi == 0) & (qi == 0), (gi == n_g - 1) & (qi == n_q - 1), all_refs)
        refs = tuple(got[0]) + tuple(got[1])
        q_ref, k_ref, v_ref, o_ref, lse_ref, do_ref = refs[:6]
        dq_ref, dk_ref, dv_ref = refs[n_in:n_in + 3]

        @pl.when(qi == 0)
        def _():
            dk_ref[...] = jnp.zeros_like(dk_ref)
            dv_ref[...] = jnp.zeros_like(dv_ref)
            if has_bias:
                refs[n_in + 4][...] = jnp.zeros_like(refs[n_in + 4])

        masks = _lane_masks(hd, tq)
        qs = q_ref[...].astype(F32) * scale
        qm = [_keep(qs, m).astype(BF16) for m in masks]
        do32 = do_ref[...]
        dov = [_keep(do32, m).astype(BF16) for m in masks]
        prod = do32 * o_ref[...].astype(F32)
        delta = [jnp.sum(_keep(prod, m), axis=1, keepdims=True) for m in masks]
        lses = [_row_value(lse_ref[...], m) for m in masks]
        allowed = lax.broadcasted_iota(jnp.int32, (tq, tk), 1) <= lax.broadcasted_iota(jnp.int32, (tq, tk), 0)

        def tile(kb, carry, diag):
            ks = pl.multiple_of(kb * tk, tk)
            kv = k_ref[pl.ds(ks, tk), :].astype(BF16)
            vv = v_ref[pl.ds(ks, tk), :].astype(BF16)
            out = []
            dk_t, dv_t = None, None
            for hh in range(hpg):
                dq, db_row = carry[2 * hh:2 * hh + 2]
                z = _dot(qm[hh], kv, _NT)
                if has_bias:
                    z = z + refs[6][hh] - refs[7][hh, kb]
                p = jnp.exp(z - lses[hh])
                if diag:
                    p = jnp.where(allowed, p, 0.0)
                dz = p * (_dot(dov[hh], vv, _NT) - delta[hh])
                dzb = dz.astype(BF16)
                dk_h = _dot(dzb, qm[hh], _TN)
                dv_h = _dot(p.astype(BF16), dov[hh], _TN)
                dk_t = dk_h if dk_t is None else dk_t + dk_h
                dv_t = dv_h if dv_t is None else dv_t + dv_h
                if has_bias:
                    db_row = db_row + jnp.sum(dz, axis=1, keepdims=True)
                    refs[n_in + 4][hh, kb] += jnp.sum(dz, axis=0, keepdims=True)
                out += [dq + _dot(dzb, kv), db_row]
            dk_ref[pl.ds(ks, tk), :] += dk_t
            dv_ref[pl.ds(ks, tk), :] += dv_t
            return tuple(out)

        init = (jnp.zeros((tq, LANES), F32), jnp.zeros((tq, 1), F32)) * hpg
        if causal:
            carry = lax.fori_loop(0, qi, lambda kb, c: tile(kb, c, False), init)
            carry = tile(qi, carry, True)
        else:
            carry = lax.fori_loop(0, nk, lambda kb, c: tile(kb, c, False), init)
        dq_ref[...] = _merge(carry[0::2], masks) * scale
        if has_bias:
            for hh in range(hpg):
                refs[n_in + 3][hh] = carry[2 * hh + 1]
        got[2]()

    in_specs = [_cols(tq, qo), _cols_all(Sk, ko), _cols_all(Sk, vo), _cols(tq, o[1]), _cols(tq, lse[1]), _cols(tq, do[1])]
    ins = [qa, ka, va, o[0], lse[0], do[0]]
    out_specs = [_cols(tq, 0), _cols_all(Sk, 0), _cols_all(Sk, 0)]
    out_shape = [jax.ShapeDtypeStruct((S, width), F32), jax.ShapeDtypeStruct((Sk, width), F32),
                 jax.ShapeDtypeStruct((Sk, width), F32)]
    if has_bias:
        rspec = pl.BlockSpec((hpg, tq, 1), lambda g, i: (g, i, 0))
        cspec = pl.BlockSpec((hpg, nk, 1, tk), lambda g, i: (g, 0, 0, 0))
        in_specs += [rspec, cspec]
        ins += list(bias)
        out_specs += [rspec, cspec]
        n_heads = width // hd
        out_shape += [jax.ShapeDtypeStruct((n_heads, S, 1), F32), jax.ShapeDtypeStruct((n_heads, nk, 1, tk), F32)]
    c_ins, c_in_specs, c_out_specs, c_out_shape, c_scratch = _comm_args(comm)
    return pl.pallas_call(
        body, name=name, grid=(n_g, n_q),
        in_specs=in_specs + c_in_specs, out_specs=out_specs + c_out_specs, out_shape=out_shape + c_out_shape,
        scratch_shapes=c_scratch,
        compiler_params=_params(("arbitrary", "arbitrary") if comm is not None else ("parallel", "arbitrary")),
    )(*ins, *c_ins)


def _head_sums(t, masks):
    sums = [jnp.sum(_keep(t, m), axis=1, keepdims=True) for m in masks]
    return _merge([jnp.broadcast_to(s, t.shape) for s in sums], masks) if len(masks) > 1 else sums[0]


def _hnorm_fwd(x, g_lanes, *, width, hd, name, tr=1024):
    xa, xo = x
    R = xa.shape[0]
    tr = _tile(R, tr, align=16)
    n_blk = width // LANES

    def body(x_ref, g_ref, o_ref):
        masks = _lane_masks(hd, tr)
        for j in range(n_blk):
            sl = slice(j * LANES, (j + 1) * LANES)
            xv = x_ref[:, sl].astype(F32)
            r = lax.rsqrt(_head_sums(xv * xv, masks) * (1.0 / hd) + EPS)
            o_ref[:, sl] = (xv * r * g_ref[...]).astype(o_ref.dtype)

    assert (xo * LANES) % width == 0
    return pl.pallas_call(
        body, name=name, grid=(R // tr,),
        in_specs=[pl.BlockSpec((tr, width), lambda i: (i, xo * LANES // width)), pl.BlockSpec((1, LANES), lambda i: (0, 0))],
        out_specs=pl.BlockSpec((tr, width), lambda i: (i, 0)),
        out_shape=jax.ShapeDtypeStruct((R, width), BF16),
        compiler_params=_params(("parallel",)),
    )(xa, g_lanes)


def _hnorm_bwd(x, g_lanes, dy, *, width, hd, name, tr=1024):
    xa, xo = x
    R = xa.shape[0]
    tr = _tile(R, tr, align=16)
    n_blk = width // LANES

    def body(x_ref, g_ref, dy_ref, dx_ref, dg_ref):
        masks = _lane_masks(hd, tr)
        dg = jnp.zeros((1, LANES), F32)
        for j in range(n_blk):
            sl = slice(j * LANES, (j + 1) * LANES)
            xv = x_ref[:, sl].astype(F32)
            dyv = dy_ref[:, sl].astype(F32)
            r = lax.rsqrt(_head_sums(xv * xv, masks) * (1.0 / hd) + EPS)
            xh = xv * r
            dyg = dyv * g_ref[...]
            c = _head_sums(dyg * xh, masks) * (1.0 / hd)
            dx_ref[:, sl] = (r * (dyg - xh * c)).astype(dx_ref.dtype)
            dg = dg + jnp.sum(dyv * xh, axis=0, keepdims=True)
        if hd * 2 == LANES:
            dg8 = jnp.broadcast_to(dg, (8, LANES))
            dg = (dg8 + pltpu.roll(dg8, shift=hd, axis=1))[0:1]
        else:
            assert hd == LANES

        @pl.when(pl.program_id(0) == 0)
        def _():
            dg_ref[...] = jnp.zeros_like(dg_ref)

        dg_ref[...] += dg

    assert (xo * LANES) % width == 0
    return pl.pallas_call(
        body, name=name, grid=(R // tr,),
        in_specs=[pl.BlockSpec((tr, width), lambda i: (i, xo * LANES // width)), pl.BlockSpec((1, LANES), lambda i: (0, 0)),
                  pl.BlockSpec((tr, width), lambda i: (i, 0))],
        out_specs=[pl.BlockSpec((tr, width), lambda i: (i, 0)), pl.BlockSpec((1, LANES), lambda i: (0, 0))],
        out_shape=[jax.ShapeDtypeStruct((R, width), BF16), jax.ShapeDtypeStruct((1, LANES), F32)],
        compiler_params=_params(("arbitrary",)),
    )(xa, g_lanes, dy)


def _split3_dot(x, tri):
    a = x.astype(BF16)
    r = x - a.astype(F32)
    b = r.astype(BF16)
    c = (r - b.astype(F32)).astype(BF16)
    return _dot(a, tri) + _dot(b, tri) + _dot(c, tri)


def _forget_fwd(logit_t, b_col, *, name, blk=512):
    H, S = logit_t.shape
    blk = _tile(S, blk)

    def body(l_ref, b_ref, f_ref):
        r_i = lax.broadcasted_iota(jnp.int32, (blk, blk), 0)
        c_i = lax.broadcasted_iota(jnp.int32, (blk, blk), 1)
        upto = (r_i <= c_i).astype(BF16)
        carry = jnp.zeros((H, 1), F32)
        for j in range(S // blk):
            u = l_ref[:, j * blk:(j + 1) * blk] + b_ref[...]
            lf, _ = _log_sigmoid_pair(u)
            f_ref[:, j * blk:(j + 1) * blk] = _split3_dot(lf, upto) + carry
            carry = carry + jnp.sum(lf, axis=1, keepdims=True)

    return pl.pallas_call(
        body, name=name,
        out_shape=jax.ShapeDtypeStruct((H, S), F32),
        compiler_params=pltpu.CompilerParams(vmem_limit_bytes=VMEM_LIMIT),
    )(logit_t, b_col)


def _forget_bwd(logit_t, b_col, d_f, *, name, blk=512):
    H, S = logit_t.shape
    blk = _tile(S, blk)

    def body(l_ref, b_ref, df_ref, dl_ref, db_ref):
        r_i = lax.broadcasted_iota(jnp.int32, (blk, blk), 0)
        c_i = lax.broadcasted_iota(jnp.int32, (blk, blk), 1)
        fromon = (r_i >= c_i).astype(BF16)
        carry = jnp.zeros((H, 1), F32)
        db = jnp.zeros((H, 1), F32)
        for j in reversed(range(S // blk)):
            sl = slice(j * blk, (j + 1) * blk)
            dfv = df_ref[:, sl]
            d_lf = _split3_dot(dfv, fromon) + carry
            carry = carry + jnp.sum(dfv, axis=1, keepdims=True)
            u = l_ref[:, sl] + b_ref[...]
            _, lsn = _log_sigmoid_pair(u)
            dl = d_lf * jnp.exp(lsn)
            dl_ref[:, sl] = dl
            db = db + jnp.sum(dl, axis=1, keepdims=True)
        db_ref[...] = db

    return pl.pallas_call(
        body, name=name,
        out_shape=[jax.ShapeDtypeStruct((H, S), F32), jax.ShapeDtypeStruct((H, 1), F32)],
        compiler_params=pltpu.CompilerParams(vmem_limit_bytes=VMEM_LIMIT),
    )(logit_t, b_col, d_f)


def _sigmoid(t):
    return 1.0 / (1.0 + jnp.exp(-t))


def _gate_fwd(o3, w3, proj, D, *, name, tm=512):
    S = proj.shape[0]
    tm = _tile(S, tm)

    def body(o0, o1, o2, w0, w1, w2, g0, g1, g2, out_ref):
        acc = None
        for o_ref, w_ref, g_ref in ((o0, w0, g0), (o1, w1, g1), (o2, w2, g2)):
            t = _sigmoid(g_ref[...]) * _dot(o_ref[...], w_ref[...])
            acc = t if acc is None else acc + t
        out_ref[...] = acc.astype(out_ref.dtype)

    ospec = lambda d: pl.BlockSpec((tm, d), lambda i: (i, 0))
    wspec = lambda w: pl.BlockSpec(w.shape, lambda i: (0, 0))
    gspec = lambda j: pl.BlockSpec((tm, D), lambda i: (i, j))
    return pl.pallas_call(
        body, name=name, grid=(S // tm,),
        in_specs=[ospec(o.shape[1]) for o in o3] + [wspec(w) for w in w3] + [gspec(j) for j in range(3)],
        out_specs=pl.BlockSpec((tm, D), lambda i: (i, 0)),
        out_shape=jax.ShapeDtypeStruct((S, D), BF16),
        compiler_params=_params(("parallel",)),
    )(*o3, *w3, proj, proj, proj)


def _gate_bwd(o3, w3, proj, dmerged, D, *, name, tm=512):
    S = proj.shape[0]
    tm = _tile(S, tm)

    def body(o0, o1, o2, w0, w1, w2, g0, g1, g2, dm_ref, dg_ref, db0, db1, db2, do0, do1, do2):
        dm = dm_ref[...]
        for j, (o_ref, w_ref, g_ref, db_ref, do_ref) in enumerate(
                ((o0, w0, g0, db0, do0), (o1, w1, g1, db1, do1), (o2, w2, g2, db2, do2))):
            s = _sigmoid(g_ref[...])
            br = _dot(o_ref[...], w_ref[...])
            dg_ref[:, j * D:(j + 1) * D] = (dm * br * s * (1.0 - s)).astype(dg_ref.dtype)
            dbr = (dm * s).astype(BF16)
            db_ref[...] = dbr
            do_ref[...] = _dot(dbr, w_ref[...], _NT)

    ospec = lambda d: pl.BlockSpec((tm, d), lambda i: (i, 0))
    wspec = lambda w: pl.BlockSpec(w.shape, lambda i: (0, 0))
    gspec = lambda j: pl.BlockSpec((tm, D), lambda i: (i, j))
    dspec = pl.BlockSpec((tm, D), lambda i: (i, 0))
    return pl.pallas_call(
        body, name=name, grid=(S // tm,),
        in_specs=[ospec(o.shape[1]) for o in o3] + [wspec(w) for w in w3] + [gspec(j) for j in range(3)] + [dspec],
        out_specs=[pl.BlockSpec((tm, 3 * D), lambda i: (i, 0))] + [dspec] * 3 + [ospec(o.shape[1]) for o in o3],
        out_shape=[jax.ShapeDtypeStruct((S, proj.shape[1]), BF16)] + [jax.ShapeDtypeStruct((S, D), BF16)] * 3
        + [jax.ShapeDtypeStruct((S, o.shape[1]), F32) for o in o3],
        compiler_params=_params(("parallel",)),
    )(*o3, *w3, proj, proj, proj, dmerged)


def _loss_sum(dy, D, *, name):
    R, C = dy.shape
    tr = _row_tile(R, C, 2)

    def body(dy_ref, out_ref):
        @pl.when(pl.program_id(0) == 0)
        def _():
            out_ref[...] = jnp.zeros_like(out_ref)

        v = dy_ref[...]
        out_ref[...] += (0.5 * D) * jnp.sum(v * v)

    return pl.pallas_call(
        body, name=name, grid=(R // tr,),
        in_specs=[pl.BlockSpec((tr, C), lambda i: (i, 0))],
        out_specs=pl.BlockSpec((8, 128), lambda i: (0, 0)),
        out_shape=jax.ShapeDtypeStruct((8, 128), F32),
        compiler_params=_params(("arbitrary",)),
    )(dy)[0, 0]


def _pair_sum(stacked, got, core, *, name):
    n, _, r, c = stacked.shape
    tr = _row_tile(r, c, 3)

    def body(core_ref, a_ref, b_ref, o_ref):
        o_ref[...] = (a_ref[0].astype(F32) + b_ref[...].astype(F32)).astype(o_ref.dtype)

    spec = pl.BlockSpec((1, tr, c), lambda s, i, core_ref: (s, i, 0))
    return pl.pallas_call(
        body, name=name,
        grid_spec=pltpu.PrefetchScalarGridSpec(
            num_scalar_prefetch=1, grid=(n, r // tr),
            in_specs=[pl.BlockSpec((1, 1, tr, c), lambda s, i, core_ref: (s, core_ref[0], i, 0)), spec],
            out_specs=spec),
        out_shape=jax.ShapeDtypeStruct((n, r, c), BF16),
        compiler_params=_params(("parallel", "parallel")),
    )(core.astype(jnp.int32).reshape(1), stacked, got)


def _chip_sum(parts, got, chip, *, name):
    _, r, c = parts.shape
    tr = _row_tile(r, c, 6)

    def body(chip_ref, p_ref, q0_ref, q1_ref, q2_ref, o_ref):
        o_ref[...] = ((p_ref[0].astype(F32) + q0_ref[0].astype(F32)) + q1_ref[0].astype(F32)) + q2_ref[0].astype(F32)

    from_chip = lambda j: pl.BlockSpec((1, tr, c), lambda i, chip_ref: (j, i, 0))
    return pl.pallas_call(
        body, name=name,
        grid_spec=pltpu.PrefetchScalarGridSpec(
            num_scalar_prefetch=1, grid=(r // tr,),
            in_specs=[pl.BlockSpec((1, tr, c), lambda i, chip_ref: (chip_ref[0], i, 0))] + [from_chip(j) for j in range(3)],
            out_specs=pl.BlockSpec((tr, c), lambda i, chip_ref: (i, 0))),
        out_shape=jax.ShapeDtypeStruct((r, c), F32),
        compiler_params=_params(("parallel",)),
    )(chip.astype(jnp.int32).reshape(1), parts, got, got, got)


def _adamw_math(w, g, m, v):
    m2 = ADAM_B1 * m + (1.0 - ADAM_B1) * g
    v2 = ADAM_B2 * v + (1.0 - ADAM_B2) * (g * g)
    m_hat = m2 / (1.0 - ADAM_B1 ** ADAM_STEP)
    v_hat = v2 / (1.0 - ADAM_B2 ** ADAM_STEP)
    delta = -ADAM_LR * (m_hat / (jnp.sqrt(v_hat) + ADAM_EPS) + ADAM_WD * w)
    return delta, m2, v2


def _adamw(w, g, m, v, *, name):
    return _ew(_adamw_math, [w, g, m, v], (F32, F32, F32), name=name)


def _adamw_small(w, parts, m, v, *, name):
    n = parts.shape[0]

    def body(w_ref, p_ref, m_ref, v_ref, g_ref, d_ref, m2_ref, v2_ref):
        g = p_ref[0]
        for i in range(1, n):
            g = g + p_ref[i]
        g_ref[...] = g
        d_ref[...], m2_ref[...], v2_ref[...] = _adamw_math(w_ref[...], g, m_ref[...], v_ref[...])

    shp = jax.ShapeDtypeStruct(w.shape, F32)
    return pl.pallas_call(body, name=name, out_shape=[shp] * 4)(w, parts, m, v)


_ANY = pl.BlockSpec(memory_space=pl.ANY)


def _mesh_place():
    x, y, c = lax.axis_index("x"), lax.axis_index("y"), lax.axis_index("c")
    chips = [(1 - x, y), (x, 1 - y), (1 - x, 1 - y)]
    return x, y, c, chips


def _remote(src, dst, sems, i, to):
    send_sems, recv_sems = sems
    return pltpu.make_async_remote_copy(src_ref=src, dst_ref=dst, send_sem=send_sems.at[i], recv_sem=recv_sems.at[i],
                                        device_id=to, device_id_type=MESH_ID)


def _gather_weights(shards, *, name):
    n = len(shards)

    def body(*refs):
        ins, outs = refs[:n], refs[n:2 * n]
        sems = refs[2 * n:2 * n + 2]
        x, y, c, chips = _mesh_place()
        me = 2 * x + y
        sibling = (x, y, 1 - c)
        sent = []
        for w in range(n):
            for j, chip in enumerate(chips):
                cp = _remote(ins[w].at[c], outs[w].at[me, c], sems, 6 * w + j, (chip[0], chip[1], c))
                cp.start()
                sent.append(cp)
        for w in range(n):
            for j, chip in enumerate(chips):
                got = outs[w].at[2 * chip[0] + chip[1], c]
                _remote(got, got, sems, 6 * w + j, sibling).wait_recv()
                cp = _remote(got, got, sems, 6 * w + 3 + j, sibling)
                cp.start()
                sent.append(cp)
        for w in range(n):
            for j, chip in enumerate(chips):
                got = outs[w].at[2 * chip[0] + chip[1], 1 - c]
                _remote(got, got, sems, 6 * w + 3 + j, sibling).wait_recv()
        for cp in sent:
            cp.wait_send()

    outs = pl.pallas_call(
        body, name=name,
        in_specs=[_ANY] * n, out_specs=[_ANY] * n,
        out_shape=[jax.ShapeDtypeStruct((N_CHIPS,) + s.shape, s.dtype) for s in shards],
        scratch_shapes=[pltpu.SemaphoreType.DMA((6 * n,)), pltpu.SemaphoreType.DMA((6 * n,))],
    )(*shards)
    me = 2 * lax.axis_index("x") + lax.axis_index("y")
    return [lax.dynamic_update_index_in_dim(o, s, me, 0) for o, s in zip(outs, shards)]


def _forward_halves(gathered, *, name):
    n = len(gathered)

    def body(*refs):
        ins, outs = refs[:n], refs[n:2 * n]
        sems = refs[2 * n:2 * n + 2]
        x, y, c, chips = _mesh_place()
        sibling = (x, y, 1 - c)
        sent = []
        for w in range(n):
            for j, chip in enumerate(chips):
                peer = 2 * chip[0] + chip[1]
                cp = _remote(ins[w].at[peer, c], outs[w].at[peer, c], sems, 3 * w + j, sibling)
                cp.start()
                sent.append(cp)
        for w in range(n):
            for j, chip in enumerate(chips):
                land = outs[w].at[2 * chip[0] + chip[1], 1 - c]
                _remote(land, land, sems, 3 * w + j, sibling).wait_recv()
        for cp in sent:
            cp.wait_send()

    return pl.pallas_call(
        body, name=name,
        in_specs=[_ANY] * n, out_specs=[_ANY] * n,
        out_shape=[jax.ShapeDtypeStruct(g.shape, g.dtype) for g in gathered],
        input_output_aliases={w: w for w in range(n)},
        scratch_shapes=[pltpu.SemaphoreType.DMA((3 * n,)), pltpu.SemaphoreType.DMA((3 * n,))],
    )(*gathered)


def _exchange_siblings(grads, *, name):
    n = len(grads)

    def body(*refs):
        ins, got = refs[:n], refs[n:2 * n]
        sems = refs[2 * n:2 * n + 2]
        x, y, c, _ = _mesh_place()
        sibling = (x, y, 1 - c)
        sent = []
        for w in range(n):
            for s in range(N_CHIPS):
                cp = _remote(ins[w].at[s, 1 - c], got[w].at[s], sems, N_CHIPS * w + s, sibling)
                cp.start()
                sent.append(cp)
        for w in range(n):
            for s in range(N_CHIPS):
                _remote(got[w].at[s], got[w].at[s], sems, N_CHIPS * w + s, sibling).wait_recv()
        for cp in sent:
            cp.wait_send()

    n_sem = N_CHIPS * n
    return pl.pallas_call(
        body, name=name,
        in_specs=[_ANY] * n, out_specs=[_ANY] * n,
        out_shape=[jax.ShapeDtypeStruct((N_CHIPS,) + g.shape[2:], g.dtype) for g in grads],
        scratch_shapes=[pltpu.SemaphoreType.DMA((n_sem,)), pltpu.SemaphoreType.DMA((n_sem,))],
    )(*grads)


def _share_halves(halves, small):
    n = len(halves)

    def body(*refs):
        ins, small_ref = refs[:n], refs[n]
        outs, small_out = refs[n + 1:2 * n + 1], refs[2 * n + 1]
        sems = refs[2 * n + 2:2 * n + 4]
        x, y, c, chips = _mesh_place()
        sibling = (x, y, 1 - c)
        me = 4 * x + 2 * y + c
        sent = [_remote(ins[w], outs[w].at[c], sems, w, sibling) for w in range(n)]
        peers = [sibling] + [(ch[0], ch[1], cc) for ch in chips for cc in (c, 1 - c)]
        sent += [_remote(small_ref, small_out.at[me], sems, n + j, peer) for j, peer in enumerate(peers)]
        for cp in sent:
            cp.start()
        for w in range(n):
            _remote(outs[w].at[1 - c], outs[w].at[1 - c], sems, w, sibling).wait_recv()
        for j, peer in enumerate(peers):
            frm = small_out.at[4 * peer[0] + 2 * peer[1] + peer[2]]
            _remote(frm, frm, sems, n + j, peer).wait_recv()
        for cp in sent:
            cp.wait_send()

    n_sem = n + 7
    outs = pl.pallas_call(
        body, name="share_halves",
        in_specs=[_ANY] * (n + 1), out_specs=[_ANY] * (n + 1),
        out_shape=[jax.ShapeDtypeStruct((2,) + h.shape, h.dtype) for h in halves]
        + [jax.ShapeDtypeStruct((8,) + small.shape, small.dtype)],
        scratch_shapes=[pltpu.SemaphoreType.DMA((n_sem,)), pltpu.SemaphoreType.DMA((n_sem,))],
    )(*halves, small)
    c = lax.axis_index("c")
    me = 4 * lax.axis_index("x") + 2 * lax.axis_index("y") + c
    return ([lax.dynamic_update_index_in_dim(o, h, c, 0) for o, h in zip(outs[:n], halves)],
            lax.dynamic_update_index_in_dim(outs[n], small, me, 0))


EARLY = ("w_ff_down", "w_ff_up", "w_out", "w_branch_sb", "w_branch_fox", "w_branch_mem", "w_mem_kv")


def _norm_bwd_tail(dy, x, add, g):
    r = lax.rsqrt(jnp.mean(x * x, axis=-1, keepdims=True) + EPS)
    xh = x * r
    dyg = dy * g
    c = jnp.mean(dyg * xh, axis=-1, keepdims=True)
    return r * (dyg - xh * c) + add, jnp.sum(dy * xh, axis=0, keepdims=True)


def _split(outs, n):
    outs = list(outs) if isinstance(outs, (list, tuple)) else [outs]
    return outs[:n], outs[n:]


def _local_step(x, mem, target, small, W, gather_rest=None, reduce_early=None, reduce_late=None):
    S, D = x.shape
    o_qkv, o_mq, o_f = 3 * D, 3 * D + 2 * 3 * D_SB, 3 * D + 2 * 3 * D_SB + D_MEM
    tq = 512

    g_comms, finish_weights = gather_rest if gather_rest is not None else ([None] * 3, None)
    h = _rmsnorm_fwd(x, small["g_mix_norm"], BF16, name="mix_norm")
    (proj,), landed = _split(_mm(h, W["w_in"], name="in_proj", tb=True, tn=768, comm=g_comms[0]), 1)
    blk = lambda j: (proj, (o_qkv + j * D_SB) // LANES)
    sb_q, sb_k, sb_v, fx_q, fx_k, fx_v = [blk(j) for j in range(6)]
    m_q = (proj, o_mq // LANES)
    f_logit_t = _mm(W["w_in"][o_f:o_f + ROW_TILE], h, name="forget_logits", tb=True)[:FOX_HEADS]
    b_col = small["b_forget"].reshape(FOX_HEADS, 1)
    lanes = lambda g: jnp.tile(g, (1, LANES // g.shape[1]))
    g_fq, g_fk, g_mq, g_mk = [lanes(small[k]) for k in ("g_fox_q", "g_fox_k", "g_mem_q", "g_mem_k")]

    (o_sb, sb_tot), more = _split(_sbl_fwd(sb_q, sb_k, sb_v, width=D_SB, hd=HD, name="sb_fwd", tq=tq, comm=g_comms[1]), 2)
    landed += more

    fq = _hnorm_fwd(fx_q, g_fq, width=D_FOX, hd=HD, name="fox_q_norm")
    fk = _hnorm_fwd(fx_k, g_fk, width=D_FOX, hd=HD, name="fox_k_norm")
    f_cum = _forget_fwd(f_logit_t, b_col, name="forget_fwd")
    tkf = _tile(S, tq)
    f_bias = (f_cum.reshape(FOX_HEADS, S, 1), f_cum.reshape(FOX_HEADS, S // tkf, 1, tkf))
    (o_fox, fox_lse), more = _split(_sml_fwd((fq, 0), (fk, 0), fx_v, f_bias, width=D_FOX, hd=HD, causal=True,
                                             name="fox_fwd", tq=tq, tk=tq, comm=g_comms[2]), 2)
    landed += more
    if finish_weights is not None:
        W = {**W, **finish_weights(landed)}

    mh = _rmsnorm_fwd(mem, small["g_mem_norm"], BF16, name="mem_norm")
    mkv = _mm(mh, W["w_mem_kv"], name="mem_kv")
    mv = (mkv, D_MEM // LANES)
    mq = _hnorm_fwd(m_q, g_mq, width=D_MEM, hd=MEM_HD, name="mem_q_norm")
    mk = _hnorm_fwd((mkv, 0), g_mk, width=D_MEM, hd=MEM_HD, name="mem_k_norm")
    o_mem, mem_lse = _sml_fwd((mq, 0), (mk, 0), mv, width=D_MEM, hd=MEM_HD, causal=False, name="mem_fwd", tq=tq, tk=256)

    o3 = [o_sb, o_fox, o_mem]
    w3 = [W["w_branch_sb"], W["w_branch_fox"], W["w_branch_mem"]]
    merged = _gate_fwd(o3, w3, proj, D, name="gate_fwd")
    def residual_and_norm(acc, res, g):
        t = res + acc
        return t, t * lax.rsqrt(jnp.mean(t * t, axis=-1, keepdims=True) + EPS) * g

    assert W["w_out"].shape[1] == D
    x1, h2 = _mm(merged, W["w_out"], name="out_proj", extras=(x, small["g_mlp_norm"]), out_dtypes=(F32, BF16),
                 epilogue=residual_and_norm, tn=D)

    def relu2(acc):
        u = jnp.maximum(acc, 0.0)
        return u, u * u

    u, a = _mm(h2, W["w_ff_up"], name="ff_up", out_dtypes=(BF16, BF16), epilogue=relu2)
    def head(acc, res, tgt):
        d = (res + acc - tgt) * (1.0 / D)
        return d, d

    dy, dy16 = _mm(a, W["w_ff_down"], name="ff_down", extras=(x1, target), out_dtypes=(F32, BF16), epilogue=head, tn=512)
    loss = _loss_sum(dy, D, name="loss")

    G = {}
    du = _mm(dy16, W["w_ff_down"], name="d_ff_act", tb=True, out_dtypes=(BF16,), extras=(u,),
             epilogue=lambda acc, uu: (acc * (2.0 * uu.astype(F32)),))
    G["w_ff_down"] = _mm(a, dy16, name="d_w_ff_down", ta=True, out_dtypes=(BF16,))
    G["w_ff_up"] = _mm(h2, du, name="d_w_ff_up", ta=True, out_dtypes=(BF16,))
    dx1, dg_rows = _mm(du, W["w_ff_up"], name="d_mlp_in", tb=True, extras=(x1, dy, small["g_mlp_norm"]),
                       epilogue=_norm_bwd_tail, col_sums=1, tm=512, tn=D)
    dg_mlp = jnp.sum(dg_rows, axis=0, keepdims=True)
    dmerged = _mm(dx1, W["w_out"], name="d_merged", tb=True)
    G["w_out"] = _mm(merged, dx1, name="d_w_out", ta=True, out_dtypes=(BF16,))
    dgate, db0, db1, db2, do_sb, do_fox, do_mem = _gate_bwd(o3, w3, proj, dmerged, D, name="gate_bwd")
    for nm, o, db in zip(("w_branch_sb", "w_branch_fox", "w_branch_mem"), o3, (db0, db1, db2)):
        G[nm] = _mm(o, db, name="d_" + nm, ta=True, out_dtypes=(BF16,))

    dmq_n, dmk_n, dmv = _sml_bwd((mq, 0), (mk, 0), mv, (o_mem, 0), (mem_lse, 0), (do_mem, 0), width=D_MEM, hd=MEM_HD,
                                 causal=False, name="mem_bwd", tq=tq, tk=256)
    dm_q, dg_mem_q = _hnorm_bwd(m_q, g_mq, dmq_n, width=D_MEM, hd=MEM_HD, name="d_mem_q_norm")
    dmk_raw, dg_mem_k = _hnorm_bwd((mkv, 0), g_mk, dmk_n, width=D_MEM, hd=MEM_HD, name="d_mem_k_norm")
    dmkv = jnp.concatenate([dmk_raw, dmv.astype(BF16)], axis=1)
    G["w_mem_kv"] = _mm(mh, dmkv, name="d_w_mem_kv", ta=True, out_dtypes=(BF16,))
    dmh = _mm(dmkv, W["w_mem_kv"], name="d_mem_h", tb=True)
    _, dg_mem = _rmsnorm_bwd(mem, small["g_mem_norm"], dmh, name="d_mem_norm")

    r_comms, r_finish = reduce_early({k: G.pop(k) for k in EARLY}) if reduce_early is not None else ([None] * 2, None)
    dsb, landed_sb = _split(_sbl_bwd(sb_q, sb_k, sb_v, (do_sb, 0), (sb_tot, 0), width=D_SB, hd=HD, name="sb_bwd", tq=tq,
                                     comm=r_comms[0]), 3)
    (dfq, dfk, dfv, df_row, df_col), landed_fox = _split(
        _sml_bwd((fq, 0), (fk, 0), fx_v, (o_fox, 0), (fox_lse, 0), (do_fox, 0), f_bias, width=D_FOX, hd=HD, causal=True,
                 name="fox_bwd", tq=tq, tk=tq, comm=r_comms[1]), 5)
    early = r_finish(landed_sb, landed_fox) if r_finish is not None else {}
    dfx_q, dg_fox_q = _hnorm_bwd(fx_q, g_fq, dfq, width=D_FOX, hd=HD, name="d_fox_q_norm")
    dfx_k, dg_fox_k = _hnorm_bwd(fx_k, g_fk, dfk, width=D_FOX, hd=HD, name="d_fox_k_norm")
    d_fcum = df_row.reshape(FOX_HEADS, S) - df_col.reshape(FOX_HEADS, S)
    d_flogit_t, db_forget = _forget_bwd(f_logit_t, b_col, d_fcum, name="forget_bwd")
    dg_fox_q, dg_fox_k = dg_fox_q[:, :HD], dg_fox_k[:, :HD]

    rest_cols = jnp.concatenate([t.astype(BF16) for t in (*dsb, dfx_q, dfx_k, dfv, dm_q)]
                                + [d_flogit_t.T.astype(BF16), jnp.zeros((S, F_PAD - FOX_HEADS), BF16)], axis=1)
    dproj = lax.dynamic_update_slice(dgate, rest_cols, (0, 3 * D))
    g_w_in = _mm(dproj, h, name="d_w_in", ta=True, out_dtypes=(BF16,), tm=768)
    comm, finish = reduce_late({"w_in": g_w_in}) if reduce_late is not None else (None, None)
    (grad_x, dg_rows), landed = _split(
        _mm(dproj, W["w_in"], name="d_mix_in", tk=2304, tm=512, tn=D, extras=(x, dx1, small["g_mix_norm"]),
            epilogue=_norm_bwd_tail, col_sums=1, comm=comm), 2)
    dg_mix = jnp.sum(dg_rows, axis=0, keepdims=True)
    if finish is None:
        G["w_in"] = g_w_in
    else:
        early.update(finish(landed))

    small_grads = dict(g_mix_norm=dg_mix, g_mem_norm=dg_mem, b_forget=db_forget.reshape(1, FOX_HEADS),
                       g_fox_q=dg_fox_q, g_fox_k=dg_fox_k, g_mem_q=dg_mem_q, g_mem_k=dg_mem_k, g_mlp_norm=dg_mlp)
    return loss, grad_x, G, small_grads, early


BIG = ("w_in", "w_mem_kv", "w_branch_sb", "w_branch_fox", "w_branch_mem", "w_out", "w_ff_up", "w_ff_down")
COLUMN_SHARDED = ("w_in", "w_branch_sb", "w_branch_fox", "w_branch_mem", "w_ff_up")
SMALL = ("g_mix_norm", "g_mem_norm", "b_forget", "g_fox_q", "g_fox_k", "g_mem_q", "g_mem_k", "g_mlp_norm")
ORDER = ("g_mix_norm", "g_mem_norm", "w_in", "b_forget", "g_fox_q", "g_fox_k", "g_mem_q", "g_mem_k", "w_mem_kv",
         "w_branch_sb", "w_branch_fox", "w_branch_mem", "w_out", "g_mlp_norm", "w_ff_up", "w_ff_down")


def _unshard(name, gathered):
    n, _, rh, c = gathered.shape
    t = gathered.reshape(n, 2 * rh, c)
    if name in COLUMN_SHARDED:
        return t.transpose(1, 0, 2).reshape(2 * rh, n * c)
    return t.reshape(n * 2 * rh, c)


def _reshard(name, full):
    if name in COLUMN_SHARDED:
        r, c = full.shape
        t = full.reshape(r, N_CHIPS, c // N_CHIPS).transpose(1, 0, 2)
    else:
        r, c = full.shape[0] // N_CHIPS, full.shape[1]
        t = full.reshape(N_CHIPS, r, c)
    return t.reshape(N_CHIPS, 2, t.shape[1] // 2, t.shape[2])


ROW_TILE = 16
IN_BUF_ALIGN = 256


def _in_segments(D):
    n_qkv = 6 * D_SB
    o_mq, o_gate = n_qkv + FOX_HEADS, n_qkv + FOX_HEADS + D_MEM
    return [(0, n_qkv, 3 * D), (n_qkv, o_mq, 3 * D + n_qkv + D_MEM), (o_mq, o_gate, 3 * D + n_qkv), (o_gate, o_gate + 3 * D, 0)]


class _InLayout:
    def __init__(self, D, shard, n):
        self.D, self.shard, self.n = D, shard, n
        down = lambda v: v // ROW_TILE * ROW_TILE
        up = lambda v: -(-v // ROW_TILE) * ROW_TILE
        self.pieces = []
        ends = []
        for s in range(n):
            cursor, mine = 0, []
            for a, b, p in _in_segments(D):
                x0, x1 = max(a, s * shard), min(b, (s + 1) * shard)
                if x0 < x1:
                    p0 = p + x0 - a
                    rows = up(p0 + x1 - x0) - down(p0)
                    mine.append((x0 - s * shard, x1 - x0, p0, cursor, rows))
                    cursor += rows
            self.pieces.append(mine)
            ends.append(cursor)
        self.rows = -(-max(ends) // IN_BUF_ALIGN) * IN_BUF_ALIGN
        self.padded_rows = 3 * D + 6 * D_SB + D_MEM + F_PAD

    def _per_shard(self, fn, chip, operand):
        return lax.switch(chip, [functools.partial(fn, s) for s in range(self.n)], operand)

    def pack(self, chip, rows):
        def one(s, t):
            out, at = [], 0
            for x0, n_rows, p0, start, region in self.pieces[s]:
                lead = p0 % ROW_TILE
                out += [jnp.zeros((start + lead - at, t.shape[1]), t.dtype), t[x0:x0 + n_rows]]
                at = start + lead + n_rows
            return jnp.concatenate(out + [jnp.zeros((self.rows - at, t.shape[1]), t.dtype)], axis=0)
        return self._per_shard(one, chip, rows)

    def unpack(self, chip, buf, pad_to):
        def one(s, t):
            out = [t[start + p0 % ROW_TILE:start + p0 % ROW_TILE + n_rows] for _, n_rows, p0, start, _ in self.pieces[s]]
            return jnp.concatenate(out + [jnp.zeros((pad_to - self.shard, t.shape[1]), t.dtype)], axis=0)
        return self._per_shard(one, chip, buf)

    def to_padded(self, bufs):
        runs = sorted((p0, s, start, region) for s in range(self.n) for _, _, p0, start, region in self.pieces[s])
        chunks, end = [], 0
        for p0, s, start, region in runs:
            d0 = p0 // ROW_TILE * ROW_TILE
            src = bufs[s, start:start + region]
            if d0 < end:
                assert end - d0 == ROW_TILE
                last = chunks.pop()
                chunks += [last[:-ROW_TILE], last[-ROW_TILE:] + src[:ROW_TILE], src[ROW_TILE:]]
            else:
                if d0 > end:
                    chunks.append(jnp.zeros((d0 - end, bufs.shape[2]), bufs.dtype))
                chunks.append(src)
            end = d0 + region
        chunks.append(jnp.zeros((self.padded_rows - end, bufs.shape[2]), bufs.dtype))
        return jnp.concatenate(chunks, axis=0)

    def from_padded(self, gp):
        bufs = []
        for s in range(self.n):
            out, at = [], 0
            for _, n_rows, p0, start, region in self.pieces[s]:
                d0 = p0 // ROW_TILE * ROW_TILE
                row = d0 + lax.broadcasted_iota(jnp.int32, (region, 1), 0)
                out.append(jnp.where((row >= p0) & (row < p0 + n_rows), gp[d0:d0 + region], jnp.zeros((), gp.dtype)))
                at = start + region
            bufs.append(jnp.concatenate(out + [jnp.zeros((self.rows - at, gp.shape[1]), gp.dtype)], axis=0))
        return jnp.stack(bufs)


SMALL_ROWS = 16


def _pack_small(vals, scalar=None):
    width = max(vals[k].shape[1] for k in SMALL)
    rows = [jnp.pad(vals[k].astype(F32), ((0, 0), (0, width - vals[k].shape[1]))) for k in SMALL]
    extra = jnp.zeros((SMALL_ROWS - len(SMALL), width), F32)
    if scalar is not None:
        extra = extra.at[0, 0].set(scalar)
    return jnp.concatenate(rows + [extra], axis=0)


def _unpack_small(packed, like):
    return {k: packed[i:i + 1, :like[k].shape[1]] for i, k in enumerate(SMALL)}


def kernel(x, mem, g_mix_norm, g_mem_norm, w_in, b_forget, g_fox_q, g_fox_k, g_mem_q, g_mem_k, w_mem_kv, w_branch_sb, w_branch_fox, w_branch_mem, w_out, g_mlp_norm, w_ff_up, w_ff_down, loss_target, m_g_mix_norm, m_g_mem_norm, m_w_in, m_b_forget, m_g_fox_q, m_g_fox_k, m_g_mem_q, m_g_mem_k, m_w_mem_kv, m_w_branch_sb, m_w_branch_fox, m_w_branch_mem, m_w_out, m_g_mlp_norm, m_w_ff_up, m_w_ff_down, v_g_mix_norm, v_g_mem_norm, v_w_in, v_b_forget, v_g_fox_q, v_g_fox_k, v_g_mem_q, v_g_mem_k, v_w_mem_kv, v_w_branch_sb, v_w_branch_fox, v_w_branch_mem, v_w_out, v_g_mlp_norm, v_w_ff_up, v_w_ff_down):
    given = dict(locals())
    D = x.shape[-1]
    weights = {k: given[k] for k in ORDER}
    moms = {k: given["m_" + k] for k in ORDER}
    vars_ = {k: given["v_" + k] for k in ORDER}

    me_chip = 2 * lax.axis_index("x") + lax.axis_index("y")

    n_in = w_in.shape[2]
    lay = _InLayout(D, n_in, N_CHIPS)
    transposed = lambda t: jnp.transpose(t[0])
    shards = {}
    for k in BIG:
        w = weights[k][0].astype(BF16)
        if k == "w_in":
            w = lay.pack(me_chip, jnp.transpose(w))
        shards[k] = w.reshape(2, w.shape[0] // 2, w.shape[1])
    gathered_in = _gather_weights([shards["w_in"]], name="gather_w_in")[0]
    W = {"w_in": lay.to_padded(gathered_in.reshape(N_CHIPS, lay.rows, D))}
    carried = (("w_branch_sb", "w_branch_fox", "w_branch_mem", "w_out"), ("w_ff_up", "w_mem_kv"), ("w_ff_down",))
    rest = [k for grp in carried for k in grp]
    assert sorted(rest + ["w_in"]) == sorted(BIG)

    def finish_weights(landed):
        full = _forward_halves(landed, name="forward_halves")
        full = [lax.dynamic_update_index_in_dim(o, shards[k], me_chip, 0) for k, o in zip(rest, full)]
        return {k: _unshard(k, g) for k, g in zip(rest, full)}

    me_core = lax.axis_index("c")

    def sum_chips(names, parts, got):
        return {k: _chip_sum(p, q, me_chip, name="sum_chips_" + k) for k, p, q in zip(names, parts, got)}

    def pair_sums(grads, tag):
        names = list(grads)
        stacked = {k: _reshard(k, grads[k]) for k in names if k != "w_in"}
        if "w_in" in grads:
            stacked["w_in"] = lay.from_padded(grads["w_in"]).reshape(N_CHIPS, 2, lay.rows // 2, D)
        got = _exchange_siblings([stacked[k] for k in names], name="exchange_siblings_" + tag)
        return {k: _pair_sum(stacked[k], q, me_core, name="sum_pair_" + k) for k, q in zip(names, got)}

    def reduce_early(grads):
        parts = pair_sums(grads, "early")
        groups = [[k for k in parts if k in ("w_ff_down", "w_ff_up")], [k for k in parts if k not in ("w_ff_down", "w_ff_up")]]
        comms = [_ChipExchange("scatter", [parts[k] for k in grp]) for grp in groups]

        def finish(*landed):
            out = {}
            for grp, got in zip(groups, landed):
                out.update(sum_chips(grp, [parts[k] for k in grp], got))
            return out
        return comms, finish

    def reduce_late(grads):
        parts = pair_sums(grads, "late")
        names = list(parts)
        return _ChipExchange("scatter", [parts[k] for k in names]), functools.partial(sum_chips, names, [parts[k] for k in names])

    small = {k: weights[k] for k in SMALL}
    loss_part, grad_x, G, small_grads, halves = _local_step(
        x[0], mem[0], loss_target[0], small, W,
        gather_rest=([_ChipExchange("gather", [shards[k] for k in grp]) for grp in carried], finish_weights),
        reduce_early=reduce_early, reduce_late=reduce_late)
    assert not G, list(G)
    reduced, small_parts = _share_halves([halves[k] for k in BIG], _pack_small(small_grads, loss_part))

    grads, deltas, new_m, new_v = {}, {}, {}, {}
    for k, g in zip(BIG, reduced):
        shp = weights[k].shape
        if k == "w_in":
            g2 = lay.unpack(me_chip, g.reshape(lay.rows, D), lay.rows)
            padded = lambda t: jnp.pad(transposed(t), ((0, lay.rows - n_in), (0, 0)))
            outs = _adamw(padded(weights[k]), g2, padded(moms[k]), padded(vars_[k]), name="adamw_" + k)
            g2, d, m2, v2 = [jnp.transpose(t[:n_in]) for t in (g2, *outs)]
        else:
            g2 = g.reshape(shp[1], shp[2])
            d, m2, v2 = _adamw(weights[k][0], g2, moms[k][0], vars_[k][0], name="adamw_" + k)
        grads[k], deltas[k], new_m[k], new_v[k] = g2.reshape(shp), d.reshape(shp), m2.reshape(shp), v2.reshape(shp)
    sg, sd, sm, sv = _adamw_small(_pack_small(small), small_parts, _pack_small({k: moms[k] for k in SMALL}),
                                  _pack_small({k: vars_[k] for k in SMALL}), name="adamw_small")
    for dst, packed in ((grads, sg), (deltas, sd), (new_m, sm), (new_v, sv)):
        dst.update(_unpack_small(packed, small))

    loss = sg[len(SMALL), 0]
    return (loss, grad_x[None], *[grads[k] for k in ORDER], *[deltas[k] for k in ORDER],
            *[new_m[k] for k in ORDER], *[new_v[k] for k in ORDER])
```

```python
import functools

import jax
import jax.numpy as jnp
from jax import lax
from jax.experimental import pallas as pl
from jax.experimental.pallas import tpu as pltpu

F32 = jnp.float32
BF16 = jnp.bfloat16
MESH_ID = pl.DeviceIdType.MESH

HD = 64
SB_HEADS = 8
FOX_HEADS = 8
MEM_HEADS = 4
MEM_HD = 128
D_SB = SB_HEADS * HD
D_FOX = FOX_HEADS * HD
D_MEM = MEM_HEADS * MEM_HD
EPS = 1e-6
NEG_INF = -1e30

ADAM_LR = 0.001
ADAM_B1 = 0.9
ADAM_B2 = 0.999
ADAM_EPS = 1e-08
ADAM_WD = 0.01
ADAM_STEP = 10

N_CHIPS = 4
VMEM_LIMIT = 56 * 1024 * 1024

F_PAD = 256


def _tile(n, target, align=128):
    if n <= target:
        return n
    best = None
    t = align
    while t <= target:
        if n % t == 0:
            best = t
        t += align
    assert best is not None, (n, target, align)
    return best


def _params(sem):
    return pltpu.CompilerParams(dimension_semantics=sem, vmem_limit_bytes=VMEM_LIMIT)


def _mm(a, b, *, name, ta=False, tb=False, out_dtypes=(F32,), epilogue=None, extras=(),
        tm=1024, tn=1024, tk=2048, comm=None, col_sums=0, a_gain=None):
    if ta:
        K, M = a.shape
    else:
        M, K = a.shape
    if tb:
        N, K2 = b.shape
    else:
        K2, N = b.shape
    assert K == K2, (a.shape, b.shape, ta, tb)
    tm, tn, tk = _tile(M, tm), _tile(N, tn), _tile(K, tk)
    nk = K // tk
    normed = a_gain is not None
    assert not normed or (nk == 1 and not ta)
    n_extra, n_out = len(extras) + normed, len(out_dtypes) + col_sums + normed
    if epilogue is None:
        epilogue = lambda acc: (acc,)
    dims = (((0 if ta else 1,), (1 if tb else 0,)), ((), ()))

    gm, gn = M // tm, N // tn

    def body(*refs):
        i, j, k = pl.program_id(0), pl.program_id(1), pl.program_id(2)
        got = _carry(comm, 2 + n_extra, n_out, (i == 0) & (j == 0) & (k == 0),
                     (i == gm - 1) & (j == gn - 1) & (k == nk - 1), refs)
        (a_ref, b_ref, *extra_refs), out_refs = got[0], list(got[1])
        if normed:
            gain_ref, normed_ref = extra_refs.pop(), out_refs.pop()
            av = a_ref[...].astype(F32)
            av = (av * lax.rsqrt(jnp.mean(av * av, axis=-1, keepdims=True) + EPS) * gain_ref[...]).astype(BF16)

            @pl.when(j == 0)
            def _():
                normed_ref[...] = av
        else:
            av = a_ref[...].astype(BF16)
        part = lax.dot_general(av, b_ref[...].astype(BF16), dims, preferred_element_type=F32)

        def finish(acc):
            outs = epilogue(acc, *[r[...] for r in extra_refs])
            for o_ref, o in zip(out_refs[:len(out_dtypes)], outs):
                o_ref[...] = o.astype(o_ref.dtype)
            for o_ref, o in zip(out_refs[len(out_dtypes):], outs[len(out_dtypes):]):
                first = lax.broadcasted_iota(jnp.int32, o_ref.shape, 0) == 0
                o_ref[...] = jnp.where(first, jnp.broadcast_to(o, o_ref.shape), 0.0)

        if nk == 1:
            finish(part)
        else:
            acc_ref = refs[-1]

            @pl.when(k == 0)
            def _():
                acc_ref[...] = part

            @pl.when((k > 0) & (k < nk - 1))
            def _():
                acc_ref[...] += part

            @pl.when(k == nk - 1)
            def _():
                finish(acc_ref[...] + part)

        got[2]()

    a_spec = pl.BlockSpec((tk, tm), lambda i, j, k: (k, i)) if ta else pl.BlockSpec((tm, tk), lambda i, j, k: (i, k))
    b_spec = pl.BlockSpec((tn, tk), lambda i, j, k: (j, k)) if tb else pl.BlockSpec((tk, tn), lambda i, j, k: (k, j))
    mn_spec = pl.BlockSpec((tm, tn), lambda i, j, k: (i, j))
    row_spec = pl.BlockSpec((1, tn), lambda i, j, k: (0, j))
    sum_spec = pl.BlockSpec((8, tn), lambda i, j, k: (i, j))
    c_ins, c_in_specs, c_out_specs, c_out_shape, c_scratch = _comm_args(comm)
    sem = ("parallel", "arbitrary" if normed else "parallel", "arbitrary") if comm is None else ("arbitrary",) * 3
    outs = pl.pallas_call(
        body, name=name,
        grid=(gm, gn, nk),
        in_specs=[a_spec, b_spec] + [row_spec if e.shape[0] == 1 else mn_spec for e in extras]
        + ([pl.BlockSpec((1, K), lambda i, j, k: (0, 0))] if normed else []) + c_in_specs,
        out_specs=[mn_spec] * len(out_dtypes) + [sum_spec] * col_sums
        + ([pl.BlockSpec((tm, K), lambda i, j, k: (i, 0))] if normed else []) + c_out_specs,
        out_shape=[jax.ShapeDtypeStruct((M, N), dt) for dt in out_dtypes]
        + [jax.ShapeDtypeStruct((8 * gm, N), F32)] * col_sums
        + ([jax.ShapeDtypeStruct((M, K), BF16)] if normed else []) + c_out_shape,
        scratch_shapes=c_scratch + ([pltpu.VMEM((tm, tn), F32)] if nk > 1 else []),
        compiler_params=_params(sem),
    )(a, b, *extras, *([a_gain] if normed else []), *c_ins)
    return outs if len(outs) > 1 else outs[0]


def _row_tile(rows, cols, n_arrays):
    budget = 10 * 1024 * 1024
    cols_padded = -(-cols // 128) * 128
    target = max(16, budget // (cols_padded * 4 * n_arrays * 2))
    return _tile(rows, target, align=16)


def _ew(fn, ins, out_dtypes, *, name):
    R, C = ins[0].shape
    n_in, n_out = len(ins), len(out_dtypes)
    tr = _row_tile(R, C, n_in + n_out)

    def body(*refs):
        outs = fn(*[r[...] for r in refs[:n_in]])
        for o_ref, o in zip(refs[n_in:], outs):
            o_ref[...] = o.astype(o_ref.dtype)

    spec = pl.BlockSpec((tr, C), lambda i: (i, 0))
    outs = pl.pallas_call(
        body, name=name, grid=(R // tr,),
        in_specs=[spec] * n_in, out_specs=[spec] * n_out,
        out_shape=[jax.ShapeDtypeStruct((R, C), dt) for dt in out_dtypes],
        compiler_params=_params(("parallel",)),
    )(*ins)
    return outs if n_out > 1 else outs[0]


def _rmsnorm_fwd(x, g, out_dtype, *, name):
    R, d = x.shape
    tr = _row_tile(R, d, 3)

    def body(x_ref, g_ref, o_ref):
        xv = x_ref[...].astype(F32)
        r = lax.rsqrt(jnp.mean(xv * xv, axis=-1, keepdims=True) + EPS)
        o_ref[...] = (xv * r * g_ref[...]).astype(o_ref.dtype)

    return pl.pallas_call(
        body, name=name, grid=(R // tr,),
        in_specs=[pl.BlockSpec((tr, d), lambda i: (i, 0)), pl.BlockSpec((1, d), lambda i: (0, 0))],
        out_specs=pl.BlockSpec((tr, d), lambda i: (i, 0)),
        out_shape=jax.ShapeDtypeStruct((R, d), out_dtype),
        compiler_params=_params(("parallel",)),
    )(x, g)


def _rmsnorm_bwd(x, g, dy, add=None, *, name):
    R, d = x.shape
    has_add = add is not None
    tr = _row_tile(R, d, 5)

    def body(*refs):
        x_ref, g_ref, dy_ref = refs[:3]
        add_ref = refs[3] if has_add else None
        dx_ref, dg_ref = refs[-2:]
        xv = x_ref[...].astype(F32)
        dyv = dy_ref[...].astype(F32)
        r = lax.rsqrt(jnp.mean(xv * xv, axis=-1, keepdims=True) + EPS)
        xh = xv * r
        dyg = dyv * g_ref[...]
        c = jnp.mean(dyg * xh, axis=-1, keepdims=True)
        dx = r * (dyg - xh * c)
        if has_add:
            dx = dx + add_ref[...]
        dx_ref[...] = dx

        @pl.when(pl.program_id(0) == 0)
        def _():
            dg_ref[...] = jnp.zeros_like(dg_ref)

        dg_ref[...] += jnp.sum(dyv * xh, axis=0, keepdims=True)

    row = pl.BlockSpec((tr, d), lambda i: (i, 0))
    vec = pl.BlockSpec((1, d), lambda i: (0, 0))
    ins = [x, g, dy] + ([add] if has_add else [])
    return pl.pallas_call(
        body, name=name, grid=(R // tr,),
        in_specs=[row, vec, row] + ([row] if has_add else []),
        out_specs=[row, vec],
        out_shape=[jax.ShapeDtypeStruct((R, d), F32), jax.ShapeDtypeStruct((1, d), F32)],
        compiler_params=_params(("arbitrary",)),
    )(*ins)


_NT = (((1,), (1,)), ((), ()))
_TN = (((0,), (0,)), ((), ()))


def _dot(a, b, dims=(((1,), (0,)), ((), ()))):
    return lax.dot_general(a, b, dims, preferred_element_type=F32)


def _log_sigmoid_pair(z):
    sp = jnp.log(1.0 + jnp.exp(-jnp.abs(z)))
    return jnp.minimum(z, 0.0) - sp, jnp.minimum(-z, 0.0) - sp


LANES = 128
_LOW = -3e38


def _lane_masks(hd, rows):
    if hd == LANES:
        return [None]
    lane = lax.broadcasted_iota(jnp.int32, (rows, LANES), 1)
    return [(lane >= hh * hd) & (lane < (hh + 1) * hd) for hh in range(LANES // hd)]


def _keep(t, m):
    return t if m is None else jnp.where(m, t, 0.0)


def _merge(parts, masks):
    out = parts[-1]
    for p, m in zip(parts[-2::-1], masks[-2::-1]):
        out = jnp.where(m, p, out)
    return out


def _row_value(t, m):
    return jnp.max(t if m is None else jnp.where(m, t, _LOW), axis=1, keepdims=True)


def _cols(tq, off):
    return pl.BlockSpec((tq, LANES), lambda g, i: (i, off + g))


def _cols_all(rows, off):
    return pl.BlockSpec((rows, LANES), lambda g, i: (0, off + g))


SCAN_BLOCK = 256


def _tri(kind, cols):
    n = min(SCAN_BLOCK, cols)
    r = lax.broadcasted_iota(jnp.int32, (n, n), 0)
    c = lax.broadcasted_iota(jnp.int32, (n, n), 1)
    return ((r > c) if kind == "after" else (r < c)).astype(BF16)


def _scan_cols(x, tri, reverse):
    cols = x.shape[1]
    cb = min(SCAN_BLOCK, cols)
    assert cols % cb == 0 and tri.shape == (cb, cb)
    nb = cols // cb
    blocks = [x[:, b * cb:(b + 1) * cb] for b in range(nb)]
    outs, carry = [None] * nb, None
    for b in (reversed(range(nb)) if reverse else range(nb)):
        y = _dot(blocks[b].astype(BF16), tri)
        outs[b] = y if carry is None else y + carry
        s = jnp.sum(blocks[b], axis=1, keepdims=True)
        carry = s if carry is None else carry + s
    return (outs[0] if nb == 1 else jnp.concatenate(outs, axis=1)), carry


def _softplus_parts(z):
    pos = jnp.maximum(z, 0.0) + jnp.log(1.0 + jnp.exp(-jnp.abs(z)))
    return pos, z - pos


class _ChipExchange:
    def __init__(self, kind, ins):
        assert kind in ("gather", "scatter")
        self.kind, self.ins = kind, list(ins)
        lead = (lambda s: (N_CHIPS,) + s) if kind == "gather" else (lambda s: (3,) + s[1:])
        self.out_shape = [jax.ShapeDtypeStruct(lead(a.shape), a.dtype) for a in ins]
        n = 3 * len(ins)
        self.scratch = [pltpu.SemaphoreType.DMA((n,)), pltpu.SemaphoreType.DMA((n,))]

    def _copies(self, in_refs, out_refs, sems, landing):
        x, y, c, chips = _mesh_place()
        me = 2 * x + y
        out = []
        for w in range(len(self.ins)):
            for j, chip in enumerate(chips):
                peer = 2 * chip[0] + chip[1]
                if self.kind == "gather":
                    src, dst, land = in_refs[w].at[c], out_refs[w].at[me, c], out_refs[w].at[peer, c]
                else:
                    src, dst, land = in_refs[w].at[peer], out_refs[w].at[j], out_refs[w].at[j]
                if landing:
                    src, dst = land, land
                out.append(_remote(src, dst, sems, 3 * w + j, (chip[0], chip[1], c)))
        return out

    def start(self, in_refs, out_refs, sems):
        for cp in self._copies(in_refs, out_refs, sems, False):
            cp.start()

    def finish(self, in_refs, out_refs, sems):
        for cp in self._copies(in_refs, out_refs, sems, True):
            cp.wait_recv()
        for cp in self._copies(in_refs, out_refs, sems, False):
            cp.wait_send()


def _carry(comm, n_in, n_out, first, last, refs):
    if comm is None:
        return refs[:n_in], refs[n_in:n_in + n_out], (lambda: None)
    a, b = len(comm.ins), len(comm.out_shape)
    ins, c_in = refs[:n_in], refs[n_in:n_in + a]
    outs, c_out = refs[n_in + a:n_in + a + n_out], refs[n_in + a + n_out:n_in + a + n_out + b]
    sems = refs[n_in + a + n_out + b:n_in + a + n_out + b + 2]
    pl.when(first)(lambda: comm.start(c_in, c_out, sems))
    return ins, outs, (lambda: pl.when(last)(lambda: comm.finish(c_in, c_out, sems)))


def _sbl_fwd(q, k, v, *, width, hd, name, tq=256, comm=None):
    (qa, qo), (ka, ko), (va, vo) = q, k, v
    S = qa.shape[0]
    tq = _tile(S, tq)
    tk = tq
    scale = hd ** -0.5
    n_g, n_q = width // LANES, S // tq

    def body(*refs):
        qi = pl.program_id(1)
        gi = pl.program_id(0)
        got = _carry(comm, 3, 2, (gi == 0) & (qi == 0), (gi == n_g - 1) & (qi == n_q - 1), refs)
        (q_ref, k_ref, v_ref), (o_ref, tot_ref) = got[0], got[1]
        masks = _lane_masks(hd, tq)
        qs = q_ref[...].astype(F32) * scale
        qm = [_keep(qs, m).astype(BF16) for m in masks]
        strict = lax.broadcasted_iota(jnp.int32, (tq, tk), 1) < lax.broadcasted_iota(jnp.int32, (tq, tk), 0)
        later = _tri("after", tk)

        def tile(kb, carry, diag):
            ks = pl.multiple_of(kb * tk, tk)
            kv = k_ref[pl.ds(ks, tk), :].astype(BF16)
            vv = v_ref[pl.ds(ks, tk), :].astype(BF16)
            out = []
            for hh in range(len(masks)):
                acc, c_pos = carry[2 * hh], carry[2 * hh + 1]
                pos, ls = _softplus_parts(_dot(qm[hh], kv, _NT))
                if diag:
                    pos = jnp.where(strict, pos, 0.0)
                pos_after, pos_all = _scan_cols(pos, later, True)
                w = jnp.exp(ls - (pos_after + c_pos))
                if diag:
                    w = jnp.where(strict, w, 0.0)
                out += [acc + _dot(w.astype(BF16), vv), c_pos + pos_all]
            return tuple(out)

        init = (jnp.zeros((tq, LANES), F32), jnp.zeros((tq, 1), F32)) * len(masks)
        carry = tile(qi, init, True)
        carry = lax.fori_loop(0, qi, lambda i, c: tile(qi - 1 - i, c, False), carry)
        o_ref[...] = _merge(carry[0::2], masks).astype(o_ref.dtype)
        tot_ref[...] = _merge([jnp.broadcast_to(-c, (tq, LANES)) for c in carry[1::2]], masks)
        got[2]()

    c_ins, c_in_specs, c_out_specs, c_out_shape, c_scratch = _comm_args(comm)
    return pl.pallas_call(
        body, name=name, grid=(n_g, n_q),
        in_specs=[_cols(tq, qo), _cols_all(S, ko), _cols_all(S, vo)] + c_in_specs,
        out_specs=[_cols(tq, 0), _cols(tq, 0)] + c_out_specs,
        out_shape=[jax.ShapeDtypeStruct((S, width), BF16), jax.ShapeDtypeStruct((S, width), F32)] + c_out_shape,
        scratch_shapes=c_scratch,
        compiler_params=_params(("arbitrary", "arbitrary")),
    )(qa, ka, va, *c_ins)


def _comm_args(comm):
    if comm is None:
        return [], [], [], [], []
    return comm.ins, [_ANY] * len(comm.ins), [_ANY] * len(comm.out_shape), comm.out_shape, comm.scratch


def _sbl_bwd(q, k, v, do, tot, *, width, hd, name, tq=256, comm=None):
    (qa, qo), (ka, ko), (va, vo) = q, k, v
    S = qa.shape[0]
    tq = _tile(S, tq)
    tk = tq
    scale = hd ** -0.5
    n_g, n_q = width // LANES, S // tq

    def body(*refs):
        qi = pl.program_id(1)
        gi = pl.program_id(0)
        got = _carry(comm, 5, 3, (gi == 0) & (qi == 0), (gi == n_g - 1) & (qi == n_q - 1), refs)
        (q_ref, k_ref, v_ref, do_ref, tot_ref), (dq_ref, dk_ref, dv_ref) = got[0], got[1]

        @pl.when(qi == 0)
        def _():
            dk_ref[...] = jnp.zeros_like(dk_ref)
            dv_ref[...] = jnp.zeros_like(dv_ref)

        masks = _lane_masks(hd, tq)
        qs = q_ref[...].astype(F32) * scale
        qm = [_keep(qs, m).astype(BF16) for m in masks]
        dov = [_keep(do_ref[...], m).astype(BF16) for m in masks]
        rest = [-_row_value(tot_ref[...], m) for m in masks]
        strict = lax.broadcasted_iota(jnp.int32, (tq, tk), 1) < lax.broadcasted_iota(jnp.int32, (tq, tk), 0)
        later, before = _tri("after", tk), _tri("before", tk)

        def tile(kb, carry, diag):
            ks = pl.multiple_of(kb * tk, tk)
            kv = k_ref[pl.ds(ks, tk), :].astype(BF16)
            vv = v_ref[pl.ds(ks, tk), :].astype(BF16)
            out = []
            dk_t, dv_t = None, None
            for hh in range(len(masks)):
                dq, c_pos, c_g = carry[3 * hh:3 * hh + 3]
                pos, ls = _softplus_parts(_dot(qm[hh], kv, _NT))
                if diag:
                    pos = jnp.where(strict, pos, 0.0)
                pos_after, pos_all = _scan_cols(pos, later, True)
                c_pos = c_pos + pos_all
                w = jnp.exp(ls - (pos_after + (rest[hh] - c_pos)))
                if diag:
                    w = jnp.where(strict, w, 0.0)
                g = _dot(dov[hh], vv, _NT) * w
                g_before, g_all = _scan_cols(g, before, False)
                g_before = g_before + c_g
                dz = g - jnp.exp(ls) * (g + g_before)
                if diag:
                    dz = jnp.where(strict, dz, 0.0)
                dzb = dz.astype(BF16)
                dk_h = _dot(dzb, qm[hh], _TN)
                dv_h = _dot(w.astype(BF16), dov[hh], _TN)
                dk_t = dk_h if dk_t is None else dk_t + dk_h
                dv_t = dv_h if dv_t is None else dv_t + dv_h
                out += [dq + _dot(dzb, kv), c_pos, c_g + g_all]
            dk_ref[pl.ds(ks, tk), :] += dk_t
            dv_ref[pl.ds(ks, tk), :] += dv_t
            return tuple(out)

        zero = jnp.zeros((tq, 1), F32)
        init = (jnp.zeros((tq, LANES), F32), zero, zero) * len(masks)
        carry = lax.fori_loop(0, qi, lambda kb, c: tile(kb, c, False), init)
        carry = tile(qi, carry, True)
        dq_ref[...] = _merge(carry[0::3], masks) * scale
        got[2]()

    full = jax.ShapeDtypeStruct((S, width), F32)
    c_ins, c_in_specs, c_out_specs, c_out_shape, c_scratch = _comm_args(comm)
    return pl.pallas_call(
        body, name=name, grid=(n_g, n_q),
        in_specs=[_cols(tq, qo), _cols_all(S, ko), _cols_all(S, vo), _cols(tq, do[1]), _cols(tq, tot[1])] + c_in_specs,
        out_specs=[_cols(tq, 0), _cols_all(S, 0), _cols_all(S, 0)] + c_out_specs,
        out_shape=[full, full, full] + c_out_shape,
        scratch_shapes=c_scratch,
        compiler_params=_params(("arbitrary", "arbitrary")),
    )(qa, ka, va, do[0], tot[0], *c_ins)


def _sml_fwd(q, k, v, bias=None, *, width, hd, causal, name, tq=256, tk=256, comm=None):
    (qa, qo), (ka, ko), (va, vo) = q, k, v
    S, Sk = qa.shape[0], ka.shape[0]
    tq, tk = _tile(S, tq), _tile(Sk, tk)
    if causal:
        assert tq == tk and S == Sk
    nk = Sk // tk
    hpg = LANES // hd
    scale = hd ** -0.5
    has_bias = bias is not None
    n_g, n_q = width // LANES, S // tq

    def body(*all_refs):
        qi, gi = pl.program_id(1), pl.program_id(0)
        got = _carry(comm, 5 if has_bias else 3, 2, (gi == 0) & (qi == 0), (gi == n_g - 1) & (qi == n_q - 1), all_refs)
        refs = tuple(got[0]) + tuple(got[1])
        q_ref, k_ref, v_ref = refs[:3]
        o_ref, lse_ref = refs[-2:]
        masks = _lane_masks(hd, tq)
        qs = q_ref[...].astype(F32) * scale
        qm = [_keep(qs, m).astype(BF16) for m in masks]
        allowed = lax.broadcasted_iota(jnp.int32, (tq, tk), 1) <= lax.broadcasted_iota(jnp.int32, (tq, tk), 0)

        def tile(kb, carry, diag):
            ks = pl.multiple_of(kb * tk, tk)
            kv = k_ref[pl.ds(ks, tk), :].astype(BF16)
            vv = v_ref[pl.ds(ks, tk), :].astype(BF16)
            out = []
            for hh in range(hpg):
                m, l, acc = carry[3 * hh:3 * hh + 3]
                z = _dot(qm[hh], kv, _NT)
                if has_bias:
                    z = z + refs[3][hh] - refs[4][hh, kb]
                if diag:
                    z = jnp.where(allowed, z, NEG_INF)
                m2 = jnp.maximum(m, jnp.max(z, axis=1, keepdims=True))
                p = jnp.exp(z - m2)
                alpha = jnp.exp(m - m2)
                out += [m2, alpha * l + jnp.sum(p, axis=1, keepdims=True), alpha * acc + _dot(p.astype(BF16), vv)]
            return tuple(out)

        init = (jnp.full((tq, 1), NEG_INF, F32), jnp.zeros((tq, 1), F32), jnp.zeros((tq, LANES), F32)) * hpg
        if causal:
            carry = lax.fori_loop(0, qi, lambda kb, c: tile(kb, c, False), init)
            carry = tile(qi, carry, True)
        else:
            carry = lax.fori_loop(0, nk, lambda kb, c: tile(kb, c, False), init)
        o_ref[...] = _merge([acc / l for l, acc in zip(carry[1::3], carry[2::3])], masks).astype(o_ref.dtype)
        lse_ref[...] = _merge([jnp.broadcast_to(m + jnp.log(l), (tq, LANES)) for m, l in zip(carry[0::3], carry[1::3])], masks)
        got[2]()

    in_specs = [_cols(tq, qo), _cols_all(Sk, ko), _cols_all(Sk, vo)]
    ins = [qa, ka, va]
    if has_bias:
        in_specs += [pl.BlockSpec((hpg, tq, 1), lambda g, i: (g, i, 0)),
                     pl.BlockSpec((hpg, nk, 1, tk), lambda g, i: (g, 0, 0, 0))]
        ins += list(bias)
    c_ins, c_in_specs, c_out_specs, c_out_shape, c_scratch = _comm_args(comm)
    return pl.pallas_call(
        body, name=name, grid=(n_g, n_q),
        in_specs=in_specs + c_in_specs, out_specs=[_cols(tq, 0), _cols(tq, 0)] + c_out_specs,
        out_shape=[jax.ShapeDtypeStruct((S, width), BF16), jax.ShapeDtypeStruct((S, width), F32)] + c_out_shape,
        scratch_shapes=c_scratch,
        compiler_params=_params(("arbitrary", "arbitrary") if comm is not None else ("parallel", "arbitrary")),
    )(*ins, *c_ins)


def _sml_bwd(q, k, v, o, lse, do, bias=None, *, width, hd, causal, name, tq=256, tk=256, comm=None):
    (qa, qo), (ka, ko), (va, vo) = q, k, v
    S, Sk = qa.shape[0], ka.shape[0]
    tq, tk = _tile(S, tq), _tile(Sk, tk)
    nk = Sk // tk
    hpg = LANES // hd
    scale = hd ** -0.5
    has_bias = bias is not None
    n_in = 8 if has_bias else 6
    n_g, n_q = width // LANES, S // tq

    def body(*all_refs):
        qi, gi = pl.program_id(1), pl.program_id(0)
        got = _carry(comm, n_in, 5 if has_bias else 3, (gi == 0) & (qi == 0), (gi == n_g - 1) & (qi == n_q - 1), all_refs)
        refs = tuple(got[0]) + tuple(got[1])
        q_ref, k_ref, v_ref, o_ref, lse_ref, do_ref = refs[:6]
        dq_ref, dk_ref, dv_ref = refs[n_in:n_in + 3]

        @pl.when(qi == 0)
        def _():
            dk_ref[...] = jnp.zeros_like(dk_ref)
            dv_ref[...] = jnp.zeros_like(dv_ref)
            if has_bias:
                refs[n_in + 4][...] = jnp.zeros_like(refs[n_in + 4])

        masks = _lane_masks(hd, tq)
        qs = q_ref[...].astype(F32) * scale
        qm = [_keep(qs, m).astype(BF16) for m in masks]
        do32 = do_ref[...]
        dov = [_keep(do32, m).astype(BF16) for m in masks]
        prod = do32 * o_ref[...].astype(F32)
        delta = [jnp.sum(_keep(prod, m), axis=1, keepdims=True) for m in masks]
        lses = [_row_value(lse_ref[...], m) for m in masks]
        allowed = lax.broadcasted_iota(jnp.int32, (tq, tk), 1) <= lax.broadcasted_iota(jnp.int32, (tq, tk), 0)

        def tile(kb, carry, diag):
            ks = pl.multiple_of(kb * tk, tk)
            kv = k_ref[pl.ds(ks, tk), :].astype(BF16)
            vv = v_ref[pl.ds(ks, tk), :].astype(BF16)
            out = []
            dk_t, dv_t = None, None
            for hh in range(hpg):
                dq, db_row = carry[2 * hh:2 * hh + 2]
                z = _dot(qm[hh], kv, _NT)
                if has_bias:
                    z = z + refs[6][hh] - refs[7][hh, kb]
                p = jnp.exp(z - lses[hh])
                if diag:
                    p = jnp.where(allowed, p, 0.0)
                dz = p * (_dot(dov[hh], vv, _NT) - delta[hh])
                dzb = dz.astype(BF16)
                dk_h = _dot(dzb, qm[hh], _TN)
                dv_h = _dot(p.astype(BF16), dov[hh], _TN)
                dk_t = dk_h if dk_t is None else dk_t + dk_h
                dv_t = dv_h if dv_t is None else dv_t + dv_h
                if has_bias:
                    db_row = db_row + jnp.sum(dz, axis=1, keepdims=True)
                    refs[n_in + 4][hh, kb] += jnp.sum(dz, axis=0, keepdims=True)
                out += [dq + _dot(dzb, kv), db_row]
            dk_ref[pl.ds(ks, tk), :] += dk_t
            dv_ref[pl.ds(ks, tk), :] += dv_t
            return tuple(out)

        init = (jnp.zeros((tq, LANES), F32), jnp.zeros((tq, 1), F32)) * hpg
        if causal:
            carry = lax.fori_loop(0, qi, lambda kb, c: tile(kb, c, False), init)
            carry = tile(qi, carry, True)
        else:
            carry = lax.fori_loop(0, nk, lambda kb, c: tile(kb, c, False), init)
        dq_ref[...] = _merge(carry[0::2], masks) * scale
        if has_bias:
            for hh in range(hpg):
                refs[n_in + 3][hh] = carry[2 * hh + 1]
        got[2]()

    in_specs = [_cols(tq, qo), _cols_all(Sk, ko), _cols_all(Sk, vo), _cols(tq, o[1]), _cols(tq, lse[1]), _cols(tq, do[1])]
    ins = [qa, ka, va, o[0], lse[0], do[0]]
    out_specs = [_cols(tq, 0), _cols_all(Sk, 0), _cols_all(Sk, 0)]
    out_shape = [jax.ShapeDtypeStruct((S, width), F32), jax.ShapeDtypeStruct((Sk, width), F32),
                 jax.ShapeDtypeStruct((Sk, width), F32)]
    if has_bias:
        rspec = pl.BlockSpec((hpg, tq, 1), lambda g, i: (g, i, 0))
        cspec = pl.BlockSpec((hpg, nk, 1, tk), lambda g, i: (g, 0, 0, 0))
        in_specs += [rspec, cspec]
        ins += list(bias)
        out_specs += [rspec, cspec]
        n_heads = width // hd
        out_shape += [jax.ShapeDtypeStruct((n_heads, S, 1), F32), jax.ShapeDtypeStruct((n_heads, nk, 1, tk), F32)]
    c_ins, c_in_specs, c_out_specs, c_out_shape, c_scratch = _comm_args(comm)
    return pl.pallas_call(
        body, name=name, grid=(n_g, n_q),
        in_specs=in_specs + c_in_specs, out_specs=out_specs + c_out_specs, out_shape=out_shape + c_out_shape,
        scratch_shapes=c_scratch,
        compiler_params=_params(("arbitrary", "arbitrary") if comm is not None else ("parallel", "arbitrary")),
    )(*ins, *c_ins)


def _head_sums(t, masks):
    sums = [jnp.sum(_keep(t, m), axis=1, keepdims=True) for m in masks]
    return _merge([jnp.broadcast_to(s, t.shape) for s in sums], masks) if len(masks) > 1 else sums[0]


def _hnorm_fwd(x, g_lanes, *, width, hd, name, tr=1024):
    xa, xo = x
    R = xa.shape[0]
    tr = _tile(R, tr, align=16)
    n_blk = width // LANES

    def body(x_ref, g_ref, o_ref):
        masks = _lane_masks(hd, tr)
        for j in range(n_blk):
            sl = slice(j * LANES, (j + 1) * LANES)
            xv = x_ref[:, sl].astype(F32)
            r = lax.rsqrt(_head_sums(xv * xv, masks) * (1.0 / hd) + EPS)
            o_ref[:, sl] = (xv * r * g_ref[...]).astype(o_ref.dtype)

    assert (xo * LANES) % width == 0
    return pl.pallas_call(
        body, name=name, grid=(R // tr,),
        in_specs=[pl.BlockSpec((tr, width), lambda i: (i, xo * LANES // width)), pl.BlockSpec((1, LANES), lambda i: (0, 0))],
        out_specs=pl.BlockSpec((tr, width), lambda i: (i, 0)),
        out_shape=jax.ShapeDtypeStruct((R, width), BF16),
        compiler_params=_params(("parallel",)),
    )(xa, g_lanes)


def _hnorm_bwd(x, g_lanes, dy, *, width, hd, name, tr=1024):
    xa, xo = x
    R = xa.shape[0]
    tr = _tile(R, tr, align=16)
    n_blk = width // LANES

    def body(x_ref, g_ref, dy_ref, dx_ref, dg_ref):
        masks = _lane_masks(hd, tr)
        dg = jnp.zeros((1, LANES), F32)
        for j in range(n_blk):
            sl = slice(j * LANES, (j + 1) * LANES)
            xv = x_ref[:, sl].astype(F32)
            dyv = dy_ref[:, sl].astype(F32)
            r = lax.rsqrt(_head_sums(xv * xv, masks) * (1.0 / hd) + EPS)
            xh = xv * r
            dyg = dyv * g_ref[...]
            c = _head_sums(dyg * xh, masks) * (1.0 / hd)
            dx_ref[:, sl] = (r * (dyg - xh * c)).astype(dx_ref.dtype)
            dg = dg + jnp.sum(dyv * xh, axis=0, keepdims=True)
        if hd * 2 == LANES:
            dg8 = jnp.broadcast_to(dg, (8, LANES))
            dg = (dg8 + pltpu.roll(dg8, shift=hd, axis=1))[0:1]
        else:
            assert hd == LANES

        @pl.when(pl.program_id(0) == 0)
        def _():
            dg_ref[...] = jnp.zeros_like(dg_ref)

        dg_ref[...] += dg

    assert (xo * LANES) % width == 0
    return pl.pallas_call(
        body, name=name, grid=(R // tr,),
        in_specs=[pl.BlockSpec((tr, width), lambda i: (i, xo * LANES // width)), pl.BlockSpec((1, LANES), lambda i: (0, 0)),
                  pl.BlockSpec((tr, width), lambda i: (i, 0))],
        out_specs=[pl.BlockSpec((tr, width), lambda i: (i, 0)), pl.BlockSpec((1, LANES), lambda i: (0, 0))],
        out_shape=[jax.ShapeDtypeStruct((R, width), BF16), jax.ShapeDtypeStruct((1, LANES), F32)],
        compiler_params=_params(("arbitrary",)),
    )(xa, g_lanes, dy)


def _split3_dot(x, tri):
    a = x.astype(BF16)
    r = x - a.astype(F32)
    b = r.astype(BF16)
    c = (r - b.astype(F32)).astype(BF16)
    return _dot(a, tri) + _dot(b, tri) + _dot(c, tri)


def _forget_fwd(logit_t, b_col, *, name, blk=512):
    H, S = logit_t.shape
    blk = _tile(S, blk)

    def body(l_ref, b_ref, f_ref):
        r_i = lax.broadcasted_iota(jnp.int32, (blk, blk), 0)
        c_i = lax.broadcasted_iota(jnp.int32, (blk, blk), 1)
        upto = (r_i <= c_i).astype(BF16)
        carry = jnp.zeros((H, 1), F32)
        for j in range(S // blk):
            u = l_ref[:, j * blk:(j + 1) * blk] + b_ref[...]
            lf, _ = _log_sigmoid_pair(u)
            f_ref[:, j * blk:(j + 1) * blk] = _split3_dot(lf, upto) + carry
            carry = carry + jnp.sum(lf, axis=1, keepdims=True)

    return pl.pallas_call(
        body, name=name,
        out_shape=jax.ShapeDtypeStruct((H, S), F32),
        compiler_params=pltpu.CompilerParams(vmem_limit_bytes=VMEM_LIMIT),
    )(logit_t, b_col)


def _forget_bwd(logit_t, b_col, d_f, *, name, blk=512):
    H, S = logit_t.shape
    blk = _tile(S, blk)

    def body(l_ref, b_ref, df_ref, dl_ref, db_ref):
        r_i = lax.broadcasted_iota(jnp.int32, (blk, blk), 0)
        c_i = lax.broadcasted_iota(jnp.int32, (blk, blk), 1)
        fromon = (r_i >= c_i).astype(BF16)
        carry = jnp.zeros((H, 1), F32)
        db = jnp.zeros((H, 1), F32)
        for j in reversed(range(S // blk)):
            sl = slice(j * blk, (j + 1) * blk)
            dfv = df_ref[:, sl]
            d_lf = _split3_dot(dfv, fromon) + carry
            carry = carry + jnp.sum(dfv, axis=1, keepdims=True)
            u = l_ref[:, sl] + b_ref[...]
            _, lsn = _log_sigmoid_pair(u)
            dl = d_lf * jnp.exp(lsn)
            dl_ref[:, sl] = dl
            db = db + jnp.sum(dl, axis=1, keepdims=True)
        db_ref[...] = db

    return pl.pallas_call(
        body, name=name,
        out_shape=[jax.ShapeDtypeStruct((H, S), F32), jax.ShapeDtypeStruct((H, 1), F32)],
        compiler_params=pltpu.CompilerParams(vmem_limit_bytes=VMEM_LIMIT),
    )(logit_t, b_col, d_f)


def _sigmoid(t):
    return 1.0 / (1.0 + jnp.exp(-t))


def _gate_fwd(o3, w3, proj, D, *, name, tm=512):
    S = proj.shape[0]
    tm = _tile(S, tm)

    def body(o0, o1, o2, w0, w1, w2, g0, g1, g2, out_ref):
        acc = None
        for o_ref, w_ref, g_ref in ((o0, w0, g0), (o1, w1, g1), (o2, w2, g2)):
            t = _sigmoid(g_ref[...]) * _dot(o_ref[...], w_ref[...])
            acc = t if acc is None else acc + t
        out_ref[...] = acc.astype(out_ref.dtype)

    ospec = lambda d: pl.BlockSpec((tm, d), lambda i: (i, 0))
    wspec = lambda w: pl.BlockSpec(w.shape, lambda i: (0, 0))
    gspec = lambda j: pl.BlockSpec((tm, D), lambda i: (i, j))
    return pl.pallas_call(
        body, name=name, grid=(S // tm,),
        in_specs=[ospec(o.shape[1]) for o in o3] + [wspec(w) for w in w3] + [gspec(j) for j in range(3)],
        out_specs=pl.BlockSpec((tm, D), lambda i: (i, 0)),
        out_shape=jax.ShapeDtypeStruct((S, D), BF16),
        compiler_params=_params(("parallel",)),
    )(*o3, *w3, proj, proj, proj)


def _gate_bwd(o3, w3, proj, dmerged, D, *, name, tm=512):
    S = proj.shape[0]
    tm = _tile(S, tm)

    def body(o0, o1, o2, w0, w1, w2, g0, g1, g2, dm_ref, dg_ref, db0, db1, db2, do0, do1, do2):
        dm = dm_ref[...]
        for j, (o_ref, w_ref, g_ref, db_ref, do_ref) in enumerate(
                ((o0, w0, g0, db0, do0), (o1, w1, g1, db1, do1), (o2, w2, g2, db2, do2))):
            s = _sigmoid(g_ref[...])
            br = _dot(o_ref[...], w_ref[...])
            dg_ref[:, j * D:(j + 1) * D] = (dm * br * s * (1.0 - s)).astype(dg_ref.dtype)
            dbr = (dm * s).astype(BF16)
            db_ref[...] = dbr
            do_ref[...] = _dot(dbr, w_ref[...], _NT)

    ospec = lambda d: pl.BlockSpec((tm, d), lambda i: (i, 0))
    wspec = lambda w: pl.BlockSpec(w.shape, lambda i: (0, 0))
    gspec = lambda j: pl.BlockSpec((tm, D), lambda i: (i, j))
    dspec = pl.BlockSpec((tm, D), lambda i: (i, 0))
    return pl.pallas_call(
        body, name=name, grid=(S // tm,),
        in_specs=[ospec(o.shape[1]) for o in o3] + [wspec(w) for w in w3] + [gspec(j) for j in range(3)] + [dspec],
        out_specs=[pl.BlockSpec((tm, 3 * D), lambda i: (i, 0))] + [dspec] * 3 + [ospec(o.shape[1]) for o in o3],
        out_shape=[jax.ShapeDtypeStruct((S, proj.shape[1]), BF16)] + [jax.ShapeDtypeStruct((S, D), BF16)] * 3
        + [jax.ShapeDtypeStruct((S, o.shape[1]), F32) for o in o3],
        compiler_params=_params(("parallel",)),
    )(*o3, *w3, proj, proj, proj, dmerged)


def _pair_sum(stacked, got, core, *, name):
    n, _, r, c = stacked.shape
    tr = _row_tile(r, c, 3)

    def body(core_ref, a_ref, b_ref, o_ref):
        o_ref[...] = (a_ref[0].astype(F32) + b_ref[...].astype(F32)).astype(o_ref.dtype)

    spec = pl.BlockSpec((1, tr, c), lambda s, i, core_ref: (s, i, 0))
    return pl.pallas_call(
        body, name=name,
        grid_spec=pltpu.PrefetchScalarGridSpec(
            num_scalar_prefetch=1, grid=(n, r // tr),
            in_specs=[pl.BlockSpec((1, 1, tr, c), lambda s, i, core_ref: (s, core_ref[0], i, 0)), spec],
            out_specs=spec),
        out_shape=jax.ShapeDtypeStruct((n, r, c), BF16),
        compiler_params=_params(("parallel", "parallel")),
    )(core.astype(jnp.int32).reshape(1), stacked, got)


def _chip_sum(parts, got, chip, *, name):
    _, r, c = parts.shape
    tr = _row_tile(r, c, 6)

    def body(chip_ref, p_ref, q0_ref, q1_ref, q2_ref, o_ref):
        o_ref[...] = ((p_ref[0].astype(F32) + q0_ref[0].astype(F32)) + q1_ref[0].astype(F32)) + q2_ref[0].astype(F32)

    from_chip = lambda j: pl.BlockSpec((1, tr, c), lambda i, chip_ref: (j, i, 0))
    return pl.pallas_call(
        body, name=name,
        grid_spec=pltpu.PrefetchScalarGridSpec(
            num_scalar_prefetch=1, grid=(r // tr,),
            in_specs=[pl.BlockSpec((1, tr, c), lambda i, chip_ref: (chip_ref[0], i, 0))] + [from_chip(j) for j in range(3)],
            out_specs=pl.BlockSpec((tr, c), lambda i, chip_ref: (i, 0))),
        out_shape=jax.ShapeDtypeStruct((r, c), F32),
        compiler_params=_params(("parallel",)),
    )(chip.astype(jnp.int32).reshape(1), parts, got, got, got)


def _adamw_math(w, g, m, v):
    m2 = ADAM_B1 * m + (1.0 - ADAM_B1) * g
    v2 = ADAM_B2 * v + (1.0 - ADAM_B2) * (g * g)
    m_hat = m2 / (1.0 - ADAM_B1 ** ADAM_STEP)
    v_hat = v2 / (1.0 - ADAM_B2 ** ADAM_STEP)
    delta = -ADAM_LR * (m_hat / (jnp.sqrt(v_hat) + ADAM_EPS) + ADAM_WD * w)
    return delta, m2, v2


def _adamw(w, g, m, v, *, name):
    return _ew(_adamw_math, [w, g, m, v], (F32, F32, F32), name=name)


def _adamw_small(w, parts, m, v, *, name):
    n = parts.shape[0]

    def body(w_ref, p_ref, m_ref, v_ref, g_ref, d_ref, m2_ref, v2_ref):
        g = p_ref[0]
        for i in range(1, n):
            g = g + p_ref[i]
        g_ref[...] = g
        d_ref[...], m2_ref[...], v2_ref[...] = _adamw_math(w_ref[...], g, m_ref[...], v_ref[...])

    shp = jax.ShapeDtypeStruct(w.shape, F32)
    return pl.pallas_call(body, name=name, out_shape=[shp] * 4)(w, parts, m, v)


_ANY = pl.BlockSpec(memory_space=pl.ANY)


def _mesh_place():
    x, y, c = lax.axis_index("x"), lax.axis_index("y"), lax.axis_index("c")
    chips = [(1 - x, y), (x, 1 - y), (1 - x, 1 - y)]
    return x, y, c, chips


def _remote(src, dst, sems, i, to):
    send_sems, recv_sems = sems
    return pltpu.make_async_remote_copy(src_ref=src, dst_ref=dst, send_sem=send_sems.at[i], recv_sem=recv_sems.at[i],
                                        device_id=to, device_id_type=MESH_ID)


def _gather_weights(shards, *, name):
    n = len(shards)

    def body(*refs):
        ins, outs = refs[:n], refs[n:2 * n]
        sems = refs[2 * n:2 * n + 2]
        x, y, c, chips = _mesh_place()
        me = 2 * x + y
        sibling = (x, y, 1 - c)
        sent = []
        for w in range(n):
            for j, chip in enumerate(chips):
                cp = _remote(ins[w].at[c], outs[w].at[me, c], sems, 6 * w + j, (chip[0], chip[1], c))
                cp.start()
                sent.append(cp)
        for w in range(n):
            for j, chip in enumerate(chips):
                got = outs[w].at[2 * chip[0] + chip[1], c]
                _remote(got, got, sems, 6 * w + j, sibling).wait_recv()
                cp = _remote(got, got, sems, 6 * w + 3 + j, sibling)
                cp.start()
                sent.append(cp)
        for w in range(n):
            for j, chip in enumerate(chips):
                got = outs[w].at[2 * chip[0] + chip[1], 1 - c]
                _remote(got, got, sems, 6 * w + 3 + j, sibling).wait_recv()
        for cp in sent:
            cp.wait_send()

    outs = pl.pallas_call(
        body, name=name,
        in_specs=[_ANY] * n, out_specs=[_ANY] * n,
        out_shape=[jax.ShapeDtypeStruct((N_CHIPS,) + s.shape, s.dtype) for s in shards],
        scratch_shapes=[pltpu.SemaphoreType.DMA((6 * n,)), pltpu.SemaphoreType.DMA((6 * n,))],
    )(*shards)
    me = 2 * lax.axis_index("x") + lax.axis_index("y")
    return [lax.dynamic_update_index_in_dim(o, s, me, 0) for o, s in zip(outs, shards)]


def _forward_halves(gathered, *, name):
    n = len(gathered)

    def body(*refs):
        ins, outs = refs[:n], refs[n:2 * n]
        sems = refs[2 * n:2 * n + 2]
        x, y, c, chips = _mesh_place()
        sibling = (x, y, 1 - c)
        sent = []
        for w in range(n):
            for j, chip in enumerate(chips):
                peer = 2 * chip[0] + chip[1]
                cp = _remote(ins[w].at[peer, c], outs[w].at[peer, c], sems, 3 * w + j, sibling)
                cp.start()
                sent.append(cp)
        for w in range(n):
            for j, chip in enumerate(chips):
                land = outs[w].at[2 * chip[0] + chip[1], 1 - c]
                _remote(land, land, sems, 3 * w + j, sibling).wait_recv()
        for cp in sent:
            cp.wait_send()

    return pl.pallas_call(
        body, name=name,
        in_specs=[_ANY] * n, out_specs=[_ANY] * n,
        out_shape=[jax.ShapeDtypeStruct(g.shape, g.dtype) for g in gathered],
        input_output_aliases={w: w for w in range(n)},
        scratch_shapes=[pltpu.SemaphoreType.DMA((3 * n,)), pltpu.SemaphoreType.DMA((3 * n,))],
    )(*gathered)


def _exchange_siblings(grads, *, name):
    n = len(grads)

    def body(*refs):
        ins, got = refs[:n], refs[n:2 * n]
        sems = refs[2 * n:2 * n + 2]
        x, y, c, _ = _mesh_place()
        sibling = (x, y, 1 - c)
        sent = []
        for w in range(n):
            for s in range(N_CHIPS):
                cp = _remote(ins[w].at[s, 1 - c], got[w].at[s], sems, N_CHIPS * w + s, sibling)
                cp.start()
                sent.append(cp)
        for w in range(n):
            for s in range(N_CHIPS):
                _remote(got[w].at[s], got[w].at[s], sems, N_CHIPS * w + s, sibling).wait_recv()
        for cp in sent:
            cp.wait_send()

    n_sem = N_CHIPS * n
    return pl.pallas_call(
        body, name=name,
        in_specs=[_ANY] * n, out_specs=[_ANY] * n,
        out_shape=[jax.ShapeDtypeStruct((N_CHIPS,) + g.shape[2:], g.dtype) for g in grads],
        scratch_shapes=[pltpu.SemaphoreType.DMA((n_sem,)), pltpu.SemaphoreType.DMA((n_sem,))],
    )(*grads)


def _share_halves(halves, small):
    n = len(halves)

    def body(*refs):
        ins, small_ref = refs[:n], refs[n]
        outs, small_out = refs[n + 1:2 * n + 1], refs[2 * n + 1]
        sems = refs[2 * n + 2:2 * n + 4]
        x, y, c, chips = _mesh_place()
        sibling = (x, y, 1 - c)
        me = 4 * x + 2 * y + c
        sent = [_remote(ins[w], outs[w].at[c], sems, w, sibling) for w in range(n)]
        peers = [sibling] + [(ch[0], ch[1], cc) for ch in chips for cc in (c, 1 - c)]
        sent += [_remote(small_ref, small_out.at[me], sems, n + j, peer) for j, peer in enumerate(peers)]
        for cp in sent:
            cp.start()
        for w in range(n):
            _remote(outs[w].at[1 - c], outs[w].at[1 - c], sems, w, sibling).wait_recv()
        for j, peer in enumerate(peers):
            frm = small_out.at[4 * peer[0] + 2 * peer[1] + peer[2]]
            _remote(frm, frm, sems, n + j, peer).wait_recv()
        for cp in sent:
            cp.wait_send()

    n_sem = n + 7
    outs = pl.pallas_call(
        body, name="share_halves",
        in_specs=[_ANY] * (n + 1), out_specs=[_ANY] * (n + 1),
        out_shape=[jax.ShapeDtypeStruct((2,) + h.shape, h.dtype) for h in halves]
        + [jax.ShapeDtypeStruct((8,) + small.shape, small.dtype)],
        scratch_shapes=[pltpu.SemaphoreType.DMA((n_sem,)), pltpu.SemaphoreType.DMA((n_sem,))],
    )(*halves, small)
    c = lax.axis_index("c")
    me = 4 * lax.axis_index("x") + 2 * lax.axis_index("y") + c
    return ([lax.dynamic_update_index_in_dim(o, h, c, 0) for o, h in zip(outs[:n], halves)],
            lax.dynamic_update_index_in_dim(outs[n], small, me, 0))


EARLY = ("w_ff_down", "w_ff_up", "w_out", "w_branch_sb", "w_branch_fox", "w_branch_mem", "w_mem_kv")


def _norm_bwd_tail(dy, x, add, g):
    r = lax.rsqrt(jnp.mean(x * x, axis=-1, keepdims=True) + EPS)
    xh = x * r
    dyg = dy * g
    c = jnp.mean(dyg * xh, axis=-1, keepdims=True)
    return r * (dyg - xh * c) + add, jnp.sum(dy * xh, axis=0, keepdims=True)


def _split(outs, n):
    outs = list(outs) if isinstance(outs, (list, tuple)) else [outs]
    return outs[:n], outs[n:]


def _local_step(x, mem, target, small, W, gather_rest=None, reduce_early=None, reduce_late=None):
    S, D = x.shape
    o_qkv, o_mq, o_f = 3 * D, 3 * D + 2 * 3 * D_SB, 3 * D + 2 * 3 * D_SB + D_MEM
    tq = 512

    g_comms, finish_weights = gather_rest if gather_rest is not None else ([None] * 3, None)
    (proj, h), landed = _split(_mm(x, W["w_in"], name="in_proj", tb=True, tn=768, a_gain=small["g_mix_norm"],
                                   comm=g_comms[0]), 2)
    blk = lambda j: (proj, (o_qkv + j * D_SB) // LANES)
    sb_q, sb_k, sb_v, fx_q, fx_k, fx_v = [blk(j) for j in range(6)]
    m_q = (proj, o_mq // LANES)
    f_logit_t = _mm(W["w_in"][o_f:o_f + ROW_TILE], h, name="forget_logits", tb=True)[:FOX_HEADS]
    b_col = small["b_forget"].reshape(FOX_HEADS, 1)
    lanes = lambda g: jnp.tile(g, (1, LANES // g.shape[1]))
    g_fq, g_fk, g_mq, g_mk = [lanes(small[k]) for k in ("g_fox_q", "g_fox_k", "g_mem_q", "g_mem_k")]

    (o_sb, sb_tot), more = _split(_sbl_fwd(sb_q, sb_k, sb_v, width=D_SB, hd=HD, name="sb_fwd", tq=tq, comm=g_comms[1]), 2)
    landed += more

    fq = _hnorm_fwd(fx_q, g_fq, width=D_FOX, hd=HD, name="fox_q_norm")
    fk = _hnorm_fwd(fx_k, g_fk, width=D_FOX, hd=HD, name="fox_k_norm")
    f_cum = _forget_fwd(f_logit_t, b_col, name="forget_fwd")
    tkf = _tile(S, tq)
    f_bias = (f_cum.reshape(FOX_HEADS, S, 1), f_cum.reshape(FOX_HEADS, S // tkf, 1, tkf))
    (o_fox, fox_lse), more = _split(_sml_fwd((fq, 0), (fk, 0), fx_v, f_bias, width=D_FOX, hd=HD, causal=True,
                                             name="fox_fwd", tq=tq, tk=tq, comm=g_comms[2]), 2)
    landed += more
    if finish_weights is not None:
        W = {**W, **finish_weights(landed)}

    mh = _rmsnorm_fwd(mem, small["g_mem_norm"], BF16, name="mem_norm")
    mkv = _mm(mh, W["w_mem_kv"], name="mem_kv")
    mv = (mkv, D_MEM // LANES)
    mq = _hnorm_fwd(m_q, g_mq, width=D_MEM, hd=MEM_HD, name="mem_q_norm")
    mk = _hnorm_fwd((mkv, 0), g_mk, width=D_MEM, hd=MEM_HD, name="mem_k_norm")
    o_mem, mem_lse = _sml_fwd((mq, 0), (mk, 0), mv, width=D_MEM, hd=MEM_HD, causal=False, name="mem_fwd", tq=tq, tk=256)

    o3 = [o_sb, o_fox, o_mem]
    w3 = [W["w_branch_sb"], W["w_branch_fox"], W["w_branch_mem"]]
    merged = _gate_fwd(o3, w3, proj, D, name="gate_fwd")
    def residual_and_norm(acc, res, g):
        t = res + acc
        return t, t * lax.rsqrt(jnp.mean(t * t, axis=-1, keepdims=True) + EPS) * g

    assert W["w_out"].shape[1] == D
    x1, h2 = _mm(merged, W["w_out"], name="out_proj", extras=(x, small["g_mlp_norm"]), out_dtypes=(F32, BF16),
                 epilogue=residual_and_norm, tn=D)

    def relu2(acc):
        u = jnp.maximum(acc, 0.0)
        return u, u * u

    u, a = _mm(h2, W["w_ff_up"], name="ff_up", out_dtypes=(BF16, BF16), epilogue=relu2)
    def head(acc, res, tgt):
        d = (res + acc - tgt) * (1.0 / D)
        return d, d, jnp.sum(d * d, axis=0, keepdims=True)

    dy, dy16, sq_rows = _mm(a, W["w_ff_down"], name="ff_down", extras=(x1, target), out_dtypes=(F32, BF16),
                            epilogue=head, col_sums=1, tn=512)
    loss = (0.5 * D) * jnp.sum(sq_rows)

    G = {}
    du = _mm(dy16, W["w_ff_down"], name="d_ff_act", tb=True, out_dtypes=(BF16,), extras=(u,),
             epilogue=lambda acc, uu: (acc * (2.0 * uu.astype(F32)),))
    G["w_ff_down"] = _mm(a, dy16, name="d_w_ff_down", ta=True, out_dtypes=(BF16,))
    G["w_ff_up"] = _mm(h2, du, name="d_w_ff_up", ta=True, out_dtypes=(BF16,))
    dx1, dg_rows = _mm(du, W["w_ff_up"], name="d_mlp_in", tb=True, extras=(x1, dy, small["g_mlp_norm"]),
                       epilogue=_norm_bwd_tail, col_sums=1, tm=512, tn=D)
    dg_mlp = jnp.sum(dg_rows, axis=0, keepdims=True)
    dmerged = _mm(dx1, W["w_out"], name="d_merged", tb=True)
    G["w_out"] = _mm(merged, dx1, name="d_w_out", ta=True, out_dtypes=(BF16,))
    dgate, db0, db1, db2, do_sb, do_fox, do_mem = _gate_bwd(o3, w3, proj, dmerged, D, name="gate_bwd")
    for nm, o, db in zip(("w_branch_sb", "w_branch_fox", "w_branch_mem"), o3, (db0, db1, db2)):
        G[nm] = _mm(o, db, name="d_" + nm, ta=True, out_dtypes=(BF16,))

    dmq_n, dmk_n, dmv = _sml_bwd((mq, 0), (mk, 0), mv, (o_mem, 0), (mem_lse, 0), (do_mem, 0), width=D_MEM, hd=MEM_HD,
                                 causal=False, name="mem_bwd", tq=tq, tk=256)
    dm_q, dg_mem_q = _hnorm_bwd(m_q, g_mq, dmq_n, width=D_MEM, hd=MEM_HD, name="d_mem_q_norm")
    dmk_raw, dg_mem_k = _hnorm_bwd((mkv, 0), g_mk, dmk_n, width=D_MEM, hd=MEM_HD, name="d_mem_k_norm")
    dmkv = jnp.concatenate([dmk_raw, dmv.astype(BF16)], axis=1)
    G["w_mem_kv"] = _mm(mh, dmkv, name="d_w_mem_kv", ta=True, out_dtypes=(BF16,))
    dmh = _mm(dmkv, W["w_mem_kv"], name="d_mem_h", tb=True)
    _, dg_mem = _rmsnorm_bwd(mem, small["g_mem_norm"], dmh, name="d_mem_norm")

    r_comms, r_finish = reduce_early({k: G.pop(k) for k in EARLY}) if reduce_early is not None else ([None] * 2, None)
    dsb, landed_sb = _split(_sbl_bwd(sb_q, sb_k, sb_v, (do_sb, 0), (sb_tot, 0), width=D_SB, hd=HD, name="sb_bwd", tq=tq,
                                     comm=r_comms[0]), 3)
    (dfq, dfk, dfv, df_row, df_col), landed_fox = _split(
        _sml_bwd((fq, 0), (fk, 0), fx_v, (o_fox, 0), (fox_lse, 0), (do_fox, 0), f_bias, width=D_FOX, hd=HD, causal=True,
                 name="fox_bwd", tq=tq, tk=tq, comm=r_comms[1]), 5)
    early = r_finish(landed_sb, landed_fox) if r_finish is not None else {}
    dfx_q, dg_fox_q = _hnorm_bwd(fx_q, g_fq, dfq, width=D_FOX, hd=HD, name="d_fox_q_norm")
    dfx_k, dg_fox_k = _hnorm_bwd(fx_k, g_fk, dfk, width=D_FOX, hd=HD, name="d_fox_k_norm")
    d_fcum = df_row.reshape(FOX_HEADS, S) - df_col.reshape(FOX_HEADS, S)
    d_flogit_t, db_forget = _forget_bwd(f_logit_t, b_col, d_fcum, name="forget_bwd")
    dg_fox_q, dg_fox_k = dg_fox_q[:, :HD], dg_fox_k[:, :HD]

    rest_cols = jnp.concatenate([t.astype(BF16) for t in (*dsb, dfx_q, dfx_k, dfv, dm_q)]
                                + [d_flogit_t.T.astype(BF16), jnp.zeros((S, F_PAD - FOX_HEADS), BF16)], axis=1)
    dproj = lax.dynamic_update_slice(dgate, rest_cols, (0, 3 * D))
    g_w_in = _mm(dproj, h, name="d_w_in", ta=True, out_dtypes=(BF16,), tm=768)
    comm, finish = reduce_late({"w_in": g_w_in}) if reduce_late is not None else (None, None)
    (grad_x, dg_rows), landed = _split(
        _mm(dproj, W["w_in"], name="d_mix_in", tk=2304, tm=512, tn=D, extras=(x, dx1, small["g_mix_norm"]),
            epilogue=_norm_bwd_tail, col_sums=1, comm=comm), 2)
    dg_mix = jnp.sum(dg_rows, axis=0, keepdims=True)
    if finish is None:
        G["w_in"] = g_w_in
    else:
        early.update(finish(landed))

    small_grads = dict(g_mix_norm=dg_mix, g_mem_norm=dg_mem, b_forget=db_forget.reshape(1, FOX_HEADS),
                       g_fox_q=dg_fox_q, g_fox_k=dg_fox_k, g_mem_q=dg_mem_q, g_mem_k=dg_mem_k, g_mlp_norm=dg_mlp)
    return loss, grad_x, G, small_grads, early


BIG = ("w_in", "w_mem_kv", "w_branch_sb", "w_branch_fox", "w_branch_mem", "w_out", "w_ff_up", "w_ff_down")
COLUMN_SHARDED = ("w_in", "w_branch_sb", "w_branch_fox", "w_branch_mem", "w_ff_up")
SMALL = ("g_mix_norm", "g_mem_norm", "b_forget", "g_fox_q", "g_fox_k", "g_mem_q", "g_mem_k", "g_mlp_norm")
ORDER = ("g_mix_norm", "g_mem_norm", "w_in", "b_forget", "g_fox_q", "g_fox_k", "g_mem_q", "g_mem_k", "w_mem_kv",
         "w_branch_sb", "w_branch_fox", "w_branch_mem", "w_out", "g_mlp_norm", "w_ff_up", "w_ff_down")


def _unshard(name, gathered):
    n, _, rh, c = gathered.shape
    t = gathered.reshape(n, 2 * rh, c)
    if name in COLUMN_SHARDED:
        return t.transpose(1, 0, 2).reshape(2 * rh, n * c)
    return t.reshape(n * 2 * rh, c)


def _reshard(name, full):
    if name in COLUMN_SHARDED:
        r, c = full.shape
        t = full.reshape(r, N_CHIPS, c // N_CHIPS).transpose(1, 0, 2)
    else:
        r, c = full.shape[0] // N_CHIPS, full.shape[1]
        t = full.reshape(N_CHIPS, r, c)
    return t.reshape(N_CHIPS, 2, t.shape[1] // 2, t.shape[2])


ROW_TILE = 16
IN_BUF_ALIGN = 256


def _in_segments(D):
    n_qkv = 6 * D_SB
    o_mq, o_gate = n_qkv + FOX_HEADS, n_qkv + FOX_HEADS + D_MEM
    return [(0, n_qkv, 3 * D), (n_qkv, o_mq, 3 * D + n_qkv + D_MEM), (o_mq, o_gate, 3 * D + n_qkv), (o_gate, o_gate + 3 * D, 0)]


class _InLayout:
    def __init__(self, D, shard, n):
        self.D, self.shard, self.n = D, shard, n
        down = lambda v: v // ROW_TILE * ROW_TILE
        up = lambda v: -(-v // ROW_TILE) * ROW_TILE
        self.pieces = []
        ends = []
        for s in range(n):
            cursor, mine = 0, []
            for a, b, p in _in_segments(D):
                x0, x1 = max(a, s * shard), min(b, (s + 1) * shard)
                if x0 < x1:
                    p0 = p + x0 - a
                    rows = up(p0 + x1 - x0) - down(p0)
                    mine.append((x0 - s * shard, x1 - x0, p0, cursor, rows))
                    cursor += rows
            self.pieces.append(mine)
            ends.append(cursor)
        self.rows = -(-max(ends) // IN_BUF_ALIGN) * IN_BUF_ALIGN
        self.padded_rows = 3 * D + 6 * D_SB + D_MEM + F_PAD

    def _per_shard(self, fn, chip, operand):
        return lax.switch(chip, [functools.partial(fn, s) for s in range(self.n)], operand)

    def pack(self, chip, rows):
        def one(s, t):
            out, at = [], 0
            for x0, n_rows, p0, start, region in self.pieces[s]:
                lead = p0 % ROW_TILE
                out += [jnp.zeros((start + lead - at, t.shape[1]), t.dtype), t[x0:x0 + n_rows]]
                at = start + lead + n_rows
            return jnp.concatenate(out + [jnp.zeros((self.rows - at, t.shape[1]), t.dtype)], axis=0)
        return self._per_shard(one, chip, rows)

    def unpack(self, chip, buf, pad_to):
        def one(s, t):
            out = [t[start + p0 % ROW_TILE:start + p0 % ROW_TILE + n_rows] for _, n_rows, p0, start, _ in self.pieces[s]]
            return jnp.concatenate(out + [jnp.zeros((pad_to - self.shard, t.shape[1]), t.dtype)], axis=0)
        return self._per_shard(one, chip, buf)

    def to_padded(self, bufs):
        runs = sorted((p0, s, start, region) for s in range(self.n) for _, _, p0, start, region in self.pieces[s])
        chunks, end = [], 0
        for p0, s, start, region in runs:
            d0 = p0 // ROW_TILE * ROW_TILE
            src = bufs[s, start:start + region]
            if d0 < end:
                assert end - d0 == ROW_TILE
                last = chunks.pop()
                chunks += [last[:-ROW_TILE], last[-ROW_TILE:] + src[:ROW_TILE], src[ROW_TILE:]]
            else:
                if d0 > end:
                    chunks.append(jnp.zeros((d0 - end, bufs.shape[2]), bufs.dtype))
                chunks.append(src)
            end = d0 + region
        chunks.append(jnp.zeros((self.padded_rows - end, bufs.shape[2]), bufs.dtype))
        return jnp.concatenate(chunks, axis=0)

    def from_padded(self, gp):
        bufs = []
        for s in range(self.n):
            out, at = [], 0
            for _, n_rows, p0, start, region in self.pieces[s]:
                d0 = p0 // ROW_TILE * ROW_TILE
                row = d0 + lax.broadcasted_iota(jnp.int32, (region, 1), 0)
                out.append(jnp.where((row >= p0) & (row < p0 + n_rows), gp[d0:d0 + region], jnp.zeros((), gp.dtype)))
                at = start + region
            bufs.append(jnp.concatenate(out + [jnp.zeros((self.rows - at, gp.shape[1]), gp.dtype)], axis=0))
        return jnp.stack(bufs)


SMALL_ROWS = 16


def _pack_small(vals, scalar=None):
    width = max(vals[k].shape[1] for k in SMALL)
    rows = [jnp.pad(vals[k].astype(F32), ((0, 0), (0, width - vals[k].shape[1]))) for k in SMALL]
    extra = jnp.zeros((SMALL_ROWS - len(SMALL), width), F32)
    if scalar is not None:
        extra = extra.at[0, 0].set(scalar)
    return jnp.concatenate(rows + [extra], axis=0)


def _unpack_small(packed, like):
    return {k: packed[i:i + 1, :like[k].shape[1]] for i, k in enumerate(SMALL)}


def kernel(x, mem, g_mix_norm, g_mem_norm, w_in, b_forget, g_fox_q, g_fox_k, g_mem_q, g_mem_k, w_mem_kv, w_branch_sb, w_branch_fox, w_branch_mem, w_out, g_mlp_norm, w_ff_up, w_ff_down, loss_target, m_g_mix_norm, m_g_mem_norm, m_w_in, m_b_forget, m_g_fox_q, m_g_fox_k, m_g_mem_q, m_g_mem_k, m_w_mem_kv, m_w_branch_sb, m_w_branch_fox, m_w_branch_mem, m_w_out, m_g_mlp_norm, m_w_ff_up, m_w_ff_down, v_g_mix_norm, v_g_mem_norm, v_w_in, v_b_forget, v_g_fox_q, v_g_fox_k, v_g_mem_q, v_g_mem_k, v_w_mem_kv, v_w_branch_sb, v_w_branch_fox, v_w_branch_mem, v_w_out, v_g_mlp_norm, v_w_ff_up, v_w_ff_down):
    given = dict(locals())
    D = x.shape[-1]
    weights = {k: given[k] for k in ORDER}
    moms = {k: given["m_" + k] for k in ORDER}
    vars_ = {k: given["v_" + k] for k in ORDER}

    me_chip = 2 * lax.axis_index("x") + lax.axis_index("y")

    n_in = w_in.shape[2]
    lay = _InLayout(D, n_in, N_CHIPS)
    transposed = lambda t: jnp.transpose(t[0])
    shards = {}
    for k in BIG:
        w = weights[k][0].astype(BF16)
        if k == "w_in":
            w = lay.pack(me_chip, jnp.transpose(w))
        shards[k] = w.reshape(2, w.shape[0] // 2, w.shape[1])
    gathered_in = _gather_weights([shards["w_in"]], name="gather_w_in")[0]
    W = {"w_in": lay.to_padded(gathered_in.reshape(N_CHIPS, lay.rows, D))}
    carried = (("w_branch_sb", "w_branch_fox", "w_branch_mem", "w_out"), ("w_ff_up", "w_mem_kv"), ("w_ff_down",))
    rest = [k for grp in carried for k in grp]
    assert sorted(rest + ["w_in"]) == sorted(BIG)

    def finish_weights(landed):
        full = _forward_halves(landed, name="forward_halves")
        full = [lax.dynamic_update_index_in_dim(o, shards[k], me_chip, 0) for k, o in zip(rest, full)]
        return {k: _unshard(k, g) for k, g in zip(rest, full)}

    me_core = lax.axis_index("c")

    def sum_chips(names, parts, got):
        return {k: _chip_sum(p, q, me_chip, name="sum_chips_" + k) for k, p, q in zip(names, parts, got)}

    def pair_sums(grads, tag):
        names = list(grads)
        stacked = {k: _reshard(k, grads[k]) for k in names if k != "w_in"}
        if "w_in" in grads:
            stacked["w_in"] = lay.from_padded(grads["w_in"]).reshape(N_CHIPS, 2, lay.rows // 2, D)
        got = _exchange_siblings([stacked[k] for k in names], name="exchange_siblings_" + tag)
        return {k: _pair_sum(stacked[k], q, me_core, name="sum_pair_" + k) for k, q in zip(names, got)}

    def reduce_early(grads):
        parts = pair_sums(grads, "early")
        groups = [[k for k in parts if k in ("w_ff_down", "w_ff_up")], [k for k in parts if k not in ("w_ff_down", "w_ff_up")]]
        comms = [_ChipExchange("scatter", [parts[k] for k in grp]) for grp in groups]

        def finish(*landed):
            out = {}
            for grp, got in zip(groups, landed):
                out.update(sum_chips(grp, [parts[k] for k in grp], got))
            return out
        return comms, finish

    def reduce_late(grads):
        parts = pair_sums(grads, "late")
        names = list(parts)
        return _ChipExchange("scatter", [parts[k] for k in names]), functools.partial(sum_chips, names, [parts[k] for k in names])

    small = {k: weights[k] for k in SMALL}
    loss_part, grad_x, G, small_grads, halves = _local_step(
        x[0], mem[0], loss_target[0], small, W,
        gather_rest=([_ChipExchange("gather", [shards[k] for k in grp]) for grp in carried], finish_weights),
        reduce_early=reduce_early, reduce_late=reduce_late)
    assert not G, list(G)
    reduced, small_parts = _share_halves([halves[k] for k in BIG], _pack_small(small_grads, loss_part))

    grads, deltas, new_m, new_v = {}, {}, {}, {}
    for k, g in zip(BIG, reduced):
        shp = weights[k].shape
        if k == "w_in":
            g2 = lay.unpack(me_chip, g.reshape(lay.rows, D), lay.rows)
            padded = lambda t: jnp.pad(transposed(t), ((0, lay.rows - n_in), (0, 0)))
            outs = _adamw(padded(weights[k]), g2, padded(moms[k]), padded(vars_[k]), name="adamw_" + k)
            g2, d, m2, v2 = [jnp.transpose(t[:n_in]) for t in (g2, *outs)]
        else:
            g2 = g.reshape(shp[1], shp[2])
            d, m2, v2 = _adamw(weights[k][0], g2, moms[k][0], vars_[k][0], name="adamw_" + k)
        grads[k], deltas[k], new_m[k], new_v[k] = g2.reshape(shp), d.reshape(shp), m2.reshape(shp), v2.reshape(shp)
    sg, sd, sm, sv = _adamw_small(_pack_small(small), small_parts, _pack_small({k: moms[k] for k in SMALL}),
                                  _pack_small({k: vars_[k] for k in SMALL}), name="adamw_small")
    for dst, packed in ((grads, sg), (deltas, sd), (new_m, sm), (new_v, sv)):
        dst.update(_unpack_small(packed, small))

    loss = sg[len(SMALL), 0]
    return (loss, grad_x[None], *[grads[k] for k in ORDER], *[deltas[k] for k in ORDER],
            *[new_m[k] for k in ORDER], *[new_v[k] for k in ORDER])
```

```python
import functools

import jax
import jax.numpy as jnp
from jax import lax
from jax.experimental import pallas as pl
from jax.experimental.pallas import tpu as pltpu

F32 = jnp.float32
BF16 = jnp.bfloat16
MESH_ID = pl.DeviceIdType.MESH

HD = 64
SB_HEADS = 8
FOX_HEADS = 8
MEM_HEADS = 4
MEM_HD = 128
D_SB = SB_HEADS * HD
D_FOX = FOX_HEADS * HD
D_MEM = MEM_HEADS * MEM_HD
EPS = 1e-6
NEG_INF = -1e30

ADAM_LR = 0.001
ADAM_B1 = 0.9
ADAM_B2 = 0.999
ADAM_EPS = 1e-08
ADAM_WD = 0.01
ADAM_STEP = 10

N_CHIPS = 4
VMEM_LIMIT = 56 * 1024 * 1024

F_PAD = 256


def _tile(n, target, align=128):
    if n <= target:
        return n
    best = None
    t = align
    while t <= target:
        if n % t == 0:
            best = t
        t += align
    assert best is not None, (n, target, align)
    return best


def _params(sem):
    return pltpu.CompilerParams(dimension_semantics=sem, vmem_limit_bytes=VMEM_LIMIT)


def _mm(a, b, *, name, ta=False, tb=False, out_dtypes=(F32,), epilogue=None, extras=(),
        tm=1024, tn=1024, tk=2048, comm=None, col_sums=0, a_gain=None):
    if ta:
        K, M = a.shape
    else:
        M, K = a.shape
    if tb:
        N, K2 = b.shape
    else:
        K2, N = b.shape
    assert K == K2, (a.shape, b.shape, ta, tb)
    tm, tn, tk = _tile(M, tm), _tile(N, tn), _tile(K, tk)
    nk = K // tk
    normed = a_gain is not None
    assert not normed or (nk == 1 and not ta)
    n_extra, n_out = len(extras) + normed, len(out_dtypes) + col_sums + normed
    if epilogue is None:
        epilogue = lambda acc: (acc,)
    dims = (((0 if ta else 1,), (1 if tb else 0,)), ((), ()))

    gm, gn = M // tm, N // tn

    def body(*refs):
        i, j, k = pl.program_id(0), pl.program_id(1), pl.program_id(2)
        got = _carry(comm, 2 + n_extra, n_out, (i == 0) & (j == 0) & (k == 0),
                     (i == gm - 1) & (j == gn - 1) & (k == nk - 1), refs)
        (a_ref, b_ref, *extra_refs), out_refs = got[0], list(got[1])
        if normed:
            gain_ref, normed_ref = extra_refs.pop(), out_refs.pop()
            av = a_ref[...].astype(F32)
            av = (av * lax.rsqrt(jnp.mean(av * av, axis=-1, keepdims=True) + EPS) * gain_ref[...]).astype(BF16)

            @pl.when(j == 0)
            def _():
                normed_ref[...] = av
        else:
            av = a_ref[...].astype(BF16)
        part = lax.dot_general(av, b_ref[...].astype(BF16), dims, preferred_element_type=F32)

        def finish(acc):
            outs = epilogue(acc, *[r[...] for r in extra_refs])
            for o_ref, o in zip(out_refs[:len(out_dtypes)], outs):
                o_ref[...] = o.astype(o_ref.dtype)
            for o_ref, o in zip(out_refs[len(out_dtypes):], outs[len(out_dtypes):]):
                first = lax.broadcasted_iota(jnp.int32, o_ref.shape, 0) == 0
                o_ref[...] = jnp.where(first, jnp.broadcast_to(o, o_ref.shape), 0.0)

        if nk == 1:
            finish(part)
        else:
            acc_ref = refs[-1]

            @pl.when(k == 0)
            def _():
                acc_ref[...] = part

            @pl.when((k > 0) & (k < nk - 1))
            def _():
                acc_ref[...] += part

            @pl.when(k == nk - 1)
            def _():
                finish(acc_ref[...] + part)

        got[2]()

    a_spec = pl.BlockSpec((tk, tm), lambda i, j, k: (k, i)) if ta else pl.BlockSpec((tm, tk), lambda i, j, k: (i, k))
    b_spec = pl.BlockSpec((tn, tk), lambda i, j, k: (j, k)) if tb else pl.BlockSpec((tk, tn), lambda i, j, k: (k, j))
    mn_spec = pl.BlockSpec((tm, tn), lambda i, j, k: (i, j))
    row_spec = pl.BlockSpec((1, tn), lambda i, j, k: (0, j))
    sum_spec = pl.BlockSpec((8, tn), lambda i, j, k: (i, j))
    c_ins, c_in_specs, c_out_specs, c_out_shape, c_scratch = _comm_args(comm)
    sem = ("parallel", "arbitrary" if normed else "parallel", "arbitrary") if comm is None else ("arbitrary",) * 3
    outs = pl.pallas_call(
        body, name=name,
        grid=(gm, gn, nk),
        in_specs=[a_spec, b_spec] + [row_spec if e.shape[0] == 1 else mn_spec for e in extras]
        + ([pl.BlockSpec((1, K), lambda i, j, k: (0, 0))] if normed else []) + c_in_specs,
        out_specs=[mn_spec] * len(out_dtypes) + [sum_spec] * col_sums
        + ([pl.BlockSpec((tm, K), lambda i, j, k: (i, 0))] if normed else []) + c_out_specs,
        out_shape=[jax.ShapeDtypeStruct((M, N), dt) for dt in out_dtypes]
        + [jax.ShapeDtypeStruct((8 * gm, N), F32)] * col_sums
        + ([jax.ShapeDtypeStruct((M, K), BF16)] if normed else []) + c_out_shape,
        scratch_shapes=c_scratch + ([pltpu.VMEM((tm, tn), F32)] if nk > 1 else []),
        compiler_params=_params(sem),
    )(a, b, *extras, *([a_gain] if normed else []), *c_ins)
    return outs if len(outs) > 1 else outs[0]


def _row_tile(rows, cols, n_arrays):
    budget = 10 * 1024 * 1024
    cols_padded = -(-cols // 128) * 128
    target = max(16, budget // (cols_padded * 4 * n_arrays * 2))
    return _tile(rows, target, align=16)


def _ew(fn, ins, out_dtypes, *, name):
    R, C = ins[0].shape
    n_in, n_out = len(ins), len(out_dtypes)
    tr = _row_tile(R, C, n_in + n_out)

    def body(*refs):
        outs = fn(*[r[...] for r in refs[:n_in]])
        for o_ref, o in zip(refs[n_in:], outs):
            o_ref[...] = o.astype(o_ref.dtype)

    spec = pl.BlockSpec((tr, C), lambda i: (i, 0))
    outs = pl.pallas_call(
        body, name=name, grid=(R // tr,),
        in_specs=[spec] * n_in, out_specs=[spec] * n_out,
        out_shape=[jax.ShapeDtypeStruct((R, C), dt) for dt in out_dtypes],
        compiler_params=_params(("parallel",)),
    )(*ins)
    return outs if n_out > 1 else outs[0]


def _rmsnorm_fwd(x, g, out_dtype, *, name):
    R, d = x.shape
    tr = _row_tile(R, d, 3)

    def body(x_ref, g_ref, o_ref):
        xv = x_ref[...].astype(F32)
        r = lax.rsqrt(jnp.mean(xv * xv, axis=-1, keepdims=True) + EPS)
        o_ref[...] = (xv * r * g_ref[...]).astype(o_ref.dtype)

    return pl.pallas_call(
        body, name=name, grid=(R // tr,),
        in_specs=[pl.BlockSpec((tr, d), lambda i: (i, 0)), pl.BlockSpec((1, d), lambda i: (0, 0))],
        out_specs=pl.BlockSpec((tr, d), lambda i: (i, 0)),
        out_shape=jax.ShapeDtypeStruct((R, d), out_dtype),
        compiler_params=_params(("parallel",)),
    )(x, g)


def _rmsnorm_bwd(x, g, dy, add=None, *, name):
    R, d = x.shape
    has_add = add is not None
    tr = _row_tile(R, d, 5)

    def body(*refs):
        x_ref, g_ref, dy_ref = refs[:3]
        add_ref = refs[3] if has_add else None
        dx_ref, dg_ref = refs[-2:]
        xv = x_ref[...].astype(F32)
        dyv = dy_ref[...].astype(F32)
        r = lax.rsqrt(jnp.mean(xv * xv, axis=-1, keepdims=True) + EPS)
        xh = xv * r
        dyg = dyv * g_ref[...]
        c = jnp.mean(dyg * xh, axis=-1, keepdims=True)
        dx = r * (dyg - xh * c)
        if has_add:
            dx = dx + add_ref[...]
        dx_ref[...] = dx

        @pl.when(pl.program_id(0) == 0)
        def _():
            dg_ref[...] = jnp.zeros_like(dg_ref)

        dg_ref[...] += jnp.sum(dyv * xh, axis=0, keepdims=True)

    row = pl.BlockSpec((tr, d), lambda i: (i, 0))
    vec = pl.BlockSpec((1, d), lambda i: (0, 0))
    ins = [x, g, dy] + ([add] if has_add else [])
    return pl.pallas_call(
        body, name=name, grid=(R // tr,),
        in_specs=[row, vec, row] + ([row] if has_add else []),
        out_specs=[row, vec],
        out_shape=[jax.ShapeDtypeStruct((R, d), F32), jax.ShapeDtypeStruct((1, d), F32)],
        compiler_params=_params(("arbitrary",)),
    )(*ins)


_NT = (((1,), (1,)), ((), ()))
_TN = (((0,), (0,)), ((), ()))


def _dot(a, b, dims=(((1,), (0,)), ((), ()))):
    return lax.dot_general(a, b, dims, preferred_element_type=F32)


def _log_sigmoid_pair(z):
    sp = jnp.log(1.0 + jnp.exp(-jnp.abs(z)))
    return jnp.minimum(z, 0.0) - sp, jnp.minimum(-z, 0.0) - sp


LANES = 128
_LOW = -3e38


def _lane_masks(hd, rows):
    if hd == LANES:
        return [None]
    lane = lax.broadcasted_iota(jnp.int32, (rows, LANES), 1)
    return [(lane >= hh * hd) & (lane < (hh + 1) * hd) for hh in range(LANES // hd)]


def _keep(t, m):
    return t if m is None else jnp.where(m, t, 0.0)


def _merge(parts, masks):
    out = parts[-1]
    for p, m in zip(parts[-2::-1], masks[-2::-1]):
        out = jnp.where(m, p, out)
    return out


def _row_value(t, m):
    return jnp.max(t if m is None else jnp.where(m, t, _LOW), axis=1, keepdims=True)


def _cols(tq, off):
    return pl.BlockSpec((tq, LANES), lambda g, i: (i, off + g))


def _cols_all(rows, off):
    return pl.BlockSpec((rows, LANES), lambda g, i: (0, off + g))


SCAN_BLOCK = 256


def _tri(kind, cols):
    n = min(SCAN_BLOCK, cols)
    r = lax.broadcasted_iota(jnp.int32, (n, n), 0)
    c = lax.broadcasted_iota(jnp.int32, (n, n), 1)
    return ((r > c) if kind == "after" else (r < c)).astype(BF16)


def _scan_cols(x, tri, reverse):
    cols = x.shape[1]
    cb = min(SCAN_BLOCK, cols)
    assert cols % cb == 0 and tri.shape == (cb, cb)
    nb = cols // cb
    blocks = [x[:, b * cb:(b + 1) * cb] for b in range(nb)]
    outs, carry = [None] * nb, None
    for b in (reversed(range(nb)) if reverse else range(nb)):
        y = _dot(blocks[b].astype(BF16), tri)
        outs[b] = y if carry is None else y + carry
        s = jnp.sum(blocks[b], axis=1, keepdims=True)
        carry = s if carry is None else carry + s
    return (outs[0] if nb == 1 else jnp.concatenate(outs, axis=1)), carry


def _softplus_parts(z):
    pos = jnp.maximum(z, 0.0) + jnp.log(1.0 + jnp.exp(-jnp.abs(z)))
    return pos, z - pos


class _ChipExchange:
    def __init__(self, kind, ins):
        assert kind in ("gather", "scatter")
        self.kind, self.ins = kind, list(ins)
        lead = (lambda s: (N_CHIPS,) + s) if kind == "gather" else (lambda s: (3,) + s[1:])
        self.out_shape = [jax.ShapeDtypeStruct(lead(a.shape), a.dtype) for a in ins]
        n = 3 * len(ins)
        self.scratch = [pltpu.SemaphoreType.DMA((n,)), pltpu.SemaphoreType.DMA((n,))]

    def _copies(self, in_refs, out_refs, sems, landing):
        x, y, c, chips = _mesh_place()
        me = 2 * x + y
        out = []
        for w in range(len(self.ins)):
            for j, chip in enumerate(chips):
                peer = 2 * chip[0] + chip[1]
                if self.kind == "gather":
                    src, dst, land = in_refs[w].at[c], out_refs[w].at[me, c], out_refs[w].at[peer, c]
                else:
                    src, dst, land = in_refs[w].at[peer], out_refs[w].at[j], out_refs[w].at[j]
                if landing:
                    src, dst = land, land
                out.append(_remote(src, dst, sems, 3 * w + j, (chip[0], chip[1], c)))
        return out

    def start(self, in_refs, out_refs, sems):
        for cp in self._copies(in_refs, out_refs, sems, False):
            cp.start()

    def finish(self, in_refs, out_refs, sems):
        for cp in self._copies(in_refs, out_refs, sems, True):
            cp.wait_recv()
        for cp in self._copies(in_refs, out_refs, sems, False):
            cp.wait_send()


def _carry(comm, n_in, n_out, first, last, refs):
    if comm is None:
        return refs[:n_in], refs[n_in:n_in + n_out], (lambda: None)
    a, b = len(comm.ins), len(comm.out_shape)
    ins, c_in = refs[:n_in], refs[n_in:n_in + a]
    outs, c_out = refs[n_in + a:n_in + a + n_out], refs[n_in + a + n_out:n_in + a + n_out + b]
    sems = refs[n_in + a + n_out + b:n_in + a + n_out + b + 2]
    pl.when(first)(lambda: comm.start(c_in, c_out, sems))
    return ins, outs, (lambda: pl.when(last)(lambda: comm.finish(c_in, c_out, sems)))


def _sbl_fwd(q, k, v, *, width, hd, name, tq=256, comm=None):
    (qa, qo), (ka, ko), (va, vo) = q, k, v
    S = qa.shape[0]
    tq = _tile(S, tq)
    tk = tq
    scale = hd ** -0.5
    n_g, n_q = width // LANES, S // tq

    def body(*refs):
        qi = pl.program_id(1)
        gi = pl.program_id(0)
        got = _carry(comm, 3, 2, (gi == 0) & (qi == 0), (gi == n_g - 1) & (qi == n_q - 1), refs)
        (q_ref, k_ref, v_ref), (o_ref, tot_ref) = got[0], got[1]
        masks = _lane_masks(hd, tq)
        qs = q_ref[...].astype(F32) * scale
        qm = [_keep(qs, m).astype(BF16) for m in masks]
        strict = lax.broadcasted_iota(jnp.int32, (tq, tk), 1) < lax.broadcasted_iota(jnp.int32, (tq, tk), 0)
        later = _tri("after", tk)

        def tile(kb, carry, diag):
            ks = pl.multiple_of(kb * tk, tk)
            kv = k_ref[pl.ds(ks, tk), :].astype(BF16)
            vv = v_ref[pl.ds(ks, tk), :].astype(BF16)
            out = []
            for hh in range(len(masks)):
                acc, c_pos = carry[2 * hh], carry[2 * hh + 1]
                pos, ls = _softplus_parts(_dot(qm[hh], kv, _NT))
                if diag:
                    pos = jnp.where(strict, pos, 0.0)
                pos_after, pos_all = _scan_cols(pos, later, True)
                w = jnp.exp(ls - (pos_after + c_pos))
                if diag:
                    w = jnp.where(strict, w, 0.0)
                out += [acc + _dot(w.astype(BF16), vv), c_pos + pos_all]
            return tuple(out)

        init = (jnp.zeros((tq, LANES), F32), jnp.zeros((tq, 1), F32)) * len(masks)
        carry = tile(qi, init, True)
        carry = lax.fori_loop(0, qi, lambda i, c: tile(qi - 1 - i, c, False), carry)
        o_ref[...] = _merge(carry[0::2], masks).astype(o_ref.dtype)
        tot_ref[...] = _merge([jnp.broadcast_to(-c, (tq, LANES)) for c in carry[1::2]], masks)
        got[2]()

    c_ins, c_in_specs, c_out_specs, c_out_shape, c_scratch = _comm_args(comm)
    return pl.pallas_call(
        body, name=name, grid=(n_g, n_q),
        in_specs=[_cols(tq, qo), _cols_all(S, ko), _cols_all(S, vo)] + c_in_specs,
        out_specs=[_cols(tq, 0), _cols(tq, 0)] + c_out_specs,
        out_shape=[jax.ShapeDtypeStruct((S, width), BF16), jax.ShapeDtypeStruct((S, width), F32)] + c_out_shape,
        scratch_shapes=c_scratch,
        compiler_params=_params(("arbitrary", "arbitrary")),
    )(qa, ka, va, *c_ins)


def _comm_args(comm):
    if comm is None:
        return [], [], [], [], []
    return comm.ins, [_ANY] * len(comm.ins), [_ANY] * len(comm.out_shape), comm.out_shape, comm.scratch


def _sbl_bwd(q, k, v, do, tot, *, width, hd, name, tq=256, comm=None):
    (qa, qo), (ka, ko), (va, vo) = q, k, v
    S = qa.shape[0]
    tq = _tile(S, tq)
    tk = tq
    scale = hd ** -0.5
    n_g, n_q = width // LANES, S // tq

    def body(*refs):
        qi = pl.program_id(1)
        gi = pl.program_id(0)
        got = _carry(comm, 5, 3, (gi == 0) & (qi == 0), (gi == n_g - 1) & (qi == n_q - 1), refs)
        (q_ref, k_ref, v_ref, do_ref, tot_ref), (dq_ref, dk_ref, dv_ref) = got[0], got[1]

        @pl.when(qi == 0)
        def _():
            dk_ref[...] = jnp.zeros_like(dk_ref)
            dv_ref[...] = jnp.zeros_like(dv_ref)

        masks = _lane_masks(hd, tq)
        qs = q_ref[...].astype(F32) * scale
        qm = [_keep(qs, m).astype(BF16) for m in masks]
        dov = [_keep(do_ref[...], m).astype(BF16) for m in masks]
        rest = [-_row_value(tot_ref[...], m) for m in masks]
        strict = lax.broadcasted_iota(jnp.int32, (tq, tk), 1) < lax.broadcasted_iota(jnp.int32, (tq, tk), 0)
        later, before = _tri("after", tk), _tri("before", tk)

        def tile(kb, carry, diag):
            ks = pl.multiple_of(kb * tk, tk)
            kv = k_ref[pl.ds(ks, tk), :].astype(BF16)
            vv = v_ref[pl.ds(ks, tk), :].astype(BF16)
            out = []
            dk_t, dv_t = None, None
            for hh in range(len(masks)):
                dq, c_pos, c_g = carry[3 * hh:3 * hh + 3]
                pos, ls = _softplus_parts(_dot(qm[hh], kv, _NT))
                if diag:
                    pos = jnp.where(strict, pos, 0.0)
                pos_after, pos_all = _scan_cols(pos, later, True)
                c_pos = c_pos + pos_all
                w = jnp.exp(ls - (pos_after + (rest[hh] - c_pos)))
                if diag:
                    w = jnp.where(strict, w, 0.0)
                g = _dot(dov[hh], vv, _NT) * w
                g_before, g_all = _scan_cols(g, before, False)
                g_before = g_before + c_g
                dz = g - jnp.exp(ls) * (g + g_before)
                if diag:
                    dz = jnp.where(strict, dz, 0.0)
                dzb = dz.astype(BF16)
                dk_h = _dot(dzb, qm[hh], _TN)
                dv_h = _dot(w.astype(BF16), dov[hh], _TN)
                dk_t = dk_h if dk_t is None else dk_t + dk_h
                dv_t = dv_h if dv_t is None else dv_t + dv_h
                out += [dq + _dot(dzb, kv), c_pos, c_g + g_all]
            dk_ref[pl.ds(ks, tk), :] += dk_t
            dv_ref[pl.ds(ks, tk), :] += dv_t
            return tuple(out)

        zero = jnp.zeros((tq, 1), F32)
        init = (jnp.zeros((tq, LANES), F32), zero, zero) * len(masks)
        carry = lax.fori_loop(0, qi, lambda kb, c: tile(kb, c, False), init)
        carry = tile(qi, carry, True)
        dq_ref[...] = _merge(carry[0::3], masks) * scale
        got[2]()

    full = jax.ShapeDtypeStruct((S, width), F32)
    c_ins, c_in_specs, c_out_specs, c_out_shape, c_scratch = _comm_args(comm)
    return pl.pallas_call(
        body, name=name, grid=(n_g, n_q),
        in_specs=[_cols(tq, qo), _cols_all(S, ko), _cols_all(S, vo), _cols(tq, do[1]), _cols(tq, tot[1])] + c_in_specs,
        out_specs=[_cols(tq, 0), _cols_all(S, 0), _cols_all(S, 0)] + c_out_specs,
        out_shape=[full, full, full] + c_out_shape,
        scratch_shapes=c_scratch,
        compiler_params=_params(("arbitrary", "arbitrary")),
    )(qa, ka, va, do[0], tot[0], *c_ins)


def _sml_fwd(q, k, v, bias=None, *, width, hd, causal, name, tq=256, tk=256, comm=None):
    (qa, qo), (ka, ko), (va, vo) = q, k, v
    S, Sk = qa.shape[0], ka.shape[0]
    tq, tk = _tile(S, tq), _tile(Sk, tk)
    if causal:
        assert tq == tk and S == Sk
    nk = Sk // tk
    hpg = LANES // hd
    scale = hd ** -0.5
    has_bias = bias is not None
    n_g, n_q = width // LANES, S // tq

    def body(*all_refs):
        qi, gi = pl.program_id(1), pl.program_id(0)
        got = _carry(comm, 5 if has_bias else 3, 2, (gi == 0) & (qi == 0), (gi == n_g - 1) & (qi == n_q - 1), all_refs)
        refs = tuple(got[0]) + tuple(got[1])
        q_ref, k_ref, v_ref = refs[:3]
        o_ref, lse_ref = refs[-2:]
        masks = _lane_masks(hd, tq)
        qs = q_ref[...].astype(F32) * scale
        qm = [_keep(qs, m).astype(BF16) for m in masks]
        allowed = lax.broadcasted_iota(jnp.int32, (tq, tk), 1) <= lax.broadcasted_iota(jnp.int32, (tq, tk), 0)

        def tile(kb, carry, diag):
            ks = pl.multiple_of(kb * tk, tk)
            kv = k_ref[pl.ds(ks, tk), :].astype(BF16)
            vv = v_ref[pl.ds(ks, tk), :].astype(BF16)
            out = []
            for hh in range(hpg):
                m, l, acc = carry[3 * hh:3 * hh + 3]
                z = _dot(qm[hh], kv, _NT)
                if has_bias:
                    z = z + refs[3][hh] - refs[4][hh, kb]
                if diag:
                    z = jnp.where(allowed, z, NEG_INF)
                m2 = jnp.maximum(m, jnp.max(z, axis=1, keepdims=True))
                p = jnp.exp(z - m2)
                alpha = jnp.exp(m - m2)
                out += [m2, alpha * l + jnp.sum(p, axis=1, keepdims=True), alpha * acc + _dot(p.astype(BF16), vv)]
            return tuple(out)

        init = (jnp.full((tq, 1), NEG_INF, F32), jnp.zeros((tq, 1), F32), jnp.zeros((tq, LANES), F32)) * hpg
        if causal:
            carry = lax.fori_loop(0, qi, lambda kb, c: tile(kb, c, False), init)
            carry = tile(qi, carry, True)
        else:
            carry = lax.fori_loop(0, nk, lambda kb, c: tile(kb, c, False), init)
        o_ref[...] = _merge([acc / l for l, acc in zip(carry[1::3], carry[2::3])], masks).astype(o_ref.dtype)
        lse_ref[...] = _merge([jnp.broadcast_to(m + jnp.log(l), (tq, LANES)) for m, l in zip(carry[0::3], carry[1::3])], masks)
        got[2]()

    in_specs = [_cols(tq, qo), _cols_all(Sk, ko), _cols_all(Sk, vo)]
    ins = [qa, ka, va]
    if has_bias:
        in_specs += [pl.BlockSpec((hpg, tq, 1), lambda g, i: (g, i, 0)),
                     pl.BlockSpec((hpg, nk, 1, tk), lambda g, i: (g, 0, 0, 0))]
        ins += list(bias)
    c_ins, c_in_specs, c_out_specs, c_out_shape, c_scratch = _comm_args(comm)
    return pl.pallas_call(
        body, name=name, grid=(n_g, n_q),
        in_specs=in_specs + c_in_specs, out_specs=[_cols(tq, 0), _cols(tq, 0)] + c_out_specs,
        out_shape=[jax.ShapeDtypeStruct((S, width), BF16), jax.ShapeDtypeStruct((S, width), F32)] + c_out_shape,
        scratch_shapes=c_scratch,
        compiler_params=_params(("arbitrary", "arbitrary") if comm is not None else ("parallel", "arbitrary")),
    )(*ins, *c_ins)


def _sml_bwd(q, k, v, o, lse, do, bias=None, *, width, hd, causal, name, tq=256, tk=256, comm=None):
    (qa, qo), (ka, ko), (va, vo) = q, k, v
    S, Sk = qa.shape[0], ka.shape[0]
    tq, tk = _tile(S, tq), _tile(Sk, tk)
    nk = Sk // tk
    hpg = LANES // hd
    scale = hd ** -0.5
    has_bias = bias is not None
    n_in = 8 if has_bias else 6
    n_g, n_q = width // LANES, S // tq

    def body(*all_refs):
        qi, gi = pl.program_id(1), pl.program_id(0)
        got = _carry(comm, n_in, 5 if has_bias else 3, (gi == 0) & (qi == 0), (gi == n_g - 1) & (qi == n_q - 1), all_refs)
        refs = tuple(got[0]) + tuple(got[1])
        q_ref, k_ref, v_ref, o_ref, lse_ref, do_ref = refs[:6]
        dq_ref, dk_ref, dv_ref = refs[n_in:n_in + 3]

        @pl.when(qi == 0)
        def _():
            dk_ref[...] = jnp.zeros_like(dk_ref)
            dv_ref[...] = jnp.zeros_like(dv_ref)
            if has_bias:
                refs[n_in + 4][...] = jnp.zeros_like(refs[n_in + 4])

        masks = _lane_masks(hd, tq)
        qs = q_ref[...].astype(F32) * scale
        qm = [_keep(qs, m).astype(BF16) for m in masks]
        do32 = do_ref[...]
        dov = [_keep(do32, m).astype(BF16) for m in masks]
        prod = do32 * o_ref[...].astype(F32)
        delta = [jnp.sum(_keep(prod, m), axis=1, keepdims=True) for m in masks]
        lses = [_row_value(lse_ref[...], m) for m in masks]
        allowed = lax.broadcasted_iota(jnp.int32, (tq, tk), 1) <= lax.broadcasted_iota(jnp.int32, (tq, tk), 0)

        def tile(kb, carry, diag):
            ks = pl.multiple_of(kb * tk, tk)
            kv = k_ref[pl.ds(ks, tk), :].astype(BF16)
            vv = v_ref[pl.ds(ks, tk), :].astype(BF16)
            out = []
            dk_t, dv_t = None, None
            for hh in range(hpg):
                dq, db_row = carry[2 * hh:2 * hh + 2]
                z = _dot(qm[hh], kv, _NT)
                if has_bias:
                    z = z + refs[6][hh] - refs[7][hh, kb]
                p = jnp.exp(z - lses[hh])
                if diag:
                    p = jnp.where(allowed, p, 0.0)
                dz = p * (_dot(dov[hh], vv, _NT) - delta[hh])
                dzb = dz.astype(BF16)
                dk_h = _dot(dzb, qm[hh], _TN)
                dv_h = _dot(p.astype(BF16), dov[hh], _TN)
                dk_t = dk_h if dk_t is None else dk_t + dk_h
                dv_t = dv_h if dv_t is None else dv_t + dv_h
                if has_bias:
                    db_row = db_row + jnp.sum(dz, axis=1, keepdims=True)
                    refs[n_in + 4][hh, kb] += jnp.sum(dz, axis=0, keepdims=True)
                out += [dq + _dot(dzb, kv), db_row]
            dk_ref[pl.ds(ks, tk), :] += dk_t
            dv_ref[pl.ds(ks, tk), :] += dv_t
            return tuple(out)

        init = (jnp.zeros((tq, LANES), F32), jnp.zeros((tq, 1), F32)) * hpg
        if causal:
            carry = lax.fori_loop(0, qi, lambda kb, c: tile(kb, c, False), init)
            carry = tile(qi, carry, True)
        else:
            carry = lax.fori_loop(0, nk, lambda kb, c: tile(kb, c, False), init)
        dq_ref[...] = _merge(carry[0::2], masks) * scale
        if has_bias:
            for hh in range(hpg):
                refs[n_in + 3][hh] = carry[2 * hh + 1]
        got[2]()

    in_specs = [_cols(tq, qo), _cols_all(Sk, ko), _cols_all(Sk, vo), _cols(tq, o[1]), _cols(tq, lse[1]), _cols(tq, do[1])]
    ins = [qa, ka, va, o[0], lse[0], do[0]]
    out_specs = [_cols(tq, 0), _cols_all(Sk, 0), _cols_all(Sk, 0)]
    out_shape = [jax.ShapeDtypeStruct((S, width), F32), jax.ShapeDtypeStruct((Sk, width), F32),
                 jax.ShapeDtypeStruct((Sk, width), F32)]
    if has_bias:
        rspec = pl.BlockSpec((hpg, tq, 1), lambda g, i: (g, i, 0))
        cspec = pl.BlockSpec((hpg, nk, 1, tk), lambda g, i: (g, 0, 0, 0))
        in_specs += [rspec, cspec]
        ins += list(bias)
        out_specs += [rspec, cspec]
        n_heads = width // hd
        out_shape += [jax.ShapeDtypeStruct((n_heads, S, 1), F32), jax.ShapeDtypeStruct((n_heads, nk, 1, tk), F32)]
    c_ins, c_in_specs, c_out_specs, c_out_shape, c_scratch = _comm_args(comm)
    return pl.pallas_call(
        body, name=name, grid=(n_g, n_q),
        in_specs=in_specs + c_in_specs, out_specs=out_specs + c_out_specs, out_shape=out_shape + c_out_shape,
        scratch_shapes=c_scratch,
        compiler_params=_params(("arbitrary", "arbitrary") if comm is not None else ("parallel", "arbitrary")),
    )(*ins, *c_ins)


def _head_sums(t, masks):
    sums = [jnp.sum(_keep(t, m), axis=1, keepdims=True) for m in masks]
    return _merge([jnp.broadcast_to(s, t.shape) for s in sums], masks) if len(masks) > 1 else sums[0]


def _hnorm_fwd(x, g_lanes, *, width, hd, name, tr=1024):
    xa, xo = x
    R = xa.shape[0]
    tr = _tile(R, tr, align=16)
    n_blk = width // LANES

    def body(x_ref, g_ref, o_ref):
        masks = _lane_masks(hd, tr)
        for j in range(n_blk):
            sl = slice(j * LANES, (j + 1) * LANES)
            xv = x_ref[:, sl].astype(F32)
            r = lax.rsqrt(_head_sums(xv * xv, masks) * (1.0 / hd) + EPS)
            o_ref[:, sl] = (xv * r * g_ref[...]).astype(o_ref.dtype)

    assert (xo * LANES) % width == 0
    return pl.pallas_call(
        body, name=name, grid=(R // tr,),
        in_specs=[pl.BlockSpec((tr, width), lambda i: (i, xo * LANES // width)), pl.BlockSpec((1, LANES), lambda i: (0, 0))],
        out_specs=pl.BlockSpec((tr, width), lambda i: (i, 0)),
        out_shape=jax.ShapeDtypeStruct((R, width), BF16),
        compiler_params=_params(("parallel",)),
    )(xa, g_lanes)


def _hnorm_bwd(x, g_lanes, dy, *, width, hd, name, tr=1024):
    xa, xo = x
    R = xa.shape[0]
    tr = _tile(R, tr, align=16)
    n_blk = width // LANES

    def body(x_ref, g_ref, dy_ref, dx_ref, dg_ref):
        masks = _lane_masks(hd, tr)
        dg = jnp.zeros((1, LANES), F32)
        for j in range(n_blk):
            sl = slice(j * LANES, (j + 1) * LANES)
            xv = x_ref[:, sl].astype(F32)
            dyv = dy_ref[:, sl].astype(F32)
            r = lax.rsqrt(_head_sums(xv * xv, masks) * (1.0 / hd) + EPS)
            xh = xv * r
            dyg = dyv * g_ref[...]
            c = _head_sums(dyg * xh, masks) * (1.0 / hd)
            dx_ref[:, sl] = (r * (dyg - xh * c)).astype(dx_ref.dtype)
            dg = dg + jnp.sum(dyv * xh, axis=0, keepdims=True)
        if hd * 2 == LANES:
            dg8 = jnp.broadcast_to(dg, (8, LANES))
            dg = (dg8 + pltpu.roll(dg8, shift=hd, axis=1))[0:1]
        else:
            assert hd == LANES

        @pl.when(pl.program_id(0) == 0)
        def _():
            dg_ref[...] = jnp.zeros_like(dg_ref)

        dg_ref[...] += dg

    assert (xo * LANES) % width == 0
    return pl.pallas_call(
        body, name=name, grid=(R // tr,),
        in_specs=[pl.BlockSpec((tr, width), lambda i: (i, xo * LANES // width)), pl.BlockSpec((1, LANES), lambda i: (0, 0)),
                  pl.BlockSpec((tr, width), lambda i: (i, 0))],
        out_specs=[pl.BlockSpec((tr, width), lambda i: (i, 0)), pl.BlockSpec((1, LANES), lambda i: (0, 0))],
        out_shape=[jax.ShapeDtypeStruct((R, width), BF16), jax.ShapeDtypeStruct((1, LANES), F32)],
        compiler_params=_params(("arbitrary",)),
    )(xa, g_lanes, dy)


def _split3_dot(x, tri):
    a = x.astype(BF16)
    r = x - a.astype(F32)
    b = r.astype(BF16)
    c = (r - b.astype(F32)).astype(BF16)
    return _dot(a, tri) + _dot(b, tri) + _dot(c, tri)


def _forget_fwd(logit_t, b_col, *, name, blk=512):
    H, S = logit_t.shape
    blk = _tile(S, blk)

    def body(l_ref, b_ref, f_ref):
        r_i = lax.broadcasted_iota(jnp.int32, (blk, blk), 0)
        c_i = lax.broadcasted_iota(jnp.int32, (blk, blk), 1)
        upto = (r_i <= c_i).astype(BF16)
        carry = jnp.zeros((H, 1), F32)
        for j in range(S // blk):
            u = l_ref[:, j * blk:(j + 1) * blk] + b_ref[...]
            lf, _ = _log_sigmoid_pair(u)
            f_ref[:, j * blk:(j + 1) * blk] = _split3_dot(lf, upto) + carry
            carry = carry + jnp.sum(lf, axis=1, keepdims=True)

    return pl.pallas_call(
        body, name=name,
        out_shape=jax.ShapeDtypeStruct((H, S), F32),
        compiler_params=pltpu.CompilerParams(vmem_limit_bytes=VMEM_LIMIT),
    )(logit_t, b_col)


def _forget_bwd(logit_t, b_col, d_f, *, name, blk=512):
    H, S = logit_t.shape
    blk = _tile(S, blk)

    def body(l_ref, b_ref, df_ref, dl_ref, db_ref):
        r_i = lax.broadcasted_iota(jnp.int32, (blk, blk), 0)
        c_i = lax.broadcasted_iota(jnp.int32, (blk, blk), 1)
        fromon = (r_i >= c_i).astype(BF16)
        carry = jnp.zeros((H, 1), F32)
        db = jnp.zeros((H, 1), F32)
        for j in reversed(range(S // blk)):
            sl = slice(j * blk, (j + 1) * blk)
            dfv = df_ref[:, sl]
            d_lf = _split3_dot(dfv, fromon) + carry
            carry = carry + jnp.sum(dfv, axis=1, keepdims=True)
            u = l_ref[:, sl] + b_ref[...]
            _, lsn = _log_sigmoid_pair(u)
            dl = d_lf * jnp.exp(lsn)
            dl_ref[:, sl] = dl
            db = db + jnp.sum(dl, axis=1, keepdims=True)
        db_ref[...] = db

    return pl.pallas_call(
        body, name=name,
        out_shape=[jax.ShapeDtypeStruct((H, S), F32), jax.ShapeDtypeStruct((H, 1), F32)],
        compiler_params=pltpu.CompilerParams(vmem_limit_bytes=VMEM_LIMIT),
    )(logit_t, b_col, d_f)


def _sigmoid(t):
    return 1.0 / (1.0 + jnp.exp(-t))


def _gate_fwd(o3, w3, proj, x, w_out, g_norm, D, *, name, tm=512):
    S = proj.shape[0]
    tm = _tile(S, tm)

    def body(o0, o1, o2, w0, w1, w2, g0, g1, g2, x_ref, wo_ref, gn_ref, merged_ref, x1_ref, h2_ref):
        acc = None
        for o_ref, w_ref, g_ref in ((o0, w0, g0), (o1, w1, g1), (o2, w2, g2)):
            t = _sigmoid(g_ref[...]) * _dot(o_ref[...], w_ref[...])
            acc = t if acc is None else acc + t
        merged = acc.astype(BF16)
        merged_ref[...] = merged
        x1 = x_ref[...] + _dot(merged, wo_ref[...])
        x1_ref[...] = x1
        h2_ref[...] = (x1 * lax.rsqrt(jnp.mean(x1 * x1, axis=-1, keepdims=True) + EPS) * gn_ref[...]).astype(BF16)

    ospec = lambda d: pl.BlockSpec((tm, d), lambda i: (i, 0))
    wspec = lambda w: pl.BlockSpec(w.shape, lambda i: (0, 0))
    gspec = lambda j: pl.BlockSpec((tm, D), lambda i: (i, j))
    row = pl.BlockSpec((tm, D), lambda i: (i, 0))
    return pl.pallas_call(
        body, name=name, grid=(S // tm,),
        in_specs=[ospec(o.shape[1]) for o in o3] + [wspec(w) for w in w3] + [gspec(j) for j in range(3)]
        + [row, wspec(w_out), wspec(g_norm)],
        out_specs=[row, row, row],
        out_shape=[jax.ShapeDtypeStruct((S, D), BF16), jax.ShapeDtypeStruct((S, D), F32), jax.ShapeDtypeStruct((S, D), BF16)],
        compiler_params=_params(("parallel",)),
    )(*o3, *w3, proj, proj, proj, x, w_out, g_norm)


def _gate_bwd(o3, w3, proj, dx1, w_out, D, *, name, tm=256):
    S = proj.shape[0]
    tm = _tile(S, tm)

    def body(o0, o1, o2, w0, w1, w2, g0, g1, g2, dx_ref, wo_ref, dg_ref, db0, db1, db2, do0, do1, do2):
        dm = _dot(dx_ref[...].astype(BF16), wo_ref[...], _NT)
        for j, (o_ref, w_ref, g_ref, db_ref, do_ref) in enumerate(
                ((o0, w0, g0, db0, do0), (o1, w1, g1, db1, do1), (o2, w2, g2, db2, do2))):
            s = _sigmoid(g_ref[...])
            br = _dot(o_ref[...], w_ref[...])
            dg_ref[:, j * D:(j + 1) * D] = (dm * br * s * (1.0 - s)).astype(dg_ref.dtype)
            dbr = (dm * s).astype(BF16)
            db_ref[...] = dbr
            do_ref[...] = _dot(dbr, w_ref[...], _NT)

    ospec = lambda d: pl.BlockSpec((tm, d), lambda i: (i, 0))
    wspec = lambda w: pl.BlockSpec(w.shape, lambda i: (0, 0))
    gspec = lambda j: pl.BlockSpec((tm, D), lambda i: (i, j))
    dspec = pl.BlockSpec((tm, D), lambda i: (i, 0))
    return pl.pallas_call(
        body, name=name, grid=(S // tm,),
        in_specs=[ospec(o.shape[1]) for o in o3] + [wspec(w) for w in w3] + [gspec(j) for j in range(3)]
        + [dspec, wspec(w_out)],
        out_specs=[pl.BlockSpec((tm, 3 * D), lambda i: (i, 0))] + [dspec] * 3 + [ospec(o.shape[1]) for o in o3],
        out_shape=[jax.ShapeDtypeStruct((S, proj.shape[1]), BF16)] + [jax.ShapeDtypeStruct((S, D), BF16)] * 3
        + [jax.ShapeDtypeStruct((S, o.shape[1]), F32) for o in o3],
        compiler_params=_params(("parallel",)),
    )(*o3, *w3, proj, proj, proj, dx1, w_out)


def _pair_sum(stacked, got, core, *, name):
    n, _, r, c = stacked.shape
    tr = _row_tile(r, c, 3)

    def body(core_ref, a_ref, b_ref, o_ref):
        o_ref[...] = (a_ref[0].astype(F32) + b_ref[...].astype(F32)).astype(o_ref.dtype)

    spec = pl.BlockSpec((1, tr, c), lambda s, i, core_ref: (s, i, 0))
    return pl.pallas_call(
        body, name=name,
        grid_spec=pltpu.PrefetchScalarGridSpec(
            num_scalar_prefetch=1, grid=(n, r // tr),
            in_specs=[pl.BlockSpec((1, 1, tr, c), lambda s, i, core_ref: (s, core_ref[0], i, 0)), spec],
            out_specs=spec),
        out_shape=jax.ShapeDtypeStruct((n, r, c), BF16),
        compiler_params=_params(("parallel", "parallel")),
    )(core.astype(jnp.int32).reshape(1), stacked, got)


def _chip_sum(parts, got, chip, *, name):
    _, r, c = parts.shape
    tr = _row_tile(r, c, 6)

    def body(chip_ref, p_ref, q0_ref, q1_ref, q2_ref, o_ref):
        o_ref[...] = ((p_ref[0].astype(F32) + q0_ref[0].astype(F32)) + q1_ref[0].astype(F32)) + q2_ref[0].astype(F32)

    from_chip = lambda j: pl.BlockSpec((1, tr, c), lambda i, chip_ref: (j, i, 0))
    return pl.pallas_call(
        body, name=name,
        grid_spec=pltpu.PrefetchScalarGridSpec(
            num_scalar_prefetch=1, grid=(r // tr,),
            in_specs=[pl.BlockSpec((1, tr, c), lambda i, chip_ref: (chip_ref[0], i, 0))] + [from_chip(j) for j in range(3)],
            out_specs=pl.BlockSpec((tr, c), lambda i, chip_ref: (i, 0))),
        out_shape=jax.ShapeDtypeStruct((r, c), F32),
        compiler_params=_params(("parallel",)),
    )(chip.astype(jnp.int32).reshape(1), parts, got, got, got)


def _adamw_math(w, g, m, v):
    m2 = ADAM_B1 * m + (1.0 - ADAM_B1) * g
    v2 = ADAM_B2 * v + (1.0 - ADAM_B2) * (g * g)
    m_hat = m2 / (1.0 - ADAM_B1 ** ADAM_STEP)
    v_hat = v2 / (1.0 - ADAM_B2 ** ADAM_STEP)
    delta = -ADAM_LR * (m_hat / (jnp.sqrt(v_hat) + ADAM_EPS) + ADAM_WD * w)
    return delta, m2, v2


def _adamw(w, g, m, v, *, name):
    return _ew(_adamw_math, [w, g, m, v], (F32, F32, F32), name=name)


def _adamw_small(w, parts, m, v, *, name):
    n = parts.shape[0]

    def body(w_ref, p_ref, m_ref, v_ref, g_ref, d_ref, m2_ref, v2_ref):
        g = p_ref[0]
        for i in range(1, n):
            g = g + p_ref[i]
        g_ref[...] = g
        d_ref[...], m2_ref[...], v2_ref[...] = _adamw_math(w_ref[...], g, m_ref[...], v_ref[...])

    shp = jax.ShapeDtypeStruct(w.shape, F32)
    return pl.pallas_call(body, name=name, out_shape=[shp] * 4)(w, parts, m, v)


_ANY = pl.BlockSpec(memory_space=pl.ANY)


def _mesh_place():
    x, y, c = lax.axis_index("x"), lax.axis_index("y"), lax.axis_index("c")
    chips = [(1 - x, y), (x, 1 - y), (1 - x, 1 - y)]
    return x, y, c, chips


def _remote(src, dst, sems, i, to):
    send_sems, recv_sems = sems
    return pltpu.make_async_remote_copy(src_ref=src, dst_ref=dst, send_sem=send_sems.at[i], recv_sem=recv_sems.at[i],
                                        device_id=to, device_id_type=MESH_ID)


def _gather_weights(shards, *, name):
    n = len(shards)

    def body(*refs):
        ins, outs = refs[:n], refs[n:2 * n]
        sems = refs[2 * n:2 * n + 2]
        x, y, c, chips = _mesh_place()
        me = 2 * x + y
        sibling = (x, y, 1 - c)
        sent = []
        for w in range(n):
            for j, chip in enumerate(chips):
                cp = _remote(ins[w].at[c], outs[w].at[me, c], sems, 6 * w + j, (chip[0], chip[1], c))
                cp.start()
                sent.append(cp)
        for w in range(n):
            for j, chip in enumerate(chips):
                got = outs[w].at[2 * chip[0] + chip[1], c]
                _remote(got, got, sems, 6 * w + j, sibling).wait_recv()
                cp = _remote(got, got, sems, 6 * w + 3 + j, sibling)
                cp.start()
                sent.append(cp)
        for w in range(n):
            for j, chip in enumerate(chips):
                got = outs[w].at[2 * chip[0] + chip[1], 1 - c]
                _remote(got, got, sems, 6 * w + 3 + j, sibling).wait_recv()
        for cp in sent:
            cp.wait_send()

    outs = pl.pallas_call(
        body, name=name,
        in_specs=[_ANY] * n, out_specs=[_ANY] * n,
        out_shape=[jax.ShapeDtypeStruct((N_CHIPS,) + s.shape, s.dtype) for s in shards],
        scratch_shapes=[pltpu.SemaphoreType.DMA((6 * n,)), pltpu.SemaphoreType.DMA((6 * n,))],
    )(*shards)
    me = 2 * lax.axis_index("x") + lax.axis_index("y")
    return [lax.dynamic_update_index_in_dim(o, s, me, 0) for o, s in zip(outs, shards)]


def _forward_halves(gathered, *, name):
    n = len(gathered)

    def body(*refs):
        ins, outs = refs[:n], refs[n:2 * n]
        sems = refs[2 * n:2 * n + 2]
        x, y, c, chips = _mesh_place()
        sibling = (x, y, 1 - c)
        sent = []
        for w in range(n):
            for j, chip in enumerate(chips):
                peer = 2 * chip[0] + chip[1]
                cp = _remote(ins[w].at[peer, c], outs[w].at[peer, c], sems, 3 * w + j, sibling)
                cp.start()
                sent.append(cp)
        for w in range(n):
            for j, chip in enumerate(chips):
                land = outs[w].at[2 * chip[0] + chip[1], 1 - c]
                _remote(land, land, sems, 3 * w + j, sibling).wait_recv()
        for cp in sent:
            cp.wait_send()

    return pl.pallas_call(
        body, name=name,
        in_specs=[_ANY] * n, out_specs=[_ANY] * n,
        out_shape=[jax.ShapeDtypeStruct(g.shape, g.dtype) for g in gathered],
        input_output_aliases={w: w for w in range(n)},
        scratch_shapes=[pltpu.SemaphoreType.DMA((3 * n,)), pltpu.SemaphoreType.DMA((3 * n,))],
    )(*gathered)


def _exchange_siblings(grads, *, name):
    n = len(grads)

    def body(*refs):
        ins, got = refs[:n], refs[n:2 * n]
        sems = refs[2 * n:2 * n + 2]
        x, y, c, _ = _mesh_place()
        sibling = (x, y, 1 - c)
        sent = []
        for w in range(n):
            for s in range(N_CHIPS):
                cp = _remote(ins[w].at[s, 1 - c], got[w].at[s], sems, N_CHIPS * w + s, sibling)
                cp.start()
                sent.append(cp)
        for w in range(n):
            for s in range(N_CHIPS):
                _remote(got[w].at[s], got[w].at[s], sems, N_CHIPS * w + s, sibling).wait_recv()
        for cp in sent:
            cp.wait_send()

    n_sem = N_CHIPS * n
    return pl.pallas_call(
        body, name=name,
        in_specs=[_ANY] * n, out_specs=[_ANY] * n,
        out_shape=[jax.ShapeDtypeStruct((N_CHIPS,) + g.shape[2:], g.dtype) for g in grads],
        scratch_shapes=[pltpu.SemaphoreType.DMA((n_sem,)), pltpu.SemaphoreType.DMA((n_sem,))],
    )(*grads)


def _share_halves(halves, small):
    n = len(halves)

    def body(*refs):
        ins, small_ref = refs[:n], refs[n]
        outs, small_out = refs[n + 1:2 * n + 1], refs[2 * n + 1]
        sems = refs[2 * n + 2:2 * n + 4]
        x, y, c, chips = _mesh_place()
        sibling = (x, y, 1 - c)
        me = 4 * x + 2 * y + c
        sent = [_remote(ins[w], outs[w].at[c], sems, w, sibling) for w in range(n)]
        peers = [sibling] + [(ch[0], ch[1], cc) for ch in chips for cc in (c, 1 - c)]
        sent += [_remote(small_ref, small_out.at[me], sems, n + j, peer) for j, peer in enumerate(peers)]
        for cp in sent:
            cp.start()
        for w in range(n):
            _remote(outs[w].at[1 - c], outs[w].at[1 - c], sems, w, sibling).wait_recv()
        for j, peer in enumerate(peers):
            frm = small_out.at[4 * peer[0] + 2 * peer[1] + peer[2]]
            _remote(frm, frm, sems, n + j, peer).wait_recv()
        for cp in sent:
            cp.wait_send()

    n_sem = n + 7
    outs = pl.pallas_call(
        body, name="share_halves",
        in_specs=[_ANY] * (n + 1), out_specs=[_ANY] * (n + 1),
        out_shape=[jax.ShapeDtypeStruct((2,) + h.shape, h.dtype) for h in halves]
        + [jax.ShapeDtypeStruct((8,) + small.shape, small.dtype)],
        scratch_shapes=[pltpu.SemaphoreType.DMA((n_sem,)), pltpu.SemaphoreType.DMA((n_sem,))],
    )(*halves, small)
    c = lax.axis_index("c")
    me = 4 * lax.axis_index("x") + 2 * lax.axis_index("y") + c
    return ([lax.dynamic_update_index_in_dim(o, h, c, 0) for o, h in zip(outs[:n], halves)],
            lax.dynamic_update_index_in_dim(outs[n], small, me, 0))


EARLY = ("w_ff_down", "w_ff_up", "w_out", "w_branch_sb", "w_branch_fox", "w_branch_mem", "w_mem_kv")


def _norm_bwd_tail(dy, x, add, g):
    r = lax.rsqrt(jnp.mean(x * x, axis=-1, keepdims=True) + EPS)
    xh = x * r
    dyg = dy * g
    c = jnp.mean(dyg * xh, axis=-1, keepdims=True)
    return r * (dyg - xh * c) + add, jnp.sum(dy * xh, axis=0, keepdims=True)


def _split(outs, n):
    outs = list(outs) if isinstance(outs, (list, tuple)) else [outs]
    return outs[:n], outs[n:]


def _local_step(x, mem, target, small, W, gather_rest=None, reduce_early=None, reduce_late=None):
    S, D = x.shape
    o_qkv, o_mq, o_f = 3 * D, 3 * D + 2 * 3 * D_SB, 3 * D + 2 * 3 * D_SB + D_MEM
    tq = 512

    g_comms, finish_weights = gather_rest if gather_rest is not None else ([None] * 3, None)
    (proj, h), landed = _split(_mm(x, W["w_in"], name="in_proj", tb=True, tn=768, a_gain=small["g_mix_norm"],
                                   comm=g_comms[0]), 2)
    blk = lambda j: (proj, (o_qkv + j * D_SB) // LANES)
    sb_q, sb_k, sb_v, fx_q, fx_k, fx_v = [blk(j) for j in range(6)]
    m_q = (proj, o_mq // LANES)
    f_logit_t = _mm(W["w_in"][o_f:o_f + ROW_TILE], h, name="forget_logits", tb=True)[:FOX_HEADS]
    b_col = small["b_forget"].reshape(FOX_HEADS, 1)
    lanes = lambda g: jnp.tile(g, (1, LANES // g.shape[1]))
    g_fq, g_fk, g_mq, g_mk = [lanes(small[k]) for k in ("g_fox_q", "g_fox_k", "g_mem_q", "g_mem_k")]

    (o_sb, sb_tot), more = _split(_sbl_fwd(sb_q, sb_k, sb_v, width=D_SB, hd=HD, name="sb_fwd", tq=tq, comm=g_comms[1]), 2)
    landed += more

    fq = _hnorm_fwd(fx_q, g_fq, width=D_FOX, hd=HD, name="fox_q_norm")
    fk = _hnorm_fwd(fx_k, g_fk, width=D_FOX, hd=HD, name="fox_k_norm")
    f_cum = _forget_fwd(f_logit_t, b_col, name="forget_fwd")
    tkf = _tile(S, tq)
    f_bias = (f_cum.reshape(FOX_HEADS, S, 1), f_cum.reshape(FOX_HEADS, S // tkf, 1, tkf))
    (o_fox, fox_lse), more = _split(_sml_fwd((fq, 0), (fk, 0), fx_v, f_bias, width=D_FOX, hd=HD, causal=True,
                                             name="fox_fwd", tq=tq, tk=tq, comm=g_comms[2]), 2)
    landed += more
    if finish_weights is not None:
        W = {**W, **finish_weights(landed)}

    mh = _rmsnorm_fwd(mem, small["g_mem_norm"], BF16, name="mem_norm")
    mkv = _mm(mh, W["w_mem_kv"], name="mem_kv")
    mv = (mkv, D_MEM // LANES)
    mq = _hnorm_fwd(m_q, g_mq, width=D_MEM, hd=MEM_HD, name="mem_q_norm")
    mk = _hnorm_fwd((mkv, 0), g_mk, width=D_MEM, hd=MEM_HD, name="mem_k_norm")
    o_mem, mem_lse = _sml_fwd((mq, 0), (mk, 0), mv, width=D_MEM, hd=MEM_HD, causal=False, name="mem_fwd", tq=tq, tk=256)

    o3 = [o_sb, o_fox, o_mem]
    w3 = [W["w_branch_sb"], W["w_branch_fox"], W["w_branch_mem"]]
    merged, x1, h2 = _gate_fwd(o3, w3, proj, x, W["w_out"], small["g_mlp_norm"], D, name="gate_fwd")

    def relu2(acc):
        u = jnp.maximum(acc, 0.0)
        return u, u * u

    u, a = _mm(h2, W["w_ff_up"], name="ff_up", out_dtypes=(BF16, BF16), epilogue=relu2)
    def head(acc, res, tgt):
        d = (res + acc - tgt) * (1.0 / D)
        return d, d, jnp.sum(d * d, axis=0, keepdims=True)

    dy, dy16, sq_rows = _mm(a, W["w_ff_down"], name="ff_down", extras=(x1, target), out_dtypes=(F32, BF16),
                            epilogue=head, col_sums=1, tn=512)
    loss = (0.5 * D) * jnp.sum(sq_rows)

    G = {}
    du = _mm(dy16, W["w_ff_down"], name="d_ff_act", tb=True, out_dtypes=(BF16,), extras=(u,),
             epilogue=lambda acc, uu: (acc * (2.0 * uu.astype(F32)),))
    G["w_ff_down"] = _mm(a, dy16, name="d_w_ff_down", ta=True, out_dtypes=(BF16,))
    G["w_ff_up"] = _mm(h2, du, name="d_w_ff_up", ta=True, out_dtypes=(BF16,))
    dx1, dg_rows = _mm(du, W["w_ff_up"], name="d_mlp_in", tb=True, extras=(x1, dy, small["g_mlp_norm"]),
                       epilogue=_norm_bwd_tail, col_sums=1, tm=512, tn=D)
    dg_mlp = jnp.sum(dg_rows, axis=0, keepdims=True)
    G["w_out"] = _mm(merged, dx1, name="d_w_out", ta=True, out_dtypes=(BF16,))
    dgate, db0, db1, db2, do_sb, do_fox, do_mem = _gate_bwd(o3, w3, proj, dx1, W["w_out"], D, name="gate_bwd")
    for nm, o, db in zip(("w_branch_sb", "w_branch_fox", "w_branch_mem"), o3, (db0, db1, db2)):
        G[nm] = _mm(o, db, name="d_" + nm, ta=True, out_dtypes=(BF16,))

    dmq_n, dmk_n, dmv = _sml_bwd((mq, 0), (mk, 0), mv, (o_mem, 0), (mem_lse, 0), (do_mem, 0), width=D_MEM, hd=MEM_HD,
                                 causal=False, name="mem_bwd", tq=tq, tk=256)
    dm_q, dg_mem_q = _hnorm_bwd(m_q, g_mq, dmq_n, width=D_MEM, hd=MEM_HD, name="d_mem_q_norm")
    dmk_raw, dg_mem_k = _hnorm_bwd((mkv, 0), g_mk, dmk_n, width=D_MEM, hd=MEM_HD, name="d_mem_k_norm")
    dmkv = jnp.concatenate([dmk_raw, dmv.astype(BF16)], axis=1)
    G["w_mem_kv"] = _mm(mh, dmkv, name="d_w_mem_kv", ta=True, out_dtypes=(BF16,))
    dmh = _mm(dmkv, W["w_mem_kv"], name="d_mem_h", tb=True)
    _, dg_mem = _rmsnorm_bwd(mem, small["g_mem_norm"], dmh, name="d_mem_norm")

    r_comms, r_finish = reduce_early({k: G.pop(k) for k in EARLY}) if reduce_early is not None else ([None] * 2, None)
    dsb, landed_sb = _split(_sbl_bwd(sb_q, sb_k, sb_v, (do_sb, 0), (sb_tot, 0), width=D_SB, hd=HD, name="sb_bwd", tq=tq,
                                     comm=r_comms[0]), 3)
    (dfq, dfk, dfv, df_row, df_col), landed_fox = _split(
        _sml_bwd((fq, 0), (fk, 0), fx_v, (o_fox, 0), (fox_lse, 0), (do_fox, 0), f_bias, width=D_FOX, hd=HD, causal=True,
                 name="fox_bwd", tq=tq, tk=tq, comm=r_comms[1]), 5)
    early = r_finish(landed_sb, landed_fox) if r_finish is not None else {}
    dfx_q, dg_fox_q = _hnorm_bwd(fx_q, g_fq, dfq, width=D_FOX, hd=HD, name="d_fox_q_norm")
    dfx_k, dg_fox_k = _hnorm_bwd(fx_k, g_fk, dfk, width=D_FOX, hd=HD, name="d_fox_k_norm")
    d_fcum = df_row.reshape(FOX_HEADS, S) - df_col.reshape(FOX_HEADS, S)
    d_flogit_t, db_forget = _forget_bwd(f_logit_t, b_col, d_fcum, name="forget_bwd")
    dg_fox_q, dg_fox_k = dg_fox_q[:, :HD], dg_fox_k[:, :HD]

    rest_cols = jnp.concatenate([t.astype(BF16) for t in (*dsb, dfx_q, dfx_k, dfv, dm_q)]
                                + [d_flogit_t.T.astype(BF16), jnp.zeros((S, F_PAD - FOX_HEADS), BF16)], axis=1)
    dproj = lax.dynamic_update_slice(dgate, rest_cols, (0, 3 * D))
    g_w_in = _mm(dproj, h, name="d_w_in", ta=True, out_dtypes=(BF16,), tm=768)
    comm, finish = reduce_late({"w_in": g_w_in}) if reduce_late is not None else (None, None)
    (grad_x, dg_rows), landed = _split(
        _mm(dproj, W["w_in"], name="d_mix_in", tk=2304, tm=512, tn=D, extras=(x, dx1, small["g_mix_norm"]),
            epilogue=_norm_bwd_tail, col_sums=1, comm=comm), 2)
    dg_mix = jnp.sum(dg_rows, axis=0, keepdims=True)
    if finish is None:
        G["w_in"] = g_w_in
    else:
        early.update(finish(landed))

    small_grads = dict(g_mix_norm=dg_mix, g_mem_norm=dg_mem, b_forget=db_forget.reshape(1, FOX_HEADS),
                       g_fox_q=dg_fox_q, g_fox_k=dg_fox_k, g_mem_q=dg_mem_q, g_mem_k=dg_mem_k, g_mlp_norm=dg_mlp)
    return loss, grad_x, G, small_grads, early


BIG = ("w_in", "w_mem_kv", "w_branch_sb", "w_branch_fox", "w_branch_mem", "w_out", "w_ff_up", "w_ff_down")
COLUMN_SHARDED = ("w_in", "w_branch_sb", "w_branch_fox", "w_branch_mem", "w_ff_up")
SMALL = ("g_mix_norm", "g_mem_norm", "b_forget", "g_fox_q", "g_fox_k", "g_mem_q", "g_mem_k", "g_mlp_norm")
ORDER = ("g_mix_norm", "g_mem_norm", "w_in", "b_forget", "g_fox_q", "g_fox_k", "g_mem_q", "g_mem_k", "w_mem_kv",
         "w_branch_sb", "w_branch_fox", "w_branch_mem", "w_out", "g_mlp_norm", "w_ff_up", "w_ff_down")


def _unshard(name, gathered):
    n, _, rh, c = gathered.shape
    t = gathered.reshape(n, 2 * rh, c)
    if name in COLUMN_SHARDED:
        return t.transpose(1, 0, 2).reshape(2 * rh, n * c)
    return t.reshape(n * 2 * rh, c)


def _reshard(name, full):
    if name in COLUMN_SHARDED:
        r, c = full.shape
        t = full.reshape(r, N_CHIPS, c // N_CHIPS).transpose(1, 0, 2)
    else:
        r, c = full.shape[0] // N_CHIPS, full.shape[1]
        t = full.reshape(N_CHIPS, r, c)
    return t.reshape(N_CHIPS, 2, t.shape[1] // 2, t.shape[2])


ROW_TILE = 16
IN_BUF_ALIGN = 256


def _in_segments(D):
    n_qkv = 6 * D_SB
    o_mq, o_gate = n_qkv + FOX_HEADS, n_qkv + FOX_HEADS + D_MEM
    return [(0, n_qkv, 3 * D), (n_qkv, o_mq, 3 * D + n_qkv + D_MEM), (o_mq, o_gate, 3 * D + n_qkv), (o_gate, o_gate + 3 * D, 0)]


class _InLayout:
    def __init__(self, D, shard, n):
        self.D, self.shard, self.n = D, shard, n
        down = lambda v: v // ROW_TILE * ROW_TILE
        up = lambda v: -(-v // ROW_TILE) * ROW_TILE
        self.pieces = []
        ends = []
        for s in range(n):
            cursor, mine = 0, []
            for a, b, p in _in_segments(D):
                x0, x1 = max(a, s * shard), min(b, (s + 1) * shard)
                if x0 < x1:
                    p0 = p + x0 - a
                    rows = up(p0 + x1 - x0) - down(p0)
                    mine.append((x0 - s * shard, x1 - x0, p0, cursor, rows))
                    cursor += rows
            self.pieces.append(mine)
            ends.append(cursor)
        self.rows = -(-max(ends) // IN_BUF_ALIGN) * IN_BUF_ALIGN
        self.padded_rows = 3 * D + 6 * D_SB + D_MEM + F_PAD

    def _per_shard(self, fn, chip, operand):
        return lax.switch(chip, [functools.partial(fn, s) for s in range(self.n)], operand)

    def pack(self, chip, rows):
        def one(s, t):
            out, at = [], 0
            for x0, n_rows, p0, start, region in self.pieces[s]:
                lead = p0 % ROW_TILE
                out += [jnp.zeros((start + lead - at, t.shape[1]), t.dtype), t[x0:x0 + n_rows]]
                at = start + lead + n_rows
            return jnp.concatenate(out + [jnp.zeros((self.rows - at, t.shape[1]), t.dtype)], axis=0)
        return self._per_shard(one, chip, rows)

    def unpack(self, chip, buf, pad_to):
        def one(s, t):
            out = [t[start + p0 % ROW_TILE:start + p0 % ROW_TILE + n_rows] for _, n_rows, p0, start, _ in self.pieces[s]]
            return jnp.concatenate(out + [jnp.zeros((pad_to - self.shard, t.shape[1]), t.dtype)], axis=0)
        return self._per_shard(one, chip, buf)

    def to_padded(self, bufs):
        runs = sorted((p0, s, start, region) for s in range(self.n) for _, _, p0, start, region in self.pieces[s])
        chunks, end = [], 0
        for p0, s, start, region in runs:
            d0 = p0 // ROW_TILE * ROW_TILE
            src = bufs[s, start:start + region]
            if d0 < end:
                assert end - d0 == ROW_TILE
                last = chunks.pop()
                chunks += [last[:-ROW_TILE], last[-ROW_TILE:] + src[:ROW_TILE], src[ROW_TILE:]]
            else:
                if d0 > end:
                    chunks.append(jnp.zeros((d0 - end, bufs.shape[2]), bufs.dtype))
                chunks.append(src)
            end = d0 + region
        chunks.append(jnp.zeros((self.padded_rows - end, bufs.shape[2]), bufs.dtype))
        return jnp.concatenate(chunks, axis=0)

    def from_padded(self, gp):
        bufs = []
        for s in range(self.n):
            out, at = [], 0
            for _, n_rows, p0, start, region in self.pieces[s]:
                d0 = p0 // ROW_TILE * ROW_TILE
                row = d0 + lax.broadcasted_iota(jnp.int32, (region, 1), 0)
                out.append(jnp.where((row >= p0) & (row < p0 + n_rows), gp[d0:d0 + region], jnp.zeros((), gp.dtype)))
                at = start + region
            bufs.append(jnp.concatenate(out + [jnp.zeros((self.rows - at, gp.shape[1]), gp.dtype)], axis=0))
        return jnp.stack(bufs)


SMALL_ROWS = 16


def _pack_small(vals, scalar=None):
    width = max(vals[k].shape[1] for k in SMALL)
    rows = [jnp.pad(vals[k].astype(F32), ((0, 0), (0, width - vals[k].shape[1]))) for k in SMALL]
    extra = jnp.zeros((SMALL_ROWS - len(SMALL), width), F32)
    if scalar is not None:
        extra = extra.at[0, 0].set(scalar)
    return jnp.concatenate(rows + [extra], axis=0)


def _unpack_small(packed, like):
    return {k: packed[i:i + 1, :like[k].shape[1]] for i, k in enumerate(SMALL)}


def kernel(x, mem, g_mix_norm, g_mem_norm, w_in, b_forget, g_fox_q, g_fox_k, g_mem_q, g_mem_k, w_mem_kv, w_branch_sb, w_branch_fox, w_branch_mem, w_out, g_mlp_norm, w_ff_up, w_ff_down, loss_target, m_g_mix_norm, m_g_mem_norm, m_w_in, m_b_forget, m_g_fox_q, m_g_fox_k, m_g_mem_q, m_g_mem_k, m_w_mem_kv, m_w_branch_sb, m_w_branch_fox, m_w_branch_mem, m_w_out, m_g_mlp_norm, m_w_ff_up, m_w_ff_down, v_g_mix_norm, v_g_mem_norm, v_w_in, v_b_forget, v_g_fox_q, v_g_fox_k, v_g_mem_q, v_g_mem_k, v_w_mem_kv, v_w_branch_sb, v_w_branch_fox, v_w_branch_mem, v_w_out, v_g_mlp_norm, v_w_ff_up, v_w_ff_down):
    given = dict(locals())
    D = x.shape[-1]
    weights = {k: given[k] for k in ORDER}
    moms = {k: given["m_" + k] for k in ORDER}
    vars_ = {k: given["v_" + k] for k in ORDER}

    me_chip = 2 * lax.axis_index("x") + lax.axis_index("y")

    n_in = w_in.shape[2]
    lay = _InLayout(D, n_in, N_CHIPS)
    transposed = lambda t: jnp.transpose(t[0])
    shards = {}
    for k in BIG:
        w = weights[k][0].astype(BF16)
        if k == "w_in":
            w = lay.pack(me_chip, jnp.transpose(w))
        shards[k] = w.reshape(2, w.shape[0] // 2, w.shape[1])
    gathered_in = _gather_weights([shards["w_in"]], name="gather_w_in")[0]
    W = {"w_in": lay.to_padded(gathered_in.reshape(N_CHIPS, lay.rows, D))}
    carried = (("w_branch_sb", "w_branch_fox", "w_branch_mem", "w_out"), ("w_ff_up", "w_mem_kv"), ("w_ff_down",))
    rest = [k for grp in carried for k in grp]
    assert sorted(rest + ["w_in"]) == sorted(BIG)

    def finish_weights(landed):
        full = _forward_halves(landed, name="forward_halves")
        full = [lax.dynamic_update_index_in_dim(o, shards[k], me_chip, 0) for k, o in zip(rest, full)]
        return {k: _unshard(k, g) for k, g in zip(rest, full)}

    me_core = lax.axis_index("c")

    def sum_chips(names, parts, got):
        return {k: _chip_sum(p, q, me_chip, name="sum_chips_" + k) for k, p, q in zip(names, parts, got)}

    def pair_sums(grads, tag):
        names = list(grads)
        stacked = {k: _reshard(k, grads[k]) for k in names if k != "w_in"}
        if "w_in" in grads:
            stacked["w_in"] = lay.from_padded(grads["w_in"]).reshape(N_CHIPS, 2, lay.rows // 2, D)
        got = _exchange_siblings([stacked[k] for k in names], name="exchange_siblings_" + tag)
        return {k: _pair_sum(stacked[k], q, me_core, name="sum_pair_" + k) for k, q in zip(names, got)}

    def reduce_early(grads):
        parts = pair_sums(grads, "early")
        groups = [[k for k in parts if k in ("w_ff_down", "w_ff_up")], [k for k in parts if k not in ("w_ff_down", "w_ff_up")]]
        comms = [_ChipExchange("scatter", [parts[k] for k in grp]) for grp in groups]

        def finish(*landed):
            out = {}
            for grp, got in zip(groups, landed):
                out.update(sum_chips(grp, [parts[k] for k in grp], got))
            return out
        return comms, finish

    def reduce_late(grads):
        parts = pair_sums(grads, "late")
        names = list(parts)
        return _ChipExchange("scatter", [parts[k] for k in names]), functools.partial(sum_chips, names, [parts[k] for k in names])

    small = {k: weights[k] for k in SMALL}
    loss_part, grad_x, G, small_grads, halves = _local_step(
        x[0], mem[0], loss_target[0], small, W,
        gather_rest=([_ChipExchange("gather", [shards[k] for k in grp]) for grp in carried], finish_weights),
        reduce_early=reduce_early, reduce_late=reduce_late)
    assert not G, list(G)
    reduced, small_parts = _share_halves([halves[k] for k in BIG], _pack_small(small_grads, loss_part))

    grads, deltas, new_m, new_v = {}, {}, {}, {}
    for k, g in zip(BIG, reduced):
        shp = weights[k].shape
        if k == "w_in":
            g2 = lay.unpack(me_chip, g.reshape(lay.rows, D), lay.rows)
            padded = lambda t: jnp.pad(transposed(t), ((0, lay.rows - n_in), (0, 0)))
            outs = _adamw(padded(weights[k]), g2, padded(moms[k]), padded(vars_[k]), name="adamw_" + k)
            g2, d, m2, v2 = [jnp.transpose(t[:n_in]) for t in (g2, *outs)]
        else:
            g2 = g.reshape(shp[1], shp[2])
            d, m2, v2 = _adamw(weights[k][0], g2, moms[k][0], vars_[k][0], name="adamw_" + k)
        grads[k], deltas[k], new_m[k], new_v[k] = g2.reshape(shp), d.reshape(shp), m2.reshape(shp), v2.reshape(shp)
    sg, sd, sm, sv = _adamw_small(_pack_small(small), small_parts, _pack_small({k: moms[k] for k in SMALL}),
                                  _pack_small({k: vars_[k] for k in SMALL}), name="adamw_small")
    for dst, packed in ((grads, sg), (deltas, sd), (new_m, sm), (new_v, sv)):
        dst.update(_unpack_small(packed, small))

    loss = sg[len(SMALL), 0]
    return (loss, grad_x[None], *[grads[k] for k in ORDER], *[deltas[k] for k in ORDER],
            *[new_m[k] for k in ORDER], *[new_v[k] for k in ORDER])
```

```python
import functools

import jax
import jax.numpy as jnp
from jax import lax
from jax.experimental import pallas as pl
from jax.experimental.pallas import tpu as pltpu

F32 = jnp.float32
BF16 = jnp.bfloat16
MESH_ID = pl.DeviceIdType.MESH

HD = 64
SB_HEADS = 8
FOX_HEADS = 8
MEM_HEADS = 4
MEM_HD = 128
D_SB = SB_HEADS * HD
D_FOX = FOX_HEADS * HD
D_MEM = MEM_HEADS * MEM_HD
EPS = 1e-6
NEG_INF = -1e30

ADAM_LR = 0.001
ADAM_B1 = 0.9
ADAM_B2 = 0.999
ADAM_EPS = 1e-08
ADAM_WD = 0.01
ADAM_STEP = 10

N_CHIPS = 4
VMEM_LIMIT = 56 * 1024 * 1024

F_PAD = 256


def _tile(n, target, align=128):
    if n <= target:
        return n
    best = None
    t = align
    while t <= target:
        if n % t == 0:
            best = t
        t += align
    assert best is not None, (n, target, align)
    return best


def _params(sem):
    return pltpu.CompilerParams(dimension_semantics=sem, vmem_limit_bytes=VMEM_LIMIT)


def _mm(a, b, *, name, ta=False, tb=False, out_dtypes=(F32,), epilogue=None, extras=(),
        tm=1024, tn=1024, tk=2048, comm=None, col_sums=0, a_gain=None):
    if ta:
        K, M = a.shape
    else:
        M, K = a.shape
    if tb:
        N, K2 = b.shape
    else:
        K2, N = b.shape
    assert K == K2, (a.shape, b.shape, ta, tb)
    tm, tn, tk = _tile(M, tm), _tile(N, tn), _tile(K, tk)
    nk = K // tk
    normed = a_gain is not None
    assert not normed or (nk == 1 and not ta)
    n_extra, n_out = len(extras) + normed, len(out_dtypes) + col_sums + normed
    if epilogue is None:
        epilogue = lambda acc: (acc,)
    dims = (((0 if ta else 1,), (1 if tb else 0,)), ((), ()))

    gm, gn = M // tm, N // tn

    def body(*refs):
        i, j, k = pl.program_id(0), pl.program_id(1), pl.program_id(2)
        got = _carry(comm, 2 + n_extra, n_out, (i == 0) & (j == 0) & (k == 0),
                     (i == gm - 1) & (j == gn - 1) & (k == nk - 1), refs)
        (a_ref, b_ref, *extra_refs), out_refs = got[0], list(got[1])
        if normed:
            gain_ref, normed_ref = extra_refs.pop(), out_refs.pop()
            av = a_ref[...].astype(F32)
            av = (av * lax.rsqrt(jnp.mean(av * av, axis=-1, keepdims=True) + EPS) * gain_ref[...]).astype(BF16)

            @pl.when(j == 0)
            def _():
                normed_ref[...] = av
        else:
            av = a_ref[...].astype(BF16)
        part = lax.dot_general(av, b_ref[...].astype(BF16), dims, preferred_element_type=F32)

        def finish(acc):
            outs = epilogue(acc, *[r[...] for r in extra_refs])
            for o_ref, o in zip(out_refs[:len(out_dtypes)], outs):
                o_ref[...] = o.astype(o_ref.dtype)
            for o_ref, o in zip(out_refs[len(out_dtypes):], outs[len(out_dtypes):]):
                first = lax.broadcasted_iota(jnp.int32, o_ref.shape, 0) == 0
                o_ref[...] = jnp.where(first, jnp.broadcast_to(o, o_ref.shape), 0.0)

        if nk == 1:
            finish(part)
        else:
            acc_ref = refs[-1]

            @pl.when(k == 0)
            def _():
                acc_ref[...] = part

            @pl.when((k > 0) & (k < nk - 1))
            def _():
                acc_ref[...] += part

            @pl.when(k == nk - 1)
            def _():
                finish(acc_ref[...] + part)

        got[2]()

    a_spec = pl.BlockSpec((tk, tm), lambda i, j, k: (k, i)) if ta else pl.BlockSpec((tm, tk), lambda i, j, k: (i, k))
    b_spec = pl.BlockSpec((tn, tk), lambda i, j, k: (j, k)) if tb else pl.BlockSpec((tk, tn), lambda i, j, k: (k, j))
    mn_spec = pl.BlockSpec((tm, tn), lambda i, j, k: (i, j))
    row_spec = pl.BlockSpec((1, tn), lambda i, j, k: (0, j))
    sum_spec = pl.BlockSpec((8, tn), lambda i, j, k: (i, j))
    c_ins, c_in_specs, c_out_specs, c_out_shape, c_scratch = _comm_args(comm)
    sem = ("parallel", "arbitrary" if normed else "parallel", "arbitrary") if comm is None else ("arbitrary",) * 3
    outs = pl.pallas_call(
        body, name=name,
        grid=(gm, gn, nk),
        in_specs=[a_spec, b_spec] + [row_spec if e.shape[0] == 1 else mn_spec for e in extras]
        + ([pl.BlockSpec((1, K), lambda i, j, k: (0, 0))] if normed else []) + c_in_specs,
        out_specs=[mn_spec] * len(out_dtypes) + [sum_spec] * col_sums
        + ([pl.BlockSpec((tm, K), lambda i, j, k: (i, 0))] if normed else []) + c_out_specs,
        out_shape=[jax.ShapeDtypeStruct((M, N), dt) for dt in out_dtypes]
        + [jax.ShapeDtypeStruct((8 * gm, N), F32)] * col_sums
        + ([jax.ShapeDtypeStruct((M, K), BF16)] if normed else []) + c_out_shape,
        scratch_shapes=c_scratch + ([pltpu.VMEM((tm, tn), F32)] if nk > 1 else []),
        compiler_params=_params(sem),
    )(a, b, *extras, *([a_gain] if normed else []), *c_ins)
    return outs if len(outs) > 1 else outs[0]


def _row_tile(rows, cols, n_arrays):
    budget = 10 * 1024 * 1024
    cols_padded = -(-cols // 128) * 128
    target = max(16, budget // (cols_padded * 4 * n_arrays * 2))
    return _tile(rows, target, align=16)


def _ew(fn, ins, out_dtypes, *, name):
    R, C = ins[0].shape
    n_in, n_out = len(ins), len(out_dtypes)
    tr = _row_tile(R, C, n_in + n_out)

    def body(*refs):
        outs = fn(*[r[...] for r in refs[:n_in]])
        for o_ref, o in zip(refs[n_in:], outs):
            o_ref[...] = o.astype(o_ref.dtype)

    spec = pl.BlockSpec((tr, C), lambda i: (i, 0))
    outs = pl.pallas_call(
        body, name=name, grid=(R // tr,),
        in_specs=[spec] * n_in, out_specs=[spec] * n_out,
        out_shape=[jax.ShapeDtypeStruct((R, C), dt) for dt in out_dtypes],
        compiler_params=_params(("parallel",)),
    )(*ins)
    return outs if n_out > 1 else outs[0]


def _rmsnorm_fwd(x, g, out_dtype, *, name):
    R, d = x.shape
    tr = _row_tile(R, d, 3)

    def body(x_ref, g_ref, o_ref):
        xv = x_ref[...].astype(F32)
        r = lax.rsqrt(jnp.mean(xv * xv, axis=-1, keepdims=True) + EPS)
        o_ref[...] = (xv * r * g_ref[...]).astype(o_ref.dtype)

    return pl.pallas_call(
        body, name=name, grid=(R // tr,),
        in_specs=[pl.BlockSpec((tr, d), lambda i: (i, 0)), pl.BlockSpec((1, d), lambda i: (0, 0))],
        out_specs=pl.BlockSpec((tr, d), lambda i: (i, 0)),
        out_shape=jax.ShapeDtypeStruct((R, d), out_dtype),
        compiler_params=_params(("parallel",)),
    )(x, g)


def _rmsnorm_bwd(x, g, dy, add=None, *, name):
    R, d = x.shape
    has_add = add is not None
    tr = _row_tile(R, d, 5)

    def body(*refs):
        x_ref, g_ref, dy_ref = refs[:3]
        add_ref = refs[3] if has_add else None
        dx_ref, dg_ref = refs[-2:]
        xv = x_ref[...].astype(F32)
        dyv = dy_ref[...].astype(F32)
        r = lax.rsqrt(jnp.mean(xv * xv, axis=-1, keepdims=True) + EPS)
        xh = xv * r
        dyg = dyv * g_ref[...]
        c = jnp.mean(dyg * xh, axis=-1, keepdims=True)
        dx = r * (dyg - xh * c)
        if has_add:
            dx = dx + add_ref[...]
        dx_ref[...] = dx

        @pl.when(pl.program_id(0) == 0)
        def _():
            dg_ref[...] = jnp.zeros_like(dg_ref)

        dg_ref[...] += jnp.sum(dyv * xh, axis=0, keepdims=True)

    row = pl.BlockSpec((tr, d), lambda i: (i, 0))
    vec = pl.BlockSpec((1, d), lambda i: (0, 0))
    ins = [x, g, dy] + ([add] if has_add else [])
    return pl.pallas_call(
        body, name=name, grid=(R // tr,),
        in_specs=[row, vec, row] + ([row] if has_add else []),
        out_specs=[row, vec],
        out_shape=[jax.ShapeDtypeStruct((R, d), F32), jax.ShapeDtypeStruct((1, d), F32)],
        compiler_params=_params(("arbitrary",)),
    )(*ins)


_NT = (((1,), (1,)), ((), ()))
_TN = (((0,), (0,)), ((), ()))


def _dot(a, b, dims=(((1,), (0,)), ((), ()))):
    return lax.dot_general(a, b, dims, preferred_element_type=F32)


def _log_sigmoid_pair(z):
    sp = jnp.log(1.0 + jnp.exp(-jnp.abs(z)))
    return jnp.minimum(z, 0.0) - sp, jnp.minimum(-z, 0.0) - sp


LANES = 128
_LOW = -3e38


def _lane_masks(hd, rows):
    if hd == LANES:
        return [None]
    lane = lax.broadcasted_iota(jnp.int32, (rows, LANES), 1)
    return [(lane >= hh * hd) & (lane < (hh + 1) * hd) for hh in range(LANES // hd)]


def _keep(t, m):
    return t if m is None else jnp.where(m, t, 0.0)


def _merge(parts, masks):
    out = parts[-1]
    for p, m in zip(parts[-2::-1], masks[-2::-1]):
        out = jnp.where(m, p, out)
    return out


def _row_value(t, m):
    return jnp.max(t if m is None else jnp.where(m, t, _LOW), axis=1, keepdims=True)


def _cols(tq, off):
    return pl.BlockSpec((tq, LANES), lambda g, i: (i, off + g))


def _cols_all(rows, off):
    return pl.BlockSpec((rows, LANES), lambda g, i: (0, off + g))


SCAN_BLOCK = 256


def _tri(kind, cols):
    n = min(SCAN_BLOCK, cols)
    r = lax.broadcasted_iota(jnp.int32, (n, n), 0)
    c = lax.broadcasted_iota(jnp.int32, (n, n), 1)
    return ((r > c) if kind == "after" else (r < c)).astype(BF16)


def _scan_cols(x, tri, reverse):
    cols = x.shape[1]
    cb = min(SCAN_BLOCK, cols)
    assert cols % cb == 0 and tri.shape == (cb, cb)
    nb = cols // cb
    blocks = [x[:, b * cb:(b + 1) * cb] for b in range(nb)]
    outs, carry = [None] * nb, None
    for b in (reversed(range(nb)) if reverse else range(nb)):
        y = _dot(blocks[b].astype(BF16), tri)
        outs[b] = y if carry is None else y + carry
        s = jnp.sum(blocks[b], axis=1, keepdims=True)
        carry = s if carry is None else carry + s
    return (outs[0] if nb == 1 else jnp.concatenate(outs, axis=1)), carry


def _softplus_parts(z):
    pos = jnp.maximum(z, 0.0) + jnp.log(1.0 + jnp.exp(-jnp.abs(z)))
    return pos, z - pos


class _ChipExchange:
    def __init__(self, kind, ins):
        assert kind in ("gather", "scatter")
        self.kind, self.ins = kind, list(ins)
        lead = (lambda s: (N_CHIPS,) + s) if kind == "gather" else (lambda s: (3,) + s[1:])
        self.out_shape = [jax.ShapeDtypeStruct(lead(a.shape), a.dtype) for a in ins]
        n = 3 * len(ins)
        self.scratch = [pltpu.SemaphoreType.DMA((n,)), pltpu.SemaphoreType.DMA((n,))]

    def _copies(self, in_refs, out_refs, sems, landing):
        x, y, c, chips = _mesh_place()
        me = 2 * x + y
        out = []
        for w in range(len(self.ins)):
            for j, chip in enumerate(chips):
                peer = 2 * chip[0] + chip[1]
                if self.kind == "gather":
                    src, dst, land = in_refs[w].at[c], out_refs[w].at[me, c], out_refs[w].at[peer, c]
                else:
                    src, dst, land = in_refs[w].at[peer], out_refs[w].at[j], out_refs[w].at[j]
                if landing:
                    src, dst = land, land
                out.append(_remote(src, dst, sems, 3 * w + j, (chip[0], chip[1], c)))
        return out

    def start(self, in_refs, out_refs, sems):
        for cp in self._copies(in_refs, out_refs, sems, False):
            cp.start()

    def finish(self, in_refs, out_refs, sems):
        for cp in self._copies(in_refs, out_refs, sems, True):
            cp.wait_recv()
        for cp in self._copies(in_refs, out_refs, sems, False):
            cp.wait_send()


def _carry(comm, n_in, n_out, first, last, refs):
    if comm is None:
        return refs[:n_in], refs[n_in:n_in + n_out], (lambda: None)
    a, b = len(comm.ins), len(comm.out_shape)
    ins, c_in = refs[:n_in], refs[n_in:n_in + a]
    outs, c_out = refs[n_in + a:n_in + a + n_out], refs[n_in + a + n_out:n_in + a + n_out + b]
    sems = refs[n_in + a + n_out + b:n_in + a + n_out + b + 2]
    pl.when(first)(lambda: comm.start(c_in, c_out, sems))
    return ins, outs, (lambda: pl.when(last)(lambda: comm.finish(c_in, c_out, sems)))


def _sbl_fwd(q, k, v, *, width, hd, name, tq=256, comm=None):
    (qa, qo), (ka, ko), (va, vo) = q, k, v
    S = qa.shape[0]
    tq = _tile(S, tq)
    tk = tq
    scale = hd ** -0.5
    n_g, n_q = width // LANES, S // tq

    def body(*refs):
        qi = pl.program_id(1)
        gi = pl.program_id(0)
        got = _carry(comm, 3, 2, (gi == 0) & (qi == 0), (gi == n_g - 1) & (qi == n_q - 1), refs)
        (q_ref, k_ref, v_ref), (o_ref, tot_ref) = got[0], got[1]
        masks = _lane_masks(hd, tq)
        qs = q_ref[...].astype(F32) * scale
        qm = [_keep(qs, m).astype(BF16) for m in masks]
        strict = lax.broadcasted_iota(jnp.int32, (tq, tk), 1) < lax.broadcasted_iota(jnp.int32, (tq, tk), 0)
        later = _tri("after", tk)

        def tile(kb, carry, diag):
            ks = pl.multiple_of(kb * tk, tk)
            kv = k_ref[pl.ds(ks, tk), :].astype(BF16)
            vv = v_ref[pl.ds(ks, tk), :].astype(BF16)
            out = []
            for hh in range(len(masks)):
                acc, c_pos = carry[2 * hh], carry[2 * hh + 1]
                pos, ls = _softplus_parts(_dot(qm[hh], kv, _NT))
                if diag:
                    pos = jnp.where(strict, pos, 0.0)
                pos_after, pos_all = _scan_cols(pos, later, True)
                w = jnp.exp(ls - (pos_after + c_pos))
                if diag:
                    w = jnp.where(strict, w, 0.0)
                out += [acc + _dot(w.astype(BF16), vv), c_pos + pos_all]
            return tuple(out)

        init = (jnp.zeros((tq, LANES), F32), jnp.zeros((tq, 1), F32)) * len(masks)
        carry = tile(qi, init, True)
        carry = lax.fori_loop(0, qi, lambda i, c: tile(qi - 1 - i, c, False), carry)
        o_ref[...] = _merge(carry[0::2], masks).astype(o_ref.dtype)
        tot_ref[...] = _merge([jnp.broadcast_to(-c, (tq, LANES)) for c in carry[1::2]], masks)
        got[2]()

    c_ins, c_in_specs, c_out_specs, c_out_shape, c_scratch = _comm_args(comm)
    return pl.pallas_call(
        body, name=name, grid=(n_g, n_q),
        in_specs=[_cols(tq, qo), _cols_all(S, ko), _cols_all(S, vo)] + c_in_specs,
        out_specs=[_cols(tq, 0), _cols(tq, 0)] + c_out_specs,
        out_shape=[jax.ShapeDtypeStruct((S, width), BF16), jax.ShapeDtypeStruct((S, width), F32)] + c_out_shape,
        scratch_shapes=c_scratch,
        compiler_params=_params(("arbitrary", "arbitrary")),
    )(qa, ka, va, *c_ins)


def _comm_args(comm):
    if comm is None:
        return [], [], [], [], []
    return comm.ins, [_ANY] * len(comm.ins), [_ANY] * len(comm.out_shape), comm.out_shape, comm.scratch


def _sbl_bwd(q, k, v, do, tot, *, width, hd, name, tq=256, comm=None):
    (qa, qo), (ka, ko), (va, vo) = q, k, v
    S = qa.shape[0]
    tq = _tile(S, tq)
    tk = tq
    scale = hd ** -0.5
    n_g, n_q = width // LANES, S // tq

    def body(*refs):
        qi = pl.program_id(1)
        gi = pl.program_id(0)
        got = _carry(comm, 5, 3, (gi == 0) & (qi == 0), (gi == n_g - 1) & (qi == n_q - 1), refs)
        (q_ref, k_ref, v_ref, do_ref, tot_ref), (dq_ref, dk_ref, dv_ref) = got[0], got[1]

        @pl.when(qi == 0)
        def _():
            dk_ref[...] = jnp.zeros_like(dk_ref)
            dv_ref[...] = jnp.zeros_like(dv_ref)

        masks = _lane_masks(hd, tq)
        qs = q_ref[...].astype(F32) * scale
        qm = [_keep(qs, m).astype(BF16) for m in masks]
        dov = [_keep(do_ref[...], m).astype(BF16) for m in masks]
        rest = [-_row_value(tot_ref[...], m) for m in masks]
        strict = lax.broadcasted_iota(jnp.int32, (tq, tk), 1) < lax.broadcasted_iota(jnp.int32, (tq, tk), 0)
        later, before = _tri("after", tk), _tri("before", tk)

        def tile(kb, carry, diag):
            ks = pl.multiple_of(kb * tk, tk)
            kv = k_ref[pl.ds(ks, tk), :].astype(BF16)
            vv = v_ref[pl.ds(ks, tk), :].astype(BF16)
            out = []
            dk_t, dv_t = None, None
            for hh in range(len(masks)):
                dq, c_pos, c_g = carry[3 * hh:3 * hh + 3]
                pos, ls = _softplus_parts(_dot(qm[hh], kv, _NT))
                if diag:
                    pos = jnp.where(strict, pos, 0.0)
                pos_after, pos_all = _scan_cols(pos, later, True)
                c_pos = c_pos + pos_all
                w = jnp.exp(ls - (pos_after + (rest[hh] - c_pos)))
                if diag:
                    w = jnp.where(strict, w, 0.0)
                g = _dot(dov[hh], vv, _NT) * w
                g_before, g_all = _scan_cols(g, before, False)
                g_before = g_before + c_g
                dz = g - jnp.exp(ls) * (g + g_before)
                if diag:
                    dz = jnp.where(strict, dz, 0.0)
                dzb = dz.astype(BF16)
                dk_h = _dot(dzb, qm[hh], _TN)
                dv_h = _dot(w.astype(BF16), dov[hh], _TN)
                dk_t = dk_h if dk_t is None else dk_t + dk_h
                dv_t = dv_h if dv_t is None else dv_t + dv_h
                out += [dq + _dot(dzb, kv), c_pos, c_g + g_all]
            dk_ref[pl.ds(ks, tk), :] += dk_t
            dv_ref[pl.ds(ks, tk), :] += dv_t
            return tuple(out)

        zero = jnp.zeros((tq, 1), F32)
        init = (jnp.zeros((tq, LANES), F32), zero, zero) * len(masks)
        carry = lax.fori_loop(0, qi, lambda kb, c: tile(kb, c, False), init)
        carry = tile(qi, carry, True)
        dq_ref[...] = _merge(carry[0::3], masks) * scale
        got[2]()

    full = jax.ShapeDtypeStruct((S, width), F32)
    c_ins, c_in_specs, c_out_specs, c_out_shape, c_scratch = _comm_args(comm)
    return pl.pallas_call(
        body, name=name, grid=(n_g, n_q),
        in_specs=[_cols(tq, qo), _cols_all(S, ko), _cols_all(S, vo), _cols(tq, do[1]), _cols(tq, tot[1])] + c_in_specs,
        out_specs=[_cols(tq, 0), _cols_all(S, 0), _cols_all(S, 0)] + c_out_specs,
        out_shape=[full, full, full] + c_out_shape,
        scratch_shapes=c_scratch,
        compiler_params=_params(("arbitrary", "arbitrary")),
    )(qa, ka, va, do[0], tot[0], *c_ins)


def _sml_fwd(q, k, v, bias=None, *, width, hd, causal, name, tq=256, tk=256, comm=None):
    (qa, qo), (ka, ko), (va, vo) = q, k, v
    S, Sk = qa.shape[0], ka.shape[0]
    tq, tk = _tile(S, tq), _tile(Sk, tk)
    if causal:
        assert tq == tk and S == Sk
    nk = Sk // tk
    hpg = LANES // hd
    scale = hd ** -0.5
    has_bias = bias is not None
    n_g, n_q = width // LANES, S // tq

    def body(*all_refs):
        qi, gi = pl.program_id(1), pl.program_id(0)
        got = _carry(comm, 5 if has_bias else 3, 2, (gi == 0) & (qi == 0), (gi == n_g - 1) & (qi == n_q - 1), all_refs)
        refs = tuple(got[0]) + tuple(got[1])
        q_ref, k_ref, v_ref = refs[:3]
        o_ref, lse_ref = refs[-2:]
        masks = _lane_masks(hd, tq)
        qs = q_ref[...].astype(F32) * scale
        qm = [_keep(qs, m).astype(BF16) for m in masks]
        allowed = lax.broadcasted_iota(jnp.int32, (tq, tk), 1) <= lax.broadcasted_iota(jnp.int32, (tq, tk), 0)

        def tile(kb, carry, diag):
            ks = pl.multiple_of(kb * tk, tk)
            kv = k_ref[pl.ds(ks, tk), :].astype(BF16)
            vv = v_ref[pl.ds(ks, tk), :].astype(BF16)
            out = []
            for hh in range(hpg):
                m, l, acc = carry[3 * hh:3 * hh + 3]
                z = _dot(qm[hh], kv, _NT)
                if has_bias:
                    z = z + refs[3][hh] - refs[4][hh, kb]
                if diag:
                    z = jnp.where(allowed, z, NEG_INF)
                m2 = jnp.maximum(m, jnp.max(z, axis=1, keepdims=True))
                p = jnp.exp(z - m2)
                alpha = jnp.exp(m - m2)
                out += [m2, alpha * l + jnp.sum(p, axis=1, keepdims=True), alpha * acc + _dot(p.astype(BF16), vv)]
            return tuple(out)

        init = (jnp.full((tq, 1), NEG_INF, F32), jnp.zeros((tq, 1), F32), jnp.zeros((tq, LANES), F32)) * hpg
        if causal:
            carry = lax.fori_loop(0, qi, lambda kb, c: tile(kb, c, False), init)
            carry = tile(qi, carry, True)
        else:
            carry = lax.fori_loop(0, nk, lambda kb, c: tile(kb, c, False), init)
        o_ref[...] = _merge([acc / l for l, acc in zip(carry[1::3], carry[2::3])], masks).astype(o_ref.dtype)
        lse_ref[...] = _merge([jnp.broadcast_to(m + jnp.log(l), (tq, LANES)) for m, l in zip(carry[0::3], carry[1::3])], masks)
        got[2]()

    in_specs = [_cols(tq, qo), _cols_all(Sk, ko), _cols_all(Sk, vo)]
    ins = [qa, ka, va]
    if has_bias:
        in_specs += [pl.BlockSpec((hpg, tq, 1), lambda g, i: (g, i, 0)),
                     pl.BlockSpec((hpg, nk, 1, tk), lambda g, i: (g, 0, 0, 0))]
        ins += list(bias)
    c_ins, c_in_specs, c_out_specs, c_out_shape, c_scratch = _comm_args(comm)
    return pl.pallas_call(
        body, name=name, grid=(n_g, n_q),
        in_specs=in_specs + c_in_specs, out_specs=[_cols(tq, 0), _cols(tq, 0)] + c_out_specs,
        out_shape=[jax.ShapeDtypeStruct((S, width), BF16), jax.ShapeDtypeStruct((S, width), F32)] + c_out_shape,
        scratch_shapes=c_scratch,
        compiler_params=_params(("arbitrary", "arbitrary") if comm is not None else ("parallel", "arbitrary")),
    )(*ins, *c_ins)


def _sml_bwd(q, k, v, o, lse, do, bias=None, *, width, hd, causal, name, tq=256, tk=256, comm=None):
    (qa, qo), (ka, ko), (va, vo) = q, k, v
    S, Sk = qa.shape[0], ka.shape[0]
    tq, tk = _tile(S, tq), _tile(Sk, tk)
    nk = Sk // tk
    hpg = LANES // hd
    scale = hd ** -0.5
    has_bias = bias is not None
    n_in = 8 if has_bias else 6
    n_g, n_q = width // LANES, S // tq

    def body(*all_refs):
        qi, gi = pl.program_id(1), pl.program_id(0)
        got = _carry(comm, n_in, 5 if has_bias else 3, (gi == 0) & (qi == 0), (gi == n_g - 1) & (qi == n_q - 1), all_refs)
        refs = tuple(got[0]) + tuple(got[1])
        q_ref, k_ref, v_ref, o_ref, lse_ref, do_ref = refs[:6]
        dq_ref, dk_ref, dv_ref = refs[n_in:n_in + 3]

        @pl.when(qi == 0)
        def _():
            dk_ref[...] = jnp.zeros_like(dk_ref)
            dv_ref[...] = jnp.zeros_like(dv_ref)
            if has_bias:
                refs[n_in + 4][...] = jnp.zeros_like(refs[n_in + 4])

        masks = _lane_masks(hd, tq)
        qs = q_ref[...].astype(F32) * scale
        qm = [_keep(qs, m).astype(BF16) for m in masks]
        do32 = do_ref[...]
        dov = [_keep(do32, m).astype(BF16) for m in masks]
        prod = do32 * o_ref[...].astype(F32)
        delta = [jnp.sum(_keep(prod, m), axis=1, keepdims=True) for m in masks]
        lses = [_row_value(lse_ref[...], m) for m in masks]
        allowed = lax.broadcasted_iota(jnp.int32, (tq, tk), 1) <= lax.broadcasted_iota(jnp.int32, (tq, tk), 0)

        def tile(kb, carry, diag):
            ks = pl.multiple_of(kb * tk, tk)
            kv = k_ref[pl.ds(ks, tk), :].astype(BF16)
            vv = v_ref[pl.ds(ks, tk), :].astype(BF16)
            out = []
            dk_t, dv_t = None, None
            for hh in range(hpg):
                dq, db_row = carry[2 * hh:2 * hh + 2]
                z = _dot(qm[hh], kv, _NT)
                if has_bias:
                    z = z + refs[6][hh] - refs[7][hh, kb]
                p = jnp.exp(z - lses[hh])
                if diag:
                    p = jnp.where(allowed, p, 0.0)
                dz = p * (_dot(dov[hh], vv, _NT) - delta[hh])
                dzb = dz.astype(BF16)
                dk_h = _dot(dzb, qm[hh], _TN)
                dv_h = _dot(p.astype(BF16), dov[hh], _TN)
                dk_t = dk_h if dk_t is None else dk_t + dk_h
                dv_t = dv_h if dv_t is None else dv_t + dv_h
                if has_bias:
                    db_row = db_row + jnp.sum(dz, axis=1, keepdims=True)
                    refs[n_in + 4][hh, kb] += jnp.sum(dz, axis=0, keepdims=True)
                out += [dq + _dot(dzb, kv), db_row]
            dk_ref[pl.ds(ks, tk), :] += dk_t
            dv_ref[pl.ds(ks, tk), :] += dv_t
            return tuple(out)

        init = (jnp.zeros((tq, LANES), F32), jnp.zeros((tq, 1), F32)) * hpg
        if causal:
            carry = lax.fori_loop(0, qi, lambda kb, c: tile(kb, c, False), init)
            carry = tile(qi, carry, True)
        else:
            carry = lax.fori_loop(0, nk, lambda kb, c: tile(kb, c, False), init)
        dq_ref[...] = _merge(carry[0::2], masks) * scale
        if has_bias:
            for hh in range(hpg):
                refs[n_in + 3][hh] = carry[2 * hh + 1]
        got[2]()

    in_specs = [_cols(tq, qo), _cols_all(Sk, ko), _cols_all(Sk, vo), _cols(tq, o[1]), _cols(tq, lse[1]), _cols(tq, do[1])]
    ins = [qa, ka, va, o[0], lse[0], do[0]]
    out_specs = [_cols(tq, 0), _cols_all(Sk, 0), _cols_all(Sk, 0)]
    out_shape = [jax.ShapeDtypeStruct((S, width), F32), jax.ShapeDtypeStruct((Sk, width), F32),
                 jax.ShapeDtypeStruct((Sk, width), F32)]
    if has_bias:
        rspec = pl.BlockSpec((hpg, tq, 1), lambda g, i: (g, i, 0))
        cspec = pl.BlockSpec((hpg, nk, 1, tk), lambda g, i: (g, 0, 0, 0))
        in_specs += [rspec, cspec]
        ins += list(bias)
        out_specs += [rspec, cspec]
        n_heads = width // hd
        out_shape += [jax.ShapeDtypeStruct((n_heads, S, 1), F32), jax.ShapeDtypeStruct((n_heads, nk, 1, tk), F32)]
    c_ins, c_in_specs, c_out_specs, c_out_shape, c_scratch = _comm_args(comm)
    return pl.pallas_call(
        body, name=name, grid=(n_g, n_q),
        in_specs=in_specs + c_in_specs, out_specs=out_specs + c_out_specs, out_shape=out_shape + c_out_shape,
        scratch_shapes=c_scratch,
        compiler_params=_params(("arbitrary", "arbitrary") if comm is not None else ("parallel", "arbitrary")),
    )(*ins, *c_ins)


def _head_sums(t, masks):
    sums = [jnp.sum(_keep(t, m), axis=1, keepdims=True) for m in masks]
    return _merge([jnp.broadcast_to(s, t.shape) for s in sums], masks) if len(masks) > 1 else sums[0]


def _hnorm_fwd(x, g_lanes, *, width, hd, name, tr=1024):
    xa, xo = x
    R = xa.shape[0]
    tr = _tile(R, tr, align=16)
    n_blk = width // LANES

    def body(x_ref, g_ref, o_ref):
        masks = _lane_masks(hd, tr)
        for j in range(n_blk):
            sl = slice(j * LANES, (j + 1) * LANES)
            xv = x_ref[:, sl].astype(F32)
            r = lax.rsqrt(_head_sums(xv * xv, masks) * (1.0 / hd) + EPS)
            o_ref[:, sl] = (xv * r * g_ref[...]).astype(o_ref.dtype)

    assert (xo * LANES) % width == 0
    return pl.pallas_call(
        body, name=name, grid=(R // tr,),
        in_specs=[pl.BlockSpec((tr, width), lambda i: (i, xo * LANES // width)), pl.BlockSpec((1, LANES), lambda i: (0, 0))],
        out_specs=pl.BlockSpec((tr, width), lambda i: (i, 0)),
        out_shape=jax.ShapeDtypeStruct((R, width), BF16),
        compiler_params=_params(("parallel",)),
    )(xa, g_lanes)


def _hnorm_bwd(x, g_lanes, dy, *, width, hd, name, tr=1024):
    xa, xo = x
    R = xa.shape[0]
    tr = _tile(R, tr, align=16)
    n_blk = width // LANES

    def body(x_ref, g_ref, dy_ref, dx_ref, dg_ref):
        masks = _lane_masks(hd, tr)
        dg = jnp.zeros((1, LANES), F32)
        for j in range(n_blk):
            sl = slice(j * LANES, (j + 1) * LANES)
            xv = x_ref[:, sl].astype(F32)
            dyv = dy_ref[:, sl].astype(F32)
            r = lax.rsqrt(_head_sums(xv * xv, masks) * (1.0 / hd) + EPS)
            xh = xv * r
            dyg = dyv * g_ref[...]
            c = _head_sums(dyg * xh, masks) * (1.0 / hd)
            dx_ref[:, sl] = (r * (dyg - xh * c)).astype(dx_ref.dtype)
            dg = dg + jnp.sum(dyv * xh, axis=0, keepdims=True)
        if hd * 2 == LANES:
            dg8 = jnp.broadcast_to(dg, (8, LANES))
            dg = (dg8 + pltpu.roll(dg8, shift=hd, axis=1))[0:1]
        else:
            assert hd == LANES

        @pl.when(pl.program_id(0) == 0)
        def _():
            dg_ref[...] = jnp.zeros_like(dg_ref)

        dg_ref[...] += dg

    assert (xo * LANES) % width == 0
    return pl.pallas_call(
        body, name=name, grid=(R // tr,),
        in_specs=[pl.BlockSpec((tr, width), lambda i: (i, xo * LANES // width)), pl.BlockSpec((1, LANES), lambda i: (0, 0)),
                  pl.BlockSpec((tr, width), lambda i: (i, 0))],
        out_specs=[pl.BlockSpec((tr, width), lambda i: (i, 0)), pl.BlockSpec((1, LANES), lambda i: (0, 0))],
        out_shape=[jax.ShapeDtypeStruct((R, width), BF16), jax.ShapeDtypeStruct((1, LANES), F32)],
        compiler_params=_params(("arbitrary",)),
    )(xa, g_lanes, dy)


def _split3_dot(x, tri):
    a = x.astype(BF16)
    r = x - a.astype(F32)
    b = r.astype(BF16)
    c = (r - b.astype(F32)).astype(BF16)
    return _dot(a, tri) + _dot(b, tri) + _dot(c, tri)


def _forget_fwd(logit_t, b_col, *, name, blk=512):
    H, S = logit_t.shape
    blk = _tile(S, blk)

    def body(l_ref, b_ref, f_ref):
        r_i = lax.broadcasted_iota(jnp.int32, (blk, blk), 0)
        c_i = lax.broadcasted_iota(jnp.int32, (blk, blk), 1)
        upto = (r_i <= c_i).astype(BF16)
        carry = jnp.zeros((H, 1), F32)
        for j in range(S // blk):
            u = l_ref[:, j * blk:(j + 1) * blk] + b_ref[...]
            lf, _ = _log_sigmoid_pair(u)
            f_ref[:, j * blk:(j + 1) * blk] = _split3_dot(lf, upto) + carry
            carry = carry + jnp.sum(lf, axis=1, keepdims=True)

    return pl.pallas_call(
        body, name=name,
        out_shape=jax.ShapeDtypeStruct((H, S), F32),
        compiler_params=pltpu.CompilerParams(vmem_limit_bytes=VMEM_LIMIT),
    )(logit_t, b_col)


def _forget_bwd(logit_t, b_col, d_f, *, name, blk=512):
    H, S = logit_t.shape
    blk = _tile(S, blk)

    def body(l_ref, b_ref, df_ref, dl_ref, db_ref):
        r_i = lax.broadcasted_iota(jnp.int32, (blk, blk), 0)
        c_i = lax.broadcasted_iota(jnp.int32, (blk, blk), 1)
        fromon = (r_i >= c_i).astype(BF16)
        carry = jnp.zeros((H, 1), F32)
        db = jnp.zeros((H, 1), F32)
        for j in reversed(range(S // blk)):
            sl = slice(j * blk, (j + 1) * blk)
            dfv = df_ref[:, sl]
            d_lf = _split3_dot(dfv, fromon) + carry
            carry = carry + jnp.sum(dfv, axis=1, keepdims=True)
            u = l_ref[:, sl] + b_ref[...]
            _, lsn = _log_sigmoid_pair(u)
            dl = d_lf * jnp.exp(lsn)
            dl_ref[:, sl] = dl
            db = db + jnp.sum(dl, axis=1, keepdims=True)
        db_ref[...] = db

    return pl.pallas_call(
        body, name=name,
        out_shape=[jax.ShapeDtypeStruct((H, S), F32), jax.ShapeDtypeStruct((H, 1), F32)],
        compiler_params=pltpu.CompilerParams(vmem_limit_bytes=VMEM_LIMIT),
    )(logit_t, b_col, d_f)


def _sigmoid(t):
    return 1.0 / (1.0 + jnp.exp(-t))


def _gate_fwd(o3, w3, proj, x, w_out, g_norm, D, *, name, tm=512):
    S = proj.shape[0]
    tm = _tile(S, tm)

    def body(o0, o1, o2, w0, w1, w2, g0, g1, g2, x_ref, wo_ref, gn_ref, merged_ref, x1_ref, h2_ref):
        acc = None
        for o_ref, w_ref, g_ref in ((o0, w0, g0), (o1, w1, g1), (o2, w2, g2)):
            t = _sigmoid(g_ref[...]) * _dot(o_ref[...], w_ref[...])
            acc = t if acc is None else acc + t
        merged = acc.astype(BF16)
        merged_ref[...] = merged
        x1 = x_ref[...] + _dot(merged, wo_ref[...])
        x1_ref[...] = x1
        h2_ref[...] = (x1 * lax.rsqrt(jnp.mean(x1 * x1, axis=-1, keepdims=True) + EPS) * gn_ref[...]).astype(BF16)

    ospec = lambda d: pl.BlockSpec((tm, d), lambda i: (i, 0))
    wspec = lambda w: pl.BlockSpec(w.shape, lambda i: (0, 0))
    gspec = lambda j: pl.BlockSpec((tm, D), lambda i: (i, j))
    row = pl.BlockSpec((tm, D), lambda i: (i, 0))
    return pl.pallas_call(
        body, name=name, grid=(S // tm,),
        in_specs=[ospec(o.shape[1]) for o in o3] + [wspec(w) for w in w3] + [gspec(j) for j in range(3)]
        + [row, wspec(w_out), wspec(g_norm)],
        out_specs=[row, row, row],
        out_shape=[jax.ShapeDtypeStruct((S, D), BF16), jax.ShapeDtypeStruct((S, D), F32), jax.ShapeDtypeStruct((S, D), BF16)],
        compiler_params=_params(("parallel",)),
    )(*o3, *w3, proj, proj, proj, x, w_out, g_norm)


def _gate_bwd(o3, w3, proj, dx1, w_out, D, *, name, tm=256):
    S = proj.shape[0]
    tm = _tile(S, tm)

    def body(o0, o1, o2, w0, w1, w2, g0, g1, g2, dx_ref, wo_ref, dg_ref, dw0, dw1, dw2, do0, do1, do2):
        first = pl.program_id(0) == 0
        dm = _dot(dx_ref[...].astype(BF16), wo_ref[...], _NT)
        for j, (o_ref, w_ref, g_ref, dw_ref, do_ref) in enumerate(
                ((o0, w0, g0, dw0, do0), (o1, w1, g1, dw1, do1), (o2, w2, g2, dw2, do2))):
            s = _sigmoid(g_ref[...])
            br = _dot(o_ref[...], w_ref[...])
            dg_ref[:, j * D:(j + 1) * D] = (dm * br * s * (1.0 - s)).astype(dg_ref.dtype)
            dbr = (dm * s).astype(BF16)
            do_ref[...] = _dot(dbr, w_ref[...], _NT)
            part = _dot(o_ref[...], dbr, _TN)

            @pl.when(first)
            def _():
                dw_ref[...] = part

            @pl.when(jnp.logical_not(first))
            def _():
                dw_ref[...] += part

    ospec = lambda d: pl.BlockSpec((tm, d), lambda i: (i, 0))
    wspec = lambda w: pl.BlockSpec(w.shape, lambda i: (0, 0))
    gspec = lambda j: pl.BlockSpec((tm, D), lambda i: (i, j))
    dspec = pl.BlockSpec((tm, D), lambda i: (i, 0))
    return pl.pallas_call(
        body, name=name, grid=(S // tm,),
        in_specs=[ospec(o.shape[1]) for o in o3] + [wspec(w) for w in w3] + [gspec(j) for j in range(3)]
        + [dspec, wspec(w_out)],
        out_specs=[pl.BlockSpec((tm, 3 * D), lambda i: (i, 0))] + [wspec(w) for w in w3] + [ospec(o.shape[1]) for o in o3],
        out_shape=[jax.ShapeDtypeStruct((S, proj.shape[1]), BF16)] + [jax.ShapeDtypeStruct(w.shape, F32) for w in w3]
        + [jax.ShapeDtypeStruct((S, o.shape[1]), F32) for o in o3],
        compiler_params=_params(("arbitrary",)),
    )(*o3, *w3, proj, proj, proj, dx1, w_out)


def _pair_sum(stacked, got, core, *, name):
    n, _, r, c = stacked.shape
    tr = _row_tile(r, c, 3)

    def body(core_ref, a_ref, b_ref, o_ref):
        o_ref[...] = (a_ref[0].astype(F32) + b_ref[...].astype(F32)).astype(o_ref.dtype)

    spec = pl.BlockSpec((1, tr, c), lambda s, i, core_ref: (s, i, 0))
    return pl.pallas_call(
        body, name=name,
        grid_spec=pltpu.PrefetchScalarGridSpec(
            num_scalar_prefetch=1, grid=(n, r // tr),
            in_specs=[pl.BlockSpec((1, 1, tr, c), lambda s, i, core_ref: (s, core_ref[0], i, 0)), spec],
            out_specs=spec),
        out_shape=jax.ShapeDtypeStruct((n, r, c), BF16),
        compiler_params=_params(("parallel", "parallel")),
    )(core.astype(jnp.int32).reshape(1), stacked, got)


def _chip_sum(parts, got, chip, *, name):
    _, r, c = parts.shape
    tr = _row_tile(r, c, 6)

    def body(chip_ref, p_ref, q0_ref, q1_ref, q2_ref, o_ref):
        o_ref[...] = ((p_ref[0].astype(F32) + q0_ref[0].astype(F32)) + q1_ref[0].astype(F32)) + q2_ref[0].astype(F32)

    from_chip = lambda j: pl.BlockSpec((1, tr, c), lambda i, chip_ref: (j, i, 0))
    return pl.pallas_call(
        body, name=name,
        grid_spec=pltpu.PrefetchScalarGridSpec(
            num_scalar_prefetch=1, grid=(r // tr,),
            in_specs=[pl.BlockSpec((1, tr, c), lambda i, chip_ref: (chip_ref[0], i, 0))] + [from_chip(j) for j in range(3)],
            out_specs=pl.BlockSpec((tr, c), lambda i, chip_ref: (i, 0))),
        out_shape=jax.ShapeDtypeStruct((r, c), F32),
        compiler_params=_params(("parallel",)),
    )(chip.astype(jnp.int32).reshape(1), parts, got, got, got)


def _adamw_math(w, g, m, v):
    m2 = ADAM_B1 * m + (1.0 - ADAM_B1) * g
    v2 = ADAM_B2 * v + (1.0 - ADAM_B2) * (g * g)
    m_hat = m2 / (1.0 - ADAM_B1 ** ADAM_STEP)
    v_hat = v2 / (1.0 - ADAM_B2 ** ADAM_STEP)
    delta = -ADAM_LR * (m_hat / (jnp.sqrt(v_hat) + ADAM_EPS) + ADAM_WD * w)
    return delta, m2, v2


def _adamw(w, g, m, v, *, name):
    return _ew(_adamw_math, [w, g, m, v], (F32, F32, F32), name=name)


def _adamw_small(w, parts, m, v, *, name):
    n = parts.shape[0]

    def body(w_ref, p_ref, m_ref, v_ref, g_ref, d_ref, m2_ref, v2_ref):
        g = p_ref[0]
        for i in range(1, n):
            g = g + p_ref[i]
        g_ref[...] = g
        d_ref[...], m2_ref[...], v2_ref[...] = _adamw_math(w_ref[...], g, m_ref[...], v_ref[...])

    shp = jax.ShapeDtypeStruct(w.shape, F32)
    return pl.pallas_call(body, name=name, out_shape=[shp] * 4)(w, parts, m, v)


_ANY = pl.BlockSpec(memory_space=pl.ANY)


def _mesh_place():
    x, y, c = lax.axis_index("x"), lax.axis_index("y"), lax.axis_index("c")
    chips = [(1 - x, y), (x, 1 - y), (1 - x, 1 - y)]
    return x, y, c, chips


def _remote(src, dst, sems, i, to):
    send_sems, recv_sems = sems
    return pltpu.make_async_remote_copy(src_ref=src, dst_ref=dst, send_sem=send_sems.at[i], recv_sem=recv_sems.at[i],
                                        device_id=to, device_id_type=MESH_ID)


def _gather_weights(shards, *, name):
    n = len(shards)

    def body(*refs):
        ins, outs = refs[:n], refs[n:2 * n]
        sems = refs[2 * n:2 * n + 2]
        x, y, c, chips = _mesh_place()
        me = 2 * x + y
        sibling = (x, y, 1 - c)
        sent = []
        for w in range(n):
            for j, chip in enumerate(chips):
                cp = _remote(ins[w].at[c], outs[w].at[me, c], sems, 6 * w + j, (chip[0], chip[1], c))
                cp.start()
                sent.append(cp)
        for w in range(n):
            for j, chip in enumerate(chips):
                got = outs[w].at[2 * chip[0] + chip[1], c]
                _remote(got, got, sems, 6 * w + j, sibling).wait_recv()
                cp = _remote(got, got, sems, 6 * w + 3 + j, sibling)
                cp.start()
                sent.append(cp)
        for w in range(n):
            for j, chip in enumerate(chips):
                got = outs[w].at[2 * chip[0] + chip[1], 1 - c]
                _remote(got, got, sems, 6 * w + 3 + j, sibling).wait_recv()
        for cp in sent:
            cp.wait_send()

    outs = pl.pallas_call(
        body, name=name,
        in_specs=[_ANY] * n, out_specs=[_ANY] * n,
        out_shape=[jax.ShapeDtypeStruct((N_CHIPS,) + s.shape, s.dtype) for s in shards],
        scratch_shapes=[pltpu.SemaphoreType.DMA((6 * n,)), pltpu.SemaphoreType.DMA((6 * n,))],
    )(*shards)
    me = 2 * lax.axis_index("x") + lax.axis_index("y")
    return [lax.dynamic_update_index_in_dim(o, s, me, 0) for o, s in zip(outs, shards)]


def _forward_halves(gathered, *, name):
    n = len(gathered)

    def body(*refs):
        ins, outs = refs[:n], refs[n:2 * n]
        sems = refs[2 * n:2 * n + 2]
        x, y, c, chips = _mesh_place()
        sibling = (x, y, 1 - c)
        sent = []
        for w in range(n):
            for j, chip in enumerate(chips):
                peer = 2 * chip[0] + chip[1]
                cp = _remote(ins[w].at[peer, c], outs[w].at[peer, c], sems, 3 * w + j, sibling)
                cp.start()
                sent.append(cp)
        for w in range(n):
            for j, chip in enumerate(chips):
                land = outs[w].at[2 * chip[0] + chip[1], 1 - c]
                _remote(land, land, sems, 3 * w + j, sibling).wait_recv()
        for cp in sent:
            cp.wait_send()

    return pl.pallas_call(
        body, name=name,
        in_specs=[_ANY] * n, out_specs=[_ANY] * n,
        out_shape=[jax.ShapeDtypeStruct(g.shape, g.dtype) for g in gathered],
        input_output_aliases={w: w for w in range(n)},
        scratch_shapes=[pltpu.SemaphoreType.DMA((3 * n,)), pltpu.SemaphoreType.DMA((3 * n,))],
    )(*gathered)


def _exchange_siblings(grads, *, name):
    n = len(grads)

    def body(*refs):
        ins, got = refs[:n], refs[n:2 * n]
        sems = refs[2 * n:2 * n + 2]
        x, y, c, _ = _mesh_place()
        sibling = (x, y, 1 - c)
        sent = []
        for w in range(n):
            for s in range(N_CHIPS):
                cp = _remote(ins[w].at[s, 1 - c], got[w].at[s], sems, N_CHIPS * w + s, sibling)
                cp.start()
                sent.append(cp)
        for w in range(n):
            for s in range(N_CHIPS):
                _remote(got[w].at[s], got[w].at[s], sems, N_CHIPS * w + s, sibling).wait_recv()
        for cp in sent:
            cp.wait_send()

    n_sem = N_CHIPS * n
    return pl.pallas_call(
        body, name=name,
        in_specs=[_ANY] * n, out_specs=[_ANY] * n,
        out_shape=[jax.ShapeDtypeStruct((N_CHIPS,) + g.shape[2:], g.dtype) for g in grads],
        scratch_shapes=[pltpu.SemaphoreType.DMA((n_sem,)), pltpu.SemaphoreType.DMA((n_sem,))],
    )(*grads)


def _share_halves(halves, small):
    n = len(halves)

    def body(*refs):
        ins, small_ref = refs[:n], refs[n]
        outs, small_out = refs[n + 1:2 * n + 1], refs[2 * n + 1]
        sems = refs[2 * n + 2:2 * n + 4]
        x, y, c, chips = _mesh_place()
        sibling = (x, y, 1 - c)
        me = 4 * x + 2 * y + c
        sent = [_remote(ins[w], outs[w].at[c], sems, w, sibling) for w in range(n)]
        peers = [sibling] + [(ch[0], ch[1], cc) for ch in chips for cc in (c, 1 - c)]
        sent += [_remote(small_ref, small_out.at[me], sems, n + j, peer) for j, peer in enumerate(peers)]
        for cp in sent:
            cp.start()
        for w in range(n):
            _remote(outs[w].at[1 - c], outs[w].at[1 - c], sems, w, sibling).wait_recv()
        for j, peer in enumerate(peers):
            frm = small_out.at[4 * peer[0] + 2 * peer[1] + peer[2]]
            _remote(frm, frm, sems, n + j, peer).wait_recv()
        for cp in sent:
            cp.wait_send()

    n_sem = n + 7
    outs = pl.pallas_call(
        body, name="share_halves",
        in_specs=[_ANY] * (n + 1), out_specs=[_ANY] * (n + 1),
        out_shape=[jax.ShapeDtypeStruct((2,) + h.shape, h.dtype) for h in halves]
        + [jax.ShapeDtypeStruct((8,) + small.shape, small.dtype)],
        scratch_shapes=[pltpu.SemaphoreType.DMA((n_sem,)), pltpu.SemaphoreType.DMA((n_sem,))],
    )(*halves, small)
    c = lax.axis_index("c")
    me = 4 * lax.axis_index("x") + 2 * lax.axis_index("y") + c
    return ([lax.dynamic_update_index_in_dim(o, h, c, 0) for o, h in zip(outs[:n], halves)],
            lax.dynamic_update_index_in_dim(outs[n], small, me, 0))


EARLY = ("w_ff_down", "w_ff_up", "w_out", "w_branch_sb", "w_branch_fox", "w_branch_mem", "w_mem_kv")


def _norm_bwd_tail(dy, x, add, g):
    r = lax.rsqrt(jnp.mean(x * x, axis=-1, keepdims=True) + EPS)
    xh = x * r
    dyg = dy * g
    c = jnp.mean(dyg * xh, axis=-1, keepdims=True)
    return r * (dyg - xh * c) + add, jnp.sum(dy * xh, axis=0, keepdims=True)


def _split(outs, n):
    outs = list(outs) if isinstance(outs, (list, tuple)) else [outs]
    return outs[:n], outs[n:]


def _local_step(x, mem, target, small, W, gather_rest=None, reduce_early=None, reduce_late=None):
    S, D = x.shape
    o_qkv, o_mq, o_f = 3 * D, 3 * D + 2 * 3 * D_SB, 3 * D + 2 * 3 * D_SB + D_MEM
    tq = 512

    g_comms, finish_weights = gather_rest if gather_rest is not None else ([None] * 3, None)
    (proj, h), landed = _split(_mm(x, W["w_in"], name="in_proj", tb=True, tn=768, a_gain=small["g_mix_norm"],
                                   comm=g_comms[0]), 2)
    blk = lambda j: (proj, (o_qkv + j * D_SB) // LANES)
    sb_q, sb_k, sb_v, fx_q, fx_k, fx_v = [blk(j) for j in range(6)]
    m_q = (proj, o_mq // LANES)
    f_logit_t = _mm(W["w_in"][o_f:o_f + ROW_TILE], h, name="forget_logits", tb=True)[:FOX_HEADS]
    b_col = small["b_forget"].reshape(FOX_HEADS, 1)
    lanes = lambda g: jnp.tile(g, (1, LANES // g.shape[1]))
    g_fq, g_fk, g_mq, g_mk = [lanes(small[k]) for k in ("g_fox_q", "g_fox_k", "g_mem_q", "g_mem_k")]

    (o_sb, sb_tot), more = _split(_sbl_fwd(sb_q, sb_k, sb_v, width=D_SB, hd=HD, name="sb_fwd", tq=tq, comm=g_comms[1]), 2)
    landed += more

    fq = _hnorm_fwd(fx_q, g_fq, width=D_FOX, hd=HD, name="fox_q_norm")
    fk = _hnorm_fwd(fx_k, g_fk, width=D_FOX, hd=HD, name="fox_k_norm")
    f_cum = _forget_fwd(f_logit_t, b_col, name="forget_fwd")
    tkf = _tile(S, tq)
    f_bias = (f_cum.reshape(FOX_HEADS, S, 1), f_cum.reshape(FOX_HEADS, S // tkf, 1, tkf))
    (o_fox, fox_lse), more = _split(_sml_fwd((fq, 0), (fk, 0), fx_v, f_bias, width=D_FOX, hd=HD, causal=True,
                                             name="fox_fwd", tq=tq, tk=tq, comm=g_comms[2]), 2)
    landed += more
    if finish_weights is not None:
        W = {**W, **finish_weights(landed)}

    mh = _rmsnorm_fwd(mem, small["g_mem_norm"], BF16, name="mem_norm")
    mkv = _mm(mh, W["w_mem_kv"], name="mem_kv")
    mv = (mkv, D_MEM // LANES)
    mq = _hnorm_fwd(m_q, g_mq, width=D_MEM, hd=MEM_HD, name="mem_q_norm")
    mk = _hnorm_fwd((mkv, 0), g_mk, width=D_MEM, hd=MEM_HD, name="mem_k_norm")
    o_mem, mem_lse = _sml_fwd((mq, 0), (mk, 0), mv, width=D_MEM, hd=MEM_HD, causal=False, name="mem_fwd", tq=tq, tk=256)

    o3 = [o_sb, o_fox, o_mem]
    w3 = [W["w_branch_sb"], W["w_branch_fox"], W["w_branch_mem"]]
    merged, x1, h2 = _gate_fwd(o3, w3, proj, x, W["w_out"], small["g_mlp_norm"], D, name="gate_fwd")

    def relu2(acc):
        u = jnp.maximum(acc, 0.0)
        return u, u * u

    u, a = _mm(h2, W["w_ff_up"], name="ff_up", out_dtypes=(BF16, BF16), epilogue=relu2)
    def head(acc, res, tgt):
        d = (res + acc - tgt) * (1.0 / D)
        return d, d, jnp.sum(d * d, axis=0, keepdims=True)

    dy, dy16, sq_rows = _mm(a, W["w_ff_down"], name="ff_down", extras=(x1, target), out_dtypes=(F32, BF16),
                            epilogue=head, col_sums=1, tn=512)
    loss = (0.5 * D) * jnp.sum(sq_rows)

    G = {}
    du = _mm(dy16, W["w_ff_down"], name="d_ff_act", tb=True, out_dtypes=(BF16,), extras=(u,),
             epilogue=lambda acc, uu: (acc * (2.0 * uu.astype(F32)),))
    G["w_ff_down"] = _mm(a, dy16, name="d_w_ff_down", ta=True, out_dtypes=(BF16,))
    G["w_ff_up"] = _mm(h2, du, name="d_w_ff_up", ta=True, out_dtypes=(BF16,))
    dx1, dg_rows = _mm(du, W["w_ff_up"], name="d_mlp_in", tb=True, extras=(x1, dy, small["g_mlp_norm"]),
                       epilogue=_norm_bwd_tail, col_sums=1, tm=512, tn=D)
    dg_mlp = jnp.sum(dg_rows, axis=0, keepdims=True)
    G["w_out"] = _mm(merged, dx1, name="d_w_out", ta=True, out_dtypes=(BF16,))
    dgate, dw0, dw1, dw2, do_sb, do_fox, do_mem = _gate_bwd(o3, w3, proj, dx1, W["w_out"], D, name="gate_bwd")
    for nm, dw in zip(("w_branch_sb", "w_branch_fox", "w_branch_mem"), (dw0, dw1, dw2)):
        G[nm] = dw.astype(BF16)

    dmq_n, dmk_n, dmv = _sml_bwd((mq, 0), (mk, 0), mv, (o_mem, 0), (mem_lse, 0), (do_mem, 0), width=D_MEM, hd=MEM_HD,
                                 causal=False, name="mem_bwd", tq=tq, tk=256)
    dm_q, dg_mem_q = _hnorm_bwd(m_q, g_mq, dmq_n, width=D_MEM, hd=MEM_HD, name="d_mem_q_norm")
    dmk_raw, dg_mem_k = _hnorm_bwd((mkv, 0), g_mk, dmk_n, width=D_MEM, hd=MEM_HD, name="d_mem_k_norm")
    dmkv = jnp.concatenate([dmk_raw, dmv.astype(BF16)], axis=1)
    G["w_mem_kv"] = _mm(mh, dmkv, name="d_w_mem_kv", ta=True, out_dtypes=(BF16,))
    dmh = _mm(dmkv, W["w_mem_kv"], name="d_mem_h", tb=True)
    _, dg_mem = _rmsnorm_bwd(mem, small["g_mem_norm"], dmh, name="d_mem_norm")

    r_comms, r_finish = reduce_early({k: G.pop(k) for k in EARLY}) if reduce_early is not None else ([None] * 2, None)
    dsb, landed_sb = _split(_sbl_bwd(sb_q, sb_k, sb_v, (do_sb, 0), (sb_tot, 0), width=D_SB, hd=HD, name="sb_bwd", tq=tq,
                                     comm=r_comms[0]), 3)
    (dfq, dfk, dfv, df_row, df_col), landed_fox = _split(
        _sml_bwd((fq, 0), (fk, 0), fx_v, (o_fox, 0), (fox_lse, 0), (do_fox, 0), f_bias, width=D_FOX, hd=HD, causal=True,
                 name="fox_bwd", tq=tq, tk=tq, comm=r_comms[1]), 5)
    early = r_finish(landed_sb, landed_fox) if r_finish is not None else {}
    dfx_q, dg_fox_q = _hnorm_bwd(fx_q, g_fq, dfq, width=D_FOX, hd=HD, name="d_fox_q_norm")
    dfx_k, dg_fox_k = _hnorm_bwd(fx_k, g_fk, dfk, width=D_FOX, hd=HD, name="d_fox_k_norm")
    d_fcum = df_row.reshape(FOX_HEADS, S) - df_col.reshape(FOX_HEADS, S)
    d_flogit_t, db_forget = _forget_bwd(f_logit_t, b_col, d_fcum, name="forget_bwd")
    dg_fox_q, dg_fox_k = dg_fox_q[:, :HD], dg_fox_k[:, :HD]

    rest_cols = jnp.concatenate([t.astype(BF16) for t in (*dsb, dfx_q, dfx_k, dfv, dm_q)]
                                + [d_flogit_t.T.astype(BF16), jnp.zeros((S, F_PAD - FOX_HEADS), BF16)], axis=1)
    dproj = lax.dynamic_update_slice(dgate, rest_cols, (0, 3 * D))
    g_w_in = _mm(dproj, h, name="d_w_in", ta=True, out_dtypes=(BF16,), tm=768)
    comm, finish = reduce_late({"w_in": g_w_in}) if reduce_late is not None else (None, None)
    (grad_x, dg_rows), landed = _split(
        _mm(dproj, W["w_in"], name="d_mix_in", tk=2304, tm=512, tn=D, extras=(x, dx1, small["g_mix_norm"]),
            epilogue=_norm_bwd_tail, col_sums=1, comm=comm), 2)
    dg_mix = jnp.sum(dg_rows, axis=0, keepdims=True)
    if finish is None:
        G["w_in"] = g_w_in
    else:
        early.update(finish(landed))

    small_grads = dict(g_mix_norm=dg_mix, g_mem_norm=dg_mem, b_forget=db_forget.reshape(1, FOX_HEADS),
                       g_fox_q=dg_fox_q, g_fox_k=dg_fox_k, g_mem_q=dg_mem_q, g_mem_k=dg_mem_k, g_mlp_norm=dg_mlp)
    return loss, grad_x, G, small_grads, early


BIG = ("w_in", "w_mem_kv", "w_branch_sb", "w_branch_fox", "w_branch_mem", "w_out", "w_ff_up", "w_ff_down")
COLUMN_SHARDED = ("w_in", "w_branch_sb", "w_branch_fox", "w_branch_mem", "w_ff_up")
SMALL = ("g_mix_norm", "g_mem_norm", "b_forget", "g_fox_q", "g_fox_k", "g_mem_q", "g_mem_k", "g_mlp_norm")
ORDER = ("g_mix_norm", "g_mem_norm", "w_in", "b_forget", "g_fox_q", "g_fox_k", "g_mem_q", "g_mem_k", "w_mem_kv",
         "w_branch_sb", "w_branch_fox", "w_branch_mem", "w_out", "g_mlp_norm", "w_ff_up", "w_ff_down")


def _unshard(name, gathered):
    n, _, rh, c = gathered.shape
    t = gathered.reshape(n, 2 * rh, c)
    if name in COLUMN_SHARDED:
        return t.transpose(1, 0, 2).reshape(2 * rh, n * c)
    return t.reshape(n * 2 * rh, c)


def _reshard(name, full):
    if name in COLUMN_SHARDED:
        r, c = full.shape
        t = full.reshape(r, N_CHIPS, c // N_CHIPS).transpose(1, 0, 2)
    else:
        r, c = full.shape[0] // N_CHIPS, full.shape[1]
        t = full.reshape(N_CHIPS, r, c)
    return t.reshape(N_CHIPS, 2, t.shape[1] // 2, t.shape[2])


ROW_TILE = 16
IN_BUF_ALIGN = 256


def _in_segments(D):
    n_qkv = 6 * D_SB
    o_mq, o_gate = n_qkv + FOX_HEADS, n_qkv + FOX_HEADS + D_MEM
    return [(0, n_qkv, 3 * D), (n_qkv, o_mq, 3 * D + n_qkv + D_MEM), (o_mq, o_gate, 3 * D + n_qkv), (o_gate, o_gate + 3 * D, 0)]


class _InLayout:
    def __init__(self, D, shard, n):
        self.D, self.shard, self.n = D, shard, n
        down = lambda v: v // ROW_TILE * ROW_TILE
        up = lambda v: -(-v // ROW_TILE) * ROW_TILE
        self.pieces = []
        ends = []
        for s in range(n):
            cursor, mine = 0, []
            for a, b, p in _in_segments(D):
                x0, x1 = max(a, s * shard), min(b, (s + 1) * shard)
                if x0 < x1:
                    p0 = p + x0 - a
                    rows = up(p0 + x1 - x0) - down(p0)
                    mine.append((x0 - s * shard, x1 - x0, p0, cursor, rows))
                    cursor += rows
            self.pieces.append(mine)
            ends.append(cursor)
        self.rows = -(-max(ends) // IN_BUF_ALIGN) * IN_BUF_ALIGN
        self.padded_rows = 3 * D + 6 * D_SB + D_MEM + F_PAD

    def _per_shard(self, fn, chip, operand):
        return lax.switch(chip, [functools.partial(fn, s) for s in range(self.n)], operand)

    def pack(self, chip, rows):
        def one(s, t):
            out, at = [], 0
            for x0, n_rows, p0, start, region in self.pieces[s]:
                lead = p0 % ROW_TILE
                out += [jnp.zeros((start + lead - at, t.shape[1]), t.dtype), t[x0:x0 + n_rows]]
                at = start + lead + n_rows
            return jnp.concatenate(out + [jnp.zeros((self.rows - at, t.shape[1]), t.dtype)], axis=0)
        return self._per_shard(one, chip, rows)

    def unpack(self, chip, buf, pad_to):
        def one(s, t):
            out = [t[start + p0 % ROW_TILE:start + p0 % ROW_TILE + n_rows] for _, n_rows, p0, start, _ in self.pieces[s]]
            return jnp.concatenate(out + [jnp.zeros((pad_to - self.shard, t.shape[1]), t.dtype)], axis=0)
        return self._per_shard(one, chip, buf)

    def to_padded(self, bufs):
        runs = sorted((p0, s, start, region) for s in range(self.n) for _, _, p0, start, region in self.pieces[s])
        chunks, end = [], 0
        for p0, s, start, region in runs:
            d0 = p0 // ROW_TILE * ROW_TILE
            src = bufs[s, start:start + region]
            if d0 < end:
                assert end - d0 == ROW_TILE
                last = chunks.pop()
                chunks += [last[:-ROW_TILE], last[-ROW_TILE:] + src[:ROW_TILE], src[ROW_TILE:]]
            else:
                if d0 > end:
                    chunks.append(jnp.zeros((d0 - end, bufs.shape[2]), bufs.dtype))
                chunks.append(src)
            end = d0 + region
        chunks.append(jnp.zeros((self.padded_rows - end, bufs.shape[2]), bufs.dtype))
        return jnp.concatenate(chunks, axis=0)

    def from_padded(self, gp):
        bufs = []
        for s in range(self.n):
            out, at = [], 0
            for _, n_rows, p0, start, region in self.pieces[s]:
                d0 = p0 // ROW_TILE * ROW_TILE
                row = d0 + lax.broadcasted_iota(jnp.int32, (region, 1), 0)
                out.append(jnp.where((row >= p0) & (row < p0 + n_rows), gp[d0:d0 + region], jnp.zeros((), gp.dtype)))
                at = start + region
            bufs.append(jnp.concatenate(out + [jnp.zeros((self.rows - at, gp.shape[1]), gp.dtype)], axis=0))
        return jnp.stack(bufs)


SMALL_ROWS = 16


def _pack_small(vals, scalar=None):
    width = max(vals[k].shape[1] for k in SMALL)
    rows = [jnp.pad(vals[k].astype(F32), ((0, 0), (0, width - vals[k].shape[1]))) for k in SMALL]
    extra = jnp.zeros((SMALL_ROWS - len(SMALL), width), F32)
    if scalar is not None:
        extra = extra.at[0, 0].set(scalar)
    return jnp.concatenate(rows + [extra], axis=0)


def _unpack_small(packed, like):
    return {k: packed[i:i + 1, :like[k].shape[1]] for i, k in enumerate(SMALL)}


def kernel(x, mem, g_mix_norm, g_mem_norm, w_in, b_forget, g_fox_q, g_fox_k, g_mem_q, g_mem_k, w_mem_kv, w_branch_sb, w_branch_fox, w_branch_mem, w_out, g_mlp_norm, w_ff_up, w_ff_down, loss_target, m_g_mix_norm, m_g_mem_norm, m_w_in, m_b_forget, m_g_fox_q, m_g_fox_k, m_g_mem_q, m_g_mem_k, m_w_mem_kv, m_w_branch_sb, m_w_branch_fox, m_w_branch_mem, m_w_out, m_g_mlp_norm, m_w_ff_up, m_w_ff_down, v_g_mix_norm, v_g_mem_norm, v_w_in, v_b_forget, v_g_fox_q, v_g_fox_k, v_g_mem_q, v_g_mem_k, v_w_mem_kv, v_w_branch_sb, v_w_branch_fox, v_w_branch_mem, v_w_out, v_g_mlp_norm, v_w_ff_up, v_w_ff_down):
    given = dict(locals())
    D = x.shape[-1]
    weights = {k: given[k] for k in ORDER}
    moms = {k: given["m_" + k] for k in ORDER}
    vars_ = {k: given["v_" + k] for k in ORDER}

    me_chip = 2 * lax.axis_index("x") + lax.axis_index("y")

    n_in = w_in.shape[2]
    lay = _InLayout(D, n_in, N_CHIPS)
    transposed = lambda t: jnp.transpose(t[0])
    shards = {}
    for k in BIG:
        w = weights[k][0].astype(BF16)
        if k == "w_in":
            w = lay.pack(me_chip, jnp.transpose(w))
        shards[k] = w.reshape(2, w.shape[0] // 2, w.shape[1])
    gathered_in = _gather_weights([shards["w_in"]], name="gather_w_in")[0]
    W = {"w_in": lay.to_padded(gathered_in.reshape(N_CHIPS, lay.rows, D))}
    carried = (("w_branch_sb", "w_branch_fox", "w_branch_mem", "w_out"), ("w_ff_up", "w_mem_kv"), ("w_ff_down",))
    rest = [k for grp in carried for k in grp]
    assert sorted(rest + ["w_in"]) == sorted(BIG)

    def finish_weights(landed):
        full = _forward_halves(landed, name="forward_halves")
        full = [lax.dynamic_update_index_in_dim(o, shards[k], me_chip, 0) for k, o in zip(rest, full)]
        return {k: _unshard(k, g) for k, g in zip(rest, full)}

    me_core = lax.axis_index("c")

    def sum_chips(names, parts, got):
        return {k: _chip_sum(p, q, me_chip, name="sum_chips_" + k) for k, p, q in zip(names, parts, got)}

    def pair_sums(grads, tag):
        names = list(grads)
        stacked = {k: _reshard(k, grads[k]) for k in names if k != "w_in"}
        if "w_in" in grads:
            stacked["w_in"] = lay.from_padded(grads["w_in"]).reshape(N_CHIPS, 2, lay.rows // 2, D)
        got = _exchange_siblings([stacked[k] for k in names], name="exchange_siblings_" + tag)
        return {k: _pair_sum(stacked[k], q, me_core, name="sum_pair_" + k) for k, q in zip(names, got)}

    def reduce_early(grads):
        parts = pair_sums(grads, "early")
        groups = [[k for k in parts if k in ("w_ff_down", "w_ff_up")], [k for k in parts if k not in ("w_ff_down", "w_ff_up")]]
        comms = [_ChipExchange("scatter", [parts[k] for k in grp]) for grp in groups]

        def finish(*landed):
            out = {}
            for grp, got in zip(groups, landed):
                out.update(sum_chips(grp, [parts[k] for k in grp], got))
            return out
        return comms, finish

    def reduce_late(grads):
        parts = pair_sums(grads, "late")
        names = list(parts)
        return _ChipExchange("scatter", [parts[k] for k in names]), functools.partial(sum_chips, names, [parts[k] for k in names])

    small = {k: weights[k] for k in SMALL}
    loss_part, grad_x, G, small_grads, halves = _local_step(
        x[0], mem[0], loss_target[0], small, W,
        gather_rest=([_ChipExchange("gather", [shards[k] for k in grp]) for grp in carried], finish_weights),
        reduce_early=reduce_early, reduce_late=reduce_late)
    assert not G, list(G)
    reduced, small_parts = _share_halves([halves[k] for k in BIG], _pack_small(small_grads, loss_part))

    grads, deltas, new_m, new_v = {}, {}, {}, {}
    for k, g in zip(BIG, reduced):
        shp = weights[k].shape
        if k == "w_in":
            g2 = lay.unpack(me_chip, g.reshape(lay.rows, D), lay.rows)
            padded = lambda t: jnp.pad(transposed(t), ((0, lay.rows - n_in), (0, 0)))
            outs = _adamw(padded(weights[k]), g2, padded(moms[k]), padded(vars_[k]), name="adamw_" + k)
            g2, d, m2, v2 = [jnp.transpose(t[:n_in]) for t in (g2, *outs)]
        else:
            g2 = g.reshape(shp[1], shp[2])
            d, m2, v2 = _adamw(weights[k][0], g2, moms[k][0], vars_[k][0], name="adamw_" + k)
        grads[k], deltas[k], new_m[k], new_v[k] = g2.reshape(shp), d.reshape(shp), m2.reshape(shp), v2.reshape(shp)
    sg, sd, sm, sv = _adamw_small(_pack_small(small), small_parts, _pack_small({k: moms[k] for k in SMALL}),
                                  _pack_small({k: vars_[k] for k in SMALL}), name="adamw_small")
    for dst, packed in ((grads, sg), (deltas, sd), (new_m, sm), (new_v, sv)):
        dst.update(_unpack_small(packed, small))

    loss = sg[len(SMALL), 0]
    return (loss, grad_x[None], *[grads[k] for k in ORDER], *[deltas[k] for k in ORDER],
            *[new_m[k] for k in ORDER], *[new_v[k] for k in ORDER])
```

```python
import functools

import jax
import jax.numpy as jnp
from jax import lax
from jax.experimental import pallas as pl
from jax.experimental.pallas import tpu as pltpu

F32 = jnp.float32
BF16 = jnp.bfloat16
MESH_ID = pl.DeviceIdType.MESH

HD = 64
SB_HEADS = 8
FOX_HEADS = 8
MEM_HEADS = 4
MEM_HD = 128
D_SB = SB_HEADS * HD
D_FOX = FOX_HEADS * HD
D_MEM = MEM_HEADS * MEM_HD
EPS = 1e-6
NEG_INF = -1e30

ADAM_LR = 0.001
ADAM_B1 = 0.9
ADAM_B2 = 0.999
ADAM_EPS = 1e-08
ADAM_WD = 0.01
ADAM_STEP = 10

N_CHIPS = 4
VMEM_LIMIT = 56 * 1024 * 1024

F_PAD = 256


def _tile(n, target, align=128):
    if n <= target:
        return n
    best = None
    t = align
    while t <= target:
        if n % t == 0:
            best = t
        t += align
    assert best is not None, (n, target, align)
    return best


def _params(sem):
    return pltpu.CompilerParams(dimension_semantics=sem, vmem_limit_bytes=VMEM_LIMIT)


def _mm(a, b, *, name, ta=False, tb=False, out_dtypes=(F32,), epilogue=None, extras=(),
        tm=1024, tn=1024, tk=2048, comm=None, col_sums=0, a_gain=None):
    if ta:
        K, M = a.shape
    else:
        M, K = a.shape
    if tb:
        N, K2 = b.shape
    else:
        K2, N = b.shape
    assert K == K2, (a.shape, b.shape, ta, tb)
    tm, tn, tk = _tile(M, tm), _tile(N, tn), _tile(K, tk)
    nk = K // tk
    normed = a_gain is not None
    assert not normed or (nk == 1 and not ta)
    n_extra, n_out = len(extras) + normed, len(out_dtypes) + col_sums + normed
    if epilogue is None:
        epilogue = lambda acc: (acc,)
    dims = (((0 if ta else 1,), (1 if tb else 0,)), ((), ()))

    gm, gn = M // tm, N // tn

    def body(*refs):
        i, j, k = pl.program_id(0), pl.program_id(1), pl.program_id(2)
        got = _carry(comm, 2 + n_extra, n_out, (i == 0) & (j == 0) & (k == 0),
                     (i == gm - 1) & (j == gn - 1) & (k == nk - 1), refs)
        (a_ref, b_ref, *extra_refs), out_refs = got[0], list(got[1])
        if normed:
            gain_ref, normed_ref = extra_refs.pop(), out_refs.pop()
            av = a_ref[...].astype(F32)
            av = (av * lax.rsqrt(jnp.mean(av * av, axis=-1, keepdims=True) + EPS) * gain_ref[...]).astype(BF16)

            @pl.when(j == 0)
            def _():
                normed_ref[...] = av
        else:
            av = a_ref[...].astype(BF16)
        part = lax.dot_general(av, b_ref[...].astype(BF16), dims, preferred_element_type=F32)

        def finish(acc):
            outs = epilogue(acc, *[r[...] for r in extra_refs])
            for o_ref, o in zip(out_refs[:len(out_dtypes)], outs):
                o_ref[...] = o.astype(o_ref.dtype)
            for o_ref, o in zip(out_refs[len(out_dtypes):], outs[len(out_dtypes):]):
                first = lax.broadcasted_iota(jnp.int32, o_ref.shape, 0) == 0
                o_ref[...] = jnp.where(first, jnp.broadcast_to(o, o_ref.shape), 0.0)

        if nk == 1:
            finish(part)
        else:
            acc_ref = refs[-1]

            @pl.when(k == 0)
            def _():
                acc_ref[...] = part

            @pl.when((k > 0) & (k < nk - 1))
            def _():
                acc_ref[...] += part

            @pl.when(k == nk - 1)
            def _():
                finish(acc_ref[...] + part)

        got[2]()

    a_spec = pl.BlockSpec((tk, tm), lambda i, j, k: (k, i)) if ta else pl.BlockSpec((tm, tk), lambda i, j, k: (i, k))
    b_spec = pl.BlockSpec((tn, tk), lambda i, j, k: (j, k)) if tb else pl.BlockSpec((tk, tn), lambda i, j, k: (k, j))
    mn_spec = pl.BlockSpec((tm, tn), lambda i, j, k: (i, j))
    row_spec = pl.BlockSpec((1, tn), lambda i, j, k: (0, j))
    sum_spec = pl.BlockSpec((8, tn), lambda i, j, k: (i, j))
    c_ins, c_in_specs, c_out_specs, c_out_shape, c_scratch = _comm_args(comm)
    sem = ("parallel", "arbitrary" if normed else "parallel", "arbitrary") if comm is None else ("arbitrary",) * 3
    outs = pl.pallas_call(
        body, name=name,
        grid=(gm, gn, nk),
        in_specs=[a_spec, b_spec] + [row_spec if e.shape[0] == 1 else mn_spec for e in extras]
        + ([pl.BlockSpec((1, K), lambda i, j, k: (0, 0))] if normed else []) + c_in_specs,
        out_specs=[mn_spec] * len(out_dtypes) + [sum_spec] * col_sums
        + ([pl.BlockSpec((tm, K), lambda i, j, k: (i, 0))] if normed else []) + c_out_specs,
        out_shape=[jax.ShapeDtypeStruct((M, N), dt) for dt in out_dtypes]
        + [jax.ShapeDtypeStruct((8 * gm, N), F32)] * col_sums
        + ([jax.ShapeDtypeStruct((M, K), BF16)] if normed else []) + c_out_shape,
        scratch_shapes=c_scratch + ([pltpu.VMEM((tm, tn), F32)] if nk > 1 else []),
        compiler_params=_params(sem),
    )(a, b, *extras, *([a_gain] if normed else []), *c_ins)
    return outs if len(outs) > 1 else outs[0]


def _row_tile(rows, cols, n_arrays):
    budget = 10 * 1024 * 1024
    cols_padded = -(-cols // 128) * 128
    target = max(16, budget // (cols_padded * 4 * n_arrays * 2))
    return _tile(rows, target, align=16)


def _ew(fn, ins, out_dtypes, *, name):
    R, C = ins[0].shape
    n_in, n_out = len(ins), len(out_dtypes)
    tr = _row_tile(R, C, n_in + n_out)

    def body(*refs):
        outs = fn(*[r[...] for r in refs[:n_in]])
        for o_ref, o in zip(refs[n_in:], outs):
            o_ref[...] = o.astype(o_ref.dtype)

    spec = pl.BlockSpec((tr, C), lambda i: (i, 0))
    outs = pl.pallas_call(
        body, name=name, grid=(R // tr,),
        in_specs=[spec] * n_in, out_specs=[spec] * n_out,
        out_shape=[jax.ShapeDtypeStruct((R, C), dt) for dt in out_dtypes],
        compiler_params=_params(("parallel",)),
    )(*ins)
    return outs if n_out > 1 else outs[0]


def _rmsnorm_fwd(x, g, out_dtype, *, name):
    R, d = x.shape
    tr = _row_tile(R, d, 3)

    def body(x_ref, g_ref, o_ref):
        xv = x_ref[...].astype(F32)
        r = lax.rsqrt(jnp.mean(xv * xv, axis=-1, keepdims=True) + EPS)
        o_ref[...] = (xv * r * g_ref[...]).astype(o_ref.dtype)

    return pl.pallas_call(
        body, name=name, grid=(R // tr,),
        in_specs=[pl.BlockSpec((tr, d), lambda i: (i, 0)), pl.BlockSpec((1, d), lambda i: (0, 0))],
        out_specs=pl.BlockSpec((tr, d), lambda i: (i, 0)),
        out_shape=jax.ShapeDtypeStruct((R, d), out_dtype),
        compiler_params=_params(("parallel",)),
    )(x, g)


def _rmsnorm_bwd(x, g, dy, add=None, *, name):
    R, d = x.shape
    has_add = add is not None
    tr = _row_tile(R, d, 5)

    def body(*refs):
        x_ref, g_ref, dy_ref = refs[:3]
        add_ref = refs[3] if has_add else None
        dx_ref, dg_ref = refs[-2:]
        xv = x_ref[...].astype(F32)
        dyv = dy_ref[...].astype(F32)
        r = lax.rsqrt(jnp.mean(xv * xv, axis=-1, keepdims=True) + EPS)
        xh = xv * r
        dyg = dyv * g_ref[...]
        c = jnp.mean(dyg * xh, axis=-1, keepdims=True)
        dx = r * (dyg - xh * c)
        if has_add:
            dx = dx + add_ref[...]
        dx_ref[...] = dx

        @pl.when(pl.program_id(0) == 0)
        def _():
            dg_ref[...] = jnp.zeros_like(dg_ref)

        dg_ref[...] += jnp.sum(dyv * xh, axis=0, keepdims=True)

    row = pl.BlockSpec((tr, d), lambda i: (i, 0))
    vec = pl.BlockSpec((1, d), lambda i: (0, 0))
    ins = [x, g, dy] + ([add] if has_add else [])
    return pl.pallas_call(
        body, name=name, grid=(R // tr,),
        in_specs=[row, vec, row] + ([row] if has_add else []),
        out_specs=[row, vec],
        out_shape=[jax.ShapeDtypeStruct((R, d), F32), jax.ShapeDtypeStruct((1, d), F32)],
        compiler_params=_params(("arbitrary",)),
    )(*ins)


_NT = (((1,), (1,)), ((), ()))
_TN = (((0,), (0,)), ((), ()))


def _dot(a, b, dims=(((1,), (0,)), ((), ()))):
    return lax.dot_general(a, b, dims, preferred_element_type=F32)


def _log_sigmoid_pair(z):
    sp = jnp.log(1.0 + jnp.exp(-jnp.abs(z)))
    return jnp.minimum(z, 0.0) - sp, jnp.minimum(-z, 0.0) - sp


LANES = 128
_LOW = -3e38


def _lane_masks(hd, rows):
    if hd == LANES:
        return [None]
    lane = lax.broadcasted_iota(jnp.int32, (rows, LANES), 1)
    return [(lane >= hh * hd) & (lane < (hh + 1) * hd) for hh in range(LANES // hd)]


def _keep(t, m):
    return t if m is None else jnp.where(m, t, 0.0)


def _merge(parts, masks):
    out = parts[-1]
    for p, m in zip(parts[-2::-1], masks[-2::-1]):
        out = jnp.where(m, p, out)
    return out


def _row_value(t, m):
    return jnp.max(t if m is None else jnp.where(m, t, _LOW), axis=1, keepdims=True)


def _cols(tq, off):
    return pl.BlockSpec((tq, LANES), lambda g, i: (i, off + g))


def _cols_all(rows, off):
    return pl.BlockSpec((rows, LANES), lambda g, i: (0, off + g))


SCAN_BLOCK = 256


def _tri(kind, cols):
    n = min(SCAN_BLOCK, cols)
    r = lax.broadcasted_iota(jnp.int32, (n, n), 0)
    c = lax.broadcasted_iota(jnp.int32, (n, n), 1)
    return ((r > c) if kind == "after" else (r < c)).astype(BF16)


def _scan_cols(x, tri, reverse):
    cols = x.shape[1]
    cb = min(SCAN_BLOCK, cols)
    assert cols % cb == 0 and tri.shape == (cb, cb)
    nb = cols // cb
    blocks = [x[:, b * cb:(b + 1) * cb] for b in range(nb)]
    outs, carry = [None] * nb, None
    for b in (reversed(range(nb)) if reverse else range(nb)):
        y = _dot(blocks[b].astype(BF16), tri)
        outs[b] = y if carry is None else y + carry
        s = jnp.sum(blocks[b], axis=1, keepdims=True)
        carry = s if carry is None else carry + s
    return (outs[0] if nb == 1 else jnp.concatenate(outs, axis=1)), carry


def _softplus_parts(z):
    pos = jnp.maximum(z, 0.0) + jnp.log(1.0 + jnp.exp(-jnp.abs(z)))
    return pos, z - pos


class _ChipExchange:
    def __init__(self, kind, ins):
        assert kind in ("gather", "scatter", "siblings")
        self.kind, self.ins = kind, list(ins)
        lead = {"gather": lambda s: (N_CHIPS,) + s, "scatter": lambda s: (3,) + s[1:], "siblings": lambda s: s[:1] + s[2:]}[kind]
        self.out_shape = [jax.ShapeDtypeStruct(lead(a.shape), a.dtype) for a in ins]
        n = (N_CHIPS if kind == "siblings" else 3) * len(ins)
        self.scratch = [pltpu.SemaphoreType.DMA((n,)), pltpu.SemaphoreType.DMA((n,))]

    def _copies(self, in_refs, out_refs, sems, landing):
        x, y, c, chips = _mesh_place()
        me = 2 * x + y
        out = []
        for w in range(len(self.ins)):
            if self.kind == "siblings":
                for s in range(N_CHIPS):
                    land = out_refs[w].at[s]
                    src = land if landing else in_refs[w].at[s, 1 - c]
                    out.append(_remote(src, land, sems, N_CHIPS * w + s, (x, y, 1 - c)))
                continue
            for j, chip in enumerate(chips):
                peer = 2 * chip[0] + chip[1]
                if self.kind == "gather":
                    src, dst, land = in_refs[w].at[c], out_refs[w].at[me, c], out_refs[w].at[peer, c]
                else:
                    src, dst, land = in_refs[w].at[peer], out_refs[w].at[j], out_refs[w].at[j]
                if landing:
                    src, dst = land, land
                out.append(_remote(src, dst, sems, 3 * w + j, (chip[0], chip[1], c)))
        return out

    def start(self, in_refs, out_refs, sems):
        for cp in self._copies(in_refs, out_refs, sems, False):
            cp.start()

    def finish(self, in_refs, out_refs, sems):
        for cp in self._copies(in_refs, out_refs, sems, True):
            cp.wait_recv()
        for cp in self._copies(in_refs, out_refs, sems, False):
            cp.wait_send()


def _carry(comm, n_in, n_out, first, last, refs):
    if comm is None:
        return refs[:n_in], refs[n_in:n_in + n_out], (lambda: None)
    a, b = len(comm.ins), len(comm.out_shape)
    ins, c_in = refs[:n_in], refs[n_in:n_in + a]
    outs, c_out = refs[n_in + a:n_in + a + n_out], refs[n_in + a + n_out:n_in + a + n_out + b]
    sems = refs[n_in + a + n_out + b:n_in + a + n_out + b + 2]
    pl.when(first)(lambda: comm.start(c_in, c_out, sems))
    return ins, outs, (lambda: pl.when(last)(lambda: comm.finish(c_in, c_out, sems)))


def _sbl_fwd(q, k, v, *, width, hd, name, tq=256, comm=None):
    (qa, qo), (ka, ko), (va, vo) = q, k, v
    S = qa.shape[0]
    tq = _tile(S, tq)
    tk = tq
    scale = hd ** -0.5
    n_g, n_q = width // LANES, S // tq

    def body(*refs):
        qi = pl.program_id(1)
        gi = pl.program_id(0)
        got = _carry(comm, 3, 2, (gi == 0) & (qi == 0), (gi == n_g - 1) & (qi == n_q - 1), refs)
        (q_ref, k_ref, v_ref), (o_ref, tot_ref) = got[0], got[1]
        masks = _lane_masks(hd, tq)
        qs = q_ref[...].astype(F32) * scale
        qm = [_keep(qs, m).astype(BF16) for m in masks]
        strict = lax.broadcasted_iota(jnp.int32, (tq, tk), 1) < lax.broadcasted_iota(jnp.int32, (tq, tk), 0)
        later = _tri("after", tk)

        def tile(kb, carry, diag):
            ks = pl.multiple_of(kb * tk, tk)
            kv = k_ref[pl.ds(ks, tk), :].astype(BF16)
            vv = v_ref[pl.ds(ks, tk), :].astype(BF16)
            out = []
            for hh in range(len(masks)):
                acc, c_pos = carry[2 * hh], carry[2 * hh + 1]
                pos, ls = _softplus_parts(_dot(qm[hh], kv, _NT))
                if diag:
                    pos = jnp.where(strict, pos, 0.0)
                pos_after, pos_all = _scan_cols(pos, later, True)
                w = jnp.exp(ls - (pos_after + c_pos))
                if diag:
                    w = jnp.where(strict, w, 0.0)
                out += [acc + _dot(w.astype(BF16), vv), c_pos + pos_all]
            return tuple(out)

        init = (jnp.zeros((tq, LANES), F32), jnp.zeros((tq, 1), F32)) * len(masks)
        carry = tile(qi, init, True)
        carry = lax.fori_loop(0, qi, lambda i, c: tile(qi - 1 - i, c, False), carry)
        o_ref[...] = _merge(carry[0::2], masks).astype(o_ref.dtype)
        tot_ref[...] = _merge([jnp.broadcast_to(-c, (tq, LANES)) for c in carry[1::2]], masks)
        got[2]()

    c_ins, c_in_specs, c_out_specs, c_out_shape, c_scratch = _comm_args(comm)
    return pl.pallas_call(
        body, name=name, grid=(n_g, n_q),
        in_specs=[_cols(tq, qo), _cols_all(S, ko), _cols_all(S, vo)] + c_in_specs,
        out_specs=[_cols(tq, 0), _cols(tq, 0)] + c_out_specs,
        out_shape=[jax.ShapeDtypeStruct((S, width), BF16), jax.ShapeDtypeStruct((S, width), F32)] + c_out_shape,
        scratch_shapes=c_scratch,
        compiler_params=_params(("arbitrary", "arbitrary")),
    )(qa, ka, va, *c_ins)


def _comm_args(comm):
    if comm is None:
        return [], [], [], [], []
    return comm.ins, [_ANY] * len(comm.ins), [_ANY] * len(comm.out_shape), comm.out_shape, comm.scratch


def _sbl_bwd(q, k, v, do, tot, *, width, hd, name, tq=256, comm=None):
    (qa, qo), (ka, ko), (va, vo) = q, k, v
    S = qa.shape[0]
    tq = _tile(S, tq)
    tk = tq
    scale = hd ** -0.5
    n_g, n_q = width // LANES, S // tq

    def body(*refs):
        qi = pl.program_id(1)
        gi = pl.program_id(0)
        got = _carry(comm, 5, 3, (gi == 0) & (qi == 0), (gi == n_g - 1) & (qi == n_q - 1), refs)
        (q_ref, k_ref, v_ref, do_ref, tot_ref), (dq_ref, dk_ref, dv_ref) = got[0], got[1]

        @pl.when(qi == 0)
        def _():
            dk_ref[...] = jnp.zeros_like(dk_ref)
            dv_ref[...] = jnp.zeros_like(dv_ref)

        masks = _lane_masks(hd, tq)
        qs = q_ref[...].astype(F32) * scale
        qm = [_keep(qs, m).astype(BF16) for m in masks]
        dov = [_keep(do_ref[...], m).astype(BF16) for m in masks]
        rest = [-_row_value(tot_ref[...], m) for m in masks]
        strict = lax.broadcasted_iota(jnp.int32, (tq, tk), 1) < lax.broadcasted_iota(jnp.int32, (tq, tk), 0)
        later, before = _tri("after", tk), _tri("before", tk)

        def tile(kb, carry, diag):
            ks = pl.multiple_of(kb * tk, tk)
            kv = k_ref[pl.ds(ks, tk), :].astype(BF16)
            vv = v_ref[pl.ds(ks, tk), :].astype(BF16)
            out = []
            dk_t, dv_t = None, None
            for hh in range(len(masks)):
                dq, c_pos, c_g = carry[3 * hh:3 * hh + 3]
                pos, ls = _softplus_parts(_dot(qm[hh], kv, _NT))
                if diag:
                    pos = jnp.where(strict, pos, 0.0)
                pos_after, pos_all = _scan_cols(pos, later, True)
                c_pos = c_pos + pos_all
                w = jnp.exp(ls - (pos_after + (rest[hh] - c_pos)))
                if diag:
                    w = jnp.where(strict, w, 0.0)
                g = _dot(dov[hh], vv, _NT) * w
                g_before, g_all = _scan_cols(g, before, False)
                g_before = g_before + c_g
                dz = g - jnp.exp(ls) * (g + g_before)
                if diag:
                    dz = jnp.where(strict, dz, 0.0)
                dzb = dz.astype(BF16)
                dk_h = _dot(dzb, qm[hh], _TN)
                dv_h = _dot(w.astype(BF16), dov[hh], _TN)
                dk_t = dk_h if dk_t is None else dk_t + dk_h
                dv_t = dv_h if dv_t is None else dv_t + dv_h
                out += [dq + _dot(dzb, kv), c_pos, c_g + g_all]
            dk_ref[pl.ds(ks, tk), :] += dk_t
            dv_ref[pl.ds(ks, tk), :] += dv_t
            return tuple(out)

        zero = jnp.zeros((tq, 1), F32)
        init = (jnp.zeros((tq, LANES), F32), zero, zero) * len(masks)
        carry = lax.fori_loop(0, qi, lambda kb, c: tile(kb, c, False), init)
        carry = tile(qi, carry, True)
        dq_ref[...] = _merge(carry[0::3], masks) * scale
        got[2]()

    full = jax.ShapeDtypeStruct((S, width), F32)
    c_ins, c_in_specs, c_out_specs, c_out_shape, c_scratch = _comm_args(comm)
    return pl.pallas_call(
        body, name=name, grid=(n_g, n_q),
        in_specs=[_cols(tq, qo), _cols_all(S, ko), _cols_all(S, vo), _cols(tq, do[1]), _cols(tq, tot[1])] + c_in_specs,
        out_specs=[_cols(tq, 0), _cols_all(S, 0), _cols_all(S, 0)] + c_out_specs,
        out_shape=[full, full, full] + c_out_shape,
        scratch_shapes=c_scratch,
        compiler_params=_params(("arbitrary", "arbitrary")),
    )(qa, ka, va, do[0], tot[0], *c_ins)


def _sml_fwd(q, k, v, bias=None, *, width, hd, causal, name, tq=256, tk=256, comm=None):
    (qa, qo), (ka, ko), (va, vo) = q, k, v
    S, Sk = qa.shape[0], ka.shape[0]
    tq, tk = _tile(S, tq), _tile(Sk, tk)
    if causal:
        assert tq == tk and S == Sk
    nk = Sk // tk
    hpg = LANES // hd
    scale = hd ** -0.5
    has_bias = bias is not None
    n_g, n_q = width // LANES, S // tq

    def body(*all_refs):
        qi, gi = pl.program_id(1), pl.program_id(0)
        got = _carry(comm, 5 if has_bias else 3, 2, (gi == 0) & (qi == 0), (gi == n_g - 1) & (qi == n_q - 1), all_refs)
        refs = tuple(got[0]) + tuple(got[1])
        q_ref, k_ref, v_ref = refs[:3]
        o_ref, lse_ref = refs[-2:]
        masks = _lane_masks(hd, tq)
        qs = q_ref[...].astype(F32) * scale
        qm = [_keep(qs, m).astype(BF16) for m in masks]
        allowed = lax.broadcasted_iota(jnp.int32, (tq, tk), 1) <= lax.broadcasted_iota(jnp.int32, (tq, tk), 0)

        def tile(kb, carry, diag):
            ks = pl.multiple_of(kb * tk, tk)
            kv = k_ref[pl.ds(ks, tk), :].astype(BF16)
            vv = v_ref[pl.ds(ks, tk), :].astype(BF16)
            out = []
            for hh in range(hpg):
                m, l, acc = carry[3 * hh:3 * hh + 3]
                z = _dot(qm[hh], kv, _NT)
                if has_bias:
                    z = z + refs[3][hh] - refs[4][hh, kb]
                if diag:
                    z = jnp.where(allowed, z, NEG_INF)
                m2 = jnp.maximum(m, jnp.max(z, axis=1, keepdims=True))
                p = jnp.exp(z - m2)
                alpha = jnp.exp(m - m2)
                out += [m2, alpha * l + jnp.sum(p, axis=1, keepdims=True), alpha * acc + _dot(p.astype(BF16), vv)]
            return tuple(out)

        init = (jnp.full((tq, 1), NEG_INF, F32), jnp.zeros((tq, 1), F32), jnp.zeros((tq, LANES), F32)) * hpg
        if causal:
            carry = lax.fori_loop(0, qi, lambda kb, c: tile(kb, c, False), init)
            carry = tile(qi, carry, True)
        else:
            carry = lax.fori_loop(0, nk, lambda kb, c: tile(kb, c, False), init)
        o_ref[...] = _merge([acc / l for l, acc in zip(carry[1::3], carry[2::3])], masks).astype(o_ref.dtype)
        lse_ref[...] = _merge([jnp.broadcast_to(m + jnp.log(l), (tq, LANES)) for m, l in zip(carry[0::3], carry[1::3])], masks)
        got[2]()

    in_specs = [_cols(tq, qo), _cols_all(Sk, ko), _cols_all(Sk, vo)]
    ins = [qa, ka, va]
    if has_bias:
        in_specs += [pl.BlockSpec((hpg, tq, 1), lambda g, i: (g, i, 0)),
                     pl.BlockSpec((hpg, nk, 1, tk), lambda g, i: (g, 0, 0, 0))]
        ins += list(bias)
    c_ins, c_in_specs, c_out_specs, c_out_shape, c_scratch = _comm_args(comm)
    return pl.pallas_call(
        body, name=name, grid=(n_g, n_q),
        in_specs=in_specs + c_in_specs, out_specs=[_cols(tq, 0), _cols(tq, 0)] + c_out_specs,
        out_shape=[jax.ShapeDtypeStruct((S, width), BF16), jax.ShapeDtypeStruct((S, width), F32)] + c_out_shape,
        scratch_shapes=c_scratch,
        compiler_params=_params(("arbitrary", "arbitrary") if comm is not None else ("parallel", "arbitrary")),
    )(*ins, *c_ins)


def _sml_bwd(q, k, v, o, lse, do, bias=None, *, width, hd, causal, name, tq=256, tk=256, comm=None):
    (qa, qo), (ka, ko), (va, vo) = q, k, v
    S, Sk = qa.shape[0], ka.shape[0]
    tq, tk = _tile(S, tq), _tile(Sk, tk)
    nk = Sk // tk
    hpg = LANES // hd
    scale = hd ** -0.5
    has_bias = bias is not None
    n_in = 8 if has_bias else 6
    n_g, n_q = width // LANES, S // tq

    def body(*all_refs):
        qi, gi = pl.program_id(1), pl.program_id(0)
        got = _carry(comm, n_in, 5 if has_bias else 3, (gi == 0) & (qi == 0), (gi == n_g - 1) & (qi == n_q - 1), all_refs)
        refs = tuple(got[0]) + tuple(got[1])
        q_ref, k_ref, v_ref, o_ref, lse_ref, do_ref = refs[:6]
        dq_ref, dk_ref, dv_ref = refs[n_in:n_in + 3]

        @pl.when(qi == 0)
        def _():
            dk_ref[...] = jnp.zeros_like(dk_ref)
            dv_ref[...] = jnp.zeros_like(dv_ref)
            if has_bias:
                refs[n_in + 4][...] = jnp.zeros_like(refs[n_in + 4])

        masks = _lane_masks(hd, tq)
        qs = q_ref[...].astype(F32) * scale
        qm = [_keep(qs, m).astype(BF16) for m in masks]
        do32 = do_ref[...]
        dov = [_keep(do32, m).astype(BF16) for m in masks]
        prod = do32 * o_ref[...].astype(F32)
        delta = [jnp.sum(_keep(prod, m), axis=1, keepdims=True) for m in masks]
        lses = [_row_value(lse_ref[...], m) for m in masks]
        allowed = lax.broadcasted_iota(jnp.int32, (tq, tk), 1) <= lax.broadcasted_iota(jnp.int32, (tq, tk), 0)

        def tile(kb, carry, diag):
            ks = pl.multiple_of(kb * tk, tk)
            kv = k_ref[pl.ds(ks, tk), :].astype(BF16)
            vv = v_ref[pl.ds(ks, tk), :].astype(BF16)
            out = []
            dk_t, dv_t = None, None
            for hh in range(hpg):
                dq, db_row = carry[2 * hh:2 * hh + 2]
                z = _dot(qm[hh], kv, _NT)
                if has_bias:
                    z = z + refs[6][hh] - refs[7][hh, kb]
                p = jnp.exp(z - lses[hh])
                if diag:
                    p = jnp.where(allowed, p, 0.0)
                dz = p * (_dot(dov[hh], vv, _NT) - delta[hh])
                dzb = dz.astype(BF16)
                dk_h = _dot(dzb, qm[hh], _TN)
                dv_h = _dot(p.astype(BF16), dov[hh], _TN)
                dk_t = dk_h if dk_t is None else dk_t + dk_h
                dv_t = dv_h if dv_t is None else dv_t + dv_h
                if has_bias:
                    db_row = db_row + jnp.sum(dz, axis=1, keepdims=True)
                    refs[n_in + 4][hh, kb] += jnp.sum(dz, axis=0, keepdims=True)
                out += [dq + _dot(dzb, kv), db_row]
            dk_ref[pl.ds(ks, tk), :] += dk_t
            dv_ref[pl.ds(ks, tk), :] += dv_t
            return tuple(out)

        init = (jnp.zeros((tq, LANES), F32), jnp.zeros((tq, 1), F32)) * hpg
        if causal:
            carry = lax.fori_loop(0, qi, lambda kb, c: tile(kb, c, False), init)
            carry = tile(qi, carry, True)
        else:
            carry = lax.fori_loop(0, nk, lambda kb, c: tile(kb, c, False), init)
        dq_ref[...] = _merge(carry[0::2], masks) * scale
        if has_bias:
            for hh in range(hpg):
                refs[n_in + 3][hh] = carry[2 * hh + 1]
        got[2]()

    in_specs = [_cols(tq, qo), _cols_all(Sk, ko), _cols_all(Sk, vo), _cols(tq, o[1]), _cols(tq, lse[1]), _cols(tq, do[1])]
    ins = [qa, ka, va, o[0], lse[0], do[0]]
    out_specs = [_cols(tq, 0), _cols_all(Sk, 0), _cols_all(Sk, 0)]
    out_shape = [jax.ShapeDtypeStruct((S, width), F32), jax.ShapeDtypeStruct((Sk, width), F32),
                 jax.ShapeDtypeStruct((Sk, width), F32)]
    if has_bias:
        rspec = pl.BlockSpec((hpg, tq, 1), lambda g, i: (g, i, 0))
        cspec = pl.BlockSpec((hpg, nk, 1, tk), lambda g, i: (g, 0, 0, 0))
        in_specs += [rspec, cspec]
        ins += list(bias)
        out_specs += [rspec, cspec]
        n_heads = width // hd
        out_shape += [jax.ShapeDtypeStruct((n_heads, S, 1), F32), jax.ShapeDtypeStruct((n_heads, nk, 1, tk), F32)]
    c_ins, c_in_specs, c_out_specs, c_out_shape, c_scratch = _comm_args(comm)
    return pl.pallas_call(
        body, name=name, grid=(n_g, n_q),
        in_specs=in_specs + c_in_specs, out_specs=out_specs + c_out_specs, out_shape=out_shape + c_out_shape,
        scratch_shapes=c_scratch,
        compiler_params=_params(("arbitrary", "arbitrary") if comm is not None else ("parallel", "arbitrary")),
    )(*ins, *c_ins)


def _head_sums(t, masks):
    sums = [jnp.sum(_keep(t, m), axis=1, keepdims=True) for m in masks]
    return _merge([jnp.broadcast_to(s, t.shape) for s in sums], masks) if len(masks) > 1 else sums[0]


def _hnorm_fwd(x, g_lanes, *, width, hd, name, tr=1024):
    xa, xo = x
    R = xa.shape[0]
    tr = _tile(R, tr, align=16)
    n_blk = width // LANES

    def body(x_ref, g_ref, o_ref):
        masks = _lane_masks(hd, tr)
        for j in range(n_blk):
            sl = slice(j * LANES, (j + 1) * LANES)
            xv = x_ref[:, sl].astype(F32)
            r = lax.rsqrt(_head_sums(xv * xv, masks) * (1.0 / hd) + EPS)
            o_ref[:, sl] = (xv * r * g_ref[...]).astype(o_ref.dtype)

    assert (xo * LANES) % width == 0
    return pl.pallas_call(
        body, name=name, grid=(R // tr,),
        in_specs=[pl.BlockSpec((tr, width), lambda i: (i, xo * LANES // width)), pl.BlockSpec((1, LANES), lambda i: (0, 0))],
        out_specs=pl.BlockSpec((tr, width), lambda i: (i, 0)),
        out_shape=jax.ShapeDtypeStruct((R, width), BF16),
        compiler_params=_params(("parallel",)),
    )(xa, g_lanes)


def _hnorm_bwd(x, g_lanes, dy, *, width, hd, name, tr=1024):
    xa, xo = x
    R = xa.shape[0]
    tr = _tile(R, tr, align=16)
    n_blk = width // LANES

    def body(x_ref, g_ref, dy_ref, dx_ref, dg_ref):
        masks = _lane_masks(hd, tr)
        dg = jnp.zeros((1, LANES), F32)
        for j in range(n_blk):
            sl = slice(j * LANES, (j + 1) * LANES)
            xv = x_ref[:, sl].astype(F32)
            dyv = dy_ref[:, sl].astype(F32)
            r = lax.rsqrt(_head_sums(xv * xv, masks) * (1.0 / hd) + EPS)
            xh = xv * r
            dyg = dyv * g_ref[...]
            c = _head_sums(dyg * xh, masks) * (1.0 / hd)
            dx_ref[:, sl] = (r * (dyg - xh * c)).astype(dx_ref.dtype)
            dg = dg + jnp.sum(dyv * xh, axis=0, keepdims=True)
        if hd * 2 == LANES:
            dg8 = jnp.broadcast_to(dg, (8, LANES))
            dg = (dg8 + pltpu.roll(dg8, shift=hd, axis=1))[0:1]
        else:
            assert hd == LANES

        @pl.when(pl.program_id(0) == 0)
        def _():
            dg_ref[...] = jnp.zeros_like(dg_ref)

        dg_ref[...] += dg

    assert (xo * LANES) % width == 0
    return pl.pallas_call(
        body, name=name, grid=(R // tr,),
        in_specs=[pl.BlockSpec((tr, width), lambda i: (i, xo * LANES // width)), pl.BlockSpec((1, LANES), lambda i: (0, 0)),
                  pl.BlockSpec((tr, width), lambda i: (i, 0))],
        out_specs=[pl.BlockSpec((tr, width), lambda i: (i, 0)), pl.BlockSpec((1, LANES), lambda i: (0, 0))],
        out_shape=[jax.ShapeDtypeStruct((R, width), BF16), jax.ShapeDtypeStruct((1, LANES), F32)],
        compiler_params=_params(("arbitrary",)),
    )(xa, g_lanes, dy)


def _split3_dot(x, tri):
    a = x.astype(BF16)
    r = x - a.astype(F32)
    b = r.astype(BF16)
    c = (r - b.astype(F32)).astype(BF16)
    return _dot(a, tri) + _dot(b, tri) + _dot(c, tri)


def _forget_fwd(logit_t, b_col, *, name, blk=512):
    H, S = logit_t.shape
    blk = _tile(S, blk)

    def body(l_ref, b_ref, f_ref):
        r_i = lax.broadcasted_iota(jnp.int32, (blk, blk), 0)
        c_i = lax.broadcasted_iota(jnp.int32, (blk, blk), 1)
        upto = (r_i <= c_i).astype(BF16)
        carry = jnp.zeros((H, 1), F32)
        for j in range(S // blk):
            u = l_ref[:, j * blk:(j + 1) * blk] + b_ref[...]
            lf, _ = _log_sigmoid_pair(u)
            f_ref[:, j * blk:(j + 1) * blk] = _split3_dot(lf, upto) + carry
            carry = carry + jnp.sum(lf, axis=1, keepdims=True)

    return pl.pallas_call(
        body, name=name,
        out_shape=jax.ShapeDtypeStruct((H, S), F32),
        compiler_params=pltpu.CompilerParams(vmem_limit_bytes=VMEM_LIMIT),
    )(logit_t, b_col)


def _forget_bwd(logit_t, b_col, d_f, *, name, blk=512):
    H, S = logit_t.shape
    blk = _tile(S, blk)

    def body(l_ref, b_ref, df_ref, dl_ref, db_ref):
        r_i = lax.broadcasted_iota(jnp.int32, (blk, blk), 0)
        c_i = lax.broadcasted_iota(jnp.int32, (blk, blk), 1)
        fromon = (r_i >= c_i).astype(BF16)
        carry = jnp.zeros((H, 1), F32)
        db = jnp.zeros((H, 1), F32)
        for j in reversed(range(S // blk)):
            sl = slice(j * blk, (j + 1) * blk)
            dfv = df_ref[:, sl]
            d_lf = _split3_dot(dfv, fromon) + carry
            carry = carry + jnp.sum(dfv, axis=1, keepdims=True)
            u = l_ref[:, sl] + b_ref[...]
            _, lsn = _log_sigmoid_pair(u)
            dl = d_lf * jnp.exp(lsn)
            dl_ref[:, sl] = dl
            db = db + jnp.sum(dl, axis=1, keepdims=True)
        db_ref[...] = db

    return pl.pallas_call(
        body, name=name,
        out_shape=[jax.ShapeDtypeStruct((H, S), F32), jax.ShapeDtypeStruct((H, 1), F32)],
        compiler_params=pltpu.CompilerParams(vmem_limit_bytes=VMEM_LIMIT),
    )(logit_t, b_col, d_f)


def _sigmoid(t):
    return 1.0 / (1.0 + jnp.exp(-t))


def _gate_fwd(o3, w3, proj, x, w_out, g_norm, D, *, name, tm=512):
    S = proj.shape[0]
    tm = _tile(S, tm)

    def body(o0, o1, o2, w0, w1, w2, g0, g1, g2, x_ref, wo_ref, gn_ref, merged_ref, x1_ref, h2_ref):
        acc = None
        for o_ref, w_ref, g_ref in ((o0, w0, g0), (o1, w1, g1), (o2, w2, g2)):
            t = _sigmoid(g_ref[...]) * _dot(o_ref[...], w_ref[...])
            acc = t if acc is None else acc + t
        merged = acc.astype(BF16)
        merged_ref[...] = merged
        x1 = x_ref[...] + _dot(merged, wo_ref[...])
        x1_ref[...] = x1
        h2_ref[...] = (x1 * lax.rsqrt(jnp.mean(x1 * x1, axis=-1, keepdims=True) + EPS) * gn_ref[...]).astype(BF16)

    ospec = lambda d: pl.BlockSpec((tm, d), lambda i: (i, 0))
    wspec = lambda w: pl.BlockSpec(w.shape, lambda i: (0, 0))
    gspec = lambda j: pl.BlockSpec((tm, D), lambda i: (i, j))
    row = pl.BlockSpec((tm, D), lambda i: (i, 0))
    return pl.pallas_call(
        body, name=name, grid=(S // tm,),
        in_specs=[ospec(o.shape[1]) for o in o3] + [wspec(w) for w in w3] + [gspec(j) for j in range(3)]
        + [row, wspec(w_out), wspec(g_norm)],
        out_specs=[row, row, row],
        out_shape=[jax.ShapeDtypeStruct((S, D), BF16), jax.ShapeDtypeStruct((S, D), F32), jax.ShapeDtypeStruct((S, D), BF16)],
        compiler_params=_params(("parallel",)),
    )(*o3, *w3, proj, proj, proj, x, w_out, g_norm)


def _gate_bwd(o3, w3, proj, dx1, w_out, D, *, name, tm=256):
    S = proj.shape[0]
    tm = _tile(S, tm)

    def body(o0, o1, o2, w0, w1, w2, g0, g1, g2, dx_ref, wo_ref, dg_ref, dw0, dw1, dw2, do0, do1, do2):
        first = pl.program_id(0) == 0
        dm = _dot(dx_ref[...].astype(BF16), wo_ref[...], _NT)
        for j, (o_ref, w_ref, g_ref, dw_ref, do_ref) in enumerate(
                ((o0, w0, g0, dw0, do0), (o1, w1, g1, dw1, do1), (o2, w2, g2, dw2, do2))):
            s = _sigmoid(g_ref[...])
            br = _dot(o_ref[...], w_ref[...])
            dg_ref[:, j * D:(j + 1) * D] = (dm * br * s * (1.0 - s)).astype(dg_ref.dtype)
            dbr = (dm * s).astype(BF16)
            do_ref[...] = _dot(dbr, w_ref[...], _NT)
            part = _dot(o_ref[...], dbr, _TN)

            @pl.when(first)
            def _():
                dw_ref[...] = part

            @pl.when(jnp.logical_not(first))
            def _():
                dw_ref[...] += part

    ospec = lambda d: pl.BlockSpec((tm, d), lambda i: (i, 0))
    wspec = lambda w: pl.BlockSpec(w.shape, lambda i: (0, 0))
    gspec = lambda j: pl.BlockSpec((tm, D), lambda i: (i, j))
    dspec = pl.BlockSpec((tm, D), lambda i: (i, 0))
    return pl.pallas_call(
        body, name=name, grid=(S // tm,),
        in_specs=[ospec(o.shape[1]) for o in o3] + [wspec(w) for w in w3] + [gspec(j) for j in range(3)]
        + [dspec, wspec(w_out)],
        out_specs=[pl.BlockSpec((tm, 3 * D), lambda i: (i, 0))] + [wspec(w) for w in w3] + [ospec(o.shape[1]) for o in o3],
        out_shape=[jax.ShapeDtypeStruct((S, proj.shape[1]), BF16)] + [jax.ShapeDtypeStruct(w.shape, F32) for w in w3]
        + [jax.ShapeDtypeStruct((S, o.shape[1]), F32) for o in o3],
        compiler_params=_params(("arbitrary",)),
    )(*o3, *w3, proj, proj, proj, dx1, w_out)


def _pair_sum(stacked, got, core, *, name):
    n, _, r, c = stacked.shape
    tr = _row_tile(r, c, 3)

    def body(core_ref, a_ref, b_ref, o_ref):
        o_ref[...] = (a_ref[0].astype(F32) + b_ref[...].astype(F32)).astype(o_ref.dtype)

    spec = pl.BlockSpec((1, tr, c), lambda s, i, core_ref: (s, i, 0))
    return pl.pallas_call(
        body, name=name,
        grid_spec=pltpu.PrefetchScalarGridSpec(
            num_scalar_prefetch=1, grid=(n, r // tr),
            in_specs=[pl.BlockSpec((1, 1, tr, c), lambda s, i, core_ref: (s, core_ref[0], i, 0)), spec],
            out_specs=spec),
        out_shape=jax.ShapeDtypeStruct((n, r, c), BF16),
        compiler_params=_params(("parallel", "parallel")),
    )(core.astype(jnp.int32).reshape(1), stacked, got)


def _chip_sum(parts, got, chip, *, name):
    _, r, c = parts.shape
    tr = _row_tile(r, c, 6)

    def body(chip_ref, p_ref, q0_ref, q1_ref, q2_ref, o_ref):
        o_ref[...] = ((p_ref[0].astype(F32) + q0_ref[0].astype(F32)) + q1_ref[0].astype(F32)) + q2_ref[0].astype(F32)

    from_chip = lambda j: pl.BlockSpec((1, tr, c), lambda i, chip_ref: (j, i, 0))
    return pl.pallas_call(
        body, name=name,
        grid_spec=pltpu.PrefetchScalarGridSpec(
            num_scalar_prefetch=1, grid=(r // tr,),
            in_specs=[pl.BlockSpec((1, tr, c), lambda i, chip_ref: (chip_ref[0], i, 0))] + [from_chip(j) for j in range(3)],
            out_specs=pl.BlockSpec((tr, c), lambda i, chip_ref: (i, 0))),
        out_shape=jax.ShapeDtypeStruct((r, c), F32),
        compiler_params=_params(("parallel",)),
    )(chip.astype(jnp.int32).reshape(1), parts, got, got, got)


def _adamw_math(w, g, m, v):
    m2 = ADAM_B1 * m + (1.0 - ADAM_B1) * g
    v2 = ADAM_B2 * v + (1.0 - ADAM_B2) * (g * g)
    m_hat = m2 / (1.0 - ADAM_B1 ** ADAM_STEP)
    v_hat = v2 / (1.0 - ADAM_B2 ** ADAM_STEP)
    delta = -ADAM_LR * (m_hat / (jnp.sqrt(v_hat) + ADAM_EPS) + ADAM_WD * w)
    return delta, m2, v2


def _adamw(w, g, m, v, *, name):
    return _ew(_adamw_math, [w, g, m, v], (F32, F32, F32), name=name)


def _adamw_small(w, parts, m, v, *, name):
    n = parts.shape[0]

    def body(w_ref, p_ref, m_ref, v_ref, g_ref, d_ref, m2_ref, v2_ref):
        g = p_ref[0]
        for i in range(1, n):
            g = g + p_ref[i]
        g_ref[...] = g
        d_ref[...], m2_ref[...], v2_ref[...] = _adamw_math(w_ref[...], g, m_ref[...], v_ref[...])

    shp = jax.ShapeDtypeStruct(w.shape, F32)
    return pl.pallas_call(body, name=name, out_shape=[shp] * 4)(w, parts, m, v)


_ANY = pl.BlockSpec(memory_space=pl.ANY)


def _mesh_place():
    x, y, c = lax.axis_index("x"), lax.axis_index("y"), lax.axis_index("c")
    chips = [(1 - x, y), (x, 1 - y), (1 - x, 1 - y)]
    return x, y, c, chips


def _remote(src, dst, sems, i, to):
    send_sems, recv_sems = sems
    return pltpu.make_async_remote_copy(src_ref=src, dst_ref=dst, send_sem=send_sems.at[i], recv_sem=recv_sems.at[i],
                                        device_id=to, device_id_type=MESH_ID)


def _gather_weights(shards, *, name):
    n = len(shards)

    def body(*refs):
        ins, outs = refs[:n], refs[n:2 * n]
        sems = refs[2 * n:2 * n + 2]
        x, y, c, chips = _mesh_place()
        me = 2 * x + y
        sibling = (x, y, 1 - c)
        sent = []
        for w in range(n):
            for j, chip in enumerate(chips):
                cp = _remote(ins[w].at[c], outs[w].at[me, c], sems, 6 * w + j, (chip[0], chip[1], c))
                cp.start()
                sent.append(cp)
        for w in range(n):
            for j, chip in enumerate(chips):
                got = outs[w].at[2 * chip[0] + chip[1], c]
                _remote(got, got, sems, 6 * w + j, sibling).wait_recv()
                cp = _remote(got, got, sems, 6 * w + 3 + j, sibling)
                cp.start()
                sent.append(cp)
        for w in range(n):
            for j, chip in enumerate(chips):
                got = outs[w].at[2 * chip[0] + chip[1], 1 - c]
                _remote(got, got, sems, 6 * w + 3 + j, sibling).wait_recv()
        for cp in sent:
            cp.wait_send()

    outs = pl.pallas_call(
        body, name=name,
        in_specs=[_ANY] * n, out_specs=[_ANY] * n,
        out_shape=[jax.ShapeDtypeStruct((N_CHIPS,) + s.shape, s.dtype) for s in shards],
        scratch_shapes=[pltpu.SemaphoreType.DMA((6 * n,)), pltpu.SemaphoreType.DMA((6 * n,))],
    )(*shards)
    me = 2 * lax.axis_index("x") + lax.axis_index("y")
    return [lax.dynamic_update_index_in_dim(o, s, me, 0) for o, s in zip(outs, shards)]


def _forward_halves(gathered, *, name):
    n = len(gathered)

    def body(*refs):
        ins, outs = refs[:n], refs[n:2 * n]
        sems = refs[2 * n:2 * n + 2]
        x, y, c, chips = _mesh_place()
        sibling = (x, y, 1 - c)
        sent = []
        for w in range(n):
            for j, chip in enumerate(chips):
                peer = 2 * chip[0] + chip[1]
                cp = _remote(ins[w].at[peer, c], outs[w].at[peer, c], sems, 3 * w + j, sibling)
                cp.start()
                sent.append(cp)
        for w in range(n):
            for j, chip in enumerate(chips):
                land = outs[w].at[2 * chip[0] + chip[1], 1 - c]
                _remote(land, land, sems, 3 * w + j, sibling).wait_recv()
        for cp in sent:
            cp.wait_send()

    return pl.pallas_call(
        body, name=name,
        in_specs=[_ANY] * n, out_specs=[_ANY] * n,
        out_shape=[jax.ShapeDtypeStruct(g.shape, g.dtype) for g in gathered],
        input_output_aliases={w: w for w in range(n)},
        scratch_shapes=[pltpu.SemaphoreType.DMA((3 * n,)), pltpu.SemaphoreType.DMA((3 * n,))],
    )(*gathered)


def _exchange_siblings(grads, *, name):
    n = len(grads)

    def body(*refs):
        ins, got = refs[:n], refs[n:2 * n]
        sems = refs[2 * n:2 * n + 2]
        x, y, c, _ = _mesh_place()
        sibling = (x, y, 1 - c)
        sent = []
        for w in range(n):
            for s in range(N_CHIPS):
                cp = _remote(ins[w].at[s, 1 - c], got[w].at[s], sems, N_CHIPS * w + s, sibling)
                cp.start()
                sent.append(cp)
        for w in range(n):
            for s in range(N_CHIPS):
                _remote(got[w].at[s], got[w].at[s], sems, N_CHIPS * w + s, sibling).wait_recv()
        for cp in sent:
            cp.wait_send()

    n_sem = N_CHIPS * n
    return pl.pallas_call(
        body, name=name,
        in_specs=[_ANY] * n, out_specs=[_ANY] * n,
        out_shape=[jax.ShapeDtypeStruct((N_CHIPS,) + g.shape[2:], g.dtype) for g in grads],
        scratch_shapes=[pltpu.SemaphoreType.DMA((n_sem,)), pltpu.SemaphoreType.DMA((n_sem,))],
    )(*grads)


def _share_halves(halves, small):
    n = len(halves)

    def body(*refs):
        ins, small_ref = refs[:n], refs[n]
        outs, small_out = refs[n + 1:2 * n + 1], refs[2 * n + 1]
        sems = refs[2 * n + 2:2 * n + 4]
        x, y, c, chips = _mesh_place()
        sibling = (x, y, 1 - c)
        me = 4 * x + 2 * y + c
        sent = [_remote(ins[w], outs[w].at[c], sems, w, sibling) for w in range(n)]
        peers = [sibling] + [(ch[0], ch[1], cc) for ch in chips for cc in (c, 1 - c)]
        sent += [_remote(small_ref, small_out.at[me], sems, n + j, peer) for j, peer in enumerate(peers)]
        for cp in sent:
            cp.start()
        for w in range(n):
            _remote(outs[w].at[1 - c], outs[w].at[1 - c], sems, w, sibling).wait_recv()
        for j, peer in enumerate(peers):
            frm = small_out.at[4 * peer[0] + 2 * peer[1] + peer[2]]
            _remote(frm, frm, sems, n + j, peer).wait_recv()
        for cp in sent:
            cp.wait_send()

    n_sem = n + 7
    outs = pl.pallas_call(
        body, name="share_halves",
        in_specs=[_ANY] * (n + 1), out_specs=[_ANY] * (n + 1),
        out_shape=[jax.ShapeDtypeStruct((2,) + h.shape, h.dtype) for h in halves]
        + [jax.ShapeDtypeStruct((8,) + small.shape, small.dtype)],
        scratch_shapes=[pltpu.SemaphoreType.DMA((n_sem,)), pltpu.SemaphoreType.DMA((n_sem,))],
    )(*halves, small)
    c = lax.axis_index("c")
    me = 4 * lax.axis_index("x") + 2 * lax.axis_index("y") + c
    return ([lax.dynamic_update_index_in_dim(o, h, c, 0) for o, h in zip(outs[:n], halves)],
            lax.dynamic_update_index_in_dim(outs[n], small, me, 0))


EARLY = ("w_ff_down", "w_ff_up", "w_out", "w_branch_sb", "w_branch_fox", "w_branch_mem", "w_mem_kv")


def _norm_bwd_tail(dy, x, add, g):
    r = lax.rsqrt(jnp.mean(x * x, axis=-1, keepdims=True) + EPS)
    xh = x * r
    dyg = dy * g
    c = jnp.mean(dyg * xh, axis=-1, keepdims=True)
    return r * (dyg - xh * c) + add, jnp.sum(dy * xh, axis=0, keepdims=True)


def _split(outs, n):
    outs = list(outs) if isinstance(outs, (list, tuple)) else [outs]
    return outs[:n], outs[n:]


def _local_step(x, mem, target, small, W, gather_rest=None, reduce_early=None, reduce_late=None):
    S, D = x.shape
    o_qkv, o_mq, o_f = 3 * D, 3 * D + 2 * 3 * D_SB, 3 * D + 2 * 3 * D_SB + D_MEM
    tq = 512

    g_comms, finish_weights = gather_rest if gather_rest is not None else ([None] * 3, None)
    (proj, h), landed = _split(_mm(x, W["w_in"], name="in_proj", tb=True, tn=768, a_gain=small["g_mix_norm"],
                                   comm=g_comms[0]), 2)
    blk = lambda j: (proj, (o_qkv + j * D_SB) // LANES)
    sb_q, sb_k, sb_v, fx_q, fx_k, fx_v = [blk(j) for j in range(6)]
    m_q = (proj, o_mq // LANES)
    f_logit_t = _mm(W["w_in"][o_f:o_f + ROW_TILE], h, name="forget_logits", tb=True)[:FOX_HEADS]
    b_col = small["b_forget"].reshape(FOX_HEADS, 1)
    lanes = lambda g: jnp.tile(g, (1, LANES // g.shape[1]))
    g_fq, g_fk, g_mq, g_mk = [lanes(small[k]) for k in ("g_fox_q", "g_fox_k", "g_mem_q", "g_mem_k")]

    (o_sb, sb_tot), more = _split(_sbl_fwd(sb_q, sb_k, sb_v, width=D_SB, hd=HD, name="sb_fwd", tq=tq, comm=g_comms[1]), 2)
    landed += more

    fq = _hnorm_fwd(fx_q, g_fq, width=D_FOX, hd=HD, name="fox_q_norm")
    fk = _hnorm_fwd(fx_k, g_fk, width=D_FOX, hd=HD, name="fox_k_norm")
    f_cum = _forget_fwd(f_logit_t, b_col, name="forget_fwd")
    tkf = _tile(S, tq)
    f_bias = (f_cum.reshape(FOX_HEADS, S, 1), f_cum.reshape(FOX_HEADS, S // tkf, 1, tkf))
    (o_fox, fox_lse), more = _split(_sml_fwd((fq, 0), (fk, 0), fx_v, f_bias, width=D_FOX, hd=HD, causal=True,
                                             name="fox_fwd", tq=tq, tk=tq, comm=g_comms[2]), 2)
    landed += more
    if finish_weights is not None:
        W = {**W, **finish_weights(landed)}

    mh = _rmsnorm_fwd(mem, small["g_mem_norm"], BF16, name="mem_norm")
    mkv = _mm(mh, W["w_mem_kv"], name="mem_kv")
    mv = (mkv, D_MEM // LANES)
    mq = _hnorm_fwd(m_q, g_mq, width=D_MEM, hd=MEM_HD, name="mem_q_norm")
    mk = _hnorm_fwd((mkv, 0), g_mk, width=D_MEM, hd=MEM_HD, name="mem_k_norm")
    o_mem, mem_lse = _sml_fwd((mq, 0), (mk, 0), mv, width=D_MEM, hd=MEM_HD, causal=False, name="mem_fwd", tq=tq, tk=256)

    o3 = [o_sb, o_fox, o_mem]
    w3 = [W["w_branch_sb"], W["w_branch_fox"], W["w_branch_mem"]]
    merged, x1, h2 = _gate_fwd(o3, w3, proj, x, W["w_out"], small["g_mlp_norm"], D, name="gate_fwd")

    def relu2(acc):
        u = jnp.maximum(acc, 0.0)
        return u, u * u

    u, a = _mm(h2, W["w_ff_up"], name="ff_up", out_dtypes=(BF16, BF16), epilogue=relu2)
    def head(acc, res, tgt):
        d = (res + acc - tgt) * (1.0 / D)
        return d, d, jnp.sum(d * d, axis=0, keepdims=True)

    dy, dy16, sq_rows = _mm(a, W["w_ff_down"], name="ff_down", extras=(x1, target), out_dtypes=(F32, BF16),
                            epilogue=head, col_sums=1, tn=512)
    loss = (0.5 * D) * jnp.sum(sq_rows)

    G = {}
    du = _mm(dy16, W["w_ff_down"], name="d_ff_act", tb=True, out_dtypes=(BF16,), extras=(u,),
             epilogue=lambda acc, uu: (acc * (2.0 * uu.astype(F32)),))
    G["w_ff_down"] = _mm(a, dy16, name="d_w_ff_down", ta=True, out_dtypes=(BF16,))
    G["w_ff_up"] = _mm(h2, du, name="d_w_ff_up", ta=True, out_dtypes=(BF16,))
    dx1, dg_rows = _mm(du, W["w_ff_up"], name="d_mlp_in", tb=True, extras=(x1, dy, small["g_mlp_norm"]),
                       epilogue=_norm_bwd_tail, col_sums=1, tm=512, tn=D)
    dg_mlp = jnp.sum(dg_rows, axis=0, keepdims=True)
    G["w_out"] = _mm(merged, dx1, name="d_w_out", ta=True, out_dtypes=(BF16,))
    dgate, dw0, dw1, dw2, do_sb, do_fox, do_mem = _gate_bwd(o3, w3, proj, dx1, W["w_out"], D, name="gate_bwd")
    for nm, dw in zip(("w_branch_sb", "w_branch_fox", "w_branch_mem"), (dw0, dw1, dw2)):
        G[nm] = dw.astype(BF16)

    sib_comm, after_siblings = (reduce_early({k: G.pop(k) for k in EARLY if k != "w_mem_kv"})
                                if reduce_early is not None else (None, None))
    (dmq_n, dmk_n, dmv), landed_sib = _split(
        _sml_bwd((mq, 0), (mk, 0), mv, (o_mem, 0), (mem_lse, 0), (do_mem, 0), width=D_MEM, hd=MEM_HD, causal=False,
                 name="mem_bwd", tq=tq, tk=256, comm=sib_comm), 3)
    dm_q, dg_mem_q = _hnorm_bwd(m_q, g_mq, dmq_n, width=D_MEM, hd=MEM_HD, name="d_mem_q_norm")
    dmk_raw, dg_mem_k = _hnorm_bwd((mkv, 0), g_mk, dmk_n, width=D_MEM, hd=MEM_HD, name="d_mem_k_norm")
    dmkv = jnp.concatenate([dmk_raw, dmv.astype(BF16)], axis=1)
    G["w_mem_kv"] = _mm(mh, dmkv, name="d_w_mem_kv", ta=True, out_dtypes=(BF16,))
    dmh = _mm(dmkv, W["w_mem_kv"], name="d_mem_h", tb=True)
    _, dg_mem = _rmsnorm_bwd(mem, small["g_mem_norm"], dmh, name="d_mem_norm")

    r_comms, r_finish = (after_siblings(landed_sib, {"w_mem_kv": G.pop("w_mem_kv")})
                         if after_siblings is not None else ([None] * 2, None))
    dsb, landed_sb = _split(_sbl_bwd(sb_q, sb_k, sb_v, (do_sb, 0), (sb_tot, 0), width=D_SB, hd=HD, name="sb_bwd", tq=tq,
                                     comm=r_comms[0]), 3)
    (dfq, dfk, dfv, df_row, df_col), landed_fox = _split(
        _sml_bwd((fq, 0), (fk, 0), fx_v, (o_fox, 0), (fox_lse, 0), (do_fox, 0), f_bias, width=D_FOX, hd=HD, causal=True,
                 name="fox_bwd", tq=tq, tk=tq, comm=r_comms[1]), 5)
    early = r_finish(landed_sb, landed_fox) if r_finish is not None else {}
    dfx_q, dg_fox_q = _hnorm_bwd(fx_q, g_fq, dfq, width=D_FOX, hd=HD, name="d_fox_q_norm")
    dfx_k, dg_fox_k = _hnorm_bwd(fx_k, g_fk, dfk, width=D_FOX, hd=HD, name="d_fox_k_norm")
    d_fcum = df_row.reshape(FOX_HEADS, S) - df_col.reshape(FOX_HEADS, S)
    d_flogit_t, db_forget = _forget_bwd(f_logit_t, b_col, d_fcum, name="forget_bwd")
    dg_fox_q, dg_fox_k = dg_fox_q[:, :HD], dg_fox_k[:, :HD]

    rest_cols = jnp.concatenate([t.astype(BF16) for t in (*dsb, dfx_q, dfx_k, dfv, dm_q)]
                                + [d_flogit_t.T.astype(BF16), jnp.zeros((S, F_PAD - FOX_HEADS), BF16)], axis=1)
    dproj = lax.dynamic_update_slice(dgate, rest_cols, (0, 3 * D))
    g_w_in = _mm(dproj, h, name="d_w_in", ta=True, out_dtypes=(BF16,), tm=768)
    comm, finish = reduce_late({"w_in": g_w_in}) if reduce_late is not None else (None, None)
    (grad_x, dg_rows), landed = _split(
        _mm(dproj, W["w_in"], name="d_mix_in", tk=2304, tm=512, tn=D, extras=(x, dx1, small["g_mix_norm"]),
            epilogue=_norm_bwd_tail, col_sums=1, comm=comm), 2)
    dg_mix = jnp.sum(dg_rows, axis=0, keepdims=True)
    if finish is None:
        G["w_in"] = g_w_in
    else:
        early.update(finish(landed))

    small_grads = dict(g_mix_norm=dg_mix, g_mem_norm=dg_mem, b_forget=db_forget.reshape(1, FOX_HEADS),
                       g_fox_q=dg_fox_q, g_fox_k=dg_fox_k, g_mem_q=dg_mem_q, g_mem_k=dg_mem_k, g_mlp_norm=dg_mlp)
    return loss, grad_x, G, small_grads, early


BIG = ("w_in", "w_mem_kv", "w_branch_sb", "w_branch_fox", "w_branch_mem", "w_out", "w_ff_up", "w_ff_down")
COLUMN_SHARDED = ("w_in", "w_branch_sb", "w_branch_fox", "w_branch_mem", "w_ff_up")
SMALL = ("g_mix_norm", "g_mem_norm", "b_forget", "g_fox_q", "g_fox_k", "g_mem_q", "g_mem_k", "g_mlp_norm")
ORDER = ("g_mix_norm", "g_mem_norm", "w_in", "b_forget", "g_fox_q", "g_fox_k", "g_mem_q", "g_mem_k", "w_mem_kv",
         "w_branch_sb", "w_branch_fox", "w_branch_mem", "w_out", "g_mlp_norm", "w_ff_up", "w_ff_down")


def _unshard(name, gathered):
    n, _, rh, c = gathered.shape
    t = gathered.reshape(n, 2 * rh, c)
    if name in COLUMN_SHARDED:
        return t.transpose(1, 0, 2).reshape(2 * rh, n * c)
    return t.reshape(n * 2 * rh, c)


def _reshard(name, full):
    if name in COLUMN_SHARDED:
        r, c = full.shape
        t = full.reshape(r, N_CHIPS, c // N_CHIPS).transpose(1, 0, 2)
    else:
        r, c = full.shape[0] // N_CHIPS, full.shape[1]
        t = full.reshape(N_CHIPS, r, c)
    return t.reshape(N_CHIPS, 2, t.shape[1] // 2, t.shape[2])


ROW_TILE = 16
IN_BUF_ALIGN = 256


def _in_segments(D):
    n_qkv = 6 * D_SB
    o_mq, o_gate = n_qkv + FOX_HEADS, n_qkv + FOX_HEADS + D_MEM
    return [(0, n_qkv, 3 * D), (n_qkv, o_mq, 3 * D + n_qkv + D_MEM), (o_mq, o_gate, 3 * D + n_qkv), (o_gate, o_gate + 3 * D, 0)]


class _InLayout:
    def __init__(self, D, shard, n):
        self.D, self.shard, self.n = D, shard, n
        down = lambda v: v // ROW_TILE * ROW_TILE
        up = lambda v: -(-v // ROW_TILE) * ROW_TILE
        self.pieces = []
        ends = []
        for s in range(n):
            cursor, mine = 0, []
            for a, b, p in _in_segments(D):
                x0, x1 = max(a, s * shard), min(b, (s + 1) * shard)
                if x0 < x1:
                    p0 = p + x0 - a
                    rows = up(p0 + x1 - x0) - down(p0)
                    mine.append((x0 - s * shard, x1 - x0, p0, cursor, rows))
                    cursor += rows
            self.pieces.append(mine)
            ends.append(cursor)
        self.rows = -(-max(ends) // IN_BUF_ALIGN) * IN_BUF_ALIGN
        self.padded_rows = 3 * D + 6 * D_SB + D_MEM + F_PAD

    def _per_shard(self, fn, chip, operand):
        return lax.switch(chip, [functools.partial(fn, s) for s in range(self.n)], operand)

    def pack(self, chip, rows):
        def one(s, t):
            out, at = [], 0
            for x0, n_rows, p0, start, region in self.pieces[s]:
                lead = p0 % ROW_TILE
                out += [jnp.zeros((start + lead - at, t.shape[1]), t.dtype), t[x0:x0 + n_rows]]
                at = start + lead + n_rows
            return jnp.concatenate(out + [jnp.zeros((self.rows - at, t.shape[1]), t.dtype)], axis=0)
        return self._per_shard(one, chip, rows)

    def unpack(self, chip, buf, pad_to):
        def one(s, t):
            out = [t[start + p0 % ROW_TILE:start + p0 % ROW_TILE + n_rows] for _, n_rows, p0, start, _ in self.pieces[s]]
            return jnp.concatenate(out + [jnp.zeros((pad_to - self.shard, t.shape[1]), t.dtype)], axis=0)
        return self._per_shard(one, chip, buf)

    def to_padded(self, bufs):
        runs = sorted((p0, s, start, region) for s in range(self.n) for _, _, p0, start, region in self.pieces[s])
        chunks, end = [], 0
        for p0, s, start, region in runs:
            d0 = p0 // ROW_TILE * ROW_TILE
            src = bufs[s, start:start + region]
            if d0 < end:
                assert end - d0 == ROW_TILE
                last = chunks.pop()
                chunks += [last[:-ROW_TILE], last[-ROW_TILE:] + src[:ROW_TILE], src[ROW_TILE:]]
            else:
                if d0 > end:
                    chunks.append(jnp.zeros((d0 - end, bufs.shape[2]), bufs.dtype))
                chunks.append(src)
            end = d0 + region
        chunks.append(jnp.zeros((self.padded_rows - end, bufs.shape[2]), bufs.dtype))
        return jnp.concatenate(chunks, axis=0)

    def from_padded(self, gp):
        bufs = []
        for s in range(self.n):
            out, at = [], 0
            for _, n_rows, p0, start, region in self.pieces[s]:
                d0 = p0 // ROW_TILE * ROW_TILE
                row = d0 + lax.broadcasted_iota(jnp.int32, (region, 1), 0)
                out.append(jnp.where((row >= p0) & (row < p0 + n_rows), gp[d0:d0 + region], jnp.zeros((), gp.dtype)))
                at = start + region
            bufs.append(jnp.concatenate(out + [jnp.zeros((self.rows - at, gp.shape[1]), gp.dtype)], axis=0))
        return jnp.stack(bufs)


SMALL_ROWS = 16


def _pack_small(vals, scalar=None):
    width = max(vals[k].shape[1] for k in SMALL)
    rows = [jnp.pad(vals[k].astype(F32), ((0, 0), (0, width - vals[k].shape[1]))) for k in SMALL]
    extra = jnp.zeros((SMALL_ROWS - len(SMALL), width), F32)
    if scalar is not None:
        extra = extra.at[0, 0].set(scalar)
    return jnp.concatenate(rows + [extra], axis=0)


def _unpack_small(packed, like):
    return {k: packed[i:i + 1, :like[k].shape[1]] for i, k in enumerate(SMALL)}


def kernel(x, mem, g_mix_norm, g_mem_norm, w_in, b_forget, g_fox_q, g_fox_k, g_mem_q, g_mem_k, w_mem_kv, w_branch_sb, w_branch_fox, w_branch_mem, w_out, g_mlp_norm, w_ff_up, w_ff_down, loss_target, m_g_mix_norm, m_g_mem_norm, m_w_in, m_b_forget, m_g_fox_q, m_g_fox_k, m_g_mem_q, m_g_mem_k, m_w_mem_kv, m_w_branch_sb, m_w_branch_fox, m_w_branch_mem, m_w_out, m_g_mlp_norm, m_w_ff_up, m_w_ff_down, v_g_mix_norm, v_g_mem_norm, v_w_in, v_b_forget, v_g_fox_q, v_g_fox_k, v_g_mem_q, v_g_mem_k, v_w_mem_kv, v_w_branch_sb, v_w_branch_fox, v_w_branch_mem, v_w_out, v_g_mlp_norm, v_w_ff_up, v_w_ff_down):
    given = dict(locals())
    D = x.shape[-1]
    weights = {k: given[k] for k in ORDER}
    moms = {k: given["m_" + k] for k in ORDER}
    vars_ = {k: given["v_" + k] for k in ORDER}

    me_chip = 2 * lax.axis_index("x") + lax.axis_index("y")

    n_in = w_in.shape[2]
    lay = _InLayout(D, n_in, N_CHIPS)
    transposed = lambda t: jnp.transpose(t[0])
    shards = {}
    for k in BIG:
        w = weights[k][0].astype(BF16)
        if k == "w_in":
            w = lay.pack(me_chip, jnp.transpose(w))
        shards[k] = w.reshape(2, w.shape[0] // 2, w.shape[1])
    gathered_in = _gather_weights([shards["w_in"]], name="gather_w_in")[0]
    W = {"w_in": lay.to_padded(gathered_in.reshape(N_CHIPS, lay.rows, D))}
    carried = (("w_branch_sb", "w_branch_fox", "w_branch_mem", "w_out"), ("w_ff_up", "w_mem_kv"), ("w_ff_down",))
    rest = [k for grp in carried for k in grp]
    assert sorted(rest + ["w_in"]) == sorted(BIG)

    def finish_weights(landed):
        full = _forward_halves(landed, name="forward_halves")
        full = [lax.dynamic_update_index_in_dim(o, shards[k], me_chip, 0) for k, o in zip(rest, full)]
        return {k: _unshard(k, g) for k, g in zip(rest, full)}

    me_core = lax.axis_index("c")

    def sum_chips(names, parts, got):
        return {k: _chip_sum(p, q, me_chip, name="sum_chips_" + k) for k, p, q in zip(names, parts, got)}

    def pair_sums(grads, tag):
        names = list(grads)
        stacked = {k: _reshard(k, grads[k]) for k in names if k != "w_in"}
        if "w_in" in grads:
            stacked["w_in"] = lay.from_padded(grads["w_in"]).reshape(N_CHIPS, 2, lay.rows // 2, D)
        got = _exchange_siblings([stacked[k] for k in names], name="exchange_siblings_" + tag)
        return {k: _pair_sum(stacked[k], q, me_core, name="sum_pair_" + k) for k, q in zip(names, got)}

    def reduce_early(grads):
        names = list(grads)
        stacked = [_reshard(k, grads[k]) for k in names]

        def after_siblings(landed, more):
            parts = {k: _pair_sum(t, q, me_core, name="sum_pair_" + k) for k, t, q in zip(names, stacked, landed)}
            parts.update(pair_sums(more, "early"))
            groups = [[k for k in parts if k in ("w_ff_down", "w_ff_up")], [k for k in parts if k not in ("w_ff_down", "w_ff_up")]]
            comms = [_ChipExchange("scatter", [parts[k] for k in grp]) for grp in groups]

            def finish(*got):
                out = {}
                for grp, q in zip(groups, got):
                    out.update(sum_chips(grp, [parts[k] for k in grp], q))
                return out
            return comms, finish
        return _ChipExchange("siblings", stacked), after_siblings

    def reduce_late(grads):
        parts = pair_sums(grads, "late")
        names = list(parts)
        return _ChipExchange("scatter", [parts[k] for k in names]), functools.partial(sum_chips, names, [parts[k] for k in names])

    small = {k: weights[k] for k in SMALL}
    loss_part, grad_x, G, small_grads, halves = _local_step(
        x[0], mem[0], loss_target[0], small, W,
        gather_rest=([_ChipExchange("gather", [shards[k] for k in grp]) for grp in carried], finish_weights),
        reduce_early=reduce_early, reduce_late=reduce_late)
    assert not G, list(G)
    reduced, small_parts = _share_halves([halves[k] for k in BIG], _pack_small(small_grads, loss_part))

    grads, deltas, new_m, new_v = {}, {}, {}, {}
    for k, g in zip(BIG, reduced):
        shp = weights[k].shape
        if k == "w_in":
            g2 = lay.unpack(me_chip, g.reshape(lay.rows, D), lay.rows)
            padded = lambda t: jnp.pad(transposed(t), ((0, lay.rows - n_in), (0, 0)))
            outs = _adamw(padded(weights[k]), g2, padded(moms[k]), padded(vars_[k]), name="adamw_" + k)
            g2, d, m2, v2 = [jnp.transpose(t[:n_in]) for t in (g2, *outs)]
        else:
            g2 = g.reshape(shp[1], shp[2])
            d, m2, v2 = _adamw(weights[k][0], g2, moms[k][0], vars_[k][0], name="adamw_" + k)
        grads[k], deltas[k], new_m[k], new_v[k] = g2.reshape(shp), d.reshape(shp), m2.reshape(shp), v2.reshape(shp)
    sg, sd, sm, sv = _adamw_small(_pack_small(small), small_parts, _pack_small({k: moms[k] for k in SMALL}),
                                  _pack_small({k: vars_[k] for k in SMALL}), name="adamw_small")
    for dst, packed in ((grads, sg), (deltas, sd), (new_m, sm), (new_v, sv)):
        dst.update(_unpack_small(packed, small))

    loss = sg[len(SMALL), 0]
    return (loss, grad_x[None], *[grads[k] for k in ORDER], *[deltas[k] for k in ORDER],
            *[new_m[k] for k in ORDER], *[new_v[k] for k in ORDER])
```

```python
import functools

import jax
import jax.numpy as jnp
from jax import lax
from jax.experimental import pallas as pl
from jax.experimental.pallas import tpu as pltpu

F32 = jnp.float32
BF16 = jnp.bfloat16
MESH_ID = pl.DeviceIdType.MESH

HD = 64
SB_HEADS = 8
FOX_HEADS = 8
MEM_HEADS = 4
MEM_HD = 128
D_SB = SB_HEADS * HD
D_FOX = FOX_HEADS * HD
D_MEM = MEM_HEADS * MEM_HD
EPS = 1e-6
NEG_INF = -1e30

ADAM_LR = 0.001
ADAM_B1 = 0.9
ADAM_B2 = 0.999
ADAM_EPS = 1e-08
ADAM_WD = 0.01
ADAM_STEP = 10

N_CHIPS = 4
VMEM_LIMIT = 56 * 1024 * 1024

F_PAD = 256


def _tile(n, target, align=128):
    if n <= target:
        return n
    best = None
    t = align
    while t <= target:
        if n % t == 0:
            best = t
        t += align
    assert best is not None, (n, target, align)
    return best


def _params(sem):
    return pltpu.CompilerParams(dimension_semantics=sem, vmem_limit_bytes=VMEM_LIMIT)


def _mm(a, b, *, name, ta=False, tb=False, out_dtypes=(F32,), epilogue=None, extras=(),
        tm=1024, tn=1024, tk=2048, comm=None, col_sums=0, a_gain=None):
    if ta:
        K, M = a.shape
    else:
        M, K = a.shape
    if tb:
        N, K2 = b.shape
    else:
        K2, N = b.shape
    assert K == K2, (a.shape, b.shape, ta, tb)
    tm, tn, tk = _tile(M, tm), _tile(N, tn), _tile(K, tk)
    nk = K // tk
    normed = a_gain is not None
    assert not normed or (nk == 1 and not ta)
    n_extra, n_out = len(extras) + normed, len(out_dtypes) + col_sums + normed
    if epilogue is None:
        epilogue = lambda acc: (acc,)
    dims = (((0 if ta else 1,), (1 if tb else 0,)), ((), ()))

    gm, gn = M // tm, N // tn

    def body(*refs):
        i, j, k = pl.program_id(0), pl.program_id(1), pl.program_id(2)
        got = _carry(comm, 2 + n_extra, n_out, (i == 0) & (j == 0) & (k == 0),
                     (i == gm - 1) & (j == gn - 1) & (k == nk - 1), refs)
        (a_ref, b_ref, *extra_refs), out_refs = got[0], list(got[1])
        if normed:
            gain_ref, normed_ref = extra_refs.pop(), out_refs.pop()
            av = a_ref[...].astype(F32)
            av = (av * lax.rsqrt(jnp.mean(av * av, axis=-1, keepdims=True) + EPS) * gain_ref[...]).astype(BF16)

            @pl.when(j == 0)
            def _():
                normed_ref[...] = av
        else:
            av = a_ref[...].astype(BF16)
        part = lax.dot_general(av, b_ref[...].astype(BF16), dims, preferred_element_type=F32)

        def finish(acc):
            outs = epilogue(acc, *[r[...] for r in extra_refs])
            for o_ref, o in zip(out_refs[:len(out_dtypes)], outs):
                o_ref[...] = o.astype(o_ref.dtype)
            for o_ref, o in zip(out_refs[len(out_dtypes):], outs[len(out_dtypes):]):
                first = lax.broadcasted_iota(jnp.int32, o_ref.shape, 0) == 0
                o_ref[...] = jnp.where(first, jnp.broadcast_to(o, o_ref.shape), 0.0)

        if nk == 1:
            finish(part)
        else:
            acc_ref = refs[-1]

            @pl.when(k == 0)
            def _():
                acc_ref[...] = part

            @pl.when((k > 0) & (k < nk - 1))
            def _():
                acc_ref[...] += part

            @pl.when(k == nk - 1)
            def _():
                finish(acc_ref[...] + part)

        got[2]()

    a_spec = pl.BlockSpec((tk, tm), lambda i, j, k: (k, i)) if ta else pl.BlockSpec((tm, tk), lambda i, j, k: (i, k))
    b_spec = pl.BlockSpec((tn, tk), lambda i, j, k: (j, k)) if tb else pl.BlockSpec((tk, tn), lambda i, j, k: (k, j))
    mn_spec = pl.BlockSpec((tm, tn), lambda i, j, k: (i, j))
    row_spec = pl.BlockSpec((1, tn), lambda i, j, k: (0, j))
    sum_spec = pl.BlockSpec((8, tn), lambda i, j, k: (i, j))
    c_ins, c_in_specs, c_out_specs, c_out_shape, c_scratch = _comm_args(comm)
    sem = ("parallel", "arbitrary" if normed else "parallel", "arbitrary") if comm is None else ("arbitrary",) * 3
    outs = pl.pallas_call(
        body, name=name,
        grid=(gm, gn, nk),
        in_specs=[a_spec, b_spec] + [row_spec if e.shape[0] == 1 else mn_spec for e in extras]
        + ([pl.BlockSpec((1, K), lambda i, j, k: (0, 0))] if normed else []) + c_in_specs,
        out_specs=[mn_spec] * len(out_dtypes) + [sum_spec] * col_sums
        + ([pl.BlockSpec((tm, K), lambda i, j, k: (i, 0))] if normed else []) + c_out_specs,
        out_shape=[jax.ShapeDtypeStruct((M, N), dt) for dt in out_dtypes]
        + [jax.ShapeDtypeStruct((8 * gm, N), F32)] * col_sums
        + ([jax.ShapeDtypeStruct((M, K), BF16)] if normed else []) + c_out_shape,
        scratch_shapes=c_scratch + ([pltpu.VMEM((tm, tn), F32)] if nk > 1 else []),
        compiler_params=_params(sem),
    )(a, b, *extras, *([a_gain] if normed else []), *c_ins)
    return outs if len(outs) > 1 else outs[0]


def _row_tile(rows, cols, n_arrays):
    budget = 10 * 1024 * 1024
    cols_padded = -(-cols // 128) * 128
    target = max(16, budget // (cols_padded * 4 * n_arrays * 2))
    return _tile(rows, target, align=16)


def _ew(fn, ins, out_dtypes, *, name):
    R, C = ins[0].shape
    n_in, n_out = len(ins), len(out_dtypes)
    tr = _row_tile(R, C, n_in + n_out)

    def body(*refs):
        outs = fn(*[r[...] for r in refs[:n_in]])
        for o_ref, o in zip(refs[n_in:], outs):
            o_ref[...] = o.astype(o_ref.dtype)

    spec = pl.BlockSpec((tr, C), lambda i: (i, 0))
    outs = pl.pallas_call(
        body, name=name, grid=(R // tr,),
        in_specs=[spec] * n_in, out_specs=[spec] * n_out,
        out_shape=[jax.ShapeDtypeStruct((R, C), dt) for dt in out_dtypes],
        compiler_params=_params(("parallel",)),
    )(*ins)
    return outs if n_out > 1 else outs[0]


def _rmsnorm_fwd(x, g, out_dtype, *, name):
    R, d = x.shape
    tr = _row_tile(R, d, 3)

    def body(x_ref, g_ref, o_ref):
        xv = x_ref[...].astype(F32)
        r = lax.rsqrt(jnp.mean(xv * xv, axis=-1, keepdims=True) + EPS)
        o_ref[...] = (xv * r * g_ref[...]).astype(o_ref.dtype)

    return pl.pallas_call(
        body, name=name, grid=(R // tr,),
        in_specs=[pl.BlockSpec((tr, d), lambda i: (i, 0)), pl.BlockSpec((1, d), lambda i: (0, 0))],
        out_specs=pl.BlockSpec((tr, d), lambda i: (i, 0)),
        out_shape=jax.ShapeDtypeStruct((R, d), out_dtype),
        compiler_params=_params(("parallel",)),
    )(x, g)


def _rmsnorm_bwd(x, g, dy, add=None, *, name):
    R, d = x.shape
    has_add = add is not None
    tr = _row_tile(R, d, 5)

    def body(*refs):
        x_ref, g_ref, dy_ref = refs[:3]
        add_ref = refs[3] if has_add else None
        dx_ref, dg_ref = refs[-2:]
        xv = x_ref[...].astype(F32)
        dyv = dy_ref[...].astype(F32)
        r = lax.rsqrt(jnp.mean(xv * xv, axis=-1, keepdims=True) + EPS)
        xh = xv * r
        dyg = dyv * g_ref[...]
        c = jnp.mean(dyg * xh, axis=-1, keepdims=True)
        dx = r * (dyg - xh * c)
        if has_add:
            dx = dx + add_ref[...]
        dx_ref[...] = dx

        @pl.when(pl.program_id(0) == 0)
        def _():
            dg_ref[...] = jnp.zeros_like(dg_ref)

        dg_ref[...] += jnp.sum(dyv * xh, axis=0, keepdims=True)

    row = pl.BlockSpec((tr, d), lambda i: (i, 0))
    vec = pl.BlockSpec((1, d), lambda i: (0, 0))
    ins = [x, g, dy] + ([add] if has_add else [])
    return pl.pallas_call(
        body, name=name, grid=(R // tr,),
        in_specs=[row, vec, row] + ([row] if has_add else []),
        out_specs=[row, vec],
        out_shape=[jax.ShapeDtypeStruct((R, d), F32), jax.ShapeDtypeStruct((1, d), F32)],
        compiler_params=_params(("arbitrary",)),
    )(*ins)


_NT = (((1,), (1,)), ((), ()))
_TN = (((0,), (0,)), ((), ()))


def _dot(a, b, dims=(((1,), (0,)), ((), ()))):
    return lax.dot_general(a, b, dims, preferred_element_type=F32)


def _log_sigmoid_pair(z):
    sp = jnp.log(1.0 + jnp.exp(-jnp.abs(z)))
    return jnp.minimum(z, 0.0) - sp, jnp.minimum(-z, 0.0) - sp


LANES = 128
_LOW = -3e38


def _lane_masks(hd, rows):
    if hd == LANES:
        return [None]
    lane = lax.broadcasted_iota(jnp.int32, (rows, LANES), 1)
    return [(lane >= hh * hd) & (lane < (hh + 1) * hd) for hh in range(LANES // hd)]


def _keep(t, m):
    return t if m is None else jnp.where(m, t, 0.0)


def _merge(parts, masks):
    out = parts[-1]
    for p, m in zip(parts[-2::-1], masks[-2::-1]):
        out = jnp.where(m, p, out)
    return out


def _row_value(t, m):
    return jnp.max(t if m is None else jnp.where(m, t, _LOW), axis=1, keepdims=True)


def _cols(tq, off):
    return pl.BlockSpec((tq, LANES), lambda g, i: (i, off + g))


def _cols_all(rows, off):
    return pl.BlockSpec((rows, LANES), lambda g, i: (0, off + g))


SCAN_BLOCK = 256


def _tri(kind, cols):
    n = min(SCAN_BLOCK, cols)
    r = lax.broadcasted_iota(jnp.int32, (n, n), 0)
    c = lax.broadcasted_iota(jnp.int32, (n, n), 1)
    return ((r > c) if kind == "after" else (r < c)).astype(BF16)


def _scan_cols(x, tri, reverse):
    cols = x.shape[1]
    cb = min(SCAN_BLOCK, cols)
    assert cols % cb == 0 and tri.shape == (cb, cb)
    nb = cols // cb
    blocks = [x[:, b * cb:(b + 1) * cb] for b in range(nb)]
    outs, carry = [None] * nb, None
    for b in (reversed(range(nb)) if reverse else range(nb)):
        y = _dot(blocks[b].astype(BF16), tri)
        outs[b] = y if carry is None else y + carry
        s = jnp.sum(blocks[b], axis=1, keepdims=True)
        carry = s if carry is None else carry + s
    return (outs[0] if nb == 1 else jnp.concatenate(outs, axis=1)), carry


def _softplus_parts(z):
    pos = jnp.maximum(z, 0.0) + jnp.log(1.0 + jnp.exp(-jnp.abs(z)))
    return pos, z - pos


class _ChipExchange:
    def __init__(self, kind, ins):
        assert kind in ("gather", "scatter", "siblings", "forward")
        self.kind, self.ins = kind, list(ins)
        lead = {"gather": lambda s: (N_CHIPS,) + s, "scatter": lambda s: (3,) + s[1:], "siblings": lambda s: s[:1] + s[2:],
                "forward": lambda s: s}[kind]
        self.out_shape = [jax.ShapeDtypeStruct(lead(a.shape), a.dtype) for a in ins]
        self.aliases = [(w, w) for w in range(len(ins))] if kind == "forward" else []
        n = (N_CHIPS if kind == "siblings" else 3) * len(ins)
        self.scratch = [pltpu.SemaphoreType.DMA((n,)), pltpu.SemaphoreType.DMA((n,))]

    def _copies(self, in_refs, out_refs, sems, landing):
        x, y, c, chips = _mesh_place()
        me = 2 * x + y
        out = []
        for w in range(len(self.ins)):
            if self.kind == "siblings":
                for s in range(N_CHIPS):
                    land = out_refs[w].at[s]
                    src = land if landing else in_refs[w].at[s, 1 - c]
                    out.append(_remote(src, land, sems, N_CHIPS * w + s, (x, y, 1 - c)))
                continue
            for j, chip in enumerate(chips):
                peer = 2 * chip[0] + chip[1]
                to = (chip[0], chip[1], c)
                if self.kind == "gather":
                    src, dst, land = in_refs[w].at[c], out_refs[w].at[me, c], out_refs[w].at[peer, c]
                elif self.kind == "scatter":
                    src, dst, land = in_refs[w].at[peer], out_refs[w].at[j], out_refs[w].at[j]
                else:
                    src, dst, land, to = in_refs[w].at[peer, c], out_refs[w].at[peer, c], out_refs[w].at[peer, 1 - c], (x, y, 1 - c)
                if landing:
                    src, dst = land, land
                out.append(_remote(src, dst, sems, 3 * w + j, to))
        return out

    def start(self, in_refs, out_refs, sems):
        for cp in self._copies(in_refs, out_refs, sems, False):
            cp.start()

    def finish(self, in_refs, out_refs, sems):
        for cp in self._copies(in_refs, out_refs, sems, True):
            cp.wait_recv()
        for cp in self._copies(in_refs, out_refs, sems, False):
            cp.wait_send()


class _Joint:
    def __init__(self, parts):
        self.parts = list(parts)
        self.ins = [a for p in self.parts for a in p.ins]
        self.out_shape = [s for p in self.parts for s in p.out_shape]
        self.scratch = [s for p in self.parts for s in p.scratch]
        self.aliases, n_in, n_out = [], 0, 0
        for p in self.parts:
            self.aliases += [(n_in + i, n_out + o) for i, o in p.aliases]
            n_in, n_out = n_in + len(p.ins), n_out + len(p.out_shape)

    def _each(self, step, in_refs, out_refs, sems):
        i = o = s = 0
        for p in self.parts:
            a, b, c = len(p.ins), len(p.out_shape), len(p.scratch)
            getattr(p, step)(in_refs[i:i + a], out_refs[o:o + b], sems[s:s + c])
            i, o, s = i + a, o + b, s + c

    def start(self, in_refs, out_refs, sems):
        self._each("start", in_refs, out_refs, sems)

    def finish(self, in_refs, out_refs, sems):
        self._each("finish", in_refs, out_refs, sems)


def _carry(comm, n_in, n_out, first, last, refs):
    if comm is None:
        return refs[:n_in], refs[n_in:n_in + n_out], (lambda: None)
    a, b = len(comm.ins), len(comm.out_shape)
    ins, c_in = refs[:n_in], refs[n_in:n_in + a]
    outs, c_out = refs[n_in + a:n_in + a + n_out], refs[n_in + a + n_out:n_in + a + n_out + b]
    sems = refs[n_in + a + n_out + b:n_in + a + n_out + b + len(comm.scratch)]
    pl.when(first)(lambda: comm.start(c_in, c_out, sems))
    return ins, outs, (lambda: pl.when(last)(lambda: comm.finish(c_in, c_out, sems)))


def _sbl_fwd(q, k, v, *, width, hd, name, tq=256, comm=None):
    (qa, qo), (ka, ko), (va, vo) = q, k, v
    S = qa.shape[0]
    tq = _tile(S, tq)
    tk = tq
    scale = hd ** -0.5
    n_g, n_q = width // LANES, S // tq

    def body(*refs):
        qi = pl.program_id(1)
        gi = pl.program_id(0)
        got = _carry(comm, 3, 2, (gi == 0) & (qi == 0), (gi == n_g - 1) & (qi == n_q - 1), refs)
        (q_ref, k_ref, v_ref), (o_ref, tot_ref) = got[0], got[1]
        masks = _lane_masks(hd, tq)
        qs = q_ref[...].astype(F32) * scale
        qm = [_keep(qs, m).astype(BF16) for m in masks]
        strict = lax.broadcasted_iota(jnp.int32, (tq, tk), 1) < lax.broadcasted_iota(jnp.int32, (tq, tk), 0)
        later = _tri("after", tk)

        def tile(kb, carry, diag):
            ks = pl.multiple_of(kb * tk, tk)
            kv = k_ref[pl.ds(ks, tk), :].astype(BF16)
            vv = v_ref[pl.ds(ks, tk), :].astype(BF16)
            out = []
            for hh in range(len(masks)):
                acc, c_pos = carry[2 * hh], carry[2 * hh + 1]
                pos, ls = _softplus_parts(_dot(qm[hh], kv, _NT))
                if diag:
                    pos = jnp.where(strict, pos, 0.0)
                pos_after, pos_all = _scan_cols(pos, later, True)
                w = jnp.exp(ls - (pos_after + c_pos))
                if diag:
                    w = jnp.where(strict, w, 0.0)
                out += [acc + _dot(w.astype(BF16), vv), c_pos + pos_all]
            return tuple(out)

        init = (jnp.zeros((tq, LANES), F32), jnp.zeros((tq, 1), F32)) * len(masks)
        carry = tile(qi, init, True)
        carry = lax.fori_loop(0, qi, lambda i, c: tile(qi - 1 - i, c, False), carry)
        o_ref[...] = _merge(carry[0::2], masks).astype(o_ref.dtype)
        tot_ref[...] = _merge([jnp.broadcast_to(-c, (tq, LANES)) for c in carry[1::2]], masks)
        got[2]()

    c_ins, c_in_specs, c_out_specs, c_out_shape, c_scratch = _comm_args(comm)
    return pl.pallas_call(
        body, name=name, grid=(n_g, n_q),
        in_specs=[_cols(tq, qo), _cols_all(S, ko), _cols_all(S, vo)] + c_in_specs,
        out_specs=[_cols(tq, 0), _cols(tq, 0)] + c_out_specs,
        out_shape=[jax.ShapeDtypeStruct((S, width), BF16), jax.ShapeDtypeStruct((S, width), F32)] + c_out_shape,
        scratch_shapes=c_scratch,
        compiler_params=_params(("arbitrary", "arbitrary")),
    )(qa, ka, va, *c_ins)


def _comm_args(comm, aliased=False):
    if comm is None:
        return [], [], [], [], []
    assert aliased or not comm.aliases
    return comm.ins, [_ANY] * len(comm.ins), [_ANY] * len(comm.out_shape), comm.out_shape, comm.scratch


def _comm_aliases(comm, n_in, n_out):
    return {} if comm is None else {n_in + i: n_out + o for i, o in comm.aliases}


def _sbl_bwd(q, k, v, do, tot, *, width, hd, name, tq=256, comm=None):
    (qa, qo), (ka, ko), (va, vo) = q, k, v
    S = qa.shape[0]
    tq = _tile(S, tq)
    tk = tq
    scale = hd ** -0.5
    n_g, n_q = width // LANES, S // tq

    def body(*refs):
        qi = pl.program_id(1)
        gi = pl.program_id(0)
        got = _carry(comm, 5, 3, (gi == 0) & (qi == 0), (gi == n_g - 1) & (qi == n_q - 1), refs)
        (q_ref, k_ref, v_ref, do_ref, tot_ref), (dq_ref, dk_ref, dv_ref) = got[0], got[1]

        @pl.when(qi == 0)
        def _():
            dk_ref[...] = jnp.zeros_like(dk_ref)
            dv_ref[...] = jnp.zeros_like(dv_ref)

        masks = _lane_masks(hd, tq)
        qs = q_ref[...].astype(F32) * scale
        qm = [_keep(qs, m).astype(BF16) for m in masks]
        dov = [_keep(do_ref[...], m).astype(BF16) for m in masks]
        rest = [-_row_value(tot_ref[...], m) for m in masks]
        strict = lax.broadcasted_iota(jnp.int32, (tq, tk), 1) < lax.broadcasted_iota(jnp.int32, (tq, tk), 0)
        later, before = _tri("after", tk), _tri("before", tk)

        def tile(kb, carry, diag):
            ks = pl.multiple_of(kb * tk, tk)
            kv = k_ref[pl.ds(ks, tk), :].astype(BF16)
            vv = v_ref[pl.ds(ks, tk), :].astype(BF16)
            out = []
            dk_t, dv_t = None, None
            for hh in range(len(masks)):
                dq, c_pos, c_g = carry[3 * hh:3 * hh + 3]
                pos, ls = _softplus_parts(_dot(qm[hh], kv, _NT))
                if diag:
                    pos = jnp.where(strict, pos, 0.0)
                pos_after, pos_all = _scan_cols(pos, later, True)
                c_pos = c_pos + pos_all
                w = jnp.exp(ls - (pos_after + (rest[hh] - c_pos)))
                if diag:
                    w = jnp.where(strict, w, 0.0)
                g = _dot(dov[hh], vv, _NT) * w
                g_before, g_all = _scan_cols(g, before, False)
                g_before = g_before + c_g
                dz = g - jnp.exp(ls) * (g + g_before)
                if diag:
                    dz = jnp.where(strict, dz, 0.0)
                dzb = dz.astype(BF16)
                dk_h = _dot(dzb, qm[hh], _TN)
                dv_h = _dot(w.astype(BF16), dov[hh], _TN)
                dk_t = dk_h if dk_t is None else dk_t + dk_h
                dv_t = dv_h if dv_t is None else dv_t + dv_h
                out += [dq + _dot(dzb, kv), c_pos, c_g + g_all]
            dk_ref[pl.ds(ks, tk), :] += dk_t
            dv_ref[pl.ds(ks, tk), :] += dv_t
            return tuple(out)

        zero = jnp.zeros((tq, 1), F32)
        init = (jnp.zeros((tq, LANES), F32), zero, zero) * len(masks)
        carry = lax.fori_loop(0, qi, lambda kb, c: tile(kb, c, False), init)
        carry = tile(qi, carry, True)
        dq_ref[...] = _merge(carry[0::3], masks) * scale
        got[2]()

    full = jax.ShapeDtypeStruct((S, width), F32)
    c_ins, c_in_specs, c_out_specs, c_out_shape, c_scratch = _comm_args(comm)
    return pl.pallas_call(
        body, name=name, grid=(n_g, n_q),
        in_specs=[_cols(tq, qo), _cols_all(S, ko), _cols_all(S, vo), _cols(tq, do[1]), _cols(tq, tot[1])] + c_in_specs,
        out_specs=[_cols(tq, 0), _cols_all(S, 0), _cols_all(S, 0)] + c_out_specs,
        out_shape=[full, full, full] + c_out_shape,
        scratch_shapes=c_scratch,
        compiler_params=_params(("arbitrary", "arbitrary")),
    )(qa, ka, va, do[0], tot[0], *c_ins)


def _sml_fwd(q, k, v, bias=None, *, width, hd, causal, name, tq=256, tk=256, comm=None):
    (qa, qo), (ka, ko), (va, vo) = q, k, v
    S, Sk = qa.shape[0], ka.shape[0]
    tq, tk = _tile(S, tq), _tile(Sk, tk)
    if causal:
        assert tq == tk and S == Sk
    nk = Sk // tk
    hpg = LANES // hd
    scale = hd ** -0.5
    has_bias = bias is not None
    n_g, n_q = width // LANES, S // tq

    def body(*all_refs):
        qi, gi = pl.program_id(1), pl.program_id(0)
        got = _carry(comm, 5 if has_bias else 3, 2, (gi == 0) & (qi == 0), (gi == n_g - 1) & (qi == n_q - 1), all_refs)
        refs = tuple(got[0]) + tuple(got[1])
        q_ref, k_ref, v_ref = refs[:3]
        o_ref, lse_ref = refs[-2:]
        masks = _lane_masks(hd, tq)
        qs = q_ref[...].astype(F32) * scale
        qm = [_keep(qs, m).astype(BF16) for m in masks]
        allowed = lax.broadcasted_iota(jnp.int32, (tq, tk), 1) <= lax.broadcasted_iota(jnp.int32, (tq, tk), 0)

        def tile(kb, carry, diag):
            ks = pl.multiple_of(kb * tk, tk)
            kv = k_ref[pl.ds(ks, tk), :].astype(BF16)
            vv = v_ref[pl.ds(ks, tk), :].astype(BF16)
            out = []
            for hh in range(hpg):
                m, l, acc = carry[3 * hh:3 * hh + 3]
                z = _dot(qm[hh], kv, _NT)
                if has_bias:
                    z = z + refs[3][hh] - refs[4][hh, kb]
                if diag:
                    z = jnp.where(allowed, z, NEG_INF)
                m2 = jnp.maximum(m, jnp.max(z, axis=1, keepdims=True))
                p = jnp.exp(z - m2)
                alpha = jnp.exp(m - m2)
                out += [m2, alpha * l + jnp.sum(p, axis=1, keepdims=True), alpha * acc + _dot(p.astype(BF16), vv)]
            return tuple(out)

        init = (jnp.full((tq, 1), NEG_INF, F32), jnp.zeros((tq, 1), F32), jnp.zeros((tq, LANES), F32)) * hpg
        if causal:
            carry = lax.fori_loop(0, qi, lambda kb, c: tile(kb, c, False), init)
            carry = tile(qi, carry, True)
        else:
            carry = lax.fori_loop(0, nk, lambda kb, c: tile(kb, c, False), init)
        o_ref[...] = _merge([acc / l for l, acc in zip(carry[1::3], carry[2::3])], masks).astype(o_ref.dtype)
        lse_ref[...] = _merge([jnp.broadcast_to(m + jnp.log(l), (tq, LANES)) for m, l in zip(carry[0::3], carry[1::3])], masks)
        got[2]()

    in_specs = [_cols(tq, qo), _cols_all(Sk, ko), _cols_all(Sk, vo)]
    ins = [qa, ka, va]
    if has_bias:
        in_specs += [pl.BlockSpec((hpg, tq, 1), lambda g, i: (g, i, 0)),
                     pl.BlockSpec((hpg, nk, 1, tk), lambda g, i: (g, 0, 0, 0))]
        ins += list(bias)
    c_ins, c_in_specs, c_out_specs, c_out_shape, c_scratch = _comm_args(comm, aliased=True)
    return pl.pallas_call(
        body, name=name, grid=(n_g, n_q),
        in_specs=in_specs + c_in_specs, out_specs=[_cols(tq, 0), _cols(tq, 0)] + c_out_specs,
        out_shape=[jax.ShapeDtypeStruct((S, width), BF16), jax.ShapeDtypeStruct((S, width), F32)] + c_out_shape,
        input_output_aliases=_comm_aliases(comm, len(ins), 2),
        scratch_shapes=c_scratch,
        compiler_params=_params(("arbitrary", "arbitrary") if comm is not None else ("parallel", "arbitrary")),
    )(*ins, *c_ins)


def _sml_bwd(q, k, v, o, lse, do, bias=None, *, width, hd, causal, name, tq=256, tk=256, comm=None):
    (qa, qo), (ka, ko), (va, vo) = q, k, v
    S, Sk = qa.shape[0], ka.shape[0]
    tq, tk = _tile(S, tq), _tile(Sk, tk)
    nk = Sk // tk
    hpg = LANES // hd
    scale = hd ** -0.5
    has_bias = bias is not None
    n_in = 8 if has_bias else 6
    n_g, n_q = width // LANES, S // tq

    def body(*all_refs):
        qi, gi = pl.program_id(1), pl.program_id(0)
        got = _carry(comm, n_in, 5 if has_bias else 3, (gi == 0) & (qi == 0), (gi == n_g - 1) & (qi == n_q - 1), all_refs)
        refs = tuple(got[0]) + tuple(got[1])
        q_ref, k_ref, v_ref, o_ref, lse_ref, do_ref = refs[:6]
        dq_ref, dk_ref, dv_ref = refs[n_in:n_in + 3]

        @pl.when(qi == 0)
        def _():
            dk_ref[...] = jnp.zeros_like(dk_ref)
            dv_ref[...] = jnp.zeros_like(dv_ref)
            if has_bias:
                refs[n_in + 4][...] = jnp.zeros_like(refs[n_in + 4])

        masks = _lane_masks(hd, tq)
        qs = q_ref[...].astype(F32) * scale
        qm = [_keep(qs, m).astype(BF16) for m in masks]
        do32 = do_ref[...]
        dov = [_keep(do32, m).astype(BF16) for m in masks]
        prod = do32 * o_ref[...].astype(F32)
        delta = [jnp.sum(_keep(prod, m), axis=1, keepdims=True) for m in masks]
        lses = [_row_value(lse_ref[...], m) for m in masks]
        allowed = lax.broadcasted_iota(jnp.int32, (tq, tk), 1) <= lax.broadcasted_iota(jnp.int32, (tq, tk), 0)

        def tile(kb, carry, diag):
            ks = pl.multiple_of(kb * tk, tk)
            kv = k_ref[pl.ds(ks, tk), :].astype(BF16)
            vv = v_ref[pl.ds(ks, tk), :].astype(BF16)
            out = []
            dk_t, dv_t = None, None
            for hh in range(hpg):
                dq, db_row = carry[2 * hh:2 * hh + 2]
                z = _dot(qm[hh], kv, _NT)
                if has_bias:
                    z = z + refs[6][hh] - refs[7][hh, kb]
                p = jnp.exp(z - lses[hh])
                if diag:
                    p = jnp.where(allowed, p, 0.0)
                dz = p * (_dot(dov[hh], vv, _NT) - delta[hh])
                dzb = dz.astype(BF16)
                dk_h = _dot(dzb, qm[hh], _TN)
                dv_h = _dot(p.astype(BF16), dov[hh], _TN)
                dk_t = dk_h if dk_t is None else dk_t + dk_h
                dv_t = dv_h if dv_t is None else dv_t + dv_h
                if has_bias:
                    db_row = db_row + jnp.sum(dz, axis=1, keepdims=True)
                    refs[n_in + 4][hh, kb] += jnp.sum(dz, axis=0, keepdims=True)
                out += [dq + _dot(dzb, kv), db_row]
            dk_ref[pl.ds(ks, tk), :] += dk_t
            dv_ref[pl.ds(ks, tk), :] += dv_t
            return tuple(out)

        init = (jnp.zeros((tq, LANES), F32), jnp.zeros((tq, 1), F32)) * hpg
        if causal:
            carry = lax.fori_loop(0, qi, lambda kb, c: tile(kb, c, False), init)
            carry = tile(qi, carry, True)
        else:
            carry = lax.fori_loop(0, nk, lambda kb, c: tile(kb, c, False), init)
        dq_ref[...] = _merge(carry[0::2], masks) * scale
        if has_bias:
            for hh in range(hpg):
                refs[n_in + 3][hh] = carry[2 * hh + 1]
        got[2]()

    in_specs = [_cols(tq, qo), _cols_all(Sk, ko), _cols_all(Sk, vo), _cols(tq, o[1]), _cols(tq, lse[1]), _cols(tq, do[1])]
    ins = [qa, ka, va, o[0], lse[0], do[0]]
    out_specs = [_cols(tq, 0), _cols_all(Sk, 0), _cols_all(Sk, 0)]
    out_shape = [jax.ShapeDtypeStruct((S, width), F32), jax.ShapeDtypeStruct((Sk, width), F32),
                 jax.ShapeDtypeStruct((Sk, width), F32)]
    if has_bias:
        rspec = pl.BlockSpec((hpg, tq, 1), lambda g, i: (g, i, 0))
        cspec = pl.BlockSpec((hpg, nk, 1, tk), lambda g, i: (g, 0, 0, 0))
        in_specs += [rspec, cspec]
        ins += list(bias)
        out_specs += [rspec, cspec]
        n_heads = width // hd
        out_shape += [jax.ShapeDtypeStruct((n_heads, S, 1), F32), jax.ShapeDtypeStruct((n_heads, nk, 1, tk), F32)]
    c_ins, c_in_specs, c_out_specs, c_out_shape, c_scratch = _comm_args(comm)
    return pl.pallas_call(
        body, name=name, grid=(n_g, n_q),
        in_specs=in_specs + c_in_specs, out_specs=out_specs + c_out_specs, out_shape=out_shape + c_out_shape,
        scratch_shapes=c_scratch,
        compiler_params=_params(("arbitrary", "arbitrary") if comm is not None else ("parallel", "arbitrary")),
    )(*ins, *c_ins)


def _head_sums(t, masks):
    sums = [jnp.sum(_keep(t, m), axis=1, keepdims=True) for m in masks]
    return _merge([jnp.broadcast_to(s, t.shape) for s in sums], masks) if len(masks) > 1 else sums[0]


def _hnorm_fwd(x, g_lanes, *, width, hd, name, tr=1024):
    xa, xo = x
    R = xa.shape[0]
    tr = _tile(R, tr, align=16)
    n_blk = width // LANES

    def body(x_ref, g_ref, o_ref):
        masks = _lane_masks(hd, tr)
        for j in range(n_blk):
            sl = slice(j * LANES, (j + 1) * LANES)
            xv = x_ref[:, sl].astype(F32)
            r = lax.rsqrt(_head_sums(xv * xv, masks) * (1.0 / hd) + EPS)
            o_ref[:, sl] = (xv * r * g_ref[...]).astype(o_ref.dtype)

    assert (xo * LANES) % width == 0
    return pl.pallas_call(
        body, name=name, grid=(R // tr,),
        in_specs=[pl.BlockSpec((tr, width), lambda i: (i, xo * LANES // width)), pl.BlockSpec((1, LANES), lambda i: (0, 0))],
        out_specs=pl.BlockSpec((tr, width), lambda i: (i, 0)),
        out_shape=jax.ShapeDtypeStruct((R, width), BF16),
        compiler_params=_params(("parallel",)),
    )(xa, g_lanes)


def _hnorm_bwd(x, g_lanes, dy, *, width, hd, name, tr=1024):
    xa, xo = x
    R = xa.shape[0]
    tr = _tile(R, tr, align=16)
    n_blk = width // LANES

    def body(x_ref, g_ref, dy_ref, dx_ref, dg_ref):
        masks = _lane_masks(hd, tr)
        dg = jnp.zeros((1, LANES), F32)
        for j in range(n_blk):
            sl = slice(j * LANES, (j + 1) * LANES)
            xv = x_ref[:, sl].astype(F32)
            dyv = dy_ref[:, sl].astype(F32)
            r = lax.rsqrt(_head_sums(xv * xv, masks) * (1.0 / hd) + EPS)
            xh = xv * r
            dyg = dyv * g_ref[...]
            c = _head_sums(dyg * xh, masks) * (1.0 / hd)
            dx_ref[:, sl] = (r * (dyg - xh * c)).astype(dx_ref.dtype)
            dg = dg + jnp.sum(dyv * xh, axis=0, keepdims=True)
        if hd * 2 == LANES:
            dg8 = jnp.broadcast_to(dg, (8, LANES))
            dg = (dg8 + pltpu.roll(dg8, shift=hd, axis=1))[0:1]
        else:
            assert hd == LANES

        @pl.when(pl.program_id(0) == 0)
        def _():
            dg_ref[...] = jnp.zeros_like(dg_ref)

        dg_ref[...] += dg

    assert (xo * LANES) % width == 0
    return pl.pallas_call(
        body, name=name, grid=(R // tr,),
        in_specs=[pl.BlockSpec((tr, width), lambda i: (i, xo * LANES // width)), pl.BlockSpec((1, LANES), lambda i: (0, 0)),
                  pl.BlockSpec((tr, width), lambda i: (i, 0))],
        out_specs=[pl.BlockSpec((tr, width), lambda i: (i, 0)), pl.BlockSpec((1, LANES), lambda i: (0, 0))],
        out_shape=[jax.ShapeDtypeStruct((R, width), BF16), jax.ShapeDtypeStruct((1, LANES), F32)],
        compiler_params=_params(("arbitrary",)),
    )(xa, g_lanes, dy)


def _split3_dot(x, tri):
    a = x.astype(BF16)
    r = x - a.astype(F32)
    b = r.astype(BF16)
    c = (r - b.astype(F32)).astype(BF16)
    return _dot(a, tri) + _dot(b, tri) + _dot(c, tri)


def _forget_fwd(logit_t, b_col, *, name, blk=512):
    H, S = logit_t.shape
    blk = _tile(S, blk)

    def body(l_ref, b_ref, f_ref):
        r_i = lax.broadcasted_iota(jnp.int32, (blk, blk), 0)
        c_i = lax.broadcasted_iota(jnp.int32, (blk, blk), 1)
        upto = (r_i <= c_i).astype(BF16)
        carry = jnp.zeros((H, 1), F32)
        for j in range(S // blk):
            u = l_ref[:, j * blk:(j + 1) * blk] + b_ref[...]
            lf, _ = _log_sigmoid_pair(u)
            f_ref[:, j * blk:(j + 1) * blk] = _split3_dot(lf, upto) + carry
            carry = carry + jnp.sum(lf, axis=1, keepdims=True)

    return pl.pallas_call(
        body, name=name,
        out_shape=jax.ShapeDtypeStruct((H, S), F32),
        compiler_params=pltpu.CompilerParams(vmem_limit_bytes=VMEM_LIMIT),
    )(logit_t, b_col)


def _forget_bwd(logit_t, b_col, d_f, *, name, blk=512):
    H, S = logit_t.shape
    blk = _tile(S, blk)

    def body(l_ref, b_ref, df_ref, dl_ref, db_ref):
        r_i = lax.broadcasted_iota(jnp.int32, (blk, blk), 0)
        c_i = lax.broadcasted_iota(jnp.int32, (blk, blk), 1)
        fromon = (r_i >= c_i).astype(BF16)
        carry = jnp.zeros((H, 1), F32)
        db = jnp.zeros((H, 1), F32)
        for j in reversed(range(S // blk)):
            sl = slice(j * blk, (j + 1) * blk)
            dfv = df_ref[:, sl]
            d_lf = _split3_dot(dfv, fromon) + carry
            carry = carry + jnp.sum(dfv, axis=1, keepdims=True)
            u = l_ref[:, sl] + b_ref[...]
            _, lsn = _log_sigmoid_pair(u)
            dl = d_lf * jnp.exp(lsn)
            dl_ref[:, sl] = dl
            db = db + jnp.sum(dl, axis=1, keepdims=True)
        db_ref[...] = db

    return pl.pallas_call(
        body, name=name,
        out_shape=[jax.ShapeDtypeStruct((H, S), F32), jax.ShapeDtypeStruct((H, 1), F32)],
        compiler_params=pltpu.CompilerParams(vmem_limit_bytes=VMEM_LIMIT),
    )(logit_t, b_col, d_f)


def _sigmoid(t):
    return 1.0 / (1.0 + jnp.exp(-t))


def _gate_fwd(o3, w3, proj, x, w_out, g_norm, D, *, name, tm=512):
    S = proj.shape[0]
    tm = _tile(S, tm)

    def body(o0, o1, o2, w0, w1, w2, g0, g1, g2, x_ref, wo_ref, gn_ref, merged_ref, x1_ref, h2_ref):
        acc = None
        for o_ref, w_ref, g_ref in ((o0, w0, g0), (o1, w1, g1), (o2, w2, g2)):
            t = _sigmoid(g_ref[...]) * _dot(o_ref[...], w_ref[...])
            acc = t if acc is None else acc + t
        merged = acc.astype(BF16)
        merged_ref[...] = merged
        x1 = x_ref[...] + _dot(merged, wo_ref[...])
        x1_ref[...] = x1
        h2_ref[...] = (x1 * lax.rsqrt(jnp.mean(x1 * x1, axis=-1, keepdims=True) + EPS) * gn_ref[...]).astype(BF16)

    ospec = lambda d: pl.BlockSpec((tm, d), lambda i: (i, 0))
    wspec = lambda w: pl.BlockSpec(w.shape, lambda i: (0, 0))
    gspec = lambda j: pl.BlockSpec((tm, D), lambda i: (i, j))
    row = pl.BlockSpec((tm, D), lambda i: (i, 0))
    return pl.pallas_call(
        body, name=name, grid=(S // tm,),
        in_specs=[ospec(o.shape[1]) for o in o3] + [wspec(w) for w in w3] + [gspec(j) for j in range(3)]
        + [row, wspec(w_out), wspec(g_norm)],
        out_specs=[row, row, row],
        out_shape=[jax.ShapeDtypeStruct((S, D), BF16), jax.ShapeDtypeStruct((S, D), F32), jax.ShapeDtypeStruct((S, D), BF16)],
        compiler_params=_params(("parallel",)),
    )(*o3, *w3, proj, proj, proj, x, w_out, g_norm)


def _gate_bwd(o3, w3, proj, dx1, w_out, D, *, name, tm=256):
    S = proj.shape[0]
    tm = _tile(S, tm)

    def body(o0, o1, o2, w0, w1, w2, g0, g1, g2, dx_ref, wo_ref, dg_ref, dw0, dw1, dw2, do0, do1, do2):
        first = pl.program_id(0) == 0
        dm = _dot(dx_ref[...].astype(BF16), wo_ref[...], _NT)
        for j, (o_ref, w_ref, g_ref, dw_ref, do_ref) in enumerate(
                ((o0, w0, g0, dw0, do0), (o1, w1, g1, dw1, do1), (o2, w2, g2, dw2, do2))):
            s = _sigmoid(g_ref[...])
            br = _dot(o_ref[...], w_ref[...])
            dg_ref[:, j * D:(j + 1) * D] = (dm * br * s * (1.0 - s)).astype(dg_ref.dtype)
            dbr = (dm * s).astype(BF16)
            do_ref[...] = _dot(dbr, w_ref[...], _NT)
            part = _dot(o_ref[...], dbr, _TN)

            @pl.when(first)
            def _():
                dw_ref[...] = part

            @pl.when(jnp.logical_not(first))
            def _():
                dw_ref[...] += part

    ospec = lambda d: pl.BlockSpec((tm, d), lambda i: (i, 0))
    wspec = lambda w: pl.BlockSpec(w.shape, lambda i: (0, 0))
    gspec = lambda j: pl.BlockSpec((tm, D), lambda i: (i, j))
    dspec = pl.BlockSpec((tm, D), lambda i: (i, 0))
    return pl.pallas_call(
        body, name=name, grid=(S // tm,),
        in_specs=[ospec(o.shape[1]) for o in o3] + [wspec(w) for w in w3] + [gspec(j) for j in range(3)]
        + [dspec, wspec(w_out)],
        out_specs=[pl.BlockSpec((tm, 3 * D), lambda i: (i, 0))] + [wspec(w) for w in w3] + [ospec(o.shape[1]) for o in o3],
        out_shape=[jax.ShapeDtypeStruct((S, proj.shape[1]), BF16)] + [jax.ShapeDtypeStruct(w.shape, F32) for w in w3]
        + [jax.ShapeDtypeStruct((S, o.shape[1]), F32) for o in o3],
        compiler_params=_params(("arbitrary",)),
    )(*o3, *w3, proj, proj, proj, dx1, w_out)


def _pair_sum(stacked, got, core, *, name):
    n, _, r, c = stacked.shape
    tr = _row_tile(r, c, 3)

    def body(core_ref, a_ref, b_ref, o_ref):
        o_ref[...] = (a_ref[0].astype(F32) + b_ref[...].astype(F32)).astype(o_ref.dtype)

    spec = pl.BlockSpec((1, tr, c), lambda s, i, core_ref: (s, i, 0))
    return pl.pallas_call(
        body, name=name,
        grid_spec=pltpu.PrefetchScalarGridSpec(
            num_scalar_prefetch=1, grid=(n, r // tr),
            in_specs=[pl.BlockSpec((1, 1, tr, c), lambda s, i, core_ref: (s, core_ref[0], i, 0)), spec],
            out_specs=spec),
        out_shape=jax.ShapeDtypeStruct((n, r, c), BF16),
        compiler_params=_params(("parallel", "parallel")),
    )(core.astype(jnp.int32).reshape(1), stacked, got)


def _chip_sum(parts, got, chip, *, name):
    _, r, c = parts.shape
    tr = _row_tile(r, c, 6)

    def body(chip_ref, p_ref, q0_ref, q1_ref, q2_ref, o_ref):
        o_ref[...] = ((p_ref[0].astype(F32) + q0_ref[0].astype(F32)) + q1_ref[0].astype(F32)) + q2_ref[0].astype(F32)

    from_chip = lambda j: pl.BlockSpec((1, tr, c), lambda i, chip_ref: (j, i, 0))
    return pl.pallas_call(
        body, name=name,
        grid_spec=pltpu.PrefetchScalarGridSpec(
            num_scalar_prefetch=1, grid=(r // tr,),
            in_specs=[pl.BlockSpec((1, tr, c), lambda i, chip_ref: (chip_ref[0], i, 0))] + [from_chip(j) for j in range(3)],
            out_specs=pl.BlockSpec((tr, c), lambda i, chip_ref: (i, 0))),
        out_shape=jax.ShapeDtypeStruct((r, c), F32),
        compiler_params=_params(("parallel",)),
    )(chip.astype(jnp.int32).reshape(1), parts, got, got, got)


def _adamw_math(w, g, m, v):
    m2 = ADAM_B1 * m + (1.0 - ADAM_B1) * g
    v2 = ADAM_B2 * v + (1.0 - ADAM_B2) * (g * g)
    m_hat = m2 / (1.0 - ADAM_B1 ** ADAM_STEP)
    v_hat = v2 / (1.0 - ADAM_B2 ** ADAM_STEP)
    delta = -ADAM_LR * (m_hat / (jnp.sqrt(v_hat) + ADAM_EPS) + ADAM_WD * w)
    return delta, m2, v2


def _adamw(w, g, m, v, *, name):
    return _ew(_adamw_math, [w, g, m, v], (F32, F32, F32), name=name)


def _adamw_small(w, parts, m, v, *, name):
    n = parts.shape[0]

    def body(w_ref, p_ref, m_ref, v_ref, g_ref, d_ref, m2_ref, v2_ref):
        g = p_ref[0]
        for i in range(1, n):
            g = g + p_ref[i]
        g_ref[...] = g
        d_ref[...], m2_ref[...], v2_ref[...] = _adamw_math(w_ref[...], g, m_ref[...], v_ref[...])

    shp = jax.ShapeDtypeStruct(w.shape, F32)
    return pl.pallas_call(body, name=name, out_shape=[shp] * 4)(w, parts, m, v)


_ANY = pl.BlockSpec(memory_space=pl.ANY)


def _mesh_place():
    x, y, c = lax.axis_index("x"), lax.axis_index("y"), lax.axis_index("c")
    chips = [(1 - x, y), (x, 1 - y), (1 - x, 1 - y)]
    return x, y, c, chips


def _remote(src, dst, sems, i, to):
    send_sems, recv_sems = sems
    return pltpu.make_async_remote_copy(src_ref=src, dst_ref=dst, send_sem=send_sems.at[i], recv_sem=recv_sems.at[i],
                                        device_id=to, device_id_type=MESH_ID)


def _gather_weights(shards, *, name):
    n = len(shards)

    def body(*refs):
        ins, outs = refs[:n], refs[n:2 * n]
        sems = refs[2 * n:2 * n + 2]
        x, y, c, chips = _mesh_place()
        me = 2 * x + y
        sibling = (x, y, 1 - c)
        sent = []
        for w in range(n):
            for j, chip in enumerate(chips):
                cp = _remote(ins[w].at[c], outs[w].at[me, c], sems, 6 * w + j, (chip[0], chip[1], c))
                cp.start()
                sent.append(cp)
        for w in range(n):
            for j, chip in enumerate(chips):
                got = outs[w].at[2 * chip[0] + chip[1], c]
                _remote(got, got, sems, 6 * w + j, sibling).wait_recv()
                cp = _remote(got, got, sems, 6 * w + 3 + j, sibling)
                cp.start()
                sent.append(cp)
        for w in range(n):
            for j, chip in enumerate(chips):
                got = outs[w].at[2 * chip[0] + chip[1], 1 - c]
                _remote(got, got, sems, 6 * w + 3 + j, sibling).wait_recv()
        for cp in sent:
            cp.wait_send()

    outs = pl.pallas_call(
        body, name=name,
        in_specs=[_ANY] * n, out_specs=[_ANY] * n,
        out_shape=[jax.ShapeDtypeStruct((N_CHIPS,) + s.shape, s.dtype) for s in shards],
        scratch_shapes=[pltpu.SemaphoreType.DMA((6 * n,)), pltpu.SemaphoreType.DMA((6 * n,))],
    )(*shards)
    me = 2 * lax.axis_index("x") + lax.axis_index("y")
    return [lax.dynamic_update_index_in_dim(o, s, me, 0) for o, s in zip(outs, shards)]


def _forward_halves(gathered, *, name):
    n = len(gathered)

    def body(*refs):
        ins, outs = refs[:n], refs[n:2 * n]
        sems = refs[2 * n:2 * n + 2]
        x, y, c, chips = _mesh_place()
        sibling = (x, y, 1 - c)
        sent = []
        for w in range(n):
            for j, chip in enumerate(chips):
                peer = 2 * chip[0] + chip[1]
                cp = _remote(ins[w].at[peer, c], outs[w].at[peer, c], sems, 3 * w + j, sibling)
                cp.start()
                sent.append(cp)
        for w in range(n):
            for j, chip in enumerate(chips):
                land = outs[w].at[2 * chip[0] + chip[1], 1 - c]
                _remote(land, land, sems, 3 * w + j, sibling).wait_recv()
        for cp in sent:
            cp.wait_send()

    return pl.pallas_call(
        body, name=name,
        in_specs=[_ANY] * n, out_specs=[_ANY] * n,
        out_shape=[jax.ShapeDtypeStruct(g.shape, g.dtype) for g in gathered],
        input_output_aliases={w: w for w in range(n)},
        scratch_shapes=[pltpu.SemaphoreType.DMA((3 * n,)), pltpu.SemaphoreType.DMA((3 * n,))],
    )(*gathered)


def _exchange_siblings(grads, *, name):
    n = len(grads)

    def body(*refs):
        ins, got = refs[:n], refs[n:2 * n]
        sems = refs[2 * n:2 * n + 2]
        x, y, c, _ = _mesh_place()
        sibling = (x, y, 1 - c)
        sent = []
        for w in range(n):
            for s in range(N_CHIPS):
                cp = _remote(ins[w].at[s, 1 - c], got[w].at[s], sems, N_CHIPS * w + s, sibling)
                cp.start()
                sent.append(cp)
        for w in range(n):
            for s in range(N_CHIPS):
                _remote(got[w].at[s], got[w].at[s], sems, N_CHIPS * w + s, sibling).wait_recv()
        for cp in sent:
            cp.wait_send()

    n_sem = N_CHIPS * n
    return pl.pallas_call(
        body, name=name,
        in_specs=[_ANY] * n, out_specs=[_ANY] * n,
        out_shape=[jax.ShapeDtypeStruct((N_CHIPS,) + g.shape[2:], g.dtype) for g in grads],
        scratch_shapes=[pltpu.SemaphoreType.DMA((n_sem,)), pltpu.SemaphoreType.DMA((n_sem,))],
    )(*grads)


def _share_halves(halves, small):
    n = len(halves)

    def body(*refs):
        ins, small_ref = refs[:n], refs[n]
        outs, small_out = refs[n + 1:2 * n + 1], refs[2 * n + 1]
        sems = refs[2 * n + 2:2 * n + 4]
        x, y, c, chips = _mesh_place()
        sibling = (x, y, 1 - c)
        me = 4 * x + 2 * y + c
        sent = [_remote(ins[w], outs[w].at[c], sems, w, sibling) for w in range(n)]
        peers = [sibling] + [(ch[0], ch[1], cc) for ch in chips for cc in (c, 1 - c)]
        sent += [_remote(small_ref, small_out.at[me], sems, n + j, peer) for j, peer in enumerate(peers)]
        for cp in sent:
            cp.start()
        for w in range(n):
            _remote(outs[w].at[1 - c], outs[w].at[1 - c], sems, w, sibling).wait_recv()
        for j, peer in enumerate(peers):
            frm = small_out.at[4 * peer[0] + 2 * peer[1] + peer[2]]
            _remote(frm, frm, sems, n + j, peer).wait_recv()
        for cp in sent:
            cp.wait_send()

    n_sem = n + 7
    outs = pl.pallas_call(
        body, name="share_halves",
        in_specs=[_ANY] * (n + 1), out_specs=[_ANY] * (n + 1),
        out_shape=[jax.ShapeDtypeStruct((2,) + h.shape, h.dtype) for h in halves]
        + [jax.ShapeDtypeStruct((8,) + small.shape, small.dtype)],
        scratch_shapes=[pltpu.SemaphoreType.DMA((n_sem,)), pltpu.SemaphoreType.DMA((n_sem,))],
    )(*halves, small)
    c = lax.axis_index("c")
    me = 4 * lax.axis_index("x") + 2 * lax.axis_index("y") + c
    return ([lax.dynamic_update_index_in_dim(o, h, c, 0) for o, h in zip(outs[:n], halves)],
            lax.dynamic_update_index_in_dim(outs[n], small, me, 0))


EARLY = ("w_ff_down", "w_ff_up", "w_out", "w_branch_sb", "w_branch_fox", "w_branch_mem", "w_mem_kv")


def _norm_bwd_tail(dy, x, add, g):
    r = lax.rsqrt(jnp.mean(x * x, axis=-1, keepdims=True) + EPS)
    xh = x * r
    dyg = dy * g
    c = jnp.mean(dyg * xh, axis=-1, keepdims=True)
    return r * (dyg - xh * c) + add, jnp.sum(dy * xh, axis=0, keepdims=True)


def _split(outs, n):
    outs = list(outs) if isinstance(outs, (list, tuple)) else [outs]
    return outs[:n], outs[n:]


def _local_step(x, mem, target, small, W, gather_rest=None, reduce_early=None, reduce_late=None):
    S, D = x.shape
    o_qkv, o_mq, o_f = 3 * D, 3 * D + 2 * 3 * D_SB, 3 * D + 2 * 3 * D_SB + D_MEM
    tq = 512

    g_comms, finish_weights = gather_rest if gather_rest is not None else ([None] * 3, None)
    (proj, h), landed = _split(_mm(x, W["w_in"], name="in_proj", tb=True, tn=768, a_gain=small["g_mix_norm"],
                                   comm=g_comms[0]), 2)
    blk = lambda j: (proj, (o_qkv + j * D_SB) // LANES)
    sb_q, sb_k, sb_v, fx_q, fx_k, fx_v = [blk(j) for j in range(6)]
    m_q = (proj, o_mq // LANES)
    f_logit_t = _mm(W["w_in"][o_f:o_f + ROW_TILE], h, name="forget_logits", tb=True)[:FOX_HEADS]
    b_col = small["b_forget"].reshape(FOX_HEADS, 1)
    lanes = lambda g: jnp.tile(g, (1, LANES // g.shape[1]))
    g_fq, g_fk, g_mq, g_mk = [lanes(small[k]) for k in ("g_fox_q", "g_fox_k", "g_mem_q", "g_mem_k")]

    (o_sb, sb_tot), more = _split(_sbl_fwd(sb_q, sb_k, sb_v, width=D_SB, hd=HD, name="sb_fwd", tq=tq, comm=g_comms[1]), 2)
    landed += more

    fq = _hnorm_fwd(fx_q, g_fq, width=D_FOX, hd=HD, name="fox_q_norm")
    fk = _hnorm_fwd(fx_k, g_fk, width=D_FOX, hd=HD, name="fox_k_norm")
    f_cum = _forget_fwd(f_logit_t, b_col, name="forget_fwd")
    tkf = _tile(S, tq)
    f_bias = (f_cum.reshape(FOX_HEADS, S, 1), f_cum.reshape(FOX_HEADS, S // tkf, 1, tkf))
    fox_comm = None if finish_weights is None else _Joint([g_comms[2], _ChipExchange("forward", landed)])
    (o_fox, fox_lse), more = _split(_sml_fwd((fq, 0), (fk, 0), fx_v, f_bias, width=D_FOX, hd=HD, causal=True,
                                             name="fox_fwd", tq=tq, tk=tq, comm=fox_comm), 2)
    if finish_weights is not None:
        n_last = len(g_comms[2].ins)
        W = {**W, **finish_weights(more[:n_last], more[n_last:])}

    mh = _rmsnorm_fwd(mem, small["g_mem_norm"], BF16, name="mem_norm")
    mkv = _mm(mh, W["w_mem_kv"], name="mem_kv")
    mv = (mkv, D_MEM // LANES)
    mq = _hnorm_fwd(m_q, g_mq, width=D_MEM, hd=MEM_HD, name="mem_q_norm")
    mk = _hnorm_fwd((mkv, 0), g_mk, width=D_MEM, hd=MEM_HD, name="mem_k_norm")
    o_mem, mem_lse = _sml_fwd((mq, 0), (mk, 0), mv, width=D_MEM, hd=MEM_HD, causal=False, name="mem_fwd", tq=tq, tk=256)

    o3 = [o_sb, o_fox, o_mem]
    w3 = [W["w_branch_sb"], W["w_branch_fox"], W["w_branch_mem"]]
    merged, x1, h2 = _gate_fwd(o3, w3, proj, x, W["w_out"], small["g_mlp_norm"], D, name="gate_fwd")

    def relu2(acc):
        u = jnp.maximum(acc, 0.0)
        return u, u * u

    u, a = _mm(h2, W["w_ff_up"], name="ff_up", out_dtypes=(BF16, BF16), epilogue=relu2)
    def head(acc, res, tgt):
        d = (res + acc - tgt) * (1.0 / D)
        return d, d, jnp.sum(d * d, axis=0, keepdims=True)

    dy, dy16, sq_rows = _mm(a, W["w_ff_down"], name="ff_down", extras=(x1, target), out_dtypes=(F32, BF16),
                            epilogue=head, col_sums=1, tn=512)
    loss = (0.5 * D) * jnp.sum(sq_rows)

    G = {}
    du = _mm(dy16, W["w_ff_down"], name="d_ff_act", tb=True, out_dtypes=(BF16,), extras=(u,),
             epilogue=lambda acc, uu: (acc * (2.0 * uu.astype(F32)),))
    G["w_ff_down"] = _mm(a, dy16, name="d_w_ff_down", ta=True, out_dtypes=(BF16,))
    G["w_ff_up"] = _mm(h2, du, name="d_w_ff_up", ta=True, out_dtypes=(BF16,))
    dx1, dg_rows = _mm(du, W["w_ff_up"], name="d_mlp_in", tb=True, extras=(x1, dy, small["g_mlp_norm"]),
                       epilogue=_norm_bwd_tail, col_sums=1, tm=512, tn=D)
    dg_mlp = jnp.sum(dg_rows, axis=0, keepdims=True)
    G["w_out"] = _mm(merged, dx1, name="d_w_out", ta=True, out_dtypes=(BF16,))
    dgate, dw0, dw1, dw2, do_sb, do_fox, do_mem = _gate_bwd(o3, w3, proj, dx1, W["w_out"], D, name="gate_bwd")
    for nm, dw in zip(("w_branch_sb", "w_branch_fox", "w_branch_mem"), (dw0, dw1, dw2)):
        G[nm] = dw.astype(BF16)

    sib_comm, after_siblings = (reduce_early({k: G.pop(k) for k in EARLY if k != "w_mem_kv"})
                                if reduce_early is not None else (None, None))
    (dmq_n, dmk_n, dmv), landed_sib = _split(
        _sml_bwd((mq, 0), (mk, 0), mv, (o_mem, 0), (mem_lse, 0), (do_mem, 0), width=D_MEM, hd=MEM_HD, causal=False,
                 name="mem_bwd", tq=tq, tk=256, comm=sib_comm), 3)
    dm_q, dg_mem_q = _hnorm_bwd(m_q, g_mq, dmq_n, width=D_MEM, hd=MEM_HD, name="d_mem_q_norm")
    dmk_raw, dg_mem_k = _hnorm_bwd((mkv, 0), g_mk, dmk_n, width=D_MEM, hd=MEM_HD, name="d_mem_k_norm")
    dmkv = jnp.concatenate([dmk_raw, dmv.astype(BF16)], axis=1)
    G["w_mem_kv"] = _mm(mh, dmkv, name="d_w_mem_kv", ta=True, out_dtypes=(BF16,))
    dmh = _mm(dmkv, W["w_mem_kv"], name="d_mem_h", tb=True)
    _, dg_mem = _rmsnorm_bwd(mem, small["g_mem_norm"], dmh, name="d_mem_norm")

    r_comms, r_finish = (after_siblings(landed_sib, {"w_mem_kv": G.pop("w_mem_kv")})
                         if after_siblings is not None else ([None] * 2, None))
    dsb, landed_sb = _split(_sbl_bwd(sb_q, sb_k, sb_v, (do_sb, 0), (sb_tot, 0), width=D_SB, hd=HD, name="sb_bwd", tq=tq,
                                     comm=r_comms[0]), 3)
    (dfq, dfk, dfv, df_row, df_col), landed_fox = _split(
        _sml_bwd((fq, 0), (fk, 0), fx_v, (o_fox, 0), (fox_lse, 0), (do_fox, 0), f_bias, width=D_FOX, hd=HD, causal=True,
                 name="fox_bwd", tq=tq, tk=tq, comm=r_comms[1]), 5)
    early = r_finish(landed_sb, landed_fox) if r_finish is not None else {}
    dfx_q, dg_fox_q = _hnorm_bwd(fx_q, g_fq, dfq, width=D_FOX, hd=HD, name="d_fox_q_norm")
    dfx_k, dg_fox_k = _hnorm_bwd(fx_k, g_fk, dfk, width=D_FOX, hd=HD, name="d_fox_k_norm")
    d_fcum = df_row.reshape(FOX_HEADS, S) - df_col.reshape(FOX_HEADS, S)
    d_flogit_t, db_forget = _forget_bwd(f_logit_t, b_col, d_fcum, name="forget_bwd")
    dg_fox_q, dg_fox_k = dg_fox_q[:, :HD], dg_fox_k[:, :HD]

    rest_cols = jnp.concatenate([t.astype(BF16) for t in (*dsb, dfx_q, dfx_k, dfv, dm_q)]
                                + [d_flogit_t.T.astype(BF16), jnp.zeros((S, F_PAD - FOX_HEADS), BF16)], axis=1)
    dproj = lax.dynamic_update_slice(dgate, rest_cols, (0, 3 * D))
    g_w_in = _mm(dproj, h, name="d_w_in", ta=True, out_dtypes=(BF16,), tm=768)
    comm, finish = reduce_late({"w_in": g_w_in}) if reduce_late is not None else (None, None)
    (grad_x, dg_rows), landed = _split(
        _mm(dproj, W["w_in"], name="d_mix_in", tk=2304, tm=512, tn=D, extras=(x, dx1, small["g_mix_norm"]),
            epilogue=_norm_bwd_tail, col_sums=1, comm=comm), 2)
    dg_mix = jnp.sum(dg_rows, axis=0, keepdims=True)
    if finish is None:
        G["w_in"] = g_w_in
    else:
        early.update(finish(landed))

    small_grads = dict(g_mix_norm=dg_mix, g_mem_norm=dg_mem, b_forget=db_forget.reshape(1, FOX_HEADS),
                       g_fox_q=dg_fox_q, g_fox_k=dg_fox_k, g_mem_q=dg_mem_q, g_mem_k=dg_mem_k, g_mlp_norm=dg_mlp)
    return loss, grad_x, G, small_grads, early


BIG = ("w_in", "w_mem_kv", "w_branch_sb", "w_branch_fox", "w_branch_mem", "w_out", "w_ff_up", "w_ff_down")
COLUMN_SHARDED = ("w_in", "w_branch_sb", "w_branch_fox", "w_branch_mem", "w_ff_up")
SMALL = ("g_mix_norm", "g_mem_norm", "b_forget", "g_fox_q", "g_fox_k", "g_mem_q", "g_mem_k", "g_mlp_norm")
ORDER = ("g_mix_norm", "g_mem_norm", "w_in", "b_forget", "g_fox_q", "g_fox_k", "g_mem_q", "g_mem_k", "w_mem_kv",
         "w_branch_sb", "w_branch_fox", "w_branch_mem", "w_out", "g_mlp_norm", "w_ff_up", "w_ff_down")


def _unshard(name, gathered):
    n, _, rh, c = gathered.shape
    t = gathered.reshape(n, 2 * rh, c)
    if name in COLUMN_SHARDED:
        return t.transpose(1, 0, 2).reshape(2 * rh, n * c)
    return t.reshape(n * 2 * rh, c)


def _reshard(name, full):
    if name in COLUMN_SHARDED:
        r, c = full.shape
        t = full.reshape(r, N_CHIPS, c // N_CHIPS).transpose(1, 0, 2)
    else:
        r, c = full.shape[0] // N_CHIPS, full.shape[1]
        t = full.reshape(N_CHIPS, r, c)
    return t.reshape(N_CHIPS, 2, t.shape[1] // 2, t.shape[2])


ROW_TILE = 16
IN_BUF_ALIGN = 256


def _in_segments(D):
    n_qkv = 6 * D_SB
    o_mq, o_gate = n_qkv + FOX_HEADS, n_qkv + FOX_HEADS + D_MEM
    return [(0, n_qkv, 3 * D), (n_qkv, o_mq, 3 * D + n_qkv + D_MEM), (o_mq, o_gate, 3 * D + n_qkv), (o_gate, o_gate + 3 * D, 0)]


class _InLayout:
    def __init__(self, D, shard, n):
        self.D, self.shard, self.n = D, shard, n
        down = lambda v: v // ROW_TILE * ROW_TILE
        up = lambda v: -(-v // ROW_TILE) * ROW_TILE
        self.pieces = []
        ends = []
        for s in range(n):
            cursor, mine = 0, []
            for a, b, p in _in_segments(D):
                x0, x1 = max(a, s * shard), min(b, (s + 1) * shard)
                if x0 < x1:
                    p0 = p + x0 - a
                    rows = up(p0 + x1 - x0) - down(p0)
                    mine.append((x0 - s * shard, x1 - x0, p0, cursor, rows))
                    cursor += rows
            self.pieces.append(mine)
            ends.append(cursor)
        self.rows = -(-max(ends) // IN_BUF_ALIGN) * IN_BUF_ALIGN
        self.padded_rows = 3 * D + 6 * D_SB + D_MEM + F_PAD

    def _per_shard(self, fn, chip, operand):
        return lax.switch(chip, [functools.partial(fn, s) for s in range(self.n)], operand)

    def pack(self, chip, rows):
        def one(s, t):
            out, at = [], 0
            for x0, n_rows, p0, start, region in self.pieces[s]:
                lead = p0 % ROW_TILE
                out += [jnp.zeros((start + lead - at, t.shape[1]), t.dtype), t[x0:x0 + n_rows]]
                at = start + lead + n_rows
            return jnp.concatenate(out + [jnp.zeros((self.rows - at, t.shape[1]), t.dtype)], axis=0)
        return self._per_shard(one, chip, rows)

    def unpack(self, chip, buf, pad_to):
        def one(s, t):
            out = [t[start + p0 % ROW_TILE:start + p0 % ROW_TILE + n_rows] for _, n_rows, p0, start, _ in self.pieces[s]]
            return jnp.concatenate(out + [jnp.zeros((pad_to - self.shard, t.shape[1]), t.dtype)], axis=0)
        return self._per_shard(one, chip, buf)

    def to_padded(self, bufs):
        runs = sorted((p0, s, start, region) for s in range(self.n) for _, _, p0, start, region in self.pieces[s])
        chunks, end = [], 0
        for p0, s, start, region in runs:
            d0 = p0 // ROW_TILE * ROW_TILE
            src = bufs[s, start:start + region]
            if d0 < end:
                assert end - d0 == ROW_TILE
                last = chunks.pop()
                chunks += [last[:-ROW_TILE], last[-ROW_TILE:] + src[:ROW_TILE], src[ROW_TILE:]]
            else:
                if d0 > end:
                    chunks.append(jnp.zeros((d0 - end, bufs.shape[2]), bufs.dtype))
                chunks.append(src)
            end = d0 + region
        chunks.append(jnp.zeros((self.padded_rows - end, bufs.shape[2]), bufs.dtype))
        return jnp.concatenate(chunks, axis=0)

    def from_padded(self, gp):
        bufs = []
        for s in range(self.n):
            out, at = [], 0
            for _, n_rows, p0, start, region in self.pieces[s]:
                d0 = p0 // ROW_TILE * ROW_TILE
                row = d0 + lax.broadcasted_iota(jnp.int32, (region, 1), 0)
                out.append(jnp.where((row >= p0) & (row < p0 + n_rows), gp[d0:d0 + region], jnp.zeros((), gp.dtype)))
                at = start + region
            bufs.append(jnp.concatenate(out + [jnp.zeros((self.rows - at, gp.shape[1]), gp.dtype)], axis=0))
        return jnp.stack(bufs)


SMALL_ROWS = 16


def _pack_small(vals, scalar=None):
    width = max(vals[k].shape[1] for k in SMALL)
    rows = [jnp.pad(vals[k].astype(F32), ((0, 0), (0, width - vals[k].shape[1]))) for k in SMALL]
    extra = jnp.zeros((SMALL_ROWS - len(SMALL), width), F32)
    if scalar is not None:
        extra = extra.at[0, 0].set(scalar)
    return jnp.concatenate(rows + [extra], axis=0)


def _unpack_small(packed, like):
    return {k: packed[i:i + 1, :like[k].shape[1]] for i, k in enumerate(SMALL)}


def kernel(x, mem, g_mix_norm, g_mem_norm, w_in, b_forget, g_fox_q, g_fox_k, g_mem_q, g_mem_k, w_mem_kv, w_branch_sb, w_branch_fox, w_branch_mem, w_out, g_mlp_norm, w_ff_up, w_ff_down, loss_target, m_g_mix_norm, m_g_mem_norm, m_w_in, m_b_forget, m_g_fox_q, m_g_fox_k, m_g_mem_q, m_g_mem_k, m_w_mem_kv, m_w_branch_sb, m_w_branch_fox, m_w_branch_mem, m_w_out, m_g_mlp_norm, m_w_ff_up, m_w_ff_down, v_g_mix_norm, v_g_mem_norm, v_w_in, v_b_forget, v_g_fox_q, v_g_fox_k, v_g_mem_q, v_g_mem_k, v_w_mem_kv, v_w_branch_sb, v_w_branch_fox, v_w_branch_mem, v_w_out, v_g_mlp_norm, v_w_ff_up, v_w_ff_down):
    given = dict(locals())
    D = x.shape[-1]
    weights = {k: given[k] for k in ORDER}
    moms = {k: given["m_" + k] for k in ORDER}
    vars_ = {k: given["v_" + k] for k in ORDER}

    me_chip = 2 * lax.axis_index("x") + lax.axis_index("y")

    n_in = w_in.shape[2]
    lay = _InLayout(D, n_in, N_CHIPS)
    transposed = lambda t: jnp.transpose(t[0])
    shards = {}
    for k in BIG:
        w = weights[k][0].astype(BF16)
        if k == "w_in":
            w = lay.pack(me_chip, jnp.transpose(w))
        shards[k] = w.reshape(2, w.shape[0] // 2, w.shape[1])
    gathered_in = _gather_weights([shards["w_in"]], name="gather_w_in")[0]
    W = {"w_in": lay.to_padded(gathered_in.reshape(N_CHIPS, lay.rows, D))}
    carried = (("w_branch_sb", "w_branch_fox", "w_branch_mem", "w_out"), ("w_ff_up", "w_mem_kv"), ("w_ff_down",))
    rest = [k for grp in carried for k in grp]
    assert sorted(rest + ["w_in"]) == sorted(BIG)

    def finish_weights(landed_last, forwarded):
        full = list(forwarded) + list(_forward_halves(landed_last, name="forward_halves"))
        full = [lax.dynamic_update_index_in_dim(o, shards[k], me_chip, 0) for k, o in zip(rest, full)]
        return {k: _unshard(k, g) for k, g in zip(rest, full)}

    me_core = lax.axis_index("c")

    def sum_chips(names, parts, got):
        return {k: _chip_sum(p, q, me_chip, name="sum_chips_" + k) for k, p, q in zip(names, parts, got)}

    def pair_sums(grads, tag):
        names = list(grads)
        stacked = {k: _reshard(k, grads[k]) for k in names if k != "w_in"}
        if "w_in" in grads:
            stacked["w_in"] = lay.from_padded(grads["w_in"]).reshape(N_CHIPS, 2, lay.rows // 2, D)
        got = _exchange_siblings([stacked[k] for k in names], name="exchange_siblings_" + tag)
        return {k: _pair_sum(stacked[k], q, me_core, name="sum_pair_" + k) for k, q in zip(names, got)}

    def reduce_early(grads):
        names = list(grads)
        stacked = [_reshard(k, grads[k]) for k in names]

        def after_siblings(landed, more):
            parts = {k: _pair_sum(t, q, me_core, name="sum_pair_" + k) for k, t, q in zip(names, stacked, landed)}
            parts.update(pair_sums(more, "early"))
            groups = [[k for k in parts if k in ("w_ff_down", "w_ff_up")], [k for k in parts if k not in ("w_ff_down", "w_ff_up")]]
            comms = [_ChipExchange("scatter", [parts[k] for k in grp]) for grp in groups]

            def finish(*got):
                out = {}
                for grp, q in zip(groups, got):
                    out.update(sum_chips(grp, [parts[k] for k in grp], q))
                return out
            return comms, finish
        return _ChipExchange("siblings", stacked), after_siblings

    def reduce_late(grads):
        parts = pair_sums(grads, "late")
        names = list(parts)
        return _ChipExchange("scatter", [parts[k] for k in names]), functools.partial(sum_chips, names, [parts[k] for k in names])

    small = {k: weights[k] for k in SMALL}
    loss_part, grad_x, G, small_grads, halves = _local_step(
        x[0], mem[0], loss_target[0], small, W,
        gather_rest=([_ChipExchange("gather", [shards[k] for k in grp]) for grp in carried], finish_weights),
        reduce_early=reduce_early, reduce_late=reduce_late)
    assert not G, list(G)
    reduced, small_parts = _share_halves([halves[k] for k in BIG], _pack_small(small_grads, loss_part))

    grads, deltas, new_m, new_v = {}, {}, {}, {}
    for k, g in zip(BIG, reduced):
        shp = weights[k].shape
        if k == "w_in":
            g2 = lay.unpack(me_chip, g.reshape(lay.rows, D), lay.rows)
            padded = lambda t: jnp.pad(transposed(t), ((0, lay.rows - n_in), (0, 0)))
            outs = _adamw(padded(weights[k]), g2, padded(moms[k]), padded(vars_[k]), name="adamw_" + k)
            g2, d, m2, v2 = [jnp.transpose(t[:n_in]) for t in (g2, *outs)]
        else:
            g2 = g.reshape(shp[1], shp[2])
            d, m2, v2 = _adamw(weights[k][0], g2, moms[k][0], vars_[k][0], name="adamw_" + k)
        grads[k], deltas[k], new_m[k], new_v[k] = g2.reshape(shp), d.reshape(shp), m2.reshape(shp), v2.reshape(shp)
    sg, sd, sm, sv = _adamw_small(_pack_small(small), small_parts, _pack_small({k: moms[k] for k in SMALL}),
                                  _pack_small({k: vars_[k] for k in SMALL}), name="adamw_small")
    for dst, packed in ((grads, sg), (deltas, sd), (new_m, sm), (new_v, sv)):
        dst.update(_unpack_small(packed, small))

    loss = sg[len(SMALL), 0]
    return (loss, grad_x[None], *[grads[k] for k in ORDER], *[deltas[k] for k in ORDER],
            *[new_m[k] for k in ORDER], *[new_v[k] for k in ORDER])
```

```python
import functools

import jax
import jax.numpy as jnp
from jax import lax
from jax.experimental import pallas as pl
from jax.experimental.pallas import tpu as pltpu

F32 = jnp.float32
BF16 = jnp.bfloat16
MESH_ID = pl.DeviceIdType.MESH

HD = 64
SB_HEADS = 8
FOX_HEADS = 8
MEM_HEADS = 4
MEM_HD = 128
D_SB = SB_HEADS * HD
D_FOX = FOX_HEADS * HD
D_MEM = MEM_HEADS * MEM_HD
EPS = 1e-6
NEG_INF = -1e30

ADAM_LR = 0.001
ADAM_B1 = 0.9
ADAM_B2 = 0.999
ADAM_EPS = 1e-08
ADAM_WD = 0.01
ADAM_STEP = 10

N_CHIPS = 4
VMEM_LIMIT = 56 * 1024 * 1024

F_PAD = 256


def _tile(n, target, align=128):
    if n <= target:
        return n
    best = None
    t = align
    while t <= target:
        if n % t == 0:
            best = t
        t += align
    assert best is not None, (n, target, align)
    return best


def _params(sem):
    return pltpu.CompilerParams(dimension_semantics=sem, vmem_limit_bytes=VMEM_LIMIT)


def _mm(a, b, *, name, ta=False, tb=False, out_dtypes=(F32,), epilogue=None, extras=(),
        tm=1024, tn=1024, tk=2048, comm=None, col_sums=0, a_gain=None):
    if ta:
        K, M = a.shape
    else:
        M, K = a.shape
    if tb:
        N, K2 = b.shape
    else:
        K2, N = b.shape
    assert K == K2, (a.shape, b.shape, ta, tb)
    tm, tn, tk = _tile(M, tm), _tile(N, tn), _tile(K, tk)
    nk = K // tk
    normed = a_gain is not None
    assert not normed or (nk == 1 and not ta)
    n_extra, n_out = len(extras) + normed, len(out_dtypes) + col_sums + normed
    if epilogue is None:
        epilogue = lambda acc: (acc,)
    dims = (((0 if ta else 1,), (1 if tb else 0,)), ((), ()))

    gm, gn = M // tm, N // tn

    def body(*refs):
        i, j, k = pl.program_id(0), pl.program_id(1), pl.program_id(2)
        got = _carry(comm, 2 + n_extra, n_out, (i == 0) & (j == 0) & (k == 0),
                     (i == gm - 1) & (j == gn - 1) & (k == nk - 1), refs)
        (a_ref, b_ref, *extra_refs), out_refs = got[0], list(got[1])
        if normed:
            gain_ref, normed_ref = extra_refs.pop(), out_refs.pop()

            @pl.when(j == 0)
            def _():
                xa = a_ref[...].astype(F32)
                normed_ref[...] = (xa * lax.rsqrt(jnp.mean(xa * xa, axis=-1, keepdims=True) + EPS) * gain_ref[...]).astype(BF16)

            av = normed_ref[...]
        else:
            av = a_ref[...].astype(BF16)
        part = lax.dot_general(av, b_ref[...].astype(BF16), dims, preferred_element_type=F32)

        def finish(acc):
            outs = epilogue(acc, *[r[...] for r in extra_refs])
            for o_ref, o in zip(out_refs[:len(out_dtypes)], outs):
                o_ref[...] = o.astype(o_ref.dtype)
            for o_ref, o in zip(out_refs[len(out_dtypes):], outs[len(out_dtypes):]):
                first = lax.broadcasted_iota(jnp.int32, o_ref.shape, 0) == 0
                o_ref[...] = jnp.where(first, jnp.broadcast_to(o, o_ref.shape), 0.0)

        if nk == 1:
            finish(part)
        else:
            acc_ref = refs[-1]

            @pl.when(k == 0)
            def _():
                acc_ref[...] = part

            @pl.when((k > 0) & (k < nk - 1))
            def _():
                acc_ref[...] += part

            @pl.when(k == nk - 1)
            def _():
                finish(acc_ref[...] + part)

        got[2]()

    a_spec = pl.BlockSpec((tk, tm), lambda i, j, k: (k, i)) if ta else pl.BlockSpec((tm, tk), lambda i, j, k: (i, k))
    b_spec = pl.BlockSpec((tn, tk), lambda i, j, k: (j, k)) if tb else pl.BlockSpec((tk, tn), lambda i, j, k: (k, j))
    mn_spec = pl.BlockSpec((tm, tn), lambda i, j, k: (i, j))
    row_spec = pl.BlockSpec((1, tn), lambda i, j, k: (0, j))
    sum_spec = pl.BlockSpec((8, tn), lambda i, j, k: (i, j))
    c_ins, c_in_specs, c_out_specs, c_out_shape, c_scratch = _comm_args(comm)
    sem = ("parallel", "arbitrary" if normed else "parallel", "arbitrary") if comm is None else ("arbitrary",) * 3
    outs = pl.pallas_call(
        body, name=name,
        grid=(gm, gn, nk),
        in_specs=[a_spec, b_spec] + [row_spec if e.shape[0] == 1 else mn_spec for e in extras]
        + ([pl.BlockSpec((1, K), lambda i, j, k: (0, 0))] if normed else []) + c_in_specs,
        out_specs=[mn_spec] * len(out_dtypes) + [sum_spec] * col_sums
        + ([pl.BlockSpec((tm, K), lambda i, j, k: (i, 0))] if normed else []) + c_out_specs,
        out_shape=[jax.ShapeDtypeStruct((M, N), dt) for dt in out_dtypes]
        + [jax.ShapeDtypeStruct((8 * gm, N), F32)] * col_sums
        + ([jax.ShapeDtypeStruct((M, K), BF16)] if normed else []) + c_out_shape,
        scratch_shapes=c_scratch + ([pltpu.VMEM((tm, tn), F32)] if nk > 1 else []),
        compiler_params=_params(sem),
    )(a, b, *extras, *([a_gain] if normed else []), *c_ins)
    return outs if len(outs) > 1 else outs[0]


def _row_tile(rows, cols, n_arrays):
    budget = 10 * 1024 * 1024
    cols_padded = -(-cols // 128) * 128
    target = max(16, budget // (cols_padded * 4 * n_arrays * 2))
    return _tile(rows, target, align=16)


def _ew(fn, ins, out_dtypes, *, name):
    R, C = ins[0].shape
    n_in, n_out = len(ins), len(out_dtypes)
    tr = _row_tile(R, C, n_in + n_out)

    def body(*refs):
        outs = fn(*[r[...] for r in refs[:n_in]])
        for o_ref, o in zip(refs[n_in:], outs):
            o_ref[...] = o.astype(o_ref.dtype)

    spec = pl.BlockSpec((tr, C), lambda i: (i, 0))
    outs = pl.pallas_call(
        body, name=name, grid=(R // tr,),
        in_specs=[spec] * n_in, out_specs=[spec] * n_out,
        out_shape=[jax.ShapeDtypeStruct((R, C), dt) for dt in out_dtypes],
        compiler_params=_params(("parallel",)),
    )(*ins)
    return outs if n_out > 1 else outs[0]


def _rmsnorm_fwd(x, g, out_dtype, *, name):
    R, d = x.shape
    tr = _row_tile(R, d, 3)

    def body(x_ref, g_ref, o_ref):
        xv = x_ref[...].astype(F32)
        r = lax.rsqrt(jnp.mean(xv * xv, axis=-1, keepdims=True) + EPS)
        o_ref[...] = (xv * r * g_ref[...]).astype(o_ref.dtype)

    return pl.pallas_call(
        body, name=name, grid=(R // tr,),
        in_specs=[pl.BlockSpec((tr, d), lambda i: (i, 0)), pl.BlockSpec((1, d), lambda i: (0, 0))],
        out_specs=pl.BlockSpec((tr, d), lambda i: (i, 0)),
        out_shape=jax.ShapeDtypeStruct((R, d), out_dtype),
        compiler_params=_params(("parallel",)),
    )(x, g)


def _rmsnorm_bwd(x, g, dy, add=None, *, name):
    R, d = x.shape
    has_add = add is not None
    tr = _row_tile(R, d, 5)

    def body(*refs):
        x_ref, g_ref, dy_ref = refs[:3]
        add_ref = refs[3] if has_add else None
        dx_ref, dg_ref = refs[-2:]
        xv = x_ref[...].astype(F32)
        dyv = dy_ref[...].astype(F32)
        r = lax.rsqrt(jnp.mean(xv * xv, axis=-1, keepdims=True) + EPS)
        xh = xv * r
        dyg = dyv * g_ref[...]
        c = jnp.mean(dyg * xh, axis=-1, keepdims=True)
        dx = r * (dyg - xh * c)
        if has_add:
            dx = dx + add_ref[...]
        dx_ref[...] = dx

        @pl.when(pl.program_id(0) == 0)
        def _():
            dg_ref[...] = jnp.zeros_like(dg_ref)

        dg_ref[...] += jnp.sum(dyv * xh, axis=0, keepdims=True)

    row = pl.BlockSpec((tr, d), lambda i: (i, 0))
    vec = pl.BlockSpec((1, d), lambda i: (0, 0))
    ins = [x, g, dy] + ([add] if has_add else [])
    return pl.pallas_call(
        body, name=name, grid=(R // tr,),
        in_specs=[row, vec, row] + ([row] if has_add else []),
        out_specs=[row, vec],
        out_shape=[jax.ShapeDtypeStruct((R, d), F32), jax.ShapeDtypeStruct((1, d), F32)],
        compiler_params=_params(("arbitrary",)),
    )(*ins)


_NT = (((1,), (1,)), ((), ()))
_TN = (((0,), (0,)), ((), ()))


def _dot(a, b, dims=(((1,), (0,)), ((), ()))):
    return lax.dot_general(a, b, dims, preferred_element_type=F32)


def _log_sigmoid_pair(z):
    sp = jnp.log(1.0 + jnp.exp(-jnp.abs(z)))
    return jnp.minimum(z, 0.0) - sp, jnp.minimum(-z, 0.0) - sp


LANES = 128
_LOW = -3e38


def _lane_masks(hd, rows):
    if hd == LANES:
        return [None]
    lane = lax.broadcasted_iota(jnp.int32, (rows, LANES), 1)
    return [(lane >= hh * hd) & (lane < (hh + 1) * hd) for hh in range(LANES // hd)]


def _keep(t, m):
    return t if m is None else jnp.where(m, t, 0.0)


def _merge(parts, masks):
    out = parts[-1]
    for p, m in zip(parts[-2::-1], masks[-2::-1]):
        out = jnp.where(m, p, out)
    return out


def _row_value(t, m):
    return jnp.max(t if m is None else jnp.where(m, t, _LOW), axis=1, keepdims=True)


def _cols(tq, off):
    return pl.BlockSpec((tq, LANES), lambda g, i: (i, off + g))


def _cols_all(rows, off):
    return pl.BlockSpec((rows, LANES), lambda g, i: (0, off + g))


SCAN_BLOCK = 256


def _tri(kind, cols):
    n = min(SCAN_BLOCK, cols)
    r = lax.broadcasted_iota(jnp.int32, (n, n), 0)
    c = lax.broadcasted_iota(jnp.int32, (n, n), 1)
    return ((r > c) if kind == "after" else (r < c)).astype(BF16)


def _scan_cols(x, tri, reverse):
    cols = x.shape[1]
    cb = min(SCAN_BLOCK, cols)
    assert cols % cb == 0 and tri.shape == (cb, cb)
    nb = cols // cb
    blocks = [x[:, b * cb:(b + 1) * cb] for b in range(nb)]
    outs, carry = [None] * nb, None
    for b in (reversed(range(nb)) if reverse else range(nb)):
        y = _dot(blocks[b].astype(BF16), tri)
        outs[b] = y if carry is None else y + carry
        s = jnp.sum(blocks[b], axis=1, keepdims=True)
        carry = s if carry is None else carry + s
    return (outs[0] if nb == 1 else jnp.concatenate(outs, axis=1)), carry


def _softplus_parts(z):
    pos = jnp.maximum(z, 0.0) + jnp.log(1.0 + jnp.exp(-jnp.abs(z)))
    return pos, z - pos


class _ChipExchange:
    def __init__(self, kind, ins):
        assert kind in ("gather", "scatter", "siblings", "forward")
        self.kind, self.ins = kind, list(ins)
        lead = {"gather": lambda s: (N_CHIPS,) + s, "scatter": lambda s: (3,) + s[1:], "siblings": lambda s: s[:1] + s[2:],
                "forward": lambda s: s}[kind]
        self.out_shape = [jax.ShapeDtypeStruct(lead(a.shape), a.dtype) for a in ins]
        self.aliases = [(w, w) for w in range(len(ins))] if kind == "forward" else []
        n = (N_CHIPS if kind == "siblings" else 3) * len(ins)
        self.scratch = [pltpu.SemaphoreType.DMA((n,)), pltpu.SemaphoreType.DMA((n,))]

    def _copies(self, in_refs, out_refs, sems, landing):
        x, y, c, chips = _mesh_place()
        me = 2 * x + y
        out = []
        for w in range(len(self.ins)):
            if self.kind == "siblings":
                for s in range(N_CHIPS):
                    land = out_refs[w].at[s]
                    src = land if landing else in_refs[w].at[s, 1 - c]
                    out.append(_remote(src, land, sems, N_CHIPS * w + s, (x, y, 1 - c)))
                continue
            for j, chip in enumerate(chips):
                peer = 2 * chip[0] + chip[1]
                to = (chip[0], chip[1], c)
                if self.kind == "gather":
                    src, dst, land = in_refs[w].at[c], out_refs[w].at[me, c], out_refs[w].at[peer, c]
                elif self.kind == "scatter":
                    src, dst, land = in_refs[w].at[peer], out_refs[w].at[j], out_refs[w].at[j]
                else:
                    src, dst, land, to = in_refs[w].at[peer, c], out_refs[w].at[peer, c], out_refs[w].at[peer, 1 - c], (x, y, 1 - c)
                if landing:
                    src, dst = land, land
                out.append(_remote(src, dst, sems, 3 * w + j, to))
        return out

    def start(self, in_refs, out_refs, sems):
        for cp in self._copies(in_refs, out_refs, sems, False):
            cp.start()

    def finish(self, in_refs, out_refs, sems):
        for cp in self._copies(in_refs, out_refs, sems, True):
            cp.wait_recv()
        for cp in self._copies(in_refs, out_refs, sems, False):
            cp.wait_send()


class _Joint:
    def __init__(self, parts):
        self.parts = list(parts)
        self.ins = [a for p in self.parts for a in p.ins]
        self.out_shape = [s for p in self.parts for s in p.out_shape]
        self.scratch = [s for p in self.parts for s in p.scratch]
        self.aliases, n_in, n_out = [], 0, 0
        for p in self.parts:
            self.aliases += [(n_in + i, n_out + o) for i, o in p.aliases]
            n_in, n_out = n_in + len(p.ins), n_out + len(p.out_shape)

    def _each(self, step, in_refs, out_refs, sems):
        i = o = s = 0
        for p in self.parts:
            a, b, c = len(p.ins), len(p.out_shape), len(p.scratch)
            getattr(p, step)(in_refs[i:i + a], out_refs[o:o + b], sems[s:s + c])
            i, o, s = i + a, o + b, s + c

    def start(self, in_refs, out_refs, sems):
        self._each("start", in_refs, out_refs, sems)

    def finish(self, in_refs, out_refs, sems):
        self._each("finish", in_refs, out_refs, sems)


def _carry(comm, n_in, n_out, first, last, refs):
    if comm is None:
        return refs[:n_in], refs[n_in:n_in + n_out], (lambda: None)
    a, b = len(comm.ins), len(comm.out_shape)
    ins, c_in = refs[:n_in], refs[n_in:n_in + a]
    outs, c_out = refs[n_in + a:n_in + a + n_out], refs[n_in + a + n_out:n_in + a + n_out + b]
    sems = refs[n_in + a + n_out + b:n_in + a + n_out + b + len(comm.scratch)]
    pl.when(first)(lambda: comm.start(c_in, c_out, sems))
    return ins, outs, (lambda: pl.when(last)(lambda: comm.finish(c_in, c_out, sems)))


def _sbl_fwd(q, k, v, *, width, hd, name, tq=256, comm=None):
    (qa, qo), (ka, ko), (va, vo) = q, k, v
    S = qa.shape[0]
    tq = _tile(S, tq)
    tk = tq
    scale = hd ** -0.5
    n_g, n_q = width // LANES, S // tq

    def body(*refs):
        qi = pl.program_id(1)
        gi = pl.program_id(0)
        got = _carry(comm, 3, 2, (gi == 0) & (qi == 0), (gi == n_g - 1) & (qi == n_q - 1), refs)
        (q_ref, k_ref, v_ref), (o_ref, tot_ref) = got[0], got[1]
        masks = _lane_masks(hd, tq)
        qs = q_ref[...].astype(F32) * scale
        qm = [_keep(qs, m).astype(BF16) for m in masks]
        strict = lax.broadcasted_iota(jnp.int32, (tq, tk), 1) < lax.broadcasted_iota(jnp.int32, (tq, tk), 0)
        later = _tri("after", tk)

        def tile(kb, carry, diag):
            ks = pl.multiple_of(kb * tk, tk)
            kv = k_ref[pl.ds(ks, tk), :].astype(BF16)
            vv = v_ref[pl.ds(ks, tk), :].astype(BF16)
            out = []
            for hh in range(len(masks)):
                acc, c_pos = carry[2 * hh], carry[2 * hh + 1]
                pos, ls = _softplus_parts(_dot(qm[hh], kv, _NT))
                if diag:
                    pos = jnp.where(strict, pos, 0.0)
                pos_after, pos_all = _scan_cols(pos, later, True)
                w = jnp.exp(ls - (pos_after + c_pos))
                if diag:
                    w = jnp.where(strict, w, 0.0)
                out += [acc + _dot(w.astype(BF16), vv), c_pos + pos_all]
            return tuple(out)

        init = (jnp.zeros((tq, LANES), F32), jnp.zeros((tq, 1), F32)) * len(masks)
        carry = tile(qi, init, True)
        carry = lax.fori_loop(0, qi, lambda i, c: tile(qi - 1 - i, c, False), carry)
        o_ref[...] = _merge(carry[0::2], masks).astype(o_ref.dtype)
        tot_ref[...] = _merge([jnp.broadcast_to(-c, (tq, LANES)) for c in carry[1::2]], masks)
        got[2]()

    c_ins, c_in_specs, c_out_specs, c_out_shape, c_scratch = _comm_args(comm)
    return pl.pallas_call(
        body, name=name, grid=(n_g, n_q),
        in_specs=[_cols(tq, qo), _cols_all(S, ko), _cols_all(S, vo)] + c_in_specs,
        out_specs=[_cols(tq, 0), _cols(tq, 0)] + c_out_specs,
        out_shape=[jax.ShapeDtypeStruct((S, width), BF16), jax.ShapeDtypeStruct((S, width), F32)] + c_out_shape,
        scratch_shapes=c_scratch,
        compiler_params=_params(("arbitrary", "arbitrary")),
    )(qa, ka, va, *c_ins)


def _comm_args(comm, aliased=False):
    if comm is None:
        return [], [], [], [], []
    assert aliased or not comm.aliases
    return comm.ins, [_ANY] * len(comm.ins), [_ANY] * len(comm.out_shape), comm.out_shape, comm.scratch


def _comm_aliases(comm, n_in, n_out):
    return {} if comm is None else {n_in + i: n_out + o for i, o in comm.aliases}


def _sbl_bwd(q, k, v, do, tot, *, width, hd, name, tq=256, comm=None):
    (qa, qo), (ka, ko), (va, vo) = q, k, v
    S = qa.shape[0]
    tq = _tile(S, tq)
    tk = tq
    scale = hd ** -0.5
    n_g, n_q = width // LANES, S // tq

    def body(*refs):
        qi = pl.program_id(1)
        gi = pl.program_id(0)
        got = _carry(comm, 5, 3, (gi == 0) & (qi == 0), (gi == n_g - 1) & (qi == n_q - 1), refs)
        (q_ref, k_ref, v_ref, do_ref, tot_ref), (dq_ref, dk_ref, dv_ref) = got[0], got[1]

        @pl.when(qi == 0)
        def _():
            dk_ref[...] = jnp.zeros_like(dk_ref)
            dv_ref[...] = jnp.zeros_like(dv_ref)

        masks = _lane_masks(hd, tq)
        qs = q_ref[...].astype(F32) * scale
        qm = [_keep(qs, m).astype(BF16) for m in masks]
        dov = [_keep(do_ref[...], m).astype(BF16) for m in masks]
        rest = [-_row_value(tot_ref[...], m) for m in masks]
        strict = lax.broadcasted_iota(jnp.int32, (tq, tk), 1) < lax.broadcasted_iota(jnp.int32, (tq, tk), 0)
        later, before = _tri("after", tk), _tri("before", tk)

        def tile(kb, carry, diag):
            ks = pl.multiple_of(kb * tk, tk)
            kv = k_ref[pl.ds(ks, tk), :].astype(BF16)
            vv = v_ref[pl.ds(ks, tk), :].astype(BF16)
            out = []
            dk_t, dv_t = None, None
            for hh in range(len(masks)):
                dq, c_pos, c_g = carry[3 * hh:3 * hh + 3]
                pos, ls = _softplus_parts(_dot(qm[hh], kv, _NT))
                if diag:
                    pos = jnp.where(strict, pos, 0.0)
                pos_after, pos_all = _scan_cols(pos, later, True)
                c_pos = c_pos + pos_all
                w = jnp.exp(ls - (pos_after + (rest[hh] - c_pos)))
                if diag:
                    w = jnp.where(strict, w, 0.0)
                g = _dot(dov[hh], vv, _NT) * w
                g_before, g_all = _scan_cols(g, before, False)
                g_before = g_before + c_g
                dz = g - jnp.exp(ls) * (g + g_before)
                if diag:
                    dz = jnp.where(strict, dz, 0.0)
                dzb = dz.astype(BF16)
                dk_h = _dot(dzb, qm[hh], _TN)
                dv_h = _dot(w.astype(BF16), dov[hh], _TN)
                dk_t = dk_h if dk_t is None else dk_t + dk_h
                dv_t = dv_h if dv_t is None else dv_t + dv_h
                out += [dq + _dot(dzb, kv), c_pos, c_g + g_all]
            dk_ref[pl.ds(ks, tk), :] += dk_t
            dv_ref[pl.ds(ks, tk), :] += dv_t
            return tuple(out)

        zero = jnp.zeros((tq, 1), F32)
        init = (jnp.zeros((tq, LANES), F32), zero, zero) * len(masks)
        carry = lax.fori_loop(0, qi, lambda kb, c: tile(kb, c, False), init)
        carry = tile(qi, carry, True)
        dq_ref[...] = _merge(carry[0::3], masks) * scale
        got[2]()

    full = jax.ShapeDtypeStruct((S, width), F32)
    c_ins, c_in_specs, c_out_specs, c_out_shape, c_scratch = _comm_args(comm)
    return pl.pallas_call(
        body, name=name, grid=(n_g, n_q),
        in_specs=[_cols(tq, qo), _cols_all(S, ko), _cols_all(S, vo), _cols(tq, do[1]), _cols(tq, tot[1])] + c_in_specs,
        out_specs=[_cols(tq, 0), _cols_all(S, 0), _cols_all(S, 0)] + c_out_specs,
        out_shape=[full, full, full] + c_out_shape,
        scratch_shapes=c_scratch,
        compiler_params=_params(("arbitrary", "arbitrary")),
    )(qa, ka, va, do[0], tot[0], *c_ins)


def _sml_fwd(q, k, v, bias=None, *, width, hd, causal, name, tq=256, tk=256, comm=None):
    (qa, qo), (ka, ko), (va, vo) = q, k, v
    S, Sk = qa.shape[0], ka.shape[0]
    tq, tk = _tile(S, tq), _tile(Sk, tk)
    if causal:
        assert tq == tk and S == Sk
    nk = Sk // tk
    hpg = LANES // hd
    scale = hd ** -0.5
    has_bias = bias is not None
    n_g, n_q = width // LANES, S // tq

    def body(*all_refs):
        qi, gi = pl.program_id(1), pl.program_id(0)
        got = _carry(comm, 5 if has_bias else 3, 2, (gi == 0) & (qi == 0), (gi == n_g - 1) & (qi == n_q - 1), all_refs)
        refs = tuple(got[0]) + tuple(got[1])
        q_ref, k_ref, v_ref = refs[:3]
        o_ref, lse_ref = refs[-2:]
        masks = _lane_masks(hd, tq)
        qs = q_ref[...].astype(F32) * scale
        qm = [_keep(qs, m).astype(BF16) for m in masks]
        allowed = lax.broadcasted_iota(jnp.int32, (tq, tk), 1) <= lax.broadcasted_iota(jnp.int32, (tq, tk), 0)

        def tile(kb, carry, diag):
            ks = pl.multiple_of(kb * tk, tk)
            kv = k_ref[pl.ds(ks, tk), :].astype(BF16)
            vv = v_ref[pl.ds(ks, tk), :].astype(BF16)
            out = []
            for hh in range(hpg):
                m, l, acc = carry[3 * hh:3 * hh + 3]
                z = _dot(qm[hh], kv, _NT)
                if has_bias:
                    z = z + refs[3][hh] - refs[4][hh, kb]
                if diag:
                    z = jnp.where(allowed, z, NEG_INF)
                m2 = jnp.maximum(m, jnp.max(z, axis=1, keepdims=True))
                p = jnp.exp(z - m2)
                alpha = jnp.exp(m - m2)
                out += [m2, alpha * l + jnp.sum(p, axis=1, keepdims=True), alpha * acc + _dot(p.astype(BF16), vv)]
            return tuple(out)

        init = (jnp.full((tq, 1), NEG_INF, F32), jnp.zeros((tq, 1), F32), jnp.zeros((tq, LANES), F32)) * hpg
        if causal:
            carry = lax.fori_loop(0, qi, lambda kb, c: tile(kb, c, False), init)
            carry = tile(qi, carry, True)
        else:
            carry = lax.fori_loop(0, nk, lambda kb, c: tile(kb, c, False), init)
        o_ref[...] = _merge([acc / l for l, acc in zip(carry[1::3], carry[2::3])], masks).astype(o_ref.dtype)
        lse_ref[...] = _merge([jnp.broadcast_to(m + jnp.log(l), (tq, LANES)) for m, l in zip(carry[0::3], carry[1::3])], masks)
        got[2]()

    in_specs = [_cols(tq, qo), _cols_all(Sk, ko), _cols_all(Sk, vo)]
    ins = [qa, ka, va]
    if has_bias:
        in_specs += [pl.BlockSpec((hpg, tq, 1), lambda g, i: (g, i, 0)),
                     pl.BlockSpec((hpg, nk, 1, tk), lambda g, i: (g, 0, 0, 0))]
        ins += list(bias)
    c_ins, c_in_specs, c_out_specs, c_out_shape, c_scratch = _comm_args(comm, aliased=True)
    return pl.pallas_call(
        body, name=name, grid=(n_g, n_q),
        in_specs=in_specs + c_in_specs, out_specs=[_cols(tq, 0), _cols(tq, 0)] + c_out_specs,
        out_shape=[jax.ShapeDtypeStruct((S, width), BF16), jax.ShapeDtypeStruct((S, width), F32)] + c_out_shape,
        input_output_aliases=_comm_aliases(comm, len(ins), 2),
        scratch_shapes=c_scratch,
        compiler_params=_params(("arbitrary", "arbitrary") if comm is not None else ("parallel", "arbitrary")),
    )(*ins, *c_ins)


def _sml_bwd(q, k, v, o, lse, do, bias=None, *, width, hd, causal, name, tq=256, tk=256, comm=None):
    (qa, qo), (ka, ko), (va, vo) = q, k, v
    S, Sk = qa.shape[0], ka.shape[0]
    tq, tk = _tile(S, tq), _tile(Sk, tk)
    nk = Sk // tk
    hpg = LANES // hd
    scale = hd ** -0.5
    has_bias = bias is not None
    n_in = 8 if has_bias else 6
    n_g, n_q = width // LANES, S // tq

    def body(*all_refs):
        qi, gi = pl.program_id(1), pl.program_id(0)
        got = _carry(comm, n_in, 5 if has_bias else 3, (gi == 0) & (qi == 0), (gi == n_g - 1) & (qi == n_q - 1), all_refs)
        refs = tuple(got[0]) + tuple(got[1])
        q_ref, k_ref, v_ref, o_ref, lse_ref, do_ref = refs[:6]
        dq_ref, dk_ref, dv_ref = refs[n_in:n_in + 3]

        @pl.when(qi == 0)
        def _():
            dk_ref[...] = jnp.zeros_like(dk_ref)
            dv_ref[...] = jnp.zeros_like(dv_ref)
            if has_bias:
                refs[n_in + 4][...] = jnp.zeros_like(refs[n_in + 4])

        masks = _lane_masks(hd, tq)
        qs = q_ref[...].astype(F32) * scale
        qm = [_keep(qs, m).astype(BF16) for m in masks]
        do32 = do_ref[...]
        dov = [_keep(do32, m).astype(BF16) for m in masks]
        prod = do32 * o_ref[...].astype(F32)
        delta = [jnp.sum(_keep(prod, m), axis=1, keepdims=True) for m in masks]
        lses = [_row_value(lse_ref[...], m) for m in masks]
        allowed = lax.broadcasted_iota(jnp.int32, (tq, tk), 1) <= lax.broadcasted_iota(jnp.int32, (tq, tk), 0)

        def tile(kb, carry, diag):
            ks = pl.multiple_of(kb * tk, tk)
            kv = k_ref[pl.ds(ks, tk), :].astype(BF16)
            vv = v_ref[pl.ds(ks, tk), :].astype(BF16)
            out = []
            dk_t, dv_t = None, None
            for hh in range(hpg):
                dq, db_row = carry[2 * hh:2 * hh + 2]
                z = _dot(qm[hh], kv, _NT)
                if has_bias:
                    z = z + refs[6][hh] - refs[7][hh, kb]
                p = jnp.exp(z - lses[hh])
                if diag:
                    p = jnp.where(allowed, p, 0.0)
                dz = p * (_dot(dov[hh], vv, _NT) - delta[hh])
                dzb = dz.astype(BF16)
                dk_h = _dot(dzb, qm[hh], _TN)
                dv_h = _dot(p.astype(BF16), dov[hh], _TN)
                dk_t = dk_h if dk_t is None else dk_t + dk_h
                dv_t = dv_h if dv_t is None else dv_t + dv_h
                if has_bias:
                    db_row = db_row + jnp.sum(dz, axis=1, keepdims=True)
                    refs[n_in + 4][hh, kb] += jnp.sum(dz, axis=0, keepdims=True)
                out += [dq + _dot(dzb, kv), db_row]
            dk_ref[pl.ds(ks, tk), :] += dk_t
            dv_ref[pl.ds(ks, tk), :] += dv_t
            return tuple(out)

        init = (jnp.zeros((tq, LANES), F32), jnp.zeros((tq, 1), F32)) * hpg
        if causal:
            carry = lax.fori_loop(0, qi, lambda kb, c: tile(kb, c, False), init)
            carry = tile(qi, carry, True)
        else:
            carry = lax.fori_loop(0, nk, lambda kb, c: tile(kb, c, False), init)
        dq_ref[...] = _merge(carry[0::2], masks) * scale
        if has_bias:
            for hh in range(hpg):
                refs[n_in + 3][hh] = carry[2 * hh + 1]
        got[2]()

    in_specs = [_cols(tq, qo), _cols_all(Sk, ko), _cols_all(Sk, vo), _cols(tq, o[1]), _cols(tq, lse[1]), _cols(tq, do[1])]
    ins = [qa, ka, va, o[0], lse[0], do[0]]
    out_specs = [_cols(tq, 0), _cols_all(Sk, 0), _cols_all(Sk, 0)]
    out_shape = [jax.ShapeDtypeStruct((S, width), F32), jax.ShapeDtypeStruct((Sk, width), F32),
                 jax.ShapeDtypeStruct((Sk, width), F32)]
    if has_bias:
        rspec = pl.BlockSpec((hpg, tq, 1), lambda g, i: (g, i, 0))
        cspec = pl.BlockSpec((hpg, nk, 1, tk), lambda g, i: (g, 0, 0, 0))
        in_specs += [rspec, cspec]
        ins += list(bias)
        out_specs += [rspec, cspec]
        n_heads = width // hd
        out_shape += [jax.ShapeDtypeStruct((n_heads, S, 1), F32), jax.ShapeDtypeStruct((n_heads, nk, 1, tk), F32)]
    c_ins, c_in_specs, c_out_specs, c_out_shape, c_scratch = _comm_args(comm)
    return pl.pallas_call(
        body, name=name, grid=(n_g, n_q),
        in_specs=in_specs + c_in_specs, out_specs=out_specs + c_out_specs, out_shape=out_shape + c_out_shape,
        scratch_shapes=c_scratch,
        compiler_params=_params(("arbitrary", "arbitrary") if comm is not None else ("parallel", "arbitrary")),
    )(*ins, *c_ins)


def _head_sums(t, masks):
    sums = [jnp.sum(_keep(t, m), axis=1, keepdims=True) for m in masks]
    return _merge([jnp.broadcast_to(s, t.shape) for s in sums], masks) if len(masks) > 1 else sums[0]


def _hnorm_fwd(x, g_lanes, *, width, hd, name, tr=1024):
    xa, xo = x
    R = xa.shape[0]
    tr = _tile(R, tr, align=16)
    n_blk = width // LANES

    def body(x_ref, g_ref, o_ref):
        masks = _lane_masks(hd, tr)
        for j in range(n_blk):
            sl = slice(j * LANES, (j + 1) * LANES)
            xv = x_ref[:, sl].astype(F32)
            r = lax.rsqrt(_head_sums(xv * xv, masks) * (1.0 / hd) + EPS)
            o_ref[:, sl] = (xv * r * g_ref[...]).astype(o_ref.dtype)

    assert (xo * LANES) % width == 0
    return pl.pallas_call(
        body, name=name, grid=(R // tr,),
        in_specs=[pl.BlockSpec((tr, width), lambda i: (i, xo * LANES // width)), pl.BlockSpec((1, LANES), lambda i: (0, 0))],
        out_specs=pl.BlockSpec((tr, width), lambda i: (i, 0)),
        out_shape=jax.ShapeDtypeStruct((R, width), BF16),
        compiler_params=_params(("parallel",)),
    )(xa, g_lanes)


def _hnorm_bwd(x, g_lanes, dy, *, width, hd, name, tr=1024):
    xa, xo = x
    R = xa.shape[0]
    tr = _tile(R, tr, align=16)
    n_blk = width // LANES

    def body(x_ref, g_ref, dy_ref, dx_ref, dg_ref):
        masks = _lane_masks(hd, tr)
        dg = jnp.zeros((1, LANES), F32)
        for j in range(n_blk):
            sl = slice(j * LANES, (j + 1) * LANES)
            xv = x_ref[:, sl].astype(F32)
            dyv = dy_ref[:, sl].astype(F32)
            r = lax.rsqrt(_head_sums(xv * xv, masks) * (1.0 / hd) + EPS)
            xh = xv * r
            dyg = dyv * g_ref[...]
            c = _head_sums(dyg * xh, masks) * (1.0 / hd)
            dx_ref[:, sl] = (r * (dyg - xh * c)).astype(dx_ref.dtype)
            dg = dg + jnp.sum(dyv * xh, axis=0, keepdims=True)
        if hd * 2 == LANES:
            dg8 = jnp.broadcast_to(dg, (8, LANES))
            dg = (dg8 + pltpu.roll(dg8, shift=hd, axis=1))[0:1]
        else:
            assert hd == LANES

        @pl.when(pl.program_id(0) == 0)
        def _():
            dg_ref[...] = jnp.zeros_like(dg_ref)

        dg_ref[...] += dg

    assert (xo * LANES) % width == 0
    return pl.pallas_call(
        body, name=name, grid=(R // tr,),
        in_specs=[pl.BlockSpec((tr, width), lambda i: (i, xo * LANES // width)), pl.BlockSpec((1, LANES), lambda i: (0, 0)),
                  pl.BlockSpec((tr, width), lambda i: (i, 0))],
        out_specs=[pl.BlockSpec((tr, width), lambda i: (i, 0)), pl.BlockSpec((1, LANES), lambda i: (0, 0))],
        out_shape=[jax.ShapeDtypeStruct((R, width), BF16), jax.ShapeDtypeStruct((1, LANES), F32)],
        compiler_params=_params(("arbitrary",)),
    )(xa, g_lanes, dy)


def _split3_dot(x, tri):
    a = x.astype(BF16)
    r = x - a.astype(F32)
    b = r.astype(BF16)
    c = (r - b.astype(F32)).astype(BF16)
    return _dot(a, tri) + _dot(b, tri) + _dot(c, tri)


def _forget_fwd(logit_t, b_col, *, name, blk=512):
    H, S = logit_t.shape
    blk = _tile(S, blk)

    def body(l_ref, b_ref, f_ref):
        r_i = lax.broadcasted_iota(jnp.int32, (blk, blk), 0)
        c_i = lax.broadcasted_iota(jnp.int32, (blk, blk), 1)
        upto = (r_i <= c_i).astype(BF16)
        carry = jnp.zeros((H, 1), F32)
        for j in range(S // blk):
            u = l_ref[:, j * blk:(j + 1) * blk] + b_ref[...]
            lf, _ = _log_sigmoid_pair(u)
            f_ref[:, j * blk:(j + 1) * blk] = _split3_dot(lf, upto) + carry
            carry = carry + jnp.sum(lf, axis=1, keepdims=True)

    return pl.pallas_call(
        body, name=name,
        out_shape=jax.ShapeDtypeStruct((H, S), F32),
        compiler_params=pltpu.CompilerParams(vmem_limit_bytes=VMEM_LIMIT),
    )(logit_t, b_col)


def _forget_bwd(logit_t, b_col, d_f, *, name, blk=512):
    H, S = logit_t.shape
    blk = _tile(S, blk)

    def body(l_ref, b_ref, df_ref, dl_ref, db_ref):
        r_i = lax.broadcasted_iota(jnp.int32, (blk, blk), 0)
        c_i = lax.broadcasted_iota(jnp.int32, (blk, blk), 1)
        fromon = (r_i >= c_i).astype(BF16)
        carry = jnp.zeros((H, 1), F32)
        db = jnp.zeros((H, 1), F32)
        for j in reversed(range(S // blk)):
            sl = slice(j * blk, (j + 1) * blk)
            dfv = df_ref[:, sl]
            d_lf = _split3_dot(dfv, fromon) + carry
            carry = carry + jnp.sum(dfv, axis=1, keepdims=True)
            u = l_ref[:, sl] + b_ref[...]
            _, lsn = _log_sigmoid_pair(u)
            dl = d_lf * jnp.exp(lsn)
            dl_ref[:, sl] = dl
            db = db + jnp.sum(dl, axis=1, keepdims=True)
        db_ref[...] = db

    return pl.pallas_call(
        body, name=name,
        out_shape=[jax.ShapeDtypeStruct((H, S), F32), jax.ShapeDtypeStruct((H, 1), F32)],
        compiler_params=pltpu.CompilerParams(vmem_limit_bytes=VMEM_LIMIT),
    )(logit_t, b_col, d_f)


def _sigmoid(t):
    return 1.0 / (1.0 + jnp.exp(-t))


def _gate_fwd(o3, w3, proj, x, w_out, g_norm, D, *, name, tm=512):
    S = proj.shape[0]
    tm = _tile(S, tm)

    def body(o0, o1, o2, w0, w1, w2, g0, g1, g2, x_ref, wo_ref, gn_ref, merged_ref, x1_ref, h2_ref):
        acc = None
        for o_ref, w_ref, g_ref in ((o0, w0, g0), (o1, w1, g1), (o2, w2, g2)):
            t = _sigmoid(g_ref[...]) * _dot(o_ref[...], w_ref[...])
            acc = t if acc is None else acc + t
        merged = acc.astype(BF16)
        merged_ref[...] = merged
        x1 = x_ref[...] + _dot(merged, wo_ref[...])
        x1_ref[...] = x1
        h2_ref[...] = (x1 * lax.rsqrt(jnp.mean(x1 * x1, axis=-1, keepdims=True) + EPS) * gn_ref[...]).astype(BF16)

    ospec = lambda d: pl.BlockSpec((tm, d), lambda i: (i, 0))
    wspec = lambda w: pl.BlockSpec(w.shape, lambda i: (0, 0))
    gspec = lambda j: pl.BlockSpec((tm, D), lambda i: (i, j))
    row = pl.BlockSpec((tm, D), lambda i: (i, 0))
    return pl.pallas_call(
        body, name=name, grid=(S // tm,),
        in_specs=[ospec(o.shape[1]) for o in o3] + [wspec(w) for w in w3] + [gspec(j) for j in range(3)]
        + [row, wspec(w_out), wspec(g_norm)],
        out_specs=[row, row, row],
        out_shape=[jax.ShapeDtypeStruct((S, D), BF16), jax.ShapeDtypeStruct((S, D), F32), jax.ShapeDtypeStruct((S, D), BF16)],
        compiler_params=_params(("parallel",)),
    )(*o3, *w3, proj, proj, proj, x, w_out, g_norm)


def _gate_bwd(o3, w3, proj, dx1, w_out, D, *, name, tm=256):
    S = proj.shape[0]
    tm = _tile(S, tm)

    def body(o0, o1, o2, w0, w1, w2, g0, g1, g2, dx_ref, wo_ref, dg_ref, dw0, dw1, dw2, do0, do1, do2):
        first = pl.program_id(0) == 0
        dm = _dot(dx_ref[...].astype(BF16), wo_ref[...], _NT)
        for j, (o_ref, w_ref, g_ref, dw_ref, do_ref) in enumerate(
                ((o0, w0, g0, dw0, do0), (o1, w1, g1, dw1, do1), (o2, w2, g2, dw2, do2))):
            s = _sigmoid(g_ref[...])
            br = _dot(o_ref[...], w_ref[...])
            dg_ref[:, j * D:(j + 1) * D] = (dm * br * s * (1.0 - s)).astype(dg_ref.dtype)
            dbr = (dm * s).astype(BF16)
            do_ref[...] = _dot(dbr, w_ref[...], _NT)
            part = _dot(o_ref[...], dbr, _TN)

            @pl.when(first)
            def _():
                dw_ref[...] = part

            @pl.when(jnp.logical_not(first))
            def _():
                dw_ref[...] += part

    ospec = lambda d: pl.BlockSpec((tm, d), lambda i: (i, 0))
    wspec = lambda w: pl.BlockSpec(w.shape, lambda i: (0, 0))
    gspec = lambda j: pl.BlockSpec((tm, D), lambda i: (i, j))
    dspec = pl.BlockSpec((tm, D), lambda i: (i, 0))
    return pl.pallas_call(
        body, name=name, grid=(S // tm,),
        in_specs=[ospec(o.shape[1]) for o in o3] + [wspec(w) for w in w3] + [gspec(j) for j in range(3)]
        + [dspec, wspec(w_out)],
        out_specs=[pl.BlockSpec((tm, 3 * D), lambda i: (i, 0))] + [wspec(w) for w in w3] + [ospec(o.shape[1]) for o in o3],
        out_shape=[jax.ShapeDtypeStruct((S, proj.shape[1]), BF16)] + [jax.ShapeDtypeStruct(w.shape, F32) for w in w3]
        + [jax.ShapeDtypeStruct((S, o.shape[1]), F32) for o in o3],
        compiler_params=_params(("arbitrary",)),
    )(*o3, *w3, proj, proj, proj, dx1, w_out)


def _pair_sum(stacked, got, core, *, name):
    n, _, r, c = stacked.shape
    tr = _row_tile(r, c, 3)

    def body(core_ref, a_ref, b_ref, o_ref):
        o_ref[...] = (a_ref[0].astype(F32) + b_ref[...].astype(F32)).astype(o_ref.dtype)

    spec = pl.BlockSpec((1, tr, c), lambda s, i, core_ref: (s, i, 0))
    return pl.pallas_call(
        body, name=name,
        grid_spec=pltpu.PrefetchScalarGridSpec(
            num_scalar_prefetch=1, grid=(n, r // tr),
            in_specs=[pl.BlockSpec((1, 1, tr, c), lambda s, i, core_ref: (s, core_ref[0], i, 0)), spec],
            out_specs=spec),
        out_shape=jax.ShapeDtypeStruct((n, r, c), BF16),
        compiler_params=_params(("parallel", "parallel")),
    )(core.astype(jnp.int32).reshape(1), stacked, got)


def _chip_sum(parts, got, chip, *, name):
    _, r, c = parts.shape
    tr = _row_tile(r, c, 6)

    def body(chip_ref, p_ref, q0_ref, q1_ref, q2_ref, o_ref):
        o_ref[...] = ((p_ref[0].astype(F32) + q0_ref[0].astype(F32)) + q1_ref[0].astype(F32)) + q2_ref[0].astype(F32)

    from_chip = lambda j: pl.BlockSpec((1, tr, c), lambda i, chip_ref: (j, i, 0))
    return pl.pallas_call(
        body, name=name,
        grid_spec=pltpu.PrefetchScalarGridSpec(
            num_scalar_prefetch=1, grid=(r // tr,),
            in_specs=[pl.BlockSpec((1, tr, c), lambda i, chip_ref: (chip_ref[0], i, 0))] + [from_chip(j) for j in range(3)],
            out_specs=pl.BlockSpec((tr, c), lambda i, chip_ref: (i, 0))),
        out_shape=jax.ShapeDtypeStruct((r, c), F32),
        compiler_params=_params(("parallel",)),
    )(chip.astype(jnp.int32).reshape(1), parts, got, got, got)


def _adamw_math(w, g, m, v):
    m2 = ADAM_B1 * m + (1.0 - ADAM_B1) * g
    v2 = ADAM_B2 * v + (1.0 - ADAM_B2) * (g * g)
    m_hat = m2 / (1.0 - ADAM_B1 ** ADAM_STEP)
    v_hat = v2 / (1.0 - ADAM_B2 ** ADAM_STEP)
    delta = -ADAM_LR * (m_hat / (jnp.sqrt(v_hat) + ADAM_EPS) + ADAM_WD * w)
    return delta, m2, v2


def _adamw(w, g, m, v, *, name):
    return _ew(_adamw_math, [w, g, m, v], (F32, F32, F32), name=name)


def _adamw_small(w, parts, m, v, *, name):
    n = parts.shape[0]

    def body(w_ref, p_ref, m_ref, v_ref, g_ref, d_ref, m2_ref, v2_ref):
        g = p_ref[0]
        for i in range(1, n):
            g = g + p_ref[i]
        g_ref[...] = g
        d_ref[...], m2_ref[...], v2_ref[...] = _adamw_math(w_ref[...], g, m_ref[...], v_ref[...])

    shp = jax.ShapeDtypeStruct(w.shape, F32)
    return pl.pallas_call(body, name=name, out_shape=[shp] * 4)(w, parts, m, v)


_ANY = pl.BlockSpec(memory_space=pl.ANY)


def _mesh_place():
    x, y, c = lax.axis_index("x"), lax.axis_index("y"), lax.axis_index("c")
    chips = [(1 - x, y), (x, 1 - y), (1 - x, 1 - y)]
    return x, y, c, chips


def _remote(src, dst, sems, i, to):
    send_sems, recv_sems = sems
    return pltpu.make_async_remote_copy(src_ref=src, dst_ref=dst, send_sem=send_sems.at[i], recv_sem=recv_sems.at[i],
                                        device_id=to, device_id_type=MESH_ID)


def _gather_weights(shards, *, name):
    n = len(shards)

    def body(*refs):
        ins, outs = refs[:n], refs[n:2 * n]
        sems = refs[2 * n:2 * n + 2]
        x, y, c, chips = _mesh_place()
        me = 2 * x + y
        sibling = (x, y, 1 - c)
        sent = []
        for w in range(n):
            for j, chip in enumerate(chips):
                cp = _remote(ins[w].at[c], outs[w].at[me, c], sems, 6 * w + j, (chip[0], chip[1], c))
                cp.start()
                sent.append(cp)
        for w in range(n):
            for j, chip in enumerate(chips):
                got = outs[w].at[2 * chip[0] + chip[1], c]
                _remote(got, got, sems, 6 * w + j, sibling).wait_recv()
                cp = _remote(got, got, sems, 6 * w + 3 + j, sibling)
                cp.start()
                sent.append(cp)
        for w in range(n):
            for j, chip in enumerate(chips):
                got = outs[w].at[2 * chip[0] + chip[1], 1 - c]
                _remote(got, got, sems, 6 * w + 3 + j, sibling).wait_recv()
        for cp in sent:
            cp.wait_send()

    outs = pl.pallas_call(
        body, name=name,
        in_specs=[_ANY] * n, out_specs=[_ANY] * n,
        out_shape=[jax.ShapeDtypeStruct((N_CHIPS,) + s.shape, s.dtype) for s in shards],
        scratch_shapes=[pltpu.SemaphoreType.DMA((6 * n,)), pltpu.SemaphoreType.DMA((6 * n,))],
    )(*shards)
    me = 2 * lax.axis_index("x") + lax.axis_index("y")
    return [lax.dynamic_update_index_in_dim(o, s, me, 0) for o, s in zip(outs, shards)]


def _forward_halves(gathered, *, name):
    n = len(gathered)

    def body(*refs):
        ins, outs = refs[:n], refs[n:2 * n]
        sems = refs[2 * n:2 * n + 2]
        x, y, c, chips = _mesh_place()
        sibling = (x, y, 1 - c)
        sent = []
        for w in range(n):
            for j, chip in enumerate(chips):
                peer = 2 * chip[0] + chip[1]
                cp = _remote(ins[w].at[peer, c], outs[w].at[peer, c], sems, 3 * w + j, sibling)
                cp.start()
                sent.append(cp)
        for w in range(n):
            for j, chip in enumerate(chips):
                land = outs[w].at[2 * chip[0] + chip[1], 1 - c]
                _remote(land, land, sems, 3 * w + j, sibling).wait_recv()
        for cp in sent:
            cp.wait_send()

    return pl.pallas_call(
        body, name=name,
        in_specs=[_ANY] * n, out_specs=[_ANY] * n,
        out_shape=[jax.ShapeDtypeStruct(g.shape, g.dtype) for g in gathered],
        input_output_aliases={w: w for w in range(n)},
        scratch_shapes=[pltpu.SemaphoreType.DMA((3 * n,)), pltpu.SemaphoreType.DMA((3 * n,))],
    )(*gathered)


def _exchange_siblings(grads, *, name):
    n = len(grads)

    def body(*refs):
        ins, got = refs[:n], refs[n:2 * n]
        sems = refs[2 * n:2 * n + 2]
        x, y, c, _ = _mesh_place()
        sibling = (x, y, 1 - c)
        sent = []
        for w in range(n):
            for s in range(N_CHIPS):
                cp = _remote(ins[w].at[s, 1 - c], got[w].at[s], sems, N_CHIPS * w + s, sibling)
                cp.start()
                sent.append(cp)
        for w in range(n):
            for s in range(N_CHIPS):
                _remote(got[w].at[s], got[w].at[s], sems, N_CHIPS * w + s, sibling).wait_recv()
        for cp in sent:
            cp.wait_send()

    n_sem = N_CHIPS * n
    return pl.pallas_call(
        body, name=name,
        in_specs=[_ANY] * n, out_specs=[_ANY] * n,
        out_shape=[jax.ShapeDtypeStruct((N_CHIPS,) + g.shape[2:], g.dtype) for g in grads],
        scratch_shapes=[pltpu.SemaphoreType.DMA((n_sem,)), pltpu.SemaphoreType.DMA((n_sem,))],
    )(*grads)


def _share_halves(halves, small):
    n = len(halves)

    def body(*refs):
        ins, small_ref = refs[:n], refs[n]
        outs, small_out = refs[n + 1:2 * n + 1], refs[2 * n + 1]
        sems = refs[2 * n + 2:2 * n + 4]
        x, y, c, chips = _mesh_place()
        sibling = (x, y, 1 - c)
        me = 4 * x + 2 * y + c
        sent = [_remote(ins[w], outs[w].at[c], sems, w, sibling) for w in range(n)]
        peers = [sibling] + [(ch[0], ch[1], cc) for ch in chips for cc in (c, 1 - c)]
        sent += [_remote(small_ref, small_out.at[me], sems, n + j, peer) for j, peer in enumerate(peers)]
        for cp in sent:
            cp.start()
        for w in range(n):
            _remote(outs[w].at[1 - c], outs[w].at[1 - c], sems, w, sibling).wait_recv()
        for j, peer in enumerate(peers):
            frm = small_out.at[4 * peer[0] + 2 * peer[1] + peer[2]]
            _remote(frm, frm, sems, n + j, peer).wait_recv()
        for cp in sent:
            cp.wait_send()

    n_sem = n + 7
    outs = pl.pallas_call(
        body, name="share_halves",
        in_specs=[_ANY] * (n + 1), out_specs=[_ANY] * (n + 1),
        out_shape=[jax.ShapeDtypeStruct((2,) + h.shape, h.dtype) for h in halves]
        + [jax.ShapeDtypeStruct((8,) + small.shape, small.dtype)],
        scratch_shapes=[pltpu.SemaphoreType.DMA((n_sem,)), pltpu.SemaphoreType.DMA((n_sem,))],
    )(*halves, small)
    c = lax.axis_index("c")
    me = 4 * lax.axis_index("x") + 2 * lax.axis_index("y") + c
    return ([lax.dynamic_update_index_in_dim(o, h, c, 0) for o, h in zip(outs[:n], halves)],
            lax.dynamic_update_index_in_dim(outs[n], small, me, 0))


EARLY = ("w_ff_down", "w_ff_up", "w_out", "w_branch_sb", "w_branch_fox", "w_branch_mem", "w_mem_kv")


def _norm_bwd_tail(dy, x, add, g):
    r = lax.rsqrt(jnp.mean(x * x, axis=-1, keepdims=True) + EPS)
    xh = x * r
    dyg = dy * g
    c = jnp.mean(dyg * xh, axis=-1, keepdims=True)
    return r * (dyg - xh * c) + add, jnp.sum(dy * xh, axis=0, keepdims=True)


def _split(outs, n):
    outs = list(outs) if isinstance(outs, (list, tuple)) else [outs]
    return outs[:n], outs[n:]


def _local_step(x, mem, target, small, W, gather_rest=None, reduce_early=None, reduce_late=None):
    S, D = x.shape
    o_qkv, o_mq, o_f = 3 * D, 3 * D + 2 * 3 * D_SB, 3 * D + 2 * 3 * D_SB + D_MEM
    tq = 512

    g_comms, finish_weights = gather_rest if gather_rest is not None else ([None] * 3, None)
    (proj, h), landed = _split(_mm(x, W["w_in"], name="in_proj", tb=True, tn=768, a_gain=small["g_mix_norm"],
                                   comm=g_comms[0]), 2)
    blk = lambda j: (proj, (o_qkv + j * D_SB) // LANES)
    sb_q, sb_k, sb_v, fx_q, fx_k, fx_v = [blk(j) for j in range(6)]
    m_q = (proj, o_mq // LANES)
    f_logit_t = _mm(W["w_in"][o_f:o_f + ROW_TILE], h, name="forget_logits", tb=True)[:FOX_HEADS]
    b_col = small["b_forget"].reshape(FOX_HEADS, 1)
    lanes = lambda g: jnp.tile(g, (1, LANES // g.shape[1]))
    g_fq, g_fk, g_mq, g_mk = [lanes(small[k]) for k in ("g_fox_q", "g_fox_k", "g_mem_q", "g_mem_k")]

    (o_sb, sb_tot), more = _split(_sbl_fwd(sb_q, sb_k, sb_v, width=D_SB, hd=HD, name="sb_fwd", tq=tq, comm=g_comms[1]), 2)
    landed += more

    fq = _hnorm_fwd(fx_q, g_fq, width=D_FOX, hd=HD, name="fox_q_norm")
    fk = _hnorm_fwd(fx_k, g_fk, width=D_FOX, hd=HD, name="fox_k_norm")
    f_cum = _forget_fwd(f_logit_t, b_col, name="forget_fwd")
    tkf = _tile(S, tq)
    f_bias = (f_cum.reshape(FOX_HEADS, S, 1), f_cum.reshape(FOX_HEADS, S // tkf, 1, tkf))
    fox_comm = None if finish_weights is None else _Joint([g_comms[2], _ChipExchange("forward", landed)])
    (o_fox, fox_lse), more = _split(_sml_fwd((fq, 0), (fk, 0), fx_v, f_bias, width=D_FOX, hd=HD, causal=True,
                                             name="fox_fwd", tq=tq, tk=tq, comm=fox_comm), 2)
    if finish_weights is not None:
        n_last = len(g_comms[2].ins)
        W = {**W, **finish_weights(more[:n_last], more[n_last:])}

    mh = _rmsnorm_fwd(mem, small["g_mem_norm"], BF16, name="mem_norm")
    mkv = _mm(mh, W["w_mem_kv"], name="mem_kv")
    mv = (mkv, D_MEM // LANES)
    mq = _hnorm_fwd(m_q, g_mq, width=D_MEM, hd=MEM_HD, name="mem_q_norm")
    mk = _hnorm_fwd((mkv, 0), g_mk, width=D_MEM, hd=MEM_HD, name="mem_k_norm")
    o_mem, mem_lse = _sml_fwd((mq, 0), (mk, 0), mv, width=D_MEM, hd=MEM_HD, causal=False, name="mem_fwd", tq=tq, tk=256)

    o3 = [o_sb, o_fox, o_mem]
    w3 = [W["w_branch_sb"], W["w_branch_fox"], W["w_branch_mem"]]
    merged, x1, h2 = _gate_fwd(o3, w3, proj, x, W["w_out"], small["g_mlp_norm"], D, name="gate_fwd")

    def relu2(acc):
        u = jnp.maximum(acc, 0.0)
        return u, u * u

    u, a = _mm(h2, W["w_ff_up"], name="ff_up", out_dtypes=(BF16, BF16), epilogue=relu2)
    def head(acc, res, tgt):
        d = (res + acc - tgt) * (1.0 / D)
        return d, d, jnp.sum(d * d, axis=0, keepdims=True)

    dy, dy16, sq_rows = _mm(a, W["w_ff_down"], name="ff_down", extras=(x1, target), out_dtypes=(F32, BF16),
                            epilogue=head, col_sums=1, tn=512)
    loss = (0.5 * D) * jnp.sum(sq_rows)

    G = {}
    du = _mm(dy16, W["w_ff_down"], name="d_ff_act", tb=True, out_dtypes=(BF16,), extras=(u,),
             epilogue=lambda acc, uu: (acc * (2.0 * uu.astype(F32)),))
    G["w_ff_down"] = _mm(a, dy16, name="d_w_ff_down", ta=True, out_dtypes=(BF16,))
    G["w_ff_up"] = _mm(h2, du, name="d_w_ff_up", ta=True, out_dtypes=(BF16,))
    dx1, dg_rows = _mm(du, W["w_ff_up"], name="d_mlp_in", tb=True, extras=(x1, dy, small["g_mlp_norm"]),
                       epilogue=_norm_bwd_tail, col_sums=1, tm=512, tn=D)
    dg_mlp = jnp.sum(dg_rows, axis=0, keepdims=True)
    G["w_out"] = _mm(merged, dx1, name="d_w_out", ta=True, out_dtypes=(BF16,))
    dgate, dw0, dw1, dw2, do_sb, do_fox, do_mem = _gate_bwd(o3, w3, proj, dx1, W["w_out"], D, name="gate_bwd")
    for nm, dw in zip(("w_branch_sb", "w_branch_fox", "w_branch_mem"), (dw0, dw1, dw2)):
        G[nm] = dw.astype(BF16)

    sib_comm, after_siblings = (reduce_early({k: G.pop(k) for k in EARLY if k != "w_mem_kv"})
                                if reduce_early is not None else (None, None))
    (dmq_n, dmk_n, dmv), landed_sib = _split(
        _sml_bwd((mq, 0), (mk, 0), mv, (o_mem, 0), (mem_lse, 0), (do_mem, 0), width=D_MEM, hd=MEM_HD, causal=False,
                 name="mem_bwd", tq=tq, tk=256, comm=sib_comm), 3)
    dm_q, dg_mem_q = _hnorm_bwd(m_q, g_mq, dmq_n, width=D_MEM, hd=MEM_HD, name="d_mem_q_norm")
    dmk_raw, dg_mem_k = _hnorm_bwd((mkv, 0), g_mk, dmk_n, width=D_MEM, hd=MEM_HD, name="d_mem_k_norm")
    dmkv = jnp.concatenate([dmk_raw, dmv.astype(BF16)], axis=1)
    G["w_mem_kv"] = _mm(mh, dmkv, name="d_w_mem_kv", ta=True, out_dtypes=(BF16,))
    dmh = _mm(dmkv, W["w_mem_kv"], name="d_mem_h", tb=True)
    _, dg_mem = _rmsnorm_bwd(mem, small["g_mem_norm"], dmh, name="d_mem_norm")

    r_comms, r_finish = (after_siblings(landed_sib, {"w_mem_kv": G.pop("w_mem_kv")})
                         if after_siblings is not None else ([None] * 2, None))
    dsb, landed_sb = _split(_sbl_bwd(sb_q, sb_k, sb_v, (do_sb, 0), (sb_tot, 0), width=D_SB, hd=HD, name="sb_bwd", tq=tq,
                                     comm=r_comms[0]), 3)
    (dfq, dfk, dfv, df_row, df_col), landed_fox = _split(
        _sml_bwd((fq, 0), (fk, 0), fx_v, (o_fox, 0), (fox_lse, 0), (do_fox, 0), f_bias, width=D_FOX, hd=HD, causal=True,
                 name="fox_bwd", tq=tq, tk=tq, comm=r_comms[1]), 5)
    early = r_finish(landed_sb, landed_fox) if r_finish is not None else {}
    dfx_q, dg_fox_q = _hnorm_bwd(fx_q, g_fq, dfq, width=D_FOX, hd=HD, name="d_fox_q_norm")
    dfx_k, dg_fox_k = _hnorm_bwd(fx_k, g_fk, dfk, width=D_FOX, hd=HD, name="d_fox_k_norm")
    d_fcum = df_row.reshape(FOX_HEADS, S) - df_col.reshape(FOX_HEADS, S)
    d_flogit_t, db_forget = _forget_bwd(f_logit_t, b_col, d_fcum, name="forget_bwd")
    dg_fox_q, dg_fox_k = dg_fox_q[:, :HD], dg_fox_k[:, :HD]

    rest_cols = jnp.concatenate([t.astype(BF16) for t in (*dsb, dfx_q, dfx_k, dfv, dm_q)]
                                + [d_flogit_t.T.astype(BF16), jnp.zeros((S, F_PAD - FOX_HEADS), BF16)], axis=1)
    dproj = lax.dynamic_update_slice(dgate, rest_cols, (0, 3 * D))
    g_w_in = _mm(dproj, h, name="d_w_in", ta=True, out_dtypes=(BF16,), tm=768)
    comm, finish = reduce_late({"w_in": g_w_in}) if reduce_late is not None else (None, None)
    (grad_x, dg_rows), landed = _split(
        _mm(dproj, W["w_in"], name="d_mix_in", tk=2304, tm=512, tn=D, extras=(x, dx1, small["g_mix_norm"]),
            epilogue=_norm_bwd_tail, col_sums=1, comm=comm), 2)
    dg_mix = jnp.sum(dg_rows, axis=0, keepdims=True)
    if finish is None:
        G["w_in"] = g_w_in
    else:
        early.update(finish(landed))

    small_grads = dict(g_mix_norm=dg_mix, g_mem_norm=dg_mem, b_forget=db_forget.reshape(1, FOX_HEADS),
                       g_fox_q=dg_fox_q, g_fox_k=dg_fox_k, g_mem_q=dg_mem_q, g_mem_k=dg_mem_k, g_mlp_norm=dg_mlp)
    return loss, grad_x, G, small_grads, early


BIG = ("w_in", "w_mem_kv", "w_branch_sb", "w_branch_fox", "w_branch_mem", "w_out", "w_ff_up", "w_ff_down")
COLUMN_SHARDED = ("w_in", "w_branch_sb", "w_branch_fox", "w_branch_mem", "w_ff_up")
SMALL = ("g_mix_norm", "g_mem_norm", "b_forget", "g_fox_q", "g_fox_k", "g_mem_q", "g_mem_k", "g_mlp_norm")
ORDER = ("g_mix_norm", "g_mem_norm", "w_in", "b_forget", "g_fox_q", "g_fox_k", "g_mem_q", "g_mem_k", "w_mem_kv",
         "w_branch_sb", "w_branch_fox", "w_branch_mem", "w_out", "g_mlp_norm", "w_ff_up", "w_ff_down")


def _unshard(name, gathered):
    n, _, rh, c = gathered.shape
    t = gathered.reshape(n, 2 * rh, c)
    if name in COLUMN_SHARDED:
        return t.transpose(1, 0, 2).reshape(2 * rh, n * c)
    return t.reshape(n * 2 * rh, c)


def _reshard(name, full):
    if name in COLUMN_SHARDED:
        r, c = full.shape
        t = full.reshape(r, N_CHIPS, c // N_CHIPS).transpose(1, 0, 2)
    else:
        r, c = full.shape[0] // N_CHIPS, full.shape[1]
        t = full.reshape(N_CHIPS, r, c)
    return t.reshape(N_CHIPS, 2, t.shape[1] // 2, t.shape[2])


ROW_TILE = 16
IN_BUF_ALIGN = 256


def _in_segments(D):
    n_qkv = 6 * D_SB
    o_mq, o_gate = n_qkv + FOX_HEADS, n_qkv + FOX_HEADS + D_MEM
    return [(0, n_qkv, 3 * D), (n_qkv, o_mq, 3 * D + n_qkv + D_MEM), (o_mq, o_gate, 3 * D + n_qkv), (o_gate, o_gate + 3 * D, 0)]


class _InLayout:
    def __init__(self, D, shard, n):
        self.D, self.shard, self.n = D, shard, n
        down = lambda v: v // ROW_TILE * ROW_TILE
        up = lambda v: -(-v // ROW_TILE) * ROW_TILE
        self.pieces = []
        ends = []
        for s in range(n):
            cursor, mine = 0, []
            for a, b, p in _in_segments(D):
                x0, x1 = max(a, s * shard), min(b, (s + 1) * shard)
                if x0 < x1:
                    p0 = p + x0 - a
                    rows = up(p0 + x1 - x0) - down(p0)
                    mine.append((x0 - s * shard, x1 - x0, p0, cursor, rows))
                    cursor += rows
            self.pieces.append(mine)
            ends.append(cursor)
        self.rows = -(-max(ends) // IN_BUF_ALIGN) * IN_BUF_ALIGN
        self.padded_rows = 3 * D + 6 * D_SB + D_MEM + F_PAD

    def _per_shard(self, fn, chip, operand):
        return lax.switch(chip, [functools.partial(fn, s) for s in range(self.n)], operand)

    def pack(self, chip, rows):
        def one(s, t):
            out, at = [], 0
            for x0, n_rows, p0, start, region in self.pieces[s]:
                lead = p0 % ROW_TILE
                out += [jnp.zeros((start + lead - at, t.shape[1]), t.dtype), t[x0:x0 + n_rows]]
                at = start + lead + n_rows
            return jnp.concatenate(out + [jnp.zeros((self.rows - at, t.shape[1]), t.dtype)], axis=0)
        return self._per_shard(one, chip, rows)

    def unpack(self, chip, buf, pad_to):
        def one(s, t):
            out = [t[start + p0 % ROW_TILE:start + p0 % ROW_TILE + n_rows] for _, n_rows, p0, start, _ in self.pieces[s]]
            return jnp.concatenate(out + [jnp.zeros((pad_to - self.shard, t.shape[1]), t.dtype)], axis=0)
        return self._per_shard(one, chip, buf)

    def to_padded(self, bufs):
        runs = sorted((p0, s, start, region) for s in range(self.n) for _, _, p0, start, region in self.pieces[s])
        chunks, end = [], 0
        for p0, s, start, region in runs:
            d0 = p0 // ROW_TILE * ROW_TILE
            src = bufs[s, start:start + region]
            if d0 < end:
                assert end - d0 == ROW_TILE
                last = chunks.pop()
                chunks += [last[:-ROW_TILE], last[-ROW_TILE:] + src[:ROW_TILE], src[ROW_TILE:]]
            else:
                if d0 > end:
                    chunks.append(jnp.zeros((d0 - end, bufs.shape[2]), bufs.dtype))
                chunks.append(src)
            end = d0 + region
        chunks.append(jnp.zeros((self.padded_rows - end, bufs.shape[2]), bufs.dtype))
        return jnp.concatenate(chunks, axis=0)

    def from_padded(self, gp):
        bufs = []
        for s in range(self.n):
            out, at = [], 0
            for _, n_rows, p0, start, region in self.pieces[s]:
                d0 = p0 // ROW_TILE * ROW_TILE
                row = d0 + lax.broadcasted_iota(jnp.int32, (region, 1), 0)
                out.append(jnp.where((row >= p0) & (row < p0 + n_rows), gp[d0:d0 + region], jnp.zeros((), gp.dtype)))
                at = start + region
            bufs.append(jnp.concatenate(out + [jnp.zeros((self.rows - at, gp.shape[1]), gp.dtype)], axis=0))
        return jnp.stack(bufs)


SMALL_ROWS = 16


def _pack_small(vals, scalar=None):
    width = max(vals[k].shape[1] for k in SMALL)
    rows = [jnp.pad(vals[k].astype(F32), ((0, 0), (0, width - vals[k].shape[1]))) for k in SMALL]
    extra = jnp.zeros((SMALL_ROWS - len(SMALL), width), F32)
    if scalar is not None:
        extra = extra.at[0, 0].set(scalar)
    return jnp.concatenate(rows + [extra], axis=0)


def _unpack_small(packed, like):
    return {k: packed[i:i + 1, :like[k].shape[1]] for i, k in enumerate(SMALL)}


def kernel(x, mem, g_mix_norm, g_mem_norm, w_in, b_forget, g_fox_q, g_fox_k, g_mem_q, g_mem_k, w_mem_kv, w_branch_sb, w_branch_fox, w_branch_mem, w_out, g_mlp_norm, w_ff_up, w_ff_down, loss_target, m_g_mix_norm, m_g_mem_norm, m_w_in, m_b_forget, m_g_fox_q, m_g_fox_k, m_g_mem_q, m_g_mem_k, m_w_mem_kv, m_w_branch_sb, m_w_branch_fox, m_w_branch_mem, m_w_out, m_g_mlp_norm, m_w_ff_up, m_w_ff_down, v_g_mix_norm, v_g_mem_norm, v_w_in, v_b_forget, v_g_fox_q, v_g_fox_k, v_g_mem_q, v_g_mem_k, v_w_mem_kv, v_w_branch_sb, v_w_branch_fox, v_w_branch_mem, v_w_out, v_g_mlp_norm, v_w_ff_up, v_w_ff_down):
    given = dict(locals())
    D = x.shape[-1]
    weights = {k: given[k] for k in ORDER}
    moms = {k: given["m_" + k] for k in ORDER}
    vars_ = {k: given["v_" + k] for k in ORDER}

    me_chip = 2 * lax.axis_index("x") + lax.axis_index("y")

    n_in = w_in.shape[2]
    lay = _InLayout(D, n_in, N_CHIPS)
    transposed = lambda t: jnp.transpose(t[0])
    shards = {}
    for k in BIG:
        w = weights[k][0].astype(BF16)
        if k == "w_in":
            w = lay.pack(me_chip, jnp.transpose(w))
        shards[k] = w.reshape(2, w.shape[0] // 2, w.shape[1])
    gathered_in = _gather_weights([shards["w_in"]], name="gather_w_in")[0]
    W = {"w_in": lay.to_padded(gathered_in.reshape(N_CHIPS, lay.rows, D))}
    carried = (("w_branch_sb", "w_branch_fox", "w_branch_mem", "w_out"), ("w_ff_up", "w_mem_kv"), ("w_ff_down",))
    rest = [k for grp in carried for k in grp]
    assert sorted(rest + ["w_in"]) == sorted(BIG)

    def finish_weights(landed_last, forwarded):
        full = list(forwarded) + list(_forward_halves(landed_last, name="forward_halves"))
        full = [lax.dynamic_update_index_in_dim(o, shards[k], me_chip, 0) for k, o in zip(rest, full)]
        return {k: _unshard(k, g) for k, g in zip(rest, full)}

    me_core = lax.axis_index("c")

    def sum_chips(names, parts, got):
        return {k: _chip_sum(p, q, me_chip, name="sum_chips_" + k) for k, p, q in zip(names, parts, got)}

    def pair_sums(grads, tag):
        names = list(grads)
        stacked = {k: _reshard(k, grads[k]) for k in names if k != "w_in"}
        if "w_in" in grads:
            stacked["w_in"] = lay.from_padded(grads["w_in"]).reshape(N_CHIPS, 2, lay.rows // 2, D)
        got = _exchange_siblings([stacked[k] for k in names], name="exchange_siblings_" + tag)
        return {k: _pair_sum(stacked[k], q, me_core, name="sum_pair_" + k) for k, q in zip(names, got)}

    def reduce_early(grads):
        names = list(grads)
        stacked = [_reshard(k, grads[k]) for k in names]

        def after_siblings(landed, more):
            parts = {k: _pair_sum(t, q, me_core, name="sum_pair_" + k) for k, t, q in zip(names, stacked, landed)}
            parts.update(pair_sums(more, "early"))
            groups = [[k for k in parts if k in ("w_ff_down", "w_ff_up")], [k for k in parts if k not in ("w_ff_down", "w_ff_up")]]
            comms = [_ChipExchange("scatter", [parts[k] for k in grp]) for grp in groups]

            def finish(*got):
                out = {}
                for grp, q in zip(groups, got):
                    out.update(sum_chips(grp, [parts[k] for k in grp], q))
                return out
            return comms, finish
        return _ChipExchange("siblings", stacked), after_siblings

    def reduce_late(grads):
        parts = pair_sums(grads, "late")
        names = list(parts)
        return _ChipExchange("scatter", [parts[k] for k in names]), functools.partial(sum_chips, names, [parts[k] for k in names])

    small = {k: weights[k] for k in SMALL}
    loss_part, grad_x, G, small_grads, halves = _local_step(
        x[0], mem[0], loss_target[0], small, W,
        gather_rest=([_ChipExchange("gather", [shards[k] for k in grp]) for grp in carried], finish_weights),
        reduce_early=reduce_early, reduce_late=reduce_late)
    assert not G, list(G)
    reduced, small_parts = _share_halves([halves[k] for k in BIG], _pack_small(small_grads, loss_part))

    grads, deltas, new_m, new_v = {}, {}, {}, {}
    for k, g in zip(BIG, reduced):
        shp = weights[k].shape
        if k == "w_in":
            g2 = lay.unpack(me_chip, g.reshape(lay.rows, D), lay.rows)
            padded = lambda t: jnp.pad(transposed(t), ((0, lay.rows - n_in), (0, 0)))
            outs = _adamw(padded(weights[k]), g2, padded(moms[k]), padded(vars_[k]), name="adamw_" + k)
            g2, d, m2, v2 = [jnp.transpose(t[:n_in]) for t in (g2, *outs)]
        else:
            g2 = g.reshape(shp[1], shp[2])
            d, m2, v2 = _adamw(weights[k][0], g2, moms[k][0], vars_[k][0], name="adamw_" + k)
        grads[k], deltas[k], new_m[k], new_v[k] = g2.reshape(shp), d.reshape(shp), m2.reshape(shp), v2.reshape(shp)
    sg, sd, sm, sv = _adamw_small(_pack_small(small), small_parts, _pack_small({k: moms[k] for k in SMALL}),
                                  _pack_small({k: vars_[k] for k in SMALL}), name="adamw_small")
    for dst, packed in ((grads, sg), (deltas, sd), (new_m, sm), (new_v, sv)):
        dst.update(_unpack_small(packed, small))

    loss = sg[len(SMALL), 0]
    return (loss, grad_x[None], *[grads[k] for k in ORDER], *[deltas[k] for k in ORDER],
            *[new_m[k] for k in ORDER], *[new_v[k] for k in ORDER])
```

```python
import functools

import jax
import jax.numpy as jnp
from jax import lax
from jax.experimental import pallas as pl
from jax.experimental.pallas import tpu as pltpu

F32 = jnp.float32
BF16 = jnp.bfloat16
MESH_ID = pl.DeviceIdType.MESH

HD = 64
SB_HEADS = 8
FOX_HEADS = 8
MEM_HEADS = 4
MEM_HD = 128
D_SB = SB_HEADS * HD
D_FOX = FOX_HEADS * HD
D_MEM = MEM_HEADS * MEM_HD
EPS = 1e-6
NEG_INF = -1e30

ADAM_LR = 0.001
ADAM_B1 = 0.9
ADAM_B2 = 0.999
ADAM_EPS = 1e-08
ADAM_WD = 0.01
ADAM_STEP = 10

N_CHIPS = 4
VMEM_LIMIT = 56 * 1024 * 1024

F_PAD = 256


def _tile(n, target, align=128):
    if n <= target:
        return n
    best = None
    t = align
    while t <= target:
        if n % t == 0:
            best = t
        t += align
    assert best is not None, (n, target, align)
    return best


def _params(sem):
    return pltpu.CompilerParams(dimension_semantics=sem, vmem_limit_bytes=VMEM_LIMIT)


def _mm(a, b, *, name, ta=False, tb=False, out_dtypes=(F32,), epilogue=None, extras=(),
        tm=1024, tn=1024, tk=2048, comm=None, col_sums=0, a_gain=None):
    if ta:
        K, M = a.shape
    else:
        M, K = a.shape
    if tb:
        N, K2 = b.shape
    else:
        K2, N = b.shape
    assert K == K2, (a.shape, b.shape, ta, tb)
    tm, tn, tk = _tile(M, tm), _tile(N, tn), _tile(K, tk)
    nk = K // tk
    normed = a_gain is not None
    assert not normed or (nk == 1 and not ta)
    n_extra, n_out = len(extras) + normed, len(out_dtypes) + col_sums + normed
    if epilogue is None:
        epilogue = lambda acc: (acc,)
    dims = (((0 if ta else 1,), (1 if tb else 0,)), ((), ()))

    gm, gn = M // tm, N // tn

    def body(*refs):
        i, j, k = pl.program_id(0), pl.program_id(1), pl.program_id(2)
        got = _carry(comm, 2 + n_extra, n_out, (i == 0) & (j == 0) & (k == 0),
                     (i == gm - 1) & (j == gn - 1) & (k == nk - 1), refs)
        (a_ref, b_ref, *extra_refs), out_refs = got[0], list(got[1])
        if normed:
            gain_ref, normed_ref = extra_refs.pop(), out_refs.pop()

            @pl.when(j == 0)
            def _():
                xa = a_ref[...].astype(F32)
                normed_ref[...] = (xa * lax.rsqrt(jnp.mean(xa * xa, axis=-1, keepdims=True) + EPS) * gain_ref[...]).astype(BF16)

            av = normed_ref[...]
        else:
            av = a_ref[...].astype(BF16)
        part = lax.dot_general(av, b_ref[...].astype(BF16), dims, preferred_element_type=F32)

        def finish(acc):
            outs = epilogue(acc, *[r[...] for r in extra_refs])
            for o_ref, o in zip(out_refs[:len(out_dtypes)], outs):
                o_ref[...] = o.astype(o_ref.dtype)
            for o_ref, o in zip(out_refs[len(out_dtypes):], outs[len(out_dtypes):]):
                first = lax.broadcasted_iota(jnp.int32, o_ref.shape, 0) == 0
                o_ref[...] = jnp.where(first, jnp.broadcast_to(o, o_ref.shape), 0.0)

        if nk == 1:
            finish(part)
        else:
            acc_ref = refs[-1]

            @pl.when(k == 0)
            def _():
                acc_ref[...] = part

            @pl.when((k > 0) & (k < nk - 1))
            def _():
                acc_ref[...] += part

            @pl.when(k == nk - 1)
            def _():
                finish(acc_ref[...] + part)

        got[2]()

    a_spec = pl.BlockSpec((tk, tm), lambda i, j, k: (k, i)) if ta else pl.BlockSpec((tm, tk), lambda i, j, k: (i, k))
    b_spec = pl.BlockSpec((tn, tk), lambda i, j, k: (j, k)) if tb else pl.BlockSpec((tk, tn), lambda i, j, k: (k, j))
    mn_spec = pl.BlockSpec((tm, tn), lambda i, j, k: (i, j))
    row_spec = pl.BlockSpec((1, tn), lambda i, j, k: (0, j))
    sum_spec = pl.BlockSpec((8, tn), lambda i, j, k: (i, j))
    c_ins, c_in_specs, c_out_specs, c_out_shape, c_scratch = _comm_args(comm)
    sem = ("parallel", "arbitrary" if normed else "parallel", "arbitrary") if comm is None else ("arbitrary",) * 3
    outs = pl.pallas_call(
        body, name=name,
        grid=(gm, gn, nk),
        in_specs=[a_spec, b_spec] + [row_spec if e.shape[0] == 1 else mn_spec for e in extras]
        + ([pl.BlockSpec((1, K), lambda i, j, k: (0, 0))] if normed else []) + c_in_specs,
        out_specs=[mn_spec] * len(out_dtypes) + [sum_spec] * col_sums
        + ([pl.BlockSpec((tm, K), lambda i, j, k: (i, 0))] if normed else []) + c_out_specs,
        out_shape=[jax.ShapeDtypeStruct((M, N), dt) for dt in out_dtypes]
        + [jax.ShapeDtypeStruct((8 * gm, N), F32)] * col_sums
        + ([jax.ShapeDtypeStruct((M, K), BF16)] if normed else []) + c_out_shape,
        scratch_shapes=c_scratch + ([pltpu.VMEM((tm, tn), F32)] if nk > 1 else []),
        compiler_params=_params(sem),
    )(a, b, *extras, *([a_gain] if normed else []), *c_ins)
    return outs if len(outs) > 1 else outs[0]


def _row_tile(rows, cols, n_arrays):
    budget = 10 * 1024 * 1024
    cols_padded = -(-cols // 128) * 128
    target = max(16, budget // (cols_padded * 4 * n_arrays * 2))
    return _tile(rows, target, align=16)


def _ew(fn, ins, out_dtypes, *, name):
    R, C = ins[0].shape
    n_in, n_out = len(ins), len(out_dtypes)
    tr = _row_tile(R, C, n_in + n_out)

    def body(*refs):
        outs = fn(*[r[...] for r in refs[:n_in]])
        for o_ref, o in zip(refs[n_in:], outs):
            o_ref[...] = o.astype(o_ref.dtype)

    spec = pl.BlockSpec((tr, C), lambda i: (i, 0))
    outs = pl.pallas_call(
        body, name=name, grid=(R // tr,),
        in_specs=[spec] * n_in, out_specs=[spec] * n_out,
        out_shape=[jax.ShapeDtypeStruct((R, C), dt) for dt in out_dtypes],
        compiler_params=_params(("parallel",)),
    )(*ins)
    return outs if n_out > 1 else outs[0]


def _rmsnorm_fwd(x, g, out_dtype, *, name):
    R, d = x.shape
    tr = _row_tile(R, d, 3)

    def body(x_ref, g_ref, o_ref):
        xv = x_ref[...].astype(F32)
        r = lax.rsqrt(jnp.mean(xv * xv, axis=-1, keepdims=True) + EPS)
        o_ref[...] = (xv * r * g_ref[...]).astype(o_ref.dtype)

    return pl.pallas_call(
        body, name=name, grid=(R // tr,),
        in_specs=[pl.BlockSpec((tr, d), lambda i: (i, 0)), pl.BlockSpec((1, d), lambda i: (0, 0))],
        out_specs=pl.BlockSpec((tr, d), lambda i: (i, 0)),
        out_shape=jax.ShapeDtypeStruct((R, d), out_dtype),
        compiler_params=_params(("parallel",)),
    )(x, g)


def _rmsnorm_bwd(x, g, dy, add=None, *, name):
    R, d = x.shape
    has_add = add is not None
    tr = _row_tile(R, d, 5)

    def body(*refs):
        x_ref, g_ref, dy_ref = refs[:3]
        add_ref = refs[3] if has_add else None
        dx_ref, dg_ref = refs[-2:]
        xv = x_ref[...].astype(F32)
        dyv = dy_ref[...].astype(F32)
        r = lax.rsqrt(jnp.mean(xv * xv, axis=-1, keepdims=True) + EPS)
        xh = xv * r
        dyg = dyv * g_ref[...]
        c = jnp.mean(dyg * xh, axis=-1, keepdims=True)
        dx = r * (dyg - xh * c)
        if has_add:
            dx = dx + add_ref[...]
        dx_ref[...] = dx

        @pl.when(pl.program_id(0) == 0)
        def _():
            dg_ref[...] = jnp.zeros_like(dg_ref)

        dg_ref[...] += jnp.sum(dyv * xh, axis=0, keepdims=True)

    row = pl.BlockSpec((tr, d), lambda i: (i, 0))
    vec = pl.BlockSpec((1, d), lambda i: (0, 0))
    ins = [x, g, dy] + ([add] if has_add else [])
    return pl.pallas_call(
        body, name=name, grid=(R // tr,),
        in_specs=[row, vec, row] + ([row] if has_add else []),
        out_specs=[row, vec],
        out_shape=[jax.ShapeDtypeStruct((R, d), F32), jax.ShapeDtypeStruct((1, d), F32)],
        compiler_params=_params(("arbitrary",)),
    )(*ins)


_NT = (((1,), (1,)), ((), ()))
_TN = (((0,), (0,)), ((), ()))


def _dot(a, b, dims=(((1,), (0,)), ((), ()))):
    return lax.dot_general(a, b, dims, preferred_element_type=F32)


def _log_sigmoid_pair(z):
    sp = jnp.log(1.0 + jnp.exp(-jnp.abs(z)))
    return jnp.minimum(z, 0.0) - sp, jnp.minimum(-z, 0.0) - sp


LANES = 128
_LOW = -3e38


def _lane_masks(hd, rows):
    if hd == LANES:
        return [None]
    lane = lax.broadcasted_iota(jnp.int32, (rows, LANES), 1)
    return [(lane >= hh * hd) & (lane < (hh + 1) * hd) for hh in range(LANES // hd)]


def _keep(t, m):
    return t if m is None else jnp.where(m, t, 0.0)


def _merge(parts, masks):
    out = parts[-1]
    for p, m in zip(parts[-2::-1], masks[-2::-1]):
        out = jnp.where(m, p, out)
    return out


def _row_value(t, m):
    return jnp.max(t if m is None else jnp.where(m, t, _LOW), axis=1, keepdims=True)


def _cols(tq, off):
    return pl.BlockSpec((tq, LANES), lambda g, i: (i, off + g))


def _cols_all(rows, off):
    return pl.BlockSpec((rows, LANES), lambda g, i: (0, off + g))


SCAN_BLOCK = 256


def _tri(kind, cols):
    n = min(SCAN_BLOCK, cols)
    r = lax.broadcasted_iota(jnp.int32, (n, n), 0)
    c = lax.broadcasted_iota(jnp.int32, (n, n), 1)
    return ((r > c) if kind == "after" else (r < c)).astype(BF16)


def _scan_cols(x, tri, reverse):
    cols = x.shape[1]
    cb = min(SCAN_BLOCK, cols)
    assert cols % cb == 0 and tri.shape == (cb, cb)
    nb = cols // cb
    blocks = [x[:, b * cb:(b + 1) * cb] for b in range(nb)]
    outs, carry = [None] * nb, None
    for b in (reversed(range(nb)) if reverse else range(nb)):
        y = _dot(blocks[b].astype(BF16), tri)
        outs[b] = y if carry is None else y + carry
        s = jnp.sum(blocks[b], axis=1, keepdims=True)
        carry = s if carry is None else carry + s
    return (outs[0] if nb == 1 else jnp.concatenate(outs, axis=1)), carry


def _softplus_parts(z):
    pos = jnp.maximum(z, 0.0) + jnp.log(1.0 + jnp.exp(-jnp.abs(z)))
    return pos, z - pos


class _ChipExchange:
    def __init__(self, kind, ins):
        assert kind in ("gather", "scatter", "siblings", "forward")
        self.kind, self.ins = kind, list(ins)
        lead = {"gather": lambda s: (N_CHIPS,) + s, "scatter": lambda s: (3,) + s[1:], "siblings": lambda s: s[:1] + s[2:],
                "forward": lambda s: s}[kind]
        self.out_shape = [jax.ShapeDtypeStruct(lead(a.shape), a.dtype) for a in ins]
        self.aliases = [(w, w) for w in range(len(ins))] if kind == "forward" else []
        n = (N_CHIPS if kind == "siblings" else 3) * len(ins)
        self.scratch = [pltpu.SemaphoreType.DMA((n,)), pltpu.SemaphoreType.DMA((n,))]

    def _copies(self, in_refs, out_refs, sems, landing):
        x, y, c, chips = _mesh_place()
        me = 2 * x + y
        out = []
        for w in range(len(self.ins)):
            if self.kind == "siblings":
                for s in range(N_CHIPS):
                    land = out_refs[w].at[s]
                    src = land if landing else in_refs[w].at[s, 1 - c]
                    out.append(_remote(src, land, sems, N_CHIPS * w + s, (x, y, 1 - c)))
                continue
            for j, chip in enumerate(chips):
                peer = 2 * chip[0] + chip[1]
                to = (chip[0], chip[1], c)
                if self.kind == "gather":
                    src, dst, land = in_refs[w].at[c], out_refs[w].at[me, c], out_refs[w].at[peer, c]
                elif self.kind == "scatter":
                    src, dst, land = in_refs[w].at[peer], out_refs[w].at[j], out_refs[w].at[j]
                else:
                    src, dst, land, to = in_refs[w].at[peer, c], out_refs[w].at[peer, c], out_refs[w].at[peer, 1 - c], (x, y, 1 - c)
                if landing:
                    src, dst = land, land
                out.append(_remote(src, dst, sems, 3 * w + j, to))
        return out

    def start(self, in_refs, out_refs, sems):
        for cp in self._copies(in_refs, out_refs, sems, False):
            cp.start()

    def finish(self, in_refs, out_refs, sems):
        for cp in self._copies(in_refs, out_refs, sems, True):
            cp.wait_recv()
        for cp in self._copies(in_refs, out_refs, sems, False):
            cp.wait_send()


class _Joint:
    def __init__(self, parts):
        self.parts = list(parts)
        self.ins = [a for p in self.parts for a in p.ins]
        self.out_shape = [s for p in self.parts for s in p.out_shape]
        self.scratch = [s for p in self.parts for s in p.scratch]
        self.aliases, n_in, n_out = [], 0, 0
        for p in self.parts:
            self.aliases += [(n_in + i, n_out + o) for i, o in p.aliases]
            n_in, n_out = n_in + len(p.ins), n_out + len(p.out_shape)

    def _each(self, step, in_refs, out_refs, sems):
        i = o = s = 0
        for p in self.parts:
            a, b, c = len(p.ins), len(p.out_shape), len(p.scratch)
            getattr(p, step)(in_refs[i:i + a], out_refs[o:o + b], sems[s:s + c])
            i, o, s = i + a, o + b, s + c

    def start(self, in_refs, out_refs, sems):
        self._each("start", in_refs, out_refs, sems)

    def finish(self, in_refs, out_refs, sems):
        self._each("finish", in_refs, out_refs, sems)


def _carry(comm, n_in, n_out, first, last, refs):
    if comm is None:
        return refs[:n_in], refs[n_in:n_in + n_out], (lambda: None)
    a, b = len(comm.ins), len(comm.out_shape)
    ins, c_in = refs[:n_in], refs[n_in:n_in + a]
    outs, c_out = refs[n_in + a:n_in + a + n_out], refs[n_in + a + n_out:n_in + a + n_out + b]
    sems = refs[n_in + a + n_out + b:n_in + a + n_out + b + len(comm.scratch)]
    pl.when(first)(lambda: comm.start(c_in, c_out, sems))
    return ins, outs, (lambda: pl.when(last)(lambda: comm.finish(c_in, c_out, sems)))


def _sbl_fwd(q, k, v, *, width, hd, name, tq=256, comm=None):
    (qa, qo), (ka, ko), (va, vo) = q, k, v
    S = qa.shape[0]
    tq = _tile(S, tq)
    tk = tq
    scale = hd ** -0.5
    n_g, n_q = width // LANES, S // tq

    def body(*refs):
        qi = pl.program_id(1)
        gi = pl.program_id(0)
        got = _carry(comm, 3, 2, (gi == 0) & (qi == 0), (gi == n_g - 1) & (qi == n_q - 1), refs)
        (q_ref, k_ref, v_ref), (o_ref, tot_ref) = got[0], got[1]
        masks = _lane_masks(hd, tq)
        qs = q_ref[...].astype(F32) * scale
        qm = [_keep(qs, m).astype(BF16) for m in masks]
        strict = lax.broadcasted_iota(jnp.int32, (tq, tk), 1) < lax.broadcasted_iota(jnp.int32, (tq, tk), 0)
        later = _tri("after", tk)

        def tile(kb, carry, diag):
            ks = pl.multiple_of(kb * tk, tk)
            kv = k_ref[pl.ds(ks, tk), :].astype(BF16)
            vv = v_ref[pl.ds(ks, tk), :].astype(BF16)
            out = []
            for hh in range(len(masks)):
                acc, c_pos = carry[2 * hh], carry[2 * hh + 1]
                pos, ls = _softplus_parts(_dot(qm[hh], kv, _NT))
                if diag:
                    pos = jnp.where(strict, pos, 0.0)
                pos_after, pos_all = _scan_cols(pos, later, True)
                w = jnp.exp(ls - (pos_after + c_pos))
                if diag:
                    w = jnp.where(strict, w, 0.0)
                out += [acc + _dot(w.astype(BF16), vv), c_pos + pos_all]
            return tuple(out)

        init = (jnp.zeros((tq, LANES), F32), jnp.zeros((tq, 1), F32)) * len(masks)
        carry = tile(qi, init, True)
        carry = lax.fori_loop(0, qi, lambda i, c: tile(qi - 1 - i, c, False), carry)
        o_ref[...] = _merge(carry[0::2], masks).astype(o_ref.dtype)
        tot_ref[...] = _merge([jnp.broadcast_to(-c, (tq, LANES)) for c in carry[1::2]], masks)
        got[2]()

    c_ins, c_in_specs, c_out_specs, c_out_shape, c_scratch = _comm_args(comm)
    return pl.pallas_call(
        body, name=name, grid=(n_g, n_q),
        in_specs=[_cols(tq, qo), _cols_all(S, ko), _cols_all(S, vo)] + c_in_specs,
        out_specs=[_cols(tq, 0), _cols(tq, 0)] + c_out_specs,
        out_shape=[jax.ShapeDtypeStruct((S, width), BF16), jax.ShapeDtypeStruct((S, width), F32)] + c_out_shape,
        scratch_shapes=c_scratch,
        compiler_params=_params(("arbitrary", "arbitrary")),
    )(qa, ka, va, *c_ins)


def _comm_args(comm, aliased=False):
    if comm is None:
        return [], [], [], [], []
    assert aliased or not comm.aliases
    return comm.ins, [_ANY] * len(comm.ins), [_ANY] * len(comm.out_shape), comm.out_shape, comm.scratch


def _comm_aliases(comm, n_in, n_out):
    return {} if comm is None else {n_in + i: n_out + o for i, o in comm.aliases}


def _sbl_bwd(q, k, v, do, tot, *, width, hd, name, tq=256, comm=None):
    (qa, qo), (ka, ko), (va, vo) = q, k, v
    S = qa.shape[0]
    tq = _tile(S, tq)
    tk = tq
    scale = hd ** -0.5
    n_g, n_q = width // LANES, S // tq

    def body(*refs):
        qi = pl.program_id(1)
        gi = pl.program_id(0)
        got = _carry(comm, 5, 3, (gi == 0) & (qi == 0), (gi == n_g - 1) & (qi == n_q - 1), refs)
        (q_ref, k_ref, v_ref, do_ref, tot_ref), (dq_ref, dk_ref, dv_ref) = got[0], got[1]

        @pl.when(qi == 0)
        def _():
            dk_ref[...] = jnp.zeros_like(dk_ref)
            dv_ref[...] = jnp.zeros_like(dv_ref)

        masks = _lane_masks(hd, tq)
        qs = q_ref[...].astype(F32) * scale
        qm = [_keep(qs, m).astype(BF16) for m in masks]
        dov = [_keep(do_ref[...], m).astype(BF16) for m in masks]
        rest = [-_row_value(tot_ref[...], m) for m in masks]
        strict = lax.broadcasted_iota(jnp.int32, (tq, tk), 1) < lax.broadcasted_iota(jnp.int32, (tq, tk), 0)
        later, before = _tri("after", tk), _tri("before", tk)

        def tile(kb, carry, diag):
            ks = pl.multiple_of(kb * tk, tk)
            kv = k_ref[pl.ds(ks, tk), :].astype(BF16)
            vv = v_ref[pl.ds(ks, tk), :].astype(BF16)
            out = []
            dk_t, dv_t = None, None
            for hh in range(len(masks)):
                dq, c_pos, c_g = carry[3 * hh:3 * hh + 3]
                pos, ls = _softplus_parts(_dot(qm[hh], kv, _NT))
                if diag:
                    pos = jnp.where(strict, pos, 0.0)
                pos_after, pos_all = _scan_cols(pos, later, True)
                c_pos = c_pos + pos_all
                w = jnp.exp(ls - (pos_after + (rest[hh] - c_pos)))
                if diag:
                    w = jnp.where(strict, w, 0.0)
                g = _dot(dov[hh], vv, _NT) * w
                g_before, g_all = _scan_cols(g, before, False)
                g_before = g_before + c_g
                dz = g - jnp.exp(ls) * (g + g_before)
                if diag:
                    dz = jnp.where(strict, dz, 0.0)
                dzb = dz.astype(BF16)
                dk_h = _dot(dzb, qm[hh], _TN)
                dv_h = _dot(w.astype(BF16), dov[hh], _TN)
                dk_t = dk_h if dk_t is None else dk_t + dk_h
                dv_t = dv_h if dv_t is None else dv_t + dv_h
                out += [dq + _dot(dzb, kv), c_pos, c_g + g_all]
            dk_ref[pl.ds(ks, tk), :] += dk_t
            dv_ref[pl.ds(ks, tk), :] += dv_t
            return tuple(out)

        zero = jnp.zeros((tq, 1), F32)
        init = (jnp.zeros((tq, LANES), F32), zero, zero) * len(masks)
        carry = lax.fori_loop(0, qi, lambda kb, c: tile(kb, c, False), init)
        carry = tile(qi, carry, True)
        dq_ref[...] = _merge(carry[0::3], masks) * scale
        got[2]()

    full = jax.ShapeDtypeStruct((S, width), F32)
    c_ins, c_in_specs, c_out_specs, c_out_shape, c_scratch = _comm_args(comm)
    return pl.pallas_call(
        body, name=name, grid=(n_g, n_q),
        in_specs=[_cols(tq, qo), _cols_all(S, ko), _cols_all(S, vo), _cols(tq, do[1]), _cols(tq, tot[1])] + c_in_specs,
        out_specs=[_cols(tq, 0), _cols_all(S, 0), _cols_all(S, 0)] + c_out_specs,
        out_shape=[full, full, full] + c_out_shape,
        scratch_shapes=c_scratch,
        compiler_params=_params(("arbitrary", "arbitrary")),
    )(qa, ka, va, do[0], tot[0], *c_ins)


def _sml_fwd(q, k, v, bias=None, *, width, hd, causal, name, tq=256, tk=256, comm=None):
    (qa, qo), (ka, ko), (va, vo) = q, k, v
    S, Sk = qa.shape[0], ka.shape[0]
    tq, tk = _tile(S, tq), _tile(Sk, tk)
    if causal:
        assert tq == tk and S == Sk
    nk = Sk // tk
    hpg = LANES // hd
    scale = hd ** -0.5
    has_bias = bias is not None
    n_g, n_q = width // LANES, S // tq

    def body(*all_refs):
        qi, gi = pl.program_id(1), pl.program_id(0)
        got = _carry(comm, 5 if has_bias else 3, 2, (gi == 0) & (qi == 0), (gi == n_g - 1) & (qi == n_q - 1), all_refs)
        refs = tuple(got[0]) + tuple(got[1])
        q_ref, k_ref, v_ref = refs[:3]
        o_ref, lse_ref = refs[-2:]
        masks = _lane_masks(hd, tq)
        qs = q_ref[...].astype(F32) * scale
        qm = [_keep(qs, m).astype(BF16) for m in masks]
        allowed = lax.broadcasted_iota(jnp.int32, (tq, tk), 1) <= lax.broadcasted_iota(jnp.int32, (tq, tk), 0)

        def tile(kb, carry, diag):
            ks = pl.multiple_of(kb * tk, tk)
            kv = k_ref[pl.ds(ks, tk), :].astype(BF16)
            vv = v_ref[pl.ds(ks, tk), :].astype(BF16)
            out = []
            for hh in range(hpg):
                m, l, acc = carry[3 * hh:3 * hh + 3]
                z = _dot(qm[hh], kv, _NT)
                if has_bias:
                    z = z + refs[3][hh] - refs[4][hh, kb]
                if diag:
                    z = jnp.where(allowed, z, NEG_INF)
                m2 = jnp.maximum(m, jnp.max(z, axis=1, keepdims=True))
                p = jnp.exp(z - m2)
                alpha = jnp.exp(m - m2)
                out += [m2, alpha * l + jnp.sum(p, axis=1, keepdims=True), alpha * acc + _dot(p.astype(BF16), vv)]
            return tuple(out)

        init = (jnp.full((tq, 1), NEG_INF, F32), jnp.zeros((tq, 1), F32), jnp.zeros((tq, LANES), F32)) * hpg
        if causal:
            carry = lax.fori_loop(0, qi, lambda kb, c: tile(kb, c, False), init)
            carry = tile(qi, carry, True)
        else:
            carry = lax.fori_loop(0, nk, lambda kb, c: tile(kb, c, False), init)
        o_ref[...] = _merge([acc / l for l, acc in zip(carry[1::3], carry[2::3])], masks).astype(o_ref.dtype)
        lse_ref[...] = _merge([jnp.broadcast_to(m + jnp.log(l), (tq, LANES)) for m, l in zip(carry[0::3], carry[1::3])], masks)
        got[2]()

    in_specs = [_cols(tq, qo), _cols_all(Sk, ko), _cols_all(Sk, vo)]
    ins = [qa, ka, va]
    if has_bias:
        in_specs += [pl.BlockSpec((hpg, tq, 1), lambda g, i: (g, i, 0)),
                     pl.BlockSpec((hpg, nk, 1, tk), lambda g, i: (g, 0, 0, 0))]
        ins += list(bias)
    c_ins, c_in_specs, c_out_specs, c_out_shape, c_scratch = _comm_args(comm, aliased=True)
    return pl.pallas_call(
        body, name=name, grid=(n_g, n_q),
        in_specs=in_specs + c_in_specs, out_specs=[_cols(tq, 0), _cols(tq, 0)] + c_out_specs,
        out_shape=[jax.ShapeDtypeStruct((S, width), BF16), jax.ShapeDtypeStruct((S, width), F32)] + c_out_shape,
        input_output_aliases=_comm_aliases(comm, len(ins), 2),
        scratch_shapes=c_scratch,
        compiler_params=_params(("arbitrary", "arbitrary") if comm is not None else ("parallel", "arbitrary")),
    )(*ins, *c_ins)


def _sml_bwd(q, k, v, o, lse, do, bias=None, *, width, hd, causal, name, tq=256, tk=256, comm=None):
    (qa, qo), (ka, ko), (va, vo) = q, k, v
    S, Sk = qa.shape[0], ka.shape[0]
    tq, tk = _tile(S, tq), _tile(Sk, tk)
    nk = Sk // tk
    hpg = LANES // hd
    scale = hd ** -0.5
    has_bias = bias is not None
    n_in = 8 if has_bias else 6
    n_g, n_q = width // LANES, S // tq

    def body(*all_refs):
        qi, gi = pl.program_id(1), pl.program_id(0)
        got = _carry(comm, n_in, 5 if has_bias else 3, (gi == 0) & (qi == 0), (gi == n_g - 1) & (qi == n_q - 1), all_refs)
        refs = tuple(got[0]) + tuple(got[1])
        q_ref, k_ref, v_ref, o_ref, lse_ref, do_ref = refs[:6]
        dq_ref, dk_ref, dv_ref = refs[n_in:n_in + 3]

        @pl.when(qi == 0)
        def _():
            dk_ref[...] = jnp.zeros_like(dk_ref)
            dv_ref[...] = jnp.zeros_like(dv_ref)
            if has_bias:
                refs[n_in + 4][...] = jnp.zeros_like(refs[n_in + 4])

        masks = _lane_masks(hd, tq)
        qs = q_ref[...].astype(F32) * scale
        qm = [_keep(qs, m).astype(BF16) for m in masks]
        do32 = do_ref[...]
        dov = [_keep(do32, m).astype(BF16) for m in masks]
        prod = do32 * o_ref[...].astype(F32)
        delta = [jnp.sum(_keep(prod, m), axis=1, keepdims=True) for m in masks]
        lses = [_row_value(lse_ref[...], m) for m in masks]
        allowed = lax.broadcasted_iota(jnp.int32, (tq, tk), 1) <= lax.broadcasted_iota(jnp.int32, (tq, tk), 0)

        def tile(kb, carry, diag):
            ks = pl.multiple_of(kb * tk, tk)
            kv = k_ref[pl.ds(ks, tk), :].astype(BF16)
            vv = v_ref[pl.ds(ks, tk), :].astype(BF16)
            out = []
            dk_t, dv_t = None, None
            for hh in range(hpg):
                dq, db_row = carry[2 * hh:2 * hh + 2]
                z = _dot(qm[hh], kv, _NT)
                if has_bias:
                    z = z + refs[6][hh] - refs[7][hh, kb]
                p = jnp.exp(z - lses[hh])
                if diag:
                    p = jnp.where(allowed, p, 0.0)
                dz = p * (_dot(dov[hh], vv, _NT) - delta[hh])
                dzb = dz.astype(BF16)
                dk_h = _dot(dzb, qm[hh], _TN)
                dv_h = _dot(p.astype(BF16), dov[hh], _TN)
                dk_t = dk_h if dk_t is None else dk_t + dk_h
                dv_t = dv_h if dv_t is None else dv_t + dv_h
                if has_bias:
                    db_row = db_row + jnp.sum(dz, axis=1, keepdims=True)
                    refs[n_in + 4][hh, kb] += jnp.sum(dz, axis=0, keepdims=True)
                out += [dq + _dot(dzb, kv), db_row]
            dk_ref[pl.ds(ks, tk), :] += dk_t
            dv_ref[pl.ds(ks, tk), :] += dv_t
            return tuple(out)

        init = (jnp.zeros((tq, LANES), F32), jnp.zeros((tq, 1), F32)) * hpg
        if causal:
            carry = lax.fori_loop(0, qi, lambda kb, c: tile(kb, c, False), init)
            carry = tile(qi, carry, True)
        else:
            carry = lax.fori_loop(0, nk, lambda kb, c: tile(kb, c, False), init)
        dq_ref[...] = _merge(carry[0::2], masks) * scale
        if has_bias:
            for hh in range(hpg):
                refs[n_in + 3][hh] = carry[2 * hh + 1]
        got[2]()

    in_specs = [_cols(tq, qo), _cols_all(Sk, ko), _cols_all(Sk, vo), _cols(tq, o[1]), _cols(tq, lse[1]), _cols(tq, do[1])]
    ins = [qa, ka, va, o[0], lse[0], do[0]]
    out_specs = [_cols(tq, 0), _cols_all(Sk, 0), _cols_all(Sk, 0)]
    out_shape = [jax.ShapeDtypeStruct((S, width), F32), jax.ShapeDtypeStruct((Sk, width), F32),
                 jax.ShapeDtypeStruct((Sk, width), F32)]
    if has_bias:
        rspec = pl.BlockSpec((hpg, tq, 1), lambda g, i: (g, i, 0))
        cspec = pl.BlockSpec((hpg, nk, 1, tk), lambda g, i: (g, 0, 0, 0))
        in_specs += [rspec, cspec]
        ins += list(bias)
        out_specs += [rspec, cspec]
        n_heads = width // hd
        out_shape += [jax.ShapeDtypeStruct((n_heads, S, 1), F32), jax.ShapeDtypeStruct((n_heads, nk, 1, tk), F32)]
    c_ins, c_in_specs, c_out_specs, c_out_shape, c_scratch = _comm_args(comm)
    return pl.pallas_call(
        body, name=name, grid=(n_g, n_q),
        in_specs=in_specs + c_in_specs, out_specs=out_specs + c_out_specs, out_shape=out_shape + c_out_shape,
        scratch_shapes=c_scratch,
        compiler_params=_params(("arbitrary", "arbitrary") if comm is not None else ("parallel", "arbitrary")),
    )(*ins, *c_ins)


def _head_sums(t, masks):
    sums = [jnp.sum(_keep(t, m), axis=1, keepdims=True) for m in masks]
    return _merge([jnp.broadcast_to(s, t.shape) for s in sums], masks) if len(masks) > 1 else sums[0]


def _hnorm_fwd(x, g_lanes, *, width, hd, name, tr=1024):
    xa, xo = x
    R = xa.shape[0]
    tr = _tile(R, tr, align=16)
    n_blk = width // LANES

    def body(x_ref, g_ref, o_ref):
        masks = _lane_masks(hd, tr)
        for j in range(n_blk):
            sl = slice(j * LANES, (j + 1) * LANES)
            xv = x_ref[:, sl].astype(F32)
            r = lax.rsqrt(_head_sums(xv * xv, masks) * (1.0 / hd) + EPS)
            o_ref[:, sl] = (xv * r * g_ref[...]).astype(o_ref.dtype)

    assert (xo * LANES) % width == 0
    return pl.pallas_call(
        body, name=name, grid=(R // tr,),
        in_specs=[pl.BlockSpec((tr, width), lambda i: (i, xo * LANES // width)), pl.BlockSpec((1, LANES), lambda i: (0, 0))],
        out_specs=pl.BlockSpec((tr, width), lambda i: (i, 0)),
        out_shape=jax.ShapeDtypeStruct((R, width), BF16),
        compiler_params=_params(("parallel",)),
    )(xa, g_lanes)


def _hnorm_bwd(x, g_lanes, dy, *, width, hd, name, tr=1024):
    xa, xo = x
    R = xa.shape[0]
    tr = _tile(R, tr, align=16)
    n_blk = width // LANES

    def body(x_ref, g_ref, dy_ref, dx_ref, dg_ref):
        masks = _lane_masks(hd, tr)
        dg = jnp.zeros((1, LANES), F32)
        for j in range(n_blk):
            sl = slice(j * LANES, (j + 1) * LANES)
            xv = x_ref[:, sl].astype(F32)
            dyv = dy_ref[:, sl].astype(F32)
            r = lax.rsqrt(_head_sums(xv * xv, masks) * (1.0 / hd) + EPS)
            xh = xv * r
            dyg = dyv * g_ref[...]
            c = _head_sums(dyg * xh, masks) * (1.0 / hd)
            dx_ref[:, sl] = (r * (dyg - xh * c)).astype(dx_ref.dtype)
            dg = dg + jnp.sum(dyv * xh, axis=0, keepdims=True)
        if hd * 2 == LANES:
            dg8 = jnp.broadcast_to(dg, (8, LANES))
            dg = (dg8 + pltpu.roll(dg8, shift=hd, axis=1))[0:1]
        else:
            assert hd == LANES

        @pl.when(pl.program_id(0) == 0)
        def _():
            dg_ref[...] = jnp.zeros_like(dg_ref)

        dg_ref[...] += dg

    assert (xo * LANES) % width == 0
    return pl.pallas_call(
        body, name=name, grid=(R // tr,),
        in_specs=[pl.BlockSpec((tr, width), lambda i: (i, xo * LANES // width)), pl.BlockSpec((1, LANES), lambda i: (0, 0)),
                  pl.BlockSpec((tr, width), lambda i: (i, 0))],
        out_specs=[pl.BlockSpec((tr, width), lambda i: (i, 0)), pl.BlockSpec((1, LANES), lambda i: (0, 0))],
        out_shape=[jax.ShapeDtypeStruct((R, width), BF16), jax.ShapeDtypeStruct((1, LANES), F32)],
        compiler_params=_params(("arbitrary",)),
    )(xa, g_lanes, dy)


def _split3_dot(x, tri):
    a = x.astype(BF16)
    r = x - a.astype(F32)
    b = r.astype(BF16)
    c = (r - b.astype(F32)).astype(BF16)
    return _dot(a, tri) + _dot(b, tri) + _dot(c, tri)


def _forget_fwd(logit_t, b_col, *, name, blk=512):
    H, S = logit_t.shape
    blk = _tile(S, blk)

    def body(l_ref, b_ref, f_ref):
        r_i = lax.broadcasted_iota(jnp.int32, (blk, blk), 0)
        c_i = lax.broadcasted_iota(jnp.int32, (blk, blk), 1)
        upto = (r_i <= c_i).astype(BF16)
        carry = jnp.zeros((H, 1), F32)
        for j in range(S // blk):
            u = l_ref[:, j * blk:(j + 1) * blk] + b_ref[...]
            lf, _ = _log_sigmoid_pair(u)
            f_ref[:, j * blk:(j + 1) * blk] = _split3_dot(lf, upto) + carry
            carry = carry + jnp.sum(lf, axis=1, keepdims=True)

    return pl.pallas_call(
        body, name=name,
        out_shape=jax.ShapeDtypeStruct((H, S), F32),
        compiler_params=pltpu.CompilerParams(vmem_limit_bytes=VMEM_LIMIT),
    )(logit_t, b_col)


def _forget_bwd(logit_t, b_col, d_f, *, name, blk=512):
    H, S = logit_t.shape
    blk = _tile(S, blk)

    def body(l_ref, b_ref, df_ref, dl_ref, db_ref):
        r_i = lax.broadcasted_iota(jnp.int32, (blk, blk), 0)
        c_i = lax.broadcasted_iota(jnp.int32, (blk, blk), 1)
        fromon = (r_i >= c_i).astype(BF16)
        carry = jnp.zeros((H, 1), F32)
        db = jnp.zeros((H, 1), F32)
        for j in reversed(range(S // blk)):
            sl = slice(j * blk, (j + 1) * blk)
            dfv = df_ref[:, sl]
            d_lf = _split3_dot(dfv, fromon) + carry
            carry = carry + jnp.sum(dfv, axis=1, keepdims=True)
            u = l_ref[:, sl] + b_ref[...]
            _, lsn = _log_sigmoid_pair(u)
            dl = d_lf * jnp.exp(lsn)
            dl_ref[:, sl] = dl
            db = db + jnp.sum(dl, axis=1, keepdims=True)
        db_ref[...] = db

    return pl.pallas_call(
        body, name=name,
        out_shape=[jax.ShapeDtypeStruct((H, S), F32), jax.ShapeDtypeStruct((H, 1), F32)],
        compiler_params=pltpu.CompilerParams(vmem_limit_bytes=VMEM_LIMIT),
    )(logit_t, b_col, d_f)


def _sigmoid(t):
    return 1.0 / (1.0 + jnp.exp(-t))


def _gate_fwd(o3, w3, proj, x, w_out, g_norm, D, *, name, tm=512):
    S = proj.shape[0]
    tm = _tile(S, tm)

    def body(o0, o1, o2, w0, w1, w2, g0, g1, g2, x_ref, wo_ref, gn_ref, merged_ref, x1_ref, h2_ref):
        acc = None
        for o_ref, w_ref, g_ref in ((o0, w0, g0), (o1, w1, g1), (o2, w2, g2)):
            t = _sigmoid(g_ref[...]) * _dot(o_ref[...], w_ref[...])
            acc = t if acc is None else acc + t
        merged = acc.astype(BF16)
        merged_ref[...] = merged
        x1 = x_ref[...] + _dot(merged, wo_ref[...])
        x1_ref[...] = x1
        h2_ref[...] = (x1 * lax.rsqrt(jnp.mean(x1 * x1, axis=-1, keepdims=True) + EPS) * gn_ref[...]).astype(BF16)

    ospec = lambda d: pl.BlockSpec((tm, d), lambda i: (i, 0))
    wspec = lambda w: pl.BlockSpec(w.shape, lambda i: (0, 0))
    gspec = lambda j: pl.BlockSpec((tm, D), lambda i: (i, j))
    row = pl.BlockSpec((tm, D), lambda i: (i, 0))
    return pl.pallas_call(
        body, name=name, grid=(S // tm,),
        in_specs=[ospec(o.shape[1]) for o in o3] + [wspec(w) for w in w3] + [gspec(j) for j in range(3)]
        + [row, wspec(w_out), wspec(g_norm)],
        out_specs=[row, row, row],
        out_shape=[jax.ShapeDtypeStruct((S, D), BF16), jax.ShapeDtypeStruct((S, D), F32), jax.ShapeDtypeStruct((S, D), BF16)],
        compiler_params=_params(("parallel",)),
    )(*o3, *w3, proj, proj, proj, x, w_out, g_norm)


def _gate_bwd(o3, w3, proj, dx1, w_out, D, *, name, tm=256):
    S = proj.shape[0]
    tm = _tile(S, tm)

    def body(o0, o1, o2, w0, w1, w2, g0, g1, g2, dx_ref, wo_ref, dg_ref, dw0, dw1, dw2, do0, do1, do2):
        first = pl.program_id(0) == 0
        dm = _dot(dx_ref[...].astype(BF16), wo_ref[...], _NT)
        for j, (o_ref, w_ref, g_ref, dw_ref, do_ref) in enumerate(
                ((o0, w0, g0, dw0, do0), (o1, w1, g1, dw1, do1), (o2, w2, g2, dw2, do2))):
            s = _sigmoid(g_ref[...])
            br = _dot(o_ref[...], w_ref[...])
            dg_ref[:, j * D:(j + 1) * D] = (dm * br * s * (1.0 - s)).astype(dg_ref.dtype)
            dbr = (dm * s).astype(BF16)
            do_ref[...] = _dot(dbr, w_ref[...], _NT)
            part = _dot(o_ref[...], dbr, _TN)

            @pl.when(first)
            def _():
                dw_ref[...] = part

            @pl.when(jnp.logical_not(first))
            def _():
                dw_ref[...] += part

    ospec = lambda d: pl.BlockSpec((tm, d), lambda i: (i, 0))
    wspec = lambda w: pl.BlockSpec(w.shape, lambda i: (0, 0))
    gspec = lambda j: pl.BlockSpec((tm, D), lambda i: (i, j))
    dspec = pl.BlockSpec((tm, D), lambda i: (i, 0))
    return pl.pallas_call(
        body, name=name, grid=(S // tm,),
        in_specs=[ospec(o.shape[1]) for o in o3] + [wspec(w) for w in w3] + [gspec(j) for j in range(3)]
        + [dspec, wspec(w_out)],
        out_specs=[pl.BlockSpec((tm, 3 * D), lambda i: (i, 0))] + [wspec(w) for w in w3] + [ospec(o.shape[1]) for o in o3],
        out_shape=[jax.ShapeDtypeStruct((S, proj.shape[1]), BF16)] + [jax.ShapeDtypeStruct(w.shape, F32) for w in w3]
        + [jax.ShapeDtypeStruct((S, o.shape[1]), F32) for o in o3],
        compiler_params=_params(("arbitrary",)),
    )(*o3, *w3, proj, proj, proj, dx1, w_out)


def _pair_sum(stacked, got, core, *, name):
    n, _, r, c = stacked.shape
    tr = _row_tile(r, c, 3)

    def body(core_ref, a_ref, b_ref, o_ref):
        o_ref[...] = (a_ref[0].astype(F32) + b_ref[...].astype(F32)).astype(o_ref.dtype)

    spec = pl.BlockSpec((1, tr, c), lambda s, i, core_ref: (s, i, 0))
    return pl.pallas_call(
        body, name=name,
        grid_spec=pltpu.PrefetchScalarGridSpec(
            num_scalar_prefetch=1, grid=(n, r // tr),
            in_specs=[pl.BlockSpec((1, 1, tr, c), lambda s, i, core_ref: (s, core_ref[0], i, 0)), spec],
            out_specs=spec),
        out_shape=jax.ShapeDtypeStruct((n, r, c), BF16),
        compiler_params=_params(("parallel", "parallel")),
    )(core.astype(jnp.int32).reshape(1), stacked, got)


def _chip_sum(parts, got, chip, *, name):
    _, r, c = parts.shape
    tr = _row_tile(r, c, 6)

    def body(chip_ref, p_ref, q0_ref, q1_ref, q2_ref, o_ref):
        o_ref[...] = ((p_ref[0].astype(F32) + q0_ref[0].astype(F32)) + q1_ref[0].astype(F32)) + q2_ref[0].astype(F32)

    from_chip = lambda j: pl.BlockSpec((1, tr, c), lambda i, chip_ref: (j, i, 0))
    return pl.pallas_call(
        body, name=name,
        grid_spec=pltpu.PrefetchScalarGridSpec(
            num_scalar_prefetch=1, grid=(r // tr,),
            in_specs=[pl.BlockSpec((1, tr, c), lambda i, chip_ref: (chip_ref[0], i, 0))] + [from_chip(j) for j in range(3)],
            out_specs=pl.BlockSpec((tr, c), lambda i, chip_ref: (i, 0))),
        out_shape=jax.ShapeDtypeStruct((r, c), F32),
        compiler_params=_params(("parallel",)),
    )(chip.astype(jnp.int32).reshape(1), parts, got, got, got)


def _adamw_math(w, g, m, v):
    m2 = ADAM_B1 * m + (1.0 - ADAM_B1) * g
    v2 = ADAM_B2 * v + (1.0 - ADAM_B2) * (g * g)
    m_hat = m2 / (1.0 - ADAM_B1 ** ADAM_STEP)
    v_hat = v2 / (1.0 - ADAM_B2 ** ADAM_STEP)
    delta = -ADAM_LR * (m_hat / (jnp.sqrt(v_hat) + ADAM_EPS) + ADAM_WD * w)
    return delta, m2, v2


def _adamw(w, g, m, v, *, name):
    return _ew(_adamw_math, [w, g, m, v], (F32, F32, F32), name=name)


def _adamw_small(w, parts, m, v, *, name):
    n = parts.shape[0]

    def body(w_ref, p_ref, m_ref, v_ref, g_ref, d_ref, m2_ref, v2_ref):
        g = p_ref[0]
        for i in range(1, n):
            g = g + p_ref[i]
        g_ref[...] = g
        d_ref[...], m2_ref[...], v2_ref[...] = _adamw_math(w_ref[...], g, m_ref[...], v_ref[...])

    shp = jax.ShapeDtypeStruct(w.shape, F32)
    return pl.pallas_call(body, name=name, out_shape=[shp] * 4)(w, parts, m, v)


_ANY = pl.BlockSpec(memory_space=pl.ANY)


def _mesh_place():
    x, y, c = lax.axis_index("x"), lax.axis_index("y"), lax.axis_index("c")
    chips = [(1 - x, y), (x, 1 - y), (1 - x, 1 - y)]
    return x, y, c, chips


def _remote(src, dst, sems, i, to):
    send_sems, recv_sems = sems
    return pltpu.make_async_remote_copy(src_ref=src, dst_ref=dst, send_sem=send_sems.at[i], recv_sem=recv_sems.at[i],
                                        device_id=to, device_id_type=MESH_ID)


def _gather_weights(shards, *, name):
    n = len(shards)

    def body(*refs):
        ins, outs = refs[:n], refs[n:2 * n]
        sems = refs[2 * n:2 * n + 2]
        x, y, c, chips = _mesh_place()
        me = 2 * x + y
        sibling = (x, y, 1 - c)
        sent = []
        for w in range(n):
            for j, chip in enumerate(chips):
                cp = _remote(ins[w].at[c], outs[w].at[me, c], sems, 6 * w + j, (chip[0], chip[1], c))
                cp.start()
                sent.append(cp)
        for w in range(n):
            for j, chip in enumerate(chips):
                got = outs[w].at[2 * chip[0] + chip[1], c]
                _remote(got, got, sems, 6 * w + j, sibling).wait_recv()
                cp = _remote(got, got, sems, 6 * w + 3 + j, sibling)
                cp.start()
                sent.append(cp)
        for w in range(n):
            for j, chip in enumerate(chips):
                got = outs[w].at[2 * chip[0] + chip[1], 1 - c]
                _remote(got, got, sems, 6 * w + 3 + j, sibling).wait_recv()
        for cp in sent:
            cp.wait_send()

    outs = pl.pallas_call(
        body, name=name,
        in_specs=[_ANY] * n, out_specs=[_ANY] * n,
        out_shape=[jax.ShapeDtypeStruct((N_CHIPS,) + s.shape, s.dtype) for s in shards],
        scratch_shapes=[pltpu.SemaphoreType.DMA((6 * n,)), pltpu.SemaphoreType.DMA((6 * n,))],
    )(*shards)
    me = 2 * lax.axis_index("x") + lax.axis_index("y")
    return [lax.dynamic_update_index_in_dim(o, s, me, 0) for o, s in zip(outs, shards)]


def _exchange_siblings(grads, *, name):
    n = len(grads)

    def body(*refs):
        ins, got = refs[:n], refs[n:2 * n]
        sems = refs[2 * n:2 * n + 2]
        x, y, c, _ = _mesh_place()
        sibling = (x, y, 1 - c)
        sent = []
        for w in range(n):
            for s in range(N_CHIPS):
                cp = _remote(ins[w].at[s, 1 - c], got[w].at[s], sems, N_CHIPS * w + s, sibling)
                cp.start()
                sent.append(cp)
        for w in range(n):
            for s in range(N_CHIPS):
                _remote(got[w].at[s], got[w].at[s], sems, N_CHIPS * w + s, sibling).wait_recv()
        for cp in sent:
            cp.wait_send()

    n_sem = N_CHIPS * n
    return pl.pallas_call(
        body, name=name,
        in_specs=[_ANY] * n, out_specs=[_ANY] * n,
        out_shape=[jax.ShapeDtypeStruct((N_CHIPS,) + g.shape[2:], g.dtype) for g in grads],
        scratch_shapes=[pltpu.SemaphoreType.DMA((n_sem,)), pltpu.SemaphoreType.DMA((n_sem,))],
    )(*grads)


def _share_halves(halves, small):
    n = len(halves)

    def body(*refs):
        ins, small_ref = refs[:n], refs[n]
        outs, small_out = refs[n + 1:2 * n + 1], refs[2 * n + 1]
        sems = refs[2 * n + 2:2 * n + 4]
        x, y, c, chips = _mesh_place()
        sibling = (x, y, 1 - c)
        me = 4 * x + 2 * y + c
        sent = [_remote(ins[w], outs[w].at[c], sems, w, sibling) for w in range(n)]
        peers = [sibling] + [(ch[0], ch[1], cc) for ch in chips for cc in (c, 1 - c)]
        sent += [_remote(small_ref, small_out.at[me], sems, n + j, peer) for j, peer in enumerate(peers)]
        for cp in sent:
            cp.start()
        for w in range(n):
            _remote(outs[w].at[1 - c], outs[w].at[1 - c], sems, w, sibling).wait_recv()
        for j, peer in enumerate(peers):
            frm = small_out.at[4 * peer[0] + 2 * peer[1] + peer[2]]
            _remote(frm, frm, sems, n + j, peer).wait_recv()
        for cp in sent:
            cp.wait_send()

    n_sem = n + 7
    outs = pl.pallas_call(
        body, name="share_halves",
        in_specs=[_ANY] * (n + 1), out_specs=[_ANY] * (n + 1),
        out_shape=[jax.ShapeDtypeStruct((2,) + h.shape, h.dtype) for h in halves]
        + [jax.ShapeDtypeStruct((8,) + small.shape, small.dtype)],
        scratch_shapes=[pltpu.SemaphoreType.DMA((n_sem,)), pltpu.SemaphoreType.DMA((n_sem,))],
    )(*halves, small)
    c = lax.axis_index("c")
    me = 4 * lax.axis_index("x") + 2 * lax.axis_index("y") + c
    return ([lax.dynamic_update_index_in_dim(o, h, c, 0) for o, h in zip(outs[:n], halves)],
            lax.dynamic_update_index_in_dim(outs[n], small, me, 0))


EARLY = ("w_ff_down", "w_ff_up", "w_out", "w_branch_sb", "w_branch_fox", "w_branch_mem", "w_mem_kv")


def _norm_bwd_tail(dy, x, add, g):
    r = lax.rsqrt(jnp.mean(x * x, axis=-1, keepdims=True) + EPS)
    xh = x * r
    dyg = dy * g
    c = jnp.mean(dyg * xh, axis=-1, keepdims=True)
    return r * (dyg - xh * c) + add, jnp.sum(dy * xh, axis=0, keepdims=True)


def _split(outs, n):
    outs = list(outs) if isinstance(outs, (list, tuple)) else [outs]
    return outs[:n], outs[n:]


def _local_step(x, mem, target, small, W, gather_rest=None, reduce_early=None, reduce_late=None):
    S, D = x.shape
    o_qkv, o_mq, o_f = 3 * D, 3 * D + 2 * 3 * D_SB, 3 * D + 2 * 3 * D_SB + D_MEM
    tq = 512

    g_comms, finish_weights = gather_rest if gather_rest is not None else ([None] * 3, None)
    (proj, h), landed = _split(_mm(x, W["w_in"], name="in_proj", tb=True, tn=768, a_gain=small["g_mix_norm"],
                                   comm=g_comms[0]), 2)
    blk = lambda j: (proj, (o_qkv + j * D_SB) // LANES)
    sb_q, sb_k, sb_v, fx_q, fx_k, fx_v = [blk(j) for j in range(6)]
    m_q = (proj, o_mq // LANES)
    f_logit_t = _mm(W["w_in"][o_f:o_f + ROW_TILE], h, name="forget_logits", tb=True)[:FOX_HEADS]
    b_col = small["b_forget"].reshape(FOX_HEADS, 1)
    lanes = lambda g: jnp.tile(g, (1, LANES // g.shape[1]))
    g_fq, g_fk, g_mq, g_mk = [lanes(small[k]) for k in ("g_fox_q", "g_fox_k", "g_mem_q", "g_mem_k")]

    (o_sb, sb_tot), more = _split(_sbl_fwd(sb_q, sb_k, sb_v, width=D_SB, hd=HD, name="sb_fwd", tq=tq, comm=g_comms[1]), 2)
    landed += more

    fq = _hnorm_fwd(fx_q, g_fq, width=D_FOX, hd=HD, name="fox_q_norm")
    fk = _hnorm_fwd(fx_k, g_fk, width=D_FOX, hd=HD, name="fox_k_norm")
    f_cum = _forget_fwd(f_logit_t, b_col, name="forget_fwd")
    tkf = _tile(S, tq)
    f_bias = (f_cum.reshape(FOX_HEADS, S, 1), f_cum.reshape(FOX_HEADS, S // tkf, 1, tkf))
    fox_comm = None if finish_weights is None else _Joint([g_comms[2], _ChipExchange("forward", landed)])
    (o_fox, fox_lse), more = _split(_sml_fwd((fq, 0), (fk, 0), fx_v, f_bias, width=D_FOX, hd=HD, causal=True,
                                             name="fox_fwd", tq=tq, tk=tq, comm=fox_comm), 2)
    mem_comm = None
    if finish_weights is not None:
        n_last = len(g_comms[2].ins)
        W = {**W, **finish_weights((0, 1), more[n_last:])}
        mem_comm = _ChipExchange("forward", more[:n_last])

    mh = _rmsnorm_fwd(mem, small["g_mem_norm"], BF16, name="mem_norm")
    mkv = _mm(mh, W["w_mem_kv"], name="mem_kv")
    mv = (mkv, D_MEM // LANES)
    mq = _hnorm_fwd(m_q, g_mq, width=D_MEM, hd=MEM_HD, name="mem_q_norm")
    mk = _hnorm_fwd((mkv, 0), g_mk, width=D_MEM, hd=MEM_HD, name="mem_k_norm")
    (o_mem, mem_lse), more = _split(_sml_fwd((mq, 0), (mk, 0), mv, width=D_MEM, hd=MEM_HD, causal=False, name="mem_fwd",
                                             tq=tq, tk=256, comm=mem_comm), 2)
    if finish_weights is not None:
        W = {**W, **finish_weights((2,), more)}

    o3 = [o_sb, o_fox, o_mem]
    w3 = [W["w_branch_sb"], W["w_branch_fox"], W["w_branch_mem"]]
    merged, x1, h2 = _gate_fwd(o3, w3, proj, x, W["w_out"], small["g_mlp_norm"], D, name="gate_fwd")

    def relu2(acc):
        u = jnp.maximum(acc, 0.0)
        return u, u * u

    u, a = _mm(h2, W["w_ff_up"], name="ff_up", out_dtypes=(BF16, BF16), epilogue=relu2)
    def head(acc, res, tgt):
        d = (res + acc - tgt) * (1.0 / D)
        return d, d, jnp.sum(d * d, axis=0, keepdims=True)

    dy, dy16, sq_rows = _mm(a, W["w_ff_down"], name="ff_down", extras=(x1, target), out_dtypes=(F32, BF16),
                            epilogue=head, col_sums=1, tn=512)
    loss = (0.5 * D) * jnp.sum(sq_rows)

    G = {}
    du = _mm(dy16, W["w_ff_down"], name="d_ff_act", tb=True, out_dtypes=(BF16,), extras=(u,),
             epilogue=lambda acc, uu: (acc * (2.0 * uu.astype(F32)),))
    G["w_ff_down"] = _mm(a, dy16, name="d_w_ff_down", ta=True, out_dtypes=(BF16,))
    G["w_ff_up"] = _mm(h2, du, name="d_w_ff_up", ta=True, out_dtypes=(BF16,))
    dx1, dg_rows = _mm(du, W["w_ff_up"], name="d_mlp_in", tb=True, extras=(x1, dy, small["g_mlp_norm"]),
                       epilogue=_norm_bwd_tail, col_sums=1, tm=512, tn=D)
    dg_mlp = jnp.sum(dg_rows, axis=0, keepdims=True)
    G["w_out"] = _mm(merged, dx1, name="d_w_out", ta=True, out_dtypes=(BF16,))
    dgate, dw0, dw1, dw2, do_sb, do_fox, do_mem = _gate_bwd(o3, w3, proj, dx1, W["w_out"], D, name="gate_bwd")
    for nm, dw in zip(("w_branch_sb", "w_branch_fox", "w_branch_mem"), (dw0, dw1, dw2)):
        G[nm] = dw.astype(BF16)

    sib_comm, after_siblings = (reduce_early({k: G.pop(k) for k in EARLY if k != "w_mem_kv"})
                                if reduce_early is not None else (None, None))
    (dmq_n, dmk_n, dmv), landed_sib = _split(
        _sml_bwd((mq, 0), (mk, 0), mv, (o_mem, 0), (mem_lse, 0), (do_mem, 0), width=D_MEM, hd=MEM_HD, causal=False,
                 name="mem_bwd", tq=tq, tk=256, comm=sib_comm), 3)
    dm_q, dg_mem_q = _hnorm_bwd(m_q, g_mq, dmq_n, width=D_MEM, hd=MEM_HD, name="d_mem_q_norm")
    dmk_raw, dg_mem_k = _hnorm_bwd((mkv, 0), g_mk, dmk_n, width=D_MEM, hd=MEM_HD, name="d_mem_k_norm")
    dmkv = jnp.concatenate([dmk_raw, dmv.astype(BF16)], axis=1)
    G["w_mem_kv"] = _mm(mh, dmkv, name="d_w_mem_kv", ta=True, out_dtypes=(BF16,))
    dmh = _mm(dmkv, W["w_mem_kv"], name="d_mem_h", tb=True)
    _, dg_mem = _rmsnorm_bwd(mem, small["g_mem_norm"], dmh, name="d_mem_norm")

    r_comms, r_finish = (after_siblings(landed_sib, {"w_mem_kv": G.pop("w_mem_kv")})
                         if after_siblings is not None else ([None] * 2, None))
    dsb, landed_sb = _split(_sbl_bwd(sb_q, sb_k, sb_v, (do_sb, 0), (sb_tot, 0), width=D_SB, hd=HD, name="sb_bwd", tq=tq,
                                     comm=r_comms[0]), 3)
    (dfq, dfk, dfv, df_row, df_col), landed_fox = _split(
        _sml_bwd((fq, 0), (fk, 0), fx_v, (o_fox, 0), (fox_lse, 0), (do_fox, 0), f_bias, width=D_FOX, hd=HD, causal=True,
                 name="fox_bwd", tq=tq, tk=tq, comm=r_comms[1]), 5)
    early = r_finish(landed_sb, landed_fox) if r_finish is not None else {}
    dfx_q, dg_fox_q = _hnorm_bwd(fx_q, g_fq, dfq, width=D_FOX, hd=HD, name="d_fox_q_norm")
    dfx_k, dg_fox_k = _hnorm_bwd(fx_k, g_fk, dfk, width=D_FOX, hd=HD, name="d_fox_k_norm")
    d_fcum = df_row.reshape(FOX_HEADS, S) - df_col.reshape(FOX_HEADS, S)
    d_flogit_t, db_forget = _forget_bwd(f_logit_t, b_col, d_fcum, name="forget_bwd")
    dg_fox_q, dg_fox_k = dg_fox_q[:, :HD], dg_fox_k[:, :HD]

    rest_cols = jnp.concatenate([t.astype(BF16) for t in (*dsb, dfx_q, dfx_k, dfv, dm_q)]
                                + [d_flogit_t.T.astype(BF16), jnp.zeros((S, F_PAD - FOX_HEADS), BF16)], axis=1)
    dproj = lax.dynamic_update_slice(dgate, rest_cols, (0, 3 * D))
    g_w_in = _mm(dproj, h, name="d_w_in", ta=True, out_dtypes=(BF16,), tm=768)
    comm, finish = reduce_late({"w_in": g_w_in}) if reduce_late is not None else (None, None)
    (grad_x, dg_rows), landed = _split(
        _mm(dproj, W["w_in"], name="d_mix_in", tk=2304, tm=512, tn=D, extras=(x, dx1, small["g_mix_norm"]),
            epilogue=_norm_bwd_tail, col_sums=1, comm=comm), 2)
    dg_mix = jnp.sum(dg_rows, axis=0, keepdims=True)
    if finish is None:
        G["w_in"] = g_w_in
    else:
        early.update(finish(landed))

    small_grads = dict(g_mix_norm=dg_mix, g_mem_norm=dg_mem, b_forget=db_forget.reshape(1, FOX_HEADS),
                       g_fox_q=dg_fox_q, g_fox_k=dg_fox_k, g_mem_q=dg_mem_q, g_mem_k=dg_mem_k, g_mlp_norm=dg_mlp)
    return loss, grad_x, G, small_grads, early


BIG = ("w_in", "w_mem_kv", "w_branch_sb", "w_branch_fox", "w_branch_mem", "w_out", "w_ff_up", "w_ff_down")
COLUMN_SHARDED = ("w_in", "w_branch_sb", "w_branch_fox", "w_branch_mem", "w_ff_up")
SMALL = ("g_mix_norm", "g_mem_norm", "b_forget", "g_fox_q", "g_fox_k", "g_mem_q", "g_mem_k", "g_mlp_norm")
ORDER = ("g_mix_norm", "g_mem_norm", "w_in", "b_forget", "g_fox_q", "g_fox_k", "g_mem_q", "g_mem_k", "w_mem_kv",
         "w_branch_sb", "w_branch_fox", "w_branch_mem", "w_out", "g_mlp_norm", "w_ff_up", "w_ff_down")


def _unshard(name, gathered):
    n, _, rh, c = gathered.shape
    t = gathered.reshape(n, 2 * rh, c)
    if name in COLUMN_SHARDED:
        return t.transpose(1, 0, 2).reshape(2 * rh, n * c)
    return t.reshape(n * 2 * rh, c)


def _reshard(name, full):
    if name in COLUMN_SHARDED:
        r, c = full.shape
        t = full.reshape(r, N_CHIPS, c // N_CHIPS).transpose(1, 0, 2)
    else:
        r, c = full.shape[0] // N_CHIPS, full.shape[1]
        t = full.reshape(N_CHIPS, r, c)
    return t.reshape(N_CHIPS, 2, t.shape[1] // 2, t.shape[2])


ROW_TILE = 16
IN_BUF_ALIGN = 256


def _in_segments(D):
    n_qkv = 6 * D_SB
    o_mq, o_gate = n_qkv + FOX_HEADS, n_qkv + FOX_HEADS + D_MEM
    return [(0, n_qkv, 3 * D), (n_qkv, o_mq, 3 * D + n_qkv + D_MEM), (o_mq, o_gate, 3 * D + n_qkv), (o_gate, o_gate + 3 * D, 0)]


class _InLayout:
    def __init__(self, D, shard, n):
        self.D, self.shard, self.n = D, shard, n
        down = lambda v: v // ROW_TILE * ROW_TILE
        up = lambda v: -(-v // ROW_TILE) * ROW_TILE
        self.pieces = []
        ends = []
        for s in range(n):
            cursor, mine = 0, []
            for a, b, p in _in_segments(D):
                x0, x1 = max(a, s * shard), min(b, (s + 1) * shard)
                if x0 < x1:
                    p0 = p + x0 - a
                    rows = up(p0 + x1 - x0) - down(p0)
                    mine.append((x0 - s * shard, x1 - x0, p0, cursor, rows))
                    cursor += rows
            self.pieces.append(mine)
            ends.append(cursor)
        self.rows = -(-max(ends) // IN_BUF_ALIGN) * IN_BUF_ALIGN
        self.padded_rows = 3 * D + 6 * D_SB + D_MEM + F_PAD

    def _per_shard(self, fn, chip, operand):
        return lax.switch(chip, [functools.partial(fn, s) for s in range(self.n)], operand)

    def pack(self, chip, rows):
        def one(s, t):
            out, at = [], 0
            for x0, n_rows, p0, start, region in self.pieces[s]:
                lead = p0 % ROW_TILE
                out += [jnp.zeros((start + lead - at, t.shape[1]), t.dtype), t[x0:x0 + n_rows]]
                at = start + lead + n_rows
            return jnp.concatenate(out + [jnp.zeros((self.rows - at, t.shape[1]), t.dtype)], axis=0)
        return self._per_shard(one, chip, rows)

    def unpack(self, chip, buf, pad_to):
        def one(s, t):
            out = [t[start + p0 % ROW_TILE:start + p0 % ROW_TILE + n_rows] for _, n_rows, p0, start, _ in self.pieces[s]]
            return jnp.concatenate(out + [jnp.zeros((pad_to - self.shard, t.shape[1]), t.dtype)], axis=0)
        return self._per_shard(one, chip, buf)

    def to_padded(self, bufs):
        runs = sorted((p0, s, start, region) for s in range(self.n) for _, _, p0, start, region in self.pieces[s])
        chunks, end = [], 0
        for p0, s, start, region in runs:
            d0 = p0 // ROW_TILE * ROW_TILE
            src = bufs[s, start:start + region]
            if d0 < end:
                assert end - d0 == ROW_TILE
                last = chunks.pop()
                chunks += [last[:-ROW_TILE], last[-ROW_TILE:] + src[:ROW_TILE], src[ROW_TILE:]]
            else:
                if d0 > end:
                    chunks.append(jnp.zeros((d0 - end, bufs.shape[2]), bufs.dtype))
                chunks.append(src)
            end = d0 + region
        chunks.append(jnp.zeros((self.padded_rows - end, bufs.shape[2]), bufs.dtype))
        return jnp.concatenate(chunks, axis=0)

    def from_padded(self, gp):
        bufs = []
        for s in range(self.n):
            out, at = [], 0
            for _, n_rows, p0, start, region in self.pieces[s]:
                d0 = p0 // ROW_TILE * ROW_TILE
                row = d0 + lax.broadcasted_iota(jnp.int32, (region, 1), 0)
                out.append(jnp.where((row >= p0) & (row < p0 + n_rows), gp[d0:d0 + region], jnp.zeros((), gp.dtype)))
                at = start + region
            bufs.append(jnp.concatenate(out + [jnp.zeros((self.rows - at, gp.shape[1]), gp.dtype)], axis=0))
        return jnp.stack(bufs)


SMALL_ROWS = 16


def _pack_small(vals, scalar=None):
    width = max(vals[k].shape[1] for k in SMALL)
    rows = [jnp.pad(vals[k].astype(F32), ((0, 0), (0, width - vals[k].shape[1]))) for k in SMALL]
    extra = jnp.zeros((SMALL_ROWS - len(SMALL), width), F32)
    if scalar is not None:
        extra = extra.at[0, 0].set(scalar)
    return jnp.concatenate(rows + [extra], axis=0)


def _unpack_small(packed, like):
    return {k: packed[i:i + 1, :like[k].shape[1]] for i, k in enumerate(SMALL)}


def kernel(x, mem, g_mix_norm, g_mem_norm, w_in, b_forget, g_fox_q, g_fox_k, g_mem_q, g_mem_k, w_mem_kv, w_branch_sb, w_branch_fox, w_branch_mem, w_out, g_mlp_norm, w_ff_up, w_ff_down, loss_target, m_g_mix_norm, m_g_mem_norm, m_w_in, m_b_forget, m_g_fox_q, m_g_fox_k, m_g_mem_q, m_g_mem_k, m_w_mem_kv, m_w_branch_sb, m_w_branch_fox, m_w_branch_mem, m_w_out, m_g_mlp_norm, m_w_ff_up, m_w_ff_down, v_g_mix_norm, v_g_mem_norm, v_w_in, v_b_forget, v_g_fox_q, v_g_fox_k, v_g_mem_q, v_g_mem_k, v_w_mem_kv, v_w_branch_sb, v_w_branch_fox, v_w_branch_mem, v_w_out, v_g_mlp_norm, v_w_ff_up, v_w_ff_down):
    given = dict(locals())
    D = x.shape[-1]
    weights = {k: given[k] for k in ORDER}
    moms = {k: given["m_" + k] for k in ORDER}
    vars_ = {k: given["v_" + k] for k in ORDER}

    me_chip = 2 * lax.axis_index("x") + lax.axis_index("y")

    n_in = w_in.shape[2]
    lay = _InLayout(D, n_in, N_CHIPS)
    transposed = lambda t: jnp.transpose(t[0])
    shards = {}
    for k in BIG:
        w = weights[k][0].astype(BF16)
        if k == "w_in":
            w = lay.pack(me_chip, jnp.transpose(w))
        shards[k] = w.reshape(2, w.shape[0] // 2, w.shape[1])
    gathered_in = _gather_weights([shards["w_in"]], name="gather_w_in")[0]
    W = {"w_in": lay.to_padded(gathered_in.reshape(N_CHIPS, lay.rows, D))}
    carried = (("w_branch_sb", "w_branch_fox", "w_branch_mem", "w_out"), ("w_ff_up", "w_mem_kv"), ("w_ff_down",))
    rest = [k for grp in carried for k in grp]
    assert sorted(rest + ["w_in"]) == sorted(BIG)

    def finish_weights(groups, arrived):
        names = [k for gi in groups for k in carried[gi]]
        full = [lax.dynamic_update_index_in_dim(o, shards[k], me_chip, 0) for k, o in zip(names, arrived)]
        return {k: _unshard(k, g) for k, g in zip(names, full)}

    me_core = lax.axis_index("c")

    def sum_chips(names, parts, got):
        return {k: _chip_sum(p, q, me_chip, name="sum_chips_" + k) for k, p, q in zip(names, parts, got)}

    def pair_sums(grads, tag):
        names = list(grads)
        stacked = {k: _reshard(k, grads[k]) for k in names if k != "w_in"}
        if "w_in" in grads:
            stacked["w_in"] = lay.from_padded(grads["w_in"]).reshape(N_CHIPS, 2, lay.rows // 2, D)
        got = _exchange_siblings([stacked[k] for k in names], name="exchange_siblings_" + tag)
        return {k: _pair_sum(stacked[k], q, me_core, name="sum_pair_" + k) for k, q in zip(names, got)}

    def reduce_early(grads):
        names = list(grads)
        stacked = [_reshard(k, grads[k]) for k in names]

        def after_siblings(landed, more):
            parts = {k: _pair_sum(t, q, me_core, name="sum_pair_" + k) for k, t, q in zip(names, stacked, landed)}
            parts.update(pair_sums(more, "early"))
            groups = [[k for k in parts if k in ("w_ff_down", "w_ff_up")], [k for k in parts if k not in ("w_ff_down", "w_ff_up")]]
            comms = [_ChipExchange("scatter", [parts[k] for k in grp]) for grp in groups]

            def finish(*got):
                out = {}
                for grp, q in zip(groups, got):
                    out.update(sum_chips(grp, [parts[k] for k in grp], q))
                return out
            return comms, finish
        return _ChipExchange("siblings", stacked), after_siblings

    def reduce_late(grads):
        parts = pair_sums(grads, "late")
        names = list(parts)
        return _ChipExchange("scatter", [parts[k] for k in names]), functools.partial(sum_chips, names, [parts[k] for k in names])

    small = {k: weights[k] for k in SMALL}
    loss_part, grad_x, G, small_grads, halves = _local_step(
        x[0], mem[0], loss_target[0], small, W,
        gather_rest=([_ChipExchange("gather", [shards[k] for k in grp]) for grp in carried], finish_weights),
        reduce_early=reduce_early, reduce_late=reduce_late)
    assert not G, list(G)
    reduced, small_parts = _share_halves([halves[k] for k in BIG], _pack_small(small_grads, loss_part))

    grads, deltas, new_m, new_v = {}, {}, {}, {}
    for k, g in zip(BIG, reduced):
        shp = weights[k].shape
        if k == "w_in":
            g2 = lay.unpack(me_chip, g.reshape(lay.rows, D), lay.rows)
            padded = lambda t: jnp.pad(transposed(t), ((0, lay.rows - n_in), (0, 0)))
            outs = _adamw(padded(weights[k]), g2, padded(moms[k]), padded(vars_[k]), name="adamw_" + k)
            g2, d, m2, v2 = [jnp.transpose(t[:n_in]) for t in (g2, *outs)]
        else:
            g2 = g.reshape(shp[1], shp[2])
            d, m2, v2 = _adamw(weights[k][0], g2, moms[k][0], vars_[k][0], name="adamw_" + k)
        grads[k], deltas[k], new_m[k], new_v[k] = g2.reshape(shp), d.reshape(shp), m2.reshape(shp), v2.reshape(shp)
    sg, sd, sm, sv = _adamw_small(_pack_small(small), small_parts, _pack_small({k: moms[k] for k in SMALL}),
                                  _pack_small({k: vars_[k] for k in SMALL}), name="adamw_small")
    for dst, packed in ((grads, sg), (deltas, sd), (new_m, sm), (new_v, sv)):
        dst.update(_unpack_small(packed, small))

    loss = sg[len(SMALL), 0]
    return (loss, grad_x[None], *[grads[k] for k in ORDER], *[deltas[k] for k in ORDER],
            *[new_m[k] for k in ORDER], *[new_v[k] for k in ORDER])
```

```python
import functools

import jax
import jax.numpy as jnp
from jax import lax
from jax.experimental import pallas as pl
from jax.experimental.pallas import tpu as pltpu

F32 = jnp.float32
BF16 = jnp.bfloat16
MESH_ID = pl.DeviceIdType.MESH

HD = 64
SB_HEADS = 8
FOX_HEADS = 8
MEM_HEADS = 4
MEM_HD = 128
D_SB = SB_HEADS * HD
D_FOX = FOX_HEADS * HD
D_MEM = MEM_HEADS * MEM_HD
EPS = 1e-6
NEG_INF = -1e30

ADAM_LR = 0.001
ADAM_B1 = 0.9
ADAM_B2 = 0.999
ADAM_EPS = 1e-08
ADAM_WD = 0.01
ADAM_STEP = 10

N_CHIPS = 4
VMEM_LIMIT = 56 * 1024 * 1024

F_PAD = 256


def _tile(n, target, align=128):
    if n <= target:
        return n
    best = None
    t = align
    while t <= target:
        if n % t == 0:
            best = t
        t += align
    assert best is not None, (n, target, align)
    return best


def _params(sem):
    return pltpu.CompilerParams(dimension_semantics=sem, vmem_limit_bytes=VMEM_LIMIT)


def _mm(a, b, *, name, ta=False, tb=False, out_dtypes=(F32,), epilogue=None, extras=(),
        tm=1024, tn=1024, tk=2048, comm=None, col_sums=0, a_gain=None):
    if ta:
        K, M = a.shape
    else:
        M, K = a.shape
    if tb:
        N, K2 = b.shape
    else:
        K2, N = b.shape
    assert K == K2, (a.shape, b.shape, ta, tb)
    tm, tn, tk = _tile(M, tm), _tile(N, tn), _tile(K, tk)
    nk = K // tk
    normed = a_gain is not None
    assert not normed or (nk == 1 and not ta)
    n_extra, n_out = len(extras) + normed, len(out_dtypes) + col_sums + normed
    if epilogue is None:
        epilogue = lambda acc: (acc,)
    dims = (((0 if ta else 1,), (1 if tb else 0,)), ((), ()))

    gm, gn = M // tm, N // tn

    def body(*refs):
        i, j, k = pl.program_id(0), pl.program_id(1), pl.program_id(2)
        got = _carry(comm, 2 + n_extra, n_out, (i == 0) & (j == 0) & (k == 0),
                     (i == gm - 1) & (j == gn - 1) & (k == nk - 1), refs)
        (a_ref, b_ref, *extra_refs), out_refs = got[0], list(got[1])
        if normed:
            gain_ref, normed_ref = extra_refs.pop(), out_refs.pop()

            @pl.when(j == 0)
            def _():
                xa = a_ref[...].astype(F32)
                normed_ref[...] = (xa * lax.rsqrt(jnp.mean(xa * xa, axis=-1, keepdims=True) + EPS) * gain_ref[...]).astype(BF16)

            av = normed_ref[...]
        else:
            av = a_ref[...].astype(BF16)
        part = lax.dot_general(av, b_ref[...].astype(BF16), dims, preferred_element_type=F32)

        def finish(acc):
            outs = epilogue(acc, *[r[...] for r in extra_refs])
            for o_ref, o in zip(out_refs[:len(out_dtypes)], outs):
                o_ref[...] = o.astype(o_ref.dtype)
            for o_ref, o in zip(out_refs[len(out_dtypes):], outs[len(out_dtypes):]):
                first = lax.broadcasted_iota(jnp.int32, o_ref.shape, 0) == 0
                o_ref[...] = jnp.where(first, jnp.broadcast_to(o, o_ref.shape), 0.0)

        if nk == 1:
            finish(part)
        else:
            acc_ref = refs[-1]

            @pl.when(k == 0)
            def _():
                acc_ref[...] = part

            @pl.when((k > 0) & (k < nk - 1))
            def _():
                acc_ref[...] += part

            @pl.when(k == nk - 1)
            def _():
                finish(acc_ref[...] + part)

        got[2]()

    a_spec = pl.BlockSpec((tk, tm), lambda i, j, k: (k, i)) if ta else pl.BlockSpec((tm, tk), lambda i, j, k: (i, k))
    b_spec = pl.BlockSpec((tn, tk), lambda i, j, k: (j, k)) if tb else pl.BlockSpec((tk, tn), lambda i, j, k: (k, j))
    mn_spec = pl.BlockSpec((tm, tn), lambda i, j, k: (i, j))
    row_spec = pl.BlockSpec((1, tn), lambda i, j, k: (0, j))
    sum_spec = pl.BlockSpec((8, tn), lambda i, j, k: (i, j))
    c_ins, c_in_specs, c_out_specs, c_out_shape, c_scratch = _comm_args(comm)
    sem = ("parallel", "arbitrary" if normed else "parallel", "arbitrary") if comm is None else ("arbitrary",) * 3
    outs = pl.pallas_call(
        body, name=name,
        grid=(gm, gn, nk),
        in_specs=[a_spec, b_spec] + [row_spec if e.shape[0] == 1 else mn_spec for e in extras]
        + ([pl.BlockSpec((1, K), lambda i, j, k: (0, 0))] if normed else []) + c_in_specs,
        out_specs=[mn_spec] * len(out_dtypes) + [sum_spec] * col_sums
        + ([pl.BlockSpec((tm, K), lambda i, j, k: (i, 0))] if normed else []) + c_out_specs,
        out_shape=[jax.ShapeDtypeStruct((M, N), dt) for dt in out_dtypes]
        + [jax.ShapeDtypeStruct((8 * gm, N), F32)] * col_sums
        + ([jax.ShapeDtypeStruct((M, K), BF16)] if normed else []) + c_out_shape,
        scratch_shapes=c_scratch + ([pltpu.VMEM((tm, tn), F32)] if nk > 1 else []),
        compiler_params=_params(sem),
    )(a, b, *extras, *([a_gain] if normed else []), *c_ins)
    return outs if len(outs) > 1 else outs[0]


def _row_tile(rows, cols, n_arrays):
    budget = 10 * 1024 * 1024
    cols_padded = -(-cols // 128) * 128
    target = max(16, budget // (cols_padded * 4 * n_arrays * 2))
    return _tile(rows, target, align=16)


def _ew(fn, ins, out_dtypes, *, name):
    R, C = ins[0].shape
    n_in, n_out = len(ins), len(out_dtypes)
    tr = _row_tile(R, C, n_in + n_out)

    def body(*refs):
        outs = fn(*[r[...] for r in refs[:n_in]])
        for o_ref, o in zip(refs[n_in:], outs):
            o_ref[...] = o.astype(o_ref.dtype)

    spec = pl.BlockSpec((tr, C), lambda i: (i, 0))
    outs = pl.pallas_call(
        body, name=name, grid=(R // tr,),
        in_specs=[spec] * n_in, out_specs=[spec] * n_out,
        out_shape=[jax.ShapeDtypeStruct((R, C), dt) for dt in out_dtypes],
        compiler_params=_params(("parallel",)),
    )(*ins)
    return outs if n_out > 1 else outs[0]


def _rmsnorm_fwd(x, g, out_dtype, *, name):
    R, d = x.shape
    tr = _row_tile(R, d, 3)

    def body(x_ref, g_ref, o_ref):
        xv = x_ref[...].astype(F32)
        r = lax.rsqrt(jnp.mean(xv * xv, axis=-1, keepdims=True) + EPS)
        o_ref[...] = (xv * r * g_ref[...]).astype(o_ref.dtype)

    return pl.pallas_call(
        body, name=name, grid=(R // tr,),
        in_specs=[pl.BlockSpec((tr, d), lambda i: (i, 0)), pl.BlockSpec((1, d), lambda i: (0, 0))],
        out_specs=pl.BlockSpec((tr, d), lambda i: (i, 0)),
        out_shape=jax.ShapeDtypeStruct((R, d), out_dtype),
        compiler_params=_params(("parallel",)),
    )(x, g)


def _rmsnorm_bwd(x, g, dy, add=None, *, name):
    R, d = x.shape
    has_add = add is not None
    tr = _row_tile(R, d, 5)

    def body(*refs):
        x_ref, g_ref, dy_ref = refs[:3]
        add_ref = refs[3] if has_add else None
        dx_ref, dg_ref = refs[-2:]
        xv = x_ref[...].astype(F32)
        dyv = dy_ref[...].astype(F32)
        r = lax.rsqrt(jnp.mean(xv * xv, axis=-1, keepdims=True) + EPS)
        xh = xv * r
        dyg = dyv * g_ref[...]
        c = jnp.mean(dyg * xh, axis=-1, keepdims=True)
        dx = r * (dyg - xh * c)
        if has_add:
            dx = dx + add_ref[...]
        dx_ref[...] = dx

        @pl.when(pl.program_id(0) == 0)
        def _():
            dg_ref[...] = jnp.zeros_like(dg_ref)

        dg_ref[...] += jnp.sum(dyv * xh, axis=0, keepdims=True)

    row = pl.BlockSpec((tr, d), lambda i: (i, 0))
    vec = pl.BlockSpec((1, d), lambda i: (0, 0))
    ins = [x, g, dy] + ([add] if has_add else [])
    return pl.pallas_call(
        body, name=name, grid=(R // tr,),
        in_specs=[row, vec, row] + ([row] if has_add else []),
        out_specs=[row, vec],
        out_shape=[jax.ShapeDtypeStruct((R, d), F32), jax.ShapeDtypeStruct((1, d), F32)],
        compiler_params=_params(("arbitrary",)),
    )(*ins)


_NT = (((1,), (1,)), ((), ()))
_TN = (((0,), (0,)), ((), ()))


def _dot(a, b, dims=(((1,), (0,)), ((), ()))):
    return lax.dot_general(a, b, dims, preferred_element_type=F32)


def _log_sigmoid_pair(z):
    sp = jnp.log(1.0 + jnp.exp(-jnp.abs(z)))
    return jnp.minimum(z, 0.0) - sp, jnp.minimum(-z, 0.0) - sp


LANES = 128
_LOW = -3e38


def _lane_masks(hd, rows):
    if hd == LANES:
        return [None]
    lane = lax.broadcasted_iota(jnp.int32, (rows, LANES), 1)
    return [(lane >= hh * hd) & (lane < (hh + 1) * hd) for hh in range(LANES // hd)]


def _keep(t, m):
    return t if m is None else jnp.where(m, t, 0.0)


def _merge(parts, masks):
    out = parts[-1]
    for p, m in zip(parts[-2::-1], masks[-2::-1]):
        out = jnp.where(m, p, out)
    return out


def _row_value(t, m):
    return jnp.max(t if m is None else jnp.where(m, t, _LOW), axis=1, keepdims=True)


def _cols(tq, off):
    return pl.BlockSpec((tq, LANES), lambda g, i: (i, off + g))


def _cols_all(rows, off):
    return pl.BlockSpec((rows, LANES), lambda g, i: (0, off + g))


SCAN_BLOCK = 256


def _tri(kind, cols):
    n = min(SCAN_BLOCK, cols)
    r = lax.broadcasted_iota(jnp.int32, (n, n), 0)
    c = lax.broadcasted_iota(jnp.int32, (n, n), 1)
    return ((r > c) if kind == "after" else (r < c)).astype(BF16)


def _scan_cols(x, tri, reverse):
    cols = x.shape[1]
    cb = min(SCAN_BLOCK, cols)
    assert cols % cb == 0 and tri.shape == (cb, cb)
    nb = cols // cb
    blocks = [x[:, b * cb:(b + 1) * cb] for b in range(nb)]
    outs, carry = [None] * nb, None
    for b in (reversed(range(nb)) if reverse else range(nb)):
        y = _dot(blocks[b].astype(BF16), tri)
        outs[b] = y if carry is None else y + carry
        s = jnp.sum(blocks[b], axis=1, keepdims=True)
        carry = s if carry is None else carry + s
    return (outs[0] if nb == 1 else jnp.concatenate(outs, axis=1)), carry


def _softplus_parts(z):
    pos = jnp.maximum(z, 0.0) + jnp.log(1.0 + jnp.exp(-jnp.abs(z)))
    return pos, z - pos


class _ChipExchange:
    def __init__(self, kind, ins):
        assert kind in ("gather", "scatter", "siblings", "forward", "halves")
        self.kind, self.ins = kind, list(ins)
        lead = {"gather": lambda s: (N_CHIPS,) + s, "scatter": lambda s: (3,) + s[1:], "siblings": lambda s: s[:1] + s[2:],
                "forward": lambda s: s, "halves": lambda s: (2,) + s}[kind]
        self.out_shape = [jax.ShapeDtypeStruct(lead(a.shape), a.dtype) for a in ins]
        self.aliases = [(w, w) for w in range(len(ins))] if kind == "forward" else []
        n = {"siblings": N_CHIPS, "halves": 1}.get(kind, 3) * len(ins)
        self.scratch = [pltpu.SemaphoreType.DMA((n,)), pltpu.SemaphoreType.DMA((n,))]

    def _copies(self, in_refs, out_refs, sems, landing):
        x, y, c, chips = _mesh_place()
        me = 2 * x + y
        out = []
        for w in range(len(self.ins)):
            if self.kind == "siblings":
                for s in range(N_CHIPS):
                    land = out_refs[w].at[s]
                    src = land if landing else in_refs[w].at[s, 1 - c]
                    out.append(_remote(src, land, sems, N_CHIPS * w + s, (x, y, 1 - c)))
                continue
            if self.kind == "halves":
                land = out_refs[w].at[1 - c]
                src, dst = (land, land) if landing else (in_refs[w], out_refs[w].at[c])
                out.append(_remote(src, dst, sems, w, (x, y, 1 - c)))
                continue
            for j, chip in enumerate(chips):
                peer = 2 * chip[0] + chip[1]
                to = (chip[0], chip[1], c)
                if self.kind == "gather":
                    src, dst, land = in_refs[w].at[c], out_refs[w].at[me, c], out_refs[w].at[peer, c]
                elif self.kind == "scatter":
                    src, dst, land = in_refs[w].at[peer], out_refs[w].at[j], out_refs[w].at[j]
                else:
                    src, dst, land, to = in_refs[w].at[peer, c], out_refs[w].at[peer, c], out_refs[w].at[peer, 1 - c], (x, y, 1 - c)
                if landing:
                    src, dst = land, land
                out.append(_remote(src, dst, sems, 3 * w + j, to))
        return out

    def start(self, in_refs, out_refs, sems):
        for cp in self._copies(in_refs, out_refs, sems, False):
            cp.start()

    def finish(self, in_refs, out_refs, sems):
        for cp in self._copies(in_refs, out_refs, sems, True):
            cp.wait_recv()
        for cp in self._copies(in_refs, out_refs, sems, False):
            cp.wait_send()


class _Joint:
    def __init__(self, parts):
        self.parts = list(parts)
        self.ins = [a for p in self.parts for a in p.ins]
        self.out_shape = [s for p in self.parts for s in p.out_shape]
        self.scratch = [s for p in self.parts for s in p.scratch]
        self.aliases, n_in, n_out = [], 0, 0
        for p in self.parts:
            self.aliases += [(n_in + i, n_out + o) for i, o in p.aliases]
            n_in, n_out = n_in + len(p.ins), n_out + len(p.out_shape)

    def _each(self, step, in_refs, out_refs, sems):
        i = o = s = 0
        for p in self.parts:
            a, b, c = len(p.ins), len(p.out_shape), len(p.scratch)
            getattr(p, step)(in_refs[i:i + a], out_refs[o:o + b], sems[s:s + c])
            i, o, s = i + a, o + b, s + c

    def start(self, in_refs, out_refs, sems):
        self._each("start", in_refs, out_refs, sems)

    def finish(self, in_refs, out_refs, sems):
        self._each("finish", in_refs, out_refs, sems)


def _carry(comm, n_in, n_out, first, last, refs):
    if comm is None:
        return refs[:n_in], refs[n_in:n_in + n_out], (lambda: None)
    a, b = len(comm.ins), len(comm.out_shape)
    ins, c_in = refs[:n_in], refs[n_in:n_in + a]
    outs, c_out = refs[n_in + a:n_in + a + n_out], refs[n_in + a + n_out:n_in + a + n_out + b]
    sems = refs[n_in + a + n_out + b:n_in + a + n_out + b + len(comm.scratch)]
    pl.when(first)(lambda: comm.start(c_in, c_out, sems))
    return ins, outs, (lambda: pl.when(last)(lambda: comm.finish(c_in, c_out, sems)))


def _sbl_fwd(q, k, v, *, width, hd, name, tq=256, comm=None):
    (qa, qo), (ka, ko), (va, vo) = q, k, v
    S = qa.shape[0]
    tq = _tile(S, tq)
    tk = tq
    scale = hd ** -0.5
    n_g, n_q = width // LANES, S // tq

    def body(*refs):
        qi = pl.program_id(1)
        gi = pl.program_id(0)
        got = _carry(comm, 3, 2, (gi == 0) & (qi == 0), (gi == n_g - 1) & (qi == n_q - 1), refs)
        (q_ref, k_ref, v_ref), (o_ref, tot_ref) = got[0], got[1]
        masks = _lane_masks(hd, tq)
        qs = q_ref[...].astype(F32) * scale
        qm = [_keep(qs, m).astype(BF16) for m in masks]
        strict = lax.broadcasted_iota(jnp.int32, (tq, tk), 1) < lax.broadcasted_iota(jnp.int32, (tq, tk), 0)
        later = _tri("after", tk)

        def tile(kb, carry, diag):
            ks = pl.multiple_of(kb * tk, tk)
            kv = k_ref[pl.ds(ks, tk), :].astype(BF16)
            vv = v_ref[pl.ds(ks, tk), :].astype(BF16)
            out = []
            for hh in range(len(masks)):
                acc, c_pos = carry[2 * hh], carry[2 * hh + 1]
                pos, ls = _softplus_parts(_dot(qm[hh], kv, _NT))
                if diag:
                    pos = jnp.where(strict, pos, 0.0)
                pos_after, pos_all = _scan_cols(pos, later, True)
                w = jnp.exp(ls - (pos_after + c_pos))
                if diag:
                    w = jnp.where(strict, w, 0.0)
                out += [acc + _dot(w.astype(BF16), vv), c_pos + pos_all]
            return tuple(out)

        init = (jnp.zeros((tq, LANES), F32), jnp.zeros((tq, 1), F32)) * len(masks)
        carry = tile(qi, init, True)
        carry = lax.fori_loop(0, qi, lambda i, c: tile(qi - 1 - i, c, False), carry)
        o_ref[...] = _merge(carry[0::2], masks).astype(o_ref.dtype)
        tot_ref[...] = _merge([jnp.broadcast_to(-c, (tq, LANES)) for c in carry[1::2]], masks)
        got[2]()

    c_ins, c_in_specs, c_out_specs, c_out_shape, c_scratch = _comm_args(comm)
    return pl.pallas_call(
        body, name=name, grid=(n_g, n_q),
        in_specs=[_cols(tq, qo), _cols_all(S, ko), _cols_all(S, vo)] + c_in_specs,
        out_specs=[_cols(tq, 0), _cols(tq, 0)] + c_out_specs,
        out_shape=[jax.ShapeDtypeStruct((S, width), BF16), jax.ShapeDtypeStruct((S, width), F32)] + c_out_shape,
        scratch_shapes=c_scratch,
        compiler_params=_params(("arbitrary", "arbitrary")),
    )(qa, ka, va, *c_ins)


def _comm_args(comm, aliased=False):
    if comm is None:
        return [], [], [], [], []
    assert aliased or not comm.aliases
    return comm.ins, [_ANY] * len(comm.ins), [_ANY] * len(comm.out_shape), comm.out_shape, comm.scratch


def _comm_aliases(comm, n_in, n_out):
    return {} if comm is None else {n_in + i: n_out + o for i, o in comm.aliases}


def _sbl_bwd(q, k, v, do, tot, *, width, hd, name, tq=256, comm=None):
    (qa, qo), (ka, ko), (va, vo) = q, k, v
    S = qa.shape[0]
    tq = _tile(S, tq)
    tk = tq
    scale = hd ** -0.5
    n_g, n_q = width // LANES, S // tq

    def body(*refs):
        qi = pl.program_id(1)
        gi = pl.program_id(0)
        got = _carry(comm, 5, 3, (gi == 0) & (qi == 0), (gi == n_g - 1) & (qi == n_q - 1), refs)
        (q_ref, k_ref, v_ref, do_ref, tot_ref), (dq_ref, dk_ref, dv_ref) = got[0], got[1]

        @pl.when(qi == 0)
        def _():
            dk_ref[...] = jnp.zeros_like(dk_ref)
            dv_ref[...] = jnp.zeros_like(dv_ref)

        masks = _lane_masks(hd, tq)
        qs = q_ref[...].astype(F32) * scale
        qm = [_keep(qs, m).astype(BF16) for m in masks]
        dov = [_keep(do_ref[...], m).astype(BF16) for m in masks]
        rest = [-_row_value(tot_ref[...], m) for m in masks]
        strict = lax.broadcasted_iota(jnp.int32, (tq, tk), 1) < lax.broadcasted_iota(jnp.int32, (tq, tk), 0)
        later, before = _tri("after", tk), _tri("before", tk)

        def tile(kb, carry, diag):
            ks = pl.multiple_of(kb * tk, tk)
            kv = k_ref[pl.ds(ks, tk), :].astype(BF16)
            vv = v_ref[pl.ds(ks, tk), :].astype(BF16)
            out = []
            dk_t, dv_t = None, None
            for hh in range(len(masks)):
                dq, c_pos, c_g = carry[3 * hh:3 * hh + 3]
                pos, ls = _softplus_parts(_dot(qm[hh], kv, _NT))
                if diag:
                    pos = jnp.where(strict, pos, 0.0)
                pos_after, pos_all = _scan_cols(pos, later, True)
                c_pos = c_pos + pos_all
                w = jnp.exp(ls - (pos_after + (rest[hh] - c_pos)))
                if diag:
                    w = jnp.where(strict, w, 0.0)
                g = _dot(dov[hh], vv, _NT) * w
                g_before, g_all = _scan_cols(g, before, False)
                g_before = g_before + c_g
                dz = g - jnp.exp(ls) * (g + g_before)
                if diag:
                    dz = jnp.where(strict, dz, 0.0)
                dzb = dz.astype(BF16)
                dk_h = _dot(dzb, qm[hh], _TN)
                dv_h = _dot(w.astype(BF16), dov[hh], _TN)
                dk_t = dk_h if dk_t is None else dk_t + dk_h
                dv_t = dv_h if dv_t is None else dv_t + dv_h
                out += [dq + _dot(dzb, kv), c_pos, c_g + g_all]
            dk_ref[pl.ds(ks, tk), :] += dk_t
            dv_ref[pl.ds(ks, tk), :] += dv_t
            return tuple(out)

        zero = jnp.zeros((tq, 1), F32)
        init = (jnp.zeros((tq, LANES), F32), zero, zero) * len(masks)
        carry = lax.fori_loop(0, qi, lambda kb, c: tile(kb, c, False), init)
        carry = tile(qi, carry, True)
        dq_ref[...] = _merge(carry[0::3], masks) * scale
        got[2]()

    full = jax.ShapeDtypeStruct((S, width), F32)
    c_ins, c_in_specs, c_out_specs, c_out_shape, c_scratch = _comm_args(comm)
    return pl.pallas_call(
        body, name=name, grid=(n_g, n_q),
        in_specs=[_cols(tq, qo), _cols_all(S, ko), _cols_all(S, vo), _cols(tq, do[1]), _cols(tq, tot[1])] + c_in_specs,
        out_specs=[_cols(tq, 0), _cols_all(S, 0), _cols_all(S, 0)] + c_out_specs,
        out_shape=[full, full, full] + c_out_shape,
        scratch_shapes=c_scratch,
        compiler_params=_params(("arbitrary", "arbitrary")),
    )(qa, ka, va, do[0], tot[0], *c_ins)


def _sml_fwd(q, k, v, bias=None, *, width, hd, causal, name, tq=256, tk=256, comm=None):
    (qa, qo), (ka, ko), (va, vo) = q, k, v
    S, Sk = qa.shape[0], ka.shape[0]
    tq, tk = _tile(S, tq), _tile(Sk, tk)
    if causal:
        assert tq == tk and S == Sk
    nk = Sk // tk
    hpg = LANES // hd
    scale = hd ** -0.5
    has_bias = bias is not None
    n_g, n_q = width // LANES, S // tq

    def body(*all_refs):
        qi, gi = pl.program_id(1), pl.program_id(0)
        got = _carry(comm, 5 if has_bias else 3, 2, (gi == 0) & (qi == 0), (gi == n_g - 1) & (qi == n_q - 1), all_refs)
        refs = tuple(got[0]) + tuple(got[1])
        q_ref, k_ref, v_ref = refs[:3]
        o_ref, lse_ref = refs[-2:]
        masks = _lane_masks(hd, tq)
        qs = q_ref[...].astype(F32) * scale
        qm = [_keep(qs, m).astype(BF16) for m in masks]
        allowed = lax.broadcasted_iota(jnp.int32, (tq, tk), 1) <= lax.broadcasted_iota(jnp.int32, (tq, tk), 0)

        def tile(kb, carry, diag):
            ks = pl.multiple_of(kb * tk, tk)
            kv = k_ref[pl.ds(ks, tk), :].astype(BF16)
            vv = v_ref[pl.ds(ks, tk), :].astype(BF16)
            out = []
            for hh in range(hpg):
                m, l, acc = carry[3 * hh:3 * hh + 3]
                z = _dot(qm[hh], kv, _NT)
                if has_bias:
                    z = z + refs[3][hh] - refs[4][hh, kb]
                if diag:
                    z = jnp.where(allowed, z, NEG_INF)
                m2 = jnp.maximum(m, jnp.max(z, axis=1, keepdims=True))
                p = jnp.exp(z - m2)
                alpha = jnp.exp(m - m2)
                out += [m2, alpha * l + jnp.sum(p, axis=1, keepdims=True), alpha * acc + _dot(p.astype(BF16), vv)]
            return tuple(out)

        init = (jnp.full((tq, 1), NEG_INF, F32), jnp.zeros((tq, 1), F32), jnp.zeros((tq, LANES), F32)) * hpg
        if causal:
            carry = lax.fori_loop(0, qi, lambda kb, c: tile(kb, c, False), init)
            carry = tile(qi, carry, True)
        else:
            carry = lax.fori_loop(0, nk, lambda kb, c: tile(kb, c, False), init)
        o_ref[...] = _merge([acc / l for l, acc in zip(carry[1::3], carry[2::3])], masks).astype(o_ref.dtype)
        lse_ref[...] = _merge([jnp.broadcast_to(m + jnp.log(l), (tq, LANES)) for m, l in zip(carry[0::3], carry[1::3])], masks)
        got[2]()

    in_specs = [_cols(tq, qo), _cols_all(Sk, ko), _cols_all(Sk, vo)]
    ins = [qa, ka, va]
    if has_bias:
        in_specs += [pl.BlockSpec((hpg, tq, 1), lambda g, i: (g, i, 0)),
                     pl.BlockSpec((hpg, nk, 1, tk), lambda g, i: (g, 0, 0, 0))]
        ins += list(bias)
    c_ins, c_in_specs, c_out_specs, c_out_shape, c_scratch = _comm_args(comm, aliased=True)
    return pl.pallas_call(
        body, name=name, grid=(n_g, n_q),
        in_specs=in_specs + c_in_specs, out_specs=[_cols(tq, 0), _cols(tq, 0)] + c_out_specs,
        out_shape=[jax.ShapeDtypeStruct((S, width), BF16), jax.ShapeDtypeStruct((S, width), F32)] + c_out_shape,
        input_output_aliases=_comm_aliases(comm, len(ins), 2),
        scratch_shapes=c_scratch,
        compiler_params=_params(("arbitrary", "arbitrary") if comm is not None else ("parallel", "arbitrary")),
    )(*ins, *c_ins)


def _sml_bwd(q, k, v, o, lse, do, bias=None, *, width, hd, causal, name, tq=256, tk=256, comm=None):
    (qa, qo), (ka, ko), (va, vo) = q, k, v
    S, Sk = qa.shape[0], ka.shape[0]
    tq, tk = _tile(S, tq), _tile(Sk, tk)
    nk = Sk // tk
    hpg = LANES // hd
    scale = hd ** -0.5
    has_bias = bias is not None
    n_in = 8 if has_bias else 6
    n_g, n_q = width // LANES, S // tq

    def body(*all_refs):
        qi, gi = pl.program_id(1), pl.program_id(0)
        got = _carry(comm, n_in, 5 if has_bias else 3, (gi == 0) & (qi == 0), (gi == n_g - 1) & (qi == n_q - 1), all_refs)
        refs = tuple(got[0]) + tuple(got[1])
        q_ref, k_ref, v_ref, o_ref, lse_ref, do_ref = refs[:6]
        dq_ref, dk_ref, dv_ref = refs[n_in:n_in + 3]

        @pl.when(qi == 0)
        def _():
            dk_ref[...] = jnp.zeros_like(dk_ref)
            dv_ref[...] = jnp.zeros_like(dv_ref)
            if has_bias:
                refs[n_in + 4][...] = jnp.zeros_like(refs[n_in + 4])

        masks = _lane_masks(hd, tq)
        qs = q_ref[...].astype(F32) * scale
        qm = [_keep(qs, m).astype(BF16) for m in masks]
        do32 = do_ref[...]
        dov = [_keep(do32, m).astype(BF16) for m in masks]
        prod = do32 * o_ref[...].astype(F32)
        delta = [jnp.sum(_keep(prod, m), axis=1, keepdims=True) for m in masks]
        lses = [_row_value(lse_ref[...], m) for m in masks]
        allowed = lax.broadcasted_iota(jnp.int32, (tq, tk), 1) <= lax.broadcasted_iota(jnp.int32, (tq, tk), 0)

        def tile(kb, carry, diag):
            ks = pl.multiple_of(kb * tk, tk)
            kv = k_ref[pl.ds(ks, tk), :].astype(BF16)
            vv = v_ref[pl.ds(ks, tk), :].astype(BF16)
            out = []
            dk_t, dv_t = None, None
            for hh in range(hpg):
                dq, db_row = carry[2 * hh:2 * hh + 2]
                z = _dot(qm[hh], kv, _NT)
                if has_bias:
                    z = z + refs[6][hh] - refs[7][hh, kb]
                p = jnp.exp(z - lses[hh])
                if diag:
                    p = jnp.where(allowed, p, 0.0)
                dz = p * (_dot(dov[hh], vv, _NT) - delta[hh])
                dzb = dz.astype(BF16)
                dk_h = _dot(dzb, qm[hh], _TN)
                dv_h = _dot(p.astype(BF16), dov[hh], _TN)
                dk_t = dk_h if dk_t is None else dk_t + dk_h
                dv_t = dv_h if dv_t is None else dv_t + dv_h
                if has_bias:
                    db_row = db_row + jnp.sum(dz, axis=1, keepdims=True)
                    refs[n_in + 4][hh, kb] += jnp.sum(dz, axis=0, keepdims=True)
                out += [dq + _dot(dzb, kv), db_row]
            dk_ref[pl.ds(ks, tk), :] += dk_t
            dv_ref[pl.ds(ks, tk), :] += dv_t
            return tuple(out)

        init = (jnp.zeros((tq, LANES), F32), jnp.zeros((tq, 1), F32)) * hpg
        if causal:
            carry = lax.fori_loop(0, qi, lambda kb, c: tile(kb, c, False), init)
            carry = tile(qi, carry, True)
        else:
            carry = lax.fori_loop(0, nk, lambda kb, c: tile(kb, c, False), init)
        dq_ref[...] = _merge(carry[0::2], masks) * scale
        if has_bias:
            for hh in range(hpg):
                refs[n_in + 3][hh] = carry[2 * hh + 1]
        got[2]()

    in_specs = [_cols(tq, qo), _cols_all(Sk, ko), _cols_all(Sk, vo), _cols(tq, o[1]), _cols(tq, lse[1]), _cols(tq, do[1])]
    ins = [qa, ka, va, o[0], lse[0], do[0]]
    out_specs = [_cols(tq, 0), _cols_all(Sk, 0), _cols_all(Sk, 0)]
    out_shape = [jax.ShapeDtypeStruct((S, width), F32), jax.ShapeDtypeStruct((Sk, width), F32),
                 jax.ShapeDtypeStruct((Sk, width), F32)]
    if has_bias:
        rspec = pl.BlockSpec((hpg, tq, 1), lambda g, i: (g, i, 0))
        cspec = pl.BlockSpec((hpg, nk, 1, tk), lambda g, i: (g, 0, 0, 0))
        in_specs += [rspec, cspec]
        ins += list(bias)
        out_specs += [rspec, cspec]
        n_heads = width // hd
        out_shape += [jax.ShapeDtypeStruct((n_heads, S, 1), F32), jax.ShapeDtypeStruct((n_heads, nk, 1, tk), F32)]
    c_ins, c_in_specs, c_out_specs, c_out_shape, c_scratch = _comm_args(comm)
    return pl.pallas_call(
        body, name=name, grid=(n_g, n_q),
        in_specs=in_specs + c_in_specs, out_specs=out_specs + c_out_specs, out_shape=out_shape + c_out_shape,
        scratch_shapes=c_scratch,
        compiler_params=_params(("arbitrary", "arbitrary") if comm is not None else ("parallel", "arbitrary")),
    )(*ins, *c_ins)


def _head_sums(t, masks):
    sums = [jnp.sum(_keep(t, m), axis=1, keepdims=True) for m in masks]
    return _merge([jnp.broadcast_to(s, t.shape) for s in sums], masks) if len(masks) > 1 else sums[0]


def _hnorm_fwd(x, g_lanes, *, width, hd, name, tr=1024):
    xa, xo = x
    R = xa.shape[0]
    tr = _tile(R, tr, align=16)
    n_blk = width // LANES

    def body(x_ref, g_ref, o_ref):
        masks = _lane_masks(hd, tr)
        for j in range(n_blk):
            sl = slice(j * LANES, (j + 1) * LANES)
            xv = x_ref[:, sl].astype(F32)
            r = lax.rsqrt(_head_sums(xv * xv, masks) * (1.0 / hd) + EPS)
            o_ref[:, sl] = (xv * r * g_ref[...]).astype(o_ref.dtype)

    assert (xo * LANES) % width == 0
    return pl.pallas_call(
        body, name=name, grid=(R // tr,),
        in_specs=[pl.BlockSpec((tr, width), lambda i: (i, xo * LANES // width)), pl.BlockSpec((1, LANES), lambda i: (0, 0))],
        out_specs=pl.BlockSpec((tr, width), lambda i: (i, 0)),
        out_shape=jax.ShapeDtypeStruct((R, width), BF16),
        compiler_params=_params(("parallel",)),
    )(xa, g_lanes)


def _hnorm_bwd(x, g_lanes, dy, *, width, hd, name, tr=1024):
    xa, xo = x
    R = xa.shape[0]
    tr = _tile(R, tr, align=16)
    n_blk = width // LANES

    def body(x_ref, g_ref, dy_ref, dx_ref, dg_ref):
        masks = _lane_masks(hd, tr)
        dg = jnp.zeros((1, LANES), F32)
        for j in range(n_blk):
            sl = slice(j * LANES, (j + 1) * LANES)
            xv = x_ref[:, sl].astype(F32)
            dyv = dy_ref[:, sl].astype(F32)
            r = lax.rsqrt(_head_sums(xv * xv, masks) * (1.0 / hd) + EPS)
            xh = xv * r
            dyg = dyv * g_ref[...]
            c = _head_sums(dyg * xh, masks) * (1.0 / hd)
            dx_ref[:, sl] = (r * (dyg - xh * c)).astype(dx_ref.dtype)
            dg = dg + jnp.sum(dyv * xh, axis=0, keepdims=True)
        if hd * 2 == LANES:
            dg8 = jnp.broadcast_to(dg, (8, LANES))
            dg = (dg8 + pltpu.roll(dg8, shift=hd, axis=1))[0:1]
        else:
            assert hd == LANES

        @pl.when(pl.program_id(0) == 0)
        def _():
            dg_ref[...] = jnp.zeros_like(dg_ref)

        dg_ref[...] += dg

    assert (xo * LANES) % width == 0
    return pl.pallas_call(
        body, name=name, grid=(R // tr,),
        in_specs=[pl.BlockSpec((tr, width), lambda i: (i, xo * LANES // width)), pl.BlockSpec((1, LANES), lambda i: (0, 0)),
                  pl.BlockSpec((tr, width), lambda i: (i, 0))],
        out_specs=[pl.BlockSpec((tr, width), lambda i: (i, 0)), pl.BlockSpec((1, LANES), lambda i: (0, 0))],
        out_shape=[jax.ShapeDtypeStruct((R, width), BF16), jax.ShapeDtypeStruct((1, LANES), F32)],
        compiler_params=_params(("arbitrary",)),
    )(xa, g_lanes, dy)


def _split3_dot(x, tri):
    a = x.astype(BF16)
    r = x - a.astype(F32)
    b = r.astype(BF16)
    c = (r - b.astype(F32)).astype(BF16)
    return _dot(a, tri) + _dot(b, tri) + _dot(c, tri)


def _forget_fwd(logit_t, b_col, *, name, blk=512):
    H, S = logit_t.shape
    blk = _tile(S, blk)

    def body(l_ref, b_ref, f_ref):
        r_i = lax.broadcasted_iota(jnp.int32, (blk, blk), 0)
        c_i = lax.broadcasted_iota(jnp.int32, (blk, blk), 1)
        upto = (r_i <= c_i).astype(BF16)
        carry = jnp.zeros((H, 1), F32)
        for j in range(S // blk):
            u = l_ref[:, j * blk:(j + 1) * blk] + b_ref[...]
            lf, _ = _log_sigmoid_pair(u)
            f_ref[:, j * blk:(j + 1) * blk] = _split3_dot(lf, upto) + carry
            carry = carry + jnp.sum(lf, axis=1, keepdims=True)

    return pl.pallas_call(
        body, name=name,
        out_shape=jax.ShapeDtypeStruct((H, S), F32),
        compiler_params=pltpu.CompilerParams(vmem_limit_bytes=VMEM_LIMIT),
    )(logit_t, b_col)


def _forget_bwd(logit_t, b_col, d_f, *, name, blk=512):
    H, S = logit_t.shape
    blk = _tile(S, blk)

    def body(l_ref, b_ref, df_ref, dl_ref, db_ref):
        r_i = lax.broadcasted_iota(jnp.int32, (blk, blk), 0)
        c_i = lax.broadcasted_iota(jnp.int32, (blk, blk), 1)
        fromon = (r_i >= c_i).astype(BF16)
        carry = jnp.zeros((H, 1), F32)
        db = jnp.zeros((H, 1), F32)
        for j in reversed(range(S // blk)):
            sl = slice(j * blk, (j + 1) * blk)
            dfv = df_ref[:, sl]
            d_lf = _split3_dot(dfv, fromon) + carry
            carry = carry + jnp.sum(dfv, axis=1, keepdims=True)
            u = l_ref[:, sl] + b_ref[...]
            _, lsn = _log_sigmoid_pair(u)
            dl = d_lf * jnp.exp(lsn)
            dl_ref[:, sl] = dl
            db = db + jnp.sum(dl, axis=1, keepdims=True)
        db_ref[...] = db

    return pl.pallas_call(
        body, name=name,
        out_shape=[jax.ShapeDtypeStruct((H, S), F32), jax.ShapeDtypeStruct((H, 1), F32)],
        compiler_params=pltpu.CompilerParams(vmem_limit_bytes=VMEM_LIMIT),
    )(logit_t, b_col, d_f)


def _sigmoid(t):
    return 1.0 / (1.0 + jnp.exp(-t))


def _gate_fwd(o3, w3, proj, x, w_out, g_norm, D, *, name, tm=512):
    S = proj.shape[0]
    tm = _tile(S, tm)

    def body(o0, o1, o2, w0, w1, w2, g0, g1, g2, x_ref, wo_ref, gn_ref, merged_ref, x1_ref, h2_ref):
        acc = None
        for o_ref, w_ref, g_ref in ((o0, w0, g0), (o1, w1, g1), (o2, w2, g2)):
            t = _sigmoid(g_ref[...]) * _dot(o_ref[...], w_ref[...])
            acc = t if acc is None else acc + t
        merged = acc.astype(BF16)
        merged_ref[...] = merged
        x1 = x_ref[...] + _dot(merged, wo_ref[...])
        x1_ref[...] = x1
        h2_ref[...] = (x1 * lax.rsqrt(jnp.mean(x1 * x1, axis=-1, keepdims=True) + EPS) * gn_ref[...]).astype(BF16)

    ospec = lambda d: pl.BlockSpec((tm, d), lambda i: (i, 0))
    wspec = lambda w: pl.BlockSpec(w.shape, lambda i: (0, 0))
    gspec = lambda j: pl.BlockSpec((tm, D), lambda i: (i, j))
    row = pl.BlockSpec((tm, D), lambda i: (i, 0))
    return pl.pallas_call(
        body, name=name, grid=(S // tm,),
        in_specs=[ospec(o.shape[1]) for o in o3] + [wspec(w) for w in w3] + [gspec(j) for j in range(3)]
        + [row, wspec(w_out), wspec(g_norm)],
        out_specs=[row, row, row],
        out_shape=[jax.ShapeDtypeStruct((S, D), BF16), jax.ShapeDtypeStruct((S, D), F32), jax.ShapeDtypeStruct((S, D), BF16)],
        compiler_params=_params(("parallel",)),
    )(*o3, *w3, proj, proj, proj, x, w_out, g_norm)


def _gate_bwd(o3, w3, proj, dx1, w_out, D, *, name, tm=256):
    S = proj.shape[0]
    tm = _tile(S, tm)

    def body(o0, o1, o2, w0, w1, w2, g0, g1, g2, dx_ref, wo_ref, dg_ref, dw0, dw1, dw2, do0, do1, do2):
        first = pl.program_id(0) == 0
        dm = _dot(dx_ref[...].astype(BF16), wo_ref[...], _NT)
        for j, (o_ref, w_ref, g_ref, dw_ref, do_ref) in enumerate(
                ((o0, w0, g0, dw0, do0), (o1, w1, g1, dw1, do1), (o2, w2, g2, dw2, do2))):
            s = _sigmoid(g_ref[...])
            br = _dot(o_ref[...], w_ref[...])
            dg_ref[:, j * D:(j + 1) * D] = (dm * br * s * (1.0 - s)).astype(dg_ref.dtype)
            dbr = (dm * s).astype(BF16)
            do_ref[...] = _dot(dbr, w_ref[...], _NT)
            part = _dot(o_ref[...], dbr, _TN)

            @pl.when(first)
            def _():
                dw_ref[...] = part

            @pl.when(jnp.logical_not(first))
            def _():
                dw_ref[...] += part

    ospec = lambda d: pl.BlockSpec((tm, d), lambda i: (i, 0))
    wspec = lambda w: pl.BlockSpec(w.shape, lambda i: (0, 0))
    gspec = lambda j: pl.BlockSpec((tm, D), lambda i: (i, j))
    dspec = pl.BlockSpec((tm, D), lambda i: (i, 0))
    return pl.pallas_call(
        body, name=name, grid=(S // tm,),
        in_specs=[ospec(o.shape[1]) for o in o3] + [wspec(w) for w in w3] + [gspec(j) for j in range(3)]
        + [dspec, wspec(w_out)],
        out_specs=[pl.BlockSpec((tm, 3 * D), lambda i: (i, 0))] + [wspec(w) for w in w3] + [ospec(o.shape[1]) for o in o3],
        out_shape=[jax.ShapeDtypeStruct((S, proj.shape[1]), BF16)] + [jax.ShapeDtypeStruct(w.shape, F32) for w in w3]
        + [jax.ShapeDtypeStruct((S, o.shape[1]), F32) for o in o3],
        compiler_params=_params(("arbitrary",)),
    )(*o3, *w3, proj, proj, proj, dx1, w_out)


def _pair_sum(stacked, got, core, *, name):
    n, _, r, c = stacked.shape
    tr = _row_tile(r, c, 3)

    def body(core_ref, a_ref, b_ref, o_ref):
        o_ref[...] = (a_ref[0].astype(F32) + b_ref[...].astype(F32)).astype(o_ref.dtype)

    spec = pl.BlockSpec((1, tr, c), lambda s, i, core_ref: (s, i, 0))
    return pl.pallas_call(
        body, name=name,
        grid_spec=pltpu.PrefetchScalarGridSpec(
            num_scalar_prefetch=1, grid=(n, r // tr),
            in_specs=[pl.BlockSpec((1, 1, tr, c), lambda s, i, core_ref: (s, core_ref[0], i, 0)), spec],
            out_specs=spec),
        out_shape=jax.ShapeDtypeStruct((n, r, c), BF16),
        compiler_params=_params(("parallel", "parallel")),
    )(core.astype(jnp.int32).reshape(1), stacked, got)


def _chip_sum(parts, got, chip, *, name):
    _, r, c = parts.shape
    tr = _row_tile(r, c, 6)

    def body(chip_ref, p_ref, q0_ref, q1_ref, q2_ref, o_ref):
        o_ref[...] = ((p_ref[0].astype(F32) + q0_ref[0].astype(F32)) + q1_ref[0].astype(F32)) + q2_ref[0].astype(F32)

    from_chip = lambda j: pl.BlockSpec((1, tr, c), lambda i, chip_ref: (j, i, 0))
    return pl.pallas_call(
        body, name=name,
        grid_spec=pltpu.PrefetchScalarGridSpec(
            num_scalar_prefetch=1, grid=(r // tr,),
            in_specs=[pl.BlockSpec((1, tr, c), lambda i, chip_ref: (chip_ref[0], i, 0))] + [from_chip(j) for j in range(3)],
            out_specs=pl.BlockSpec((tr, c), lambda i, chip_ref: (i, 0))),
        out_shape=jax.ShapeDtypeStruct((r, c), F32),
        compiler_params=_params(("parallel",)),
    )(chip.astype(jnp.int32).reshape(1), parts, got, got, got)


def _adamw_math(w, g, m, v):
    m2 = ADAM_B1 * m + (1.0 - ADAM_B1) * g
    v2 = ADAM_B2 * v + (1.0 - ADAM_B2) * (g * g)
    m_hat = m2 / (1.0 - ADAM_B1 ** ADAM_STEP)
    v_hat = v2 / (1.0 - ADAM_B2 ** ADAM_STEP)
    delta = -ADAM_LR * (m_hat / (jnp.sqrt(v_hat) + ADAM_EPS) + ADAM_WD * w)
    return delta, m2, v2


def _adamw(w, g, m, v, *, name):
    return _ew(_adamw_math, [w, g, m, v], (F32, F32, F32), name=name)


def _adamw_small(w, parts, m, v, *, name):
    n = parts.shape[0]

    def body(w_ref, p_ref, m_ref, v_ref, g_ref, d_ref, m2_ref, v2_ref):
        g = p_ref[0]
        for i in range(1, n):
            g = g + p_ref[i]
        g_ref[...] = g
        d_ref[...], m2_ref[...], v2_ref[...] = _adamw_math(w_ref[...], g, m_ref[...], v_ref[...])

    shp = jax.ShapeDtypeStruct(w.shape, F32)
    return pl.pallas_call(body, name=name, out_shape=[shp] * 4)(w, parts, m, v)


_ANY = pl.BlockSpec(memory_space=pl.ANY)


def _mesh_place():
    x, y, c = lax.axis_index("x"), lax.axis_index("y"), lax.axis_index("c")
    chips = [(1 - x, y), (x, 1 - y), (1 - x, 1 - y)]
    return x, y, c, chips


def _remote(src, dst, sems, i, to):
    send_sems, recv_sems = sems
    return pltpu.make_async_remote_copy(src_ref=src, dst_ref=dst, send_sem=send_sems.at[i], recv_sem=recv_sems.at[i],
                                        device_id=to, device_id_type=MESH_ID)


def _gather_weights(shards, *, name):
    n = len(shards)

    def body(*refs):
        ins, outs = refs[:n], refs[n:2 * n]
        sems = refs[2 * n:2 * n + 2]
        x, y, c, chips = _mesh_place()
        me = 2 * x + y
        sibling = (x, y, 1 - c)
        sent = []
        for w in range(n):
            for j, chip in enumerate(chips):
                cp = _remote(ins[w].at[c], outs[w].at[me, c], sems, 6 * w + j, (chip[0], chip[1], c))
                cp.start()
                sent.append(cp)
        for w in range(n):
            for j, chip in enumerate(chips):
                got = outs[w].at[2 * chip[0] + chip[1], c]
                _remote(got, got, sems, 6 * w + j, sibling).wait_recv()
                cp = _remote(got, got, sems, 6 * w + 3 + j, sibling)
                cp.start()
                sent.append(cp)
        for w in range(n):
            for j, chip in enumerate(chips):
                got = outs[w].at[2 * chip[0] + chip[1], 1 - c]
                _remote(got, got, sems, 6 * w + 3 + j, sibling).wait_recv()
        for cp in sent:
            cp.wait_send()

    outs = pl.pallas_call(
        body, name=name,
        in_specs=[_ANY] * n, out_specs=[_ANY] * n,
        out_shape=[jax.ShapeDtypeStruct((N_CHIPS,) + s.shape, s.dtype) for s in shards],
        scratch_shapes=[pltpu.SemaphoreType.DMA((6 * n,)), pltpu.SemaphoreType.DMA((6 * n,))],
    )(*shards)
    me = 2 * lax.axis_index("x") + lax.axis_index("y")
    return [lax.dynamic_update_index_in_dim(o, s, me, 0) for o, s in zip(outs, shards)]


def _exchange_siblings(grads, *, name):
    n = len(grads)

    def body(*refs):
        ins, got = refs[:n], refs[n:2 * n]
        sems = refs[2 * n:2 * n + 2]
        x, y, c, _ = _mesh_place()
        sibling = (x, y, 1 - c)
        sent = []
        for w in range(n):
            for s in range(N_CHIPS):
                cp = _remote(ins[w].at[s, 1 - c], got[w].at[s], sems, N_CHIPS * w + s, sibling)
                cp.start()
                sent.append(cp)
        for w in range(n):
            for s in range(N_CHIPS):
                _remote(got[w].at[s], got[w].at[s], sems, N_CHIPS * w + s, sibling).wait_recv()
        for cp in sent:
            cp.wait_send()

    n_sem = N_CHIPS * n
    return pl.pallas_call(
        body, name=name,
        in_specs=[_ANY] * n, out_specs=[_ANY] * n,
        out_shape=[jax.ShapeDtypeStruct((N_CHIPS,) + g.shape[2:], g.dtype) for g in grads],
        scratch_shapes=[pltpu.SemaphoreType.DMA((n_sem,)), pltpu.SemaphoreType.DMA((n_sem,))],
    )(*grads)


def _share_halves(halves, small):
    n = len(halves)

    def body(*refs):
        ins, small_ref = refs[:n], refs[n]
        outs, small_out = refs[n + 1:2 * n + 1], refs[2 * n + 1]
        sems = refs[2 * n + 2:2 * n + 4]
        x, y, c, chips = _mesh_place()
        sibling = (x, y, 1 - c)
        me = 4 * x + 2 * y + c
        sent = [_remote(ins[w], outs[w].at[c], sems, w, sibling) for w in range(n)]
        peers = [sibling] + [(ch[0], ch[1], cc) for ch in chips for cc in (c, 1 - c)]
        sent += [_remote(small_ref, small_out.at[me], sems, n + j, peer) for j, peer in enumerate(peers)]
        for cp in sent:
            cp.start()
        for w in range(n):
            _remote(outs[w].at[1 - c], outs[w].at[1 - c], sems, w, sibling).wait_recv()
        for j, peer in enumerate(peers):
            frm = small_out.at[4 * peer[0] + 2 * peer[1] + peer[2]]
            _remote(frm, frm, sems, n + j, peer).wait_recv()
        for cp in sent:
            cp.wait_send()

    n_sem = n + 7
    outs = pl.pallas_call(
        body, name="share_halves",
        in_specs=[_ANY] * (n + 1), out_specs=[_ANY] * (n + 1),
        out_shape=[jax.ShapeDtypeStruct((2,) + h.shape, h.dtype) for h in halves]
        + [jax.ShapeDtypeStruct((8,) + small.shape, small.dtype)],
        scratch_shapes=[pltpu.SemaphoreType.DMA((n_sem,)), pltpu.SemaphoreType.DMA((n_sem,))],
    )(*halves, small)
    c = lax.axis_index("c")
    me = 4 * lax.axis_index("x") + 2 * lax.axis_index("y") + c
    return ([lax.dynamic_update_index_in_dim(o, h, c, 0) for o, h in zip(outs[:n], halves)],
            lax.dynamic_update_index_in_dim(outs[n], small, me, 0))


EARLY = ("w_ff_down", "w_ff_up", "w_out", "w_branch_sb", "w_branch_fox", "w_branch_mem", "w_mem_kv")


def _norm_bwd_tail(dy, x, add, g):
    r = lax.rsqrt(jnp.mean(x * x, axis=-1, keepdims=True) + EPS)
    xh = x * r
    dyg = dy * g
    c = jnp.mean(dyg * xh, axis=-1, keepdims=True)
    return r * (dyg - xh * c) + add, jnp.sum(dy * xh, axis=0, keepdims=True)


def _split(outs, n):
    outs = list(outs) if isinstance(outs, (list, tuple)) else [outs]
    return outs[:n], outs[n:]


def _local_step(x, mem, target, small, W, gather_rest=None, reduce_early=None, reduce_late=None, share_early=None):
    S, D = x.shape
    o_qkv, o_mq, o_f = 3 * D, 3 * D + 2 * 3 * D_SB, 3 * D + 2 * 3 * D_SB + D_MEM
    tq = 512

    g_comms, finish_weights = gather_rest if gather_rest is not None else ([None] * 3, None)
    (proj, h), landed = _split(_mm(x, W["w_in"], name="in_proj", tb=True, tn=768, a_gain=small["g_mix_norm"],
                                   comm=g_comms[0]), 2)
    blk = lambda j: (proj, (o_qkv + j * D_SB) // LANES)
    sb_q, sb_k, sb_v, fx_q, fx_k, fx_v = [blk(j) for j in range(6)]
    m_q = (proj, o_mq // LANES)
    f_logit_t = _mm(W["w_in"][o_f:o_f + ROW_TILE], h, name="forget_logits", tb=True)[:FOX_HEADS]
    b_col = small["b_forget"].reshape(FOX_HEADS, 1)
    lanes = lambda g: jnp.tile(g, (1, LANES // g.shape[1]))
    g_fq, g_fk, g_mq, g_mk = [lanes(small[k]) for k in ("g_fox_q", "g_fox_k", "g_mem_q", "g_mem_k")]

    (o_sb, sb_tot), more = _split(_sbl_fwd(sb_q, sb_k, sb_v, width=D_SB, hd=HD, name="sb_fwd", tq=tq, comm=g_comms[1]), 2)
    landed += more

    fq = _hnorm_fwd(fx_q, g_fq, width=D_FOX, hd=HD, name="fox_q_norm")
    fk = _hnorm_fwd(fx_k, g_fk, width=D_FOX, hd=HD, name="fox_k_norm")
    f_cum = _forget_fwd(f_logit_t, b_col, name="forget_fwd")
    tkf = _tile(S, tq)
    f_bias = (f_cum.reshape(FOX_HEADS, S, 1), f_cum.reshape(FOX_HEADS, S // tkf, 1, tkf))
    fox_comm = None if finish_weights is None else _Joint([g_comms[2], _ChipExchange("forward", landed)])
    (o_fox, fox_lse), more = _split(_sml_fwd((fq, 0), (fk, 0), fx_v, f_bias, width=D_FOX, hd=HD, causal=True,
                                             name="fox_fwd", tq=tq, tk=tq, comm=fox_comm), 2)
    mem_comm = None
    if finish_weights is not None:
        n_last = len(g_comms[2].ins)
        W = {**W, **finish_weights((0, 1), more[n_last:])}
        mem_comm = _ChipExchange("forward", more[:n_last])

    mh = _rmsnorm_fwd(mem, small["g_mem_norm"], BF16, name="mem_norm")
    mkv = _mm(mh, W["w_mem_kv"], name="mem_kv")
    mv = (mkv, D_MEM // LANES)
    mq = _hnorm_fwd(m_q, g_mq, width=D_MEM, hd=MEM_HD, name="mem_q_norm")
    mk = _hnorm_fwd((mkv, 0), g_mk, width=D_MEM, hd=MEM_HD, name="mem_k_norm")
    (o_mem, mem_lse), more = _split(_sml_fwd((mq, 0), (mk, 0), mv, width=D_MEM, hd=MEM_HD, causal=False, name="mem_fwd",
                                             tq=tq, tk=256, comm=mem_comm), 2)
    if finish_weights is not None:
        W = {**W, **finish_weights((2,), more)}

    o3 = [o_sb, o_fox, o_mem]
    w3 = [W["w_branch_sb"], W["w_branch_fox"], W["w_branch_mem"]]
    merged, x1, h2 = _gate_fwd(o3, w3, proj, x, W["w_out"], small["g_mlp_norm"], D, name="gate_fwd")

    def relu2(acc):
        u = jnp.maximum(acc, 0.0)
        return u, u * u

    u, a = _mm(h2, W["w_ff_up"], name="ff_up", out_dtypes=(BF16, BF16), epilogue=relu2)
    def head(acc, res, tgt):
        d = (res + acc - tgt) * (1.0 / D)
        return d, d, jnp.sum(d * d, axis=0, keepdims=True)

    dy, dy16, sq_rows = _mm(a, W["w_ff_down"], name="ff_down", extras=(x1, target), out_dtypes=(F32, BF16),
                            epilogue=head, col_sums=1, tn=512)
    loss = (0.5 * D) * jnp.sum(sq_rows)

    G = {}
    du = _mm(dy16, W["w_ff_down"], name="d_ff_act", tb=True, out_dtypes=(BF16,), extras=(u,),
             epilogue=lambda acc, uu: (acc * (2.0 * uu.astype(F32)),))
    G["w_ff_down"] = _mm(a, dy16, name="d_w_ff_down", ta=True, out_dtypes=(BF16,))
    G["w_ff_up"] = _mm(h2, du, name="d_w_ff_up", ta=True, out_dtypes=(BF16,))
    dx1, dg_rows = _mm(du, W["w_ff_up"], name="d_mlp_in", tb=True, extras=(x1, dy, small["g_mlp_norm"]),
                       epilogue=_norm_bwd_tail, col_sums=1, tm=512, tn=D)
    dg_mlp = jnp.sum(dg_rows, axis=0, keepdims=True)
    G["w_out"] = _mm(merged, dx1, name="d_w_out", ta=True, out_dtypes=(BF16,))
    dgate, dw0, dw1, dw2, do_sb, do_fox, do_mem = _gate_bwd(o3, w3, proj, dx1, W["w_out"], D, name="gate_bwd")
    for nm, dw in zip(("w_branch_sb", "w_branch_fox", "w_branch_mem"), (dw0, dw1, dw2)):
        G[nm] = dw.astype(BF16)

    sib_comm, after_siblings = (reduce_early({k: G.pop(k) for k in EARLY if k != "w_mem_kv"})
                                if reduce_early is not None else (None, None))
    (dmq_n, dmk_n, dmv), landed_sib = _split(
        _sml_bwd((mq, 0), (mk, 0), mv, (o_mem, 0), (mem_lse, 0), (do_mem, 0), width=D_MEM, hd=MEM_HD, causal=False,
                 name="mem_bwd", tq=tq, tk=256, comm=sib_comm), 3)
    dm_q, dg_mem_q = _hnorm_bwd(m_q, g_mq, dmq_n, width=D_MEM, hd=MEM_HD, name="d_mem_q_norm")
    dmk_raw, dg_mem_k = _hnorm_bwd((mkv, 0), g_mk, dmk_n, width=D_MEM, hd=MEM_HD, name="d_mem_k_norm")
    dmkv = jnp.concatenate([dmk_raw, dmv.astype(BF16)], axis=1)
    G["w_mem_kv"] = _mm(mh, dmkv, name="d_w_mem_kv", ta=True, out_dtypes=(BF16,))
    dmh = _mm(dmkv, W["w_mem_kv"], name="d_mem_h", tb=True)
    _, dg_mem = _rmsnorm_bwd(mem, small["g_mem_norm"], dmh, name="d_mem_norm")

    r_comms, r_finish = (after_siblings(landed_sib, {"w_mem_kv": G.pop("w_mem_kv")})
                         if after_siblings is not None else ([None] * 2, None))
    dsb, landed_sb = _split(_sbl_bwd(sb_q, sb_k, sb_v, (do_sb, 0), (sb_tot, 0), width=D_SB, hd=HD, name="sb_bwd", tq=tq,
                                     comm=r_comms[0]), 3)
    (dfq, dfk, dfv, df_row, df_col), landed_fox = _split(
        _sml_bwd((fq, 0), (fk, 0), fx_v, (o_fox, 0), (fox_lse, 0), (do_fox, 0), f_bias, width=D_FOX, hd=HD, causal=True,
                 name="fox_bwd", tq=tq, tk=tq, comm=r_comms[1]), 5)
    early = r_finish(landed_sb, landed_fox) if r_finish is not None else {}
    dfx_q, dg_fox_q = _hnorm_bwd(fx_q, g_fq, dfq, width=D_FOX, hd=HD, name="d_fox_q_norm")
    dfx_k, dg_fox_k = _hnorm_bwd(fx_k, g_fk, dfk, width=D_FOX, hd=HD, name="d_fox_k_norm")
    d_fcum = df_row.reshape(FOX_HEADS, S) - df_col.reshape(FOX_HEADS, S)
    d_flogit_t, db_forget = _forget_bwd(f_logit_t, b_col, d_fcum, name="forget_bwd")
    dg_fox_q, dg_fox_k = dg_fox_q[:, :HD], dg_fox_k[:, :HD]

    rest_cols = jnp.concatenate([t.astype(BF16) for t in (*dsb, dfx_q, dfx_k, dfv, dm_q)]
                                + [d_flogit_t.T.astype(BF16), jnp.zeros((S, F_PAD - FOX_HEADS), BF16)], axis=1)
    dproj = lax.dynamic_update_slice(dgate, rest_cols, (0, 3 * D))
    share_comm, shared = share_early(early) if share_early is not None else (None, None)
    (g_w_in,), landed = _split(_mm(dproj, h, name="d_w_in", ta=True, out_dtypes=(BF16,), tm=768, comm=share_comm), 1)
    if shared is not None:
        early = shared(landed)
    comm, finish = reduce_late({"w_in": g_w_in}) if reduce_late is not None else (None, None)
    (grad_x, dg_rows), landed = _split(
        _mm(dproj, W["w_in"], name="d_mix_in", tk=2304, tm=512, tn=D, extras=(x, dx1, small["g_mix_norm"]),
            epilogue=_norm_bwd_tail, col_sums=1, comm=comm), 2)
    dg_mix = jnp.sum(dg_rows, axis=0, keepdims=True)
    if finish is None:
        G["w_in"] = g_w_in
    else:
        early.update(finish(landed))

    small_grads = dict(g_mix_norm=dg_mix, g_mem_norm=dg_mem, b_forget=db_forget.reshape(1, FOX_HEADS),
                       g_fox_q=dg_fox_q, g_fox_k=dg_fox_k, g_mem_q=dg_mem_q, g_mem_k=dg_mem_k, g_mlp_norm=dg_mlp)
    return loss, grad_x, G, small_grads, early


BIG = ("w_in", "w_mem_kv", "w_branch_sb", "w_branch_fox", "w_branch_mem", "w_out", "w_ff_up", "w_ff_down")
COLUMN_SHARDED = ("w_in", "w_branch_sb", "w_branch_fox", "w_branch_mem", "w_ff_up")
SMALL = ("g_mix_norm", "g_mem_norm", "b_forget", "g_fox_q", "g_fox_k", "g_mem_q", "g_mem_k", "g_mlp_norm")
ORDER = ("g_mix_norm", "g_mem_norm", "w_in", "b_forget", "g_fox_q", "g_fox_k", "g_mem_q", "g_mem_k", "w_mem_kv",
         "w_branch_sb", "w_branch_fox", "w_branch_mem", "w_out", "g_mlp_norm", "w_ff_up", "w_ff_down")


def _unshard(name, gathered):
    n, _, rh, c = gathered.shape
    t = gathered.reshape(n, 2 * rh, c)
    if name in COLUMN_SHARDED:
        return t.transpose(1, 0, 2).reshape(2 * rh, n * c)
    return t.reshape(n * 2 * rh, c)


def _reshard(name, full):
    if name in COLUMN_SHARDED:
        r, c = full.shape
        t = full.reshape(r, N_CHIPS, c // N_CHIPS).transpose(1, 0, 2)
    else:
        r, c = full.shape[0] // N_CHIPS, full.shape[1]
        t = full.reshape(N_CHIPS, r, c)
    return t.reshape(N_CHIPS, 2, t.shape[1] // 2, t.shape[2])


ROW_TILE = 16
IN_BUF_ALIGN = 256


def _in_segments(D):
    n_qkv = 6 * D_SB
    o_mq, o_gate = n_qkv + FOX_HEADS, n_qkv + FOX_HEADS + D_MEM
    return [(0, n_qkv, 3 * D), (n_qkv, o_mq, 3 * D + n_qkv + D_MEM), (o_mq, o_gate, 3 * D + n_qkv), (o_gate, o_gate + 3 * D, 0)]


class _InLayout:
    def __init__(self, D, shard, n):
        self.D, self.shard, self.n = D, shard, n
        down = lambda v: v // ROW_TILE * ROW_TILE
        up = lambda v: -(-v // ROW_TILE) * ROW_TILE
        self.pieces = []
        ends = []
        for s in range(n):
            cursor, mine = 0, []
            for a, b, p in _in_segments(D):
                x0, x1 = max(a, s * shard), min(b, (s + 1) * shard)
                if x0 < x1:
                    p0 = p + x0 - a
                    rows = up(p0 + x1 - x0) - down(p0)
                    mine.append((x0 - s * shard, x1 - x0, p0, cursor, rows))
                    cursor += rows
            self.pieces.append(mine)
            ends.append(cursor)
        self.rows = -(-max(ends) // IN_BUF_ALIGN) * IN_BUF_ALIGN
        self.padded_rows = 3 * D + 6 * D_SB + D_MEM + F_PAD

    def _per_shard(self, fn, chip, operand):
        return lax.switch(chip, [functools.partial(fn, s) for s in range(self.n)], operand)

    def pack(self, chip, rows):
        def one(s, t):
            out, at = [], 0
            for x0, n_rows, p0, start, region in self.pieces[s]:
                lead = p0 % ROW_TILE
                out += [jnp.zeros((start + lead - at, t.shape[1]), t.dtype), t[x0:x0 + n_rows]]
                at = start + lead + n_rows
            return jnp.concatenate(out + [jnp.zeros((self.rows - at, t.shape[1]), t.dtype)], axis=0)
        return self._per_shard(one, chip, rows)

    def unpack(self, chip, buf, pad_to):
        def one(s, t):
            out = [t[start + p0 % ROW_TILE:start + p0 % ROW_TILE + n_rows] for _, n_rows, p0, start, _ in self.pieces[s]]
            return jnp.concatenate(out + [jnp.zeros((pad_to - self.shard, t.shape[1]), t.dtype)], axis=0)
        return self._per_shard(one, chip, buf)

    def to_padded(self, bufs):
        runs = sorted((p0, s, start, region) for s in range(self.n) for _, _, p0, start, region in self.pieces[s])
        chunks, end = [], 0
        for p0, s, start, region in runs:
            d0 = p0 // ROW_TILE * ROW_TILE
            src = bufs[s, start:start + region]
            if d0 < end:
                assert end - d0 == ROW_TILE
                last = chunks.pop()
                chunks += [last[:-ROW_TILE], last[-ROW_TILE:] + src[:ROW_TILE], src[ROW_TILE:]]
            else:
                if d0 > end:
                    chunks.append(jnp.zeros((d0 - end, bufs.shape[2]), bufs.dtype))
                chunks.append(src)
            end = d0 + region
        chunks.append(jnp.zeros((self.padded_rows - end, bufs.shape[2]), bufs.dtype))
        return jnp.concatenate(chunks, axis=0)

    def from_padded(self, gp):
        bufs = []
        for s in range(self.n):
            out, at = [], 0
            for _, n_rows, p0, start, region in self.pieces[s]:
                d0 = p0 // ROW_TILE * ROW_TILE
                row = d0 + lax.broadcasted_iota(jnp.int32, (region, 1), 0)
                out.append(jnp.where((row >= p0) & (row < p0 + n_rows), gp[d0:d0 + region], jnp.zeros((), gp.dtype)))
                at = start + region
            bufs.append(jnp.concatenate(out + [jnp.zeros((self.rows - at, gp.shape[1]), gp.dtype)], axis=0))
        return jnp.stack(bufs)


SMALL_ROWS = 16


def _pack_small(vals, scalar=None):
    width = max(vals[k].shape[1] for k in SMALL)
    rows = [jnp.pad(vals[k].astype(F32), ((0, 0), (0, width - vals[k].shape[1]))) for k in SMALL]
    extra = jnp.zeros((SMALL_ROWS - len(SMALL), width), F32)
    if scalar is not None:
        extra = extra.at[0, 0].set(scalar)
    return jnp.concatenate(rows + [extra], axis=0)


def _unpack_small(packed, like):
    return {k: packed[i:i + 1, :like[k].shape[1]] for i, k in enumerate(SMALL)}


def kernel(x, mem, g_mix_norm, g_mem_norm, w_in, b_forget, g_fox_q, g_fox_k, g_mem_q, g_mem_k, w_mem_kv, w_branch_sb, w_branch_fox, w_branch_mem, w_out, g_mlp_norm, w_ff_up, w_ff_down, loss_target, m_g_mix_norm, m_g_mem_norm, m_w_in, m_b_forget, m_g_fox_q, m_g_fox_k, m_g_mem_q, m_g_mem_k, m_w_mem_kv, m_w_branch_sb, m_w_branch_fox, m_w_branch_mem, m_w_out, m_g_mlp_norm, m_w_ff_up, m_w_ff_down, v_g_mix_norm, v_g_mem_norm, v_w_in, v_b_forget, v_g_fox_q, v_g_fox_k, v_g_mem_q, v_g_mem_k, v_w_mem_kv, v_w_branch_sb, v_w_branch_fox, v_w_branch_mem, v_w_out, v_g_mlp_norm, v_w_ff_up, v_w_ff_down):
    given = dict(locals())
    D = x.shape[-1]
    weights = {k: given[k] for k in ORDER}
    moms = {k: given["m_" + k] for k in ORDER}
    vars_ = {k: given["v_" + k] for k in ORDER}

    me_chip = 2 * lax.axis_index("x") + lax.axis_index("y")

    n_in = w_in.shape[2]
    lay = _InLayout(D, n_in, N_CHIPS)
    transposed = lambda t: jnp.transpose(t[0])
    shards = {}
    for k in BIG:
        w = weights[k][0].astype(BF16)
        if k == "w_in":
            w = lay.pack(me_chip, jnp.transpose(w))
        shards[k] = w.reshape(2, w.shape[0] // 2, w.shape[1])
    gathered_in = _gather_weights([shards["w_in"]], name="gather_w_in")[0]
    W = {"w_in": lay.to_padded(gathered_in.reshape(N_CHIPS, lay.rows, D))}
    carried = (("w_branch_sb", "w_branch_fox", "w_branch_mem", "w_out"), ("w_ff_up", "w_mem_kv"), ("w_ff_down",))
    rest = [k for grp in carried for k in grp]
    assert sorted(rest + ["w_in"]) == sorted(BIG)

    def finish_weights(groups, arrived):
        names = [k for gi in groups for k in carried[gi]]
        full = [lax.dynamic_update_index_in_dim(o, shards[k], me_chip, 0) for k, o in zip(names, arrived)]
        return {k: _unshard(k, g) for k, g in zip(names, full)}

    me_core = lax.axis_index("c")

    def sum_chips(names, parts, got):
        return {k: _chip_sum(p, q, me_chip, name="sum_chips_" + k) for k, p, q in zip(names, parts, got)}

    def pair_sums(grads, tag):
        names = list(grads)
        stacked = {k: _reshard(k, grads[k]) for k in names if k != "w_in"}
        if "w_in" in grads:
            stacked["w_in"] = lay.from_padded(grads["w_in"]).reshape(N_CHIPS, 2, lay.rows // 2, D)
        got = _exchange_siblings([stacked[k] for k in names], name="exchange_siblings_" + tag)
        return {k: _pair_sum(stacked[k], q, me_core, name="sum_pair_" + k) for k, q in zip(names, got)}

    def reduce_early(grads):
        names = list(grads)
        stacked = [_reshard(k, grads[k]) for k in names]

        def after_siblings(landed, more):
            parts = {k: _pair_sum(t, q, me_core, name="sum_pair_" + k) for k, t, q in zip(names, stacked, landed)}
            parts.update(pair_sums(more, "early"))
            groups = [[k for k in parts if k in ("w_ff_down", "w_ff_up")], [k for k in parts if k not in ("w_ff_down", "w_ff_up")]]
            comms = [_ChipExchange("scatter", [parts[k] for k in grp]) for grp in groups]

            def finish(*got):
                out = {}
                for grp, q in zip(groups, got):
                    out.update(sum_chips(grp, [parts[k] for k in grp], q))
                return out
            return comms, finish
        return _ChipExchange("siblings", stacked), after_siblings

    def reduce_late(grads):
        parts = pair_sums(grads, "late")
        names = list(parts)
        return _ChipExchange("scatter", [parts[k] for k in names]), functools.partial(sum_chips, names, [parts[k] for k in names])

    def share_early(halves):
        names = list(halves)

        def shared(landed):
            return {k: lax.dynamic_update_index_in_dim(o, halves[k], me_core, 0) for k, o in zip(names, landed)}
        return _ChipExchange("halves", [halves[k] for k in names]), shared

    small = {k: weights[k] for k in SMALL}
    loss_part, grad_x, G, small_grads, reduced = _local_step(
        x[0], mem[0], loss_target[0], small, W,
        gather_rest=([_ChipExchange("gather", [shards[k] for k in grp]) for grp in carried], finish_weights),
        reduce_early=reduce_early, reduce_late=reduce_late, share_early=share_early)
    assert not G, list(G)
    (reduced["w_in"],), small_parts = _share_halves([reduced["w_in"]], _pack_small(small_grads, loss_part))

    grads, deltas, new_m, new_v = {}, {}, {}, {}
    for k in BIG:
        g = reduced[k]
        shp = weights[k].shape
        if k == "w_in":
            g2 = lay.unpack(me_chip, g.reshape(lay.rows, D), lay.rows)
            padded = lambda t: jnp.pad(transposed(t), ((0, lay.rows - n_in), (0, 0)))
            outs = _adamw(padded(weights[k]), g2, padded(moms[k]), padded(vars_[k]), name="adamw_" + k)
            g2, d, m2, v2 = [jnp.transpose(t[:n_in]) for t in (g2, *outs)]
        else:
            g2 = g.reshape(shp[1], shp[2])
            d, m2, v2 = _adamw(weights[k][0], g2, moms[k][0], vars_[k][0], name="adamw_" + k)
        grads[k], deltas[k], new_m[k], new_v[k] = g2.reshape(shp), d.reshape(shp), m2.reshape(shp), v2.reshape(shp)
    sg, sd, sm, sv = _adamw_small(_pack_small(small), small_parts, _pack_small({k: moms[k] for k in SMALL}),
                                  _pack_small({k: vars_[k] for k in SMALL}), name="adamw_small")
    for dst, packed in ((grads, sg), (deltas, sd), (new_m, sm), (new_v, sv)):
        dst.update(_unpack_small(packed, small))

    loss = sg[len(SMALL), 0]
    return (loss, grad_x[None], *[grads[k] for k in ORDER], *[deltas[k] for k in ORDER],
            *[new_m[k] for k in ORDER], *[new_v[k] for k in ORDER])
```

```python
import functools

import jax
import jax.numpy as jnp
from jax import lax
from jax.experimental import pallas as pl
from jax.experimental.pallas import tpu as pltpu

F32 = jnp.float32
BF16 = jnp.bfloat16
MESH_ID = pl.DeviceIdType.MESH

HD = 64
SB_HEADS = 8
FOX_HEADS = 8
MEM_HEADS = 4
MEM_HD = 128
D_SB = SB_HEADS * HD
D_FOX = FOX_HEADS * HD
D_MEM = MEM_HEADS * MEM_HD
EPS = 1e-6
NEG_INF = -1e30

ADAM_LR = 0.001
ADAM_B1 = 0.9
ADAM_B2 = 0.999
ADAM_EPS = 1e-08
ADAM_WD = 0.01
ADAM_STEP = 10

N_CHIPS = 4
VMEM_LIMIT = 56 * 1024 * 1024

F_PAD = 256


def _tile(n, target, align=128):
    if n <= target:
        return n
    best = None
    t = align
    while t <= target:
        if n % t == 0:
            best = t
        t += align
    assert best is not None, (n, target, align)
    return best


def _params(sem):
    return pltpu.CompilerParams(dimension_semantics=sem, vmem_limit_bytes=VMEM_LIMIT)


def _mm(a, b, *, name, ta=False, tb=False, out_dtypes=(F32,), epilogue=None, extras=(),
        tm=1024, tn=1024, tk=2048, comm=None, col_sums=0, a_gain=None):
    if ta:
        K, M = a.shape
    else:
        M, K = a.shape
    if tb:
        N, K2 = b.shape
    else:
        K2, N = b.shape
    assert K == K2, (a.shape, b.shape, ta, tb)
    tm, tn, tk = _tile(M, tm), _tile(N, tn), _tile(K, tk)
    nk = K // tk
    normed = a_gain is not None
    assert not normed or (nk == 1 and not ta)
    n_extra, n_out = len(extras) + normed, len(out_dtypes) + col_sums + normed
    if epilogue is None:
        epilogue = lambda acc: (acc,)
    dims = (((0 if ta else 1,), (1 if tb else 0,)), ((), ()))

    gm, gn = M // tm, N // tn

    def body(*refs):
        i, j, k = pl.program_id(0), pl.program_id(1), pl.program_id(2)
        got = _carry(comm, 2 + n_extra, n_out, (i == 0) & (j == 0) & (k == 0),
                     (i == gm - 1) & (j == gn - 1) & (k == nk - 1), refs)
        (a_ref, b_ref, *extra_refs), out_refs = got[0], list(got[1])
        if normed:
            gain_ref, normed_ref = extra_refs.pop(), out_refs.pop()

            @pl.when(j == 0)
            def _():
                xa = a_ref[...].astype(F32)
                normed_ref[...] = (xa * lax.rsqrt(jnp.mean(xa * xa, axis=-1, keepdims=True) + EPS) * gain_ref[...]).astype(BF16)

            av = normed_ref[...]
        else:
            av = a_ref[...].astype(BF16)
        part = lax.dot_general(av, b_ref[...].astype(BF16), dims, preferred_element_type=F32)

        def finish(acc):
            outs = epilogue(acc, *[r[...] for r in extra_refs])
            for o_ref, o in zip(out_refs[:len(out_dtypes)], outs):
                o_ref[...] = o.astype(o_ref.dtype)
            for o_ref, o in zip(out_refs[len(out_dtypes):], outs[len(out_dtypes):]):
                first = lax.broadcasted_iota(jnp.int32, o_ref.shape, 0) == 0
                o_ref[...] = jnp.where(first, jnp.broadcast_to(o, o_ref.shape), 0.0)

        if nk == 1:
            finish(part)
        else:
            acc_ref = refs[-1]

            @pl.when(k == 0)
            def _():
                acc_ref[...] = part

            @pl.when((k > 0) & (k < nk - 1))
            def _():
                acc_ref[...] += part

            @pl.when(k == nk - 1)
            def _():
                finish(acc_ref[...] + part)

        got[2]()

    a_spec = pl.BlockSpec((tk, tm), lambda i, j, k: (k, i)) if ta else pl.BlockSpec((tm, tk), lambda i, j, k: (i, k))
    b_spec = pl.BlockSpec((tn, tk), lambda i, j, k: (j, k)) if tb else pl.BlockSpec((tk, tn), lambda i, j, k: (k, j))
    mn_spec = pl.BlockSpec((tm, tn), lambda i, j, k: (i, j))
    row_spec = pl.BlockSpec((1, tn), lambda i, j, k: (0, j))
    sum_spec = pl.BlockSpec((8, tn), lambda i, j, k: (i, j))
    c_ins, c_in_specs, c_out_specs, c_out_shape, c_scratch = _comm_args(comm)
    sem = ("parallel", "arbitrary" if normed else "parallel", "arbitrary") if comm is None else ("arbitrary",) * 3
    outs = pl.pallas_call(
        body, name=name,
        grid=(gm, gn, nk),
        in_specs=[a_spec, b_spec] + [row_spec if e.shape[0] == 1 else mn_spec for e in extras]
        + ([pl.BlockSpec((1, K), lambda i, j, k: (0, 0))] if normed else []) + c_in_specs,
        out_specs=[mn_spec] * len(out_dtypes) + [sum_spec] * col_sums
        + ([pl.BlockSpec((tm, K), lambda i, j, k: (i, 0))] if normed else []) + c_out_specs,
        out_shape=[jax.ShapeDtypeStruct((M, N), dt) for dt in out_dtypes]
        + [jax.ShapeDtypeStruct((8 * gm, N), F32)] * col_sums
        + ([jax.ShapeDtypeStruct((M, K), BF16)] if normed else []) + c_out_shape,
        scratch_shapes=c_scratch + ([pltpu.VMEM((tm, tn), F32)] if nk > 1 else []),
        compiler_params=_params(sem),
    )(a, b, *extras, *([a_gain] if normed else []), *c_ins)
    return outs if len(outs) > 1 else outs[0]


def _row_tile(rows, cols, n_arrays):
    budget = 28 * 1024 * 1024
    cols_padded = -(-cols // 128) * 128
    target = max(16, budget // (cols_padded * 4 * n_arrays * 2))
    return _tile(rows, target, align=16)


def _ew(fn, ins, out_dtypes, *, name):
    R, C = ins[0].shape
    n_in, n_out = len(ins), len(out_dtypes)
    tr = _row_tile(R, C, n_in + n_out)

    def body(*refs):
        outs = fn(*[r[...] for r in refs[:n_in]])
        for o_ref, o in zip(refs[n_in:], outs):
            o_ref[...] = o.astype(o_ref.dtype)

    spec = pl.BlockSpec((tr, C), lambda i: (i, 0))
    outs = pl.pallas_call(
        body, name=name, grid=(R // tr,),
        in_specs=[spec] * n_in, out_specs=[spec] * n_out,
        out_shape=[jax.ShapeDtypeStruct((R, C), dt) for dt in out_dtypes],
        compiler_params=_params(("parallel",)),
    )(*ins)
    return outs if n_out > 1 else outs[0]


def _rmsnorm_fwd(x, g, out_dtype, *, name):
    R, d = x.shape
    tr = _row_tile(R, d, 3)

    def body(x_ref, g_ref, o_ref):
        xv = x_ref[...].astype(F32)
        r = lax.rsqrt(jnp.mean(xv * xv, axis=-1, keepdims=True) + EPS)
        o_ref[...] = (xv * r * g_ref[...]).astype(o_ref.dtype)

    return pl.pallas_call(
        body, name=name, grid=(R // tr,),
        in_specs=[pl.BlockSpec((tr, d), lambda i: (i, 0)), pl.BlockSpec((1, d), lambda i: (0, 0))],
        out_specs=pl.BlockSpec((tr, d), lambda i: (i, 0)),
        out_shape=jax.ShapeDtypeStruct((R, d), out_dtype),
        compiler_params=_params(("parallel",)),
    )(x, g)


def _rmsnorm_bwd(x, g, dy, add=None, *, name):
    R, d = x.shape
    has_add = add is not None
    tr = _row_tile(R, d, 5)

    def body(*refs):
        x_ref, g_ref, dy_ref = refs[:3]
        add_ref = refs[3] if has_add else None
        dx_ref, dg_ref = refs[-2:]
        xv = x_ref[...].astype(F32)
        dyv = dy_ref[...].astype(F32)
        r = lax.rsqrt(jnp.mean(xv * xv, axis=-1, keepdims=True) + EPS)
        xh = xv * r
        dyg = dyv * g_ref[...]
        c = jnp.mean(dyg * xh, axis=-1, keepdims=True)
        dx = r * (dyg - xh * c)
        if has_add:
            dx = dx + add_ref[...]
        dx_ref[...] = dx

        @pl.when(pl.program_id(0) == 0)
        def _():
            dg_ref[...] = jnp.zeros_like(dg_ref)

        dg_ref[...] += jnp.sum(dyv * xh, axis=0, keepdims=True)

    row = pl.BlockSpec((tr, d), lambda i: (i, 0))
    vec = pl.BlockSpec((1, d), lambda i: (0, 0))
    ins = [x, g, dy] + ([add] if has_add else [])
    return pl.pallas_call(
        body, name=name, grid=(R // tr,),
        in_specs=[row, vec, row] + ([row] if has_add else []),
        out_specs=[row, vec],
        out_shape=[jax.ShapeDtypeStruct((R, d), F32), jax.ShapeDtypeStruct((1, d), F32)],
        compiler_params=_params(("arbitrary",)),
    )(*ins)


_NT = (((1,), (1,)), ((), ()))
_TN = (((0,), (0,)), ((), ()))


def _dot(a, b, dims=(((1,), (0,)), ((), ()))):
    return lax.dot_general(a, b, dims, preferred_element_type=F32)


def _log_sigmoid_pair(z):
    sp = jnp.log(1.0 + jnp.exp(-jnp.abs(z)))
    return jnp.minimum(z, 0.0) - sp, jnp.minimum(-z, 0.0) - sp


LANES = 128
_LOW = -3e38


def _lane_masks(hd, rows):
    if hd == LANES:
        return [None]
    lane = lax.broadcasted_iota(jnp.int32, (rows, LANES), 1)
    return [(lane >= hh * hd) & (lane < (hh + 1) * hd) for hh in range(LANES // hd)]


def _keep(t, m):
    return t if m is None else jnp.where(m, t, 0.0)


def _merge(parts, masks):
    out = parts[-1]
    for p, m in zip(parts[-2::-1], masks[-2::-1]):
        out = jnp.where(m, p, out)
    return out


def _row_value(t, m):
    return jnp.max(t if m is None else jnp.where(m, t, _LOW), axis=1, keepdims=True)


def _cols(tq, off):
    return pl.BlockSpec((tq, LANES), lambda g, i: (i, off + g))


def _cols_all(rows, off):
    return pl.BlockSpec((rows, LANES), lambda g, i: (0, off + g))


SCAN_BLOCK = 256


def _tri(kind, cols):
    n = min(SCAN_BLOCK, cols)
    r = lax.broadcasted_iota(jnp.int32, (n, n), 0)
    c = lax.broadcasted_iota(jnp.int32, (n, n), 1)
    return ((r > c) if kind == "after" else (r < c)).astype(BF16)


def _scan_cols(x, tri, reverse):
    cols = x.shape[1]
    cb = min(SCAN_BLOCK, cols)
    assert cols % cb == 0 and tri.shape == (cb, cb)
    nb = cols // cb
    blocks = [x[:, b * cb:(b + 1) * cb] for b in range(nb)]
    outs, carry = [None] * nb, None
    for b in (reversed(range(nb)) if reverse else range(nb)):
        y = _dot(blocks[b].astype(BF16), tri)
        outs[b] = y if carry is None else y + carry
        s = jnp.sum(blocks[b], axis=1, keepdims=True)
        carry = s if carry is None else carry + s
    return (outs[0] if nb == 1 else jnp.concatenate(outs, axis=1)), carry


def _softplus_parts(z):
    pos = jnp.maximum(z, 0.0) + jnp.log(1.0 + jnp.exp(-jnp.abs(z)))
    return pos, z - pos


class _ChipExchange:
    def __init__(self, kind, ins):
        assert kind in ("gather", "scatter", "siblings", "forward", "halves")
        self.kind, self.ins = kind, list(ins)
        lead = {"gather": lambda s: (N_CHIPS,) + s, "scatter": lambda s: (3,) + s[1:], "siblings": lambda s: s[:1] + s[2:],
                "forward": lambda s: s, "halves": lambda s: (2,) + s}[kind]
        self.out_shape = [jax.ShapeDtypeStruct(lead(a.shape), a.dtype) for a in ins]
        self.aliases = [(w, w) for w in range(len(ins))] if kind == "forward" else []
        n = {"siblings": N_CHIPS, "halves": 1}.get(kind, 3) * len(ins)
        self.scratch = [pltpu.SemaphoreType.DMA((n,)), pltpu.SemaphoreType.DMA((n,))]

    def _copies(self, in_refs, out_refs, sems, landing):
        x, y, c, chips = _mesh_place()
        me = 2 * x + y
        out = []
        for w in range(len(self.ins)):
            if self.kind == "siblings":
                for s in range(N_CHIPS):
                    land = out_refs[w].at[s]
                    src = land if landing else in_refs[w].at[s, 1 - c]
                    out.append(_remote(src, land, sems, N_CHIPS * w + s, (x, y, 1 - c)))
                continue
            if self.kind == "halves":
                land = out_refs[w].at[1 - c]
                src, dst = (land, land) if landing else (in_refs[w], out_refs[w].at[c])
                out.append(_remote(src, dst, sems, w, (x, y, 1 - c)))
                continue
            for j, chip in enumerate(chips):
                peer = 2 * chip[0] + chip[1]
                to = (chip[0], chip[1], c)
                if self.kind == "gather":
                    src, dst, land = in_refs[w].at[c], out_refs[w].at[me, c], out_refs[w].at[peer, c]
                elif self.kind == "scatter":
                    src, dst, land = in_refs[w].at[peer], out_refs[w].at[j], out_refs[w].at[j]
                else:
                    src, dst, land, to = in_refs[w].at[peer, c], out_refs[w].at[peer, c], out_refs[w].at[peer, 1 - c], (x, y, 1 - c)
                if landing:
                    src, dst = land, land
                out.append(_remote(src, dst, sems, 3 * w + j, to))
        return out

    def start(self, in_refs, out_refs, sems):
        for cp in self._copies(in_refs, out_refs, sems, False):
            cp.start()

    def finish(self, in_refs, out_refs, sems):
        for cp in self._copies(in_refs, out_refs, sems, True):
            cp.wait_recv()
        for cp in self._copies(in_refs, out_refs, sems, False):
            cp.wait_send()


class _Joint:
    def __init__(self, parts):
        self.parts = list(parts)
        self.ins = [a for p in self.parts for a in p.ins]
        self.out_shape = [s for p in self.parts for s in p.out_shape]
        self.scratch = [s for p in self.parts for s in p.scratch]
        self.aliases, n_in, n_out = [], 0, 0
        for p in self.parts:
            self.aliases += [(n_in + i, n_out + o) for i, o in p.aliases]
            n_in, n_out = n_in + len(p.ins), n_out + len(p.out_shape)

    def _each(self, step, in_refs, out_refs, sems):
        i = o = s = 0
        for p in self.parts:
            a, b, c = len(p.ins), len(p.out_shape), len(p.scratch)
            getattr(p, step)(in_refs[i:i + a], out_refs[o:o + b], sems[s:s + c])
            i, o, s = i + a, o + b, s + c

    def start(self, in_refs, out_refs, sems):
        self._each("start", in_refs, out_refs, sems)

    def finish(self, in_refs, out_refs, sems):
        self._each("finish", in_refs, out_refs, sems)


def _carry(comm, n_in, n_out, first, last, refs):
    if comm is None:
        return refs[:n_in], refs[n_in:n_in + n_out], (lambda: None)
    a, b = len(comm.ins), len(comm.out_shape)
    ins, c_in = refs[:n_in], refs[n_in:n_in + a]
    outs, c_out = refs[n_in + a:n_in + a + n_out], refs[n_in + a + n_out:n_in + a + n_out + b]
    sems = refs[n_in + a + n_out + b:n_in + a + n_out + b + len(comm.scratch)]
    pl.when(first)(lambda: comm.start(c_in, c_out, sems))
    return ins, outs, (lambda: pl.when(last)(lambda: comm.finish(c_in, c_out, sems)))


def _sbl_fwd(q, k, v, *, width, hd, name, tq=256, comm=None):
    (qa, qo), (ka, ko), (va, vo) = q, k, v
    S = qa.shape[0]
    tq = _tile(S, tq)
    tk = tq
    scale = hd ** -0.5
    n_g, n_q = width // LANES, S // tq

    def body(*refs):
        qi = pl.program_id(1)
        gi = pl.program_id(0)
        got = _carry(comm, 3, 2, (gi == 0) & (qi == 0), (gi == n_g - 1) & (qi == n_q - 1), refs)
        (q_ref, k_ref, v_ref), (o_ref, tot_ref) = got[0], got[1]
        masks = _lane_masks(hd, tq)
        qs = q_ref[...].astype(F32) * scale
        qm = [_keep(qs, m).astype(BF16) for m in masks]
        strict = lax.broadcasted_iota(jnp.int32, (tq, tk), 1) < lax.broadcasted_iota(jnp.int32, (tq, tk), 0)
        later = _tri("after", tk)

        def tile(kb, carry, diag):
            ks = pl.multiple_of(kb * tk, tk)
            kv = k_ref[pl.ds(ks, tk), :].astype(BF16)
            vv = v_ref[pl.ds(ks, tk), :].astype(BF16)
            out = []
            for hh in range(len(masks)):
                acc, c_pos = carry[2 * hh], carry[2 * hh + 1]
                pos, ls = _softplus_parts(_dot(qm[hh], kv, _NT))
                if diag:
                    pos = jnp.where(strict, pos, 0.0)
                pos_after, pos_all = _scan_cols(pos, later, True)
                w = jnp.exp(ls - (pos_after + c_pos))
                if diag:
                    w = jnp.where(strict, w, 0.0)
                out += [acc + _dot(w.astype(BF16), vv), c_pos + pos_all]
            return tuple(out)

        init = (jnp.zeros((tq, LANES), F32), jnp.zeros((tq, 1), F32)) * len(masks)
        carry = tile(qi, init, True)
        carry = lax.fori_loop(0, qi, lambda i, c: tile(qi - 1 - i, c, False), carry)
        o_ref[...] = _merge(carry[0::2], masks).astype(o_ref.dtype)
        tot_ref[...] = _merge([jnp.broadcast_to(-c, (tq, LANES)) for c in carry[1::2]], masks)
        got[2]()

    c_ins, c_in_specs, c_out_specs, c_out_shape, c_scratch = _comm_args(comm)
    return pl.pallas_call(
        body, name=name, grid=(n_g, n_q),
        in_specs=[_cols(tq, qo), _cols_all(S, ko), _cols_all(S, vo)] + c_in_specs,
        out_specs=[_cols(tq, 0), _cols(tq, 0)] + c_out_specs,
        out_shape=[jax.ShapeDtypeStruct((S, width), BF16), jax.ShapeDtypeStruct((S, width), F32)] + c_out_shape,
        scratch_shapes=c_scratch,
        compiler_params=_params(("arbitrary", "arbitrary")),
    )(qa, ka, va, *c_ins)


def _comm_args(comm, aliased=False):
    if comm is None:
        return [], [], [], [], []
    assert aliased or not comm.aliases
    return comm.ins, [_ANY] * len(comm.ins), [_ANY] * len(comm.out_shape), comm.out_shape, comm.scratch


def _comm_aliases(comm, n_in, n_out):
    return {} if comm is None else {n_in + i: n_out + o for i, o in comm.aliases}


def _sbl_bwd(q, k, v, do, tot, *, width, hd, name, tq=256, comm=None):
    (qa, qo), (ka, ko), (va, vo) = q, k, v
    S = qa.shape[0]
    tq = _tile(S, tq)
    tk = tq
    scale = hd ** -0.5
    n_g, n_q = width // LANES, S // tq

    def body(*refs):
        qi = pl.program_id(1)
        gi = pl.program_id(0)
        got = _carry(comm, 5, 3, (gi == 0) & (qi == 0), (gi == n_g - 1) & (qi == n_q - 1), refs)
        (q_ref, k_ref, v_ref, do_ref, tot_ref), (dq_ref, dk_ref, dv_ref) = got[0], got[1]

        @pl.when(qi == 0)
        def _():
            dk_ref[...] = jnp.zeros_like(dk_ref)
            dv_ref[...] = jnp.zeros_like(dv_ref)

        masks = _lane_masks(hd, tq)
        qs = q_ref[...].astype(F32) * scale
        qm = [_keep(qs, m).astype(BF16) for m in masks]
        dov = [_keep(do_ref[...], m).astype(BF16) for m in masks]
        rest = [-_row_value(tot_ref[...], m) for m in masks]
        strict = lax.broadcasted_iota(jnp.int32, (tq, tk), 1) < lax.broadcasted_iota(jnp.int32, (tq, tk), 0)
        later, before = _tri("after", tk), _tri("before", tk)

        def tile(kb, carry, diag):
            ks = pl.multiple_of(kb * tk, tk)
            kv = k_ref[pl.ds(ks, tk), :].astype(BF16)
            vv = v_ref[pl.ds(ks, tk), :].astype(BF16)
            out = []
            dk_t, dv_t = None, None
            for hh in range(len(masks)):
                dq, c_pos, c_g = carry[3 * hh:3 * hh + 3]
                pos, ls = _softplus_parts(_dot(qm[hh], kv, _NT))
                if diag:
                    pos = jnp.where(strict, pos, 0.0)
                pos_after, pos_all = _scan_cols(pos, later, True)
                c_pos = c_pos + pos_all
                w = jnp.exp(ls - (pos_after + (rest[hh] - c_pos)))
                if diag:
                    w = jnp.where(strict, w, 0.0)
                g = _dot(dov[hh], vv, _NT) * w
                g_before, g_all = _scan_cols(g, before, False)
                g_before = g_before + c_g
                dz = g - jnp.exp(ls) * (g + g_before)
                if diag:
                    dz = jnp.where(strict, dz, 0.0)
                dzb = dz.astype(BF16)
                dk_h = _dot(dzb, qm[hh], _TN)
                dv_h = _dot(w.astype(BF16), dov[hh], _TN)
                dk_t = dk_h if dk_t is None else dk_t + dk_h
                dv_t = dv_h if dv_t is None else dv_t + dv_h
                out += [dq + _dot(dzb, kv), c_pos, c_g + g_all]
            dk_ref[pl.ds(ks, tk), :] += dk_t
            dv_ref[pl.ds(ks, tk), :] += dv_t
            return tuple(out)

        zero = jnp.zeros((tq, 1), F32)
        init = (jnp.zeros((tq, LANES), F32), zero, zero) * len(masks)
        carry = lax.fori_loop(0, qi, lambda kb, c: tile(kb, c, False), init)
        carry = tile(qi, carry, True)
        dq_ref[...] = _merge(carry[0::3], masks) * scale
        got[2]()

    full = jax.ShapeDtypeStruct((S, width), F32)
    c_ins, c_in_specs, c_out_specs, c_out_shape, c_scratch = _comm_args(comm)
    return pl.pallas_call(
        body, name=name, grid=(n_g, n_q),
        in_specs=[_cols(tq, qo), _cols_all(S, ko), _cols_all(S, vo), _cols(tq, do[1]), _cols(tq, tot[1])] + c_in_specs,
        out_specs=[_cols(tq, 0), _cols_all(S, 0), _cols_all(S, 0)] + c_out_specs,
        out_shape=[full, full, full] + c_out_shape,
        scratch_shapes=c_scratch,
        compiler_params=_params(("arbitrary", "arbitrary")),
    )(qa, ka, va, do[0], tot[0], *c_ins)


def _sml_fwd(q, k, v, bias=None, *, width, hd, causal, name, tq=256, tk=256, comm=None):
    (qa, qo), (ka, ko), (va, vo) = q, k, v
    S, Sk = qa.shape[0], ka.shape[0]
    tq, tk = _tile(S, tq), _tile(Sk, tk)
    if causal:
        assert tq == tk and S == Sk
    nk = Sk // tk
    hpg = LANES // hd
    scale = hd ** -0.5
    has_bias = bias is not None
    n_g, n_q = width // LANES, S // tq

    def body(*all_refs):
        qi, gi = pl.program_id(1), pl.program_id(0)
        got = _carry(comm, 5 if has_bias else 3, 2, (gi == 0) & (qi == 0), (gi == n_g - 1) & (qi == n_q - 1), all_refs)
        refs = tuple(got[0]) + tuple(got[1])
        q_ref, k_ref, v_ref = refs[:3]
        o_ref, lse_ref = refs[-2:]
        masks = _lane_masks(hd, tq)
        qs = q_ref[...].astype(F32) * scale
        qm = [_keep(qs, m).astype(BF16) for m in masks]
        allowed = lax.broadcasted_iota(jnp.int32, (tq, tk), 1) <= lax.broadcasted_iota(jnp.int32, (tq, tk), 0)

        def tile(kb, carry, diag):
            ks = pl.multiple_of(kb * tk, tk)
            kv = k_ref[pl.ds(ks, tk), :].astype(BF16)
            vv = v_ref[pl.ds(ks, tk), :].astype(BF16)
            out = []
            for hh in range(hpg):
                m, l, acc = carry[3 * hh:3 * hh + 3]
                z = _dot(qm[hh], kv, _NT)
                if has_bias:
                    z = z + refs[3][hh] - refs[4][hh, kb]
                if diag:
                    z = jnp.where(allowed, z, NEG_INF)
                m2 = jnp.maximum(m, jnp.max(z, axis=1, keepdims=True))
                p = jnp.exp(z - m2)
                alpha = jnp.exp(m - m2)
                out += [m2, alpha * l + jnp.sum(p, axis=1, keepdims=True), alpha * acc + _dot(p.astype(BF16), vv)]
            return tuple(out)

        init = (jnp.full((tq, 1), NEG_INF, F32), jnp.zeros((tq, 1), F32), jnp.zeros((tq, LANES), F32)) * hpg
        if causal:
            carry = lax.fori_loop(0, qi, lambda kb, c: tile(kb, c, False), init)
            carry = tile(qi, carry, True)
        else:
            carry = lax.fori_loop(0, nk, lambda kb, c: tile(kb, c, False), init)
        o_ref[...] = _merge([acc / l for l, acc in zip(carry[1::3], carry[2::3])], masks).astype(o_ref.dtype)
        lse_ref[...] = _merge([jnp.broadcast_to(m + jnp.log(l), (tq, LANES)) for m, l in zip(carry[0::3], carry[1::3])], masks)
        got[2]()

    in_specs = [_cols(tq, qo), _cols_all(Sk, ko), _cols_all(Sk, vo)]
    ins = [qa, ka, va]
    if has_bias:
        in_specs += [pl.BlockSpec((hpg, tq, 1), lambda g, i: (g, i, 0)),
                     pl.BlockSpec((hpg, nk, 1, tk), lambda g, i: (g, 0, 0, 0))]
        ins += list(bias)
    c_ins, c_in_specs, c_out_specs, c_out_shape, c_scratch = _comm_args(comm, aliased=True)
    return pl.pallas_call(
        body, name=name, grid=(n_g, n_q),
        in_specs=in_specs + c_in_specs, out_specs=[_cols(tq, 0), _cols(tq, 0)] + c_out_specs,
        out_shape=[jax.ShapeDtypeStruct((S, width), BF16), jax.ShapeDtypeStruct((S, width), F32)] + c_out_shape,
        input_output_aliases=_comm_aliases(comm, len(ins), 2),
        scratch_shapes=c_scratch,
        compiler_params=_params(("arbitrary", "arbitrary") if comm is not None else ("parallel", "arbitrary")),
    )(*ins, *c_ins)


def _sml_bwd(q, k, v, o, lse, do, bias=None, *, width, hd, causal, name, tq=256, tk=256, comm=None):
    (qa, qo), (ka, ko), (va, vo) = q, k, v
    S, Sk = qa.shape[0], ka.shape[0]
    tq, tk = _tile(S, tq), _tile(Sk, tk)
    nk = Sk // tk
    hpg = LANES // hd
    scale = hd ** -0.5
    has_bias = bias is not None
    n_in = 8 if has_bias else 6
    n_g, n_q = width // LANES, S // tq

    def body(*all_refs):
        qi, gi = pl.program_id(1), pl.program_id(0)
        got = _carry(comm, n_in, 5 if has_bias else 3, (gi == 0) & (qi == 0), (gi == n_g - 1) & (qi == n_q - 1), all_refs)
        refs = tuple(got[0]) + tuple(got[1])
        q_ref, k_ref, v_ref, o_ref, lse_ref, do_ref = refs[:6]
        dq_ref, dk_ref, dv_ref = refs[n_in:n_in + 3]

        @pl.when(qi == 0)
        def _():
            dk_ref[...] = jnp.zeros_like(dk_ref)
            dv_ref[...] = jnp.zeros_like(dv_ref)
            if has_bias:
                refs[n_in + 4][...] = jnp.zeros_like(refs[n_in + 4])

        masks = _lane_masks(hd, tq)
        qs = q_ref[...].astype(F32) * scale
        qm = [_keep(qs, m).astype(BF16) for m in masks]
        do32 = do_ref[...]
        dov = [_keep(do32, m).astype(BF16) for m in masks]
        prod = do32 * o_ref[...].astype(F32)
        delta = [jnp.sum(_keep(prod, m), axis=1, keepdims=True) for m in masks]
        lses = [_row_value(lse_ref[...], m) for m in masks]
        allowed = lax.broadcasted_iota(jnp.int32, (tq, tk), 1) <= lax.broadcasted_iota(jnp.int32, (tq, tk), 0)

        def tile(kb, carry, diag):
            ks = pl.multiple_of(kb * tk, tk)
            kv = k_ref[pl.ds(ks, tk), :].astype(BF16)
            vv = v_ref[pl.ds(ks, tk), :].astype(BF16)
            out = []
            dk_t, dv_t = None, None
            for hh in range(hpg):
                dq, db_row = carry[2 * hh:2 * hh + 2]
                z = _dot(qm[hh], kv, _NT)
                if has_bias:
                    z = z + refs[6][hh] - refs[7][hh, kb]
                p = jnp.exp(z - lses[hh])
                if diag:
                    p = jnp.where(allowed, p, 0.0)
                dz = p * (_dot(dov[hh], vv, _NT) - delta[hh])
                dzb = dz.astype(BF16)
                dk_h = _dot(dzb, qm[hh], _TN)
                dv_h = _dot(p.astype(BF16), dov[hh], _TN)
                dk_t = dk_h if dk_t is None else dk_t + dk_h
                dv_t = dv_h if dv_t is None else dv_t + dv_h
                if has_bias:
                    db_row = db_row + jnp.sum(dz, axis=1, keepdims=True)
                    refs[n_in + 4][hh, kb] += jnp.sum(dz, axis=0, keepdims=True)
                out += [dq + _dot(dzb, kv), db_row]
            dk_ref[pl.ds(ks, tk), :] += dk_t
            dv_ref[pl.ds(ks, tk), :] += dv_t
            return tuple(out)

        init = (jnp.zeros((tq, LANES), F32), jnp.zeros((tq, 1), F32)) * hpg
        if causal:
            carry = lax.fori_loop(0, qi, lambda kb, c: tile(kb, c, False), init)
            carry = tile(qi, carry, True)
        else:
            carry = lax.fori_loop(0, nk, lambda kb, c: tile(kb, c, False), init)
        dq_ref[...] = _merge(carry[0::2], masks) * scale
        if has_bias:
            for hh in range(hpg):
                refs[n_in + 3][hh] = carry[2 * hh + 1]
        got[2]()

    in_specs = [_cols(tq, qo), _cols_all(Sk, ko), _cols_all(Sk, vo), _cols(tq, o[1]), _cols(tq, lse[1]), _cols(tq, do[1])]
    ins = [qa, ka, va, o[0], lse[0], do[0]]
    out_specs = [_cols(tq, 0), _cols_all(Sk, 0), _cols_all(Sk, 0)]
    out_shape = [jax.ShapeDtypeStruct((S, width), F32), jax.ShapeDtypeStruct((Sk, width), F32),
                 jax.ShapeDtypeStruct((Sk, width), F32)]
    if has_bias:
        rspec = pl.BlockSpec((hpg, tq, 1), lambda g, i: (g, i, 0))
        cspec = pl.BlockSpec((hpg, nk, 1, tk), lambda g, i: (g, 0, 0, 0))
        in_specs += [rspec, cspec]
        ins += list(bias)
        out_specs += [rspec, cspec]
        n_heads = width // hd
        out_shape += [jax.ShapeDtypeStruct((n_heads, S, 1), F32), jax.ShapeDtypeStruct((n_heads, nk, 1, tk), F32)]
    c_ins, c_in_specs, c_out_specs, c_out_shape, c_scratch = _comm_args(comm)
    return pl.pallas_call(
        body, name=name, grid=(n_g, n_q),
        in_specs=in_specs + c_in_specs, out_specs=out_specs + c_out_specs, out_shape=out_shape + c_out_shape,
        scratch_shapes=c_scratch,
        compiler_params=_params(("arbitrary", "arbitrary") if comm is not None else ("parallel", "arbitrary")),
    )(*ins, *c_ins)


def _head_sums(t, masks):
    sums = [jnp.sum(_keep(t, m), axis=1, keepdims=True) for m in masks]
    return _merge([jnp.broadcast_to(s, t.shape) for s in sums], masks) if len(masks) > 1 else sums[0]


def _hnorm_fwd(x, g_lanes, *, width, hd, name, tr=1024):
    xa, xo = x
    R = xa.shape[0]
    tr = _tile(R, tr, align=16)
    n_blk = width // LANES

    def body(x_ref, g_ref, o_ref):
        masks = _lane_masks(hd, tr)
        for j in range(n_blk):
            sl = slice(j * LANES, (j + 1) * LANES)
            xv = x_ref[:, sl].astype(F32)
            r = lax.rsqrt(_head_sums(xv * xv, masks) * (1.0 / hd) + EPS)
            o_ref[:, sl] = (xv * r * g_ref[...]).astype(o_ref.dtype)

    assert (xo * LANES) % width == 0
    return pl.pallas_call(
        body, name=name, grid=(R // tr,),
        in_specs=[pl.BlockSpec((tr, width), lambda i: (i, xo * LANES // width)), pl.BlockSpec((1, LANES), lambda i: (0, 0))],
        out_specs=pl.BlockSpec((tr, width), lambda i: (i, 0)),
        out_shape=jax.ShapeDtypeStruct((R, width), BF16),
        compiler_params=_params(("parallel",)),
    )(xa, g_lanes)


def _hnorm_bwd(x, g_lanes, dy, *, width, hd, name, tr=1024):
    xa, xo = x
    R = xa.shape[0]
    tr = _tile(R, tr, align=16)
    n_blk = width // LANES

    def body(x_ref, g_ref, dy_ref, dx_ref, dg_ref):
        masks = _lane_masks(hd, tr)
        dg = jnp.zeros((1, LANES), F32)
        for j in range(n_blk):
            sl = slice(j * LANES, (j + 1) * LANES)
            xv = x_ref[:, sl].astype(F32)
            dyv = dy_ref[:, sl].astype(F32)
            r = lax.rsqrt(_head_sums(xv * xv, masks) * (1.0 / hd) + EPS)
            xh = xv * r
            dyg = dyv * g_ref[...]
            c = _head_sums(dyg * xh, masks) * (1.0 / hd)
            dx_ref[:, sl] = (r * (dyg - xh * c)).astype(dx_ref.dtype)
            dg = dg + jnp.sum(dyv * xh, axis=0, keepdims=True)
        if hd * 2 == LANES:
            dg8 = jnp.broadcast_to(dg, (8, LANES))
            dg = (dg8 + pltpu.roll(dg8, shift=hd, axis=1))[0:1]
        else:
            assert hd == LANES

        @pl.when(pl.program_id(0) == 0)
        def _():
            dg_ref[...] = jnp.zeros_like(dg_ref)

        dg_ref[...] += dg

    assert (xo * LANES) % width == 0
    return pl.pallas_call(
        body, name=name, grid=(R // tr,),
        in_specs=[pl.BlockSpec((tr, width), lambda i: (i, xo * LANES // width)), pl.BlockSpec((1, LANES), lambda i: (0, 0)),
                  pl.BlockSpec((tr, width), lambda i: (i, 0))],
        out_specs=[pl.BlockSpec((tr, width), lambda i: (i, 0)), pl.BlockSpec((1, LANES), lambda i: (0, 0))],
        out_shape=[jax.ShapeDtypeStruct((R, width), BF16), jax.ShapeDtypeStruct((1, LANES), F32)],
        compiler_params=_params(("arbitrary",)),
    )(xa, g_lanes, dy)


def _split3_dot(x, tri):
    a = x.astype(BF16)
    r = x - a.astype(F32)
    b = r.astype(BF16)
    c = (r - b.astype(F32)).astype(BF16)
    return _dot(a, tri) + _dot(b, tri) + _dot(c, tri)


def _forget_fwd(logit_t, b_col, *, name, blk=512):
    H, S = logit_t.shape
    blk = _tile(S, blk)

    def body(l_ref, b_ref, f_ref):
        r_i = lax.broadcasted_iota(jnp.int32, (blk, blk), 0)
        c_i = lax.broadcasted_iota(jnp.int32, (blk, blk), 1)
        upto = (r_i <= c_i).astype(BF16)
        carry = jnp.zeros((H, 1), F32)
        for j in range(S // blk):
            u = l_ref[:, j * blk:(j + 1) * blk] + b_ref[...]
            lf, _ = _log_sigmoid_pair(u)
            f_ref[:, j * blk:(j + 1) * blk] = _split3_dot(lf, upto) + carry
            carry = carry + jnp.sum(lf, axis=1, keepdims=True)

    return pl.pallas_call(
        body, name=name,
        out_shape=jax.ShapeDtypeStruct((H, S), F32),
        compiler_params=pltpu.CompilerParams(vmem_limit_bytes=VMEM_LIMIT),
    )(logit_t, b_col)


def _forget_bwd(logit_t, b_col, d_f, *, name, blk=512):
    H, S = logit_t.shape
    blk = _tile(S, blk)

    def body(l_ref, b_ref, df_ref, dl_ref, db_ref):
        r_i = lax.broadcasted_iota(jnp.int32, (blk, blk), 0)
        c_i = lax.broadcasted_iota(jnp.int32, (blk, blk), 1)
        fromon = (r_i >= c_i).astype(BF16)
        carry = jnp.zeros((H, 1), F32)
        db = jnp.zeros((H, 1), F32)
        for j in reversed(range(S // blk)):
            sl = slice(j * blk, (j + 1) * blk)
            dfv = df_ref[:, sl]
            d_lf = _split3_dot(dfv, fromon) + carry
            carry = carry + jnp.sum(dfv, axis=1, keepdims=True)
            u = l_ref[:, sl] + b_ref[...]
            _, lsn = _log_sigmoid_pair(u)
            dl = d_lf * jnp.exp(lsn)
            dl_ref[:, sl] = dl
            db = db + jnp.sum(dl, axis=1, keepdims=True)
        db_ref[...] = db

    return pl.pallas_call(
        body, name=name,
        out_shape=[jax.ShapeDtypeStruct((H, S), F32), jax.ShapeDtypeStruct((H, 1), F32)],
        compiler_params=pltpu.CompilerParams(vmem_limit_bytes=VMEM_LIMIT),
    )(logit_t, b_col, d_f)


def _sigmoid(t):
    return 1.0 / (1.0 + jnp.exp(-t))


def _gate_fwd(o3, w3, proj, x, w_out, g_norm, D, *, name, tm=512):
    S = proj.shape[0]
    tm = _tile(S, tm)

    def body(o0, o1, o2, w0, w1, w2, g0, g1, g2, x_ref, wo_ref, gn_ref, merged_ref, x1_ref, h2_ref):
        acc = None
        for o_ref, w_ref, g_ref in ((o0, w0, g0), (o1, w1, g1), (o2, w2, g2)):
            t = _sigmoid(g_ref[...]) * _dot(o_ref[...], w_ref[...])
            acc = t if acc is None else acc + t
        merged = acc.astype(BF16)
        merged_ref[...] = merged
        x1 = x_ref[...] + _dot(merged, wo_ref[...])
        x1_ref[...] = x1
        h2_ref[...] = (x1 * lax.rsqrt(jnp.mean(x1 * x1, axis=-1, keepdims=True) + EPS) * gn_ref[...]).astype(BF16)

    ospec = lambda d: pl.BlockSpec((tm, d), lambda i: (i, 0))
    wspec = lambda w: pl.BlockSpec(w.shape, lambda i: (0, 0))
    gspec = lambda j: pl.BlockSpec((tm, D), lambda i: (i, j))
    row = pl.BlockSpec((tm, D), lambda i: (i, 0))
    return pl.pallas_call(
        body, name=name, grid=(S // tm,),
        in_specs=[ospec(o.shape[1]) for o in o3] + [wspec(w) for w in w3] + [gspec(j) for j in range(3)]
        + [row, wspec(w_out), wspec(g_norm)],
        out_specs=[row, row, row],
        out_shape=[jax.ShapeDtypeStruct((S, D), BF16), jax.ShapeDtypeStruct((S, D), F32), jax.ShapeDtypeStruct((S, D), BF16)],
        compiler_params=_params(("parallel",)),
    )(*o3, *w3, proj, proj, proj, x, w_out, g_norm)


def _gate_bwd(o3, w3, proj, dx1, w_out, D, *, name, tm=256):
    S = proj.shape[0]
    tm = _tile(S, tm)

    def body(o0, o1, o2, w0, w1, w2, g0, g1, g2, dx_ref, wo_ref, dg_ref, dw0, dw1, dw2, do0, do1, do2):
        first = pl.program_id(0) == 0
        dm = _dot(dx_ref[...].astype(BF16), wo_ref[...], _NT)
        for j, (o_ref, w_ref, g_ref, dw_ref, do_ref) in enumerate(
                ((o0, w0, g0, dw0, do0), (o1, w1, g1, dw1, do1), (o2, w2, g2, dw2, do2))):
            s = _sigmoid(g_ref[...])
            br = _dot(o_ref[...], w_ref[...])
            dg_ref[:, j * D:(j + 1) * D] = (dm * br * s * (1.0 - s)).astype(dg_ref.dtype)
            dbr = (dm * s).astype(BF16)
            do_ref[...] = _dot(dbr, w_ref[...], _NT)
            part = _dot(o_ref[...], dbr, _TN)

            @pl.when(first)
            def _():
                dw_ref[...] = part

            @pl.when(jnp.logical_not(first))
            def _():
                dw_ref[...] += part

    ospec = lambda d: pl.BlockSpec((tm, d), lambda i: (i, 0))
    wspec = lambda w: pl.BlockSpec(w.shape, lambda i: (0, 0))
    gspec = lambda j: pl.BlockSpec((tm, D), lambda i: (i, j))
    dspec = pl.BlockSpec((tm, D), lambda i: (i, 0))
    return pl.pallas_call(
        body, name=name, grid=(S // tm,),
        in_specs=[ospec(o.shape[1]) for o in o3] + [wspec(w) for w in w3] + [gspec(j) for j in range(3)]
        + [dspec, wspec(w_out)],
        out_specs=[pl.BlockSpec((tm, 3 * D), lambda i: (i, 0))] + [wspec(w) for w in w3] + [ospec(o.shape[1]) for o in o3],
        out_shape=[jax.ShapeDtypeStruct((S, proj.shape[1]), BF16)] + [jax.ShapeDtypeStruct(w.shape, F32) for w in w3]
        + [jax.ShapeDtypeStruct((S, o.shape[1]), F32) for o in o3],
        compiler_params=_params(("arbitrary",)),
    )(*o3, *w3, proj, proj, proj, dx1, w_out)


def _pair_sum(stacked, got, core, *, name):
    n, _, r, c = stacked.shape
    tr = _row_tile(r, c, 3)

    def body(core_ref, a_ref, b_ref, o_ref):
        o_ref[...] = (a_ref[0].astype(F32) + b_ref[...].astype(F32)).astype(o_ref.dtype)

    spec = pl.BlockSpec((1, tr, c), lambda s, i, core_ref: (s, i, 0))
    return pl.pallas_call(
        body, name=name,
        grid_spec=pltpu.PrefetchScalarGridSpec(
            num_scalar_prefetch=1, grid=(n, r // tr),
            in_specs=[pl.BlockSpec((1, 1, tr, c), lambda s, i, core_ref: (s, core_ref[0], i, 0)), spec],
            out_specs=spec),
        out_shape=jax.ShapeDtypeStruct((n, r, c), BF16),
        compiler_params=_params(("parallel", "parallel")),
    )(core.astype(jnp.int32).reshape(1), stacked, got)


def _chip_sum(parts, got, chip, *, name):
    _, r, c = parts.shape
    tr = _row_tile(r, c, 6)

    def body(chip_ref, p_ref, q0_ref, q1_ref, q2_ref, o_ref):
        o_ref[...] = ((p_ref[0].astype(F32) + q0_ref[0].astype(F32)) + q1_ref[0].astype(F32)) + q2_ref[0].astype(F32)

    from_chip = lambda j: pl.BlockSpec((1, tr, c), lambda i, chip_ref: (j, i, 0))
    return pl.pallas_call(
        body, name=name,
        grid_spec=pltpu.PrefetchScalarGridSpec(
            num_scalar_prefetch=1, grid=(r // tr,),
            in_specs=[pl.BlockSpec((1, tr, c), lambda i, chip_ref: (chip_ref[0], i, 0))] + [from_chip(j) for j in range(3)],
            out_specs=pl.BlockSpec((tr, c), lambda i, chip_ref: (i, 0))),
        out_shape=jax.ShapeDtypeStruct((r, c), F32),
        compiler_params=_params(("parallel",)),
    )(chip.astype(jnp.int32).reshape(1), parts, got, got, got)


def _adamw_math(w, g, m, v):
    m2 = ADAM_B1 * m + (1.0 - ADAM_B1) * g
    v2 = ADAM_B2 * v + (1.0 - ADAM_B2) * (g * g)
    m_hat = m2 / (1.0 - ADAM_B1 ** ADAM_STEP)
    v_hat = v2 / (1.0 - ADAM_B2 ** ADAM_STEP)
    delta = -ADAM_LR * (m_hat / (jnp.sqrt(v_hat) + ADAM_EPS) + ADAM_WD * w)
    return delta, m2, v2


def _adamw(w, g, m, v, *, name):
    return _ew(_adamw_math, [w, g, m, v], (F32, F32, F32), name=name)


def _adamw_small(w, parts, m, v, *, name):
    n = parts.shape[0]

    def body(w_ref, p_ref, m_ref, v_ref, g_ref, d_ref, m2_ref, v2_ref):
        g = p_ref[0]
        for i in range(1, n):
            g = g + p_ref[i]
        g_ref[...] = g
        d_ref[...], m2_ref[...], v2_ref[...] = _adamw_math(w_ref[...], g, m_ref[...], v_ref[...])

    shp = jax.ShapeDtypeStruct(w.shape, F32)
    return pl.pallas_call(body, name=name, out_shape=[shp] * 4)(w, parts, m, v)


_ANY = pl.BlockSpec(memory_space=pl.ANY)


def _mesh_place():
    x, y, c = lax.axis_index("x"), lax.axis_index("y"), lax.axis_index("c")
    chips = [(1 - x, y), (x, 1 - y), (1 - x, 1 - y)]
    return x, y, c, chips


def _remote(src, dst, sems, i, to):
    send_sems, recv_sems = sems
    return pltpu.make_async_remote_copy(src_ref=src, dst_ref=dst, send_sem=send_sems.at[i], recv_sem=recv_sems.at[i],
                                        device_id=to, device_id_type=MESH_ID)


def _gather_weights(shards, *, name):
    n = len(shards)

    def body(*refs):
        ins, outs = refs[:n], refs[n:2 * n]
        sems = refs[2 * n:2 * n + 2]
        x, y, c, chips = _mesh_place()
        me = 2 * x + y
        sibling = (x, y, 1 - c)
        sent = []
        for w in range(n):
            for j, chip in enumerate(chips):
                cp = _remote(ins[w].at[c], outs[w].at[me, c], sems, 6 * w + j, (chip[0], chip[1], c))
                cp.start()
                sent.append(cp)
        for w in range(n):
            for j, chip in enumerate(chips):
                got = outs[w].at[2 * chip[0] + chip[1], c]
                _remote(got, got, sems, 6 * w + j, sibling).wait_recv()
                cp = _remote(got, got, sems, 6 * w + 3 + j, sibling)
                cp.start()
                sent.append(cp)
        for w in range(n):
            for j, chip in enumerate(chips):
                got = outs[w].at[2 * chip[0] + chip[1], 1 - c]
                _remote(got, got, sems, 6 * w + 3 + j, sibling).wait_recv()
        for cp in sent:
            cp.wait_send()

    outs = pl.pallas_call(
        body, name=name,
        in_specs=[_ANY] * n, out_specs=[_ANY] * n,
        out_shape=[jax.ShapeDtypeStruct((N_CHIPS,) + s.shape, s.dtype) for s in shards],
        scratch_shapes=[pltpu.SemaphoreType.DMA((6 * n,)), pltpu.SemaphoreType.DMA((6 * n,))],
    )(*shards)
    me = 2 * lax.axis_index("x") + lax.axis_index("y")
    return [lax.dynamic_update_index_in_dim(o, s, me, 0) for o, s in zip(outs, shards)]


def _exchange_siblings(grads, *, name):
    n = len(grads)

    def body(*refs):
        ins, got = refs[:n], refs[n:2 * n]
        sems = refs[2 * n:2 * n + 2]
        x, y, c, _ = _mesh_place()
        sibling = (x, y, 1 - c)
        sent = []
        for w in range(n):
            for s in range(N_CHIPS):
                cp = _remote(ins[w].at[s, 1 - c], got[w].at[s], sems, N_CHIPS * w + s, sibling)
                cp.start()
                sent.append(cp)
        for w in range(n):
            for s in range(N_CHIPS):
                _remote(got[w].at[s], got[w].at[s], sems, N_CHIPS * w + s, sibling).wait_recv()
        for cp in sent:
            cp.wait_send()

    n_sem = N_CHIPS * n
    return pl.pallas_call(
        body, name=name,
        in_specs=[_ANY] * n, out_specs=[_ANY] * n,
        out_shape=[jax.ShapeDtypeStruct((N_CHIPS,) + g.shape[2:], g.dtype) for g in grads],
        scratch_shapes=[pltpu.SemaphoreType.DMA((n_sem,)), pltpu.SemaphoreType.DMA((n_sem,))],
    )(*grads)


def _share_halves(halves, small):
    n = len(halves)

    def body(*refs):
        ins, small_ref = refs[:n], refs[n]
        outs, small_out = refs[n + 1:2 * n + 1], refs[2 * n + 1]
        sems = refs[2 * n + 2:2 * n + 4]
        x, y, c, chips = _mesh_place()
        sibling = (x, y, 1 - c)
        me = 4 * x + 2 * y + c
        sent = [_remote(ins[w], outs[w].at[c], sems, w, sibling) for w in range(n)]
        peers = [sibling] + [(ch[0], ch[1], cc) for ch in chips for cc in (c, 1 - c)]
        sent += [_remote(small_ref, small_out.at[me], sems, n + j, peer) for j, peer in enumerate(peers)]
        for cp in sent:
            cp.start()
        for w in range(n):
            _remote(outs[w].at[1 - c], outs[w].at[1 - c], sems, w, sibling).wait_recv()
        for j, peer in enumerate(peers):
            frm = small_out.at[4 * peer[0] + 2 * peer[1] + peer[2]]
            _remote(frm, frm, sems, n + j, peer).wait_recv()
        for cp in sent:
            cp.wait_send()

    n_sem = n + 7
    outs = pl.pallas_call(
        body, name="share_halves",
        in_specs=[_ANY] * (n + 1), out_specs=[_ANY] * (n + 1),
        out_shape=[jax.ShapeDtypeStruct((2,) + h.shape, h.dtype) for h in halves]
        + [jax.ShapeDtypeStruct((8,) + small.shape, small.dtype)],
        scratch_shapes=[pltpu.SemaphoreType.DMA((n_sem,)), pltpu.SemaphoreType.DMA((n_sem,))],
    )(*halves, small)
    c = lax.axis_index("c")
    me = 4 * lax.axis_index("x") + 2 * lax.axis_index("y") + c
    return ([lax.dynamic_update_index_in_dim(o, h, c, 0) for o, h in zip(outs[:n], halves)],
            lax.dynamic_update_index_in_dim(outs[n], small, me, 0))


EARLY = ("w_ff_down", "w_ff_up", "w_out", "w_branch_sb", "w_branch_fox", "w_branch_mem", "w_mem_kv")


def _norm_bwd_tail(dy, x, add, g):
    r = lax.rsqrt(jnp.mean(x * x, axis=-1, keepdims=True) + EPS)
    xh = x * r
    dyg = dy * g
    c = jnp.mean(dyg * xh, axis=-1, keepdims=True)
    return r * (dyg - xh * c) + add, jnp.sum(dy * xh, axis=0, keepdims=True)


def _split(outs, n):
    outs = list(outs) if isinstance(outs, (list, tuple)) else [outs]
    return outs[:n], outs[n:]


def _local_step(x, mem, target, small, W, gather_rest=None, reduce_early=None, reduce_late=None, share_early=None):
    S, D = x.shape
    o_qkv, o_mq, o_f = 3 * D, 3 * D + 2 * 3 * D_SB, 3 * D + 2 * 3 * D_SB + D_MEM
    tq = 512

    g_comms, finish_weights = gather_rest if gather_rest is not None else ([None] * 3, None)
    (proj, h), landed = _split(_mm(x, W["w_in"], name="in_proj", tb=True, tn=768, a_gain=small["g_mix_norm"],
                                   comm=g_comms[0]), 2)
    blk = lambda j: (proj, (o_qkv + j * D_SB) // LANES)
    sb_q, sb_k, sb_v, fx_q, fx_k, fx_v = [blk(j) for j in range(6)]
    m_q = (proj, o_mq // LANES)
    f_logit_t = _mm(W["w_in"][o_f:o_f + ROW_TILE], h, name="forget_logits", tb=True)[:FOX_HEADS]
    b_col = small["b_forget"].reshape(FOX_HEADS, 1)
    lanes = lambda g: jnp.tile(g, (1, LANES // g.shape[1]))
    g_fq, g_fk, g_mq, g_mk = [lanes(small[k]) for k in ("g_fox_q", "g_fox_k", "g_mem_q", "g_mem_k")]

    (o_sb, sb_tot), more = _split(_sbl_fwd(sb_q, sb_k, sb_v, width=D_SB, hd=HD, name="sb_fwd", tq=tq, comm=g_comms[1]), 2)
    landed += more

    fq = _hnorm_fwd(fx_q, g_fq, width=D_FOX, hd=HD, name="fox_q_norm")
    fk = _hnorm_fwd(fx_k, g_fk, width=D_FOX, hd=HD, name="fox_k_norm")
    f_cum = _forget_fwd(f_logit_t, b_col, name="forget_fwd")
    tkf = _tile(S, tq)
    f_bias = (f_cum.reshape(FOX_HEADS, S, 1), f_cum.reshape(FOX_HEADS, S // tkf, 1, tkf))
    fox_comm = None if finish_weights is None else _Joint([g_comms[2], _ChipExchange("forward", landed)])
    (o_fox, fox_lse), more = _split(_sml_fwd((fq, 0), (fk, 0), fx_v, f_bias, width=D_FOX, hd=HD, causal=True,
                                             name="fox_fwd", tq=tq, tk=tq, comm=fox_comm), 2)
    mem_comm = None
    if finish_weights is not None:
        n_last = len(g_comms[2].ins)
        W = {**W, **finish_weights((0, 1), more[n_last:])}
        mem_comm = _ChipExchange("forward", more[:n_last])

    mh = _rmsnorm_fwd(mem, small["g_mem_norm"], BF16, name="mem_norm")
    mkv = _mm(mh, W["w_mem_kv"], name="mem_kv")
    mv = (mkv, D_MEM // LANES)
    mq = _hnorm_fwd(m_q, g_mq, width=D_MEM, hd=MEM_HD, name="mem_q_norm")
    mk = _hnorm_fwd((mkv, 0), g_mk, width=D_MEM, hd=MEM_HD, name="mem_k_norm")
    (o_mem, mem_lse), more = _split(_sml_fwd((mq, 0), (mk, 0), mv, width=D_MEM, hd=MEM_HD, causal=False, name="mem_fwd",
                                             tq=tq, tk=256, comm=mem_comm), 2)
    if finish_weights is not None:
        W = {**W, **finish_weights((2,), more)}

    o3 = [o_sb, o_fox, o_mem]
    w3 = [W["w_branch_sb"], W["w_branch_fox"], W["w_branch_mem"]]
    merged, x1, h2 = _gate_fwd(o3, w3, proj, x, W["w_out"], small["g_mlp_norm"], D, name="gate_fwd")

    def relu2(acc):
        u = jnp.maximum(acc, 0.0)
        return u, u * u

    u, a = _mm(h2, W["w_ff_up"], name="ff_up", out_dtypes=(BF16, BF16), epilogue=relu2)
    def head(acc, res, tgt):
        d = (res + acc - tgt) * (1.0 / D)
        return d, d, jnp.sum(d * d, axis=0, keepdims=True)

    dy, dy16, sq_rows = _mm(a, W["w_ff_down"], name="ff_down", extras=(x1, target), out_dtypes=(F32, BF16),
                            epilogue=head, col_sums=1, tn=512)
    loss = (0.5 * D) * jnp.sum(sq_rows)

    G = {}
    du = _mm(dy16, W["w_ff_down"], name="d_ff_act", tb=True, out_dtypes=(BF16,), extras=(u,),
             epilogue=lambda acc, uu: (acc * (2.0 * uu.astype(F32)),))
    G["w_ff_down"] = _mm(a, dy16, name="d_w_ff_down", ta=True, out_dtypes=(BF16,))
    G["w_ff_up"] = _mm(h2, du, name="d_w_ff_up", ta=True, out_dtypes=(BF16,))
    dx1, dg_rows = _mm(du, W["w_ff_up"], name="d_mlp_in", tb=True, extras=(x1, dy, small["g_mlp_norm"]),
                       epilogue=_norm_bwd_tail, col_sums=1, tm=512, tn=D)
    dg_mlp = jnp.sum(dg_rows, axis=0, keepdims=True)
    G["w_out"] = _mm(merged, dx1, name="d_w_out", ta=True, out_dtypes=(BF16,))
    dgate, dw0, dw1, dw2, do_sb, do_fox, do_mem = _gate_bwd(o3, w3, proj, dx1, W["w_out"], D, name="gate_bwd")
    for nm, dw in zip(("w_branch_sb", "w_branch_fox", "w_branch_mem"), (dw0, dw1, dw2)):
        G[nm] = dw.astype(BF16)

    sib_comm, after_siblings = (reduce_early({k: G.pop(k) for k in EARLY if k != "w_mem_kv"})
                                if reduce_early is not None else (None, None))
    (dmq_n, dmk_n, dmv), landed_sib = _split(
        _sml_bwd((mq, 0), (mk, 0), mv, (o_mem, 0), (mem_lse, 0), (do_mem, 0), width=D_MEM, hd=MEM_HD, causal=False,
                 name="mem_bwd", tq=tq, tk=256, comm=sib_comm), 3)
    dm_q, dg_mem_q = _hnorm_bwd(m_q, g_mq, dmq_n, width=D_MEM, hd=MEM_HD, name="d_mem_q_norm")
    dmk_raw, dg_mem_k = _hnorm_bwd((mkv, 0), g_mk, dmk_n, width=D_MEM, hd=MEM_HD, name="d_mem_k_norm")
    dmkv = jnp.concatenate([dmk_raw, dmv.astype(BF16)], axis=1)
    G["w_mem_kv"] = _mm(mh, dmkv, name="d_w_mem_kv", ta=True, out_dtypes=(BF16,))
    dmh = _mm(dmkv, W["w_mem_kv"], name="d_mem_h", tb=True)
    _, dg_mem = _rmsnorm_bwd(mem, small["g_mem_norm"], dmh, name="d_mem_norm")

    r_comms, r_finish = (after_siblings(landed_sib, {"w_mem_kv": G.pop("w_mem_kv")})
                         if after_siblings is not None else ([None] * 2, None))
    dsb, landed_sb = _split(_sbl_bwd(sb_q, sb_k, sb_v, (do_sb, 0), (sb_tot, 0), width=D_SB, hd=HD, name="sb_bwd", tq=tq,
                                     comm=r_comms[0]), 3)
    (dfq, dfk, dfv, df_row, df_col), landed_fox = _split(
        _sml_bwd((fq, 0), (fk, 0), fx_v, (o_fox, 0), (fox_lse, 0), (do_fox, 0), f_bias, width=D_FOX, hd=HD, causal=True,
                 name="fox_bwd", tq=tq, tk=tq, comm=r_comms[1]), 5)
    early = r_finish(landed_sb, landed_fox) if r_finish is not None else {}
    dfx_q, dg_fox_q = _hnorm_bwd(fx_q, g_fq, dfq, width=D_FOX, hd=HD, name="d_fox_q_norm")
    dfx_k, dg_fox_k = _hnorm_bwd(fx_k, g_fk, dfk, width=D_FOX, hd=HD, name="d_fox_k_norm")
    d_fcum = df_row.reshape(FOX_HEADS, S) - df_col.reshape(FOX_HEADS, S)
    d_flogit_t, db_forget = _forget_bwd(f_logit_t, b_col, d_fcum, name="forget_bwd")
    dg_fox_q, dg_fox_k = dg_fox_q[:, :HD], dg_fox_k[:, :HD]

    rest_cols = jnp.concatenate([t.astype(BF16) for t in (*dsb, dfx_q, dfx_k, dfv, dm_q)]
                                + [d_flogit_t.T.astype(BF16), jnp.zeros((S, F_PAD - FOX_HEADS), BF16)], axis=1)
    dproj = lax.dynamic_update_slice(dgate, rest_cols, (0, 3 * D))
    share_comm, shared = share_early(early) if share_early is not None else (None, None)
    (g_w_in,), landed = _split(_mm(dproj, h, name="d_w_in", ta=True, out_dtypes=(BF16,), tm=768, comm=share_comm), 1)
    if shared is not None:
        early = shared(landed)
    comm, finish = reduce_late({"w_in": g_w_in}) if reduce_late is not None else (None, None)
    (grad_x, dg_rows), landed = _split(
        _mm(dproj, W["w_in"], name="d_mix_in", tk=2304, tm=512, tn=D, extras=(x, dx1, small["g_mix_norm"]),
            epilogue=_norm_bwd_tail, col_sums=1, comm=comm), 2)
    dg_mix = jnp.sum(dg_rows, axis=0, keepdims=True)
    if finish is None:
        G["w_in"] = g_w_in
    else:
        early.update(finish(landed))

    small_grads = dict(g_mix_norm=dg_mix, g_mem_norm=dg_mem, b_forget=db_forget.reshape(1, FOX_HEADS),
                       g_fox_q=dg_fox_q, g_fox_k=dg_fox_k, g_mem_q=dg_mem_q, g_mem_k=dg_mem_k, g_mlp_norm=dg_mlp)
    return loss, grad_x, G, small_grads, early


BIG = ("w_in", "w_mem_kv", "w_branch_sb", "w_branch_fox", "w_branch_mem", "w_out", "w_ff_up", "w_ff_down")
COLUMN_SHARDED = ("w_in", "w_branch_sb", "w_branch_fox", "w_branch_mem", "w_ff_up")
SMALL = ("g_mix_norm", "g_mem_norm", "b_forget", "g_fox_q", "g_fox_k", "g_mem_q", "g_mem_k", "g_mlp_norm")
ORDER = ("g_mix_norm", "g_mem_norm", "w_in", "b_forget", "g_fox_q", "g_fox_k", "g_mem_q", "g_mem_k", "w_mem_kv",
         "w_branch_sb", "w_branch_fox", "w_branch_mem", "w_out", "g_mlp_norm", "w_ff_up", "w_ff_down")


def _unshard(name, gathered):
    n, _, rh, c = gathered.shape
    t = gathered.reshape(n, 2 * rh, c)
    if name in COLUMN_SHARDED:
        return t.transpose(1, 0, 2).reshape(2 * rh, n * c)
    return t.reshape(n * 2 * rh, c)


def _reshard(name, full):
    if name in COLUMN_SHARDED:
        r, c = full.shape
        t = full.reshape(r, N_CHIPS, c // N_CHIPS).transpose(1, 0, 2)
    else:
        r, c = full.shape[0] // N_CHIPS, full.shape[1]
        t = full.reshape(N_CHIPS, r, c)
    return t.reshape(N_CHIPS, 2, t.shape[1] // 2, t.shape[2])


ROW_TILE = 16
IN_BUF_ALIGN = 256


def _in_segments(D):
    n_qkv = 6 * D_SB
    o_mq, o_gate = n_qkv + FOX_HEADS, n_qkv + FOX_HEADS + D_MEM
    return [(0, n_qkv, 3 * D), (n_qkv, o_mq, 3 * D + n_qkv + D_MEM), (o_mq, o_gate, 3 * D + n_qkv), (o_gate, o_gate + 3 * D, 0)]


class _InLayout:
    def __init__(self, D, shard, n):
        self.D, self.shard, self.n = D, shard, n
        down = lambda v: v // ROW_TILE * ROW_TILE
        up = lambda v: -(-v // ROW_TILE) * ROW_TILE
        self.pieces = []
        ends = []
        for s in range(n):
            cursor, mine = 0, []
            for a, b, p in _in_segments(D):
                x0, x1 = max(a, s * shard), min(b, (s + 1) * shard)
                if x0 < x1:
                    p0 = p + x0 - a
                    rows = up(p0 + x1 - x0) - down(p0)
                    mine.append((x0 - s * shard, x1 - x0, p0, cursor, rows))
                    cursor += rows
            self.pieces.append(mine)
            ends.append(cursor)
        self.rows = -(-max(ends) // IN_BUF_ALIGN) * IN_BUF_ALIGN
        self.padded_rows = 3 * D + 6 * D_SB + D_MEM + F_PAD

    def _per_shard(self, fn, chip, operand):
        return lax.switch(chip, [functools.partial(fn, s) for s in range(self.n)], operand)

    def pack(self, chip, rows):
        def one(s, t):
            out, at = [], 0
            for x0, n_rows, p0, start, region in self.pieces[s]:
                lead = p0 % ROW_TILE
                out += [jnp.zeros((start + lead - at, t.shape[1]), t.dtype), t[x0:x0 + n_rows]]
                at = start + lead + n_rows
            return jnp.concatenate(out + [jnp.zeros((self.rows - at, t.shape[1]), t.dtype)], axis=0)
        return self._per_shard(one, chip, rows)

    def unpack(self, chip, buf, pad_to):
        def one(s, t):
            out = [t[start + p0 % ROW_TILE:start + p0 % ROW_TILE + n_rows] for _, n_rows, p0, start, _ in self.pieces[s]]
            return jnp.concatenate(out + [jnp.zeros((pad_to - self.shard, t.shape[1]), t.dtype)], axis=0)
        return self._per_shard(one, chip, buf)

    def to_padded(self, bufs):
        runs = sorted((p0, s, start, region) for s in range(self.n) for _, _, p0, start, region in self.pieces[s])
        chunks, end = [], 0
        for p0, s, start, region in runs:
            d0 = p0 // ROW_TILE * ROW_TILE
            src = bufs[s, start:start + region]
            if d0 < end:
                assert end - d0 == ROW_TILE
                last = chunks.pop()
                chunks += [last[:-ROW_TILE], last[-ROW_TILE:] + src[:ROW_TILE], src[ROW_TILE:]]
            else:
                if d0 > end:
                    chunks.append(jnp.zeros((d0 - end, bufs.shape[2]), bufs.dtype))
                chunks.append(src)
            end = d0 + region
        chunks.append(jnp.zeros((self.padded_rows - end, bufs.shape[2]), bufs.dtype))
        return jnp.concatenate(chunks, axis=0)

    def from_padded(self, gp):
        bufs = []
        for s in range(self.n):
            out, at = [], 0
            for _, n_rows, p0, start, region in self.pieces[s]:
                d0 = p0 // ROW_TILE * ROW_TILE
                row = d0 + lax.broadcasted_iota(jnp.int32, (region, 1), 0)
                out.append(jnp.where((row >= p0) & (row < p0 + n_rows), gp[d0:d0 + region], jnp.zeros((), gp.dtype)))
                at = start + region
            bufs.append(jnp.concatenate(out + [jnp.zeros((self.rows - at, gp.shape[1]), gp.dtype)], axis=0))
        return jnp.stack(bufs)


SMALL_ROWS = 16


def _pack_small(vals, scalar=None):
    width = max(vals[k].shape[1] for k in SMALL)
    rows = [jnp.pad(vals[k].astype(F32), ((0, 0), (0, width - vals[k].shape[1]))) for k in SMALL]
    extra = jnp.zeros((SMALL_ROWS - len(SMALL), width), F32)
    if scalar is not None:
        extra = extra.at[0, 0].set(scalar)
    return jnp.concatenate(rows + [extra], axis=0)


def _unpack_small(packed, like):
    return {k: packed[i:i + 1, :like[k].shape[1]] for i, k in enumerate(SMALL)}


def kernel(x, mem, g_mix_norm, g_mem_norm, w_in, b_forget, g_fox_q, g_fox_k, g_mem_q, g_mem_k, w_mem_kv, w_branch_sb, w_branch_fox, w_branch_mem, w_out, g_mlp_norm, w_ff_up, w_ff_down, loss_target, m_g_mix_norm, m_g_mem_norm, m_w_in, m_b_forget, m_g_fox_q, m_g_fox_k, m_g_mem_q, m_g_mem_k, m_w_mem_kv, m_w_branch_sb, m_w_branch_fox, m_w_branch_mem, m_w_out, m_g_mlp_norm, m_w_ff_up, m_w_ff_down, v_g_mix_norm, v_g_mem_norm, v_w_in, v_b_forget, v_g_fox_q, v_g_fox_k, v_g_mem_q, v_g_mem_k, v_w_mem_kv, v_w_branch_sb, v_w_branch_fox, v_w_branch_mem, v_w_out, v_g_mlp_norm, v_w_ff_up, v_w_ff_down):
    given = dict(locals())
    D = x.shape[-1]
    weights = {k: given[k] for k in ORDER}
    moms = {k: given["m_" + k] for k in ORDER}
    vars_ = {k: given["v_" + k] for k in ORDER}

    me_chip = 2 * lax.axis_index("x") + lax.axis_index("y")

    n_in = w_in.shape[2]
    lay = _InLayout(D, n_in, N_CHIPS)
    transposed = lambda t: jnp.transpose(t[0])
    shards = {}
    for k in BIG:
        w = weights[k][0].astype(BF16)
        if k == "w_in":
            w = lay.pack(me_chip, jnp.transpose(w))
        shards[k] = w.reshape(2, w.shape[0] // 2, w.shape[1])
    gathered_in = _gather_weights([shards["w_in"]], name="gather_w_in")[0]
    W = {"w_in": lay.to_padded(gathered_in.reshape(N_CHIPS, lay.rows, D))}
    carried = (("w_branch_sb", "w_branch_fox", "w_branch_mem", "w_out"), ("w_ff_up", "w_mem_kv"), ("w_ff_down",))
    rest = [k for grp in carried for k in grp]
    assert sorted(rest + ["w_in"]) == sorted(BIG)

    def finish_weights(groups, arrived):
        names = [k for gi in groups for k in carried[gi]]
        full = [lax.dynamic_update_index_in_dim(o, shards[k], me_chip, 0) for k, o in zip(names, arrived)]
        return {k: _unshard(k, g) for k, g in zip(names, full)}

    me_core = lax.axis_index("c")

    def sum_chips(names, parts, got):
        return {k: _chip_sum(p, q, me_chip, name="sum_chips_" + k) for k, p, q in zip(names, parts, got)}

    def pair_sums(grads, tag):
        names = list(grads)
        stacked = {k: _reshard(k, grads[k]) for k in names if k != "w_in"}
        if "w_in" in grads:
            stacked["w_in"] = lay.from_padded(grads["w_in"]).reshape(N_CHIPS, 2, lay.rows // 2, D)
        got = _exchange_siblings([stacked[k] for k in names], name="exchange_siblings_" + tag)
        return {k: _pair_sum(stacked[k], q, me_core, name="sum_pair_" + k) for k, q in zip(names, got)}

    def reduce_early(grads):
        names = list(grads)
        stacked = [_reshard(k, grads[k]) for k in names]

        def after_siblings(landed, more):
            parts = {k: _pair_sum(t, q, me_core, name="sum_pair_" + k) for k, t, q in zip(names, stacked, landed)}
            parts.update(pair_sums(more, "early"))
            groups = [[k for k in parts if k in ("w_ff_down", "w_ff_up")], [k for k in parts if k not in ("w_ff_down", "w_ff_up")]]
            comms = [_ChipExchange("scatter", [parts[k] for k in grp]) for grp in groups]

            def finish(*got):
                out = {}
                for grp, q in zip(groups, got):
                    out.update(sum_chips(grp, [parts[k] for k in grp], q))
                return out
            return comms, finish
        return _ChipExchange("siblings", stacked), after_siblings

    def reduce_late(grads):
        parts = pair_sums(grads, "late")
        names = list(parts)
        return _ChipExchange("scatter", [parts[k] for k in names]), functools.partial(sum_chips, names, [parts[k] for k in names])

    def share_early(halves):
        names = list(halves)

        def shared(landed):
            return {k: lax.dynamic_update_index_in_dim(o, halves[k], me_core, 0) for k, o in zip(names, landed)}
        return _ChipExchange("halves", [halves[k] for k in names]), shared

    small = {k: weights[k] for k in SMALL}
    loss_part, grad_x, G, small_grads, reduced = _local_step(
        x[0], mem[0], loss_target[0], small, W,
        gather_rest=([_ChipExchange("gather", [shards[k] for k in grp]) for grp in carried], finish_weights),
        reduce_early=reduce_early, reduce_late=reduce_late, share_early=share_early)
    assert not G, list(G)
    (reduced["w_in"],), small_parts = _share_halves([reduced["w_in"]], _pack_small(small_grads, loss_part))

    grads, deltas, new_m, new_v = {}, {}, {}, {}
    for k in BIG:
        g = reduced[k]
        shp = weights[k].shape
        if k == "w_in":
            g2 = lay.unpack(me_chip, g.reshape(lay.rows, D), lay.rows)
            padded = lambda t: jnp.pad(transposed(t), ((0, lay.rows - n_in), (0, 0)))
            outs = _adamw(padded(weights[k]), g2, padded(moms[k]), padded(vars_[k]), name="adamw_" + k)
            g2, d, m2, v2 = [jnp.transpose(t[:n_in]) for t in (g2, *outs)]
        else:
            g2 = g.reshape(shp[1], shp[2])
            d, m2, v2 = _adamw(weights[k][0], g2, moms[k][0], vars_[k][0], name="adamw_" + k)
        grads[k], deltas[k], new_m[k], new_v[k] = g2.reshape(shp), d.reshape(shp), m2.reshape(shp), v2.reshape(shp)
    sg, sd, sm, sv = _adamw_small(_pack_small(small), small_parts, _pack_small({k: moms[k] for k in SMALL}),
                                  _pack_small({k: vars_[k] for k in SMALL}), name="adamw_small")
    for dst, packed in ((grads, sg), (deltas, sd), (new_m, sm), (new_v, sv)):
        dst.update(_unpack_small(packed, small))

    loss = sg[len(SMALL), 0]
    return (loss, grad_x[None], *[grads[k] for k in ORDER], *[deltas[k] for k in ORDER],
            *[new_m[k] for k in ORDER], *[new_v[k] for k in ORDER])
```

```python
import functools

import jax
import jax.numpy as jnp
from jax import lax
from jax.experimental import pallas as pl
from jax.experimental.pallas import tpu as pltpu

F32 = jnp.float32
BF16 = jnp.bfloat16
MESH_ID = pl.DeviceIdType.MESH

HD = 64
SB_HEADS = 8
FOX_HEADS = 8
MEM_HEADS = 4
MEM_HD = 128
D_SB = SB_HEADS * HD
D_FOX = FOX_HEADS * HD
D_MEM = MEM_HEADS * MEM_HD
EPS = 1e-6
NEG_INF = -1e30

ADAM_LR = 0.001
ADAM_B1 = 0.9
ADAM_B2 = 0.999
ADAM_EPS = 1e-08
ADAM_WD = 0.01
ADAM_STEP = 10

N_CHIPS = 4
VMEM_LIMIT = 56 * 1024 * 1024

F_PAD = 256


def _tile(n, target, align=128):
    if n <= target:
        return n
    best = None
    t = align
    while t <= target:
        if n % t == 0:
            best = t
        t += align
    assert best is not None, (n, target, align)
    return best


def _params(sem):
    return pltpu.CompilerParams(dimension_semantics=sem, vmem_limit_bytes=VMEM_LIMIT)


def _mm(a, b, *, name, ta=False, tb=False, out_dtypes=(F32,), epilogue=None, extras=(),
        tm=1024, tn=1024, tk=2048, comm=None, col_sums=0, a_gain=None):
    if ta:
        K, M = a.shape
    else:
        M, K = a.shape
    if tb:
        N, K2 = b.shape
    else:
        K2, N = b.shape
    assert K == K2, (a.shape, b.shape, ta, tb)
    tm, tn, tk = _tile(M, tm), _tile(N, tn), _tile(K, tk)
    nk = K // tk
    normed = a_gain is not None
    assert not normed or (nk == 1 and not ta)
    n_extra, n_out = len(extras) + normed, len(out_dtypes) + col_sums + normed
    if epilogue is None:
        epilogue = lambda acc: (acc,)
    dims = (((0 if ta else 1,), (1 if tb else 0,)), ((), ()))

    gm, gn = M // tm, N // tn

    def body(*refs):
        i, j, k = pl.program_id(0), pl.program_id(1), pl.program_id(2)
        got = _carry(comm, 2 + n_extra, n_out, (i == 0) & (j == 0) & (k == 0),
                     (i == gm - 1) & (j == gn - 1) & (k == nk - 1), refs)
        (a_ref, b_ref, *extra_refs), out_refs = got[0], list(got[1])
        if normed:
            gain_ref, normed_ref = extra_refs.pop(), out_refs.pop()

            @pl.when(j == 0)
            def _():
                xa = a_ref[...].astype(F32)
                normed_ref[...] = (xa * lax.rsqrt(jnp.mean(xa * xa, axis=-1, keepdims=True) + EPS) * gain_ref[...]).astype(BF16)

            av = normed_ref[...]
        else:
            av = a_ref[...].astype(BF16)
        part = lax.dot_general(av, b_ref[...].astype(BF16), dims, preferred_element_type=F32)

        def finish(acc):
            outs = epilogue(acc, *[r[...] for r in extra_refs])
            for o_ref, o in zip(out_refs[:len(out_dtypes)], outs):
                o_ref[...] = o.astype(o_ref.dtype)
            for o_ref, o in zip(out_refs[len(out_dtypes):], outs[len(out_dtypes):]):
                first = lax.broadcasted_iota(jnp.int32, o_ref.shape, 0) == 0
                o_ref[...] = jnp.where(first, jnp.broadcast_to(o, o_ref.shape), 0.0)

        if nk == 1:
            finish(part)
        else:
            acc_ref = refs[-1]

            @pl.when(k == 0)
            def _():
                acc_ref[...] = part

            @pl.when((k > 0) & (k < nk - 1))
            def _():
                acc_ref[...] += part

            @pl.when(k == nk - 1)
            def _():
                finish(acc_ref[...] + part)

        got[2]()

    a_spec = pl.BlockSpec((tk, tm), lambda i, j, k: (k, i)) if ta else pl.BlockSpec((tm, tk), lambda i, j, k: (i, k))
    b_spec = pl.BlockSpec((tn, tk), lambda i, j, k: (j, k)) if tb else pl.BlockSpec((tk, tn), lambda i, j, k: (k, j))
    mn_spec = pl.BlockSpec((tm, tn), lambda i, j, k: (i, j))
    row_spec = pl.BlockSpec((1, tn), lambda i, j, k: (0, j))
    sum_spec = pl.BlockSpec((8, tn), lambda i, j, k: (i, j))
    c_ins, c_in_specs, c_out_specs, c_out_shape, c_scratch = _comm_args(comm)
    sem = ("parallel", "arbitrary" if normed else "parallel", "arbitrary") if comm is None else ("arbitrary",) * 3
    outs = pl.pallas_call(
        body, name=name,
        grid=(gm, gn, nk),
        in_specs=[a_spec, b_spec] + [row_spec if e.shape[0] == 1 else mn_spec for e in extras]
        + ([pl.BlockSpec((1, K), lambda i, j, k: (0, 0))] if normed else []) + c_in_specs,
        out_specs=[mn_spec] * len(out_dtypes) + [sum_spec] * col_sums
        + ([pl.BlockSpec((tm, K), lambda i, j, k: (i, 0))] if normed else []) + c_out_specs,
        out_shape=[jax.ShapeDtypeStruct((M, N), dt) for dt in out_dtypes]
        + [jax.ShapeDtypeStruct((8 * gm, N), F32)] * col_sums
        + ([jax.ShapeDtypeStruct((M, K), BF16)] if normed else []) + c_out_shape,
        scratch_shapes=c_scratch + ([pltpu.VMEM((tm, tn), F32)] if nk > 1 else []),
        compiler_params=_params(sem),
    )(a, b, *extras, *([a_gain] if normed else []), *c_ins)
    return outs if len(outs) > 1 else outs[0]


def _row_tile(rows, cols, n_arrays):
    budget = 28 * 1024 * 1024
    cols_padded = -(-cols // 128) * 128
    target = max(16, budget // (cols_padded * 4 * n_arrays * 2))
    return _tile(rows, target, align=16)


def _ew(fn, ins, out_dtypes, *, name):
    R, C = ins[0].shape
    n_in, n_out = len(ins), len(out_dtypes)
    tr = _row_tile(R, C, n_in + n_out)

    def body(*refs):
        outs = fn(*[r[...] for r in refs[:n_in]])
        for o_ref, o in zip(refs[n_in:], outs):
            o_ref[...] = o.astype(o_ref.dtype)

    spec = pl.BlockSpec((tr, C), lambda i: (i, 0))
    outs = pl.pallas_call(
        body, name=name, grid=(R // tr,),
        in_specs=[spec] * n_in, out_specs=[spec] * n_out,
        out_shape=[jax.ShapeDtypeStruct((R, C), dt) for dt in out_dtypes],
        compiler_params=_params(("parallel",)),
    )(*ins)
    return outs if n_out > 1 else outs[0]


def _rmsnorm_fwd(x, g, out_dtype, *, name):
    R, d = x.shape
    tr = _row_tile(R, d, 3)

    def body(x_ref, g_ref, o_ref):
        xv = x_ref[...].astype(F32)
        r = lax.rsqrt(jnp.mean(xv * xv, axis=-1, keepdims=True) + EPS)
        o_ref[...] = (xv * r * g_ref[...]).astype(o_ref.dtype)

    return pl.pallas_call(
        body, name=name, grid=(R // tr,),
        in_specs=[pl.BlockSpec((tr, d), lambda i: (i, 0)), pl.BlockSpec((1, d), lambda i: (0, 0))],
        out_specs=pl.BlockSpec((tr, d), lambda i: (i, 0)),
        out_shape=jax.ShapeDtypeStruct((R, d), out_dtype),
        compiler_params=_params(("parallel",)),
    )(x, g)


def _rmsnorm_bwd(x, g, dy, add=None, *, name):
    R, d = x.shape
    has_add = add is not None
    tr = _row_tile(R, d, 5)

    def body(*refs):
        x_ref, g_ref, dy_ref = refs[:3]
        add_ref = refs[3] if has_add else None
        dx_ref, dg_ref = refs[-2:]
        xv = x_ref[...].astype(F32)
        dyv = dy_ref[...].astype(F32)
        r = lax.rsqrt(jnp.mean(xv * xv, axis=-1, keepdims=True) + EPS)
        xh = xv * r
        dyg = dyv * g_ref[...]
        c = jnp.mean(dyg * xh, axis=-1, keepdims=True)
        dx = r * (dyg - xh * c)
        if has_add:
            dx = dx + add_ref[...]
        dx_ref[...] = dx

        @pl.when(pl.program_id(0) == 0)
        def _():
            dg_ref[...] = jnp.zeros_like(dg_ref)

        dg_ref[...] += jnp.sum(dyv * xh, axis=0, keepdims=True)

    row = pl.BlockSpec((tr, d), lambda i: (i, 0))
    vec = pl.BlockSpec((1, d), lambda i: (0, 0))
    ins = [x, g, dy] + ([add] if has_add else [])
    return pl.pallas_call(
        body, name=name, grid=(R // tr,),
        in_specs=[row, vec, row] + ([row] if has_add else []),
        out_specs=[row, vec],
        out_shape=[jax.ShapeDtypeStruct((R, d), F32), jax.ShapeDtypeStruct((1, d), F32)],
        compiler_params=_params(("arbitrary",)),
    )(*ins)


_NT = (((1,), (1,)), ((), ()))
_TN = (((0,), (0,)), ((), ()))


def _dot(a, b, dims=(((1,), (0,)), ((), ()))):
    return lax.dot_general(a, b, dims, preferred_element_type=F32)


def _log_sigmoid_pair(z):
    sp = jnp.log(1.0 + jnp.exp(-jnp.abs(z)))
    return jnp.minimum(z, 0.0) - sp, jnp.minimum(-z, 0.0) - sp


LANES = 128
_LOW = -3e38


def _lane_masks(hd, rows):
    if hd == LANES:
        return [None]
    lane = lax.broadcasted_iota(jnp.int32, (rows, LANES), 1)
    return [(lane >= hh * hd) & (lane < (hh + 1) * hd) for hh in range(LANES // hd)]


def _keep(t, m):
    return t if m is None else jnp.where(m, t, 0.0)


def _merge(parts, masks):
    out = parts[-1]
    for p, m in zip(parts[-2::-1], masks[-2::-1]):
        out = jnp.where(m, p, out)
    return out


def _row_value(t, m):
    return jnp.max(t if m is None else jnp.where(m, t, _LOW), axis=1, keepdims=True)


def _cols(tq, off):
    return pl.BlockSpec((tq, LANES), lambda g, i: (i, off + g))


def _cols_all(rows, off):
    return pl.BlockSpec((rows, LANES), lambda g, i: (0, off + g))


SCAN_BLOCK = 256


def _tri(kind, cols):
    n = min(SCAN_BLOCK, cols)
    r = lax.broadcasted_iota(jnp.int32, (n, n), 0)
    c = lax.broadcasted_iota(jnp.int32, (n, n), 1)
    return ((r > c) if kind == "after" else (r < c)).astype(BF16)


def _scan_cols(x, tri, reverse):
    cols = x.shape[1]
    cb = min(SCAN_BLOCK, cols)
    assert cols % cb == 0 and tri.shape == (cb, cb)
    nb = cols // cb
    blocks = [x[:, b * cb:(b + 1) * cb] for b in range(nb)]
    outs, carry = [None] * nb, None
    for b in (reversed(range(nb)) if reverse else range(nb)):
        y = _dot(blocks[b].astype(BF16), tri)
        outs[b] = y if carry is None else y + carry
        s = jnp.sum(blocks[b], axis=1, keepdims=True)
        carry = s if carry is None else carry + s
    return (outs[0] if nb == 1 else jnp.concatenate(outs, axis=1)), carry


def _softplus_parts(z):
    pos = jnp.maximum(z, 0.0) + jnp.log(1.0 + jnp.exp(-jnp.abs(z)))
    return pos, z - pos


class _ChipExchange:
    def __init__(self, kind, ins):
        assert kind in ("gather", "scatter", "siblings", "forward", "halves")
        self.kind, self.ins = kind, list(ins)
        lead = {"gather": lambda s: (N_CHIPS,) + s, "scatter": lambda s: (3,) + s[1:], "siblings": lambda s: s[:1] + s[2:],
                "forward": lambda s: s, "halves": lambda s: (2,) + s}[kind]
        self.out_shape = [jax.ShapeDtypeStruct(lead(a.shape), a.dtype) for a in ins]
        self.aliases = [(w, w) for w in range(len(ins))] if kind == "forward" else []
        n = {"siblings": N_CHIPS, "halves": 1}.get(kind, 3) * len(ins)
        self.scratch = [pltpu.SemaphoreType.DMA((n,)), pltpu.SemaphoreType.DMA((n,))]

    def _copies(self, in_refs, out_refs, sems, landing):
        x, y, c, chips = _mesh_place()
        me = 2 * x + y
        out = []
        for w in range(len(self.ins)):
            if self.kind == "siblings":
                for s in range(N_CHIPS):
                    land = out_refs[w].at[s]
                    src = land if landing else in_refs[w].at[s, 1 - c]
                    out.append(_remote(src, land, sems, N_CHIPS * w + s, (x, y, 1 - c)))
                continue
            if self.kind == "halves":
                land = out_refs[w].at[1 - c]
                src, dst = (land, land) if landing else (in_refs[w], out_refs[w].at[c])
                out.append(_remote(src, dst, sems, w, (x, y, 1 - c)))
                continue
            for j, chip in enumerate(chips):
                peer = 2 * chip[0] + chip[1]
                to = (chip[0], chip[1], c)
                if self.kind == "gather":
                    src, dst, land = in_refs[w].at[c], out_refs[w].at[me, c], out_refs[w].at[peer, c]
                elif self.kind == "scatter":
                    src, dst, land = in_refs[w].at[peer], out_refs[w].at[j], out_refs[w].at[j]
                else:
                    src, dst, land, to = in_refs[w].at[peer, c], out_refs[w].at[peer, c], out_refs[w].at[peer, 1 - c], (x, y, 1 - c)
                if landing:
                    src, dst = land, land
                out.append(_remote(src, dst, sems, 3 * w + j, to))
        return out

    def start(self, in_refs, out_refs, sems):
        for cp in self._copies(in_refs, out_refs, sems, False):
            cp.start()

    def finish(self, in_refs, out_refs, sems):
        for cp in self._copies(in_refs, out_refs, sems, True):
            cp.wait_recv()
        for cp in self._copies(in_refs, out_refs, sems, False):
            cp.wait_send()


class _Joint:
    def __init__(self, parts):
        self.parts = list(parts)
        self.ins = [a for p in self.parts for a in p.ins]
        self.out_shape = [s for p in self.parts for s in p.out_shape]
        self.scratch = [s for p in self.parts for s in p.scratch]
        self.aliases, n_in, n_out = [], 0, 0
        for p in self.parts:
            self.aliases += [(n_in + i, n_out + o) for i, o in p.aliases]
            n_in, n_out = n_in + len(p.ins), n_out + len(p.out_shape)

    def _each(self, step, in_refs, out_refs, sems):
        i = o = s = 0
        for p in self.parts:
            a, b, c = len(p.ins), len(p.out_shape), len(p.scratch)
            getattr(p, step)(in_refs[i:i + a], out_refs[o:o + b], sems[s:s + c])
            i, o, s = i + a, o + b, s + c

    def start(self, in_refs, out_refs, sems):
        self._each("start", in_refs, out_refs, sems)

    def finish(self, in_refs, out_refs, sems):
        self._each("finish", in_refs, out_refs, sems)


def _carry(comm, n_in, n_out, first, last, refs):
    if comm is None:
        return refs[:n_in], refs[n_in:n_in + n_out], (lambda: None)
    a, b = len(comm.ins), len(comm.out_shape)
    ins, c_in = refs[:n_in], refs[n_in:n_in + a]
    outs, c_out = refs[n_in + a:n_in + a + n_out], refs[n_in + a + n_out:n_in + a + n_out + b]
    sems = refs[n_in + a + n_out + b:n_in + a + n_out + b + len(comm.scratch)]
    pl.when(first)(lambda: comm.start(c_in, c_out, sems))
    return ins, outs, (lambda: pl.when(last)(lambda: comm.finish(c_in, c_out, sems)))


def _sbl_fwd(q, k, v, *, width, hd, name, tq=256, comm=None):
    (qa, qo), (ka, ko), (va, vo) = q, k, v
    S = qa.shape[0]
    tq = _tile(S, tq)
    tk = tq
    scale = hd ** -0.5
    n_g, n_q = width // LANES, S // tq

    def body(*refs):
        qi = pl.program_id(1)
        gi = pl.program_id(0)
        got = _carry(comm, 3, 2, (gi == 0) & (qi == 0), (gi == n_g - 1) & (qi == n_q - 1), refs)
        (q_ref, k_ref, v_ref), (o_ref, tot_ref) = got[0], got[1]
        masks = _lane_masks(hd, tq)
        qs = q_ref[...].astype(F32) * scale
        qm = [_keep(qs, m).astype(BF16) for m in masks]
        strict = lax.broadcasted_iota(jnp.int32, (tq, tk), 1) < lax.broadcasted_iota(jnp.int32, (tq, tk), 0)
        later = _tri("after", tk)

        def tile(kb, carry, diag):
            ks = pl.multiple_of(kb * tk, tk)
            kv = k_ref[pl.ds(ks, tk), :].astype(BF16)
            vv = v_ref[pl.ds(ks, tk), :].astype(BF16)
            out = []
            for hh in range(len(masks)):
                acc, c_pos = carry[2 * hh], carry[2 * hh + 1]
                pos, ls = _softplus_parts(_dot(qm[hh], kv, _NT))
                if diag:
                    pos = jnp.where(strict, pos, 0.0)
                pos_after, pos_all = _scan_cols(pos, later, True)
                w = jnp.exp(ls - (pos_after + c_pos))
                if diag:
                    w = jnp.where(strict, w, 0.0)
                out += [acc + _dot(w.astype(BF16), vv), c_pos + pos_all]
            return tuple(out)

        init = (jnp.zeros((tq, LANES), F32), jnp.zeros((tq, 1), F32)) * len(masks)
        carry = tile(qi, init, True)
        carry = lax.fori_loop(0, qi, lambda i, c: tile(qi - 1 - i, c, False), carry)
        o_ref[...] = _merge(carry[0::2], masks).astype(o_ref.dtype)
        tot_ref[...] = _merge([jnp.broadcast_to(-c, (tq, LANES)) for c in carry[1::2]], masks)
        got[2]()

    c_ins, c_in_specs, c_out_specs, c_out_shape, c_scratch = _comm_args(comm)
    return pl.pallas_call(
        body, name=name, grid=(n_g, n_q),
        in_specs=[_cols(tq, qo), _cols_all(S, ko), _cols_all(S, vo)] + c_in_specs,
        out_specs=[_cols(tq, 0), _cols(tq, 0)] + c_out_specs,
        out_shape=[jax.ShapeDtypeStruct((S, width), BF16), jax.ShapeDtypeStruct((S, width), F32)] + c_out_shape,
        scratch_shapes=c_scratch,
        compiler_params=_params(("arbitrary", "arbitrary")),
    )(qa, ka, va, *c_ins)


def _comm_args(comm, aliased=False):
    if comm is None:
        return [], [], [], [], []
    assert aliased or not comm.aliases
    return comm.ins, [_ANY] * len(comm.ins), [_ANY] * len(comm.out_shape), comm.out_shape, comm.scratch


def _comm_aliases(comm, n_in, n_out):
    return {} if comm is None else {n_in + i: n_out + o for i, o in comm.aliases}


def _sbl_bwd(q, k, v, do, tot, *, width, hd, name, tq=256, comm=None):
    (qa, qo), (ka, ko), (va, vo) = q, k, v
    S = qa.shape[0]
    tq = _tile(S, tq)
    tk = tq
    scale = hd ** -0.5
    n_g, n_q = width // LANES, S // tq

    def body(*refs):
        qi = pl.program_id(1)
        gi = pl.program_id(0)
        got = _carry(comm, 5, 3, (gi == 0) & (qi == 0), (gi == n_g - 1) & (qi == n_q - 1), refs)
        (q_ref, k_ref, v_ref, do_ref, tot_ref), (dq_ref, dk_ref, dv_ref) = got[0], got[1]

        @pl.when(qi == 0)
        def _():
            dk_ref[...] = jnp.zeros_like(dk_ref)
            dv_ref[...] = jnp.zeros_like(dv_ref)

        masks = _lane_masks(hd, tq)
        qs = q_ref[...].astype(F32) * scale
        qm = [_keep(qs, m).astype(BF16) for m in masks]
        dov = [_keep(do_ref[...], m).astype(BF16) for m in masks]
        rest = [-_row_value(tot_ref[...], m) for m in masks]
        strict = lax.broadcasted_iota(jnp.int32, (tq, tk), 1) < lax.broadcasted_iota(jnp.int32, (tq, tk), 0)
        later, before = _tri("after", tk), _tri("before", tk)

        def tile(kb, carry, diag):
            ks = pl.multiple_of(kb * tk, tk)
            kv = k_ref[pl.ds(ks, tk), :].astype(BF16)
            vv = v_ref[pl.ds(ks, tk), :].astype(BF16)
            out = []
            dk_t, dv_t = None, None
            for hh in range(len(masks)):
                dq, c_pos, c_g = carry[3 * hh:3 * hh + 3]
                pos, ls = _softplus_parts(_dot(qm[hh], kv, _NT))
                if diag:
                    pos = jnp.where(strict, pos, 0.0)
                pos_after, pos_all = _scan_cols(pos, later, True)
                c_pos = c_pos + pos_all
                w = jnp.exp(ls - (pos_after + (rest[hh] - c_pos)))
                if diag:
                    w = jnp.where(strict, w, 0.0)
                g = _dot(dov[hh], vv, _NT) * w
                g_before, g_all = _scan_cols(g, before, False)
                g_before = g_before + c_g
                dz = g - jnp.exp(ls) * (g + g_before)
                if diag:
                    dz = jnp.where(strict, dz, 0.0)
                dzb = dz.astype(BF16)
                dk_h = _dot(dzb, qm[hh], _TN)
                dv_h = _dot(w.astype(BF16), dov[hh], _TN)
                dk_t = dk_h if dk_t is None else dk_t + dk_h
                dv_t = dv_h if dv_t is None else dv_t + dv_h
                out += [dq + _dot(dzb, kv), c_pos, c_g + g_all]
            dk_ref[pl.ds(ks, tk), :] += dk_t
            dv_ref[pl.ds(ks, tk), :] += dv_t
            return tuple(out)

        zero = jnp.zeros((tq, 1), F32)
        init = (jnp.zeros((tq, LANES), F32), zero, zero) * len(masks)
        carry = lax.fori_loop(0, qi, lambda kb, c: tile(kb, c, False), init)
        carry = tile(qi, carry, True)
        dq_ref[...] = _merge(carry[0::3], masks) * scale
        got[2]()

    full = jax.ShapeDtypeStruct((S, width), F32)
    c_ins, c_in_specs, c_out_specs, c_out_shape, c_scratch = _comm_args(comm)
    return pl.pallas_call(
        body, name=name, grid=(n_g, n_q),
        in_specs=[_cols(tq, qo), _cols_all(S, ko), _cols_all(S, vo), _cols(tq, do[1]), _cols(tq, tot[1])] + c_in_specs,
        out_specs=[_cols(tq, 0), _cols_all(S, 0), _cols_all(S, 0)] + c_out_specs,
        out_shape=[full, full, full] + c_out_shape,
        scratch_shapes=c_scratch,
        compiler_params=_params(("arbitrary", "arbitrary")),
    )(qa, ka, va, do[0], tot[0], *c_ins)


def _sml_fwd(q, k, v, bias=None, *, width, hd, causal, name, tq=256, tk=256, comm=None):
    (qa, qo), (ka, ko), (va, vo) = q, k, v
    S, Sk = qa.shape[0], ka.shape[0]
    tq, tk = _tile(S, tq), _tile(Sk, tk)
    if causal:
        assert tq == tk and S == Sk
    nk = Sk // tk
    hpg = LANES // hd
    scale = hd ** -0.5
    has_bias = bias is not None
    n_g, n_q = width // LANES, S // tq

    def body(*all_refs):
        qi, gi = pl.program_id(1), pl.program_id(0)
        got = _carry(comm, 5 if has_bias else 3, 2, (gi == 0) & (qi == 0), (gi == n_g - 1) & (qi == n_q - 1), all_refs)
        refs = tuple(got[0]) + tuple(got[1])
        q_ref, k_ref, v_ref = refs[:3]
        o_ref, lse_ref = refs[-2:]
        masks = _lane_masks(hd, tq)
        qs = q_ref[...].astype(F32) * scale
        qm = [_keep(qs, m).astype(BF16) for m in masks]
        allowed = lax.broadcasted_iota(jnp.int32, (tq, tk), 1) <= lax.broadcasted_iota(jnp.int32, (tq, tk), 0)

        def tile(kb, carry, diag):
            ks = pl.multiple_of(kb * tk, tk)
            kv = k_ref[pl.ds(ks, tk), :].astype(BF16)
            vv = v_ref[pl.ds(ks, tk), :].astype(BF16)
            out = []
            for hh in range(hpg):
                m, l, acc = carry[3 * hh:3 * hh + 3]
                z = _dot(qm[hh], kv, _NT)
                if has_bias:
                    z = z + refs[3][hh] - refs[4][hh, kb]
                if diag:
                    z = jnp.where(allowed, z, NEG_INF)
                m2 = jnp.maximum(m, jnp.max(z, axis=1, keepdims=True))
                p = jnp.exp(z - m2)
                alpha = jnp.exp(m - m2)
                out += [m2, alpha * l + jnp.sum(p, axis=1, keepdims=True), alpha * acc + _dot(p.astype(BF16), vv)]
            return tuple(out)

        init = (jnp.full((tq, 1), NEG_INF, F32), jnp.zeros((tq, 1), F32), jnp.zeros((tq, LANES), F32)) * hpg
        if causal:
            carry = lax.fori_loop(0, qi, lambda kb, c: tile(kb, c, False), init)
            carry = tile(qi, carry, True)
        else:
            carry = lax.fori_loop(0, nk, lambda kb, c: tile(kb, c, False), init)
        o_ref[...] = _merge([acc / l for l, acc in zip(carry[1::3], carry[2::3])], masks).astype(o_ref.dtype)
        lse_ref[...] = _merge([jnp.broadcast_to(m + jnp.log(l), (tq, LANES)) for m, l in zip(carry[0::3], carry[1::3])], masks)
        got[2]()

    in_specs = [_cols(tq, qo), _cols_all(Sk, ko), _cols_all(Sk, vo)]
    ins = [qa, ka, va]
    if has_bias:
        in_specs += [pl.BlockSpec((hpg, tq, 1), lambda g, i: (g, i, 0)),
                     pl.BlockSpec((hpg, nk, 1, tk), lambda g, i: (g, 0, 0, 0))]
        ins += list(bias)
    c_ins, c_in_specs, c_out_specs, c_out_shape, c_scratch = _comm_args(comm, aliased=True)
    return pl.pallas_call(
        body, name=name, grid=(n_g, n_q),
        in_specs=in_specs + c_in_specs, out_specs=[_cols(tq, 0), _cols(tq, 0)] + c_out_specs,
        out_shape=[jax.ShapeDtypeStruct((S, width), BF16), jax.ShapeDtypeStruct((S, width), F32)] + c_out_shape,
        input_output_aliases=_comm_aliases(comm, len(ins), 2),
        scratch_shapes=c_scratch,
        compiler_params=_params(("arbitrary", "arbitrary") if comm is not None else ("parallel", "arbitrary")),
    )(*ins, *c_ins)


def _sml_bwd(q, k, v, o, lse, do, bias=None, *, width, hd, causal, name, tq=256, tk=256, comm=None):
    (qa, qo), (ka, ko), (va, vo) = q, k, v
    S, Sk = qa.shape[0], ka.shape[0]
    tq, tk = _tile(S, tq), _tile(Sk, tk)
    nk = Sk // tk
    hpg = LANES // hd
    scale = hd ** -0.5
    has_bias = bias is not None
    n_in = 8 if has_bias else 6
    n_g, n_q = width // LANES, S // tq

    def body(*all_refs):
        qi, gi = pl.program_id(1), pl.program_id(0)
        got = _carry(comm, n_in, 5 if has_bias else 3, (gi == 0) & (qi == 0), (gi == n_g - 1) & (qi == n_q - 1), all_refs)
        refs = tuple(got[0]) + tuple(got[1])
        q_ref, k_ref, v_ref, o_ref, lse_ref, do_ref = refs[:6]
        dq_ref, dk_ref, dv_ref = refs[n_in:n_in + 3]

        @pl.when(qi == 0)
        def _():
            dk_ref[...] = jnp.zeros_like(dk_ref)
            dv_ref[...] = jnp.zeros_like(dv_ref)
            if has_bias:
                refs[n_in + 4][...] = jnp.zeros_like(refs[n_in + 4])

        masks = _lane_masks(hd, tq)
        qs = q_ref[...].astype(F32) * scale
        qm = [_keep(qs, m).astype(BF16) for m in masks]
        do32 = do_ref[...]
        dov = [_keep(do32, m).astype(BF16) for m in masks]
        prod = do32 * o_ref[...].astype(F32)
        delta = [jnp.sum(_keep(prod, m), axis=1, keepdims=True) for m in masks]
        lses = [_row_value(lse_ref[...], m) for m in masks]
        allowed = lax.broadcasted_iota(jnp.int32, (tq, tk), 1) <= lax.broadcasted_iota(jnp.int32, (tq, tk), 0)

        def tile(kb, carry, diag):
            ks = pl.multiple_of(kb * tk, tk)
            kv = k_ref[pl.ds(ks, tk), :].astype(BF16)
            vv = v_ref[pl.ds(ks, tk), :].astype(BF16)
            out = []
            dk_t, dv_t = None, None
            for hh in range(hpg):
                dq, db_row = carry[2 * hh:2 * hh + 2]
                z = _dot(qm[hh], kv, _NT)
                if has_bias:
                    z = z + refs[6][hh] - refs[7][hh, kb]
                p = jnp.exp(z - lses[hh])
                if diag:
                    p = jnp.where(allowed, p, 0.0)
                dz = p * (_dot(dov[hh], vv, _NT) - delta[hh])
                dzb = dz.astype(BF16)
                dk_h = _dot(dzb, qm[hh], _TN)
                dv_h = _dot(p.astype(BF16), dov[hh], _TN)
                dk_t = dk_h if dk_t is None else dk_t + dk_h
                dv_t = dv_h if dv_t is None else dv_t + dv_h
                if has_bias:
                    db_row = db_row + jnp.sum(dz, axis=1, keepdims=True)
                    refs[n_in + 4][hh, kb] += jnp.sum(dz, axis=0, keepdims=True)
                out += [dq + _dot(dzb, kv), db_row]
            dk_ref[pl.ds(ks, tk), :] += dk_t
            dv_ref[pl.ds(ks, tk), :] += dv_t
            return tuple(out)

        init = (jnp.zeros((tq, LANES), F32), jnp.zeros((tq, 1), F32)) * hpg
        if causal:
            carry = lax.fori_loop(0, qi, lambda kb, c: tile(kb, c, False), init)
            carry = tile(qi, carry, True)
        else:
            carry = lax.fori_loop(0, nk, lambda kb, c: tile(kb, c, False), init)
        dq_ref[...] = _merge(carry[0::2], masks) * scale
        if has_bias:
            for hh in range(hpg):
                refs[n_in + 3][hh] = carry[2 * hh + 1]
        got[2]()

    in_specs = [_cols(tq, qo), _cols_all(Sk, ko), _cols_all(Sk, vo), _cols(tq, o[1]), _cols(tq, lse[1]), _cols(tq, do[1])]
    ins = [qa, ka, va, o[0], lse[0], do[0]]
    out_specs = [_cols(tq, 0), _cols_all(Sk, 0), _cols_all(Sk, 0)]
    out_shape = [jax.ShapeDtypeStruct((S, width), F32), jax.ShapeDtypeStruct((Sk, width), F32),
                 jax.ShapeDtypeStruct((Sk, width), F32)]
    if has_bias:
        rspec = pl.BlockSpec((hpg, tq, 1), lambda g, i: (g, i, 0))
        cspec = pl.BlockSpec((hpg, nk, 1, tk), lambda g, i: (g, 0, 0, 0))
        in_specs += [rspec, cspec]
        ins += list(bias)
        out_specs += [rspec, cspec]
        n_heads = width // hd
        out_shape += [jax.ShapeDtypeStruct((n_heads, S, 1), F32), jax.ShapeDtypeStruct((n_heads, nk, 1, tk), F32)]
    c_ins, c_in_specs, c_out_specs, c_out_shape, c_scratch = _comm_args(comm)
    return pl.pallas_call(
        body, name=name, grid=(n_g, n_q),
        in_specs=in_specs + c_in_specs, out_specs=out_specs + c_out_specs, out_shape=out_shape + c_out_shape,
        scratch_shapes=c_scratch,
        compiler_params=_params(("arbitrary", "arbitrary") if comm is not None else ("parallel", "arbitrary")),
    )(*ins, *c_ins)


def _head_sums(t, masks):
    sums = [jnp.sum(_keep(t, m), axis=1, keepdims=True) for m in masks]
    return _merge([jnp.broadcast_to(s, t.shape) for s in sums], masks) if len(masks) > 1 else sums[0]


def _hnorm_fwd(x, g_lanes, *, width, hd, name, tr=1024):
    xa, xo = x
    R = xa.shape[0]
    tr = _tile(R, tr, align=16)
    n_blk = width // LANES

    def body(x_ref, g_ref, o_ref):
        masks = _lane_masks(hd, tr)
        for j in range(n_blk):
            sl = slice(j * LANES, (j + 1) * LANES)
            xv = x_ref[:, sl].astype(F32)
            r = lax.rsqrt(_head_sums(xv * xv, masks) * (1.0 / hd) + EPS)
            o_ref[:, sl] = (xv * r * g_ref[...]).astype(o_ref.dtype)

    assert (xo * LANES) % width == 0
    return pl.pallas_call(
        body, name=name, grid=(R // tr,),
        in_specs=[pl.BlockSpec((tr, width), lambda i: (i, xo * LANES // width)), pl.BlockSpec((1, LANES), lambda i: (0, 0))],
        out_specs=pl.BlockSpec((tr, width), lambda i: (i, 0)),
        out_shape=jax.ShapeDtypeStruct((R, width), BF16),
        compiler_params=_params(("parallel",)),
    )(xa, g_lanes)


def _hnorm_bwd(x, g_lanes, dy, *, width, hd, name, tr=1024):
    xa, xo = x
    R = xa.shape[0]
    tr = _tile(R, tr, align=16)
    n_blk = width // LANES

    def body(x_ref, g_ref, dy_ref, dx_ref, dg_ref):
        masks = _lane_masks(hd, tr)
        dg = jnp.zeros((1, LANES), F32)
        for j in range(n_blk):
            sl = slice(j * LANES, (j + 1) * LANES)
            xv = x_ref[:, sl].astype(F32)
            dyv = dy_ref[:, sl].astype(F32)
            r = lax.rsqrt(_head_sums(xv * xv, masks) * (1.0 / hd) + EPS)
            xh = xv * r
            dyg = dyv * g_ref[...]
            c = _head_sums(dyg * xh, masks) * (1.0 / hd)
            dx_ref[:, sl] = (r * (dyg - xh * c)).astype(dx_ref.dtype)
            dg = dg + jnp.sum(dyv * xh, axis=0, keepdims=True)
        if hd * 2 == LANES:
            dg8 = jnp.broadcast_to(dg, (8, LANES))
            dg = (dg8 + pltpu.roll(dg8, shift=hd, axis=1))[0:1]
        else:
            assert hd == LANES

        @pl.when(pl.program_id(0) == 0)
        def _():
            dg_ref[...] = jnp.zeros_like(dg_ref)

        dg_ref[...] += dg

    assert (xo * LANES) % width == 0
    return pl.pallas_call(
        body, name=name, grid=(R // tr,),
        in_specs=[pl.BlockSpec((tr, width), lambda i: (i, xo * LANES // width)), pl.BlockSpec((1, LANES), lambda i: (0, 0)),
                  pl.BlockSpec((tr, width), lambda i: (i, 0))],
        out_specs=[pl.BlockSpec((tr, width), lambda i: (i, 0)), pl.BlockSpec((1, LANES), lambda i: (0, 0))],
        out_shape=[jax.ShapeDtypeStruct((R, width), BF16), jax.ShapeDtypeStruct((1, LANES), F32)],
        compiler_params=_params(("arbitrary",)),
    )(xa, g_lanes, dy)


def _split3_dot(x, tri):
    a = x.astype(BF16)
    r = x - a.astype(F32)
    b = r.astype(BF16)
    c = (r - b.astype(F32)).astype(BF16)
    return _dot(a, tri) + _dot(b, tri) + _dot(c, tri)


def _forget_fwd(logit_t, b_col, *, name, blk=512):
    H, S = logit_t.shape
    blk = _tile(S, blk)

    def body(l_ref, b_ref, f_ref):
        r_i = lax.broadcasted_iota(jnp.int32, (blk, blk), 0)
        c_i = lax.broadcasted_iota(jnp.int32, (blk, blk), 1)
        upto = (r_i <= c_i).astype(BF16)
        carry = jnp.zeros((H, 1), F32)
        for j in range(S // blk):
            u = l_ref[:, j * blk:(j + 1) * blk] + b_ref[...]
            lf, _ = _log_sigmoid_pair(u)
            f_ref[:, j * blk:(j + 1) * blk] = _split3_dot(lf, upto) + carry
            carry = carry + jnp.sum(lf, axis=1, keepdims=True)

    return pl.pallas_call(
        body, name=name,
        out_shape=jax.ShapeDtypeStruct((H, S), F32),
        compiler_params=pltpu.CompilerParams(vmem_limit_bytes=VMEM_LIMIT),
    )(logit_t, b_col)


def _forget_bwd(logit_t, b_col, d_f, *, name, blk=512):
    H, S = logit_t.shape
    blk = _tile(S, blk)

    def body(l_ref, b_ref, df_ref, dl_ref, db_ref):
        r_i = lax.broadcasted_iota(jnp.int32, (blk, blk), 0)
        c_i = lax.broadcasted_iota(jnp.int32, (blk, blk), 1)
        fromon = (r_i >= c_i).astype(BF16)
        carry = jnp.zeros((H, 1), F32)
        db = jnp.zeros((H, 1), F32)
        for j in reversed(range(S // blk)):
            sl = slice(j * blk, (j + 1) * blk)
            dfv = df_ref[:, sl]
            d_lf = _split3_dot(dfv, fromon) + carry
            carry = carry + jnp.sum(dfv, axis=1, keepdims=True)
            u = l_ref[:, sl] + b_ref[...]
            _, lsn = _log_sigmoid_pair(u)
            dl = d_lf * jnp.exp(lsn)
            dl_ref[:, sl] = dl
            db = db + jnp.sum(dl, axis=1, keepdims=True)
        db_ref[...] = db

    return pl.pallas_call(
        body, name=name,
        out_shape=[jax.ShapeDtypeStruct((H, S), F32), jax.ShapeDtypeStruct((H, 1), F32)],
        compiler_params=pltpu.CompilerParams(vmem_limit_bytes=VMEM_LIMIT),
    )(logit_t, b_col, d_f)


def _sigmoid(t):
    return 1.0 / (1.0 + jnp.exp(-t))


def _gate_fwd(o3, w3, proj, x, w_out, g_norm, D, *, name, tm=512):
    S = proj.shape[0]
    tm = _tile(S, tm)

    def body(o0, o1, o2, w0, w1, w2, g0, g1, g2, x_ref, wo_ref, gn_ref, merged_ref, x1_ref, h2_ref):
        acc = None
        for o_ref, w_ref, g_ref in ((o0, w0, g0), (o1, w1, g1), (o2, w2, g2)):
            t = _sigmoid(g_ref[...]) * _dot(o_ref[...], w_ref[...])
            acc = t if acc is None else acc + t
        merged = acc.astype(BF16)
        merged_ref[...] = merged
        x1 = x_ref[...] + _dot(merged, wo_ref[...])
        x1_ref[...] = x1
        h2_ref[...] = (x1 * lax.rsqrt(jnp.mean(x1 * x1, axis=-1, keepdims=True) + EPS) * gn_ref[...]).astype(BF16)

    ospec = lambda d: pl.BlockSpec((tm, d), lambda i: (i, 0))
    wspec = lambda w: pl.BlockSpec(w.shape, lambda i: (0, 0))
    gspec = lambda j: pl.BlockSpec((tm, D), lambda i: (i, j))
    row = pl.BlockSpec((tm, D), lambda i: (i, 0))
    return pl.pallas_call(
        body, name=name, grid=(S // tm,),
        in_specs=[ospec(o.shape[1]) for o in o3] + [wspec(w) for w in w3] + [gspec(j) for j in range(3)]
        + [row, wspec(w_out), wspec(g_norm)],
        out_specs=[row, row, row],
        out_shape=[jax.ShapeDtypeStruct((S, D), BF16), jax.ShapeDtypeStruct((S, D), F32), jax.ShapeDtypeStruct((S, D), BF16)],
        compiler_params=_params(("parallel",)),
    )(*o3, *w3, proj, proj, proj, x, w_out, g_norm)


def _gate_bwd(o3, w3, proj, dx1, w_out, D, *, name, tm=256):
    S = proj.shape[0]
    tm = _tile(S, tm)

    def body(o0, o1, o2, w0, w1, w2, g0, g1, g2, dx_ref, wo_ref, dg_ref, dw0, dw1, dw2, do0, do1, do2):
        first = pl.program_id(0) == 0
        dm = _dot(dx_ref[...].astype(BF16), wo_ref[...], _NT)
        for j, (o_ref, w_ref, g_ref, dw_ref, do_ref) in enumerate(
                ((o0, w0, g0, dw0, do0), (o1, w1, g1, dw1, do1), (o2, w2, g2, dw2, do2))):
            s = _sigmoid(g_ref[...])
            br = _dot(o_ref[...], w_ref[...])
            dg_ref[:, j * D:(j + 1) * D] = (dm * br * s * (1.0 - s)).astype(dg_ref.dtype)
            dbr = (dm * s).astype(BF16)
            do_ref[...] = _dot(dbr, w_ref[...], _NT)
            part = _dot(o_ref[...], dbr, _TN)

            @pl.when(first)
            def _():
                dw_ref[...] = part

            @pl.when(jnp.logical_not(first))
            def _():
                dw_ref[...] += part

    ospec = lambda d: pl.BlockSpec((tm, d), lambda i: (i, 0))
    wspec = lambda w: pl.BlockSpec(w.shape, lambda i: (0, 0))
    gspec = lambda j: pl.BlockSpec((tm, D), lambda i: (i, j))
    dspec = pl.BlockSpec((tm, D), lambda i: (i, 0))
    return pl.pallas_call(
        body, name=name, grid=(S // tm,),
        in_specs=[ospec(o.shape[1]) for o in o3] + [wspec(w) for w in w3] + [gspec(j) for j in range(3)]
        + [dspec, wspec(w_out)],
        out_specs=[pl.BlockSpec((tm, 3 * D), lambda i: (i, 0))] + [wspec(w) for w in w3] + [ospec(o.shape[1]) for o in o3],
        out_shape=[jax.ShapeDtypeStruct((S, proj.shape[1]), BF16)] + [jax.ShapeDtypeStruct(w.shape, F32) for w in w3]
        + [jax.ShapeDtypeStruct((S, o.shape[1]), F32) for o in o3],
        compiler_params=_params(("arbitrary",)),
    )(*o3, *w3, proj, proj, proj, dx1, w_out)


def _pair_sum(stacked, got, core, *, name):
    n, _, r, c = stacked.shape
    tr = _row_tile(r, c, 3)

    def body(core_ref, a_ref, b_ref, o_ref):
        o_ref[...] = (a_ref[0].astype(F32) + b_ref[...].astype(F32)).astype(o_ref.dtype)

    spec = pl.BlockSpec((1, tr, c), lambda s, i, core_ref: (s, i, 0))
    return pl.pallas_call(
        body, name=name,
        grid_spec=pltpu.PrefetchScalarGridSpec(
            num_scalar_prefetch=1, grid=(n, r // tr),
            in_specs=[pl.BlockSpec((1, 1, tr, c), lambda s, i, core_ref: (s, core_ref[0], i, 0)), spec],
            out_specs=spec),
        out_shape=jax.ShapeDtypeStruct((n, r, c), BF16),
        compiler_params=_params(("parallel", "parallel")),
    )(core.astype(jnp.int32).reshape(1), stacked, got)


def _chip_sum(parts, got, chip, *, name):
    _, r, c = parts.shape
    tr = _row_tile(r, c, 6)

    def body(chip_ref, p_ref, q0_ref, q1_ref, q2_ref, o_ref):
        o_ref[...] = ((p_ref[0].astype(F32) + q0_ref[0].astype(F32)) + q1_ref[0].astype(F32)) + q2_ref[0].astype(F32)

    from_chip = lambda j: pl.BlockSpec((1, tr, c), lambda i, chip_ref: (j, i, 0))
    return pl.pallas_call(
        body, name=name,
        grid_spec=pltpu.PrefetchScalarGridSpec(
            num_scalar_prefetch=1, grid=(r // tr,),
            in_specs=[pl.BlockSpec((1, tr, c), lambda i, chip_ref: (chip_ref[0], i, 0))] + [from_chip(j) for j in range(3)],
            out_specs=pl.BlockSpec((tr, c), lambda i, chip_ref: (i, 0))),
        out_shape=jax.ShapeDtypeStruct((r, c), F32),
        compiler_params=_params(("parallel",)),
    )(chip.astype(jnp.int32).reshape(1), parts, got, got, got)


def _adamw_math(w, g, m, v):
    m2 = ADAM_B1 * m + (1.0 - ADAM_B1) * g
    v2 = ADAM_B2 * v + (1.0 - ADAM_B2) * (g * g)
    m_hat = m2 / (1.0 - ADAM_B1 ** ADAM_STEP)
    v_hat = v2 / (1.0 - ADAM_B2 ** ADAM_STEP)
    delta = -ADAM_LR * (m_hat / (jnp.sqrt(v_hat) + ADAM_EPS) + ADAM_WD * w)
    return delta, m2, v2


def _adamw(w, g, m, v, *, name):
    return _ew(_adamw_math, [w, g, m, v], (F32, F32, F32), name=name)


def _adamw_small(w, parts, m, v, *, name):
    n = parts.shape[0]

    def body(w_ref, p_ref, m_ref, v_ref, g_ref, d_ref, m2_ref, v2_ref):
        g = p_ref[0]
        for i in range(1, n):
            g = g + p_ref[i]
        g_ref[...] = g
        d_ref[...], m2_ref[...], v2_ref[...] = _adamw_math(w_ref[...], g, m_ref[...], v_ref[...])

    shp = jax.ShapeDtypeStruct(w.shape, F32)
    return pl.pallas_call(body, name=name, out_shape=[shp] * 4)(w, parts, m, v)


_ANY = pl.BlockSpec(memory_space=pl.ANY)


def _mesh_place():
    x, y, c = lax.axis_index("x"), lax.axis_index("y"), lax.axis_index("c")
    chips = [(1 - x, y), (x, 1 - y), (1 - x, 1 - y)]
    return x, y, c, chips


def _remote(src, dst, sems, i, to):
    send_sems, recv_sems = sems
    return pltpu.make_async_remote_copy(src_ref=src, dst_ref=dst, send_sem=send_sems.at[i], recv_sem=recv_sems.at[i],
                                        device_id=to, device_id_type=MESH_ID)


def _gather_weights(shards, *, name):
    n = len(shards)

    def body(*refs):
        ins, outs = refs[:n], refs[n:2 * n]
        sems = refs[2 * n:2 * n + 2]
        x, y, c, chips = _mesh_place()
        me = 2 * x + y
        sibling = (x, y, 1 - c)
        sent = []
        for w in range(n):
            for j, chip in enumerate(chips):
                cp = _remote(ins[w].at[c], outs[w].at[me, c], sems, 6 * w + j, (chip[0], chip[1], c))
                cp.start()
                sent.append(cp)
        for w in range(n):
            for j, chip in enumerate(chips):
                got = outs[w].at[2 * chip[0] + chip[1], c]
                _remote(got, got, sems, 6 * w + j, sibling).wait_recv()
                cp = _remote(got, got, sems, 6 * w + 3 + j, sibling)
                cp.start()
                sent.append(cp)
        for w in range(n):
            for j, chip in enumerate(chips):
                got = outs[w].at[2 * chip[0] + chip[1], 1 - c]
                _remote(got, got, sems, 6 * w + 3 + j, sibling).wait_recv()
        for cp in sent:
            cp.wait_send()

    outs = pl.pallas_call(
        body, name=name,
        in_specs=[_ANY] * n, out_specs=[_ANY] * n,
        out_shape=[jax.ShapeDtypeStruct((N_CHIPS,) + s.shape, s.dtype) for s in shards],
        scratch_shapes=[pltpu.SemaphoreType.DMA((6 * n,)), pltpu.SemaphoreType.DMA((6 * n,))],
    )(*shards)
    me = 2 * lax.axis_index("x") + lax.axis_index("y")
    return [lax.dynamic_update_index_in_dim(o, s, me, 0) for o, s in zip(outs, shards)]


def _exchange_siblings(grads, *, name):
    n = len(grads)

    def body(*refs):
        ins, got = refs[:n], refs[n:2 * n]
        sems = refs[2 * n:2 * n + 2]
        x, y, c, _ = _mesh_place()
        sibling = (x, y, 1 - c)
        sent = []
        for w in range(n):
            for s in range(N_CHIPS):
                cp = _remote(ins[w].at[s, 1 - c], got[w].at[s], sems, N_CHIPS * w + s, sibling)
                cp.start()
                sent.append(cp)
        for w in range(n):
            for s in range(N_CHIPS):
                _remote(got[w].at[s], got[w].at[s], sems, N_CHIPS * w + s, sibling).wait_recv()
        for cp in sent:
            cp.wait_send()

    n_sem = N_CHIPS * n
    return pl.pallas_call(
        body, name=name,
        in_specs=[_ANY] * n, out_specs=[_ANY] * n,
        out_shape=[jax.ShapeDtypeStruct((N_CHIPS,) + g.shape[2:], g.dtype) for g in grads],
        scratch_shapes=[pltpu.SemaphoreType.DMA((n_sem,)), pltpu.SemaphoreType.DMA((n_sem,))],
    )(*grads)


def _share_halves(halves, small):
    n = len(halves)

    def body(*refs):
        ins, small_ref = refs[:n], refs[n]
        outs, small_out = refs[n + 1:2 * n + 1], refs[2 * n + 1]
        sems = refs[2 * n + 2:2 * n + 4]
        x, y, c, chips = _mesh_place()
        sibling = (x, y, 1 - c)
        me = 4 * x + 2 * y + c
        sent = [_remote(ins[w], outs[w].at[c], sems, w, sibling) for w in range(n)]
        peers = [sibling] + [(ch[0], ch[1], cc) for ch in chips for cc in (c, 1 - c)]
        sent += [_remote(small_ref, small_out.at[me], sems, n + j, peer) for j, peer in enumerate(peers)]
        for cp in sent:
            cp.start()
        for w in range(n):
            _remote(outs[w].at[1 - c], outs[w].at[1 - c], sems, w, sibling).wait_recv()
        for j, peer in enumerate(peers):
            frm = small_out.at[4 * peer[0] + 2 * peer[1] + peer[2]]
            _remote(frm, frm, sems, n + j, peer).wait_recv()
        for cp in sent:
            cp.wait_send()

    n_sem = n + 7
    outs = pl.pallas_call(
        body, name="share_halves",
        in_specs=[_ANY] * (n + 1), out_specs=[_ANY] * (n + 1),
        out_shape=[jax.ShapeDtypeStruct((2,) + h.shape, h.dtype) for h in halves]
        + [jax.ShapeDtypeStruct((8,) + small.shape, small.dtype)],
        scratch_shapes=[pltpu.SemaphoreType.DMA((n_sem,)), pltpu.SemaphoreType.DMA((n_sem,))],
    )(*halves, small)
    c = lax.axis_index("c")
    me = 4 * lax.axis_index("x") + 2 * lax.axis_index("y") + c
    return ([lax.dynamic_update_index_in_dim(o, h, c, 0) for o, h in zip(outs[:n], halves)],
            lax.dynamic_update_index_in_dim(outs[n], small, me, 0))


EARLY = ("w_ff_down", "w_ff_up", "w_out", "w_branch_sb", "w_branch_fox", "w_branch_mem", "w_mem_kv")


def _norm_bwd_tail(dy, x, add, g):
    r = lax.rsqrt(jnp.mean(x * x, axis=-1, keepdims=True) + EPS)
    xh = x * r
    dyg = dy * g
    c = jnp.mean(dyg * xh, axis=-1, keepdims=True)
    return r * (dyg - xh * c) + add, jnp.sum(dy * xh, axis=0, keepdims=True)


def _split(outs, n):
    outs = list(outs) if isinstance(outs, (list, tuple)) else [outs]
    return outs[:n], outs[n:]


def _local_step(x, mem, target, small, W, gather_rest=None, reduce_early=None, reduce_late=None, share_early=None):
    S, D = x.shape
    o_qkv, o_mq, o_f = 3 * D, 3 * D + 2 * 3 * D_SB, 3 * D + 2 * 3 * D_SB + D_MEM
    tq = 512

    g_comms, finish_weights = gather_rest if gather_rest is not None else ([None] * 3, None)
    (proj, h), landed = _split(_mm(x, W["w_in"], name="in_proj", tb=True, tn=1152, a_gain=small["g_mix_norm"],
                                   comm=g_comms[0]), 2)
    blk = lambda j: (proj, (o_qkv + j * D_SB) // LANES)
    sb_q, sb_k, sb_v, fx_q, fx_k, fx_v = [blk(j) for j in range(6)]
    m_q = (proj, o_mq // LANES)
    f_logit_t = _mm(W["w_in"][o_f:o_f + ROW_TILE], h, name="forget_logits", tb=True)[:FOX_HEADS]
    b_col = small["b_forget"].reshape(FOX_HEADS, 1)
    lanes = lambda g: jnp.tile(g, (1, LANES // g.shape[1]))
    g_fq, g_fk, g_mq, g_mk = [lanes(small[k]) for k in ("g_fox_q", "g_fox_k", "g_mem_q", "g_mem_k")]

    (o_sb, sb_tot), more = _split(_sbl_fwd(sb_q, sb_k, sb_v, width=D_SB, hd=HD, name="sb_fwd", tq=tq, comm=g_comms[1]), 2)
    landed += more

    fq = _hnorm_fwd(fx_q, g_fq, width=D_FOX, hd=HD, name="fox_q_norm")
    fk = _hnorm_fwd(fx_k, g_fk, width=D_FOX, hd=HD, name="fox_k_norm")
    f_cum = _forget_fwd(f_logit_t, b_col, name="forget_fwd")
    tkf = _tile(S, tq)
    f_bias = (f_cum.reshape(FOX_HEADS, S, 1), f_cum.reshape(FOX_HEADS, S // tkf, 1, tkf))
    fox_comm = None if finish_weights is None else _Joint([g_comms[2], _ChipExchange("forward", landed)])
    (o_fox, fox_lse), more = _split(_sml_fwd((fq, 0), (fk, 0), fx_v, f_bias, width=D_FOX, hd=HD, causal=True,
                                             name="fox_fwd", tq=tq, tk=tq, comm=fox_comm), 2)
    mem_comm = None
    if finish_weights is not None:
        n_last = len(g_comms[2].ins)
        W = {**W, **finish_weights((0, 1), more[n_last:])}
        mem_comm = _ChipExchange("forward", more[:n_last])

    mh = _rmsnorm_fwd(mem, small["g_mem_norm"], BF16, name="mem_norm")
    mkv = _mm(mh, W["w_mem_kv"], name="mem_kv")
    mv = (mkv, D_MEM // LANES)
    mq = _hnorm_fwd(m_q, g_mq, width=D_MEM, hd=MEM_HD, name="mem_q_norm")
    mk = _hnorm_fwd((mkv, 0), g_mk, width=D_MEM, hd=MEM_HD, name="mem_k_norm")
    (o_mem, mem_lse), more = _split(_sml_fwd((mq, 0), (mk, 0), mv, width=D_MEM, hd=MEM_HD, causal=False, name="mem_fwd",
                                             tq=tq, tk=256, comm=mem_comm), 2)
    if finish_weights is not None:
        W = {**W, **finish_weights((2,), more)}

    o3 = [o_sb, o_fox, o_mem]
    w3 = [W["w_branch_sb"], W["w_branch_fox"], W["w_branch_mem"]]
    merged, x1, h2 = _gate_fwd(o3, w3, proj, x, W["w_out"], small["g_mlp_norm"], D, name="gate_fwd")

    def relu2(acc):
        u = jnp.maximum(acc, 0.0)
        return u, u * u

    u, a = _mm(h2, W["w_ff_up"], name="ff_up", out_dtypes=(BF16, BF16), epilogue=relu2)
    def head(acc, res, tgt):
        d = (res + acc - tgt) * (1.0 / D)
        return d, d, jnp.sum(d * d, axis=0, keepdims=True)

    dy, dy16, sq_rows = _mm(a, W["w_ff_down"], name="ff_down", extras=(x1, target), out_dtypes=(F32, BF16),
                            epilogue=head, col_sums=1, tn=512)
    loss = (0.5 * D) * jnp.sum(sq_rows)

    G = {}
    du = _mm(dy16, W["w_ff_down"], name="d_ff_act", tb=True, out_dtypes=(BF16,), extras=(u,),
             epilogue=lambda acc, uu: (acc * (2.0 * uu.astype(F32)),))
    G["w_ff_down"] = _mm(a, dy16, name="d_w_ff_down", ta=True, out_dtypes=(BF16,))
    G["w_ff_up"] = _mm(h2, du, name="d_w_ff_up", ta=True, out_dtypes=(BF16,))
    dx1, dg_rows = _mm(du, W["w_ff_up"], name="d_mlp_in", tb=True, extras=(x1, dy, small["g_mlp_norm"]),
                       epilogue=_norm_bwd_tail, col_sums=1, tm=512, tn=D)
    dg_mlp = jnp.sum(dg_rows, axis=0, keepdims=True)
    G["w_out"] = _mm(merged, dx1, name="d_w_out", ta=True, out_dtypes=(BF16,))
    dgate, dw0, dw1, dw2, do_sb, do_fox, do_mem = _gate_bwd(o3, w3, proj, dx1, W["w_out"], D, name="gate_bwd")
    for nm, dw in zip(("w_branch_sb", "w_branch_fox", "w_branch_mem"), (dw0, dw1, dw2)):
        G[nm] = dw.astype(BF16)

    sib_comm, after_siblings = (reduce_early({k: G.pop(k) for k in EARLY if k != "w_mem_kv"})
                                if reduce_early is not None else (None, None))
    (dmq_n, dmk_n, dmv), landed_sib = _split(
        _sml_bwd((mq, 0), (mk, 0), mv, (o_mem, 0), (mem_lse, 0), (do_mem, 0), width=D_MEM, hd=MEM_HD, causal=False,
                 name="mem_bwd", tq=tq, tk=256, comm=sib_comm), 3)
    dm_q, dg_mem_q = _hnorm_bwd(m_q, g_mq, dmq_n, width=D_MEM, hd=MEM_HD, name="d_mem_q_norm")
    dmk_raw, dg_mem_k = _hnorm_bwd((mkv, 0), g_mk, dmk_n, width=D_MEM, hd=MEM_HD, name="d_mem_k_norm")
    dmkv = jnp.concatenate([dmk_raw, dmv.astype(BF16)], axis=1)
    G["w_mem_kv"] = _mm(mh, dmkv, name="d_w_mem_kv", ta=True, out_dtypes=(BF16,))
    dmh = _mm(dmkv, W["w_mem_kv"], name="d_mem_h", tb=True)
    _, dg_mem = _rmsnorm_bwd(mem, small["g_mem_norm"], dmh, name="d_mem_norm")

    r_comms, r_finish = (after_siblings(landed_sib, {"w_mem_kv": G.pop("w_mem_kv")})
                         if after_siblings is not None else ([None] * 2, None))
    dsb, landed_sb = _split(_sbl_bwd(sb_q, sb_k, sb_v, (do_sb, 0), (sb_tot, 0), width=D_SB, hd=HD, name="sb_bwd", tq=tq,
                                     comm=r_comms[0]), 3)
    (dfq, dfk, dfv, df_row, df_col), landed_fox = _split(
        _sml_bwd((fq, 0), (fk, 0), fx_v, (o_fox, 0), (fox_lse, 0), (do_fox, 0), f_bias, width=D_FOX, hd=HD, causal=True,
                 name="fox_bwd", tq=tq, tk=tq, comm=r_comms[1]), 5)
    early = r_finish(landed_sb, landed_fox) if r_finish is not None else {}
    dfx_q, dg_fox_q = _hnorm_bwd(fx_q, g_fq, dfq, width=D_FOX, hd=HD, name="d_fox_q_norm")
    dfx_k, dg_fox_k = _hnorm_bwd(fx_k, g_fk, dfk, width=D_FOX, hd=HD, name="d_fox_k_norm")
    d_fcum = df_row.reshape(FOX_HEADS, S) - df_col.reshape(FOX_HEADS, S)
    d_flogit_t, db_forget = _forget_bwd(f_logit_t, b_col, d_fcum, name="forget_bwd")
    dg_fox_q, dg_fox_k = dg_fox_q[:, :HD], dg_fox_k[:, :HD]

    rest_cols = jnp.concatenate([t.astype(BF16) for t in (*dsb, dfx_q, dfx_k, dfv, dm_q)]
                                + [d_flogit_t.T.astype(BF16), jnp.zeros((S, F_PAD - FOX_HEADS), BF16)], axis=1)
    dproj = lax.dynamic_update_slice(dgate, rest_cols, (0, 3 * D))
    share_comm, shared = share_early(early) if share_early is not None else (None, None)
    (g_w_in,), landed = _split(_mm(dproj, h, name="d_w_in", ta=True, out_dtypes=(BF16,), tm=768, comm=share_comm), 1)
    if shared is not None:
        early = shared(landed)
    comm, finish = reduce_late({"w_in": g_w_in}) if reduce_late is not None else (None, None)
    (grad_x, dg_rows), landed = _split(
        _mm(dproj, W["w_in"], name="d_mix_in", tk=2304, tm=512, tn=D, extras=(x, dx1, small["g_mix_norm"]),
            epilogue=_norm_bwd_tail, col_sums=1, comm=comm), 2)
    dg_mix = jnp.sum(dg_rows, axis=0, keepdims=True)
    if finish is None:
        G["w_in"] = g_w_in
    else:
        early.update(finish(landed))

    small_grads = dict(g_mix_norm=dg_mix, g_mem_norm=dg_mem, b_forget=db_forget.reshape(1, FOX_HEADS),
                       g_fox_q=dg_fox_q, g_fox_k=dg_fox_k, g_mem_q=dg_mem_q, g_mem_k=dg_mem_k, g_mlp_norm=dg_mlp)
    return loss, grad_x, G, small_grads, early


BIG = ("w_in", "w_mem_kv", "w_branch_sb", "w_branch_fox", "w_branch_mem", "w_out", "w_ff_up", "w_ff_down")
COLUMN_SHARDED = ("w_in", "w_branch_sb", "w_branch_fox", "w_branch_mem", "w_ff_up")
SMALL = ("g_mix_norm", "g_mem_norm", "b_forget", "g_fox_q", "g_fox_k", "g_mem_q", "g_mem_k", "g_mlp_norm")
ORDER = ("g_mix_norm", "g_mem_norm", "w_in", "b_forget", "g_fox_q", "g_fox_k", "g_mem_q", "g_mem_k", "w_mem_kv",
         "w_branch_sb", "w_branch_fox", "w_branch_mem", "w_out", "g_mlp_norm", "w_ff_up", "w_ff_down")


def _unshard(name, gathered):
    n, _, rh, c = gathered.shape
    t = gathered.reshape(n, 2 * rh, c)
    if name in COLUMN_SHARDED:
        return t.transpose(1, 0, 2).reshape(2 * rh, n * c)
    return t.reshape(n * 2 * rh, c)


def _reshard(name, full):
    if name in COLUMN_SHARDED:
        r, c = full.shape
        t = full.reshape(r, N_CHIPS, c // N_CHIPS).transpose(1, 0, 2)
    else:
        r, c = full.shape[0] // N_CHIPS, full.shape[1]
        t = full.reshape(N_CHIPS, r, c)
    return t.reshape(N_CHIPS, 2, t.shape[1] // 2, t.shape[2])


ROW_TILE = 16
IN_BUF_ALIGN = 256


def _in_segments(D):
    n_qkv = 6 * D_SB
    o_mq, o_gate = n_qkv + FOX_HEADS, n_qkv + FOX_HEADS + D_MEM
    return [(0, n_qkv, 3 * D), (n_qkv, o_mq, 3 * D + n_qkv + D_MEM), (o_mq, o_gate, 3 * D + n_qkv), (o_gate, o_gate + 3 * D, 0)]


class _InLayout:
    def __init__(self, D, shard, n):
        self.D, self.shard, self.n = D, shard, n
        down = lambda v: v // ROW_TILE * ROW_TILE
        up = lambda v: -(-v // ROW_TILE) * ROW_TILE
        self.pieces = []
        ends = []
        for s in range(n):
            cursor, mine = 0, []
            for a, b, p in _in_segments(D):
                x0, x1 = max(a, s * shard), min(b, (s + 1) * shard)
                if x0 < x1:
                    p0 = p + x0 - a
                    rows = up(p0 + x1 - x0) - down(p0)
                    mine.append((x0 - s * shard, x1 - x0, p0, cursor, rows))
                    cursor += rows
            self.pieces.append(mine)
            ends.append(cursor)
        self.rows = -(-max(ends) // IN_BUF_ALIGN) * IN_BUF_ALIGN
        self.padded_rows = 3 * D + 6 * D_SB + D_MEM + F_PAD

    def _per_shard(self, fn, chip, operand):
        return lax.switch(chip, [functools.partial(fn, s) for s in range(self.n)], operand)

    def pack(self, chip, rows):
        def one(s, t):
            out, at = [], 0
            for x0, n_rows, p0, start, region in self.pieces[s]:
                lead = p0 % ROW_TILE
                out += [jnp.zeros((start + lead - at, t.shape[1]), t.dtype), t[x0:x0 + n_rows]]
                at = start + lead + n_rows
            return jnp.concatenate(out + [jnp.zeros((self.rows - at, t.shape[1]), t.dtype)], axis=0)
        return self._per_shard(one, chip, rows)

    def unpack(self, chip, buf, pad_to):
        def one(s, t):
            out = [t[start + p0 % ROW_TILE:start + p0 % ROW_TILE + n_rows] for _, n_rows, p0, start, _ in self.pieces[s]]
            return jnp.concatenate(out + [jnp.zeros((pad_to - self.shard, t.shape[1]), t.dtype)], axis=0)
        return self._per_shard(one, chip, buf)

    def to_padded(self, bufs):
        runs = sorted((p0, s, start, region) for s in range(self.n) for _, _, p0, start, region in self.pieces[s])
        chunks, end = [], 0
        for p0, s, start, region in runs:
            d0 = p0 // ROW_TILE * ROW_TILE
            src = bufs[s, start:start + region]
            if d0 < end:
                assert end - d0 == ROW_TILE
                last = chunks.pop()
                chunks += [last[:-ROW_TILE], last[-ROW_TILE:] + src[:ROW_TILE], src[ROW_TILE:]]
            else:
                if d0 > end:
                    chunks.append(jnp.zeros((d0 - end, bufs.shape[2]), bufs.dtype))
                chunks.append(src)
            end = d0 + region
        chunks.append(jnp.zeros((self.padded_rows - end, bufs.shape[2]), bufs.dtype))
        return jnp.concatenate(chunks, axis=0)

    def from_padded(self, gp):
        bufs = []
        for s in range(self.n):
            out, at = [], 0
            for _, n_rows, p0, start, region in self.pieces[s]:
                d0 = p0 // ROW_TILE * ROW_TILE
                row = d0 + lax.broadcasted_iota(jnp.int32, (region, 1), 0)
                out.append(jnp.where((row >= p0) & (row < p0 + n_rows), gp[d0:d0 + region], jnp.zeros((), gp.dtype)))
                at = start + region
            bufs.append(jnp.concatenate(out + [jnp.zeros((self.rows - at, gp.shape[1]), gp.dtype)], axis=0))
        return jnp.stack(bufs)


SMALL_ROWS = 16


def _pack_small(vals, scalar=None):
    width = max(vals[k].shape[1] for k in SMALL)
    rows = [jnp.pad(vals[k].astype(F32), ((0, 0), (0, width - vals[k].shape[1]))) for k in SMALL]
    extra = jnp.zeros((SMALL_ROWS - len(SMALL), width), F32)
    if scalar is not None:
        extra = extra.at[0, 0].set(scalar)
    return jnp.concatenate(rows + [extra], axis=0)


def _unpack_small(packed, like):
    return {k: packed[i:i + 1, :like[k].shape[1]] for i, k in enumerate(SMALL)}


def kernel(x, mem, g_mix_norm, g_mem_norm, w_in, b_forget, g_fox_q, g_fox_k, g_mem_q, g_mem_k, w_mem_kv, w_branch_sb, w_branch_fox, w_branch_mem, w_out, g_mlp_norm, w_ff_up, w_ff_down, loss_target, m_g_mix_norm, m_g_mem_norm, m_w_in, m_b_forget, m_g_fox_q, m_g_fox_k, m_g_mem_q, m_g_mem_k, m_w_mem_kv, m_w_branch_sb, m_w_branch_fox, m_w_branch_mem, m_w_out, m_g_mlp_norm, m_w_ff_up, m_w_ff_down, v_g_mix_norm, v_g_mem_norm, v_w_in, v_b_forget, v_g_fox_q, v_g_fox_k, v_g_mem_q, v_g_mem_k, v_w_mem_kv, v_w_branch_sb, v_w_branch_fox, v_w_branch_mem, v_w_out, v_g_mlp_norm, v_w_ff_up, v_w_ff_down):
    given = dict(locals())
    D = x.shape[-1]
    weights = {k: given[k] for k in ORDER}
    moms = {k: given["m_" + k] for k in ORDER}
    vars_ = {k: given["v_" + k] for k in ORDER}

    me_chip = 2 * lax.axis_index("x") + lax.axis_index("y")

    n_in = w_in.shape[2]
    lay = _InLayout(D, n_in, N_CHIPS)
    transposed = lambda t: jnp.transpose(t[0])
    shards = {}
    for k in BIG:
        w = weights[k][0].astype(BF16)
        if k == "w_in":
            w = lay.pack(me_chip, jnp.transpose(w))
        shards[k] = w.reshape(2, w.shape[0] // 2, w.shape[1])
    gathered_in = _gather_weights([shards["w_in"]], name="gather_w_in")[0]
    W = {"w_in": lay.to_padded(gathered_in.reshape(N_CHIPS, lay.rows, D))}
    carried = (("w_branch_sb", "w_branch_fox", "w_branch_mem", "w_out"), ("w_ff_up", "w_mem_kv"), ("w_ff_down",))
    rest = [k for grp in carried for k in grp]
    assert sorted(rest + ["w_in"]) == sorted(BIG)

    def finish_weights(groups, arrived):
        names = [k for gi in groups for k in carried[gi]]
        full = [lax.dynamic_update_index_in_dim(o, shards[k], me_chip, 0) for k, o in zip(names, arrived)]
        return {k: _unshard(k, g) for k, g in zip(names, full)}

    me_core = lax.axis_index("c")

    def sum_chips(names, parts, got):
        return {k: _chip_sum(p, q, me_chip, name="sum_chips_" + k) for k, p, q in zip(names, parts, got)}

    def pair_sums(grads, tag):
        names = list(grads)
        stacked = {k: _reshard(k, grads[k]) for k in names if k != "w_in"}
        if "w_in" in grads:
            stacked["w_in"] = lay.from_padded(grads["w_in"]).reshape(N_CHIPS, 2, lay.rows // 2, D)
        got = _exchange_siblings([stacked[k] for k in names], name="exchange_siblings_" + tag)
        return {k: _pair_sum(stacked[k], q, me_core, name="sum_pair_" + k) for k, q in zip(names, got)}

    def reduce_early(grads):
        names = list(grads)
        stacked = [_reshard(k, grads[k]) for k in names]

        def after_siblings(landed, more):
            parts = {k: _pair_sum(t, q, me_core, name="sum_pair_" + k) for k, t, q in zip(names, stacked, landed)}
            parts.update(pair_sums(more, "early"))
            groups = [[k for k in parts if k in ("w_ff_down", "w_ff_up")], [k for k in parts if k not in ("w_ff_down", "w_ff_up")]]
            comms = [_ChipExchange("scatter", [parts[k] for k in grp]) for grp in groups]

            def finish(*got):
                out = {}
                for grp, q in zip(groups, got):
                    out.update(sum_chips(grp, [parts[k] for k in grp], q))
                return out
            return comms, finish
        return _ChipExchange("siblings", stacked), after_siblings

    def reduce_late(grads):
        parts = pair_sums(grads, "late")
        names = list(parts)
        return _ChipExchange("scatter", [parts[k] for k in names]), functools.partial(sum_chips, names, [parts[k] for k in names])

    def share_early(halves):
        names = list(halves)

        def shared(landed):
            return {k: lax.dynamic_update_index_in_dim(o, halves[k], me_core, 0) for k, o in zip(names, landed)}
        return _ChipExchange("halves", [halves[k] for k in names]), shared

    small = {k: weights[k] for k in SMALL}
    loss_part, grad_x, G, small_grads, reduced = _local_step(
        x[0], mem[0], loss_target[0], small, W,
        gather_rest=([_ChipExchange("gather", [shards[k] for k in grp]) for grp in carried], finish_weights),
        reduce_early=reduce_early, reduce_late=reduce_late, share_early=share_early)
    assert not G, list(G)
    (reduced["w_in"],), small_parts = _share_halves([reduced["w_in"]], _pack_small(small_grads, loss_part))

    grads, deltas, new_m, new_v = {}, {}, {}, {}
    for k in BIG:
        g = reduced[k]
        shp = weights[k].shape
        if k == "w_in":
            g2 = lay.unpack(me_chip, g.reshape(lay.rows, D), lay.rows)
            padded = lambda t: jnp.pad(transposed(t), ((0, lay.rows - n_in), (0, 0)))
            outs = _adamw(padded(weights[k]), g2, padded(moms[k]), padded(vars_[k]), name="adamw_" + k)
            g2, d, m2, v2 = [jnp.transpose(t[:n_in]) for t in (g2, *outs)]
        else:
            g2 = g.reshape(shp[1], shp[2])
            d, m2, v2 = _adamw(weights[k][0], g2, moms[k][0], vars_[k][0], name="adamw_" + k)
        grads[k], deltas[k], new_m[k], new_v[k] = g2.reshape(shp), d.reshape(shp), m2.reshape(shp), v2.reshape(shp)
    sg, sd, sm, sv = _adamw_small(_pack_small(small), small_parts, _pack_small({k: moms[k] for k in SMALL}),
                                  _pack_small({k: vars_[k] for k in SMALL}), name="adamw_small")
    for dst, packed in ((grads, sg), (deltas, sd), (new_m, sm), (new_v, sv)):
        dst.update(_unpack_small(packed, small))

    loss = sg[len(SMALL), 0]
    return (loss, grad_x[None], *[grads[k] for k in ORDER], *[deltas[k] for k in ORDER],
            *[new_m[k] for k in ORDER], *[new_v[k] for k in ORDER])
```
